```python
import jax, jax.numpy as jnp
from jax import lax
import numpy as np

D_MODEL = 2048
BATCH = 8
SEQ = 2048
DEPTH = 1

N_META = 16
D_RNN = D_MODEL
N_RNN_BLOCKS = 8
RNN_BLOCK = D_RNN // N_RNN_BLOCKS
CONV_WIDTH = 4
LRU_C = 8.0
LRU_MIN_RAD = 0.9
LRU_MAX_RAD = 0.999
HEAD_DIM = 64
N_Q_HEADS = D_MODEL // HEAD_DIM
N_KV_HEADS = N_Q_HEADS // 8
GROUP = N_Q_HEADS // N_KV_HEADS
D_ATTN = N_Q_HEADS * HEAD_DIM
D_KV = N_KV_HEADS * HEAD_DIM
WINDOW = 128
BLOCK = 128
ROPE_THETA = 10000.0
NEG_INF = -1e30
N_BRANCHES = 2
LN_EPS = 1e-5
DEEPNORM_ALPHA = (2.0 * DEPTH) ** 0.25
DEEPNORM_BETA = (8.0 * DEPTH) ** -0.25
OFF_GR = D_RNN
OFF_Q = 2 * D_RNN
OFF_K = OFF_Q + D_ATTN
OFF_V = OFF_K + D_KV
OFF_GA = OFF_V + D_KV
OFF_G = OFF_GA + D_ATTN
D_IN = OFF_G + N_BRANCHES * D_MODEL

kernel_name = "hybrid_rglru_swa_sink_gated_merge"


def layer_norm(x, g, b):
    xf = x.astype(jnp.float32)
    mu = xf.mean(-1, keepdims=True)
    var = jnp.square(xf - mu).mean(-1, keepdims=True)
    y = (xf - mu) * lax.rsqrt(var + LN_EPS)
    return (y * g.astype(jnp.float32) + b.astype(jnp.float32)).astype(x.dtype)


def rope(x, pos):
    half = HEAD_DIM // 2
    inv = ROPE_THETA ** (-jnp.arange(half, dtype=jnp.float32) / half)
    ang = pos.astype(jnp.float32)[:, None] * inv[None, :]
    cos = jnp.cos(ang)[None, :, None, :]
    sin = jnp.sin(ang)[None, :, None, :]
    xf = x.astype(jnp.float32)
    x1, x2 = xf[..., :half], xf[..., half:]
    return jnp.concatenate([x1 * cos - x2 * sin, x2 * cos + x1 * sin], axis=-1).astype(x.dtype)


def causal_depthwise_conv(x, w, b):
    T = x.shape[1]
    xp = jnp.pad(x, ((0, 0), (CONV_WIDTH - 1, 0), (0, 0)))
    y = b
    for k in range(CONV_WIDTH):
        s = CONV_WIDTH - 1 - k
        y = y + w[k] * xp[:, s:s + T]
    return y


def rg_lru(x, w_ra, b_ra, w_ri, b_ri, lam):
    B, T, _ = x.shape
    xb = x.reshape(B, T, N_RNN_BLOCKS, RNN_BLOCK)
    gate_r = jax.nn.sigmoid((jnp.einsum('btnc,ncd->btnd', xb, w_ra).reshape(B, T, D_RNN) + b_ra).astype(jnp.float32))
    gate_i = jax.nn.sigmoid((jnp.einsum('btnc,ncd->btnd', xb, w_ri).reshape(B, T, D_RNN) + b_ri).astype(jnp.float32))
    log_a = LRU_C * gate_r * jax.nn.log_sigmoid(lam.astype(jnp.float32))
    a = jnp.exp(log_a)
    mult = jnp.sqrt(-jnp.expm1(2.0 * log_a))
    mult = jnp.where((jnp.arange(T) == 0)[None, :, None], 1.0, mult)
    u = mult * gate_i * x.astype(jnp.float32)

    def combine(left, right):
        a1, b1 = left
        a2, b2 = right
        return a1 * a2, a2 * b1 + b2

    _, h = lax.associative_scan(combine, (a, u), axis=1)
    return h.astype(x.dtype)


def sliding_window_sink_attention(q, k, v, sinks):
    B, T = q.shape[:2]
    pad = BLOCK - N_META
    Lp = T + pad
    NB = Lp // BLOCK
    padf = lambda t: jnp.pad(t, ((0, 0), (pad, 0), (0, 0), (0, 0)))
    qb = padf(q).reshape(B, NB, BLOCK, N_KV_HEADS, GROUP, HEAD_DIM)
    kb = padf(k).reshape(B, NB, BLOCK, N_KV_HEADS, HEAD_DIM)
    vb = padf(v).reshape(B, NB, BLOCK, N_KV_HEADS, HEAD_DIM)

    def banded(t):
        prev = jnp.pad(t, ((0, 0), (1, 0), (0, 0), (0, 0), (0, 0)))[:, :NB]
        meta = jnp.broadcast_to(t[:, :1, pad:], (B, NB, N_META, N_KV_HEADS, HEAD_DIM))
        return jnp.concatenate([meta, prev, t], axis=2)

    kk, vv = banded(kb), banded(vb)

    qi = jnp.arange(NB)[:, None] * BLOCK + jnp.arange(BLOCK)[None, :]
    jb = (jnp.arange(NB)[:, None] - 1) * BLOCK + jnp.arange(2 * BLOCK)[None, :]
    jm = pad + jnp.arange(N_META)
    band_ok = ((jb[:, None, :] >= BLOCK) & (jb[:, None, :] <= qi[:, :, None])
               & (qi[:, :, None] - jb[:, None, :] < WINDOW))
    meta_ok = jnp.broadcast_to(jm[None, None, :] <= qi[:, :, None], (NB, BLOCK, N_META))
    mask = jnp.concatenate([meta_ok, band_ok], axis=-1)

    s = jnp.einsum('bnqkgd,bnskd->bnkgqs', qb, kk).astype(jnp.float32) * (HEAD_DIM ** -0.5)
    s = jnp.where(mask[None, :, None, None], s, NEG_INF)
    sink = sinks.astype(jnp.float32).reshape(N_KV_HEADS, GROUP)[None, None, :, :, None, None]
    m = jnp.maximum(s.max(-1, keepdims=True), sink)
    p = jnp.exp(s - m)
    denom = p.sum(-1, keepdims=True) + jnp.exp(sink - m)
    o = jnp.einsum('bnkgqs,bnskd->bnqkgd', (p / denom).astype(v.dtype), vv)
    return o.reshape(B, Lp, D_ATTN)[:, pad:]


def hybrid_layer(h, pos, w_in, b_in, conv_w, conv_b, w_ra, b_ra, w_ri, b_ri, lam, sinks,
                 w_rnn_out, w_attn_out, w_o, b_o, ln_g, ln_b):
    B, T, _ = h.shape
    z = h @ w_in + b_in
    xr, gr, q, k, v, ga, mg = jnp.split(z, [OFF_GR, OFF_Q, OFF_K, OFF_V, OFF_GA, OFF_G], axis=-1)

    hr = rg_lru(causal_depthwise_conv(xr, conv_w, conv_b), w_ra, b_ra, w_ri, b_ri, lam)
    y_a = (hr * jax.nn.silu(gr)) @ w_rnn_out

    q = rope(q.reshape(B, T, N_Q_HEADS, HEAD_DIM), pos)
    k = rope(k.reshape(B, T, N_KV_HEADS, HEAD_DIM), pos)
    v = v.reshape(B, T, N_KV_HEADS, HEAD_DIM)
    o = sliding_window_sink_attention(q, k, v, sinks)
    y_b = (o * jax.nn.silu(ga)) @ w_attn_out

    g = jax.nn.sigmoid(mg.astype(jnp.float32)).astype(h.dtype)
    mixed = g[..., :D_MODEL] * y_a + g[..., D_MODEL:] * y_b
    out = mixed @ w_o + b_o
    return layer_norm(DEEPNORM_ALPHA * h + out, ln_g, ln_b)


def _fwd_setup_inputs(seed: int = 0) -> dict:
    key = jax.random.key(seed)
    ks = jax.random.split(key, 24)
    f32 = jnp.float32
    nrm = lambda k, shape, scale: jax.random.normal(k, shape, f32) * scale
    u = jax.random.uniform(ks[12], (DEPTH, D_RNN), f32, LRU_MIN_RAD, LRU_MAX_RAD)
    s_rad = u ** (1.0 / LRU_C)
    lru_lambda = jnp.log(s_rad) - jnp.log1p(-s_rad)
    return {
        "x": jax.random.normal(ks[0], (BATCH, SEQ, D_MODEL), f32),
        "meta_tokens": nrm(ks[1], (N_META, D_MODEL), 1.0),
        "ln_emb_g": 1.0 + nrm(ks[2], (D_MODEL,), 0.01),
        "ln_emb_b": nrm(ks[3], (D_MODEL,), 0.01),
        "w_in": nrm(ks[4], (DEPTH, D_MODEL, D_IN), D_MODEL ** -0.5),
        "b_in": nrm(ks[5], (DEPTH, D_IN), 0.01),
        "conv_w": nrm(ks[6], (DEPTH, CONV_WIDTH, D_RNN), CONV_WIDTH ** -0.5),
        "conv_b": nrm(ks[7], (DEPTH, D_RNN), 0.01),
        "w_ra": nrm(ks[8], (DEPTH, N_RNN_BLOCKS, RNN_BLOCK, RNN_BLOCK), RNN_BLOCK ** -0.5),
        "b_ra": nrm(ks[9], (DEPTH, D_RNN), 0.01),
        "w_ri": nrm(ks[10], (DEPTH, N_RNN_BLOCKS, RNN_BLOCK, RNN_BLOCK), RNN_BLOCK ** -0.5),
        "b_ri": nrm(ks[11], (DEPTH, D_RNN), 0.01),
        "lru_lambda": lru_lambda,
        "sinks": nrm(ks[13], (DEPTH, N_Q_HEADS), 0.5),
        "w_rnn_out": nrm(ks[14], (DEPTH, D_RNN, D_MODEL), D_RNN ** -0.5 * DEEPNORM_BETA),
        "w_attn_out": nrm(ks[15], (DEPTH, D_ATTN, D_MODEL), D_ATTN ** -0.5 * DEEPNORM_BETA),
        "w_o": nrm(ks[16], (DEPTH, D_MODEL, D_MODEL), D_MODEL ** -0.5 * DEEPNORM_BETA),
        "b_o": nrm(ks[17], (DEPTH, D_MODEL), 0.01),
        "ln_g": 1.0 + nrm(ks[18], (DEPTH, D_MODEL), 0.01),
        "ln_b": nrm(ks[19], (DEPTH, D_MODEL), 0.01),
    }


def _fwd_reference(x, meta_tokens, ln_emb_g, ln_emb_b, w_in, b_in, conv_w, conv_b, w_ra, b_ra, w_ri, b_ri,
              lru_lambda, sinks, w_rnn_out, w_attn_out, w_o, b_o, ln_g, ln_b):
    B = x.shape[0]
    meta = jnp.broadcast_to(meta_tokens.astype(x.dtype)[None], (B, N_META, D_MODEL))
    h = jnp.concatenate([meta, x], axis=1)
    h = layer_norm(h, ln_emb_g, ln_emb_b)
    pos = jnp.arange(h.shape[1])
    for l in range(DEPTH):
        h = hybrid_layer(h, pos, w_in[l], b_in[l], conv_w[l], conv_b[l], w_ra[l], b_ra[l], w_ri[l], b_ri[l],
                         lru_lambda[l], sinks[l], w_rnn_out[l], w_attn_out[l], w_o[l], b_o[l], ln_g[l], ln_b[l])
    return h[:, N_META:]


import jax as _jax
import jax.numpy as _jnp

TWIN_FORMAT = 'train_step'
FWD_PARAMS = ['x', 'meta_tokens', 'ln_emb_g', 'ln_emb_b', 'w_in', 'b_in', 'conv_w', 'conv_b', 'w_ra', 'b_ra', 'w_ri', 'b_ri', 'lru_lambda', 'sinks', 'w_rnn_out', 'w_attn_out', 'w_o', 'b_o', 'ln_g', 'ln_b']
TWIN_WEIGHTS = ['meta_tokens', 'ln_emb_g', 'ln_emb_b', 'w_in', 'b_in', 'conv_w', 'conv_b', 'w_ra', 'b_ra', 'w_ri', 'b_ri', 'lru_lambda', 'sinks', 'w_rnn_out', 'w_attn_out', 'w_o', 'b_o', 'ln_g', 'ln_b']
TWIN_DIFF_INPUT = 'x'
TWIN_INPUTS = ['x', 'meta_tokens', 'ln_emb_g', 'ln_emb_b', 'w_in', 'b_in', 'conv_w', 'conv_b', 'w_ra', 'b_ra', 'w_ri', 'b_ri', 'lru_lambda', 'sinks', 'w_rnn_out', 'w_attn_out', 'w_o', 'b_o', 'ln_g', 'ln_b', 'loss_target', 'm_meta_tokens', 'm_ln_emb_g', 'm_ln_emb_b', 'm_w_in', 'm_b_in', 'm_conv_w', 'm_conv_b', 'm_w_ra', 'm_b_ra', 'm_w_ri', 'm_b_ri', 'm_lru_lambda', 'm_sinks', 'm_w_rnn_out', 'm_w_attn_out', 'm_w_o', 'm_b_o', 'm_ln_g', 'm_ln_b', 'v_meta_tokens', 'v_ln_emb_g', 'v_ln_emb_b', 'v_w_in', 'v_b_in', 'v_conv_w', 'v_conv_b', 'v_w_ra', 'v_b_ra', 'v_w_ri', 'v_b_ri', 'v_lru_lambda', 'v_sinks', 'v_w_rnn_out', 'v_w_attn_out', 'v_w_o', 'v_b_o', 'v_ln_g', 'v_ln_b']
TWIN_OUTPUTS = ['loss', 'grad_x', 'grad_meta_tokens', 'grad_ln_emb_g', 'grad_ln_emb_b', 'grad_w_in', 'grad_b_in', 'grad_conv_w', 'grad_conv_b', 'grad_w_ra', 'grad_b_ra', 'grad_w_ri', 'grad_b_ri', 'grad_lru_lambda', 'grad_sinks', 'grad_w_rnn_out', 'grad_w_attn_out', 'grad_w_o', 'grad_b_o', 'grad_ln_g', 'grad_ln_b', 'delta_meta_tokens', 'delta_ln_emb_g', 'delta_ln_emb_b', 'delta_w_in', 'delta_b_in', 'delta_conv_w', 'delta_conv_b', 'delta_w_ra', 'delta_b_ra', 'delta_w_ri', 'delta_b_ri', 'delta_lru_lambda', 'delta_sinks', 'delta_w_rnn_out', 'delta_w_attn_out', 'delta_w_o', 'delta_b_o', 'delta_ln_g', 'delta_ln_b', 'new_m_meta_tokens', 'new_m_ln_emb_g', 'new_m_ln_emb_b', 'new_m_w_in', 'new_m_b_in', 'new_m_conv_w', 'new_m_conv_b', 'new_m_w_ra', 'new_m_b_ra', 'new_m_w_ri', 'new_m_b_ri', 'new_m_lru_lambda', 'new_m_sinks', 'new_m_w_rnn_out', 'new_m_w_attn_out', 'new_m_w_o', 'new_m_b_o', 'new_m_ln_g', 'new_m_ln_b', 'new_v_meta_tokens', 'new_v_ln_emb_g', 'new_v_ln_emb_b', 'new_v_w_in', 'new_v_b_in', 'new_v_conv_w', 'new_v_conv_b', 'new_v_w_ra', 'new_v_b_ra', 'new_v_w_ri', 'new_v_b_ri', 'new_v_lru_lambda', 'new_v_sinks', 'new_v_w_rnn_out', 'new_v_w_attn_out', 'new_v_w_o', 'new_v_b_o', 'new_v_ln_g', 'new_v_ln_b']
TWIN_LEAF_KINDS = {'loss': 'loss', 'grad_x': 'grad_x', 'grad_meta_tokens': 'grad_w', 'grad_ln_emb_g': 'grad_w', 'grad_ln_emb_b': 'grad_w', 'grad_w_in': 'grad_w', 'grad_b_in': 'grad_w', 'grad_conv_w': 'grad_w', 'grad_conv_b': 'grad_w', 'grad_w_ra': 'grad_w', 'grad_b_ra': 'grad_w', 'grad_w_ri': 'grad_w', 'grad_b_ri': 'grad_w', 'grad_lru_lambda': 'grad_w', 'grad_sinks': 'grad_w', 'grad_w_rnn_out': 'grad_w', 'grad_w_attn_out': 'grad_w', 'grad_w_o': 'grad_w', 'grad_b_o': 'grad_w', 'grad_ln_g': 'grad_w', 'grad_ln_b': 'grad_w', 'delta_meta_tokens': 'delta_w', 'delta_ln_emb_g': 'delta_w', 'delta_ln_emb_b': 'delta_w', 'delta_w_in': 'delta_w', 'delta_b_in': 'delta_w', 'delta_conv_w': 'delta_w', 'delta_conv_b': 'delta_w', 'delta_w_ra': 'delta_w', 'delta_b_ra': 'delta_w', 'delta_w_ri': 'delta_w', 'delta_b_ri': 'delta_w', 'delta_lru_lambda': 'delta_w', 'delta_sinks': 'delta_w', 'delta_w_rnn_out': 'delta_w', 'delta_w_attn_out': 'delta_w', 'delta_w_o': 'delta_w', 'delta_b_o': 'delta_w', 'delta_ln_g': 'delta_w', 'delta_ln_b': 'delta_w', 'new_m_meta_tokens': 'new_m', 'new_m_ln_emb_g': 'new_m', 'new_m_ln_emb_b': 'new_m', 'new_m_w_in': 'new_m', 'new_m_b_in': 'new_m', 'new_m_conv_w': 'new_m', 'new_m_conv_b': 'new_m', 'new_m_w_ra': 'new_m', 'new_m_b_ra': 'new_m', 'new_m_w_ri': 'new_m', 'new_m_b_ri': 'new_m', 'new_m_lru_lambda': 'new_m', 'new_m_sinks': 'new_m', 'new_m_w_rnn_out': 'new_m', 'new_m_w_attn_out': 'new_m', 'new_m_w_o': 'new_m', 'new_m_b_o': 'new_m', 'new_m_ln_g': 'new_m', 'new_m_ln_b': 'new_m', 'new_v_meta_tokens': 'new_v', 'new_v_ln_emb_g': 'new_v', 'new_v_ln_emb_b': 'new_v', 'new_v_w_in': 'new_v', 'new_v_b_in': 'new_v', 'new_v_conv_w': 'new_v', 'new_v_conv_b': 'new_v', 'new_v_w_ra': 'new_v', 'new_v_b_ra': 'new_v', 'new_v_w_ri': 'new_v', 'new_v_b_ri': 'new_v', 'new_v_lru_lambda': 'new_v', 'new_v_sinks': 'new_v', 'new_v_w_rnn_out': 'new_v', 'new_v_w_attn_out': 'new_v', 'new_v_w_o': 'new_v', 'new_v_b_o': 'new_v', 'new_v_ln_g': 'new_v', 'new_v_ln_b': 'new_v'}


def _forward(args):
    return _fwd_reference(*[args[k] for k in FWD_PARAMS])


def _output_shape():
    out = _jax.eval_shape(lambda: _forward(_fwd_setup_inputs(0)))
    return out.shape, out.dtype

N_MICROBATCH = 1
ADAM_LR = 0.001
ADAM_B1 = 0.9
ADAM_B2 = 0.999
ADAM_EPS = 1e-08
ADAM_WD = 0.01
ADAM_STEP = 10
PER_EXAMPLE_BATCH_AXIS = {'x': 0, 'loss_target': 0}
SHARED_INPUTS = []
_WEIGHT_DTYPES = {'meta_tokens': _jnp.float32, 'ln_emb_g': _jnp.float32, 'ln_emb_b': _jnp.float32, 'w_in': _jnp.float32, 'b_in': _jnp.float32, 'conv_w': _jnp.float32, 'conv_b': _jnp.float32, 'w_ra': _jnp.float32, 'b_ra': _jnp.float32, 'w_ri': _jnp.float32, 'b_ri': _jnp.float32, 'lru_lambda': _jnp.float32, 'sinks': _jnp.float32, 'w_rnn_out': _jnp.float32, 'w_attn_out': _jnp.float32, 'w_o': _jnp.float32, 'b_o': _jnp.float32, 'ln_g': _jnp.float32, 'ln_b': _jnp.float32}
MOMENT_SCALE = {'meta_tokens': 3.008150e-04, 'ln_emb_g': 1.434481e-01, 'ln_emb_b': 9.386288e-02, 'w_in': 2.247195e-03, 'b_in': 1.853074e-02, 'conv_w': 3.859671e-03, 'conv_b': 4.360165e-02, 'w_ra': 9.987538e-04, 'b_ra': 8.457696e-04, 'w_ri': 1.764056e-03, 'b_ri': 1.340093e-03, 'lru_lambda': 1.703726e-03, 'sinks': 1.573049e-04, 'w_rnn_out': 6.145867e-03, 'w_attn_out': 1.740981e-03, 'w_o': 6.254181e-03, 'b_o': 7.199596e-02, 'ln_g': 7.999321e+00, 'ln_b': 1.417043e-01}


def _to_microbatches(a, axis):
    t = _jnp.moveaxis(a, axis, 0)
    t = t.reshape((N_MICROBATCH, t.shape[0] // N_MICROBATCH) + t.shape[1:])
    return _jnp.moveaxis(t, 1, axis + 1)


def setup_inputs(seed: int = 0) -> dict:
    inp = _fwd_setup_inputs(seed)
    key = _jax.random.fold_in(_jax.random.key(seed), 7919)
    shape, _ = _output_shape()
    out = dict(inp)
    out["loss_target"] = _jax.random.normal(_jax.random.fold_in(key, 0), shape, _jnp.float32)
    for i, name in enumerate(TWIN_WEIGHTS):
        w = inp[name].astype(_jnp.float32)
        if MOMENT_SCALE is None:
            s = _jnp.sqrt(_jnp.mean(_jnp.square(w)) + 1e-30)
        else:
            s = MOMENT_SCALE[name]
        km, kv = _jax.random.split(_jax.random.fold_in(key, i + 1))
        out[name] = w
        out["m_" + name] = s * _jax.random.normal(km, w.shape, _jnp.float32)
        out["v_" + name] = (s * s) * _jax.random.uniform(kv, w.shape, _jnp.float32, 0.5, 1.5)
    if N_MICROBATCH > 1:
        for name, axis in PER_EXAMPLE_BATCH_AXIS.items():
            out[name] = _to_microbatches(out[name], axis)
    return {'x': out['x'], 'meta_tokens': out['meta_tokens'], 'ln_emb_g': out['ln_emb_g'], 'ln_emb_b': out['ln_emb_b'], 'w_in': out['w_in'], 'b_in': out['b_in'], 'conv_w': out['conv_w'], 'conv_b': out['conv_b'], 'w_ra': out['w_ra'], 'b_ra': out['b_ra'], 'w_ri': out['w_ri'], 'b_ri': out['b_ri'], 'lru_lambda': out['lru_lambda'], 'sinks': out['sinks'], 'w_rnn_out': out['w_rnn_out'], 'w_attn_out': out['w_attn_out'], 'w_o': out['w_o'], 'b_o': out['b_o'], 'ln_g': out['ln_g'], 'ln_b': out['ln_b'], 'loss_target': out['loss_target'], 'm_meta_tokens': out['m_meta_tokens'], 'm_ln_emb_g': out['m_ln_emb_g'], 'm_ln_emb_b': out['m_ln_emb_b'], 'm_w_in': out['m_w_in'], 'm_b_in': out['m_b_in'], 'm_conv_w': out['m_conv_w'], 'm_conv_b': out['m_conv_b'], 'm_w_ra': out['m_w_ra'], 'm_b_ra': out['m_b_ra'], 'm_w_ri': out['m_w_ri'], 'm_b_ri': out['m_b_ri'], 'm_lru_lambda': out['m_lru_lambda'], 'm_sinks': out['m_sinks'], 'm_w_rnn_out': out['m_w_rnn_out'], 'm_w_attn_out': out['m_w_attn_out'], 'm_w_o': out['m_w_o'], 'm_b_o': out['m_b_o'], 'm_ln_g': out['m_ln_g'], 'm_ln_b': out['m_ln_b'], 'v_meta_tokens': out['v_meta_tokens'], 'v_ln_emb_g': out['v_ln_emb_g'], 'v_ln_emb_b': out['v_ln_emb_b'], 'v_w_in': out['v_w_in'], 'v_b_in': out['v_b_in'], 'v_conv_w': out['v_conv_w'], 'v_conv_b': out['v_conv_b'], 'v_w_ra': out['v_w_ra'], 'v_b_ra': out['v_b_ra'], 'v_w_ri': out['v_w_ri'], 'v_b_ri': out['v_b_ri'], 'v_lru_lambda': out['v_lru_lambda'], 'v_sinks': out['v_sinks'], 'v_w_rnn_out': out['v_w_rnn_out'], 'v_w_attn_out': out['v_w_attn_out'], 'v_w_o': out['v_w_o'], 'v_b_o': out['v_b_o'], 'v_ln_g': out['v_ln_g'], 'v_ln_b': out['v_ln_b']}


def _loss(weights, diff, rest, loss_target):
    with _jax.named_scope("forward"):
        args = {**rest, TWIN_DIFF_INPUT: diff, **{k: w.astype(_WEIGHT_DTYPES[k]) for k, w in weights.items()}}
        y = _forward(args)
    with _jax.named_scope("loss_head"):
        err = _jnp.square(y.astype(_jnp.float32) - loss_target)
        return 0.5 * _jnp.sum(_jnp.mean(err, axis=-1)) if err.ndim else 0.5 * err


def _adamw(w, g, m, v):
    m = ADAM_B1 * m + (1.0 - ADAM_B1) * g
    v = ADAM_B2 * v + (1.0 - ADAM_B2) * _jnp.square(g)
    m_hat = m / (1.0 - ADAM_B1 ** ADAM_STEP)
    v_hat = v / (1.0 - ADAM_B2 ** ADAM_STEP)
    delta = -ADAM_LR * (m_hat / (_jnp.sqrt(v_hat) + ADAM_EPS) + ADAM_WD * w)
    return delta, m, v


def reference(x, meta_tokens, ln_emb_g, ln_emb_b, w_in, b_in, conv_w, conv_b, w_ra, b_ra, w_ri, b_ri, lru_lambda, sinks, w_rnn_out, w_attn_out, w_o, b_o, ln_g, ln_b, loss_target, m_meta_tokens, m_ln_emb_g, m_ln_emb_b, m_w_in, m_b_in, m_conv_w, m_conv_b, m_w_ra, m_b_ra, m_w_ri, m_b_ri, m_lru_lambda, m_sinks, m_w_rnn_out, m_w_attn_out, m_w_o, m_b_o, m_ln_g, m_ln_b, v_meta_tokens, v_ln_emb_g, v_ln_emb_b, v_w_in, v_b_in, v_conv_w, v_conv_b, v_w_ra, v_b_ra, v_w_ri, v_b_ri, v_lru_lambda, v_sinks, v_w_rnn_out, v_w_attn_out, v_w_o, v_b_o, v_ln_g, v_ln_b):
    given = dict(x=x, meta_tokens=meta_tokens, ln_emb_g=ln_emb_g, ln_emb_b=ln_emb_b, w_in=w_in, b_in=b_in, conv_w=conv_w, conv_b=conv_b, w_ra=w_ra, b_ra=b_ra, w_ri=w_ri, b_ri=b_ri, lru_lambda=lru_lambda, sinks=sinks, w_rnn_out=w_rnn_out, w_attn_out=w_attn_out, w_o=w_o, b_o=b_o, ln_g=ln_g, ln_b=ln_b, loss_target=loss_target, m_meta_tokens=m_meta_tokens, m_ln_emb_g=m_ln_emb_g, m_ln_emb_b=m_ln_emb_b, m_w_in=m_w_in, m_b_in=m_b_in, m_conv_w=m_conv_w, m_conv_b=m_conv_b, m_w_ra=m_w_ra, m_b_ra=m_b_ra, m_w_ri=m_w_ri, m_b_ri=m_b_ri, m_lru_lambda=m_lru_lambda, m_sinks=m_sinks, m_w_rnn_out=m_w_rnn_out, m_w_attn_out=m_w_attn_out, m_w_o=m_w_o, m_b_o=m_b_o, m_ln_g=m_ln_g, m_ln_b=m_ln_b, v_meta_tokens=v_meta_tokens, v_ln_emb_g=v_ln_emb_g, v_ln_emb_b=v_ln_emb_b, v_w_in=v_w_in, v_b_in=v_b_in, v_conv_w=v_conv_w, v_conv_b=v_conv_b, v_w_ra=v_w_ra, v_b_ra=v_b_ra, v_w_ri=v_w_ri, v_b_ri=v_b_ri, v_lru_lambda=v_lru_lambda, v_sinks=v_sinks, v_w_rnn_out=v_w_rnn_out, v_w_attn_out=v_w_attn_out, v_w_o=v_w_o, v_b_o=v_b_o, v_ln_g=v_ln_g, v_ln_b=v_ln_b)
    weights = {n: given[n] for n in TWIN_WEIGHTS}
    shared = {n: given[n] for n in SHARED_INPUTS}
    per_example = {n: given[n] for n in ['x']}
    grad_fn = _jax.value_and_grad(_loss, argnums=(0, 1))

    def one_microbatch(ex, loss_target):
        ex = dict(ex)
        diff = ex.pop(TWIN_DIFF_INPUT)
        return grad_fn(weights, diff, {**shared, **ex}, loss_target)

    if N_MICROBATCH == 1:
        loss, (grad_w, grad_x) = one_microbatch(per_example, given["loss_target"])
    else:
        def body(carry, xs):
            loss_sum, grad_sum = carry
            l_k, (gw_k, gx_k) = one_microbatch(xs[0], xs[1])
            with _jax.named_scope("update"):
                return (loss_sum + l_k, _jax.tree.map(_jnp.add, grad_sum, gw_k)), gx_k

        init = (_jnp.zeros((), _jnp.float32), _jax.tree.map(_jnp.zeros_like, weights))
        (loss, grad_w), grad_x = _jax.lax.scan(body, init, (per_example, given["loss_target"]))
    with _jax.named_scope("update"):
        delta_w, new_m, new_v = {}, {}, {}
        for n in TWIN_WEIGHTS:
            delta_w[n], new_m[n], new_v[n] = _adamw(weights[n], grad_w[n], given["m_" + n], given["v_" + n])
    return (loss, grad_x, *[grad_w[n] for n in TWIN_WEIGHTS], *[delta_w[n] for n in TWIN_WEIGHTS],
            *[new_m[n] for n in TWIN_WEIGHTS], *[new_v[n] for n in TWIN_WEIGHTS])
```

```python
import functools

import jax
import jax.numpy as jnp
from jax import lax
from jax.experimental import pallas as pl
from jax.experimental.pallas import tpu as pltpu

F32 = jnp.float32
BF16 = jnp.bfloat16
_MXU = jnp.bfloat16

D = 2048
SEQ = 2048
N_META = 16
BLK = 128
PAD = BLK - N_META
TP = PAD + N_META + SEQ
NBLK = TP // BLK
N_RB = 8
RB = 256
CONV_W = 4
LRU_C = 8.0
HD = 64
N_Q = 32
N_KV = 4
GRP = 8
D_KV = 256
NEG_INF = -1e30
LN_EPS = 1e-5
ALPHA = 2.0 ** 0.25
ROPE_THETA = 10000.0
OFF_GR, OFF_Q, OFF_K, OFF_V, OFF_GA, OFF_G = 2048, 4096, 6144, 6400, 6656, 8704
D_IN = 12800
N_SH = 4
W_IN_COLS = D_IN // N_SH
TN_IN = 640

ADAM_LR = 0.001
ADAM_B1 = 0.9
ADAM_B2 = 0.999
ADAM_EPS = 1e-08
ADAM_WD = 0.01
ADAM_STEP = 10

MESH = pl.DeviceIdType.MESH
_MIB = 1024 * 1024


def _cp(vmem_mib=48, sem=None):
    return pltpu.CompilerParams(vmem_limit_bytes=vmem_mib * _MIB, dimension_semantics=sem)


def _sds(shape, dtype):
    return jax.ShapeDtypeStruct(shape, dtype)


def _sigmoid(x):
    return 1.0 / (1.0 + jnp.exp(-x))


def _row_ids(i, rows):
    return lax.broadcasted_iota(jnp.int32, (rows, 1), 0) + i * rows


def _mm_nn(a, w3, bias, *, name, tm, tn, out_dtype=F32):
    m, k = a.shape
    s, _, ns = w3.shape
    per = ns // tn

    def body(a_ref, w_ref, b_ref, o_ref):
        acc = jnp.dot(a_ref[...].astype(_MXU), w_ref[...].astype(_MXU), preferred_element_type=F32)
        o_ref[...] = (acc + b_ref[...]).astype(out_dtype)

    return pl.pallas_call(
        body, name=name, grid=(m // tm, s * per),
        in_specs=[pl.BlockSpec((tm, k), lambda i, j: (i, 0)),
                  pl.BlockSpec((None, k, tn), lambda i, j: (j // per, 0, j % per)),
                  pl.BlockSpec((1, tn), lambda i, j: (0, j))],
        out_specs=pl.BlockSpec((tm, tn), lambda i, j: (i, j)),
        out_shape=_sds((m, s * ns), out_dtype),
        compiler_params=_cp(56, ("parallel", "parallel")),
    )(a, w3, bias)


def _mm_nt(a, w3, *, name, tm, tn, tk):
    m, kk = a.shape
    s, ko, ns = w3.shape
    per = ns // tk
    nk = kk // tk

    def body(a_ref, w_ref, o_ref, acc_ref):
        part = lax.dot_general(a_ref[...].astype(_MXU), w_ref[...].astype(_MXU),
                               (((1,), (1,)), ((), ())), preferred_element_type=F32)
        if nk == 1:
            o_ref[...] = part
        else:
            kidx = pl.program_id(2)

            @pl.when(kidx == 0)
            def _():
                acc_ref[...] = part

            @pl.when(kidx > 0)
            def _():
                acc_ref[...] += part

            @pl.when(kidx == nk - 1)
            def _():
                o_ref[...] = acc_ref[...]

    return pl.pallas_call(
        body, name=name, grid=(m // tm, ko // tn, nk),
        in_specs=[pl.BlockSpec((tm, tk), lambda i, j, q: (i, q)),
                  pl.BlockSpec((None, tn, tk), lambda i, j, q: (q // per, j, q % per))],
        out_specs=pl.BlockSpec((tm, tn), lambda i, j, q: (i, j)),
        out_shape=_sds((m, ko), F32),
        scratch_shapes=[pltpu.VMEM((tm, tn) if nk > 1 else (8, 128), F32)],
        compiler_params=_cp(56, ("parallel", "parallel", "arbitrary")),
    )(a, w3)


def _mm_tn(a, b, *, name, tm, tn, n_out_shards):
    t, m = a.shape
    _, n = b.shape
    ns = n // n_out_shards
    per = ns // tn

    def body(a_ref, b_ref, o_ref):
        o_ref[...] = lax.dot_general(a_ref[...].astype(_MXU), b_ref[...].astype(_MXU),
                                     (((0,), (0,)), ((), ())), preferred_element_type=F32)

    return pl.pallas_call(
        body, name=name, grid=(m // tm, n // tn),
        in_specs=[pl.BlockSpec((t, tm), lambda i, j: (0, i)),
                  pl.BlockSpec((t, tn), lambda i, j: (0, j))],
        out_specs=pl.BlockSpec((None, tm, tn), lambda i, j: (j // per, i, j % per)),
        out_shape=_sds((n_out_shards, m, ns), F32),
        compiler_params=_cp(56, ("parallel", "parallel")),
    )(a, b)


def _ln_emb_fwd(h0p, g, b):
    def body(x_ref, g_ref, b_ref, h32_ref, h16_ref):
        x = x_ref[...]
        mu = jnp.mean(x, axis=-1, keepdims=True)
        xc = x - mu
        var = jnp.mean(xc * xc, axis=-1, keepdims=True)
        y = xc * lax.rsqrt(var + LN_EPS) * g_ref[...] + b_ref[...]
        h32_ref[...] = y
        h16_ref[...] = y.astype(_MXU)

    row = pl.BlockSpec((BLK, D), lambda i: (i, 0))
    vec = pl.BlockSpec((1, D), lambda i: (0, 0))
    return pl.pallas_call(
        body, name="ln_emb_fwd", grid=(NBLK,), in_specs=[row, vec, vec], out_specs=[row, row],
        out_shape=[_sds((TP, D), F32), _sds((TP, D), _MXU)], compiler_params=_cp(32, ("parallel",)),
    )(h0p, g, b)


def _ln_emb_bwd(h0p, g, dr, dhz):
    def body(x_ref, g_ref, dr_ref, dhz_ref, dx_ref, acc_ref):
        @pl.when(pl.program_id(0) == 0)
        def _():
            acc_ref[...] = jnp.zeros_like(acc_ref)

        x = x_ref[...]
        mu = jnp.mean(x, axis=-1, keepdims=True)
        xc = x - mu
        var = jnp.mean(xc * xc, axis=-1, keepdims=True)
        rstd = lax.rsqrt(var + LN_EPS)
        xhat = xc * rstd
        dh = ALPHA * dr_ref[...] + dhz_ref[...]
        acc_ref[0:1, :] += jnp.sum(dh * xhat, axis=0, keepdims=True)
        acc_ref[1:2, :] += jnp.sum(dh, axis=0, keepdims=True)
        dxh = dh * g_ref[...]
        m1 = jnp.mean(dxh, axis=-1, keepdims=True)
        m2 = jnp.mean(dxh * xhat, axis=-1, keepdims=True)
        dx_ref[...] = rstd * (dxh - m1 - xhat * m2)

    row = pl.BlockSpec((BLK, D), lambda i: (i, 0))
    vec = pl.BlockSpec((1, D), lambda i: (0, 0))
    return pl.pallas_call(
        body, name="ln_emb_bwd", grid=(NBLK,), in_specs=[row, vec, row, row],
        out_specs=[row, pl.BlockSpec((8, D), lambda i: (0, 0))],
        out_shape=[_sds((TP, D), F32), _sds((8, D), F32)], compiler_params=_cp(32, ("arbitrary",)),
    )(h0p, g, dr, dhz)


def _mul_silu_fwd(a, z, off, *, name):
    w = 512

    def body(a_ref, z_ref, o_ref):
        zz = z_ref[...]
        o_ref[...] = (a_ref[...] * (zz * _sigmoid(zz))).astype(_MXU)

    return pl.pallas_call(
        body, name=name, grid=(NBLK, D // w),
        in_specs=[pl.BlockSpec((BLK, w), lambda i, j: (i, j)),
                  pl.BlockSpec((BLK, w), lambda i, j: (i, off // w + j))],
        out_specs=pl.BlockSpec((BLK, w), lambda i, j: (i, j)),
        out_shape=_sds((TP, D), _MXU), compiler_params=_cp(32, ("parallel", "parallel")),
    )(a, z)


def _mul_silu_bwd(dy, a, z, off, *, name):
    w = 512

    def body(dy_ref, a_ref, z_ref, da_ref, dg_ref):
        zz = z_ref[...]
        sg = _sigmoid(zz)
        d = dy_ref[...]
        da_ref[...] = d * (zz * sg)
        dg_ref[...] = (d * a_ref[...] * (sg * (1.0 + zz * (1.0 - sg)))).astype(_MXU)

    blk = pl.BlockSpec((BLK, w), lambda i, j: (i, j))
    return pl.pallas_call(
        body, name=name, grid=(NBLK, D // w),
        in_specs=[blk, blk, pl.BlockSpec((BLK, w), lambda i, j: (i, off // w + j))],
        out_specs=[blk, blk], out_shape=[_sds((TP, D), F32), _sds((TP, D), _MXU)],
        compiler_params=_cp(32, ("parallel", "parallel")),
    )(dy, a, z)


def _merge_fwd(ya, yb, z):
    w = 512

    def body(ya_ref, yb_ref, ga_ref, gb_ref, o_ref):
        o_ref[...] = (_sigmoid(ga_ref[...]) * ya_ref[...] + _sigmoid(gb_ref[...]) * yb_ref[...]).astype(_MXU)

    blk = pl.BlockSpec((BLK, w), lambda i, j: (i, j))
    return pl.pallas_call(
        body, name="merge_fwd", grid=(NBLK, D // w),
        in_specs=[blk, blk, pl.BlockSpec((BLK, w), lambda i, j: (i, OFF_G // w + j)),
                  pl.BlockSpec((BLK, w), lambda i, j: (i, (OFF_G + D) // w + j))],
        out_specs=blk, out_shape=_sds((TP, D), _MXU), compiler_params=_cp(32, ("parallel", "parallel")),
    )(ya, yb, z, z)


def _merge_bwd(dmix, ya, yb, z):
    w = 512

    def body(dm_ref, ya_ref, yb_ref, ga_ref, gb_ref, dya_ref, dyb_ref, dga_ref, dgb_ref):
        dm = dm_ref[...]
        sa = _sigmoid(ga_ref[...])
        sb = _sigmoid(gb_ref[...])
        dya_ref[...] = (dm * sa).astype(_MXU)
        dyb_ref[...] = (dm * sb).astype(_MXU)
        dga_ref[...] = (dm * ya_ref[...] * sa * (1.0 - sa)).astype(_MXU)
        dgb_ref[...] = (dm * yb_ref[...] * sb * (1.0 - sb)).astype(_MXU)

    blk = pl.BlockSpec((BLK, w), lambda i, j: (i, j))
    return pl.pallas_call(
        body, name="merge_bwd", grid=(NBLK, D // w),
        in_specs=[blk, blk, blk, pl.BlockSpec((BLK, w), lambda i, j: (i, OFF_G // w + j)),
                  pl.BlockSpec((BLK, w), lambda i, j: (i, (OFF_G + D) // w + j))],
        out_specs=[blk] * 4, out_shape=[_sds((TP, D), _MXU)] * 4,
        compiler_params=_cp(32, ("parallel", "parallel")),
    )(dmix, ya, yb, z, z)


def _ln_out_loss(h32, out, target, g, b):
    def body(h_ref, o_ref, t_ref, g_ref, b_ref, dr_ref, acc_ref):
        i = pl.program_id(0)

        @pl.when(i == 0)
        def _():
            acc_ref[...] = jnp.zeros_like(acc_ref)

        r = ALPHA * h_ref[...] + o_ref[...]
        mu = jnp.mean(r, axis=-1, keepdims=True)
        rc = r - mu
        var = jnp.mean(rc * rc, axis=-1, keepdims=True)
        rstd = lax.rsqrt(var + LN_EPS)
        xhat = rc * rstd
        gg = g_ref[...]
        y = xhat * gg + b_ref[...]
        real = (i >= 1).astype(F32)
        diff = (y - t_ref[...]) * real
        dy = diff * (1.0 / D)
        dxh = dy * gg
        m1 = jnp.mean(dxh, axis=-1, keepdims=True)
        m2 = jnp.mean(dxh * xhat, axis=-1, keepdims=True)
        dr = rstd * (dxh - m1 - xhat * m2)
        dr_ref[...] = dr
        acc_ref[0:1, :] += jnp.sum(dy * xhat, axis=0, keepdims=True)
        acc_ref[1:2, :] += jnp.sum(dy, axis=0, keepdims=True)
        acc_ref[2:3, :] += jnp.sum(dr, axis=0, keepdims=True)
        acc_ref[3:4, :] += (0.5 / D) * jnp.sum(diff * diff)

    row = pl.BlockSpec((BLK, D), lambda i: (i, 0))
    vec = pl.BlockSpec((1, D), lambda i: (0, 0))
    return pl.pallas_call(
        body, name="ln_out_loss", grid=(NBLK,),
        in_specs=[row, row, pl.BlockSpec((BLK, D), lambda i: (jnp.maximum(i - 1, 0), 0)), vec, vec],
        out_specs=[row, pl.BlockSpec((8, D), lambda i: (0, 0))],
        out_shape=[_sds((TP, D), F32), _sds((8, D), F32)], compiler_params=_cp(32, ("arbitrary",)),
    )(h32, out, target, g, b)


def _colsum(x, *, name, tn):
    _, n = x.shape

    def body(x_ref, o_ref):
        o_ref[...] = jnp.sum(x_ref[...].astype(F32), axis=0, keepdims=True)

    return pl.pallas_call(
        body, name=name, grid=(n // tn,), in_specs=[pl.BlockSpec((TP, tn), lambda j: (0, j))],
        out_specs=pl.BlockSpec((1, tn), lambda j: (0, j)), out_shape=_sds((1, n), F32),
        compiler_params=_cp(32, ("parallel",)),
    )(x)


def _rnn_recompute(xr_ref, cw_ref, cb_ref, wra_ref, wri_ref, bra_ref, bri_ref, lam_ref):
    rows = _row_ids(0, TP)
    valid = (rows >= PAD).astype(F32)
    first = rows == PAD
    x = xr_ref[...] * valid
    cw = cw_ref[...]
    shifted = [x, pltpu.roll(x, 1, 0), pltpu.roll(x, 2, 0), pltpu.roll(x, 3, 0)]
    c = cb_ref[...] + cw[0:1, :] * shifted[0] + cw[1:2, :] * shifted[1] + cw[2:3, :] * shifted[2] + cw[3:4, :] * shifted[3]
    cm = c.astype(_MXU)
    gr = _sigmoid(jnp.dot(cm, wra_ref[...].astype(_MXU), preferred_element_type=F32) + bra_ref[...])
    gi = _sigmoid(jnp.dot(cm, wri_ref[...].astype(_MXU), preferred_element_type=F32) + bri_ref[...])
    lam = lam_ref[...]
    ls = jnp.minimum(lam, 0.0) - jnp.log(1.0 + jnp.exp(-jnp.abs(lam)))
    log_a = LRU_C * gr * ls
    a = jnp.exp(log_a)
    mult = jnp.where(first, 1.0, jnp.sqrt(1.0 - jnp.exp(2.0 * log_a)))
    return dict(valid=valid, first=first, shifted=shifted, c=c, cm=cm, gr=gr, gi=gi, ls=ls, a=a, mult=mult, lam=lam)


def _rnn_specs():
    col = pl.BlockSpec((TP, RB), lambda n: (0, n))
    vec = pl.BlockSpec((1, RB), lambda n: (0, n))
    return dict(col=col, vec=vec, cw=pl.BlockSpec((CONV_W, RB), lambda n: (0, n)),
                wblk=pl.BlockSpec((None, RB, RB), lambda n: (n, 0, 0)))


def _rnn_gates_fwd(z, conv_w, conv_b, w_ra, w_ri, b_ra, b_ri, lam):
    def body(xr_ref, cw_ref, cb_ref, wra_ref, wri_ref, bra_ref, bri_ref, lam_ref, a_ref, u_ref):
        r = _rnn_recompute(xr_ref, cw_ref, cb_ref, wra_ref, wri_ref, bra_ref, bri_ref, lam_ref)
        a_ref[...] = r["a"]
        u_ref[...] = r["mult"] * r["gi"] * r["c"] * r["valid"]

    s = _rnn_specs()
    return pl.pallas_call(
        body, name="rnn_gates_fwd", grid=(N_RB,),
        in_specs=[s["col"], s["cw"], s["vec"], s["wblk"], s["wblk"], s["vec"], s["vec"], s["vec"]],
        out_specs=[s["col"], s["col"]], out_shape=[_sds((TP, D), F32)] * 2,
        compiler_params=_cp(56, ("parallel",)),
    )(z, conv_w, conv_b, w_ra, w_ri, b_ra, b_ri, lam)


SCAN_ROWS = 272


def _scan_fwd(a, u):
    def body(a_ref, u_ref, h_ref, carry_ref):
        @pl.when(pl.program_id(0) == 0)
        def _():
            carry_ref[...] = jnp.zeros_like(carry_ref)

        def step(t, h):
            h = a_ref[t] * h + u_ref[t]
            h_ref[t] = h
            return h

        carry_ref[...] = lax.fori_loop(0, SCAN_ROWS, step, carry_ref[...], unroll=8)

    blk = pl.BlockSpec((SCAN_ROWS, 16, 128), lambda i: (i, 0, 0))
    h = pl.pallas_call(
        body, name="scan_fwd", grid=(TP // SCAN_ROWS,), in_specs=[blk, blk], out_specs=blk,
        out_shape=_sds((TP, 16, 128), F32), scratch_shapes=[pltpu.VMEM((16, 128), F32)],
        compiler_params=_cp(32, ("arbitrary",)),
    )(a.reshape(TP, 16, 128), u.reshape(TP, 16, 128))
    return h.reshape(TP, D)


def _scan_bwd(a, dh):
    nst = TP // SCAN_ROWS

    def body(a_ref, d_ref, o_ref, lam_ref, anext_ref):
        @pl.when(pl.program_id(0) == 0)
        def _():
            lam_ref[...] = jnp.zeros_like(lam_ref)
            anext_ref[...] = jnp.zeros_like(anext_ref)

        def step(q, carry):
            lam, an = carry
            t = SCAN_ROWS - 1 - q
            lam = d_ref[t] + an * lam
            o_ref[t] = lam
            return lam, a_ref[t]

        lam, an = lax.fori_loop(0, SCAN_ROWS, step, (lam_ref[...], anext_ref[...]), unroll=8)
        lam_ref[...] = lam
        anext_ref[...] = an

    blk = pl.BlockSpec((SCAN_ROWS, 16, 128), lambda i: (nst - 1 - i, 0, 0))
    o = pl.pallas_call(
        body, name="scan_bwd", grid=(nst,), in_specs=[blk, blk], out_specs=blk,
        out_shape=_sds((TP, 16, 128), F32),
        scratch_shapes=[pltpu.VMEM((16, 128), F32), pltpu.VMEM((16, 128), F32)],
        compiler_params=_cp(32, ("arbitrary",)),
    )(a.reshape(TP, 16, 128), dh.reshape(TP, 16, 128))
    return o.reshape(TP, D)


def _rnn_gates_bwd(z, lam_s, hr, conv_w, conv_b, w_ra, w_ri, b_ra, b_ri, lam):
    def body(xr_ref, ls_ref, hr_ref, cw_ref, cb_ref, wra_ref, wri_ref, bra_ref, bri_ref, lam_ref,
             dx_ref, dwra_ref, dwri_ref, sums_ref):
        r = _rnn_recompute(xr_ref, cw_ref, cb_ref, wra_ref, wri_ref, bra_ref, bri_ref, lam_ref)
        valid, c, gr, gi, a, mult = r["valid"], r["c"], r["gr"], r["gi"], r["a"], r["mult"]
        du = ls_ref[...] * valid
        da = du * pltpu.roll(hr_ref[...], 1, 0)
        d_gi = du * mult * c
        dc = du * mult * gi
        dmult = du * gi * c
        dlog_a = da * a + jnp.where(r["first"], 0.0, -dmult * a * a / mult)
        d_gr = dlog_a * (LRU_C * r["ls"])
        dls = jnp.sum(dlog_a * (LRU_C * gr), axis=0, keepdims=True)
        dpre_r = d_gr * gr * (1.0 - gr)
        dpre_i = d_gi * gi * (1.0 - gi)
        pr = dpre_r.astype(_MXU)
        pi = dpre_i.astype(_MXU)
        tn = (((0,), (0,)), ((), ()))
        nt = (((1,), (1,)), ((), ()))
        dwra_ref[...] = lax.dot_general(r["cm"], pr, tn, preferred_element_type=F32)
        dwri_ref[...] = lax.dot_general(r["cm"], pi, tn, preferred_element_type=F32)
        dc = dc + lax.dot_general(pr, wra_ref[...].astype(_MXU), nt, preferred_element_type=F32)
        dc = dc + lax.dot_general(pi, wri_ref[...].astype(_MXU), nt, preferred_element_type=F32)
        cw = cw_ref[...]
        dx = cw[0:1, :] * dc
        for k in range(1, CONV_W):
            dx = dx + cw[k:k + 1, :] * pltpu.roll(dc, TP - k, 0)
        dx_ref[...] = (dx * valid).astype(_MXU)
        for k in range(CONV_W):
            sums_ref[k:k + 1, :] = jnp.sum(dc * r["shifted"][k], axis=0, keepdims=True)
        sums_ref[4:5, :] = jnp.sum(dc, axis=0, keepdims=True)
        sums_ref[5:6, :] = jnp.sum(dpre_r, axis=0, keepdims=True)
        sums_ref[6:7, :] = jnp.sum(dpre_i, axis=0, keepdims=True)
        sums_ref[7:8, :] = dls * _sigmoid(-r["lam"])

    s = _rnn_specs()
    return pl.pallas_call(
        body, name="rnn_gates_bwd", grid=(N_RB,),
        in_specs=[s["col"], s["col"], s["col"], s["cw"], s["vec"], s["wblk"], s["wblk"], s["vec"], s["vec"], s["vec"]],
        out_specs=[s["col"], s["wblk"], s["wblk"], pl.BlockSpec((8, RB), lambda n: (0, n))],
        out_shape=[_sds((TP, D), _MXU), _sds((N_RB, RB, RB), F32), _sds((N_RB, RB, RB), F32), _sds((8, D), F32)],
        compiler_params=_cp(60, ("parallel",)),
    )(z, lam_s, hr, conv_w, conv_b, w_ra, w_ri, b_ra, b_ri, lam)


def _rope_tables():
    half = HD // 2
    inv = ROPE_THETA ** (-jnp.arange(half, dtype=F32) / half)
    pos = (jnp.arange(TP) - PAD).astype(F32)
    ang = pos[:, None] * inv[None, :]
    return jnp.tile(jnp.cos(ang), (1, 4)), jnp.tile(jnp.sin(ang), (1, 4))


def _rope(x, cos_t, sin_t, sign):
    w = x.shape[1]
    lane = lax.broadcasted_iota(jnp.int32, x.shape, 1)
    first = (lane % HD) < (HD // 2)
    swapped = jnp.where(first, pltpu.roll(x, w - HD // 2, 1), pltpu.roll(x, HD // 2, 1))
    ct = jnp.tile(cos_t, (1, w // 128))
    st = jnp.tile(sin_t, (1, w // 128))
    return x * ct + swapped * jnp.where(first, -sign * st, sign * st)


def _rope_fwd(z, cos_t, sin_t):
    def body(q_ref, k_ref, v_ref, c_ref, s_ref, qo_ref, ko_ref, vo_ref):
        c = c_ref[...]
        s = s_ref[...]
        qo_ref[...] = _rope(q_ref[...], c, s, 1.0).astype(_MXU)
        ko_ref[...] = _rope(k_ref[...], c, s, 1.0).astype(_MXU)
        vo_ref[...] = v_ref[...].astype(_MXU)

    tab = pl.BlockSpec((BLK, 128), lambda i: (i, 0))
    return pl.pallas_call(
        body, name="rope_fwd", grid=(NBLK,),
        in_specs=[pl.BlockSpec((BLK, D), lambda i: (i, OFF_Q // D)),
                  pl.BlockSpec((BLK, D_KV), lambda i: (i, OFF_K // D_KV)),
                  pl.BlockSpec((BLK, D_KV), lambda i: (i, OFF_V // D_KV)), tab, tab],
        out_specs=[pl.BlockSpec((BLK, D), lambda i: (i, 0)), pl.BlockSpec((BLK, D_KV), lambda i: (i, 0)),
                   pl.BlockSpec((BLK, D_KV), lambda i: (i, 0))],
        out_shape=[_sds((TP, D), _MXU), _sds((TP, D_KV), _MXU), _sds((TP, D_KV), _MXU)],
        compiler_params=_cp(32, ("parallel",)),
    )(z, z, z, cos_t, sin_t)


def _rope_bwd_k(dk, dv, cos_t, sin_t):
    def body(dk_ref, dv_ref, c_ref, s_ref, ko_ref, vo_ref):
        ko_ref[...] = _rope(dk_ref[...], c_ref[...], s_ref[...], -1.0).astype(_MXU)
        vo_ref[...] = dv_ref[...].astype(_MXU)

    tab = pl.BlockSpec((BLK, 128), lambda i: (i, 0))
    kv = pl.BlockSpec((BLK, D_KV), lambda i: (i, 0))
    return pl.pallas_call(
        body, name="rope_bwd_k", grid=(NBLK,), in_specs=[kv, kv, tab, tab], out_specs=[kv, kv],
        out_shape=[_sds((TP, D_KV), _MXU)] * 2, compiler_params=_cp(32, ("parallel",)),
    )(dk, dv, cos_t, sin_t)


def _attn_masks(i):
    ql = lax.broadcasted_iota(jnp.int32, (BLK, BLK), 0)
    kl = lax.broadcasted_iota(jnp.int32, (BLK, BLK), 1)
    meta = (kl >= PAD) & ((i >= 1) | (kl <= ql))
    prev = (i >= 2) & (kl > ql)
    cur = (i >= 1) & (kl <= ql)
    return meta, prev, cur


def _kv_specs():
    return [pl.BlockSpec((BLK, D_KV), lambda i: (0, 0)),
            pl.BlockSpec((BLK, D_KV), lambda i: (jnp.maximum(i - 1, 0), 0)),
            pl.BlockSpec((BLK, D_KV), lambda i: (i, 0))]


_NT = (((1,), (1,)), ((), ()))
_TN = (((0,), (0,)), ((), ()))


def _attn_fwd(q, k, v, sinks):
    def body(q_ref, k0_ref, kp_ref, kc_ref, v0_ref, vp_ref, vc_ref, sink_ref, o_ref, lse_ref):
        masks = _attn_masks(pl.program_id(0))
        k_refs = (k0_ref, kp_ref, kc_ref)
        v_refs = (v0_ref, vp_ref, vc_ref)
        for g in range(N_KV):
            gs = slice(HD * g, HD * (g + 1))
            ks = [r[:, gs] for r in k_refs]
            vs = [r[:, gs] for r in v_refs]
            for j in range(GRP):
                h = GRP * g + j
                hs = slice(HD * h, HD * (h + 1))
                qh = q_ref[:, hs]
                s = [jnp.where(m, lax.dot_general(qh, kk, _NT, preferred_element_type=F32) * (HD ** -0.5), NEG_INF)
                     for m, kk in zip(masks, ks)]
                sink = sink_ref[h]
                mx = jnp.maximum(jnp.maximum(jnp.max(s[0], -1, keepdims=True), jnp.max(s[1], -1, keepdims=True)),
                                 jnp.maximum(jnp.max(s[2], -1, keepdims=True), sink))
                p = [jnp.exp(t - mx) for t in s]
                den = (jnp.sum(p[0], -1, keepdims=True) + jnp.sum(p[1], -1, keepdims=True)
                       + jnp.sum(p[2], -1, keepdims=True) + jnp.exp(sink - mx))
                inv = 1.0 / den
                o = jnp.dot((p[0] * inv).astype(_MXU), vs[0], preferred_element_type=F32)
                o = o + jnp.dot((p[1] * inv).astype(_MXU), vs[1], preferred_element_type=F32)
                o = o + jnp.dot((p[2] * inv).astype(_MXU), vs[2], preferred_element_type=F32)
                o_ref[:, hs] = o
                lse_ref[:, h:h + 1] = mx + jnp.log(den)

    return pl.pallas_call(
        body, name="attn_fwd", grid=(NBLK,),
        in_specs=[pl.BlockSpec((BLK, D), lambda i: (i, 0))] + _kv_specs() + _kv_specs()
                 + [pl.BlockSpec(memory_space=pltpu.SMEM)],
        out_specs=[pl.BlockSpec((BLK, D), lambda i: (i, 0)), pl.BlockSpec((BLK, N_Q), lambda i: (i, 0))],
        out_shape=[_sds((TP, D), F32), _sds((TP, N_Q), F32)],
        compiler_params=_cp(32, ("parallel",)),
    )(q, k, k, k, v, v, v, sinks)


def _attn_bwd(q, k, v, sinks, do, o, lse, cos_t, sin_t):
    def body(q_ref, k0_ref, kp_ref, kc_ref, v0_ref, vp_ref, vc_ref, sink_ref, do_ref, o_ref, lse_ref, c_ref, s_ref,
             dq_ref, dk_ref, dv_ref, dsink_ref, dqrot_ref):
        i = pl.program_id(0)

        @pl.when(i == 0)
        def _():
            dk_ref[...] = jnp.zeros_like(dk_ref)
            dv_ref[...] = jnp.zeros_like(dv_ref)
            dsink_ref[...] = jnp.zeros_like(dsink_ref)

        masks = _attn_masks(i)
        k_refs = (k0_ref, kp_ref, kc_ref)
        v_refs = (v0_ref, vp_ref, vc_ref)
        row_starts = (0, pl.multiple_of(jnp.maximum(i - 1, 0) * BLK, BLK), pl.multiple_of(i * BLK, BLK))
        scale = HD ** -0.5
        for g in range(N_KV):
            gs = slice(HD * g, HD * (g + 1))
            ks = [r[:, gs] for r in k_refs]
            vs = [r[:, gs] for r in v_refs]
            dk_acc = [jnp.zeros((BLK, HD), F32) for _ in range(3)]
            dv_acc = [jnp.zeros((BLK, HD), F32) for _ in range(3)]
            for j in range(GRP):
                h = GRP * g + j
                hs = slice(HD * h, HD * (h + 1))
                qh = q_ref[:, hs]
                doh = do_ref[:, hs]
                dom = doh.astype(_MXU)
                delta = jnp.sum(doh * o_ref[:, hs], axis=-1, keepdims=True)
                lse_h = lse_ref[:, h:h + 1]
                sink = sink_ref[h]
                psink = jnp.exp(sink - lse_h)
                dsink_ref[0:1, h:h + 1] += jnp.sum(-psink * delta, axis=0, keepdims=True)
                dq = jnp.zeros((BLK, HD), F32)
                for part in range(3):
                    s = jnp.where(masks[part],
                                  lax.dot_general(qh, ks[part], _NT, preferred_element_type=F32) * scale, NEG_INF)
                    p = jnp.exp(s - lse_h)
                    dp = lax.dot_general(dom, vs[part], _NT, preferred_element_type=F32)
                    ds = (p * (dp - delta) * scale).astype(_MXU)
                    dq = dq + jnp.dot(ds, ks[part], preferred_element_type=F32)
                    dk_acc[part] = dk_acc[part] + lax.dot_general(ds, qh, _TN, preferred_element_type=F32)
                    dv_acc[part] = dv_acc[part] + lax.dot_general(p.astype(_MXU), dom, _TN, preferred_element_type=F32)
                dqrot_ref[:, hs] = dq
            for part in range(3):
                rows = pl.ds(row_starts[part], BLK)
                dk_ref[rows, gs] += dk_acc[part]
                dv_ref[rows, gs] += dv_acc[part]
        dq_ref[...] = _rope(dqrot_ref[...], c_ref[...], s_ref[...], -1.0).astype(_MXU)

    row = pl.BlockSpec((BLK, D), lambda i: (i, 0))
    tab = pl.BlockSpec((BLK, 128), lambda i: (i, 0))
    full_kv = pl.BlockSpec((TP, D_KV), lambda i: (0, 0))
    return pl.pallas_call(
        body, name="attn_bwd", grid=(NBLK,),
        in_specs=[row] + _kv_specs() + _kv_specs() + [pl.BlockSpec(memory_space=pltpu.SMEM), row, row,
                  pl.BlockSpec((BLK, N_Q), lambda i: (i, 0)), tab, tab],
        out_specs=[row, full_kv, full_kv, pl.BlockSpec((8, 128), lambda i: (0, 0))],
        out_shape=[_sds((TP, D), _MXU), _sds((TP, D_KV), F32), _sds((TP, D_KV), F32), _sds((8, 128), F32)],
        scratch_shapes=[pltpu.VMEM((BLK, D), F32)],
        compiler_params=_cp(40, ("arbitrary",)),
    )(q, k, k, k, v, v, v, sinks, do, o, lse, cos_t, sin_t)


def _local_step(x, target, meta, ln_emb_g, ln_emb_b, w_in4, b_in, conv_w, conv_b, w_ra, b_ra, w_ri, b_ri, lam, sinks,
                w_rnn_out, w_attn_out, w_o, b_o, ln_g, ln_b):
    cos_t, sin_t = _rope_tables()
    zero_bias = jnp.zeros((1, D), F32)
    h0p = jnp.concatenate([jnp.zeros((PAD, D), F32), meta, x], axis=0)

    h32, h16 = _ln_emb_fwd(h0p, ln_emb_g, ln_emb_b)
    z = _mm_nn(h16, w_in4, b_in, name="mm_z", tm=TP // 2, tn=TN_IN)
    a, u = _rnn_gates_fwd(z, conv_w, conv_b, w_ra, w_ri, b_ra, b_ri, lam)
    hr = _scan_fwd(a, u)
    ya_in = _mul_silu_fwd(hr, z, OFF_GR, name="gate_a_fwd")
    ya = _mm_nn(ya_in, w_rnn_out, zero_bias, name="mm_ya", tm=TP // 2, tn=512)
    q, k, v = _rope_fwd(z, cos_t, sin_t)
    o, lse = _attn_fwd(q, k, v, sinks)
    yb_in = _mul_silu_fwd(o, z, OFF_GA, name="gate_b_fwd")
    yb = _mm_nn(yb_in, w_attn_out, zero_bias, name="mm_yb", tm=TP // 2, tn=512)
    mixed = _merge_fwd(ya, yb, z)
    out = _mm_nn(mixed, w_o, b_o, name="mm_out", tm=TP // 2, tn=512)
    dr, sums_o = _ln_out_loss(h32, out, target, ln_g, ln_b)

    g = {"loss": sums_o[3, 0], "ln_g": sums_o[0:1], "ln_b": sums_o[1:2], "b_o": sums_o[2:3]}
    g["w_o"] = _mm_tn(mixed, dr, name="mm_dwo", tm=1024, tn=512, n_out_shards=1)
    dmix = _mm_nt(dr, w_o, name="mm_dmix", tm=TP // 2, tn=512, tk=D)
    dya, dyb, dmga, dmgb = _merge_bwd(dmix, ya, yb, z)
    g["w_rnn_out"] = _mm_tn(ya_in, dya, name="mm_dwrnn", tm=1024, tn=512, n_out_shards=1)
    g["w_attn_out"] = _mm_tn(yb_in, dyb, name="mm_dwattn", tm=1024, tn=512, n_out_shards=1)
    dya_in = _mm_nt(dya, w_rnn_out, name="mm_dyain", tm=TP // 2, tn=512, tk=D)
    dyb_in = _mm_nt(dyb, w_attn_out, name="mm_dybin", tm=TP // 2, tn=512, tk=D)
    dhr, dgr = _mul_silu_bwd(dya_in, hr, z, OFF_GR, name="gate_a_bwd")
    do, dga = _mul_silu_bwd(dyb_in, o, z, OFF_GA, name="gate_b_bwd")
    dq, dk_rot, dv32, dsink = _attn_bwd(q, k, v, sinks, do, o, lse, cos_t, sin_t)
    dk, dv = _rope_bwd_k(dk_rot, dv32, cos_t, sin_t)
    g["sinks"] = dsink[0:1, 0:N_Q]
    lam_s = _scan_bwd(a, dhr)
    dxr, g["w_ra"], g["w_ri"], sums_r = _rnn_gates_bwd(z, lam_s, hr, conv_w, conv_b, w_ra, w_ri, b_ra, b_ri, lam)
    g["conv_w"], g["conv_b"], g["b_ra"], g["b_ri"], g["lru_lambda"] = (
        sums_r[0:4], sums_r[4:5], sums_r[5:6], sums_r[6:7], sums_r[7:8])
    dz = jnp.concatenate([dxr, dgr, dq, dk, dv, dga, dmga, dmgb], axis=1)
    g["b_in"] = _colsum(dz, name="colsum_dz", tn=TN_IN)
    g["w_in"] = _mm_tn(h16, dz, name="mm_dwin", tm=1024, tn=TN_IN, n_out_shards=N_SH)
    dhz = _mm_nt(dz, w_in4, name="mm_dhz", tm=TP // 2, tn=1024, tk=TN_IN)
    dh0, sums_e = _ln_emb_bwd(h0p, ln_emb_g, dr, dhz)
    g["ln_emb_g"], g["ln_emb_b"] = sums_e[0:1], sums_e[1:2]
    g["x"] = dh0[BLK:]
    g["meta_tokens"] = dh0[PAD:BLK]
    return g


_ANY = pl.BlockSpec(memory_space=pl.ANY)


def _my_pos():
    return lax.axis_index("x"), lax.axis_index("y"), lax.axis_index("c")


def _other_chips(x, y):
    return [(1 - x, y), (x, 1 - y), (1 - x, 1 - y)]


def _remote(src, dst, send_sems, recv_sems, k, dev):
    return pltpu.make_async_remote_copy(src_ref=src, dst_ref=dst, send_sem=send_sems.at[k], recv_sem=recv_sems.at[k],
                                        device_id=dev, device_id_type=MESH)


def _gather_small(blk):
    r, cc = blk.shape

    def body(x_ref, o_ref, send_sems, recv_sems):
        x, y, c = _my_pos()
        me = 2 * x + y
        o_ref[me] = x_ref[...]
        sends = [_remote(x_ref, o_ref.at[me], send_sems, recv_sems, k, (px, py, c))
                 for k, (px, py) in enumerate(_other_chips(x, y))]
        for cp in sends:
            cp.start()
        for k, (px, py) in enumerate(_other_chips(x, y)):
            _remote(x_ref, o_ref.at[2 * px + py], send_sems, recv_sems, k, (px, py, c)).wait_recv()
        for cp in sends:
            cp.wait_send()

    vm = pl.BlockSpec(memory_space=pltpu.VMEM)
    return pl.pallas_call(
        body, name="gather_small", in_specs=[vm], out_specs=vm, out_shape=_sds((N_SH, r, cc), blk.dtype),
        scratch_shapes=[pltpu.SemaphoreType.DMA((3,)), pltpu.SemaphoreType.DMA((3,))],
        compiler_params=pltpu.CompilerParams(has_side_effects=True),
    )(blk)


def _gather_big(shard, *, name):
    r, cc = shard.shape
    half = r // 2

    def body(x_ref, o_ref, send_sems, recv_sems, local_sem):
        x, y, c = _my_pos()
        me = 2 * x + y
        chips = _other_chips(x, y)
        mine_rows = pl.ds(pl.multiple_of(c * half, 16), half)
        sib_rows = pl.ds(pl.multiple_of((1 - c) * half, 16), half)
        own = pltpu.make_async_copy(x_ref, o_ref.at[me], local_sem)
        own.start()
        sends = [_remote(x_ref.at[mine_rows], o_ref.at[me, mine_rows], send_sems, recv_sems, k, (px, py, c))
                 for k, (px, py) in enumerate(chips)]
        for cp in sends:
            cp.start()
        passed = []
        for k, (px, py) in enumerate(chips):
            got = o_ref.at[2 * px + py, mine_rows]
            _remote(got, got, send_sems, recv_sems, k, (px, py, c)).wait_recv()
            fwd = _remote(got, got, send_sems, recv_sems, 3 + k, (x, y, 1 - c))
            fwd.start()
            passed.append(fwd)
        for k, (px, py) in enumerate(chips):
            got = o_ref.at[2 * px + py, sib_rows]
            _remote(got, got, send_sems, recv_sems, 3 + k, (x, y, 1 - c)).wait_recv()
        for cp in sends + passed:
            cp.wait_send()
        own.wait()

    return pl.pallas_call(
        body, name=name, in_specs=[_ANY], out_specs=_ANY, out_shape=_sds((N_SH, r, cc), shard.dtype),
        scratch_shapes=[pltpu.SemaphoreType.DMA((6,)), pltpu.SemaphoreType.DMA((6,)), pltpu.SemaphoreType.DMA],
        compiler_params=pltpu.CompilerParams(has_side_effects=True),
    )(shard)


def _pair_send_other_half(g, *, name):
    n, r, cc = g.shape
    half = r // 2

    def body(g_ref, o_ref, send_sems, recv_sems):
        x, y, c = _my_pos()
        sib_rows = pl.ds(pl.multiple_of((1 - c) * half, 8), half)
        cps = [_remote(g_ref.at[s, sib_rows], o_ref.at[s], send_sems, recv_sems, s, (x, y, 1 - c)) for s in range(n)]
        for cp in cps:
            cp.start()
        for cp in cps:
            cp.wait_recv()
        for cp in cps:
            cp.wait_send()

    return pl.pallas_call(
        body, name=name, in_specs=[_ANY], out_specs=_ANY, out_shape=_sds((n, half, cc), g.dtype),
        scratch_shapes=[pltpu.SemaphoreType.DMA((n,)), pltpu.SemaphoreType.DMA((n,))],
        compiler_params=pltpu.CompilerParams(has_side_effects=True),
    )(g)


def _chip_scatter(s16, *, name):
    n, h, cc = s16.shape

    def body(s_ref, o_ref, send_sems, recv_sems):
        x, y, c = _my_pos()
        cps = [_remote(s_ref.at[2 * px + py], o_ref.at[k], send_sems, recv_sems, k, (px, py, c))
               for k, (px, py) in enumerate(_other_chips(x, y))]
        for cp in cps:
            cp.start()
        for cp in cps:
            cp.wait_recv()
        for cp in cps:
            cp.wait_send()

    return pl.pallas_call(
        body, name=name, in_specs=[_ANY], out_specs=_ANY, out_shape=_sds((3, h, cc), s16.dtype),
        scratch_shapes=[pltpu.SemaphoreType.DMA((3,)), pltpu.SemaphoreType.DMA((3,))],
        compiler_params=pltpu.CompilerParams(has_side_effects=True),
    )(s16)


def _pair_gather_halves(mine, *, name):
    h, cc = mine.shape

    def body(m_ref, o_ref, send_sems, recv_sems, local_sem):
        x, y, c = _my_pos()
        mine_rows = pl.ds(pl.multiple_of(c * h, 8), h)
        sib_rows = pl.ds(pl.multiple_of((1 - c) * h, 8), h)
        own = pltpu.make_async_copy(m_ref, o_ref.at[mine_rows], local_sem)
        own.start()
        cp = _remote(m_ref, o_ref.at[mine_rows], send_sems, recv_sems, 0, (x, y, 1 - c))
        cp.start()
        _remote(m_ref, o_ref.at[sib_rows], send_sems, recv_sems, 0, (x, y, 1 - c)).wait_recv()
        cp.wait_send()
        own.wait()

    return pl.pallas_call(
        body, name=name, in_specs=[_ANY], out_specs=_ANY, out_shape=_sds((2 * h, cc), mine.dtype),
        scratch_shapes=[pltpu.SemaphoreType.DMA((1,)), pltpu.SemaphoreType.DMA((1,)), pltpu.SemaphoreType.DMA],
        compiler_params=pltpu.CompilerParams(has_side_effects=True),
    )(mine)


N_DEV = 8


def _allreduce_small(pack):
    r, cc = pack.shape

    def body(x_ref, o_ref, buf_ref, send_sems, recv_sems):
        x, y, c = _my_pos()
        me = 4 * x + 2 * y + c
        buf_ref[me] = x_ref[...]
        cps = []
        for k in range(1, N_DEV):
            peer = (x ^ ((k >> 2) & 1), y ^ ((k >> 1) & 1), c ^ (k & 1))
            cps.append(_remote(x_ref, buf_ref.at[me], send_sems, recv_sems, k - 1, peer))
        for cp in cps:
            cp.start()
        for k in range(1, N_DEV):
            peer = (x ^ ((k >> 2) & 1), y ^ ((k >> 1) & 1), c ^ (k & 1))
            src = 4 * peer[0] + 2 * peer[1] + peer[2]
            _remote(x_ref, buf_ref.at[src], send_sems, recv_sems, k - 1, peer).wait_recv()
        acc = buf_ref[0]
        for d in range(1, N_DEV):
            acc = acc + buf_ref[d]
        o_ref[...] = acc
        for cp in cps:
            cp.wait_send()

    vm = pl.BlockSpec(memory_space=pltpu.VMEM)
    return pl.pallas_call(
        body, name="allreduce_small", in_specs=[vm], out_specs=vm, out_shape=_sds((r, cc), F32),
        scratch_shapes=[pltpu.VMEM((N_DEV, r, cc), F32), pltpu.SemaphoreType.DMA((N_DEV - 1,)),
                        pltpu.SemaphoreType.DMA((N_DEV - 1,))],
        compiler_params=pltpu.CompilerParams(has_side_effects=True),
    )(pack)


def _pair_add(g, got, pos, *, name, tr):
    n, r, cc = g.shape
    half = r // 2
    nt = half // tr

    def body(pos_ref, g_ref, r_ref, own_ref, s16_ref):
        s = g_ref[...] + r_ref[...]
        s16_ref[...] = s.astype(BF16)

        @pl.when(pl.program_id(1) == pos_ref[1])
        def _():
            own_ref[...] = s

    return pl.pallas_call(
        body, name=name,
        grid_spec=pltpu.PrefetchScalarGridSpec(
            num_scalar_prefetch=1, grid=(nt, n),
            in_specs=[pl.BlockSpec((None, tr, cc), lambda i, s, p: (s, p[0] * nt + i, 0)),
                      pl.BlockSpec((None, tr, cc), lambda i, s, p: (s, i, 0))],
            out_specs=[pl.BlockSpec((tr, cc), lambda i, s, p: (i, 0)),
                       pl.BlockSpec((None, tr, cc), lambda i, s, p: (s, i, 0))]),
        out_shape=[_sds((half, cc), F32), _sds((n, half, cc), BF16)],
        compiler_params=_cp(40, ("parallel", "arbitrary")),
    )(pos, g, got)


def _sum_chips(own, got, *, name, tr):
    h, cc = own.shape

    def body(o_ref, r_ref, out_ref):
        acc = o_ref[...]
        for k in range(3):
            acc = acc + r_ref[k].astype(F32)
        out_ref[...] = acc

    return pl.pallas_call(
        body, name=name, grid=(h // tr,),
        in_specs=[pl.BlockSpec((tr, cc), lambda i: (i, 0)), pl.BlockSpec((3, tr, cc), lambda i: (0, i, 0))],
        out_specs=pl.BlockSpec((tr, cc), lambda i: (i, 0)), out_shape=_sds((h, cc), F32),
        compiler_params=_cp(40, ("parallel",)),
    )(own, got)


def _reduce_grads(g, pos, *, name, tr):
    got = _pair_send_other_half(g, name=name + "_pair")
    own, s16 = _pair_add(g, got, pos, name=name + "_add", tr=tr)
    others = _chip_scatter(s16, name=name + "_chips")
    mine = _sum_chips(own, others, name=name + "_sum", tr=tr)
    return _pair_gather_halves(mine, name=name + "_halves")


def _adamw(w, g, m, v, *, name, tr):
    r, cc = w.shape

    def body(w_ref, g_ref, m_ref, v_ref, d_ref, mo_ref, vo_ref):
        gg = g_ref[...]
        m_new = ADAM_B1 * m_ref[...] + (1.0 - ADAM_B1) * gg
        v_new = ADAM_B2 * v_ref[...] + (1.0 - ADAM_B2) * (gg * gg)
        m_hat = m_new / (1.0 - ADAM_B1 ** ADAM_STEP)
        v_hat = v_new / (1.0 - ADAM_B2 ** ADAM_STEP)
        d_ref[...] = -ADAM_LR * (m_hat / (jnp.sqrt(v_hat) + ADAM_EPS) + ADAM_WD * w_ref[...])
        mo_ref[...] = m_new
        vo_ref[...] = v_new

    blk = pl.BlockSpec((tr, cc), lambda i: (i, 0))
    return pl.pallas_call(
        body, name=name, grid=(r // tr,), in_specs=[blk] * 4, out_specs=[blk] * 3,
        out_shape=[_sds((r, cc), F32)] * 3, compiler_params=_cp(48, ("parallel",)),
    )(w, g, m, v)


_SMALL_ROWS = 40
_B_IN_ROWS = 7


def _row_pad(v, rows):
    flat = v.reshape(-1)
    return jnp.pad(flat, (0, rows * D - flat.shape[0])).reshape(rows, D)


def _pack_ra(w):
    return w.reshape(64, D)


def _gate_full(g4):
    return g4.reshape(N_SH, N_RB, 64, RB).transpose(1, 0, 2, 3).reshape(N_RB, RB, RB)


def _gate_shards(w):
    return w.reshape(N_RB, N_SH, 64, RB).transpose(1, 0, 2, 3).reshape(N_SH, 64, D)


def kernel(x, meta_tokens, ln_emb_g, ln_emb_b, w_in, b_in, conv_w, conv_b, w_ra, b_ra, w_ri, b_ri, lru_lambda, sinks, w_rnn_out, w_attn_out, w_o, b_o, ln_g, ln_b, loss_target, m_meta_tokens, m_ln_emb_g, m_ln_emb_b, m_w_in, m_b_in, m_conv_w, m_conv_b, m_w_ra, m_b_ra, m_w_ri, m_b_ri, m_lru_lambda, m_sinks, m_w_rnn_out, m_w_attn_out, m_w_o, m_b_o, m_ln_g, m_ln_b, v_meta_tokens, v_ln_emb_g, v_ln_emb_b, v_w_in, v_b_in, v_conv_w, v_conv_b, v_w_ra, v_b_ra, v_w_ri, v_b_ri, v_lru_lambda, v_sinks, v_w_rnn_out, v_w_attn_out, v_w_o, v_b_o, v_ln_g, v_ln_b):
    xi, yi, ci = _my_pos()
    shard = 2 * xi + yi
    pos = jnp.stack([ci, shard]).astype(jnp.int32)

    small = jnp.concatenate([conv_w[0], meta_tokens, jnp.zeros((4, 512), F32)], axis=0)
    small4 = _gather_small(small)
    conv_w_full = small4[:, 0:4].transpose(1, 0, 2).reshape(CONV_W, D)
    meta_full = small4[:, 4:20].transpose(1, 0, 2).reshape(N_META, D)
    w_in4 = _gather_big(w_in[0].astype(BF16), name="gather_w_in")
    wpack = jnp.concatenate([w_rnn_out[0], w_attn_out[0], w_o[0], _pack_ra(w_ra[0]), _pack_ra(w_ri[0])], axis=0)
    wp4 = _gather_big(wpack.astype(BF16), name="gather_w_sq")
    sq = lambda k: wp4[:, 512 * k:512 * (k + 1)].reshape(1, D, D)
    w_ra_full = _gate_full(wp4[:, 1536:1600])
    w_ri_full = _gate_full(wp4[:, 1600:1664])

    g = _local_step(x[0], loss_target[0], meta_full, ln_emb_g[None], ln_emb_b[None], w_in4, b_in, conv_w_full, conv_b,
                    w_ra_full, b_ra, w_ri_full, b_ri, lru_lambda, sinks[0], sq(0), sq(1), sq(2), b_o, ln_g, ln_b)
    loss = lax.psum(g["loss"], ("x", "y", "c"))

    g_in = _reduce_grads(g["w_in"], pos, name="red_w_in", tr=128)
    gpack = jnp.concatenate([g["w_rnn_out"].reshape(N_SH, 512, D), g["w_attn_out"].reshape(N_SH, 512, D),
                             g["w_o"].reshape(N_SH, 512, D), _gate_shards(g["w_ra"]), _gate_shards(g["w_ri"])], axis=1)
    g_sq = _reduce_grads(gpack, pos, name="red_w_sq", tr=208)
    d_in, m_in, v_in = _adamw(w_in[0], g_in, m_w_in[0], v_w_in[0], name="adamw_w_in", tr=128)
    mpack = jnp.concatenate([m_w_rnn_out[0], m_w_attn_out[0], m_w_o[0], _pack_ra(m_w_ra[0]), _pack_ra(m_w_ri[0])], axis=0)
    vpack = jnp.concatenate([v_w_rnn_out[0], v_w_attn_out[0], v_w_o[0], _pack_ra(v_w_ra[0]), _pack_ra(v_w_ri[0])], axis=0)
    d_sq, m_sq, v_sq = _adamw(wpack, g_sq, mpack, vpack, name="adamw_w_sq", tr=208)

    spack = jnp.concatenate([
        g["ln_emb_g"], g["ln_emb_b"], _row_pad(g["b_in"], _B_IN_ROWS), g["conv_w"], g["conv_b"], g["b_ra"], g["b_ri"],
        g["lru_lambda"], _row_pad(g["sinks"], 1), g["b_o"], g["ln_g"], g["ln_b"], g["meta_tokens"],
        jnp.zeros((_SMALL_ROWS - 37, D), F32)], axis=0)
    sred = _allreduce_small(spack)
    col0 = shard * 512
    g_conv_w = lax.dynamic_slice(sred[9:13], (0, col0), (CONV_W, 512))
    g_meta = lax.dynamic_slice(sred[21:37], (0, col0), (N_META, 512))
    small_g = {"ln_emb_g": sred[0:1], "ln_emb_b": sred[1:2], "b_in": sred[2:9], "conv_w": g_conv_w.reshape(1, D),
               "conv_b": sred[13:14], "b_ra": sred[14:15], "b_ri": sred[15:16], "lru_lambda": sred[16:17],
               "sinks": sred[17:18], "b_o": sred[18:19], "ln_g": sred[19:20], "ln_b": sred[20:21],
               "meta_tokens": g_meta.reshape(4, D)}
    small_names = list(small_g)

    def small_pack(vals):
        rows = []
        for n in small_names:
            a = vals[n]
            if n == "b_in":
                a = _row_pad(a, _B_IN_ROWS)
            elif n == "sinks":
                a = _row_pad(a, 1)
            else:
                a = a.reshape(-1, D)
            rows.append(a)
        return jnp.concatenate(rows + [jnp.zeros((24 - 22, D), F32)], axis=0)

    w_small = dict(ln_emb_g=ln_emb_g, ln_emb_b=ln_emb_b, b_in=b_in, conv_w=conv_w, conv_b=conv_b, b_ra=b_ra, b_ri=b_ri,
                   lru_lambda=lru_lambda, sinks=sinks, b_o=b_o, ln_g=ln_g, ln_b=ln_b, meta_tokens=meta_tokens)
    m_small = dict(ln_emb_g=m_ln_emb_g, ln_emb_b=m_ln_emb_b, b_in=m_b_in, conv_w=m_conv_w, conv_b=m_conv_b, b_ra=m_b_ra,
                   b_ri=m_b_ri, lru_lambda=m_lru_lambda, sinks=m_sinks, b_o=m_b_o, ln_g=m_ln_g, ln_b=m_ln_b,
                   meta_tokens=m_meta_tokens)
    v_small = dict(ln_emb_g=v_ln_emb_g, ln_emb_b=v_ln_emb_b, b_in=v_b_in, conv_w=v_conv_w, conv_b=v_conv_b, b_ra=v_b_ra,
                   b_ri=v_b_ri, lru_lambda=v_lru_lambda, sinks=v_sinks, b_o=v_b_o, ln_g=v_ln_g, ln_b=v_ln_b,
                   meta_tokens=v_meta_tokens)
    g_small_pack = jnp.concatenate([small_g[n] for n in small_names] + [jnp.zeros((2, D), F32)], axis=0)
    d_sm, m_sm, v_sm = _adamw(small_pack(w_small), g_small_pack, small_pack(m_small), small_pack(v_small),
                              name="adamw_small", tr=24)

    small_rows = {}
    r0 = 0
    for n in small_names:
        nrows = small_g[n].shape[0]
        small_rows[n] = (r0, nrows)
        r0 += nrows

    def small_out(packed, n, like):
        a, nrows = small_rows[n]
        flat = packed[a:a + nrows].reshape(-1)
        return flat[:like.size].reshape(like.shape)

    def sq_out(packed, k, like):
        if k < 3:
            return packed[512 * k:512 * (k + 1)].reshape(like.shape)
        return packed[1536 + 64 * (k - 3):1600 + 64 * (k - 3)].reshape(like.shape)

    sq_idx = {"w_rnn_out": 0, "w_attn_out": 1, "w_o": 2, "w_ra": 3, "w_ri": 4}
    weights = dict(meta_tokens=meta_tokens, ln_emb_g=ln_emb_g, ln_emb_b=ln_emb_b, w_in=w_in, b_in=b_in, conv_w=conv_w,
                   conv_b=conv_b, w_ra=w_ra, b_ra=b_ra, w_ri=w_ri, b_ri=b_ri, lru_lambda=lru_lambda, sinks=sinks,
                   w_rnn_out=w_rnn_out, w_attn_out=w_attn_out, w_o=w_o, b_o=b_o, ln_g=ln_g, ln_b=ln_b)

    def outputs(big_in, big_sq, small):
        res = []
        for n, like in weights.items():
            if n == "w_in":
                res.append(big_in.reshape(like.shape))
            elif n in sq_idx:
                res.append(sq_out(big_sq, sq_idx[n], like))
            else:
                res.append(small_out(small, n, like))
        return res

    return (loss, g["x"][None], *outputs(g_in, g_sq, g_small_pack), *outputs(d_in, d_sq, d_sm),
            *outputs(m_in, m_sq, m_sm), *outputs(v_in, v_sq, v_sm))
```

```python
import functools

import jax
import jax.numpy as jnp
from jax import lax
from jax.experimental import pallas as pl
from jax.experimental.pallas import tpu as pltpu

F32 = jnp.float32
BF16 = jnp.bfloat16
_MXU = jnp.bfloat16

D = 2048
SEQ = 2048
N_META = 16
BLK = 128
PAD = BLK - N_META
TP = PAD + N_META + SEQ
NBLK = TP // BLK
N_RB = 8
RB = 256
CONV_W = 4
LRU_C = 8.0
HD = 64
N_Q = 32
N_KV = 4
GRP = 8
D_KV = 256
NEG_INF = -1e30
LN_EPS = 1e-5
ALPHA = 2.0 ** 0.25
ROPE_THETA = 10000.0
OFF_GR, OFF_Q, OFF_K, OFF_V, OFF_GA, OFF_G = 2048, 4096, 6144, 6400, 6656, 8704
D_IN = 12800
N_SH = 4
W_IN_COLS = D_IN // N_SH
TN_IN = 640

ADAM_LR = 0.001
ADAM_B1 = 0.9
ADAM_B2 = 0.999
ADAM_EPS = 1e-08
ADAM_WD = 0.01
ADAM_STEP = 10

MESH = pl.DeviceIdType.MESH
_MIB = 1024 * 1024


def _cp(vmem_mib=48, sem=None):
    return pltpu.CompilerParams(vmem_limit_bytes=vmem_mib * _MIB, dimension_semantics=sem)


def _sds(shape, dtype):
    return jax.ShapeDtypeStruct(shape, dtype)


def _sigmoid(x):
    return 1.0 / (1.0 + jnp.exp(-x))


def _row_ids(i, rows):
    return lax.broadcasted_iota(jnp.int32, (rows, 1), 0) + i * rows


def _mm_nn(a, w3, bias, *, name, tm, tn, out_dtype=F32):
    m, k = a.shape
    s, _, ns = w3.shape
    per = ns // tn

    def body(a_ref, w_ref, b_ref, o_ref):
        acc = jnp.dot(a_ref[...].astype(_MXU), w_ref[...].astype(_MXU), preferred_element_type=F32)
        o_ref[...] = (acc + b_ref[...]).astype(out_dtype)

    return pl.pallas_call(
        body, name=name, grid=(m // tm, s * per),
        in_specs=[pl.BlockSpec((tm, k), lambda i, j: (i, 0)),
                  pl.BlockSpec((None, k, tn), lambda i, j: (j // per, 0, j % per)),
                  pl.BlockSpec((1, tn), lambda i, j: (0, j))],
        out_specs=pl.BlockSpec((tm, tn), lambda i, j: (i, j)),
        out_shape=_sds((m, s * ns), out_dtype),
        compiler_params=_cp(56, ("parallel", "parallel")),
    )(a, w3, bias)


def _mm_nt(a, w3, *, name, tm, tn, tk):
    m, kk = a.shape
    s, ko, ns = w3.shape
    per = ns // tk
    nk = kk // tk

    def body(a_ref, w_ref, o_ref, acc_ref):
        part = lax.dot_general(a_ref[...].astype(_MXU), w_ref[...].astype(_MXU),
                               (((1,), (1,)), ((), ())), preferred_element_type=F32)
        if nk == 1:
            o_ref[...] = part
        else:
            kidx = pl.program_id(2)

            @pl.when(kidx == 0)
            def _():
                acc_ref[...] = part

            @pl.when(kidx > 0)
            def _():
                acc_ref[...] += part

            @pl.when(kidx == nk - 1)
            def _():
                o_ref[...] = acc_ref[...]

    return pl.pallas_call(
        body, name=name, grid=(m // tm, ko // tn, nk),
        in_specs=[pl.BlockSpec((tm, tk), lambda i, j, q: (i, q)),
                  pl.BlockSpec((None, tn, tk), lambda i, j, q: (q // per, j, q % per))],
        out_specs=pl.BlockSpec((tm, tn), lambda i, j, q: (i, j)),
        out_shape=_sds((m, ko), F32),
        scratch_shapes=[pltpu.VMEM((tm, tn) if nk > 1 else (8, 128), F32)],
        compiler_params=_cp(56, ("parallel", "parallel", "arbitrary")),
    )(a, w3)


def _mm_tn(a, b, *, name, tm, tn, n_out_shards):
    t, m = a.shape
    _, n = b.shape
    ns = n // n_out_shards
    per = ns // tn

    def body(a_ref, b_ref, o_ref):
        o_ref[...] = lax.dot_general(a_ref[...].astype(_MXU), b_ref[...].astype(_MXU),
                                     (((0,), (0,)), ((), ())), preferred_element_type=F32)

    return pl.pallas_call(
        body, name=name, grid=(m // tm, n // tn),
        in_specs=[pl.BlockSpec((t, tm), lambda i, j: (0, i)),
                  pl.BlockSpec((t, tn), lambda i, j: (0, j))],
        out_specs=pl.BlockSpec((None, tm, tn), lambda i, j: (j // per, i, j % per)),
        out_shape=_sds((n_out_shards, m, ns), F32),
        compiler_params=_cp(56, ("parallel", "parallel")),
    )(a, b)


def _ln_emb_fwd(h0p, g, b):
    def body(x_ref, g_ref, b_ref, h32_ref, h16_ref):
        x = x_ref[...]
        mu = jnp.mean(x, axis=-1, keepdims=True)
        xc = x - mu
        var = jnp.mean(xc * xc, axis=-1, keepdims=True)
        y = xc * lax.rsqrt(var + LN_EPS) * g_ref[...] + b_ref[...]
        h32_ref[...] = y
        h16_ref[...] = y.astype(_MXU)

    row = pl.BlockSpec((BLK, D), lambda i: (i, 0))
    vec = pl.BlockSpec((1, D), lambda i: (0, 0))
    return pl.pallas_call(
        body, name="ln_emb_fwd", grid=(NBLK,), in_specs=[row, vec, vec], out_specs=[row, row],
        out_shape=[_sds((TP, D), F32), _sds((TP, D), _MXU)], compiler_params=_cp(32, ("parallel",)),
    )(h0p, g, b)


def _ln_emb_bwd(h0p, g, dr, dhz):
    def body(x_ref, g_ref, dr_ref, dhz_ref, dx_ref, acc_ref):
        @pl.when(pl.program_id(0) == 0)
        def _():
            acc_ref[...] = jnp.zeros_like(acc_ref)

        x = x_ref[...]
        mu = jnp.mean(x, axis=-1, keepdims=True)
        xc = x - mu
        var = jnp.mean(xc * xc, axis=-1, keepdims=True)
        rstd = lax.rsqrt(var + LN_EPS)
        xhat = xc * rstd
        dh = ALPHA * dr_ref[...] + dhz_ref[...]
        acc_ref[0:1, :] += jnp.sum(dh * xhat, axis=0, keepdims=True)
        acc_ref[1:2, :] += jnp.sum(dh, axis=0, keepdims=True)
        dxh = dh * g_ref[...]
        m1 = jnp.mean(dxh, axis=-1, keepdims=True)
        m2 = jnp.mean(dxh * xhat, axis=-1, keepdims=True)
        dx_ref[...] = rstd * (dxh - m1 - xhat * m2)

    row = pl.BlockSpec((BLK, D), lambda i: (i, 0))
    vec = pl.BlockSpec((1, D), lambda i: (0, 0))
    return pl.pallas_call(
        body, name="ln_emb_bwd", grid=(NBLK,), in_specs=[row, vec, row, row],
        out_specs=[row, pl.BlockSpec((8, D), lambda i: (0, 0))],
        out_shape=[_sds((TP, D), F32), _sds((8, D), F32)], compiler_params=_cp(32, ("arbitrary",)),
    )(h0p, g, dr, dhz)


def _mul_silu_fwd(a, z, off, *, name):
    w = 512

    def body(a_ref, z_ref, o_ref):
        zz = z_ref[...]
        o_ref[...] = (a_ref[...] * (zz * _sigmoid(zz))).astype(_MXU)

    return pl.pallas_call(
        body, name=name, grid=(NBLK, D // w),
        in_specs=[pl.BlockSpec((BLK, w), lambda i, j: (i, j)),
                  pl.BlockSpec((BLK, w), lambda i, j: (i, off // w + j))],
        out_specs=pl.BlockSpec((BLK, w), lambda i, j: (i, j)),
        out_shape=_sds((TP, D), _MXU), compiler_params=_cp(32, ("parallel", "parallel")),
    )(a, z)


def _mul_silu_bwd(dy, a, z, off, *, name):
    w = 512

    def body(dy_ref, a_ref, z_ref, da_ref, dg_ref):
        zz = z_ref[...]
        sg = _sigmoid(zz)
        d = dy_ref[...]
        da_ref[...] = d * (zz * sg)
        dg_ref[...] = (d * a_ref[...] * (sg * (1.0 + zz * (1.0 - sg)))).astype(_MXU)

    blk = pl.BlockSpec((BLK, w), lambda i, j: (i, j))
    return pl.pallas_call(
        body, name=name, grid=(NBLK, D // w),
        in_specs=[blk, blk, pl.BlockSpec((BLK, w), lambda i, j: (i, off // w + j))],
        out_specs=[blk, blk], out_shape=[_sds((TP, D), F32), _sds((TP, D), _MXU)],
        compiler_params=_cp(32, ("parallel", "parallel")),
    )(dy, a, z)


def _merge_fwd(ya, yb, z):
    w = 512

    def body(ya_ref, yb_ref, ga_ref, gb_ref, o_ref):
        o_ref[...] = (_sigmoid(ga_ref[...]) * ya_ref[...] + _sigmoid(gb_ref[...]) * yb_ref[...]).astype(_MXU)

    blk = pl.BlockSpec((BLK, w), lambda i, j: (i, j))
    return pl.pallas_call(
        body, name="merge_fwd", grid=(NBLK, D // w),
        in_specs=[blk, blk, pl.BlockSpec((BLK, w), lambda i, j: (i, OFF_G // w + j)),
                  pl.BlockSpec((BLK, w), lambda i, j: (i, (OFF_G + D) // w + j))],
        out_specs=blk, out_shape=_sds((TP, D), _MXU), compiler_params=_cp(32, ("parallel", "parallel")),
    )(ya, yb, z, z)


def _merge_bwd(dmix, ya, yb, z):
    w = 512

    def body(dm_ref, ya_ref, yb_ref, ga_ref, gb_ref, dya_ref, dyb_ref, dga_ref, dgb_ref):
        dm = dm_ref[...]
        sa = _sigmoid(ga_ref[...])
        sb = _sigmoid(gb_ref[...])
        dya_ref[...] = (dm * sa).astype(_MXU)
        dyb_ref[...] = (dm * sb).astype(_MXU)
        dga_ref[...] = (dm * ya_ref[...] * sa * (1.0 - sa)).astype(_MXU)
        dgb_ref[...] = (dm * yb_ref[...] * sb * (1.0 - sb)).astype(_MXU)

    blk = pl.BlockSpec((BLK, w), lambda i, j: (i, j))
    return pl.pallas_call(
        body, name="merge_bwd", grid=(NBLK, D // w),
        in_specs=[blk, blk, blk, pl.BlockSpec((BLK, w), lambda i, j: (i, OFF_G // w + j)),
                  pl.BlockSpec((BLK, w), lambda i, j: (i, (OFF_G + D) // w + j))],
        out_specs=[blk] * 4, out_shape=[_sds((TP, D), _MXU)] * 4,
        compiler_params=_cp(32, ("parallel", "parallel")),
    )(dmix, ya, yb, z, z)


def _ln_out_loss(h32, out, target, g, b):
    def body(h_ref, o_ref, t_ref, g_ref, b_ref, dr_ref, acc_ref):
        i = pl.program_id(0)

        @pl.when(i == 0)
        def _():
            acc_ref[...] = jnp.zeros_like(acc_ref)

        r = ALPHA * h_ref[...] + o_ref[...]
        mu = jnp.mean(r, axis=-1, keepdims=True)
        rc = r - mu
        var = jnp.mean(rc * rc, axis=-1, keepdims=True)
        rstd = lax.rsqrt(var + LN_EPS)
        xhat = rc * rstd
        gg = g_ref[...]
        y = xhat * gg + b_ref[...]
        real = (i >= 1).astype(F32)
        diff = (y - t_ref[...]) * real
        dy = diff * (1.0 / D)
        dxh = dy * gg
        m1 = jnp.mean(dxh, axis=-1, keepdims=True)
        m2 = jnp.mean(dxh * xhat, axis=-1, keepdims=True)
        dr = rstd * (dxh - m1 - xhat * m2)
        dr_ref[...] = dr
        acc_ref[0:1, :] += jnp.sum(dy * xhat, axis=0, keepdims=True)
        acc_ref[1:2, :] += jnp.sum(dy, axis=0, keepdims=True)
        acc_ref[2:3, :] += jnp.sum(dr, axis=0, keepdims=True)
        acc_ref[3:4, :] += (0.5 / D) * jnp.sum(diff * diff)

    row = pl.BlockSpec((BLK, D), lambda i: (i, 0))
    vec = pl.BlockSpec((1, D), lambda i: (0, 0))
    return pl.pallas_call(
        body, name="ln_out_loss", grid=(NBLK,),
        in_specs=[row, row, pl.BlockSpec((BLK, D), lambda i: (jnp.maximum(i - 1, 0), 0)), vec, vec],
        out_specs=[row, pl.BlockSpec((8, D), lambda i: (0, 0))],
        out_shape=[_sds((TP, D), F32), _sds((8, D), F32)], compiler_params=_cp(32, ("arbitrary",)),
    )(h32, out, target, g, b)


def _colsum(x, *, name, tn):
    _, n = x.shape

    def body(x_ref, o_ref):
        o_ref[...] = jnp.sum(x_ref[...].astype(F32), axis=0, keepdims=True)

    return pl.pallas_call(
        body, name=name, grid=(n // tn,), in_specs=[pl.BlockSpec((TP, tn), lambda j: (0, j))],
        out_specs=pl.BlockSpec((1, tn), lambda j: (0, j)), out_shape=_sds((1, n), F32),
        compiler_params=_cp(32, ("parallel",)),
    )(x)


def _rnn_recompute(xr_ref, cw_ref, cb_ref, wra_ref, wri_ref, bra_ref, bri_ref, lam_ref):
    rows = _row_ids(0, TP)
    valid = (rows >= PAD).astype(F32)
    first = rows == PAD
    x = xr_ref[...] * valid
    cw = cw_ref[...]
    shifted = [x, pltpu.roll(x, 1, 0), pltpu.roll(x, 2, 0), pltpu.roll(x, 3, 0)]
    c = cb_ref[...] + cw[0:1, :] * shifted[0] + cw[1:2, :] * shifted[1] + cw[2:3, :] * shifted[2] + cw[3:4, :] * shifted[3]
    cm = c.astype(_MXU)
    gr = _sigmoid(jnp.dot(cm, wra_ref[...].astype(_MXU), preferred_element_type=F32) + bra_ref[...])
    gi = _sigmoid(jnp.dot(cm, wri_ref[...].astype(_MXU), preferred_element_type=F32) + bri_ref[...])
    lam = lam_ref[...]
    ls = jnp.minimum(lam, 0.0) - jnp.log(1.0 + jnp.exp(-jnp.abs(lam)))
    log_a = LRU_C * gr * ls
    a = jnp.exp(log_a)
    mult = jnp.where(first, 1.0, jnp.sqrt(1.0 - jnp.exp(2.0 * log_a)))
    return dict(valid=valid, first=first, shifted=shifted, c=c, cm=cm, gr=gr, gi=gi, ls=ls, a=a, mult=mult, lam=lam)


def _rnn_specs():
    col = pl.BlockSpec((TP, RB), lambda n: (0, n))
    vec = pl.BlockSpec((1, RB), lambda n: (0, n))
    return dict(col=col, vec=vec, cw=pl.BlockSpec((CONV_W, RB), lambda n: (0, n)),
                wblk=pl.BlockSpec((None, RB, RB), lambda n: (n, 0, 0)))


def _rnn_gates_fwd(z, conv_w, conv_b, w_ra, w_ri, b_ra, b_ri, lam):
    def body(xr_ref, cw_ref, cb_ref, wra_ref, wri_ref, bra_ref, bri_ref, lam_ref, a_ref, u_ref):
        r = _rnn_recompute(xr_ref, cw_ref, cb_ref, wra_ref, wri_ref, bra_ref, bri_ref, lam_ref)
        a_ref[...] = r["a"]
        u_ref[...] = r["mult"] * r["gi"] * r["c"] * r["valid"]

    s = _rnn_specs()
    return pl.pallas_call(
        body, name="rnn_gates_fwd", grid=(N_RB,),
        in_specs=[s["col"], s["cw"], s["vec"], s["wblk"], s["wblk"], s["vec"], s["vec"], s["vec"]],
        out_specs=[s["col"], s["col"]], out_shape=[_sds((TP, D), F32)] * 2,
        compiler_params=_cp(56, ("parallel",)),
    )(z, conv_w, conv_b, w_ra, w_ri, b_ra, b_ri, lam)


SCAN_ROWS = 272


def _scan_fwd(a, u):
    def body(a_ref, u_ref, h_ref, carry_ref):
        @pl.when(pl.program_id(0) == 0)
        def _():
            carry_ref[...] = jnp.zeros_like(carry_ref)

        def step(t, h):
            h = a_ref[t] * h + u_ref[t]
            h_ref[t] = h
            return h

        carry_ref[...] = lax.fori_loop(0, SCAN_ROWS, step, carry_ref[...], unroll=8)

    blk = pl.BlockSpec((SCAN_ROWS, 16, 128), lambda i: (i, 0, 0))
    h = pl.pallas_call(
        body, name="scan_fwd", grid=(TP // SCAN_ROWS,), in_specs=[blk, blk], out_specs=blk,
        out_shape=_sds((TP, 16, 128), F32), scratch_shapes=[pltpu.VMEM((16, 128), F32)],
        compiler_params=_cp(32, ("arbitrary",)),
    )(a.reshape(TP, 16, 128), u.reshape(TP, 16, 128))
    return h.reshape(TP, D)


def _scan_bwd(a, dh):
    nst = TP // SCAN_ROWS

    def body(a_ref, d_ref, o_ref, lam_ref, anext_ref):
        @pl.when(pl.program_id(0) == 0)
        def _():
            lam_ref[...] = jnp.zeros_like(lam_ref)
            anext_ref[...] = jnp.zeros_like(anext_ref)

        def step(q, carry):
            lam, an = carry
            t = SCAN_ROWS - 1 - q
            lam = d_ref[t] + an * lam
            o_ref[t] = lam
            return lam, a_ref[t]

        lam, an = lax.fori_loop(0, SCAN_ROWS, step, (lam_ref[...], anext_ref[...]), unroll=8)
        lam_ref[...] = lam
        anext_ref[...] = an

    blk = pl.BlockSpec((SCAN_ROWS, 16, 128), lambda i: (nst - 1 - i, 0, 0))
    o = pl.pallas_call(
        body, name="scan_bwd", grid=(nst,), in_specs=[blk, blk], out_specs=blk,
        out_shape=_sds((TP, 16, 128), F32),
        scratch_shapes=[pltpu.VMEM((16, 128), F32), pltpu.VMEM((16, 128), F32)],
        compiler_params=_cp(32, ("arbitrary",)),
    )(a.reshape(TP, 16, 128), dh.reshape(TP, 16, 128))
    return o.reshape(TP, D)


def _rnn_gates_bwd(z, lam_s, hr, conv_w, conv_b, w_ra, w_ri, b_ra, b_ri, lam):
    def body(xr_ref, ls_ref, hr_ref, cw_ref, cb_ref, wra_ref, wri_ref, bra_ref, bri_ref, lam_ref,
             dx_ref, dwra_ref, dwri_ref, sums_ref):
        r = _rnn_recompute(xr_ref, cw_ref, cb_ref, wra_ref, wri_ref, bra_ref, bri_ref, lam_ref)
        valid, c, gr, gi, a, mult = r["valid"], r["c"], r["gr"], r["gi"], r["a"], r["mult"]
        du = ls_ref[...] * valid
        da = du * pltpu.roll(hr_ref[...], 1, 0)
        d_gi = du * mult * c
        dc = du * mult * gi
        dmult = du * gi * c
        dlog_a = da * a + jnp.where(r["first"], 0.0, -dmult * a * a / mult)
        d_gr = dlog_a * (LRU_C * r["ls"])
        dls = jnp.sum(dlog_a * (LRU_C * gr), axis=0, keepdims=True)
        dpre_r = d_gr * gr * (1.0 - gr)
        dpre_i = d_gi * gi * (1.0 - gi)
        pr = dpre_r.astype(_MXU)
        pi = dpre_i.astype(_MXU)
        tn = (((0,), (0,)), ((), ()))
        nt = (((1,), (1,)), ((), ()))
        dwra_ref[...] = lax.dot_general(r["cm"], pr, tn, preferred_element_type=F32)
        dwri_ref[...] = lax.dot_general(r["cm"], pi, tn, preferred_element_type=F32)
        dc = dc + lax.dot_general(pr, wra_ref[...].astype(_MXU), nt, preferred_element_type=F32)
        dc = dc + lax.dot_general(pi, wri_ref[...].astype(_MXU), nt, preferred_element_type=F32)
        cw = cw_ref[...]
        dx = cw[0:1, :] * dc
        for k in range(1, CONV_W):
            dx = dx + cw[k:k + 1, :] * pltpu.roll(dc, TP - k, 0)
        dx_ref[...] = (dx * valid).astype(_MXU)
        for k in range(CONV_W):
            sums_ref[k:k + 1, :] = jnp.sum(dc * r["shifted"][k], axis=0, keepdims=True)
        sums_ref[4:5, :] = jnp.sum(dc, axis=0, keepdims=True)
        sums_ref[5:6, :] = jnp.sum(dpre_r, axis=0, keepdims=True)
        sums_ref[6:7, :] = jnp.sum(dpre_i, axis=0, keepdims=True)
        sums_ref[7:8, :] = dls * _sigmoid(-r["lam"])

    s = _rnn_specs()
    return pl.pallas_call(
        body, name="rnn_gates_bwd", grid=(N_RB,),
        in_specs=[s["col"], s["col"], s["col"], s["cw"], s["vec"], s["wblk"], s["wblk"], s["vec"], s["vec"], s["vec"]],
        out_specs=[s["col"], s["wblk"], s["wblk"], pl.BlockSpec((8, RB), lambda n: (0, n))],
        out_shape=[_sds((TP, D), _MXU), _sds((N_RB, RB, RB), F32), _sds((N_RB, RB, RB), F32), _sds((8, D), F32)],
        compiler_params=_cp(60, ("parallel",)),
    )(z, lam_s, hr, conv_w, conv_b, w_ra, w_ri, b_ra, b_ri, lam)


def _rope_tables():
    half = HD // 2
    inv = ROPE_THETA ** (-jnp.arange(half, dtype=F32) / half)
    pos = (jnp.arange(TP) - PAD).astype(F32)
    ang = pos[:, None] * inv[None, :]
    return jnp.tile(jnp.cos(ang), (1, 4)), jnp.tile(jnp.sin(ang), (1, 4))


def _rope(x, cos_t, sin_t, sign):
    w = x.shape[1]
    lane = lax.broadcasted_iota(jnp.int32, x.shape, 1)
    first = (lane % HD) < (HD // 2)
    swapped = jnp.where(first, pltpu.roll(x, w - HD // 2, 1), pltpu.roll(x, HD // 2, 1))
    ct = jnp.tile(cos_t, (1, w // 128))
    st = jnp.tile(sin_t, (1, w // 128))
    return x * ct + swapped * jnp.where(first, -sign * st, sign * st)


def _rope_fwd(z, cos_t, sin_t):
    def body(q_ref, k_ref, v_ref, c_ref, s_ref, qo_ref, ko_ref, vo_ref):
        c = c_ref[...]
        s = s_ref[...]
        qo_ref[...] = _rope(q_ref[...], c, s, 1.0).astype(_MXU)
        ko_ref[...] = _rope(k_ref[...], c, s, 1.0).astype(_MXU)
        vo_ref[...] = v_ref[...].astype(_MXU)

    tab = pl.BlockSpec((BLK, 128), lambda i: (i, 0))
    return pl.pallas_call(
        body, name="rope_fwd", grid=(NBLK,),
        in_specs=[pl.BlockSpec((BLK, D), lambda i: (i, OFF_Q // D)),
                  pl.BlockSpec((BLK, D_KV), lambda i: (i, OFF_K // D_KV)),
                  pl.BlockSpec((BLK, D_KV), lambda i: (i, OFF_V // D_KV)), tab, tab],
        out_specs=[pl.BlockSpec((BLK, D), lambda i: (i, 0)), pl.BlockSpec((BLK, D_KV), lambda i: (i, 0)),
                   pl.BlockSpec((BLK, D_KV), lambda i: (i, 0))],
        out_shape=[_sds((TP, D), _MXU), _sds((TP, D_KV), _MXU), _sds((TP, D_KV), _MXU)],
        compiler_params=_cp(32, ("parallel",)),
    )(z, z, z, cos_t, sin_t)


def _rope_bwd_k(dk, dv, cos_t, sin_t):
    def body(dk_ref, dv_ref, c_ref, s_ref, ko_ref, vo_ref):
        ko_ref[...] = _rope(dk_ref[...], c_ref[...], s_ref[...], -1.0).astype(_MXU)
        vo_ref[...] = dv_ref[...].astype(_MXU)

    tab = pl.BlockSpec((BLK, 128), lambda i: (i, 0))
    kv = pl.BlockSpec((BLK, D_KV), lambda i: (i, 0))
    return pl.pallas_call(
        body, name="rope_bwd_k", grid=(NBLK,), in_specs=[kv, kv, tab, tab], out_specs=[kv, kv],
        out_shape=[_sds((TP, D_KV), _MXU)] * 2, compiler_params=_cp(32, ("parallel",)),
    )(dk, dv, cos_t, sin_t)


def _attn_masks(i):
    ql = lax.broadcasted_iota(jnp.int32, (GRP * BLK, BLK), 0) % BLK
    kl = lax.broadcasted_iota(jnp.int32, (GRP * BLK, BLK), 1)
    meta = (kl >= PAD) & ((i >= 1) | (kl <= ql))
    prev = (i >= 2) & (kl > ql)
    cur = (i >= 1) & (kl <= ql)
    return meta, prev, cur


def _kv_specs():
    return [pl.BlockSpec((BLK, D_KV), lambda i: (0, 0)),
            pl.BlockSpec((BLK, D_KV), lambda i: (jnp.maximum(i - 1, 0), 0)),
            pl.BlockSpec((BLK, D_KV), lambda i: (i, 0))]


_NT = (((1,), (1,)), ((), ()))
_TN = (((0,), (0,)), ((), ()))


def _stack_heads(ref, g, width=HD):
    return jnp.concatenate([ref[:, width * (GRP * g + j):width * (GRP * g + j + 1)] for j in range(GRP)], axis=0)


def _sink_column(sink_ref, g):
    return jnp.concatenate([jnp.full((BLK, 1), sink_ref[GRP * g + j], F32) for j in range(GRP)], axis=0)


def _attn_fwd(q, k, v, sinks):
    def body(q_ref, k0_ref, kp_ref, kc_ref, v0_ref, vp_ref, vc_ref, sink_ref, o_ref, lse_ref):
        masks = _attn_masks(pl.program_id(0))
        k_refs = (k0_ref, kp_ref, kc_ref)
        v_refs = (v0_ref, vp_ref, vc_ref)
        for g in range(N_KV):
            gs = slice(HD * g, HD * (g + 1))
            q8 = _stack_heads(q_ref, g)
            sink = _sink_column(sink_ref, g)
            s = [jnp.where(m, lax.dot_general(q8, r[:, gs], _NT, preferred_element_type=F32) * (HD ** -0.5), NEG_INF)
                 for m, r in zip(masks, k_refs)]
            mx = jnp.maximum(jnp.maximum(jnp.max(s[0], -1, keepdims=True), jnp.max(s[1], -1, keepdims=True)),
                             jnp.maximum(jnp.max(s[2], -1, keepdims=True), sink))
            p = [jnp.exp(t - mx) for t in s]
            den = (jnp.sum(p[0], -1, keepdims=True) + jnp.sum(p[1], -1, keepdims=True)
                   + jnp.sum(p[2], -1, keepdims=True) + jnp.exp(sink - mx))
            inv = 1.0 / den
            o8 = jnp.dot((p[0] * inv).astype(_MXU), v_refs[0][:, gs], preferred_element_type=F32)
            o8 = o8 + jnp.dot((p[1] * inv).astype(_MXU), v_refs[1][:, gs], preferred_element_type=F32)
            o8 = o8 + jnp.dot((p[2] * inv).astype(_MXU), v_refs[2][:, gs], preferred_element_type=F32)
            lse8 = mx + jnp.log(den)
            for j in range(GRP):
                h = GRP * g + j
                o_ref[:, HD * h:HD * (h + 1)] = o8[BLK * j:BLK * (j + 1)]
                lse_ref[:, h:h + 1] = lse8[BLK * j:BLK * (j + 1)]

    return pl.pallas_call(
        body, name="attn_fwd", grid=(NBLK,),
        in_specs=[pl.BlockSpec((BLK, D), lambda i: (i, 0))] + _kv_specs() + _kv_specs()
                 + [pl.BlockSpec(memory_space=pltpu.SMEM)],
        out_specs=[pl.BlockSpec((BLK, D), lambda i: (i, 0)), pl.BlockSpec((BLK, N_Q), lambda i: (i, 0))],
        out_shape=[_sds((TP, D), F32), _sds((TP, N_Q), F32)],
        compiler_params=_cp(40, ("parallel",)),
    )(q, k, k, k, v, v, v, sinks)


def _attn_bwd(q, k, v, sinks, do, o, lse, cos_t, sin_t):
    def body(q_ref, k0_ref, kp_ref, kc_ref, v0_ref, vp_ref, vc_ref, sink_ref, do_ref, o_ref, lse_ref, c_ref, s_ref,
             dq_ref, dk_ref, dv_ref, dsink_ref, dqrot_ref):
        i = pl.program_id(0)

        @pl.when(i == 0)
        def _():
            dk_ref[...] = jnp.zeros_like(dk_ref)
            dv_ref[...] = jnp.zeros_like(dv_ref)
            dsink_ref[...] = jnp.zeros_like(dsink_ref)

        masks = _attn_masks(i)
        k_refs = (k0_ref, kp_ref, kc_ref)
        v_refs = (v0_ref, vp_ref, vc_ref)
        row_starts = (0, pl.multiple_of(jnp.maximum(i - 1, 0) * BLK, BLK), pl.multiple_of(i * BLK, BLK))
        scale = HD ** -0.5
        for g in range(N_KV):
            gs = slice(HD * g, HD * (g + 1))
            q8 = _stack_heads(q_ref, g)
            do8 = _stack_heads(do_ref, g)
            dom = do8.astype(_MXU)
            delta = jnp.sum(do8 * _stack_heads(o_ref, g), axis=-1, keepdims=True)
            lse8 = _stack_heads(lse_ref, g, width=1)
            dsk = -jnp.exp(_sink_column(sink_ref, g) - lse8) * delta
            for j in range(GRP):
                h = GRP * g + j
                dsink_ref[0:1, h:h + 1] += jnp.sum(dsk[BLK * j:BLK * (j + 1)], axis=0, keepdims=True)
            dq8 = jnp.zeros((GRP * BLK, HD), F32)
            for part in range(3):
                kk = k_refs[part][:, gs]
                s = jnp.where(masks[part], lax.dot_general(q8, kk, _NT, preferred_element_type=F32) * scale, NEG_INF)
                p = jnp.exp(s - lse8)
                dp = lax.dot_general(dom, v_refs[part][:, gs], _NT, preferred_element_type=F32)
                ds = (p * (dp - delta) * scale).astype(_MXU)
                dq8 = dq8 + jnp.dot(ds, kk, preferred_element_type=F32)
                rows = pl.ds(row_starts[part], BLK)
                dk_ref[rows, gs] += lax.dot_general(ds, q8, _TN, preferred_element_type=F32)
                dv_ref[rows, gs] += lax.dot_general(p.astype(_MXU), dom, _TN, preferred_element_type=F32)
            for j in range(GRP):
                h = GRP * g + j
                dqrot_ref[:, HD * h:HD * (h + 1)] = dq8[BLK * j:BLK * (j + 1)]
        dq_ref[...] = _rope(dqrot_ref[...], c_ref[...], s_ref[...], -1.0).astype(_MXU)

    row = pl.BlockSpec((BLK, D), lambda i: (i, 0))
    tab = pl.BlockSpec((BLK, 128), lambda i: (i, 0))
    full_kv = pl.BlockSpec((TP, D_KV), lambda i: (0, 0))
    return pl.pallas_call(
        body, name="attn_bwd", grid=(NBLK,),
        in_specs=[row] + _kv_specs() + _kv_specs() + [pl.BlockSpec(memory_space=pltpu.SMEM), row, row,
                  pl.BlockSpec((BLK, N_Q), lambda i: (i, 0)), tab, tab],
        out_specs=[row, full_kv, full_kv, pl.BlockSpec((8, 128), lambda i: (0, 0))],
        out_shape=[_sds((TP, D), _MXU), _sds((TP, D_KV), F32), _sds((TP, D_KV), F32), _sds((8, 128), F32)],
        scratch_shapes=[pltpu.VMEM((BLK, D), F32)],
        compiler_params=_cp(48, ("arbitrary",)),
    )(q, k, k, k, v, v, v, sinks, do, o, lse, cos_t, sin_t)


def _local_step(x, target, meta, ln_emb_g, ln_emb_b, w_in4, b_in, conv_w, conv_b, w_ra, b_ra, w_ri, b_ri, lam, sinks,
                w_rnn_out, w_attn_out, w_o, b_o, ln_g, ln_b):
    cos_t, sin_t = _rope_tables()
    zero_bias = jnp.zeros((1, D), F32)
    h0p = jnp.concatenate([jnp.zeros((PAD, D), F32), meta, x], axis=0)

    h32, h16 = _ln_emb_fwd(h0p, ln_emb_g, ln_emb_b)
    z = _mm_nn(h16, w_in4, b_in, name="mm_z", tm=TP // 2, tn=TN_IN)
    a, u = _rnn_gates_fwd(z, conv_w, conv_b, w_ra, w_ri, b_ra, b_ri, lam)
    hr = _scan_fwd(a, u)
    ya_in = _mul_silu_fwd(hr, z, OFF_GR, name="gate_a_fwd")
    ya = _mm_nn(ya_in, w_rnn_out, zero_bias, name="mm_ya", tm=TP // 2, tn=512)
    q, k, v = _rope_fwd(z, cos_t, sin_t)
    o, lse = _attn_fwd(q, k, v, sinks)
    yb_in = _mul_silu_fwd(o, z, OFF_GA, name="gate_b_fwd")
    yb = _mm_nn(yb_in, w_attn_out, zero_bias, name="mm_yb", tm=TP // 2, tn=512)
    mixed = _merge_fwd(ya, yb, z)
    out = _mm_nn(mixed, w_o, b_o, name="mm_out", tm=TP // 2, tn=512)
    dr, sums_o = _ln_out_loss(h32, out, target, ln_g, ln_b)

    g = {"loss": sums_o[3, 0], "ln_g": sums_o[0:1], "ln_b": sums_o[1:2], "b_o": sums_o[2:3]}
    g["w_o"] = _mm_tn(mixed, dr, name="mm_dwo", tm=1024, tn=512, n_out_shards=1)
    dmix = _mm_nt(dr, w_o, name="mm_dmix", tm=TP // 2, tn=512, tk=D)
    dya, dyb, dmga, dmgb = _merge_bwd(dmix, ya, yb, z)
    g["w_rnn_out"] = _mm_tn(ya_in, dya, name="mm_dwrnn", tm=1024, tn=512, n_out_shards=1)
    g["w_attn_out"] = _mm_tn(yb_in, dyb, name="mm_dwattn", tm=1024, tn=512, n_out_shards=1)
    dya_in = _mm_nt(dya, w_rnn_out, name="mm_dyain", tm=TP // 2, tn=512, tk=D)
    dyb_in = _mm_nt(dyb, w_attn_out, name="mm_dybin", tm=TP // 2, tn=512, tk=D)
    dhr, dgr = _mul_silu_bwd(dya_in, hr, z, OFF_GR, name="gate_a_bwd")
    do, dga = _mul_silu_bwd(dyb_in, o, z, OFF_GA, name="gate_b_bwd")
    dq, dk_rot, dv32, dsink = _attn_bwd(q, k, v, sinks, do, o, lse, cos_t, sin_t)
    dk, dv = _rope_bwd_k(dk_rot, dv32, cos_t, sin_t)
    g["sinks"] = dsink[0:1, 0:N_Q]
    lam_s = _scan_bwd(a, dhr)
    dxr, g["w_ra"], g["w_ri"], sums_r = _rnn_gates_bwd(z, lam_s, hr, conv_w, conv_b, w_ra, w_ri, b_ra, b_ri, lam)
    g["conv_w"], g["conv_b"], g["b_ra"], g["b_ri"], g["lru_lambda"] = (
        sums_r[0:4], sums_r[4:5], sums_r[5:6], sums_r[6:7], sums_r[7:8])
    dz = jnp.concatenate([dxr, dgr, dq, dk, dv, dga, dmga, dmgb], axis=1)
    g["b_in"] = _colsum(dz, name="colsum_dz", tn=TN_IN)
    g["w_in"] = _mm_tn(h16, dz, name="mm_dwin", tm=1024, tn=TN_IN, n_out_shards=N_SH)
    dhz = _mm_nt(dz, w_in4, name="mm_dhz", tm=TP // 2, tn=1024, tk=TN_IN)
    dh0, sums_e = _ln_emb_bwd(h0p, ln_emb_g, dr, dhz)
    g["ln_emb_g"], g["ln_emb_b"] = sums_e[0:1], sums_e[1:2]
    g["x"] = dh0[BLK:]
    g["meta_tokens"] = dh0[PAD:BLK]
    return g


_ANY = pl.BlockSpec(memory_space=pl.ANY)


def _my_pos():
    return lax.axis_index("x"), lax.axis_index("y"), lax.axis_index("c")


def _other_chips(x, y):
    return [(1 - x, y), (x, 1 - y), (1 - x, 1 - y)]


def _remote(src, dst, send_sems, recv_sems, k, dev):
    return pltpu.make_async_remote_copy(src_ref=src, dst_ref=dst, send_sem=send_sems.at[k], recv_sem=recv_sems.at[k],
                                        device_id=dev, device_id_type=MESH)


def _gather_small(blk):
    r, cc = blk.shape

    def body(x_ref, o_ref, send_sems, recv_sems):
        x, y, c = _my_pos()
        me = 2 * x + y
        o_ref[me] = x_ref[...]
        sends = [_remote(x_ref, o_ref.at[me], send_sems, recv_sems, k, (px, py, c))
                 for k, (px, py) in enumerate(_other_chips(x, y))]
        for cp in sends:
            cp.start()
        for k, (px, py) in enumerate(_other_chips(x, y)):
            _remote(x_ref, o_ref.at[2 * px + py], send_sems, recv_sems, k, (px, py, c)).wait_recv()
        for cp in sends:
            cp.wait_send()

    vm = pl.BlockSpec(memory_space=pltpu.VMEM)
    return pl.pallas_call(
        body, name="gather_small", in_specs=[vm], out_specs=vm, out_shape=_sds((N_SH, r, cc), blk.dtype),
        scratch_shapes=[pltpu.SemaphoreType.DMA((3,)), pltpu.SemaphoreType.DMA((3,))],
        compiler_params=pltpu.CompilerParams(has_side_effects=True),
    )(blk)


def _cast_into_slot(w32, pos, *, name, tr):
    r, cc = w32.shape

    def body(pos_ref, w_ref, o_ref):
        o_ref[...] = w_ref[...].astype(BF16)

    return pl.pallas_call(
        body, name=name,
        grid_spec=pltpu.PrefetchScalarGridSpec(
            num_scalar_prefetch=1, grid=(r // tr,),
            in_specs=[pl.BlockSpec((tr, cc), lambda i, p: (i, 0))],
            out_specs=pl.BlockSpec((None, tr, cc), lambda i, p: (p[1], i, 0))),
        out_shape=_sds((N_SH, r, cc), BF16), compiler_params=_cp(32, ("parallel",)),
    )(pos, w32)


def _gather_big(buf, *, name):
    _, r, _ = buf.shape
    half = r // 2

    def body(x_ref, o_ref, send_sems, recv_sems):
        del x_ref
        x, y, c = _my_pos()
        me = 2 * x + y
        chips = _other_chips(x, y)
        mine_rows = pl.ds(pl.multiple_of(c * half, 16), half)
        sib_rows = pl.ds(pl.multiple_of((1 - c) * half, 16), half)
        mine = o_ref.at[me, mine_rows]
        sends = [_remote(mine, mine, send_sems, recv_sems, k, (px, py, c)) for k, (px, py) in enumerate(chips)]
        for cp in sends:
            cp.start()
        passed = []
        for k, (px, py) in enumerate(chips):
            got = o_ref.at[2 * px + py, mine_rows]
            _remote(got, got, send_sems, recv_sems, k, (px, py, c)).wait_recv()
            fwd = _remote(got, got, send_sems, recv_sems, 3 + k, (x, y, 1 - c))
            fwd.start()
            passed.append(fwd)
        for k, (px, py) in enumerate(chips):
            got = o_ref.at[2 * px + py, sib_rows]
            _remote(got, got, send_sems, recv_sems, 3 + k, (x, y, 1 - c)).wait_recv()
        for cp in sends + passed:
            cp.wait_send()

    return pl.pallas_call(
        body, name=name, in_specs=[_ANY], out_specs=_ANY, out_shape=_sds(buf.shape, buf.dtype),
        input_output_aliases={0: 0},
        scratch_shapes=[pltpu.SemaphoreType.DMA((6,)), pltpu.SemaphoreType.DMA((6,))],
        compiler_params=pltpu.CompilerParams(has_side_effects=True),
    )(buf)


def _pair_send_other_half(g, *, name):
    n, r, cc = g.shape
    half = r // 2

    def body(g_ref, o_ref, send_sems, recv_sems):
        x, y, c = _my_pos()
        sib_rows = pl.ds(pl.multiple_of((1 - c) * half, 8), half)
        cps = [_remote(g_ref.at[s, sib_rows], o_ref.at[s], send_sems, recv_sems, s, (x, y, 1 - c)) for s in range(n)]
        for cp in cps:
            cp.start()
        for cp in cps:
            cp.wait_recv()
        for cp in cps:
            cp.wait_send()

    return pl.pallas_call(
        body, name=name, in_specs=[_ANY], out_specs=_ANY, out_shape=_sds((n, half, cc), g.dtype),
        scratch_shapes=[pltpu.SemaphoreType.DMA((n,)), pltpu.SemaphoreType.DMA((n,))],
        compiler_params=pltpu.CompilerParams(has_side_effects=True),
    )(g)


def _chip_scatter(s16, *, name):
    n, h, cc = s16.shape

    def body(s_ref, o_ref, send_sems, recv_sems):
        x, y, c = _my_pos()
        cps = [_remote(s_ref.at[2 * px + py], o_ref.at[k], send_sems, recv_sems, k, (px, py, c))
               for k, (px, py) in enumerate(_other_chips(x, y))]
        for cp in cps:
            cp.start()
        for cp in cps:
            cp.wait_recv()
        for cp in cps:
            cp.wait_send()

    return pl.pallas_call(
        body, name=name, in_specs=[_ANY], out_specs=_ANY, out_shape=_sds((3, h, cc), s16.dtype),
        scratch_shapes=[pltpu.SemaphoreType.DMA((3,)), pltpu.SemaphoreType.DMA((3,))],
        compiler_params=pltpu.CompilerParams(has_side_effects=True),
    )(s16)


def _pair_gather_halves(full, *, name):
    r, _ = full.shape
    h = r // 2

    def body(x_ref, o_ref, send_sems, recv_sems):
        del x_ref
        x, y, c = _my_pos()
        mine = o_ref.at[pl.ds(pl.multiple_of(c * h, 8), h)]
        theirs = o_ref.at[pl.ds(pl.multiple_of((1 - c) * h, 8), h)]
        cp = _remote(mine, mine, send_sems, recv_sems, 0, (x, y, 1 - c))
        cp.start()
        _remote(theirs, theirs, send_sems, recv_sems, 0, (x, y, 1 - c)).wait_recv()
        cp.wait_send()

    return pl.pallas_call(
        body, name=name, in_specs=[_ANY], out_specs=_ANY, out_shape=_sds(full.shape, full.dtype),
        input_output_aliases={0: 0},
        scratch_shapes=[pltpu.SemaphoreType.DMA((1,)), pltpu.SemaphoreType.DMA((1,))],
        compiler_params=pltpu.CompilerParams(has_side_effects=True),
    )(full)


N_DEV = 8


def _allreduce_small(pack):
    r, cc = pack.shape

    def body(x_ref, o_ref, buf_ref, send_sems, recv_sems):
        x, y, c = _my_pos()
        me = 4 * x + 2 * y + c
        buf_ref[me] = x_ref[...]
        cps = []
        for k in range(1, N_DEV):
            peer = (x ^ ((k >> 2) & 1), y ^ ((k >> 1) & 1), c ^ (k & 1))
            cps.append(_remote(x_ref, buf_ref.at[me], send_sems, recv_sems, k - 1, peer))
        for cp in cps:
            cp.start()
        for k in range(1, N_DEV):
            peer = (x ^ ((k >> 2) & 1), y ^ ((k >> 1) & 1), c ^ (k & 1))
            src = 4 * peer[0] + 2 * peer[1] + peer[2]
            _remote(x_ref, buf_ref.at[src], send_sems, recv_sems, k - 1, peer).wait_recv()
        acc = buf_ref[0]
        for d in range(1, N_DEV):
            acc = acc + buf_ref[d]
        o_ref[...] = acc
        for cp in cps:
            cp.wait_send()

    vm = pl.BlockSpec(memory_space=pltpu.VMEM)
    return pl.pallas_call(
        body, name="allreduce_small", in_specs=[vm], out_specs=vm, out_shape=_sds((r, cc), F32),
        scratch_shapes=[pltpu.VMEM((N_DEV, r, cc), F32), pltpu.SemaphoreType.DMA((N_DEV - 1,)),
                        pltpu.SemaphoreType.DMA((N_DEV - 1,))],
        compiler_params=pltpu.CompilerParams(has_side_effects=True),
    )(pack)


def _pair_add(g, got, pos, *, name, tr):
    n, r, cc = g.shape
    half = r // 2
    nt = half // tr

    def body(pos_ref, g_ref, r_ref, own_ref, s16_ref):
        s = g_ref[...] + r_ref[...]
        s16_ref[...] = s.astype(BF16)

        @pl.when(pl.program_id(1) == pos_ref[1])
        def _():
            own_ref[...] = s

    return pl.pallas_call(
        body, name=name,
        grid_spec=pltpu.PrefetchScalarGridSpec(
            num_scalar_prefetch=1, grid=(nt, n),
            in_specs=[pl.BlockSpec((None, tr, cc), lambda i, s, p: (s, p[0] * nt + i, 0)),
                      pl.BlockSpec((None, tr, cc), lambda i, s, p: (s, i, 0))],
            out_specs=[pl.BlockSpec((tr, cc), lambda i, s, p: (i, 0)),
                       pl.BlockSpec((None, tr, cc), lambda i, s, p: (s, i, 0))]),
        out_shape=[_sds((half, cc), F32), _sds((n, half, cc), BF16)],
        compiler_params=_cp(40, ("parallel", "arbitrary")),
    )(pos, g, got)


def _sum_chips(own, got, pos, *, name, tr):
    h, cc = own.shape
    nt = h // tr

    def body(pos_ref, o_ref, r_ref, out_ref):
        acc = o_ref[...]
        for k in range(3):
            acc = acc + r_ref[k].astype(F32)
        out_ref[...] = acc

    return pl.pallas_call(
        body, name=name,
        grid_spec=pltpu.PrefetchScalarGridSpec(
            num_scalar_prefetch=1, grid=(nt,),
            in_specs=[pl.BlockSpec((tr, cc), lambda i, p: (i, 0)), pl.BlockSpec((3, tr, cc), lambda i, p: (0, i, 0))],
            out_specs=pl.BlockSpec((tr, cc), lambda i, p: (p[0] * nt + i, 0))),
        out_shape=_sds((2 * h, cc), F32), compiler_params=_cp(40, ("parallel",)),
    )(pos, own, got)


def _reduce_grads(g, pos, *, name, tr):
    got = _pair_send_other_half(g, name=name + "_pair")
    own, s16 = _pair_add(g, got, pos, name=name + "_add", tr=tr)
    others = _chip_scatter(s16, name=name + "_chips")
    mine = _sum_chips(own, others, pos, name=name + "_sum", tr=tr)
    return _pair_gather_halves(mine, name=name + "_halves")


def _adamw(w, g, m, v, *, name, tr):
    r, cc = w.shape

    def body(w_ref, g_ref, m_ref, v_ref, d_ref, mo_ref, vo_ref):
        gg = g_ref[...]
        m_new = ADAM_B1 * m_ref[...] + (1.0 - ADAM_B1) * gg
        v_new = ADAM_B2 * v_ref[...] + (1.0 - ADAM_B2) * (gg * gg)
        m_hat = m_new / (1.0 - ADAM_B1 ** ADAM_STEP)
        v_hat = v_new / (1.0 - ADAM_B2 ** ADAM_STEP)
        d_ref[...] = -ADAM_LR * (m_hat / (jnp.sqrt(v_hat) + ADAM_EPS) + ADAM_WD * w_ref[...])
        mo_ref[...] = m_new
        vo_ref[...] = v_new

    blk = pl.BlockSpec((tr, cc), lambda i: (i, 0))
    return pl.pallas_call(
        body, name=name, grid=(r // tr,), in_specs=[blk] * 4, out_specs=[blk] * 3,
        out_shape=[_sds((r, cc), F32)] * 3, compiler_params=_cp(48, ("parallel",)),
    )(w, g, m, v)


_SMALL_ROWS = 40
_B_IN_ROWS = 7


def _row_pad(v, rows):
    flat = v.reshape(-1)
    return jnp.pad(flat, (0, rows * D - flat.shape[0])).reshape(rows, D)


def _pack_ra(w):
    return w.reshape(64, D)


def _gate_full(g4):
    return g4.reshape(N_SH, N_RB, 64, RB).transpose(1, 0, 2, 3).reshape(N_RB, RB, RB)


def _gate_shards(w):
    return w.reshape(N_RB, N_SH, 64, RB).transpose(1, 0, 2, 3).reshape(N_SH, 64, D)


def kernel(x, meta_tokens, ln_emb_g, ln_emb_b, w_in, b_in, conv_w, conv_b, w_ra, b_ra, w_ri, b_ri, lru_lambda, sinks, w_rnn_out, w_attn_out, w_o, b_o, ln_g, ln_b, loss_target, m_meta_tokens, m_ln_emb_g, m_ln_emb_b, m_w_in, m_b_in, m_conv_w, m_conv_b, m_w_ra, m_b_ra, m_w_ri, m_b_ri, m_lru_lambda, m_sinks, m_w_rnn_out, m_w_attn_out, m_w_o, m_b_o, m_ln_g, m_ln_b, v_meta_tokens, v_ln_emb_g, v_ln_emb_b, v_w_in, v_b_in, v_conv_w, v_conv_b, v_w_ra, v_b_ra, v_w_ri, v_b_ri, v_lru_lambda, v_sinks, v_w_rnn_out, v_w_attn_out, v_w_o, v_b_o, v_ln_g, v_ln_b):
    xi, yi, ci = _my_pos()
    shard = 2 * xi + yi
    pos = jnp.stack([ci, shard]).astype(jnp.int32)

    small = jnp.concatenate([conv_w[0], meta_tokens, jnp.zeros((4, 512), F32)], axis=0)
    small4 = _gather_small(small)
    conv_w_full = small4[:, 0:4].transpose(1, 0, 2).reshape(CONV_W, D)
    meta_full = small4[:, 4:20].transpose(1, 0, 2).reshape(N_META, D)
    w_in4 = _gather_big(_cast_into_slot(w_in[0], pos, name="cast_w_in", tr=256), name="gather_w_in")
    wpack = jnp.concatenate([w_rnn_out[0], w_attn_out[0], w_o[0], _pack_ra(w_ra[0]), _pack_ra(w_ri[0])], axis=0)
    wp4 = _gather_big(_cast_into_slot(wpack, pos, name="cast_w_sq", tr=208), name="gather_w_sq")
    sq = lambda k: wp4[:, 512 * k:512 * (k + 1)].reshape(1, D, D)
    w_ra_full = _gate_full(wp4[:, 1536:1600])
    w_ri_full = _gate_full(wp4[:, 1600:1664])

    g = _local_step(x[0], loss_target[0], meta_full, ln_emb_g[None], ln_emb_b[None], w_in4, b_in, conv_w_full, conv_b,
                    w_ra_full, b_ra, w_ri_full, b_ri, lru_lambda, sinks[0], sq(0), sq(1), sq(2), b_o, ln_g, ln_b)
    loss = lax.psum(g["loss"], ("x", "y", "c"))

    g_in = _reduce_grads(g["w_in"], pos, name="red_w_in", tr=128)
    gpack = jnp.concatenate([g["w_rnn_out"].reshape(N_SH, 512, D), g["w_attn_out"].reshape(N_SH, 512, D),
                             g["w_o"].reshape(N_SH, 512, D), _gate_shards(g["w_ra"]), _gate_shards(g["w_ri"])], axis=1)
    g_sq = _reduce_grads(gpack, pos, name="red_w_sq", tr=208)
    d_in, m_in, v_in = _adamw(w_in[0], g_in, m_w_in[0], v_w_in[0], name="adamw_w_in", tr=128)
    mpack = jnp.concatenate([m_w_rnn_out[0], m_w_attn_out[0], m_w_o[0], _pack_ra(m_w_ra[0]), _pack_ra(m_w_ri[0])], axis=0)
    vpack = jnp.concatenate([v_w_rnn_out[0], v_w_attn_out[0], v_w_o[0], _pack_ra(v_w_ra[0]), _pack_ra(v_w_ri[0])], axis=0)
    d_sq, m_sq, v_sq = _adamw(wpack, g_sq, mpack, vpack, name="adamw_w_sq", tr=208)

    spack = jnp.concatenate([
        g["ln_emb_g"], g["ln_emb_b"], _row_pad(g["b_in"], _B_IN_ROWS), g["conv_w"], g["conv_b"], g["b_ra"], g["b_ri"],
        g["lru_lambda"], _row_pad(g["sinks"], 1), g["b_o"], g["ln_g"], g["ln_b"], g["meta_tokens"],
        jnp.zeros((_SMALL_ROWS - 37, D), F32)], axis=0)
    sred = _allreduce_small(spack)
    col0 = shard * 512
    g_conv_w = lax.dynamic_slice(sred[9:13], (0, col0), (CONV_W, 512))
    g_meta = lax.dynamic_slice(sred[21:37], (0, col0), (N_META, 512))
    small_g = {"ln_emb_g": sred[0:1], "ln_emb_b": sred[1:2], "b_in": sred[2:9], "conv_w": g_conv_w.reshape(1, D),
               "conv_b": sred[13:14], "b_ra": sred[14:15], "b_ri": sred[15:16], "lru_lambda": sred[16:17],
               "sinks": sred[17:18], "b_o": sred[18:19], "ln_g": sred[19:20], "ln_b": sred[20:21],
               "meta_tokens": g_meta.reshape(4, D)}
    small_names = list(small_g)

    def small_pack(vals):
        rows = []
        for n in small_names:
            a = vals[n]
            if n == "b_in":
                a = _row_pad(a, _B_IN_ROWS)
            elif n == "sinks":
                a = _row_pad(a, 1)
            else:
                a = a.reshape(-1, D)
            rows.append(a)
        return jnp.concatenate(rows + [jnp.zeros((24 - 22, D), F32)], axis=0)

    w_small = dict(ln_emb_g=ln_emb_g, ln_emb_b=ln_emb_b, b_in=b_in, conv_w=conv_w, conv_b=conv_b, b_ra=b_ra, b_ri=b_ri,
                   lru_lambda=lru_lambda, sinks=sinks, b_o=b_o, ln_g=ln_g, ln_b=ln_b, meta_tokens=meta_tokens)
    m_small = dict(ln_emb_g=m_ln_emb_g, ln_emb_b=m_ln_emb_b, b_in=m_b_in, conv_w=m_conv_w, conv_b=m_conv_b, b_ra=m_b_ra,
                   b_ri=m_b_ri, lru_lambda=m_lru_lambda, sinks=m_sinks, b_o=m_b_o, ln_g=m_ln_g, ln_b=m_ln_b,
                   meta_tokens=m_meta_tokens)
    v_small = dict(ln_emb_g=v_ln_emb_g, ln_emb_b=v_ln_emb_b, b_in=v_b_in, conv_w=v_conv_w, conv_b=v_conv_b, b_ra=v_b_ra,
                   b_ri=v_b_ri, lru_lambda=v_lru_lambda, sinks=v_sinks, b_o=v_b_o, ln_g=v_ln_g, ln_b=v_ln_b,
                   meta_tokens=v_meta_tokens)
    g_small_pack = jnp.concatenate([small_g[n] for n in small_names] + [jnp.zeros((2, D), F32)], axis=0)
    d_sm, m_sm, v_sm = _adamw(small_pack(w_small), g_small_pack, small_pack(m_small), small_pack(v_small),
                              name="adamw_small", tr=24)

    small_rows = {}
    r0 = 0
    for n in small_names:
        nrows = small_g[n].shape[0]
        small_rows[n] = (r0, nrows)
        r0 += nrows

    def small_out(packed, n, like):
        a, nrows = small_rows[n]
        flat = packed[a:a + nrows].reshape(-1)
        return flat[:like.size].reshape(like.shape)

    def sq_out(packed, k, like):
        if k < 3:
            return packed[512 * k:512 * (k + 1)].reshape(like.shape)
        return packed[1536 + 64 * (k - 3):1600 + 64 * (k - 3)].reshape(like.shape)

    sq_idx = {"w_rnn_out": 0, "w_attn_out": 1, "w_o": 2, "w_ra": 3, "w_ri": 4}
    weights = dict(meta_tokens=meta_tokens, ln_emb_g=ln_emb_g, ln_emb_b=ln_emb_b, w_in=w_in, b_in=b_in, conv_w=conv_w,
                   conv_b=conv_b, w_ra=w_ra, b_ra=b_ra, w_ri=w_ri, b_ri=b_ri, lru_lambda=lru_lambda, sinks=sinks,
                   w_rnn_out=w_rnn_out, w_attn_out=w_attn_out, w_o=w_o, b_o=b_o, ln_g=ln_g, ln_b=ln_b)

    def outputs(big_in, big_sq, small):
        res = []
        for n, like in weights.items():
            if n == "w_in":
                res.append(big_in.reshape(like.shape))
            elif n in sq_idx:
                res.append(sq_out(big_sq, sq_idx[n], like))
            else:
                res.append(small_out(small, n, like))
        return res

    return (loss, g["x"][None], *outputs(g_in, g_sq, g_small_pack), *outputs(d_in, d_sq, d_sm),
            *outputs(m_in, m_sq, m_sm), *outputs(v_in, v_sq, v_sm))
```

```python
import jax
import jax.numpy as jnp
from jax import lax
from jax.experimental import pallas as pl
from jax.experimental.pallas import tpu as pltpu

F32 = jnp.float32
BF16 = jnp.bfloat16
_MXU = jnp.bfloat16

D = 2048
SEQ = 2048
N_META = 16
BLK = 128
PAD = BLK - N_META
TP = PAD + N_META + SEQ
NBLK = TP // BLK
HALF_TP = TP // 2
N_RB = 8
RB = 256
CONV_W = 4
LRU_C = 8.0
HD = 64
N_Q = 32
N_KV = 4
GRP = 8
D_KV = 256
NEG_INF = -1e30
LN_EPS = 1e-5
ALPHA = 2.0 ** 0.25
ROPE_THETA = 10000.0
OFF_GR, OFF_Q, OFF_K, OFF_V, OFF_GA, OFF_G = 2048, 4096, 6144, 6400, 6656, 8704
D_IN = 12800
N_SH = 4
W_IN_COLS = D_IN // N_SH
TN_IN = 640
PER_IN = W_IN_COLS // TN_IN
CW = 512
SQ_ROWS = 512
PACK_ROWS = 3 * SQ_ROWS + 128

ADAM_LR = 0.001
ADAM_B1 = 0.9
ADAM_B2 = 0.999
ADAM_EPS = 1e-08
ADAM_WD = 0.01
ADAM_STEP = 10

MESH = pl.DeviceIdType.MESH
_MIB = 1024 * 1024
_ANY = pl.BlockSpec(memory_space=pl.ANY)
_NT = (((1,), (1,)), ((), ()))
_TN = (((0,), (0,)), ((), ()))


def _sds(shape, dtype):
    return jax.ShapeDtypeStruct(shape, dtype)


def _sigmoid(x):
    return 1.0 / (1.0 + jnp.exp(-x))


def _my_pos():
    return lax.axis_index("x"), lax.axis_index("y"), lax.axis_index("c")


def _other_chips(x, y):
    return [(1 - x, y), (x, 1 - y), (1 - x, 1 - y)]


def _remote(src, dst, send_sems, recv_sems, k, dev):
    return pltpu.make_async_remote_copy(src_ref=src, dst_ref=dst, send_sem=send_sems.at[k], recv_sem=recv_sems.at[k],
                                        device_id=dev, device_id_type=MESH)


class _Hook:
    def __init__(self, carried, landing, n_sems, start, finish):
        self.carried, self.landing, self.n_sems, self.start, self.finish = list(carried), list(landing), n_sems, start, finish


def _hook_gather(buf):
    half = buf.shape[1] // 2

    def plan(o, ss, rs):
        x, y, c = _my_pos()
        chips = _other_chips(x, y)
        mine_rows = pl.ds(pl.multiple_of(c * half, 16), half)
        sib_rows = pl.ds(pl.multiple_of((1 - c) * half, 16), half)
        mine = o.at[2 * x + y, mine_rows]
        sends = [_remote(mine, mine, ss, rs, k, (px, py, c)) for k, (px, py) in enumerate(chips)]
        return sends, chips, mine_rows, sib_rows, (x, y, c)

    def start(car, land, ss, rs):
        for cp in plan(car[0], ss, rs)[0]:
            cp.start()

    def finish(car, land, ss, rs):
        o = car[0]
        sends, chips, mine_rows, sib_rows, (x, y, c) = plan(o, ss, rs)
        passed = []
        for k, (px, py) in enumerate(chips):
            got = o.at[2 * px + py, mine_rows]
            _remote(got, got, ss, rs, k, (px, py, c)).wait_recv()
            fwd = _remote(got, got, ss, rs, 3 + k, (x, y, 1 - c))
            fwd.start()
            passed.append(fwd)
        for k, (px, py) in enumerate(chips):
            got = o.at[2 * px + py, sib_rows]
            _remote(got, got, ss, rs, 3 + k, (x, y, 1 - c)).wait_recv()
        for cp in sends + passed:
            cp.wait_send()

    return _Hook([buf], [], 6, start, finish)


def _hook_pair(g, half_rows):
    n, r, cc = g.shape
    h = r // 2 if half_rows else r

    def plan(car, land, ss, rs):
        x, y, c = _my_pos()
        cps = []
        for s in range(n):
            src = car[0].at[s, pl.ds(pl.multiple_of((1 - c) * h, 8), h)] if half_rows else car[0].at[s]
            cps.append(_remote(src, land[0].at[s], ss, rs, s, (x, y, 1 - c)))
        return cps

    def start(car, land, ss, rs):
        for cp in plan(car, land, ss, rs):
            cp.start()

    def finish(car, land, ss, rs):
        cps = plan(car, land, ss, rs)
        for cp in cps:
            cp.wait_recv()
        for cp in cps:
            cp.wait_send()

    return _Hook([g], [_sds((n, h, cc), g.dtype)], n, start, finish)


def _hook_scatter(s16):
    _, h, cc = s16.shape

    def plan(car, land, ss, rs):
        x, y, c = _my_pos()
        return [_remote(car[0].at[2 * px + py], land[0].at[k], ss, rs, k, (px, py, c))
                for k, (px, py) in enumerate(_other_chips(x, y))]

    def start(car, land, ss, rs):
        for cp in plan(car, land, ss, rs):
            cp.start()

    def finish(car, land, ss, rs):
        cps = plan(car, land, ss, rs)
        for cp in cps:
            cp.wait_recv()
        for cp in cps:
            cp.wait_send()

    return _Hook([s16], [_sds((3, h, cc), s16.dtype)], 3, start, finish)


def _hook_halves(full):
    h = full.shape[0] // 2

    def half_copy(car, ss, rs, which):
        x, y, c = _my_pos()
        rows = car[0].at[pl.ds(pl.multiple_of((c + which - 2 * c * which) * h, 8), h)]
        return _remote(rows, rows, ss, rs, 0, (x, y, 1 - c))

    def start(car, land, ss, rs):
        half_copy(car, ss, rs, 0).start()

    def finish(car, land, ss, rs):
        half_copy(car, ss, rs, 1).wait_recv()
        half_copy(car, ss, rs, 0).wait_send()

    return _Hook([full], [], 1, start, finish)


def _pcall(body, *, name, grid, in_specs, out_specs, out_shape, operands, scratch=(), vmem=48, sem=None,
           prefetch=(), aliases=None, hook=None):
    n_pre, n_in, n_out, n_scr = len(prefetch), len(in_specs), len(out_specs), len(scratch)
    in_specs, out_specs, out_shape, scratch = list(in_specs), list(out_specs), list(out_shape), list(scratch)
    io_alias = {n_pre + a: b for a, b in (aliases or {}).items()}
    operands = list(operands)
    kernel_body = body
    if hook is not None:
        n_car, n_land = len(hook.carried), len(hook.landing)
        for t, arr in enumerate(hook.carried):
            io_alias[n_pre + n_in + t] = n_out + t
        in_specs += [_ANY] * n_car
        out_specs += [_ANY] * (n_car + n_land)
        out_shape += [_sds(a.shape, a.dtype) for a in hook.carried] + hook.landing
        scratch += [pltpu.SemaphoreType.DMA((hook.n_sems,)), pltpu.SemaphoreType.DMA((hook.n_sems,))]
        operands += hook.carried
        sem = ("arbitrary",) * len(grid)

        def kernel_body(*refs):
            pre, rest = refs[:n_pre], refs[n_pre:]
            ins = rest[:n_in]
            outs = rest[n_in + n_car:n_in + n_car + n_out]
            car = rest[n_in + n_car + n_out:n_in + 2 * n_car + n_out]
            land = rest[n_in + 2 * n_car + n_out:n_in + 2 * n_car + n_out + n_land]
            scr = rest[n_in + 2 * n_car + n_out + n_land:]
            send_sems, recv_sems = scr[n_scr], scr[n_scr + 1]
            first = pl.program_id(0) == 0
            last = pl.program_id(0) == grid[0] - 1
            for d in range(1, len(grid)):
                first = first & (pl.program_id(d) == 0)
                last = last & (pl.program_id(d) == grid[d] - 1)

            @pl.when(first)
            def _():
                hook.start(car, land, send_sems, recv_sems)

            body(*pre, *ins, *outs, *scr[:n_scr])

            @pl.when(last)
            def _():
                hook.finish(car, land, send_sems, recv_sems)

    params = pltpu.CompilerParams(vmem_limit_bytes=vmem * _MIB, dimension_semantics=sem,
                                  has_side_effects=hook is not None)
    if n_pre:
        call = pl.pallas_call(
            kernel_body, name=name, out_shape=out_shape, input_output_aliases=io_alias, compiler_params=params,
            grid_spec=pltpu.PrefetchScalarGridSpec(num_scalar_prefetch=n_pre, grid=grid, in_specs=in_specs,
                                                   out_specs=out_specs, scratch_shapes=scratch))
    else:
        call = pl.pallas_call(kernel_body, name=name, grid=grid, in_specs=in_specs, out_specs=out_specs,
                              out_shape=out_shape, scratch_shapes=scratch, input_output_aliases=io_alias,
                              compiler_params=params)
    return call(*prefetch, *operands)


def _comm_call(hook, *, name):
    n_car = len(hook.carried)

    def body(*refs):
        car = refs[n_car:2 * n_car]
        land = refs[2 * n_car:2 * n_car + len(hook.landing)]
        send_sems, recv_sems = refs[-2], refs[-1]
        hook.start(car, land, send_sems, recv_sems)
        hook.finish(car, land, send_sems, recv_sems)

    return pl.pallas_call(
        body, name=name, in_specs=[_ANY] * n_car, out_specs=[_ANY] * (n_car + len(hook.landing)),
        out_shape=[_sds(a.shape, a.dtype) for a in hook.carried] + hook.landing,
        input_output_aliases={t: t for t in range(n_car)},
        scratch_shapes=[pltpu.SemaphoreType.DMA((hook.n_sems,)), pltpu.SemaphoreType.DMA((hook.n_sems,))],
        compiler_params=pltpu.CompilerParams(has_side_effects=True),
    )(*hook.carried)


def _gather_small(blk):
    r, cc = blk.shape

    def body(x_ref, o_ref, send_sems, recv_sems):
        x, y, c = _my_pos()
        me = 2 * x + y
        o_ref[me] = x_ref[...]
        sends = [_remote(x_ref, o_ref.at[me], send_sems, recv_sems, k, (px, py, c))
                 for k, (px, py) in enumerate(_other_chips(x, y))]
        for cp in sends:
            cp.start()
        for k, (px, py) in enumerate(_other_chips(x, y)):
            _remote(x_ref, o_ref.at[2 * px + py], send_sems, recv_sems, k, (px, py, c)).wait_recv()
        for cp in sends:
            cp.wait_send()

    vm = pl.BlockSpec(memory_space=pltpu.VMEM)
    return pl.pallas_call(
        body, name="gather_small", in_specs=[vm], out_specs=vm, out_shape=_sds((N_SH, r, cc), blk.dtype),
        scratch_shapes=[pltpu.SemaphoreType.DMA((3,)), pltpu.SemaphoreType.DMA((3,))],
        compiler_params=pltpu.CompilerParams(has_side_effects=True),
    )(blk)


N_DEV = 8


def _allreduce_small(pack):
    r, cc = pack.shape

    def body(x_ref, o_ref, buf_ref, send_sems, recv_sems):
        x, y, c = _my_pos()
        me = 4 * x + 2 * y + c
        buf_ref[me] = x_ref[...]
        cps = []
        for k in range(1, N_DEV):
            peer = (x ^ ((k >> 2) & 1), y ^ ((k >> 1) & 1), c ^ (k & 1))
            cps.append(_remote(x_ref, buf_ref.at[me], send_sems, recv_sems, k - 1, peer))
        for cp in cps:
            cp.start()
        for k in range(1, N_DEV):
            peer = (x ^ ((k >> 2) & 1), y ^ ((k >> 1) & 1), c ^ (k & 1))
            src = 4 * peer[0] + 2 * peer[1] + peer[2]
            _remote(x_ref, buf_ref.at[src], send_sems, recv_sems, k - 1, peer).wait_recv()
        acc = buf_ref[0]
        for d in range(1, N_DEV):
            acc = acc + buf_ref[d]
        o_ref[...] = acc
        for cp in cps:
            cp.wait_send()

    vm = pl.BlockSpec(memory_space=pltpu.VMEM)
    return pl.pallas_call(
        body, name="allreduce_small", in_specs=[vm], out_specs=vm, out_shape=_sds((r, cc), F32),
        scratch_shapes=[pltpu.VMEM((N_DEV, r, cc), F32), pltpu.SemaphoreType.DMA((N_DEV - 1,)),
                        pltpu.SemaphoreType.DMA((N_DEV - 1,))],
        compiler_params=pltpu.CompilerParams(has_side_effects=True),
    )(pack)


def _mm_nn(a, w, bias, *, name, grid, tm, tn, k, a_index, w_block, w_index, out_cols, out_index, carried=None, hook=None):
    m = a.shape[0]

    def body(a_ref, w_ref, b_ref, *rest):
        o_ref = rest[-1]
        wv = w_ref[...]
        acc = jnp.dot(a_ref[...].astype(_MXU), wv.reshape(k, tn).astype(_MXU), preferred_element_type=F32)
        o_ref[...] = acc + b_ref[...]

    operands = [a, w, bias] + ([carried] if carried is not None else [])
    return _pcall(
        body, name=name, grid=grid,
        in_specs=[pl.BlockSpec((tm, k), a_index), pl.BlockSpec(w_block, w_index),
                  pl.BlockSpec((1, tn), lambda i, j: (0, j))] + ([_ANY] if carried is not None else []),
        out_specs=[pl.BlockSpec((tm, tn), out_index)], out_shape=[_sds((m, out_cols), F32)], operands=operands,
        vmem=56, sem=("parallel", "parallel"), aliases={3: 0} if carried is not None else None, hook=hook)


def _mm_nt(a, w, *, name, grid, tm, tn, tk, a_index, w_block, w_index, out_cols, hook=None):
    m = a.shape[0]
    nk = grid[2]

    def body(a_ref, w_ref, o_ref, acc_ref):
        part = lax.dot_general(a_ref[...].astype(_MXU), w_ref[...].astype(_MXU), _NT, preferred_element_type=F32)
        if nk == 1:
            o_ref[...] = part
        else:
            kidx = pl.program_id(2)

            @pl.when(kidx == 0)
            def _():
                acc_ref[...] = part

            @pl.when(kidx > 0)
            def _():
                acc_ref[...] += part

            @pl.when(kidx == nk - 1)
            def _():
                o_ref[...] = acc_ref[...]

    return _pcall(
        body, name=name, grid=grid, in_specs=[pl.BlockSpec((tm, tk), a_index), pl.BlockSpec(w_block, w_index)],
        out_specs=[pl.BlockSpec((tm, tn), lambda i, j, q: (i, j))], out_shape=[_sds((m, out_cols), F32)],
        operands=[a, w], scratch=[pltpu.VMEM((tm, tn) if nk > 1 else (8, 128), F32)], vmem=56,
        sem=("parallel", "parallel", "arbitrary"), hook=hook)


def _mm_tn(a, b, *, name, grid, tm, tn, a_index, b_index, out_shape, out_block, out_index, carried=None,
           prefetch=(), hook=None):
    t = a.shape[0]

    def body(*refs):
        a_ref, b_ref = refs[len(prefetch)], refs[len(prefetch) + 1]
        refs[-1][...] = lax.dot_general(a_ref[...].astype(_MXU), b_ref[...].astype(_MXU), _TN, preferred_element_type=F32)

    operands = [a, b] + ([carried] if carried is not None else [])
    return _pcall(
        body, name=name, grid=grid,
        in_specs=[pl.BlockSpec((t, tm), a_index), pl.BlockSpec((t, tn), b_index)] + ([_ANY] if carried is not None else []),
        out_specs=[pl.BlockSpec(out_block, out_index)], out_shape=[_sds(out_shape, F32)], operands=operands,
        vmem=56, sem=("parallel", "parallel"), aliases={2: 0} if carried is not None else None, prefetch=prefetch, hook=hook)


def _ln_emb_fwd(h0p, g, b):
    def body(x_ref, g_ref, b_ref, h32_ref, h16_ref):
        x = x_ref[...]
        mu = jnp.mean(x, axis=-1, keepdims=True)
        xc = x - mu
        var = jnp.mean(xc * xc, axis=-1, keepdims=True)
        y = xc * lax.rsqrt(var + LN_EPS) * g_ref[...] + b_ref[...]
        h32_ref[...] = y
        h16_ref[...] = y.astype(_MXU)

    row = pl.BlockSpec((BLK, D), lambda i: (i, 0))
    vec = pl.BlockSpec((1, D), lambda i: (0, 0))
    return _pcall(body, name="ln_emb_fwd", grid=(NBLK,), in_specs=[row, vec, vec], out_specs=[row, row],
                  out_shape=[_sds((TP, D), F32), _sds((TP, D), _MXU)], operands=[h0p, g, b], vmem=32, sem=("parallel",))


def _ln_emb_bwd(h0p, g, dr, dhz, *, hook):
    def body(x_ref, g_ref, dr_ref, dhz_ref, dx_ref, acc_ref):
        @pl.when(pl.program_id(0) == 0)
        def _():
            acc_ref[...] = jnp.zeros_like(acc_ref)

        x = x_ref[...]
        mu = jnp.mean(x, axis=-1, keepdims=True)
        xc = x - mu
        var = jnp.mean(xc * xc, axis=-1, keepdims=True)
        rstd = lax.rsqrt(var + LN_EPS)
        xhat = xc * rstd
        dh = ALPHA * dr_ref[...] + dhz_ref[...]
        acc_ref[0:1, :] += jnp.sum(dh * xhat, axis=0, keepdims=True)
        acc_ref[1:2, :] += jnp.sum(dh, axis=0, keepdims=True)
        dxh = dh * g_ref[...]
        m1 = jnp.mean(dxh, axis=-1, keepdims=True)
        m2 = jnp.mean(dxh * xhat, axis=-1, keepdims=True)
        dx_ref[...] = rstd * (dxh - m1 - xhat * m2)

    row = pl.BlockSpec((BLK, D), lambda i: (i, 0))
    vec = pl.BlockSpec((1, D), lambda i: (0, 0))
    return _pcall(body, name="ln_emb_bwd", grid=(NBLK,), in_specs=[row, vec, row, row],
                  out_specs=[row, pl.BlockSpec((8, D), lambda i: (0, 0))],
                  out_shape=[_sds((TP, D), F32), _sds((8, D), F32)], operands=[h0p, g, dr, dhz], vmem=32,
                  sem=("arbitrary",), hook=hook)


def _mul_silu_fwd(a, z, off, *, name):
    def body(a_ref, z_ref, o_ref):
        zz = z_ref[...]
        o_ref[...] = (a_ref[...] * (zz * _sigmoid(zz))).astype(_MXU)

    blk = pl.BlockSpec((BLK, CW), lambda i, j: (i, j))
    return _pcall(body, name=name, grid=(NBLK, D // CW),
                  in_specs=[blk, pl.BlockSpec((BLK, CW), lambda i, j: (i, off // CW + j))], out_specs=[blk],
                  out_shape=[_sds((TP, D), _MXU)], operands=[a, z], vmem=32, sem=("parallel", "parallel"))[0]


def _mul_silu_bwd(dy, a, z, off, dz, *, name, hook=None):
    def body(dy_ref, a_ref, z_ref, dz_in, da_ref, dg_ref):
        zz = z_ref[...]
        sg = _sigmoid(zz)
        d = dy_ref[...]
        da_ref[...] = d * (zz * sg)
        dg_ref[...] = (d * a_ref[...] * (sg * (1.0 + zz * (1.0 - sg)))).astype(_MXU)

    blk = pl.BlockSpec((BLK, CW), lambda i, j: (i, j))
    zblk = pl.BlockSpec((BLK, CW), lambda i, j: (i, off // CW + j))
    return _pcall(body, name=name, grid=(NBLK, D // CW), in_specs=[blk, blk, zblk, _ANY], out_specs=[blk, zblk],
                  out_shape=[_sds((TP, D), F32), _sds((TP, D_IN), _MXU)], operands=[dy, a, z, dz], vmem=32,
                  sem=("parallel", "parallel"), aliases={3: 1}, hook=hook)


def _merge_fwd(y2, z):
    def body(ya_ref, yb_ref, ga_ref, gb_ref, o_ref):
        o_ref[...] = (_sigmoid(ga_ref[...]) * ya_ref[...] + _sigmoid(gb_ref[...]) * yb_ref[...]).astype(_MXU)

    nb = D // CW
    blk = pl.BlockSpec((BLK, CW), lambda i, j: (i, j))
    return _pcall(body, name="merge_fwd", grid=(NBLK, nb),
                  in_specs=[blk, pl.BlockSpec((BLK, CW), lambda i, j: (i, nb + j)),
                            pl.BlockSpec((BLK, CW), lambda i, j: (i, OFF_G // CW + j)),
                            pl.BlockSpec((BLK, CW), lambda i, j: (i, (OFF_G + D) // CW + j))],
                  out_specs=[blk], out_shape=[_sds((TP, D), _MXU)], operands=[y2, y2, z, z], vmem=32,
                  sem=("parallel", "parallel"))[0]


def _merge_bwd(dmix, y2, z):
    nb = D // CW

    def body(dm_ref, y_ref, g_ref, dy_ref, dg_ref):
        dm = dm_ref[...]
        sg = _sigmoid(g_ref[...])
        dy_ref[...] = (dm * sg).astype(_MXU)
        dg_ref[...] = (dm * y_ref[...] * sg * (1.0 - sg)).astype(_MXU)

    blk = pl.BlockSpec((BLK, CW), lambda i, j: (i, j))
    gblk = pl.BlockSpec((BLK, CW), lambda i, j: (i, OFF_G // CW + j))
    return _pcall(body, name="merge_bwd", grid=(NBLK, 2 * nb),
                  in_specs=[pl.BlockSpec((BLK, CW), lambda i, j: (i, j % nb)), blk, gblk], out_specs=[blk, gblk],
                  out_shape=[_sds((TP, 2 * D), _MXU), _sds((TP, D_IN), _MXU)], operands=[dmix, y2, z], vmem=32,
                  sem=("parallel", "parallel"))


def _ln_out_loss(h32, out, target, g, b):
    def body(h_ref, o_ref, t_ref, g_ref, b_ref, dr_ref, acc_ref):
        i = pl.program_id(0)

        @pl.when(i == 0)
        def _():
            acc_ref[...] = jnp.zeros_like(acc_ref)

        r = ALPHA * h_ref[...] + o_ref[...]
        mu = jnp.mean(r, axis=-1, keepdims=True)
        rc = r - mu
        var = jnp.mean(rc * rc, axis=-1, keepdims=True)
        rstd = lax.rsqrt(var + LN_EPS)
        xhat = rc * rstd
        gg = g_ref[...]
        y = xhat * gg + b_ref[...]
        real = (i >= 1).astype(F32)
        diff = (y - t_ref[...]) * real
        dy = diff * (1.0 / D)
        dxh = dy * gg
        m1 = jnp.mean(dxh, axis=-1, keepdims=True)
        m2 = jnp.mean(dxh * xhat, axis=-1, keepdims=True)
        dr = rstd * (dxh - m1 - xhat * m2)
        dr_ref[...] = dr
        acc_ref[0:1, :] += jnp.sum(dy * xhat, axis=0, keepdims=True)
        acc_ref[1:2, :] += jnp.sum(dy, axis=0, keepdims=True)
        acc_ref[2:3, :] += jnp.sum(dr, axis=0, keepdims=True)
        acc_ref[3:4, :] += (0.5 / D) * jnp.sum(diff * diff)

    row = pl.BlockSpec((BLK, D), lambda i: (i, 0))
    vec = pl.BlockSpec((1, D), lambda i: (0, 0))
    return _pcall(body, name="ln_out_loss", grid=(NBLK,),
                  in_specs=[row, row, pl.BlockSpec((BLK, D), lambda i: (jnp.maximum(i - 1, 0), 0)), vec, vec],
                  out_specs=[row, pl.BlockSpec((8, D), lambda i: (0, 0))],
                  out_shape=[_sds((TP, D), F32), _sds((8, D), F32)], operands=[h32, out, target, g, b], vmem=32,
                  sem=("arbitrary",))


def _colsum(x, *, name, tn, hook=None):
    _, n = x.shape

    def body(x_ref, o_ref):
        o_ref[...] = jnp.sum(x_ref[...].astype(F32), axis=0, keepdims=True)

    return _pcall(body, name=name, grid=(n // tn,), in_specs=[pl.BlockSpec((TP, tn), lambda j: (0, j))],
                  out_specs=[pl.BlockSpec((1, tn), lambda j: (0, j))], out_shape=[_sds((1, n), F32)], operands=[x],
                  vmem=32, sem=("parallel",), hook=hook)


def _rnn_recompute(xr_ref, cw_ref, cb_ref, wra_ref, wri_ref, bra_ref, bri_ref, lam_ref):
    rows = lax.broadcasted_iota(jnp.int32, (TP, 1), 0)
    valid = (rows >= PAD).astype(F32)
    first = rows == PAD
    x = xr_ref[...] * valid
    cw = cw_ref[...]
    shifted = [x, pltpu.roll(x, 1, 0), pltpu.roll(x, 2, 0), pltpu.roll(x, 3, 0)]
    c = cb_ref[...] + cw[0:1, :] * shifted[0] + cw[1:2, :] * shifted[1] + cw[2:3, :] * shifted[2] + cw[3:4, :] * shifted[3]
    cm = c.astype(_MXU)
    gr = _sigmoid(jnp.dot(cm, wra_ref[...].astype(_MXU), preferred_element_type=F32) + bra_ref[...])
    gi = _sigmoid(jnp.dot(cm, wri_ref[...].astype(_MXU), preferred_element_type=F32) + bri_ref[...])
    lam = lam_ref[...]
    ls = jnp.minimum(lam, 0.0) - jnp.log(1.0 + jnp.exp(-jnp.abs(lam)))
    log_a = LRU_C * gr * ls
    a = jnp.exp(log_a)
    mult = jnp.where(first, 1.0, jnp.sqrt(1.0 - jnp.exp(2.0 * log_a)))
    return dict(valid=valid, first=first, shifted=shifted, c=c, cm=cm, gr=gr, gi=gi, ls=ls, a=a, mult=mult, lam=lam)


def _rnn_specs():
    col = pl.BlockSpec((TP, RB), lambda n: (0, n))
    vec = pl.BlockSpec((1, RB), lambda n: (0, n))
    return dict(col=col, vec=vec, cw=pl.BlockSpec((CONV_W, RB), lambda n: (0, n)),
                wblk=pl.BlockSpec((None, RB, RB), lambda n: (n, 0, 0)))


def _rnn_gates_fwd(z, conv_w, conv_b, w_ra, w_ri, b_ra, b_ri, lam):
    def body(xr_ref, cw_ref, cb_ref, wra_ref, wri_ref, bra_ref, bri_ref, lam_ref, a_ref, u_ref):
        r = _rnn_recompute(xr_ref, cw_ref, cb_ref, wra_ref, wri_ref, bra_ref, bri_ref, lam_ref)
        a_ref[...] = r["a"]
        u_ref[...] = r["mult"] * r["gi"] * r["c"] * r["valid"]

    s = _rnn_specs()
    return _pcall(body, name="rnn_gates_fwd", grid=(N_RB,),
                  in_specs=[s["col"], s["cw"], s["vec"], s["wblk"], s["wblk"], s["vec"], s["vec"], s["vec"]],
                  out_specs=[s["col"], s["col"]], out_shape=[_sds((TP, D), F32)] * 2,
                  operands=[z, conv_w, conv_b, w_ra, w_ri, b_ra, b_ri, lam], vmem=56, sem=("parallel",))


SCAN_ROWS = 272


def _scan_fwd(a, u):
    def body(a_ref, u_ref, h_ref, carry_ref):
        @pl.when(pl.program_id(0) == 0)
        def _():
            carry_ref[...] = jnp.zeros_like(carry_ref)

        def step(t, h):
            h = a_ref[t] * h + u_ref[t]
            h_ref[t] = h
            return h

        carry_ref[...] = lax.fori_loop(0, SCAN_ROWS, step, carry_ref[...], unroll=8)

    blk = pl.BlockSpec((SCAN_ROWS, 16, 128), lambda i: (i, 0, 0))
    h = _pcall(body, name="scan_fwd", grid=(TP // SCAN_ROWS,), in_specs=[blk, blk], out_specs=[blk],
               out_shape=[_sds((TP, 16, 128), F32)], operands=[a.reshape(TP, 16, 128), u.reshape(TP, 16, 128)],
               scratch=[pltpu.VMEM((16, 128), F32)], vmem=32, sem=("arbitrary",))[0]
    return h.reshape(TP, D)


def _scan_bwd(a, dh):
    nst = TP // SCAN_ROWS

    def body(a_ref, d_ref, o_ref, lam_ref, anext_ref):
        @pl.when(pl.program_id(0) == 0)
        def _():
            lam_ref[...] = jnp.zeros_like(lam_ref)
            anext_ref[...] = jnp.zeros_like(anext_ref)

        def step(q, carry):
            lam, an = carry
            t = SCAN_ROWS - 1 - q
            lam = d_ref[t] + an * lam
            o_ref[t] = lam
            return lam, a_ref[t]

        lam, an = lax.fori_loop(0, SCAN_ROWS, step, (lam_ref[...], anext_ref[...]), unroll=8)
        lam_ref[...] = lam
        anext_ref[...] = an

    blk = pl.BlockSpec((SCAN_ROWS, 16, 128), lambda i: (nst - 1 - i, 0, 0))
    o = _pcall(body, name="scan_bwd", grid=(nst,), in_specs=[blk, blk], out_specs=[blk],
               out_shape=[_sds((TP, 16, 128), F32)], operands=[a.reshape(TP, 16, 128), dh.reshape(TP, 16, 128)],
               scratch=[pltpu.VMEM((16, 128), F32), pltpu.VMEM((16, 128), F32)], vmem=32, sem=("arbitrary",))[0]
    return o.reshape(TP, D)


def _rnn_gates_bwd(z, lam_s, hr, conv_w, conv_b, w_ra, w_ri, b_ra, b_ri, lam, dz, gsq):
    def body(xr_ref, ls_ref, hr_ref, cw_ref, cb_ref, wra_ref, wri_ref, bra_ref, bri_ref, lam_ref, dz_in, gsq_in,
             dx_ref, dw_ref, sums_ref):
        r = _rnn_recompute(xr_ref, cw_ref, cb_ref, wra_ref, wri_ref, bra_ref, bri_ref, lam_ref)
        valid, c, gr, gi, a, mult = r["valid"], r["c"], r["gr"], r["gi"], r["a"], r["mult"]
        du = ls_ref[...] * valid
        da = du * pltpu.roll(hr_ref[...], 1, 0)
        d_gi = du * mult * c
        dc = du * mult * gi
        dmult = du * gi * c
        dlog_a = da * a + jnp.where(r["first"], 0.0, -dmult * a * a / mult)
        d_gr = dlog_a * (LRU_C * r["ls"])
        dls = jnp.sum(dlog_a * (LRU_C * gr), axis=0, keepdims=True)
        dpre_r = d_gr * gr * (1.0 - gr)
        dpre_i = d_gi * gi * (1.0 - gi)
        pr = dpre_r.astype(_MXU)
        pi = dpre_i.astype(_MXU)
        dwra = lax.dot_general(r["cm"], pr, _TN, preferred_element_type=F32)
        dwri = lax.dot_general(r["cm"], pi, _TN, preferred_element_type=F32)
        for s in range(N_SH):
            dw_ref[s, 0:64, :] = dwra[64 * s:64 * (s + 1)]
            dw_ref[s, 64:128, :] = dwri[64 * s:64 * (s + 1)]
        dc = dc + lax.dot_general(pr, wra_ref[...].astype(_MXU), _NT, preferred_element_type=F32)
        dc = dc + lax.dot_general(pi, wri_ref[...].astype(_MXU), _NT, preferred_element_type=F32)
        cw = cw_ref[...]
        dx = cw[0:1, :] * dc
        for k in range(1, CONV_W):
            dx = dx + cw[k:k + 1, :] * pltpu.roll(dc, TP - k, 0)
        dx_ref[...] = (dx * valid).astype(_MXU)
        for k in range(CONV_W):
            sums_ref[k:k + 1, :] = jnp.sum(dc * r["shifted"][k], axis=0, keepdims=True)
        sums_ref[4:5, :] = jnp.sum(dc, axis=0, keepdims=True)
        sums_ref[5:6, :] = jnp.sum(dpre_r, axis=0, keepdims=True)
        sums_ref[6:7, :] = jnp.sum(dpre_i, axis=0, keepdims=True)
        sums_ref[7:8, :] = dls * _sigmoid(-r["lam"])

    s = _rnn_specs()
    return _pcall(
        body, name="rnn_gates_bwd", grid=(N_RB,),
        in_specs=[s["col"], s["col"], s["col"], s["cw"], s["vec"], s["wblk"], s["wblk"], s["vec"], s["vec"], s["vec"],
                  _ANY, _ANY],
        out_specs=[s["col"], pl.BlockSpec((N_SH, 128, RB), lambda n: (0, 3 * SQ_ROWS // 128, n)),
                   pl.BlockSpec((8, RB), lambda n: (0, n))],
        out_shape=[_sds((TP, D_IN), _MXU), _sds((N_SH, PACK_ROWS, D), F32), _sds((8, D), F32)],
        operands=[z, lam_s, hr, conv_w, conv_b, w_ra, w_ri, b_ra, b_ri, lam, dz, gsq], vmem=60, sem=("parallel",),
        aliases={10: 0, 11: 1})


def _rope_tables():
    half = HD // 2
    inv = ROPE_THETA ** (-jnp.arange(half, dtype=F32) / half)
    pos = (jnp.arange(TP) - PAD).astype(F32)
    ang = pos[:, None] * inv[None, :]
    return jnp.tile(jnp.cos(ang), (1, 4)), jnp.tile(jnp.sin(ang), (1, 4))


def _rope(x, cos_t, sin_t, sign):
    w = x.shape[1]
    lane = lax.broadcasted_iota(jnp.int32, x.shape, 1)
    first = (lane % HD) < (HD // 2)
    swapped = jnp.where(first, pltpu.roll(x, w - HD // 2, 1), pltpu.roll(x, HD // 2, 1))
    ct = jnp.tile(cos_t, (1, w // 128))
    st = jnp.tile(sin_t, (1, w // 128))
    return x * ct + swapped * jnp.where(first, -sign * st, sign * st)


def _rope_fwd(z, cos_t, sin_t):
    def body(q_ref, k_ref, v_ref, c_ref, s_ref, qo_ref, ko_ref, vo_ref):
        c = c_ref[...]
        s = s_ref[...]
        qo_ref[...] = _rope(q_ref[...], c, s, 1.0).astype(_MXU)
        ko_ref[...] = _rope(k_ref[...], c, s, 1.0).astype(_MXU)
        vo_ref[...] = v_ref[...].astype(_MXU)

    tab = pl.BlockSpec((BLK, 128), lambda i: (i, 0))
    kv = pl.BlockSpec((BLK, D_KV), lambda i: (i, 0))
    return _pcall(body, name="rope_fwd", grid=(NBLK,),
                  in_specs=[pl.BlockSpec((BLK, D), lambda i: (i, OFF_Q // D)),
                            pl.BlockSpec((BLK, D_KV), lambda i: (i, OFF_K // D_KV)),
                            pl.BlockSpec((BLK, D_KV), lambda i: (i, OFF_V // D_KV)), tab, tab],
                  out_specs=[pl.BlockSpec((BLK, D), lambda i: (i, 0)), kv, kv],
                  out_shape=[_sds((TP, D), _MXU), _sds((TP, D_KV), _MXU), _sds((TP, D_KV), _MXU)],
                  operands=[z, z, z, cos_t, sin_t], vmem=32, sem=("parallel",))


def _rope_bwd_k(dk, dv, cos_t, sin_t, dz):
    def body(dk_ref, dv_ref, c_ref, s_ref, dz_in, o_ref):
        o_ref[:, 0:D_KV] = _rope(dk_ref[...], c_ref[...], s_ref[...], -1.0).astype(_MXU)
        o_ref[:, D_KV:2 * D_KV] = dv_ref[...].astype(_MXU)

    tab = pl.BlockSpec((BLK, 128), lambda i: (i, 0))
    kv = pl.BlockSpec((BLK, D_KV), lambda i: (i, 0))
    return _pcall(body, name="rope_bwd_k", grid=(NBLK,), in_specs=[kv, kv, tab, tab, _ANY],
                  out_specs=[pl.BlockSpec((BLK, 2 * D_KV), lambda i: (i, OFF_K // (2 * D_KV)))],
                  out_shape=[_sds((TP, D_IN), _MXU)], operands=[dk, dv, cos_t, sin_t, dz], vmem=32, sem=("parallel",),
                  aliases={4: 0})[0]


def _attn_masks(i):
    ql = lax.broadcasted_iota(jnp.int32, (GRP * BLK, BLK), 0) % BLK
    kl = lax.broadcasted_iota(jnp.int32, (GRP * BLK, BLK), 1)
    meta = (kl >= PAD) & ((i >= 1) | (kl <= ql))
    prev = (i >= 2) & (kl > ql)
    cur = (i >= 1) & (kl <= ql)
    return meta, prev, cur


def _kv_specs():
    return [pl.BlockSpec((BLK, D_KV), lambda i: (0, 0)),
            pl.BlockSpec((BLK, D_KV), lambda i: (jnp.maximum(i - 1, 0), 0)),
            pl.BlockSpec((BLK, D_KV), lambda i: (i, 0))]


def _stack_heads(ref, g, width=HD):
    return jnp.concatenate([ref[:, width * (GRP * g + j):width * (GRP * g + j + 1)] for j in range(GRP)], axis=0)


def _sink_column(sink_ref, g):
    return jnp.concatenate([jnp.full((BLK, 1), sink_ref[GRP * g + j], F32) for j in range(GRP)], axis=0)


def _attn_fwd(q, k, v, sinks):
    def body(q_ref, k0_ref, kp_ref, kc_ref, v0_ref, vp_ref, vc_ref, sink_ref, o_ref, lse_ref):
        masks = _attn_masks(pl.program_id(0))
        k_refs = (k0_ref, kp_ref, kc_ref)
        v_refs = (v0_ref, vp_ref, vc_ref)
        for g in range(N_KV):
            gs = slice(HD * g, HD * (g + 1))
            q8 = _stack_heads(q_ref, g)
            sink = _sink_column(sink_ref, g)
            s = [jnp.where(m, lax.dot_general(q8, r[:, gs], _NT, preferred_element_type=F32) * (HD ** -0.5), NEG_INF)
                 for m, r in zip(masks, k_refs)]
            mx = jnp.maximum(jnp.maximum(jnp.max(s[0], -1, keepdims=True), jnp.max(s[1], -1, keepdims=True)),
                             jnp.maximum(jnp.max(s[2], -1, keepdims=True), sink))
            p = [jnp.exp(t - mx) for t in s]
            den = (jnp.sum(p[0], -1, keepdims=True) + jnp.sum(p[1], -1, keepdims=True)
                   + jnp.sum(p[2], -1, keepdims=True) + jnp.exp(sink - mx))
            inv = 1.0 / den
            o8 = jnp.dot((p[0] * inv).astype(_MXU), v_refs[0][:, gs], preferred_element_type=F32)
            o8 = o8 + jnp.dot((p[1] * inv).astype(_MXU), v_refs[1][:, gs], preferred_element_type=F32)
            o8 = o8 + jnp.dot((p[2] * inv).astype(_MXU), v_refs[2][:, gs], preferred_element_type=F32)
            lse8 = mx + jnp.log(den)
            for j in range(GRP):
                h = GRP * g + j
                o_ref[:, HD * h:HD * (h + 1)] = o8[BLK * j:BLK * (j + 1)]
                lse_ref[:, h:h + 1] = lse8[BLK * j:BLK * (j + 1)]

    return _pcall(body, name="attn_fwd", grid=(NBLK,),
                  in_specs=[pl.BlockSpec((BLK, D), lambda i: (i, 0))] + _kv_specs() + _kv_specs()
                           + [pl.BlockSpec(memory_space=pltpu.SMEM)],
                  out_specs=[pl.BlockSpec((BLK, D), lambda i: (i, 0)), pl.BlockSpec((BLK, N_Q), lambda i: (i, 0))],
                  out_shape=[_sds((TP, D), F32), _sds((TP, N_Q), F32)], operands=[q, k, k, k, v, v, v, sinks],
                  vmem=40, sem=("parallel",))


def _attn_bwd(q, k, v, sinks, do, o, lse, cos_t, sin_t, dz, *, hook):
    def body(q_ref, k0_ref, kp_ref, kc_ref, v0_ref, vp_ref, vc_ref, sink_ref, do_ref, o_ref, lse_ref, c_ref, s_ref, dz_in,
             dq_ref, dk_ref, dv_ref, dsink_ref, dqrot_ref):
        i = pl.program_id(0)

        @pl.when(i == 0)
        def _():
            dk_ref[...] = jnp.zeros_like(dk_ref)
            dv_ref[...] = jnp.zeros_like(dv_ref)
            dsink_ref[...] = jnp.zeros_like(dsink_ref)

        masks = _attn_masks(i)
        k_refs = (k0_ref, kp_ref, kc_ref)
        v_refs = (v0_ref, vp_ref, vc_ref)
        row_starts = (0, pl.multiple_of(jnp.maximum(i - 1, 0) * BLK, BLK), pl.multiple_of(i * BLK, BLK))
        scale = HD ** -0.5
        for g in range(N_KV):
            gs = slice(HD * g, HD * (g + 1))
            q8 = _stack_heads(q_ref, g)
            do8 = _stack_heads(do_ref, g)
            dom = do8.astype(_MXU)
            delta = jnp.sum(do8 * _stack_heads(o_ref, g), axis=-1, keepdims=True)
            lse8 = _stack_heads(lse_ref, g, width=1)
            dsk = -jnp.exp(_sink_column(sink_ref, g) - lse8) * delta
            for j in range(GRP):
                h = GRP * g + j
                dsink_ref[0:1, h:h + 1] += jnp.sum(dsk[BLK * j:BLK * (j + 1)], axis=0, keepdims=True)
            dq8 = jnp.zeros((GRP * BLK, HD), F32)
            for part in range(3):
                kk = k_refs[part][:, gs]
                s = jnp.where(masks[part], lax.dot_general(q8, kk, _NT, preferred_element_type=F32) * scale, NEG_INF)
                p = jnp.exp(s - lse8)
                dp = lax.dot_general(dom, v_refs[part][:, gs], _NT, preferred_element_type=F32)
                ds = (p * (dp - delta) * scale).astype(_MXU)
                dq8 = dq8 + jnp.dot(ds, kk, preferred_element_type=F32)
                rows = pl.ds(row_starts[part], BLK)
                dk_ref[rows, gs] += lax.dot_general(ds, q8, _TN, preferred_element_type=F32)
                dv_ref[rows, gs] += lax.dot_general(p.astype(_MXU), dom, _TN, preferred_element_type=F32)
            for j in range(GRP):
                h = GRP * g + j
                dqrot_ref[:, HD * h:HD * (h + 1)] = dq8[BLK * j:BLK * (j + 1)]
        dq_ref[...] = _rope(dqrot_ref[...], c_ref[...], s_ref[...], -1.0).astype(_MXU)

    row = pl.BlockSpec((BLK, D), lambda i: (i, 0))
    tab = pl.BlockSpec((BLK, 128), lambda i: (i, 0))
    full_kv = pl.BlockSpec((TP, D_KV), lambda i: (0, 0))
    return _pcall(
        body, name="attn_bwd", grid=(NBLK,),
        in_specs=[row] + _kv_specs() + _kv_specs() + [pl.BlockSpec(memory_space=pltpu.SMEM), row, row,
                  pl.BlockSpec((BLK, N_Q), lambda i: (i, 0)), tab, tab, _ANY],
        out_specs=[pl.BlockSpec((BLK, D), lambda i: (i, OFF_Q // D)), full_kv, full_kv,
                   pl.BlockSpec((8, 128), lambda i: (0, 0))],
        out_shape=[_sds((TP, D_IN), _MXU), _sds((TP, D_KV), F32), _sds((TP, D_KV), F32), _sds((8, 128), F32)],
        operands=[q, k, k, k, v, v, v, sinks, do, o, lse, cos_t, sin_t, dz], scratch=[pltpu.VMEM((BLK, D), F32)],
        vmem=48, sem=("arbitrary",), aliases={13: 0}, hook=hook)


def _cast_into_slot(w32, pos, *, name, tr):
    r, cc = w32.shape

    def body(pos_ref, w_ref, o_ref):
        o_ref[...] = w_ref[...].astype(BF16)

    return _pcall(body, name=name, grid=(r // tr,), in_specs=[pl.BlockSpec((tr, cc), lambda i, p: (i, 0))],
                  out_specs=[pl.BlockSpec((None, tr, cc), lambda i, p: (p[1], i, 0))],
                  out_shape=[_sds((N_SH, r, cc), BF16)], operands=[w32], vmem=32, sem=("parallel",), prefetch=(pos,))[0]


def _pair_add(g, got, pos, *, name, tr, g_has_both_halves):
    n, h, cc = got.shape
    nt = h // tr

    def body(pos_ref, g_ref, r_ref, own_ref, s16_ref):
        s = g_ref[...] + r_ref[...]
        s16_ref[...] = s.astype(BF16)

        @pl.when(pl.program_id(1) == pos_ref[1])
        def _():
            own_ref[...] = s

    g_index = (lambda i, s, p: (s, p[0] * nt + i, 0)) if g_has_both_halves else (lambda i, s, p: (s, i, 0))
    return _pcall(body, name=name, grid=(nt, n),
                  in_specs=[pl.BlockSpec((None, tr, cc), g_index), pl.BlockSpec((None, tr, cc), lambda i, s, p: (s, i, 0))],
                  out_specs=[pl.BlockSpec((tr, cc), lambda i, s, p: (i, 0)),
                             pl.BlockSpec((None, tr, cc), lambda i, s, p: (s, i, 0))],
                  out_shape=[_sds((h, cc), F32), _sds((n, h, cc), BF16)], operands=[g, got], vmem=40,
                  sem=("parallel", "arbitrary"), prefetch=(pos,))


def _sum_chips(own, got, pos, *, name, tr):
    h, cc = own.shape
    nt = h // tr

    def body(pos_ref, o_ref, r_ref, out_ref):
        acc = o_ref[...]
        for k in range(3):
            acc = acc + r_ref[k].astype(F32)
        out_ref[...] = acc

    return _pcall(body, name=name, grid=(nt,),
                  in_specs=[pl.BlockSpec((tr, cc), lambda i, p: (i, 0)), pl.BlockSpec((3, tr, cc), lambda i, p: (0, i, 0))],
                  out_specs=[pl.BlockSpec((tr, cc), lambda i, p: (p[0] * nt + i, 0))],
                  out_shape=[_sds((2 * h, cc), F32)], operands=[own, got], vmem=40, sem=("parallel",), prefetch=(pos,))[0]


def _adamw(w, g, m, v, *, name, tr):
    r, cc = w.shape

    def body(w_ref, g_ref, m_ref, v_ref, d_ref, mo_ref, vo_ref):
        gg = g_ref[...]
        m_new = ADAM_B1 * m_ref[...] + (1.0 - ADAM_B1) * gg
        v_new = ADAM_B2 * v_ref[...] + (1.0 - ADAM_B2) * (gg * gg)
        m_hat = m_new / (1.0 - ADAM_B1 ** ADAM_STEP)
        v_hat = v_new / (1.0 - ADAM_B2 ** ADAM_STEP)
        d_ref[...] = -ADAM_LR * (m_hat / (jnp.sqrt(v_hat) + ADAM_EPS) + ADAM_WD * w_ref[...])
        mo_ref[...] = m_new
        vo_ref[...] = v_new

    blk = pl.BlockSpec((tr, cc), lambda i: (i, 0))
    return _pcall(body, name=name, grid=(r // tr,), in_specs=[blk] * 4, out_specs=[blk] * 3,
                  out_shape=[_sds((r, cc), F32)] * 3, operands=[w, g, m, v], vmem=48, sem=("parallel",))


_SMALL_ROWS = 40
_B_IN_ROWS = 7


def _row_pad(v, rows):
    flat = v.reshape(-1)
    return jnp.pad(flat, (0, rows * D - flat.shape[0])).reshape(rows, D)


def _pack_ra(w):
    return w.transpose(1, 0, 2).reshape(64, D)


def _unpack_ra(p, like):
    return p.reshape(64, N_RB, RB).transpose(1, 0, 2).reshape(like.shape)


def _gate_full(g4):
    return g4.reshape(N_SH, 64, N_RB, RB).transpose(2, 0, 1, 3).reshape(N_RB, RB, RB)


def kernel(x, meta_tokens, ln_emb_g, ln_emb_b, w_in, b_in, conv_w, conv_b, w_ra, b_ra, w_ri, b_ri, lru_lambda, sinks, w_rnn_out, w_attn_out, w_o, b_o, ln_g, ln_b, loss_target, m_meta_tokens, m_ln_emb_g, m_ln_emb_b, m_w_in, m_b_in, m_conv_w, m_conv_b, m_w_ra, m_b_ra, m_w_ri, m_b_ri, m_lru_lambda, m_sinks, m_w_rnn_out, m_w_attn_out, m_w_o, m_b_o, m_ln_g, m_ln_b, v_meta_tokens, v_ln_emb_g, v_ln_emb_b, v_w_in, v_b_in, v_conv_w, v_conv_b, v_w_ra, v_b_ra, v_w_ri, v_b_ri, v_lru_lambda, v_sinks, v_w_rnn_out, v_w_attn_out, v_w_o, v_b_o, v_ln_g, v_ln_b):
    xi, yi, ci = _my_pos()
    shard = 2 * xi + yi
    pos = jnp.stack([ci, shard, 1 - ci]).astype(jnp.int32)
    cos_t, sin_t = _rope_tables()
    zero_bias = jnp.zeros((1, D), F32)
    ln_emb_g2, ln_emb_b2 = ln_emb_g[None], ln_emb_b[None]

    small = jnp.concatenate([conv_w[0], meta_tokens, jnp.zeros((4, 512), F32)], axis=0)
    small4 = _gather_small(small)
    conv_w_full = small4[:, 0:4].transpose(1, 0, 2).reshape(CONV_W, D)
    meta_full = small4[:, 4:20].transpose(1, 0, 2).reshape(N_META, D)
    w_in4 = _comm_call(_hook_gather(_cast_into_slot(w_in[0], pos, name="cast_w_in", tr=256)), name="gather_w_in")[0]
    wpack = jnp.concatenate([w_rnn_out[0], w_attn_out[0], w_o[0], _pack_ra(w_ra[0]), _pack_ra(w_ri[0])], axis=0)
    wp_own = _cast_into_slot(wpack, pos, name="cast_w_sq", tr=208)

    h0p = jnp.concatenate([jnp.zeros((PAD, D), F32), meta_full, x[0]], axis=0)
    h32, h16 = _ln_emb_fwd(h0p, ln_emb_g2, ln_emb_b2)
    z, wp4 = _mm_nn(h16, w_in4, b_in, name="mm_z", grid=(2, N_SH * PER_IN), tm=HALF_TP, tn=TN_IN, k=D,
                    a_index=lambda i, j: (i, 0), w_block=(None, D, TN_IN),
                    w_index=lambda i, j: (j // PER_IN, 0, j % PER_IN), out_cols=D_IN, out_index=lambda i, j: (i, j),
                    hook=_hook_gather(wp_own))
    w_ra_full = _gate_full(wp4[:, 3 * SQ_ROWS:3 * SQ_ROWS + 64])
    w_ri_full = _gate_full(wp4[:, 3 * SQ_ROWS + 64:PACK_ROWS])

    def sq_nn(a, kk, bias, name, out_cols, out_index, carried=None):
        return _mm_nn(a, wp4, bias, name=name, grid=(2, D // CW), tm=HALF_TP, tn=CW, k=D, a_index=lambda i, j: (i, 0),
                      w_block=(N_SH, SQ_ROWS, CW), w_index=lambda i, j: (0, kk, j), out_cols=out_cols,
                      out_index=out_index, carried=carried)[0]

    def sq_nt(a, a_blk, kk, name, hook=None):
        return _mm_nt(a, wp4, name=name, grid=(2, N_SH, 1), tm=HALF_TP, tn=SQ_ROWS, tk=D,
                      a_index=lambda i, j, q: (i, a_blk), w_block=(None, SQ_ROWS, D),
                      w_index=lambda i, j, q: (j, kk, 0), out_cols=D, hook=hook)

    a_dec, u_in = _rnn_gates_fwd(z, conv_w_full, conv_b, w_ra_full, w_ri_full, b_ra, b_ri, lru_lambda)
    hr = _scan_fwd(a_dec, u_in)
    ya_in = _mul_silu_fwd(hr, z, OFF_GR, name="gate_a_fwd")
    y2 = sq_nn(ya_in, 0, zero_bias, "mm_ya", 2 * D, lambda i, j: (i, j))
    q, k, v = _rope_fwd(z, cos_t, sin_t)
    o, lse = _attn_fwd(q, k, v, sinks[0])
    yb_in = _mul_silu_fwd(o, z, OFF_GA, name="gate_b_fwd")
    y2 = sq_nn(yb_in, 1, zero_bias, "mm_yb", 2 * D, lambda i, j: (i, D // CW + j), carried=y2)
    mixed = _merge_fwd(y2, z)
    out = sq_nn(mixed, 2, b_o, "mm_out", D, lambda i, j: (i, j))
    dr, sums_o = _ln_out_loss(h32, out, loss_target[0], ln_g, ln_b)

    def sq_tn(a, b, b_blk0, kk, name, carried=None):
        return _mm_tn(a, b, name=name, grid=(N_SH, D // CW), tm=SQ_ROWS, tn=CW, a_index=lambda i, j: (0, i),
                      b_index=lambda i, j: (0, b_blk0 + j), out_shape=(N_SH, PACK_ROWS, D),
                      out_block=(None, SQ_ROWS, CW), out_index=lambda i, j: (i, kk, j), carried=carried)[0]

    gsq = sq_tn(mixed, dr, 0, 2, "mm_dwo")
    dmix = sq_nt(dr, 0, 2, "mm_dmix")[0]
    dy2, dz = _merge_bwd(dmix, y2, z)
    gsq = sq_tn(ya_in, dy2, 0, 0, "mm_dwrnn", carried=gsq)
    gsq = sq_tn(yb_in, dy2, D // CW, 1, "mm_dwattn", carried=gsq)
    dya_in = sq_nt(dy2, 0, 0, "mm_dyain")[0]
    dhr, dz = _mul_silu_bwd(dya_in, hr, z, OFF_GR, dz, name="gate_a_bwd")
    lam_s = _scan_bwd(a_dec, dhr)
    dz, gsq, sums_r = _rnn_gates_bwd(z, lam_s, hr, conv_w_full, conv_b, w_ra_full, w_ri_full, b_ra, b_ri, lru_lambda, dz, gsq)
    dyb_in = sq_nt(dy2, 1, 1, "mm_dybin")[0]
    do, dz, gsq, got_sq = _mul_silu_bwd(dyb_in, o, z, OFF_GA, dz, name="gate_b_bwd", hook=_hook_pair(gsq, True))
    own_sq, s16_sq = _pair_add(gsq, got_sq, pos, name="red_w_sq_add", tr=208, g_has_both_halves=True)
    dz, dk_rot, dv32, dsink, s16_sq, oth_sq = _attn_bwd(q, k, v, sinks[0], do, o, lse, cos_t, sin_t, dz,
                                                        hook=_hook_scatter(s16_sq))
    dz = _rope_bwd_k(dk_rot, dv32, cos_t, sin_t, dz)
    red_sq = _sum_chips(own_sq, oth_sq, pos, name="red_w_sq_sum", tr=208)
    g_b_in, g_sq = _colsum(dz, name="colsum_dz", tn=TN_IN, hook=_hook_halves(red_sq))

    def dwin(half_idx, name, hook=None):
        return _mm_tn(h16, dz, name=name, grid=(1, N_SH * PER_IN), tm=D // 2, tn=TN_IN,
                      a_index=lambda i, j, p: (0, p[half_idx]), b_index=lambda i, j, p: (0, j),
                      out_shape=(N_SH, D // 2, W_IN_COLS), out_block=(None, D // 2, TN_IN),
                      out_index=lambda i, j, p: (j // PER_IN, 0, j % PER_IN), prefetch=(pos,), hook=hook)

    gin_sib = dwin(2, "mm_dwin_sib")[0]
    gin_own, gin_sib, got_in = dwin(0, "mm_dwin_own", hook=_hook_pair(gin_sib, False))
    own_in, s16_in = _pair_add(gin_own, got_in, pos, name="red_w_in_add", tr=128, g_has_both_halves=False)
    dhz, s16_in, oth_in = _mm_nt(dz, w_in4, name="mm_dhz", grid=(2, 2, N_SH * PER_IN), tm=HALF_TP, tn=D // 2, tk=TN_IN,
                                 a_index=lambda i, j, q: (i, q), w_block=(None, D // 2, TN_IN),
                                 w_index=lambda i, j, q: (q // PER_IN, j, q % PER_IN), out_cols=D,
                                 hook=_hook_scatter(s16_in))
    red_in = _sum_chips(own_in, oth_in, pos, name="red_w_in_sum", tr=128)
    dh0, sums_e, g_in = _ln_emb_bwd(h0p, ln_emb_g2, dr, dhz, hook=_hook_halves(red_in))

    d_in, m_in, v_in = _adamw(w_in[0], g_in, m_w_in[0], v_w_in[0], name="adamw_w_in", tr=128)
    mpack = jnp.concatenate([m_w_rnn_out[0], m_w_attn_out[0], m_w_o[0], _pack_ra(m_w_ra[0]), _pack_ra(m_w_ri[0])], axis=0)
    vpack = jnp.concatenate([v_w_rnn_out[0], v_w_attn_out[0], v_w_o[0], _pack_ra(v_w_ra[0]), _pack_ra(v_w_ri[0])], axis=0)
    d_sq, m_sq, v_sq = _adamw(wpack, g_sq, mpack, vpack, name="adamw_w_sq", tr=208)

    spack = jnp.concatenate([
        sums_e[0:1], sums_e[1:2], _row_pad(g_b_in, _B_IN_ROWS), sums_r[0:4], sums_r[4:5], sums_r[5:6], sums_r[6:7],
        sums_r[7:8], _row_pad(dsink[0:1, 0:N_Q], 1), sums_o[2:3], sums_o[0:1], sums_o[1:2], dh0[PAD:BLK], sums_o[3:4],
        jnp.zeros((_SMALL_ROWS - 38, D), F32)], axis=0)
    sred = _allreduce_small(spack)
    loss = sred[37, 0]
    col0 = shard * 512
    g_conv_w = lax.dynamic_slice(sred[9:13], (0, col0), (CONV_W, 512))
    g_meta = lax.dynamic_slice(sred[21:37], (0, col0), (N_META, 512))
    small_g = {"ln_emb_g": sred[0:1], "ln_emb_b": sred[1:2], "b_in": sred[2:9], "conv_w": g_conv_w.reshape(1, D),
               "conv_b": sred[13:14], "b_ra": sred[14:15], "b_ri": sred[15:16], "lru_lambda": sred[16:17],
               "sinks": sred[17:18], "b_o": sred[18:19], "ln_g": sred[19:20], "ln_b": sred[20:21],
               "meta_tokens": g_meta.reshape(4, D)}
    small_names = list(small_g)

    def small_pack(vals):
        rows = []
        for n in small_names:
            a = vals[n]
            if n == "b_in":
                a = _row_pad(a, _B_IN_ROWS)
            elif n == "sinks":
                a = _row_pad(a, 1)
            else:
                a = a.reshape(-1, D)
            rows.append(a)
        return jnp.concatenate(rows + [jnp.zeros((24 - 22, D), F32)], axis=0)

    w_small = dict(ln_emb_g=ln_emb_g, ln_emb_b=ln_emb_b, b_in=b_in, conv_w=conv_w, conv_b=conv_b, b_ra=b_ra, b_ri=b_ri,
                   lru_lambda=lru_lambda, sinks=sinks, b_o=b_o, ln_g=ln_g, ln_b=ln_b, meta_tokens=meta_tokens)
    m_small = dict(ln_emb_g=m_ln_emb_g, ln_emb_b=m_ln_emb_b, b_in=m_b_in, conv_w=m_conv_w, conv_b=m_conv_b, b_ra=m_b_ra,
                   b_ri=m_b_ri, lru_lambda=m_lru_lambda, sinks=m_sinks, b_o=m_b_o, ln_g=m_ln_g, ln_b=m_ln_b,
                   meta_tokens=m_meta_tokens)
    v_small = dict(ln_emb_g=v_ln_emb_g, ln_emb_b=v_ln_emb_b, b_in=v_b_in, conv_w=v_conv_w, conv_b=v_conv_b, b_ra=v_b_ra,
                   b_ri=v_b_ri, lru_lambda=v_lru_lambda, sinks=v_sinks, b_o=v_b_o, ln_g=v_ln_g, ln_b=v_ln_b,
                   meta_tokens=v_meta_tokens)
    g_small_pack = jnp.concatenate([small_g[n] for n in small_names] + [jnp.zeros((2, D), F32)], axis=0)
    d_sm, m_sm, v_sm = _adamw(small_pack(w_small), g_small_pack, small_pack(m_small), small_pack(v_small),
                              name="adamw_small", tr=24)

    small_rows = {}
    r0 = 0
    for n in small_names:
        nrows = small_g[n].shape[0]
        small_rows[n] = (r0, nrows)
        r0 += nrows

    def small_out(packed, n, like):
        a, nrows = small_rows[n]
        flat = packed[a:a + nrows].reshape(-1)
        return flat[:like.size].reshape(like.shape)

    def sq_out(packed, kk, like):
        if kk < 3:
            return packed[SQ_ROWS * kk:SQ_ROWS * (kk + 1)].reshape(like.shape)
        return _unpack_ra(packed[3 * SQ_ROWS + 64 * (kk - 3):3 * SQ_ROWS + 64 * (kk - 2)], like)

    sq_idx = {"w_rnn_out": 0, "w_attn_out": 1, "w_o": 2, "w_ra": 3, "w_ri": 4}
    weights = dict(meta_tokens=meta_tokens, ln_emb_g=ln_emb_g, ln_emb_b=ln_emb_b, w_in=w_in, b_in=b_in, conv_w=conv_w,
                   conv_b=conv_b, w_ra=w_ra, b_ra=b_ra, w_ri=w_ri, b_ri=b_ri, lru_lambda=lru_lambda, sinks=sinks,
                   w_rnn_out=w_rnn_out, w_attn_out=w_attn_out, w_o=w_o, b_o=b_o, ln_g=ln_g, ln_b=ln_b)

    def outputs(big_in, big_sq, small):
        res = []
        for n, like in weights.items():
            if n == "w_in":
                res.append(big_in.reshape(like.shape))
            elif n in sq_idx:
                res.append(sq_out(big_sq, sq_idx[n], like))
            else:
                res.append(small_out(small, n, like))
        return res

    return (loss, dh0[BLK:][None], *outputs(g_in, g_sq, g_small_pack), *outputs(d_in, d_sq, d_sm),
            *outputs(m_in, m_sq, m_sm), *outputs(v_in, v_sq, v_sm))
```

```python
import jax
import jax.numpy as jnp
from jax import lax
from jax.experimental import pallas as pl
from jax.experimental.pallas import tpu as pltpu

F32 = jnp.float32
BF16 = jnp.bfloat16
_MXU = jnp.bfloat16

D = 2048
SEQ = 2048
N_META = 16
BLK = 128
PAD = BLK - N_META
TP = PAD + N_META + SEQ
NBLK = TP // BLK
HALF_TP = TP // 2
N_RB = 8
RB = 256
CONV_W = 4
LRU_C = 8.0
HD = 64
N_Q = 32
N_KV = 4
GRP = 8
D_KV = 256
NEG_INF = -1e30
LN_EPS = 1e-5
ALPHA = 2.0 ** 0.25
ROPE_THETA = 10000.0
OFF_GR, OFF_Q, OFF_K, OFF_V, OFF_GA, OFF_G = 2048, 4096, 6144, 6400, 6656, 8704
D_IN = 12800
N_SH = 4
W_IN_COLS = D_IN // N_SH
TN_IN = 640
PER_IN = W_IN_COLS // TN_IN
CW = 512
RT = TP // 4
SQ_ROWS = 512
PACK_ROWS = 3 * SQ_ROWS + 128

ADAM_LR = 0.001
ADAM_B1 = 0.9
ADAM_B2 = 0.999
ADAM_EPS = 1e-08
ADAM_WD = 0.01
ADAM_STEP = 10

MESH = pl.DeviceIdType.MESH
_MIB = 1024 * 1024
_ANY = pl.BlockSpec(memory_space=pl.ANY)
_NT = (((1,), (1,)), ((), ()))
_TN = (((0,), (0,)), ((), ()))


def _sds(shape, dtype):
    return jax.ShapeDtypeStruct(shape, dtype)


def _sigmoid(x):
    return 1.0 / (1.0 + jnp.exp(-x))


def _my_pos():
    return lax.axis_index("x"), lax.axis_index("y"), lax.axis_index("c")


def _other_chips(x, y):
    return [(1 - x, y), (x, 1 - y), (1 - x, 1 - y)]


def _remote(src, dst, send_sems, recv_sems, k, dev):
    return pltpu.make_async_remote_copy(src_ref=src, dst_ref=dst, send_sem=send_sems.at[k], recv_sem=recv_sems.at[k],
                                        device_id=dev, device_id_type=MESH)


class _Hook:
    def __init__(self, carried, landing, n_sems, start, finish):
        self.carried, self.landing, self.n_sems, self.start, self.finish = list(carried), list(landing), n_sems, start, finish


def _hook_gather(buf):
    half = buf.shape[1] // 2

    def plan(o, ss, rs):
        x, y, c = _my_pos()
        chips = _other_chips(x, y)
        mine_rows = pl.ds(pl.multiple_of(c * half, 16), half)
        sib_rows = pl.ds(pl.multiple_of((1 - c) * half, 16), half)
        mine = o.at[2 * x + y, mine_rows]
        sends = [_remote(mine, mine, ss, rs, k, (px, py, c)) for k, (px, py) in enumerate(chips)]
        return sends, chips, mine_rows, sib_rows, (x, y, c)

    def start(car, land, ss, rs):
        for cp in plan(car[0], ss, rs)[0]:
            cp.start()

    def finish(car, land, ss, rs):
        o = car[0]
        sends, chips, mine_rows, sib_rows, (x, y, c) = plan(o, ss, rs)
        passed = []
        for k, (px, py) in enumerate(chips):
            got = o.at[2 * px + py, mine_rows]
            _remote(got, got, ss, rs, k, (px, py, c)).wait_recv()
            fwd = _remote(got, got, ss, rs, 3 + k, (x, y, 1 - c))
            fwd.start()
            passed.append(fwd)
        for k, (px, py) in enumerate(chips):
            got = o.at[2 * px + py, sib_rows]
            _remote(got, got, ss, rs, 3 + k, (x, y, 1 - c)).wait_recv()
        for cp in sends + passed:
            cp.wait_send()

    return _Hook([buf], [], 6, start, finish)


def _hook_pair(g, half_rows):
    n, r, cc = g.shape
    h = r // 2 if half_rows else r

    def plan(car, land, ss, rs):
        x, y, c = _my_pos()
        cps = []
        for s in range(n):
            src = car[0].at[s, pl.ds(pl.multiple_of((1 - c) * h, 8), h)] if half_rows else car[0].at[s]
            cps.append(_remote(src, land[0].at[s], ss, rs, s, (x, y, 1 - c)))
        return cps

    def start(car, land, ss, rs):
        for cp in plan(car, land, ss, rs):
            cp.start()

    def finish(car, land, ss, rs):
        cps = plan(car, land, ss, rs)
        for cp in cps:
            cp.wait_recv()
        for cp in cps:
            cp.wait_send()

    return _Hook([g], [_sds((n, h, cc), g.dtype)], n, start, finish)


def _hook_scatter(s16):
    _, h, cc = s16.shape

    def plan(car, land, ss, rs):
        x, y, c = _my_pos()
        return [_remote(car[0].at[2 * px + py], land[0].at[k], ss, rs, k, (px, py, c))
                for k, (px, py) in enumerate(_other_chips(x, y))]

    def start(car, land, ss, rs):
        for cp in plan(car, land, ss, rs):
            cp.start()

    def finish(car, land, ss, rs):
        cps = plan(car, land, ss, rs)
        for cp in cps:
            cp.wait_recv()
        for cp in cps:
            cp.wait_send()

    return _Hook([s16], [_sds((3, h, cc), s16.dtype)], 3, start, finish)


def _hook_halves(full):
    h = full.shape[0] // 2

    def half_copy(car, ss, rs, which):
        x, y, c = _my_pos()
        rows = car[0].at[pl.ds(pl.multiple_of((c + which - 2 * c * which) * h, 8), h)]
        return _remote(rows, rows, ss, rs, 0, (x, y, 1 - c))

    def start(car, land, ss, rs):
        half_copy(car, ss, rs, 0).start()

    def finish(car, land, ss, rs):
        half_copy(car, ss, rs, 1).wait_recv()
        half_copy(car, ss, rs, 0).wait_send()

    return _Hook([full], [], 1, start, finish)


def _pcall(body, *, name, grid, in_specs, out_specs, out_shape, operands, scratch=(), vmem=48, sem=None,
           prefetch=(), aliases=None, hook=None):
    n_pre, n_in, n_out, n_scr = len(prefetch), len(in_specs), len(out_specs), len(scratch)
    in_specs, out_specs, out_shape, scratch = list(in_specs), list(out_specs), list(out_shape), list(scratch)
    io_alias = {n_pre + a: b for a, b in (aliases or {}).items()}
    operands = list(operands)
    kernel_body = body
    if hook is not None:
        n_car, n_land = len(hook.carried), len(hook.landing)
        for t, arr in enumerate(hook.carried):
            io_alias[n_pre + n_in + t] = n_out + t
        in_specs += [_ANY] * n_car
        out_specs += [_ANY] * (n_car + n_land)
        out_shape += [_sds(a.shape, a.dtype) for a in hook.carried] + hook.landing
        scratch += [pltpu.SemaphoreType.DMA((hook.n_sems,)), pltpu.SemaphoreType.DMA((hook.n_sems,))]
        operands += hook.carried
        sem = ("arbitrary",) * len(grid)

        def kernel_body(*refs):
            pre, rest = refs[:n_pre], refs[n_pre:]
            ins = rest[:n_in]
            outs = rest[n_in + n_car:n_in + n_car + n_out]
            car = rest[n_in + n_car + n_out:n_in + 2 * n_car + n_out]
            land = rest[n_in + 2 * n_car + n_out:n_in + 2 * n_car + n_out + n_land]
            scr = rest[n_in + 2 * n_car + n_out + n_land:]
            send_sems, recv_sems = scr[n_scr], scr[n_scr + 1]
            first = pl.program_id(0) == 0
            last = pl.program_id(0) == grid[0] - 1
            for d in range(1, len(grid)):
                first = first & (pl.program_id(d) == 0)
                last = last & (pl.program_id(d) == grid[d] - 1)

            @pl.when(first)
            def _():
                hook.start(car, land, send_sems, recv_sems)

            body(*pre, *ins, *outs, *scr[:n_scr])

            @pl.when(last)
            def _():
                hook.finish(car, land, send_sems, recv_sems)

    params = pltpu.CompilerParams(vmem_limit_bytes=vmem * _MIB, dimension_semantics=sem,
                                  has_side_effects=hook is not None)
    if n_pre:
        call = pl.pallas_call(
            kernel_body, name=name, out_shape=out_shape, input_output_aliases=io_alias, compiler_params=params,
            grid_spec=pltpu.PrefetchScalarGridSpec(num_scalar_prefetch=n_pre, grid=grid, in_specs=in_specs,
                                                   out_specs=out_specs, scratch_shapes=scratch))
    else:
        call = pl.pallas_call(kernel_body, name=name, grid=grid, in_specs=in_specs, out_specs=out_specs,
                              out_shape=out_shape, scratch_shapes=scratch, input_output_aliases=io_alias,
                              compiler_params=params)
    return call(*prefetch, *operands)


def _comm_call(hook, *, name):
    n_car = len(hook.carried)

    def body(*refs):
        car = refs[n_car:2 * n_car]
        land = refs[2 * n_car:2 * n_car + len(hook.landing)]
        send_sems, recv_sems = refs[-2], refs[-1]
        hook.start(car, land, send_sems, recv_sems)
        hook.finish(car, land, send_sems, recv_sems)

    return pl.pallas_call(
        body, name=name, in_specs=[_ANY] * n_car, out_specs=[_ANY] * (n_car + len(hook.landing)),
        out_shape=[_sds(a.shape, a.dtype) for a in hook.carried] + hook.landing,
        input_output_aliases={t: t for t in range(n_car)},
        scratch_shapes=[pltpu.SemaphoreType.DMA((hook.n_sems,)), pltpu.SemaphoreType.DMA((hook.n_sems,))],
        compiler_params=pltpu.CompilerParams(has_side_effects=True),
    )(*hook.carried)


def _gather_small(blk):
    r, cc = blk.shape

    def body(x_ref, o_ref, send_sems, recv_sems):
        x, y, c = _my_pos()
        me = 2 * x + y
        o_ref[me] = x_ref[...]
        sends = [_remote(x_ref, o_ref.at[me], send_sems, recv_sems, k, (px, py, c))
                 for k, (px, py) in enumerate(_other_chips(x, y))]
        for cp in sends:
            cp.start()
        for k, (px, py) in enumerate(_other_chips(x, y)):
            _remote(x_ref, o_ref.at[2 * px + py], send_sems, recv_sems, k, (px, py, c)).wait_recv()
        for cp in sends:
            cp.wait_send()

    vm = pl.BlockSpec(memory_space=pltpu.VMEM)
    return pl.pallas_call(
        body, name="gather_small", in_specs=[vm], out_specs=vm, out_shape=_sds((N_SH, r, cc), blk.dtype),
        scratch_shapes=[pltpu.SemaphoreType.DMA((3,)), pltpu.SemaphoreType.DMA((3,))],
        compiler_params=pltpu.CompilerParams(has_side_effects=True),
    )(blk)


N_DEV = 8


def _allreduce_small(pack):
    r, cc = pack.shape

    def body(x_ref, o_ref, buf_ref, send_sems, recv_sems):
        x, y, c = _my_pos()
        me = 4 * x + 2 * y + c
        buf_ref[me] = x_ref[...]
        cps = []
        for k in range(1, N_DEV):
            peer = (x ^ ((k >> 2) & 1), y ^ ((k >> 1) & 1), c ^ (k & 1))
            cps.append(_remote(x_ref, buf_ref.at[me], send_sems, recv_sems, k - 1, peer))
        for cp in cps:
            cp.start()
        for k in range(1, N_DEV):
            peer = (x ^ ((k >> 2) & 1), y ^ ((k >> 1) & 1), c ^ (k & 1))
            src = 4 * peer[0] + 2 * peer[1] + peer[2]
            _remote(x_ref, buf_ref.at[src], send_sems, recv_sems, k - 1, peer).wait_recv()
        acc = buf_ref[0]
        for d in range(1, N_DEV):
            acc = acc + buf_ref[d]
        o_ref[...] = acc
        for cp in cps:
            cp.wait_send()

    vm = pl.BlockSpec(memory_space=pltpu.VMEM)
    return pl.pallas_call(
        body, name="allreduce_small", in_specs=[vm], out_specs=vm, out_shape=_sds((r, cc), F32),
        scratch_shapes=[pltpu.VMEM((N_DEV, r, cc), F32), pltpu.SemaphoreType.DMA((N_DEV - 1,)),
                        pltpu.SemaphoreType.DMA((N_DEV - 1,))],
        compiler_params=pltpu.CompilerParams(has_side_effects=True),
    )(pack)


def _mm_nn(a, w, bias, *, name, grid, tm, tn, k, a_index, w_block, w_index, out_cols, out_index, carried=None, hook=None):
    m = a.shape[0]

    def body(a_ref, w_ref, b_ref, *rest):
        o_ref = rest[-1]
        wv = w_ref[...]
        acc = jnp.dot(a_ref[...].astype(_MXU), wv.reshape(k, tn).astype(_MXU), preferred_element_type=F32)
        o_ref[...] = acc + b_ref[...]

    operands = [a, w, bias] + ([carried] if carried is not None else [])
    return _pcall(
        body, name=name, grid=grid,
        in_specs=[pl.BlockSpec((tm, k), a_index), pl.BlockSpec(w_block, w_index),
                  pl.BlockSpec((1, tn), lambda i, j: (0, j))] + ([_ANY] if carried is not None else []),
        out_specs=[pl.BlockSpec((tm, tn), out_index)], out_shape=[_sds((m, out_cols), F32)], operands=operands,
        vmem=56, sem=("parallel", "parallel"), aliases={3: 0} if carried is not None else None, hook=hook)


def _mm_nt(a, w, *, name, grid, tm, tn, tk, a_index, w_block, w_index, out_cols, hook=None):
    m = a.shape[0]
    nk = grid[2]

    def body(a_ref, w_ref, o_ref, acc_ref):
        part = lax.dot_general(a_ref[...].astype(_MXU), w_ref[...].astype(_MXU), _NT, preferred_element_type=F32)
        if nk == 1:
            o_ref[...] = part
        else:
            kidx = pl.program_id(2)

            @pl.when(kidx == 0)
            def _():
                acc_ref[...] = part

            @pl.when(kidx > 0)
            def _():
                acc_ref[...] += part

            @pl.when(kidx == nk - 1)
            def _():
                o_ref[...] = acc_ref[...]

    return _pcall(
        body, name=name, grid=grid, in_specs=[pl.BlockSpec((tm, tk), a_index), pl.BlockSpec(w_block, w_index)],
        out_specs=[pl.BlockSpec((tm, tn), lambda i, j, q: (i, j))], out_shape=[_sds((m, out_cols), F32)],
        operands=[a, w], scratch=[pltpu.VMEM((tm, tn) if nk > 1 else (8, 128), F32)], vmem=56,
        sem=("parallel", "parallel", "arbitrary"), hook=hook)


def _mm_tn(a, b, *, name, grid, tm, tn, a_index, b_index, out_shape, out_block, out_index, carried=None,
           prefetch=(), hook=None):
    t = a.shape[0]

    def body(*refs):
        a_ref, b_ref = refs[len(prefetch)], refs[len(prefetch) + 1]
        refs[-1][...] = lax.dot_general(a_ref[...].astype(_MXU), b_ref[...].astype(_MXU), _TN, preferred_element_type=F32)

    operands = [a, b] + ([carried] if carried is not None else [])
    return _pcall(
        body, name=name, grid=grid,
        in_specs=[pl.BlockSpec((t, tm), a_index), pl.BlockSpec((t, tn), b_index)] + ([_ANY] if carried is not None else []),
        out_specs=[pl.BlockSpec(out_block, out_index)], out_shape=[_sds(out_shape, F32)], operands=operands,
        vmem=56, sem=("parallel", "parallel"), aliases={2: 0} if carried is not None else None, prefetch=prefetch, hook=hook)


def _mm_z_gather(h16, w_own, b_in, order, wp_own):
    n_tiles = N_SH * PER_IN
    half = D // 2
    sq = _hook_gather(wp_own)

    def body(order_ref, a_ref, b_ref, w_in_ref, wp_in_ref, z_ref, w_ref, wp_ref, wbuf, tile_sems, ss, rs, sq_ss, sq_rs):
        del w_in_ref, wp_in_ref
        j = pl.program_id(0)
        x, y, c = _my_pos()
        me = 2 * x + y
        chips = _other_chips(x, y)
        mine_rows = pl.ds(pl.multiple_of(c * half, 16), half)
        sib_rows = pl.ds(pl.multiple_of((1 - c) * half, 16), half)

        def ici(k, slot):
            px, py = chips[k]
            part = w_ref.at[slot, mine_rows]
            return _remote(part, part, ss, rs, k, (px, py, c))

        def d2d(k, rows):
            px, py = chips[k]
            part = w_ref.at[2 * px + py, rows]
            return _remote(part, part, ss, rs, 3 + k, (x, y, 1 - c))

        def tile_copy(t):
            rel = t // PER_IN
            slot = me ^ jnp.where(rel == 0, 0, jnp.where(rel == 1, 2, jnp.where(rel == 2, 1, 3)))
            col = pl.multiple_of((t % PER_IN) * TN_IN, 128)
            return pltpu.make_async_copy(w_ref.at[slot, :, pl.ds(col, TN_IN)], wbuf.at[t % 2], tile_sems.at[t % 2])

        @pl.when(j == 0)
        def _():
            for k in range(3):
                ici(k, me).start()
            sq.start([wp_ref], [], sq_ss, sq_rs)
            tile_copy(0).start()

        for k in range(3):
            px, py = chips[k]

            @pl.when(j == PER_IN * (k + 1) - 2)
            def _():
                ici(k, 2 * px + py).wait_recv()
                d2d(k, mine_rows).start()

            @pl.when(j == PER_IN * (k + 1) - 1)
            def _():
                d2d(k, sib_rows).wait_recv()

        @pl.when(j + 1 < n_tiles)
        def _():
            tile_copy(j + 1).start()

        tile_copy(j).wait()
        z_ref[...] = jnp.dot(a_ref[...], wbuf[j % 2], preferred_element_type=F32) + b_ref[...]

        @pl.when(j == n_tiles - 1)
        def _():
            for k in range(3):
                ici(k, me).wait_send()
                d2d(k, mine_rows).wait_send()
            sq.finish([wp_ref], [], sq_ss, sq_rs)

    col_tile = lambda j, o: (0, o[j // PER_IN] * PER_IN + j % PER_IN)
    return pl.pallas_call(
        body, name="mm_z_gather",
        grid_spec=pltpu.PrefetchScalarGridSpec(
            num_scalar_prefetch=1, grid=(n_tiles,),
            in_specs=[pl.BlockSpec((TP, D), lambda j, o: (0, 0)), pl.BlockSpec((1, TN_IN), col_tile), _ANY, _ANY],
            out_specs=[pl.BlockSpec((TP, TN_IN), col_tile), _ANY, _ANY],
            scratch_shapes=[pltpu.VMEM((2, D, TN_IN), BF16), pltpu.SemaphoreType.DMA((2,)),
                            pltpu.SemaphoreType.DMA((6,)), pltpu.SemaphoreType.DMA((6,)),
                            pltpu.SemaphoreType.DMA((6,)), pltpu.SemaphoreType.DMA((6,))]),
        out_shape=[_sds((TP, D_IN), F32), _sds(w_own.shape, w_own.dtype), _sds(wp_own.shape, wp_own.dtype)],
        input_output_aliases={3: 1, 4: 2},
        compiler_params=pltpu.CompilerParams(vmem_limit_bytes=60 * _MIB, dimension_semantics=("arbitrary",),
                                             has_side_effects=True),
    )(order, h16, b_in, w_own, wp_own)


def _padded_rows(i, x_ref, meta_ref):
    head = jnp.concatenate([jnp.zeros((PAD, D), F32), meta_ref[...]], axis=0)
    return jnp.where(i == 0, head, x_ref[...])


def _stream_specs():
    return [pl.BlockSpec((BLK, D), lambda i: (jnp.maximum(i - 1, 0), 0)), pl.BlockSpec((N_META, D), lambda i: (0, 0))]


def _ln_emb_fwd(x, meta, g, b):
    def body(x_ref, meta_ref, g_ref, b_ref, h32_ref, h16_ref):
        x = _padded_rows(pl.program_id(0), x_ref, meta_ref)
        mu = jnp.mean(x, axis=-1, keepdims=True)
        xc = x - mu
        var = jnp.mean(xc * xc, axis=-1, keepdims=True)
        y = xc * lax.rsqrt(var + LN_EPS) * g_ref[...] + b_ref[...]
        h32_ref[...] = y
        h16_ref[...] = y.astype(_MXU)

    row = pl.BlockSpec((BLK, D), lambda i: (i, 0))
    vec = pl.BlockSpec((1, D), lambda i: (0, 0))
    return _pcall(body, name="ln_emb_fwd", grid=(NBLK,), in_specs=_stream_specs() + [vec, vec], out_specs=[row, row],
                  out_shape=[_sds((TP, D), F32), _sds((TP, D), _MXU)], operands=[x, meta, g, b], vmem=32,
                  sem=("parallel",))


def _ln_emb_bwd(x, meta, g, dr, dhz, *, hook):
    def body(x_ref, meta_ref, g_ref, dr_ref, dhz_ref, dx_ref, dmeta_ref, acc_ref):
        i = pl.program_id(0)

        @pl.when(i == 0)
        def _():
            acc_ref[...] = jnp.zeros_like(acc_ref)

        x = _padded_rows(i, x_ref, meta_ref)
        mu = jnp.mean(x, axis=-1, keepdims=True)
        xc = x - mu
        var = jnp.mean(xc * xc, axis=-1, keepdims=True)
        rstd = lax.rsqrt(var + LN_EPS)
        xhat = xc * rstd
        dh = ALPHA * dr_ref[...] + dhz_ref[...]
        acc_ref[0:1, :] += jnp.sum(dh * xhat, axis=0, keepdims=True)
        acc_ref[1:2, :] += jnp.sum(dh, axis=0, keepdims=True)
        dxh = dh * g_ref[...]
        m1 = jnp.mean(dxh, axis=-1, keepdims=True)
        m2 = jnp.mean(dxh * xhat, axis=-1, keepdims=True)
        dx = rstd * (dxh - m1 - xhat * m2)
        dx_ref[...] = dx

        @pl.when(i == 0)
        def _():
            dmeta_ref[...] = dx[PAD:BLK]

    row = pl.BlockSpec((BLK, D), lambda i: (i, 0))
    vec = pl.BlockSpec((1, D), lambda i: (0, 0))
    xs, ms = _stream_specs()
    return _pcall(body, name="ln_emb_bwd", grid=(NBLK,), in_specs=[xs, ms, vec, row, row],
                  out_specs=[xs, ms, pl.BlockSpec((8, D), lambda i: (0, 0))],
                  out_shape=[_sds((SEQ, D), F32), _sds((N_META, D), F32), _sds((8, D), F32)],
                  operands=[x, meta, g, dr, dhz], vmem=32, sem=("arbitrary",), hook=hook)


def _mul_silu_fwd(a, z, off, *, name):
    def body(a_ref, z_ref, o_ref):
        zz = z_ref[...]
        o_ref[...] = (a_ref[...] * (zz * _sigmoid(zz))).astype(_MXU)

    blk = pl.BlockSpec((RT, CW), lambda i, j: (i, j))
    return _pcall(body, name=name, grid=(TP // RT, D // CW),
                  in_specs=[blk, pl.BlockSpec((RT, CW), lambda i, j: (i, off // CW + j))], out_specs=[blk],
                  out_shape=[_sds((TP, D), _MXU)], operands=[a, z], vmem=32, sem=("parallel", "parallel"))[0]


def _mul_silu_bwd(dy, a, z, off, dz, *, name, hook=None):
    def body(dy_ref, a_ref, z_ref, dz_in, da_ref, dg_ref):
        zz = z_ref[...]
        sg = _sigmoid(zz)
        d = dy_ref[...]
        da_ref[...] = d * (zz * sg)
        dg_ref[...] = (d * a_ref[...] * (sg * (1.0 + zz * (1.0 - sg)))).astype(_MXU)

    blk = pl.BlockSpec((RT, CW), lambda i, j: (i, j))
    zblk = pl.BlockSpec((RT, CW), lambda i, j: (i, off // CW + j))
    return _pcall(body, name=name, grid=(TP // RT, D // CW), in_specs=[blk, blk, zblk, _ANY], out_specs=[blk, zblk],
                  out_shape=[_sds((TP, D), F32), _sds((TP, D_IN), _MXU)], operands=[dy, a, z, dz], vmem=32,
                  sem=("parallel", "parallel"), aliases={3: 1}, hook=hook)


def _merge_fwd(y2, z):
    def body(ya_ref, yb_ref, ga_ref, gb_ref, o_ref):
        o_ref[...] = (_sigmoid(ga_ref[...]) * ya_ref[...] + _sigmoid(gb_ref[...]) * yb_ref[...]).astype(_MXU)

    nb = D // CW
    blk = pl.BlockSpec((RT, CW), lambda i, j: (i, j))
    return _pcall(body, name="merge_fwd", grid=(TP // RT, nb),
                  in_specs=[blk, pl.BlockSpec((RT, CW), lambda i, j: (i, nb + j)),
                            pl.BlockSpec((RT, CW), lambda i, j: (i, OFF_G // CW + j)),
                            pl.BlockSpec((RT, CW), lambda i, j: (i, (OFF_G + D) // CW + j))],
                  out_specs=[blk], out_shape=[_sds((TP, D), _MXU)], operands=[y2, y2, z, z], vmem=32,
                  sem=("parallel", "parallel"))[0]


def _merge_bwd(dmix, y2, z):
    nb = D // CW

    def body(dm_ref, y_ref, g_ref, dy_ref, dg_ref):
        dm = dm_ref[...]
        sg = _sigmoid(g_ref[...])
        dy_ref[...] = (dm * sg).astype(_MXU)
        dg_ref[...] = (dm * y_ref[...] * sg * (1.0 - sg)).astype(_MXU)

    blk = pl.BlockSpec((RT, CW), lambda i, j: (i, j))
    gblk = pl.BlockSpec((RT, CW), lambda i, j: (i, OFF_G // CW + j))
    return _pcall(body, name="merge_bwd", grid=(TP // RT, 2 * nb),
                  in_specs=[pl.BlockSpec((RT, CW), lambda i, j: (i, j % nb)), blk, gblk], out_specs=[blk, gblk],
                  out_shape=[_sds((TP, 2 * D), _MXU), _sds((TP, D_IN), _MXU)], operands=[dmix, y2, z], vmem=32,
                  sem=("parallel", "parallel"))


def _ln_out_loss(h32, out, target, g, b):
    def body(h_ref, o_ref, t_ref, g_ref, b_ref, dr_ref, acc_ref):
        i = pl.program_id(0)

        @pl.when(i == 0)
        def _():
            acc_ref[...] = jnp.zeros_like(acc_ref)

        r = ALPHA * h_ref[...] + o_ref[...]
        mu = jnp.mean(r, axis=-1, keepdims=True)
        rc = r - mu
        var = jnp.mean(rc * rc, axis=-1, keepdims=True)
        rstd = lax.rsqrt(var + LN_EPS)
        xhat = rc * rstd
        gg = g_ref[...]
        y = xhat * gg + b_ref[...]
        real = (i >= 1).astype(F32)
        diff = (y - t_ref[...]) * real
        dy = diff * (1.0 / D)
        dxh = dy * gg
        m1 = jnp.mean(dxh, axis=-1, keepdims=True)
        m2 = jnp.mean(dxh * xhat, axis=-1, keepdims=True)
        dr = rstd * (dxh - m1 - xhat * m2)
        dr_ref[...] = dr
        acc_ref[0:1, :] += jnp.sum(dy * xhat, axis=0, keepdims=True)
        acc_ref[1:2, :] += jnp.sum(dy, axis=0, keepdims=True)
        acc_ref[2:3, :] += jnp.sum(dr, axis=0, keepdims=True)
        acc_ref[3:4, :] += (0.5 / D) * jnp.sum(diff * diff)

    row = pl.BlockSpec((BLK, D), lambda i: (i, 0))
    vec = pl.BlockSpec((1, D), lambda i: (0, 0))
    return _pcall(body, name="ln_out_loss", grid=(NBLK,),
                  in_specs=[row, row, pl.BlockSpec((BLK, D), lambda i: (jnp.maximum(i - 1, 0), 0)), vec, vec],
                  out_specs=[row, pl.BlockSpec((8, D), lambda i: (0, 0))],
                  out_shape=[_sds((TP, D), F32), _sds((8, D), F32)], operands=[h32, out, target, g, b], vmem=32,
                  sem=("arbitrary",))


def _colsum(x, *, name, tn, hook=None):
    _, n = x.shape

    def body(x_ref, o_ref):
        o_ref[...] = jnp.sum(x_ref[...].astype(F32), axis=0, keepdims=True)

    return _pcall(body, name=name, grid=(n // tn,), in_specs=[pl.BlockSpec((TP, tn), lambda j: (0, j))],
                  out_specs=[pl.BlockSpec((1, tn), lambda j: (0, j))], out_shape=[_sds((1, n), F32)], operands=[x],
                  vmem=32, sem=("parallel",), hook=hook)


def _rnn_recompute(xr_ref, cw_ref, cb_ref, wra_ref, wri_ref, bra_ref, bri_ref, lam_ref):
    rows = lax.broadcasted_iota(jnp.int32, (TP, 1), 0)
    valid = (rows >= PAD).astype(F32)
    first = rows == PAD
    x = xr_ref[...] * valid
    cw = cw_ref[...]
    shifted = [x, pltpu.roll(x, 1, 0), pltpu.roll(x, 2, 0), pltpu.roll(x, 3, 0)]
    c = cb_ref[...] + cw[0:1, :] * shifted[0] + cw[1:2, :] * shifted[1] + cw[2:3, :] * shifted[2] + cw[3:4, :] * shifted[3]
    cm = c.astype(_MXU)
    gr = _sigmoid(jnp.dot(cm, wra_ref[...].astype(_MXU), preferred_element_type=F32) + bra_ref[...])
    gi = _sigmoid(jnp.dot(cm, wri_ref[...].astype(_MXU), preferred_element_type=F32) + bri_ref[...])
    lam = lam_ref[...]
    ls = jnp.minimum(lam, 0.0) - jnp.log(1.0 + jnp.exp(-jnp.abs(lam)))
    log_a = LRU_C * gr * ls
    a = jnp.exp(log_a)
    mult = jnp.where(first, 1.0, jnp.sqrt(1.0 - jnp.exp(2.0 * log_a)))
    return dict(valid=valid, first=first, shifted=shifted, c=c, cm=cm, gr=gr, gi=gi, ls=ls, a=a, mult=mult, lam=lam)


def _rnn_specs():
    col = pl.BlockSpec((TP, RB), lambda n: (0, n))
    vec = pl.BlockSpec((1, RB), lambda n: (0, n))
    return dict(col=col, vec=vec, cw=pl.BlockSpec((CONV_W, RB), lambda n: (0, n)),
                wblk=pl.BlockSpec((None, RB, RB), lambda n: (n, 0, 0)))


def _rnn_gates_fwd(z, conv_w, conv_b, w_ra, w_ri, b_ra, b_ri, lam):
    def body(xr_ref, cw_ref, cb_ref, wra_ref, wri_ref, bra_ref, bri_ref, lam_ref, a_ref, u_ref):
        r = _rnn_recompute(xr_ref, cw_ref, cb_ref, wra_ref, wri_ref, bra_ref, bri_ref, lam_ref)
        a_ref[...] = r["a"]
        u_ref[...] = r["mult"] * r["gi"] * r["c"] * r["valid"]

    s = _rnn_specs()
    return _pcall(body, name="rnn_gates_fwd", grid=(N_RB,),
                  in_specs=[s["col"], s["cw"], s["vec"], s["wblk"], s["wblk"], s["vec"], s["vec"], s["vec"]],
                  out_specs=[s["col"], s["col"]], out_shape=[_sds((TP, D), F32)] * 2,
                  operands=[z, conv_w, conv_b, w_ra, w_ri, b_ra, b_ri, lam], vmem=56, sem=("parallel",))


SCAN_ROWS = 272


def _scan_fwd(a, u):
    def body(a_ref, u_ref, h_ref, carry_ref):
        @pl.when(pl.program_id(0) == 0)
        def _():
            carry_ref[...] = jnp.zeros_like(carry_ref)

        def step(t, h):
            h = a_ref[t] * h + u_ref[t]
            h_ref[t] = h
            return h

        carry_ref[...] = lax.fori_loop(0, SCAN_ROWS, step, carry_ref[...], unroll=8)

    blk = pl.BlockSpec((SCAN_ROWS, 16, 128), lambda i: (i, 0, 0))
    h = _pcall(body, name="scan_fwd", grid=(TP // SCAN_ROWS,), in_specs=[blk, blk], out_specs=[blk],
               out_shape=[_sds((TP, 16, 128), F32)], operands=[a.reshape(TP, 16, 128), u.reshape(TP, 16, 128)],
               scratch=[pltpu.VMEM((16, 128), F32)], vmem=32, sem=("arbitrary",))[0]
    return h.reshape(TP, D)


def _scan_bwd(a, dh):
    nst = TP // SCAN_ROWS

    def body(a_ref, d_ref, o_ref, lam_ref, anext_ref):
        @pl.when(pl.program_id(0) == 0)
        def _():
            lam_ref[...] = jnp.zeros_like(lam_ref)
            anext_ref[...] = jnp.zeros_like(anext_ref)

        def step(q, carry):
            lam, an = carry
            t = SCAN_ROWS - 1 - q
            lam = d_ref[t] + an * lam
            o_ref[t] = lam
            return lam, a_ref[t]

        lam, an = lax.fori_loop(0, SCAN_ROWS, step, (lam_ref[...], anext_ref[...]), unroll=8)
        lam_ref[...] = lam
        anext_ref[...] = an

    blk = pl.BlockSpec((SCAN_ROWS, 16, 128), lambda i: (nst - 1 - i, 0, 0))
    o = _pcall(body, name="scan_bwd", grid=(nst,), in_specs=[blk, blk], out_specs=[blk],
               out_shape=[_sds((TP, 16, 128), F32)], operands=[a.reshape(TP, 16, 128), dh.reshape(TP, 16, 128)],
               scratch=[pltpu.VMEM((16, 128), F32), pltpu.VMEM((16, 128), F32)], vmem=32, sem=("arbitrary",))[0]
    return o.reshape(TP, D)


def _rnn_gates_bwd(z, lam_s, hr, conv_w, conv_b, w_ra, w_ri, b_ra, b_ri, lam, dz, gsq):
    def body(xr_ref, ls_ref, hr_ref, cw_ref, cb_ref, wra_ref, wri_ref, bra_ref, bri_ref, lam_ref, dz_in, gsq_in,
             dx_ref, dw_ref, sums_ref):
        r = _rnn_recompute(xr_ref, cw_ref, cb_ref, wra_ref, wri_ref, bra_ref, bri_ref, lam_ref)
        valid, c, gr, gi, a, mult = r["valid"], r["c"], r["gr"], r["gi"], r["a"], r["mult"]
        du = ls_ref[...] * valid
        da = du * pltpu.roll(hr_ref[...], 1, 0)
        d_gi = du * mult * c
        dc = du * mult * gi
        dmult = du * gi * c
        dlog_a = da * a + jnp.where(r["first"], 0.0, -dmult * a * a / mult)
        d_gr = dlog_a * (LRU_C * r["ls"])
        dls = jnp.sum(dlog_a * (LRU_C * gr), axis=0, keepdims=True)
        dpre_r = d_gr * gr * (1.0 - gr)
        dpre_i = d_gi * gi * (1.0 - gi)
        pr = dpre_r.astype(_MXU)
        pi = dpre_i.astype(_MXU)
        dwra = lax.dot_general(r["cm"], pr, _TN, preferred_element_type=F32)
        dwri = lax.dot_general(r["cm"], pi, _TN, preferred_element_type=F32)
        for s in range(N_SH):
            dw_ref[s, 0:64, :] = dwra[64 * s:64 * (s + 1)]
            dw_ref[s, 64:128, :] = dwri[64 * s:64 * (s + 1)]
        dc = dc + lax.dot_general(pr, wra_ref[...].astype(_MXU), _NT, preferred_element_type=F32)
        dc = dc + lax.dot_general(pi, wri_ref[...].astype(_MXU), _NT, preferred_element_type=F32)
        cw = cw_ref[...]
        dx = cw[0:1, :] * dc
        for k in range(1, CONV_W):
            dx = dx + cw[k:k + 1, :] * pltpu.roll(dc, TP - k, 0)
        dx_ref[...] = (dx * valid).astype(_MXU)
        for k in range(CONV_W):
            sums_ref[k:k + 1, :] = jnp.sum(dc * r["shifted"][k], axis=0, keepdims=True)
        sums_ref[4:5, :] = jnp.sum(dc, axis=0, keepdims=True)
        sums_ref[5:6, :] = jnp.sum(dpre_r, axis=0, keepdims=True)
        sums_ref[6:7, :] = jnp.sum(dpre_i, axis=0, keepdims=True)
        sums_ref[7:8, :] = dls * _sigmoid(-r["lam"])

    s = _rnn_specs()
    return _pcall(
        body, name="rnn_gates_bwd", grid=(N_RB,),
        in_specs=[s["col"], s["col"], s["col"], s["cw"], s["vec"], s["wblk"], s["wblk"], s["vec"], s["vec"], s["vec"],
                  _ANY, _ANY],
        out_specs=[s["col"], pl.BlockSpec((N_SH, 128, RB), lambda n: (0, 3 * SQ_ROWS // 128, n)),
                   pl.BlockSpec((8, RB), lambda n: (0, n))],
        out_shape=[_sds((TP, D_IN), _MXU), _sds((N_SH, PACK_ROWS, D), F32), _sds((8, D), F32)],
        operands=[z, lam_s, hr, conv_w, conv_b, w_ra, w_ri, b_ra, b_ri, lam, dz, gsq], vmem=60, sem=("parallel",),
        aliases={10: 0, 11: 1})


def _rope_tables():
    half = HD // 2
    inv = ROPE_THETA ** (-jnp.arange(half, dtype=F32) / half)
    pos = (jnp.arange(TP) - PAD).astype(F32)
    ang = pos[:, None] * inv[None, :]
    return jnp.tile(jnp.cos(ang), (1, 4)), jnp.tile(jnp.sin(ang), (1, 4))


def _rope(x, cos_t, sin_t, sign):
    w = x.shape[1]
    lane = lax.broadcasted_iota(jnp.int32, x.shape, 1)
    first = (lane % HD) < (HD // 2)
    swapped = jnp.where(first, pltpu.roll(x, w - HD // 2, 1), pltpu.roll(x, HD // 2, 1))
    ct = jnp.tile(cos_t, (1, w // 128))
    st = jnp.tile(sin_t, (1, w // 128))
    return x * ct + swapped * jnp.where(first, -sign * st, sign * st)


def _rope_fwd(z, cos_t, sin_t):
    def body(q_ref, k_ref, v_ref, c_ref, s_ref, qo_ref, ko_ref, vo_ref):
        c = c_ref[...]
        s = s_ref[...]
        qo_ref[...] = _rope(q_ref[...], c, s, 1.0).astype(_MXU)
        ko_ref[...] = _rope(k_ref[...], c, s, 1.0).astype(_MXU)
        vo_ref[...] = v_ref[...].astype(_MXU)

    tab = pl.BlockSpec((BLK, 128), lambda i: (i, 0))
    kv = pl.BlockSpec((BLK, D_KV), lambda i: (i, 0))
    return _pcall(body, name="rope_fwd", grid=(NBLK,),
                  in_specs=[pl.BlockSpec((BLK, D), lambda i: (i, OFF_Q // D)),
                            pl.BlockSpec((BLK, D_KV), lambda i: (i, OFF_K // D_KV)),
                            pl.BlockSpec((BLK, D_KV), lambda i: (i, OFF_V // D_KV)), tab, tab],
                  out_specs=[pl.BlockSpec((BLK, D), lambda i: (i, 0)), kv, kv],
                  out_shape=[_sds((TP, D), _MXU), _sds((TP, D_KV), _MXU), _sds((TP, D_KV), _MXU)],
                  operands=[z, z, z, cos_t, sin_t], vmem=32, sem=("parallel",))


def _rope_bwd_k(dk, dv, cos_t, sin_t, dz):
    def body(dk_ref, dv_ref, c_ref, s_ref, dz_in, o_ref):
        o_ref[:, 0:D_KV] = _rope(dk_ref[...], c_ref[...], s_ref[...], -1.0).astype(_MXU)
        o_ref[:, D_KV:2 * D_KV] = dv_ref[...].astype(_MXU)

    tab = pl.BlockSpec((BLK, 128), lambda i: (i, 0))
    kv = pl.BlockSpec((BLK, D_KV), lambda i: (i, 0))
    return _pcall(body, name="rope_bwd_k", grid=(NBLK,), in_specs=[kv, kv, tab, tab, _ANY],
                  out_specs=[pl.BlockSpec((BLK, 2 * D_KV), lambda i: (i, OFF_K // (2 * D_KV)))],
                  out_shape=[_sds((TP, D_IN), _MXU)], operands=[dk, dv, cos_t, sin_t, dz], vmem=32, sem=("parallel",),
                  aliases={4: 0})[0]


def _attn_masks(i):
    ql = lax.broadcasted_iota(jnp.int32, (GRP * BLK, BLK), 0) % BLK
    kl = lax.broadcasted_iota(jnp.int32, (GRP * BLK, BLK), 1)
    meta = (kl >= PAD) & ((i >= 1) | (kl <= ql))
    prev = (i >= 2) & (kl > ql)
    cur = (i >= 1) & (kl <= ql)
    return meta, prev, cur


def _kv_specs():
    return [pl.BlockSpec((BLK, D_KV), lambda i: (0, 0)),
            pl.BlockSpec((BLK, D_KV), lambda i: (jnp.maximum(i - 1, 0), 0)),
            pl.BlockSpec((BLK, D_KV), lambda i: (i, 0))]


def _stack_heads(ref, g, width=HD):
    return jnp.concatenate([ref[:, width * (GRP * g + j):width * (GRP * g + j + 1)] for j in range(GRP)], axis=0)


def _sink_column(sink_ref, g):
    return jnp.concatenate([jnp.full((BLK, 1), sink_ref[GRP * g + j], F32) for j in range(GRP)], axis=0)


def _attn_fwd(q, k, v, sinks):
    def body(q_ref, k0_ref, kp_ref, kc_ref, v0_ref, vp_ref, vc_ref, sink_ref, o_ref, lse_ref):
        masks = _attn_masks(pl.program_id(0))
        k_refs = (k0_ref, kp_ref, kc_ref)
        v_refs = (v0_ref, vp_ref, vc_ref)
        for g in range(N_KV):
            gs = slice(HD * g, HD * (g + 1))
            q8 = _stack_heads(q_ref, g)
            sink = _sink_column(sink_ref, g)
            s = [jnp.where(m, lax.dot_general(q8, r[:, gs], _NT, preferred_element_type=F32) * (HD ** -0.5), NEG_INF)
                 for m, r in zip(masks, k_refs)]
            mx = jnp.maximum(jnp.maximum(jnp.max(s[0], -1, keepdims=True), jnp.max(s[1], -1, keepdims=True)),
                             jnp.maximum(jnp.max(s[2], -1, keepdims=True), sink))
            p = [jnp.exp(t - mx) for t in s]
            den = (jnp.sum(p[0], -1, keepdims=True) + jnp.sum(p[1], -1, keepdims=True)
                   + jnp.sum(p[2], -1, keepdims=True) + jnp.exp(sink - mx))
            inv = 1.0 / den
            o8 = jnp.dot((p[0] * inv).astype(_MXU), v_refs[0][:, gs], preferred_element_type=F32)
            o8 = o8 + jnp.dot((p[1] * inv).astype(_MXU), v_refs[1][:, gs], preferred_element_type=F32)
            o8 = o8 + jnp.dot((p[2] * inv).astype(_MXU), v_refs[2][:, gs], preferred_element_type=F32)
            lse8 = mx + jnp.log(den)
            for j in range(GRP):
                h = GRP * g + j
                o_ref[:, HD * h:HD * (h + 1)] = o8[BLK * j:BLK * (j + 1)]
                lse_ref[:, h:h + 1] = lse8[BLK * j:BLK * (j + 1)]

    return _pcall(body, name="attn_fwd", grid=(NBLK,),
                  in_specs=[pl.BlockSpec((BLK, D), lambda i: (i, 0))] + _kv_specs() + _kv_specs()
                           + [pl.BlockSpec(memory_space=pltpu.SMEM)],
                  out_specs=[pl.BlockSpec((BLK, D), lambda i: (i, 0)), pl.BlockSpec((BLK, N_Q), lambda i: (i, 0))],
                  out_shape=[_sds((TP, D), F32), _sds((TP, N_Q), F32)], operands=[q, k, k, k, v, v, v, sinks],
                  vmem=40, sem=("parallel",))


def _attn_bwd(q, k, v, sinks, do, o, lse, cos_t, sin_t, dz, *, hook):
    def body(q_ref, k0_ref, kp_ref, kc_ref, v0_ref, vp_ref, vc_ref, sink_ref, do_ref, o_ref, lse_ref, c_ref, s_ref, dz_in,
             dq_ref, dk_ref, dv_ref, dsink_ref, dqrot_ref):
        i = pl.program_id(0)

        @pl.when(i == 0)
        def _():
            dk_ref[...] = jnp.zeros_like(dk_ref)
            dv_ref[...] = jnp.zeros_like(dv_ref)
            dsink_ref[...] = jnp.zeros_like(dsink_ref)

        masks = _attn_masks(i)
        k_refs = (k0_ref, kp_ref, kc_ref)
        v_refs = (v0_ref, vp_ref, vc_ref)
        row_starts = (0, pl.multiple_of(jnp.maximum(i - 1, 0) * BLK, BLK), pl.multiple_of(i * BLK, BLK))
        scale = HD ** -0.5
        for g in range(N_KV):
            gs = slice(HD * g, HD * (g + 1))
            q8 = _stack_heads(q_ref, g)
            do8 = _stack_heads(do_ref, g)
            dom = do8.astype(_MXU)
            delta = jnp.sum(do8 * _stack_heads(o_ref, g), axis=-1, keepdims=True)
            lse8 = _stack_heads(lse_ref, g, width=1)
            dsk = -jnp.exp(_sink_column(sink_ref, g) - lse8) * delta
            for j in range(GRP):
                h = GRP * g + j
                dsink_ref[0:1, h:h + 1] += jnp.sum(dsk[BLK * j:BLK * (j + 1)], axis=0, keepdims=True)
            dq8 = jnp.zeros((GRP * BLK, HD), F32)
            for part in range(3):
                kk = k_refs[part][:, gs]
                s = jnp.where(masks[part], lax.dot_general(q8, kk, _NT, preferred_element_type=F32) * scale, NEG_INF)
                p = jnp.exp(s - lse8)
                dp = lax.dot_general(dom, v_refs[part][:, gs], _NT, preferred_element_type=F32)
                ds = (p * (dp - delta) * scale).astype(_MXU)
                dq8 = dq8 + jnp.dot(ds, kk, preferred_element_type=F32)
                rows = pl.ds(row_starts[part], BLK)
                dk_ref[rows, gs] += lax.dot_general(ds, q8, _TN, preferred_element_type=F32)
                dv_ref[rows, gs] += lax.dot_general(p.astype(_MXU), dom, _TN, preferred_element_type=F32)
            for j in range(GRP):
                h = GRP * g + j
                dqrot_ref[:, HD * h:HD * (h + 1)] = dq8[BLK * j:BLK * (j + 1)]
        dq_ref[...] = _rope(dqrot_ref[...], c_ref[...], s_ref[...], -1.0).astype(_MXU)

    row = pl.BlockSpec((BLK, D), lambda i: (i, 0))
    tab = pl.BlockSpec((BLK, 128), lambda i: (i, 0))
    full_kv = pl.BlockSpec((TP, D_KV), lambda i: (0, 0))
    return _pcall(
        body, name="attn_bwd", grid=(NBLK,),
        in_specs=[row] + _kv_specs() + _kv_specs() + [pl.BlockSpec(memory_space=pltpu.SMEM), row, row,
                  pl.BlockSpec((BLK, N_Q), lambda i: (i, 0)), tab, tab, _ANY],
        out_specs=[pl.BlockSpec((BLK, D), lambda i: (i, OFF_Q // D)), full_kv, full_kv,
                   pl.BlockSpec((8, 128), lambda i: (0, 0))],
        out_shape=[_sds((TP, D_IN), _MXU), _sds((TP, D_KV), F32), _sds((TP, D_KV), F32), _sds((8, 128), F32)],
        operands=[q, k, k, k, v, v, v, sinks, do, o, lse, cos_t, sin_t, dz], scratch=[pltpu.VMEM((BLK, D), F32)],
        vmem=48, sem=("arbitrary",), aliases={13: 0}, hook=hook)


def _cast_into_slot(w32, pos, *, name, tr):
    r, cc = w32.shape

    def body(pos_ref, w_ref, o_ref):
        o_ref[...] = w_ref[...].astype(BF16)

    return _pcall(body, name=name, grid=(r // tr,), in_specs=[pl.BlockSpec((tr, cc), lambda i, p: (i, 0))],
                  out_specs=[pl.BlockSpec((None, tr, cc), lambda i, p: (p[1], i, 0))],
                  out_shape=[_sds((N_SH, r, cc), BF16)], operands=[w32], vmem=32, sem=("parallel",), prefetch=(pos,))[0]


def _pair_add(g, got, pos, *, name, tr, g_has_both_halves):
    n, h, cc = got.shape
    nt = h // tr

    def body(pos_ref, g_ref, r_ref, own_ref, s16_ref):
        s = g_ref[...] + r_ref[...]
        s16_ref[...] = s.astype(BF16)

        @pl.when(pl.program_id(1) == pos_ref[1])
        def _():
            own_ref[...] = s

    g_index = (lambda i, s, p: (s, p[0] * nt + i, 0)) if g_has_both_halves else (lambda i, s, p: (s, i, 0))
    return _pcall(body, name=name, grid=(nt, n),
                  in_specs=[pl.BlockSpec((None, tr, cc), g_index), pl.BlockSpec((None, tr, cc), lambda i, s, p: (s, i, 0))],
                  out_specs=[pl.BlockSpec((tr, cc), lambda i, s, p: (i, 0)),
                             pl.BlockSpec((None, tr, cc), lambda i, s, p: (s, i, 0))],
                  out_shape=[_sds((h, cc), F32), _sds((n, h, cc), BF16)], operands=[g, got], vmem=40,
                  sem=("parallel", "arbitrary"), prefetch=(pos,))


def _sum_chips(own, got, pos, *, name, tr):
    h, cc = own.shape
    nt = h // tr

    def body(pos_ref, o_ref, r_ref, out_ref):
        acc = o_ref[...]
        for k in range(3):
            acc = acc + r_ref[k].astype(F32)
        out_ref[...] = acc

    return _pcall(body, name=name, grid=(nt,),
                  in_specs=[pl.BlockSpec((tr, cc), lambda i, p: (i, 0)), pl.BlockSpec((3, tr, cc), lambda i, p: (0, i, 0))],
                  out_specs=[pl.BlockSpec((tr, cc), lambda i, p: (p[0] * nt + i, 0))],
                  out_shape=[_sds((2 * h, cc), F32)], operands=[own, got], vmem=40, sem=("parallel",), prefetch=(pos,))[0]


def _adamw(w, g, m, v, *, name, tr):
    r, cc = w.shape

    def body(w_ref, g_ref, m_ref, v_ref, d_ref, mo_ref, vo_ref):
        gg = g_ref[...]
        m_new = ADAM_B1 * m_ref[...] + (1.0 - ADAM_B1) * gg
        v_new = ADAM_B2 * v_ref[...] + (1.0 - ADAM_B2) * (gg * gg)
        m_hat = m_new / (1.0 - ADAM_B1 ** ADAM_STEP)
        v_hat = v_new / (1.0 - ADAM_B2 ** ADAM_STEP)
        d_ref[...] = -ADAM_LR * (m_hat / (jnp.sqrt(v_hat) + ADAM_EPS) + ADAM_WD * w_ref[...])
        mo_ref[...] = m_new
        vo_ref[...] = v_new

    blk = pl.BlockSpec((tr, cc), lambda i: (i, 0))
    return _pcall(body, name=name, grid=(r // tr,), in_specs=[blk] * 4, out_specs=[blk] * 3,
                  out_shape=[_sds((r, cc), F32)] * 3, operands=[w, g, m, v], vmem=48, sem=("parallel",))


_SMALL_ROWS = 40
_B_IN_ROWS = 7


def _row_pad(v, rows):
    flat = v.reshape(-1)
    return jnp.pad(flat, (0, rows * D - flat.shape[0])).reshape(rows, D)


def _pack_ra(w):
    return w.transpose(1, 0, 2).reshape(64, D)


def _unpack_ra(p, like):
    return p.reshape(64, N_RB, RB).transpose(1, 0, 2).reshape(like.shape)


def _gate_full(g4):
    return g4.reshape(N_SH, 64, N_RB, RB).transpose(2, 0, 1, 3).reshape(N_RB, RB, RB)


def kernel(x, meta_tokens, ln_emb_g, ln_emb_b, w_in, b_in, conv_w, conv_b, w_ra, b_ra, w_ri, b_ri, lru_lambda, sinks, w_rnn_out, w_attn_out, w_o, b_o, ln_g, ln_b, loss_target, m_meta_tokens, m_ln_emb_g, m_ln_emb_b, m_w_in, m_b_in, m_conv_w, m_conv_b, m_w_ra, m_b_ra, m_w_ri, m_b_ri, m_lru_lambda, m_sinks, m_w_rnn_out, m_w_attn_out, m_w_o, m_b_o, m_ln_g, m_ln_b, v_meta_tokens, v_ln_emb_g, v_ln_emb_b, v_w_in, v_b_in, v_conv_w, v_conv_b, v_w_ra, v_b_ra, v_w_ri, v_b_ri, v_lru_lambda, v_sinks, v_w_rnn_out, v_w_attn_out, v_w_o, v_b_o, v_ln_g, v_ln_b):
    xi, yi, ci = _my_pos()
    shard = 2 * xi + yi
    pos = jnp.stack([ci, shard, 1 - ci]).astype(jnp.int32)
    cos_t, sin_t = _rope_tables()
    zero_bias = jnp.zeros((1, D), F32)
    ln_emb_g2, ln_emb_b2 = ln_emb_g[None], ln_emb_b[None]

    small = jnp.concatenate([conv_w[0], meta_tokens, jnp.zeros((4, 512), F32)], axis=0)
    small4 = _gather_small(small)
    conv_w_full = small4[:, 0:4].transpose(1, 0, 2).reshape(CONV_W, D)
    meta_full = small4[:, 4:20].transpose(1, 0, 2).reshape(N_META, D)
    w_own = _cast_into_slot(w_in[0], pos, name="cast_w_in", tr=256)
    wpack = jnp.concatenate([w_rnn_out[0], w_attn_out[0], w_o[0], _pack_ra(w_ra[0]), _pack_ra(w_ri[0])], axis=0)
    wp_own = _cast_into_slot(wpack, pos, name="cast_w_sq", tr=208)

    h32, h16 = _ln_emb_fwd(x[0], meta_full, ln_emb_g2, ln_emb_b2)
    order = jnp.stack([shard, shard ^ 2, shard ^ 1, shard ^ 3]).astype(jnp.int32)
    z, w_in4, wp4 = _mm_z_gather(h16, w_own, b_in, order, wp_own)
    w_ra_full = _gate_full(wp4[:, 3 * SQ_ROWS:3 * SQ_ROWS + 64])
    w_ri_full = _gate_full(wp4[:, 3 * SQ_ROWS + 64:PACK_ROWS])

    def sq_nn(a, kk, bias, name, out_cols, out_index, carried=None):
        return _mm_nn(a, wp4, bias, name=name, grid=(2, D // CW), tm=HALF_TP, tn=CW, k=D, a_index=lambda i, j: (i, 0),
                      w_block=(N_SH, SQ_ROWS, CW), w_index=lambda i, j: (0, kk, j), out_cols=out_cols,
                      out_index=out_index, carried=carried)[0]

    def sq_nt(a, a_blk, kk, name, hook=None):
        return _mm_nt(a, wp4, name=name, grid=(2, N_SH, 1), tm=HALF_TP, tn=SQ_ROWS, tk=D,
                      a_index=lambda i, j, q: (i, a_blk), w_block=(None, SQ_ROWS, D),
                      w_index=lambda i, j, q: (j, kk, 0), out_cols=D, hook=hook)

    a_dec, u_in = _rnn_gates_fwd(z, conv_w_full, conv_b, w_ra_full, w_ri_full, b_ra, b_ri, lru_lambda)
    hr = _scan_fwd(a_dec, u_in)
    ya_in = _mul_silu_fwd(hr, z, OFF_GR, name="gate_a_fwd")
    y2 = sq_nn(ya_in, 0, zero_bias, "mm_ya", 2 * D, lambda i, j: (i, j))
    q, k, v = _rope_fwd(z, cos_t, sin_t)
    o, lse = _attn_fwd(q, k, v, sinks[0])
    yb_in = _mul_silu_fwd(o, z, OFF_GA, name="gate_b_fwd")
    y2 = sq_nn(yb_in, 1, zero_bias, "mm_yb", 2 * D, lambda i, j: (i, D // CW + j), carried=y2)
    mixed = _merge_fwd(y2, z)
    out = sq_nn(mixed, 2, b_o, "mm_out", D, lambda i, j: (i, j))
    dr, sums_o = _ln_out_loss(h32, out, loss_target[0], ln_g, ln_b)

    def sq_tn(a, b, b_blk0, kk, name, carried=None):
        return _mm_tn(a, b, name=name, grid=(N_SH, D // CW), tm=SQ_ROWS, tn=CW, a_index=lambda i, j: (0, i),
                      b_index=lambda i, j: (0, b_blk0 + j), out_shape=(N_SH, PACK_ROWS, D),
                      out_block=(None, SQ_ROWS, CW), out_index=lambda i, j: (i, kk, j), carried=carried)[0]

    gsq = sq_tn(mixed, dr, 0, 2, "mm_dwo")
    dmix = sq_nt(dr, 0, 2, "mm_dmix")[0]
    dy2, dz = _merge_bwd(dmix, y2, z)
    gsq = sq_tn(ya_in, dy2, 0, 0, "mm_dwrnn", carried=gsq)
    gsq = sq_tn(yb_in, dy2, D // CW, 1, "mm_dwattn", carried=gsq)
    dya_in = sq_nt(dy2, 0, 0, "mm_dyain")[0]
    dhr, dz = _mul_silu_bwd(dya_in, hr, z, OFF_GR, dz, name="gate_a_bwd")
    lam_s = _scan_bwd(a_dec, dhr)
    dz, gsq, sums_r = _rnn_gates_bwd(z, lam_s, hr, conv_w_full, conv_b, w_ra_full, w_ri_full, b_ra, b_ri, lru_lambda, dz, gsq)
    dyb_in = sq_nt(dy2, 1, 1, "mm_dybin")[0]
    do, dz, gsq, got_sq = _mul_silu_bwd(dyb_in, o, z, OFF_GA, dz, name="gate_b_bwd", hook=_hook_pair(gsq, True))
    own_sq, s16_sq = _pair_add(gsq, got_sq, pos, name="red_w_sq_add", tr=208, g_has_both_halves=True)
    dz, dk_rot, dv32, dsink, s16_sq, oth_sq = _attn_bwd(q, k, v, sinks[0], do, o, lse, cos_t, sin_t, dz,
                                                        hook=_hook_scatter(s16_sq))
    dz = _rope_bwd_k(dk_rot, dv32, cos_t, sin_t, dz)
    red_sq = _sum_chips(own_sq, oth_sq, pos, name="red_w_sq_sum", tr=208)
    g_b_in, g_sq = _colsum(dz, name="colsum_dz", tn=TN_IN, hook=_hook_halves(red_sq))

    def dwin(half_idx, name, hook=None):
        return _mm_tn(h16, dz, name=name, grid=(1, N_SH * PER_IN), tm=D // 2, tn=TN_IN,
                      a_index=lambda i, j, p: (0, p[half_idx]), b_index=lambda i, j, p: (0, j),
                      out_shape=(N_SH, D // 2, W_IN_COLS), out_block=(None, D // 2, TN_IN),
                      out_index=lambda i, j, p: (j // PER_IN, 0, j % PER_IN), prefetch=(pos,), hook=hook)

    gin_sib = dwin(2, "mm_dwin_sib")[0]
    gin_own, gin_sib, got_in = dwin(0, "mm_dwin_own", hook=_hook_pair(gin_sib, False))
    own_in, s16_in = _pair_add(gin_own, got_in, pos, name="red_w_in_add", tr=128, g_has_both_halves=False)
    dhz, s16_in, oth_in = _mm_nt(dz, w_in4, name="mm_dhz", grid=(2, 2, N_SH * PER_IN), tm=HALF_TP, tn=D // 2, tk=TN_IN,
                                 a_index=lambda i, j, q: (i, q), w_block=(None, D // 2, TN_IN),
                                 w_index=lambda i, j, q: (q // PER_IN, j, q % PER_IN), out_cols=D,
                                 hook=_hook_scatter(s16_in))
    red_in = _sum_chips(own_in, oth_in, pos, name="red_w_in_sum", tr=128)
    g_x, g_meta_local, sums_e, g_in = _ln_emb_bwd(x[0], meta_full, ln_emb_g2, dr, dhz, hook=_hook_halves(red_in))

    d_in, m_in, v_in = _adamw(w_in[0], g_in, m_w_in[0], v_w_in[0], name="adamw_w_in", tr=128)
    mpack = jnp.concatenate([m_w_rnn_out[0], m_w_attn_out[0], m_w_o[0], _pack_ra(m_w_ra[0]), _pack_ra(m_w_ri[0])], axis=0)
    vpack = jnp.concatenate([v_w_rnn_out[0], v_w_attn_out[0], v_w_o[0], _pack_ra(v_w_ra[0]), _pack_ra(v_w_ri[0])], axis=0)
    d_sq, m_sq, v_sq = _adamw(wpack, g_sq, mpack, vpack, name="adamw_w_sq", tr=208)

    spack = jnp.concatenate([
        sums_e[0:1], sums_e[1:2], _row_pad(g_b_in, _B_IN_ROWS), sums_r[0:4], sums_r[4:5], sums_r[5:6], sums_r[6:7],
        sums_r[7:8], _row_pad(dsink[0:1, 0:N_Q], 1), sums_o[2:3], sums_o[0:1], sums_o[1:2], g_meta_local, sums_o[3:4],
        jnp.zeros((_SMALL_ROWS - 38, D), F32)], axis=0)
    sred = _allreduce_small(spack)
    loss = sred[37, 0]
    col0 = shard * 512
    g_conv_w = lax.dynamic_slice(sred[9:13], (0, col0), (CONV_W, 512))
    g_meta = lax.dynamic_slice(sred[21:37], (0, col0), (N_META, 512))
    small_g = {"ln_emb_g": sred[0:1], "ln_emb_b": sred[1:2], "b_in": sred[2:9], "conv_w": g_conv_w.reshape(1, D),
               "conv_b": sred[13:14], "b_ra": sred[14:15], "b_ri": sred[15:16], "lru_lambda": sred[16:17],
               "sinks": sred[17:18], "b_o": sred[18:19], "ln_g": sred[19:20], "ln_b": sred[20:21],
               "meta_tokens": g_meta.reshape(4, D)}
    small_names = list(small_g)

    def small_pack(vals):
        rows = []
        for n in small_names:
            a = vals[n]
            if n == "b_in":
                a = _row_pad(a, _B_IN_ROWS)
            elif n == "sinks":
                a = _row_pad(a, 1)
            else:
                a = a.reshape(-1, D)
            rows.append(a)
        return jnp.concatenate(rows + [jnp.zeros((24 - 22, D), F32)], axis=0)

    w_small = dict(ln_emb_g=ln_emb_g, ln_emb_b=ln_emb_b, b_in=b_in, conv_w=conv_w, conv_b=conv_b, b_ra=b_ra, b_ri=b_ri,
                   lru_lambda=lru_lambda, sinks=sinks, b_o=b_o, ln_g=ln_g, ln_b=ln_b, meta_tokens=meta_tokens)
    m_small = dict(ln_emb_g=m_ln_emb_g, ln_emb_b=m_ln_emb_b, b_in=m_b_in, conv_w=m_conv_w, conv_b=m_conv_b, b_ra=m_b_ra,
                   b_ri=m_b_ri, lru_lambda=m_lru_lambda, sinks=m_sinks, b_o=m_b_o, ln_g=m_ln_g, ln_b=m_ln_b,
                   meta_tokens=m_meta_tokens)
    v_small = dict(ln_emb_g=v_ln_emb_g, ln_emb_b=v_ln_emb_b, b_in=v_b_in, conv_w=v_conv_w, conv_b=v_conv_b, b_ra=v_b_ra,
                   b_ri=v_b_ri, lru_lambda=v_lru_lambda, sinks=v_sinks, b_o=v_b_o, ln_g=v_ln_g, ln_b=v_ln_b,
                   meta_tokens=v_meta_tokens)
    g_small_pack = jnp.concatenate([small_g[n] for n in small_names] + [jnp.zeros((2, D), F32)], axis=0)
    d_sm, m_sm, v_sm = _adamw(small_pack(w_small), g_small_pack, small_pack(m_small), small_pack(v_small),
                              name="adamw_small", tr=24)

    small_rows = {}
    r0 = 0
    for n in small_names:
        nrows = small_g[n].shape[0]
        small_rows[n] = (r0, nrows)
        r0 += nrows

    def small_out(packed, n, like):
        a, nrows = small_rows[n]
        flat = packed[a:a + nrows].reshape(-1)
        return flat[:like.size].reshape(like.shape)

    def sq_out(packed, kk, like):
        if kk < 3:
            return packed[SQ_ROWS * kk:SQ_ROWS * (kk + 1)].reshape(like.shape)
        return _unpack_ra(packed[3 * SQ_ROWS + 64 * (kk - 3):3 * SQ_ROWS + 64 * (kk - 2)], like)

    sq_idx = {"w_rnn_out": 0, "w_attn_out": 1, "w_o": 2, "w_ra": 3, "w_ri": 4}
    weights = dict(meta_tokens=meta_tokens, ln_emb_g=ln_emb_g, ln_emb_b=ln_emb_b, w_in=w_in, b_in=b_in, conv_w=conv_w,
                   conv_b=conv_b, w_ra=w_ra, b_ra=b_ra, w_ri=w_ri, b_ri=b_ri, lru_lambda=lru_lambda, sinks=sinks,
                   w_rnn_out=w_rnn_out, w_attn_out=w_attn_out, w_o=w_o, b_o=b_o, ln_g=ln_g, ln_b=ln_b)

    def outputs(big_in, big_sq, small):
        res = []
        for n, like in weights.items():
            if n == "w_in":
                res.append(big_in.reshape(like.shape))
            elif n in sq_idx:
                res.append(sq_out(big_sq, sq_idx[n], like))
            else:
                res.append(small_out(small, n, like))
        return res

    return (loss, g_x[None], *outputs(g_in, g_sq, g_small_pack), *outputs(d_in, d_sq, d_sm),
            *outputs(m_in, m_sq, m_sm), *outputs(v_in, v_sq, v_sm))
```

```python
import jax
import jax.numpy as jnp
from jax import lax
from jax.experimental import pallas as pl
from jax.experimental.pallas import tpu as pltpu

F32 = jnp.float32
BF16 = jnp.bfloat16
_MXU = jnp.bfloat16

D = 2048
SEQ = 2048
N_META = 16
BLK = 128
PAD = BLK - N_META
TP = PAD + N_META + SEQ
NBLK = TP // BLK
HALF_TP = TP // 2
N_RB = 8
RB = 256
CONV_W = 4
LRU_C = 8.0
HD = 64
N_Q = 32
N_KV = 4
GRP = 8
D_KV = 256
NEG_INF = -1e30
LN_EPS = 1e-5
ALPHA = 2.0 ** 0.25
ROPE_THETA = 10000.0
OFF_GR, OFF_Q, OFF_K, OFF_V, OFF_GA, OFF_G = 2048, 4096, 6144, 6400, 6656, 8704
D_IN = 12800
N_SH = 4
W_IN_COLS = D_IN // N_SH
TN_IN = 640
PER_IN = W_IN_COLS // TN_IN
CW = 512
RT = TP // 4
SQ_ROWS = 512
PACK_ROWS = 3 * SQ_ROWS + 128

ADAM_LR = 0.001
ADAM_B1 = 0.9
ADAM_B2 = 0.999
ADAM_EPS = 1e-08
ADAM_WD = 0.01
ADAM_STEP = 10

MESH = pl.DeviceIdType.MESH
_MIB = 1024 * 1024
_ANY = pl.BlockSpec(memory_space=pl.ANY)
_NT = (((1,), (1,)), ((), ()))
_TN = (((0,), (0,)), ((), ()))


def _sds(shape, dtype):
    return jax.ShapeDtypeStruct(shape, dtype)


def _sigmoid(x):
    return 1.0 / (1.0 + jnp.exp(-x))


def _my_pos():
    return lax.axis_index("x"), lax.axis_index("y"), lax.axis_index("c")


def _other_chips(x, y):
    return [(1 - x, y), (x, 1 - y), (1 - x, 1 - y)]


def _remote(src, dst, send_sems, recv_sems, k, dev):
    return pltpu.make_async_remote_copy(src_ref=src, dst_ref=dst, send_sem=send_sems.at[k], recv_sem=recv_sems.at[k],
                                        device_id=dev, device_id_type=MESH)


class _Hook:
    def __init__(self, carried, landing, n_sems, start, finish):
        self.carried, self.landing, self.n_sems, self.start, self.finish = list(carried), list(landing), n_sems, start, finish


def _hook_gather(buf):
    half = buf.shape[1] // 2

    def plan(o, ss, rs):
        x, y, c = _my_pos()
        chips = _other_chips(x, y)
        mine_rows = pl.ds(pl.multiple_of(c * half, 16), half)
        sib_rows = pl.ds(pl.multiple_of((1 - c) * half, 16), half)
        mine = o.at[2 * x + y, mine_rows]
        sends = [_remote(mine, mine, ss, rs, k, (px, py, c)) for k, (px, py) in enumerate(chips)]
        return sends, chips, mine_rows, sib_rows, (x, y, c)

    def start(car, land, ss, rs):
        for cp in plan(car[0], ss, rs)[0]:
            cp.start()

    def finish(car, land, ss, rs):
        o = car[0]
        sends, chips, mine_rows, sib_rows, (x, y, c) = plan(o, ss, rs)
        passed = []
        for k, (px, py) in enumerate(chips):
            got = o.at[2 * px + py, mine_rows]
            _remote(got, got, ss, rs, k, (px, py, c)).wait_recv()
            fwd = _remote(got, got, ss, rs, 3 + k, (x, y, 1 - c))
            fwd.start()
            passed.append(fwd)
        for k, (px, py) in enumerate(chips):
            got = o.at[2 * px + py, sib_rows]
            _remote(got, got, ss, rs, 3 + k, (x, y, 1 - c)).wait_recv()
        for cp in sends + passed:
            cp.wait_send()

    return _Hook([buf], [], 6, start, finish)


def _hook_pair(g, half_rows):
    n, r, cc = g.shape
    h = r // 2 if half_rows else r

    def plan(car, land, ss, rs):
        x, y, c = _my_pos()
        cps = []
        for s in range(n):
            src = car[0].at[s, pl.ds(pl.multiple_of((1 - c) * h, 8), h)] if half_rows else car[0].at[s]
            cps.append(_remote(src, land[0].at[s], ss, rs, s, (x, y, 1 - c)))
        return cps

    def start(car, land, ss, rs):
        for cp in plan(car, land, ss, rs):
            cp.start()

    def finish(car, land, ss, rs):
        cps = plan(car, land, ss, rs)
        for cp in cps:
            cp.wait_recv()
        for cp in cps:
            cp.wait_send()

    return _Hook([g], [_sds((n, h, cc), g.dtype)], n, start, finish)


def _hook_scatter(s16):
    _, h, cc = s16.shape

    def plan(car, land, ss, rs):
        x, y, c = _my_pos()
        return [_remote(car[0].at[2 * px + py], land[0].at[k], ss, rs, k, (px, py, c))
                for k, (px, py) in enumerate(_other_chips(x, y))]

    def start(car, land, ss, rs):
        for cp in plan(car, land, ss, rs):
            cp.start()

    def finish(car, land, ss, rs):
        cps = plan(car, land, ss, rs)
        for cp in cps:
            cp.wait_recv()
        for cp in cps:
            cp.wait_send()

    return _Hook([s16], [_sds((3, h, cc), s16.dtype)], 3, start, finish)


def _hook_halves(full):
    h = full.shape[0] // 2

    def half_copy(car, ss, rs, which):
        x, y, c = _my_pos()
        rows = car[0].at[pl.ds(pl.multiple_of((c + which - 2 * c * which) * h, 8), h)]
        return _remote(rows, rows, ss, rs, 0, (x, y, 1 - c))

    def start(car, land, ss, rs):
        half_copy(car, ss, rs, 0).start()

    def finish(car, land, ss, rs):
        half_copy(car, ss, rs, 1).wait_recv()
        half_copy(car, ss, rs, 0).wait_send()

    return _Hook([full], [], 1, start, finish)


def _pcall(body, *, name, grid, in_specs, out_specs, out_shape, operands, scratch=(), vmem=48, sem=None,
           prefetch=(), aliases=None, hook=None):
    n_pre, n_in, n_out, n_scr = len(prefetch), len(in_specs), len(out_specs), len(scratch)
    in_specs, out_specs, out_shape, scratch = list(in_specs), list(out_specs), list(out_shape), list(scratch)
    io_alias = {n_pre + a: b for a, b in (aliases or {}).items()}
    operands = list(operands)
    kernel_body = body
    if hook is not None:
        n_car, n_land = len(hook.carried), len(hook.landing)
        for t, arr in enumerate(hook.carried):
            io_alias[n_pre + n_in + t] = n_out + t
        in_specs += [_ANY] * n_car
        out_specs += [_ANY] * (n_car + n_land)
        out_shape += [_sds(a.shape, a.dtype) for a in hook.carried] + hook.landing
        scratch += [pltpu.SemaphoreType.DMA((hook.n_sems,)), pltpu.SemaphoreType.DMA((hook.n_sems,))]
        operands += hook.carried
        sem = ("arbitrary",) * len(grid)

        def kernel_body(*refs):
            pre, rest = refs[:n_pre], refs[n_pre:]
            ins = rest[:n_in]
            outs = rest[n_in + n_car:n_in + n_car + n_out]
            car = rest[n_in + n_car + n_out:n_in + 2 * n_car + n_out]
            land = rest[n_in + 2 * n_car + n_out:n_in + 2 * n_car + n_out + n_land]
            scr = rest[n_in + 2 * n_car + n_out + n_land:]
            send_sems, recv_sems = scr[n_scr], scr[n_scr + 1]
            first = pl.program_id(0) == 0
            last = pl.program_id(0) == grid[0] - 1
            for d in range(1, len(grid)):
                first = first & (pl.program_id(d) == 0)
                last = last & (pl.program_id(d) == grid[d] - 1)

            @pl.when(first)
            def _():
                hook.start(car, land, send_sems, recv_sems)

            body(*pre, *ins, *outs, *scr[:n_scr])

            @pl.when(last)
            def _():
                hook.finish(car, land, send_sems, recv_sems)

    params = pltpu.CompilerParams(vmem_limit_bytes=vmem * _MIB, dimension_semantics=sem,
                                  has_side_effects=hook is not None)
    if n_pre:
        call = pl.pallas_call(
            kernel_body, name=name, out_shape=out_shape, input_output_aliases=io_alias, compiler_params=params,
            grid_spec=pltpu.PrefetchScalarGridSpec(num_scalar_prefetch=n_pre, grid=grid, in_specs=in_specs,
                                                   out_specs=out_specs, scratch_shapes=scratch))
    else:
        call = pl.pallas_call(kernel_body, name=name, grid=grid, in_specs=in_specs, out_specs=out_specs,
                              out_shape=out_shape, scratch_shapes=scratch, input_output_aliases=io_alias,
                              compiler_params=params)
    return call(*prefetch, *operands)


def _comm_call(hook, *, name):
    n_car = len(hook.carried)

    def body(*refs):
        car = refs[n_car:2 * n_car]
        land = refs[2 * n_car:2 * n_car + len(hook.landing)]
        send_sems, recv_sems = refs[-2], refs[-1]
        hook.start(car, land, send_sems, recv_sems)
        hook.finish(car, land, send_sems, recv_sems)

    return pl.pallas_call(
        body, name=name, in_specs=[_ANY] * n_car, out_specs=[_ANY] * (n_car + len(hook.landing)),
        out_shape=[_sds(a.shape, a.dtype) for a in hook.carried] + hook.landing,
        input_output_aliases={t: t for t in range(n_car)},
        scratch_shapes=[pltpu.SemaphoreType.DMA((hook.n_sems,)), pltpu.SemaphoreType.DMA((hook.n_sems,))],
        compiler_params=pltpu.CompilerParams(has_side_effects=True),
    )(*hook.carried)


def _gather_small(blk):
    r, cc = blk.shape

    def body(x_ref, o_ref, send_sems, recv_sems):
        x, y, c = _my_pos()
        me = 2 * x + y
        o_ref[me] = x_ref[...]
        sends = [_remote(x_ref, o_ref.at[me], send_sems, recv_sems, k, (px, py, c))
                 for k, (px, py) in enumerate(_other_chips(x, y))]
        for cp in sends:
            cp.start()
        for k, (px, py) in enumerate(_other_chips(x, y)):
            _remote(x_ref, o_ref.at[2 * px + py], send_sems, recv_sems, k, (px, py, c)).wait_recv()
        for cp in sends:
            cp.wait_send()

    vm = pl.BlockSpec(memory_space=pltpu.VMEM)
    return pl.pallas_call(
        body, name="gather_small", in_specs=[vm], out_specs=vm, out_shape=_sds((N_SH, r, cc), blk.dtype),
        scratch_shapes=[pltpu.SemaphoreType.DMA((3,)), pltpu.SemaphoreType.DMA((3,))],
        compiler_params=pltpu.CompilerParams(has_side_effects=True),
    )(blk)


N_DEV = 8


def _allreduce_small(pack):
    r, cc = pack.shape

    def body(x_ref, o_ref, buf_ref, send_sems, recv_sems):
        x, y, c = _my_pos()
        me = 4 * x + 2 * y + c
        buf_ref[me] = x_ref[...]
        cps = []
        for k in range(1, N_DEV):
            peer = (x ^ ((k >> 2) & 1), y ^ ((k >> 1) & 1), c ^ (k & 1))
            cps.append(_remote(x_ref, buf_ref.at[me], send_sems, recv_sems, k - 1, peer))
        for cp in cps:
            cp.start()
        for k in range(1, N_DEV):
            peer = (x ^ ((k >> 2) & 1), y ^ ((k >> 1) & 1), c ^ (k & 1))
            src = 4 * peer[0] + 2 * peer[1] + peer[2]
            _remote(x_ref, buf_ref.at[src], send_sems, recv_sems, k - 1, peer).wait_recv()
        acc = buf_ref[0]
        for d in range(1, N_DEV):
            acc = acc + buf_ref[d]
        o_ref[...] = acc
        for cp in cps:
            cp.wait_send()

    vm = pl.BlockSpec(memory_space=pltpu.VMEM)
    return pl.pallas_call(
        body, name="allreduce_small", in_specs=[vm], out_specs=vm, out_shape=_sds((r, cc), F32),
        scratch_shapes=[pltpu.VMEM((N_DEV, r, cc), F32), pltpu.SemaphoreType.DMA((N_DEV - 1,)),
                        pltpu.SemaphoreType.DMA((N_DEV - 1,))],
        compiler_params=pltpu.CompilerParams(has_side_effects=True),
    )(pack)


def _mm_nn(a, w, bias, *, name, grid, tm, tn, k, a_index, w_block, w_index, out_cols, out_index, carried=None, hook=None):
    m = a.shape[0]

    def body(a_ref, w_ref, b_ref, *rest):
        o_ref = rest[-1]
        wv = w_ref[...]
        acc = jnp.dot(a_ref[...].astype(_MXU), wv.reshape(k, tn).astype(_MXU), preferred_element_type=F32)
        o_ref[...] = acc + b_ref[...]

    operands = [a, w, bias] + ([carried] if carried is not None else [])
    return _pcall(
        body, name=name, grid=grid,
        in_specs=[pl.BlockSpec((tm, k), a_index), pl.BlockSpec(w_block, w_index),
                  pl.BlockSpec((1, tn), lambda i, j: (0, j))] + ([_ANY] if carried is not None else []),
        out_specs=[pl.BlockSpec((tm, tn), out_index)], out_shape=[_sds((m, out_cols), F32)], operands=operands,
        vmem=56, sem=("parallel", "parallel"), aliases={3: 0} if carried is not None else None, hook=hook)


def _mm_nt(a, w, *, name, grid, tm, tn, tk, a_index, w_block, w_index, out_cols, hook=None):
    m = a.shape[0]
    nk = grid[2]

    def body(a_ref, w_ref, o_ref, acc_ref):
        part = lax.dot_general(a_ref[...].astype(_MXU), w_ref[...].astype(_MXU), _NT, preferred_element_type=F32)
        if nk == 1:
            o_ref[...] = part
        else:
            kidx = pl.program_id(2)

            @pl.when(kidx == 0)
            def _():
                acc_ref[...] = part

            @pl.when(kidx > 0)
            def _():
                acc_ref[...] += part

            @pl.when(kidx == nk - 1)
            def _():
                o_ref[...] = acc_ref[...]

    return _pcall(
        body, name=name, grid=grid, in_specs=[pl.BlockSpec((tm, tk), a_index), pl.BlockSpec(w_block, w_index)],
        out_specs=[pl.BlockSpec((tm, tn), lambda i, j, q: (i, j))], out_shape=[_sds((m, out_cols), F32)],
        operands=[a, w], scratch=[pltpu.VMEM((tm, tn) if nk > 1 else (8, 128), F32)], vmem=60,
        sem=("parallel", "parallel", "arbitrary"), hook=hook)


def _mm_tn(a, b, *, name, grid, tm, tn, a_index, b_index, out_shape, out_block, out_index, carried=None,
           prefetch=(), hook=None):
    t = a.shape[0]

    def body(*refs):
        a_ref, b_ref = refs[len(prefetch)], refs[len(prefetch) + 1]
        refs[-1][...] = lax.dot_general(a_ref[...].astype(_MXU), b_ref[...].astype(_MXU), _TN, preferred_element_type=F32)

    operands = [a, b] + ([carried] if carried is not None else [])
    return _pcall(
        body, name=name, grid=grid,
        in_specs=[pl.BlockSpec((t, tm), a_index), pl.BlockSpec((t, tn), b_index)] + ([_ANY] if carried is not None else []),
        out_specs=[pl.BlockSpec(out_block, out_index)], out_shape=[_sds(out_shape, F32)], operands=operands,
        vmem=56, sem=("parallel", "parallel"), aliases={2: 0} if carried is not None else None, prefetch=prefetch, hook=hook)


def _mm_z_gather(h16, w_own, b_in, order):
    n_tiles = N_SH * PER_IN
    half = D // 2

    def body(order_ref, a_ref, b_ref, w_in_ref, z_ref, w_ref, wbuf, tile_sems, ss, rs):
        del w_in_ref
        j = pl.program_id(0)
        x, y, c = _my_pos()
        me = 2 * x + y
        chips = _other_chips(x, y)
        mine_rows = pl.ds(pl.multiple_of(c * half, 16), half)
        sib_rows = pl.ds(pl.multiple_of((1 - c) * half, 16), half)

        def ici(k, slot):
            px, py = chips[k]
            part = w_ref.at[slot, mine_rows]
            return _remote(part, part, ss, rs, k, (px, py, c))

        def d2d(k, rows):
            px, py = chips[k]
            part = w_ref.at[2 * px + py, rows]
            return _remote(part, part, ss, rs, 3 + k, (x, y, 1 - c))

        def tile_copy(t):
            rel = t // PER_IN
            slot = me ^ jnp.where(rel == 0, 0, jnp.where(rel == 1, 2, jnp.where(rel == 2, 1, 3)))
            col = pl.multiple_of((t % PER_IN) * TN_IN, 128)
            return pltpu.make_async_copy(w_ref.at[slot, :, pl.ds(col, TN_IN)], wbuf.at[t % 2], tile_sems.at[t % 2])

        @pl.when(j == 0)
        def _():
            for k in range(3):
                ici(k, me).start()
            tile_copy(0).start()

        for k in range(3):
            px, py = chips[k]

            @pl.when(j == PER_IN * (k + 1) - 2)
            def _():
                ici(k, 2 * px + py).wait_recv()
                d2d(k, mine_rows).start()

            @pl.when(j == PER_IN * (k + 1) - 1)
            def _():
                d2d(k, sib_rows).wait_recv()

        @pl.when(j + 1 < n_tiles)
        def _():
            tile_copy(j + 1).start()

        tile_copy(j).wait()
        z_ref[...] = jnp.dot(a_ref[...], wbuf[j % 2], preferred_element_type=F32) + b_ref[...]

        @pl.when(j == n_tiles - 1)
        def _():
            for k in range(3):
                ici(k, me).wait_send()
                d2d(k, mine_rows).wait_send()

    col_tile = lambda j, o: (0, o[j // PER_IN] * PER_IN + j % PER_IN)
    return pl.pallas_call(
        body, name="mm_z_gather",
        grid_spec=pltpu.PrefetchScalarGridSpec(
            num_scalar_prefetch=1, grid=(n_tiles,),
            in_specs=[pl.BlockSpec((TP, D), lambda j, o: (0, 0)), pl.BlockSpec((1, TN_IN), col_tile), _ANY],
            out_specs=[pl.BlockSpec((TP, TN_IN), col_tile), _ANY],
            scratch_shapes=[pltpu.VMEM((2, D, TN_IN), BF16), pltpu.SemaphoreType.DMA((2,)),
                            pltpu.SemaphoreType.DMA((6,)), pltpu.SemaphoreType.DMA((6,))]),
        out_shape=[_sds((TP, D_IN), F32), _sds(w_own.shape, w_own.dtype)],
        input_output_aliases={3: 1},
        compiler_params=pltpu.CompilerParams(vmem_limit_bytes=60 * _MIB, dimension_semantics=("arbitrary",),
                                             has_side_effects=True),
    )(order, h16, b_in, w_own)


def _padded_rows(i, x_ref, meta_ref):
    head = jnp.concatenate([jnp.zeros((PAD, D), F32), meta_ref[...]], axis=0)
    return jnp.where(i == 0, head, x_ref[...])


def _stream_specs():
    return [pl.BlockSpec((BLK, D), lambda i: (jnp.maximum(i - 1, 0), 0)), pl.BlockSpec((N_META, D), lambda i: (0, 0))]


def _ln_emb_fwd(x, meta, g, b):
    def body(x_ref, meta_ref, g_ref, b_ref, h32_ref, h16_ref):
        x = _padded_rows(pl.program_id(0), x_ref, meta_ref)
        mu = jnp.mean(x, axis=-1, keepdims=True)
        xc = x - mu
        var = jnp.mean(xc * xc, axis=-1, keepdims=True)
        y = xc * lax.rsqrt(var + LN_EPS) * g_ref[...] + b_ref[...]
        h32_ref[...] = y
        h16_ref[...] = y.astype(_MXU)

    row = pl.BlockSpec((BLK, D), lambda i: (i, 0))
    vec = pl.BlockSpec((1, D), lambda i: (0, 0))
    return _pcall(body, name="ln_emb_fwd", grid=(NBLK,), in_specs=_stream_specs() + [vec, vec], out_specs=[row, row],
                  out_shape=[_sds((TP, D), F32), _sds((TP, D), _MXU)], operands=[x, meta, g, b], vmem=32,
                  sem=("parallel",))


def _ln_emb_bwd(x, meta, g, dr, dhz, *, hook):
    def body(x_ref, meta_ref, g_ref, dr_ref, dhz_ref, dx_ref, dmeta_ref, acc_ref):
        i = pl.program_id(0)

        @pl.when(i == 0)
        def _():
            acc_ref[...] = jnp.zeros_like(acc_ref)

        x = _padded_rows(i, x_ref, meta_ref)
        mu = jnp.mean(x, axis=-1, keepdims=True)
        xc = x - mu
        var = jnp.mean(xc * xc, axis=-1, keepdims=True)
        rstd = lax.rsqrt(var + LN_EPS)
        xhat = xc * rstd
        dh = ALPHA * dr_ref[...] + dhz_ref[...]
        acc_ref[0:1, :] += jnp.sum(dh * xhat, axis=0, keepdims=True)
        acc_ref[1:2, :] += jnp.sum(dh, axis=0, keepdims=True)
        dxh = dh * g_ref[...]
        m1 = jnp.mean(dxh, axis=-1, keepdims=True)
        m2 = jnp.mean(dxh * xhat, axis=-1, keepdims=True)
        dx = rstd * (dxh - m1 - xhat * m2)
        dx_ref[...] = dx

        @pl.when(i == 0)
        def _():
            dmeta_ref[...] = dx[PAD:BLK]

    row = pl.BlockSpec((BLK, D), lambda i: (i, 0))
    vec = pl.BlockSpec((1, D), lambda i: (0, 0))
    xs, ms = _stream_specs()
    return _pcall(body, name="ln_emb_bwd", grid=(NBLK,), in_specs=[xs, ms, vec, row, row],
                  out_specs=[xs, ms, pl.BlockSpec((8, D), lambda i: (0, 0))],
                  out_shape=[_sds((SEQ, D), F32), _sds((N_META, D), F32), _sds((8, D), F32)],
                  operands=[x, meta, g, dr, dhz], vmem=32, sem=("arbitrary",), hook=hook)


def _mul_silu_fwd(a, z, off, *, name):
    def body(a_ref, z_ref, o_ref):
        zz = z_ref[...]
        o_ref[...] = (a_ref[...] * (zz * _sigmoid(zz))).astype(_MXU)

    blk = pl.BlockSpec((RT, CW), lambda i, j: (i, j))
    return _pcall(body, name=name, grid=(TP // RT, D // CW),
                  in_specs=[blk, pl.BlockSpec((RT, CW), lambda i, j: (i, off // CW + j))], out_specs=[blk],
                  out_shape=[_sds((TP, D), _MXU)], operands=[a, z], vmem=32, sem=("parallel", "parallel"))[0]


def _mul_silu_bwd(dy, a, z, off, dz, *, name, hook=None):
    def body(dy_ref, a_ref, z_ref, dz_in, da_ref, dg_ref):
        zz = z_ref[...]
        sg = _sigmoid(zz)
        d = dy_ref[...]
        da_ref[...] = d * (zz * sg)
        dg_ref[...] = (d * a_ref[...] * (sg * (1.0 + zz * (1.0 - sg)))).astype(_MXU)

    blk = pl.BlockSpec((RT, CW), lambda i, j: (i, j))
    zblk = pl.BlockSpec((RT, CW), lambda i, j: (i, off // CW + j))
    return _pcall(body, name=name, grid=(TP // RT, D // CW), in_specs=[blk, blk, zblk, _ANY], out_specs=[blk, zblk],
                  out_shape=[_sds((TP, D), F32), _sds((TP, D_IN), _MXU)], operands=[dy, a, z, dz], vmem=32,
                  sem=("parallel", "parallel"), aliases={3: 1}, hook=hook)


def _merge_fwd(y2, z):
    def body(ya_ref, yb_ref, ga_ref, gb_ref, o_ref):
        o_ref[...] = (_sigmoid(ga_ref[...]) * ya_ref[...] + _sigmoid(gb_ref[...]) * yb_ref[...]).astype(_MXU)

    nb = D // CW
    blk = pl.BlockSpec((RT, CW), lambda i, j: (i, j))
    return _pcall(body, name="merge_fwd", grid=(TP // RT, nb),
                  in_specs=[blk, pl.BlockSpec((RT, CW), lambda i, j: (i, nb + j)),
                            pl.BlockSpec((RT, CW), lambda i, j: (i, OFF_G // CW + j)),
                            pl.BlockSpec((RT, CW), lambda i, j: (i, (OFF_G + D) // CW + j))],
                  out_specs=[blk], out_shape=[_sds((TP, D), _MXU)], operands=[y2, y2, z, z], vmem=32,
                  sem=("parallel", "parallel"))[0]


def _merge_bwd(dmix, y2, z):
    nb = D // CW

    def body(dm_ref, y_ref, g_ref, dy_ref, dg_ref):
        dm = dm_ref[...]
        sg = _sigmoid(g_ref[...])
        dy_ref[...] = (dm * sg).astype(_MXU)
        dg_ref[...] = (dm * y_ref[...] * sg * (1.0 - sg)).astype(_MXU)

    blk = pl.BlockSpec((RT, CW), lambda i, j: (i, j))
    gblk = pl.BlockSpec((RT, CW), lambda i, j: (i, OFF_G // CW + j))
    return _pcall(body, name="merge_bwd", grid=(TP // RT, 2 * nb),
                  in_specs=[pl.BlockSpec((RT, CW), lambda i, j: (i, j % nb)), blk, gblk], out_specs=[blk, gblk],
                  out_shape=[_sds((TP, 2 * D), _MXU), _sds((TP, D_IN), _MXU)], operands=[dmix, y2, z], vmem=32,
                  sem=("parallel", "parallel"))


def _ln_out_loss(h32, out, target, g, b):
    def body(h_ref, o_ref, t_ref, g_ref, b_ref, dr_ref, acc_ref):
        i = pl.program_id(0)

        @pl.when(i == 0)
        def _():
            acc_ref[...] = jnp.zeros_like(acc_ref)

        r = ALPHA * h_ref[...] + o_ref[...]
        mu = jnp.mean(r, axis=-1, keepdims=True)
        rc = r - mu
        var = jnp.mean(rc * rc, axis=-1, keepdims=True)
        rstd = lax.rsqrt(var + LN_EPS)
        xhat = rc * rstd
        gg = g_ref[...]
        y = xhat * gg + b_ref[...]
        real = (i >= 1).astype(F32)
        diff = (y - t_ref[...]) * real
        dy = diff * (1.0 / D)
        dxh = dy * gg
        m1 = jnp.mean(dxh, axis=-1, keepdims=True)
        m2 = jnp.mean(dxh * xhat, axis=-1, keepdims=True)
        dr = rstd * (dxh - m1 - xhat * m2)
        dr_ref[...] = dr
        acc_ref[0:1, :] += jnp.sum(dy * xhat, axis=0, keepdims=True)
        acc_ref[1:2, :] += jnp.sum(dy, axis=0, keepdims=True)
        acc_ref[2:3, :] += jnp.sum(dr, axis=0, keepdims=True)
        acc_ref[3:4, :] += (0.5 / D) * jnp.sum(diff * diff)

    row = pl.BlockSpec((BLK, D), lambda i: (i, 0))
    vec = pl.BlockSpec((1, D), lambda i: (0, 0))
    return _pcall(body, name="ln_out_loss", grid=(NBLK,),
                  in_specs=[row, row, pl.BlockSpec((BLK, D), lambda i: (jnp.maximum(i - 1, 0), 0)), vec, vec],
                  out_specs=[row, pl.BlockSpec((8, D), lambda i: (0, 0))],
                  out_shape=[_sds((TP, D), F32), _sds((8, D), F32)], operands=[h32, out, target, g, b], vmem=32,
                  sem=("arbitrary",))


def _colsum(x, *, name, tn, hook=None):
    _, n = x.shape

    def body(x_ref, o_ref):
        o_ref[...] = jnp.sum(x_ref[...].astype(F32), axis=0, keepdims=True)

    return _pcall(body, name=name, grid=(n // tn,), in_specs=[pl.BlockSpec((TP, tn), lambda j: (0, j))],
                  out_specs=[pl.BlockSpec((1, tn), lambda j: (0, j))], out_shape=[_sds((1, n), F32)], operands=[x],
                  vmem=32, sem=("parallel",), hook=hook)


def _rnn_recompute(xr_ref, cw_ref, cb_ref, wra_ref, wri_ref, bra_ref, bri_ref, lam_ref):
    rows = lax.broadcasted_iota(jnp.int32, (TP, 1), 0)
    valid = (rows >= PAD).astype(F32)
    first = rows == PAD
    x = xr_ref[...] * valid
    cw = cw_ref[...]
    shifted = [x, pltpu.roll(x, 1, 0), pltpu.roll(x, 2, 0), pltpu.roll(x, 3, 0)]
    c = cb_ref[...] + cw[0:1, :] * shifted[0] + cw[1:2, :] * shifted[1] + cw[2:3, :] * shifted[2] + cw[3:4, :] * shifted[3]
    cm = c.astype(_MXU)
    gr = _sigmoid(jnp.dot(cm, wra_ref[...].astype(_MXU), preferred_element_type=F32) + bra_ref[...])
    gi = _sigmoid(jnp.dot(cm, wri_ref[...].astype(_MXU), preferred_element_type=F32) + bri_ref[...])
    lam = lam_ref[...]
    ls = jnp.minimum(lam, 0.0) - jnp.log(1.0 + jnp.exp(-jnp.abs(lam)))
    log_a = LRU_C * gr * ls
    a = jnp.exp(log_a)
    mult = jnp.where(first, 1.0, jnp.sqrt(1.0 - jnp.exp(2.0 * log_a)))
    return dict(valid=valid, first=first, shifted=shifted, c=c, cm=cm, gr=gr, gi=gi, ls=ls, a=a, mult=mult, lam=lam)


def _rnn_specs():
    col = pl.BlockSpec((TP, RB), lambda n: (0, n))
    vec = pl.BlockSpec((1, RB), lambda n: (0, n))
    return dict(col=col, vec=vec, cw=pl.BlockSpec((CONV_W, RB), lambda n: (0, n)),
                wblk=pl.BlockSpec((None, RB, RB), lambda n: (n, 0, 0)))


def _rnn_gates_fwd(z, conv_w, conv_b, w_ra, w_ri, b_ra, b_ri, lam):
    def body(xr_ref, cw_ref, cb_ref, wra_ref, wri_ref, bra_ref, bri_ref, lam_ref, a_ref, u_ref):
        r = _rnn_recompute(xr_ref, cw_ref, cb_ref, wra_ref, wri_ref, bra_ref, bri_ref, lam_ref)
        a_ref[...] = r["a"]
        u_ref[...] = r["mult"] * r["gi"] * r["c"] * r["valid"]

    s = _rnn_specs()
    return _pcall(body, name="rnn_gates_fwd", grid=(N_RB,),
                  in_specs=[s["col"], s["cw"], s["vec"], s["wblk"], s["wblk"], s["vec"], s["vec"], s["vec"]],
                  out_specs=[s["col"], s["col"]], out_shape=[_sds((TP, D), F32)] * 2,
                  operands=[z, conv_w, conv_b, w_ra, w_ri, b_ra, b_ri, lam], vmem=56, sem=("parallel",))


SCAN_ROWS = 272


def _scan_fwd(a, u):
    def body(a_ref, u_ref, h_ref, carry_ref):
        @pl.when(pl.program_id(0) == 0)
        def _():
            carry_ref[...] = jnp.zeros_like(carry_ref)

        def step(t, h):
            h = a_ref[t] * h + u_ref[t]
            h_ref[t] = h
            return h

        carry_ref[...] = lax.fori_loop(0, SCAN_ROWS, step, carry_ref[...], unroll=8)

    blk = pl.BlockSpec((SCAN_ROWS, 16, 128), lambda i: (i, 0, 0))
    h = _pcall(body, name="scan_fwd", grid=(TP // SCAN_ROWS,), in_specs=[blk, blk], out_specs=[blk],
               out_shape=[_sds((TP, 16, 128), F32)], operands=[a.reshape(TP, 16, 128), u.reshape(TP, 16, 128)],
               scratch=[pltpu.VMEM((16, 128), F32)], vmem=32, sem=("arbitrary",))[0]
    return h.reshape(TP, D)


def _scan_bwd(a, dh):
    nst = TP // SCAN_ROWS

    def body(a_ref, d_ref, o_ref, lam_ref, anext_ref):
        @pl.when(pl.program_id(0) == 0)
        def _():
            lam_ref[...] = jnp.zeros_like(lam_ref)
            anext_ref[...] = jnp.zeros_like(anext_ref)

        def step(q, carry):
            lam, an = carry
            t = SCAN_ROWS - 1 - q
            lam = d_ref[t] + an * lam
            o_ref[t] = lam
            return lam, a_ref[t]

        lam, an = lax.fori_loop(0, SCAN_ROWS, step, (lam_ref[...], anext_ref[...]), unroll=8)
        lam_ref[...] = lam
        anext_ref[...] = an

    blk = pl.BlockSpec((SCAN_ROWS, 16, 128), lambda i: (nst - 1 - i, 0, 0))
    o = _pcall(body, name="scan_bwd", grid=(nst,), in_specs=[blk, blk], out_specs=[blk],
               out_shape=[_sds((TP, 16, 128), F32)], operands=[a.reshape(TP, 16, 128), dh.reshape(TP, 16, 128)],
               scratch=[pltpu.VMEM((16, 128), F32), pltpu.VMEM((16, 128), F32)], vmem=32, sem=("arbitrary",))[0]
    return o.reshape(TP, D)


def _rnn_gates_bwd(z, lam_s, hr, conv_w, conv_b, w_ra, w_ri, b_ra, b_ri, lam, dz, gsq):
    def body(xr_ref, ls_ref, hr_ref, cw_ref, cb_ref, wra_ref, wri_ref, bra_ref, bri_ref, lam_ref, dz_in, gsq_in,
             dx_ref, dw_ref, sums_ref):
        r = _rnn_recompute(xr_ref, cw_ref, cb_ref, wra_ref, wri_ref, bra_ref, bri_ref, lam_ref)
        valid, c, gr, gi, a, mult = r["valid"], r["c"], r["gr"], r["gi"], r["a"], r["mult"]
        du = ls_ref[...] * valid
        da = du * pltpu.roll(hr_ref[...], 1, 0)
        d_gi = du * mult * c
        dc = du * mult * gi
        dmult = du * gi * c
        dlog_a = da * a + jnp.where(r["first"], 0.0, -dmult * a * a / mult)
        d_gr = dlog_a * (LRU_C * r["ls"])
        dls = jnp.sum(dlog_a * (LRU_C * gr), axis=0, keepdims=True)
        dpre_r = d_gr * gr * (1.0 - gr)
        dpre_i = d_gi * gi * (1.0 - gi)
        pr = dpre_r.astype(_MXU)
        pi = dpre_i.astype(_MXU)
        dwra = lax.dot_general(r["cm"], pr, _TN, preferred_element_type=F32)
        dwri = lax.dot_general(r["cm"], pi, _TN, preferred_element_type=F32)
        for s in range(N_SH):
            dw_ref[s, 0:64, :] = dwra[64 * s:64 * (s + 1)]
            dw_ref[s, 64:128, :] = dwri[64 * s:64 * (s + 1)]
        dc = dc + lax.dot_general(pr, wra_ref[...].astype(_MXU), _NT, preferred_element_type=F32)
        dc = dc + lax.dot_general(pi, wri_ref[...].astype(_MXU), _NT, preferred_element_type=F32)
        cw = cw_ref[...]
        dx = cw[0:1, :] * dc
        for k in range(1, CONV_W):
            dx = dx + cw[k:k + 1, :] * pltpu.roll(dc, TP - k, 0)
        dx_ref[...] = (dx * valid).astype(_MXU)
        for k in range(CONV_W):
            sums_ref[k:k + 1, :] = jnp.sum(dc * r["shifted"][k], axis=0, keepdims=True)
        sums_ref[4:5, :] = jnp.sum(dc, axis=0, keepdims=True)
        sums_ref[5:6, :] = jnp.sum(dpre_r, axis=0, keepdims=True)
        sums_ref[6:7, :] = jnp.sum(dpre_i, axis=0, keepdims=True)
        sums_ref[7:8, :] = dls * _sigmoid(-r["lam"])

    s = _rnn_specs()
    return _pcall(
        body, name="rnn_gates_bwd", grid=(N_RB,),
        in_specs=[s["col"], s["col"], s["col"], s["cw"], s["vec"], s["wblk"], s["wblk"], s["vec"], s["vec"], s["vec"],
                  _ANY, _ANY],
        out_specs=[s["col"], pl.BlockSpec((N_SH, 128, RB), lambda n: (0, 3 * SQ_ROWS // 128, n)),
                   pl.BlockSpec((8, RB), lambda n: (0, n))],
        out_shape=[_sds((TP, D_IN), _MXU), _sds((N_SH, PACK_ROWS, D), F32), _sds((8, D), F32)],
        operands=[z, lam_s, hr, conv_w, conv_b, w_ra, w_ri, b_ra, b_ri, lam, dz, gsq], vmem=60, sem=("parallel",),
        aliases={10: 0, 11: 1})


def _rope_tables():
    half = HD // 2
    inv = ROPE_THETA ** (-jnp.arange(half, dtype=F32) / half)
    pos = (jnp.arange(TP) - PAD).astype(F32)
    ang = pos[:, None] * inv[None, :]
    return jnp.tile(jnp.cos(ang), (1, 4)), jnp.tile(jnp.sin(ang), (1, 4))


def _rope(x, cos_t, sin_t, sign):
    w = x.shape[1]
    lane = lax.broadcasted_iota(jnp.int32, x.shape, 1)
    first = (lane % HD) < (HD // 2)
    swapped = jnp.where(first, pltpu.roll(x, w - HD // 2, 1), pltpu.roll(x, HD // 2, 1))
    ct = jnp.tile(cos_t, (1, w // 128))
    st = jnp.tile(sin_t, (1, w // 128))
    return x * ct + swapped * jnp.where(first, -sign * st, sign * st)


def _rope_fwd(z, cos_t, sin_t):
    def body(q_ref, k_ref, v_ref, c_ref, s_ref, qo_ref, ko_ref, vo_ref):
        c = c_ref[...]
        s = s_ref[...]
        qo_ref[...] = _rope(q_ref[...], c, s, 1.0).astype(_MXU)
        ko_ref[...] = _rope(k_ref[...], c, s, 1.0).astype(_MXU)
        vo_ref[...] = v_ref[...].astype(_MXU)

    tab = pl.BlockSpec((BLK, 128), lambda i: (i, 0))
    kv = pl.BlockSpec((BLK, D_KV), lambda i: (i, 0))
    return _pcall(body, name="rope_fwd", grid=(NBLK,),
                  in_specs=[pl.BlockSpec((BLK, D), lambda i: (i, OFF_Q // D)),
                            pl.BlockSpec((BLK, D_KV), lambda i: (i, OFF_K // D_KV)),
                            pl.BlockSpec((BLK, D_KV), lambda i: (i, OFF_V // D_KV)), tab, tab],
                  out_specs=[pl.BlockSpec((BLK, D), lambda i: (i, 0)), kv, kv],
                  out_shape=[_sds((TP, D), _MXU), _sds((TP, D_KV), _MXU), _sds((TP, D_KV), _MXU)],
                  operands=[z, z, z, cos_t, sin_t], vmem=32, sem=("parallel",))


def _rope_bwd_k(dk, dv, cos_t, sin_t, dz):
    def body(dk_ref, dv_ref, c_ref, s_ref, dz_in, o_ref):
        o_ref[:, 0:D_KV] = _rope(dk_ref[...], c_ref[...], s_ref[...], -1.0).astype(_MXU)
        o_ref[:, D_KV:2 * D_KV] = dv_ref[...].astype(_MXU)

    tab = pl.BlockSpec((BLK, 128), lambda i: (i, 0))
    kv = pl.BlockSpec((BLK, D_KV), lambda i: (i, 0))
    return _pcall(body, name="rope_bwd_k", grid=(NBLK,), in_specs=[kv, kv, tab, tab, _ANY],
                  out_specs=[pl.BlockSpec((BLK, 2 * D_KV), lambda i: (i, OFF_K // (2 * D_KV)))],
                  out_shape=[_sds((TP, D_IN), _MXU)], operands=[dk, dv, cos_t, sin_t, dz], vmem=32, sem=("parallel",),
                  aliases={4: 0})[0]


def _attn_mask(i):
    ql = lax.broadcasted_iota(jnp.int32, (GRP * BLK, 3 * BLK), 0) % BLK
    kk = lax.broadcasted_iota(jnp.int32, (GRP * BLK, 3 * BLK), 1)
    kl = kk % BLK
    part = kk // BLK
    meta = (part == 0) & (kl >= PAD) & ((i >= 1) | (kl <= ql))
    prev = (part == 1) & (i >= 2) & (kl > ql)
    cur = (part == 2) & (i >= 1) & (kl <= ql)
    return meta | prev | cur


def _cat_kv(refs, g):
    return jnp.concatenate([r[:, HD * g:HD * (g + 1)] for r in refs], axis=0)


def _kv_specs():
    return [pl.BlockSpec((BLK, D_KV), lambda i: (0, 0)),
            pl.BlockSpec((BLK, D_KV), lambda i: (jnp.maximum(i - 1, 0), 0)),
            pl.BlockSpec((BLK, D_KV), lambda i: (i, 0))]


def _stack_heads(ref, g, width=HD):
    return jnp.concatenate([ref[:, width * (GRP * g + j):width * (GRP * g + j + 1)] for j in range(GRP)], axis=0)


def _sink_column(sink_ref, g):
    return jnp.concatenate([jnp.full((BLK, 1), sink_ref[GRP * g + j], F32) for j in range(GRP)], axis=0)


def _attn_fwd(q, k, v, sinks, *, hook):
    def body(q_ref, k0_ref, kp_ref, kc_ref, v0_ref, vp_ref, vc_ref, sink_ref, o_ref, lse_ref):
        mask = _attn_mask(pl.program_id(0))
        for g in range(N_KV):
            q8 = _stack_heads(q_ref, g)
            sink = _sink_column(sink_ref, g)
            kcat = _cat_kv((k0_ref, kp_ref, kc_ref), g)
            s = jnp.where(mask, lax.dot_general(q8, kcat, _NT, preferred_element_type=F32) * (HD ** -0.5), NEG_INF)
            mx = jnp.maximum(jnp.max(s, -1, keepdims=True), sink)
            p = jnp.exp(s - mx)
            den = jnp.sum(p, -1, keepdims=True) + jnp.exp(sink - mx)
            o8 = jnp.dot((p * (1.0 / den)).astype(_MXU), _cat_kv((v0_ref, vp_ref, vc_ref), g), preferred_element_type=F32)
            lse8 = mx + jnp.log(den)
            for j in range(GRP):
                h = GRP * g + j
                o_ref[:, HD * h:HD * (h + 1)] = o8[BLK * j:BLK * (j + 1)]
                lse_ref[:, h:h + 1] = lse8[BLK * j:BLK * (j + 1)]

    return _pcall(body, name="attn_fwd", grid=(NBLK,),
                  in_specs=[pl.BlockSpec((BLK, D), lambda i: (i, 0))] + _kv_specs() + _kv_specs()
                           + [pl.BlockSpec(memory_space=pltpu.SMEM)],
                  out_specs=[pl.BlockSpec((BLK, D), lambda i: (i, 0)), pl.BlockSpec((BLK, N_Q), lambda i: (i, 0))],
                  out_shape=[_sds((TP, D), F32), _sds((TP, N_Q), F32)], operands=[q, k, k, k, v, v, v, sinks],
                  vmem=40, sem=("parallel",), hook=hook)


def _attn_bwd(q, k, v, sinks, do, o, lse, cos_t, sin_t, dz, *, hook):
    def body(q_ref, k0_ref, kp_ref, kc_ref, v0_ref, vp_ref, vc_ref, sink_ref, do_ref, o_ref, lse_ref, c_ref, s_ref, dz_in,
             dq_ref, dk_ref, dv_ref, dsink_ref, dqrot_ref):
        i = pl.program_id(0)

        @pl.when(i == 0)
        def _():
            dk_ref[...] = jnp.zeros_like(dk_ref)
            dv_ref[...] = jnp.zeros_like(dv_ref)
            dsink_ref[...] = jnp.zeros_like(dsink_ref)

        mask = _attn_mask(i)
        row_starts = (0, pl.multiple_of(jnp.maximum(i - 1, 0) * BLK, BLK), pl.multiple_of(i * BLK, BLK))
        scale = HD ** -0.5
        for g in range(N_KV):
            gs = slice(HD * g, HD * (g + 1))
            q8 = _stack_heads(q_ref, g)
            do8 = _stack_heads(do_ref, g)
            dom = do8.astype(_MXU)
            delta = jnp.sum(do8 * _stack_heads(o_ref, g), axis=-1, keepdims=True)
            lse8 = _stack_heads(lse_ref, g, width=1)
            dsk = -jnp.exp(_sink_column(sink_ref, g) - lse8) * delta
            for j in range(GRP):
                h = GRP * g + j
                dsink_ref[0:1, h:h + 1] += jnp.sum(dsk[BLK * j:BLK * (j + 1)], axis=0, keepdims=True)
            kcat = _cat_kv((k0_ref, kp_ref, kc_ref), g)
            s = jnp.where(mask, lax.dot_general(q8, kcat, _NT, preferred_element_type=F32) * scale, NEG_INF)
            p = jnp.exp(s - lse8)
            dp = lax.dot_general(dom, _cat_kv((v0_ref, vp_ref, vc_ref), g), _NT, preferred_element_type=F32)
            ds = (p * (dp - delta) * scale).astype(_MXU)
            dq8 = jnp.dot(ds, kcat, preferred_element_type=F32)
            dkcat = lax.dot_general(ds, q8, _TN, preferred_element_type=F32)
            dvcat = lax.dot_general(p.astype(_MXU), dom, _TN, preferred_element_type=F32)
            for part in range(3):
                rows = pl.ds(row_starts[part], BLK)
                dk_ref[rows, gs] += dkcat[BLK * part:BLK * (part + 1)]
                dv_ref[rows, gs] += dvcat[BLK * part:BLK * (part + 1)]
            for j in range(GRP):
                h = GRP * g + j
                dqrot_ref[:, HD * h:HD * (h + 1)] = dq8[BLK * j:BLK * (j + 1)]
        dq_ref[...] = _rope(dqrot_ref[...], c_ref[...], s_ref[...], -1.0).astype(_MXU)

    row = pl.BlockSpec((BLK, D), lambda i: (i, 0))
    tab = pl.BlockSpec((BLK, 128), lambda i: (i, 0))
    full_kv = pl.BlockSpec((TP, D_KV), lambda i: (0, 0))
    return _pcall(
        body, name="attn_bwd", grid=(NBLK,),
        in_specs=[row] + _kv_specs() + _kv_specs() + [pl.BlockSpec(memory_space=pltpu.SMEM), row, row,
                  pl.BlockSpec((BLK, N_Q), lambda i: (i, 0)), tab, tab, _ANY],
        out_specs=[pl.BlockSpec((BLK, D), lambda i: (i, OFF_Q // D)), full_kv, full_kv,
                   pl.BlockSpec((8, 128), lambda i: (0, 0))],
        out_shape=[_sds((TP, D_IN), _MXU), _sds((TP, D_KV), F32), _sds((TP, D_KV), F32), _sds((8, 128), F32)],
        operands=[q, k, k, k, v, v, v, sinks, do, o, lse, cos_t, sin_t, dz], scratch=[pltpu.VMEM((BLK, D), F32)],
        vmem=48, sem=("arbitrary",), aliases={13: 0}, hook=hook)


def _cast_into_slot(w32, pos, *, name, tr):
    r, cc = w32.shape

    def body(pos_ref, w_ref, o_ref):
        o_ref[...] = w_ref[...].astype(BF16)

    return _pcall(body, name=name, grid=(r // tr,), in_specs=[pl.BlockSpec((tr, cc), lambda i, p: (i, 0))],
                  out_specs=[pl.BlockSpec((None, tr, cc), lambda i, p: (p[1], i, 0))],
                  out_shape=[_sds((N_SH, r, cc), BF16)], operands=[w32], vmem=32, sem=("parallel",), prefetch=(pos,))[0]


def _pair_add(g, got, pos, *, name, tr, g_has_both_halves):
    n, h, cc = got.shape
    nt = h // tr

    def body(pos_ref, g_ref, r_ref, own_ref, s16_ref):
        s = g_ref[...] + r_ref[...]
        s16_ref[...] = s.astype(BF16)

        @pl.when(pl.program_id(1) == pos_ref[1])
        def _():
            own_ref[...] = s

    g_index = (lambda i, s, p: (s, p[0] * nt + i, 0)) if g_has_both_halves else (lambda i, s, p: (s, i, 0))
    return _pcall(body, name=name, grid=(nt, n),
                  in_specs=[pl.BlockSpec((None, tr, cc), g_index), pl.BlockSpec((None, tr, cc), lambda i, s, p: (s, i, 0))],
                  out_specs=[pl.BlockSpec((tr, cc), lambda i, s, p: (i, 0)),
                             pl.BlockSpec((None, tr, cc), lambda i, s, p: (s, i, 0))],
                  out_shape=[_sds((h, cc), F32), _sds((n, h, cc), BF16)], operands=[g, got], vmem=40,
                  sem=("parallel", "arbitrary"), prefetch=(pos,))


def _sum_chips(own, got, pos, *, name, tr):
    h, cc = own.shape
    nt = h // tr

    def body(pos_ref, o_ref, r_ref, out_ref):
        acc = o_ref[...]
        for k in range(3):
            acc = acc + r_ref[k].astype(F32)
        out_ref[...] = acc

    return _pcall(body, name=name, grid=(nt,),
                  in_specs=[pl.BlockSpec((tr, cc), lambda i, p: (i, 0)), pl.BlockSpec((3, tr, cc), lambda i, p: (0, i, 0))],
                  out_specs=[pl.BlockSpec((tr, cc), lambda i, p: (p[0] * nt + i, 0))],
                  out_shape=[_sds((2 * h, cc), F32)], operands=[own, got], vmem=40, sem=("parallel",), prefetch=(pos,))[0]


def _adamw(w, g, m, v, *, name, tr):
    r, cc = w.shape

    def body(w_ref, g_ref, m_ref, v_ref, d_ref, mo_ref, vo_ref):
        gg = g_ref[...]
        m_new = ADAM_B1 * m_ref[...] + (1.0 - ADAM_B1) * gg
        v_new = ADAM_B2 * v_ref[...] + (1.0 - ADAM_B2) * (gg * gg)
        m_hat = m_new / (1.0 - ADAM_B1 ** ADAM_STEP)
        v_hat = v_new / (1.0 - ADAM_B2 ** ADAM_STEP)
        d_ref[...] = -ADAM_LR * (m_hat / (jnp.sqrt(v_hat) + ADAM_EPS) + ADAM_WD * w_ref[...])
        mo_ref[...] = m_new
        vo_ref[...] = v_new

    blk = pl.BlockSpec((tr, cc), lambda i: (i, 0))
    return _pcall(body, name=name, grid=(r // tr,), in_specs=[blk] * 4, out_specs=[blk] * 3,
                  out_shape=[_sds((r, cc), F32)] * 3, operands=[w, g, m, v], vmem=48, sem=("parallel",))


_SMALL_ROWS = 40
_B_IN_ROWS = 7


def _row_pad(v, rows):
    flat = v.reshape(-1)
    return jnp.pad(flat, (0, rows * D - flat.shape[0])).reshape(rows, D)


def _pack_ra(w):
    return w.transpose(1, 0, 2).reshape(64, D)


def _unpack_ra(p, like):
    return p.reshape(64, N_RB, RB).transpose(1, 0, 2).reshape(like.shape)


def _gate_full(g4):
    return g4.reshape(N_SH, 64, N_RB, RB).transpose(2, 0, 1, 3).reshape(N_RB, RB, RB)


def kernel(x, meta_tokens, ln_emb_g, ln_emb_b, w_in, b_in, conv_w, conv_b, w_ra, b_ra, w_ri, b_ri, lru_lambda, sinks, w_rnn_out, w_attn_out, w_o, b_o, ln_g, ln_b, loss_target, m_meta_tokens, m_ln_emb_g, m_ln_emb_b, m_w_in, m_b_in, m_conv_w, m_conv_b, m_w_ra, m_b_ra, m_w_ri, m_b_ri, m_lru_lambda, m_sinks, m_w_rnn_out, m_w_attn_out, m_w_o, m_b_o, m_ln_g, m_ln_b, v_meta_tokens, v_ln_emb_g, v_ln_emb_b, v_w_in, v_b_in, v_conv_w, v_conv_b, v_w_ra, v_b_ra, v_w_ri, v_b_ri, v_lru_lambda, v_sinks, v_w_rnn_out, v_w_attn_out, v_w_o, v_b_o, v_ln_g, v_ln_b):
    xi, yi, ci = _my_pos()
    shard = 2 * xi + yi
    pos = jnp.stack([ci, shard, 1 - ci]).astype(jnp.int32)
    cos_t, sin_t = _rope_tables()
    zero_bias = jnp.zeros((1, D), F32)
    ln_emb_g2, ln_emb_b2 = ln_emb_g[None], ln_emb_b[None]

    small = jnp.concatenate([conv_w[0], meta_tokens, jnp.zeros((4, 512), F32)], axis=0)
    small4 = _gather_small(small)
    conv_w_full = small4[:, 0:4].transpose(1, 0, 2).reshape(CONV_W, D)
    meta_full = small4[:, 4:20].transpose(1, 0, 2).reshape(N_META, D)
    w_own = _cast_into_slot(w_in[0], pos, name="cast_w_in", tr=256)
    wpack = jnp.concatenate([w_rnn_out[0], w_attn_out[0], w_o[0], _pack_ra(w_ra[0]), _pack_ra(w_ri[0])], axis=0)
    wp_own = _cast_into_slot(wpack, pos, name="cast_w_sq", tr=208)

    h32, h16 = _ln_emb_fwd(x[0], meta_full, ln_emb_g2, ln_emb_b2)
    order = jnp.stack([shard, shard ^ 2, shard ^ 1, shard ^ 3]).astype(jnp.int32)
    z, w_in4 = _mm_z_gather(h16, w_own, b_in, order)
    q, k, v = _rope_fwd(z, cos_t, sin_t)
    o, lse, wp4 = _attn_fwd(q, k, v, sinks[0], hook=_hook_gather(wp_own))
    w_ra_full = _gate_full(wp4[:, 3 * SQ_ROWS:3 * SQ_ROWS + 64])
    w_ri_full = _gate_full(wp4[:, 3 * SQ_ROWS + 64:PACK_ROWS])

    def sq_nn(a, kk, bias, name, out_cols, out_index, carried=None):
        return _mm_nn(a, wp4, bias, name=name, grid=(2, D // CW), tm=HALF_TP, tn=CW, k=D, a_index=lambda i, j: (i, 0),
                      w_block=(N_SH, SQ_ROWS, CW), w_index=lambda i, j: (0, kk, j), out_cols=out_cols,
                      out_index=out_index, carried=carried)[0]

    def sq_nt(a, a_blk, kk, name, hook=None):
        return _mm_nt(a, wp4, name=name, grid=(2, N_SH, 1), tm=HALF_TP, tn=SQ_ROWS, tk=D,
                      a_index=lambda i, j, q: (i, a_blk), w_block=(None, SQ_ROWS, D),
                      w_index=lambda i, j, q: (j, kk, 0), out_cols=D, hook=hook)

    a_dec, u_in = _rnn_gates_fwd(z, conv_w_full, conv_b, w_ra_full, w_ri_full, b_ra, b_ri, lru_lambda)
    hr = _scan_fwd(a_dec, u_in)
    ya_in = _mul_silu_fwd(hr, z, OFF_GR, name="gate_a_fwd")
    y2 = sq_nn(ya_in, 0, zero_bias, "mm_ya", 2 * D, lambda i, j: (i, j))
    yb_in = _mul_silu_fwd(o, z, OFF_GA, name="gate_b_fwd")
    y2 = sq_nn(yb_in, 1, zero_bias, "mm_yb", 2 * D, lambda i, j: (i, D // CW + j), carried=y2)
    mixed = _merge_fwd(y2, z)
    out = sq_nn(mixed, 2, b_o, "mm_out", D, lambda i, j: (i, j))
    dr, sums_o = _ln_out_loss(h32, out, loss_target[0], ln_g, ln_b)

    def sq_tn(a, b, b_blk0, kk, name, carried=None):
        return _mm_tn(a, b, name=name, grid=(N_SH, D // CW), tm=SQ_ROWS, tn=CW, a_index=lambda i, j: (0, i),
                      b_index=lambda i, j: (0, b_blk0 + j), out_shape=(N_SH, PACK_ROWS, D),
                      out_block=(None, SQ_ROWS, CW), out_index=lambda i, j: (i, kk, j), carried=carried)[0]

    gsq = sq_tn(mixed, dr, 0, 2, "mm_dwo")
    dmix = sq_nt(dr, 0, 2, "mm_dmix")[0]
    dy2, dz = _merge_bwd(dmix, y2, z)
    gsq = sq_tn(ya_in, dy2, 0, 0, "mm_dwrnn", carried=gsq)
    gsq = sq_tn(yb_in, dy2, D // CW, 1, "mm_dwattn", carried=gsq)
    dya_in = sq_nt(dy2, 0, 0, "mm_dyain")[0]
    dhr, dz = _mul_silu_bwd(dya_in, hr, z, OFF_GR, dz, name="gate_a_bwd")
    lam_s = _scan_bwd(a_dec, dhr)
    dz, gsq, sums_r = _rnn_gates_bwd(z, lam_s, hr, conv_w_full, conv_b, w_ra_full, w_ri_full, b_ra, b_ri, lru_lambda, dz, gsq)
    dyb_in = sq_nt(dy2, 1, 1, "mm_dybin")[0]
    do, dz, gsq, got_sq = _mul_silu_bwd(dyb_in, o, z, OFF_GA, dz, name="gate_b_bwd", hook=_hook_pair(gsq, True))
    own_sq, s16_sq = _pair_add(gsq, got_sq, pos, name="red_w_sq_add", tr=208, g_has_both_halves=True)
    dz, dk_rot, dv32, dsink, s16_sq, oth_sq = _attn_bwd(q, k, v, sinks[0], do, o, lse, cos_t, sin_t, dz,
                                                        hook=_hook_scatter(s16_sq))
    dz = _rope_bwd_k(dk_rot, dv32, cos_t, sin_t, dz)
    red_sq = _sum_chips(own_sq, oth_sq, pos, name="red_w_sq_sum", tr=208)
    g_b_in, g_sq = _colsum(dz, name="colsum_dz", tn=TN_IN, hook=_hook_halves(red_sq))

    def dwin(half_idx, name, hook=None):
        return _mm_tn(h16, dz, name=name, grid=(1, N_SH * PER_IN), tm=D // 2, tn=TN_IN,
                      a_index=lambda i, j, p: (0, p[half_idx]), b_index=lambda i, j, p: (0, j),
                      out_shape=(N_SH, D // 2, W_IN_COLS), out_block=(None, D // 2, TN_IN),
                      out_index=lambda i, j, p: (j // PER_IN, 0, j % PER_IN), prefetch=(pos,), hook=hook)

    gin_sib = dwin(2, "mm_dwin_sib")[0]
    gin_own, gin_sib, got_in = dwin(0, "mm_dwin_own", hook=_hook_pair(gin_sib, False))
    own_in, s16_in = _pair_add(gin_own, got_in, pos, name="red_w_in_add", tr=128, g_has_both_halves=False)
    dhz, s16_in, oth_in = _mm_nt(dz, w_in4, name="mm_dhz", grid=(2, 2, N_SH), tm=HALF_TP, tn=D // 2, tk=W_IN_COLS,
                                 a_index=lambda i, j, q: (i, q), w_block=(None, D // 2, W_IN_COLS),
                                 w_index=lambda i, j, q: (q, j, 0), out_cols=D, hook=_hook_scatter(s16_in))
    red_in = _sum_chips(own_in, oth_in, pos, name="red_w_in_sum", tr=128)
    g_x, g_meta_local, sums_e, g_in = _ln_emb_bwd(x[0], meta_full, ln_emb_g2, dr, dhz, hook=_hook_halves(red_in))

    d_in, m_in, v_in = _adamw(w_in[0], g_in, m_w_in[0], v_w_in[0], name="adamw_w_in", tr=128)
    mpack = jnp.concatenate([m_w_rnn_out[0], m_w_attn_out[0], m_w_o[0], _pack_ra(m_w_ra[0]), _pack_ra(m_w_ri[0])], axis=0)
    vpack = jnp.concatenate([v_w_rnn_out[0], v_w_attn_out[0], v_w_o[0], _pack_ra(v_w_ra[0]), _pack_ra(v_w_ri[0])], axis=0)
    d_sq, m_sq, v_sq = _adamw(wpack, g_sq, mpack, vpack, name="adamw_w_sq", tr=208)

    spack = jnp.concatenate([
        sums_e[0:1], sums_e[1:2], _row_pad(g_b_in, _B_IN_ROWS), sums_r[0:4], sums_r[4:5], sums_r[5:6], sums_r[6:7],
        sums_r[7:8], _row_pad(dsink[0:1, 0:N_Q], 1), sums_o[2:3], sums_o[0:1], sums_o[1:2], g_meta_local, sums_o[3:4],
        jnp.zeros((_SMALL_ROWS - 38, D), F32)], axis=0)
    sred = _allreduce_small(spack)
    loss = sred[37, 0]
    col0 = shard * 512
    g_conv_w = lax.dynamic_slice(sred[9:13], (0, col0), (CONV_W, 512))
    g_meta = lax.dynamic_slice(sred[21:37], (0, col0), (N_META, 512))
    small_g = {"ln_emb_g": sred[0:1], "ln_emb_b": sred[1:2], "b_in": sred[2:9], "conv_w": g_conv_w.reshape(1, D),
               "conv_b": sred[13:14], "b_ra": sred[14:15], "b_ri": sred[15:16], "lru_lambda": sred[16:17],
               "sinks": sred[17:18], "b_o": sred[18:19], "ln_g": sred[19:20], "ln_b": sred[20:21],
               "meta_tokens": g_meta.reshape(4, D)}
    small_names = list(small_g)

    def small_pack(vals):
        rows = []
        for n in small_names:
            a = vals[n]
            if n == "b_in":
                a = _row_pad(a, _B_IN_ROWS)
            elif n == "sinks":
                a = _row_pad(a, 1)
            else:
                a = a.reshape(-1, D)
            rows.append(a)
        return jnp.concatenate(rows + [jnp.zeros((24 - 22, D), F32)], axis=0)

    w_small = dict(ln_emb_g=ln_emb_g, ln_emb_b=ln_emb_b, b_in=b_in, conv_w=conv_w, conv_b=conv_b, b_ra=b_ra, b_ri=b_ri,
                   lru_lambda=lru_lambda, sinks=sinks, b_o=b_o, ln_g=ln_g, ln_b=ln_b, meta_tokens=meta_tokens)
    m_small = dict(ln_emb_g=m_ln_emb_g, ln_emb_b=m_ln_emb_b, b_in=m_b_in, conv_w=m_conv_w, conv_b=m_conv_b, b_ra=m_b_ra,
                   b_ri=m_b_ri, lru_lambda=m_lru_lambda, sinks=m_sinks, b_o=m_b_o, ln_g=m_ln_g, ln_b=m_ln_b,
                   meta_tokens=m_meta_tokens)
    v_small = dict(ln_emb_g=v_ln_emb_g, ln_emb_b=v_ln_emb_b, b_in=v_b_in, conv_w=v_conv_w, conv_b=v_conv_b, b_ra=v_b_ra,
                   b_ri=v_b_ri, lru_lambda=v_lru_lambda, sinks=v_sinks, b_o=v_b_o, ln_g=v_ln_g, ln_b=v_ln_b,
                   meta_tokens=v_meta_tokens)
    g_small_pack = jnp.concatenate([small_g[n] for n in small_names] + [jnp.zeros((2, D), F32)], axis=0)
    d_sm, m_sm, v_sm = _adamw(small_pack(w_small), g_small_pack, small_pack(m_small), small_pack(v_small),
                              name="adamw_small", tr=24)

    small_rows = {}
    r0 = 0
    for n in small_names:
        nrows = small_g[n].shape[0]
        small_rows[n] = (r0, nrows)
        r0 += nrows

    def small_out(packed, n, like):
        a, nrows = small_rows[n]
        flat = packed[a:a + nrows].reshape(-1)
        return flat[:like.size].reshape(like.shape)

    def sq_out(packed, kk, like):
        if kk < 3:
            return packed[SQ_ROWS * kk:SQ_ROWS * (kk + 1)].reshape(like.shape)
        return _unpack_ra(packed[3 * SQ_ROWS + 64 * (kk - 3):3 * SQ_ROWS + 64 * (kk - 2)], like)

    sq_idx = {"w_rnn_out": 0, "w_attn_out": 1, "w_o": 2, "w_ra": 3, "w_ri": 4}
    weights = dict(meta_tokens=meta_tokens, ln_emb_g=ln_emb_g, ln_emb_b=ln_emb_b, w_in=w_in, b_in=b_in, conv_w=conv_w,
                   conv_b=conv_b, w_ra=w_ra, b_ra=b_ra, w_ri=w_ri, b_ri=b_ri, lru_lambda=lru_lambda, sinks=sinks,
                   w_rnn_out=w_rnn_out, w_attn_out=w_attn_out, w_o=w_o, b_o=b_o, ln_g=ln_g, ln_b=ln_b)

    def outputs(big_in, big_sq, small):
        res = []
        for n, like in weights.items():
            if n == "w_in":
                res.append(big_in.reshape(like.shape))
            elif n in sq_idx:
                res.append(sq_out(big_sq, sq_idx[n], like))
            else:
                res.append(small_out(small, n, like))
        return res

    return (loss, g_x[None], *outputs(g_in, g_sq, g_small_pack), *outputs(d_in, d_sq, d_sm),
            *outputs(m_in, m_sq, m_sm), *outputs(v_in, v_sq, v_sm))
```

```python
import jax
import jax.numpy as jnp
from jax import lax
from jax.experimental import pallas as pl
from jax.experimental.pallas import tpu as pltpu

F32 = jnp.float32
BF16 = jnp.bfloat16
_MXU = jnp.bfloat16

D = 2048
SEQ = 2048
N_META = 16
BLK = 128
PAD = BLK - N_META
TP = PAD + N_META + SEQ
NBLK = TP // BLK
HALF_TP = TP // 2
N_RB = 8
RB = 256
CONV_W = 4
LRU_C = 8.0
HD = 64
N_Q = 32
N_KV = 4
GRP = 8
D_KV = 256
NEG_INF = -1e30
LN_EPS = 1e-5
ALPHA = 2.0 ** 0.25
ROPE_THETA = 10000.0
OFF_GR, OFF_Q, OFF_K, OFF_V, OFF_GA, OFF_G = 2048, 4096, 6144, 6400, 6656, 8704
D_IN = 12800
N_SH = 4
W_IN_COLS = D_IN // N_SH
TN_IN = 640
PER_IN = W_IN_COLS // TN_IN
CW = 512
RT = TP // 4
SQ_ROWS = 512
PACK_ROWS = 3 * SQ_ROWS + 128

ADAM_LR = 0.001
ADAM_B1 = 0.9
ADAM_B2 = 0.999
ADAM_EPS = 1e-08
ADAM_WD = 0.01
ADAM_STEP = 10

MESH = pl.DeviceIdType.MESH
_MIB = 1024 * 1024
_ANY = pl.BlockSpec(memory_space=pl.ANY)
_NT = (((1,), (1,)), ((), ()))
_TN = (((0,), (0,)), ((), ()))


def _sds(shape, dtype):
    return jax.ShapeDtypeStruct(shape, dtype)


def _sigmoid(x):
    return 1.0 / (1.0 + jnp.exp(-x))


def _my_pos():
    return lax.axis_index("x"), lax.axis_index("y"), lax.axis_index("c")


def _other_chips(x, y):
    return [(1 - x, y), (x, 1 - y), (1 - x, 1 - y)]


def _remote(src, dst, send_sems, recv_sems, k, dev):
    return pltpu.make_async_remote_copy(src_ref=src, dst_ref=dst, send_sem=send_sems.at[k], recv_sem=recv_sems.at[k],
                                        device_id=dev, device_id_type=MESH)


class _Hook:
    def __init__(self, carried, landing, n_sems, start, finish):
        self.carried, self.landing, self.n_sems, self.start, self.finish = list(carried), list(landing), n_sems, start, finish


def _hook_gather(buf):
    half = buf.shape[1] // 2

    def plan(o, ss, rs):
        x, y, c = _my_pos()
        chips = _other_chips(x, y)
        mine_rows = pl.ds(pl.multiple_of(c * half, 16), half)
        sib_rows = pl.ds(pl.multiple_of((1 - c) * half, 16), half)
        mine = o.at[2 * x + y, mine_rows]
        sends = [_remote(mine, mine, ss, rs, k, (px, py, c)) for k, (px, py) in enumerate(chips)]
        return sends, chips, mine_rows, sib_rows, (x, y, c)

    def start(car, land, ss, rs):
        for cp in plan(car[0], ss, rs)[0]:
            cp.start()

    def finish(car, land, ss, rs):
        o = car[0]
        sends, chips, mine_rows, sib_rows, (x, y, c) = plan(o, ss, rs)
        passed = []
        for k, (px, py) in enumerate(chips):
            got = o.at[2 * px + py, mine_rows]
            _remote(got, got, ss, rs, k, (px, py, c)).wait_recv()
            fwd = _remote(got, got, ss, rs, 3 + k, (x, y, 1 - c))
            fwd.start()
            passed.append(fwd)
        for k, (px, py) in enumerate(chips):
            got = o.at[2 * px + py, sib_rows]
            _remote(got, got, ss, rs, 3 + k, (x, y, 1 - c)).wait_recv()
        for cp in sends + passed:
            cp.wait_send()

    return _Hook([buf], [], 6, start, finish)


def _hook_pair(g, half_rows):
    n, r, cc = g.shape
    h = r // 2 if half_rows else r

    def plan(car, land, ss, rs):
        x, y, c = _my_pos()
        cps = []
        for s in range(n):
            src = car[0].at[s, pl.ds(pl.multiple_of((1 - c) * h, 8), h)] if half_rows else car[0].at[s]
            cps.append(_remote(src, land[0].at[s], ss, rs, s, (x, y, 1 - c)))
        return cps

    def start(car, land, ss, rs):
        for cp in plan(car, land, ss, rs):
            cp.start()

    def finish(car, land, ss, rs):
        cps = plan(car, land, ss, rs)
        for cp in cps:
            cp.wait_recv()
        for cp in cps:
            cp.wait_send()

    return _Hook([g], [_sds((n, h, cc), g.dtype)], n, start, finish)


def _hook_scatter(s16):
    _, h, cc = s16.shape

    def plan(car, land, ss, rs):
        x, y, c = _my_pos()
        return [_remote(car[0].at[2 * px + py], land[0].at[k], ss, rs, k, (px, py, c))
                for k, (px, py) in enumerate(_other_chips(x, y))]

    def start(car, land, ss, rs):
        for cp in plan(car, land, ss, rs):
            cp.start()

    def finish(car, land, ss, rs):
        cps = plan(car, land, ss, rs)
        for cp in cps:
            cp.wait_recv()
        for cp in cps:
            cp.wait_send()

    return _Hook([s16], [_sds((3, h, cc), s16.dtype)], 3, start, finish)


def _hook_halves(full):
    h = full.shape[0] // 2

    def half_copy(car, ss, rs, which):
        x, y, c = _my_pos()
        rows = car[0].at[pl.ds(pl.multiple_of((c + which - 2 * c * which) * h, 8), h)]
        return _remote(rows, rows, ss, rs, 0, (x, y, 1 - c))

    def start(car, land, ss, rs):
        half_copy(car, ss, rs, 0).start()

    def finish(car, land, ss, rs):
        half_copy(car, ss, rs, 1).wait_recv()
        half_copy(car, ss, rs, 0).wait_send()

    return _Hook([full], [], 1, start, finish)


def _pcall(body, *, name, grid, in_specs, out_specs, out_shape, operands, scratch=(), vmem=48, sem=None,
           prefetch=(), aliases=None, hook=None):
    n_pre, n_in, n_out, n_scr = len(prefetch), len(in_specs), len(out_specs), len(scratch)
    in_specs, out_specs, out_shape, scratch = list(in_specs), list(out_specs), list(out_shape), list(scratch)
    io_alias = {n_pre + a: b for a, b in (aliases or {}).items()}
    operands = list(operands)
    kernel_body = body
    if hook is not None:
        n_car, n_land = len(hook.carried), len(hook.landing)
        for t, arr in enumerate(hook.carried):
            io_alias[n_pre + n_in + t] = n_out + t
        in_specs += [_ANY] * n_car
        out_specs += [_ANY] * (n_car + n_land)
        out_shape += [_sds(a.shape, a.dtype) for a in hook.carried] + hook.landing
        scratch += [pltpu.SemaphoreType.DMA((hook.n_sems,)), pltpu.SemaphoreType.DMA((hook.n_sems,))]
        operands += hook.carried
        sem = ("arbitrary",) * len(grid)

        def kernel_body(*refs):
            pre, rest = refs[:n_pre], refs[n_pre:]
            ins = rest[:n_in]
            outs = rest[n_in + n_car:n_in + n_car + n_out]
            car = rest[n_in + n_car + n_out:n_in + 2 * n_car + n_out]
            land = rest[n_in + 2 * n_car + n_out:n_in + 2 * n_car + n_out + n_land]
            scr = rest[n_in + 2 * n_car + n_out + n_land:]
            send_sems, recv_sems = scr[n_scr], scr[n_scr + 1]
            first = pl.program_id(0) == 0
            last = pl.program_id(0) == grid[0] - 1
            for d in range(1, len(grid)):
                first = first & (pl.program_id(d) == 0)
                last = last & (pl.program_id(d) == grid[d] - 1)

            @pl.when(first)
            def _():
                hook.start(car, land, send_sems, recv_sems)

            body(*pre, *ins, *outs, *scr[:n_scr])

            @pl.when(last)
            def _():
                hook.finish(car, land, send_sems, recv_sems)

    params = pltpu.CompilerParams(vmem_limit_bytes=vmem * _MIB, dimension_semantics=sem,
                                  has_side_effects=hook is not None)
    if n_pre:
        call = pl.pallas_call(
            kernel_body, name=name, out_shape=out_shape, input_output_aliases=io_alias, compiler_params=params,
            grid_spec=pltpu.PrefetchScalarGridSpec(num_scalar_prefetch=n_pre, grid=grid, in_specs=in_specs,
                                                   out_specs=out_specs, scratch_shapes=scratch))
    else:
        call = pl.pallas_call(kernel_body, name=name, grid=grid, in_specs=in_specs, out_specs=out_specs,
                              out_shape=out_shape, scratch_shapes=scratch, input_output_aliases=io_alias,
                              compiler_params=params)
    return call(*prefetch, *operands)


def _comm_call(hook, *, name):
    n_car = len(hook.carried)

    def body(*refs):
        car = refs[n_car:2 * n_car]
        land = refs[2 * n_car:2 * n_car + len(hook.landing)]
        send_sems, recv_sems = refs[-2], refs[-1]
        hook.start(car, land, send_sems, recv_sems)
        hook.finish(car, land, send_sems, recv_sems)

    return pl.pallas_call(
        body, name=name, in_specs=[_ANY] * n_car, out_specs=[_ANY] * (n_car + len(hook.landing)),
        out_shape=[_sds(a.shape, a.dtype) for a in hook.carried] + hook.landing,
        input_output_aliases={t: t for t in range(n_car)},
        scratch_shapes=[pltpu.SemaphoreType.DMA((hook.n_sems,)), pltpu.SemaphoreType.DMA((hook.n_sems,))],
        compiler_params=pltpu.CompilerParams(has_side_effects=True),
    )(*hook.carried)


def _gather_small(blk):
    r, cc = blk.shape

    def body(x_ref, o_ref, send_sems, recv_sems):
        x, y, c = _my_pos()
        me = 2 * x + y
        o_ref[me] = x_ref[...]
        sends = [_remote(x_ref, o_ref.at[me], send_sems, recv_sems, k, (px, py, c))
                 for k, (px, py) in enumerate(_other_chips(x, y))]
        for cp in sends:
            cp.start()
        for k, (px, py) in enumerate(_other_chips(x, y)):
            _remote(x_ref, o_ref.at[2 * px + py], send_sems, recv_sems, k, (px, py, c)).wait_recv()
        for cp in sends:
            cp.wait_send()

    vm = pl.BlockSpec(memory_space=pltpu.VMEM)
    return pl.pallas_call(
        body, name="gather_small", in_specs=[vm], out_specs=vm, out_shape=_sds((N_SH, r, cc), blk.dtype),
        scratch_shapes=[pltpu.SemaphoreType.DMA((3,)), pltpu.SemaphoreType.DMA((3,))],
        compiler_params=pltpu.CompilerParams(has_side_effects=True),
    )(blk)


N_DEV = 8


def _allreduce_small(pack):
    r, cc = pack.shape

    def body(x_ref, o_ref, buf_ref, send_sems, recv_sems):
        x, y, c = _my_pos()
        me = 4 * x + 2 * y + c
        buf_ref[me] = x_ref[...]
        cps = []
        for k in range(1, N_DEV):
            peer = (x ^ ((k >> 2) & 1), y ^ ((k >> 1) & 1), c ^ (k & 1))
            cps.append(_remote(x_ref, buf_ref.at[me], send_sems, recv_sems, k - 1, peer))
        for cp in cps:
            cp.start()
        for k in range(1, N_DEV):
            peer = (x ^ ((k >> 2) & 1), y ^ ((k >> 1) & 1), c ^ (k & 1))
            src = 4 * peer[0] + 2 * peer[1] + peer[2]
            _remote(x_ref, buf_ref.at[src], send_sems, recv_sems, k - 1, peer).wait_recv()
        acc = buf_ref[0]
        for d in range(1, N_DEV):
            acc = acc + buf_ref[d]
        o_ref[...] = acc
        for cp in cps:
            cp.wait_send()

    vm = pl.BlockSpec(memory_space=pltpu.VMEM)
    return pl.pallas_call(
        body, name="allreduce_small", in_specs=[vm], out_specs=vm, out_shape=_sds((r, cc), F32),
        scratch_shapes=[pltpu.VMEM((N_DEV, r, cc), F32), pltpu.SemaphoreType.DMA((N_DEV - 1,)),
                        pltpu.SemaphoreType.DMA((N_DEV - 1,))],
        compiler_params=pltpu.CompilerParams(has_side_effects=True),
    )(pack)


def _mm_nn(a, w, bias, *, name, grid, tm, tn, k, a_index, w_block, w_index, out_cols, out_index, carried=None, hook=None):
    m = a.shape[0]

    def body(a_ref, w_ref, b_ref, *rest):
        o_ref = rest[-1]
        wv = w_ref[...]
        acc = jnp.dot(a_ref[...].astype(_MXU), wv.reshape(k, tn).astype(_MXU), preferred_element_type=F32)
        o_ref[...] = acc + b_ref[...]

    operands = [a, w, bias] + ([carried] if carried is not None else [])
    return _pcall(
        body, name=name, grid=grid,
        in_specs=[pl.BlockSpec((tm, k), a_index), pl.BlockSpec(w_block, w_index),
                  pl.BlockSpec((1, tn), lambda i, j: (0, j))] + ([_ANY] if carried is not None else []),
        out_specs=[pl.BlockSpec((tm, tn), out_index)], out_shape=[_sds((m, out_cols), F32)], operands=operands,
        vmem=56, sem=("parallel", "parallel"), aliases={3: 0} if carried is not None else None, hook=hook)


def _mm_nt(a, w, *, name, grid, tm, tn, tk, a_index, w_block, w_index, out_cols, hook=None):
    m = a.shape[0]
    nk = grid[2]

    def body(a_ref, w_ref, o_ref, acc_ref):
        part = lax.dot_general(a_ref[...].astype(_MXU), w_ref[...].astype(_MXU), _NT, preferred_element_type=F32)
        if nk == 1:
            o_ref[...] = part
        else:
            kidx = pl.program_id(2)

            @pl.when(kidx == 0)
            def _():
                acc_ref[...] = part

            @pl.when(kidx > 0)
            def _():
                acc_ref[...] += part

            @pl.when(kidx == nk - 1)
            def _():
                o_ref[...] = acc_ref[...]

    return _pcall(
        body, name=name, grid=grid, in_specs=[pl.BlockSpec((tm, tk), a_index), pl.BlockSpec(w_block, w_index)],
        out_specs=[pl.BlockSpec((tm, tn), lambda i, j, q: (i, j))], out_shape=[_sds((m, out_cols), F32)],
        operands=[a, w], scratch=[pltpu.VMEM((tm, tn) if nk > 1 else (8, 128), F32)], vmem=60,
        sem=("parallel", "parallel", "arbitrary"), hook=hook)


def _mm_tn(at, b, *, name, grid, tm, tn, a_index, b_index, out_shape, out_block, out_index, carried=None,
           prefetch=(), hook=None):
    t = at.shape[1]

    def body(*refs):
        a_ref, b_ref = refs[len(prefetch)], refs[len(prefetch) + 1]
        refs[-1][...] = jnp.dot(a_ref[...].astype(_MXU), b_ref[...].astype(_MXU), preferred_element_type=F32)

    operands = [at, b] + ([carried] if carried is not None else [])
    return _pcall(
        body, name=name, grid=grid,
        in_specs=[pl.BlockSpec((tm, t), a_index), pl.BlockSpec((t, tn), b_index)] + ([_ANY] if carried is not None else []),
        out_specs=[pl.BlockSpec(out_block, out_index)], out_shape=[_sds(out_shape, F32)], operands=operands,
        vmem=56, sem=("parallel", "parallel"), aliases={2: 0} if carried is not None else None, prefetch=prefetch, hook=hook)


def _mm_z_gather(h16, w_own, b_in, order):
    n_tiles = N_SH * PER_IN
    n_remote = 3 * PER_IN
    half = D // 2

    def body(order_ref, a_ref, b_ref, w_in_ref, z_ref, w_ref, wbuf, tile_sems, ss, rs):
        del w_in_ref
        j = pl.program_id(0)
        x, y, c = _my_pos()
        me = 2 * x + y
        chips = _other_chips(x, y)
        mine_rows = pl.ds(pl.multiple_of(c * half, 16), half)
        sib_rows = pl.ds(pl.multiple_of((1 - c) * half, 16), half)

        def ici(n, slot):
            px, py = chips[n % 3]
            part = w_ref.at[slot, mine_rows, pl.ds((n // 3) * TN_IN, TN_IN)]
            return _remote(part, part, ss, rs, n, (px, py, c))

        def d2d(n, rows):
            px, py = chips[n % 3]
            part = w_ref.at[2 * px + py, rows, pl.ds((n // 3) * TN_IN, TN_IN)]
            return _remote(part, part, ss, rs, n_remote + n, (x, y, 1 - c))

        def tile_copy(t):
            n = jnp.maximum(t - PER_IN, 0)
            rel = n % 3
            slot = jnp.where(t < PER_IN, me, me ^ jnp.where(rel == 0, 2, jnp.where(rel == 1, 1, 3)))
            col = pl.multiple_of(jnp.where(t < PER_IN, t, n // 3) * TN_IN, 128)
            return pltpu.make_async_copy(w_ref.at[slot, :, pl.ds(col, TN_IN)], wbuf.at[t % 2], tile_sems.at[t % 2])

        @pl.when(j == 0)
        def _():
            for n in range(n_remote):
                ici(n, me).start()
            tile_copy(0).start()

        for n in range(n_remote):
            px, py = chips[n % 3]

            @pl.when(j == n + PER_IN - 3)
            def _():
                ici(n, 2 * px + py).wait_recv()
                d2d(n, mine_rows).start()

            @pl.when(j == n + PER_IN - 2)
            def _():
                d2d(n, sib_rows).wait_recv()

        @pl.when(j + 1 < n_tiles)
        def _():
            tile_copy(j + 1).start()

        tile_copy(j).wait()
        z_ref[...] = jnp.dot(a_ref[...], wbuf[j % 2], preferred_element_type=F32) + b_ref[...]

        @pl.when(j == n_tiles - 1)
        def _():
            for n in range(n_remote):
                ici(n, me).wait_send()
                d2d(n, mine_rows).wait_send()

    def col_tile(j, o):
        n = jnp.maximum(j - PER_IN, 0)
        return 0, jnp.where(j < PER_IN, o[0] * PER_IN + j, o[1 + n % 3] * PER_IN + n // 3)

    return pl.pallas_call(
        body, name="mm_z_gather",
        grid_spec=pltpu.PrefetchScalarGridSpec(
            num_scalar_prefetch=1, grid=(n_tiles,),
            in_specs=[pl.BlockSpec((TP, D), lambda j, o: (0, 0)), pl.BlockSpec((1, TN_IN), col_tile), _ANY],
            out_specs=[pl.BlockSpec((TP, TN_IN), col_tile), _ANY],
            scratch_shapes=[pltpu.VMEM((2, D, TN_IN), BF16), pltpu.SemaphoreType.DMA((2,)),
                            pltpu.SemaphoreType.DMA((2 * n_remote,)), pltpu.SemaphoreType.DMA((2 * n_remote,))]),
        out_shape=[_sds((TP, D_IN), F32), _sds(w_own.shape, w_own.dtype)],
        input_output_aliases={3: 1},
        compiler_params=pltpu.CompilerParams(vmem_limit_bytes=60 * _MIB, dimension_semantics=("arbitrary",),
                                             has_side_effects=True),
    )(order, h16, b_in, w_own)


def _padded_rows(i, x_ref, meta_ref):
    head = jnp.concatenate([jnp.zeros((PAD, D), F32), meta_ref[...]], axis=0)
    return jnp.where(i == 0, head, x_ref[...])


def _stream_specs():
    return [pl.BlockSpec((BLK, D), lambda i: (jnp.maximum(i - 1, 0), 0)), pl.BlockSpec((N_META, D), lambda i: (0, 0))]


def _ln_emb_fwd(x, meta, g, b):
    def body(x_ref, meta_ref, g_ref, b_ref, h32_ref, h16_ref, h16t_ref):
        x = _padded_rows(pl.program_id(0), x_ref, meta_ref)
        mu = jnp.mean(x, axis=-1, keepdims=True)
        xc = x - mu
        var = jnp.mean(xc * xc, axis=-1, keepdims=True)
        y = xc * lax.rsqrt(var + LN_EPS) * g_ref[...] + b_ref[...]
        h32_ref[...] = y
        h16_ref[...] = y.astype(_MXU)
        h16t_ref[...] = y.T.astype(_MXU)

    row = pl.BlockSpec((BLK, D), lambda i: (i, 0))
    vec = pl.BlockSpec((1, D), lambda i: (0, 0))
    return _pcall(body, name="ln_emb_fwd", grid=(NBLK,), in_specs=_stream_specs() + [vec, vec],
                  out_specs=[row, row, pl.BlockSpec((D, BLK), lambda i: (0, i))],
                  out_shape=[_sds((TP, D), F32), _sds((TP, D), _MXU), _sds((D, TP), _MXU)], operands=[x, meta, g, b],
                  vmem=32, sem=("parallel",))


def _ln_emb_bwd(x, meta, g, dr, dhz, *, hook):
    def body(x_ref, meta_ref, g_ref, dr_ref, dhz_ref, dx_ref, dmeta_ref, acc_ref):
        i = pl.program_id(0)

        @pl.when(i == 0)
        def _():
            acc_ref[...] = jnp.zeros_like(acc_ref)

        x = _padded_rows(i, x_ref, meta_ref)
        mu = jnp.mean(x, axis=-1, keepdims=True)
        xc = x - mu
        var = jnp.mean(xc * xc, axis=-1, keepdims=True)
        rstd = lax.rsqrt(var + LN_EPS)
        xhat = xc * rstd
        dh = ALPHA * dr_ref[...] + dhz_ref[...]
        acc_ref[0:1, :] += jnp.sum(dh * xhat, axis=0, keepdims=True)
        acc_ref[1:2, :] += jnp.sum(dh, axis=0, keepdims=True)
        dxh = dh * g_ref[...]
        m1 = jnp.mean(dxh, axis=-1, keepdims=True)
        m2 = jnp.mean(dxh * xhat, axis=-1, keepdims=True)
        dx = rstd * (dxh - m1 - xhat * m2)
        dx_ref[...] = dx

        @pl.when(i == 0)
        def _():
            dmeta_ref[...] = dx[PAD:BLK]

    row = pl.BlockSpec((BLK, D), lambda i: (i, 0))
    vec = pl.BlockSpec((1, D), lambda i: (0, 0))
    xs, ms = _stream_specs()
    return _pcall(body, name="ln_emb_bwd", grid=(NBLK,), in_specs=[xs, ms, vec, row, row],
                  out_specs=[xs, ms, pl.BlockSpec((8, D), lambda i: (0, 0))],
                  out_shape=[_sds((SEQ, D), F32), _sds((N_META, D), F32), _sds((8, D), F32)],
                  operands=[x, meta, g, dr, dhz], vmem=32, sem=("arbitrary",), hook=hook)


def _mul_silu_fwd(a, z, off, *, name):
    def body(a_ref, z_ref, o_ref, t_ref):
        zz = z_ref[...]
        y = a_ref[...] * (zz * _sigmoid(zz))
        o_ref[...] = y.astype(_MXU)
        t_ref[...] = y.T.astype(_MXU)

    strip = pl.BlockSpec((TP, CW), lambda j: (0, j))
    return _pcall(body, name=name, grid=(D // CW,),
                  in_specs=[strip, pl.BlockSpec((TP, CW), lambda j: (0, off // CW + j))],
                  out_specs=[strip, pl.BlockSpec((CW, TP), lambda j: (j, 0))],
                  out_shape=[_sds((TP, D), _MXU), _sds((D, TP), _MXU)], operands=[a, z], vmem=56, sem=("parallel",))


def _mul_silu_bwd(dy, a, z, off, dz, *, name, hook=None):
    def body(dy_ref, a_ref, z_ref, dz_in, da_ref, dg_ref):
        zz = z_ref[...]
        sg = _sigmoid(zz)
        d = dy_ref[...]
        da_ref[...] = d * (zz * sg)
        dg_ref[...] = (d * a_ref[...] * (sg * (1.0 + zz * (1.0 - sg)))).astype(_MXU)

    blk = pl.BlockSpec((RT, CW), lambda i, j: (i, j))
    zblk = pl.BlockSpec((RT, CW), lambda i, j: (i, off // CW + j))
    return _pcall(body, name=name, grid=(TP // RT, D // CW), in_specs=[blk, blk, zblk, _ANY], out_specs=[blk, zblk],
                  out_shape=[_sds((TP, D), F32), _sds((TP, D_IN), _MXU)], operands=[dy, a, z, dz], vmem=32,
                  sem=("parallel", "parallel"), aliases={3: 1}, hook=hook)


def _merge_fwd(y2, z):
    w = 256

    def body(ya_ref, yb_ref, ga_ref, gb_ref, o_ref, t_ref):
        y = _sigmoid(ga_ref[...]) * ya_ref[...] + _sigmoid(gb_ref[...]) * yb_ref[...]
        o_ref[...] = y.astype(_MXU)
        t_ref[...] = y.T.astype(_MXU)

    nb = D // w
    strip = pl.BlockSpec((TP, w), lambda j: (0, j))
    return _pcall(body, name="merge_fwd", grid=(nb,),
                  in_specs=[strip, pl.BlockSpec((TP, w), lambda j: (0, nb + j)),
                            pl.BlockSpec((TP, w), lambda j: (0, OFF_G // w + j)),
                            pl.BlockSpec((TP, w), lambda j: (0, (OFF_G + D) // w + j))],
                  out_specs=[strip, pl.BlockSpec((w, TP), lambda j: (j, 0))],
                  out_shape=[_sds((TP, D), _MXU), _sds((D, TP), _MXU)], operands=[y2, y2, z, z], vmem=56,
                  sem=("parallel",))


def _merge_bwd(dmix, y2, z):
    nb = D // CW

    def body(dm_ref, y_ref, g_ref, dy_ref, dg_ref):
        dm = dm_ref[...]
        sg = _sigmoid(g_ref[...])
        dy_ref[...] = (dm * sg).astype(_MXU)
        dg_ref[...] = (dm * y_ref[...] * sg * (1.0 - sg)).astype(_MXU)

    blk = pl.BlockSpec((RT, CW), lambda i, j: (i, j))
    gblk = pl.BlockSpec((RT, CW), lambda i, j: (i, OFF_G // CW + j))
    return _pcall(body, name="merge_bwd", grid=(TP // RT, 2 * nb),
                  in_specs=[pl.BlockSpec((RT, CW), lambda i, j: (i, j % nb)), blk, gblk], out_specs=[blk, gblk],
                  out_shape=[_sds((TP, 2 * D), _MXU), _sds((TP, D_IN), _MXU)], operands=[dmix, y2, z], vmem=32,
                  sem=("parallel", "parallel"))


def _ln_out_loss(h32, out, target, g, b):
    def body(h_ref, o_ref, t_ref, g_ref, b_ref, dr_ref, acc_ref):
        i = pl.program_id(0)

        @pl.when(i == 0)
        def _():
            acc_ref[...] = jnp.zeros_like(acc_ref)

        r = ALPHA * h_ref[...] + o_ref[...]
        mu = jnp.mean(r, axis=-1, keepdims=True)
        rc = r - mu
        var = jnp.mean(rc * rc, axis=-1, keepdims=True)
        rstd = lax.rsqrt(var + LN_EPS)
        xhat = rc * rstd
        gg = g_ref[...]
        y = xhat * gg + b_ref[...]
        real = (i >= 1).astype(F32)
        diff = (y - t_ref[...]) * real
        dy = diff * (1.0 / D)
        dxh = dy * gg
        m1 = jnp.mean(dxh, axis=-1, keepdims=True)
        m2 = jnp.mean(dxh * xhat, axis=-1, keepdims=True)
        dr = rstd * (dxh - m1 - xhat * m2)
        dr_ref[...] = dr
        acc_ref[0:1, :] += jnp.sum(dy * xhat, axis=0, keepdims=True)
        acc_ref[1:2, :] += jnp.sum(dy, axis=0, keepdims=True)
        acc_ref[2:3, :] += jnp.sum(dr, axis=0, keepdims=True)
        acc_ref[3:4, :] += (0.5 / D) * jnp.sum(diff * diff)

    row = pl.BlockSpec((BLK, D), lambda i: (i, 0))
    vec = pl.BlockSpec((1, D), lambda i: (0, 0))
    return _pcall(body, name="ln_out_loss", grid=(NBLK,),
                  in_specs=[row, row, pl.BlockSpec((BLK, D), lambda i: (jnp.maximum(i - 1, 0), 0)), vec, vec],
                  out_specs=[row, pl.BlockSpec((8, D), lambda i: (0, 0))],
                  out_shape=[_sds((TP, D), F32), _sds((8, D), F32)], operands=[h32, out, target, g, b], vmem=32,
                  sem=("arbitrary",))


def _colsum(x, *, name, tn, hook=None):
    _, n = x.shape

    def body(x_ref, o_ref):
        o_ref[...] = jnp.sum(x_ref[...].astype(F32), axis=0, keepdims=True)

    return _pcall(body, name=name, grid=(n // tn,), in_specs=[pl.BlockSpec((TP, tn), lambda j: (0, j))],
                  out_specs=[pl.BlockSpec((1, tn), lambda j: (0, j))], out_shape=[_sds((1, n), F32)], operands=[x],
                  vmem=32, sem=("parallel",), hook=hook)


def _rnn_recompute(xr_ref, cw_ref, cb_ref, wra_ref, wri_ref, bra_ref, bri_ref, lam_ref):
    rows = lax.broadcasted_iota(jnp.int32, (TP, 1), 0)
    valid = (rows >= PAD).astype(F32)
    first = rows == PAD
    x = xr_ref[...] * valid
    cw = cw_ref[...]
    shifted = [x, pltpu.roll(x, 1, 0), pltpu.roll(x, 2, 0), pltpu.roll(x, 3, 0)]
    c = cb_ref[...] + cw[0:1, :] * shifted[0] + cw[1:2, :] * shifted[1] + cw[2:3, :] * shifted[2] + cw[3:4, :] * shifted[3]
    cm = c.astype(_MXU)
    gr = _sigmoid(jnp.dot(cm, wra_ref[...].astype(_MXU), preferred_element_type=F32) + bra_ref[...])
    gi = _sigmoid(jnp.dot(cm, wri_ref[...].astype(_MXU), preferred_element_type=F32) + bri_ref[...])
    lam = lam_ref[...]
    ls = jnp.minimum(lam, 0.0) - jnp.log(1.0 + jnp.exp(-jnp.abs(lam)))
    log_a = LRU_C * gr * ls
    a = jnp.exp(log_a)
    mult = jnp.where(first, 1.0, jnp.sqrt(1.0 - jnp.exp(2.0 * log_a)))
    return dict(valid=valid, first=first, shifted=shifted, c=c, cm=cm, gr=gr, gi=gi, ls=ls, a=a, mult=mult, lam=lam)


def _rnn_specs():
    col = pl.BlockSpec((TP, RB), lambda n: (0, n))
    vec = pl.BlockSpec((1, RB), lambda n: (0, n))
    return dict(col=col, vec=vec, cw=pl.BlockSpec((CONV_W, RB), lambda n: (0, n)),
                wblk=pl.BlockSpec((None, RB, RB), lambda n: (n, 0, 0)))


def _rnn_gates_fwd(z, conv_w, conv_b, w_ra, w_ri, b_ra, b_ri, lam):
    def body(xr_ref, cw_ref, cb_ref, wra_ref, wri_ref, bra_ref, bri_ref, lam_ref, a_ref, u_ref):
        r = _rnn_recompute(xr_ref, cw_ref, cb_ref, wra_ref, wri_ref, bra_ref, bri_ref, lam_ref)
        a_ref[...] = r["a"]
        u_ref[...] = r["mult"] * r["gi"] * r["c"] * r["valid"]

    s = _rnn_specs()
    return _pcall(body, name="rnn_gates_fwd", grid=(N_RB,),
                  in_specs=[s["col"], s["cw"], s["vec"], s["wblk"], s["wblk"], s["vec"], s["vec"], s["vec"]],
                  out_specs=[s["col"], s["col"]], out_shape=[_sds((TP, D), F32)] * 2,
                  operands=[z, conv_w, conv_b, w_ra, w_ri, b_ra, b_ri, lam], vmem=56, sem=("parallel",))


SCAN_ROWS = 272


def _scan_fwd(a, u):
    def body(a_ref, u_ref, h_ref, carry_ref):
        @pl.when(pl.program_id(0) == 0)
        def _():
            carry_ref[...] = jnp.zeros_like(carry_ref)

        def step(t, h):
            h = a_ref[t] * h + u_ref[t]
            h_ref[t] = h
            return h

        carry_ref[...] = lax.fori_loop(0, SCAN_ROWS, step, carry_ref[...], unroll=8)

    blk = pl.BlockSpec((SCAN_ROWS, 16, 128), lambda i: (i, 0, 0))
    h = _pcall(body, name="scan_fwd", grid=(TP // SCAN_ROWS,), in_specs=[blk, blk], out_specs=[blk],
               out_shape=[_sds((TP, 16, 128), F32)], operands=[a.reshape(TP, 16, 128), u.reshape(TP, 16, 128)],
               scratch=[pltpu.VMEM((16, 128), F32)], vmem=32, sem=("arbitrary",))[0]
    return h.reshape(TP, D)


def _scan_bwd(a, dh):
    nst = TP // SCAN_ROWS

    def body(a_ref, d_ref, o_ref, lam_ref, anext_ref):
        @pl.when(pl.program_id(0) == 0)
        def _():
            lam_ref[...] = jnp.zeros_like(lam_ref)
            anext_ref[...] = jnp.zeros_like(anext_ref)

        def step(q, carry):
            lam, an = carry
            t = SCAN_ROWS - 1 - q
            lam = d_ref[t] + an * lam
            o_ref[t] = lam
            return lam, a_ref[t]

        lam, an = lax.fori_loop(0, SCAN_ROWS, step, (lam_ref[...], anext_ref[...]), unroll=8)
        lam_ref[...] = lam
        anext_ref[...] = an

    blk = pl.BlockSpec((SCAN_ROWS, 16, 128), lambda i: (nst - 1 - i, 0, 0))
    o = _pcall(body, name="scan_bwd", grid=(nst,), in_specs=[blk, blk], out_specs=[blk],
               out_shape=[_sds((TP, 16, 128), F32)], operands=[a.reshape(TP, 16, 128), dh.reshape(TP, 16, 128)],
               scratch=[pltpu.VMEM((16, 128), F32), pltpu.VMEM((16, 128), F32)], vmem=32, sem=("arbitrary",))[0]
    return o.reshape(TP, D)


def _rnn_gates_bwd(z, lam_s, hr, conv_w, conv_b, w_ra, w_ri, b_ra, b_ri, lam, dz, gsq):
    def body(xr_ref, ls_ref, hr_ref, cw_ref, cb_ref, wra_ref, wri_ref, bra_ref, bri_ref, lam_ref, dz_in, gsq_in,
             dx_ref, dw_ref, sums_ref):
        r = _rnn_recompute(xr_ref, cw_ref, cb_ref, wra_ref, wri_ref, bra_ref, bri_ref, lam_ref)
        valid, c, gr, gi, a, mult = r["valid"], r["c"], r["gr"], r["gi"], r["a"], r["mult"]
        du = ls_ref[...] * valid
        da = du * pltpu.roll(hr_ref[...], 1, 0)
        d_gi = du * mult * c
        dc = du * mult * gi
        dmult = du * gi * c
        dlog_a = da * a + jnp.where(r["first"], 0.0, -dmult * a * a / mult)
        d_gr = dlog_a * (LRU_C * r["ls"])
        dls = jnp.sum(dlog_a * (LRU_C * gr), axis=0, keepdims=True)
        dpre_r = d_gr * gr * (1.0 - gr)
        dpre_i = d_gi * gi * (1.0 - gi)
        pr = dpre_r.astype(_MXU)
        pi = dpre_i.astype(_MXU)
        dwra = lax.dot_general(r["cm"], pr, _TN, preferred_element_type=F32)
        dwri = lax.dot_general(r["cm"], pi, _TN, preferred_element_type=F32)
        for s in range(N_SH):
            dw_ref[s, 0:64, :] = dwra[64 * s:64 * (s + 1)]
            dw_ref[s, 64:128, :] = dwri[64 * s:64 * (s + 1)]
        dc = dc + lax.dot_general(pr, wra_ref[...].astype(_MXU), _NT, preferred_element_type=F32)
        dc = dc + lax.dot_general(pi, wri_ref[...].astype(_MXU), _NT, preferred_element_type=F32)
        cw = cw_ref[...]
        dx = cw[0:1, :] * dc
        for k in range(1, CONV_W):
            dx = dx + cw[k:k + 1, :] * pltpu.roll(dc, TP - k, 0)
        dx_ref[...] = (dx * valid).astype(_MXU)
        for k in range(CONV_W):
            sums_ref[k:k + 1, :] = jnp.sum(dc * r["shifted"][k], axis=0, keepdims=True)
        sums_ref[4:5, :] = jnp.sum(dc, axis=0, keepdims=True)
        sums_ref[5:6, :] = jnp.sum(dpre_r, axis=0, keepdims=True)
        sums_ref[6:7, :] = jnp.sum(dpre_i, axis=0, keepdims=True)
        sums_ref[7:8, :] = dls * _sigmoid(-r["lam"])

    s = _rnn_specs()
    return _pcall(
        body, name="rnn_gates_bwd", grid=(N_RB,),
        in_specs=[s["col"], s["col"], s["col"], s["cw"], s["vec"], s["wblk"], s["wblk"], s["vec"], s["vec"], s["vec"],
                  _ANY, _ANY],
        out_specs=[s["col"], pl.BlockSpec((N_SH, 128, RB), lambda n: (0, 3 * SQ_ROWS // 128, n)),
                   pl.BlockSpec((8, RB), lambda n: (0, n))],
        out_shape=[_sds((TP, D_IN), _MXU), _sds((N_SH, PACK_ROWS, D), F32), _sds((8, D), F32)],
        operands=[z, lam_s, hr, conv_w, conv_b, w_ra, w_ri, b_ra, b_ri, lam, dz, gsq], vmem=60, sem=("parallel",),
        aliases={10: 0, 11: 1})


def _rope_tables():
    half = HD // 2
    inv = ROPE_THETA ** (-jnp.arange(half, dtype=F32) / half)
    pos = (jnp.arange(TP) - PAD).astype(F32)
    ang = pos[:, None] * inv[None, :]
    return jnp.tile(jnp.cos(ang), (1, 4)), jnp.tile(jnp.sin(ang), (1, 4))


def _rope(x, cos_t, sin_t, sign):
    w = x.shape[1]
    lane = lax.broadcasted_iota(jnp.int32, x.shape, 1)
    first = (lane % HD) < (HD // 2)
    swapped = jnp.where(first, pltpu.roll(x, w - HD // 2, 1), pltpu.roll(x, HD // 2, 1))
    ct = jnp.tile(cos_t, (1, w // 128))
    st = jnp.tile(sin_t, (1, w // 128))
    return x * ct + swapped * jnp.where(first, -sign * st, sign * st)


def _rope_fwd(z, cos_t, sin_t):
    def body(q_ref, k_ref, v_ref, c_ref, s_ref, qo_ref, ko_ref, vo_ref):
        c = c_ref[...]
        s = s_ref[...]
        qo_ref[...] = _rope(q_ref[...], c, s, 1.0).astype(_MXU)
        ko_ref[...] = _rope(k_ref[...], c, s, 1.0).astype(_MXU)
        vo_ref[...] = v_ref[...].astype(_MXU)

    tab = pl.BlockSpec((BLK, 128), lambda i: (i, 0))
    kv = pl.BlockSpec((BLK, D_KV), lambda i: (i, 0))
    return _pcall(body, name="rope_fwd", grid=(NBLK,),
                  in_specs=[pl.BlockSpec((BLK, D), lambda i: (i, OFF_Q // D)),
                            pl.BlockSpec((BLK, D_KV), lambda i: (i, OFF_K // D_KV)),
                            pl.BlockSpec((BLK, D_KV), lambda i: (i, OFF_V // D_KV)), tab, tab],
                  out_specs=[pl.BlockSpec((BLK, D), lambda i: (i, 0)), kv, kv],
                  out_shape=[_sds((TP, D), _MXU), _sds((TP, D_KV), _MXU), _sds((TP, D_KV), _MXU)],
                  operands=[z, z, z, cos_t, sin_t], vmem=32, sem=("parallel",))


def _rope_bwd_k(dk, dv, cos_t, sin_t, dz):
    def body(dk_ref, dv_ref, c_ref, s_ref, dz_in, o_ref):
        o_ref[:, 0:D_KV] = _rope(dk_ref[...], c_ref[...], s_ref[...], -1.0).astype(_MXU)
        o_ref[:, D_KV:2 * D_KV] = dv_ref[...].astype(_MXU)

    tab = pl.BlockSpec((BLK, 128), lambda i: (i, 0))
    kv = pl.BlockSpec((BLK, D_KV), lambda i: (i, 0))
    return _pcall(body, name="rope_bwd_k", grid=(NBLK,), in_specs=[kv, kv, tab, tab, _ANY],
                  out_specs=[pl.BlockSpec((BLK, 2 * D_KV), lambda i: (i, OFF_K // (2 * D_KV)))],
                  out_shape=[_sds((TP, D_IN), _MXU)], operands=[dk, dv, cos_t, sin_t, dz], vmem=32, sem=("parallel",),
                  aliases={4: 0})[0]


def _attn_mask(i):
    ql = lax.broadcasted_iota(jnp.int32, (GRP * BLK, 3 * BLK), 0) % BLK
    kk = lax.broadcasted_iota(jnp.int32, (GRP * BLK, 3 * BLK), 1)
    kl = kk % BLK
    part = kk // BLK
    meta = (part == 0) & (kl >= PAD) & ((i >= 1) | (kl <= ql))
    prev = (part == 1) & (i >= 2) & (kl > ql)
    cur = (part == 2) & (i >= 1) & (kl <= ql)
    return meta | prev | cur


def _cat_kv(refs, g):
    return jnp.concatenate([r[:, HD * g:HD * (g + 1)] for r in refs], axis=0)


def _kv_specs():
    return [pl.BlockSpec((BLK, D_KV), lambda i: (0, 0)),
            pl.BlockSpec((BLK, D_KV), lambda i: (jnp.maximum(i - 1, 0), 0)),
            pl.BlockSpec((BLK, D_KV), lambda i: (i, 0))]


def _stack_heads(ref, g, width=HD):
    return jnp.concatenate([ref[:, width * (GRP * g + j):width * (GRP * g + j + 1)] for j in range(GRP)], axis=0)


def _sink_column(sink_ref, g):
    return jnp.concatenate([jnp.full((BLK, 1), sink_ref[GRP * g + j], F32) for j in range(GRP)], axis=0)


def _attn_fwd(q, k, v, sinks, *, hook):
    def body(q_ref, k0_ref, kp_ref, kc_ref, v0_ref, vp_ref, vc_ref, sink_ref, o_ref, lse_ref):
        mask = _attn_mask(pl.program_id(0))
        for g in range(N_KV):
            q8 = _stack_heads(q_ref, g)
            sink = _sink_column(sink_ref, g)
            kcat = _cat_kv((k0_ref, kp_ref, kc_ref), g)
            s = jnp.where(mask, lax.dot_general(q8, kcat, _NT, preferred_element_type=F32) * (HD ** -0.5), NEG_INF)
            mx = jnp.maximum(jnp.max(s, -1, keepdims=True), sink)
            p = jnp.exp(s - mx)
            den = jnp.sum(p, -1, keepdims=True) + jnp.exp(sink - mx)
            o8 = jnp.dot((p * (1.0 / den)).astype(_MXU), _cat_kv((v0_ref, vp_ref, vc_ref), g), preferred_element_type=F32)
            lse8 = mx + jnp.log(den)
            for j in range(GRP):
                h = GRP * g + j
                o_ref[:, HD * h:HD * (h + 1)] = o8[BLK * j:BLK * (j + 1)]
                lse_ref[:, h:h + 1] = lse8[BLK * j:BLK * (j + 1)]

    return _pcall(body, name="attn_fwd", grid=(NBLK,),
                  in_specs=[pl.BlockSpec((BLK, D), lambda i: (i, 0))] + _kv_specs() + _kv_specs()
                           + [pl.BlockSpec(memory_space=pltpu.SMEM)],
                  out_specs=[pl.BlockSpec((BLK, D), lambda i: (i, 0)), pl.BlockSpec((BLK, N_Q), lambda i: (i, 0))],
                  out_shape=[_sds((TP, D), F32), _sds((TP, N_Q), F32)], operands=[q, k, k, k, v, v, v, sinks],
                  vmem=40, sem=("parallel",), hook=hook)


def _attn_bwd(q, k, v, sinks, do, o, lse, cos_t, sin_t, dz, *, hook):
    def body(q_ref, k0_ref, kp_ref, kc_ref, v0_ref, vp_ref, vc_ref, sink_ref, do_ref, o_ref, lse_ref, c_ref, s_ref, dz_in,
             dq_ref, dk_ref, dv_ref, dsink_ref, dqrot_ref):
        i = pl.program_id(0)

        @pl.when(i == 0)
        def _():
            dk_ref[...] = jnp.zeros_like(dk_ref)
            dv_ref[...] = jnp.zeros_like(dv_ref)
            dsink_ref[...] = jnp.zeros_like(dsink_ref)

        mask = _attn_mask(i)
        row_starts = (0, pl.multiple_of(jnp.maximum(i - 1, 0) * BLK, BLK), pl.multiple_of(i * BLK, BLK))
        scale = HD ** -0.5
        for g in range(N_KV):
            gs = slice(HD * g, HD * (g + 1))
            q8 = _stack_heads(q_ref, g)
            do8 = _stack_heads(do_ref, g)
            dom = do8.astype(_MXU)
            delta = jnp.sum(do8 * _stack_heads(o_ref, g), axis=-1, keepdims=True)
            lse8 = _stack_heads(lse_ref, g, width=1)
            dsk = -jnp.exp(_sink_column(sink_ref, g) - lse8) * delta
            for j in range(GRP):
                h = GRP * g + j
                dsink_ref[0:1, h:h + 1] += jnp.sum(dsk[BLK * j:BLK * (j + 1)], axis=0, keepdims=True)
            kcat = _cat_kv((k0_ref, kp_ref, kc_ref), g)
            s = jnp.where(mask, lax.dot_general(q8, kcat, _NT, preferred_element_type=F32) * scale, NEG_INF)
            p = jnp.exp(s - lse8)
            dp = lax.dot_general(dom, _cat_kv((v0_ref, vp_ref, vc_ref), g), _NT, preferred_element_type=F32)
            ds = (p * (dp - delta) * scale).astype(_MXU)
            dq8 = jnp.dot(ds, kcat, preferred_element_type=F32)
            dkcat = lax.dot_general(ds, q8, _TN, preferred_element_type=F32)
            dvcat = lax.dot_general(p.astype(_MXU), dom, _TN, preferred_element_type=F32)
            for part in range(3):
                rows = pl.ds(row_starts[part], BLK)
                dk_ref[rows, gs] += dkcat[BLK * part:BLK * (part + 1)]
                dv_ref[rows, gs] += dvcat[BLK * part:BLK * (part + 1)]
            for j in range(GRP):
                h = GRP * g + j
                dqrot_ref[:, HD * h:HD * (h + 1)] = dq8[BLK * j:BLK * (j + 1)]
        dq_ref[...] = _rope(dqrot_ref[...], c_ref[...], s_ref[...], -1.0).astype(_MXU)

    row = pl.BlockSpec((BLK, D), lambda i: (i, 0))
    tab = pl.BlockSpec((BLK, 128), lambda i: (i, 0))
    full_kv = pl.BlockSpec((TP, D_KV), lambda i: (0, 0))
    return _pcall(
        body, name="attn_bwd", grid=(NBLK,),
        in_specs=[row] + _kv_specs() + _kv_specs() + [pl.BlockSpec(memory_space=pltpu.SMEM), row, row,
                  pl.BlockSpec((BLK, N_Q), lambda i: (i, 0)), tab, tab, _ANY],
        out_specs=[pl.BlockSpec((BLK, D), lambda i: (i, OFF_Q // D)), full_kv, full_kv,
                   pl.BlockSpec((8, 128), lambda i: (0, 0))],
        out_shape=[_sds((TP, D_IN), _MXU), _sds((TP, D_KV), F32), _sds((TP, D_KV), F32), _sds((8, 128), F32)],
        operands=[q, k, k, k, v, v, v, sinks, do, o, lse, cos_t, sin_t, dz], scratch=[pltpu.VMEM((BLK, D), F32)],
        vmem=48, sem=("arbitrary",), aliases={13: 0}, hook=hook)


def _cast_into_slot(w32, pos, *, name, tr):
    r, cc = w32.shape

    def body(pos_ref, w_ref, o_ref):
        o_ref[...] = w_ref[...].astype(BF16)

    return _pcall(body, name=name, grid=(r // tr,), in_specs=[pl.BlockSpec((tr, cc), lambda i, p: (i, 0))],
                  out_specs=[pl.BlockSpec((None, tr, cc), lambda i, p: (p[1], i, 0))],
                  out_shape=[_sds((N_SH, r, cc), BF16)], operands=[w32], vmem=32, sem=("parallel",), prefetch=(pos,))[0]


def _pair_add(g, got, pos, *, name, tr, g_has_both_halves):
    n, h, cc = got.shape
    nt = h // tr

    def body(pos_ref, g_ref, r_ref, own_ref, s16_ref):
        s = g_ref[...] + r_ref[...]
        s16_ref[...] = s.astype(BF16)

        @pl.when(pl.program_id(1) == pos_ref[1])
        def _():
            own_ref[...] = s

    g_index = (lambda i, s, p: (s, p[0] * nt + i, 0)) if g_has_both_halves else (lambda i, s, p: (s, i, 0))
    return _pcall(body, name=name, grid=(nt, n),
                  in_specs=[pl.BlockSpec((None, tr, cc), g_index), pl.BlockSpec((None, tr, cc), lambda i, s, p: (s, i, 0))],
                  out_specs=[pl.BlockSpec((tr, cc), lambda i, s, p: (i, 0)),
                             pl.BlockSpec((None, tr, cc), lambda i, s, p: (s, i, 0))],
                  out_shape=[_sds((h, cc), F32), _sds((n, h, cc), BF16)], operands=[g, got], vmem=40,
                  sem=("parallel", "arbitrary"), prefetch=(pos,))


def _sum_chips(own, got, pos, *, name, tr):
    h, cc = own.shape
    nt = h // tr

    def body(pos_ref, o_ref, r_ref, out_ref):
        acc = o_ref[...]
        for k in range(3):
            acc = acc + r_ref[k].astype(F32)
        out_ref[...] = acc

    return _pcall(body, name=name, grid=(nt,),
                  in_specs=[pl.BlockSpec((tr, cc), lambda i, p: (i, 0)), pl.BlockSpec((3, tr, cc), lambda i, p: (0, i, 0))],
                  out_specs=[pl.BlockSpec((tr, cc), lambda i, p: (p[0] * nt + i, 0))],
                  out_shape=[_sds((2 * h, cc), F32)], operands=[own, got], vmem=40, sem=("parallel",), prefetch=(pos,))[0]


def _adamw(w, g, m, v, *, name, tr, g_row0=0):
    r, cc = w.shape
    g_blk0 = g_row0 // tr

    def body(w_ref, g_ref, m_ref, v_ref, go_ref, d_ref, mo_ref, vo_ref):
        gg = g_ref[...]
        go_ref[...] = gg
        m_new = ADAM_B1 * m_ref[...] + (1.0 - ADAM_B1) * gg
        v_new = ADAM_B2 * v_ref[...] + (1.0 - ADAM_B2) * (gg * gg)
        m_hat = m_new / (1.0 - ADAM_B1 ** ADAM_STEP)
        v_hat = v_new / (1.0 - ADAM_B2 ** ADAM_STEP)
        d_ref[...] = -ADAM_LR * (m_hat / (jnp.sqrt(v_hat) + ADAM_EPS) + ADAM_WD * w_ref[...])
        mo_ref[...] = m_new
        vo_ref[...] = v_new

    blk = pl.BlockSpec((tr, cc), lambda i: (i, 0))
    gblk = pl.BlockSpec((tr, cc), lambda i: (g_blk0 + i, 0))
    return _pcall(body, name=name, grid=(r // tr,), in_specs=[blk, gblk, blk, blk], out_specs=[blk] * 4,
                  out_shape=[_sds((r, cc), F32)] * 4, operands=[w, g, m, v], vmem=48, sem=("parallel",))


_SMALL_ROWS = 40
_B_IN_ROWS = 7


def _row_pad(v, rows):
    flat = v.reshape(-1)
    return jnp.pad(flat, (0, rows * D - flat.shape[0])).reshape(rows, D)


def _pack_ra(w):
    return w.transpose(1, 0, 2).reshape(64, D)


def _unpack_ra(p, like):
    return p.reshape(64, N_RB, RB).transpose(1, 0, 2).reshape(like.shape)


def _gate_full(g4):
    return g4.reshape(N_SH, 64, N_RB, RB).transpose(2, 0, 1, 3).reshape(N_RB, RB, RB)


def kernel(x, meta_tokens, ln_emb_g, ln_emb_b, w_in, b_in, conv_w, conv_b, w_ra, b_ra, w_ri, b_ri, lru_lambda, sinks, w_rnn_out, w_attn_out, w_o, b_o, ln_g, ln_b, loss_target, m_meta_tokens, m_ln_emb_g, m_ln_emb_b, m_w_in, m_b_in, m_conv_w, m_conv_b, m_w_ra, m_b_ra, m_w_ri, m_b_ri, m_lru_lambda, m_sinks, m_w_rnn_out, m_w_attn_out, m_w_o, m_b_o, m_ln_g, m_ln_b, v_meta_tokens, v_ln_emb_g, v_ln_emb_b, v_w_in, v_b_in, v_conv_w, v_conv_b, v_w_ra, v_b_ra, v_w_ri, v_b_ri, v_lru_lambda, v_sinks, v_w_rnn_out, v_w_attn_out, v_w_o, v_b_o, v_ln_g, v_ln_b):
    xi, yi, ci = _my_pos()
    shard = 2 * xi + yi
    pos = jnp.stack([ci, shard, 1 - ci]).astype(jnp.int32)
    cos_t, sin_t = _rope_tables()
    zero_bias = jnp.zeros((1, D), F32)
    ln_emb_g2, ln_emb_b2 = ln_emb_g[None], ln_emb_b[None]

    small = jnp.concatenate([conv_w[0], meta_tokens, jnp.zeros((4, 512), F32)], axis=0)
    small4 = _gather_small(small)
    conv_w_full = small4[:, 0:4].transpose(1, 0, 2).reshape(CONV_W, D)
    meta_full = small4[:, 4:20].transpose(1, 0, 2).reshape(N_META, D)
    w_own = _cast_into_slot(w_in[0], pos, name="cast_w_in", tr=256)
    wpack = jnp.concatenate([w_rnn_out[0], w_attn_out[0], w_o[0], _pack_ra(w_ra[0]), _pack_ra(w_ri[0])], axis=0)
    wp_own = _cast_into_slot(wpack, pos, name="cast_w_sq", tr=208)

    h32, h16, h16_t = _ln_emb_fwd(x[0], meta_full, ln_emb_g2, ln_emb_b2)
    order = jnp.stack([shard, shard ^ 2, shard ^ 1, shard ^ 3]).astype(jnp.int32)
    z, w_in4 = _mm_z_gather(h16, w_own, b_in, order)
    q, k, v = _rope_fwd(z, cos_t, sin_t)
    o, lse, wp4 = _attn_fwd(q, k, v, sinks[0], hook=_hook_gather(wp_own))
    w_ra_full = _gate_full(wp4[:, 3 * SQ_ROWS:3 * SQ_ROWS + 64])
    w_ri_full = _gate_full(wp4[:, 3 * SQ_ROWS + 64:PACK_ROWS])

    def sq_nn(a, kk, bias, name, out_cols, out_index, carried=None):
        return _mm_nn(a, wp4, bias, name=name, grid=(2, D // CW), tm=HALF_TP, tn=CW, k=D, a_index=lambda i, j: (i, 0),
                      w_block=(N_SH, SQ_ROWS, CW), w_index=lambda i, j: (0, kk, j), out_cols=out_cols,
                      out_index=out_index, carried=carried)[0]

    def sq_nt(a, a_blk, kk, name, hook=None):
        return _mm_nt(a, wp4, name=name, grid=(2, N_SH, 1), tm=HALF_TP, tn=SQ_ROWS, tk=D,
                      a_index=lambda i, j, q: (i, a_blk), w_block=(None, SQ_ROWS, D),
                      w_index=lambda i, j, q: (j, kk, 0), out_cols=D, hook=hook)

    a_dec, u_in = _rnn_gates_fwd(z, conv_w_full, conv_b, w_ra_full, w_ri_full, b_ra, b_ri, lru_lambda)
    hr = _scan_fwd(a_dec, u_in)
    ya_in, ya_in_t = _mul_silu_fwd(hr, z, OFF_GR, name="gate_a_fwd")
    y2 = sq_nn(ya_in, 0, zero_bias, "mm_ya", 2 * D, lambda i, j: (i, j))
    yb_in, yb_in_t = _mul_silu_fwd(o, z, OFF_GA, name="gate_b_fwd")
    y2 = sq_nn(yb_in, 1, zero_bias, "mm_yb", 2 * D, lambda i, j: (i, D // CW + j), carried=y2)
    mixed, mixed_t = _merge_fwd(y2, z)
    out = sq_nn(mixed, 2, b_o, "mm_out", D, lambda i, j: (i, j))
    dr, sums_o = _ln_out_loss(h32, out, loss_target[0], ln_g, ln_b)

    def sq_tn(at, b, b_blk0, kk, name, carried=None):
        return _mm_tn(at, b, name=name, grid=(N_SH, D // CW), tm=SQ_ROWS, tn=CW, a_index=lambda i, j: (i, 0),
                      b_index=lambda i, j: (0, b_blk0 + j), out_shape=(N_SH, PACK_ROWS, D),
                      out_block=(None, SQ_ROWS, CW), out_index=lambda i, j: (i, kk, j), carried=carried)[0]

    gsq = sq_tn(mixed_t, dr, 0, 2, "mm_dwo")
    dmix = sq_nt(dr, 0, 2, "mm_dmix")[0]
    dy2, dz = _merge_bwd(dmix, y2, z)
    gsq = sq_tn(ya_in_t, dy2, 0, 0, "mm_dwrnn", carried=gsq)
    gsq = sq_tn(yb_in_t, dy2, D // CW, 1, "mm_dwattn", carried=gsq)
    dya_in = sq_nt(dy2, 0, 0, "mm_dyain")[0]
    dhr, dz = _mul_silu_bwd(dya_in, hr, z, OFF_GR, dz, name="gate_a_bwd")
    lam_s = _scan_bwd(a_dec, dhr)
    dz, gsq, sums_r = _rnn_gates_bwd(z, lam_s, hr, conv_w_full, conv_b, w_ra_full, w_ri_full, b_ra, b_ri, lru_lambda, dz, gsq)
    dyb_in = sq_nt(dy2, 1, 1, "mm_dybin")[0]
    do, dz, gsq, got_sq = _mul_silu_bwd(dyb_in, o, z, OFF_GA, dz, name="gate_b_bwd", hook=_hook_pair(gsq, True))
    own_sq, s16_sq = _pair_add(gsq, got_sq, pos, name="red_w_sq_add", tr=208, g_has_both_halves=True)
    dz, dk_rot, dv32, dsink, s16_sq, oth_sq = _attn_bwd(q, k, v, sinks[0], do, o, lse, cos_t, sin_t, dz,
                                                        hook=_hook_scatter(s16_sq))
    dz = _rope_bwd_k(dk_rot, dv32, cos_t, sin_t, dz)
    red_sq = _sum_chips(own_sq, oth_sq, pos, name="red_w_sq_sum", tr=208)
    g_b_in, g_sq = _colsum(dz, name="colsum_dz", tn=TN_IN, hook=_hook_halves(red_sq))

    def dwin(half_idx, name, hook=None):
        return _mm_tn(h16_t, dz, name=name, grid=(1, N_SH * PER_IN), tm=D // 2, tn=TN_IN,
                      a_index=lambda i, j, p: (p[half_idx], 0), b_index=lambda i, j, p: (0, j),
                      out_shape=(N_SH, D // 2, W_IN_COLS), out_block=(None, D // 2, TN_IN),
                      out_index=lambda i, j, p: (j // PER_IN, 0, j % PER_IN), prefetch=(pos,), hook=hook)

    gin_sib = dwin(2, "mm_dwin_sib")[0]
    gin_own, gin_sib, got_in = dwin(0, "mm_dwin_own", hook=_hook_pair(gin_sib, False))
    own_in, s16_in = _pair_add(gin_own, got_in, pos, name="red_w_in_add", tr=128, g_has_both_halves=False)
    dhz, s16_in, oth_in = _mm_nt(dz, w_in4, name="mm_dhz", grid=(2, 2, N_SH), tm=HALF_TP, tn=D // 2, tk=W_IN_COLS,
                                 a_index=lambda i, j, q: (i, q), w_block=(None, D // 2, W_IN_COLS),
                                 w_index=lambda i, j, q: (q, j, 0), out_cols=D, hook=_hook_scatter(s16_in))
    red_in = _sum_chips(own_in, oth_in, pos, name="red_w_in_sum", tr=128)
    g_x, g_meta_local, sums_e, g_in = _ln_emb_bwd(x[0], meta_full, ln_emb_g2, dr, dhz, hook=_hook_halves(red_in))

    big = {"w_in": [t.reshape(w_in.shape) for t in
                    _adamw(w_in[0], g_in, m_w_in[0], v_w_in[0], name="adamw_w_in", tr=128)]}
    for kk, (n, w_, m_, v_) in enumerate([("w_rnn_out", w_rnn_out, m_w_rnn_out, v_w_rnn_out),
                                          ("w_attn_out", w_attn_out, m_w_attn_out, v_w_attn_out), ("w_o", w_o, m_w_o, v_w_o)]):
        big[n] = [t.reshape(w_.shape) for t in
                  _adamw(w_[0], g_sq, m_[0], v_[0], name="adamw_" + n, tr=256, g_row0=SQ_ROWS * kk)]
    for kk, (n, w_, m_, v_) in enumerate([("w_ra", w_ra, m_w_ra, v_w_ra), ("w_ri", w_ri, m_w_ri, v_w_ri)]):
        big[n] = [_unpack_ra(t, w_) for t in
                  _adamw(_pack_ra(w_[0]), g_sq, _pack_ra(m_[0]), _pack_ra(v_[0]), name="adamw_" + n, tr=64,
                         g_row0=3 * SQ_ROWS + 64 * kk)]

    spack = jnp.concatenate([
        sums_e[0:1], sums_e[1:2], _row_pad(g_b_in, _B_IN_ROWS), sums_r[0:4], sums_r[4:5], sums_r[5:6], sums_r[6:7],
        sums_r[7:8], _row_pad(dsink[0:1, 0:N_Q], 1), sums_o[2:3], sums_o[0:1], sums_o[1:2], g_meta_local, sums_o[3:4],
        jnp.zeros((_SMALL_ROWS - 38, D), F32)], axis=0)
    sred = _allreduce_small(spack)
    loss = sred[37, 0]
    col0 = shard * 512
    g_conv_w = lax.dynamic_slice(sred[9:13], (0, col0), (CONV_W, 512))
    g_meta = lax.dynamic_slice(sred[21:37], (0, col0), (N_META, 512))
    small_g = {"ln_emb_g": sred[0:1], "ln_emb_b": sred[1:2], "b_in": sred[2:9], "conv_w": g_conv_w.reshape(1, D),
               "conv_b": sred[13:14], "b_ra": sred[14:15], "b_ri": sred[15:16], "lru_lambda": sred[16:17],
               "sinks": sred[17:18], "b_o": sred[18:19], "ln_g": sred[19:20], "ln_b": sred[20:21],
               "meta_tokens": g_meta.reshape(4, D)}
    small_names = list(small_g)

    def small_pack(vals):
        rows = []
        for n in small_names:
            a = vals[n]
            if n == "b_in":
                a = _row_pad(a, _B_IN_ROWS)
            elif n == "sinks":
                a = _row_pad(a, 1)
            else:
                a = a.reshape(-1, D)
            rows.append(a)
        return jnp.concatenate(rows + [jnp.zeros((24 - 22, D), F32)], axis=0)

    w_small = dict(ln_emb_g=ln_emb_g, ln_emb_b=ln_emb_b, b_in=b_in, conv_w=conv_w, conv_b=conv_b, b_ra=b_ra, b_ri=b_ri,
                   lru_lambda=lru_lambda, sinks=sinks, b_o=b_o, ln_g=ln_g, ln_b=ln_b, meta_tokens=meta_tokens)
    m_small = dict(ln_emb_g=m_ln_emb_g, ln_emb_b=m_ln_emb_b, b_in=m_b_in, conv_w=m_conv_w, conv_b=m_conv_b, b_ra=m_b_ra,
                   b_ri=m_b_ri, lru_lambda=m_lru_lambda, sinks=m_sinks, b_o=m_b_o, ln_g=m_ln_g, ln_b=m_ln_b,
                   meta_tokens=m_meta_tokens)
    v_small = dict(ln_emb_g=v_ln_emb_g, ln_emb_b=v_ln_emb_b, b_in=v_b_in, conv_w=v_conv_w, conv_b=v_conv_b, b_ra=v_b_ra,
                   b_ri=v_b_ri, lru_lambda=v_lru_lambda, sinks=v_sinks, b_o=v_b_o, ln_g=v_ln_g, ln_b=v_ln_b,
                   meta_tokens=v_meta_tokens)
    g_small_pack = jnp.concatenate([small_g[n] for n in small_names] + [jnp.zeros((2, D), F32)], axis=0)
    small_res = _adamw(small_pack(w_small), g_small_pack, small_pack(m_small), small_pack(v_small),
                       name="adamw_small", tr=24)

    small_rows = {}
    r0 = 0
    for n in small_names:
        nrows = small_g[n].shape[0]
        small_rows[n] = (r0, nrows)
        r0 += nrows

    def small_out(packed, n, like):
        a, nrows = small_rows[n]
        flat = packed[a:a + nrows].reshape(-1)
        return flat[:like.size].reshape(like.shape)

    weights = dict(meta_tokens=meta_tokens, ln_emb_g=ln_emb_g, ln_emb_b=ln_emb_b, w_in=w_in, b_in=b_in, conv_w=conv_w,
                   conv_b=conv_b, w_ra=w_ra, b_ra=b_ra, w_ri=w_ri, b_ri=b_ri, lru_lambda=lru_lambda, sinks=sinks,
                   w_rnn_out=w_rnn_out, w_attn_out=w_attn_out, w_o=w_o, b_o=b_o, ln_g=ln_g, ln_b=ln_b)

    def outputs(which):
        return [big[n][which] if n in big else small_out(small_res[which], n, like) for n, like in weights.items()]

    return (loss, g_x[None], *outputs(0), *outputs(1), *outputs(2), *outputs(3))
```

```python
import jax
import jax.numpy as jnp
from jax import lax
from jax.experimental import pallas as pl
from jax.experimental.pallas import tpu as pltpu

F32 = jnp.float32
BF16 = jnp.bfloat16
_MXU = jnp.bfloat16

D = 2048
SEQ = 2048
N_META = 16
BLK = 128
PAD = BLK - N_META
TP = PAD + N_META + SEQ
NBLK = TP // BLK
HALF_TP = TP // 2
N_RB = 8
RB = 256
CONV_W = 4
LRU_C = 8.0
HD = 64
N_Q = 32
N_KV = 4
GRP = 8
D_KV = 256
NEG_INF = -1e30
LN_EPS = 1e-5
ALPHA = 2.0 ** 0.25
ROPE_THETA = 10000.0
OFF_GR, OFF_Q, OFF_K, OFF_V, OFF_GA, OFF_G = 2048, 4096, 6144, 6400, 6656, 8704
D_IN = 12800
N_SH = 4
W_IN_COLS = D_IN // N_SH
TN_IN = 640
PER_IN = W_IN_COLS // TN_IN
CW = 512
RT = TP // 4
SQ_ROWS = 512
PACK_ROWS = 3 * SQ_ROWS + 128

ADAM_LR = 0.001
ADAM_B1 = 0.9
ADAM_B2 = 0.999
ADAM_EPS = 1e-08
ADAM_WD = 0.01
ADAM_STEP = 10

MESH = pl.DeviceIdType.MESH
_MIB = 1024 * 1024
_ANY = pl.BlockSpec(memory_space=pl.ANY)
_NT = (((1,), (1,)), ((), ()))
_TN = (((0,), (0,)), ((), ()))


def _sds(shape, dtype):
    return jax.ShapeDtypeStruct(shape, dtype)


def _sigmoid(x):
    return 1.0 / (1.0 + jnp.exp(-x))


def _my_pos():
    return lax.axis_index("x"), lax.axis_index("y"), lax.axis_index("c")


def _other_chips(x, y):
    return [(1 - x, y), (x, 1 - y), (1 - x, 1 - y)]


def _remote(src, dst, send_sems, recv_sems, k, dev):
    return pltpu.make_async_remote_copy(src_ref=src, dst_ref=dst, send_sem=send_sems.at[k], recv_sem=recv_sems.at[k],
                                        device_id=dev, device_id_type=MESH)


class _Hook:
    def __init__(self, carried, landing, n_sems, start, finish):
        self.carried, self.landing, self.n_sems, self.start, self.finish = list(carried), list(landing), n_sems, start, finish


def _hook_gather(buf):
    half = buf.shape[1] // 2

    def plan(o, ss, rs):
        x, y, c = _my_pos()
        chips = _other_chips(x, y)
        mine_rows = pl.ds(pl.multiple_of(c * half, 16), half)
        sib_rows = pl.ds(pl.multiple_of((1 - c) * half, 16), half)
        mine = o.at[2 * x + y, mine_rows]
        sends = [_remote(mine, mine, ss, rs, k, (px, py, c)) for k, (px, py) in enumerate(chips)]
        return sends, chips, mine_rows, sib_rows, (x, y, c)

    def start(car, land, ss, rs):
        for cp in plan(car[0], ss, rs)[0]:
            cp.start()

    def finish(car, land, ss, rs):
        o = car[0]
        sends, chips, mine_rows, sib_rows, (x, y, c) = plan(o, ss, rs)
        passed = []
        for k, (px, py) in enumerate(chips):
            got = o.at[2 * px + py, mine_rows]
            _remote(got, got, ss, rs, k, (px, py, c)).wait_recv()
            fwd = _remote(got, got, ss, rs, 3 + k, (x, y, 1 - c))
            fwd.start()
            passed.append(fwd)
        for k, (px, py) in enumerate(chips):
            got = o.at[2 * px + py, sib_rows]
            _remote(got, got, ss, rs, 3 + k, (x, y, 1 - c)).wait_recv()
        for cp in sends + passed:
            cp.wait_send()

    return _Hook([buf], [], 6, start, finish)


def _hook_pair(g, half_rows):
    n, r, cc = g.shape
    h = r // 2 if half_rows else r

    def plan(car, land, ss, rs):
        x, y, c = _my_pos()
        cps = []
        for s in range(n):
            src = car[0].at[s, pl.ds(pl.multiple_of((1 - c) * h, 8), h)] if half_rows else car[0].at[s]
            cps.append(_remote(src, land[0].at[s], ss, rs, s, (x, y, 1 - c)))
        return cps

    def start(car, land, ss, rs):
        for cp in plan(car, land, ss, rs):
            cp.start()

    def finish(car, land, ss, rs):
        cps = plan(car, land, ss, rs)
        for cp in cps:
            cp.wait_recv()
        for cp in cps:
            cp.wait_send()

    return _Hook([g], [_sds((n, h, cc), g.dtype)], n, start, finish)


def _hook_scatter(s16):
    _, h, cc = s16.shape

    def plan(car, land, ss, rs):
        x, y, c = _my_pos()
        return [_remote(car[0].at[2 * px + py], land[0].at[k], ss, rs, k, (px, py, c))
                for k, (px, py) in enumerate(_other_chips(x, y))]

    def start(car, land, ss, rs):
        for cp in plan(car, land, ss, rs):
            cp.start()

    def finish(car, land, ss, rs):
        cps = plan(car, land, ss, rs)
        for cp in cps:
            cp.wait_recv()
        for cp in cps:
            cp.wait_send()

    return _Hook([s16], [_sds((3, h, cc), s16.dtype)], 3, start, finish)


def _hook_halves(full):
    h = full.shape[0] // 2

    def half_copy(car, ss, rs, which):
        x, y, c = _my_pos()
        rows = car[0].at[pl.ds(pl.multiple_of((c + which - 2 * c * which) * h, 8), h)]
        return _remote(rows, rows, ss, rs, 0, (x, y, 1 - c))

    def start(car, land, ss, rs):
        half_copy(car, ss, rs, 0).start()

    def finish(car, land, ss, rs):
        half_copy(car, ss, rs, 1).wait_recv()
        half_copy(car, ss, rs, 0).wait_send()

    return _Hook([full], [], 1, start, finish)


def _pcall(body, *, name, grid, in_specs, out_specs, out_shape, operands, scratch=(), vmem=48, sem=None,
           prefetch=(), aliases=None, hook=None):
    n_pre, n_in, n_out, n_scr = len(prefetch), len(in_specs), len(out_specs), len(scratch)
    in_specs, out_specs, out_shape, scratch = list(in_specs), list(out_specs), list(out_shape), list(scratch)
    io_alias = {n_pre + a: b for a, b in (aliases or {}).items()}
    operands = list(operands)
    kernel_body = body
    if hook is not None:
        n_car, n_land = len(hook.carried), len(hook.landing)
        for t, arr in enumerate(hook.carried):
            io_alias[n_pre + n_in + t] = n_out + t
        in_specs += [_ANY] * n_car
        out_specs += [_ANY] * (n_car + n_land)
        out_shape += [_sds(a.shape, a.dtype) for a in hook.carried] + hook.landing
        scratch += [pltpu.SemaphoreType.DMA((hook.n_sems,)), pltpu.SemaphoreType.DMA((hook.n_sems,))]
        operands += hook.carried
        sem = ("arbitrary",) * len(grid)

        def kernel_body(*refs):
            pre, rest = refs[:n_pre], refs[n_pre:]
            ins = rest[:n_in]
            outs = rest[n_in + n_car:n_in + n_car + n_out]
            car = rest[n_in + n_car + n_out:n_in + 2 * n_car + n_out]
            land = rest[n_in + 2 * n_car + n_out:n_in + 2 * n_car + n_out + n_land]
            scr = rest[n_in + 2 * n_car + n_out + n_land:]
            send_sems, recv_sems = scr[n_scr], scr[n_scr + 1]
            first = pl.program_id(0) == 0
            last = pl.program_id(0) == grid[0] - 1
            for d in range(1, len(grid)):
                first = first & (pl.program_id(d) == 0)
                last = last & (pl.program_id(d) == grid[d] - 1)

            @pl.when(first)
            def _():
                hook.start(car, land, send_sems, recv_sems)

            body(*pre, *ins, *outs, *scr[:n_scr])

            @pl.when(last)
            def _():
                hook.finish(car, land, send_sems, recv_sems)

    params = pltpu.CompilerParams(vmem_limit_bytes=vmem * _MIB, dimension_semantics=sem,
                                  has_side_effects=hook is not None)
    if n_pre:
        call = pl.pallas_call(
            kernel_body, name=name, out_shape=out_shape, input_output_aliases=io_alias, compiler_params=params,
            grid_spec=pltpu.PrefetchScalarGridSpec(num_scalar_prefetch=n_pre, grid=grid, in_specs=in_specs,
                                                   out_specs=out_specs, scratch_shapes=scratch))
    else:
        call = pl.pallas_call(kernel_body, name=name, grid=grid, in_specs=in_specs, out_specs=out_specs,
                              out_shape=out_shape, scratch_shapes=scratch, input_output_aliases=io_alias,
                              compiler_params=params)
    return call(*prefetch, *operands)


def _comm_call(hook, *, name):
    n_car = len(hook.carried)

    def body(*refs):
        car = refs[n_car:2 * n_car]
        land = refs[2 * n_car:2 * n_car + len(hook.landing)]
        send_sems, recv_sems = refs[-2], refs[-1]
        hook.start(car, land, send_sems, recv_sems)
        hook.finish(car, land, send_sems, recv_sems)

    return pl.pallas_call(
        body, name=name, in_specs=[_ANY] * n_car, out_specs=[_ANY] * (n_car + len(hook.landing)),
        out_shape=[_sds(a.shape, a.dtype) for a in hook.carried] + hook.landing,
        input_output_aliases={t: t for t in range(n_car)},
        scratch_shapes=[pltpu.SemaphoreType.DMA((hook.n_sems,)), pltpu.SemaphoreType.DMA((hook.n_sems,))],
        compiler_params=pltpu.CompilerParams(has_side_effects=True),
    )(*hook.carried)


def _gather_small(blk):
    r, cc = blk.shape

    def body(x_ref, o_ref, send_sems, recv_sems):
        x, y, c = _my_pos()
        me = 2 * x + y
        o_ref[me] = x_ref[...]
        sends = [_remote(x_ref, o_ref.at[me], send_sems, recv_sems, k, (px, py, c))
                 for k, (px, py) in enumerate(_other_chips(x, y))]
        for cp in sends:
            cp.start()
        for k, (px, py) in enumerate(_other_chips(x, y)):
            _remote(x_ref, o_ref.at[2 * px + py], send_sems, recv_sems, k, (px, py, c)).wait_recv()
        for cp in sends:
            cp.wait_send()

    vm = pl.BlockSpec(memory_space=pltpu.VMEM)
    return pl.pallas_call(
        body, name="gather_small", in_specs=[vm], out_specs=vm, out_shape=_sds((N_SH, r, cc), blk.dtype),
        scratch_shapes=[pltpu.SemaphoreType.DMA((3,)), pltpu.SemaphoreType.DMA((3,))],
        compiler_params=pltpu.CompilerParams(has_side_effects=True),
    )(blk)


N_DEV = 8


def _allreduce_small(pack):
    r, cc = pack.shape

    def body(x_ref, o_ref, buf_ref, send_sems, recv_sems):
        x, y, c = _my_pos()
        me = 4 * x + 2 * y + c
        buf_ref[me] = x_ref[...]
        cps = []
        for k in range(1, N_DEV):
            peer = (x ^ ((k >> 2) & 1), y ^ ((k >> 1) & 1), c ^ (k & 1))
            cps.append(_remote(x_ref, buf_ref.at[me], send_sems, recv_sems, k - 1, peer))
        for cp in cps:
            cp.start()
        for k in range(1, N_DEV):
            peer = (x ^ ((k >> 2) & 1), y ^ ((k >> 1) & 1), c ^ (k & 1))
            src = 4 * peer[0] + 2 * peer[1] + peer[2]
            _remote(x_ref, buf_ref.at[src], send_sems, recv_sems, k - 1, peer).wait_recv()
        acc = buf_ref[0]
        for d in range(1, N_DEV):
            acc = acc + buf_ref[d]
        o_ref[...] = acc
        for cp in cps:
            cp.wait_send()

    vm = pl.BlockSpec(memory_space=pltpu.VMEM)
    return pl.pallas_call(
        body, name="allreduce_small", in_specs=[vm], out_specs=vm, out_shape=_sds((r, cc), F32),
        scratch_shapes=[pltpu.VMEM((N_DEV, r, cc), F32), pltpu.SemaphoreType.DMA((N_DEV - 1,)),
                        pltpu.SemaphoreType.DMA((N_DEV - 1,))],
        compiler_params=pltpu.CompilerParams(has_side_effects=True),
    )(pack)


def _mm_nn(a, w, bias, *, name, grid, tm, tn, k, a_index, w_block, w_index, out_cols, out_index, carried=None, hook=None):
    m = a.shape[0]

    def body(a_ref, w_ref, b_ref, *rest):
        o_ref = rest[-1]
        wv = w_ref[...]
        acc = jnp.dot(a_ref[...].astype(_MXU), wv.reshape(k, tn).astype(_MXU), preferred_element_type=F32)
        o_ref[...] = acc + b_ref[...]

    operands = [a, w, bias] + ([carried] if carried is not None else [])
    return _pcall(
        body, name=name, grid=grid,
        in_specs=[pl.BlockSpec((tm, k), a_index), pl.BlockSpec(w_block, w_index),
                  pl.BlockSpec((1, tn), lambda i, j: (0, j))] + ([_ANY] if carried is not None else []),
        out_specs=[pl.BlockSpec((tm, tn), out_index)], out_shape=[_sds((m, out_cols), F32)], operands=operands,
        vmem=56, sem=("parallel", "parallel"), aliases={3: 0} if carried is not None else None, hook=hook)


def _mm_nt(a, w, *, name, grid, tm, tn, tk, a_index, w_block, w_index, out_cols, hook=None):
    m = a.shape[0]
    nk = grid[2]

    def body(a_ref, w_ref, o_ref, acc_ref):
        part = lax.dot_general(a_ref[...].astype(_MXU), w_ref[...].astype(_MXU), _NT, preferred_element_type=F32)
        if nk == 1:
            o_ref[...] = part
        else:
            kidx = pl.program_id(2)

            @pl.when(kidx == 0)
            def _():
                acc_ref[...] = part

            @pl.when(kidx > 0)
            def _():
                acc_ref[...] += part

            @pl.when(kidx == nk - 1)
            def _():
                o_ref[...] = acc_ref[...]

    return _pcall(
        body, name=name, grid=grid, in_specs=[pl.BlockSpec((tm, tk), a_index), pl.BlockSpec(w_block, w_index)],
        out_specs=[pl.BlockSpec((tm, tn), lambda i, j, q: (i, j))], out_shape=[_sds((m, out_cols), F32)],
        operands=[a, w], scratch=[pltpu.VMEM((tm, tn) if nk > 1 else (8, 128), F32)], vmem=60,
        sem=("parallel", "parallel", "arbitrary"), hook=hook)


def _mm_tn(at, b, *, name, grid, tm, tn, a_index, b_index, out_shape, out_block, out_index, carried=None,
           prefetch=(), hook=None):
    t = at.shape[1]

    def body(*refs):
        a_ref, b_ref = refs[len(prefetch)], refs[len(prefetch) + 1]
        refs[-1][...] = jnp.dot(a_ref[...].astype(_MXU), b_ref[...].astype(_MXU), preferred_element_type=F32)

    operands = [at, b] + ([carried] if carried is not None else [])
    return _pcall(
        body, name=name, grid=grid,
        in_specs=[pl.BlockSpec((tm, t), a_index), pl.BlockSpec((t, tn), b_index)] + ([_ANY] if carried is not None else []),
        out_specs=[pl.BlockSpec(out_block, out_index)], out_shape=[_sds(out_shape, F32)], operands=operands,
        vmem=56, sem=("parallel", "parallel"), aliases={2: 0} if carried is not None else None, prefetch=prefetch, hook=hook)


def _mm_z_gather(h16, w_own, b_in, order):
    n_tiles = N_SH * PER_IN
    n_remote = 3 * PER_IN
    half = D // 2

    def body(order_ref, a_ref, b_ref, w_in_ref, z_ref, w_ref, wbuf, tile_sems, ss, rs):
        del w_in_ref
        j = pl.program_id(0)
        x, y, c = _my_pos()
        me = 2 * x + y
        chips = _other_chips(x, y)
        mine_rows = pl.ds(pl.multiple_of(c * half, 16), half)
        sib_rows = pl.ds(pl.multiple_of((1 - c) * half, 16), half)

        def ici(n, slot):
            px, py = chips[n % 3]
            part = w_ref.at[slot, mine_rows, pl.ds((n // 3) * TN_IN, TN_IN)]
            return _remote(part, part, ss, rs, n, (px, py, c))

        def d2d(n, rows):
            px, py = chips[n % 3]
            part = w_ref.at[2 * px + py, rows, pl.ds((n // 3) * TN_IN, TN_IN)]
            return _remote(part, part, ss, rs, n_remote + n, (x, y, 1 - c))

        def tile_copy(t):
            n = jnp.maximum(t - PER_IN, 0)
            rel = n % 3
            slot = jnp.where(t < PER_IN, me, me ^ jnp.where(rel == 0, 2, jnp.where(rel == 1, 1, 3)))
            col = pl.multiple_of(jnp.where(t < PER_IN, t, n // 3) * TN_IN, 128)
            return pltpu.make_async_copy(w_ref.at[slot, :, pl.ds(col, TN_IN)], wbuf.at[t % 2], tile_sems.at[t % 2])

        @pl.when(j == 0)
        def _():
            for n in range(n_remote):
                ici(n, me).start()
            tile_copy(0).start()

        for n in range(n_remote):
            px, py = chips[n % 3]

            @pl.when(j == n + PER_IN - 3)
            def _():
                ici(n, 2 * px + py).wait_recv()
                d2d(n, mine_rows).start()

            @pl.when(j == n + PER_IN - 2)
            def _():
                d2d(n, sib_rows).wait_recv()

        @pl.when(j + 1 < n_tiles)
        def _():
            tile_copy(j + 1).start()

        tile_copy(j).wait()
        z_ref[...] = jnp.dot(a_ref[...], wbuf[j % 2], preferred_element_type=F32) + b_ref[...]

        @pl.when(j == n_tiles - 1)
        def _():
            for n in range(n_remote):
                ici(n, me).wait_send()
                d2d(n, mine_rows).wait_send()

    def col_tile(j, o):
        n = jnp.maximum(j - PER_IN, 0)
        return 0, jnp.where(j < PER_IN, o[0] * PER_IN + j, o[1 + n % 3] * PER_IN + n // 3)

    return pl.pallas_call(
        body, name="mm_z_gather",
        grid_spec=pltpu.PrefetchScalarGridSpec(
            num_scalar_prefetch=1, grid=(n_tiles,),
            in_specs=[pl.BlockSpec((TP, D), lambda j, o: (0, 0)), pl.BlockSpec((1, TN_IN), col_tile), _ANY],
            out_specs=[pl.BlockSpec((TP, TN_IN), col_tile), _ANY],
            scratch_shapes=[pltpu.VMEM((2, D, TN_IN), BF16), pltpu.SemaphoreType.DMA((2,)),
                            pltpu.SemaphoreType.DMA((2 * n_remote,)), pltpu.SemaphoreType.DMA((2 * n_remote,))]),
        out_shape=[_sds((TP, D_IN), F32), _sds(w_own.shape, w_own.dtype)],
        input_output_aliases={3: 1},
        compiler_params=pltpu.CompilerParams(vmem_limit_bytes=60 * _MIB, dimension_semantics=("arbitrary",),
                                             has_side_effects=True),
    )(order, h16, b_in, w_own)


def _padded_rows(i, x_ref, meta_ref):
    head = jnp.concatenate([jnp.zeros((PAD, D), F32), meta_ref[...]], axis=0)
    return jnp.where(i == 0, head, x_ref[...])


def _stream_specs():
    return [pl.BlockSpec((BLK, D), lambda i: (jnp.maximum(i - 1, 0), 0)), pl.BlockSpec((N_META, D), lambda i: (0, 0))]


def _ln_emb_fwd(x, meta, g, b):
    def body(x_ref, meta_ref, g_ref, b_ref, h32_ref, h16_ref, h16t_ref):
        x = _padded_rows(pl.program_id(0), x_ref, meta_ref)
        mu = jnp.mean(x, axis=-1, keepdims=True)
        xc = x - mu
        var = jnp.mean(xc * xc, axis=-1, keepdims=True)
        y = xc * lax.rsqrt(var + LN_EPS) * g_ref[...] + b_ref[...]
        h32_ref[...] = y
        h16_ref[...] = y.astype(_MXU)
        h16t_ref[...] = y.T.astype(_MXU)

    row = pl.BlockSpec((BLK, D), lambda i: (i, 0))
    vec = pl.BlockSpec((1, D), lambda i: (0, 0))
    return _pcall(body, name="ln_emb_fwd", grid=(NBLK,), in_specs=_stream_specs() + [vec, vec],
                  out_specs=[row, row, pl.BlockSpec((D, BLK), lambda i: (0, i))],
                  out_shape=[_sds((TP, D), F32), _sds((TP, D), _MXU), _sds((D, TP), _MXU)], operands=[x, meta, g, b],
                  vmem=32, sem=("parallel",))


def _ln_emb_bwd(x, meta, g, dr, dhz, *, hook):
    def body(x_ref, meta_ref, g_ref, dr_ref, dhz_ref, dx_ref, dmeta_ref, acc_ref):
        i = pl.program_id(0)

        @pl.when(i == 0)
        def _():
            acc_ref[...] = jnp.zeros_like(acc_ref)

        x = _padded_rows(i, x_ref, meta_ref)
        mu = jnp.mean(x, axis=-1, keepdims=True)
        xc = x - mu
        var = jnp.mean(xc * xc, axis=-1, keepdims=True)
        rstd = lax.rsqrt(var + LN_EPS)
        xhat = xc * rstd
        dh = ALPHA * dr_ref[...] + dhz_ref[...]
        acc_ref[0:1, :] += jnp.sum(dh * xhat, axis=0, keepdims=True)
        acc_ref[1:2, :] += jnp.sum(dh, axis=0, keepdims=True)
        dxh = dh * g_ref[...]
        m1 = jnp.mean(dxh, axis=-1, keepdims=True)
        m2 = jnp.mean(dxh * xhat, axis=-1, keepdims=True)
        dx = rstd * (dxh - m1 - xhat * m2)
        dx_ref[...] = dx

        @pl.when(i == 0)
        def _():
            dmeta_ref[...] = dx[PAD:BLK]

    row = pl.BlockSpec((BLK, D), lambda i: (i, 0))
    vec = pl.BlockSpec((1, D), lambda i: (0, 0))
    xs, ms = _stream_specs()
    return _pcall(body, name="ln_emb_bwd", grid=(NBLK,), in_specs=[xs, ms, vec, row, row],
                  out_specs=[xs, ms, pl.BlockSpec((8, D), lambda i: (0, 0))],
                  out_shape=[_sds((SEQ, D), F32), _sds((N_META, D), F32), _sds((8, D), F32)],
                  operands=[x, meta, g, dr, dhz], vmem=32, sem=("arbitrary",), hook=hook)


def _mul_silu_fwd(a, z, off, *, name):
    def body(a_ref, z_ref, o_ref, t_ref):
        zz = z_ref[...]
        y = a_ref[...] * (zz * _sigmoid(zz))
        o_ref[...] = y.astype(_MXU)
        t_ref[...] = y.T.astype(_MXU)

    strip = pl.BlockSpec((TP, CW), lambda j: (0, j))
    return _pcall(body, name=name, grid=(D // CW,),
                  in_specs=[strip, pl.BlockSpec((TP, CW), lambda j: (0, off // CW + j))],
                  out_specs=[strip, pl.BlockSpec((CW, TP), lambda j: (j, 0))],
                  out_shape=[_sds((TP, D), _MXU), _sds((D, TP), _MXU)], operands=[a, z], vmem=56, sem=("parallel",))


def _mul_silu_bwd(dy, a, z, off, dz, *, name, hook=None):
    def body(dy_ref, a_ref, z_ref, dz_in, da_ref, dg_ref):
        zz = z_ref[...]
        sg = _sigmoid(zz)
        d = dy_ref[...]
        da_ref[...] = d * (zz * sg)
        dg_ref[...] = (d * a_ref[...] * (sg * (1.0 + zz * (1.0 - sg)))).astype(_MXU)

    blk = pl.BlockSpec((RT, CW), lambda i, j: (i, j))
    zblk = pl.BlockSpec((RT, CW), lambda i, j: (i, off // CW + j))
    return _pcall(body, name=name, grid=(TP // RT, D // CW), in_specs=[blk, blk, zblk, _ANY], out_specs=[blk, zblk],
                  out_shape=[_sds((TP, D), F32), _sds((TP, D_IN), _MXU)], operands=[dy, a, z, dz], vmem=32,
                  sem=("parallel", "parallel"), aliases={3: 1}, hook=hook)


def _merge_fwd(y2, z):
    w = 256

    def body(ya_ref, yb_ref, ga_ref, gb_ref, o_ref, t_ref):
        y = _sigmoid(ga_ref[...]) * ya_ref[...] + _sigmoid(gb_ref[...]) * yb_ref[...]
        o_ref[...] = y.astype(_MXU)
        t_ref[...] = y.T.astype(_MXU)

    nb = D // w
    strip = pl.BlockSpec((TP, w), lambda j: (0, j))
    return _pcall(body, name="merge_fwd", grid=(nb,),
                  in_specs=[strip, pl.BlockSpec((TP, w), lambda j: (0, nb + j)),
                            pl.BlockSpec((TP, w), lambda j: (0, OFF_G // w + j)),
                            pl.BlockSpec((TP, w), lambda j: (0, (OFF_G + D) // w + j))],
                  out_specs=[strip, pl.BlockSpec((w, TP), lambda j: (j, 0))],
                  out_shape=[_sds((TP, D), _MXU), _sds((D, TP), _MXU)], operands=[y2, y2, z, z], vmem=56,
                  sem=("parallel",))


def _merge_bwd(dmix, y2, z):
    nb = D // CW

    def body(dm_ref, y_ref, g_ref, dy_ref, dg_ref):
        dm = dm_ref[...]
        sg = _sigmoid(g_ref[...])
        dy_ref[...] = (dm * sg).astype(_MXU)
        dg_ref[...] = (dm * y_ref[...] * sg * (1.0 - sg)).astype(_MXU)

    blk = pl.BlockSpec((RT, CW), lambda i, j: (i, j))
    gblk = pl.BlockSpec((RT, CW), lambda i, j: (i, OFF_G // CW + j))
    return _pcall(body, name="merge_bwd", grid=(TP // RT, 2 * nb),
                  in_specs=[pl.BlockSpec((RT, CW), lambda i, j: (i, j % nb)), blk, gblk], out_specs=[blk, gblk],
                  out_shape=[_sds((TP, 2 * D), _MXU), _sds((TP, D_IN), _MXU)], operands=[dmix, y2, z], vmem=32,
                  sem=("parallel", "parallel"))


def _ln_out_loss(h32, out, target, g, b):
    def body(h_ref, o_ref, t_ref, g_ref, b_ref, dr_ref, acc_ref):
        i = pl.program_id(0)

        @pl.when(i == 0)
        def _():
            acc_ref[...] = jnp.zeros_like(acc_ref)

        r = ALPHA * h_ref[...] + o_ref[...]
        mu = jnp.mean(r, axis=-1, keepdims=True)
        rc = r - mu
        var = jnp.mean(rc * rc, axis=-1, keepdims=True)
        rstd = lax.rsqrt(var + LN_EPS)
        xhat = rc * rstd
        gg = g_ref[...]
        y = xhat * gg + b_ref[...]
        real = (i >= 1).astype(F32)
        diff = (y - t_ref[...]) * real
        dy = diff * (1.0 / D)
        dxh = dy * gg
        m1 = jnp.mean(dxh, axis=-1, keepdims=True)
        m2 = jnp.mean(dxh * xhat, axis=-1, keepdims=True)
        dr = rstd * (dxh - m1 - xhat * m2)
        dr_ref[...] = dr
        acc_ref[0:1, :] += jnp.sum(dy * xhat, axis=0, keepdims=True)
        acc_ref[1:2, :] += jnp.sum(dy, axis=0, keepdims=True)
        acc_ref[2:3, :] += jnp.sum(dr, axis=0, keepdims=True)
        acc_ref[3:4, :] += (0.5 / D) * jnp.sum(diff * diff)

    row = pl.BlockSpec((BLK, D), lambda i: (i, 0))
    vec = pl.BlockSpec((1, D), lambda i: (0, 0))
    return _pcall(body, name="ln_out_loss", grid=(NBLK,),
                  in_specs=[row, row, pl.BlockSpec((BLK, D), lambda i: (jnp.maximum(i - 1, 0), 0)), vec, vec],
                  out_specs=[row, pl.BlockSpec((8, D), lambda i: (0, 0))],
                  out_shape=[_sds((TP, D), F32), _sds((8, D), F32)], operands=[h32, out, target, g, b], vmem=32,
                  sem=("arbitrary",))


def _colsum(x, *, name, tn, hook=None):
    _, n = x.shape

    def body(x_ref, o_ref):
        o_ref[...] = jnp.sum(x_ref[...].astype(F32), axis=0, keepdims=True)

    return _pcall(body, name=name, grid=(n // tn,), in_specs=[pl.BlockSpec((TP, tn), lambda j: (0, j))],
                  out_specs=[pl.BlockSpec((1, tn), lambda j: (0, j))], out_shape=[_sds((1, n), F32)], operands=[x],
                  vmem=32, sem=("parallel",), hook=hook)


def _rnn_recompute(xr_ref, cw_ref, cb_ref, wra_ref, wri_ref, bra_ref, bri_ref, lam_ref):
    rows = lax.broadcasted_iota(jnp.int32, (TP, 1), 0)
    valid = (rows >= PAD).astype(F32)
    first = rows == PAD
    x = xr_ref[...] * valid
    cw = cw_ref[...]
    shifted = [x, pltpu.roll(x, 1, 0), pltpu.roll(x, 2, 0), pltpu.roll(x, 3, 0)]
    c = cb_ref[...] + cw[0:1, :] * shifted[0] + cw[1:2, :] * shifted[1] + cw[2:3, :] * shifted[2] + cw[3:4, :] * shifted[3]
    cm = c.astype(_MXU)
    gr = _sigmoid(jnp.dot(cm, wra_ref[...].astype(_MXU), preferred_element_type=F32) + bra_ref[...])
    gi = _sigmoid(jnp.dot(cm, wri_ref[...].astype(_MXU), preferred_element_type=F32) + bri_ref[...])
    lam = lam_ref[...]
    ls = jnp.minimum(lam, 0.0) - jnp.log(1.0 + jnp.exp(-jnp.abs(lam)))
    log_a = LRU_C * gr * ls
    a = jnp.exp(log_a)
    mult = jnp.where(first, 1.0, jnp.sqrt(1.0 - jnp.exp(2.0 * log_a)))
    return dict(valid=valid, first=first, shifted=shifted, c=c, cm=cm, gr=gr, gi=gi, ls=ls, a=a, mult=mult, lam=lam)


def _rnn_specs():
    col = pl.BlockSpec((TP, RB), lambda n: (0, n))
    vec = pl.BlockSpec((1, RB), lambda n: (0, n))
    return dict(col=col, vec=vec, cw=pl.BlockSpec((CONV_W, RB), lambda n: (0, n)),
                wblk=pl.BlockSpec((None, RB, RB), lambda n: (n, 0, 0)))


def _rnn_gates_fwd(z, conv_w, conv_b, w_ra, w_ri, b_ra, b_ri, lam):
    def body(xr_ref, cw_ref, cb_ref, wra_ref, wri_ref, bra_ref, bri_ref, lam_ref, a_ref, u_ref):
        r = _rnn_recompute(xr_ref, cw_ref, cb_ref, wra_ref, wri_ref, bra_ref, bri_ref, lam_ref)
        a_ref[...] = r["a"]
        u_ref[...] = r["mult"] * r["gi"] * r["c"] * r["valid"]

    s = _rnn_specs()
    return _pcall(body, name="rnn_gates_fwd", grid=(N_RB,),
                  in_specs=[s["col"], s["cw"], s["vec"], s["wblk"], s["wblk"], s["vec"], s["vec"], s["vec"]],
                  out_specs=[s["col"], s["col"]], out_shape=[_sds((TP, D), F32)] * 2,
                  operands=[z, conv_w, conv_b, w_ra, w_ri, b_ra, b_ri, lam], vmem=56, sem=("parallel",))


SCAN_ROWS = 272


SUB = 8


def _tile_scan(a, u, reverse):
    rows = lax.broadcasted_iota(jnp.int32, a.shape, 0)
    for d in (1, 2, 4):
        shift = SUB - d if reverse else d
        inside = (rows < SUB - d) if reverse else (rows >= d)
        u = u + a * jnp.where(inside, pltpu.roll(u, shift, 0), 0.0)
        a = a * jnp.where(inside, pltpu.roll(a, shift, 0), 1.0)
    return a, u


def _scan_fwd(a, u):
    def body(a_ref, u_ref, h_ref, carry_ref):
        @pl.when(pl.program_id(1) == 0)
        def _():
            carry_ref[...] = jnp.zeros_like(carry_ref)

        def step(r, h):
            rows = pl.ds(pl.multiple_of(r * SUB, SUB), SUB)
            prod, part = _tile_scan(a_ref[rows, :], u_ref[rows, :], False)
            ht = part + prod * h
            h_ref[rows, :] = ht
            return ht[SUB - 1:SUB, :]

        carry_ref[...] = lax.fori_loop(0, SCAN_ROWS // SUB, step, carry_ref[...], unroll=2)

    blk = pl.BlockSpec((SCAN_ROWS, CW), lambda j, i: (i, j))
    return _pcall(body, name="scan_fwd", grid=(D // CW, TP // SCAN_ROWS), in_specs=[blk, blk], out_specs=[blk],
                  out_shape=[_sds((TP, D), F32)], operands=[a, u], scratch=[pltpu.VMEM((1, CW), F32)], vmem=32,
                  sem=("parallel", "arbitrary"))[0]


def _scan_bwd(a, dh):
    nst = TP // SCAN_ROWS
    n_tiles = SCAN_ROWS // SUB

    def body(a_ref, d_ref, o_ref, lam_ref, anext_ref):
        @pl.when(pl.program_id(1) == 0)
        def _():
            lam_ref[...] = jnp.zeros_like(lam_ref)
            anext_ref[...] = jnp.zeros_like(anext_ref)

        def step(q, carry):
            lam_next, a_next = carry
            rows = pl.ds(pl.multiple_of((n_tiles - 1 - q) * SUB, SUB), SUB)
            at = a_ref[rows, :]
            last = lax.broadcasted_iota(jnp.int32, at.shape, 0) == SUB - 1
            b = jnp.where(last, a_next, pltpu.roll(at, SUB - 1, 0))
            prod, part = _tile_scan(b, d_ref[rows, :], True)
            lam = part + prod * lam_next
            o_ref[rows, :] = lam
            return lam[0:1, :], at[0:1, :]

        lam, an = lax.fori_loop(0, n_tiles, step, (lam_ref[...], anext_ref[...]), unroll=2)
        lam_ref[...] = lam
        anext_ref[...] = an

    blk = pl.BlockSpec((SCAN_ROWS, CW), lambda j, i: (nst - 1 - i, j))
    return _pcall(body, name="scan_bwd", grid=(D // CW, nst), in_specs=[blk, blk], out_specs=[blk],
                  out_shape=[_sds((TP, D), F32)], operands=[a, dh],
                  scratch=[pltpu.VMEM((1, CW), F32), pltpu.VMEM((1, CW), F32)], vmem=32,
                  sem=("parallel", "arbitrary"))[0]


def _rnn_gates_bwd(z, lam_s, hr, conv_w, conv_b, w_ra, w_ri, b_ra, b_ri, lam, dz, gsq):
    def body(xr_ref, ls_ref, hr_ref, cw_ref, cb_ref, wra_ref, wri_ref, bra_ref, bri_ref, lam_ref, dz_in, gsq_in,
             dx_ref, dw_ref, sums_ref):
        r = _rnn_recompute(xr_ref, cw_ref, cb_ref, wra_ref, wri_ref, bra_ref, bri_ref, lam_ref)
        valid, c, gr, gi, a, mult = r["valid"], r["c"], r["gr"], r["gi"], r["a"], r["mult"]
        du = ls_ref[...] * valid
        da = du * pltpu.roll(hr_ref[...], 1, 0)
        d_gi = du * mult * c
        dc = du * mult * gi
        dmult = du * gi * c
        dlog_a = da * a + jnp.where(r["first"], 0.0, -dmult * a * a / mult)
        d_gr = dlog_a * (LRU_C * r["ls"])
        dls = jnp.sum(dlog_a * (LRU_C * gr), axis=0, keepdims=True)
        dpre_r = d_gr * gr * (1.0 - gr)
        dpre_i = d_gi * gi * (1.0 - gi)
        pr = dpre_r.astype(_MXU)
        pi = dpre_i.astype(_MXU)
        dwra = lax.dot_general(r["cm"], pr, _TN, preferred_element_type=F32)
        dwri = lax.dot_general(r["cm"], pi, _TN, preferred_element_type=F32)
        for s in range(N_SH):
            dw_ref[s, 0:64, :] = dwra[64 * s:64 * (s + 1)]
            dw_ref[s, 64:128, :] = dwri[64 * s:64 * (s + 1)]
        dc = dc + lax.dot_general(pr, wra_ref[...].astype(_MXU), _NT, preferred_element_type=F32)
        dc = dc + lax.dot_general(pi, wri_ref[...].astype(_MXU), _NT, preferred_element_type=F32)
        cw = cw_ref[...]
        dx = cw[0:1, :] * dc
        for k in range(1, CONV_W):
            dx = dx + cw[k:k + 1, :] * pltpu.roll(dc, TP - k, 0)
        dx_ref[...] = (dx * valid).astype(_MXU)
        for k in range(CONV_W):
            sums_ref[k:k + 1, :] = jnp.sum(dc * r["shifted"][k], axis=0, keepdims=True)
        sums_ref[4:5, :] = jnp.sum(dc, axis=0, keepdims=True)
        sums_ref[5:6, :] = jnp.sum(dpre_r, axis=0, keepdims=True)
        sums_ref[6:7, :] = jnp.sum(dpre_i, axis=0, keepdims=True)
        sums_ref[7:8, :] = dls * _sigmoid(-r["lam"])

    s = _rnn_specs()
    return _pcall(
        body, name="rnn_gates_bwd", grid=(N_RB,),
        in_specs=[s["col"], s["col"], s["col"], s["cw"], s["vec"], s["wblk"], s["wblk"], s["vec"], s["vec"], s["vec"],
                  _ANY, _ANY],
        out_specs=[s["col"], pl.BlockSpec((N_SH, 128, RB), lambda n: (0, 3 * SQ_ROWS // 128, n)),
                   pl.BlockSpec((8, RB), lambda n: (0, n))],
        out_shape=[_sds((TP, D_IN), _MXU), _sds((N_SH, PACK_ROWS, D), F32), _sds((8, D), F32)],
        operands=[z, lam_s, hr, conv_w, conv_b, w_ra, w_ri, b_ra, b_ri, lam, dz, gsq], vmem=60, sem=("parallel",),
        aliases={10: 0, 11: 1})


def _rope_tables():
    half = HD // 2
    inv = ROPE_THETA ** (-jnp.arange(half, dtype=F32) / half)
    pos = (jnp.arange(TP) - PAD).astype(F32)
    ang = pos[:, None] * inv[None, :]
    return jnp.tile(jnp.cos(ang), (1, 4)), jnp.tile(jnp.sin(ang), (1, 4))


def _rope(x, cos_t, sin_t, sign):
    w = x.shape[1]
    lane = lax.broadcasted_iota(jnp.int32, x.shape, 1)
    first = (lane % HD) < (HD // 2)
    swapped = jnp.where(first, pltpu.roll(x, w - HD // 2, 1), pltpu.roll(x, HD // 2, 1))
    ct = jnp.tile(cos_t, (1, w // 128))
    st = jnp.tile(sin_t, (1, w // 128))
    return x * ct + swapped * jnp.where(first, -sign * st, sign * st)


def _rope_fwd(z, cos_t, sin_t):
    def body(q_ref, k_ref, v_ref, c_ref, s_ref, qo_ref, ko_ref, vo_ref):
        c = c_ref[...]
        s = s_ref[...]
        qo_ref[...] = _rope(q_ref[...], c, s, 1.0).astype(_MXU)
        ko_ref[...] = _rope(k_ref[...], c, s, 1.0).astype(_MXU)
        vo_ref[...] = v_ref[...].astype(_MXU)

    tab = pl.BlockSpec((BLK, 128), lambda i: (i, 0))
    kv = pl.BlockSpec((BLK, D_KV), lambda i: (i, 0))
    return _pcall(body, name="rope_fwd", grid=(NBLK,),
                  in_specs=[pl.BlockSpec((BLK, D), lambda i: (i, OFF_Q // D)),
                            pl.BlockSpec((BLK, D_KV), lambda i: (i, OFF_K // D_KV)),
                            pl.BlockSpec((BLK, D_KV), lambda i: (i, OFF_V // D_KV)), tab, tab],
                  out_specs=[pl.BlockSpec((BLK, D), lambda i: (i, 0)), kv, kv],
                  out_shape=[_sds((TP, D), _MXU), _sds((TP, D_KV), _MXU), _sds((TP, D_KV), _MXU)],
                  operands=[z, z, z, cos_t, sin_t], vmem=32, sem=("parallel",))


def _rope_bwd_k(dk, dv, cos_t, sin_t, dz):
    def body(dk_ref, dv_ref, c_ref, s_ref, dz_in, o_ref):
        o_ref[:, 0:D_KV] = _rope(dk_ref[...], c_ref[...], s_ref[...], -1.0).astype(_MXU)
        o_ref[:, D_KV:2 * D_KV] = dv_ref[...].astype(_MXU)

    tab = pl.BlockSpec((BLK, 128), lambda i: (i, 0))
    kv = pl.BlockSpec((BLK, D_KV), lambda i: (i, 0))
    return _pcall(body, name="rope_bwd_k", grid=(NBLK,), in_specs=[kv, kv, tab, tab, _ANY],
                  out_specs=[pl.BlockSpec((BLK, 2 * D_KV), lambda i: (i, OFF_K // (2 * D_KV)))],
                  out_shape=[_sds((TP, D_IN), _MXU)], operands=[dk, dv, cos_t, sin_t, dz], vmem=32, sem=("parallel",),
                  aliases={4: 0})[0]


def _attn_mask(i):
    ql = lax.broadcasted_iota(jnp.int32, (BLK, 3 * BLK), 0)
    kk = lax.broadcasted_iota(jnp.int32, (BLK, 3 * BLK), 1)
    kl = kk % BLK
    part = kk // BLK
    meta = (part == 0) & (kl >= PAD) & ((i >= 1) | (kl <= ql))
    prev = (part == 1) & (i >= 2) & (kl > ql)
    cur = (part == 2) & (i >= 1) & (kl <= ql)
    return meta | prev | cur


def _cat_kv(refs, g):
    return jnp.concatenate([r[:, HD * g:HD * (g + 1)] for r in refs], axis=0)


def _kv_specs():
    return [pl.BlockSpec((BLK, D_KV), lambda i: (0, 0)),
            pl.BlockSpec((BLK, D_KV), lambda i: (jnp.maximum(i - 1, 0), 0)),
            pl.BlockSpec((BLK, D_KV), lambda i: (i, 0))]


def _stack_heads(ref, g, width=HD):
    return jnp.concatenate([ref[:, width * (GRP * g + j):width * (GRP * g + j + 1)] for j in range(GRP)], axis=0)


def _attn_fwd(q, k, v, sinks, *, hook):
    def body(q_ref, k0_ref, kp_ref, kc_ref, v0_ref, vp_ref, vc_ref, sink_ref, o_ref, lse_ref, s_scr, p_scr):
        mask = _attn_mask(pl.program_id(0))
        for g in range(N_KV):
            s_scr[...] = lax.dot_general(_stack_heads(q_ref, g), _cat_kv((k0_ref, kp_ref, kc_ref), g), _NT,
                                         preferred_element_type=F32)
            for j in range(GRP):
                h = GRP * g + j
                rows = slice(BLK * j, BLK * (j + 1))
                sink = sink_ref[h]
                s = jnp.where(mask, s_scr[rows, :] * (HD ** -0.5), NEG_INF)
                mx = jnp.maximum(jnp.max(s, -1, keepdims=True), sink)
                p = jnp.exp(s - mx)
                den = jnp.sum(p, -1, keepdims=True) + jnp.exp(sink - mx)
                p_scr[rows, :] = (p * (1.0 / den)).astype(_MXU)
                lse_ref[:, h:h + 1] = mx + jnp.log(den)
            o8 = jnp.dot(p_scr[...], _cat_kv((v0_ref, vp_ref, vc_ref), g), preferred_element_type=F32)
            for j in range(GRP):
                h = GRP * g + j
                o_ref[:, HD * h:HD * (h + 1)] = o8[BLK * j:BLK * (j + 1)]

    return _pcall(body, name="attn_fwd", grid=(NBLK,),
                  in_specs=[pl.BlockSpec((BLK, D), lambda i: (i, 0))] + _kv_specs() + _kv_specs()
                           + [pl.BlockSpec(memory_space=pltpu.SMEM)],
                  out_specs=[pl.BlockSpec((BLK, D), lambda i: (i, 0)), pl.BlockSpec((BLK, N_Q), lambda i: (i, 0))],
                  out_shape=[_sds((TP, D), F32), _sds((TP, N_Q), F32)], operands=[q, k, k, k, v, v, v, sinks],
                  scratch=[pltpu.VMEM((GRP * BLK, 3 * BLK), F32), pltpu.VMEM((GRP * BLK, 3 * BLK), _MXU)],
                  vmem=40, sem=("parallel",), hook=hook)


def _attn_bwd(q, k, v, sinks, do, o, lse, cos_t, sin_t, dz, *, hook):
    def body(q_ref, k0_ref, kp_ref, kc_ref, v0_ref, vp_ref, vc_ref, sink_ref, do_ref, o_ref, lse_ref, c_ref, s_ref, dz_in,
             dq_ref, dk_ref, dv_ref, dsink_ref, dqrot_ref, s_scr, dp_scr, p_scr, ds_scr):
        i = pl.program_id(0)

        @pl.when(i == 0)
        def _():
            dk_ref[...] = jnp.zeros_like(dk_ref)
            dv_ref[...] = jnp.zeros_like(dv_ref)
            dsink_ref[...] = jnp.zeros_like(dsink_ref)

        mask = _attn_mask(i)
        row_starts = (0, pl.multiple_of(jnp.maximum(i - 1, 0) * BLK, BLK), pl.multiple_of(i * BLK, BLK))
        scale = HD ** -0.5
        for g in range(N_KV):
            gs = slice(HD * g, HD * (g + 1))
            q8 = _stack_heads(q_ref, g)
            dom = _stack_heads(do_ref, g).astype(_MXU)
            kcat = _cat_kv((k0_ref, kp_ref, kc_ref), g)
            s_scr[...] = lax.dot_general(q8, kcat, _NT, preferred_element_type=F32)
            dp_scr[...] = lax.dot_general(dom, _cat_kv((v0_ref, vp_ref, vc_ref), g), _NT, preferred_element_type=F32)
            for j in range(GRP):
                h = GRP * g + j
                hs = slice(HD * h, HD * (h + 1))
                rows = slice(BLK * j, BLK * (j + 1))
                lse_h = lse_ref[:, h:h + 1]
                delta = jnp.sum(do_ref[:, hs] * o_ref[:, hs], axis=-1, keepdims=True)
                dsink_ref[0:1, h:h + 1] += jnp.sum(-jnp.exp(sink_ref[h] - lse_h) * delta, axis=0, keepdims=True)
                p = jnp.exp(jnp.where(mask, s_scr[rows, :] * scale, NEG_INF) - lse_h)
                p_scr[rows, :] = p.astype(_MXU)
                ds_scr[rows, :] = (p * (dp_scr[rows, :] - delta) * scale).astype(_MXU)
            ds = ds_scr[...]
            dq8 = jnp.dot(ds, kcat, preferred_element_type=F32)
            dkcat = lax.dot_general(ds, q8, _TN, preferred_element_type=F32)
            dvcat = lax.dot_general(p_scr[...], dom, _TN, preferred_element_type=F32)
            for part in range(3):
                rows = pl.ds(row_starts[part], BLK)
                dk_ref[rows, gs] += dkcat[BLK * part:BLK * (part + 1)]
                dv_ref[rows, gs] += dvcat[BLK * part:BLK * (part + 1)]
            for j in range(GRP):
                h = GRP * g + j
                dqrot_ref[:, HD * h:HD * (h + 1)] = dq8[BLK * j:BLK * (j + 1)]
        dq_ref[...] = _rope(dqrot_ref[...], c_ref[...], s_ref[...], -1.0).astype(_MXU)

    row = pl.BlockSpec((BLK, D), lambda i: (i, 0))
    tab = pl.BlockSpec((BLK, 128), lambda i: (i, 0))
    full_kv = pl.BlockSpec((TP, D_KV), lambda i: (0, 0))
    return _pcall(
        body, name="attn_bwd", grid=(NBLK,),
        in_specs=[row] + _kv_specs() + _kv_specs() + [pl.BlockSpec(memory_space=pltpu.SMEM), row, row,
                  pl.BlockSpec((BLK, N_Q), lambda i: (i, 0)), tab, tab, _ANY],
        out_specs=[pl.BlockSpec((BLK, D), lambda i: (i, OFF_Q // D)), full_kv, full_kv,
                   pl.BlockSpec((8, 128), lambda i: (0, 0))],
        out_shape=[_sds((TP, D_IN), _MXU), _sds((TP, D_KV), F32), _sds((TP, D_KV), F32), _sds((8, 128), F32)],
        operands=[q, k, k, k, v, v, v, sinks, do, o, lse, cos_t, sin_t, dz],
        scratch=[pltpu.VMEM((BLK, D), F32), pltpu.VMEM((GRP * BLK, 3 * BLK), F32), pltpu.VMEM((GRP * BLK, 3 * BLK), F32),
                 pltpu.VMEM((GRP * BLK, 3 * BLK), _MXU), pltpu.VMEM((GRP * BLK, 3 * BLK), _MXU)],
        vmem=48, sem=("arbitrary",), aliases={13: 0}, hook=hook)


def _cast_into_slot(w32, pos, *, name, tr):
    r, cc = w32.shape

    def body(pos_ref, w_ref, o_ref):
        o_ref[...] = w_ref[...].astype(BF16)

    return _pcall(body, name=name, grid=(r // tr,), in_specs=[pl.BlockSpec((tr, cc), lambda i, p: (i, 0))],
                  out_specs=[pl.BlockSpec((None, tr, cc), lambda i, p: (p[1], i, 0))],
                  out_shape=[_sds((N_SH, r, cc), BF16)], operands=[w32], vmem=32, sem=("parallel",), prefetch=(pos,))[0]


def _pair_add(g, got, pos, *, name, tr, g_has_both_halves):
    n, h, cc = got.shape
    nt = h // tr

    def body(pos_ref, g_ref, r_ref, own_ref, s16_ref):
        s = g_ref[...] + r_ref[...]
        s16_ref[...] = s.astype(BF16)

        @pl.when(pl.program_id(1) == pos_ref[1])
        def _():
            own_ref[...] = s

    g_index = (lambda i, s, p: (s, p[0] * nt + i, 0)) if g_has_both_halves else (lambda i, s, p: (s, i, 0))
    return _pcall(body, name=name, grid=(nt, n),
                  in_specs=[pl.BlockSpec((None, tr, cc), g_index), pl.BlockSpec((None, tr, cc), lambda i, s, p: (s, i, 0))],
                  out_specs=[pl.BlockSpec((tr, cc), lambda i, s, p: (i, 0)),
                             pl.BlockSpec((None, tr, cc), lambda i, s, p: (s, i, 0))],
                  out_shape=[_sds((h, cc), F32), _sds((n, h, cc), BF16)], operands=[g, got], vmem=40,
                  sem=("parallel", "arbitrary"), prefetch=(pos,))


def _sum_chips(own, got, pos, *, name, tr):
    h, cc = own.shape
    nt = h // tr

    def body(pos_ref, o_ref, r_ref, out_ref):
        acc = o_ref[...]
        for k in range(3):
            acc = acc + r_ref[k].astype(F32)
        out_ref[...] = acc

    return _pcall(body, name=name, grid=(nt,),
                  in_specs=[pl.BlockSpec((tr, cc), lambda i, p: (i, 0)), pl.BlockSpec((3, tr, cc), lambda i, p: (0, i, 0))],
                  out_specs=[pl.BlockSpec((tr, cc), lambda i, p: (p[0] * nt + i, 0))],
                  out_shape=[_sds((2 * h, cc), F32)], operands=[own, got], vmem=40, sem=("parallel",), prefetch=(pos,))[0]


def _adamw(w, g, m, v, *, name, tr, g_row0=0):
    r, cc = w.shape
    g_blk0 = g_row0 // tr

    def body(w_ref, g_ref, m_ref, v_ref, go_ref, d_ref, mo_ref, vo_ref):
        gg = g_ref[...]
        go_ref[...] = gg
        m_new = ADAM_B1 * m_ref[...] + (1.0 - ADAM_B1) * gg
        v_new = ADAM_B2 * v_ref[...] + (1.0 - ADAM_B2) * (gg * gg)
        m_hat = m_new / (1.0 - ADAM_B1 ** ADAM_STEP)
        v_hat = v_new / (1.0 - ADAM_B2 ** ADAM_STEP)
        d_ref[...] = -ADAM_LR * (m_hat / (jnp.sqrt(v_hat) + ADAM_EPS) + ADAM_WD * w_ref[...])
        mo_ref[...] = m_new
        vo_ref[...] = v_new

    blk = pl.BlockSpec((tr, cc), lambda i: (i, 0))
    gblk = pl.BlockSpec((tr, cc), lambda i: (g_blk0 + i, 0))
    return _pcall(body, name=name, grid=(r // tr,), in_specs=[blk, gblk, blk, blk], out_specs=[blk] * 4,
                  out_shape=[_sds((r, cc), F32)] * 4, operands=[w, g, m, v], vmem=48, sem=("parallel",))


_SMALL_ROWS = 40
_B_IN_ROWS = 7


def _row_pad(v, rows):
    flat = v.reshape(-1)
    return jnp.pad(flat, (0, rows * D - flat.shape[0])).reshape(rows, D)


def _pack_ra(w):
    return w.transpose(1, 0, 2).reshape(64, D)


def _unpack_ra(p, like):
    return p.reshape(64, N_RB, RB).transpose(1, 0, 2).reshape(like.shape)


def _gate_full(g4):
    return g4.reshape(N_SH, 64, N_RB, RB).transpose(2, 0, 1, 3).reshape(N_RB, RB, RB)


def kernel(x, meta_tokens, ln_emb_g, ln_emb_b, w_in, b_in, conv_w, conv_b, w_ra, b_ra, w_ri, b_ri, lru_lambda, sinks, w_rnn_out, w_attn_out, w_o, b_o, ln_g, ln_b, loss_target, m_meta_tokens, m_ln_emb_g, m_ln_emb_b, m_w_in, m_b_in, m_conv_w, m_conv_b, m_w_ra, m_b_ra, m_w_ri, m_b_ri, m_lru_lambda, m_sinks, m_w_rnn_out, m_w_attn_out, m_w_o, m_b_o, m_ln_g, m_ln_b, v_meta_tokens, v_ln_emb_g, v_ln_emb_b, v_w_in, v_b_in, v_conv_w, v_conv_b, v_w_ra, v_b_ra, v_w_ri, v_b_ri, v_lru_lambda, v_sinks, v_w_rnn_out, v_w_attn_out, v_w_o, v_b_o, v_ln_g, v_ln_b):
    xi, yi, ci = _my_pos()
    shard = 2 * xi + yi
    pos = jnp.stack([ci, shard, 1 - ci]).astype(jnp.int32)
    cos_t, sin_t = _rope_tables()
    zero_bias = jnp.zeros((1, D), F32)
    ln_emb_g2, ln_emb_b2 = ln_emb_g[None], ln_emb_b[None]

    small = jnp.concatenate([conv_w[0], meta_tokens, jnp.zeros((4, 512), F32)], axis=0)
    small4 = _gather_small(small)
    conv_w_full = small4[:, 0:4].transpose(1, 0, 2).reshape(CONV_W, D)
    meta_full = small4[:, 4:20].transpose(1, 0, 2).reshape(N_META, D)
    w_own = _cast_into_slot(w_in[0], pos, name="cast_w_in", tr=256)
    wpack = jnp.concatenate([w_rnn_out[0], w_attn_out[0], w_o[0], _pack_ra(w_ra[0]), _pack_ra(w_ri[0])], axis=0)
    wp_own = _cast_into_slot(wpack, pos, name="cast_w_sq", tr=208)

    h32, h16, h16_t = _ln_emb_fwd(x[0], meta_full, ln_emb_g2, ln_emb_b2)
    order = jnp.stack([shard, shard ^ 2, shard ^ 1, shard ^ 3]).astype(jnp.int32)
    z, w_in4 = _mm_z_gather(h16, w_own, b_in, order)
    q, k, v = _rope_fwd(z, cos_t, sin_t)
    o, lse, wp4 = _attn_fwd(q, k, v, sinks[0], hook=_hook_gather(wp_own))
    w_ra_full = _gate_full(wp4[:, 3 * SQ_ROWS:3 * SQ_ROWS + 64])
    w_ri_full = _gate_full(wp4[:, 3 * SQ_ROWS + 64:PACK_ROWS])

    def sq_nn(a, kk, bias, name, out_cols, out_index, carried=None):
        return _mm_nn(a, wp4, bias, name=name, grid=(2, D // CW), tm=HALF_TP, tn=CW, k=D, a_index=lambda i, j: (i, 0),
                      w_block=(N_SH, SQ_ROWS, CW), w_index=lambda i, j: (0, kk, j), out_cols=out_cols,
                      out_index=out_index, carried=carried)[0]

    def sq_nt(a, a_blk, kk, name, hook=None):
        return _mm_nt(a, wp4, name=name, grid=(2, N_SH, 1), tm=HALF_TP, tn=SQ_ROWS, tk=D,
                      a_index=lambda i, j, q: (i, a_blk), w_block=(None, SQ_ROWS, D),
                      w_index=lambda i, j, q: (j, kk, 0), out_cols=D, hook=hook)

    a_dec, u_in = _rnn_gates_fwd(z, conv_w_full, conv_b, w_ra_full, w_ri_full, b_ra, b_ri, lru_lambda)
    hr = _scan_fwd(a_dec, u_in)
    ya_in, ya_in_t = _mul_silu_fwd(hr, z, OFF_GR, name="gate_a_fwd")
    y2 = sq_nn(ya_in, 0, zero_bias, "mm_ya", 2 * D, lambda i, j: (i, j))
    yb_in, yb_in_t = _mul_silu_fwd(o, z, OFF_GA, name="gate_b_fwd")
    y2 = sq_nn(yb_in, 1, zero_bias, "mm_yb", 2 * D, lambda i, j: (i, D // CW + j), carried=y2)
    mixed, mixed_t = _merge_fwd(y2, z)
    out = sq_nn(mixed, 2, b_o, "mm_out", D, lambda i, j: (i, j))
    dr, sums_o = _ln_out_loss(h32, out, loss_target[0], ln_g, ln_b)

    def sq_tn(at, b, b_blk0, kk, name, carried=None):
        return _mm_tn(at, b, name=name, grid=(N_SH, D // CW), tm=SQ_ROWS, tn=CW, a_index=lambda i, j: (i, 0),
                      b_index=lambda i, j: (0, b_blk0 + j), out_shape=(N_SH, PACK_ROWS, D),
                      out_block=(None, SQ_ROWS, CW), out_index=lambda i, j: (i, kk, j), carried=carried)[0]

    gsq = sq_tn(mixed_t, dr, 0, 2, "mm_dwo")
    dmix = sq_nt(dr, 0, 2, "mm_dmix")[0]
    dy2, dz = _merge_bwd(dmix, y2, z)
    gsq = sq_tn(ya_in_t, dy2, 0, 0, "mm_dwrnn", carried=gsq)
    gsq = sq_tn(yb_in_t, dy2, D // CW, 1, "mm_dwattn", carried=gsq)
    dya_in = sq_nt(dy2, 0, 0, "mm_dyain")[0]
    dhr, dz = _mul_silu_bwd(dya_in, hr, z, OFF_GR, dz, name="gate_a_bwd")
    lam_s = _scan_bwd(a_dec, dhr)
    dz, gsq, sums_r = _rnn_gates_bwd(z, lam_s, hr, conv_w_full, conv_b, w_ra_full, w_ri_full, b_ra, b_ri, lru_lambda, dz, gsq)
    dyb_in = sq_nt(dy2, 1, 1, "mm_dybin")[0]
    do, dz, gsq, got_sq = _mul_silu_bwd(dyb_in, o, z, OFF_GA, dz, name="gate_b_bwd", hook=_hook_pair(gsq, True))
    own_sq, s16_sq = _pair_add(gsq, got_sq, pos, name="red_w_sq_add", tr=208, g_has_both_halves=True)
    dz, dk_rot, dv32, dsink, s16_sq, oth_sq = _attn_bwd(q, k, v, sinks[0], do, o, lse, cos_t, sin_t, dz,
                                                        hook=_hook_scatter(s16_sq))
    dz = _rope_bwd_k(dk_rot, dv32, cos_t, sin_t, dz)
    red_sq = _sum_chips(own_sq, oth_sq, pos, name="red_w_sq_sum", tr=208)
    g_b_in, g_sq = _colsum(dz, name="colsum_dz", tn=TN_IN, hook=_hook_halves(red_sq))

    def dwin(half_idx, name, hook=None):
        return _mm_tn(h16_t, dz, name=name, grid=(1, N_SH * PER_IN), tm=D // 2, tn=TN_IN,
                      a_index=lambda i, j, p: (p[half_idx], 0), b_index=lambda i, j, p: (0, j),
                      out_shape=(N_SH, D // 2, W_IN_COLS), out_block=(None, D // 2, TN_IN),
                      out_index=lambda i, j, p: (j // PER_IN, 0, j % PER_IN), prefetch=(pos,), hook=hook)

    gin_sib = dwin(2, "mm_dwin_sib")[0]
    gin_own, gin_sib, got_in = dwin(0, "mm_dwin_own", hook=_hook_pair(gin_sib, False))
    own_in, s16_in = _pair_add(gin_own, got_in, pos, name="red_w_in_add", tr=128, g_has_both_halves=False)
    dhz, s16_in, oth_in = _mm_nt(dz, w_in4, name="mm_dhz", grid=(2, 2, N_SH), tm=HALF_TP, tn=D // 2, tk=W_IN_COLS,
                                 a_index=lambda i, j, q: (i, q), w_block=(None, D // 2, W_IN_COLS),
                                 w_index=lambda i, j, q: (q, j, 0), out_cols=D, hook=_hook_scatter(s16_in))
    red_in = _sum_chips(own_in, oth_in, pos, name="red_w_in_sum", tr=128)
    g_x, g_meta_local, sums_e, g_in = _ln_emb_bwd(x[0], meta_full, ln_emb_g2, dr, dhz, hook=_hook_halves(red_in))

    big = {"w_in": [t.reshape(w_in.shape) for t in
                    _adamw(w_in[0], g_in, m_w_in[0], v_w_in[0], name="adamw_w_in", tr=128)]}
    for kk, (n, w_, m_, v_) in enumerate([("w_rnn_out", w_rnn_out, m_w_rnn_out, v_w_rnn_out),
                                          ("w_attn_out", w_attn_out, m_w_attn_out, v_w_attn_out), ("w_o", w_o, m_w_o, v_w_o)]):
        big[n] = [t.reshape(w_.shape) for t in
                  _adamw(w_[0], g_sq, m_[0], v_[0], name="adamw_" + n, tr=256, g_row0=SQ_ROWS * kk)]
    for kk, (n, w_, m_, v_) in enumerate([("w_ra", w_ra, m_w_ra, v_w_ra), ("w_ri", w_ri, m_w_ri, v_w_ri)]):
        big[n] = [_unpack_ra(t, w_) for t in
                  _adamw(_pack_ra(w_[0]), g_sq, _pack_ra(m_[0]), _pack_ra(v_[0]), name="adamw_" + n, tr=64,
                         g_row0=3 * SQ_ROWS + 64 * kk)]

    spack = jnp.concatenate([
        sums_e[0:1], sums_e[1:2], _row_pad(g_b_in, _B_IN_ROWS), sums_r[0:4], sums_r[4:5], sums_r[5:6], sums_r[6:7],
        sums_r[7:8], _row_pad(dsink[0:1, 0:N_Q], 1), sums_o[2:3], sums_o[0:1], sums_o[1:2], g_meta_local, sums_o[3:4],
        jnp.zeros((_SMALL_ROWS - 38, D), F32)], axis=0)
    sred = _allreduce_small(spack)
    loss = sred[37, 0]
    col0 = shard * 512
    g_conv_w = lax.dynamic_slice(sred[9:13], (0, col0), (CONV_W, 512))
    g_meta = lax.dynamic_slice(sred[21:37], (0, col0), (N_META, 512))
    small_g = {"ln_emb_g": sred[0:1], "ln_emb_b": sred[1:2], "b_in": sred[2:9], "conv_w": g_conv_w.reshape(1, D),
               "conv_b": sred[13:14], "b_ra": sred[14:15], "b_ri": sred[15:16], "lru_lambda": sred[16:17],
               "sinks": sred[17:18], "b_o": sred[18:19], "ln_g": sred[19:20], "ln_b": sred[20:21],
               "meta_tokens": g_meta.reshape(4, D)}
    small_names = list(small_g)

    def small_pack(vals):
        rows = []
        for n in small_names:
            a = vals[n]
            if n == "b_in":
                a = _row_pad(a, _B_IN_ROWS)
            elif n == "sinks":
                a = _row_pad(a, 1)
            else:
                a = a.reshape(-1, D)
            rows.append(a)
        return jnp.concatenate(rows + [jnp.zeros((24 - 22, D), F32)], axis=0)

    w_small = dict(ln_emb_g=ln_emb_g, ln_emb_b=ln_emb_b, b_in=b_in, conv_w=conv_w, conv_b=conv_b, b_ra=b_ra, b_ri=b_ri,
                   lru_lambda=lru_lambda, sinks=sinks, b_o=b_o, ln_g=ln_g, ln_b=ln_b, meta_tokens=meta_tokens)
    m_small = dict(ln_emb_g=m_ln_emb_g, ln_emb_b=m_ln_emb_b, b_in=m_b_in, conv_w=m_conv_w, conv_b=m_conv_b, b_ra=m_b_ra,
                   b_ri=m_b_ri, lru_lambda=m_lru_lambda, sinks=m_sinks, b_o=m_b_o, ln_g=m_ln_g, ln_b=m_ln_b,
                   meta_tokens=m_meta_tokens)
    v_small = dict(ln_emb_g=v_ln_emb_g, ln_emb_b=v_ln_emb_b, b_in=v_b_in, conv_w=v_conv_w, conv_b=v_conv_b, b_ra=v_b_ra,
                   b_ri=v_b_ri, lru_lambda=v_lru_lambda, sinks=v_sinks, b_o=v_b_o, ln_g=v_ln_g, ln_b=v_ln_b,
                   meta_tokens=v_meta_tokens)
    g_small_pack = jnp.concatenate([small_g[n] for n in small_names] + [jnp.zeros((2, D), F32)], axis=0)
    small_res = _adamw(small_pack(w_small), g_small_pack, small_pack(m_small), small_pack(v_small),
                       name="adamw_small", tr=24)

    small_rows = {}
    r0 = 0
    for n in small_names:
        nrows = small_g[n].shape[0]
        small_rows[n] = (r0, nrows)
        r0 += nrows

    def small_out(packed, n, like):
        a, nrows = small_rows[n]
        flat = packed[a:a + nrows].reshape(-1)
        return flat[:like.size].reshape(like.shape)

    weights = dict(meta_tokens=meta_tokens, ln_emb_g=ln_emb_g, ln_emb_b=ln_emb_b, w_in=w_in, b_in=b_in, conv_w=conv_w,
                   conv_b=conv_b, w_ra=w_ra, b_ra=b_ra, w_ri=w_ri, b_ri=b_ri, lru_lambda=lru_lambda, sinks=sinks,
                   w_rnn_out=w_rnn_out, w_attn_out=w_attn_out, w_o=w_o, b_o=b_o, ln_g=ln_g, ln_b=ln_b)

    def outputs(which):
        return [big[n][which] if n in big else small_out(small_res[which], n, like) for n, like in weights.items()]

    return (loss, g_x[None], *outputs(0), *outputs(1), *outputs(2), *outputs(3))
```

```python
import jax
import jax.numpy as jnp
from jax import lax
from jax.experimental import pallas as pl
from jax.experimental.pallas import tpu as pltpu

F32 = jnp.float32
BF16 = jnp.bfloat16
_MXU = jnp.bfloat16

D = 2048
SEQ = 2048
N_META = 16
BLK = 128
PAD = BLK - N_META
TP = PAD + N_META + SEQ
NBLK = TP // BLK
HALF_TP = TP // 2
N_RB = 8
RB = 256
CONV_W = 4
LRU_C = 8.0
HD = 64
N_Q = 32
N_KV = 4
GRP = 8
D_KV = 256
NEG_INF = -1e30
LN_EPS = 1e-5
ALPHA = 2.0 ** 0.25
ROPE_THETA = 10000.0
OFF_GR, OFF_Q, OFF_K, OFF_V, OFF_GA, OFF_G = 2048, 4096, 6144, 6400, 6656, 8704
D_IN = 12800
N_SH = 4
W_IN_COLS = D_IN // N_SH
TN_IN = 640
PER_IN = W_IN_COLS // TN_IN
CW = 512
RT = TP // 4
SQ_ROWS = 512
PACK_ROWS = 3 * SQ_ROWS + 128

ADAM_LR = 0.001
ADAM_B1 = 0.9
ADAM_B2 = 0.999
ADAM_EPS = 1e-08
ADAM_WD = 0.01
ADAM_STEP = 10

MESH = pl.DeviceIdType.MESH
_MIB = 1024 * 1024
_ANY = pl.BlockSpec(memory_space=pl.ANY)
_NT = (((1,), (1,)), ((), ()))
_TN = (((0,), (0,)), ((), ()))


def _sds(shape, dtype):
    return jax.ShapeDtypeStruct(shape, dtype)


def _sigmoid(x):
    return 1.0 / (1.0 + jnp.exp(-x))


def _my_pos():
    return lax.axis_index("x"), lax.axis_index("y"), lax.axis_index("c")


def _other_chips(x, y):
    return [(1 - x, y), (x, 1 - y), (1 - x, 1 - y)]


def _remote(src, dst, send_sems, recv_sems, k, dev):
    return pltpu.make_async_remote_copy(src_ref=src, dst_ref=dst, send_sem=send_sems.at[k], recv_sem=recv_sems.at[k],
                                        device_id=dev, device_id_type=MESH)


class _Hook:
    def __init__(self, carried, landing, n_sems, start, finish, mid=None, mid_frac=0.5):
        self.carried, self.landing, self.n_sems, self.start, self.finish = list(carried), list(landing), n_sems, start, finish
        self.mid, self.mid_frac = mid, mid_frac


def _hook_gather(buf, mid_frac):
    half = buf.shape[1] // 2
    quarter = half // 2

    def geom(o, ss, rs):
        x, y, c = _my_pos()
        xn, yn, dg = _other_chips(x, y)
        slot = lambda p: 2 * p[0] + p[1]
        mine_rows = pl.ds(pl.multiple_of(c * half, 16), half)
        sib_rows = pl.ds(pl.multiple_of((1 - c) * half, 16), half)
        q_rows = lambda r: pl.ds(pl.multiple_of(c * half + r * quarter, 16), quarter)

        def cp(k, s, rows, dev):
            part = o.at[s, rows]
            return _remote(part, part, ss, rs, k, dev)

        return dict(
            direct=lambda k, s: cp(k, s, mine_rows, ((xn, yn)[k][0], (xn, yn)[k][1], c)),
            relay=lambda r, s: cp(2 + r, s, q_rows(r), ((yn, xn)[r][0], (yn, xn)[r][1], c)),
            sibling=lambda k, s, mine: cp(4 + k, s, mine_rows if mine else sib_rows, (x, y, 1 - c)),
            me=slot((x, y)), slots=(slot(xn), slot(yn), slot(dg)))

    def start(car, land, ss, rs):
        g = geom(car[0], ss, rs)
        g["direct"](0, g["me"]).start()
        g["direct"](1, g["me"]).start()

    def mid(car, land, ss, rs):
        g = geom(car[0], ss, rs)
        for k in range(2):
            g["direct"](k, g["slots"][k]).wait_recv()
            g["relay"](k, g["slots"][k]).start()
            g["sibling"](k, g["slots"][k], True).start()

    def finish(car, land, ss, rs):
        g = geom(car[0], ss, rs)
        dslot = g["slots"][2]
        g["relay"](0, dslot).wait_recv()
        g["relay"](1, dslot).wait_recv()
        g["sibling"](2, dslot, True).start()
        for k in range(3):
            g["sibling"](k, g["slots"][k], False).wait_recv()
        for k in range(2):
            g["direct"](k, g["me"]).wait_send()
            g["relay"](k, g["slots"][k]).wait_send()
        for k in range(3):
            g["sibling"](k, g["slots"][k], True).wait_send()

    return _Hook([buf], [], 7, start, finish, mid=mid, mid_frac=mid_frac)


def _hook_pair(g, half_rows):
    n, r, cc = g.shape
    h = r // 2 if half_rows else r

    def plan(car, land, ss, rs):
        x, y, c = _my_pos()
        cps = []
        for s in range(n):
            src = car[0].at[s, pl.ds(pl.multiple_of((1 - c) * h, 8), h)] if half_rows else car[0].at[s]
            cps.append(_remote(src, land[0].at[s], ss, rs, s, (x, y, 1 - c)))
        return cps

    def start(car, land, ss, rs):
        for cp in plan(car, land, ss, rs):
            cp.start()

    def finish(car, land, ss, rs):
        cps = plan(car, land, ss, rs)
        for cp in cps:
            cp.wait_recv()
        for cp in cps:
            cp.wait_send()

    return _Hook([g], [_sds((n, h, cc), g.dtype)], n, start, finish)


def _hook_scatter(s16):
    _, h, cc = s16.shape

    def plan(car, land, ss, rs):
        x, y, c = _my_pos()
        return [_remote(car[0].at[2 * px + py], land[0].at[k], ss, rs, k, (px, py, c))
                for k, (px, py) in enumerate(_other_chips(x, y))]

    def start(car, land, ss, rs):
        for cp in plan(car, land, ss, rs):
            cp.start()

    def finish(car, land, ss, rs):
        cps = plan(car, land, ss, rs)
        for cp in cps:
            cp.wait_recv()
        for cp in cps:
            cp.wait_send()

    return _Hook([s16], [_sds((3, h, cc), s16.dtype)], 3, start, finish)


def _hook_halves(full):
    h = full.shape[0] // 2

    def half_copy(car, ss, rs, which):
        x, y, c = _my_pos()
        rows = car[0].at[pl.ds(pl.multiple_of((c + which - 2 * c * which) * h, 8), h)]
        return _remote(rows, rows, ss, rs, 0, (x, y, 1 - c))

    def start(car, land, ss, rs):
        half_copy(car, ss, rs, 0).start()

    def finish(car, land, ss, rs):
        half_copy(car, ss, rs, 1).wait_recv()
        half_copy(car, ss, rs, 0).wait_send()

    return _Hook([full], [], 1, start, finish)


def _pcall(body, *, name, grid, in_specs, out_specs, out_shape, operands, scratch=(), vmem=48, sem=None,
           prefetch=(), aliases=None, hook=None):
    n_pre, n_in, n_out, n_scr = len(prefetch), len(in_specs), len(out_specs), len(scratch)
    in_specs, out_specs, out_shape, scratch = list(in_specs), list(out_specs), list(out_shape), list(scratch)
    io_alias = {n_pre + a: b for a, b in (aliases or {}).items()}
    operands = list(operands)
    kernel_body = body
    if hook is not None:
        n_car, n_land = len(hook.carried), len(hook.landing)
        for t, arr in enumerate(hook.carried):
            io_alias[n_pre + n_in + t] = n_out + t
        in_specs += [_ANY] * n_car
        out_specs += [_ANY] * (n_car + n_land)
        out_shape += [_sds(a.shape, a.dtype) for a in hook.carried] + hook.landing
        scratch += [pltpu.SemaphoreType.DMA((hook.n_sems,)), pltpu.SemaphoreType.DMA((hook.n_sems,))]
        operands += hook.carried
        sem = ("arbitrary",) * len(grid)

        def kernel_body(*refs):
            pre, rest = refs[:n_pre], refs[n_pre:]
            ins = rest[:n_in]
            outs = rest[n_in + n_car:n_in + n_car + n_out]
            car = rest[n_in + n_car + n_out:n_in + 2 * n_car + n_out]
            land = rest[n_in + 2 * n_car + n_out:n_in + 2 * n_car + n_out + n_land]
            scr = rest[n_in + 2 * n_car + n_out + n_land:]
            send_sems, recv_sems = scr[n_scr], scr[n_scr + 1]
            first = pl.program_id(0) == 0
            last = pl.program_id(0) == grid[0] - 1
            for d in range(1, len(grid)):
                first = first & (pl.program_id(d) == 0)
                last = last & (pl.program_id(d) == grid[d] - 1)

            @pl.when(first)
            def _():
                hook.start(car, land, send_sems, recv_sems)

            if hook.mid is not None:
                step = pl.program_id(0)
                total = grid[0]
                for d in range(1, len(grid)):
                    step = step * grid[d] + pl.program_id(d)
                    total *= grid[d]

                @pl.when(step == int(total * hook.mid_frac))
                def _():
                    hook.mid(car, land, send_sems, recv_sems)

            body(*pre, *ins, *outs, *scr[:n_scr])

            @pl.when(last)
            def _():
                hook.finish(car, land, send_sems, recv_sems)

    params = pltpu.CompilerParams(vmem_limit_bytes=vmem * _MIB, dimension_semantics=sem,
                                  has_side_effects=hook is not None)
    if n_pre:
        call = pl.pallas_call(
            kernel_body, name=name, out_shape=out_shape, input_output_aliases=io_alias, compiler_params=params,
            grid_spec=pltpu.PrefetchScalarGridSpec(num_scalar_prefetch=n_pre, grid=grid, in_specs=in_specs,
                                                   out_specs=out_specs, scratch_shapes=scratch))
    else:
        call = pl.pallas_call(kernel_body, name=name, grid=grid, in_specs=in_specs, out_specs=out_specs,
                              out_shape=out_shape, scratch_shapes=scratch, input_output_aliases=io_alias,
                              compiler_params=params)
    return call(*prefetch, *operands)


def _gather_small(blk):
    r, cc = blk.shape

    def body(x_ref, o_ref, send_sems, recv_sems):
        x, y, c = _my_pos()
        me = 2 * x + y
        o_ref[me] = x_ref[...]
        sends = [_remote(x_ref, o_ref.at[me], send_sems, recv_sems, k, (px, py, c))
                 for k, (px, py) in enumerate(_other_chips(x, y))]
        for cp in sends:
            cp.start()
        for k, (px, py) in enumerate(_other_chips(x, y)):
            _remote(x_ref, o_ref.at[2 * px + py], send_sems, recv_sems, k, (px, py, c)).wait_recv()
        for cp in sends:
            cp.wait_send()

    vm = pl.BlockSpec(memory_space=pltpu.VMEM)
    return pl.pallas_call(
        body, name="gather_small", in_specs=[vm], out_specs=vm, out_shape=_sds((N_SH, r, cc), blk.dtype),
        scratch_shapes=[pltpu.SemaphoreType.DMA((3,)), pltpu.SemaphoreType.DMA((3,))],
        compiler_params=pltpu.CompilerParams(has_side_effects=True),
    )(blk)


N_DEV = 8


def _allreduce_small(pack):
    r, cc = pack.shape

    def body(x_ref, o_ref, buf_ref, send_sems, recv_sems):
        x, y, c = _my_pos()
        me = 4 * x + 2 * y + c
        buf_ref[me] = x_ref[...]
        cps = []
        for k in range(1, N_DEV):
            peer = (x ^ ((k >> 2) & 1), y ^ ((k >> 1) & 1), c ^ (k & 1))
            cps.append(_remote(x_ref, buf_ref.at[me], send_sems, recv_sems, k - 1, peer))
        for cp in cps:
            cp.start()
        for k in range(1, N_DEV):
            peer = (x ^ ((k >> 2) & 1), y ^ ((k >> 1) & 1), c ^ (k & 1))
            src = 4 * peer[0] + 2 * peer[1] + peer[2]
            _remote(x_ref, buf_ref.at[src], send_sems, recv_sems, k - 1, peer).wait_recv()
        acc = buf_ref[0]
        for d in range(1, N_DEV):
            acc = acc + buf_ref[d]
        o_ref[...] = acc
        for cp in cps:
            cp.wait_send()

    vm = pl.BlockSpec(memory_space=pltpu.VMEM)
    return pl.pallas_call(
        body, name="allreduce_small", in_specs=[vm], out_specs=vm, out_shape=_sds((r, cc), F32),
        scratch_shapes=[pltpu.VMEM((N_DEV, r, cc), F32), pltpu.SemaphoreType.DMA((N_DEV - 1,)),
                        pltpu.SemaphoreType.DMA((N_DEV - 1,))],
        compiler_params=pltpu.CompilerParams(has_side_effects=True),
    )(pack)


def _mm_nn(a, w, bias, *, name, grid, tm, tn, k, a_index, w_block, w_index, out_cols, out_index, carried=None, hook=None):
    m = a.shape[0]

    def body(a_ref, w_ref, b_ref, *rest):
        o_ref = rest[-1]
        wv = w_ref[...]
        acc = jnp.dot(a_ref[...].astype(_MXU), wv.reshape(k, tn).astype(_MXU), preferred_element_type=F32)
        o_ref[...] = acc + b_ref[...]

    operands = [a, w, bias] + ([carried] if carried is not None else [])
    return _pcall(
        body, name=name, grid=grid,
        in_specs=[pl.BlockSpec((tm, k), a_index), pl.BlockSpec(w_block, w_index),
                  pl.BlockSpec((1, tn), lambda i, j: (0, j))] + ([_ANY] if carried is not None else []),
        out_specs=[pl.BlockSpec((tm, tn), out_index)], out_shape=[_sds((m, out_cols), F32)], operands=operands,
        vmem=56, sem=("parallel", "parallel"), aliases={3: 0} if carried is not None else None, hook=hook)


def _mm_nt(a, w, *, name, grid, tm, tn, tk, a_index, w_block, w_index, out_cols, hook=None):
    m = a.shape[0]
    nk = grid[2]

    def body(a_ref, w_ref, o_ref, acc_ref):
        part = lax.dot_general(a_ref[...].astype(_MXU), w_ref[...].astype(_MXU), _NT, preferred_element_type=F32)
        if nk == 1:
            o_ref[...] = part
        else:
            kidx = pl.program_id(2)

            @pl.when(kidx == 0)
            def _():
                acc_ref[...] = part

            @pl.when(kidx > 0)
            def _():
                acc_ref[...] += part

            @pl.when(kidx == nk - 1)
            def _():
                o_ref[...] = acc_ref[...]

    return _pcall(
        body, name=name, grid=grid, in_specs=[pl.BlockSpec((tm, tk), a_index), pl.BlockSpec(w_block, w_index)],
        out_specs=[pl.BlockSpec((tm, tn), lambda i, j, q: (i, j))], out_shape=[_sds((m, out_cols), F32)],
        operands=[a, w], scratch=[pltpu.VMEM((tm, tn) if nk > 1 else (8, 128), F32)], vmem=60,
        sem=("parallel", "parallel", "arbitrary"), hook=hook)


def _mm_tn(at, b, *, name, grid, tm, tn, a_index, b_index, out_shape, out_block, out_index, carried=None,
           prefetch=(), hook=None):
    t = at.shape[1]

    def body(*refs):
        a_ref, b_ref = refs[len(prefetch)], refs[len(prefetch) + 1]
        refs[-1][...] = jnp.dot(a_ref[...].astype(_MXU), b_ref[...].astype(_MXU), preferred_element_type=F32)

    operands = [at, b] + ([carried] if carried is not None else [])
    return _pcall(
        body, name=name, grid=grid,
        in_specs=[pl.BlockSpec((tm, t), a_index), pl.BlockSpec((t, tn), b_index)] + ([_ANY] if carried is not None else []),
        out_specs=[pl.BlockSpec(out_block, out_index)], out_shape=[_sds(out_shape, F32)], operands=operands,
        vmem=56, sem=("parallel", "parallel"), aliases={2: 0} if carried is not None else None, prefetch=prefetch, hook=hook)


def _mm_z_gather(h16, w_own, b_in, order):
    n_tiles = N_SH * PER_IN
    n_remote = 3 * PER_IN
    half = D // 2

    def body(order_ref, a_ref, b_ref, w_in_ref, z_ref, w_ref, wbuf, tile_sems, ss, rs):
        del w_in_ref
        j = pl.program_id(0)
        x, y, c = _my_pos()
        me = 2 * x + y
        chips = _other_chips(x, y)
        mine_rows = pl.ds(pl.multiple_of(c * half, 16), half)
        sib_rows = pl.ds(pl.multiple_of((1 - c) * half, 16), half)

        slots = [2 * px + py for px, py in chips]
        cols = lambda t: pl.ds(t * TN_IN, TN_IN)
        q_rows = lambda r: pl.ds(pl.multiple_of(c * half + r * (half // 2), 16), half // 2)

        def direct(rel, t, slot):
            px, py = chips[rel]
            part = w_ref.at[slot, mine_rows, cols(t)]
            return _remote(part, part, ss, rs, 2 * t + rel, (px, py, c))

        def relay(r, t, slot):
            px, py = chips[1 - r]
            part = w_ref.at[slot, q_rows(r), cols(t)]
            return _remote(part, part, ss, rs, 2 * PER_IN + 2 * t + r, (px, py, c))

        def d2d(n, rows):
            part = w_ref.at[slots[n % 3], rows, cols(n // 3)]
            return _remote(part, part, ss, rs, 4 * PER_IN + n, (x, y, 1 - c))

        def tile_copy(t):
            n = jnp.maximum(t - PER_IN, 0)
            rel = n % 3
            slot = jnp.where(t < PER_IN, me, me ^ jnp.where(rel == 0, 2, jnp.where(rel == 1, 1, 3)))
            col = pl.multiple_of(jnp.where(t < PER_IN, t, n // 3) * TN_IN, 128)
            return pltpu.make_async_copy(w_ref.at[slot, :, pl.ds(col, TN_IN)], wbuf.at[t % 2], tile_sems.at[t % 2])

        @pl.when(j == 0)
        def _():
            for t in range(PER_IN):
                direct(0, t, me).start()
                direct(1, t, me).start()
            tile_copy(0).start()

        for n in range(n_remote):
            rel, t = n % 3, n // 3

            @pl.when(j == n + PER_IN - 3)
            def _():
                if rel < 2:
                    direct(rel, t, slots[rel]).wait_recv()
                    relay(rel, t, slots[rel]).start()
                else:
                    relay(0, t, slots[2]).wait_recv()
                    relay(1, t, slots[2]).wait_recv()
                d2d(n, mine_rows).start()

            @pl.when(j == n + PER_IN - 2)
            def _():
                d2d(n, sib_rows).wait_recv()

        @pl.when(j + 1 < n_tiles)
        def _():
            tile_copy(j + 1).start()

        tile_copy(j).wait()
        z_ref[...] = jnp.dot(a_ref[...], wbuf[j % 2], preferred_element_type=F32) + b_ref[...]

        @pl.when(j == n_tiles - 1)
        def _():
            for t in range(PER_IN):
                for r in range(2):
                    direct(r, t, me).wait_send()
                    relay(r, t, slots[r]).wait_send()
            for n in range(n_remote):
                d2d(n, mine_rows).wait_send()

    def col_tile(j, o):
        n = jnp.maximum(j - PER_IN, 0)
        return 0, jnp.where(j < PER_IN, o[0] * PER_IN + j, o[1 + n % 3] * PER_IN + n // 3)

    return pl.pallas_call(
        body, name="mm_z_gather",
        grid_spec=pltpu.PrefetchScalarGridSpec(
            num_scalar_prefetch=1, grid=(n_tiles,),
            in_specs=[pl.BlockSpec((TP, D), lambda j, o: (0, 0)), pl.BlockSpec((1, TN_IN), col_tile), _ANY],
            out_specs=[pl.BlockSpec((TP, TN_IN), col_tile), _ANY],
            scratch_shapes=[pltpu.VMEM((2, D, TN_IN), BF16), pltpu.SemaphoreType.DMA((2,)),
                            pltpu.SemaphoreType.DMA((4 * PER_IN + n_remote,)),
                            pltpu.SemaphoreType.DMA((4 * PER_IN + n_remote,))]),
        out_shape=[_sds((TP, D_IN), F32), _sds(w_own.shape, w_own.dtype)],
        input_output_aliases={3: 1},
        compiler_params=pltpu.CompilerParams(vmem_limit_bytes=60 * _MIB, dimension_semantics=("arbitrary",),
                                             has_side_effects=True),
    )(order, h16, b_in, w_own)


def _padded_rows(i, x_ref, meta_ref):
    head = jnp.concatenate([jnp.zeros((PAD, D), F32), meta_ref[...]], axis=0)
    return jnp.where(i == 0, head, x_ref[...])


def _stream_specs():
    return [pl.BlockSpec((BLK, D), lambda i: (jnp.maximum(i - 1, 0), 0)), pl.BlockSpec((N_META, D), lambda i: (0, 0))]


def _ln_emb_fwd(x, meta, g, b):
    def body(x_ref, meta_ref, g_ref, b_ref, h32_ref, h16_ref, h16t_ref):
        x = _padded_rows(pl.program_id(0), x_ref, meta_ref)
        mu = jnp.mean(x, axis=-1, keepdims=True)
        xc = x - mu
        var = jnp.mean(xc * xc, axis=-1, keepdims=True)
        y = xc * lax.rsqrt(var + LN_EPS) * g_ref[...] + b_ref[...]
        h32_ref[...] = y
        h16_ref[...] = y.astype(_MXU)
        h16t_ref[...] = y.T.astype(_MXU)

    row = pl.BlockSpec((BLK, D), lambda i: (i, 0))
    vec = pl.BlockSpec((1, D), lambda i: (0, 0))
    return _pcall(body, name="ln_emb_fwd", grid=(NBLK,), in_specs=_stream_specs() + [vec, vec],
                  out_specs=[row, row, pl.BlockSpec((D, BLK), lambda i: (0, i))],
                  out_shape=[_sds((TP, D), F32), _sds((TP, D), _MXU), _sds((D, TP), _MXU)], operands=[x, meta, g, b],
                  vmem=32, sem=("parallel",))


def _ln_emb_bwd(x, meta, g, dr, dhz, *, hook):
    def body(x_ref, meta_ref, g_ref, dr_ref, dhz_ref, dx_ref, dmeta_ref, acc_ref):
        i = pl.program_id(0)

        @pl.when(i == 0)
        def _():
            acc_ref[...] = jnp.zeros_like(acc_ref)

        x = _padded_rows(i, x_ref, meta_ref)
        mu = jnp.mean(x, axis=-1, keepdims=True)
        xc = x - mu
        var = jnp.mean(xc * xc, axis=-1, keepdims=True)
        rstd = lax.rsqrt(var + LN_EPS)
        xhat = xc * rstd
        dh = ALPHA * dr_ref[...] + dhz_ref[...]
        acc_ref[0:1, :] += jnp.sum(dh * xhat, axis=0, keepdims=True)
        acc_ref[1:2, :] += jnp.sum(dh, axis=0, keepdims=True)
        dxh = dh * g_ref[...]
        m1 = jnp.mean(dxh, axis=-1, keepdims=True)
        m2 = jnp.mean(dxh * xhat, axis=-1, keepdims=True)
        dx = rstd * (dxh - m1 - xhat * m2)
        dx_ref[...] = dx

        @pl.when(i == 0)
        def _():
            dmeta_ref[...] = dx[PAD:BLK]

    row = pl.BlockSpec((BLK, D), lambda i: (i, 0))
    vec = pl.BlockSpec((1, D), lambda i: (0, 0))
    xs, ms = _stream_specs()
    return _pcall(body, name="ln_emb_bwd", grid=(NBLK,), in_specs=[xs, ms, vec, row, row],
                  out_specs=[xs, ms, pl.BlockSpec((8, D), lambda i: (0, 0))],
                  out_shape=[_sds((SEQ, D), F32), _sds((N_META, D), F32), _sds((8, D), F32)],
                  operands=[x, meta, g, dr, dhz], vmem=32, sem=("arbitrary",), hook=hook)


def _mul_silu_fwd(a, z, off, *, name):
    def body(a_ref, z_ref, o_ref, t_ref):
        zz = z_ref[...]
        y = a_ref[...] * (zz * _sigmoid(zz))
        o_ref[...] = y.astype(_MXU)
        t_ref[...] = y.T.astype(_MXU)

    strip = pl.BlockSpec((TP, CW), lambda j: (0, j))
    return _pcall(body, name=name, grid=(D // CW,),
                  in_specs=[strip, pl.BlockSpec((TP, CW), lambda j: (0, off // CW + j))],
                  out_specs=[strip, pl.BlockSpec((CW, TP), lambda j: (j, 0))],
                  out_shape=[_sds((TP, D), _MXU), _sds((D, TP), _MXU)], operands=[a, z], vmem=56, sem=("parallel",))


def _mul_silu_bwd(dy, a, z, off, dz, *, name, hook=None):
    def body(dy_ref, a_ref, z_ref, dz_in, da_ref, dg_ref):
        zz = z_ref[...]
        sg = _sigmoid(zz)
        d = dy_ref[...]
        da_ref[...] = d * (zz * sg)
        dg_ref[...] = (d * a_ref[...] * (sg * (1.0 + zz * (1.0 - sg)))).astype(_MXU)

    blk = pl.BlockSpec((RT, CW), lambda i, j: (i, j))
    zblk = pl.BlockSpec((RT, CW), lambda i, j: (i, off // CW + j))
    return _pcall(body, name=name, grid=(TP // RT, D // CW), in_specs=[blk, blk, zblk, _ANY], out_specs=[blk, zblk],
                  out_shape=[_sds((TP, D), F32), _sds((TP, D_IN), _MXU)], operands=[dy, a, z, dz], vmem=32,
                  sem=("parallel", "parallel"), aliases={3: 1}, hook=hook)


def _merge_fwd(y2, z):
    w = 256

    def body(ya_ref, yb_ref, ga_ref, gb_ref, o_ref, t_ref):
        y = _sigmoid(ga_ref[...]) * ya_ref[...] + _sigmoid(gb_ref[...]) * yb_ref[...]
        o_ref[...] = y.astype(_MXU)
        t_ref[...] = y.T.astype(_MXU)

    nb = D // w
    strip = pl.BlockSpec((TP, w), lambda j: (0, j))
    return _pcall(body, name="merge_fwd", grid=(nb,),
                  in_specs=[strip, pl.BlockSpec((TP, w), lambda j: (0, nb + j)),
                            pl.BlockSpec((TP, w), lambda j: (0, OFF_G // w + j)),
                            pl.BlockSpec((TP, w), lambda j: (0, (OFF_G + D) // w + j))],
                  out_specs=[strip, pl.BlockSpec((w, TP), lambda j: (j, 0))],
                  out_shape=[_sds((TP, D), _MXU), _sds((D, TP), _MXU)], operands=[y2, y2, z, z], vmem=56,
                  sem=("parallel",))


def _merge_bwd(dmix, y2, z):
    nb = D // CW

    def body(dm_ref, y_ref, g_ref, dy_ref, dg_ref):
        dm = dm_ref[...]
        sg = _sigmoid(g_ref[...])
        dy_ref[...] = (dm * sg).astype(_MXU)
        dg_ref[...] = (dm * y_ref[...] * sg * (1.0 - sg)).astype(_MXU)

    blk = pl.BlockSpec((RT, CW), lambda i, j: (i, j))
    gblk = pl.BlockSpec((RT, CW), lambda i, j: (i, OFF_G // CW + j))
    return _pcall(body, name="merge_bwd", grid=(TP // RT, 2 * nb),
                  in_specs=[pl.BlockSpec((RT, CW), lambda i, j: (i, j % nb)), blk, gblk], out_specs=[blk, gblk],
                  out_shape=[_sds((TP, 2 * D), _MXU), _sds((TP, D_IN), _MXU)], operands=[dmix, y2, z], vmem=32,
                  sem=("parallel", "parallel"))


def _ln_out_loss(h32, out, target, g, b):
    def body(h_ref, o_ref, t_ref, g_ref, b_ref, dr_ref, acc_ref):
        i = pl.program_id(0)

        @pl.when(i == 0)
        def _():
            acc_ref[...] = jnp.zeros_like(acc_ref)

        r = ALPHA * h_ref[...] + o_ref[...]
        mu = jnp.mean(r, axis=-1, keepdims=True)
        rc = r - mu
        var = jnp.mean(rc * rc, axis=-1, keepdims=True)
        rstd = lax.rsqrt(var + LN_EPS)
        xhat = rc * rstd
        gg = g_ref[...]
        y = xhat * gg + b_ref[...]
        real = (i >= 1).astype(F32)
        diff = (y - t_ref[...]) * real
        dy = diff * (1.0 / D)
        dxh = dy * gg
        m1 = jnp.mean(dxh, axis=-1, keepdims=True)
        m2 = jnp.mean(dxh * xhat, axis=-1, keepdims=True)
        dr = rstd * (dxh - m1 - xhat * m2)
        dr_ref[...] = dr
        acc_ref[0:1, :] += jnp.sum(dy * xhat, axis=0, keepdims=True)
        acc_ref[1:2, :] += jnp.sum(dy, axis=0, keepdims=True)
        acc_ref[2:3, :] += jnp.sum(dr, axis=0, keepdims=True)
        acc_ref[3:4, :] += (0.5 / D) * jnp.sum(diff * diff)

    row = pl.BlockSpec((BLK, D), lambda i: (i, 0))
    vec = pl.BlockSpec((1, D), lambda i: (0, 0))
    return _pcall(body, name="ln_out_loss", grid=(NBLK,),
                  in_specs=[row, row, pl.BlockSpec((BLK, D), lambda i: (jnp.maximum(i - 1, 0), 0)), vec, vec],
                  out_specs=[row, pl.BlockSpec((8, D), lambda i: (0, 0))],
                  out_shape=[_sds((TP, D), F32), _sds((8, D), F32)], operands=[h32, out, target, g, b], vmem=32,
                  sem=("arbitrary",))


def _colsum(x, *, name, tn, hook=None):
    _, n = x.shape

    def body(x_ref, o_ref):
        o_ref[...] = jnp.sum(x_ref[...].astype(F32), axis=0, keepdims=True)

    return _pcall(body, name=name, grid=(n // tn,), in_specs=[pl.BlockSpec((TP, tn), lambda j: (0, j))],
                  out_specs=[pl.BlockSpec((1, tn), lambda j: (0, j))], out_shape=[_sds((1, n), F32)], operands=[x],
                  vmem=32, sem=("parallel",), hook=hook)


def _rnn_recompute(xr_ref, cw_ref, cb_ref, wra_ref, wri_ref, bra_ref, bri_ref, lam_ref):
    rows = lax.broadcasted_iota(jnp.int32, (TP, 1), 0)
    valid = (rows >= PAD).astype(F32)
    first = rows == PAD
    x = xr_ref[...] * valid
    cw = cw_ref[...]
    shifted = [x, pltpu.roll(x, 1, 0), pltpu.roll(x, 2, 0), pltpu.roll(x, 3, 0)]
    c = cb_ref[...] + cw[0:1, :] * shifted[0] + cw[1:2, :] * shifted[1] + cw[2:3, :] * shifted[2] + cw[3:4, :] * shifted[3]
    cm = c.astype(_MXU)
    gr = _sigmoid(jnp.dot(cm, wra_ref[...].astype(_MXU), preferred_element_type=F32) + bra_ref[...])
    gi = _sigmoid(jnp.dot(cm, wri_ref[...].astype(_MXU), preferred_element_type=F32) + bri_ref[...])
    lam = lam_ref[...]
    ls = jnp.minimum(lam, 0.0) - jnp.log(1.0 + jnp.exp(-jnp.abs(lam)))
    log_a = LRU_C * gr * ls
    a = jnp.exp(log_a)
    mult = jnp.where(first, 1.0, jnp.sqrt(1.0 - jnp.exp(2.0 * log_a)))
    return dict(valid=valid, first=first, shifted=shifted, c=c, cm=cm, gr=gr, gi=gi, ls=ls, a=a, mult=mult, lam=lam)


def _rnn_specs():
    col = pl.BlockSpec((TP, RB), lambda n: (0, n))
    vec = pl.BlockSpec((1, RB), lambda n: (0, n))
    return dict(col=col, vec=vec, cw=pl.BlockSpec((CONV_W, RB), lambda n: (0, n)),
                wblk=pl.BlockSpec((None, RB, RB), lambda n: (n, 0, 0)))


def _rnn_gates_fwd(z, conv_w, conv_b, w_ra, w_ri, b_ra, b_ri, lam):
    def body(xr_ref, cw_ref, cb_ref, wra_ref, wri_ref, bra_ref, bri_ref, lam_ref, a_ref, u_ref):
        r = _rnn_recompute(xr_ref, cw_ref, cb_ref, wra_ref, wri_ref, bra_ref, bri_ref, lam_ref)
        a_ref[...] = r["a"]
        u_ref[...] = r["mult"] * r["gi"] * r["c"] * r["valid"]

    s = _rnn_specs()
    return _pcall(body, name="rnn_gates_fwd", grid=(N_RB,),
                  in_specs=[s["col"], s["cw"], s["vec"], s["wblk"], s["wblk"], s["vec"], s["vec"], s["vec"]],
                  out_specs=[s["col"], s["col"]], out_shape=[_sds((TP, D), F32)] * 2,
                  operands=[z, conv_w, conv_b, w_ra, w_ri, b_ra, b_ri, lam], vmem=56, sem=("parallel",))


SCAN_ROWS = 272


SUB = 8


def _tile_scan(a, u, reverse):
    rows = lax.broadcasted_iota(jnp.int32, a.shape, 0)
    for d in (1, 2, 4):
        shift = SUB - d if reverse else d
        inside = (rows < SUB - d) if reverse else (rows >= d)
        u = u + a * jnp.where(inside, pltpu.roll(u, shift, 0), 0.0)
        a = a * jnp.where(inside, pltpu.roll(a, shift, 0), 1.0)
    return a, u


def _scan_fwd(a, u, *, hook):
    def body(a_ref, u_ref, h_ref, carry_ref):
        @pl.when(pl.program_id(1) == 0)
        def _():
            carry_ref[...] = jnp.zeros_like(carry_ref)

        def step(r, h):
            rows = pl.ds(pl.multiple_of(r * SUB, SUB), SUB)
            prod, part = _tile_scan(a_ref[rows, :], u_ref[rows, :], False)
            ht = part + prod * h
            h_ref[rows, :] = ht
            return ht[SUB - 1:SUB, :]

        carry_ref[...] = lax.fori_loop(0, SCAN_ROWS // SUB, step, carry_ref[...], unroll=2)

    blk = pl.BlockSpec((SCAN_ROWS, CW), lambda j, i: (i, j))
    return _pcall(body, name="scan_fwd", grid=(D // CW, TP // SCAN_ROWS), in_specs=[blk, blk], out_specs=[blk],
                  out_shape=[_sds((TP, D), F32)], operands=[a, u], scratch=[pltpu.VMEM((1, CW), F32)], vmem=32,
                  sem=("parallel", "arbitrary"), hook=hook)


def _scan_bwd(a, dh):
    nst = TP // SCAN_ROWS
    n_tiles = SCAN_ROWS // SUB

    def body(a_ref, d_ref, o_ref, lam_ref, anext_ref):
        @pl.when(pl.program_id(1) == 0)
        def _():
            lam_ref[...] = jnp.zeros_like(lam_ref)
            anext_ref[...] = jnp.zeros_like(anext_ref)

        def step(q, carry):
            lam_next, a_next = carry
            rows = pl.ds(pl.multiple_of((n_tiles - 1 - q) * SUB, SUB), SUB)
            at = a_ref[rows, :]
            last = lax.broadcasted_iota(jnp.int32, at.shape, 0) == SUB - 1
            b = jnp.where(last, a_next, pltpu.roll(at, SUB - 1, 0))
            prod, part = _tile_scan(b, d_ref[rows, :], True)
            lam = part + prod * lam_next
            o_ref[rows, :] = lam
            return lam[0:1, :], at[0:1, :]

        lam, an = lax.fori_loop(0, n_tiles, step, (lam_ref[...], anext_ref[...]), unroll=2)
        lam_ref[...] = lam
        anext_ref[...] = an

    blk = pl.BlockSpec((SCAN_ROWS, CW), lambda j, i: (nst - 1 - i, j))
    return _pcall(body, name="scan_bwd", grid=(D // CW, nst), in_specs=[blk, blk], out_specs=[blk],
                  out_shape=[_sds((TP, D), F32)], operands=[a, dh],
                  scratch=[pltpu.VMEM((1, CW), F32), pltpu.VMEM((1, CW), F32)], vmem=32,
                  sem=("parallel", "arbitrary"))[0]


def _rnn_gates_bwd(z, lam_s, hr, conv_w, conv_b, w_ra, w_ri, b_ra, b_ri, lam, dz, gsq):
    def body(xr_ref, ls_ref, hr_ref, cw_ref, cb_ref, wra_ref, wri_ref, bra_ref, bri_ref, lam_ref, dz_in, gsq_in,
             dx_ref, dw_ref, sums_ref):
        r = _rnn_recompute(xr_ref, cw_ref, cb_ref, wra_ref, wri_ref, bra_ref, bri_ref, lam_ref)
        valid, c, gr, gi, a, mult = r["valid"], r["c"], r["gr"], r["gi"], r["a"], r["mult"]
        du = ls_ref[...] * valid
        da = du * pltpu.roll(hr_ref[...], 1, 0)
        d_gi = du * mult * c
        dc = du * mult * gi
        dmult = du * gi * c
        dlog_a = da * a + jnp.where(r["first"], 0.0, -dmult * a * a / mult)
        d_gr = dlog_a * (LRU_C * r["ls"])
        dls = jnp.sum(dlog_a * (LRU_C * gr), axis=0, keepdims=True)
        dpre_r = d_gr * gr * (1.0 - gr)
        dpre_i = d_gi * gi * (1.0 - gi)
        pr = dpre_r.astype(_MXU)
        pi = dpre_i.astype(_MXU)
        dwra = lax.dot_general(r["cm"], pr, _TN, preferred_element_type=F32)
        dwri = lax.dot_general(r["cm"], pi, _TN, preferred_element_type=F32)
        for s in range(N_SH):
            dw_ref[s, 0:64, :] = dwra[64 * s:64 * (s + 1)]
            dw_ref[s, 64:128, :] = dwri[64 * s:64 * (s + 1)]
        dc = dc + lax.dot_general(pr, wra_ref[...].astype(_MXU), _NT, preferred_element_type=F32)
        dc = dc + lax.dot_general(pi, wri_ref[...].astype(_MXU), _NT, preferred_element_type=F32)
        cw = cw_ref[...]
        dx = cw[0:1, :] * dc
        for k in range(1, CONV_W):
            dx = dx + cw[k:k + 1, :] * pltpu.roll(dc, TP - k, 0)
        dx_ref[...] = (dx * valid).astype(_MXU)
        for k in range(CONV_W):
            sums_ref[k:k + 1, :] = jnp.sum(dc * r["shifted"][k], axis=0, keepdims=True)
        sums_ref[4:5, :] = jnp.sum(dc, axis=0, keepdims=True)
        sums_ref[5:6, :] = jnp.sum(dpre_r, axis=0, keepdims=True)
        sums_ref[6:7, :] = jnp.sum(dpre_i, axis=0, keepdims=True)
        sums_ref[7:8, :] = dls * _sigmoid(-r["lam"])

    s = _rnn_specs()
    return _pcall(
        body, name="rnn_gates_bwd", grid=(N_RB,),
        in_specs=[s["col"], s["col"], s["col"], s["cw"], s["vec"], s["wblk"], s["wblk"], s["vec"], s["vec"], s["vec"],
                  _ANY, _ANY],
        out_specs=[s["col"], pl.BlockSpec((N_SH, 128, RB), lambda n: (0, 3 * SQ_ROWS // 128, n)),
                   pl.BlockSpec((8, RB), lambda n: (0, n))],
        out_shape=[_sds((TP, D_IN), _MXU), _sds((N_SH, PACK_ROWS, D), F32), _sds((8, D), F32)],
        operands=[z, lam_s, hr, conv_w, conv_b, w_ra, w_ri, b_ra, b_ri, lam, dz, gsq], vmem=60, sem=("parallel",),
        aliases={10: 0, 11: 1})


def _rope_tables():
    half = HD // 2
    inv = ROPE_THETA ** (-jnp.arange(half, dtype=F32) / half)
    pos = (jnp.arange(TP) - PAD).astype(F32)
    ang = pos[:, None] * inv[None, :]
    return jnp.tile(jnp.cos(ang), (1, 4)), jnp.tile(jnp.sin(ang), (1, 4))


def _rope(x, cos_t, sin_t, sign):
    w = x.shape[1]
    lane = lax.broadcasted_iota(jnp.int32, x.shape, 1)
    first = (lane % HD) < (HD // 2)
    swapped = jnp.where(first, pltpu.roll(x, w - HD // 2, 1), pltpu.roll(x, HD // 2, 1))
    ct = jnp.tile(cos_t, (1, w // 128))
    st = jnp.tile(sin_t, (1, w // 128))
    return x * ct + swapped * jnp.where(first, -sign * st, sign * st)


def _rope_fwd(z, cos_t, sin_t):
    def body(q_ref, k_ref, v_ref, c_ref, s_ref, qo_ref, ko_ref, vo_ref):
        c = c_ref[...]
        s = s_ref[...]
        qo_ref[...] = _rope(q_ref[...], c, s, 1.0).astype(_MXU)
        ko_ref[...] = _rope(k_ref[...], c, s, 1.0).astype(_MXU)
        vo_ref[...] = v_ref[...].astype(_MXU)

    tab = pl.BlockSpec((BLK, 128), lambda i: (i, 0))
    kv = pl.BlockSpec((BLK, D_KV), lambda i: (i, 0))
    return _pcall(body, name="rope_fwd", grid=(NBLK,),
                  in_specs=[pl.BlockSpec((BLK, D), lambda i: (i, OFF_Q // D)),
                            pl.BlockSpec((BLK, D_KV), lambda i: (i, OFF_K // D_KV)),
                            pl.BlockSpec((BLK, D_KV), lambda i: (i, OFF_V // D_KV)), tab, tab],
                  out_specs=[pl.BlockSpec((BLK, D), lambda i: (i, 0)), kv, kv],
                  out_shape=[_sds((TP, D), _MXU), _sds((TP, D_KV), _MXU), _sds((TP, D_KV), _MXU)],
                  operands=[z, z, z, cos_t, sin_t], vmem=32, sem=("parallel",))


def _rope_bwd_k(dk, dv, cos_t, sin_t, dz):
    def body(dk_ref, dv_ref, c_ref, s_ref, dz_in, o_ref):
        o_ref[:, 0:D_KV] = _rope(dk_ref[...], c_ref[...], s_ref[...], -1.0).astype(_MXU)
        o_ref[:, D_KV:2 * D_KV] = dv_ref[...].astype(_MXU)

    tab = pl.BlockSpec((BLK, 128), lambda i: (i, 0))
    kv = pl.BlockSpec((BLK, D_KV), lambda i: (i, 0))
    return _pcall(body, name="rope_bwd_k", grid=(NBLK,), in_specs=[kv, kv, tab, tab, _ANY],
                  out_specs=[pl.BlockSpec((BLK, 2 * D_KV), lambda i: (i, OFF_K // (2 * D_KV)))],
                  out_shape=[_sds((TP, D_IN), _MXU)], operands=[dk, dv, cos_t, sin_t, dz], vmem=32, sem=("parallel",),
                  aliases={4: 0})[0]


def _attn_mask(i):
    ql = lax.broadcasted_iota(jnp.int32, (BLK, 3 * BLK), 0)
    kk = lax.broadcasted_iota(jnp.int32, (BLK, 3 * BLK), 1)
    kl = kk % BLK
    part = kk // BLK
    meta = (part == 0) & (kl >= PAD) & ((i >= 1) | (kl <= ql))
    prev = (part == 1) & (i >= 2) & (kl > ql)
    cur = (part == 2) & (i >= 1) & (kl <= ql)
    return meta | prev | cur


def _cat_kv(refs, g):
    return jnp.concatenate([r[:, HD * g:HD * (g + 1)] for r in refs], axis=0)


def _kv_specs():
    return [pl.BlockSpec((BLK, D_KV), lambda i: (0, 0)),
            pl.BlockSpec((BLK, D_KV), lambda i: (jnp.maximum(i - 1, 0), 0)),
            pl.BlockSpec((BLK, D_KV), lambda i: (i, 0))]


def _stack_heads(ref, g, width=HD):
    return jnp.concatenate([ref[:, width * (GRP * g + j):width * (GRP * g + j + 1)] for j in range(GRP)], axis=0)


def _attn_fwd(q, k, v, sinks, *, hook):
    def body(q_ref, k0_ref, kp_ref, kc_ref, v0_ref, vp_ref, vc_ref, sink_ref, o_ref, lse_ref, s_scr, p_scr):
        mask = _attn_mask(pl.program_id(0))
        for g in range(N_KV):
            s_scr[...] = lax.dot_general(_stack_heads(q_ref, g), _cat_kv((k0_ref, kp_ref, kc_ref), g), _NT,
                                         preferred_element_type=F32)
            for j in range(GRP):
                h = GRP * g + j
                rows = slice(BLK * j, BLK * (j + 1))
                sink = sink_ref[h]
                s = jnp.where(mask, s_scr[rows, :] * (HD ** -0.5), NEG_INF)
                mx = jnp.maximum(jnp.max(s, -1, keepdims=True), sink)
                p = jnp.exp(s - mx)
                den = jnp.sum(p, -1, keepdims=True) + jnp.exp(sink - mx)
                p_scr[rows, :] = (p * (1.0 / den)).astype(_MXU)
                lse_ref[:, h:h + 1] = mx + jnp.log(den)
            o8 = jnp.dot(p_scr[...], _cat_kv((v0_ref, vp_ref, vc_ref), g), preferred_element_type=F32)
            for j in range(GRP):
                h = GRP * g + j
                o_ref[:, HD * h:HD * (h + 1)] = o8[BLK * j:BLK * (j + 1)]

    return _pcall(body, name="attn_fwd", grid=(NBLK,),
                  in_specs=[pl.BlockSpec((BLK, D), lambda i: (i, 0))] + _kv_specs() + _kv_specs()
                           + [pl.BlockSpec(memory_space=pltpu.SMEM)],
                  out_specs=[pl.BlockSpec((BLK, D), lambda i: (i, 0)), pl.BlockSpec((BLK, N_Q), lambda i: (i, 0))],
                  out_shape=[_sds((TP, D), F32), _sds((TP, N_Q), F32)], operands=[q, k, k, k, v, v, v, sinks],
                  scratch=[pltpu.VMEM((GRP * BLK, 3 * BLK), F32), pltpu.VMEM((GRP * BLK, 3 * BLK), _MXU)],
                  vmem=40, sem=("parallel",), hook=hook)


def _attn_bwd(q, k, v, sinks, do, o, lse, cos_t, sin_t, dz, *, hook):
    def body(q_ref, k0_ref, kp_ref, kc_ref, v0_ref, vp_ref, vc_ref, sink_ref, do_ref, o_ref, lse_ref, c_ref, s_ref, dz_in,
             dq_ref, dk_ref, dv_ref, dsink_ref, dqrot_ref, s_scr, dp_scr, p_scr, ds_scr):
        i = pl.program_id(0)

        @pl.when(i == 0)
        def _():
            dk_ref[...] = jnp.zeros_like(dk_ref)
            dv_ref[...] = jnp.zeros_like(dv_ref)
            dsink_ref[...] = jnp.zeros_like(dsink_ref)

        mask = _attn_mask(i)
        row_starts = (0, pl.multiple_of(jnp.maximum(i - 1, 0) * BLK, BLK), pl.multiple_of(i * BLK, BLK))
        scale = HD ** -0.5
        for g in range(N_KV):
            gs = slice(HD * g, HD * (g + 1))
            q8 = _stack_heads(q_ref, g)
            dom = _stack_heads(do_ref, g).astype(_MXU)
            kcat = _cat_kv((k0_ref, kp_ref, kc_ref), g)
            s_scr[...] = lax.dot_general(q8, kcat, _NT, preferred_element_type=F32)
            dp_scr[...] = lax.dot_general(dom, _cat_kv((v0_ref, vp_ref, vc_ref), g), _NT, preferred_element_type=F32)
            for j in range(GRP):
                h = GRP * g + j
                hs = slice(HD * h, HD * (h + 1))
                rows = slice(BLK * j, BLK * (j + 1))
                lse_h = lse_ref[:, h:h + 1]
                delta = jnp.sum(do_ref[:, hs] * o_ref[:, hs], axis=-1, keepdims=True)
                dsink_ref[0:1, h:h + 1] += jnp.sum(-jnp.exp(sink_ref[h] - lse_h) * delta, axis=0, keepdims=True)
                p = jnp.exp(jnp.where(mask, s_scr[rows, :] * scale, NEG_INF) - lse_h)
                p_scr[rows, :] = p.astype(_MXU)
                ds_scr[rows, :] = (p * (dp_scr[rows, :] - delta) * scale).astype(_MXU)
            ds = ds_scr[...]
            dq8 = jnp.dot(ds, kcat, preferred_element_type=F32)
            dkcat = lax.dot_general(ds, q8, _TN, preferred_element_type=F32)
            dvcat = lax.dot_general(p_scr[...], dom, _TN, preferred_element_type=F32)
            for part in range(3):
                rows = pl.ds(row_starts[part], BLK)
                dk_ref[rows, gs] += dkcat[BLK * part:BLK * (part + 1)]
                dv_ref[rows, gs] += dvcat[BLK * part:BLK * (part + 1)]
            for j in range(GRP):
                h = GRP * g + j
                dqrot_ref[:, HD * h:HD * (h + 1)] = dq8[BLK * j:BLK * (j + 1)]
        dq_ref[...] = _rope(dqrot_ref[...], c_ref[...], s_ref[...], -1.0).astype(_MXU)

    row = pl.BlockSpec((BLK, D), lambda i: (i, 0))
    tab = pl.BlockSpec((BLK, 128), lambda i: (i, 0))
    full_kv = pl.BlockSpec((TP, D_KV), lambda i: (0, 0))
    return _pcall(
        body, name="attn_bwd", grid=(NBLK,),
        in_specs=[row] + _kv_specs() + _kv_specs() + [pl.BlockSpec(memory_space=pltpu.SMEM), row, row,
                  pl.BlockSpec((BLK, N_Q), lambda i: (i, 0)), tab, tab, _ANY],
        out_specs=[pl.BlockSpec((BLK, D), lambda i: (i, OFF_Q // D)), full_kv, full_kv,
                   pl.BlockSpec((8, 128), lambda i: (0, 0))],
        out_shape=[_sds((TP, D_IN), _MXU), _sds((TP, D_KV), F32), _sds((TP, D_KV), F32), _sds((8, 128), F32)],
        operands=[q, k, k, k, v, v, v, sinks, do, o, lse, cos_t, sin_t, dz],
        scratch=[pltpu.VMEM((BLK, D), F32), pltpu.VMEM((GRP * BLK, 3 * BLK), F32), pltpu.VMEM((GRP * BLK, 3 * BLK), F32),
                 pltpu.VMEM((GRP * BLK, 3 * BLK), _MXU), pltpu.VMEM((GRP * BLK, 3 * BLK), _MXU)],
        vmem=48, sem=("arbitrary",), aliases={13: 0}, hook=hook)


def _cast_into_slot(w32, pos, *, name, tr):
    r, cc = w32.shape

    def body(pos_ref, w_ref, o_ref):
        o_ref[...] = w_ref[...].astype(BF16)

    return _pcall(body, name=name, grid=(r // tr,), in_specs=[pl.BlockSpec((tr, cc), lambda i, p: (i, 0))],
                  out_specs=[pl.BlockSpec((None, tr, cc), lambda i, p: (p[1], i, 0))],
                  out_shape=[_sds((N_SH, r, cc), BF16)], operands=[w32], vmem=32, sem=("parallel",), prefetch=(pos,))[0]


def _pair_add(g, got, pos, *, name, tr, g_has_both_halves):
    n, h, cc = got.shape
    nt = h // tr

    def body(pos_ref, g_ref, r_ref, own_ref, s16_ref):
        s = g_ref[...] + r_ref[...]
        s16_ref[...] = s.astype(BF16)

        @pl.when(pl.program_id(1) == pos_ref[1])
        def _():
            own_ref[...] = s

    g_index = (lambda i, s, p: (s, p[0] * nt + i, 0)) if g_has_both_halves else (lambda i, s, p: (s, i, 0))
    return _pcall(body, name=name, grid=(nt, n),
                  in_specs=[pl.BlockSpec((None, tr, cc), g_index), pl.BlockSpec((None, tr, cc), lambda i, s, p: (s, i, 0))],
                  out_specs=[pl.BlockSpec((tr, cc), lambda i, s, p: (i, 0)),
                             pl.BlockSpec((None, tr, cc), lambda i, s, p: (s, i, 0))],
                  out_shape=[_sds((h, cc), F32), _sds((n, h, cc), BF16)], operands=[g, got], vmem=40,
                  sem=("parallel", "arbitrary"), prefetch=(pos,))


def _sum_chips(own, got, pos, *, name, tr):
    h, cc = own.shape
    nt = h // tr

    def body(pos_ref, o_ref, r_ref, out_ref):
        acc = o_ref[...]
        for k in range(3):
            acc = acc + r_ref[k].astype(F32)
        out_ref[...] = acc

    return _pcall(body, name=name, grid=(nt,),
                  in_specs=[pl.BlockSpec((tr, cc), lambda i, p: (i, 0)), pl.BlockSpec((3, tr, cc), lambda i, p: (0, i, 0))],
                  out_specs=[pl.BlockSpec((tr, cc), lambda i, p: (p[0] * nt + i, 0))],
                  out_shape=[_sds((2 * h, cc), F32)], operands=[own, got], vmem=40, sem=("parallel",), prefetch=(pos,))[0]


def _adamw(w, g, m, v, *, name, tr, g_row0=0):
    r, cc = w.shape
    g_blk0 = g_row0 // tr

    def body(w_ref, g_ref, m_ref, v_ref, go_ref, d_ref, mo_ref, vo_ref):
        gg = g_ref[...]
        go_ref[...] = gg
        m_new = ADAM_B1 * m_ref[...] + (1.0 - ADAM_B1) * gg
        v_new = ADAM_B2 * v_ref[...] + (1.0 - ADAM_B2) * (gg * gg)
        m_hat = m_new / (1.0 - ADAM_B1 ** ADAM_STEP)
        v_hat = v_new / (1.0 - ADAM_B2 ** ADAM_STEP)
        d_ref[...] = -ADAM_LR * (m_hat / (jnp.sqrt(v_hat) + ADAM_EPS) + ADAM_WD * w_ref[...])
        mo_ref[...] = m_new
        vo_ref[...] = v_new

    blk = pl.BlockSpec((tr, cc), lambda i: (i, 0))
    gblk = pl.BlockSpec((tr, cc), lambda i: (g_blk0 + i, 0))
    return _pcall(body, name=name, grid=(r // tr,), in_specs=[blk, gblk, blk, blk], out_specs=[blk] * 4,
                  out_shape=[_sds((r, cc), F32)] * 4, operands=[w, g, m, v], vmem=48, sem=("parallel",))


_SMALL_ROWS = 40
_B_IN_ROWS = 7


def _row_pad(v, rows):
    flat = v.reshape(-1)
    return jnp.pad(flat, (0, rows * D - flat.shape[0])).reshape(rows, D)


def _pack_ra(w):
    return w.transpose(1, 0, 2).reshape(64, D)


def _unpack_ra(p, like):
    return p.reshape(64, N_RB, RB).transpose(1, 0, 2).reshape(like.shape)


def _gate_full(g4):
    return g4.reshape(N_SH, 64, N_RB, RB).transpose(2, 0, 1, 3).reshape(N_RB, RB, RB)


def kernel(x, meta_tokens, ln_emb_g, ln_emb_b, w_in, b_in, conv_w, conv_b, w_ra, b_ra, w_ri, b_ri, lru_lambda, sinks, w_rnn_out, w_attn_out, w_o, b_o, ln_g, ln_b, loss_target, m_meta_tokens, m_ln_emb_g, m_ln_emb_b, m_w_in, m_b_in, m_conv_w, m_conv_b, m_w_ra, m_b_ra, m_w_ri, m_b_ri, m_lru_lambda, m_sinks, m_w_rnn_out, m_w_attn_out, m_w_o, m_b_o, m_ln_g, m_ln_b, v_meta_tokens, v_ln_emb_g, v_ln_emb_b, v_w_in, v_b_in, v_conv_w, v_conv_b, v_w_ra, v_b_ra, v_w_ri, v_b_ri, v_lru_lambda, v_sinks, v_w_rnn_out, v_w_attn_out, v_w_o, v_b_o, v_ln_g, v_ln_b):
    xi, yi, ci = _my_pos()
    shard = 2 * xi + yi
    pos = jnp.stack([ci, shard, 1 - ci]).astype(jnp.int32)
    cos_t, sin_t = _rope_tables()
    zero_bias = jnp.zeros((1, D), F32)
    ln_emb_g2, ln_emb_b2 = ln_emb_g[None], ln_emb_b[None]

    small = jnp.concatenate([conv_w[0], meta_tokens, jnp.zeros((4, 512), F32)], axis=0)
    small4 = _gather_small(small)
    conv_w_full = small4[:, 0:4].transpose(1, 0, 2).reshape(CONV_W, D)
    meta_full = small4[:, 4:20].transpose(1, 0, 2).reshape(N_META, D)
    w_own = _cast_into_slot(w_in[0], pos, name="cast_w_in", tr=256)
    wa_own = _cast_into_slot(jnp.concatenate([w_attn_out[0], w_o[0], _pack_ra(w_ra[0]), _pack_ra(w_ri[0])], axis=0),
                             pos, name="cast_w_a", tr=288)
    wb_own = _cast_into_slot(w_rnn_out[0], pos, name="cast_w_b", tr=256)

    h32, h16, h16_t = _ln_emb_fwd(x[0], meta_full, ln_emb_g2, ln_emb_b2)
    order = jnp.stack([shard, shard ^ 2, shard ^ 1, shard ^ 3]).astype(jnp.int32)
    z, w_in4 = _mm_z_gather(h16, w_own, b_in, order)
    q, k, v = _rope_fwd(z, cos_t, sin_t)
    o, lse, wa4 = _attn_fwd(q, k, v, sinks[0], hook=_hook_gather(wa_own, 0.6))
    w_ra_full = _gate_full(wa4[:, 2 * SQ_ROWS:2 * SQ_ROWS + 64])
    w_ri_full = _gate_full(wa4[:, 2 * SQ_ROWS + 64:2 * SQ_ROWS + 128])
    a_dec, u_in = _rnn_gates_fwd(z, conv_w_full, conv_b, w_ra_full, w_ri_full, b_ra, b_ri, lru_lambda)
    hr, wb4 = _scan_fwd(a_dec, u_in, hook=_hook_gather(wb_own, 0.6))
    sq_w = {0: (wb4, 0), 1: (wa4, 0), 2: (wa4, 1)}

    def sq_nn(a, kk, bias, name, out_cols, out_index, carried=None):
        wp, blk = sq_w[kk]
        return _mm_nn(a, wp, bias, name=name, grid=(2, D // CW), tm=HALF_TP, tn=CW, k=D, a_index=lambda i, j: (i, 0),
                      w_block=(N_SH, SQ_ROWS, CW), w_index=lambda i, j: (0, blk, j), out_cols=out_cols,
                      out_index=out_index, carried=carried)[0]

    def sq_nt(a, a_blk, kk, name, hook=None):
        wp, blk = sq_w[kk]
        return _mm_nt(a, wp, name=name, grid=(2, N_SH, 1), tm=HALF_TP, tn=SQ_ROWS, tk=D,
                      a_index=lambda i, j, q: (i, a_blk), w_block=(None, SQ_ROWS, D),
                      w_index=lambda i, j, q: (j, blk, 0), out_cols=D, hook=hook)

    ya_in, ya_in_t = _mul_silu_fwd(hr, z, OFF_GR, name="gate_a_fwd")
    y2 = sq_nn(ya_in, 0, zero_bias, "mm_ya", 2 * D, lambda i, j: (i, j))
    yb_in, yb_in_t = _mul_silu_fwd(o, z, OFF_GA, name="gate_b_fwd")
    y2 = sq_nn(yb_in, 1, zero_bias, "mm_yb", 2 * D, lambda i, j: (i, D // CW + j), carried=y2)
    mixed, mixed_t = _merge_fwd(y2, z)
    out = sq_nn(mixed, 2, b_o, "mm_out", D, lambda i, j: (i, j))
    dr, sums_o = _ln_out_loss(h32, out, loss_target[0], ln_g, ln_b)

    def sq_tn(at, b, b_blk0, kk, name, carried=None):
        return _mm_tn(at, b, name=name, grid=(N_SH, D // CW), tm=SQ_ROWS, tn=CW, a_index=lambda i, j: (i, 0),
                      b_index=lambda i, j: (0, b_blk0 + j), out_shape=(N_SH, PACK_ROWS, D),
                      out_block=(None, SQ_ROWS, CW), out_index=lambda i, j: (i, kk, j), carried=carried)[0]

    gsq = sq_tn(mixed_t, dr, 0, 2, "mm_dwo")
    dmix = sq_nt(dr, 0, 2, "mm_dmix")[0]
    dy2, dz = _merge_bwd(dmix, y2, z)
    gsq = sq_tn(ya_in_t, dy2, 0, 0, "mm_dwrnn", carried=gsq)
    gsq = sq_tn(yb_in_t, dy2, D // CW, 1, "mm_dwattn", carried=gsq)
    dya_in = sq_nt(dy2, 0, 0, "mm_dyain")[0]
    dhr, dz = _mul_silu_bwd(dya_in, hr, z, OFF_GR, dz, name="gate_a_bwd")
    lam_s = _scan_bwd(a_dec, dhr)
    dz, gsq, sums_r = _rnn_gates_bwd(z, lam_s, hr, conv_w_full, conv_b, w_ra_full, w_ri_full, b_ra, b_ri, lru_lambda, dz, gsq)
    dyb_in = sq_nt(dy2, 1, 1, "mm_dybin")[0]
    do, dz, gsq, got_sq = _mul_silu_bwd(dyb_in, o, z, OFF_GA, dz, name="gate_b_bwd", hook=_hook_pair(gsq, True))
    own_sq, s16_sq = _pair_add(gsq, got_sq, pos, name="red_w_sq_add", tr=208, g_has_both_halves=True)
    dz, dk_rot, dv32, dsink, s16_sq, oth_sq = _attn_bwd(q, k, v, sinks[0], do, o, lse, cos_t, sin_t, dz,
                                                        hook=_hook_scatter(s16_sq))
    dz = _rope_bwd_k(dk_rot, dv32, cos_t, sin_t, dz)
    red_sq = _sum_chips(own_sq, oth_sq, pos, name="red_w_sq_sum", tr=208)
    g_b_in, g_sq = _colsum(dz, name="colsum_dz", tn=TN_IN, hook=_hook_halves(red_sq))

    def dwin(half_idx, name, hook=None):
        return _mm_tn(h16_t, dz, name=name, grid=(1, N_SH * PER_IN), tm=D // 2, tn=TN_IN,
                      a_index=lambda i, j, p: (p[half_idx], 0), b_index=lambda i, j, p: (0, j),
                      out_shape=(N_SH, D // 2, W_IN_COLS), out_block=(None, D // 2, TN_IN),
                      out_index=lambda i, j, p: (j // PER_IN, 0, j % PER_IN), prefetch=(pos,), hook=hook)

    gin_sib = dwin(2, "mm_dwin_sib")[0]
    gin_own, gin_sib, got_in = dwin(0, "mm_dwin_own", hook=_hook_pair(gin_sib, False))
    own_in, s16_in = _pair_add(gin_own, got_in, pos, name="red_w_in_add", tr=128, g_has_both_halves=False)
    dhz, s16_in, oth_in = _mm_nt(dz, w_in4, name="mm_dhz", grid=(2, 2, N_SH), tm=HALF_TP, tn=D // 2, tk=W_IN_COLS,
                                 a_index=lambda i, j, q: (i, q), w_block=(None, D // 2, W_IN_COLS),
                                 w_index=lambda i, j, q: (q, j, 0), out_cols=D, hook=_hook_scatter(s16_in))
    red_in = _sum_chips(own_in, oth_in, pos, name="red_w_in_sum", tr=128)
    g_x, g_meta_local, sums_e, g_in = _ln_emb_bwd(x[0], meta_full, ln_emb_g2, dr, dhz, hook=_hook_halves(red_in))

    big = {"w_in": [t.reshape(w_in.shape) for t in
                    _adamw(w_in[0], g_in, m_w_in[0], v_w_in[0], name="adamw_w_in", tr=128)]}
    for kk, (n, w_, m_, v_) in enumerate([("w_rnn_out", w_rnn_out, m_w_rnn_out, v_w_rnn_out),
                                          ("w_attn_out", w_attn_out, m_w_attn_out, v_w_attn_out), ("w_o", w_o, m_w_o, v_w_o)]):
        big[n] = [t.reshape(w_.shape) for t in
                  _adamw(w_[0], g_sq, m_[0], v_[0], name="adamw_" + n, tr=256, g_row0=SQ_ROWS * kk)]
    for kk, (n, w_, m_, v_) in enumerate([("w_ra", w_ra, m_w_ra, v_w_ra), ("w_ri", w_ri, m_w_ri, v_w_ri)]):
        big[n] = [_unpack_ra(t, w_) for t in
                  _adamw(_pack_ra(w_[0]), g_sq, _pack_ra(m_[0]), _pack_ra(v_[0]), name="adamw_" + n, tr=64,
                         g_row0=3 * SQ_ROWS + 64 * kk)]

    spack = jnp.concatenate([
        sums_e[0:1], sums_e[1:2], _row_pad(g_b_in, _B_IN_ROWS), sums_r[0:4], sums_r[4:5], sums_r[5:6], sums_r[6:7],
        sums_r[7:8], _row_pad(dsink[0:1, 0:N_Q], 1), sums_o[2:3], sums_o[0:1], sums_o[1:2], g_meta_local, sums_o[3:4],
        jnp.zeros((_SMALL_ROWS - 38, D), F32)], axis=0)
    sred = _allreduce_small(spack)
    loss = sred[37, 0]
    col0 = shard * 512
    g_conv_w = lax.dynamic_slice(sred[9:13], (0, col0), (CONV_W, 512))
    g_meta = lax.dynamic_slice(sred[21:37], (0, col0), (N_META, 512))
    small_g = {"ln_emb_g": sred[0:1], "ln_emb_b": sred[1:2], "b_in": sred[2:9], "conv_w": g_conv_w.reshape(1, D),
               "conv_b": sred[13:14], "b_ra": sred[14:15], "b_ri": sred[15:16], "lru_lambda": sred[16:17],
               "sinks": sred[17:18], "b_o": sred[18:19], "ln_g": sred[19:20], "ln_b": sred[20:21],
               "meta_tokens": g_meta.reshape(4, D)}
    small_names = list(small_g)

    def small_pack(vals):
        rows = []
        for n in small_names:
            a = vals[n]
            if n == "b_in":
                a = _row_pad(a, _B_IN_ROWS)
            elif n == "sinks":
                a = _row_pad(a, 1)
            else:
                a = a.reshape(-1, D)
            rows.append(a)
        return jnp.concatenate(rows + [jnp.zeros((24 - 22, D), F32)], axis=0)

    w_small = dict(ln_emb_g=ln_emb_g, ln_emb_b=ln_emb_b, b_in=b_in, conv_w=conv_w, conv_b=conv_b, b_ra=b_ra, b_ri=b_ri,
                   lru_lambda=lru_lambda, sinks=sinks, b_o=b_o, ln_g=ln_g, ln_b=ln_b, meta_tokens=meta_tokens)
    m_small = dict(ln_emb_g=m_ln_emb_g, ln_emb_b=m_ln_emb_b, b_in=m_b_in, conv_w=m_conv_w, conv_b=m_conv_b, b_ra=m_b_ra,
                   b_ri=m_b_ri, lru_lambda=m_lru_lambda, sinks=m_sinks, b_o=m_b_o, ln_g=m_ln_g, ln_b=m_ln_b,
                   meta_tokens=m_meta_tokens)
    v_small = dict(ln_emb_g=v_ln_emb_g, ln_emb_b=v_ln_emb_b, b_in=v_b_in, conv_w=v_conv_w, conv_b=v_conv_b, b_ra=v_b_ra,
                   b_ri=v_b_ri, lru_lambda=v_lru_lambda, sinks=v_sinks, b_o=v_b_o, ln_g=v_ln_g, ln_b=v_ln_b,
                   meta_tokens=v_meta_tokens)
    g_small_pack = jnp.concatenate([small_g[n] for n in small_names] + [jnp.zeros((2, D), F32)], axis=0)
    small_res = _adamw(small_pack(w_small), g_small_pack, small_pack(m_small), small_pack(v_small),
                       name="adamw_small", tr=24)

    small_rows = {}
    r0 = 0
    for n in small_names:
        nrows = small_g[n].shape[0]
        small_rows[n] = (r0, nrows)
        r0 += nrows

    def small_out(packed, n, like):
        a, nrows = small_rows[n]
        flat = packed[a:a + nrows].reshape(-1)
        return flat[:like.size].reshape(like.shape)

    weights = dict(meta_tokens=meta_tokens, ln_emb_g=ln_emb_g, ln_emb_b=ln_emb_b, w_in=w_in, b_in=b_in, conv_w=conv_w,
                   conv_b=conv_b, w_ra=w_ra, b_ra=b_ra, w_ri=w_ri, b_ri=b_ri, lru_lambda=lru_lambda, sinks=sinks,
                   w_rnn_out=w_rnn_out, w_attn_out=w_attn_out, w_o=w_o, b_o=b_o, ln_g=ln_g, ln_b=ln_b)

    def outputs(which):
        return [big[n][which] if n in big else small_out(small_res[which], n, like) for n, like in weights.items()]

    return (loss, g_x[None], *outputs(0), *outputs(1), *outputs(2), *outputs(3))
```

```python
import jax
import jax.numpy as jnp
from jax import lax
from jax.experimental import pallas as pl
from jax.experimental.pallas import tpu as pltpu

F32 = jnp.float32
BF16 = jnp.bfloat16
_MXU = jnp.bfloat16

D = 2048
SEQ = 2048
N_META = 16
BLK = 128
PAD = BLK - N_META
TP = PAD + N_META + SEQ
NBLK = TP // BLK
HALF_TP = TP // 2
N_RB = 8
RB = 256
CONV_W = 4
LRU_C = 8.0
HD = 64
N_Q = 32
N_KV = 4
GRP = 8
D_KV = 256
NEG_INF = -1e30
LN_EPS = 1e-5
ALPHA = 2.0 ** 0.25
ROPE_THETA = 10000.0
OFF_GR, OFF_Q, OFF_K, OFF_V, OFF_GA, OFF_G = 2048, 4096, 6144, 6400, 6656, 8704
D_IN = 12800
N_SH = 4
W_IN_COLS = D_IN // N_SH
TN_IN = 640
PER_IN = W_IN_COLS // TN_IN
CW = 512
RT = TP // 4
SQ_ROWS = 512
PACK_ROWS = 3 * SQ_ROWS + 128

ADAM_LR = 0.001
ADAM_B1 = 0.9
ADAM_B2 = 0.999
ADAM_EPS = 1e-08
ADAM_WD = 0.01
ADAM_STEP = 10

MESH = pl.DeviceIdType.MESH
_MIB = 1024 * 1024
_ANY = pl.BlockSpec(memory_space=pl.ANY)
_NT = (((1,), (1,)), ((), ()))
_TN = (((0,), (0,)), ((), ()))


def _sds(shape, dtype):
    return jax.ShapeDtypeStruct(shape, dtype)


def _sigmoid(x):
    return 1.0 / (1.0 + jnp.exp(-x))


def _my_pos():
    return lax.axis_index("x"), lax.axis_index("y"), lax.axis_index("c")


def _other_chips(x, y):
    return [(1 - x, y), (x, 1 - y), (1 - x, 1 - y)]


def _remote(src, dst, send_sems, recv_sems, k, dev):
    return pltpu.make_async_remote_copy(src_ref=src, dst_ref=dst, send_sem=send_sems.at[k], recv_sem=recv_sems.at[k],
                                        device_id=dev, device_id_type=MESH)


class _Hook:
    def __init__(self, carried, landing, n_sems, start, finish, mid=None, mid_frac=0.5):
        self.carried, self.landing, self.n_sems, self.start, self.finish = list(carried), list(landing), n_sems, start, finish
        self.mid, self.mid_frac = mid, mid_frac


def _hook_gather(buf, mid_frac):
    half = buf.shape[1] // 2
    quarter = half // 2

    def geom(o, ss, rs):
        x, y, c = _my_pos()
        xn, yn, dg = _other_chips(x, y)
        slot = lambda p: 2 * p[0] + p[1]
        mine_rows = pl.ds(pl.multiple_of(c * half, 16), half)
        sib_rows = pl.ds(pl.multiple_of((1 - c) * half, 16), half)
        q_rows = lambda r: pl.ds(pl.multiple_of(c * half + r * quarter, 16), quarter)

        def cp(k, s, rows, dev):
            part = o.at[s, rows]
            return _remote(part, part, ss, rs, k, dev)

        return dict(
            direct=lambda k, s: cp(k, s, mine_rows, ((xn, yn)[k][0], (xn, yn)[k][1], c)),
            relay=lambda r, s: cp(2 + r, s, q_rows(r), ((yn, xn)[r][0], (yn, xn)[r][1], c)),
            sibling=lambda k, s, mine: cp(4 + k, s, mine_rows if mine else sib_rows, (x, y, 1 - c)),
            me=slot((x, y)), slots=(slot(xn), slot(yn), slot(dg)))

    def start(car, land, ss, rs):
        g = geom(car[0], ss, rs)
        g["direct"](0, g["me"]).start()
        g["direct"](1, g["me"]).start()

    def mid(car, land, ss, rs):
        g = geom(car[0], ss, rs)
        for k in range(2):
            g["direct"](k, g["slots"][k]).wait_recv()
            g["relay"](k, g["slots"][k]).start()
            g["sibling"](k, g["slots"][k], True).start()

    def finish(car, land, ss, rs):
        g = geom(car[0], ss, rs)
        dslot = g["slots"][2]
        g["relay"](0, dslot).wait_recv()
        g["relay"](1, dslot).wait_recv()
        g["sibling"](2, dslot, True).start()
        for k in range(3):
            g["sibling"](k, g["slots"][k], False).wait_recv()
        for k in range(2):
            g["direct"](k, g["me"]).wait_send()
            g["relay"](k, g["slots"][k]).wait_send()
        for k in range(3):
            g["sibling"](k, g["slots"][k], True).wait_send()

    return _Hook([buf], [], 7, start, finish, mid=mid, mid_frac=mid_frac)


def _hook_pair(g, half_rows):
    n, r, cc = g.shape
    h = r // 2 if half_rows else r

    def plan(car, land, ss, rs):
        x, y, c = _my_pos()
        cps = []
        for s in range(n):
            src = car[0].at[s, pl.ds(pl.multiple_of((1 - c) * h, 8), h)] if half_rows else car[0].at[s]
            cps.append(_remote(src, land[0].at[s], ss, rs, s, (x, y, 1 - c)))
        return cps

    def start(car, land, ss, rs):
        for cp in plan(car, land, ss, rs):
            cp.start()

    def finish(car, land, ss, rs):
        cps = plan(car, land, ss, rs)
        for cp in cps:
            cp.wait_recv()
        for cp in cps:
            cp.wait_send()

    return _Hook([g], [_sds((n, h, cc), g.dtype)], n, start, finish)


def _hook_scatter(s16):
    _, h, cc = s16.shape

    def plan(car, land, ss, rs):
        x, y, c = _my_pos()
        return [_remote(car[0].at[2 * px + py], land[0].at[k], ss, rs, k, (px, py, c))
                for k, (px, py) in enumerate(_other_chips(x, y))]

    def start(car, land, ss, rs):
        for cp in plan(car, land, ss, rs):
            cp.start()

    def finish(car, land, ss, rs):
        cps = plan(car, land, ss, rs)
        for cp in cps:
            cp.wait_recv()
        for cp in cps:
            cp.wait_send()

    return _Hook([s16], [_sds((3, h, cc), s16.dtype)], 3, start, finish)


def _hook_halves(full):
    h = full.shape[0] // 2

    def half_copy(car, ss, rs, which):
        x, y, c = _my_pos()
        rows = car[0].at[pl.ds(pl.multiple_of((c + which - 2 * c * which) * h, 8), h)]
        return _remote(rows, rows, ss, rs, 0, (x, y, 1 - c))

    def start(car, land, ss, rs):
        half_copy(car, ss, rs, 0).start()

    def finish(car, land, ss, rs):
        half_copy(car, ss, rs, 1).wait_recv()
        half_copy(car, ss, rs, 0).wait_send()

    return _Hook([full], [], 1, start, finish)


def _pcall(body, *, name, grid, in_specs, out_specs, out_shape, operands, scratch=(), vmem=48, sem=None,
           prefetch=(), aliases=None, hook=None):
    n_pre, n_in, n_out, n_scr = len(prefetch), len(in_specs), len(out_specs), len(scratch)
    in_specs, out_specs, out_shape, scratch = list(in_specs), list(out_specs), list(out_shape), list(scratch)
    io_alias = {n_pre + a: b for a, b in (aliases or {}).items()}
    operands = list(operands)
    kernel_body = body
    if hook is not None:
        n_car, n_land = len(hook.carried), len(hook.landing)
        for t, arr in enumerate(hook.carried):
            io_alias[n_pre + n_in + t] = n_out + t
        in_specs += [_ANY] * n_car
        out_specs += [_ANY] * (n_car + n_land)
        out_shape += [_sds(a.shape, a.dtype) for a in hook.carried] + hook.landing
        scratch += [pltpu.SemaphoreType.DMA((hook.n_sems,)), pltpu.SemaphoreType.DMA((hook.n_sems,))]
        operands += hook.carried
        sem = ("arbitrary",) * len(grid)

        def kernel_body(*refs):
            pre, rest = refs[:n_pre], refs[n_pre:]
            ins = rest[:n_in]
            outs = rest[n_in + n_car:n_in + n_car + n_out]
            car = rest[n_in + n_car + n_out:n_in + 2 * n_car + n_out]
            land = rest[n_in + 2 * n_car + n_out:n_in + 2 * n_car + n_out + n_land]
            scr = rest[n_in + 2 * n_car + n_out + n_land:]
            send_sems, recv_sems = scr[n_scr], scr[n_scr + 1]
            first = pl.program_id(0) == 0
            last = pl.program_id(0) == grid[0] - 1
            for d in range(1, len(grid)):
                first = first & (pl.program_id(d) == 0)
                last = last & (pl.program_id(d) == grid[d] - 1)

            @pl.when(first)
            def _():
                hook.start(car, land, send_sems, recv_sems)

            if hook.mid is not None:
                step = pl.program_id(0)
                total = grid[0]
                for d in range(1, len(grid)):
                    step = step * grid[d] + pl.program_id(d)
                    total *= grid[d]

                @pl.when(step == int(total * hook.mid_frac))
                def _():
                    hook.mid(car, land, send_sems, recv_sems)

            body(*pre, *ins, *outs, *scr[:n_scr])

            @pl.when(last)
            def _():
                hook.finish(car, land, send_sems, recv_sems)

    params = pltpu.CompilerParams(vmem_limit_bytes=vmem * _MIB, dimension_semantics=sem,
                                  has_side_effects=hook is not None)
    if n_pre:
        call = pl.pallas_call(
            kernel_body, name=name, out_shape=out_shape, input_output_aliases=io_alias, compiler_params=params,
            grid_spec=pltpu.PrefetchScalarGridSpec(num_scalar_prefetch=n_pre, grid=grid, in_specs=in_specs,
                                                   out_specs=out_specs, scratch_shapes=scratch))
    else:
        call = pl.pallas_call(kernel_body, name=name, grid=grid, in_specs=in_specs, out_specs=out_specs,
                              out_shape=out_shape, scratch_shapes=scratch, input_output_aliases=io_alias,
                              compiler_params=params)
    return call(*prefetch, *operands)


def _gather_small(blk):
    r, cc = blk.shape

    def body(x_ref, o_ref, send_sems, recv_sems):
        x, y, c = _my_pos()
        me = 2 * x + y
        o_ref[me] = x_ref[...]
        sends = [_remote(x_ref, o_ref.at[me], send_sems, recv_sems, k, (px, py, c))
                 for k, (px, py) in enumerate(_other_chips(x, y))]
        for cp in sends:
            cp.start()
        for k, (px, py) in enumerate(_other_chips(x, y)):
            _remote(x_ref, o_ref.at[2 * px + py], send_sems, recv_sems, k, (px, py, c)).wait_recv()
        for cp in sends:
            cp.wait_send()

    vm = pl.BlockSpec(memory_space=pltpu.VMEM)
    return pl.pallas_call(
        body, name="gather_small", in_specs=[vm], out_specs=vm, out_shape=_sds((N_SH, r, cc), blk.dtype),
        scratch_shapes=[pltpu.SemaphoreType.DMA((3,)), pltpu.SemaphoreType.DMA((3,))],
        compiler_params=pltpu.CompilerParams(has_side_effects=True),
    )(blk)


N_DEV = 8


def _allreduce_small(pack):
    r, cc = pack.shape

    def body(x_ref, o_ref, buf_ref, send_sems, recv_sems):
        x, y, c = _my_pos()
        me = 4 * x + 2 * y + c
        buf_ref[me] = x_ref[...]
        cps = []
        for k in range(1, N_DEV):
            peer = (x ^ ((k >> 2) & 1), y ^ ((k >> 1) & 1), c ^ (k & 1))
            cps.append(_remote(x_ref, buf_ref.at[me], send_sems, recv_sems, k - 1, peer))
        for cp in cps:
            cp.start()
        for k in range(1, N_DEV):
            peer = (x ^ ((k >> 2) & 1), y ^ ((k >> 1) & 1), c ^ (k & 1))
            src = 4 * peer[0] + 2 * peer[1] + peer[2]
            _remote(x_ref, buf_ref.at[src], send_sems, recv_sems, k - 1, peer).wait_recv()
        acc = buf_ref[0]
        for d in range(1, N_DEV):
            acc = acc + buf_ref[d]
        o_ref[...] = acc
        for cp in cps:
            cp.wait_send()

    vm = pl.BlockSpec(memory_space=pltpu.VMEM)
    return pl.pallas_call(
        body, name="allreduce_small", in_specs=[vm], out_specs=vm, out_shape=_sds((r, cc), F32),
        scratch_shapes=[pltpu.VMEM((N_DEV, r, cc), F32), pltpu.SemaphoreType.DMA((N_DEV - 1,)),
                        pltpu.SemaphoreType.DMA((N_DEV - 1,))],
        compiler_params=pltpu.CompilerParams(has_side_effects=True),
    )(pack)


def _mm_nn(a, w, bias, *, name, grid, tm, tn, k, a_index, w_block, w_index, out_cols, out_index, carried=None, hook=None):
    m = a.shape[0]

    def body(a_ref, w_ref, b_ref, *rest):
        o_ref = rest[-1]
        wv = w_ref[...]
        acc = jnp.dot(a_ref[...].astype(_MXU), wv.reshape(k, tn).astype(_MXU), preferred_element_type=F32)
        o_ref[...] = acc + b_ref[...]

    operands = [a, w, bias] + ([carried] if carried is not None else [])
    return _pcall(
        body, name=name, grid=grid,
        in_specs=[pl.BlockSpec((tm, k), a_index), pl.BlockSpec(w_block, w_index),
                  pl.BlockSpec((1, tn), lambda i, j: (0, j))] + ([_ANY] if carried is not None else []),
        out_specs=[pl.BlockSpec((tm, tn), out_index)], out_shape=[_sds((m, out_cols), F32)], operands=operands,
        vmem=56, sem=("parallel", "parallel"), aliases={3: 0} if carried is not None else None, hook=hook)


def _mm_nt(a, w, *, name, grid, tm, tn, tk, a_index, w_block, w_index, out_cols, hook=None):
    m = a.shape[0]
    nk = grid[2]

    def body(a_ref, w_ref, o_ref, acc_ref):
        part = lax.dot_general(a_ref[...].astype(_MXU), w_ref[...].astype(_MXU), _NT, preferred_element_type=F32)
        if nk == 1:
            o_ref[...] = part
        else:
            kidx = pl.program_id(2)

            @pl.when(kidx == 0)
            def _():
                acc_ref[...] = part

            @pl.when(kidx > 0)
            def _():
                acc_ref[...] += part

            @pl.when(kidx == nk - 1)
            def _():
                o_ref[...] = acc_ref[...]

    return _pcall(
        body, name=name, grid=grid, in_specs=[pl.BlockSpec((tm, tk), a_index), pl.BlockSpec(w_block, w_index)],
        out_specs=[pl.BlockSpec((tm, tn), lambda i, j, q: (i, j))], out_shape=[_sds((m, out_cols), F32)],
        operands=[a, w], scratch=[pltpu.VMEM((tm, tn) if nk > 1 else (8, 128), F32)], vmem=60,
        sem=("parallel", "parallel", "arbitrary"), hook=hook)


def _mm_tn(at, b, *, name, grid, tm, tn, a_index, b_index, out_shape, out_block, out_index, carried=None,
           prefetch=(), hook=None):
    t = at.shape[1]

    def body(*refs):
        a_ref, b_ref = refs[len(prefetch)], refs[len(prefetch) + 1]
        refs[-1][...] = jnp.dot(a_ref[...].astype(_MXU), b_ref[...].astype(_MXU), preferred_element_type=F32)

    operands = [at, b] + ([carried] if carried is not None else [])
    return _pcall(
        body, name=name, grid=grid,
        in_specs=[pl.BlockSpec((tm, t), a_index), pl.BlockSpec((t, tn), b_index)] + ([_ANY] if carried is not None else []),
        out_specs=[pl.BlockSpec(out_block, out_index)], out_shape=[_sds(out_shape, F32)], operands=operands,
        vmem=56, sem=("parallel", "parallel"), aliases={2: 0} if carried is not None else None, prefetch=prefetch, hook=hook)


TILE_PARTS = 4


def _mm_z_gather(h16, w_own, b_in, order):
    n_tiles = N_SH * PER_IN
    n_remote = 3 * PER_IN
    half = D // 2

    def body(order_ref, a_ref, b_ref, w_in_ref, z_ref, w_ref, wbuf, tile_sems, ss, rs):
        del w_in_ref
        j = pl.program_id(0)
        x, y, c = _my_pos()
        me = 2 * x + y
        chips = _other_chips(x, y)
        mine_rows = pl.ds(pl.multiple_of(c * half, 16), half)
        sib_rows = pl.ds(pl.multiple_of((1 - c) * half, 16), half)

        slots = [2 * px + py for px, py in chips]
        cols = lambda t: pl.ds(t * TN_IN, TN_IN)
        q_rows = lambda r: pl.ds(pl.multiple_of(c * half + r * (half // 2), 16), half // 2)

        def direct(rel, t, slot):
            px, py = chips[rel]
            part = w_ref.at[slot, mine_rows, cols(t)]
            return _remote(part, part, ss, rs, 2 * t + rel, (px, py, c))

        def relay(r, t, slot):
            px, py = chips[1 - r]
            part = w_ref.at[slot, q_rows(r), cols(t)]
            return _remote(part, part, ss, rs, 2 * PER_IN + 2 * t + r, (px, py, c))

        def d2d(n, rows):
            part = w_ref.at[slots[n % 3], rows, cols(n // 3)]
            return _remote(part, part, ss, rs, 4 * PER_IN + n, (x, y, 1 - c))

        class tile_copy:
            def __init__(self, t):
                n = jnp.maximum(t - PER_IN, 0)
                rel = n % 3
                slot = jnp.where(t < PER_IN, me, me ^ jnp.where(rel == 0, 2, jnp.where(rel == 1, 1, 3)))
                col = pl.multiple_of(jnp.where(t < PER_IN, t, n // 3) * TN_IN, 128)
                rows = D // TILE_PARTS
                self.parts = [pltpu.make_async_copy(w_ref.at[slot, pl.ds(rows * r, rows), pl.ds(col, TN_IN)],
                                                    wbuf.at[t % 2, pl.ds(rows * r, rows)],
                                                    tile_sems.at[TILE_PARTS * (t % 2) + r]) for r in range(TILE_PARTS)]

            def start(self):
                for cp in self.parts:
                    cp.start()

            def wait(self):
                for cp in self.parts:
                    cp.wait()

        @pl.when(j == 0)
        def _():
            for t in range(PER_IN):
                direct(0, t, me).start()
                direct(1, t, me).start()
            tile_copy(0).start()

        for n in range(n_remote):
            rel, t = n % 3, n // 3

            @pl.when(j == n + PER_IN - 3)
            def _():
                if rel < 2:
                    direct(rel, t, slots[rel]).wait_recv()
                    relay(rel, t, slots[rel]).start()
                else:
                    relay(0, t, slots[2]).wait_recv()
                    relay(1, t, slots[2]).wait_recv()
                d2d(n, mine_rows).start()

            @pl.when(j == n + PER_IN - 2)
            def _():
                d2d(n, sib_rows).wait_recv()

        @pl.when(j + 1 < n_tiles)
        def _():
            tile_copy(j + 1).start()

        tile_copy(j).wait()
        z_ref[...] = jnp.dot(a_ref[...], wbuf[j % 2], preferred_element_type=F32) + b_ref[...]

        @pl.when(j == n_tiles - 1)
        def _():
            for t in range(PER_IN):
                for r in range(2):
                    direct(r, t, me).wait_send()
                    relay(r, t, slots[r]).wait_send()
            for n in range(n_remote):
                d2d(n, mine_rows).wait_send()

    def col_tile(j, o):
        n = jnp.maximum(j - PER_IN, 0)
        return 0, jnp.where(j < PER_IN, o[0] * PER_IN + j, o[1 + n % 3] * PER_IN + n // 3)

    return pl.pallas_call(
        body, name="mm_z_gather",
        grid_spec=pltpu.PrefetchScalarGridSpec(
            num_scalar_prefetch=1, grid=(n_tiles,),
            in_specs=[pl.BlockSpec((TP, D), lambda j, o: (0, 0)), pl.BlockSpec((1, TN_IN), col_tile), _ANY],
            out_specs=[pl.BlockSpec((TP, TN_IN), col_tile), _ANY],
            scratch_shapes=[pltpu.VMEM((2, D, TN_IN), BF16), pltpu.SemaphoreType.DMA((2 * TILE_PARTS,)),
                            pltpu.SemaphoreType.DMA((4 * PER_IN + n_remote,)),
                            pltpu.SemaphoreType.DMA((4 * PER_IN + n_remote,))]),
        out_shape=[_sds((TP, D_IN), F32), _sds(w_own.shape, w_own.dtype)],
        input_output_aliases={3: 1},
        compiler_params=pltpu.CompilerParams(vmem_limit_bytes=60 * _MIB, dimension_semantics=("arbitrary",),
                                             has_side_effects=True),
    )(order, h16, b_in, w_own)


def _padded_rows(i, x_ref, meta_ref):
    head = jnp.concatenate([jnp.zeros((PAD, D), F32), meta_ref[...]], axis=0)
    return jnp.where(i == 0, head, x_ref[...])


def _stream_specs():
    return [pl.BlockSpec((BLK, D), lambda i: (jnp.maximum(i - 1, 0), 0)), pl.BlockSpec((N_META, D), lambda i: (0, 0))]


def _ln_emb_fwd(x, meta, g, b):
    def body(x_ref, meta_ref, g_ref, b_ref, h32_ref, h16_ref, h16t_ref):
        x = _padded_rows(pl.program_id(0), x_ref, meta_ref)
        mu = jnp.mean(x, axis=-1, keepdims=True)
        xc = x - mu
        var = jnp.mean(xc * xc, axis=-1, keepdims=True)
        y = xc * lax.rsqrt(var + LN_EPS) * g_ref[...] + b_ref[...]
        h32_ref[...] = y
        h16_ref[...] = y.astype(_MXU)
        h16t_ref[...] = y.T.astype(_MXU)

    row = pl.BlockSpec((BLK, D), lambda i: (i, 0))
    vec = pl.BlockSpec((1, D), lambda i: (0, 0))
    return _pcall(body, name="ln_emb_fwd", grid=(NBLK,), in_specs=_stream_specs() + [vec, vec],
                  out_specs=[row, row, pl.BlockSpec((D, BLK), lambda i: (0, i))],
                  out_shape=[_sds((TP, D), F32), _sds((TP, D), _MXU), _sds((D, TP), _MXU)], operands=[x, meta, g, b],
                  vmem=32, sem=("parallel",))


def _ln_emb_bwd(x, meta, g, dr, dhz, *, hook):
    def body(x_ref, meta_ref, g_ref, dr_ref, dhz_ref, dx_ref, dmeta_ref, acc_ref):
        i = pl.program_id(0)

        @pl.when(i == 0)
        def _():
            acc_ref[...] = jnp.zeros_like(acc_ref)

        x = _padded_rows(i, x_ref, meta_ref)
        mu = jnp.mean(x, axis=-1, keepdims=True)
        xc = x - mu
        var = jnp.mean(xc * xc, axis=-1, keepdims=True)
        rstd = lax.rsqrt(var + LN_EPS)
        xhat = xc * rstd
        dh = ALPHA * dr_ref[...] + dhz_ref[...]
        acc_ref[0:1, :] += jnp.sum(dh * xhat, axis=0, keepdims=True)
        acc_ref[1:2, :] += jnp.sum(dh, axis=0, keepdims=True)
        dxh = dh * g_ref[...]
        m1 = jnp.mean(dxh, axis=-1, keepdims=True)
        m2 = jnp.mean(dxh * xhat, axis=-1, keepdims=True)
        dx = rstd * (dxh - m1 - xhat * m2)
        dx_ref[...] = dx

        @pl.when(i == 0)
        def _():
            dmeta_ref[...] = dx[PAD:BLK]

    row = pl.BlockSpec((BLK, D), lambda i: (i, 0))
    vec = pl.BlockSpec((1, D), lambda i: (0, 0))
    xs, ms = _stream_specs()
    return _pcall(body, name="ln_emb_bwd", grid=(NBLK,), in_specs=[xs, ms, vec, row, row],
                  out_specs=[xs, ms, pl.BlockSpec((8, D), lambda i: (0, 0))],
                  out_shape=[_sds((SEQ, D), F32), _sds((N_META, D), F32), _sds((8, D), F32)],
                  operands=[x, meta, g, dr, dhz], vmem=32, sem=("arbitrary",), hook=hook)


def _mul_silu_fwd(a, z, off, *, name):
    def body(a_ref, z_ref, o_ref, t_ref):
        zz = z_ref[...]
        y = a_ref[...] * (zz * _sigmoid(zz))
        o_ref[...] = y.astype(_MXU)
        t_ref[...] = y.T.astype(_MXU)

    strip = pl.BlockSpec((TP, CW), lambda j: (0, j))
    return _pcall(body, name=name, grid=(D // CW,),
                  in_specs=[strip, pl.BlockSpec((TP, CW), lambda j: (0, off // CW + j))],
                  out_specs=[strip, pl.BlockSpec((CW, TP), lambda j: (j, 0))],
                  out_shape=[_sds((TP, D), _MXU), _sds((D, TP), _MXU)], operands=[a, z], vmem=56, sem=("parallel",))


def _mul_silu_bwd(dy, a, z, off, dz, *, name, hook=None):
    def body(dy_ref, a_ref, z_ref, dz_in, da_ref, dg_ref):
        zz = z_ref[...]
        sg = _sigmoid(zz)
        d = dy_ref[...]
        da_ref[...] = d * (zz * sg)
        dg_ref[...] = (d * a_ref[...] * (sg * (1.0 + zz * (1.0 - sg)))).astype(_MXU)

    blk = pl.BlockSpec((RT, CW), lambda i, j: (i, j))
    zblk = pl.BlockSpec((RT, CW), lambda i, j: (i, off // CW + j))
    return _pcall(body, name=name, grid=(TP // RT, D // CW), in_specs=[blk, blk, zblk, _ANY], out_specs=[blk, zblk],
                  out_shape=[_sds((TP, D), F32), _sds((TP, D_IN), _MXU)], operands=[dy, a, z, dz], vmem=32,
                  sem=("parallel", "parallel"), aliases={3: 1}, hook=hook)


def _merge_fwd(y2, z):
    w = 256

    def body(ya_ref, yb_ref, ga_ref, gb_ref, o_ref, t_ref):
        y = _sigmoid(ga_ref[...]) * ya_ref[...] + _sigmoid(gb_ref[...]) * yb_ref[...]
        o_ref[...] = y.astype(_MXU)
        t_ref[...] = y.T.astype(_MXU)

    nb = D // w
    strip = pl.BlockSpec((TP, w), lambda j: (0, j))
    return _pcall(body, name="merge_fwd", grid=(nb,),
                  in_specs=[strip, pl.BlockSpec((TP, w), lambda j: (0, nb + j)),
                            pl.BlockSpec((TP, w), lambda j: (0, OFF_G // w + j)),
                            pl.BlockSpec((TP, w), lambda j: (0, (OFF_G + D) // w + j))],
                  out_specs=[strip, pl.BlockSpec((w, TP), lambda j: (j, 0))],
                  out_shape=[_sds((TP, D), _MXU), _sds((D, TP), _MXU)], operands=[y2, y2, z, z], vmem=56,
                  sem=("parallel",))


def _merge_bwd(dmix, y2, z):
    nb = D // CW

    def body(dm_ref, y_ref, g_ref, dy_ref, dg_ref):
        dm = dm_ref[...]
        sg = _sigmoid(g_ref[...])
        dy_ref[...] = (dm * sg).astype(_MXU)
        dg_ref[...] = (dm * y_ref[...] * sg * (1.0 - sg)).astype(_MXU)

    blk = pl.BlockSpec((RT, CW), lambda i, j: (i, j))
    gblk = pl.BlockSpec((RT, CW), lambda i, j: (i, OFF_G // CW + j))
    return _pcall(body, name="merge_bwd", grid=(TP // RT, 2 * nb),
                  in_specs=[pl.BlockSpec((RT, CW), lambda i, j: (i, j % nb)), blk, gblk], out_specs=[blk, gblk],
                  out_shape=[_sds((TP, 2 * D), _MXU), _sds((TP, D_IN), _MXU)], operands=[dmix, y2, z], vmem=32,
                  sem=("parallel", "parallel"))


def _ln_out_loss(h32, out, target, g, b):
    def body(h_ref, o_ref, t_ref, g_ref, b_ref, dr_ref, acc_ref):
        i = pl.program_id(0)

        @pl.when(i == 0)
        def _():
            acc_ref[...] = jnp.zeros_like(acc_ref)

        r = ALPHA * h_ref[...] + o_ref[...]
        mu = jnp.mean(r, axis=-1, keepdims=True)
        rc = r - mu
        var = jnp.mean(rc * rc, axis=-1, keepdims=True)
        rstd = lax.rsqrt(var + LN_EPS)
        xhat = rc * rstd
        gg = g_ref[...]
        y = xhat * gg + b_ref[...]
        real = (i >= 1).astype(F32)
        diff = (y - t_ref[...]) * real
        dy = diff * (1.0 / D)
        dxh = dy * gg
        m1 = jnp.mean(dxh, axis=-1, keepdims=True)
        m2 = jnp.mean(dxh * xhat, axis=-1, keepdims=True)
        dr = rstd * (dxh - m1 - xhat * m2)
        dr_ref[...] = dr
        acc_ref[0:1, :] += jnp.sum(dy * xhat, axis=0, keepdims=True)
        acc_ref[1:2, :] += jnp.sum(dy, axis=0, keepdims=True)
        acc_ref[2:3, :] += jnp.sum(dr, axis=0, keepdims=True)
        acc_ref[3:4, :] += (0.5 / D) * jnp.sum(diff * diff)

    row = pl.BlockSpec((BLK, D), lambda i: (i, 0))
    vec = pl.BlockSpec((1, D), lambda i: (0, 0))
    return _pcall(body, name="ln_out_loss", grid=(NBLK,),
                  in_specs=[row, row, pl.BlockSpec((BLK, D), lambda i: (jnp.maximum(i - 1, 0), 0)), vec, vec],
                  out_specs=[row, pl.BlockSpec((8, D), lambda i: (0, 0))],
                  out_shape=[_sds((TP, D), F32), _sds((8, D), F32)], operands=[h32, out, target, g, b], vmem=32,
                  sem=("arbitrary",))


def _colsum(x, *, name, tn, hook=None):
    _, n = x.shape

    def body(x_ref, o_ref):
        o_ref[...] = jnp.sum(x_ref[...].astype(F32), axis=0, keepdims=True)

    return _pcall(body, name=name, grid=(n // tn,), in_specs=[pl.BlockSpec((TP, tn), lambda j: (0, j))],
                  out_specs=[pl.BlockSpec((1, tn), lambda j: (0, j))], out_shape=[_sds((1, n), F32)], operands=[x],
                  vmem=32, sem=("parallel",), hook=hook)


def _rnn_recompute(xr_ref, cw_ref, cb_ref, wra_ref, wri_ref, bra_ref, bri_ref, lam_ref):
    rows = lax.broadcasted_iota(jnp.int32, (TP, 1), 0)
    valid = (rows >= PAD).astype(F32)
    first = rows == PAD
    x = xr_ref[...] * valid
    cw = cw_ref[...]
    shifted = [x, pltpu.roll(x, 1, 0), pltpu.roll(x, 2, 0), pltpu.roll(x, 3, 0)]
    c = cb_ref[...] + cw[0:1, :] * shifted[0] + cw[1:2, :] * shifted[1] + cw[2:3, :] * shifted[2] + cw[3:4, :] * shifted[3]
    cm = c.astype(_MXU)
    gr = _sigmoid(jnp.dot(cm, wra_ref[...].astype(_MXU), preferred_element_type=F32) + bra_ref[...])
    gi = _sigmoid(jnp.dot(cm, wri_ref[...].astype(_MXU), preferred_element_type=F32) + bri_ref[...])
    lam = lam_ref[...]
    ls = jnp.minimum(lam, 0.0) - jnp.log(1.0 + jnp.exp(-jnp.abs(lam)))
    log_a = LRU_C * gr * ls
    a = jnp.exp(log_a)
    mult = jnp.where(first, 1.0, jnp.sqrt(1.0 - jnp.exp(2.0 * log_a)))
    return dict(valid=valid, first=first, shifted=shifted, c=c, cm=cm, gr=gr, gi=gi, ls=ls, a=a, mult=mult, lam=lam)


def _rnn_specs():
    col = pl.BlockSpec((TP, RB), lambda n: (0, n))
    vec = pl.BlockSpec((1, RB), lambda n: (0, n))
    return dict(col=col, vec=vec, cw=pl.BlockSpec((CONV_W, RB), lambda n: (0, n)),
                wblk=pl.BlockSpec((None, RB, RB), lambda n: (n, 0, 0)))


def _rnn_gates_fwd(z, conv_w, conv_b, w_ra, w_ri, b_ra, b_ri, lam):
    def body(xr_ref, cw_ref, cb_ref, wra_ref, wri_ref, bra_ref, bri_ref, lam_ref, a_ref, u_ref):
        r = _rnn_recompute(xr_ref, cw_ref, cb_ref, wra_ref, wri_ref, bra_ref, bri_ref, lam_ref)
        a_ref[...] = r["a"]
        u_ref[...] = r["mult"] * r["gi"] * r["c"] * r["valid"]

    s = _rnn_specs()
    return _pcall(body, name="rnn_gates_fwd", grid=(N_RB,),
                  in_specs=[s["col"], s["cw"], s["vec"], s["wblk"], s["wblk"], s["vec"], s["vec"], s["vec"]],
                  out_specs=[s["col"], s["col"]], out_shape=[_sds((TP, D), F32)] * 2,
                  operands=[z, conv_w, conv_b, w_ra, w_ri, b_ra, b_ri, lam], vmem=56, sem=("parallel",))


SCAN_ROWS = 272


SUB = 8


def _tile_scan(a, u, reverse):
    rows = lax.broadcasted_iota(jnp.int32, a.shape, 0)
    for d in (1, 2, 4):
        shift = SUB - d if reverse else d
        inside = (rows < SUB - d) if reverse else (rows >= d)
        u = u + a * jnp.where(inside, pltpu.roll(u, shift, 0), 0.0)
        a = a * jnp.where(inside, pltpu.roll(a, shift, 0), 1.0)
    return a, u


def _scan_fwd(a, u, *, hook):
    def body(a_ref, u_ref, h_ref, carry_ref):
        @pl.when(pl.program_id(1) == 0)
        def _():
            carry_ref[...] = jnp.zeros_like(carry_ref)

        def step(r, h):
            rows = pl.ds(pl.multiple_of(r * SUB, SUB), SUB)
            prod, part = _tile_scan(a_ref[rows, :], u_ref[rows, :], False)
            ht = part + prod * h
            h_ref[rows, :] = ht
            return ht[SUB - 1:SUB, :]

        carry_ref[...] = lax.fori_loop(0, SCAN_ROWS // SUB, step, carry_ref[...], unroll=2)

    blk = pl.BlockSpec((SCAN_ROWS, CW), lambda j, i: (i, j))
    return _pcall(body, name="scan_fwd", grid=(D // CW, TP // SCAN_ROWS), in_specs=[blk, blk], out_specs=[blk],
                  out_shape=[_sds((TP, D), F32)], operands=[a, u], scratch=[pltpu.VMEM((1, CW), F32)], vmem=32,
                  sem=("parallel", "arbitrary"), hook=hook)


def _scan_bwd(a, dh):
    nst = TP // SCAN_ROWS
    n_tiles = SCAN_ROWS // SUB

    def body(a_ref, d_ref, o_ref, lam_ref, anext_ref):
        @pl.when(pl.program_id(1) == 0)
        def _():
            lam_ref[...] = jnp.zeros_like(lam_ref)
            anext_ref[...] = jnp.zeros_like(anext_ref)

        def step(q, carry):
            lam_next, a_next = carry
            rows = pl.ds(pl.multiple_of((n_tiles - 1 - q) * SUB, SUB), SUB)
            at = a_ref[rows, :]
            last = lax.broadcasted_iota(jnp.int32, at.shape, 0) == SUB - 1
            b = jnp.where(last, a_next, pltpu.roll(at, SUB - 1, 0))
            prod, part = _tile_scan(b, d_ref[rows, :], True)
            lam = part + prod * lam_next
            o_ref[rows, :] = lam
            return lam[0:1, :], at[0:1, :]

        lam, an = lax.fori_loop(0, n_tiles, step, (lam_ref[...], anext_ref[...]), unroll=2)
        lam_ref[...] = lam
        anext_ref[...] = an

    blk = pl.BlockSpec((SCAN_ROWS, CW), lambda j, i: (nst - 1 - i, j))
    return _pcall(body, name="scan_bwd", grid=(D // CW, nst), in_specs=[blk, blk], out_specs=[blk],
                  out_shape=[_sds((TP, D), F32)], operands=[a, dh],
                  scratch=[pltpu.VMEM((1, CW), F32), pltpu.VMEM((1, CW), F32)], vmem=32,
                  sem=("parallel", "arbitrary"))[0]


def _rnn_gates_bwd(z, lam_s, hr, conv_w, conv_b, w_ra, w_ri, b_ra, b_ri, lam, dz, gsq):
    def body(xr_ref, ls_ref, hr_ref, cw_ref, cb_ref, wra_ref, wri_ref, bra_ref, bri_ref, lam_ref, dz_in, gsq_in,
             dx_ref, dw_ref, sums_ref):
        r = _rnn_recompute(xr_ref, cw_ref, cb_ref, wra_ref, wri_ref, bra_ref, bri_ref, lam_ref)
        valid, c, gr, gi, a, mult = r["valid"], r["c"], r["gr"], r["gi"], r["a"], r["mult"]
        du = ls_ref[...] * valid
        da = du * pltpu.roll(hr_ref[...], 1, 0)
        d_gi = du * mult * c
        dc = du * mult * gi
        dmult = du * gi * c
        dlog_a = da * a + jnp.where(r["first"], 0.0, -dmult * a * a / mult)
        d_gr = dlog_a * (LRU_C * r["ls"])
        dls = jnp.sum(dlog_a * (LRU_C * gr), axis=0, keepdims=True)
        dpre_r = d_gr * gr * (1.0 - gr)
        dpre_i = d_gi * gi * (1.0 - gi)
        pr = dpre_r.astype(_MXU)
        pi = dpre_i.astype(_MXU)
        dwra = lax.dot_general(r["cm"], pr, _TN, preferred_element_type=F32)
        dwri = lax.dot_general(r["cm"], pi, _TN, preferred_element_type=F32)
        for s in range(N_SH):
            dw_ref[s, 0:64, :] = dwra[64 * s:64 * (s + 1)]
            dw_ref[s, 64:128, :] = dwri[64 * s:64 * (s + 1)]
        dc = dc + lax.dot_general(pr, wra_ref[...].astype(_MXU), _NT, preferred_element_type=F32)
        dc = dc + lax.dot_general(pi, wri_ref[...].astype(_MXU), _NT, preferred_element_type=F32)
        cw = cw_ref[...]
        dx = cw[0:1, :] * dc
        for k in range(1, CONV_W):
            dx = dx + cw[k:k + 1, :] * pltpu.roll(dc, TP - k, 0)
        dx_ref[...] = (dx * valid).astype(_MXU)
        for k in range(CONV_W):
            sums_ref[k:k + 1, :] = jnp.sum(dc * r["shifted"][k], axis=0, keepdims=True)
        sums_ref[4:5, :] = jnp.sum(dc, axis=0, keepdims=True)
        sums_ref[5:6, :] = jnp.sum(dpre_r, axis=0, keepdims=True)
        sums_ref[6:7, :] = jnp.sum(dpre_i, axis=0, keepdims=True)
        sums_ref[7:8, :] = dls * _sigmoid(-r["lam"])

    s = _rnn_specs()
    return _pcall(
        body, name="rnn_gates_bwd", grid=(N_RB,),
        in_specs=[s["col"], s["col"], s["col"], s["cw"], s["vec"], s["wblk"], s["wblk"], s["vec"], s["vec"], s["vec"],
                  _ANY, _ANY],
        out_specs=[s["col"], pl.BlockSpec((N_SH, 128, RB), lambda n: (0, 3 * SQ_ROWS // 128, n)),
                   pl.BlockSpec((8, RB), lambda n: (0, n))],
        out_shape=[_sds((TP, D_IN), _MXU), _sds((N_SH, PACK_ROWS, D), F32), _sds((8, D), F32)],
        operands=[z, lam_s, hr, conv_w, conv_b, w_ra, w_ri, b_ra, b_ri, lam, dz, gsq], vmem=60, sem=("parallel",),
        aliases={10: 0, 11: 1})


def _rope_tables():
    half = HD // 2
    inv = ROPE_THETA ** (-jnp.arange(half, dtype=F32) / half)
    pos = (jnp.arange(TP) - PAD).astype(F32)
    ang = pos[:, None] * inv[None, :]
    return jnp.tile(jnp.cos(ang), (1, 4)), jnp.tile(jnp.sin(ang), (1, 4))


def _rope(x, cos_t, sin_t, sign):
    w = x.shape[1]
    lane = lax.broadcasted_iota(jnp.int32, x.shape, 1)
    first = (lane % HD) < (HD // 2)
    swapped = jnp.where(first, pltpu.roll(x, w - HD // 2, 1), pltpu.roll(x, HD // 2, 1))
    ct = jnp.tile(cos_t, (1, w // 128))
    st = jnp.tile(sin_t, (1, w // 128))
    return x * ct + swapped * jnp.where(first, -sign * st, sign * st)


def _rope_fwd(z, cos_t, sin_t):
    def body(q_ref, k_ref, v_ref, c_ref, s_ref, qo_ref, ko_ref, vo_ref):
        c = c_ref[...]
        s = s_ref[...]
        qo_ref[...] = _rope(q_ref[...], c, s, 1.0).astype(_MXU)
        ko_ref[...] = _rope(k_ref[...], c, s, 1.0).astype(_MXU)
        vo_ref[...] = v_ref[...].astype(_MXU)

    tab = pl.BlockSpec((BLK, 128), lambda i: (i, 0))
    kv = pl.BlockSpec((BLK, D_KV), lambda i: (i, 0))
    return _pcall(body, name="rope_fwd", grid=(NBLK,),
                  in_specs=[pl.BlockSpec((BLK, D), lambda i: (i, OFF_Q // D)),
                            pl.BlockSpec((BLK, D_KV), lambda i: (i, OFF_K // D_KV)),
                            pl.BlockSpec((BLK, D_KV), lambda i: (i, OFF_V // D_KV)), tab, tab],
                  out_specs=[pl.BlockSpec((BLK, D), lambda i: (i, 0)), kv, kv],
                  out_shape=[_sds((TP, D), _MXU), _sds((TP, D_KV), _MXU), _sds((TP, D_KV), _MXU)],
                  operands=[z, z, z, cos_t, sin_t], vmem=32, sem=("parallel",))


def _rope_bwd_k(dk, dv, cos_t, sin_t, dz):
    def body(dk_ref, dv_ref, c_ref, s_ref, dz_in, o_ref):
        o_ref[:, 0:D_KV] = _rope(dk_ref[...], c_ref[...], s_ref[...], -1.0).astype(_MXU)
        o_ref[:, D_KV:2 * D_KV] = dv_ref[...].astype(_MXU)

    tab = pl.BlockSpec((BLK, 128), lambda i: (i, 0))
    kv = pl.BlockSpec((BLK, D_KV), lambda i: (i, 0))
    return _pcall(body, name="rope_bwd_k", grid=(NBLK,), in_specs=[kv, kv, tab, tab, _ANY],
                  out_specs=[pl.BlockSpec((BLK, 2 * D_KV), lambda i: (i, OFF_K // (2 * D_KV)))],
                  out_shape=[_sds((TP, D_IN), _MXU)], operands=[dk, dv, cos_t, sin_t, dz], vmem=32, sem=("parallel",),
                  aliases={4: 0})[0]


def _attn_mask(i):
    ql = lax.broadcasted_iota(jnp.int32, (BLK, 3 * BLK), 0)
    kk = lax.broadcasted_iota(jnp.int32, (BLK, 3 * BLK), 1)
    kl = kk % BLK
    part = kk // BLK
    meta = (part == 0) & (kl >= PAD) & ((i >= 1) | (kl <= ql))
    prev = (part == 1) & (i >= 2) & (kl > ql)
    cur = (part == 2) & (i >= 1) & (kl <= ql)
    return meta | prev | cur


def _cat_kv(refs, g):
    return jnp.concatenate([r[:, HD * g:HD * (g + 1)] for r in refs], axis=0)


def _kv_specs():
    return [pl.BlockSpec((BLK, D_KV), lambda i: (0, 0)),
            pl.BlockSpec((BLK, D_KV), lambda i: (jnp.maximum(i - 1, 0), 0)),
            pl.BlockSpec((BLK, D_KV), lambda i: (i, 0))]


def _stack_heads(ref, g, width=HD):
    return jnp.concatenate([ref[:, width * (GRP * g + j):width * (GRP * g + j + 1)] for j in range(GRP)], axis=0)


def _attn_fwd(q, k, v, sinks, *, hook):
    def body(q_ref, k0_ref, kp_ref, kc_ref, v0_ref, vp_ref, vc_ref, sink_ref, o_ref, lse_ref, s_scr, p_scr):
        mask = _attn_mask(pl.program_id(0))
        for g in range(N_KV):
            s_scr[...] = lax.dot_general(_stack_heads(q_ref, g), _cat_kv((k0_ref, kp_ref, kc_ref), g), _NT,
                                         preferred_element_type=F32)
            for j in range(GRP):
                h = GRP * g + j
                rows = slice(BLK * j, BLK * (j + 1))
                sink = sink_ref[h]
                s = jnp.where(mask, s_scr[rows, :] * (HD ** -0.5), NEG_INF)
                mx = jnp.maximum(jnp.max(s, -1, keepdims=True), sink)
                p = jnp.exp(s - mx)
                den = jnp.sum(p, -1, keepdims=True) + jnp.exp(sink - mx)
                p_scr[rows, :] = (p * (1.0 / den)).astype(_MXU)
                lse_ref[:, h:h + 1] = mx + jnp.log(den)
            o8 = jnp.dot(p_scr[...], _cat_kv((v0_ref, vp_ref, vc_ref), g), preferred_element_type=F32)
            for j in range(GRP):
                h = GRP * g + j
                o_ref[:, HD * h:HD * (h + 1)] = o8[BLK * j:BLK * (j + 1)]

    return _pcall(body, name="attn_fwd", grid=(NBLK,),
                  in_specs=[pl.BlockSpec((BLK, D), lambda i: (i, 0))] + _kv_specs() + _kv_specs()
                           + [pl.BlockSpec(memory_space=pltpu.SMEM)],
                  out_specs=[pl.BlockSpec((BLK, D), lambda i: (i, 0)), pl.BlockSpec((BLK, N_Q), lambda i: (i, 0))],
                  out_shape=[_sds((TP, D), F32), _sds((TP, N_Q), F32)], operands=[q, k, k, k, v, v, v, sinks],
                  scratch=[pltpu.VMEM((GRP * BLK, 3 * BLK), F32), pltpu.VMEM((GRP * BLK, 3 * BLK), _MXU)],
                  vmem=40, sem=("parallel",), hook=hook)


def _attn_bwd(q, k, v, sinks, do, o, lse, cos_t, sin_t, dz, *, hook):
    def body(q_ref, k0_ref, kp_ref, kc_ref, v0_ref, vp_ref, vc_ref, sink_ref, do_ref, o_ref, lse_ref, c_ref, s_ref, dz_in,
             dq_ref, dk_ref, dv_ref, dsink_ref, dqrot_ref, s_scr, dp_scr, p_scr, ds_scr):
        i = pl.program_id(0)

        @pl.when(i == 0)
        def _():
            dk_ref[...] = jnp.zeros_like(dk_ref)
            dv_ref[...] = jnp.zeros_like(dv_ref)
            dsink_ref[...] = jnp.zeros_like(dsink_ref)

        mask = _attn_mask(i)
        row_starts = (0, pl.multiple_of(jnp.maximum(i - 1, 0) * BLK, BLK), pl.multiple_of(i * BLK, BLK))
        scale = HD ** -0.5
        for g in range(N_KV):
            gs = slice(HD * g, HD * (g + 1))
            q8 = _stack_heads(q_ref, g)
            dom = _stack_heads(do_ref, g).astype(_MXU)
            kcat = _cat_kv((k0_ref, kp_ref, kc_ref), g)
            s_scr[...] = lax.dot_general(q8, kcat, _NT, preferred_element_type=F32)
            dp_scr[...] = lax.dot_general(dom, _cat_kv((v0_ref, vp_ref, vc_ref), g), _NT, preferred_element_type=F32)
            for j in range(GRP):
                h = GRP * g + j
                hs = slice(HD * h, HD * (h + 1))
                rows = slice(BLK * j, BLK * (j + 1))
                lse_h = lse_ref[:, h:h + 1]
                delta = jnp.sum(do_ref[:, hs] * o_ref[:, hs], axis=-1, keepdims=True)
                dsink_ref[0:1, h:h + 1] += jnp.sum(-jnp.exp(sink_ref[h] - lse_h) * delta, axis=0, keepdims=True)
                p = jnp.exp(jnp.where(mask, s_scr[rows, :] * scale, NEG_INF) - lse_h)
                p_scr[rows, :] = p.astype(_MXU)
                ds_scr[rows, :] = (p * (dp_scr[rows, :] - delta) * scale).astype(_MXU)
            ds = ds_scr[...]
            dq8 = jnp.dot(ds, kcat, preferred_element_type=F32)
            dkcat = lax.dot_general(ds, q8, _TN, preferred_element_type=F32)
            dvcat = lax.dot_general(p_scr[...], dom, _TN, preferred_element_type=F32)
            for part in range(3):
                rows = pl.ds(row_starts[part], BLK)
                dk_ref[rows, gs] += dkcat[BLK * part:BLK * (part + 1)]
                dv_ref[rows, gs] += dvcat[BLK * part:BLK * (part + 1)]
            for j in range(GRP):
                h = GRP * g + j
                dqrot_ref[:, HD * h:HD * (h + 1)] = dq8[BLK * j:BLK * (j + 1)]
        dq_ref[...] = _rope(dqrot_ref[...], c_ref[...], s_ref[...], -1.0).astype(_MXU)

    row = pl.BlockSpec((BLK, D), lambda i: (i, 0))
    tab = pl.BlockSpec((BLK, 128), lambda i: (i, 0))
    full_kv = pl.BlockSpec((TP, D_KV), lambda i: (0, 0))
    return _pcall(
        body, name="attn_bwd", grid=(NBLK,),
        in_specs=[row] + _kv_specs() + _kv_specs() + [pl.BlockSpec(memory_space=pltpu.SMEM), row, row,
                  pl.BlockSpec((BLK, N_Q), lambda i: (i, 0)), tab, tab, _ANY],
        out_specs=[pl.BlockSpec((BLK, D), lambda i: (i, OFF_Q // D)), full_kv, full_kv,
                   pl.BlockSpec((8, 128), lambda i: (0, 0))],
        out_shape=[_sds((TP, D_IN), _MXU), _sds((TP, D_KV), F32), _sds((TP, D_KV), F32), _sds((8, 128), F32)],
        operands=[q, k, k, k, v, v, v, sinks, do, o, lse, cos_t, sin_t, dz],
        scratch=[pltpu.VMEM((BLK, D), F32), pltpu.VMEM((GRP * BLK, 3 * BLK), F32), pltpu.VMEM((GRP * BLK, 3 * BLK), F32),
                 pltpu.VMEM((GRP * BLK, 3 * BLK), _MXU), pltpu.VMEM((GRP * BLK, 3 * BLK), _MXU)],
        vmem=48, sem=("arbitrary",), aliases={13: 0}, hook=hook)


def _cast_into_slot(w32, pos, *, name, tr):
    r, cc = w32.shape

    def body(pos_ref, w_ref, o_ref):
        o_ref[...] = w_ref[...].astype(BF16)

    return _pcall(body, name=name, grid=(r // tr,), in_specs=[pl.BlockSpec((tr, cc), lambda i, p: (i, 0))],
                  out_specs=[pl.BlockSpec((None, tr, cc), lambda i, p: (p[1], i, 0))],
                  out_shape=[_sds((N_SH, r, cc), BF16)], operands=[w32], vmem=32, sem=("parallel",), prefetch=(pos,))[0]


def _pair_add(g, got, pos, *, name, tr, g_has_both_halves):
    n, h, cc = got.shape
    nt = h // tr

    def body(pos_ref, g_ref, r_ref, own_ref, s16_ref):
        s = g_ref[...] + r_ref[...]
        s16_ref[...] = s.astype(BF16)

        @pl.when(pl.program_id(1) == pos_ref[1])
        def _():
            own_ref[...] = s

    g_index = (lambda i, s, p: (s, p[0] * nt + i, 0)) if g_has_both_halves else (lambda i, s, p: (s, i, 0))
    return _pcall(body, name=name, grid=(nt, n),
                  in_specs=[pl.BlockSpec((None, tr, cc), g_index), pl.BlockSpec((None, tr, cc), lambda i, s, p: (s, i, 0))],
                  out_specs=[pl.BlockSpec((tr, cc), lambda i, s, p: (i, 0)),
                             pl.BlockSpec((None, tr, cc), lambda i, s, p: (s, i, 0))],
                  out_shape=[_sds((h, cc), F32), _sds((n, h, cc), BF16)], operands=[g, got], vmem=40,
                  sem=("parallel", "arbitrary"), prefetch=(pos,))


def _sum_chips(own, got, pos, *, name, tr):
    h, cc = own.shape
    nt = h // tr

    def body(pos_ref, o_ref, r_ref, out_ref):
        acc = o_ref[...]
        for k in range(3):
            acc = acc + r_ref[k].astype(F32)
        out_ref[...] = acc

    return _pcall(body, name=name, grid=(nt,),
                  in_specs=[pl.BlockSpec((tr, cc), lambda i, p: (i, 0)), pl.BlockSpec((3, tr, cc), lambda i, p: (0, i, 0))],
                  out_specs=[pl.BlockSpec((tr, cc), lambda i, p: (p[0] * nt + i, 0))],
                  out_shape=[_sds((2 * h, cc), F32)], operands=[own, got], vmem=40, sem=("parallel",), prefetch=(pos,))[0]


def _adamw(w, g, m, v, *, name, tr, g_row0=0):
    r, cc = w.shape
    g_blk0 = g_row0 // tr

    def body(w_ref, g_ref, m_ref, v_ref, go_ref, d_ref, mo_ref, vo_ref):
        gg = g_ref[...]
        go_ref[...] = gg
        m_new = ADAM_B1 * m_ref[...] + (1.0 - ADAM_B1) * gg
        v_new = ADAM_B2 * v_ref[...] + (1.0 - ADAM_B2) * (gg * gg)
        m_hat = m_new / (1.0 - ADAM_B1 ** ADAM_STEP)
        v_hat = v_new / (1.0 - ADAM_B2 ** ADAM_STEP)
        d_ref[...] = -ADAM_LR * (m_hat / (jnp.sqrt(v_hat) + ADAM_EPS) + ADAM_WD * w_ref[...])
        mo_ref[...] = m_new
        vo_ref[...] = v_new

    blk = pl.BlockSpec((tr, cc), lambda i: (i, 0))
    gblk = pl.BlockSpec((tr, cc), lambda i: (g_blk0 + i, 0))
    return _pcall(body, name=name, grid=(r // tr,), in_specs=[blk, gblk, blk, blk], out_specs=[blk] * 4,
                  out_shape=[_sds((r, cc), F32)] * 4, operands=[w, g, m, v], vmem=48, sem=("parallel",))


_SMALL_ROWS = 40
_B_IN_ROWS = 7


def _row_pad(v, rows):
    flat = v.reshape(-1)
    return jnp.pad(flat, (0, rows * D - flat.shape[0])).reshape(rows, D)


def _pack_ra(w):
    return w.transpose(1, 0, 2).reshape(64, D)


def _unpack_ra(p, like):
    return p.reshape(64, N_RB, RB).transpose(1, 0, 2).reshape(like.shape)


def _gate_full(g4):
    return g4.reshape(N_SH, 64, N_RB, RB).transpose(2, 0, 1, 3).reshape(N_RB, RB, RB)


def kernel(x, meta_tokens, ln_emb_g, ln_emb_b, w_in, b_in, conv_w, conv_b, w_ra, b_ra, w_ri, b_ri, lru_lambda, sinks, w_rnn_out, w_attn_out, w_o, b_o, ln_g, ln_b, loss_target, m_meta_tokens, m_ln_emb_g, m_ln_emb_b, m_w_in, m_b_in, m_conv_w, m_conv_b, m_w_ra, m_b_ra, m_w_ri, m_b_ri, m_lru_lambda, m_sinks, m_w_rnn_out, m_w_attn_out, m_w_o, m_b_o, m_ln_g, m_ln_b, v_meta_tokens, v_ln_emb_g, v_ln_emb_b, v_w_in, v_b_in, v_conv_w, v_conv_b, v_w_ra, v_b_ra, v_w_ri, v_b_ri, v_lru_lambda, v_sinks, v_w_rnn_out, v_w_attn_out, v_w_o, v_b_o, v_ln_g, v_ln_b):
    xi, yi, ci = _my_pos()
    shard = 2 * xi + yi
    pos = jnp.stack([ci, shard, 1 - ci]).astype(jnp.int32)
    cos_t, sin_t = _rope_tables()
    zero_bias = jnp.zeros((1, D), F32)
    ln_emb_g2, ln_emb_b2 = ln_emb_g[None], ln_emb_b[None]

    small = jnp.concatenate([conv_w[0], meta_tokens, jnp.zeros((4, 512), F32)], axis=0)
    small4 = _gather_small(small)
    conv_w_full = small4[:, 0:4].transpose(1, 0, 2).reshape(CONV_W, D)
    meta_full = small4[:, 4:20].transpose(1, 0, 2).reshape(N_META, D)
    w_own = _cast_into_slot(w_in[0], pos, name="cast_w_in", tr=256)
    wa_own = _cast_into_slot(jnp.concatenate([w_attn_out[0], w_o[0], _pack_ra(w_ra[0]), _pack_ra(w_ri[0])], axis=0),
                             pos, name="cast_w_a", tr=288)
    wb_own = _cast_into_slot(w_rnn_out[0], pos, name="cast_w_b", tr=256)

    h32, h16, h16_t = _ln_emb_fwd(x[0], meta_full, ln_emb_g2, ln_emb_b2)
    order = jnp.stack([shard, shard ^ 2, shard ^ 1, shard ^ 3]).astype(jnp.int32)
    z, w_in4 = _mm_z_gather(h16, w_own, b_in, order)
    q, k, v = _rope_fwd(z, cos_t, sin_t)
    o, lse, wa4 = _attn_fwd(q, k, v, sinks[0], hook=_hook_gather(wa_own, 0.6))
    w_ra_full = _gate_full(wa4[:, 2 * SQ_ROWS:2 * SQ_ROWS + 64])
    w_ri_full = _gate_full(wa4[:, 2 * SQ_ROWS + 64:2 * SQ_ROWS + 128])
    a_dec, u_in = _rnn_gates_fwd(z, conv_w_full, conv_b, w_ra_full, w_ri_full, b_ra, b_ri, lru_lambda)
    hr, wb4 = _scan_fwd(a_dec, u_in, hook=_hook_gather(wb_own, 0.6))
    sq_w = {0: (wb4, 0), 1: (wa4, 0), 2: (wa4, 1)}

    def sq_nn(a, kk, bias, name, out_cols, out_index, carried=None):
        wp, blk = sq_w[kk]
        return _mm_nn(a, wp, bias, name=name, grid=(2, D // CW), tm=HALF_TP, tn=CW, k=D, a_index=lambda i, j: (i, 0),
                      w_block=(N_SH, SQ_ROWS, CW), w_index=lambda i, j: (0, blk, j), out_cols=out_cols,
                      out_index=out_index, carried=carried)[0]

    def sq_nt(a, a_blk, kk, name, hook=None):
        wp, blk = sq_w[kk]
        return _mm_nt(a, wp, name=name, grid=(2, N_SH, 1), tm=HALF_TP, tn=SQ_ROWS, tk=D,
                      a_index=lambda i, j, q: (i, a_blk), w_block=(None, SQ_ROWS, D),
                      w_index=lambda i, j, q: (j, blk, 0), out_cols=D, hook=hook)

    ya_in, ya_in_t = _mul_silu_fwd(hr, z, OFF_GR, name="gate_a_fwd")
    y2 = sq_nn(ya_in, 0, zero_bias, "mm_ya", 2 * D, lambda i, j: (i, j))
    yb_in, yb_in_t = _mul_silu_fwd(o, z, OFF_GA, name="gate_b_fwd")
    y2 = sq_nn(yb_in, 1, zero_bias, "mm_yb", 2 * D, lambda i, j: (i, D // CW + j), carried=y2)
    mixed, mixed_t = _merge_fwd(y2, z)
    out = sq_nn(mixed, 2, b_o, "mm_out", D, lambda i, j: (i, j))
    dr, sums_o = _ln_out_loss(h32, out, loss_target[0], ln_g, ln_b)

    def sq_tn(at, b, b_blk0, kk, name, carried=None):
        return _mm_tn(at, b, name=name, grid=(N_SH, D // CW), tm=SQ_ROWS, tn=CW, a_index=lambda i, j: (i, 0),
                      b_index=lambda i, j: (0, b_blk0 + j), out_shape=(N_SH, PACK_ROWS, D),
                      out_block=(None, SQ_ROWS, CW), out_index=lambda i, j: (i, kk, j), carried=carried)[0]

    gsq = sq_tn(mixed_t, dr, 0, 2, "mm_dwo")
    dmix = sq_nt(dr, 0, 2, "mm_dmix")[0]
    dy2, dz = _merge_bwd(dmix, y2, z)
    gsq = sq_tn(ya_in_t, dy2, 0, 0, "mm_dwrnn", carried=gsq)
    gsq = sq_tn(yb_in_t, dy2, D // CW, 1, "mm_dwattn", carried=gsq)
    dya_in = sq_nt(dy2, 0, 0, "mm_dyain")[0]
    dhr, dz = _mul_silu_bwd(dya_in, hr, z, OFF_GR, dz, name="gate_a_bwd")
    lam_s = _scan_bwd(a_dec, dhr)
    dz, gsq, sums_r = _rnn_gates_bwd(z, lam_s, hr, conv_w_full, conv_b, w_ra_full, w_ri_full, b_ra, b_ri, lru_lambda, dz, gsq)
    dyb_in = sq_nt(dy2, 1, 1, "mm_dybin")[0]
    do, dz, gsq, got_sq = _mul_silu_bwd(dyb_in, o, z, OFF_GA, dz, name="gate_b_bwd", hook=_hook_pair(gsq, True))
    own_sq, s16_sq = _pair_add(gsq, got_sq, pos, name="red_w_sq_add", tr=208, g_has_both_halves=True)
    dz, dk_rot, dv32, dsink, s16_sq, oth_sq = _attn_bwd(q, k, v, sinks[0], do, o, lse, cos_t, sin_t, dz,
                                                        hook=_hook_scatter(s16_sq))
    dz = _rope_bwd_k(dk_rot, dv32, cos_t, sin_t, dz)
    red_sq = _sum_chips(own_sq, oth_sq, pos, name="red_w_sq_sum", tr=208)
    g_b_in, g_sq = _colsum(dz, name="colsum_dz", tn=TN_IN, hook=_hook_halves(red_sq))

    def dwin(half_idx, name, hook=None):
        return _mm_tn(h16_t, dz, name=name, grid=(1, N_SH * PER_IN), tm=D // 2, tn=TN_IN,
                      a_index=lambda i, j, p: (p[half_idx], 0), b_index=lambda i, j, p: (0, j),
                      out_shape=(N_SH, D // 2, W_IN_COLS), out_block=(None, D // 2, TN_IN),
                      out_index=lambda i, j, p: (j // PER_IN, 0, j % PER_IN), prefetch=(pos,), hook=hook)

    gin_sib = dwin(2, "mm_dwin_sib")[0]
    gin_own, gin_sib, got_in = dwin(0, "mm_dwin_own", hook=_hook_pair(gin_sib, False))
    own_in, s16_in = _pair_add(gin_own, got_in, pos, name="red_w_in_add", tr=128, g_has_both_halves=False)
    dhz, s16_in, oth_in = _mm_nt(dz, w_in4, name="mm_dhz", grid=(2, 2, N_SH), tm=HALF_TP, tn=D // 2, tk=W_IN_COLS,
                                 a_index=lambda i, j, q: (i, q), w_block=(None, D // 2, W_IN_COLS),
                                 w_index=lambda i, j, q: (q, j, 0), out_cols=D, hook=_hook_scatter(s16_in))
    red_in = _sum_chips(own_in, oth_in, pos, name="red_w_in_sum", tr=128)
    g_x, g_meta_local, sums_e, g_in = _ln_emb_bwd(x[0], meta_full, ln_emb_g2, dr, dhz, hook=_hook_halves(red_in))

    big = {"w_in": [t.reshape(w_in.shape) for t in
                    _adamw(w_in[0], g_in, m_w_in[0], v_w_in[0], name="adamw_w_in", tr=128)]}
    for kk, (n, w_, m_, v_) in enumerate([("w_rnn_out", w_rnn_out, m_w_rnn_out, v_w_rnn_out),
                                          ("w_attn_out", w_attn_out, m_w_attn_out, v_w_attn_out), ("w_o", w_o, m_w_o, v_w_o)]):
        big[n] = [t.reshape(w_.shape) for t in
                  _adamw(w_[0], g_sq, m_[0], v_[0], name="adamw_" + n, tr=256, g_row0=SQ_ROWS * kk)]
    for kk, (n, w_, m_, v_) in enumerate([("w_ra", w_ra, m_w_ra, v_w_ra), ("w_ri", w_ri, m_w_ri, v_w_ri)]):
        big[n] = [_unpack_ra(t, w_) for t in
                  _adamw(_pack_ra(w_[0]), g_sq, _pack_ra(m_[0]), _pack_ra(v_[0]), name="adamw_" + n, tr=64,
                         g_row0=3 * SQ_ROWS + 64 * kk)]

    spack = jnp.concatenate([
        sums_e[0:1], sums_e[1:2], _row_pad(g_b_in, _B_IN_ROWS), sums_r[0:4], sums_r[4:5], sums_r[5:6], sums_r[6:7],
        sums_r[7:8], _row_pad(dsink[0:1, 0:N_Q], 1), sums_o[2:3], sums_o[0:1], sums_o[1:2], g_meta_local, sums_o[3:4],
        jnp.zeros((_SMALL_ROWS - 38, D), F32)], axis=0)
    sred = _allreduce_small(spack)
    loss = sred[37, 0]
    col0 = shard * 512
    g_conv_w = lax.dynamic_slice(sred[9:13], (0, col0), (CONV_W, 512))
    g_meta = lax.dynamic_slice(sred[21:37], (0, col0), (N_META, 512))
    small_g = {"ln_emb_g": sred[0:1], "ln_emb_b": sred[1:2], "b_in": sred[2:9], "conv_w": g_conv_w.reshape(1, D),
               "conv_b": sred[13:14], "b_ra": sred[14:15], "b_ri": sred[15:16], "lru_lambda": sred[16:17],
               "sinks": sred[17:18], "b_o": sred[18:19], "ln_g": sred[19:20], "ln_b": sred[20:21],
               "meta_tokens": g_meta.reshape(4, D)}
    small_names = list(small_g)

    def small_pack(vals):
        rows = []
        for n in small_names:
            a = vals[n]
            if n == "b_in":
                a = _row_pad(a, _B_IN_ROWS)
            elif n == "sinks":
                a = _row_pad(a, 1)
            else:
                a = a.reshape(-1, D)
            rows.append(a)
        return jnp.concatenate(rows + [jnp.zeros((24 - 22, D), F32)], axis=0)

    w_small = dict(ln_emb_g=ln_emb_g, ln_emb_b=ln_emb_b, b_in=b_in, conv_w=conv_w, conv_b=conv_b, b_ra=b_ra, b_ri=b_ri,
                   lru_lambda=lru_lambda, sinks=sinks, b_o=b_o, ln_g=ln_g, ln_b=ln_b, meta_tokens=meta_tokens)
    m_small = dict(ln_emb_g=m_ln_emb_g, ln_emb_b=m_ln_emb_b, b_in=m_b_in, conv_w=m_conv_w, conv_b=m_conv_b, b_ra=m_b_ra,
                   b_ri=m_b_ri, lru_lambda=m_lru_lambda, sinks=m_sinks, b_o=m_b_o, ln_g=m_ln_g, ln_b=m_ln_b,
                   meta_tokens=m_meta_tokens)
    v_small = dict(ln_emb_g=v_ln_emb_g, ln_emb_b=v_ln_emb_b, b_in=v_b_in, conv_w=v_conv_w, conv_b=v_conv_b, b_ra=v_b_ra,
                   b_ri=v_b_ri, lru_lambda=v_lru_lambda, sinks=v_sinks, b_o=v_b_o, ln_g=v_ln_g, ln_b=v_ln_b,
                   meta_tokens=v_meta_tokens)
    g_small_pack = jnp.concatenate([small_g[n] for n in small_names] + [jnp.zeros((2, D), F32)], axis=0)
    small_res = _adamw(small_pack(w_small), g_small_pack, small_pack(m_small), small_pack(v_small),
                       name="adamw_small", tr=24)

    small_rows = {}
    r0 = 0
    for n in small_names:
        nrows = small_g[n].shape[0]
        small_rows[n] = (r0, nrows)
        r0 += nrows

    def small_out(packed, n, like):
        a, nrows = small_rows[n]
        flat = packed[a:a + nrows].reshape(-1)
        return flat[:like.size].reshape(like.shape)

    weights = dict(meta_tokens=meta_tokens, ln_emb_g=ln_emb_g, ln_emb_b=ln_emb_b, w_in=w_in, b_in=b_in, conv_w=conv_w,
                   conv_b=conv_b, w_ra=w_ra, b_ra=b_ra, w_ri=w_ri, b_ri=b_ri, lru_lambda=lru_lambda, sinks=sinks,
                   w_rnn_out=w_rnn_out, w_attn_out=w_attn_out, w_o=w_o, b_o=b_o, ln_g=ln_g, ln_b=ln_b)

    def outputs(which):
        return [big[n][which] if n in big else small_out(small_res[which], n, like) for n, like in weights.items()]

    return (loss, g_x[None], *outputs(0), *outputs(1), *outputs(2), *outputs(3))
```

```python
import jax
import jax.numpy as jnp
from jax import lax
from jax.experimental import pallas as pl
from jax.experimental.pallas import tpu as pltpu

F32 = jnp.float32
BF16 = jnp.bfloat16
_MXU = jnp.bfloat16

D = 2048
SEQ = 2048
N_META = 16
BLK = 128
PAD = BLK - N_META
TP = PAD + N_META + SEQ
NBLK = TP // BLK
HALF_TP = TP // 2
N_RB = 8
RB = 256
CONV_W = 4
LRU_C = 8.0
HD = 64
N_Q = 32
N_KV = 4
GRP = 8
D_KV = 256
NEG_INF = -1e30
LN_EPS = 1e-5
ALPHA = 2.0 ** 0.25
ROPE_THETA = 10000.0
OFF_GR, OFF_Q, OFF_K, OFF_V, OFF_GA, OFF_G = 2048, 4096, 6144, 6400, 6656, 8704
D_IN = 12800
N_SH = 4
W_IN_COLS = D_IN // N_SH
TN_IN = 640
PER_IN = W_IN_COLS // TN_IN
CW = 512
RT = TP // 4
SQ_ROWS = 512
PACK_ROWS = 3 * SQ_ROWS + 128

ADAM_LR = 0.001
ADAM_B1 = 0.9
ADAM_B2 = 0.999
ADAM_EPS = 1e-08
ADAM_WD = 0.01
ADAM_STEP = 10

MESH = pl.DeviceIdType.MESH
_MIB = 1024 * 1024
_ANY = pl.BlockSpec(memory_space=pl.ANY)
_NT = (((1,), (1,)), ((), ()))
_TN = (((0,), (0,)), ((), ()))


def _sds(shape, dtype):
    return jax.ShapeDtypeStruct(shape, dtype)


def _sigmoid(x):
    return 1.0 / (1.0 + jnp.exp(-x))


def _my_pos():
    return lax.axis_index("x"), lax.axis_index("y"), lax.axis_index("c")


def _other_chips(x, y):
    return [(1 - x, y), (x, 1 - y), (1 - x, 1 - y)]


def _remote(src, dst, send_sems, recv_sems, k, dev):
    return pltpu.make_async_remote_copy(src_ref=src, dst_ref=dst, send_sem=send_sems.at[k], recv_sem=recv_sems.at[k],
                                        device_id=dev, device_id_type=MESH)


class _Hook:
    def __init__(self, carried, landing, n_sems, start, finish, mid=None, mid_frac=0.5):
        self.carried, self.landing, self.n_sems, self.start, self.finish = list(carried), list(landing), n_sems, start, finish
        self.mid, self.mid_frac = mid, mid_frac


def _hook_gather(buf, mid_frac):
    half = buf.shape[1] // 2
    quarter = half // 2

    def geom(o, ss, rs):
        x, y, c = _my_pos()
        xn, yn, dg = _other_chips(x, y)
        slot = lambda p: 2 * p[0] + p[1]
        mine_rows = pl.ds(pl.multiple_of(c * half, 16), half)
        sib_rows = pl.ds(pl.multiple_of((1 - c) * half, 16), half)
        q_rows = lambda r: pl.ds(pl.multiple_of(c * half + r * quarter, 16), quarter)

        def cp(k, s, rows, dev):
            part = o.at[s, rows]
            return _remote(part, part, ss, rs, k, dev)

        return dict(
            direct=lambda k, s: cp(k, s, mine_rows, ((xn, yn)[k][0], (xn, yn)[k][1], c)),
            relay=lambda r, s: cp(2 + r, s, q_rows(r), ((yn, xn)[r][0], (yn, xn)[r][1], c)),
            sibling=lambda k, s, mine: cp(4 + k, s, mine_rows if mine else sib_rows, (x, y, 1 - c)),
            me=slot((x, y)), slots=(slot(xn), slot(yn), slot(dg)))

    def start(car, land, ss, rs):
        g = geom(car[0], ss, rs)
        g["direct"](0, g["me"]).start()
        g["direct"](1, g["me"]).start()

    def mid(car, land, ss, rs):
        g = geom(car[0], ss, rs)
        for k in range(2):
            g["direct"](k, g["slots"][k]).wait_recv()
            g["relay"](k, g["slots"][k]).start()
            g["sibling"](k, g["slots"][k], True).start()

    def finish(car, land, ss, rs):
        g = geom(car[0], ss, rs)
        dslot = g["slots"][2]
        g["relay"](0, dslot).wait_recv()
        g["relay"](1, dslot).wait_recv()
        g["sibling"](2, dslot, True).start()
        for k in range(3):
            g["sibling"](k, g["slots"][k], False).wait_recv()
        for k in range(2):
            g["direct"](k, g["me"]).wait_send()
            g["relay"](k, g["slots"][k]).wait_send()
        for k in range(3):
            g["sibling"](k, g["slots"][k], True).wait_send()

    return _Hook([buf], [], 7, start, finish, mid=mid, mid_frac=mid_frac)


def _hook_pair(g, half_rows):
    n, r, cc = g.shape
    h = r // 2 if half_rows else r

    def plan(car, land, ss, rs):
        x, y, c = _my_pos()
        cps = []
        for s in range(n):
            src = car[0].at[s, pl.ds(pl.multiple_of((1 - c) * h, 8), h)] if half_rows else car[0].at[s]
            cps.append(_remote(src, land[0].at[s], ss, rs, s, (x, y, 1 - c)))
        return cps

    def start(car, land, ss, rs):
        for cp in plan(car, land, ss, rs):
            cp.start()

    def finish(car, land, ss, rs):
        cps = plan(car, land, ss, rs)
        for cp in cps:
            cp.wait_recv()
        for cp in cps:
            cp.wait_send()

    return _Hook([g], [_sds((n, h, cc), g.dtype)], n, start, finish)


def _hook_scatter(s16):
    _, h, cc = s16.shape

    def plan(car, land, ss, rs):
        x, y, c = _my_pos()
        return [_remote(car[0].at[2 * px + py], land[0].at[k], ss, rs, k, (px, py, c))
                for k, (px, py) in enumerate(_other_chips(x, y))]

    def start(car, land, ss, rs):
        for cp in plan(car, land, ss, rs):
            cp.start()

    def finish(car, land, ss, rs):
        cps = plan(car, land, ss, rs)
        for cp in cps:
            cp.wait_recv()
        for cp in cps:
            cp.wait_send()

    return _Hook([s16], [_sds((3, h, cc), s16.dtype)], 3, start, finish)


def _hook_halves(full):
    h = full.shape[0] // 2

    def half_copy(car, ss, rs, which):
        x, y, c = _my_pos()
        rows = car[0].at[pl.ds(pl.multiple_of((c + which - 2 * c * which) * h, 8), h)]
        return _remote(rows, rows, ss, rs, 0, (x, y, 1 - c))

    def start(car, land, ss, rs):
        half_copy(car, ss, rs, 0).start()

    def finish(car, land, ss, rs):
        half_copy(car, ss, rs, 1).wait_recv()
        half_copy(car, ss, rs, 0).wait_send()

    return _Hook([full], [], 1, start, finish)


def _pcall(body, *, name, grid, in_specs, out_specs, out_shape, operands, scratch=(), vmem=48, sem=None,
           prefetch=(), aliases=None, hook=None):
    n_pre, n_in, n_out, n_scr = len(prefetch), len(in_specs), len(out_specs), len(scratch)
    in_specs, out_specs, out_shape, scratch = list(in_specs), list(out_specs), list(out_shape), list(scratch)
    io_alias = {n_pre + a: b for a, b in (aliases or {}).items()}
    operands = list(operands)
    kernel_body = body
    if hook is not None:
        n_car, n_land = len(hook.carried), len(hook.landing)
        for t, arr in enumerate(hook.carried):
            io_alias[n_pre + n_in + t] = n_out + t
        in_specs += [_ANY] * n_car
        out_specs += [_ANY] * (n_car + n_land)
        out_shape += [_sds(a.shape, a.dtype) for a in hook.carried] + hook.landing
        scratch += [pltpu.SemaphoreType.DMA((hook.n_sems,)), pltpu.SemaphoreType.DMA((hook.n_sems,))]
        operands += hook.carried
        sem = ("arbitrary",) * len(grid)

        def kernel_body(*refs):
            pre, rest = refs[:n_pre], refs[n_pre:]
            ins = rest[:n_in]
            outs = rest[n_in + n_car:n_in + n_car + n_out]
            car = rest[n_in + n_car + n_out:n_in + 2 * n_car + n_out]
            land = rest[n_in + 2 * n_car + n_out:n_in + 2 * n_car + n_out + n_land]
            scr = rest[n_in + 2 * n_car + n_out + n_land:]
            send_sems, recv_sems = scr[n_scr], scr[n_scr + 1]
            first = pl.program_id(0) == 0
            last = pl.program_id(0) == grid[0] - 1
            for d in range(1, len(grid)):
                first = first & (pl.program_id(d) == 0)
                last = last & (pl.program_id(d) == grid[d] - 1)

            @pl.when(first)
            def _():
                hook.start(car, land, send_sems, recv_sems)

            if hook.mid is not None:
                step = pl.program_id(0)
                total = grid[0]
                for d in range(1, len(grid)):
                    step = step * grid[d] + pl.program_id(d)
                    total *= grid[d]

                @pl.when(step == int(total * hook.mid_frac))
                def _():
                    hook.mid(car, land, send_sems, recv_sems)

            body(*pre, *ins, *outs, *scr[:n_scr])

            @pl.when(last)
            def _():
                hook.finish(car, land, send_sems, recv_sems)

    params = pltpu.CompilerParams(vmem_limit_bytes=vmem * _MIB, dimension_semantics=sem,
                                  has_side_effects=hook is not None)
    if n_pre:
        call = pl.pallas_call(
            kernel_body, name=name, out_shape=out_shape, input_output_aliases=io_alias, compiler_params=params,
            grid_spec=pltpu.PrefetchScalarGridSpec(num_scalar_prefetch=n_pre, grid=grid, in_specs=in_specs,
                                                   out_specs=out_specs, scratch_shapes=scratch))
    else:
        call = pl.pallas_call(kernel_body, name=name, grid=grid, in_specs=in_specs, out_specs=out_specs,
                              out_shape=out_shape, scratch_shapes=scratch, input_output_aliases=io_alias,
                              compiler_params=params)
    return call(*prefetch, *operands)


def _gather_small(blk):
    r, cc = blk.shape

    def body(x_ref, o_ref, send_sems, recv_sems):
        x, y, c = _my_pos()
        me = 2 * x + y
        o_ref[me] = x_ref[...]
        sends = [_remote(x_ref, o_ref.at[me], send_sems, recv_sems, k, (px, py, c))
                 for k, (px, py) in enumerate(_other_chips(x, y))]
        for cp in sends:
            cp.start()
        for k, (px, py) in enumerate(_other_chips(x, y)):
            _remote(x_ref, o_ref.at[2 * px + py], send_sems, recv_sems, k, (px, py, c)).wait_recv()
        for cp in sends:
            cp.wait_send()

    vm = pl.BlockSpec(memory_space=pltpu.VMEM)
    return pl.pallas_call(
        body, name="gather_small", in_specs=[vm], out_specs=vm, out_shape=_sds((N_SH, r, cc), blk.dtype),
        scratch_shapes=[pltpu.SemaphoreType.DMA((3,)), pltpu.SemaphoreType.DMA((3,))],
        compiler_params=pltpu.CompilerParams(has_side_effects=True),
    )(blk)


N_DEV = 8


def _allreduce_small(pack):
    r, cc = pack.shape

    def body(x_ref, o_ref, buf_ref, send_sems, recv_sems):
        x, y, c = _my_pos()
        me = 4 * x + 2 * y + c
        buf_ref[me] = x_ref[...]
        cps = []
        for k in range(1, N_DEV):
            peer = (x ^ ((k >> 2) & 1), y ^ ((k >> 1) & 1), c ^ (k & 1))
            cps.append(_remote(x_ref, buf_ref.at[me], send_sems, recv_sems, k - 1, peer))
        for cp in cps:
            cp.start()
        for k in range(1, N_DEV):
            peer = (x ^ ((k >> 2) & 1), y ^ ((k >> 1) & 1), c ^ (k & 1))
            src = 4 * peer[0] + 2 * peer[1] + peer[2]
            _remote(x_ref, buf_ref.at[src], send_sems, recv_sems, k - 1, peer).wait_recv()
        acc = buf_ref[0]
        for d in range(1, N_DEV):
            acc = acc + buf_ref[d]
        o_ref[...] = acc
        for cp in cps:
            cp.wait_send()

    vm = pl.BlockSpec(memory_space=pltpu.VMEM)
    return pl.pallas_call(
        body, name="allreduce_small", in_specs=[vm], out_specs=vm, out_shape=_sds((r, cc), F32),
        scratch_shapes=[pltpu.VMEM((N_DEV, r, cc), F32), pltpu.SemaphoreType.DMA((N_DEV - 1,)),
                        pltpu.SemaphoreType.DMA((N_DEV - 1,))],
        compiler_params=pltpu.CompilerParams(has_side_effects=True),
    )(pack)


def _mm_nn(a, w, bias, *, name, grid, tm, tn, k, a_index, w_block, w_index, out_cols, out_index, carried=None, hook=None):
    m = a.shape[0]

    def body(a_ref, w_ref, b_ref, *rest):
        o_ref = rest[-1]
        wv = w_ref[...]
        acc = jnp.dot(a_ref[...].astype(_MXU), wv.reshape(k, tn).astype(_MXU), preferred_element_type=F32)
        o_ref[...] = acc + b_ref[...]

    operands = [a, w, bias] + ([carried] if carried is not None else [])
    return _pcall(
        body, name=name, grid=grid,
        in_specs=[pl.BlockSpec((tm, k), a_index), pl.BlockSpec(w_block, w_index),
                  pl.BlockSpec((1, tn), lambda i, j: (0, j))] + ([_ANY] if carried is not None else []),
        out_specs=[pl.BlockSpec((tm, tn), out_index)], out_shape=[_sds((m, out_cols), F32)], operands=operands,
        vmem=56, sem=("parallel", "parallel"), aliases={3: 0} if carried is not None else None, hook=hook)


def _mm_nt(a, w, *, name, grid, tm, tn, tk, a_index, w_block, w_index, out_cols, hook=None):
    m = a.shape[0]
    nk = grid[2]

    def body(a_ref, w_ref, o_ref, acc_ref):
        part = lax.dot_general(a_ref[...].astype(_MXU), w_ref[...].astype(_MXU), _NT, preferred_element_type=F32)
        if nk == 1:
            o_ref[...] = part
        else:
            kidx = pl.program_id(2)

            @pl.when(kidx == 0)
            def _():
                acc_ref[...] = part

            @pl.when(kidx > 0)
            def _():
                acc_ref[...] += part

            @pl.when(kidx == nk - 1)
            def _():
                o_ref[...] = acc_ref[...]

    return _pcall(
        body, name=name, grid=grid, in_specs=[pl.BlockSpec((tm, tk), a_index), pl.BlockSpec(w_block, w_index)],
        out_specs=[pl.BlockSpec((tm, tn), lambda i, j, q: (i, j))], out_shape=[_sds((m, out_cols), F32)],
        operands=[a, w], scratch=[pltpu.VMEM((tm, tn) if nk > 1 else (8, 128), F32)], vmem=60,
        sem=("parallel", "parallel", "arbitrary"), hook=hook)


def _mm_tn(at, b, *, name, grid, tm, tn, a_index, b_index, out_shape, out_block, out_index, carried=None,
           prefetch=(), hook=None):
    t = at.shape[1]

    def body(*refs):
        a_ref, b_ref = refs[len(prefetch)], refs[len(prefetch) + 1]
        refs[-1][...] = jnp.dot(a_ref[...].astype(_MXU), b_ref[...].astype(_MXU), preferred_element_type=F32)

    operands = [at, b] + ([carried] if carried is not None else [])
    return _pcall(
        body, name=name, grid=grid,
        in_specs=[pl.BlockSpec((tm, t), a_index), pl.BlockSpec((t, tn), b_index)] + ([_ANY] if carried is not None else []),
        out_specs=[pl.BlockSpec(out_block, out_index)], out_shape=[_sds(out_shape, F32)], operands=operands,
        vmem=56, sem=("parallel", "parallel"), aliases={2: 0} if carried is not None else None, prefetch=prefetch, hook=hook)


def _remote_tile(n):
    near = 2 * PER_IN
    if isinstance(n, int):
        return (n % 2, n // 2) if n < near else (2, n - near)
    return jnp.where(n < near, n % 2, 2), jnp.where(n < near, n // 2, n - near)


def _mm_z_gather(h16, w_own, b_in, order):
    n_tiles = N_SH * PER_IN
    n_remote = 3 * PER_IN
    half = D // 2

    def body(order_ref, a_ref, b_ref, w_in_ref, z_ref, w_ref, wbuf, tile_sems, ss, rs):
        del w_in_ref
        j = pl.program_id(0)
        x, y, c = _my_pos()
        me = 2 * x + y
        chips = _other_chips(x, y)
        mine_rows = pl.ds(pl.multiple_of(c * half, 16), half)
        sib_rows = pl.ds(pl.multiple_of((1 - c) * half, 16), half)

        slots = [2 * px + py for px, py in chips]
        cols = lambda t: pl.ds(t * TN_IN, TN_IN)
        q_rows = lambda r: pl.ds(pl.multiple_of(c * half + r * (half // 2), 16), half // 2)

        def direct(rel, t, slot):
            px, py = chips[rel]
            part = w_ref.at[slot, mine_rows, cols(t)]
            return _remote(part, part, ss, rs, 2 * t + rel, (px, py, c))

        def relay(r, t, slot):
            px, py = chips[1 - r]
            part = w_ref.at[slot, q_rows(r), cols(t)]
            return _remote(part, part, ss, rs, 2 * PER_IN + 2 * t + r, (px, py, c))

        def d2d(n, rows):
            rel, t = _remote_tile(n)
            part = w_ref.at[slots[rel], rows, cols(t)]
            return _remote(part, part, ss, rs, 4 * PER_IN + n, (x, y, 1 - c))

        def tile_copy(step):
            rel, t = _remote_tile(jnp.maximum(step - PER_IN, 0))
            slot = jnp.where(step < PER_IN, me, me ^ jnp.where(rel == 0, 2, jnp.where(rel == 1, 1, 3)))
            col = pl.multiple_of(jnp.where(step < PER_IN, step, t) * TN_IN, 128)
            return pltpu.make_async_copy(w_ref.at[slot, :, pl.ds(col, TN_IN)], wbuf.at[step % 2], tile_sems.at[step % 2])

        @pl.when(j == 0)
        def _():
            for t in range(PER_IN):
                direct(0, t, me).start()
                direct(1, t, me).start()
            tile_copy(0).start()

        for n in range(n_remote):
            rel, t = _remote_tile(n)

            @pl.when(j == n + PER_IN - 3)
            def _():
                if rel < 2:
                    direct(rel, t, slots[rel]).wait_recv()
                    relay(rel, t, slots[rel]).start()
                else:
                    relay(0, t, slots[2]).wait_recv()
                    relay(1, t, slots[2]).wait_recv()
                d2d(n, mine_rows).start()

            @pl.when(j == n + PER_IN - 2)
            def _():
                d2d(n, sib_rows).wait_recv()

        @pl.when(j + 1 < n_tiles)
        def _():
            tile_copy(j + 1).start()

        tile_copy(j).wait()
        z_ref[...] = jnp.dot(a_ref[...], wbuf[j % 2], preferred_element_type=F32) + b_ref[...]

        @pl.when(j == n_tiles - 1)
        def _():
            for t in range(PER_IN):
                for r in range(2):
                    direct(r, t, me).wait_send()
                    relay(r, t, slots[r]).wait_send()
            for n in range(n_remote):
                d2d(n, mine_rows).wait_send()

    def col_tile(j, o):
        rel, t = _remote_tile(jnp.maximum(j - PER_IN, 0))
        return 0, jnp.where(j < PER_IN, o[0] * PER_IN + j, o[1 + rel] * PER_IN + t)

    return pl.pallas_call(
        body, name="mm_z_gather",
        grid_spec=pltpu.PrefetchScalarGridSpec(
            num_scalar_prefetch=1, grid=(n_tiles,),
            in_specs=[pl.BlockSpec((TP, D), lambda j, o: (0, 0)), pl.BlockSpec((1, TN_IN), col_tile), _ANY],
            out_specs=[pl.BlockSpec((TP, TN_IN), col_tile), _ANY],
            scratch_shapes=[pltpu.VMEM((2, D, TN_IN), BF16), pltpu.SemaphoreType.DMA((2,)),
                            pltpu.SemaphoreType.DMA((4 * PER_IN + n_remote,)),
                            pltpu.SemaphoreType.DMA((4 * PER_IN + n_remote,))]),
        out_shape=[_sds((TP, D_IN), F32), _sds(w_own.shape, w_own.dtype)],
        input_output_aliases={3: 1},
        compiler_params=pltpu.CompilerParams(vmem_limit_bytes=60 * _MIB, dimension_semantics=("arbitrary",),
                                             has_side_effects=True),
    )(order, h16, b_in, w_own)


def _padded_rows(i, x_ref, meta_ref):
    head = jnp.concatenate([jnp.zeros((PAD, D), F32), meta_ref[...]], axis=0)
    return jnp.where(i == 0, head, x_ref[...])


def _stream_specs():
    return [pl.BlockSpec((BLK, D), lambda i: (jnp.maximum(i - 1, 0), 0)), pl.BlockSpec((N_META, D), lambda i: (0, 0))]


def _ln_emb_fwd(x, meta, g, b):
    def body(x_ref, meta_ref, g_ref, b_ref, h32_ref, h16_ref, h16t_ref):
        x = _padded_rows(pl.program_id(0), x_ref, meta_ref)
        mu = jnp.mean(x, axis=-1, keepdims=True)
        xc = x - mu
        var = jnp.mean(xc * xc, axis=-1, keepdims=True)
        y = xc * lax.rsqrt(var + LN_EPS) * g_ref[...] + b_ref[...]
        h32_ref[...] = y
        h16_ref[...] = y.astype(_MXU)
        h16t_ref[...] = y.T.astype(_MXU)

    row = pl.BlockSpec((BLK, D), lambda i: (i, 0))
    vec = pl.BlockSpec((1, D), lambda i: (0, 0))
    return _pcall(body, name="ln_emb_fwd", grid=(NBLK,), in_specs=_stream_specs() + [vec, vec],
                  out_specs=[row, row, pl.BlockSpec((D, BLK), lambda i: (0, i))],
                  out_shape=[_sds((TP, D), F32), _sds((TP, D), _MXU), _sds((D, TP), _MXU)], operands=[x, meta, g, b],
                  vmem=32, sem=("parallel",))


def _ln_emb_bwd(x, meta, g, dr, dhz, *, hook):
    def body(x_ref, meta_ref, g_ref, dr_ref, dhz_ref, dx_ref, dmeta_ref, acc_ref):
        i = pl.program_id(0)

        @pl.when(i == 0)
        def _():
            acc_ref[...] = jnp.zeros_like(acc_ref)

        x = _padded_rows(i, x_ref, meta_ref)
        mu = jnp.mean(x, axis=-1, keepdims=True)
        xc = x - mu
        var = jnp.mean(xc * xc, axis=-1, keepdims=True)
        rstd = lax.rsqrt(var + LN_EPS)
        xhat = xc * rstd
        dh = ALPHA * dr_ref[...] + dhz_ref[...]
        acc_ref[0:1, :] += jnp.sum(dh * xhat, axis=0, keepdims=True)
        acc_ref[1:2, :] += jnp.sum(dh, axis=0, keepdims=True)
        dxh = dh * g_ref[...]
        m1 = jnp.mean(dxh, axis=-1, keepdims=True)
        m2 = jnp.mean(dxh * xhat, axis=-1, keepdims=True)
        dx = rstd * (dxh - m1 - xhat * m2)
        dx_ref[...] = dx

        @pl.when(i == 0)
        def _():
            dmeta_ref[...] = dx[PAD:BLK]

    row = pl.BlockSpec((BLK, D), lambda i: (i, 0))
    vec = pl.BlockSpec((1, D), lambda i: (0, 0))
    xs, ms = _stream_specs()
    return _pcall(body, name="ln_emb_bwd", grid=(NBLK,), in_specs=[xs, ms, vec, row, row],
                  out_specs=[xs, ms, pl.BlockSpec((8, D), lambda i: (0, 0))],
                  out_shape=[_sds((SEQ, D), F32), _sds((N_META, D), F32), _sds((8, D), F32)],
                  operands=[x, meta, g, dr, dhz], vmem=32, sem=("arbitrary",), hook=hook)


def _mul_silu_fwd(a, z, off, *, name):
    def body(a_ref, z_ref, o_ref, t_ref):
        zz = z_ref[...]
        y = a_ref[...] * (zz * _sigmoid(zz))
        o_ref[...] = y.astype(_MXU)
        t_ref[...] = y.T.astype(_MXU)

    strip = pl.BlockSpec((TP, CW), lambda j: (0, j))
    return _pcall(body, name=name, grid=(D // CW,),
                  in_specs=[strip, pl.BlockSpec((TP, CW), lambda j: (0, off // CW + j))],
                  out_specs=[strip, pl.BlockSpec((CW, TP), lambda j: (j, 0))],
                  out_shape=[_sds((TP, D), _MXU), _sds((D, TP), _MXU)], operands=[a, z], vmem=56, sem=("parallel",))


def _mul_silu_bwd(dy, a, z, off, dz, *, name, hook=None):
    def body(dy_ref, a_ref, z_ref, dz_in, da_ref, dg_ref):
        zz = z_ref[...]
        sg = _sigmoid(zz)
        d = dy_ref[...]
        da_ref[...] = d * (zz * sg)
        dg_ref[...] = (d * a_ref[...] * (sg * (1.0 + zz * (1.0 - sg)))).astype(_MXU)

    blk = pl.BlockSpec((RT, CW), lambda i, j: (i, j))
    zblk = pl.BlockSpec((RT, CW), lambda i, j: (i, off // CW + j))
    return _pcall(body, name=name, grid=(TP // RT, D // CW), in_specs=[blk, blk, zblk, _ANY], out_specs=[blk, zblk],
                  out_shape=[_sds((TP, D), F32), _sds((TP, D_IN), _MXU)], operands=[dy, a, z, dz], vmem=32,
                  sem=("parallel", "parallel"), aliases={3: 1}, hook=hook)


def _merge_fwd(y2, z):
    w = 256

    def body(ya_ref, yb_ref, ga_ref, gb_ref, o_ref, t_ref):
        y = _sigmoid(ga_ref[...]) * ya_ref[...] + _sigmoid(gb_ref[...]) * yb_ref[...]
        o_ref[...] = y.astype(_MXU)
        t_ref[...] = y.T.astype(_MXU)

    nb = D // w
    strip = pl.BlockSpec((TP, w), lambda j: (0, j))
    return _pcall(body, name="merge_fwd", grid=(nb,),
                  in_specs=[strip, pl.BlockSpec((TP, w), lambda j: (0, nb + j)),
                            pl.BlockSpec((TP, w), lambda j: (0, OFF_G // w + j)),
                            pl.BlockSpec((TP, w), lambda j: (0, (OFF_G + D) // w + j))],
                  out_specs=[strip, pl.BlockSpec((w, TP), lambda j: (j, 0))],
                  out_shape=[_sds((TP, D), _MXU), _sds((D, TP), _MXU)], operands=[y2, y2, z, z], vmem=56,
                  sem=("parallel",))


def _merge_bwd(dmix, y2, z):
    nb = D // CW

    def body(dm_ref, y_ref, g_ref, dy_ref, dg_ref):
        dm = dm_ref[...]
        sg = _sigmoid(g_ref[...])
        dy_ref[...] = (dm * sg).astype(_MXU)
        dg_ref[...] = (dm * y_ref[...] * sg * (1.0 - sg)).astype(_MXU)

    blk = pl.BlockSpec((RT, CW), lambda i, j: (i, j))
    gblk = pl.BlockSpec((RT, CW), lambda i, j: (i, OFF_G // CW + j))
    return _pcall(body, name="merge_bwd", grid=(TP // RT, 2 * nb),
                  in_specs=[pl.BlockSpec((RT, CW), lambda i, j: (i, j % nb)), blk, gblk], out_specs=[blk, gblk],
                  out_shape=[_sds((TP, 2 * D), _MXU), _sds((TP, D_IN), _MXU)], operands=[dmix, y2, z], vmem=32,
                  sem=("parallel", "parallel"))


def _ln_out_loss(h32, out, target, g, b):
    def body(h_ref, o_ref, t_ref, g_ref, b_ref, dr_ref, acc_ref):
        i = pl.program_id(0)

        @pl.when(i == 0)
        def _():
            acc_ref[...] = jnp.zeros_like(acc_ref)

        r = ALPHA * h_ref[...] + o_ref[...]
        mu = jnp.mean(r, axis=-1, keepdims=True)
        rc = r - mu
        var = jnp.mean(rc * rc, axis=-1, keepdims=True)
        rstd = lax.rsqrt(var + LN_EPS)
        xhat = rc * rstd
        gg = g_ref[...]
        y = xhat * gg + b_ref[...]
        real = (i >= 1).astype(F32)
        diff = (y - t_ref[...]) * real
        dy = diff * (1.0 / D)
        dxh = dy * gg
        m1 = jnp.mean(dxh, axis=-1, keepdims=True)
        m2 = jnp.mean(dxh * xhat, axis=-1, keepdims=True)
        dr = rstd * (dxh - m1 - xhat * m2)
        dr_ref[...] = dr
        acc_ref[0:1, :] += jnp.sum(dy * xhat, axis=0, keepdims=True)
        acc_ref[1:2, :] += jnp.sum(dy, axis=0, keepdims=True)
        acc_ref[2:3, :] += jnp.sum(dr, axis=0, keepdims=True)
        acc_ref[3:4, :] += (0.5 / D) * jnp.sum(diff * diff)

    row = pl.BlockSpec((BLK, D), lambda i: (i, 0))
    vec = pl.BlockSpec((1, D), lambda i: (0, 0))
    return _pcall(body, name="ln_out_loss", grid=(NBLK,),
                  in_specs=[row, row, pl.BlockSpec((BLK, D), lambda i: (jnp.maximum(i - 1, 0), 0)), vec, vec],
                  out_specs=[row, pl.BlockSpec((8, D), lambda i: (0, 0))],
                  out_shape=[_sds((TP, D), F32), _sds((8, D), F32)], operands=[h32, out, target, g, b], vmem=32,
                  sem=("arbitrary",))


def _colsum(x, *, name, tn, hook=None):
    _, n = x.shape

    def body(x_ref, o_ref):
        o_ref[...] = jnp.sum(x_ref[...].astype(F32), axis=0, keepdims=True)

    return _pcall(body, name=name, grid=(n // tn,), in_specs=[pl.BlockSpec((TP, tn), lambda j: (0, j))],
                  out_specs=[pl.BlockSpec((1, tn), lambda j: (0, j))], out_shape=[_sds((1, n), F32)], operands=[x],
                  vmem=32, sem=("parallel",), hook=hook)


def _rnn_recompute(xr_ref, cw_ref, cb_ref, wra_ref, wri_ref, bra_ref, bri_ref, lam_ref):
    rows = lax.broadcasted_iota(jnp.int32, (TP, 1), 0)
    valid = (rows >= PAD).astype(F32)
    first = rows == PAD
    x = xr_ref[...] * valid
    cw = cw_ref[...]
    shifted = [x, pltpu.roll(x, 1, 0), pltpu.roll(x, 2, 0), pltpu.roll(x, 3, 0)]
    c = cb_ref[...] + cw[0:1, :] * shifted[0] + cw[1:2, :] * shifted[1] + cw[2:3, :] * shifted[2] + cw[3:4, :] * shifted[3]
    cm = c.astype(_MXU)
    gr = _sigmoid(jnp.dot(cm, wra_ref[...].astype(_MXU), preferred_element_type=F32) + bra_ref[...])
    gi = _sigmoid(jnp.dot(cm, wri_ref[...].astype(_MXU), preferred_element_type=F32) + bri_ref[...])
    lam = lam_ref[...]
    ls = jnp.minimum(lam, 0.0) - jnp.log(1.0 + jnp.exp(-jnp.abs(lam)))
    log_a = LRU_C * gr * ls
    a = jnp.exp(log_a)
    mult = jnp.where(first, 1.0, jnp.sqrt(1.0 - jnp.exp(2.0 * log_a)))
    return dict(valid=valid, first=first, shifted=shifted, c=c, cm=cm, gr=gr, gi=gi, ls=ls, a=a, mult=mult, lam=lam)


def _rnn_specs():
    col = pl.BlockSpec((TP, RB), lambda n: (0, n))
    vec = pl.BlockSpec((1, RB), lambda n: (0, n))
    return dict(col=col, vec=vec, cw=pl.BlockSpec((CONV_W, RB), lambda n: (0, n)),
                wblk=pl.BlockSpec((None, RB, RB), lambda n: (n, 0, 0)))


def _rnn_gates_fwd(z, conv_w, conv_b, w_ra, w_ri, b_ra, b_ri, lam):
    def body(xr_ref, cw_ref, cb_ref, wra_ref, wri_ref, bra_ref, bri_ref, lam_ref, a_ref, u_ref):
        r = _rnn_recompute(xr_ref, cw_ref, cb_ref, wra_ref, wri_ref, bra_ref, bri_ref, lam_ref)
        a_ref[...] = r["a"]
        u_ref[...] = r["mult"] * r["gi"] * r["c"] * r["valid"]

    s = _rnn_specs()
    return _pcall(body, name="rnn_gates_fwd", grid=(N_RB,),
                  in_specs=[s["col"], s["cw"], s["vec"], s["wblk"], s["wblk"], s["vec"], s["vec"], s["vec"]],
                  out_specs=[s["col"], s["col"]], out_shape=[_sds((TP, D), F32)] * 2,
                  operands=[z, conv_w, conv_b, w_ra, w_ri, b_ra, b_ri, lam], vmem=56, sem=("parallel",))


SCAN_ROWS = 272


SUB = 8


def _tile_scan(a, u, reverse):
    rows = lax.broadcasted_iota(jnp.int32, a.shape, 0)
    for d in (1, 2, 4):
        shift = SUB - d if reverse else d
        inside = (rows < SUB - d) if reverse else (rows >= d)
        u = u + a * jnp.where(inside, pltpu.roll(u, shift, 0), 0.0)
        a = a * jnp.where(inside, pltpu.roll(a, shift, 0), 1.0)
    return a, u


def _scan_fwd(a, u, *, hook):
    def body(a_ref, u_ref, h_ref, carry_ref):
        @pl.when(pl.program_id(1) == 0)
        def _():
            carry_ref[...] = jnp.zeros_like(carry_ref)

        def step(r, h):
            rows = pl.ds(pl.multiple_of(r * SUB, SUB), SUB)
            prod, part = _tile_scan(a_ref[rows, :], u_ref[rows, :], False)
            ht = part + prod * h
            h_ref[rows, :] = ht
            return ht[SUB - 1:SUB, :]

        carry_ref[...] = lax.fori_loop(0, SCAN_ROWS // SUB, step, carry_ref[...], unroll=2)

    blk = pl.BlockSpec((SCAN_ROWS, CW), lambda j, i: (i, j))
    return _pcall(body, name="scan_fwd", grid=(D // CW, TP // SCAN_ROWS), in_specs=[blk, blk], out_specs=[blk],
                  out_shape=[_sds((TP, D), F32)], operands=[a, u], scratch=[pltpu.VMEM((1, CW), F32)], vmem=32,
                  sem=("parallel", "arbitrary"), hook=hook)


def _scan_bwd(a, dh):
    nst = TP // SCAN_ROWS
    n_tiles = SCAN_ROWS // SUB

    def body(a_ref, d_ref, o_ref, lam_ref, anext_ref):
        @pl.when(pl.program_id(1) == 0)
        def _():
            lam_ref[...] = jnp.zeros_like(lam_ref)
            anext_ref[...] = jnp.zeros_like(anext_ref)

        def step(q, carry):
            lam_next, a_next = carry
            rows = pl.ds(pl.multiple_of((n_tiles - 1 - q) * SUB, SUB), SUB)
            at = a_ref[rows, :]
            last = lax.broadcasted_iota(jnp.int32, at.shape, 0) == SUB - 1
            b = jnp.where(last, a_next, pltpu.roll(at, SUB - 1, 0))
            prod, part = _tile_scan(b, d_ref[rows, :], True)
            lam = part + prod * lam_next
            o_ref[rows, :] = lam
            return lam[0:1, :], at[0:1, :]

        lam, an = lax.fori_loop(0, n_tiles, step, (lam_ref[...], anext_ref[...]), unroll=2)
        lam_ref[...] = lam
        anext_ref[...] = an

    blk = pl.BlockSpec((SCAN_ROWS, CW), lambda j, i: (nst - 1 - i, j))
    return _pcall(body, name="scan_bwd", grid=(D // CW, nst), in_specs=[blk, blk], out_specs=[blk],
                  out_shape=[_sds((TP, D), F32)], operands=[a, dh],
                  scratch=[pltpu.VMEM((1, CW), F32), pltpu.VMEM((1, CW), F32)], vmem=32,
                  sem=("parallel", "arbitrary"))[0]


def _rnn_gates_bwd(z, lam_s, hr, conv_w, conv_b, w_ra, w_ri, b_ra, b_ri, lam, dz, gsq):
    def body(xr_ref, ls_ref, hr_ref, cw_ref, cb_ref, wra_ref, wri_ref, bra_ref, bri_ref, lam_ref, dz_in, gsq_in,
             dx_ref, dw_ref, sums_ref):
        r = _rnn_recompute(xr_ref, cw_ref, cb_ref, wra_ref, wri_ref, bra_ref, bri_ref, lam_ref)
        valid, c, gr, gi, a, mult = r["valid"], r["c"], r["gr"], r["gi"], r["a"], r["mult"]
        du = ls_ref[...] * valid
        da = du * pltpu.roll(hr_ref[...], 1, 0)
        d_gi = du * mult * c
        dc = du * mult * gi
        dmult = du * gi * c
        dlog_a = da * a + jnp.where(r["first"], 0.0, -dmult * a * a / mult)
        d_gr = dlog_a * (LRU_C * r["ls"])
        dls = jnp.sum(dlog_a * (LRU_C * gr), axis=0, keepdims=True)
        dpre_r = d_gr * gr * (1.0 - gr)
        dpre_i = d_gi * gi * (1.0 - gi)
        pr = dpre_r.astype(_MXU)
        pi = dpre_i.astype(_MXU)
        dwra = lax.dot_general(r["cm"], pr, _TN, preferred_element_type=F32)
        dwri = lax.dot_general(r["cm"], pi, _TN, preferred_element_type=F32)
        for s in range(N_SH):
            dw_ref[s, 0:64, :] = dwra[64 * s:64 * (s + 1)]
            dw_ref[s, 64:128, :] = dwri[64 * s:64 * (s + 1)]
        dc = dc + lax.dot_general(pr, wra_ref[...].astype(_MXU), _NT, preferred_element_type=F32)
        dc = dc + lax.dot_general(pi, wri_ref[...].astype(_MXU), _NT, preferred_element_type=F32)
        cw = cw_ref[...]
        dx = cw[0:1, :] * dc
        for k in range(1, CONV_W):
            dx = dx + cw[k:k + 1, :] * pltpu.roll(dc, TP - k, 0)
        dx_ref[...] = (dx * valid).astype(_MXU)
        for k in range(CONV_W):
            sums_ref[k:k + 1, :] = jnp.sum(dc * r["shifted"][k], axis=0, keepdims=True)
        sums_ref[4:5, :] = jnp.sum(dc, axis=0, keepdims=True)
        sums_ref[5:6, :] = jnp.sum(dpre_r, axis=0, keepdims=True)
        sums_ref[6:7, :] = jnp.sum(dpre_i, axis=0, keepdims=True)
        sums_ref[7:8, :] = dls * _sigmoid(-r["lam"])

    s = _rnn_specs()
    return _pcall(
        body, name="rnn_gates_bwd", grid=(N_RB,),
        in_specs=[s["col"], s["col"], s["col"], s["cw"], s["vec"], s["wblk"], s["wblk"], s["vec"], s["vec"], s["vec"],
                  _ANY, _ANY],
        out_specs=[s["col"], pl.BlockSpec((N_SH, 128, RB), lambda n: (0, 3 * SQ_ROWS // 128, n)),
                   pl.BlockSpec((8, RB), lambda n: (0, n))],
        out_shape=[_sds((TP, D_IN), _MXU), _sds((N_SH, PACK_ROWS, D), F32), _sds((8, D), F32)],
        operands=[z, lam_s, hr, conv_w, conv_b, w_ra, w_ri, b_ra, b_ri, lam, dz, gsq], vmem=60, sem=("parallel",),
        aliases={10: 0, 11: 1})


def _rope_tables():
    half = HD // 2
    inv = ROPE_THETA ** (-jnp.arange(half, dtype=F32) / half)
    pos = (jnp.arange(TP) - PAD).astype(F32)
    ang = pos[:, None] * inv[None, :]
    return jnp.tile(jnp.cos(ang), (1, 4)), jnp.tile(jnp.sin(ang), (1, 4))


def _rope(x, cos_t, sin_t, sign):
    w = x.shape[1]
    lane = lax.broadcasted_iota(jnp.int32, x.shape, 1)
    first = (lane % HD) < (HD // 2)
    swapped = jnp.where(first, pltpu.roll(x, w - HD // 2, 1), pltpu.roll(x, HD // 2, 1))
    ct = jnp.tile(cos_t, (1, w // 128))
    st = jnp.tile(sin_t, (1, w // 128))
    return x * ct + swapped * jnp.where(first, -sign * st, sign * st)


def _rope_fwd(z, cos_t, sin_t):
    def body(q_ref, k_ref, v_ref, c_ref, s_ref, qo_ref, ko_ref, vo_ref):
        c = c_ref[...]
        s = s_ref[...]
        qo_ref[...] = _rope(q_ref[...], c, s, 1.0).astype(_MXU)
        ko_ref[...] = _rope(k_ref[...], c, s, 1.0).astype(_MXU)
        vo_ref[...] = v_ref[...].astype(_MXU)

    tab = pl.BlockSpec((BLK, 128), lambda i: (i, 0))
    kv = pl.BlockSpec((BLK, D_KV), lambda i: (i, 0))
    return _pcall(body, name="rope_fwd", grid=(NBLK,),
                  in_specs=[pl.BlockSpec((BLK, D), lambda i: (i, OFF_Q // D)),
                            pl.BlockSpec((BLK, D_KV), lambda i: (i, OFF_K // D_KV)),
                            pl.BlockSpec((BLK, D_KV), lambda i: (i, OFF_V // D_KV)), tab, tab],
                  out_specs=[pl.BlockSpec((BLK, D), lambda i: (i, 0)), kv, kv],
                  out_shape=[_sds((TP, D), _MXU), _sds((TP, D_KV), _MXU), _sds((TP, D_KV), _MXU)],
                  operands=[z, z, z, cos_t, sin_t], vmem=32, sem=("parallel",))


def _rope_bwd_k(dk, dv, cos_t, sin_t, dz):
    def body(dk_ref, dv_ref, c_ref, s_ref, dz_in, o_ref):
        o_ref[:, 0:D_KV] = _rope(dk_ref[...], c_ref[...], s_ref[...], -1.0).astype(_MXU)
        o_ref[:, D_KV:2 * D_KV] = dv_ref[...].astype(_MXU)

    tab = pl.BlockSpec((BLK, 128), lambda i: (i, 0))
    kv = pl.BlockSpec((BLK, D_KV), lambda i: (i, 0))
    return _pcall(body, name="rope_bwd_k", grid=(NBLK,), in_specs=[kv, kv, tab, tab, _ANY],
                  out_specs=[pl.BlockSpec((BLK, 2 * D_KV), lambda i: (i, OFF_K // (2 * D_KV)))],
                  out_shape=[_sds((TP, D_IN), _MXU)], operands=[dk, dv, cos_t, sin_t, dz], vmem=32, sem=("parallel",),
                  aliases={4: 0})[0]


def _attn_mask(i):
    ql = lax.broadcasted_iota(jnp.int32, (BLK, 3 * BLK), 0)
    kk = lax.broadcasted_iota(jnp.int32, (BLK, 3 * BLK), 1)
    kl = kk % BLK
    part = kk // BLK
    meta = (part == 0) & (kl >= PAD) & ((i >= 1) | (kl <= ql))
    prev = (part == 1) & (i >= 2) & (kl > ql)
    cur = (part == 2) & (i >= 1) & (kl <= ql)
    return meta | prev | cur


def _cat_kv(refs, g):
    return jnp.concatenate([r[:, HD * g:HD * (g + 1)] for r in refs], axis=0)


def _kv_specs():
    return [pl.BlockSpec((BLK, D_KV), lambda i: (0, 0)),
            pl.BlockSpec((BLK, D_KV), lambda i: (jnp.maximum(i - 1, 0), 0)),
            pl.BlockSpec((BLK, D_KV), lambda i: (i, 0))]


def _stack_heads(ref, g, width=HD):
    return jnp.concatenate([ref[:, width * (GRP * g + j):width * (GRP * g + j + 1)] for j in range(GRP)], axis=0)


def _attn_fwd(q, k, v, sinks, *, hook):
    def body(q_ref, k0_ref, kp_ref, kc_ref, v0_ref, vp_ref, vc_ref, sink_ref, o_ref, lse_ref, s_scr, p_scr):
        mask = _attn_mask(pl.program_id(0))
        for g in range(N_KV):
            s_scr[...] = lax.dot_general(_stack_heads(q_ref, g), _cat_kv((k0_ref, kp_ref, kc_ref), g), _NT,
                                         preferred_element_type=F32)
            for j in range(GRP):
                h = GRP * g + j
                rows = slice(BLK * j, BLK * (j + 1))
                sink = sink_ref[h]
                s = jnp.where(mask, s_scr[rows, :] * (HD ** -0.5), NEG_INF)
                mx = jnp.maximum(jnp.max(s, -1, keepdims=True), sink)
                p = jnp.exp(s - mx)
                den = jnp.sum(p, -1, keepdims=True) + jnp.exp(sink - mx)
                p_scr[rows, :] = (p * (1.0 / den)).astype(_MXU)
                lse_ref[:, h:h + 1] = mx + jnp.log(den)
            o8 = jnp.dot(p_scr[...], _cat_kv((v0_ref, vp_ref, vc_ref), g), preferred_element_type=F32)
            for j in range(GRP):
                h = GRP * g + j
                o_ref[:, HD * h:HD * (h + 1)] = o8[BLK * j:BLK * (j + 1)]

    return _pcall(body, name="attn_fwd", grid=(NBLK,),
                  in_specs=[pl.BlockSpec((BLK, D), lambda i: (i, 0))] + _kv_specs() + _kv_specs()
                           + [pl.BlockSpec(memory_space=pltpu.SMEM)],
                  out_specs=[pl.BlockSpec((BLK, D), lambda i: (i, 0)), pl.BlockSpec((BLK, N_Q), lambda i: (i, 0))],
                  out_shape=[_sds((TP, D), F32), _sds((TP, N_Q), F32)], operands=[q, k, k, k, v, v, v, sinks],
                  scratch=[pltpu.VMEM((GRP * BLK, 3 * BLK), F32), pltpu.VMEM((GRP * BLK, 3 * BLK), _MXU)],
                  vmem=40, sem=("parallel",), hook=hook)


def _attn_bwd(q, k, v, sinks, do, o, lse, cos_t, sin_t, dz, *, hook):
    def body(q_ref, k0_ref, kp_ref, kc_ref, v0_ref, vp_ref, vc_ref, sink_ref, do_ref, o_ref, lse_ref, c_ref, s_ref, dz_in,
             dq_ref, dk_ref, dv_ref, dsink_ref, dqrot_ref, s_scr, dp_scr, p_scr, ds_scr):
        i = pl.program_id(0)

        @pl.when(i == 0)
        def _():
            dk_ref[...] = jnp.zeros_like(dk_ref)
            dv_ref[...] = jnp.zeros_like(dv_ref)
            dsink_ref[...] = jnp.zeros_like(dsink_ref)

        mask = _attn_mask(i)
        row_starts = (0, pl.multiple_of(jnp.maximum(i - 1, 0) * BLK, BLK), pl.multiple_of(i * BLK, BLK))
        scale = HD ** -0.5
        for g in range(N_KV):
            gs = slice(HD * g, HD * (g + 1))
            q8 = _stack_heads(q_ref, g)
            dom = _stack_heads(do_ref, g).astype(_MXU)
            kcat = _cat_kv((k0_ref, kp_ref, kc_ref), g)
            s_scr[...] = lax.dot_general(q8, kcat, _NT, preferred_element_type=F32)
            dp_scr[...] = lax.dot_general(dom, _cat_kv((v0_ref, vp_ref, vc_ref), g), _NT, preferred_element_type=F32)
            for j in range(GRP):
                h = GRP * g + j
                hs = slice(HD * h, HD * (h + 1))
                rows = slice(BLK * j, BLK * (j + 1))
                lse_h = lse_ref[:, h:h + 1]
                delta = jnp.sum(do_ref[:, hs] * o_ref[:, hs], axis=-1, keepdims=True)
                dsink_ref[0:1, h:h + 1] += jnp.sum(-jnp.exp(sink_ref[h] - lse_h) * delta, axis=0, keepdims=True)
                p = jnp.exp(jnp.where(mask, s_scr[rows, :] * scale, NEG_INF) - lse_h)
                p_scr[rows, :] = p.astype(_MXU)
                ds_scr[rows, :] = (p * (dp_scr[rows, :] - delta) * scale).astype(_MXU)
            ds = ds_scr[...]
            dq8 = jnp.dot(ds, kcat, preferred_element_type=F32)
            dkcat = lax.dot_general(ds, q8, _TN, preferred_element_type=F32)
            dvcat = lax.dot_general(p_scr[...], dom, _TN, preferred_element_type=F32)
            for part in range(3):
                rows = pl.ds(row_starts[part], BLK)
                dk_ref[rows, gs] += dkcat[BLK * part:BLK * (part + 1)]
                dv_ref[rows, gs] += dvcat[BLK * part:BLK * (part + 1)]
            for j in range(GRP):
                h = GRP * g + j
                dqrot_ref[:, HD * h:HD * (h + 1)] = dq8[BLK * j:BLK * (j + 1)]
        dq_ref[...] = _rope(dqrot_ref[...], c_ref[...], s_ref[...], -1.0).astype(_MXU)

    row = pl.BlockSpec((BLK, D), lambda i: (i, 0))
    tab = pl.BlockSpec((BLK, 128), lambda i: (i, 0))
    full_kv = pl.BlockSpec((TP, D_KV), lambda i: (0, 0))
    return _pcall(
        body, name="attn_bwd", grid=(NBLK,),
        in_specs=[row] + _kv_specs() + _kv_specs() + [pl.BlockSpec(memory_space=pltpu.SMEM), row, row,
                  pl.BlockSpec((BLK, N_Q), lambda i: (i, 0)), tab, tab, _ANY],
        out_specs=[pl.BlockSpec((BLK, D), lambda i: (i, OFF_Q // D)), full_kv, full_kv,
                   pl.BlockSpec((8, 128), lambda i: (0, 0))],
        out_shape=[_sds((TP, D_IN), _MXU), _sds((TP, D_KV), F32), _sds((TP, D_KV), F32), _sds((8, 128), F32)],
        operands=[q, k, k, k, v, v, v, sinks, do, o, lse, cos_t, sin_t, dz],
        scratch=[pltpu.VMEM((BLK, D), F32), pltpu.VMEM((GRP * BLK, 3 * BLK), F32), pltpu.VMEM((GRP * BLK, 3 * BLK), F32),
                 pltpu.VMEM((GRP * BLK, 3 * BLK), _MXU), pltpu.VMEM((GRP * BLK, 3 * BLK), _MXU)],
        vmem=48, sem=("arbitrary",), aliases={13: 0}, hook=hook)


def _cast_into_slot(w32, pos, *, name, tr):
    r, cc = w32.shape

    def body(pos_ref, w_ref, o_ref):
        o_ref[...] = w_ref[...].astype(BF16)

    return _pcall(body, name=name, grid=(r // tr,), in_specs=[pl.BlockSpec((tr, cc), lambda i, p: (i, 0))],
                  out_specs=[pl.BlockSpec((None, tr, cc), lambda i, p: (p[1], i, 0))],
                  out_shape=[_sds((N_SH, r, cc), BF16)], operands=[w32], vmem=32, sem=("parallel",), prefetch=(pos,))[0]


def _pair_add(g, got, pos, *, name, tr, g_has_both_halves):
    n, h, cc = got.shape
    nt = h // tr

    def body(pos_ref, g_ref, r_ref, own_ref, s16_ref):
        s = g_ref[...] + r_ref[...]
        s16_ref[...] = s.astype(BF16)

        @pl.when(pl.program_id(1) == pos_ref[1])
        def _():
            own_ref[...] = s

    g_index = (lambda i, s, p: (s, p[0] * nt + i, 0)) if g_has_both_halves else (lambda i, s, p: (s, i, 0))
    return _pcall(body, name=name, grid=(nt, n),
                  in_specs=[pl.BlockSpec((None, tr, cc), g_index), pl.BlockSpec((None, tr, cc), lambda i, s, p: (s, i, 0))],
                  out_specs=[pl.BlockSpec((tr, cc), lambda i, s, p: (i, 0)),
                             pl.BlockSpec((None, tr, cc), lambda i, s, p: (s, i, 0))],
                  out_shape=[_sds((h, cc), F32), _sds((n, h, cc), BF16)], operands=[g, got], vmem=40,
                  sem=("parallel", "arbitrary"), prefetch=(pos,))


def _sum_chips(own, got, pos, *, name, tr):
    h, cc = own.shape
    nt = h // tr

    def body(pos_ref, o_ref, r_ref, out_ref):
        acc = o_ref[...]
        for k in range(3):
            acc = acc + r_ref[k].astype(F32)
        out_ref[...] = acc

    return _pcall(body, name=name, grid=(nt,),
                  in_specs=[pl.BlockSpec((tr, cc), lambda i, p: (i, 0)), pl.BlockSpec((3, tr, cc), lambda i, p: (0, i, 0))],
                  out_specs=[pl.BlockSpec((tr, cc), lambda i, p: (p[0] * nt + i, 0))],
                  out_shape=[_sds((2 * h, cc), F32)], operands=[own, got], vmem=40, sem=("parallel",), prefetch=(pos,))[0]


def _adamw(w, g, m, v, *, name, tr, g_row0=0):
    r, cc = w.shape
    g_blk0 = g_row0 // tr

    def body(w_ref, g_ref, m_ref, v_ref, go_ref, d_ref, mo_ref, vo_ref):
        gg = g_ref[...]
        go_ref[...] = gg
        m_new = ADAM_B1 * m_ref[...] + (1.0 - ADAM_B1) * gg
        v_new = ADAM_B2 * v_ref[...] + (1.0 - ADAM_B2) * (gg * gg)
        m_hat = m_new / (1.0 - ADAM_B1 ** ADAM_STEP)
        v_hat = v_new / (1.0 - ADAM_B2 ** ADAM_STEP)
        d_ref[...] = -ADAM_LR * (m_hat / (jnp.sqrt(v_hat) + ADAM_EPS) + ADAM_WD * w_ref[...])
        mo_ref[...] = m_new
        vo_ref[...] = v_new

    blk = pl.BlockSpec((tr, cc), lambda i: (i, 0))
    gblk = pl.BlockSpec((tr, cc), lambda i: (g_blk0 + i, 0))
    return _pcall(body, name=name, grid=(r // tr,), in_specs=[blk, gblk, blk, blk], out_specs=[blk] * 4,
                  out_shape=[_sds((r, cc), F32)] * 4, operands=[w, g, m, v], vmem=48, sem=("parallel",))


_SMALL_ROWS = 40
_B_IN_ROWS = 7


def _row_pad(v, rows):
    flat = v.reshape(-1)
    return jnp.pad(flat, (0, rows * D - flat.shape[0])).reshape(rows, D)


def _pack_ra(w):
    return w.transpose(1, 0, 2).reshape(64, D)


def _unpack_ra(p, like):
    return p.reshape(64, N_RB, RB).transpose(1, 0, 2).reshape(like.shape)


def _gate_full(g4):
    return g4.reshape(N_SH, 64, N_RB, RB).transpose(2, 0, 1, 3).reshape(N_RB, RB, RB)


def kernel(x, meta_tokens, ln_emb_g, ln_emb_b, w_in, b_in, conv_w, conv_b, w_ra, b_ra, w_ri, b_ri, lru_lambda, sinks, w_rnn_out, w_attn_out, w_o, b_o, ln_g, ln_b, loss_target, m_meta_tokens, m_ln_emb_g, m_ln_emb_b, m_w_in, m_b_in, m_conv_w, m_conv_b, m_w_ra, m_b_ra, m_w_ri, m_b_ri, m_lru_lambda, m_sinks, m_w_rnn_out, m_w_attn_out, m_w_o, m_b_o, m_ln_g, m_ln_b, v_meta_tokens, v_ln_emb_g, v_ln_emb_b, v_w_in, v_b_in, v_conv_w, v_conv_b, v_w_ra, v_b_ra, v_w_ri, v_b_ri, v_lru_lambda, v_sinks, v_w_rnn_out, v_w_attn_out, v_w_o, v_b_o, v_ln_g, v_ln_b):
    xi, yi, ci = _my_pos()
    shard = 2 * xi + yi
    pos = jnp.stack([ci, shard, 1 - ci]).astype(jnp.int32)
    cos_t, sin_t = _rope_tables()
    zero_bias = jnp.zeros((1, D), F32)
    ln_emb_g2, ln_emb_b2 = ln_emb_g[None], ln_emb_b[None]

    small = jnp.concatenate([conv_w[0], meta_tokens, jnp.zeros((4, 512), F32)], axis=0)
    small4 = _gather_small(small)
    conv_w_full = small4[:, 0:4].transpose(1, 0, 2).reshape(CONV_W, D)
    meta_full = small4[:, 4:20].transpose(1, 0, 2).reshape(N_META, D)
    w_own = _cast_into_slot(w_in[0], pos, name="cast_w_in", tr=256)
    wa_own = _cast_into_slot(jnp.concatenate([w_attn_out[0], w_o[0], _pack_ra(w_ra[0]), _pack_ra(w_ri[0])], axis=0),
                             pos, name="cast_w_a", tr=288)
    wb_own = _cast_into_slot(w_rnn_out[0], pos, name="cast_w_b", tr=256)

    h32, h16, h16_t = _ln_emb_fwd(x[0], meta_full, ln_emb_g2, ln_emb_b2)
    order = jnp.stack([shard, shard ^ 2, shard ^ 1, shard ^ 3]).astype(jnp.int32)
    z, w_in4 = _mm_z_gather(h16, w_own, b_in, order)
    q, k, v = _rope_fwd(z, cos_t, sin_t)
    o, lse, wa4 = _attn_fwd(q, k, v, sinks[0], hook=_hook_gather(wa_own, 0.6))
    w_ra_full = _gate_full(wa4[:, 2 * SQ_ROWS:2 * SQ_ROWS + 64])
    w_ri_full = _gate_full(wa4[:, 2 * SQ_ROWS + 64:2 * SQ_ROWS + 128])
    a_dec, u_in = _rnn_gates_fwd(z, conv_w_full, conv_b, w_ra_full, w_ri_full, b_ra, b_ri, lru_lambda)
    hr, wb4 = _scan_fwd(a_dec, u_in, hook=_hook_gather(wb_own, 0.6))
    sq_w = {0: (wb4, 0), 1: (wa4, 0), 2: (wa4, 1)}

    def sq_nn(a, kk, bias, name, out_cols, out_index, carried=None):
        wp, blk = sq_w[kk]
        return _mm_nn(a, wp, bias, name=name, grid=(2, D // CW), tm=HALF_TP, tn=CW, k=D, a_index=lambda i, j: (i, 0),
                      w_block=(N_SH, SQ_ROWS, CW), w_index=lambda i, j: (0, blk, j), out_cols=out_cols,
                      out_index=out_index, carried=carried)[0]

    def sq_nt(a, a_blk, kk, name, hook=None):
        wp, blk = sq_w[kk]
        return _mm_nt(a, wp, name=name, grid=(2, N_SH, 1), tm=HALF_TP, tn=SQ_ROWS, tk=D,
                      a_index=lambda i, j, q: (i, a_blk), w_block=(None, SQ_ROWS, D),
                      w_index=lambda i, j, q: (j, blk, 0), out_cols=D, hook=hook)

    ya_in, ya_in_t = _mul_silu_fwd(hr, z, OFF_GR, name="gate_a_fwd")
    y2 = sq_nn(ya_in, 0, zero_bias, "mm_ya", 2 * D, lambda i, j: (i, j))
    yb_in, yb_in_t = _mul_silu_fwd(o, z, OFF_GA, name="gate_b_fwd")
    y2 = sq_nn(yb_in, 1, zero_bias, "mm_yb", 2 * D, lambda i, j: (i, D // CW + j), carried=y2)
    mixed, mixed_t = _merge_fwd(y2, z)
    out = sq_nn(mixed, 2, b_o, "mm_out", D, lambda i, j: (i, j))
    dr, sums_o = _ln_out_loss(h32, out, loss_target[0], ln_g, ln_b)

    def sq_tn(at, b, b_blk0, kk, name, carried=None):
        return _mm_tn(at, b, name=name, grid=(N_SH, D // CW), tm=SQ_ROWS, tn=CW, a_index=lambda i, j: (i, 0),
                      b_index=lambda i, j: (0, b_blk0 + j), out_shape=(N_SH, PACK_ROWS, D),
                      out_block=(None, SQ_ROWS, CW), out_index=lambda i, j: (i, kk, j), carried=carried)[0]

    gsq = sq_tn(mixed_t, dr, 0, 2, "mm_dwo")
    dmix = sq_nt(dr, 0, 2, "mm_dmix")[0]
    dy2, dz = _merge_bwd(dmix, y2, z)
    gsq = sq_tn(ya_in_t, dy2, 0, 0, "mm_dwrnn", carried=gsq)
    gsq = sq_tn(yb_in_t, dy2, D // CW, 1, "mm_dwattn", carried=gsq)
    dya_in = sq_nt(dy2, 0, 0, "mm_dyain")[0]
    dhr, dz = _mul_silu_bwd(dya_in, hr, z, OFF_GR, dz, name="gate_a_bwd")
    lam_s = _scan_bwd(a_dec, dhr)
    dz, gsq, sums_r = _rnn_gates_bwd(z, lam_s, hr, conv_w_full, conv_b, w_ra_full, w_ri_full, b_ra, b_ri, lru_lambda, dz, gsq)
    dyb_in = sq_nt(dy2, 1, 1, "mm_dybin")[0]
    do, dz, gsq, got_sq = _mul_silu_bwd(dyb_in, o, z, OFF_GA, dz, name="gate_b_bwd", hook=_hook_pair(gsq, True))
    own_sq, s16_sq = _pair_add(gsq, got_sq, pos, name="red_w_sq_add", tr=208, g_has_both_halves=True)
    dz, dk_rot, dv32, dsink, s16_sq, oth_sq = _attn_bwd(q, k, v, sinks[0], do, o, lse, cos_t, sin_t, dz,
                                                        hook=_hook_scatter(s16_sq))
    dz = _rope_bwd_k(dk_rot, dv32, cos_t, sin_t, dz)
    red_sq = _sum_chips(own_sq, oth_sq, pos, name="red_w_sq_sum", tr=208)
    g_b_in, g_sq = _colsum(dz, name="colsum_dz", tn=TN_IN, hook=_hook_halves(red_sq))

    def dwin(half_idx, name, hook=None):
        return _mm_tn(h16_t, dz, name=name, grid=(1, N_SH * PER_IN), tm=D // 2, tn=TN_IN,
                      a_index=lambda i, j, p: (p[half_idx], 0), b_index=lambda i, j, p: (0, j),
                      out_shape=(N_SH, D // 2, W_IN_COLS), out_block=(None, D // 2, TN_IN),
                      out_index=lambda i, j, p: (j // PER_IN, 0, j % PER_IN), prefetch=(pos,), hook=hook)

    gin_sib = dwin(2, "mm_dwin_sib")[0]
    gin_own, gin_sib, got_in = dwin(0, "mm_dwin_own", hook=_hook_pair(gin_sib, False))
    own_in, s16_in = _pair_add(gin_own, got_in, pos, name="red_w_in_add", tr=128, g_has_both_halves=False)
    dhz, s16_in, oth_in = _mm_nt(dz, w_in4, name="mm_dhz", grid=(2, 2, N_SH), tm=HALF_TP, tn=D // 2, tk=W_IN_COLS,
                                 a_index=lambda i, j, q: (i, q), w_block=(None, D // 2, W_IN_COLS),
                                 w_index=lambda i, j, q: (q, j, 0), out_cols=D, hook=_hook_scatter(s16_in))
    red_in = _sum_chips(own_in, oth_in, pos, name="red_w_in_sum", tr=128)
    g_x, g_meta_local, sums_e, g_in = _ln_emb_bwd(x[0], meta_full, ln_emb_g2, dr, dhz, hook=_hook_halves(red_in))

    big = {"w_in": [t.reshape(w_in.shape) for t in
                    _adamw(w_in[0], g_in, m_w_in[0], v_w_in[0], name="adamw_w_in", tr=128)]}
    for kk, (n, w_, m_, v_) in enumerate([("w_rnn_out", w_rnn_out, m_w_rnn_out, v_w_rnn_out),
                                          ("w_attn_out", w_attn_out, m_w_attn_out, v_w_attn_out), ("w_o", w_o, m_w_o, v_w_o)]):
        big[n] = [t.reshape(w_.shape) for t in
                  _adamw(w_[0], g_sq, m_[0], v_[0], name="adamw_" + n, tr=256, g_row0=SQ_ROWS * kk)]
    for kk, (n, w_, m_, v_) in enumerate([("w_ra", w_ra, m_w_ra, v_w_ra), ("w_ri", w_ri, m_w_ri, v_w_ri)]):
        big[n] = [_unpack_ra(t, w_) for t in
                  _adamw(_pack_ra(w_[0]), g_sq, _pack_ra(m_[0]), _pack_ra(v_[0]), name="adamw_" + n, tr=64,
                         g_row0=3 * SQ_ROWS + 64 * kk)]

    spack = jnp.concatenate([
        sums_e[0:1], sums_e[1:2], _row_pad(g_b_in, _B_IN_ROWS), sums_r[0:4], sums_r[4:5], sums_r[5:6], sums_r[6:7],
        sums_r[7:8], _row_pad(dsink[0:1, 0:N_Q], 1), sums_o[2:3], sums_o[0:1], sums_o[1:2], g_meta_local, sums_o[3:4],
        jnp.zeros((_SMALL_ROWS - 38, D), F32)], axis=0)
    sred = _allreduce_small(spack)
    loss = sred[37, 0]
    col0 = shard * 512
    g_conv_w = lax.dynamic_slice(sred[9:13], (0, col0), (CONV_W, 512))
    g_meta = lax.dynamic_slice(sred[21:37], (0, col0), (N_META, 512))
    small_g = {"ln_emb_g": sred[0:1], "ln_emb_b": sred[1:2], "b_in": sred[2:9], "conv_w": g_conv_w.reshape(1, D),
               "conv_b": sred[13:14], "b_ra": sred[14:15], "b_ri": sred[15:16], "lru_lambda": sred[16:17],
               "sinks": sred[17:18], "b_o": sred[18:19], "ln_g": sred[19:20], "ln_b": sred[20:21],
               "meta_tokens": g_meta.reshape(4, D)}
    small_names = list(small_g)

    def small_pack(vals):
        rows = []
        for n in small_names:
            a = vals[n]
            if n == "b_in":
                a = _row_pad(a, _B_IN_ROWS)
            elif n == "sinks":
                a = _row_pad(a, 1)
            else:
                a = a.reshape(-1, D)
            rows.append(a)
        return jnp.concatenate(rows + [jnp.zeros((24 - 22, D), F32)], axis=0)

    w_small = dict(ln_emb_g=ln_emb_g, ln_emb_b=ln_emb_b, b_in=b_in, conv_w=conv_w, conv_b=conv_b, b_ra=b_ra, b_ri=b_ri,
                   lru_lambda=lru_lambda, sinks=sinks, b_o=b_o, ln_g=ln_g, ln_b=ln_b, meta_tokens=meta_tokens)
    m_small = dict(ln_emb_g=m_ln_emb_g, ln_emb_b=m_ln_emb_b, b_in=m_b_in, conv_w=m_conv_w, conv_b=m_conv_b, b_ra=m_b_ra,
                   b_ri=m_b_ri, lru_lambda=m_lru_lambda, sinks=m_sinks, b_o=m_b_o, ln_g=m_ln_g, ln_b=m_ln_b,
                   meta_tokens=m_meta_tokens)
    v_small = dict(ln_emb_g=v_ln_emb_g, ln_emb_b=v_ln_emb_b, b_in=v_b_in, conv_w=v_conv_w, conv_b=v_conv_b, b_ra=v_b_ra,
                   b_ri=v_b_ri, lru_lambda=v_lru_lambda, sinks=v_sinks, b_o=v_b_o, ln_g=v_ln_g, ln_b=v_ln_b,
                   meta_tokens=v_meta_tokens)
    g_small_pack = jnp.concatenate([small_g[n] for n in small_names] + [jnp.zeros((2, D), F32)], axis=0)
    small_res = _adamw(small_pack(w_small), g_small_pack, small_pack(m_small), small_pack(v_small),
                       name="adamw_small", tr=24)

    small_rows = {}
    r0 = 0
    for n in small_names:
        nrows = small_g[n].shape[0]
        small_rows[n] = (r0, nrows)
        r0 += nrows

    def small_out(packed, n, like):
        a, nrows = small_rows[n]
        flat = packed[a:a + nrows].reshape(-1)
        return flat[:like.size].reshape(like.shape)

    weights = dict(meta_tokens=meta_tokens, ln_emb_g=ln_emb_g, ln_emb_b=ln_emb_b, w_in=w_in, b_in=b_in, conv_w=conv_w,
                   conv_b=conv_b, w_ra=w_ra, b_ra=b_ra, w_ri=w_ri, b_ri=b_ri, lru_lambda=lru_lambda, sinks=sinks,
                   w_rnn_out=w_rnn_out, w_attn_out=w_attn_out, w_o=w_o, b_o=b_o, ln_g=ln_g, ln_b=ln_b)

    def outputs(which):
        return [big[n][which] if n in big else small_out(small_res[which], n, like) for n, like in weights.items()]

    return (loss, g_x[None], *outputs(0), *outputs(1), *outputs(2), *outputs(3))
```

```python
import jax
import jax.numpy as jnp
from jax import lax
from jax.experimental import pallas as pl
from jax.experimental.pallas import tpu as pltpu

F32 = jnp.float32
BF16 = jnp.bfloat16
_MXU = jnp.bfloat16

D = 2048
SEQ = 2048
N_META = 16
BLK = 128
PAD = BLK - N_META
TP = PAD + N_META + SEQ
NBLK = TP // BLK
HALF_TP = TP // 2
N_RB = 8
RB = 256
CONV_W = 4
LRU_C = 8.0
HD = 64
N_Q = 32
N_KV = 4
GRP = 8
D_KV = 256
NEG_INF = -1e30
LN_EPS = 1e-5
ALPHA = 2.0 ** 0.25
ROPE_THETA = 10000.0
OFF_GR, OFF_Q, OFF_K, OFF_V, OFF_GA, OFF_G = 2048, 4096, 6144, 6400, 6656, 8704
D_IN = 12800
N_SH = 4
W_IN_COLS = D_IN // N_SH
TN_IN = 640
PER_IN = W_IN_COLS // TN_IN
CW = 512
RT = TP // 4
SQ_ROWS = 512
PACK_ROWS = 3 * SQ_ROWS + 128

ADAM_LR = 0.001
ADAM_B1 = 0.9
ADAM_B2 = 0.999
ADAM_EPS = 1e-08
ADAM_WD = 0.01
ADAM_STEP = 10

MESH = pl.DeviceIdType.MESH
_MIB = 1024 * 1024
_ANY = pl.BlockSpec(memory_space=pl.ANY)
_NT = (((1,), (1,)), ((), ()))
_TN = (((0,), (0,)), ((), ()))


def _sds(shape, dtype):
    return jax.ShapeDtypeStruct(shape, dtype)


def _sigmoid(x):
    return 1.0 / (1.0 + jnp.exp(-x))


def _my_pos():
    return lax.axis_index("x"), lax.axis_index("y"), lax.axis_index("c")


def _other_chips(x, y):
    return [(1 - x, y), (x, 1 - y), (1 - x, 1 - y)]


def _remote(src, dst, send_sems, recv_sems, k, dev):
    return pltpu.make_async_remote_copy(src_ref=src, dst_ref=dst, send_sem=send_sems.at[k], recv_sem=recv_sems.at[k],
                                        device_id=dev, device_id_type=MESH)


class _Hook:
    def __init__(self, carried, landing, n_sems, start, finish, mid=None, mid_frac=0.5):
        self.carried, self.landing, self.n_sems, self.start, self.finish = list(carried), list(landing), n_sems, start, finish
        self.mid, self.mid_frac = mid, mid_frac


def _hook_gather(buf, mid_frac):
    half = buf.shape[1] // 2
    quarter = half // 2

    def geom(o, ss, rs):
        x, y, c = _my_pos()
        xn, yn, dg = _other_chips(x, y)
        slot = lambda p: 2 * p[0] + p[1]
        mine_rows = pl.ds(pl.multiple_of(c * half, 16), half)
        sib_rows = pl.ds(pl.multiple_of((1 - c) * half, 16), half)
        q_rows = lambda r: pl.ds(pl.multiple_of(c * half + r * quarter, 16), quarter)

        def cp(k, s, rows, dev):
            part = o.at[s, rows]
            return _remote(part, part, ss, rs, k, dev)

        return dict(
            direct=lambda k, s: cp(k, s, mine_rows, ((xn, yn)[k][0], (xn, yn)[k][1], c)),
            relay=lambda r, s: cp(2 + r, s, q_rows(r), ((yn, xn)[r][0], (yn, xn)[r][1], c)),
            sibling=lambda k, s, mine: cp(4 + k, s, mine_rows if mine else sib_rows, (x, y, 1 - c)),
            me=slot((x, y)), slots=(slot(xn), slot(yn), slot(dg)))

    def start(car, land, ss, rs):
        g = geom(car[0], ss, rs)
        g["direct"](0, g["me"]).start()
        g["direct"](1, g["me"]).start()

    def mid(car, land, ss, rs):
        g = geom(car[0], ss, rs)
        for k in range(2):
            g["direct"](k, g["slots"][k]).wait_recv()
            g["relay"](k, g["slots"][k]).start()
            g["sibling"](k, g["slots"][k], True).start()

    def finish(car, land, ss, rs):
        g = geom(car[0], ss, rs)
        dslot = g["slots"][2]
        g["relay"](0, dslot).wait_recv()
        g["relay"](1, dslot).wait_recv()
        g["sibling"](2, dslot, True).start()
        for k in range(3):
            g["sibling"](k, g["slots"][k], False).wait_recv()
        for k in range(2):
            g["direct"](k, g["me"]).wait_send()
            g["relay"](k, g["slots"][k]).wait_send()
        for k in range(3):
            g["sibling"](k, g["slots"][k], True).wait_send()

    return _Hook([buf], [], 7, start, finish, mid=mid, mid_frac=mid_frac)


def _hook_pair(g, half_rows):
    n, r, cc = g.shape
    h = r // 2 if half_rows else r

    def plan(car, land, ss, rs):
        x, y, c = _my_pos()
        cps = []
        for s in range(n):
            src = car[0].at[s, pl.ds(pl.multiple_of((1 - c) * h, 8), h)] if half_rows else car[0].at[s]
            cps.append(_remote(src, land[0].at[s], ss, rs, s, (x, y, 1 - c)))
        return cps

    def start(car, land, ss, rs):
        for cp in plan(car, land, ss, rs):
            cp.start()

    def finish(car, land, ss, rs):
        cps = plan(car, land, ss, rs)
        for cp in cps:
            cp.wait_recv()
        for cp in cps:
            cp.wait_send()

    return _Hook([g], [_sds((n, h, cc), g.dtype)], n, start, finish)


def _hook_scatter_direct(s16):
    _, h, cc = s16.shape
    q = h // 2

    def plan(car, land, ss, rs):
        x, y, c = _my_pos()
        xn, yn, dg = _other_chips(x, y)
        s, (final, raw) = car[0], land
        slot = lambda p: 2 * p[0] + p[1]
        q0, q1 = pl.ds(0, q), pl.ds(q, q)
        return [_remote(s.at[slot(xn), q0], final.at[0, q0], ss, rs, 0, (xn[0], xn[1], c)),
                _remote(s.at[slot(yn), q1], final.at[1, q1], ss, rs, 1, (yn[0], yn[1], c)),
                _remote(s.at[slot(dg), q0], raw.at[1], ss, rs, 2, (xn[0], xn[1], c)),
                _remote(s.at[slot(dg), q1], raw.at[0], ss, rs, 3, (yn[0], yn[1], c))]

    def start(car, land, ss, rs):
        for cp in plan(car, land, ss, rs):
            cp.start()

    def finish(car, land, ss, rs):
        cps = plan(car, land, ss, rs)
        for cp in cps:
            cp.wait_recv()
        for cp in cps:
            cp.wait_send()

    return _Hook([s16], [_sds((2, h, cc), s16.dtype), _sds((2, q, cc), s16.dtype)], 4, start, finish)


def _hook_scatter_relay(comb, final):
    _, q, _ = comb.shape

    def plan(car, ss, rs):
        x, y, c = _my_pos()
        xn, yn, _ = _other_chips(x, y)
        cb, final_ref = car
        return [_remote(cb.at[0], final_ref.at[0, pl.ds(q, q)], ss, rs, 0, (xn[0], xn[1], c)),
                _remote(cb.at[1], final_ref.at[1, pl.ds(0, q)], ss, rs, 1, (yn[0], yn[1], c))]

    def start(car, land, ss, rs):
        for cp in plan(car, ss, rs):
            cp.start()

    def finish(car, land, ss, rs):
        cps = plan(car, ss, rs)
        for cp in cps:
            cp.wait_recv()
        for cp in cps:
            cp.wait_send()

    return _Hook([comb, final], [], 2, start, finish)


def _hook_halves(full):
    h = full.shape[0] // 2

    def half_copy(car, ss, rs, which):
        x, y, c = _my_pos()
        rows = car[0].at[pl.ds(pl.multiple_of((c + which - 2 * c * which) * h, 8), h)]
        return _remote(rows, rows, ss, rs, 0, (x, y, 1 - c))

    def start(car, land, ss, rs):
        half_copy(car, ss, rs, 0).start()

    def finish(car, land, ss, rs):
        half_copy(car, ss, rs, 1).wait_recv()
        half_copy(car, ss, rs, 0).wait_send()

    return _Hook([full], [], 1, start, finish)


def _pcall(body, *, name, grid, in_specs, out_specs, out_shape, operands, scratch=(), vmem=48, sem=None,
           prefetch=(), aliases=None, hook=None):
    n_pre, n_in, n_out, n_scr = len(prefetch), len(in_specs), len(out_specs), len(scratch)
    in_specs, out_specs, out_shape, scratch = list(in_specs), list(out_specs), list(out_shape), list(scratch)
    io_alias = {n_pre + a: b for a, b in (aliases or {}).items()}
    operands = list(operands)
    kernel_body = body
    if hook is not None:
        n_car, n_land = len(hook.carried), len(hook.landing)
        for t, arr in enumerate(hook.carried):
            io_alias[n_pre + n_in + t] = n_out + t
        in_specs += [_ANY] * n_car
        out_specs += [_ANY] * (n_car + n_land)
        out_shape += [_sds(a.shape, a.dtype) for a in hook.carried] + hook.landing
        scratch += [pltpu.SemaphoreType.DMA((hook.n_sems,)), pltpu.SemaphoreType.DMA((hook.n_sems,))]
        operands += hook.carried
        sem = ("arbitrary",) * len(grid)

        def kernel_body(*refs):
            pre, rest = refs[:n_pre], refs[n_pre:]
            ins = rest[:n_in]
            outs = rest[n_in + n_car:n_in + n_car + n_out]
            car = rest[n_in + n_car + n_out:n_in + 2 * n_car + n_out]
            land = rest[n_in + 2 * n_car + n_out:n_in + 2 * n_car + n_out + n_land]
            scr = rest[n_in + 2 * n_car + n_out + n_land:]
            send_sems, recv_sems = scr[n_scr], scr[n_scr + 1]
            first = pl.program_id(0) == 0
            last = pl.program_id(0) == grid[0] - 1
            for d in range(1, len(grid)):
                first = first & (pl.program_id(d) == 0)
                last = last & (pl.program_id(d) == grid[d] - 1)

            @pl.when(first)
            def _():
                hook.start(car, land, send_sems, recv_sems)

            if hook.mid is not None:
                step = pl.program_id(0)
                total = grid[0]
                for d in range(1, len(grid)):
                    step = step * grid[d] + pl.program_id(d)
                    total *= grid[d]

                @pl.when(step == int(total * hook.mid_frac))
                def _():
                    hook.mid(car, land, send_sems, recv_sems)

            body(*pre, *ins, *outs, *scr[:n_scr])

            @pl.when(last)
            def _():
                hook.finish(car, land, send_sems, recv_sems)

    params = pltpu.CompilerParams(vmem_limit_bytes=vmem * _MIB, dimension_semantics=sem,
                                  has_side_effects=hook is not None)
    if n_pre:
        call = pl.pallas_call(
            kernel_body, name=name, out_shape=out_shape, input_output_aliases=io_alias, compiler_params=params,
            grid_spec=pltpu.PrefetchScalarGridSpec(num_scalar_prefetch=n_pre, grid=grid, in_specs=in_specs,
                                                   out_specs=out_specs, scratch_shapes=scratch))
    else:
        call = pl.pallas_call(kernel_body, name=name, grid=grid, in_specs=in_specs, out_specs=out_specs,
                              out_shape=out_shape, scratch_shapes=scratch, input_output_aliases=io_alias,
                              compiler_params=params)
    return call(*prefetch, *operands)


def _gather_small(blk):
    r, cc = blk.shape

    def body(x_ref, o_ref, send_sems, recv_sems):
        x, y, c = _my_pos()
        me = 2 * x + y
        o_ref[me] = x_ref[...]
        sends = [_remote(x_ref, o_ref.at[me], send_sems, recv_sems, k, (px, py, c))
                 for k, (px, py) in enumerate(_other_chips(x, y))]
        for cp in sends:
            cp.start()
        for k, (px, py) in enumerate(_other_chips(x, y)):
            _remote(x_ref, o_ref.at[2 * px + py], send_sems, recv_sems, k, (px, py, c)).wait_recv()
        for cp in sends:
            cp.wait_send()

    vm = pl.BlockSpec(memory_space=pltpu.VMEM)
    return pl.pallas_call(
        body, name="gather_small", in_specs=[vm], out_specs=vm, out_shape=_sds((N_SH, r, cc), blk.dtype),
        scratch_shapes=[pltpu.SemaphoreType.DMA((3,)), pltpu.SemaphoreType.DMA((3,))],
        compiler_params=pltpu.CompilerParams(has_side_effects=True),
    )(blk)


N_DEV = 8


def _allreduce_small(pack, halves):
    r, cc = pack.shape
    ride = _hook_halves(halves)

    def body(x_ref, h_in_ref, o_ref, h_ref, buf_ref, send_sems, recv_sems, h_ss, h_rs):
        del h_in_ref
        ride.start([h_ref], [], h_ss, h_rs)
        x, y, c = _my_pos()
        me = 4 * x + 2 * y + c
        buf_ref[me] = x_ref[...]
        cps = []
        for k in range(1, N_DEV):
            peer = (x ^ ((k >> 2) & 1), y ^ ((k >> 1) & 1), c ^ (k & 1))
            cps.append(_remote(x_ref, buf_ref.at[me], send_sems, recv_sems, k - 1, peer))
        for cp in cps:
            cp.start()
        for k in range(1, N_DEV):
            peer = (x ^ ((k >> 2) & 1), y ^ ((k >> 1) & 1), c ^ (k & 1))
            src = 4 * peer[0] + 2 * peer[1] + peer[2]
            _remote(x_ref, buf_ref.at[src], send_sems, recv_sems, k - 1, peer).wait_recv()
        acc = buf_ref[0]
        for d in range(1, N_DEV):
            acc = acc + buf_ref[d]
        o_ref[...] = acc
        for cp in cps:
            cp.wait_send()
        ride.finish([h_ref], [], h_ss, h_rs)

    vm = pl.BlockSpec(memory_space=pltpu.VMEM)
    return pl.pallas_call(
        body, name="allreduce_small", in_specs=[vm, _ANY], out_specs=[vm, _ANY],
        out_shape=[_sds((r, cc), F32), _sds(halves.shape, halves.dtype)], input_output_aliases={1: 1},
        scratch_shapes=[pltpu.VMEM((N_DEV, r, cc), F32), pltpu.SemaphoreType.DMA((N_DEV - 1,)),
                        pltpu.SemaphoreType.DMA((N_DEV - 1,)), pltpu.SemaphoreType.DMA((1,)), pltpu.SemaphoreType.DMA((1,))],
        compiler_params=pltpu.CompilerParams(has_side_effects=True),
    )(pack, halves)


def _mm_nn(a, w, bias, *, name, grid, tm, tn, k, a_index, w_block, w_index, out_cols, out_index, carried=None, hook=None):
    m = a.shape[0]

    def body(a_ref, w_ref, b_ref, *rest):
        o_ref = rest[-1]
        wv = w_ref[...]
        acc = jnp.dot(a_ref[...].astype(_MXU), wv.reshape(k, tn).astype(_MXU), preferred_element_type=F32)
        o_ref[...] = acc + b_ref[...]

    operands = [a, w, bias] + ([carried] if carried is not None else [])
    return _pcall(
        body, name=name, grid=grid,
        in_specs=[pl.BlockSpec((tm, k), a_index), pl.BlockSpec(w_block, w_index),
                  pl.BlockSpec((1, tn), lambda i, j: (0, j))] + ([_ANY] if carried is not None else []),
        out_specs=[pl.BlockSpec((tm, tn), out_index)], out_shape=[_sds((m, out_cols), F32)], operands=operands,
        vmem=56, sem=("parallel", "parallel"), aliases={3: 0} if carried is not None else None, hook=hook)


def _mm_nt(a, w, *, name, grid, tm, tn, tk, a_index, w_block, w_index, out_cols, hook=None):
    m = a.shape[0]
    nk = grid[2]

    def body(a_ref, w_ref, o_ref, acc_ref):
        part = lax.dot_general(a_ref[...].astype(_MXU), w_ref[...].astype(_MXU), _NT, preferred_element_type=F32)
        if nk == 1:
            o_ref[...] = part
        else:
            kidx = pl.program_id(2)

            @pl.when(kidx == 0)
            def _():
                acc_ref[...] = part

            @pl.when(kidx > 0)
            def _():
                acc_ref[...] += part

            @pl.when(kidx == nk - 1)
            def _():
                o_ref[...] = acc_ref[...]

    return _pcall(
        body, name=name, grid=grid, in_specs=[pl.BlockSpec((tm, tk), a_index), pl.BlockSpec(w_block, w_index)],
        out_specs=[pl.BlockSpec((tm, tn), lambda i, j, q: (i, j))], out_shape=[_sds((m, out_cols), F32)],
        operands=[a, w], scratch=[pltpu.VMEM((tm, tn) if nk > 1 else (8, 128), F32)], vmem=60,
        sem=("parallel", "parallel", "arbitrary"), hook=hook)


def _mm_tn(at, b, *, name, grid, tm, tn, a_index, b_index, out_shape, out_block, out_index, carried=None,
           prefetch=(), hook=None):
    t = at.shape[1]

    def body(*refs):
        a_ref, b_ref = refs[len(prefetch)], refs[len(prefetch) + 1]
        refs[-1][...] = jnp.dot(a_ref[...].astype(_MXU), b_ref[...].astype(_MXU), preferred_element_type=F32)

    operands = [at, b] + ([carried] if carried is not None else [])
    return _pcall(
        body, name=name, grid=grid,
        in_specs=[pl.BlockSpec((tm, t), a_index), pl.BlockSpec((t, tn), b_index)] + ([_ANY] if carried is not None else []),
        out_specs=[pl.BlockSpec(out_block, out_index)], out_shape=[_sds(out_shape, F32)], operands=operands,
        vmem=56, sem=("parallel", "parallel"), aliases={2: 0} if carried is not None else None, prefetch=prefetch, hook=hook)


def _remote_tile(n):
    near = 2 * PER_IN
    if isinstance(n, int):
        return (n % 2, n // 2) if n < near else (2, n - near)
    return jnp.where(n < near, n % 2, 2), jnp.where(n < near, n // 2, n - near)


def _mm_z_gather(h16, w_own, b_in, order):
    n_tiles = N_SH * PER_IN
    n_remote = 3 * PER_IN
    half = D // 2

    def body(order_ref, a_ref, b_ref, w_in_ref, z_ref, w_ref, wbuf, tile_sems, ss, rs):
        del w_in_ref
        j = pl.program_id(0)
        x, y, c = _my_pos()
        me = 2 * x + y
        chips = _other_chips(x, y)
        mine_rows = pl.ds(pl.multiple_of(c * half, 16), half)
        sib_rows = pl.ds(pl.multiple_of((1 - c) * half, 16), half)

        slots = [2 * px + py for px, py in chips]
        cols = lambda t: pl.ds(t * TN_IN, TN_IN)
        q_rows = lambda r: pl.ds(pl.multiple_of(c * half + r * (half // 2), 16), half // 2)

        def direct(rel, t, slot):
            px, py = chips[rel]
            part = w_ref.at[slot, mine_rows, cols(t)]
            return _remote(part, part, ss, rs, 2 * t + rel, (px, py, c))

        def relay(r, t, slot):
            px, py = chips[1 - r]
            part = w_ref.at[slot, q_rows(r), cols(t)]
            return _remote(part, part, ss, rs, 2 * PER_IN + 2 * t + r, (px, py, c))

        def d2d(n, rows):
            rel, t = _remote_tile(n)
            part = w_ref.at[slots[rel], rows, cols(t)]
            return _remote(part, part, ss, rs, 4 * PER_IN + n, (x, y, 1 - c))

        def tile_copy(step):
            rel, t = _remote_tile(jnp.maximum(step - PER_IN, 0))
            slot = jnp.where(step < PER_IN, me, me ^ jnp.where(rel == 0, 2, jnp.where(rel == 1, 1, 3)))
            col = pl.multiple_of(jnp.where(step < PER_IN, step, t) * TN_IN, 128)
            return pltpu.make_async_copy(w_ref.at[slot, :, pl.ds(col, TN_IN)], wbuf.at[step % 2], tile_sems.at[step % 2])

        @pl.when(j == 0)
        def _():
            for t in range(PER_IN):
                direct(0, t, me).start()
                direct(1, t, me).start()
            tile_copy(0).start()

        for n in range(n_remote):
            rel, t = _remote_tile(n)

            @pl.when(j == n + PER_IN - 3)
            def _():
                if rel < 2:
                    direct(rel, t, slots[rel]).wait_recv()
                    relay(rel, t, slots[rel]).start()
                else:
                    relay(0, t, slots[2]).wait_recv()
                    relay(1, t, slots[2]).wait_recv()
                d2d(n, mine_rows).start()

            @pl.when(j == n + PER_IN - 2)
            def _():
                d2d(n, sib_rows).wait_recv()

        @pl.when(j + 1 < n_tiles)
        def _():
            tile_copy(j + 1).start()

        tile_copy(j).wait()
        z_ref[...] = jnp.dot(a_ref[...], wbuf[j % 2], preferred_element_type=F32) + b_ref[...]

        @pl.when(j == n_tiles - 1)
        def _():
            for t in range(PER_IN):
                for r in range(2):
                    direct(r, t, me).wait_send()
                    relay(r, t, slots[r]).wait_send()
            for n in range(n_remote):
                d2d(n, mine_rows).wait_send()

    def col_tile(j, o):
        rel, t = _remote_tile(jnp.maximum(j - PER_IN, 0))
        return 0, jnp.where(j < PER_IN, o[0] * PER_IN + j, o[1 + rel] * PER_IN + t)

    return pl.pallas_call(
        body, name="mm_z_gather",
        grid_spec=pltpu.PrefetchScalarGridSpec(
            num_scalar_prefetch=1, grid=(n_tiles,),
            in_specs=[pl.BlockSpec((TP, D), lambda j, o: (0, 0)), pl.BlockSpec((1, TN_IN), col_tile), _ANY],
            out_specs=[pl.BlockSpec((TP, TN_IN), col_tile), _ANY],
            scratch_shapes=[pltpu.VMEM((2, D, TN_IN), BF16), pltpu.SemaphoreType.DMA((2,)),
                            pltpu.SemaphoreType.DMA((4 * PER_IN + n_remote,)),
                            pltpu.SemaphoreType.DMA((4 * PER_IN + n_remote,))]),
        out_shape=[_sds((TP, D_IN), F32), _sds(w_own.shape, w_own.dtype)],
        input_output_aliases={3: 1},
        compiler_params=pltpu.CompilerParams(vmem_limit_bytes=60 * _MIB, dimension_semantics=("arbitrary",),
                                             has_side_effects=True),
    )(order, h16, b_in, w_own)


def _padded_rows(i, x_ref, meta_ref):
    head = jnp.concatenate([jnp.zeros((PAD, D), F32), meta_ref[...]], axis=0)
    return jnp.where(i == 0, head, x_ref[...])


def _stream_specs():
    return [pl.BlockSpec((BLK, D), lambda i: (jnp.maximum(i - 1, 0), 0)), pl.BlockSpec((N_META, D), lambda i: (0, 0))]


def _ln_emb_fwd(x, meta, g, b):
    def body(x_ref, meta_ref, g_ref, b_ref, h32_ref, h16_ref, h16t_ref):
        x = _padded_rows(pl.program_id(0), x_ref, meta_ref)
        mu = jnp.mean(x, axis=-1, keepdims=True)
        xc = x - mu
        var = jnp.mean(xc * xc, axis=-1, keepdims=True)
        y = xc * lax.rsqrt(var + LN_EPS) * g_ref[...] + b_ref[...]
        h32_ref[...] = y
        h16_ref[...] = y.astype(_MXU)
        h16t_ref[...] = y.T.astype(_MXU)

    row = pl.BlockSpec((BLK, D), lambda i: (i, 0))
    vec = pl.BlockSpec((1, D), lambda i: (0, 0))
    return _pcall(body, name="ln_emb_fwd", grid=(NBLK,), in_specs=_stream_specs() + [vec, vec],
                  out_specs=[row, row, pl.BlockSpec((D, BLK), lambda i: (0, i))],
                  out_shape=[_sds((TP, D), F32), _sds((TP, D), _MXU), _sds((D, TP), _MXU)], operands=[x, meta, g, b],
                  vmem=32, sem=("parallel",))


def _ln_emb_bwd(x, meta, g, dr, dhz, *, hook):
    def body(x_ref, meta_ref, g_ref, dr_ref, dhz_ref, dx_ref, dmeta_ref, acc_ref):
        i = pl.program_id(0)

        @pl.when(i == 0)
        def _():
            acc_ref[...] = jnp.zeros_like(acc_ref)

        x = _padded_rows(i, x_ref, meta_ref)
        mu = jnp.mean(x, axis=-1, keepdims=True)
        xc = x - mu
        var = jnp.mean(xc * xc, axis=-1, keepdims=True)
        rstd = lax.rsqrt(var + LN_EPS)
        xhat = xc * rstd
        dh = ALPHA * dr_ref[...] + dhz_ref[...]
        acc_ref[0:1, :] += jnp.sum(dh * xhat, axis=0, keepdims=True)
        acc_ref[1:2, :] += jnp.sum(dh, axis=0, keepdims=True)
        dxh = dh * g_ref[...]
        m1 = jnp.mean(dxh, axis=-1, keepdims=True)
        m2 = jnp.mean(dxh * xhat, axis=-1, keepdims=True)
        dx = rstd * (dxh - m1 - xhat * m2)
        dx_ref[...] = dx

        @pl.when(i == 0)
        def _():
            dmeta_ref[...] = dx[PAD:BLK]

    row = pl.BlockSpec((BLK, D), lambda i: (i, 0))
    vec = pl.BlockSpec((1, D), lambda i: (0, 0))
    xs, ms = _stream_specs()
    return _pcall(body, name="ln_emb_bwd", grid=(NBLK,), in_specs=[xs, ms, vec, row, row],
                  out_specs=[xs, ms, pl.BlockSpec((8, D), lambda i: (0, 0))],
                  out_shape=[_sds((SEQ, D), F32), _sds((N_META, D), F32), _sds((8, D), F32)],
                  operands=[x, meta, g, dr, dhz], vmem=32, sem=("arbitrary",), hook=hook)


def _mul_silu_fwd(a, z, off, *, name):
    def body(a_ref, z_ref, o_ref, t_ref):
        zz = z_ref[...]
        y = a_ref[...] * (zz * _sigmoid(zz))
        o_ref[...] = y.astype(_MXU)
        t_ref[...] = y.T.astype(_MXU)

    strip = pl.BlockSpec((TP, CW), lambda j: (0, j))
    return _pcall(body, name=name, grid=(D // CW,),
                  in_specs=[strip, pl.BlockSpec((TP, CW), lambda j: (0, off // CW + j))],
                  out_specs=[strip, pl.BlockSpec((CW, TP), lambda j: (j, 0))],
                  out_shape=[_sds((TP, D), _MXU), _sds((D, TP), _MXU)], operands=[a, z], vmem=56, sem=("parallel",))


def _mul_silu_bwd(dy, a, z, off, dz, *, name, hook=None):
    def body(dy_ref, a_ref, z_ref, dz_in, da_ref, dg_ref):
        zz = z_ref[...]
        sg = _sigmoid(zz)
        d = dy_ref[...]
        da_ref[...] = d * (zz * sg)
        dg_ref[...] = (d * a_ref[...] * (sg * (1.0 + zz * (1.0 - sg)))).astype(_MXU)

    blk = pl.BlockSpec((RT, CW), lambda i, j: (i, j))
    zblk = pl.BlockSpec((RT, CW), lambda i, j: (i, off // CW + j))
    return _pcall(body, name=name, grid=(TP // RT, D // CW), in_specs=[blk, blk, zblk, _ANY], out_specs=[blk, zblk],
                  out_shape=[_sds((TP, D), F32), _sds((TP, D_IN), _MXU)], operands=[dy, a, z, dz], vmem=32,
                  sem=("parallel", "parallel"), aliases={3: 1}, hook=hook)


def _merge_fwd(y2, z):
    w = 256

    def body(ya_ref, yb_ref, ga_ref, gb_ref, o_ref, t_ref):
        y = _sigmoid(ga_ref[...]) * ya_ref[...] + _sigmoid(gb_ref[...]) * yb_ref[...]
        o_ref[...] = y.astype(_MXU)
        t_ref[...] = y.T.astype(_MXU)

    nb = D // w
    strip = pl.BlockSpec((TP, w), lambda j: (0, j))
    return _pcall(body, name="merge_fwd", grid=(nb,),
                  in_specs=[strip, pl.BlockSpec((TP, w), lambda j: (0, nb + j)),
                            pl.BlockSpec((TP, w), lambda j: (0, OFF_G // w + j)),
                            pl.BlockSpec((TP, w), lambda j: (0, (OFF_G + D) // w + j))],
                  out_specs=[strip, pl.BlockSpec((w, TP), lambda j: (j, 0))],
                  out_shape=[_sds((TP, D), _MXU), _sds((D, TP), _MXU)], operands=[y2, y2, z, z], vmem=56,
                  sem=("parallel",))


def _merge_bwd(dmix, y2, z):
    nb = D // CW

    def body(dm_ref, y_ref, g_ref, dy_ref, dg_ref):
        dm = dm_ref[...]
        sg = _sigmoid(g_ref[...])
        dy_ref[...] = (dm * sg).astype(_MXU)
        dg_ref[...] = (dm * y_ref[...] * sg * (1.0 - sg)).astype(_MXU)

    blk = pl.BlockSpec((RT, CW), lambda i, j: (i, j))
    gblk = pl.BlockSpec((RT, CW), lambda i, j: (i, OFF_G // CW + j))
    return _pcall(body, name="merge_bwd", grid=(TP // RT, 2 * nb),
                  in_specs=[pl.BlockSpec((RT, CW), lambda i, j: (i, j % nb)), blk, gblk], out_specs=[blk, gblk],
                  out_shape=[_sds((TP, 2 * D), _MXU), _sds((TP, D_IN), _MXU)], operands=[dmix, y2, z], vmem=32,
                  sem=("parallel", "parallel"))


def _ln_out_loss(h32, out, target, g, b):
    def body(h_ref, o_ref, t_ref, g_ref, b_ref, dr_ref, acc_ref):
        i = pl.program_id(0)

        @pl.when(i == 0)
        def _():
            acc_ref[...] = jnp.zeros_like(acc_ref)

        r = ALPHA * h_ref[...] + o_ref[...]
        mu = jnp.mean(r, axis=-1, keepdims=True)
        rc = r - mu
        var = jnp.mean(rc * rc, axis=-1, keepdims=True)
        rstd = lax.rsqrt(var + LN_EPS)
        xhat = rc * rstd
        gg = g_ref[...]
        y = xhat * gg + b_ref[...]
        real = (i >= 1).astype(F32)
        diff = (y - t_ref[...]) * real
        dy = diff * (1.0 / D)
        dxh = dy * gg
        m1 = jnp.mean(dxh, axis=-1, keepdims=True)
        m2 = jnp.mean(dxh * xhat, axis=-1, keepdims=True)
        dr = rstd * (dxh - m1 - xhat * m2)
        dr_ref[...] = dr
        acc_ref[0:1, :] += jnp.sum(dy * xhat, axis=0, keepdims=True)
        acc_ref[1:2, :] += jnp.sum(dy, axis=0, keepdims=True)
        acc_ref[2:3, :] += jnp.sum(dr, axis=0, keepdims=True)
        acc_ref[3:4, :] += (0.5 / D) * jnp.sum(diff * diff)

    row = pl.BlockSpec((BLK, D), lambda i: (i, 0))
    vec = pl.BlockSpec((1, D), lambda i: (0, 0))
    return _pcall(body, name="ln_out_loss", grid=(NBLK,),
                  in_specs=[row, row, pl.BlockSpec((BLK, D), lambda i: (jnp.maximum(i - 1, 0), 0)), vec, vec],
                  out_specs=[row, pl.BlockSpec((8, D), lambda i: (0, 0))],
                  out_shape=[_sds((TP, D), F32), _sds((8, D), F32)], operands=[h32, out, target, g, b], vmem=32,
                  sem=("arbitrary",))


def _colsum(x, *, name, tn, hook=None):
    _, n = x.shape

    def body(x_ref, o_ref):
        o_ref[...] = jnp.sum(x_ref[...].astype(F32), axis=0, keepdims=True)

    return _pcall(body, name=name, grid=(n // tn,), in_specs=[pl.BlockSpec((TP, tn), lambda j: (0, j))],
                  out_specs=[pl.BlockSpec((1, tn), lambda j: (0, j))], out_shape=[_sds((1, n), F32)], operands=[x],
                  vmem=32, sem=("parallel",), hook=hook)


def _rnn_recompute(xr_ref, cw_ref, cb_ref, wra_ref, wri_ref, bra_ref, bri_ref, lam_ref):
    rows = lax.broadcasted_iota(jnp.int32, (TP, 1), 0)
    valid = (rows >= PAD).astype(F32)
    first = rows == PAD
    x = xr_ref[...] * valid
    cw = cw_ref[...]
    shifted = [x, pltpu.roll(x, 1, 0), pltpu.roll(x, 2, 0), pltpu.roll(x, 3, 0)]
    c = cb_ref[...] + cw[0:1, :] * shifted[0] + cw[1:2, :] * shifted[1] + cw[2:3, :] * shifted[2] + cw[3:4, :] * shifted[3]
    cm = c.astype(_MXU)
    gr = _sigmoid(jnp.dot(cm, wra_ref[...].astype(_MXU), preferred_element_type=F32) + bra_ref[...])
    gi = _sigmoid(jnp.dot(cm, wri_ref[...].astype(_MXU), preferred_element_type=F32) + bri_ref[...])
    lam = lam_ref[...]
    ls = jnp.minimum(lam, 0.0) - jnp.log(1.0 + jnp.exp(-jnp.abs(lam)))
    log_a = LRU_C * gr * ls
    a = jnp.exp(log_a)
    mult = jnp.where(first, 1.0, jnp.sqrt(1.0 - jnp.exp(2.0 * log_a)))
    return dict(valid=valid, first=first, shifted=shifted, c=c, cm=cm, gr=gr, gi=gi, ls=ls, a=a, mult=mult, lam=lam)


def _rnn_specs():
    col = pl.BlockSpec((TP, RB), lambda n: (0, n))
    vec = pl.BlockSpec((1, RB), lambda n: (0, n))
    return dict(col=col, vec=vec, cw=pl.BlockSpec((CONV_W, RB), lambda n: (0, n)),
                wblk=pl.BlockSpec((None, RB, RB), lambda n: (n, 0, 0)))


def _rnn_gates_fwd(z, conv_w, conv_b, w_ra, w_ri, b_ra, b_ri, lam):
    def body(xr_ref, cw_ref, cb_ref, wra_ref, wri_ref, bra_ref, bri_ref, lam_ref, a_ref, u_ref):
        r = _rnn_recompute(xr_ref, cw_ref, cb_ref, wra_ref, wri_ref, bra_ref, bri_ref, lam_ref)
        a_ref[...] = r["a"]
        u_ref[...] = r["mult"] * r["gi"] * r["c"] * r["valid"]

    s = _rnn_specs()
    return _pcall(body, name="rnn_gates_fwd", grid=(N_RB,),
                  in_specs=[s["col"], s["cw"], s["vec"], s["wblk"], s["wblk"], s["vec"], s["vec"], s["vec"]],
                  out_specs=[s["col"], s["col"]], out_shape=[_sds((TP, D), F32)] * 2,
                  operands=[z, conv_w, conv_b, w_ra, w_ri, b_ra, b_ri, lam], vmem=56, sem=("parallel",))


SCAN_ROWS = 272


SUB = 8


def _tile_scan(a, u, reverse):
    rows = lax.broadcasted_iota(jnp.int32, a.shape, 0)
    for d in (1, 2, 4):
        shift = SUB - d if reverse else d
        inside = (rows < SUB - d) if reverse else (rows >= d)
        u = u + a * jnp.where(inside, pltpu.roll(u, shift, 0), 0.0)
        a = a * jnp.where(inside, pltpu.roll(a, shift, 0), 1.0)
    return a, u


def _scan_fwd(a, u, *, hook):
    def body(a_ref, u_ref, h_ref, carry_ref):
        @pl.when(pl.program_id(1) == 0)
        def _():
            carry_ref[...] = jnp.zeros_like(carry_ref)

        def step(r, h):
            rows = pl.ds(pl.multiple_of(r * SUB, SUB), SUB)
            prod, part = _tile_scan(a_ref[rows, :], u_ref[rows, :], False)
            ht = part + prod * h
            h_ref[rows, :] = ht
            return ht[SUB - 1:SUB, :]

        carry_ref[...] = lax.fori_loop(0, SCAN_ROWS // SUB, step, carry_ref[...], unroll=2)

    blk = pl.BlockSpec((SCAN_ROWS, CW), lambda j, i: (i, j))
    return _pcall(body, name="scan_fwd", grid=(D // CW, TP // SCAN_ROWS), in_specs=[blk, blk], out_specs=[blk],
                  out_shape=[_sds((TP, D), F32)], operands=[a, u], scratch=[pltpu.VMEM((1, CW), F32)], vmem=32,
                  sem=("parallel", "arbitrary"), hook=hook)


def _scan_bwd(a, dh):
    nst = TP // SCAN_ROWS
    n_tiles = SCAN_ROWS // SUB

    def body(a_ref, d_ref, o_ref, lam_ref, anext_ref):
        @pl.when(pl.program_id(1) == 0)
        def _():
            lam_ref[...] = jnp.zeros_like(lam_ref)
            anext_ref[...] = jnp.zeros_like(anext_ref)

        def step(q, carry):
            lam_next, a_next = carry
            rows = pl.ds(pl.multiple_of((n_tiles - 1 - q) * SUB, SUB), SUB)
            at = a_ref[rows, :]
            last = lax.broadcasted_iota(jnp.int32, at.shape, 0) == SUB - 1
            b = jnp.where(last, a_next, pltpu.roll(at, SUB - 1, 0))
            prod, part = _tile_scan(b, d_ref[rows, :], True)
            lam = part + prod * lam_next
            o_ref[rows, :] = lam
            return lam[0:1, :], at[0:1, :]

        lam, an = lax.fori_loop(0, n_tiles, step, (lam_ref[...], anext_ref[...]), unroll=2)
        lam_ref[...] = lam
        anext_ref[...] = an

    blk = pl.BlockSpec((SCAN_ROWS, CW), lambda j, i: (nst - 1 - i, j))
    return _pcall(body, name="scan_bwd", grid=(D // CW, nst), in_specs=[blk, blk], out_specs=[blk],
                  out_shape=[_sds((TP, D), F32)], operands=[a, dh],
                  scratch=[pltpu.VMEM((1, CW), F32), pltpu.VMEM((1, CW), F32)], vmem=32,
                  sem=("parallel", "arbitrary"))[0]


def _rnn_gates_bwd(z, lam_s, hr, conv_w, conv_b, w_ra, w_ri, b_ra, b_ri, lam, dz, gsq):
    def body(xr_ref, ls_ref, hr_ref, cw_ref, cb_ref, wra_ref, wri_ref, bra_ref, bri_ref, lam_ref, dz_in, gsq_in,
             dx_ref, dw_ref, sums_ref):
        r = _rnn_recompute(xr_ref, cw_ref, cb_ref, wra_ref, wri_ref, bra_ref, bri_ref, lam_ref)
        valid, c, gr, gi, a, mult = r["valid"], r["c"], r["gr"], r["gi"], r["a"], r["mult"]
        du = ls_ref[...] * valid
        da = du * pltpu.roll(hr_ref[...], 1, 0)
        d_gi = du * mult * c
        dc = du * mult * gi
        dmult = du * gi * c
        dlog_a = da * a + jnp.where(r["first"], 0.0, -dmult * a * a / mult)
        d_gr = dlog_a * (LRU_C * r["ls"])
        dls = jnp.sum(dlog_a * (LRU_C * gr), axis=0, keepdims=True)
        dpre_r = d_gr * gr * (1.0 - gr)
        dpre_i = d_gi * gi * (1.0 - gi)
        pr = dpre_r.astype(_MXU)
        pi = dpre_i.astype(_MXU)
        dwra = lax.dot_general(r["cm"], pr, _TN, preferred_element_type=F32)
        dwri = lax.dot_general(r["cm"], pi, _TN, preferred_element_type=F32)
        for s in range(N_SH):
            dw_ref[s, 0:64, :] = dwra[64 * s:64 * (s + 1)]
            dw_ref[s, 64:128, :] = dwri[64 * s:64 * (s + 1)]
        dc = dc + lax.dot_general(pr, wra_ref[...].astype(_MXU), _NT, preferred_element_type=F32)
        dc = dc + lax.dot_general(pi, wri_ref[...].astype(_MXU), _NT, preferred_element_type=F32)
        cw = cw_ref[...]
        dx = cw[0:1, :] * dc
        for k in range(1, CONV_W):
            dx = dx + cw[k:k + 1, :] * pltpu.roll(dc, TP - k, 0)
        dx_ref[...] = (dx * valid).astype(_MXU)
        for k in range(CONV_W):
            sums_ref[k:k + 1, :] = jnp.sum(dc * r["shifted"][k], axis=0, keepdims=True)
        sums_ref[4:5, :] = jnp.sum(dc, axis=0, keepdims=True)
        sums_ref[5:6, :] = jnp.sum(dpre_r, axis=0, keepdims=True)
        sums_ref[6:7, :] = jnp.sum(dpre_i, axis=0, keepdims=True)
        sums_ref[7:8, :] = dls * _sigmoid(-r["lam"])

    s = _rnn_specs()
    return _pcall(
        body, name="rnn_gates_bwd", grid=(N_RB,),
        in_specs=[s["col"], s["col"], s["col"], s["cw"], s["vec"], s["wblk"], s["wblk"], s["vec"], s["vec"], s["vec"],
                  _ANY, _ANY],
        out_specs=[s["col"], pl.BlockSpec((N_SH, 128, RB), lambda n: (0, 3 * SQ_ROWS // 128, n)),
                   pl.BlockSpec((8, RB), lambda n: (0, n))],
        out_shape=[_sds((TP, D_IN), _MXU), _sds((N_SH, PACK_ROWS, D), F32), _sds((8, D), F32)],
        operands=[z, lam_s, hr, conv_w, conv_b, w_ra, w_ri, b_ra, b_ri, lam, dz, gsq], vmem=60, sem=("parallel",),
        aliases={10: 0, 11: 1})


def _rope_tables():
    half = HD // 2
    inv = ROPE_THETA ** (-jnp.arange(half, dtype=F32) / half)
    pos = (jnp.arange(TP) - PAD).astype(F32)
    ang = pos[:, None] * inv[None, :]
    return jnp.tile(jnp.cos(ang), (1, 4)), jnp.tile(jnp.sin(ang), (1, 4))


def _rope(x, cos_t, sin_t, sign):
    w = x.shape[1]
    lane = lax.broadcasted_iota(jnp.int32, x.shape, 1)
    first = (lane % HD) < (HD // 2)
    swapped = jnp.where(first, pltpu.roll(x, w - HD // 2, 1), pltpu.roll(x, HD // 2, 1))
    ct = jnp.tile(cos_t, (1, w // 128))
    st = jnp.tile(sin_t, (1, w // 128))
    return x * ct + swapped * jnp.where(first, -sign * st, sign * st)


def _rope_fwd(z, cos_t, sin_t):
    def body(q_ref, k_ref, v_ref, c_ref, s_ref, qo_ref, ko_ref, vo_ref):
        c = c_ref[...]
        s = s_ref[...]
        qo_ref[...] = _rope(q_ref[...], c, s, 1.0).astype(_MXU)
        ko_ref[...] = _rope(k_ref[...], c, s, 1.0).astype(_MXU)
        vo_ref[...] = v_ref[...].astype(_MXU)

    tab = pl.BlockSpec((BLK, 128), lambda i: (i, 0))
    kv = pl.BlockSpec((BLK, D_KV), lambda i: (i, 0))
    return _pcall(body, name="rope_fwd", grid=(NBLK,),
                  in_specs=[pl.BlockSpec((BLK, D), lambda i: (i, OFF_Q // D)),
                            pl.BlockSpec((BLK, D_KV), lambda i: (i, OFF_K // D_KV)),
                            pl.BlockSpec((BLK, D_KV), lambda i: (i, OFF_V // D_KV)), tab, tab],
                  out_specs=[pl.BlockSpec((BLK, D), lambda i: (i, 0)), kv, kv],
                  out_shape=[_sds((TP, D), _MXU), _sds((TP, D_KV), _MXU), _sds((TP, D_KV), _MXU)],
                  operands=[z, z, z, cos_t, sin_t], vmem=32, sem=("parallel",))


def _rope_bwd_k(dk, dv, cos_t, sin_t, dz):
    def body(dk_ref, dv_ref, c_ref, s_ref, dz_in, o_ref):
        o_ref[:, 0:D_KV] = _rope(dk_ref[...], c_ref[...], s_ref[...], -1.0).astype(_MXU)
        o_ref[:, D_KV:2 * D_KV] = dv_ref[...].astype(_MXU)

    tab = pl.BlockSpec((BLK, 128), lambda i: (i, 0))
    kv = pl.BlockSpec((BLK, D_KV), lambda i: (i, 0))
    return _pcall(body, name="rope_bwd_k", grid=(NBLK,), in_specs=[kv, kv, tab, tab, _ANY],
                  out_specs=[pl.BlockSpec((BLK, 2 * D_KV), lambda i: (i, OFF_K // (2 * D_KV)))],
                  out_shape=[_sds((TP, D_IN), _MXU)], operands=[dk, dv, cos_t, sin_t, dz], vmem=32, sem=("parallel",),
                  aliases={4: 0})[0]


def _attn_mask(i):
    ql = lax.broadcasted_iota(jnp.int32, (BLK, 3 * BLK), 0)
    kk = lax.broadcasted_iota(jnp.int32, (BLK, 3 * BLK), 1)
    kl = kk % BLK
    part = kk // BLK
    meta = (part == 0) & (kl >= PAD) & ((i >= 1) | (kl <= ql))
    prev = (part == 1) & (i >= 2) & (kl > ql)
    cur = (part == 2) & (i >= 1) & (kl <= ql)
    return meta | prev | cur


def _cat_kv(refs, g):
    return jnp.concatenate([r[:, HD * g:HD * (g + 1)] for r in refs], axis=0)


def _kv_specs():
    return [pl.BlockSpec((BLK, D_KV), lambda i: (0, 0)),
            pl.BlockSpec((BLK, D_KV), lambda i: (jnp.maximum(i - 1, 0), 0)),
            pl.BlockSpec((BLK, D_KV), lambda i: (i, 0))]


def _stack_heads(ref, g, width=HD):
    return jnp.concatenate([ref[:, width * (GRP * g + j):width * (GRP * g + j + 1)] for j in range(GRP)], axis=0)


def _attn_fwd(q, k, v, sinks, *, hook):
    def body(q_ref, k0_ref, kp_ref, kc_ref, v0_ref, vp_ref, vc_ref, sink_ref, o_ref, lse_ref, s_scr, p_scr):
        mask = _attn_mask(pl.program_id(0))
        for g in range(N_KV):
            s_scr[...] = lax.dot_general(_stack_heads(q_ref, g), _cat_kv((k0_ref, kp_ref, kc_ref), g), _NT,
                                         preferred_element_type=F32)
            for j in range(GRP):
                h = GRP * g + j
                rows = slice(BLK * j, BLK * (j + 1))
                sink = sink_ref[h]
                s = jnp.where(mask, s_scr[rows, :] * (HD ** -0.5), NEG_INF)
                mx = jnp.maximum(jnp.max(s, -1, keepdims=True), sink)
                p = jnp.exp(s - mx)
                den = jnp.sum(p, -1, keepdims=True) + jnp.exp(sink - mx)
                p_scr[rows, :] = (p * (1.0 / den)).astype(_MXU)
                lse_ref[:, h:h + 1] = mx + jnp.log(den)
            o8 = jnp.dot(p_scr[...], _cat_kv((v0_ref, vp_ref, vc_ref), g), preferred_element_type=F32)
            for j in range(GRP):
                h = GRP * g + j
                o_ref[:, HD * h:HD * (h + 1)] = o8[BLK * j:BLK * (j + 1)]

    return _pcall(body, name="attn_fwd", grid=(NBLK,),
                  in_specs=[pl.BlockSpec((BLK, D), lambda i: (i, 0))] + _kv_specs() + _kv_specs()
                           + [pl.BlockSpec(memory_space=pltpu.SMEM)],
                  out_specs=[pl.BlockSpec((BLK, D), lambda i: (i, 0)), pl.BlockSpec((BLK, N_Q), lambda i: (i, 0))],
                  out_shape=[_sds((TP, D), F32), _sds((TP, N_Q), F32)], operands=[q, k, k, k, v, v, v, sinks],
                  scratch=[pltpu.VMEM((GRP * BLK, 3 * BLK), F32), pltpu.VMEM((GRP * BLK, 3 * BLK), _MXU)],
                  vmem=40, sem=("parallel",), hook=hook)


def _attn_bwd(q, k, v, sinks, do, o, lse, cos_t, sin_t, dz, *, hook):
    def body(q_ref, k0_ref, kp_ref, kc_ref, v0_ref, vp_ref, vc_ref, sink_ref, do_ref, o_ref, lse_ref, c_ref, s_ref, dz_in,
             dq_ref, dk_ref, dv_ref, dsink_ref, dqrot_ref, s_scr, dp_scr, p_scr, ds_scr):
        i = pl.program_id(0)

        @pl.when(i == 0)
        def _():
            dk_ref[...] = jnp.zeros_like(dk_ref)
            dv_ref[...] = jnp.zeros_like(dv_ref)
            dsink_ref[...] = jnp.zeros_like(dsink_ref)

        mask = _attn_mask(i)
        row_starts = (0, pl.multiple_of(jnp.maximum(i - 1, 0) * BLK, BLK), pl.multiple_of(i * BLK, BLK))
        scale = HD ** -0.5
        for g in range(N_KV):
            gs = slice(HD * g, HD * (g + 1))
            q8 = _stack_heads(q_ref, g)
            dom = _stack_heads(do_ref, g).astype(_MXU)
            kcat = _cat_kv((k0_ref, kp_ref, kc_ref), g)
            s_scr[...] = lax.dot_general(q8, kcat, _NT, preferred_element_type=F32)
            dp_scr[...] = lax.dot_general(dom, _cat_kv((v0_ref, vp_ref, vc_ref), g), _NT, preferred_element_type=F32)
            for j in range(GRP):
                h = GRP * g + j
                hs = slice(HD * h, HD * (h + 1))
                rows = slice(BLK * j, BLK * (j + 1))
                lse_h = lse_ref[:, h:h + 1]
                delta = jnp.sum(do_ref[:, hs] * o_ref[:, hs], axis=-1, keepdims=True)
                dsink_ref[0:1, h:h + 1] += jnp.sum(-jnp.exp(sink_ref[h] - lse_h) * delta, axis=0, keepdims=True)
                p = jnp.exp(jnp.where(mask, s_scr[rows, :] * scale, NEG_INF) - lse_h)
                p_scr[rows, :] = p.astype(_MXU)
                ds_scr[rows, :] = (p * (dp_scr[rows, :] - delta) * scale).astype(_MXU)
            ds = ds_scr[...]
            dq8 = jnp.dot(ds, kcat, preferred_element_type=F32)
            dkcat = lax.dot_general(ds, q8, _TN, preferred_element_type=F32)
            dvcat = lax.dot_general(p_scr[...], dom, _TN, preferred_element_type=F32)
            for part in range(3):
                rows = pl.ds(row_starts[part], BLK)
                dk_ref[rows, gs] += dkcat[BLK * part:BLK * (part + 1)]
                dv_ref[rows, gs] += dvcat[BLK * part:BLK * (part + 1)]
            for j in range(GRP):
                h = GRP * g + j
                dqrot_ref[:, HD * h:HD * (h + 1)] = dq8[BLK * j:BLK * (j + 1)]
        dq_ref[...] = _rope(dqrot_ref[...], c_ref[...], s_ref[...], -1.0).astype(_MXU)

    row = pl.BlockSpec((BLK, D), lambda i: (i, 0))
    tab = pl.BlockSpec((BLK, 128), lambda i: (i, 0))
    full_kv = pl.BlockSpec((TP, D_KV), lambda i: (0, 0))
    return _pcall(
        body, name="attn_bwd", grid=(NBLK,),
        in_specs=[row] + _kv_specs() + _kv_specs() + [pl.BlockSpec(memory_space=pltpu.SMEM), row, row,
                  pl.BlockSpec((BLK, N_Q), lambda i: (i, 0)), tab, tab, _ANY],
        out_specs=[pl.BlockSpec((BLK, D), lambda i: (i, OFF_Q // D)), full_kv, full_kv,
                   pl.BlockSpec((8, 128), lambda i: (0, 0))],
        out_shape=[_sds((TP, D_IN), _MXU), _sds((TP, D_KV), F32), _sds((TP, D_KV), F32), _sds((8, 128), F32)],
        operands=[q, k, k, k, v, v, v, sinks, do, o, lse, cos_t, sin_t, dz],
        scratch=[pltpu.VMEM((BLK, D), F32), pltpu.VMEM((GRP * BLK, 3 * BLK), F32), pltpu.VMEM((GRP * BLK, 3 * BLK), F32),
                 pltpu.VMEM((GRP * BLK, 3 * BLK), _MXU), pltpu.VMEM((GRP * BLK, 3 * BLK), _MXU)],
        vmem=48, sem=("arbitrary",), aliases={13: 0}, hook=hook)


def _cast_into_slot(w32, pos, *, name, tr):
    r, cc = w32.shape

    def body(pos_ref, w_ref, o_ref):
        o_ref[...] = w_ref[...].astype(BF16)

    return _pcall(body, name=name, grid=(r // tr,), in_specs=[pl.BlockSpec((tr, cc), lambda i, p: (i, 0))],
                  out_specs=[pl.BlockSpec((None, tr, cc), lambda i, p: (p[1], i, 0))],
                  out_shape=[_sds((N_SH, r, cc), BF16)], operands=[w32], vmem=32, sem=("parallel",), prefetch=(pos,))[0]


def _pair_add(g, got, pos, *, name, tr, g_has_both_halves):
    n, h, cc = got.shape
    nt = h // tr

    def body(pos_ref, g_ref, r_ref, own_ref, s16_ref):
        s = g_ref[...] + r_ref[...]
        s16_ref[...] = s.astype(BF16)

        @pl.when(pl.program_id(1) == pos_ref[1])
        def _():
            own_ref[...] = s

    g_index = (lambda i, s, p: (s, p[0] * nt + i, 0)) if g_has_both_halves else (lambda i, s, p: (s, i, 0))
    return _pcall(body, name=name, grid=(nt, n),
                  in_specs=[pl.BlockSpec((None, tr, cc), g_index), pl.BlockSpec((None, tr, cc), lambda i, s, p: (s, i, 0))],
                  out_specs=[pl.BlockSpec((tr, cc), lambda i, s, p: (i, 0)),
                             pl.BlockSpec((None, tr, cc), lambda i, s, p: (s, i, 0))],
                  out_shape=[_sds((h, cc), F32), _sds((n, h, cc), BF16)], operands=[g, got], vmem=40,
                  sem=("parallel", "arbitrary"), prefetch=(pos,))


def _relay_add(s16, raw, pos, *, name, tr):
    _, q, cc = raw.shape
    nt = q // tr

    def body(pos_ref, s_ref, r_ref, o_ref):
        o_ref[...] = (s_ref[...].astype(F32) + r_ref[...].astype(F32)).astype(BF16)

    blk = pl.BlockSpec((None, tr, cc), lambda k, i, p: (k, i, 0))
    return _pcall(body, name=name, grid=(2, nt),
                  in_specs=[pl.BlockSpec((None, tr, cc), lambda k, i, p: (p[3 + k], (1 - k) * nt + i, 0)), blk],
                  out_specs=[blk], out_shape=[_sds((2, q, cc), BF16)], operands=[s16, raw], vmem=40,
                  sem=("parallel", "parallel"), prefetch=(pos,))[0]


def _sum_chips(own, got, pos, *, name, tr):
    h, cc = own.shape
    n_got = got.shape[0]
    nt = h // tr

    def body(pos_ref, o_ref, r_ref, out_ref):
        acc = o_ref[...]
        for k in range(n_got):
            acc = acc + r_ref[k].astype(F32)
        out_ref[...] = acc

    return _pcall(body, name=name, grid=(nt,),
                  in_specs=[pl.BlockSpec((tr, cc), lambda i, p: (i, 0)),
                            pl.BlockSpec((n_got, tr, cc), lambda i, p: (0, i, 0))],
                  out_specs=[pl.BlockSpec((tr, cc), lambda i, p: (p[0] * nt + i, 0))],
                  out_shape=[_sds((2 * h, cc), F32)], operands=[own, got], vmem=40, sem=("parallel",), prefetch=(pos,))[0]


def _adamw(w, g, m, v, *, name, tr, g_row0=0):
    r, cc = w.shape
    g_blk0 = g_row0 // tr

    def body(w_ref, g_ref, m_ref, v_ref, go_ref, d_ref, mo_ref, vo_ref):
        gg = g_ref[...]
        go_ref[...] = gg
        m_new = ADAM_B1 * m_ref[...] + (1.0 - ADAM_B1) * gg
        v_new = ADAM_B2 * v_ref[...] + (1.0 - ADAM_B2) * (gg * gg)
        m_hat = m_new / (1.0 - ADAM_B1 ** ADAM_STEP)
        v_hat = v_new / (1.0 - ADAM_B2 ** ADAM_STEP)
        d_ref[...] = -ADAM_LR * (m_hat / (jnp.sqrt(v_hat) + ADAM_EPS) + ADAM_WD * w_ref[...])
        mo_ref[...] = m_new
        vo_ref[...] = v_new

    blk = pl.BlockSpec((tr, cc), lambda i: (i, 0))
    gblk = pl.BlockSpec((tr, cc), lambda i: (g_blk0 + i, 0))
    return _pcall(body, name=name, grid=(r // tr,), in_specs=[blk, gblk, blk, blk], out_specs=[blk] * 4,
                  out_shape=[_sds((r, cc), F32)] * 4, operands=[w, g, m, v], vmem=48, sem=("parallel",))


_SMALL_ROWS = 40
_B_IN_ROWS = 7


def _row_pad(v, rows):
    flat = v.reshape(-1)
    return jnp.pad(flat, (0, rows * D - flat.shape[0])).reshape(rows, D)


def _pack_ra(w):
    return w.transpose(1, 0, 2).reshape(64, D)


def _unpack_ra(p, like):
    return p.reshape(64, N_RB, RB).transpose(1, 0, 2).reshape(like.shape)


def _gate_full(g4):
    return g4.reshape(N_SH, 64, N_RB, RB).transpose(2, 0, 1, 3).reshape(N_RB, RB, RB)


def kernel(x, meta_tokens, ln_emb_g, ln_emb_b, w_in, b_in, conv_w, conv_b, w_ra, b_ra, w_ri, b_ri, lru_lambda, sinks, w_rnn_out, w_attn_out, w_o, b_o, ln_g, ln_b, loss_target, m_meta_tokens, m_ln_emb_g, m_ln_emb_b, m_w_in, m_b_in, m_conv_w, m_conv_b, m_w_ra, m_b_ra, m_w_ri, m_b_ri, m_lru_lambda, m_sinks, m_w_rnn_out, m_w_attn_out, m_w_o, m_b_o, m_ln_g, m_ln_b, v_meta_tokens, v_ln_emb_g, v_ln_emb_b, v_w_in, v_b_in, v_conv_w, v_conv_b, v_w_ra, v_b_ra, v_w_ri, v_b_ri, v_lru_lambda, v_sinks, v_w_rnn_out, v_w_attn_out, v_w_o, v_b_o, v_ln_g, v_ln_b):
    xi, yi, ci = _my_pos()
    shard = 2 * xi + yi
    pos = jnp.stack([ci, shard, 1 - ci, shard ^ 2, shard ^ 1]).astype(jnp.int32)
    cos_t, sin_t = _rope_tables()
    zero_bias = jnp.zeros((1, D), F32)
    ln_emb_g2, ln_emb_b2 = ln_emb_g[None], ln_emb_b[None]

    small = jnp.concatenate([conv_w[0], meta_tokens, jnp.zeros((4, 512), F32)], axis=0)
    small4 = _gather_small(small)
    conv_w_full = small4[:, 0:4].transpose(1, 0, 2).reshape(CONV_W, D)
    meta_full = small4[:, 4:20].transpose(1, 0, 2).reshape(N_META, D)
    w_own = _cast_into_slot(w_in[0], pos, name="cast_w_in", tr=256)
    wa_own = _cast_into_slot(jnp.concatenate([w_attn_out[0], w_o[0], _pack_ra(w_ra[0]), _pack_ra(w_ri[0])], axis=0),
                             pos, name="cast_w_a", tr=288)
    wb_own = _cast_into_slot(w_rnn_out[0], pos, name="cast_w_b", tr=256)

    h32, h16, h16_t = _ln_emb_fwd(x[0], meta_full, ln_emb_g2, ln_emb_b2)
    order = jnp.stack([shard, shard ^ 2, shard ^ 1, shard ^ 3]).astype(jnp.int32)
    z, w_in4 = _mm_z_gather(h16, w_own, b_in, order)
    q, k, v = _rope_fwd(z, cos_t, sin_t)
    o, lse, wa4 = _attn_fwd(q, k, v, sinks[0], hook=_hook_gather(wa_own, 0.6))
    w_ra_full = _gate_full(wa4[:, 2 * SQ_ROWS:2 * SQ_ROWS + 64])
    w_ri_full = _gate_full(wa4[:, 2 * SQ_ROWS + 64:2 * SQ_ROWS + 128])
    a_dec, u_in = _rnn_gates_fwd(z, conv_w_full, conv_b, w_ra_full, w_ri_full, b_ra, b_ri, lru_lambda)
    hr, wb4 = _scan_fwd(a_dec, u_in, hook=_hook_gather(wb_own, 0.6))
    sq_w = {0: (wb4, 0), 1: (wa4, 0), 2: (wa4, 1)}

    def sq_nn(a, kk, bias, name, out_cols, out_index, carried=None):
        wp, blk = sq_w[kk]
        return _mm_nn(a, wp, bias, name=name, grid=(2, D // CW), tm=HALF_TP, tn=CW, k=D, a_index=lambda i, j: (i, 0),
                      w_block=(N_SH, SQ_ROWS, CW), w_index=lambda i, j: (0, blk, j), out_cols=out_cols,
                      out_index=out_index, carried=carried)[0]

    def sq_nt(a, a_blk, kk, name, hook=None):
        wp, blk = sq_w[kk]
        return _mm_nt(a, wp, name=name, grid=(2, N_SH, 1), tm=HALF_TP, tn=SQ_ROWS, tk=D,
                      a_index=lambda i, j, q: (i, a_blk), w_block=(None, SQ_ROWS, D),
                      w_index=lambda i, j, q: (j, blk, 0), out_cols=D, hook=hook)

    ya_in, ya_in_t = _mul_silu_fwd(hr, z, OFF_GR, name="gate_a_fwd")
    y2 = sq_nn(ya_in, 0, zero_bias, "mm_ya", 2 * D, lambda i, j: (i, j))
    yb_in, yb_in_t = _mul_silu_fwd(o, z, OFF_GA, name="gate_b_fwd")
    y2 = sq_nn(yb_in, 1, zero_bias, "mm_yb", 2 * D, lambda i, j: (i, D // CW + j), carried=y2)
    mixed, mixed_t = _merge_fwd(y2, z)
    out = sq_nn(mixed, 2, b_o, "mm_out", D, lambda i, j: (i, j))
    dr, sums_o = _ln_out_loss(h32, out, loss_target[0], ln_g, ln_b)

    def sq_tn(at, b, b_blk0, kk, name, carried=None):
        return _mm_tn(at, b, name=name, grid=(N_SH, D // CW), tm=SQ_ROWS, tn=CW, a_index=lambda i, j: (i, 0),
                      b_index=lambda i, j: (0, b_blk0 + j), out_shape=(N_SH, PACK_ROWS, D),
                      out_block=(None, SQ_ROWS, CW), out_index=lambda i, j: (i, kk, j), carried=carried)[0]

    gsq = sq_tn(mixed_t, dr, 0, 2, "mm_dwo")
    dmix = sq_nt(dr, 0, 2, "mm_dmix")[0]
    dy2, dz = _merge_bwd(dmix, y2, z)
    gsq = sq_tn(ya_in_t, dy2, 0, 0, "mm_dwrnn", carried=gsq)
    gsq = sq_tn(yb_in_t, dy2, D // CW, 1, "mm_dwattn", carried=gsq)
    dya_in = sq_nt(dy2, 0, 0, "mm_dyain")[0]
    dhr, dz = _mul_silu_bwd(dya_in, hr, z, OFF_GR, dz, name="gate_a_bwd")
    lam_s = _scan_bwd(a_dec, dhr)
    dz, gsq, sums_r = _rnn_gates_bwd(z, lam_s, hr, conv_w_full, conv_b, w_ra_full, w_ri_full, b_ra, b_ri, lru_lambda, dz, gsq)
    dyb_in = sq_nt(dy2, 1, 1, "mm_dybin")[0]
    do, dz, gsq, got_sq = _mul_silu_bwd(dyb_in, o, z, OFF_GA, dz, name="gate_b_bwd", hook=_hook_pair(gsq, True))
    own_sq, s16_sq = _pair_add(gsq, got_sq, pos, name="red_w_sq_add", tr=208, g_has_both_halves=True)
    dz, dk_rot, dv32, dsink, s16_sq, fin_sq, raw_sq = _attn_bwd(q, k, v, sinks[0], do, o, lse, cos_t, sin_t, dz,
                                                                hook=_hook_scatter_direct(s16_sq))
    dz = _rope_bwd_k(dk_rot, dv32, cos_t, sin_t, dz)
    comb_sq = _relay_add(s16_sq, raw_sq, pos, name="red_w_sq_relay", tr=208)
    g_b_in, comb_sq, fin_sq = _colsum(dz, name="colsum_dz", tn=TN_IN, hook=_hook_scatter_relay(comb_sq, fin_sq))
    red_sq = _sum_chips(own_sq, fin_sq, pos, name="red_w_sq_sum", tr=208)

    def dwin(half_idx, name, hook):
        return _mm_tn(h16_t, dz, name=name, grid=(1, N_SH * PER_IN), tm=D // 2, tn=TN_IN,
                      a_index=lambda i, j, p: (p[half_idx], 0), b_index=lambda i, j, p: (0, j),
                      out_shape=(N_SH, D // 2, W_IN_COLS), out_block=(None, D // 2, TN_IN),
                      out_index=lambda i, j, p: (j // PER_IN, 0, j % PER_IN), prefetch=(pos,), hook=hook)

    gin_sib, g_sq = dwin(2, "mm_dwin_sib", _hook_halves(red_sq))
    gin_own, gin_sib, got_in = dwin(0, "mm_dwin_own", _hook_pair(gin_sib, False))
    own_in, s16_in = _pair_add(gin_own, got_in, pos, name="red_w_in_add", tr=128, g_has_both_halves=False)
    dhz, s16_in, fin_in, raw_in = _mm_nt(dz, w_in4, name="mm_dhz", grid=(2, 2, N_SH), tm=HALF_TP, tn=D // 2,
                                         tk=W_IN_COLS, a_index=lambda i, j, q: (i, q),
                                         w_block=(None, D // 2, W_IN_COLS), w_index=lambda i, j, q: (q, j, 0), out_cols=D,
                                         hook=_hook_scatter_direct(s16_in))
    comb_in = _relay_add(s16_in, raw_in, pos, name="red_w_in_relay", tr=128)
    g_x, g_meta_local, sums_e, comb_in, fin_in = _ln_emb_bwd(x[0], meta_full, ln_emb_g2, dr, dhz,
                                                             hook=_hook_scatter_relay(comb_in, fin_in))
    red_in = _sum_chips(own_in, fin_in, pos, name="red_w_in_sum", tr=128)

    spack = jnp.concatenate([
        sums_e[0:1], sums_e[1:2], _row_pad(g_b_in, _B_IN_ROWS), sums_r[0:4], sums_r[4:5], sums_r[5:6], sums_r[6:7],
        sums_r[7:8], _row_pad(dsink[0:1, 0:N_Q], 1), sums_o[2:3], sums_o[0:1], sums_o[1:2], g_meta_local, sums_o[3:4],
        jnp.zeros((_SMALL_ROWS - 38, D), F32)], axis=0)
    sred, g_in = _allreduce_small(spack, red_in)

    big = {"w_in": [t.reshape(w_in.shape) for t in
                    _adamw(w_in[0], g_in, m_w_in[0], v_w_in[0], name="adamw_w_in", tr=128)]}
    for kk, (n, w_, m_, v_) in enumerate([("w_rnn_out", w_rnn_out, m_w_rnn_out, v_w_rnn_out),
                                          ("w_attn_out", w_attn_out, m_w_attn_out, v_w_attn_out), ("w_o", w_o, m_w_o, v_w_o)]):
        big[n] = [t.reshape(w_.shape) for t in
                  _adamw(w_[0], g_sq, m_[0], v_[0], name="adamw_" + n, tr=256, g_row0=SQ_ROWS * kk)]
    for kk, (n, w_, m_, v_) in enumerate([("w_ra", w_ra, m_w_ra, v_w_ra), ("w_ri", w_ri, m_w_ri, v_w_ri)]):
        big[n] = [_unpack_ra(t, w_) for t in
                  _adamw(_pack_ra(w_[0]), g_sq, _pack_ra(m_[0]), _pack_ra(v_[0]), name="adamw_" + n, tr=64,
                         g_row0=3 * SQ_ROWS + 64 * kk)]

    loss = sred[37, 0]
    col0 = shard * 512
    g_conv_w = lax.dynamic_slice(sred[9:13], (0, col0), (CONV_W, 512))
    g_meta = lax.dynamic_slice(sred[21:37], (0, col0), (N_META, 512))
    small_g = {"ln_emb_g": sred[0:1], "ln_emb_b": sred[1:2], "b_in": sred[2:9], "conv_w": g_conv_w.reshape(1, D),
               "conv_b": sred[13:14], "b_ra": sred[14:15], "b_ri": sred[15:16], "lru_lambda": sred[16:17],
               "sinks": sred[17:18], "b_o": sred[18:19], "ln_g": sred[19:20], "ln_b": sred[20:21],
               "meta_tokens": g_meta.reshape(4, D)}
    small_names = list(small_g)

    def small_pack(vals):
        rows = []
        for n in small_names:
            a = vals[n]
            if n == "b_in":
                a = _row_pad(a, _B_IN_ROWS)
            elif n == "sinks":
                a = _row_pad(a, 1)
            else:
                a = a.reshape(-1, D)
            rows.append(a)
        return jnp.concatenate(rows + [jnp.zeros((24 - 22, D), F32)], axis=0)

    w_small = dict(ln_emb_g=ln_emb_g, ln_emb_b=ln_emb_b, b_in=b_in, conv_w=conv_w, conv_b=conv_b, b_ra=b_ra, b_ri=b_ri,
                   lru_lambda=lru_lambda, sinks=sinks, b_o=b_o, ln_g=ln_g, ln_b=ln_b, meta_tokens=meta_tokens)
    m_small = dict(ln_emb_g=m_ln_emb_g, ln_emb_b=m_ln_emb_b, b_in=m_b_in, conv_w=m_conv_w, conv_b=m_conv_b, b_ra=m_b_ra,
                   b_ri=m_b_ri, lru_lambda=m_lru_lambda, sinks=m_sinks, b_o=m_b_o, ln_g=m_ln_g, ln_b=m_ln_b,
                   meta_tokens=m_meta_tokens)
    v_small = dict(ln_emb_g=v_ln_emb_g, ln_emb_b=v_ln_emb_b, b_in=v_b_in, conv_w=v_conv_w, conv_b=v_conv_b, b_ra=v_b_ra,
                   b_ri=v_b_ri, lru_lambda=v_lru_lambda, sinks=v_sinks, b_o=v_b_o, ln_g=v_ln_g, ln_b=v_ln_b,
                   meta_tokens=v_meta_tokens)
    g_small_pack = jnp.concatenate([small_g[n] for n in small_names] + [jnp.zeros((2, D), F32)], axis=0)
    small_res = _adamw(small_pack(w_small), g_small_pack, small_pack(m_small), small_pack(v_small),
                       name="adamw_small", tr=24)

    small_rows = {}
    r0 = 0
    for n in small_names:
        nrows = small_g[n].shape[0]
        small_rows[n] = (r0, nrows)
        r0 += nrows

    def small_out(packed, n, like):
        a, nrows = small_rows[n]
        flat = packed[a:a + nrows].reshape(-1)
        return flat[:like.size].reshape(like.shape)

    weights = dict(meta_tokens=meta_tokens, ln_emb_g=ln_emb_g, ln_emb_b=ln_emb_b, w_in=w_in, b_in=b_in, conv_w=conv_w,
                   conv_b=conv_b, w_ra=w_ra, b_ra=b_ra, w_ri=w_ri, b_ri=b_ri, lru_lambda=lru_lambda, sinks=sinks,
                   w_rnn_out=w_rnn_out, w_attn_out=w_attn_out, w_o=w_o, b_o=b_o, ln_g=ln_g, ln_b=ln_b)

    def outputs(which):
        return [big[n][which] if n in big else small_out(small_res[which], n, like) for n, like in weights.items()]

    return (loss, g_x[None], *outputs(0), *outputs(1), *outputs(2), *outputs(3))
```

```python
import jax
import jax.numpy as jnp
from jax import lax
from jax.experimental import pallas as pl
from jax.experimental.pallas import tpu as pltpu

F32 = jnp.float32
BF16 = jnp.bfloat16
_MXU = jnp.bfloat16

D = 2048
SEQ = 2048
N_META = 16
BLK = 128
PAD = BLK - N_META
TP = PAD + N_META + SEQ
NBLK = TP // BLK
HALF_TP = TP // 2
N_RB = 8
RB = 256
CONV_W = 4
LRU_C = 8.0
HD = 64
N_Q = 32
N_KV = 4
GRP = 8
D_KV = 256
NEG_INF = -1e30
LN_EPS = 1e-5
ALPHA = 2.0 ** 0.25
ROPE_THETA = 10000.0
OFF_GR, OFF_Q, OFF_K, OFF_V, OFF_GA, OFF_G = 2048, 4096, 6144, 6400, 6656, 8704
D_IN = 12800
N_SH = 4
W_IN_COLS = D_IN // N_SH
TN_IN = 640
PER_IN = W_IN_COLS // TN_IN
CW = 512
RT = TP // 4
SQ_ROWS = 512
PACK_ROWS = 3 * SQ_ROWS + 128

ADAM_LR = 0.001
ADAM_B1 = 0.9
ADAM_B2 = 0.999
ADAM_EPS = 1e-08
ADAM_WD = 0.01
ADAM_STEP = 10

MESH = pl.DeviceIdType.MESH
_MIB = 1024 * 1024
_ANY = pl.BlockSpec(memory_space=pl.ANY)
_NT = (((1,), (1,)), ((), ()))
_TN = (((0,), (0,)), ((), ()))


def _sds(shape, dtype):
    return jax.ShapeDtypeStruct(shape, dtype)


def _sigmoid(x):
    return 1.0 / (1.0 + jnp.exp(-x))


def _my_pos():
    return lax.axis_index("x"), lax.axis_index("y"), lax.axis_index("c")


def _other_chips(x, y):
    return [(1 - x, y), (x, 1 - y), (1 - x, 1 - y)]


def _remote(src, dst, send_sems, recv_sems, k, dev):
    return pltpu.make_async_remote_copy(src_ref=src, dst_ref=dst, send_sem=send_sems.at[k], recv_sem=recv_sems.at[k],
                                        device_id=dev, device_id_type=MESH)


class _Hook:
    def __init__(self, carried, landing, n_sems, start, finish, mid=None, mid_frac=0.5):
        self.carried, self.landing, self.n_sems, self.start, self.finish = list(carried), list(landing), n_sems, start, finish
        self.mid, self.mid_frac = mid, mid_frac


def _hook_gather(buf, mid_frac):
    half = buf.shape[1] // 2
    quarter = half // 2

    def geom(o, ss, rs):
        x, y, c = _my_pos()
        xn, yn, dg = _other_chips(x, y)
        slot = lambda p: 2 * p[0] + p[1]
        mine_rows = pl.ds(pl.multiple_of(c * half, 16), half)
        sib_rows = pl.ds(pl.multiple_of((1 - c) * half, 16), half)
        q_rows = lambda r: pl.ds(pl.multiple_of(c * half + r * quarter, 16), quarter)

        def cp(k, s, rows, dev):
            part = o.at[s, rows]
            return _remote(part, part, ss, rs, k, dev)

        return dict(
            direct=lambda k, s: cp(k, s, mine_rows, ((xn, yn)[k][0], (xn, yn)[k][1], c)),
            relay=lambda r, s: cp(2 + r, s, q_rows(r), ((yn, xn)[r][0], (yn, xn)[r][1], c)),
            sibling=lambda k, s, mine: cp(4 + k, s, mine_rows if mine else sib_rows, (x, y, 1 - c)),
            me=slot((x, y)), slots=(slot(xn), slot(yn), slot(dg)))

    def start(car, land, ss, rs):
        g = geom(car[0], ss, rs)
        g["direct"](0, g["me"]).start()
        g["direct"](1, g["me"]).start()

    def mid(car, land, ss, rs):
        g = geom(car[0], ss, rs)
        for k in range(2):
            g["direct"](k, g["slots"][k]).wait_recv()
            g["relay"](k, g["slots"][k]).start()
            g["sibling"](k, g["slots"][k], True).start()

    def finish(car, land, ss, rs):
        g = geom(car[0], ss, rs)
        dslot = g["slots"][2]
        g["relay"](0, dslot).wait_recv()
        g["relay"](1, dslot).wait_recv()
        g["sibling"](2, dslot, True).start()
        for k in range(3):
            g["sibling"](k, g["slots"][k], False).wait_recv()
        for k in range(2):
            g["direct"](k, g["me"]).wait_send()
            g["relay"](k, g["slots"][k]).wait_send()
        for k in range(3):
            g["sibling"](k, g["slots"][k], True).wait_send()

    return _Hook([buf], [], 7, start, finish, mid=mid, mid_frac=mid_frac)


def _hook_pair(g, half_rows):
    n, r, cc = g.shape
    h = r // 2 if half_rows else r

    def plan(car, land, ss, rs):
        x, y, c = _my_pos()
        cps = []
        for s in range(n):
            src = car[0].at[s, pl.ds(pl.multiple_of((1 - c) * h, 8), h)] if half_rows else car[0].at[s]
            cps.append(_remote(src, land[0].at[s], ss, rs, s, (x, y, 1 - c)))
        return cps

    def start(car, land, ss, rs):
        for cp in plan(car, land, ss, rs):
            cp.start()

    def finish(car, land, ss, rs):
        cps = plan(car, land, ss, rs)
        for cp in cps:
            cp.wait_recv()
        for cp in cps:
            cp.wait_send()

    return _Hook([g], [_sds((n, h, cc), g.dtype)], n, start, finish)


def _hook_scatter_direct(s16):
    _, h, cc = s16.shape
    q = h // 2

    def plan(car, land, ss, rs):
        x, y, c = _my_pos()
        xn, yn, dg = _other_chips(x, y)
        s, (final, raw) = car[0], land
        slot = lambda p: 2 * p[0] + p[1]
        q0, q1 = pl.ds(0, q), pl.ds(q, q)
        return [_remote(s.at[slot(xn), q0], final.at[0, q0], ss, rs, 0, (xn[0], xn[1], c)),
                _remote(s.at[slot(yn), q1], final.at[1, q1], ss, rs, 1, (yn[0], yn[1], c)),
                _remote(s.at[slot(dg), q0], raw.at[1], ss, rs, 2, (xn[0], xn[1], c)),
                _remote(s.at[slot(dg), q1], raw.at[0], ss, rs, 3, (yn[0], yn[1], c))]

    def start(car, land, ss, rs):
        for cp in plan(car, land, ss, rs):
            cp.start()

    def finish(car, land, ss, rs):
        cps = plan(car, land, ss, rs)
        for cp in cps:
            cp.wait_recv()
        for cp in cps:
            cp.wait_send()

    return _Hook([s16], [_sds((2, h, cc), s16.dtype), _sds((2, q, cc), s16.dtype)], 4, start, finish)


def _hook_scatter_relay(comb, final):
    _, q, _ = comb.shape

    def plan(car, ss, rs):
        x, y, c = _my_pos()
        xn, yn, _ = _other_chips(x, y)
        cb, final_ref = car
        return [_remote(cb.at[0], final_ref.at[0, pl.ds(q, q)], ss, rs, 0, (xn[0], xn[1], c)),
                _remote(cb.at[1], final_ref.at[1, pl.ds(0, q)], ss, rs, 1, (yn[0], yn[1], c))]

    def start(car, land, ss, rs):
        for cp in plan(car, ss, rs):
            cp.start()

    def finish(car, land, ss, rs):
        cps = plan(car, ss, rs)
        for cp in cps:
            cp.wait_recv()
        for cp in cps:
            cp.wait_send()

    return _Hook([comb, final], [], 2, start, finish)


def _hook_halves(full):
    h = full.shape[0] // 2

    def half_copy(car, ss, rs, which):
        x, y, c = _my_pos()
        rows = car[0].at[pl.ds(pl.multiple_of((c + which - 2 * c * which) * h, 8), h)]
        return _remote(rows, rows, ss, rs, 0, (x, y, 1 - c))

    def start(car, land, ss, rs):
        half_copy(car, ss, rs, 0).start()

    def finish(car, land, ss, rs):
        half_copy(car, ss, rs, 1).wait_recv()
        half_copy(car, ss, rs, 0).wait_send()

    return _Hook([full], [], 1, start, finish)


def _pcall(body, *, name, grid, in_specs, out_specs, out_shape, operands, scratch=(), vmem=48, sem=None,
           prefetch=(), aliases=None, hook=None):
    n_pre, n_in, n_out, n_scr = len(prefetch), len(in_specs), len(out_specs), len(scratch)
    in_specs, out_specs, out_shape, scratch = list(in_specs), list(out_specs), list(out_shape), list(scratch)
    io_alias = {n_pre + a: b for a, b in (aliases or {}).items()}
    operands = list(operands)
    kernel_body = body
    if hook is not None:
        n_car, n_land = len(hook.carried), len(hook.landing)
        for t, arr in enumerate(hook.carried):
            io_alias[n_pre + n_in + t] = n_out + t
        in_specs += [_ANY] * n_car
        out_specs += [_ANY] * (n_car + n_land)
        out_shape += [_sds(a.shape, a.dtype) for a in hook.carried] + hook.landing
        scratch += [pltpu.SemaphoreType.DMA((hook.n_sems,)), pltpu.SemaphoreType.DMA((hook.n_sems,))]
        operands += hook.carried
        sem = ("arbitrary",) * len(grid)

        def kernel_body(*refs):
            pre, rest = refs[:n_pre], refs[n_pre:]
            ins = rest[:n_in]
            outs = rest[n_in + n_car:n_in + n_car + n_out]
            car = rest[n_in + n_car + n_out:n_in + 2 * n_car + n_out]
            land = rest[n_in + 2 * n_car + n_out:n_in + 2 * n_car + n_out + n_land]
            scr = rest[n_in + 2 * n_car + n_out + n_land:]
            send_sems, recv_sems = scr[n_scr], scr[n_scr + 1]
            first = pl.program_id(0) == 0
            last = pl.program_id(0) == grid[0] - 1
            for d in range(1, len(grid)):
                first = first & (pl.program_id(d) == 0)
                last = last & (pl.program_id(d) == grid[d] - 1)

            @pl.when(first)
            def _():
                hook.start(car, land, send_sems, recv_sems)

            if hook.mid is not None:
                step = pl.program_id(0)
                total = grid[0]
                for d in range(1, len(grid)):
                    step = step * grid[d] + pl.program_id(d)
                    total *= grid[d]

                @pl.when(step == int(total * hook.mid_frac))
                def _():
                    hook.mid(car, land, send_sems, recv_sems)

            body(*pre, *ins, *outs, *scr[:n_scr])

            @pl.when(last)
            def _():
                hook.finish(car, land, send_sems, recv_sems)

    params = pltpu.CompilerParams(vmem_limit_bytes=vmem * _MIB, dimension_semantics=sem,
                                  has_side_effects=hook is not None)
    if n_pre:
        call = pl.pallas_call(
            kernel_body, name=name, out_shape=out_shape, input_output_aliases=io_alias, compiler_params=params,
            grid_spec=pltpu.PrefetchScalarGridSpec(num_scalar_prefetch=n_pre, grid=grid, in_specs=in_specs,
                                                   out_specs=out_specs, scratch_shapes=scratch))
    else:
        call = pl.pallas_call(kernel_body, name=name, grid=grid, in_specs=in_specs, out_specs=out_specs,
                              out_shape=out_shape, scratch_shapes=scratch, input_output_aliases=io_alias,
                              compiler_params=params)
    return call(*prefetch, *operands)


def _gather_small(blk):
    r, cc = blk.shape

    def body(x_ref, o_ref, send_sems, recv_sems):
        x, y, c = _my_pos()
        me = 2 * x + y
        o_ref[me] = x_ref[...]
        sends = [_remote(x_ref, o_ref.at[me], send_sems, recv_sems, k, (px, py, c))
                 for k, (px, py) in enumerate(_other_chips(x, y))]
        for cp in sends:
            cp.start()
        for k, (px, py) in enumerate(_other_chips(x, y)):
            _remote(x_ref, o_ref.at[2 * px + py], send_sems, recv_sems, k, (px, py, c)).wait_recv()
        for cp in sends:
            cp.wait_send()

    vm = pl.BlockSpec(memory_space=pltpu.VMEM)
    return pl.pallas_call(
        body, name="gather_small", in_specs=[vm], out_specs=vm, out_shape=_sds((N_SH, r, cc), blk.dtype),
        scratch_shapes=[pltpu.SemaphoreType.DMA((3,)), pltpu.SemaphoreType.DMA((3,))],
        compiler_params=pltpu.CompilerParams(has_side_effects=True),
    )(blk)


N_DEV = 8


def _allreduce_small(pack, halves):
    r, cc = pack.shape
    ride = _hook_halves(halves)

    def body(x_ref, h_in_ref, o_ref, h_ref, buf_ref, send_sems, recv_sems, h_ss, h_rs):
        del h_in_ref
        ride.start([h_ref], [], h_ss, h_rs)
        x, y, c = _my_pos()
        me = 4 * x + 2 * y + c
        buf_ref[me] = x_ref[...]
        cps = []
        for k in range(1, N_DEV):
            peer = (x ^ ((k >> 2) & 1), y ^ ((k >> 1) & 1), c ^ (k & 1))
            cps.append(_remote(x_ref, buf_ref.at[me], send_sems, recv_sems, k - 1, peer))
        for cp in cps:
            cp.start()
        for k in range(1, N_DEV):
            peer = (x ^ ((k >> 2) & 1), y ^ ((k >> 1) & 1), c ^ (k & 1))
            src = 4 * peer[0] + 2 * peer[1] + peer[2]
            _remote(x_ref, buf_ref.at[src], send_sems, recv_sems, k - 1, peer).wait_recv()
        acc = buf_ref[0]
        for d in range(1, N_DEV):
            acc = acc + buf_ref[d]
        o_ref[...] = acc
        for cp in cps:
            cp.wait_send()
        ride.finish([h_ref], [], h_ss, h_rs)

    vm = pl.BlockSpec(memory_space=pltpu.VMEM)
    return pl.pallas_call(
        body, name="allreduce_small", in_specs=[vm, _ANY], out_specs=[vm, _ANY],
        out_shape=[_sds((r, cc), F32), _sds(halves.shape, halves.dtype)], input_output_aliases={1: 1},
        scratch_shapes=[pltpu.VMEM((N_DEV, r, cc), F32), pltpu.SemaphoreType.DMA((N_DEV - 1,)),
                        pltpu.SemaphoreType.DMA((N_DEV - 1,)), pltpu.SemaphoreType.DMA((1,)), pltpu.SemaphoreType.DMA((1,))],
        compiler_params=pltpu.CompilerParams(has_side_effects=True),
    )(pack, halves)


def _mm_nn(a, w, bias, *, name, grid, tm, tn, k, a_index, w_block, w_index, out_cols, out_index, carried=None, hook=None):
    m = a.shape[0]

    def body(a_ref, w_ref, b_ref, *rest):
        o_ref = rest[-1]
        wv = w_ref[...]
        acc = jnp.dot(a_ref[...].astype(_MXU), wv.reshape(k, tn).astype(_MXU), preferred_element_type=F32)
        o_ref[...] = acc + b_ref[...]

    operands = [a, w, bias] + ([carried] if carried is not None else [])
    return _pcall(
        body, name=name, grid=grid,
        in_specs=[pl.BlockSpec((tm, k), a_index), pl.BlockSpec(w_block, w_index),
                  pl.BlockSpec((1, tn), lambda i, j: (0, j))] + ([_ANY] if carried is not None else []),
        out_specs=[pl.BlockSpec((tm, tn), out_index)], out_shape=[_sds((m, out_cols), F32)], operands=operands,
        vmem=56, sem=("parallel", "parallel"), aliases={3: 0} if carried is not None else None, hook=hook)


def _mm_nt(a, w, *, name, grid, tm, tn, tk, a_index, w_block, w_index, out_cols, hook=None):
    m = a.shape[0]
    nk = grid[2]

    def body(a_ref, w_ref, o_ref, acc_ref):
        part = lax.dot_general(a_ref[...].astype(_MXU), w_ref[...].astype(_MXU), _NT, preferred_element_type=F32)
        if nk == 1:
            o_ref[...] = part
        else:
            kidx = pl.program_id(2)

            @pl.when(kidx == 0)
            def _():
                acc_ref[...] = part

            @pl.when(kidx > 0)
            def _():
                acc_ref[...] += part

            @pl.when(kidx == nk - 1)
            def _():
                o_ref[...] = acc_ref[...]

    return _pcall(
        body, name=name, grid=grid, in_specs=[pl.BlockSpec((tm, tk), a_index), pl.BlockSpec(w_block, w_index)],
        out_specs=[pl.BlockSpec((tm, tn), lambda i, j, q: (i, j))], out_shape=[_sds((m, out_cols), F32)],
        operands=[a, w], scratch=[pltpu.VMEM((tm, tn) if nk > 1 else (8, 128), F32)], vmem=60,
        sem=("parallel", "parallel", "arbitrary"), hook=hook)


def _mm_tn(at, b, *, name, grid, tm, tn, a_index, b_index, out_shape, out_block, out_index, carried=None,
           prefetch=(), hook=None, out_dtype=F32, colsum=False):
    t = at.shape[1]
    n_pre = len(prefetch)

    def body(*refs):
        a_ref, b_ref = refs[n_pre], refs[n_pre + 1]
        bv = b_ref[...]
        o_ref = refs[-2] if colsum else refs[-1]
        o_ref[...] = jnp.dot(a_ref[...].astype(_MXU), bv.astype(_MXU), preferred_element_type=F32).astype(out_dtype)
        if colsum:
            refs[-1][...] = jnp.sum(bv.astype(F32), axis=0, keepdims=True)

    operands = [at, b] + ([carried] if carried is not None else [])
    out_specs = [pl.BlockSpec(out_block, out_index)]
    out_shapes = [_sds(out_shape, out_dtype)]
    if colsum:
        out_specs.append(pl.BlockSpec((1, tn), b_index))
        out_shapes.append(_sds((1, b.shape[1]), F32))
    return _pcall(
        body, name=name, grid=grid,
        in_specs=[pl.BlockSpec((tm, t), a_index), pl.BlockSpec((t, tn), b_index)] + ([_ANY] if carried is not None else []),
        out_specs=out_specs, out_shape=out_shapes, operands=operands,
        vmem=56, sem=("parallel", "parallel"), aliases={2: 0} if carried is not None else None, prefetch=prefetch, hook=hook)


def _remote_tile(n):
    near = 2 * PER_IN
    if isinstance(n, int):
        return (n % 2, n // 2) if n < near else (2, n - near)
    return jnp.where(n < near, n % 2, 2), jnp.where(n < near, n // 2, n - near)


def _mm_z_gather(h16, w_own, b_in, order):
    n_tiles = N_SH * PER_IN
    n_remote = 3 * PER_IN
    half = D // 2

    def body(order_ref, a_ref, b_ref, w_in_ref, z_ref, w_ref, wbuf, tile_sems, ss, rs):
        del w_in_ref
        j = pl.program_id(0)
        x, y, c = _my_pos()
        me = 2 * x + y
        chips = _other_chips(x, y)
        mine_rows = pl.ds(pl.multiple_of(c * half, 16), half)
        sib_rows = pl.ds(pl.multiple_of((1 - c) * half, 16), half)

        slots = [2 * px + py for px, py in chips]
        cols = lambda t: pl.ds(t * TN_IN, TN_IN)
        q_rows = lambda r: pl.ds(pl.multiple_of(c * half + r * (half // 2), 16), half // 2)

        def direct(rel, t, slot):
            px, py = chips[rel]
            part = w_ref.at[slot, mine_rows, cols(t)]
            return _remote(part, part, ss, rs, 2 * t + rel, (px, py, c))

        def relay(r, t, slot):
            px, py = chips[1 - r]
            part = w_ref.at[slot, q_rows(r), cols(t)]
            return _remote(part, part, ss, rs, 2 * PER_IN + 2 * t + r, (px, py, c))

        def d2d(n, rows):
            rel, t = _remote_tile(n)
            part = w_ref.at[slots[rel], rows, cols(t)]
            return _remote(part, part, ss, rs, 4 * PER_IN + n, (x, y, 1 - c))

        def tile_copy(step):
            rel, t = _remote_tile(jnp.maximum(step - PER_IN, 0))
            slot = jnp.where(step < PER_IN, me, me ^ jnp.where(rel == 0, 2, jnp.where(rel == 1, 1, 3)))
            col = pl.multiple_of(jnp.where(step < PER_IN, step, t) * TN_IN, 128)
            return pltpu.make_async_copy(w_ref.at[slot, :, pl.ds(col, TN_IN)], wbuf.at[step % 2], tile_sems.at[step % 2])

        @pl.when(j == 0)
        def _():
            for t in range(PER_IN):
                direct(0, t, me).start()
                direct(1, t, me).start()
            tile_copy(0).start()

        for n in range(n_remote):
            rel, t = _remote_tile(n)

            @pl.when(j == n + PER_IN - 3)
            def _():
                if rel < 2:
                    direct(rel, t, slots[rel]).wait_recv()
                    relay(rel, t, slots[rel]).start()
                else:
                    relay(0, t, slots[2]).wait_recv()
                    relay(1, t, slots[2]).wait_recv()
                d2d(n, mine_rows).start()

            @pl.when(j == n + PER_IN - 2)
            def _():
                d2d(n, sib_rows).wait_recv()

        @pl.when(j + 1 < n_tiles)
        def _():
            tile_copy(j + 1).start()

        tile_copy(j).wait()
        z_ref[...] = jnp.dot(a_ref[...], wbuf[j % 2], preferred_element_type=F32) + b_ref[...]

        @pl.when(j == n_tiles - 1)
        def _():
            for t in range(PER_IN):
                for r in range(2):
                    direct(r, t, me).wait_send()
                    relay(r, t, slots[r]).wait_send()
            for n in range(n_remote):
                d2d(n, mine_rows).wait_send()

    def col_tile(j, o):
        rel, t = _remote_tile(jnp.maximum(j - PER_IN, 0))
        return 0, jnp.where(j < PER_IN, o[0] * PER_IN + j, o[1 + rel] * PER_IN + t)

    return pl.pallas_call(
        body, name="mm_z_gather",
        grid_spec=pltpu.PrefetchScalarGridSpec(
            num_scalar_prefetch=1, grid=(n_tiles,),
            in_specs=[pl.BlockSpec((TP, D), lambda j, o: (0, 0)), pl.BlockSpec((1, TN_IN), col_tile), _ANY],
            out_specs=[pl.BlockSpec((TP, TN_IN), col_tile), _ANY],
            scratch_shapes=[pltpu.VMEM((2, D, TN_IN), BF16), pltpu.SemaphoreType.DMA((2,)),
                            pltpu.SemaphoreType.DMA((4 * PER_IN + n_remote,)),
                            pltpu.SemaphoreType.DMA((4 * PER_IN + n_remote,))]),
        out_shape=[_sds((TP, D_IN), F32), _sds(w_own.shape, w_own.dtype)],
        input_output_aliases={3: 1},
        compiler_params=pltpu.CompilerParams(vmem_limit_bytes=60 * _MIB, dimension_semantics=("arbitrary",),
                                             has_side_effects=True),
    )(order, h16, b_in, w_own)


def _padded_rows(i, x_ref, meta_ref):
    head = jnp.concatenate([jnp.zeros((PAD, D), F32), meta_ref[...]], axis=0)
    return jnp.where(i == 0, head, x_ref[...])


def _stream_specs():
    return [pl.BlockSpec((BLK, D), lambda i: (jnp.maximum(i - 1, 0), 0)), pl.BlockSpec((N_META, D), lambda i: (0, 0))]


def _ln_emb_fwd(x, meta, g, b):
    def body(x_ref, meta_ref, g_ref, b_ref, h32_ref, h16_ref, h16t_ref):
        x = _padded_rows(pl.program_id(0), x_ref, meta_ref)
        mu = jnp.mean(x, axis=-1, keepdims=True)
        xc = x - mu
        var = jnp.mean(xc * xc, axis=-1, keepdims=True)
        y = xc * lax.rsqrt(var + LN_EPS) * g_ref[...] + b_ref[...]
        h32_ref[...] = y
        h16_ref[...] = y.astype(_MXU)
        h16t_ref[...] = y.T.astype(_MXU)

    row = pl.BlockSpec((BLK, D), lambda i: (i, 0))
    vec = pl.BlockSpec((1, D), lambda i: (0, 0))
    return _pcall(body, name="ln_emb_fwd", grid=(NBLK,), in_specs=_stream_specs() + [vec, vec],
                  out_specs=[row, row, pl.BlockSpec((D, BLK), lambda i: (0, i))],
                  out_shape=[_sds((TP, D), F32), _sds((TP, D), _MXU), _sds((D, TP), _MXU)], operands=[x, meta, g, b],
                  vmem=32, sem=("parallel",))


def _ln_emb_bwd(x, meta, g, dr, dhz, *, hook):
    def body(x_ref, meta_ref, g_ref, dr_ref, dhz_ref, dx_ref, dmeta_ref, acc_ref):
        i = pl.program_id(0)

        @pl.when(i == 0)
        def _():
            acc_ref[...] = jnp.zeros_like(acc_ref)

        x = _padded_rows(i, x_ref, meta_ref)
        mu = jnp.mean(x, axis=-1, keepdims=True)
        xc = x - mu
        var = jnp.mean(xc * xc, axis=-1, keepdims=True)
        rstd = lax.rsqrt(var + LN_EPS)
        xhat = xc * rstd
        dh = ALPHA * dr_ref[...] + dhz_ref[...]
        acc_ref[0:1, :] += jnp.sum(dh * xhat, axis=0, keepdims=True)
        acc_ref[1:2, :] += jnp.sum(dh, axis=0, keepdims=True)
        dxh = dh * g_ref[...]
        m1 = jnp.mean(dxh, axis=-1, keepdims=True)
        m2 = jnp.mean(dxh * xhat, axis=-1, keepdims=True)
        dx = rstd * (dxh - m1 - xhat * m2)
        dx_ref[...] = dx

        @pl.when(i == 0)
        def _():
            dmeta_ref[...] = dx[PAD:BLK]

    row = pl.BlockSpec((BLK, D), lambda i: (i, 0))
    vec = pl.BlockSpec((1, D), lambda i: (0, 0))
    xs, ms = _stream_specs()
    return _pcall(body, name="ln_emb_bwd", grid=(NBLK,), in_specs=[xs, ms, vec, row, row],
                  out_specs=[xs, ms, pl.BlockSpec((8, D), lambda i: (0, 0))],
                  out_shape=[_sds((SEQ, D), F32), _sds((N_META, D), F32), _sds((8, D), F32)],
                  operands=[x, meta, g, dr, dhz], vmem=32, sem=("arbitrary",), hook=hook)


def _mul_silu_fwd(a, z, off, *, name):
    def body(a_ref, z_ref, o_ref, t_ref):
        zz = z_ref[...]
        y = a_ref[...] * (zz * _sigmoid(zz))
        o_ref[...] = y.astype(_MXU)
        t_ref[...] = y.T.astype(_MXU)

    strip = pl.BlockSpec((TP, CW), lambda j: (0, j))
    return _pcall(body, name=name, grid=(D // CW,),
                  in_specs=[strip, pl.BlockSpec((TP, CW), lambda j: (0, off // CW + j))],
                  out_specs=[strip, pl.BlockSpec((CW, TP), lambda j: (j, 0))],
                  out_shape=[_sds((TP, D), _MXU), _sds((D, TP), _MXU)], operands=[a, z], vmem=56, sem=("parallel",))


def _mul_silu_bwd(dy, a, z, off, dz, *, name, hook=None):
    def body(dy_ref, a_ref, z_ref, dz_in, da_ref, dg_ref):
        zz = z_ref[...]
        sg = _sigmoid(zz)
        d = dy_ref[...]
        da_ref[...] = d * (zz * sg)
        dg_ref[...] = (d * a_ref[...] * (sg * (1.0 + zz * (1.0 - sg)))).astype(_MXU)

    blk = pl.BlockSpec((RT, CW), lambda i, j: (i, j))
    zblk = pl.BlockSpec((RT, CW), lambda i, j: (i, off // CW + j))
    return _pcall(body, name=name, grid=(TP // RT, D // CW), in_specs=[blk, blk, zblk, _ANY], out_specs=[blk, zblk],
                  out_shape=[_sds((TP, D), F32), _sds((TP, D_IN), _MXU)], operands=[dy, a, z, dz], vmem=32,
                  sem=("parallel", "parallel"), aliases={3: 1}, hook=hook)


def _merge_fwd(y2, z):
    w = 256

    def body(ya_ref, yb_ref, ga_ref, gb_ref, o_ref, t_ref):
        y = _sigmoid(ga_ref[...]) * ya_ref[...] + _sigmoid(gb_ref[...]) * yb_ref[...]
        o_ref[...] = y.astype(_MXU)
        t_ref[...] = y.T.astype(_MXU)

    nb = D // w
    strip = pl.BlockSpec((TP, w), lambda j: (0, j))
    return _pcall(body, name="merge_fwd", grid=(nb,),
                  in_specs=[strip, pl.BlockSpec((TP, w), lambda j: (0, nb + j)),
                            pl.BlockSpec((TP, w), lambda j: (0, OFF_G // w + j)),
                            pl.BlockSpec((TP, w), lambda j: (0, (OFF_G + D) // w + j))],
                  out_specs=[strip, pl.BlockSpec((w, TP), lambda j: (j, 0))],
                  out_shape=[_sds((TP, D), _MXU), _sds((D, TP), _MXU)], operands=[y2, y2, z, z], vmem=56,
                  sem=("parallel",))


def _merge_bwd(dmix, y2, z):
    nb = D // CW

    def body(dm_ref, y_ref, g_ref, dy_ref, dg_ref):
        dm = dm_ref[...]
        sg = _sigmoid(g_ref[...])
        dy_ref[...] = (dm * sg).astype(_MXU)
        dg_ref[...] = (dm * y_ref[...] * sg * (1.0 - sg)).astype(_MXU)

    blk = pl.BlockSpec((RT, CW), lambda i, j: (i, j))
    gblk = pl.BlockSpec((RT, CW), lambda i, j: (i, OFF_G // CW + j))
    return _pcall(body, name="merge_bwd", grid=(TP // RT, 2 * nb),
                  in_specs=[pl.BlockSpec((RT, CW), lambda i, j: (i, j % nb)), blk, gblk], out_specs=[blk, gblk],
                  out_shape=[_sds((TP, 2 * D), _MXU), _sds((TP, D_IN), _MXU)], operands=[dmix, y2, z], vmem=32,
                  sem=("parallel", "parallel"))


def _ln_out_loss(h32, out, target, g, b):
    def body(h_ref, o_ref, t_ref, g_ref, b_ref, dr_ref, dr16_ref, acc_ref):
        i = pl.program_id(0)

        @pl.when(i == 0)
        def _():
            acc_ref[...] = jnp.zeros_like(acc_ref)

        r = ALPHA * h_ref[...] + o_ref[...]
        mu = jnp.mean(r, axis=-1, keepdims=True)
        rc = r - mu
        var = jnp.mean(rc * rc, axis=-1, keepdims=True)
        rstd = lax.rsqrt(var + LN_EPS)
        xhat = rc * rstd
        gg = g_ref[...]
        y = xhat * gg + b_ref[...]
        real = (i >= 1).astype(F32)
        diff = (y - t_ref[...]) * real
        dy = diff * (1.0 / D)
        dxh = dy * gg
        m1 = jnp.mean(dxh, axis=-1, keepdims=True)
        m2 = jnp.mean(dxh * xhat, axis=-1, keepdims=True)
        dr = rstd * (dxh - m1 - xhat * m2)
        dr_ref[...] = dr
        dr16_ref[...] = dr.astype(_MXU)
        acc_ref[0:1, :] += jnp.sum(dy * xhat, axis=0, keepdims=True)
        acc_ref[1:2, :] += jnp.sum(dy, axis=0, keepdims=True)
        acc_ref[2:3, :] += jnp.sum(dr, axis=0, keepdims=True)
        acc_ref[3:4, :] += (0.5 / D) * jnp.sum(diff * diff)

    row = pl.BlockSpec((BLK, D), lambda i: (i, 0))
    vec = pl.BlockSpec((1, D), lambda i: (0, 0))
    return _pcall(body, name="ln_out_loss", grid=(NBLK,),
                  in_specs=[row, row, pl.BlockSpec((BLK, D), lambda i: (jnp.maximum(i - 1, 0), 0)), vec, vec],
                  out_specs=[row, row, pl.BlockSpec((8, D), lambda i: (0, 0))],
                  out_shape=[_sds((TP, D), F32), _sds((TP, D), _MXU), _sds((8, D), F32)],
                  operands=[h32, out, target, g, b], vmem=32, sem=("arbitrary",))


def _rnn_recompute(xr_ref, cw_ref, cb_ref, wra_ref, wri_ref, bra_ref, bri_ref, lam_ref):
    rows = lax.broadcasted_iota(jnp.int32, (TP, 1), 0)
    valid = (rows >= PAD).astype(F32)
    first = rows == PAD
    x = xr_ref[...] * valid
    cw = cw_ref[...]
    shifted = [x, pltpu.roll(x, 1, 0), pltpu.roll(x, 2, 0), pltpu.roll(x, 3, 0)]
    c = cb_ref[...] + cw[0:1, :] * shifted[0] + cw[1:2, :] * shifted[1] + cw[2:3, :] * shifted[2] + cw[3:4, :] * shifted[3]
    cm = c.astype(_MXU)
    gr = _sigmoid(jnp.dot(cm, wra_ref[...].astype(_MXU), preferred_element_type=F32) + bra_ref[...])
    gi = _sigmoid(jnp.dot(cm, wri_ref[...].astype(_MXU), preferred_element_type=F32) + bri_ref[...])
    lam = lam_ref[...]
    ls = jnp.minimum(lam, 0.0) - jnp.log(1.0 + jnp.exp(-jnp.abs(lam)))
    log_a = LRU_C * gr * ls
    a = jnp.exp(log_a)
    mult = jnp.where(first, 1.0, jnp.sqrt(1.0 - jnp.exp(2.0 * log_a)))
    return dict(valid=valid, first=first, shifted=shifted, c=c, cm=cm, gr=gr, gi=gi, ls=ls, a=a, mult=mult, lam=lam)


def _rnn_specs():
    col = pl.BlockSpec((TP, RB), lambda n: (0, n))
    vec = pl.BlockSpec((1, RB), lambda n: (0, n))
    return dict(col=col, vec=vec, cw=pl.BlockSpec((CONV_W, RB), lambda n: (0, n)),
                wblk=pl.BlockSpec((None, RB, RB), lambda n: (n, 0, 0)))


def _rnn_gates_fwd(z, conv_w, conv_b, w_ra, w_ri, b_ra, b_ri, lam):
    def body(xr_ref, cw_ref, cb_ref, wra_ref, wri_ref, bra_ref, bri_ref, lam_ref, a_ref, u_ref):
        r = _rnn_recompute(xr_ref, cw_ref, cb_ref, wra_ref, wri_ref, bra_ref, bri_ref, lam_ref)
        a_ref[...] = r["a"]
        u_ref[...] = r["mult"] * r["gi"] * r["c"] * r["valid"]

    s = _rnn_specs()
    return _pcall(body, name="rnn_gates_fwd", grid=(N_RB,),
                  in_specs=[s["col"], s["cw"], s["vec"], s["wblk"], s["wblk"], s["vec"], s["vec"], s["vec"]],
                  out_specs=[s["col"], s["col"]], out_shape=[_sds((TP, D), F32)] * 2,
                  operands=[z, conv_w, conv_b, w_ra, w_ri, b_ra, b_ri, lam], vmem=56, sem=("parallel",))


SCAN_ROWS = 272


SUB = 8


def _tile_scan(a, u, reverse):
    rows = lax.broadcasted_iota(jnp.int32, a.shape, 0)
    for d in (1, 2, 4):
        shift = SUB - d if reverse else d
        inside = (rows < SUB - d) if reverse else (rows >= d)
        u = u + a * jnp.where(inside, pltpu.roll(u, shift, 0), 0.0)
        a = a * jnp.where(inside, pltpu.roll(a, shift, 0), 1.0)
    return a, u


def _scan_fwd(a, u, *, hook):
    def body(a_ref, u_ref, h_ref, carry_ref):
        @pl.when(pl.program_id(1) == 0)
        def _():
            carry_ref[...] = jnp.zeros_like(carry_ref)

        def step(r, h):
            rows = pl.ds(pl.multiple_of(r * SUB, SUB), SUB)
            prod, part = _tile_scan(a_ref[rows, :], u_ref[rows, :], False)
            ht = part + prod * h
            h_ref[rows, :] = ht
            return ht[SUB - 1:SUB, :]

        carry_ref[...] = lax.fori_loop(0, SCAN_ROWS // SUB, step, carry_ref[...], unroll=2)

    blk = pl.BlockSpec((SCAN_ROWS, CW), lambda j, i: (i, j))
    return _pcall(body, name="scan_fwd", grid=(D // CW, TP // SCAN_ROWS), in_specs=[blk, blk], out_specs=[blk],
                  out_shape=[_sds((TP, D), F32)], operands=[a, u], scratch=[pltpu.VMEM((1, CW), F32)], vmem=32,
                  sem=("parallel", "arbitrary"), hook=hook)


def _scan_bwd(a, dh):
    nst = TP // SCAN_ROWS
    n_tiles = SCAN_ROWS // SUB

    def body(a_ref, d_ref, o_ref, lam_ref, anext_ref):
        @pl.when(pl.program_id(1) == 0)
        def _():
            lam_ref[...] = jnp.zeros_like(lam_ref)
            anext_ref[...] = jnp.zeros_like(anext_ref)

        def step(q, carry):
            lam_next, a_next = carry
            rows = pl.ds(pl.multiple_of((n_tiles - 1 - q) * SUB, SUB), SUB)
            at = a_ref[rows, :]
            last = lax.broadcasted_iota(jnp.int32, at.shape, 0) == SUB - 1
            b = jnp.where(last, a_next, pltpu.roll(at, SUB - 1, 0))
            prod, part = _tile_scan(b, d_ref[rows, :], True)
            lam = part + prod * lam_next
            o_ref[rows, :] = lam
            return lam[0:1, :], at[0:1, :]

        lam, an = lax.fori_loop(0, n_tiles, step, (lam_ref[...], anext_ref[...]), unroll=2)
        lam_ref[...] = lam
        anext_ref[...] = an

    blk = pl.BlockSpec((SCAN_ROWS, CW), lambda j, i: (nst - 1 - i, j))
    return _pcall(body, name="scan_bwd", grid=(D // CW, nst), in_specs=[blk, blk], out_specs=[blk],
                  out_shape=[_sds((TP, D), F32)], operands=[a, dh],
                  scratch=[pltpu.VMEM((1, CW), F32), pltpu.VMEM((1, CW), F32)], vmem=32,
                  sem=("parallel", "arbitrary"))[0]


def _rnn_gates_bwd(z, lam_s, hr, conv_w, conv_b, w_ra, w_ri, b_ra, b_ri, lam, dz, gsq):
    def body(xr_ref, ls_ref, hr_ref, cw_ref, cb_ref, wra_ref, wri_ref, bra_ref, bri_ref, lam_ref, dz_in, gsq_in,
             dx_ref, dw_ref, sums_ref):
        r = _rnn_recompute(xr_ref, cw_ref, cb_ref, wra_ref, wri_ref, bra_ref, bri_ref, lam_ref)
        valid, c, gr, gi, a, mult = r["valid"], r["c"], r["gr"], r["gi"], r["a"], r["mult"]
        du = ls_ref[...] * valid
        da = du * pltpu.roll(hr_ref[...], 1, 0)
        d_gi = du * mult * c
        dc = du * mult * gi
        dmult = du * gi * c
        dlog_a = da * a + jnp.where(r["first"], 0.0, -dmult * a * a / mult)
        d_gr = dlog_a * (LRU_C * r["ls"])
        dls = jnp.sum(dlog_a * (LRU_C * gr), axis=0, keepdims=True)
        dpre_r = d_gr * gr * (1.0 - gr)
        dpre_i = d_gi * gi * (1.0 - gi)
        pr = dpre_r.astype(_MXU)
        pi = dpre_i.astype(_MXU)
        dwra = lax.dot_general(r["cm"], pr, _TN, preferred_element_type=F32)
        dwri = lax.dot_general(r["cm"], pi, _TN, preferred_element_type=F32)
        for s in range(N_SH):
            dw_ref[s, 0:64, :] = dwra[64 * s:64 * (s + 1)]
            dw_ref[s, 64:128, :] = dwri[64 * s:64 * (s + 1)]
        dc = dc + lax.dot_general(pr, wra_ref[...].astype(_MXU), _NT, preferred_element_type=F32)
        dc = dc + lax.dot_general(pi, wri_ref[...].astype(_MXU), _NT, preferred_element_type=F32)
        cw = cw_ref[...]
        dx = cw[0:1, :] * dc
        for k in range(1, CONV_W):
            dx = dx + cw[k:k + 1, :] * pltpu.roll(dc, TP - k, 0)
        dx_ref[...] = (dx * valid).astype(_MXU)
        for k in range(CONV_W):
            sums_ref[k:k + 1, :] = jnp.sum(dc * r["shifted"][k], axis=0, keepdims=True)
        sums_ref[4:5, :] = jnp.sum(dc, axis=0, keepdims=True)
        sums_ref[5:6, :] = jnp.sum(dpre_r, axis=0, keepdims=True)
        sums_ref[6:7, :] = jnp.sum(dpre_i, axis=0, keepdims=True)
        sums_ref[7:8, :] = dls * _sigmoid(-r["lam"])

    s = _rnn_specs()
    return _pcall(
        body, name="rnn_gates_bwd", grid=(N_RB,),
        in_specs=[s["col"], s["col"], s["col"], s["cw"], s["vec"], s["wblk"], s["wblk"], s["vec"], s["vec"], s["vec"],
                  _ANY, _ANY],
        out_specs=[s["col"], pl.BlockSpec((N_SH, 128, RB), lambda n: (0, 3 * SQ_ROWS // 128, n)),
                   pl.BlockSpec((8, RB), lambda n: (0, n))],
        out_shape=[_sds((TP, D_IN), _MXU), _sds((N_SH, PACK_ROWS, D), F32), _sds((8, D), F32)],
        operands=[z, lam_s, hr, conv_w, conv_b, w_ra, w_ri, b_ra, b_ri, lam, dz, gsq], vmem=60, sem=("parallel",),
        aliases={10: 0, 11: 1})


def _rope_tables():
    half = HD // 2
    inv = ROPE_THETA ** (-jnp.arange(half, dtype=F32) / half)
    pos = (jnp.arange(TP) - PAD).astype(F32)
    ang = pos[:, None] * inv[None, :]
    return jnp.tile(jnp.cos(ang), (1, 4)), jnp.tile(jnp.sin(ang), (1, 4))


def _rope(x, cos_t, sin_t, sign):
    w = x.shape[1]
    lane = lax.broadcasted_iota(jnp.int32, x.shape, 1)
    first = (lane % HD) < (HD // 2)
    swapped = jnp.where(first, pltpu.roll(x, w - HD // 2, 1), pltpu.roll(x, HD // 2, 1))
    ct = jnp.tile(cos_t, (1, w // 128))
    st = jnp.tile(sin_t, (1, w // 128))
    return x * ct + swapped * jnp.where(first, -sign * st, sign * st)


def _rope_fwd(z, cos_t, sin_t):
    def body(q_ref, k_ref, v_ref, c_ref, s_ref, qo_ref, ko_ref, vo_ref):
        c = c_ref[...]
        s = s_ref[...]
        qo_ref[...] = _rope(q_ref[...], c, s, 1.0).astype(_MXU)
        ko_ref[...] = _rope(k_ref[...], c, s, 1.0).astype(_MXU)
        vo_ref[...] = v_ref[...].astype(_MXU)

    tab = pl.BlockSpec((BLK, 128), lambda i: (i, 0))
    kv = pl.BlockSpec((BLK, D_KV), lambda i: (i, 0))
    return _pcall(body, name="rope_fwd", grid=(NBLK,),
                  in_specs=[pl.BlockSpec((BLK, D), lambda i: (i, OFF_Q // D)),
                            pl.BlockSpec((BLK, D_KV), lambda i: (i, OFF_K // D_KV)),
                            pl.BlockSpec((BLK, D_KV), lambda i: (i, OFF_V // D_KV)), tab, tab],
                  out_specs=[pl.BlockSpec((BLK, D), lambda i: (i, 0)), kv, kv],
                  out_shape=[_sds((TP, D), _MXU), _sds((TP, D_KV), _MXU), _sds((TP, D_KV), _MXU)],
                  operands=[z, z, z, cos_t, sin_t], vmem=32, sem=("parallel",))


def _rope_bwd_k(dk, dv, cos_t, sin_t, dz):
    def body(dk_ref, dv_ref, c_ref, s_ref, dz_in, o_ref):
        o_ref[:, 0:D_KV] = _rope(dk_ref[...], c_ref[...], s_ref[...], -1.0).astype(_MXU)
        o_ref[:, D_KV:2 * D_KV] = dv_ref[...].astype(_MXU)

    tab = pl.BlockSpec((BLK, 128), lambda i: (i, 0))
    kv = pl.BlockSpec((BLK, D_KV), lambda i: (i, 0))
    return _pcall(body, name="rope_bwd_k", grid=(NBLK,), in_specs=[kv, kv, tab, tab, _ANY],
                  out_specs=[pl.BlockSpec((BLK, 2 * D_KV), lambda i: (i, OFF_K // (2 * D_KV)))],
                  out_shape=[_sds((TP, D_IN), _MXU)], operands=[dk, dv, cos_t, sin_t, dz], vmem=32, sem=("parallel",),
                  aliases={4: 0})[0]


def _attn_mask(i):
    ql = lax.broadcasted_iota(jnp.int32, (BLK, 3 * BLK), 0)
    kk = lax.broadcasted_iota(jnp.int32, (BLK, 3 * BLK), 1)
    kl = kk % BLK
    part = kk // BLK
    meta = (part == 0) & (kl >= PAD) & ((i >= 1) | (kl <= ql))
    prev = (part == 1) & (i >= 2) & (kl > ql)
    cur = (part == 2) & (i >= 1) & (kl <= ql)
    return meta | prev | cur


def _cat_kv(refs, g):
    return jnp.concatenate([r[:, HD * g:HD * (g + 1)] for r in refs], axis=0)


def _kv_specs():
    return [pl.BlockSpec((BLK, D_KV), lambda i: (0, 0)),
            pl.BlockSpec((BLK, D_KV), lambda i: (jnp.maximum(i - 1, 0), 0)),
            pl.BlockSpec((BLK, D_KV), lambda i: (i, 0))]


def _stack_heads(ref, g, width=HD):
    return jnp.concatenate([ref[:, width * (GRP * g + j):width * (GRP * g + j + 1)] for j in range(GRP)], axis=0)


def _attn_fwd(q, k, v, sinks, *, hook):
    def body(q_ref, k0_ref, kp_ref, kc_ref, v0_ref, vp_ref, vc_ref, sink_ref, o_ref, lse_ref, s_scr, p_scr):
        mask = _attn_mask(pl.program_id(0))
        for g in range(N_KV):
            s_scr[...] = lax.dot_general(_stack_heads(q_ref, g), _cat_kv((k0_ref, kp_ref, kc_ref), g), _NT,
                                         preferred_element_type=F32)
            for j in range(GRP):
                h = GRP * g + j
                rows = slice(BLK * j, BLK * (j + 1))
                sink = sink_ref[h]
                s = jnp.where(mask, s_scr[rows, :] * (HD ** -0.5), NEG_INF)
                mx = jnp.maximum(jnp.max(s, -1, keepdims=True), sink)
                p = jnp.exp(s - mx)
                den = jnp.sum(p, -1, keepdims=True) + jnp.exp(sink - mx)
                p_scr[rows, :] = (p * (1.0 / den)).astype(_MXU)
                lse_ref[:, h:h + 1] = mx + jnp.log(den)
            o8 = jnp.dot(p_scr[...], _cat_kv((v0_ref, vp_ref, vc_ref), g), preferred_element_type=F32)
            for j in range(GRP):
                h = GRP * g + j
                o_ref[:, HD * h:HD * (h + 1)] = o8[BLK * j:BLK * (j + 1)]

    return _pcall(body, name="attn_fwd", grid=(NBLK,),
                  in_specs=[pl.BlockSpec((BLK, D), lambda i: (i, 0))] + _kv_specs() + _kv_specs()
                           + [pl.BlockSpec(memory_space=pltpu.SMEM)],
                  out_specs=[pl.BlockSpec((BLK, D), lambda i: (i, 0)), pl.BlockSpec((BLK, N_Q), lambda i: (i, 0))],
                  out_shape=[_sds((TP, D), F32), _sds((TP, N_Q), F32)], operands=[q, k, k, k, v, v, v, sinks],
                  scratch=[pltpu.VMEM((GRP * BLK, 3 * BLK), F32), pltpu.VMEM((GRP * BLK, 3 * BLK), _MXU)],
                  vmem=40, sem=("parallel",), hook=hook)


def _attn_bwd(q, k, v, sinks, do, o, lse, cos_t, sin_t, dz, *, hook):
    def body(q_ref, k0_ref, kp_ref, kc_ref, v0_ref, vp_ref, vc_ref, sink_ref, do_ref, o_ref, lse_ref, c_ref, s_ref, dz_in,
             dq_ref, dk_ref, dv_ref, dsink_ref, dqrot_ref, s_scr, dp_scr, p_scr, ds_scr):
        i = pl.program_id(0)

        @pl.when(i == 0)
        def _():
            dk_ref[...] = jnp.zeros_like(dk_ref)
            dv_ref[...] = jnp.zeros_like(dv_ref)
            dsink_ref[...] = jnp.zeros_like(dsink_ref)

        mask = _attn_mask(i)
        row_starts = (0, pl.multiple_of(jnp.maximum(i - 1, 0) * BLK, BLK), pl.multiple_of(i * BLK, BLK))
        scale = HD ** -0.5
        for g in range(N_KV):
            gs = slice(HD * g, HD * (g + 1))
            q8 = _stack_heads(q_ref, g)
            dom = _stack_heads(do_ref, g).astype(_MXU)
            kcat = _cat_kv((k0_ref, kp_ref, kc_ref), g)
            s_scr[...] = lax.dot_general(q8, kcat, _NT, preferred_element_type=F32)
            dp_scr[...] = lax.dot_general(dom, _cat_kv((v0_ref, vp_ref, vc_ref), g), _NT, preferred_element_type=F32)
            for j in range(GRP):
                h = GRP * g + j
                hs = slice(HD * h, HD * (h + 1))
                rows = slice(BLK * j, BLK * (j + 1))
                lse_h = lse_ref[:, h:h + 1]
                delta = jnp.sum(do_ref[:, hs] * o_ref[:, hs], axis=-1, keepdims=True)
                dsink_ref[0:1, h:h + 1] += jnp.sum(-jnp.exp(sink_ref[h] - lse_h) * delta, axis=0, keepdims=True)
                p = jnp.exp(jnp.where(mask, s_scr[rows, :] * scale, NEG_INF) - lse_h)
                p_scr[rows, :] = p.astype(_MXU)
                ds_scr[rows, :] = (p * (dp_scr[rows, :] - delta) * scale).astype(_MXU)
            ds = ds_scr[...]
            dq8 = jnp.dot(ds, kcat, preferred_element_type=F32)
            dkcat = lax.dot_general(ds, q8, _TN, preferred_element_type=F32)
            dvcat = lax.dot_general(p_scr[...], dom, _TN, preferred_element_type=F32)
            for part in range(3):
                rows = pl.ds(row_starts[part], BLK)
                dk_ref[rows, gs] += dkcat[BLK * part:BLK * (part + 1)]
                dv_ref[rows, gs] += dvcat[BLK * part:BLK * (part + 1)]
            for j in range(GRP):
                h = GRP * g + j
                dqrot_ref[:, HD * h:HD * (h + 1)] = dq8[BLK * j:BLK * (j + 1)]
        dq_ref[...] = _rope(dqrot_ref[...], c_ref[...], s_ref[...], -1.0).astype(_MXU)

    row = pl.BlockSpec((BLK, D), lambda i: (i, 0))
    tab = pl.BlockSpec((BLK, 128), lambda i: (i, 0))
    full_kv = pl.BlockSpec((TP, D_KV), lambda i: (0, 0))
    return _pcall(
        body, name="attn_bwd", grid=(NBLK,),
        in_specs=[row] + _kv_specs() + _kv_specs() + [pl.BlockSpec(memory_space=pltpu.SMEM), row, row,
                  pl.BlockSpec((BLK, N_Q), lambda i: (i, 0)), tab, tab, _ANY],
        out_specs=[pl.BlockSpec((BLK, D), lambda i: (i, OFF_Q // D)), full_kv, full_kv,
                   pl.BlockSpec((8, 128), lambda i: (0, 0))],
        out_shape=[_sds((TP, D_IN), _MXU), _sds((TP, D_KV), F32), _sds((TP, D_KV), F32), _sds((8, 128), F32)],
        operands=[q, k, k, k, v, v, v, sinks, do, o, lse, cos_t, sin_t, dz],
        scratch=[pltpu.VMEM((BLK, D), F32), pltpu.VMEM((GRP * BLK, 3 * BLK), F32), pltpu.VMEM((GRP * BLK, 3 * BLK), F32),
                 pltpu.VMEM((GRP * BLK, 3 * BLK), _MXU), pltpu.VMEM((GRP * BLK, 3 * BLK), _MXU)],
        vmem=48, sem=("arbitrary",), aliases={13: 0}, hook=hook)


def _cast_into_slot(w32, pos, *, name, tr):
    r, cc = w32.shape

    def body(pos_ref, w_ref, o_ref):
        o_ref[...] = w_ref[...].astype(BF16)

    return _pcall(body, name=name, grid=(r // tr,), in_specs=[pl.BlockSpec((tr, cc), lambda i, p: (i, 0))],
                  out_specs=[pl.BlockSpec((None, tr, cc), lambda i, p: (p[1], i, 0))],
                  out_shape=[_sds((N_SH, r, cc), BF16)], operands=[w32], vmem=32, sem=("parallel",), prefetch=(pos,))[0]


def _pair_add(g, got, pos, *, name, tr, g_has_both_halves, hook=None):
    n, h, cc = got.shape
    nt = h // tr

    def body(pos_ref, g_ref, r_ref, own_ref, s16_ref):
        s = g_ref[...] + r_ref[...].astype(F32)
        s16_ref[...] = s.astype(BF16)

        @pl.when(pl.program_id(1) == pos_ref[1])
        def _():
            own_ref[...] = s

    g_index = (lambda i, s, p: (s, p[0] * nt + i, 0)) if g_has_both_halves else (lambda i, s, p: (s, i, 0))
    return _pcall(body, name=name, grid=(nt, n),
                  in_specs=[pl.BlockSpec((None, tr, cc), g_index), pl.BlockSpec((None, tr, cc), lambda i, s, p: (s, i, 0))],
                  out_specs=[pl.BlockSpec((tr, cc), lambda i, s, p: (i, 0)),
                             pl.BlockSpec((None, tr, cc), lambda i, s, p: (s, i, 0))],
                  out_shape=[_sds((h, cc), F32), _sds((n, h, cc), BF16)], operands=[g, got], vmem=40,
                  sem=("parallel", "arbitrary"), prefetch=(pos,), hook=hook)


def _relay_add(s16, raw, pos, *, name, tr):
    _, q, cc = raw.shape
    nt = q // tr

    def body(pos_ref, s_ref, r_ref, o_ref):
        o_ref[...] = (s_ref[...].astype(F32) + r_ref[...].astype(F32)).astype(BF16)

    blk = pl.BlockSpec((None, tr, cc), lambda k, i, p: (k, i, 0))
    return _pcall(body, name=name, grid=(2, nt),
                  in_specs=[pl.BlockSpec((None, tr, cc), lambda k, i, p: (p[3 + k], (1 - k) * nt + i, 0)), blk],
                  out_specs=[blk], out_shape=[_sds((2, q, cc), BF16)], operands=[s16, raw], vmem=40,
                  sem=("parallel", "parallel"), prefetch=(pos,))[0]


def _sum_chips(own, got, pos, *, name, tr):
    h, cc = own.shape
    n_got = got.shape[0]
    nt = h // tr

    def body(pos_ref, o_ref, r_ref, out_ref):
        acc = o_ref[...]
        for k in range(n_got):
            acc = acc + r_ref[k].astype(F32)
        out_ref[...] = acc

    return _pcall(body, name=name, grid=(nt,),
                  in_specs=[pl.BlockSpec((tr, cc), lambda i, p: (i, 0)),
                            pl.BlockSpec((n_got, tr, cc), lambda i, p: (0, i, 0))],
                  out_specs=[pl.BlockSpec((tr, cc), lambda i, p: (p[0] * nt + i, 0))],
                  out_shape=[_sds((2 * h, cc), F32)], operands=[own, got], vmem=40, sem=("parallel",), prefetch=(pos,))[0]


def _adamw(w, g, m, v, *, name, tr, g_row0=0):
    r, cc = w.shape
    g_blk0 = g_row0 // tr

    def body(w_ref, g_ref, m_ref, v_ref, go_ref, d_ref, mo_ref, vo_ref):
        gg = g_ref[...]
        go_ref[...] = gg
        m_new = ADAM_B1 * m_ref[...] + (1.0 - ADAM_B1) * gg
        v_new = ADAM_B2 * v_ref[...] + (1.0 - ADAM_B2) * (gg * gg)
        m_hat = m_new / (1.0 - ADAM_B1 ** ADAM_STEP)
        v_hat = v_new / (1.0 - ADAM_B2 ** ADAM_STEP)
        d_ref[...] = -ADAM_LR * (m_hat / (jnp.sqrt(v_hat) + ADAM_EPS) + ADAM_WD * w_ref[...])
        mo_ref[...] = m_new
        vo_ref[...] = v_new

    blk = pl.BlockSpec((tr, cc), lambda i: (i, 0))
    gblk = pl.BlockSpec((tr, cc), lambda i: (g_blk0 + i, 0))
    return _pcall(body, name=name, grid=(r // tr,), in_specs=[blk, gblk, blk, blk], out_specs=[blk] * 4,
                  out_shape=[_sds((r, cc), F32)] * 4, operands=[w, g, m, v], vmem=48, sem=("parallel",))


_SMALL_ROWS = 40
_B_IN_ROWS = 7


def _row_pad(v, rows):
    flat = v.reshape(-1)
    return jnp.pad(flat, (0, rows * D - flat.shape[0])).reshape(rows, D)


def _pack_ra(w):
    return w.transpose(1, 0, 2).reshape(64, D)


def _unpack_ra(p, like):
    return p.reshape(64, N_RB, RB).transpose(1, 0, 2).reshape(like.shape)


def _gate_full(g4):
    return g4.reshape(N_SH, 64, N_RB, RB).transpose(2, 0, 1, 3).reshape(N_RB, RB, RB)


def kernel(x, meta_tokens, ln_emb_g, ln_emb_b, w_in, b_in, conv_w, conv_b, w_ra, b_ra, w_ri, b_ri, lru_lambda, sinks, w_rnn_out, w_attn_out, w_o, b_o, ln_g, ln_b, loss_target, m_meta_tokens, m_ln_emb_g, m_ln_emb_b, m_w_in, m_b_in, m_conv_w, m_conv_b, m_w_ra, m_b_ra, m_w_ri, m_b_ri, m_lru_lambda, m_sinks, m_w_rnn_out, m_w_attn_out, m_w_o, m_b_o, m_ln_g, m_ln_b, v_meta_tokens, v_ln_emb_g, v_ln_emb_b, v_w_in, v_b_in, v_conv_w, v_conv_b, v_w_ra, v_b_ra, v_w_ri, v_b_ri, v_lru_lambda, v_sinks, v_w_rnn_out, v_w_attn_out, v_w_o, v_b_o, v_ln_g, v_ln_b):
    xi, yi, ci = _my_pos()
    shard = 2 * xi + yi
    pos = jnp.stack([ci, shard, 1 - ci, shard ^ 2, shard ^ 1]).astype(jnp.int32)
    cos_t, sin_t = _rope_tables()
    zero_bias = jnp.zeros((1, D), F32)
    ln_emb_g2, ln_emb_b2 = ln_emb_g[None], ln_emb_b[None]

    small = jnp.concatenate([conv_w[0], meta_tokens, jnp.zeros((4, 512), F32)], axis=0)
    small4 = _gather_small(small)
    conv_w_full = small4[:, 0:4].transpose(1, 0, 2).reshape(CONV_W, D)
    meta_full = small4[:, 4:20].transpose(1, 0, 2).reshape(N_META, D)
    w_own = _cast_into_slot(w_in[0], pos, name="cast_w_in", tr=256)
    wa_own = _cast_into_slot(jnp.concatenate([w_attn_out[0], w_o[0], _pack_ra(w_ra[0]), _pack_ra(w_ri[0])], axis=0),
                             pos, name="cast_w_a", tr=288)
    wb_own = _cast_into_slot(w_rnn_out[0], pos, name="cast_w_b", tr=256)

    h32, h16, h16_t = _ln_emb_fwd(x[0], meta_full, ln_emb_g2, ln_emb_b2)
    order = jnp.stack([shard, shard ^ 2, shard ^ 1, shard ^ 3]).astype(jnp.int32)
    z, w_in4 = _mm_z_gather(h16, w_own, b_in, order)
    q, k, v = _rope_fwd(z, cos_t, sin_t)
    o, lse, wa4 = _attn_fwd(q, k, v, sinks[0], hook=_hook_gather(wa_own, 0.6))
    w_ra_full = _gate_full(wa4[:, 2 * SQ_ROWS:2 * SQ_ROWS + 64])
    w_ri_full = _gate_full(wa4[:, 2 * SQ_ROWS + 64:2 * SQ_ROWS + 128])
    a_dec, u_in = _rnn_gates_fwd(z, conv_w_full, conv_b, w_ra_full, w_ri_full, b_ra, b_ri, lru_lambda)
    hr, wb4 = _scan_fwd(a_dec, u_in, hook=_hook_gather(wb_own, 0.6))
    sq_w = {0: (wb4, 0), 1: (wa4, 0), 2: (wa4, 1)}

    def sq_nn(a, kk, bias, name, out_cols, out_index, carried=None):
        wp, blk = sq_w[kk]
        return _mm_nn(a, wp, bias, name=name, grid=(2, D // CW), tm=HALF_TP, tn=CW, k=D, a_index=lambda i, j: (i, 0),
                      w_block=(N_SH, SQ_ROWS, CW), w_index=lambda i, j: (0, blk, j), out_cols=out_cols,
                      out_index=out_index, carried=carried)[0]

    def sq_nt(a, a_blk, kk, name, hook=None):
        wp, blk = sq_w[kk]
        return _mm_nt(a, wp, name=name, grid=(2, N_SH, 1), tm=HALF_TP, tn=SQ_ROWS, tk=D,
                      a_index=lambda i, j, q: (i, a_blk), w_block=(None, SQ_ROWS, D),
                      w_index=lambda i, j, q: (j, blk, 0), out_cols=D, hook=hook)

    ya_in, ya_in_t = _mul_silu_fwd(hr, z, OFF_GR, name="gate_a_fwd")
    y2 = sq_nn(ya_in, 0, zero_bias, "mm_ya", 2 * D, lambda i, j: (i, j))
    yb_in, yb_in_t = _mul_silu_fwd(o, z, OFF_GA, name="gate_b_fwd")
    y2 = sq_nn(yb_in, 1, zero_bias, "mm_yb", 2 * D, lambda i, j: (i, D // CW + j), carried=y2)
    mixed, mixed_t = _merge_fwd(y2, z)
    out = sq_nn(mixed, 2, b_o, "mm_out", D, lambda i, j: (i, j))
    dr, dr16, sums_o = _ln_out_loss(h32, out, loss_target[0], ln_g, ln_b)

    def sq_tn(at, b, b_blk, kk, name, carried=None):
        return _mm_tn(at, b, name=name, grid=(N_SH, 1), tm=SQ_ROWS, tn=D, a_index=lambda i, j: (i, 0),
                      b_index=lambda i, j: (0, b_blk), out_shape=(N_SH, PACK_ROWS, D),
                      out_block=(None, SQ_ROWS, D), out_index=lambda i, j: (i, kk, 0), carried=carried)[0]

    gsq = sq_tn(mixed_t, dr16, 0, 2, "mm_dwo")
    dmix = sq_nt(dr16, 0, 2, "mm_dmix")[0]
    dy2, dz = _merge_bwd(dmix, y2, z)
    gsq = sq_tn(ya_in_t, dy2, 0, 0, "mm_dwrnn", carried=gsq)
    gsq = sq_tn(yb_in_t, dy2, 1, 1, "mm_dwattn", carried=gsq)
    dya_in = sq_nt(dy2, 0, 0, "mm_dyain")[0]
    dhr, dz = _mul_silu_bwd(dya_in, hr, z, OFF_GR, dz, name="gate_a_bwd")
    lam_s = _scan_bwd(a_dec, dhr)
    dz, gsq, sums_r = _rnn_gates_bwd(z, lam_s, hr, conv_w_full, conv_b, w_ra_full, w_ri_full, b_ra, b_ri, lru_lambda, dz, gsq)
    dyb_in = sq_nt(dy2, 1, 1, "mm_dybin")[0]
    do, dz, gsq, got_sq = _mul_silu_bwd(dyb_in, o, z, OFF_GA, dz, name="gate_b_bwd", hook=_hook_pair(gsq, True))
    own_sq, s16_sq = _pair_add(gsq, got_sq, pos, name="red_w_sq_add", tr=208, g_has_both_halves=True)
    dz, dk_rot, dv32, dsink, s16_sq, fin_sq, raw_sq = _attn_bwd(q, k, v, sinks[0], do, o, lse, cos_t, sin_t, dz,
                                                                hook=_hook_scatter_direct(s16_sq))
    dz = _rope_bwd_k(dk_rot, dv32, cos_t, sin_t, dz)
    comb_sq = _relay_add(s16_sq, raw_sq, pos, name="red_w_sq_relay", tr=208)

    def dwin(half_idx, name, hook, **kw):
        return _mm_tn(h16_t, dz, name=name, grid=(1, N_SH * PER_IN), tm=D // 2, tn=TN_IN,
                      a_index=lambda i, j, p: (p[half_idx], 0), b_index=lambda i, j, p: (0, j),
                      out_shape=(N_SH, D // 2, W_IN_COLS), out_block=(None, D // 2, TN_IN),
                      out_index=lambda i, j, p: (j // PER_IN, 0, j % PER_IN), prefetch=(pos,), hook=hook, **kw)

    gin_sib, g_b_in, comb_sq, fin_sq = dwin(2, "mm_dwin_sib", _hook_scatter_relay(comb_sq, fin_sq),
                                            out_dtype=BF16, colsum=True)
    red_sq = _sum_chips(own_sq, fin_sq, pos, name="red_w_sq_sum", tr=208)
    gin_own, gin_sib, got_in = dwin(0, "mm_dwin_own", _hook_pair(gin_sib, False))
    own_in, s16_in, g_sq = _pair_add(gin_own, got_in, pos, name="red_w_in_add", tr=128, g_has_both_halves=False,
                                     hook=_hook_halves(red_sq))
    dhz, s16_in, fin_in, raw_in = _mm_nt(dz, w_in4, name="mm_dhz", grid=(2, 2, N_SH), tm=HALF_TP, tn=D // 2,
                                         tk=W_IN_COLS, a_index=lambda i, j, q: (i, q),
                                         w_block=(None, D // 2, W_IN_COLS), w_index=lambda i, j, q: (q, j, 0), out_cols=D,
                                         hook=_hook_scatter_direct(s16_in))
    comb_in = _relay_add(s16_in, raw_in, pos, name="red_w_in_relay", tr=128)
    g_x, g_meta_local, sums_e, comb_in, fin_in = _ln_emb_bwd(x[0], meta_full, ln_emb_g2, dr, dhz,
                                                             hook=_hook_scatter_relay(comb_in, fin_in))
    red_in = _sum_chips(own_in, fin_in, pos, name="red_w_in_sum", tr=128)

    spack = jnp.concatenate([
        sums_e[0:1], sums_e[1:2], _row_pad(g_b_in, _B_IN_ROWS), sums_r[0:4], sums_r[4:5], sums_r[5:6], sums_r[6:7],
        sums_r[7:8], _row_pad(dsink[0:1, 0:N_Q], 1), sums_o[2:3], sums_o[0:1], sums_o[1:2], g_meta_local, sums_o[3:4],
        jnp.zeros((_SMALL_ROWS - 38, D), F32)], axis=0)
    sred, g_in = _allreduce_small(spack, red_in)

    big = {"w_in": [t.reshape(w_in.shape) for t in
                    _adamw(w_in[0], g_in, m_w_in[0], v_w_in[0], name="adamw_w_in", tr=128)]}
    for kk, (n, w_, m_, v_) in enumerate([("w_rnn_out", w_rnn_out, m_w_rnn_out, v_w_rnn_out),
                                          ("w_attn_out", w_attn_out, m_w_attn_out, v_w_attn_out), ("w_o", w_o, m_w_o, v_w_o)]):
        big[n] = [t.reshape(w_.shape) for t in
                  _adamw(w_[0], g_sq, m_[0], v_[0], name="adamw_" + n, tr=256, g_row0=SQ_ROWS * kk)]
    for kk, (n, w_, m_, v_) in enumerate([("w_ra", w_ra, m_w_ra, v_w_ra), ("w_ri", w_ri, m_w_ri, v_w_ri)]):
        big[n] = [_unpack_ra(t, w_) for t in
                  _adamw(_pack_ra(w_[0]), g_sq, _pack_ra(m_[0]), _pack_ra(v_[0]), name="adamw_" + n, tr=64,
                         g_row0=3 * SQ_ROWS + 64 * kk)]

    loss = sred[37, 0]
    col0 = shard * 512
    g_conv_w = lax.dynamic_slice(sred[9:13], (0, col0), (CONV_W, 512))
    g_meta = lax.dynamic_slice(sred[21:37], (0, col0), (N_META, 512))
    small_g = {"ln_emb_g": sred[0:1], "ln_emb_b": sred[1:2], "b_in": sred[2:9], "conv_w": g_conv_w.reshape(1, D),
               "conv_b": sred[13:14], "b_ra": sred[14:15], "b_ri": sred[15:16], "lru_lambda": sred[16:17],
               "sinks": sred[17:18], "b_o": sred[18:19], "ln_g": sred[19:20], "ln_b": sred[20:21],
               "meta_tokens": g_meta.reshape(4, D)}
    small_names = list(small_g)

    def small_pack(vals):
        rows = []
        for n in small_names:
            a = vals[n]
            if n == "b_in":
                a = _row_pad(a, _B_IN_ROWS)
            elif n == "sinks":
                a = _row_pad(a, 1)
            else:
                a = a.reshape(-1, D)
            rows.append(a)
        return jnp.concatenate(rows + [jnp.zeros((24 - 22, D), F32)], axis=0)

    w_small = dict(ln_emb_g=ln_emb_g, ln_emb_b=ln_emb_b, b_in=b_in, conv_w=conv_w, conv_b=conv_b, b_ra=b_ra, b_ri=b_ri,
                   lru_lambda=lru_lambda, sinks=sinks, b_o=b_o, ln_g=ln_g, ln_b=ln_b, meta_tokens=meta_tokens)
    m_small = dict(ln_emb_g=m_ln_emb_g, ln_emb_b=m_ln_emb_b, b_in=m_b_in, conv_w=m_conv_w, conv_b=m_conv_b, b_ra=m_b_ra,
                   b_ri=m_b_ri, lru_lambda=m_lru_lambda, sinks=m_sinks, b_o=m_b_o, ln_g=m_ln_g, ln_b=m_ln_b,
                   meta_tokens=m_meta_tokens)
    v_small = dict(ln_emb_g=v_ln_emb_g, ln_emb_b=v_ln_emb_b, b_in=v_b_in, conv_w=v_conv_w, conv_b=v_conv_b, b_ra=v_b_ra,
                   b_ri=v_b_ri, lru_lambda=v_lru_lambda, sinks=v_sinks, b_o=v_b_o, ln_g=v_ln_g, ln_b=v_ln_b,
                   meta_tokens=v_meta_tokens)
    g_small_pack = jnp.concatenate([small_g[n] for n in small_names] + [jnp.zeros((2, D), F32)], axis=0)
    small_res = _adamw(small_pack(w_small), g_small_pack, small_pack(m_small), small_pack(v_small),
                       name="adamw_small", tr=24)

    small_rows = {}
    r0 = 0
    for n in small_names:
        nrows = small_g[n].shape[0]
        small_rows[n] = (r0, nrows)
        r0 += nrows

    def small_out(packed, n, like):
        a, nrows = small_rows[n]
        flat = packed[a:a + nrows].reshape(-1)
        return flat[:like.size].reshape(like.shape)

    weights = dict(meta_tokens=meta_tokens, ln_emb_g=ln_emb_g, ln_emb_b=ln_emb_b, w_in=w_in, b_in=b_in, conv_w=conv_w,
                   conv_b=conv_b, w_ra=w_ra, b_ra=b_ra, w_ri=w_ri, b_ri=b_ri, lru_lambda=lru_lambda, sinks=sinks,
                   w_rnn_out=w_rnn_out, w_attn_out=w_attn_out, w_o=w_o, b_o=b_o, ln_g=ln_g, ln_b=ln_b)

    def outputs(which):
        return [big[n][which] if n in big else small_out(small_res[which], n, like) for n, like in weights.items()]

    return (loss, g_x[None], *outputs(0), *outputs(1), *outputs(2), *outputs(3))
```

```python
import jax
import jax.numpy as jnp
from jax import lax
from jax.experimental import pallas as pl
from jax.experimental.pallas import tpu as pltpu

F32 = jnp.float32
BF16 = jnp.bfloat16
_MXU = jnp.bfloat16

D = 2048
SEQ = 2048
N_META = 16
BLK = 128
PAD = BLK - N_META
TP = PAD + N_META + SEQ
NBLK = TP // BLK
HALF_TP = TP // 2
N_RB = 8
RB = 256
CONV_W = 4
LRU_C = 8.0
HD = 64
N_Q = 32
N_KV = 4
GRP = 8
D_KV = 256
NEG_INF = -1e30
LN_EPS = 1e-5
ALPHA = 2.0 ** 0.25
ROPE_THETA = 10000.0
OFF_GR, OFF_Q, OFF_K, OFF_V, OFF_GA, OFF_G = 2048, 4096, 6144, 6400, 6656, 8704
D_IN = 12800
N_SH = 4
W_IN_COLS = D_IN // N_SH
TN_IN = 640
PER_IN = W_IN_COLS // TN_IN
CW = 512
RT = TP // 4
SQ_ROWS = 512
PACK_ROWS = 3 * SQ_ROWS + 128

ADAM_LR = 0.001
ADAM_B1 = 0.9
ADAM_B2 = 0.999
ADAM_EPS = 1e-08
ADAM_WD = 0.01
ADAM_STEP = 10

MESH = pl.DeviceIdType.MESH
_MIB = 1024 * 1024
_ANY = pl.BlockSpec(memory_space=pl.ANY)
_NT = (((1,), (1,)), ((), ()))
_TN = (((0,), (0,)), ((), ()))


def _sds(shape, dtype):
    return jax.ShapeDtypeStruct(shape, dtype)


def _sigmoid(x):
    return 1.0 / (1.0 + jnp.exp(-x))


def _my_pos():
    return lax.axis_index("x"), lax.axis_index("y"), lax.axis_index("c")


def _other_chips(x, y):
    return [(1 - x, y), (x, 1 - y), (1 - x, 1 - y)]


def _remote(src, dst, send_sems, recv_sems, k, dev):
    return pltpu.make_async_remote_copy(src_ref=src, dst_ref=dst, send_sem=send_sems.at[k], recv_sem=recv_sems.at[k],
                                        device_id=dev, device_id_type=MESH)


class _Hook:
    def __init__(self, carried, landing, n_sems, start, finish, mid=None, mid_frac=0.5):
        self.carried, self.landing, self.n_sems, self.start, self.finish = list(carried), list(landing), n_sems, start, finish
        self.mid, self.mid_frac = mid, mid_frac


def _hook_gather(buf, mid_frac):
    half = buf.shape[1] // 2
    quarter = half // 2

    def geom(o, ss, rs):
        x, y, c = _my_pos()
        xn, yn, dg = _other_chips(x, y)
        slot = lambda p: 2 * p[0] + p[1]
        mine_rows = pl.ds(pl.multiple_of(c * half, 16), half)
        sib_rows = pl.ds(pl.multiple_of((1 - c) * half, 16), half)
        q_rows = lambda r: pl.ds(pl.multiple_of(c * half + r * quarter, 16), quarter)

        def cp(k, s, rows, dev):
            part = o.at[s, rows]
            return _remote(part, part, ss, rs, k, dev)

        return dict(
            direct=lambda k, s: cp(k, s, mine_rows, ((xn, yn)[k][0], (xn, yn)[k][1], c)),
            relay=lambda r, s: cp(2 + r, s, q_rows(r), ((yn, xn)[r][0], (yn, xn)[r][1], c)),
            sibling=lambda k, s, mine: cp(4 + k, s, mine_rows if mine else sib_rows, (x, y, 1 - c)),
            me=slot((x, y)), slots=(slot(xn), slot(yn), slot(dg)))

    def start(car, land, ss, rs):
        g = geom(car[0], ss, rs)
        g["direct"](0, g["me"]).start()
        g["direct"](1, g["me"]).start()

    def mid(car, land, ss, rs):
        g = geom(car[0], ss, rs)
        for k in range(2):
            g["direct"](k, g["slots"][k]).wait_recv()
            g["relay"](k, g["slots"][k]).start()
            g["sibling"](k, g["slots"][k], True).start()

    def finish(car, land, ss, rs):
        g = geom(car[0], ss, rs)
        dslot = g["slots"][2]
        g["relay"](0, dslot).wait_recv()
        g["relay"](1, dslot).wait_recv()
        g["sibling"](2, dslot, True).start()
        for k in range(3):
            g["sibling"](k, g["slots"][k], False).wait_recv()
        for k in range(2):
            g["direct"](k, g["me"]).wait_send()
            g["relay"](k, g["slots"][k]).wait_send()
        for k in range(3):
            g["sibling"](k, g["slots"][k], True).wait_send()

    return _Hook([buf], [], 7, start, finish, mid=mid, mid_frac=mid_frac)


def _hook_pair(g, half_rows):
    n, r, cc = g.shape
    h = r // 2 if half_rows else r

    def plan(car, land, ss, rs):
        x, y, c = _my_pos()
        cps = []
        for s in range(n):
            src = car[0].at[s, pl.ds(pl.multiple_of((1 - c) * h, 8), h)] if half_rows else car[0].at[s]
            cps.append(_remote(src, land[0].at[s], ss, rs, s, (x, y, 1 - c)))
        return cps

    def start(car, land, ss, rs):
        for cp in plan(car, land, ss, rs):
            cp.start()

    def finish(car, land, ss, rs):
        cps = plan(car, land, ss, rs)
        for cp in cps:
            cp.wait_recv()
        for cp in cps:
            cp.wait_send()

    return _Hook([g], [_sds((n, h, cc), g.dtype)], n, start, finish)


def _hook_scatter_direct(s16):
    _, h, cc = s16.shape
    q = h // 2

    def plan(car, land, ss, rs):
        x, y, c = _my_pos()
        xn, yn, dg = _other_chips(x, y)
        s, (final, raw) = car[0], land
        slot = lambda p: 2 * p[0] + p[1]
        q0, q1 = pl.ds(0, q), pl.ds(q, q)
        return [_remote(s.at[slot(xn), q0], final.at[0, q0], ss, rs, 0, (xn[0], xn[1], c)),
                _remote(s.at[slot(yn), q1], final.at[1, q1], ss, rs, 1, (yn[0], yn[1], c)),
                _remote(s.at[slot(dg), q0], raw.at[1], ss, rs, 2, (xn[0], xn[1], c)),
                _remote(s.at[slot(dg), q1], raw.at[0], ss, rs, 3, (yn[0], yn[1], c))]

    def start(car, land, ss, rs):
        for cp in plan(car, land, ss, rs):
            cp.start()

    def finish(car, land, ss, rs):
        cps = plan(car, land, ss, rs)
        for cp in cps:
            cp.wait_recv()
        for cp in cps:
            cp.wait_send()

    return _Hook([s16], [_sds((2, h, cc), s16.dtype), _sds((2, q, cc), s16.dtype)], 4, start, finish)


def _hook_scatter_relay(comb, final):
    _, q, _ = comb.shape

    def plan(car, ss, rs):
        x, y, c = _my_pos()
        xn, yn, _ = _other_chips(x, y)
        cb, final_ref = car
        return [_remote(cb.at[0], final_ref.at[0, pl.ds(q, q)], ss, rs, 0, (xn[0], xn[1], c)),
                _remote(cb.at[1], final_ref.at[1, pl.ds(0, q)], ss, rs, 1, (yn[0], yn[1], c))]

    def start(car, land, ss, rs):
        for cp in plan(car, ss, rs):
            cp.start()

    def finish(car, land, ss, rs):
        cps = plan(car, ss, rs)
        for cp in cps:
            cp.wait_recv()
        for cp in cps:
            cp.wait_send()

    return _Hook([comb, final], [], 2, start, finish)


def _hook_halves(full):
    h = full.shape[0] // 2

    def half_copy(car, ss, rs, which):
        x, y, c = _my_pos()
        rows = car[0].at[pl.ds(pl.multiple_of((c + which - 2 * c * which) * h, 8), h)]
        return _remote(rows, rows, ss, rs, 0, (x, y, 1 - c))

    def start(car, land, ss, rs):
        half_copy(car, ss, rs, 0).start()

    def finish(car, land, ss, rs):
        half_copy(car, ss, rs, 1).wait_recv()
        half_copy(car, ss, rs, 0).wait_send()

    return _Hook([full], [], 1, start, finish)


def _pcall(body, *, name, grid, in_specs, out_specs, out_shape, operands, scratch=(), vmem=48, sem=None,
           prefetch=(), aliases=None, hook=None):
    n_pre, n_in, n_out, n_scr = len(prefetch), len(in_specs), len(out_specs), len(scratch)
    in_specs, out_specs, out_shape, scratch = list(in_specs), list(out_specs), list(out_shape), list(scratch)
    io_alias = {n_pre + a: b for a, b in (aliases or {}).items()}
    operands = list(operands)
    kernel_body = body
    if hook is not None:
        n_car, n_land = len(hook.carried), len(hook.landing)
        for t, arr in enumerate(hook.carried):
            io_alias[n_pre + n_in + t] = n_out + t
        in_specs += [_ANY] * n_car
        out_specs += [_ANY] * (n_car + n_land)
        out_shape += [_sds(a.shape, a.dtype) for a in hook.carried] + hook.landing
        scratch += [pltpu.SemaphoreType.DMA((hook.n_sems,)), pltpu.SemaphoreType.DMA((hook.n_sems,))]
        operands += hook.carried
        sem = ("arbitrary",) * len(grid)

        def kernel_body(*refs):
            pre, rest = refs[:n_pre], refs[n_pre:]
            ins = rest[:n_in]
            outs = rest[n_in + n_car:n_in + n_car + n_out]
            car = rest[n_in + n_car + n_out:n_in + 2 * n_car + n_out]
            land = rest[n_in + 2 * n_car + n_out:n_in + 2 * n_car + n_out + n_land]
            scr = rest[n_in + 2 * n_car + n_out + n_land:]
            send_sems, recv_sems = scr[n_scr], scr[n_scr + 1]
            first = pl.program_id(0) == 0
            last = pl.program_id(0) == grid[0] - 1
            for d in range(1, len(grid)):
                first = first & (pl.program_id(d) == 0)
                last = last & (pl.program_id(d) == grid[d] - 1)

            @pl.when(first)
            def _():
                hook.start(car, land, send_sems, recv_sems)

            if hook.mid is not None:
                step = pl.program_id(0)
                total = grid[0]
                for d in range(1, len(grid)):
                    step = step * grid[d] + pl.program_id(d)
                    total *= grid[d]

                @pl.when(step == int(total * hook.mid_frac))
                def _():
                    hook.mid(car, land, send_sems, recv_sems)

            body(*pre, *ins, *outs, *scr[:n_scr])

            @pl.when(last)
            def _():
                hook.finish(car, land, send_sems, recv_sems)

    params = pltpu.CompilerParams(vmem_limit_bytes=vmem * _MIB, dimension_semantics=sem,
                                  has_side_effects=hook is not None)
    out_shape = [pltpu.HBM(s.shape, s.dtype) if len(s.shape) >= 2 else s for s in out_shape]
    if n_pre:
        call = pl.pallas_call(
            kernel_body, name=name, out_shape=out_shape, input_output_aliases=io_alias, compiler_params=params,
            grid_spec=pltpu.PrefetchScalarGridSpec(num_scalar_prefetch=n_pre, grid=grid, in_specs=in_specs,
                                                   out_specs=out_specs, scratch_shapes=scratch))
    else:
        call = pl.pallas_call(kernel_body, name=name, grid=grid, in_specs=in_specs, out_specs=out_specs,
                              out_shape=out_shape, scratch_shapes=scratch, input_output_aliases=io_alias,
                              compiler_params=params)
    operands = [pltpu.with_memory_space_constraint(o, pltpu.HBM) if o.ndim >= 2 else o for o in operands]
    return call(*prefetch, *operands)


def _gather_small(blk):
    r, cc = blk.shape

    def body(x_ref, o_ref, send_sems, recv_sems):
        x, y, c = _my_pos()
        me = 2 * x + y
        o_ref[me] = x_ref[...]
        sends = [_remote(x_ref, o_ref.at[me], send_sems, recv_sems, k, (px, py, c))
                 for k, (px, py) in enumerate(_other_chips(x, y))]
        for cp in sends:
            cp.start()
        for k, (px, py) in enumerate(_other_chips(x, y)):
            _remote(x_ref, o_ref.at[2 * px + py], send_sems, recv_sems, k, (px, py, c)).wait_recv()
        for cp in sends:
            cp.wait_send()

    vm = pl.BlockSpec(memory_space=pltpu.VMEM)
    return pl.pallas_call(
        body, name="gather_small", in_specs=[vm], out_specs=vm, out_shape=_sds((N_SH, r, cc), blk.dtype),
        scratch_shapes=[pltpu.SemaphoreType.DMA((3,)), pltpu.SemaphoreType.DMA((3,))],
        compiler_params=pltpu.CompilerParams(has_side_effects=True),
    )(blk)


N_DEV = 8


def _allreduce_small(pack, halves):
    r, cc = pack.shape
    ride = _hook_halves(halves)

    def body(x_ref, h_in_ref, o_ref, h_ref, buf_ref, send_sems, recv_sems, h_ss, h_rs):
        del h_in_ref
        ride.start([h_ref], [], h_ss, h_rs)
        x, y, c = _my_pos()
        me = 4 * x + 2 * y + c
        buf_ref[me] = x_ref[...]
        cps = []
        for k in range(1, N_DEV):
            peer = (x ^ ((k >> 2) & 1), y ^ ((k >> 1) & 1), c ^ (k & 1))
            cps.append(_remote(x_ref, buf_ref.at[me], send_sems, recv_sems, k - 1, peer))
        for cp in cps:
            cp.start()
        for k in range(1, N_DEV):
            peer = (x ^ ((k >> 2) & 1), y ^ ((k >> 1) & 1), c ^ (k & 1))
            src = 4 * peer[0] + 2 * peer[1] + peer[2]
            _remote(x_ref, buf_ref.at[src], send_sems, recv_sems, k - 1, peer).wait_recv()
        acc = buf_ref[0]
        for d in range(1, N_DEV):
            acc = acc + buf_ref[d]
        o_ref[...] = acc
        for cp in cps:
            cp.wait_send()
        ride.finish([h_ref], [], h_ss, h_rs)

    vm = pl.BlockSpec(memory_space=pltpu.VMEM)
    return pl.pallas_call(
        body, name="allreduce_small", in_specs=[vm, _ANY], out_specs=[vm, _ANY],
        out_shape=[_sds((r, cc), F32), _sds(halves.shape, halves.dtype)], input_output_aliases={1: 1},
        scratch_shapes=[pltpu.VMEM((N_DEV, r, cc), F32), pltpu.SemaphoreType.DMA((N_DEV - 1,)),
                        pltpu.SemaphoreType.DMA((N_DEV - 1,)), pltpu.SemaphoreType.DMA((1,)), pltpu.SemaphoreType.DMA((1,))],
        compiler_params=pltpu.CompilerParams(has_side_effects=True),
    )(pack, halves)


def _mm_nn(a, w, bias, *, name, grid, tm, tn, k, a_index, w_block, w_index, out_cols, out_index, carried=None, hook=None):
    m = a.shape[0]

    def body(a_ref, w_ref, b_ref, *rest):
        o_ref = rest[-1]
        wv = w_ref[...]
        acc = jnp.dot(a_ref[...].astype(_MXU), wv.reshape(k, tn).astype(_MXU), preferred_element_type=F32)
        o_ref[...] = acc + b_ref[...]

    operands = [a, w, bias] + ([carried] if carried is not None else [])
    return _pcall(
        body, name=name, grid=grid,
        in_specs=[pl.BlockSpec((tm, k), a_index), pl.BlockSpec(w_block, w_index),
                  pl.BlockSpec((1, tn), lambda i, j: (0, j))] + ([_ANY] if carried is not None else []),
        out_specs=[pl.BlockSpec((tm, tn), out_index)], out_shape=[_sds((m, out_cols), F32)], operands=operands,
        vmem=56, sem=("parallel", "parallel"), aliases={3: 0} if carried is not None else None, hook=hook)


def _mm_nt(a, w, *, name, grid, tm, tn, tk, a_index, w_block, w_index, out_cols, hook=None):
    m = a.shape[0]
    nk = grid[2]

    def body(a_ref, w_ref, o_ref, acc_ref):
        part = lax.dot_general(a_ref[...].astype(_MXU), w_ref[...].astype(_MXU), _NT, preferred_element_type=F32)
        if nk == 1:
            o_ref[...] = part
        else:
            kidx = pl.program_id(2)

            @pl.when(kidx == 0)
            def _():
                acc_ref[...] = part

            @pl.when(kidx > 0)
            def _():
                acc_ref[...] += part

            @pl.when(kidx == nk - 1)
            def _():
                o_ref[...] = acc_ref[...]

    return _pcall(
        body, name=name, grid=grid, in_specs=[pl.BlockSpec((tm, tk), a_index), pl.BlockSpec(w_block, w_index)],
        out_specs=[pl.BlockSpec((tm, tn), lambda i, j, q: (i, j))], out_shape=[_sds((m, out_cols), F32)],
        operands=[a, w], scratch=[pltpu.VMEM((tm, tn) if nk > 1 else (8, 128), F32)], vmem=60,
        sem=("parallel", "parallel", "arbitrary"), hook=hook)


def _mm_tn(at, b, *, name, grid, tm, tn, a_index, b_index, out_shape, out_block, out_index, carried=None,
           prefetch=(), hook=None, out_dtype=F32, colsum=False):
    t = at.shape[1]
    n_pre = len(prefetch)

    def body(*refs):
        a_ref, b_ref = refs[n_pre], refs[n_pre + 1]
        bv = b_ref[...]
        o_ref = refs[-2] if colsum else refs[-1]
        o_ref[...] = jnp.dot(a_ref[...].astype(_MXU), bv.astype(_MXU), preferred_element_type=F32).astype(out_dtype)
        if colsum:
            refs[-1][...] = jnp.sum(bv.astype(F32), axis=0, keepdims=True)

    operands = [at, b] + ([carried] if carried is not None else [])
    out_specs = [pl.BlockSpec(out_block, out_index)]
    out_shapes = [_sds(out_shape, out_dtype)]
    if colsum:
        out_specs.append(pl.BlockSpec((1, tn), b_index))
        out_shapes.append(_sds((1, b.shape[1]), F32))
    return _pcall(
        body, name=name, grid=grid,
        in_specs=[pl.BlockSpec((tm, t), a_index), pl.BlockSpec((t, tn), b_index)] + ([_ANY] if carried is not None else []),
        out_specs=out_specs, out_shape=out_shapes, operands=operands,
        vmem=56, sem=("parallel", "parallel"), aliases={2: 0} if carried is not None else None, prefetch=prefetch, hook=hook)


def _remote_tile(n):
    near = 2 * PER_IN
    if isinstance(n, int):
        return (n % 2, n // 2) if n < near else (2, n - near)
    return jnp.where(n < near, n % 2, 2), jnp.where(n < near, n // 2, n - near)


def _mm_z_gather(h16, w_own, b_in, order):
    n_tiles = N_SH * PER_IN
    n_remote = 3 * PER_IN
    half = D // 2

    def body(order_ref, a_ref, b_ref, w_in_ref, z_ref, w_ref, wbuf, tile_sems, ss, rs):
        del w_in_ref
        j = pl.program_id(0)
        x, y, c = _my_pos()
        me = 2 * x + y
        chips = _other_chips(x, y)
        mine_rows = pl.ds(pl.multiple_of(c * half, 16), half)
        sib_rows = pl.ds(pl.multiple_of((1 - c) * half, 16), half)

        slots = [2 * px + py for px, py in chips]
        cols = lambda t: pl.ds(t * TN_IN, TN_IN)
        q_rows = lambda r: pl.ds(pl.multiple_of(c * half + r * (half // 2), 16), half // 2)

        def direct(rel, t, slot):
            px, py = chips[rel]
            part = w_ref.at[slot, mine_rows, cols(t)]
            return _remote(part, part, ss, rs, 2 * t + rel, (px, py, c))

        def relay(r, t, slot):
            px, py = chips[1 - r]
            part = w_ref.at[slot, q_rows(r), cols(t)]
            return _remote(part, part, ss, rs, 2 * PER_IN + 2 * t + r, (px, py, c))

        def d2d(n, rows):
            rel, t = _remote_tile(n)
            part = w_ref.at[slots[rel], rows, cols(t)]
            return _remote(part, part, ss, rs, 4 * PER_IN + n, (x, y, 1 - c))

        def tile_copy(step):
            rel, t = _remote_tile(jnp.maximum(step - PER_IN, 0))
            slot = jnp.where(step < PER_IN, me, me ^ jnp.where(rel == 0, 2, jnp.where(rel == 1, 1, 3)))
            col = pl.multiple_of(jnp.where(step < PER_IN, step, t) * TN_IN, 128)
            return pltpu.make_async_copy(w_ref.at[slot, :, pl.ds(col, TN_IN)], wbuf.at[step % 2], tile_sems.at[step % 2])

        @pl.when(j == 0)
        def _():
            for t in range(PER_IN):
                direct(0, t, me).start()
                direct(1, t, me).start()
            tile_copy(0).start()

        for n in range(n_remote):
            rel, t = _remote_tile(n)

            @pl.when(j == n + PER_IN - 3)
            def _():
                if rel < 2:
                    direct(rel, t, slots[rel]).wait_recv()
                    relay(rel, t, slots[rel]).start()
                else:
                    relay(0, t, slots[2]).wait_recv()
                    relay(1, t, slots[2]).wait_recv()
                d2d(n, mine_rows).start()

            @pl.when(j == n + PER_IN - 2)
            def _():
                d2d(n, sib_rows).wait_recv()

        @pl.when(j + 1 < n_tiles)
        def _():
            tile_copy(j + 1).start()

        tile_copy(j).wait()
        z_ref[...] = jnp.dot(a_ref[...], wbuf[j % 2], preferred_element_type=F32) + b_ref[...]

        @pl.when(j == n_tiles - 1)
        def _():
            for t in range(PER_IN):
                for r in range(2):
                    direct(r, t, me).wait_send()
                    relay(r, t, slots[r]).wait_send()
            for n in range(n_remote):
                d2d(n, mine_rows).wait_send()

    def col_tile(j, o):
        rel, t = _remote_tile(jnp.maximum(j - PER_IN, 0))
        return 0, jnp.where(j < PER_IN, o[0] * PER_IN + j, o[1 + rel] * PER_IN + t)

    return pl.pallas_call(
        body, name="mm_z_gather",
        grid_spec=pltpu.PrefetchScalarGridSpec(
            num_scalar_prefetch=1, grid=(n_tiles,),
            in_specs=[pl.BlockSpec((TP, D), lambda j, o: (0, 0)), pl.BlockSpec((1, TN_IN), col_tile), _ANY],
            out_specs=[pl.BlockSpec((TP, TN_IN), col_tile), _ANY],
            scratch_shapes=[pltpu.VMEM((2, D, TN_IN), BF16), pltpu.SemaphoreType.DMA((2,)),
                            pltpu.SemaphoreType.DMA((4 * PER_IN + n_remote,)),
                            pltpu.SemaphoreType.DMA((4 * PER_IN + n_remote,))]),
        out_shape=[_sds((TP, D_IN), F32), _sds(w_own.shape, w_own.dtype)],
        input_output_aliases={3: 1},
        compiler_params=pltpu.CompilerParams(vmem_limit_bytes=60 * _MIB, dimension_semantics=("arbitrary",),
                                             has_side_effects=True),
    )(order, h16, b_in, w_own)


def _padded_rows(i, x_ref, meta_ref):
    head = jnp.concatenate([jnp.zeros((PAD, D), F32), meta_ref[...]], axis=0)
    return jnp.where(i == 0, head, x_ref[...])


def _stream_specs():
    return [pl.BlockSpec((BLK, D), lambda i: (jnp.maximum(i - 1, 0), 0)), pl.BlockSpec((N_META, D), lambda i: (0, 0))]


def _ln_emb_fwd(x, meta, g, b):
    def body(x_ref, meta_ref, g_ref, b_ref, h32_ref, h16_ref, h16t_ref):
        x = _padded_rows(pl.program_id(0), x_ref, meta_ref)
        mu = jnp.mean(x, axis=-1, keepdims=True)
        xc = x - mu
        var = jnp.mean(xc * xc, axis=-1, keepdims=True)
        y = xc * lax.rsqrt(var + LN_EPS) * g_ref[...] + b_ref[...]
        h32_ref[...] = y
        h16_ref[...] = y.astype(_MXU)
        h16t_ref[...] = y.T.astype(_MXU)

    row = pl.BlockSpec((BLK, D), lambda i: (i, 0))
    vec = pl.BlockSpec((1, D), lambda i: (0, 0))
    return _pcall(body, name="ln_emb_fwd", grid=(NBLK,), in_specs=_stream_specs() + [vec, vec],
                  out_specs=[row, row, pl.BlockSpec((D, BLK), lambda i: (0, i))],
                  out_shape=[_sds((TP, D), F32), _sds((TP, D), _MXU), _sds((D, TP), _MXU)], operands=[x, meta, g, b],
                  vmem=32, sem=("parallel",))


def _ln_emb_bwd(x, meta, g, dr, dhz, *, hook):
    def body(x_ref, meta_ref, g_ref, dr_ref, dhz_ref, dx_ref, dmeta_ref, acc_ref):
        i = pl.program_id(0)

        @pl.when(i == 0)
        def _():
            acc_ref[...] = jnp.zeros_like(acc_ref)

        x = _padded_rows(i, x_ref, meta_ref)
        mu = jnp.mean(x, axis=-1, keepdims=True)
        xc = x - mu
        var = jnp.mean(xc * xc, axis=-1, keepdims=True)
        rstd = lax.rsqrt(var + LN_EPS)
        xhat = xc * rstd
        dh = ALPHA * dr_ref[...] + dhz_ref[...]
        acc_ref[0:1, :] += jnp.sum(dh * xhat, axis=0, keepdims=True)
        acc_ref[1:2, :] += jnp.sum(dh, axis=0, keepdims=True)
        dxh = dh * g_ref[...]
        m1 = jnp.mean(dxh, axis=-1, keepdims=True)
        m2 = jnp.mean(dxh * xhat, axis=-1, keepdims=True)
        dx = rstd * (dxh - m1 - xhat * m2)
        dx_ref[...] = dx

        @pl.when(i == 0)
        def _():
            dmeta_ref[...] = dx[PAD:BLK]

    row = pl.BlockSpec((BLK, D), lambda i: (i, 0))
    vec = pl.BlockSpec((1, D), lambda i: (0, 0))
    xs, ms = _stream_specs()
    return _pcall(body, name="ln_emb_bwd", grid=(NBLK,), in_specs=[xs, ms, vec, row, row],
                  out_specs=[xs, ms, pl.BlockSpec((8, D), lambda i: (0, 0))],
                  out_shape=[_sds((SEQ, D), F32), _sds((N_META, D), F32), _sds((8, D), F32)],
                  operands=[x, meta, g, dr, dhz], vmem=32, sem=("arbitrary",), hook=hook)


def _mul_silu_fwd(a, z, off, *, name):
    def body(a_ref, z_ref, o_ref, t_ref):
        zz = z_ref[...]
        y = a_ref[...] * (zz * _sigmoid(zz))
        o_ref[...] = y.astype(_MXU)
        t_ref[...] = y.T.astype(_MXU)

    strip = pl.BlockSpec((TP, CW), lambda j: (0, j))
    return _pcall(body, name=name, grid=(D // CW,),
                  in_specs=[strip, pl.BlockSpec((TP, CW), lambda j: (0, off // CW + j))],
                  out_specs=[strip, pl.BlockSpec((CW, TP), lambda j: (j, 0))],
                  out_shape=[_sds((TP, D), _MXU), _sds((D, TP), _MXU)], operands=[a, z], vmem=56, sem=("parallel",))


def _mul_silu_bwd(dy, a, z, off, dz, *, name, hook=None):
    def body(dy_ref, a_ref, z_ref, dz_in, da_ref, dg_ref):
        zz = z_ref[...]
        sg = _sigmoid(zz)
        d = dy_ref[...]
        da_ref[...] = d * (zz * sg)
        dg_ref[...] = (d * a_ref[...] * (sg * (1.0 + zz * (1.0 - sg)))).astype(_MXU)

    blk = pl.BlockSpec((RT, CW), lambda i, j: (i, j))
    zblk = pl.BlockSpec((RT, CW), lambda i, j: (i, off // CW + j))
    return _pcall(body, name=name, grid=(TP // RT, D // CW), in_specs=[blk, blk, zblk, _ANY], out_specs=[blk, zblk],
                  out_shape=[_sds((TP, D), F32), _sds((TP, D_IN), _MXU)], operands=[dy, a, z, dz], vmem=32,
                  sem=("parallel", "parallel"), aliases={3: 1}, hook=hook)


def _merge_fwd(y2, z):
    w = 256

    def body(ya_ref, yb_ref, ga_ref, gb_ref, o_ref, t_ref):
        y = _sigmoid(ga_ref[...]) * ya_ref[...] + _sigmoid(gb_ref[...]) * yb_ref[...]
        o_ref[...] = y.astype(_MXU)
        t_ref[...] = y.T.astype(_MXU)

    nb = D // w
    strip = pl.BlockSpec((TP, w), lambda j: (0, j))
    return _pcall(body, name="merge_fwd", grid=(nb,),
                  in_specs=[strip, pl.BlockSpec((TP, w), lambda j: (0, nb + j)),
                            pl.BlockSpec((TP, w), lambda j: (0, OFF_G // w + j)),
                            pl.BlockSpec((TP, w), lambda j: (0, (OFF_G + D) // w + j))],
                  out_specs=[strip, pl.BlockSpec((w, TP), lambda j: (j, 0))],
                  out_shape=[_sds((TP, D), _MXU), _sds((D, TP), _MXU)], operands=[y2, y2, z, z], vmem=56,
                  sem=("parallel",))


def _merge_bwd(dmix, y2, z):
    nb = D // CW

    def body(dm_ref, y_ref, g_ref, dy_ref, dg_ref):
        dm = dm_ref[...]
        sg = _sigmoid(g_ref[...])
        dy_ref[...] = (dm * sg).astype(_MXU)
        dg_ref[...] = (dm * y_ref[...] * sg * (1.0 - sg)).astype(_MXU)

    blk = pl.BlockSpec((RT, CW), lambda i, j: (i, j))
    gblk = pl.BlockSpec((RT, CW), lambda i, j: (i, OFF_G // CW + j))
    return _pcall(body, name="merge_bwd", grid=(TP // RT, 2 * nb),
                  in_specs=[pl.BlockSpec((RT, CW), lambda i, j: (i, j % nb)), blk, gblk], out_specs=[blk, gblk],
                  out_shape=[_sds((TP, 2 * D), _MXU), _sds((TP, D_IN), _MXU)], operands=[dmix, y2, z], vmem=32,
                  sem=("parallel", "parallel"))


def _ln_out_loss(h32, out, target, g, b):
    def body(h_ref, o_ref, t_ref, g_ref, b_ref, dr_ref, dr16_ref, acc_ref):
        i = pl.program_id(0)

        @pl.when(i == 0)
        def _():
            acc_ref[...] = jnp.zeros_like(acc_ref)

        r = ALPHA * h_ref[...] + o_ref[...]
        mu = jnp.mean(r, axis=-1, keepdims=True)
        rc = r - mu
        var = jnp.mean(rc * rc, axis=-1, keepdims=True)
        rstd = lax.rsqrt(var + LN_EPS)
        xhat = rc * rstd
        gg = g_ref[...]
        y = xhat * gg + b_ref[...]
        real = (i >= 1).astype(F32)
        diff = (y - t_ref[...]) * real
        dy = diff * (1.0 / D)
        dxh = dy * gg
        m1 = jnp.mean(dxh, axis=-1, keepdims=True)
        m2 = jnp.mean(dxh * xhat, axis=-1, keepdims=True)
        dr = rstd * (dxh - m1 - xhat * m2)
        dr_ref[...] = dr
        dr16_ref[...] = dr.astype(_MXU)
        acc_ref[0:1, :] += jnp.sum(dy * xhat, axis=0, keepdims=True)
        acc_ref[1:2, :] += jnp.sum(dy, axis=0, keepdims=True)
        acc_ref[2:3, :] += jnp.sum(dr, axis=0, keepdims=True)
        acc_ref[3:4, :] += (0.5 / D) * jnp.sum(diff * diff)

    row = pl.BlockSpec((BLK, D), lambda i: (i, 0))
    vec = pl.BlockSpec((1, D), lambda i: (0, 0))
    return _pcall(body, name="ln_out_loss", grid=(NBLK,),
                  in_specs=[row, row, pl.BlockSpec((BLK, D), lambda i: (jnp.maximum(i - 1, 0), 0)), vec, vec],
                  out_specs=[row, row, pl.BlockSpec((8, D), lambda i: (0, 0))],
                  out_shape=[_sds((TP, D), F32), _sds((TP, D), _MXU), _sds((8, D), F32)],
                  operands=[h32, out, target, g, b], vmem=32, sem=("arbitrary",))


def _rnn_recompute(xr_ref, cw_ref, cb_ref, wra_ref, wri_ref, bra_ref, bri_ref, lam_ref):
    rows = lax.broadcasted_iota(jnp.int32, (TP, 1), 0)
    valid = (rows >= PAD).astype(F32)
    first = rows == PAD
    x = xr_ref[...] * valid
    cw = cw_ref[...]
    shifted = [x, pltpu.roll(x, 1, 0), pltpu.roll(x, 2, 0), pltpu.roll(x, 3, 0)]
    c = cb_ref[...] + cw[0:1, :] * shifted[0] + cw[1:2, :] * shifted[1] + cw[2:3, :] * shifted[2] + cw[3:4, :] * shifted[3]
    cm = c.astype(_MXU)
    gr = _sigmoid(jnp.dot(cm, wra_ref[...].astype(_MXU), preferred_element_type=F32) + bra_ref[...])
    gi = _sigmoid(jnp.dot(cm, wri_ref[...].astype(_MXU), preferred_element_type=F32) + bri_ref[...])
    lam = lam_ref[...]
    ls = jnp.minimum(lam, 0.0) - jnp.log(1.0 + jnp.exp(-jnp.abs(lam)))
    log_a = LRU_C * gr * ls
    a = jnp.exp(log_a)
    mult = jnp.where(first, 1.0, jnp.sqrt(1.0 - jnp.exp(2.0 * log_a)))
    return dict(valid=valid, first=first, shifted=shifted, c=c, cm=cm, gr=gr, gi=gi, ls=ls, a=a, mult=mult, lam=lam)


def _rnn_specs():
    col = pl.BlockSpec((TP, RB), lambda n: (0, n))
    vec = pl.BlockSpec((1, RB), lambda n: (0, n))
    return dict(col=col, vec=vec, cw=pl.BlockSpec((CONV_W, RB), lambda n: (0, n)),
                wblk=pl.BlockSpec((None, RB, RB), lambda n: (n, 0, 0)))


def _rnn_gates_fwd(z, conv_w, conv_b, w_ra, w_ri, b_ra, b_ri, lam):
    def body(xr_ref, cw_ref, cb_ref, wra_ref, wri_ref, bra_ref, bri_ref, lam_ref, a_ref, u_ref):
        r = _rnn_recompute(xr_ref, cw_ref, cb_ref, wra_ref, wri_ref, bra_ref, bri_ref, lam_ref)
        a_ref[...] = r["a"]
        u_ref[...] = r["mult"] * r["gi"] * r["c"] * r["valid"]

    s = _rnn_specs()
    return _pcall(body, name="rnn_gates_fwd", grid=(N_RB,),
                  in_specs=[s["col"], s["cw"], s["vec"], s["wblk"], s["wblk"], s["vec"], s["vec"], s["vec"]],
                  out_specs=[s["col"], s["col"]], out_shape=[_sds((TP, D), F32)] * 2,
                  operands=[z, conv_w, conv_b, w_ra, w_ri, b_ra, b_ri, lam], vmem=56, sem=("parallel",))


SCAN_ROWS = 272


SUB = 8


def _tile_scan(a, u, reverse):
    rows = lax.broadcasted_iota(jnp.int32, a.shape, 0)
    for d in (1, 2, 4):
        shift = SUB - d if reverse else d
        inside = (rows < SUB - d) if reverse else (rows >= d)
        u = u + a * jnp.where(inside, pltpu.roll(u, shift, 0), 0.0)
        a = a * jnp.where(inside, pltpu.roll(a, shift, 0), 1.0)
    return a, u


def _scan_fwd(a, u, *, hook):
    def body(a_ref, u_ref, h_ref, carry_ref):
        @pl.when(pl.program_id(1) == 0)
        def _():
            carry_ref[...] = jnp.zeros_like(carry_ref)

        def step(r, h):
            rows = pl.ds(pl.multiple_of(r * SUB, SUB), SUB)
            prod, part = _tile_scan(a_ref[rows, :], u_ref[rows, :], False)
            ht = part + prod * h
            h_ref[rows, :] = ht
            return ht[SUB - 1:SUB, :]

        carry_ref[...] = lax.fori_loop(0, SCAN_ROWS // SUB, step, carry_ref[...], unroll=2)

    blk = pl.BlockSpec((SCAN_ROWS, CW), lambda j, i: (i, j))
    return _pcall(body, name="scan_fwd", grid=(D // CW, TP // SCAN_ROWS), in_specs=[blk, blk], out_specs=[blk],
                  out_shape=[_sds((TP, D), F32)], operands=[a, u], scratch=[pltpu.VMEM((1, CW), F32)], vmem=32,
                  sem=("parallel", "arbitrary"), hook=hook)


def _scan_bwd(a, dh):
    nst = TP // SCAN_ROWS
    n_tiles = SCAN_ROWS // SUB

    def body(a_ref, d_ref, o_ref, lam_ref, anext_ref):
        @pl.when(pl.program_id(1) == 0)
        def _():
            lam_ref[...] = jnp.zeros_like(lam_ref)
            anext_ref[...] = jnp.zeros_like(anext_ref)

        def step(q, carry):
            lam_next, a_next = carry
            rows = pl.ds(pl.multiple_of((n_tiles - 1 - q) * SUB, SUB), SUB)
            at = a_ref[rows, :]
            last = lax.broadcasted_iota(jnp.int32, at.shape, 0) == SUB - 1
            b = jnp.where(last, a_next, pltpu.roll(at, SUB - 1, 0))
            prod, part = _tile_scan(b, d_ref[rows, :], True)
            lam = part + prod * lam_next
            o_ref[rows, :] = lam
            return lam[0:1, :], at[0:1, :]

        lam, an = lax.fori_loop(0, n_tiles, step, (lam_ref[...], anext_ref[...]), unroll=2)
        lam_ref[...] = lam
        anext_ref[...] = an

    blk = pl.BlockSpec((SCAN_ROWS, CW), lambda j, i: (nst - 1 - i, j))
    return _pcall(body, name="scan_bwd", grid=(D // CW, nst), in_specs=[blk, blk], out_specs=[blk],
                  out_shape=[_sds((TP, D), F32)], operands=[a, dh],
                  scratch=[pltpu.VMEM((1, CW), F32), pltpu.VMEM((1, CW), F32)], vmem=32,
                  sem=("parallel", "arbitrary"))[0]


def _rnn_gates_bwd(z, lam_s, hr, conv_w, conv_b, w_ra, w_ri, b_ra, b_ri, lam, dz, gsq):
    def body(xr_ref, ls_ref, hr_ref, cw_ref, cb_ref, wra_ref, wri_ref, bra_ref, bri_ref, lam_ref, dz_in, gsq_in,
             dx_ref, dw_ref, sums_ref):
        r = _rnn_recompute(xr_ref, cw_ref, cb_ref, wra_ref, wri_ref, bra_ref, bri_ref, lam_ref)
        valid, c, gr, gi, a, mult = r["valid"], r["c"], r["gr"], r["gi"], r["a"], r["mult"]
        du = ls_ref[...] * valid
        da = du * pltpu.roll(hr_ref[...], 1, 0)
        d_gi = du * mult * c
        dc = du * mult * gi
        dmult = du * gi * c
        dlog_a = da * a + jnp.where(r["first"], 0.0, -dmult * a * a / mult)
        d_gr = dlog_a * (LRU_C * r["ls"])
        dls = jnp.sum(dlog_a * (LRU_C * gr), axis=0, keepdims=True)
        dpre_r = d_gr * gr * (1.0 - gr)
        dpre_i = d_gi * gi * (1.0 - gi)
        pr = dpre_r.astype(_MXU)
        pi = dpre_i.astype(_MXU)
        dwra = lax.dot_general(r["cm"], pr, _TN, preferred_element_type=F32)
        dwri = lax.dot_general(r["cm"], pi, _TN, preferred_element_type=F32)
        for s in range(N_SH):
            dw_ref[s, 0:64, :] = dwra[64 * s:64 * (s + 1)]
            dw_ref[s, 64:128, :] = dwri[64 * s:64 * (s + 1)]
        dc = dc + lax.dot_general(pr, wra_ref[...].astype(_MXU), _NT, preferred_element_type=F32)
        dc = dc + lax.dot_general(pi, wri_ref[...].astype(_MXU), _NT, preferred_element_type=F32)
        cw = cw_ref[...]
        dx = cw[0:1, :] * dc
        for k in range(1, CONV_W):
            dx = dx + cw[k:k + 1, :] * pltpu.roll(dc, TP - k, 0)
        dx_ref[...] = (dx * valid).astype(_MXU)
        for k in range(CONV_W):
            sums_ref[k:k + 1, :] = jnp.sum(dc * r["shifted"][k], axis=0, keepdims=True)
        sums_ref[4:5, :] = jnp.sum(dc, axis=0, keepdims=True)
        sums_ref[5:6, :] = jnp.sum(dpre_r, axis=0, keepdims=True)
        sums_ref[6:7, :] = jnp.sum(dpre_i, axis=0, keepdims=True)
        sums_ref[7:8, :] = dls * _sigmoid(-r["lam"])

    s = _rnn_specs()
    return _pcall(
        body, name="rnn_gates_bwd", grid=(N_RB,),
        in_specs=[s["col"], s["col"], s["col"], s["cw"], s["vec"], s["wblk"], s["wblk"], s["vec"], s["vec"], s["vec"],
                  _ANY, _ANY],
        out_specs=[s["col"], pl.BlockSpec((N_SH, 128, RB), lambda n: (0, 3 * SQ_ROWS // 128, n)),
                   pl.BlockSpec((8, RB), lambda n: (0, n))],
        out_shape=[_sds((TP, D_IN), _MXU), _sds((N_SH, PACK_ROWS, D), F32), _sds((8, D), F32)],
        operands=[z, lam_s, hr, conv_w, conv_b, w_ra, w_ri, b_ra, b_ri, lam, dz, gsq], vmem=60, sem=("parallel",),
        aliases={10: 0, 11: 1})


def _rope_tables():
    half = HD // 2
    inv = ROPE_THETA ** (-jnp.arange(half, dtype=F32) / half)
    pos = (jnp.arange(TP) - PAD).astype(F32)
    ang = pos[:, None] * inv[None, :]
    return jnp.tile(jnp.cos(ang), (1, 4)), jnp.tile(jnp.sin(ang), (1, 4))


def _rope(x, cos_t, sin_t, sign):
    w = x.shape[1]
    lane = lax.broadcasted_iota(jnp.int32, x.shape, 1)
    first = (lane % HD) < (HD // 2)
    swapped = jnp.where(first, pltpu.roll(x, w - HD // 2, 1), pltpu.roll(x, HD // 2, 1))
    ct = jnp.tile(cos_t, (1, w // 128))
    st = jnp.tile(sin_t, (1, w // 128))
    return x * ct + swapped * jnp.where(first, -sign * st, sign * st)


def _rope_fwd(z, cos_t, sin_t):
    def body(q_ref, k_ref, v_ref, c_ref, s_ref, qo_ref, ko_ref, vo_ref):
        c = c_ref[...]
        s = s_ref[...]
        qo_ref[...] = _rope(q_ref[...], c, s, 1.0).astype(_MXU)
        ko_ref[...] = _rope(k_ref[...], c, s, 1.0).astype(_MXU)
        vo_ref[...] = v_ref[...].astype(_MXU)

    tab = pl.BlockSpec((BLK, 128), lambda i: (i, 0))
    kv = pl.BlockSpec((BLK, D_KV), lambda i: (i, 0))
    return _pcall(body, name="rope_fwd", grid=(NBLK,),
                  in_specs=[pl.BlockSpec((BLK, D), lambda i: (i, OFF_Q // D)),
                            pl.BlockSpec((BLK, D_KV), lambda i: (i, OFF_K // D_KV)),
                            pl.BlockSpec((BLK, D_KV), lambda i: (i, OFF_V // D_KV)), tab, tab],
                  out_specs=[pl.BlockSpec((BLK, D), lambda i: (i, 0)), kv, kv],
                  out_shape=[_sds((TP, D), _MXU), _sds((TP, D_KV), _MXU), _sds((TP, D_KV), _MXU)],
                  operands=[z, z, z, cos_t, sin_t], vmem=32, sem=("parallel",))


def _rope_bwd_k(dk, dv, cos_t, sin_t, dz):
    def body(dk_ref, dv_ref, c_ref, s_ref, dz_in, o_ref):
        o_ref[:, 0:D_KV] = _rope(dk_ref[...], c_ref[...], s_ref[...], -1.0).astype(_MXU)
        o_ref[:, D_KV:2 * D_KV] = dv_ref[...].astype(_MXU)

    tab = pl.BlockSpec((BLK, 128), lambda i: (i, 0))
    kv = pl.BlockSpec((BLK, D_KV), lambda i: (i, 0))
    return _pcall(body, name="rope_bwd_k", grid=(NBLK,), in_specs=[kv, kv, tab, tab, _ANY],
                  out_specs=[pl.BlockSpec((BLK, 2 * D_KV), lambda i: (i, OFF_K // (2 * D_KV)))],
                  out_shape=[_sds((TP, D_IN), _MXU)], operands=[dk, dv, cos_t, sin_t, dz], vmem=32, sem=("parallel",),
                  aliases={4: 0})[0]


def _attn_mask(i):
    ql = lax.broadcasted_iota(jnp.int32, (BLK, 3 * BLK), 0)
    kk = lax.broadcasted_iota(jnp.int32, (BLK, 3 * BLK), 1)
    kl = kk % BLK
    part = kk // BLK
    meta = (part == 0) & (kl >= PAD) & ((i >= 1) | (kl <= ql))
    prev = (part == 1) & (i >= 2) & (kl > ql)
    cur = (part == 2) & (i >= 1) & (kl <= ql)
    return meta | prev | cur


def _cat_kv(refs, g):
    return jnp.concatenate([r[:, HD * g:HD * (g + 1)] for r in refs], axis=0)


def _kv_specs():
    return [pl.BlockSpec((BLK, D_KV), lambda i: (0, 0)),
            pl.BlockSpec((BLK, D_KV), lambda i: (jnp.maximum(i - 1, 0), 0)),
            pl.BlockSpec((BLK, D_KV), lambda i: (i, 0))]


def _stack_heads(ref, g, width=HD):
    return jnp.concatenate([ref[:, width * (GRP * g + j):width * (GRP * g + j + 1)] for j in range(GRP)], axis=0)


def _attn_fwd(q, k, v, sinks, *, hook):
    def body(q_ref, k0_ref, kp_ref, kc_ref, v0_ref, vp_ref, vc_ref, sink_ref, o_ref, lse_ref, s_scr, p_scr):
        mask = _attn_mask(pl.program_id(0))
        for g in range(N_KV):
            s_scr[...] = lax.dot_general(_stack_heads(q_ref, g), _cat_kv((k0_ref, kp_ref, kc_ref), g), _NT,
                                         preferred_element_type=F32)
            for j in range(GRP):
                h = GRP * g + j
                rows = slice(BLK * j, BLK * (j + 1))
                sink = sink_ref[h]
                s = jnp.where(mask, s_scr[rows, :] * (HD ** -0.5), NEG_INF)
                mx = jnp.maximum(jnp.max(s, -1, keepdims=True), sink)
                p = jnp.exp(s - mx)
                den = jnp.sum(p, -1, keepdims=True) + jnp.exp(sink - mx)
                p_scr[rows, :] = (p * (1.0 / den)).astype(_MXU)
                lse_ref[:, h:h + 1] = mx + jnp.log(den)
            o8 = jnp.dot(p_scr[...], _cat_kv((v0_ref, vp_ref, vc_ref), g), preferred_element_type=F32)
            for j in range(GRP):
                h = GRP * g + j
                o_ref[:, HD * h:HD * (h + 1)] = o8[BLK * j:BLK * (j + 1)]

    return _pcall(body, name="attn_fwd", grid=(NBLK,),
                  in_specs=[pl.BlockSpec((BLK, D), lambda i: (i, 0))] + _kv_specs() + _kv_specs()
                           + [pl.BlockSpec(memory_space=pltpu.SMEM)],
                  out_specs=[pl.BlockSpec((BLK, D), lambda i: (i, 0)), pl.BlockSpec((BLK, N_Q), lambda i: (i, 0))],
                  out_shape=[_sds((TP, D), F32), _sds((TP, N_Q), F32)], operands=[q, k, k, k, v, v, v, sinks],
                  scratch=[pltpu.VMEM((GRP * BLK, 3 * BLK), F32), pltpu.VMEM((GRP * BLK, 3 * BLK), _MXU)],
                  vmem=40, sem=("parallel",), hook=hook)


def _attn_bwd(q, k, v, sinks, do, o, lse, cos_t, sin_t, dz, *, hook):
    def body(q_ref, k0_ref, kp_ref, kc_ref, v0_ref, vp_ref, vc_ref, sink_ref, do_ref, o_ref, lse_ref, c_ref, s_ref, dz_in,
             dq_ref, dk_ref, dv_ref, dsink_ref, dqrot_ref, s_scr, dp_scr, p_scr, ds_scr):
        i = pl.program_id(0)

        @pl.when(i == 0)
        def _():
            dk_ref[...] = jnp.zeros_like(dk_ref)
            dv_ref[...] = jnp.zeros_like(dv_ref)
            dsink_ref[...] = jnp.zeros_like(dsink_ref)

        mask = _attn_mask(i)
        row_starts = (0, pl.multiple_of(jnp.maximum(i - 1, 0) * BLK, BLK), pl.multiple_of(i * BLK, BLK))
        scale = HD ** -0.5
        for g in range(N_KV):
            gs = slice(HD * g, HD * (g + 1))
            q8 = _stack_heads(q_ref, g)
            dom = _stack_heads(do_ref, g).astype(_MXU)
            kcat = _cat_kv((k0_ref, kp_ref, kc_ref), g)
            s_scr[...] = lax.dot_general(q8, kcat, _NT, preferred_element_type=F32)
            dp_scr[...] = lax.dot_general(dom, _cat_kv((v0_ref, vp_ref, vc_ref), g), _NT, preferred_element_type=F32)
            for j in range(GRP):
                h = GRP * g + j
                hs = slice(HD * h, HD * (h + 1))
                rows = slice(BLK * j, BLK * (j + 1))
                lse_h = lse_ref[:, h:h + 1]
                delta = jnp.sum(do_ref[:, hs] * o_ref[:, hs], axis=-1, keepdims=True)
                dsink_ref[0:1, h:h + 1] += jnp.sum(-jnp.exp(sink_ref[h] - lse_h) * delta, axis=0, keepdims=True)
                p = jnp.exp(jnp.where(mask, s_scr[rows, :] * scale, NEG_INF) - lse_h)
                p_scr[rows, :] = p.astype(_MXU)
                ds_scr[rows, :] = (p * (dp_scr[rows, :] - delta) * scale).astype(_MXU)
            ds = ds_scr[...]
            dq8 = jnp.dot(ds, kcat, preferred_element_type=F32)
            dkcat = lax.dot_general(ds, q8, _TN, preferred_element_type=F32)
            dvcat = lax.dot_general(p_scr[...], dom, _TN, preferred_element_type=F32)
            for part in range(3):
                rows = pl.ds(row_starts[part], BLK)
                dk_ref[rows, gs] += dkcat[BLK * part:BLK * (part + 1)]
                dv_ref[rows, gs] += dvcat[BLK * part:BLK * (part + 1)]
            for j in range(GRP):
                h = GRP * g + j
                dqrot_ref[:, HD * h:HD * (h + 1)] = dq8[BLK * j:BLK * (j + 1)]
        dq_ref[...] = _rope(dqrot_ref[...], c_ref[...], s_ref[...], -1.0).astype(_MXU)

    row = pl.BlockSpec((BLK, D), lambda i: (i, 0))
    tab = pl.BlockSpec((BLK, 128), lambda i: (i, 0))
    full_kv = pl.BlockSpec((TP, D_KV), lambda i: (0, 0))
    return _pcall(
        body, name="attn_bwd", grid=(NBLK,),
        in_specs=[row] + _kv_specs() + _kv_specs() + [pl.BlockSpec(memory_space=pltpu.SMEM), row, row,
                  pl.BlockSpec((BLK, N_Q), lambda i: (i, 0)), tab, tab, _ANY],
        out_specs=[pl.BlockSpec((BLK, D), lambda i: (i, OFF_Q // D)), full_kv, full_kv,
                   pl.BlockSpec((8, 128), lambda i: (0, 0))],
        out_shape=[_sds((TP, D_IN), _MXU), _sds((TP, D_KV), F32), _sds((TP, D_KV), F32), _sds((8, 128), F32)],
        operands=[q, k, k, k, v, v, v, sinks, do, o, lse, cos_t, sin_t, dz],
        scratch=[pltpu.VMEM((BLK, D), F32), pltpu.VMEM((GRP * BLK, 3 * BLK), F32), pltpu.VMEM((GRP * BLK, 3 * BLK), F32),
                 pltpu.VMEM((GRP * BLK, 3 * BLK), _MXU), pltpu.VMEM((GRP * BLK, 3 * BLK), _MXU)],
        vmem=48, sem=("arbitrary",), aliases={13: 0}, hook=hook)


def _cast_into_slot(w32, pos, *, name, tr):
    r, cc = w32.shape

    def body(pos_ref, w_ref, o_ref):
        o_ref[...] = w_ref[...].astype(BF16)

    return _pcall(body, name=name, grid=(r // tr,), in_specs=[pl.BlockSpec((tr, cc), lambda i, p: (i, 0))],
                  out_specs=[pl.BlockSpec((None, tr, cc), lambda i, p: (p[1], i, 0))],
                  out_shape=[_sds((N_SH, r, cc), BF16)], operands=[w32], vmem=32, sem=("parallel",), prefetch=(pos,))[0]


def _pair_add(g, got, pos, *, name, tr, g_has_both_halves, hook=None):
    n, h, cc = got.shape
    nt = h // tr

    def body(pos_ref, g_ref, r_ref, own_ref, s16_ref):
        s = g_ref[...] + r_ref[...].astype(F32)
        s16_ref[...] = s.astype(BF16)

        @pl.when(pl.program_id(1) == pos_ref[1])
        def _():
            own_ref[...] = s

    g_index = (lambda i, s, p: (s, p[0] * nt + i, 0)) if g_has_both_halves else (lambda i, s, p: (s, i, 0))
    return _pcall(body, name=name, grid=(nt, n),
                  in_specs=[pl.BlockSpec((None, tr, cc), g_index), pl.BlockSpec((None, tr, cc), lambda i, s, p: (s, i, 0))],
                  out_specs=[pl.BlockSpec((tr, cc), lambda i, s, p: (i, 0)),
                             pl.BlockSpec((None, tr, cc), lambda i, s, p: (s, i, 0))],
                  out_shape=[_sds((h, cc), F32), _sds((n, h, cc), BF16)], operands=[g, got], vmem=40,
                  sem=("parallel", "arbitrary"), prefetch=(pos,), hook=hook)


def _relay_add(s16, raw, pos, *, name, tr):
    _, q, cc = raw.shape
    nt = q // tr

    def body(pos_ref, s_ref, r_ref, o_ref):
        o_ref[...] = (s_ref[...].astype(F32) + r_ref[...].astype(F32)).astype(BF16)

    blk = pl.BlockSpec((None, tr, cc), lambda k, i, p: (k, i, 0))
    return _pcall(body, name=name, grid=(2, nt),
                  in_specs=[pl.BlockSpec((None, tr, cc), lambda k, i, p: (p[3 + k], (1 - k) * nt + i, 0)), blk],
                  out_specs=[blk], out_shape=[_sds((2, q, cc), BF16)], operands=[s16, raw], vmem=40,
                  sem=("parallel", "parallel"), prefetch=(pos,))[0]


def _sum_chips(own, got, pos, *, name, tr):
    h, cc = own.shape
    n_got = got.shape[0]
    nt = h // tr

    def body(pos_ref, o_ref, r_ref, out_ref):
        acc = o_ref[...]
        for k in range(n_got):
            acc = acc + r_ref[k].astype(F32)
        out_ref[...] = acc

    return _pcall(body, name=name, grid=(nt,),
                  in_specs=[pl.BlockSpec((tr, cc), lambda i, p: (i, 0)),
                            pl.BlockSpec((n_got, tr, cc), lambda i, p: (0, i, 0))],
                  out_specs=[pl.BlockSpec((tr, cc), lambda i, p: (p[0] * nt + i, 0))],
                  out_shape=[_sds((2 * h, cc), F32)], operands=[own, got], vmem=40, sem=("parallel",), prefetch=(pos,))[0]


def _adamw(w, g, m, v, *, name, tr, g_row0=0):
    r, cc = w.shape
    g_blk0 = g_row0 // tr

    def body(w_ref, g_ref, m_ref, v_ref, go_ref, d_ref, mo_ref, vo_ref):
        gg = g_ref[...]
        go_ref[...] = gg
        m_new = ADAM_B1 * m_ref[...] + (1.0 - ADAM_B1) * gg
        v_new = ADAM_B2 * v_ref[...] + (1.0 - ADAM_B2) * (gg * gg)
        m_hat = m_new / (1.0 - ADAM_B1 ** ADAM_STEP)
        v_hat = v_new / (1.0 - ADAM_B2 ** ADAM_STEP)
        d_ref[...] = -ADAM_LR * (m_hat / (jnp.sqrt(v_hat) + ADAM_EPS) + ADAM_WD * w_ref[...])
        mo_ref[...] = m_new
        vo_ref[...] = v_new

    blk = pl.BlockSpec((tr, cc), lambda i: (i, 0))
    gblk = pl.BlockSpec((tr, cc), lambda i: (g_blk0 + i, 0))
    return _pcall(body, name=name, grid=(r // tr,), in_specs=[blk, gblk, blk, blk], out_specs=[blk] * 4,
                  out_shape=[_sds((r, cc), F32)] * 4, operands=[w, g, m, v], vmem=48, sem=("parallel",))


_SMALL_ROWS = 40
_B_IN_ROWS = 7


def _row_pad(v, rows):
    flat = v.reshape(-1)
    return jnp.pad(flat, (0, rows * D - flat.shape[0])).reshape(rows, D)


def _pack_ra(w):
    return w.transpose(1, 0, 2).reshape(64, D)


def _unpack_ra(p, like):
    return p.reshape(64, N_RB, RB).transpose(1, 0, 2).reshape(like.shape)


def _gate_full(g4):
    return g4.reshape(N_SH, 64, N_RB, RB).transpose(2, 0, 1, 3).reshape(N_RB, RB, RB)


def kernel(x, meta_tokens, ln_emb_g, ln_emb_b, w_in, b_in, conv_w, conv_b, w_ra, b_ra, w_ri, b_ri, lru_lambda, sinks, w_rnn_out, w_attn_out, w_o, b_o, ln_g, ln_b, loss_target, m_meta_tokens, m_ln_emb_g, m_ln_emb_b, m_w_in, m_b_in, m_conv_w, m_conv_b, m_w_ra, m_b_ra, m_w_ri, m_b_ri, m_lru_lambda, m_sinks, m_w_rnn_out, m_w_attn_out, m_w_o, m_b_o, m_ln_g, m_ln_b, v_meta_tokens, v_ln_emb_g, v_ln_emb_b, v_w_in, v_b_in, v_conv_w, v_conv_b, v_w_ra, v_b_ra, v_w_ri, v_b_ri, v_lru_lambda, v_sinks, v_w_rnn_out, v_w_attn_out, v_w_o, v_b_o, v_ln_g, v_ln_b):
    xi, yi, ci = _my_pos()
    shard = 2 * xi + yi
    pos = jnp.stack([ci, shard, 1 - ci, shard ^ 2, shard ^ 1]).astype(jnp.int32)
    cos_t, sin_t = _rope_tables()
    zero_bias = jnp.zeros((1, D), F32)
    ln_emb_g2, ln_emb_b2 = ln_emb_g[None], ln_emb_b[None]

    small = jnp.concatenate([conv_w[0], meta_tokens, jnp.zeros((4, 512), F32)], axis=0)
    small4 = _gather_small(small)
    conv_w_full = small4[:, 0:4].transpose(1, 0, 2).reshape(CONV_W, D)
    meta_full = small4[:, 4:20].transpose(1, 0, 2).reshape(N_META, D)
    w_own = _cast_into_slot(w_in[0], pos, name="cast_w_in", tr=256)
    wa_own = _cast_into_slot(jnp.concatenate([w_attn_out[0], w_o[0], _pack_ra(w_ra[0]), _pack_ra(w_ri[0])], axis=0),
                             pos, name="cast_w_a", tr=288)
    wb_own = _cast_into_slot(w_rnn_out[0], pos, name="cast_w_b", tr=256)

    h32, h16, h16_t = _ln_emb_fwd(x[0], meta_full, ln_emb_g2, ln_emb_b2)
    order = jnp.stack([shard, shard ^ 2, shard ^ 1, shard ^ 3]).astype(jnp.int32)
    z, w_in4 = _mm_z_gather(h16, w_own, b_in, order)
    q, k, v = _rope_fwd(z, cos_t, sin_t)
    o, lse, wa4 = _attn_fwd(q, k, v, sinks[0], hook=_hook_gather(wa_own, 0.6))
    w_ra_full = _gate_full(wa4[:, 2 * SQ_ROWS:2 * SQ_ROWS + 64])
    w_ri_full = _gate_full(wa4[:, 2 * SQ_ROWS + 64:2 * SQ_ROWS + 128])
    a_dec, u_in = _rnn_gates_fwd(z, conv_w_full, conv_b, w_ra_full, w_ri_full, b_ra, b_ri, lru_lambda)
    hr, wb4 = _scan_fwd(a_dec, u_in, hook=_hook_gather(wb_own, 0.6))
    sq_w = {0: (wb4, 0), 1: (wa4, 0), 2: (wa4, 1)}

    def sq_nn(a, kk, bias, name, out_cols, out_index, carried=None):
        wp, blk = sq_w[kk]
        return _mm_nn(a, wp, bias, name=name, grid=(2, D // CW), tm=HALF_TP, tn=CW, k=D, a_index=lambda i, j: (i, 0),
                      w_block=(N_SH, SQ_ROWS, CW), w_index=lambda i, j: (0, blk, j), out_cols=out_cols,
                      out_index=out_index, carried=carried)[0]

    def sq_nt(a, a_blk, kk, name, hook=None):
        wp, blk = sq_w[kk]
        return _mm_nt(a, wp, name=name, grid=(2, N_SH, 1), tm=HALF_TP, tn=SQ_ROWS, tk=D,
                      a_index=lambda i, j, q: (i, a_blk), w_block=(None, SQ_ROWS, D),
                      w_index=lambda i, j, q: (j, blk, 0), out_cols=D, hook=hook)

    ya_in, ya_in_t = _mul_silu_fwd(hr, z, OFF_GR, name="gate_a_fwd")
    y2 = sq_nn(ya_in, 0, zero_bias, "mm_ya", 2 * D, lambda i, j: (i, j))
    yb_in, yb_in_t = _mul_silu_fwd(o, z, OFF_GA, name="gate_b_fwd")
    y2 = sq_nn(yb_in, 1, zero_bias, "mm_yb", 2 * D, lambda i, j: (i, D // CW + j), carried=y2)
    mixed, mixed_t = _merge_fwd(y2, z)
    out = sq_nn(mixed, 2, b_o, "mm_out", D, lambda i, j: (i, j))
    dr, dr16, sums_o = _ln_out_loss(h32, out, loss_target[0], ln_g, ln_b)

    def sq_tn(at, b, b_blk, kk, name, carried=None):
        return _mm_tn(at, b, name=name, grid=(N_SH, 1), tm=SQ_ROWS, tn=D, a_index=lambda i, j: (i, 0),
                      b_index=lambda i, j: (0, b_blk), out_shape=(N_SH, PACK_ROWS, D),
                      out_block=(None, SQ_ROWS, D), out_index=lambda i, j: (i, kk, 0), carried=carried)[0]

    gsq = sq_tn(mixed_t, dr16, 0, 2, "mm_dwo")
    dmix = sq_nt(dr16, 0, 2, "mm_dmix")[0]
    dy2, dz = _merge_bwd(dmix, y2, z)
    gsq = sq_tn(ya_in_t, dy2, 0, 0, "mm_dwrnn", carried=gsq)
    gsq = sq_tn(yb_in_t, dy2, 1, 1, "mm_dwattn", carried=gsq)
    dya_in = sq_nt(dy2, 0, 0, "mm_dyain")[0]
    dhr, dz = _mul_silu_bwd(dya_in, hr, z, OFF_GR, dz, name="gate_a_bwd")
    lam_s = _scan_bwd(a_dec, dhr)
    dz, gsq, sums_r = _rnn_gates_bwd(z, lam_s, hr, conv_w_full, conv_b, w_ra_full, w_ri_full, b_ra, b_ri, lru_lambda, dz, gsq)
    dyb_in = sq_nt(dy2, 1, 1, "mm_dybin")[0]
    do, dz, gsq, got_sq = _mul_silu_bwd(dyb_in, o, z, OFF_GA, dz, name="gate_b_bwd", hook=_hook_pair(gsq, True))
    own_sq, s16_sq = _pair_add(gsq, got_sq, pos, name="red_w_sq_add", tr=208, g_has_both_halves=True)
    dz, dk_rot, dv32, dsink, s16_sq, fin_sq, raw_sq = _attn_bwd(q, k, v, sinks[0], do, o, lse, cos_t, sin_t, dz,
                                                                hook=_hook_scatter_direct(s16_sq))
    dz = _rope_bwd_k(dk_rot, dv32, cos_t, sin_t, dz)
    comb_sq = _relay_add(s16_sq, raw_sq, pos, name="red_w_sq_relay", tr=208)

    def dwin(half_idx, name, hook, **kw):
        return _mm_tn(h16_t, dz, name=name, grid=(1, N_SH * PER_IN), tm=D // 2, tn=TN_IN,
                      a_index=lambda i, j, p: (p[half_idx], 0), b_index=lambda i, j, p: (0, j),
                      out_shape=(N_SH, D // 2, W_IN_COLS), out_block=(None, D // 2, TN_IN),
                      out_index=lambda i, j, p: (j // PER_IN, 0, j % PER_IN), prefetch=(pos,), hook=hook, **kw)

    gin_sib, g_b_in, comb_sq, fin_sq = dwin(2, "mm_dwin_sib", _hook_scatter_relay(comb_sq, fin_sq),
                                            out_dtype=BF16, colsum=True)
    red_sq = _sum_chips(own_sq, fin_sq, pos, name="red_w_sq_sum", tr=208)
    gin_own, gin_sib, got_in = dwin(0, "mm_dwin_own", _hook_pair(gin_sib, False))
    own_in, s16_in, g_sq = _pair_add(gin_own, got_in, pos, name="red_w_in_add", tr=128, g_has_both_halves=False,
                                     hook=_hook_halves(red_sq))
    dhz, s16_in, fin_in, raw_in = _mm_nt(dz, w_in4, name="mm_dhz", grid=(2, 2, N_SH), tm=HALF_TP, tn=D // 2,
                                         tk=W_IN_COLS, a_index=lambda i, j, q: (i, q),
                                         w_block=(None, D // 2, W_IN_COLS), w_index=lambda i, j, q: (q, j, 0), out_cols=D,
                                         hook=_hook_scatter_direct(s16_in))
    comb_in = _relay_add(s16_in, raw_in, pos, name="red_w_in_relay", tr=128)
    g_x, g_meta_local, sums_e, comb_in, fin_in = _ln_emb_bwd(x[0], meta_full, ln_emb_g2, dr, dhz,
                                                             hook=_hook_scatter_relay(comb_in, fin_in))
    red_in = _sum_chips(own_in, fin_in, pos, name="red_w_in_sum", tr=128)

    spack = jnp.concatenate([
        sums_e[0:1], sums_e[1:2], _row_pad(g_b_in, _B_IN_ROWS), sums_r[0:4], sums_r[4:5], sums_r[5:6], sums_r[6:7],
        sums_r[7:8], _row_pad(dsink[0:1, 0:N_Q], 1), sums_o[2:3], sums_o[0:1], sums_o[1:2], g_meta_local, sums_o[3:4],
        jnp.zeros((_SMALL_ROWS - 38, D), F32)], axis=0)
    sred, g_in = _allreduce_small(spack, red_in)

    big = {"w_in": [t.reshape(w_in.shape) for t in
                    _adamw(w_in[0], g_in, m_w_in[0], v_w_in[0], name="adamw_w_in", tr=128)]}
    for kk, (n, w_, m_, v_) in enumerate([("w_rnn_out", w_rnn_out, m_w_rnn_out, v_w_rnn_out),
                                          ("w_attn_out", w_attn_out, m_w_attn_out, v_w_attn_out), ("w_o", w_o, m_w_o, v_w_o)]):
        big[n] = [t.reshape(w_.shape) for t in
                  _adamw(w_[0], g_sq, m_[0], v_[0], name="adamw_" + n, tr=256, g_row0=SQ_ROWS * kk)]
    for kk, (n, w_, m_, v_) in enumerate([("w_ra", w_ra, m_w_ra, v_w_ra), ("w_ri", w_ri, m_w_ri, v_w_ri)]):
        big[n] = [_unpack_ra(t, w_) for t in
                  _adamw(_pack_ra(w_[0]), g_sq, _pack_ra(m_[0]), _pack_ra(v_[0]), name="adamw_" + n, tr=64,
                         g_row0=3 * SQ_ROWS + 64 * kk)]

    loss = sred[37, 0]
    col0 = shard * 512
    g_conv_w = lax.dynamic_slice(sred[9:13], (0, col0), (CONV_W, 512))
    g_meta = lax.dynamic_slice(sred[21:37], (0, col0), (N_META, 512))
    small_g = {"ln_emb_g": sred[0:1], "ln_emb_b": sred[1:2], "b_in": sred[2:9], "conv_w": g_conv_w.reshape(1, D),
               "conv_b": sred[13:14], "b_ra": sred[14:15], "b_ri": sred[15:16], "lru_lambda": sred[16:17],
               "sinks": sred[17:18], "b_o": sred[18:19], "ln_g": sred[19:20], "ln_b": sred[20:21],
               "meta_tokens": g_meta.reshape(4, D)}
    small_names = list(small_g)

    def small_pack(vals):
        rows = []
        for n in small_names:
            a = vals[n]
            if n == "b_in":
                a = _row_pad(a, _B_IN_ROWS)
            elif n == "sinks":
                a = _row_pad(a, 1)
            else:
                a = a.reshape(-1, D)
            rows.append(a)
        return jnp.concatenate(rows + [jnp.zeros((24 - 22, D), F32)], axis=0)

    w_small = dict(ln_emb_g=ln_emb_g, ln_emb_b=ln_emb_b, b_in=b_in, conv_w=conv_w, conv_b=conv_b, b_ra=b_ra, b_ri=b_ri,
                   lru_lambda=lru_lambda, sinks=sinks, b_o=b_o, ln_g=ln_g, ln_b=ln_b, meta_tokens=meta_tokens)
    m_small = dict(ln_emb_g=m_ln_emb_g, ln_emb_b=m_ln_emb_b, b_in=m_b_in, conv_w=m_conv_w, conv_b=m_conv_b, b_ra=m_b_ra,
                   b_ri=m_b_ri, lru_lambda=m_lru_lambda, sinks=m_sinks, b_o=m_b_o, ln_g=m_ln_g, ln_b=m_ln_b,
                   meta_tokens=m_meta_tokens)
    v_small = dict(ln_emb_g=v_ln_emb_g, ln_emb_b=v_ln_emb_b, b_in=v_b_in, conv_w=v_conv_w, conv_b=v_conv_b, b_ra=v_b_ra,
                   b_ri=v_b_ri, lru_lambda=v_lru_lambda, sinks=v_sinks, b_o=v_b_o, ln_g=v_ln_g, ln_b=v_ln_b,
                   meta_tokens=v_meta_tokens)
    g_small_pack = jnp.concatenate([small_g[n] for n in small_names] + [jnp.zeros((2, D), F32)], axis=0)
    small_res = _adamw(small_pack(w_small), g_small_pack, small_pack(m_small), small_pack(v_small),
                       name="adamw_small", tr=24)

    small_rows = {}
    r0 = 0
    for n in small_names:
        nrows = small_g[n].shape[0]
        small_rows[n] = (r0, nrows)
        r0 += nrows

    def small_out(packed, n, like):
        a, nrows = small_rows[n]
        flat = packed[a:a + nrows].reshape(-1)
        return flat[:like.size].reshape(like.shape)

    weights = dict(meta_tokens=meta_tokens, ln_emb_g=ln_emb_g, ln_emb_b=ln_emb_b, w_in=w_in, b_in=b_in, conv_w=conv_w,
                   conv_b=conv_b, w_ra=w_ra, b_ra=b_ra, w_ri=w_ri, b_ri=b_ri, lru_lambda=lru_lambda, sinks=sinks,
                   w_rnn_out=w_rnn_out, w_attn_out=w_attn_out, w_o=w_o, b_o=b_o, ln_g=ln_g, ln_b=ln_b)

    def outputs(which):
        return [big[n][which] if n in big else small_out(small_res[which], n, like) for n, like in weights.items()]

    return (loss, g_x[None], *outputs(0), *outputs(1), *outputs(2), *outputs(3))
```

```python
import jax
import jax.numpy as jnp
from jax import lax
from jax.experimental import pallas as pl
from jax.experimental.pallas import tpu as pltpu

F32 = jnp.float32
BF16 = jnp.bfloat16
_MXU = jnp.bfloat16

D = 2048
SEQ = 2048
N_META = 16
BLK = 128
PAD = BLK - N_META
TP = PAD + N_META + SEQ
NBLK = TP // BLK
HALF_TP = TP // 2
N_RB = 8
RB = 256
CONV_W = 4
LRU_C = 8.0
HD = 64
N_Q = 32
N_KV = 4
GRP = 8
D_KV = 256
NEG_INF = -1e30
LN_EPS = 1e-5
ALPHA = 2.0 ** 0.25
ROPE_THETA = 10000.0
OFF_GR, OFF_Q, OFF_K, OFF_V, OFF_GA, OFF_G = 2048, 4096, 6144, 6400, 6656, 8704
D_IN = 12800
N_SH = 4
W_IN_COLS = D_IN // N_SH
TN_IN = 640
PER_IN = W_IN_COLS // TN_IN
CW = 512
RT = TP // 4
SQ_ROWS = 512
PACK_ROWS = 3 * SQ_ROWS + 128

ADAM_LR = 0.001
ADAM_B1 = 0.9
ADAM_B2 = 0.999
ADAM_EPS = 1e-08
ADAM_WD = 0.01
ADAM_STEP = 10

MESH = pl.DeviceIdType.MESH
_MIB = 1024 * 1024
_ANY = pl.BlockSpec(memory_space=pl.ANY)
_NT = (((1,), (1,)), ((), ()))
_TN = (((0,), (0,)), ((), ()))


def _sds(shape, dtype):
    return jax.ShapeDtypeStruct(shape, dtype)


def _sigmoid(x):
    return 1.0 / (1.0 + jnp.exp(-x))


def _my_pos():
    return lax.axis_index("x"), lax.axis_index("y"), lax.axis_index("c")


def _other_chips(x, y):
    return [(1 - x, y), (x, 1 - y), (1 - x, 1 - y)]


def _remote(src, dst, send_sems, recv_sems, k, dev):
    return pltpu.make_async_remote_copy(src_ref=src, dst_ref=dst, send_sem=send_sems.at[k], recv_sem=recv_sems.at[k],
                                        device_id=dev, device_id_type=MESH)


class _Hook:
    def __init__(self, carried, landing, n_sems, start, finish, mid=None, mid_frac=0.5):
        self.carried, self.landing, self.n_sems, self.start, self.finish = list(carried), list(landing), n_sems, start, finish
        self.mid, self.mid_frac = mid, mid_frac


def _hook_gather(buf, mid_frac):
    half = buf.shape[1] // 2
    quarter = half // 2

    def geom(o, ss, rs):
        x, y, c = _my_pos()
        xn, yn, dg = _other_chips(x, y)
        slot = lambda p: 2 * p[0] + p[1]
        mine_rows = pl.ds(pl.multiple_of(c * half, 16), half)
        sib_rows = pl.ds(pl.multiple_of((1 - c) * half, 16), half)
        q_rows = lambda r: pl.ds(pl.multiple_of(c * half + r * quarter, 16), quarter)

        def cp(k, s, rows, dev):
            part = o.at[s, rows]
            return _remote(part, part, ss, rs, k, dev)

        return dict(
            direct=lambda k, s: cp(k, s, mine_rows, ((xn, yn)[k][0], (xn, yn)[k][1], c)),
            relay=lambda r, s: cp(2 + r, s, q_rows(r), ((yn, xn)[r][0], (yn, xn)[r][1], c)),
            sibling=lambda k, s, mine: cp(4 + k, s, mine_rows if mine else sib_rows, (x, y, 1 - c)),
            me=slot((x, y)), slots=(slot(xn), slot(yn), slot(dg)))

    def start(car, land, ss, rs):
        g = geom(car[0], ss, rs)
        g["direct"](0, g["me"]).start()
        g["direct"](1, g["me"]).start()

    def mid(car, land, ss, rs):
        g = geom(car[0], ss, rs)
        for k in range(2):
            g["direct"](k, g["slots"][k]).wait_recv()
            g["relay"](k, g["slots"][k]).start()
            g["sibling"](k, g["slots"][k], True).start()

    def finish(car, land, ss, rs):
        g = geom(car[0], ss, rs)
        dslot = g["slots"][2]
        g["relay"](0, dslot).wait_recv()
        g["relay"](1, dslot).wait_recv()
        g["sibling"](2, dslot, True).start()
        for k in range(3):
            g["sibling"](k, g["slots"][k], False).wait_recv()
        for k in range(2):
            g["direct"](k, g["me"]).wait_send()
            g["relay"](k, g["slots"][k]).wait_send()
        for k in range(3):
            g["sibling"](k, g["slots"][k], True).wait_send()

    return _Hook([buf], [], 7, start, finish, mid=mid, mid_frac=mid_frac)


def _hook_pair(g, half_rows, slots=(0, 1, 2, 3), land=None):
    n, r, cc = g.shape
    h = r // 2 if half_rows else r

    def plan(car, landing, ss, rs):
        x, y, c = _my_pos()
        dst = landing[0] if land is None else car[1]
        cps = []
        for k, s in enumerate(slots):
            src = car[0].at[s, pl.ds(pl.multiple_of((1 - c) * h, 8), h)] if half_rows else car[0].at[s]
            cps.append(_remote(src, dst.at[s], ss, rs, k, (x, y, 1 - c)))
        return cps

    def start(car, land, ss, rs):
        for cp in plan(car, land, ss, rs):
            cp.start()

    def finish(car, land, ss, rs):
        cps = plan(car, land, ss, rs)
        for cp in cps:
            cp.wait_recv()
        for cp in cps:
            cp.wait_send()

    if land is None:
        return _Hook([g], [_sds((n, h, cc), g.dtype)], len(slots), start, finish)
    return _Hook([g, land], [], len(slots), start, finish)


def _hook_scatter_direct(s16):
    _, h, cc = s16.shape
    q = h // 2

    def plan(car, land, ss, rs):
        x, y, c = _my_pos()
        xn, yn, dg = _other_chips(x, y)
        s, (final, raw) = car[0], land
        slot = lambda p: 2 * p[0] + p[1]
        q0, q1 = pl.ds(0, q), pl.ds(q, q)
        return [_remote(s.at[slot(xn), q0], final.at[0, q0], ss, rs, 0, (xn[0], xn[1], c)),
                _remote(s.at[slot(yn), q1], final.at[1, q1], ss, rs, 1, (yn[0], yn[1], c)),
                _remote(s.at[slot(dg), q0], raw.at[1], ss, rs, 2, (xn[0], xn[1], c)),
                _remote(s.at[slot(dg), q1], raw.at[0], ss, rs, 3, (yn[0], yn[1], c))]

    def start(car, land, ss, rs):
        for cp in plan(car, land, ss, rs):
            cp.start()

    def finish(car, land, ss, rs):
        cps = plan(car, land, ss, rs)
        for cp in cps:
            cp.wait_recv()
        for cp in cps:
            cp.wait_send()

    return _Hook([s16], [_sds((2, h, cc), s16.dtype), _sds((2, q, cc), s16.dtype)], 4, start, finish)


def _hook_scatter_relay(comb, final):
    _, q, _ = comb.shape

    def plan(car, ss, rs):
        x, y, c = _my_pos()
        xn, yn, _ = _other_chips(x, y)
        cb, final_ref = car
        return [_remote(cb.at[0], final_ref.at[0, pl.ds(q, q)], ss, rs, 0, (xn[0], xn[1], c)),
                _remote(cb.at[1], final_ref.at[1, pl.ds(0, q)], ss, rs, 1, (yn[0], yn[1], c))]

    def start(car, land, ss, rs):
        for cp in plan(car, ss, rs):
            cp.start()

    def finish(car, land, ss, rs):
        cps = plan(car, ss, rs)
        for cp in cps:
            cp.wait_recv()
        for cp in cps:
            cp.wait_send()

    return _Hook([comb, final], [], 2, start, finish)


def _hook_halves(full):
    h = full.shape[0] // 2

    def half_copy(car, ss, rs, which):
        x, y, c = _my_pos()
        rows = car[0].at[pl.ds(pl.multiple_of((c + which - 2 * c * which) * h, 8), h)]
        return _remote(rows, rows, ss, rs, 0, (x, y, 1 - c))

    def start(car, land, ss, rs):
        half_copy(car, ss, rs, 0).start()

    def finish(car, land, ss, rs):
        half_copy(car, ss, rs, 1).wait_recv()
        half_copy(car, ss, rs, 0).wait_send()

    return _Hook([full], [], 1, start, finish)


def _pcall(body, *, name, grid, in_specs, out_specs, out_shape, operands, scratch=(), vmem=48, sem=None,
           prefetch=(), aliases=None, hook=None):
    n_pre, n_in, n_out, n_scr = len(prefetch), len(in_specs), len(out_specs), len(scratch)
    in_specs, out_specs, out_shape, scratch = list(in_specs), list(out_specs), list(out_shape), list(scratch)
    io_alias = {n_pre + a: b for a, b in (aliases or {}).items()}
    operands = list(operands)
    kernel_body = body
    if hook is not None:
        n_car, n_land = len(hook.carried), len(hook.landing)
        for t, arr in enumerate(hook.carried):
            io_alias[n_pre + n_in + t] = n_out + t
        in_specs += [_ANY] * n_car
        out_specs += [_ANY] * (n_car + n_land)
        out_shape += [_sds(a.shape, a.dtype) for a in hook.carried] + hook.landing
        scratch += [pltpu.SemaphoreType.DMA((hook.n_sems,)), pltpu.SemaphoreType.DMA((hook.n_sems,))]
        operands += hook.carried
        sem = ("arbitrary",) * len(grid)

        def kernel_body(*refs):
            pre, rest = refs[:n_pre], refs[n_pre:]
            ins = rest[:n_in]
            outs = rest[n_in + n_car:n_in + n_car + n_out]
            car = rest[n_in + n_car + n_out:n_in + 2 * n_car + n_out]
            land = rest[n_in + 2 * n_car + n_out:n_in + 2 * n_car + n_out + n_land]
            scr = rest[n_in + 2 * n_car + n_out + n_land:]
            send_sems, recv_sems = scr[n_scr], scr[n_scr + 1]
            first = pl.program_id(0) == 0
            last = pl.program_id(0) == grid[0] - 1
            for d in range(1, len(grid)):
                first = first & (pl.program_id(d) == 0)
                last = last & (pl.program_id(d) == grid[d] - 1)

            @pl.when(first)
            def _():
                hook.start(car, land, send_sems, recv_sems)

            if hook.mid is not None:
                step = pl.program_id(0)
                total = grid[0]
                for d in range(1, len(grid)):
                    step = step * grid[d] + pl.program_id(d)
                    total *= grid[d]

                @pl.when(step == int(total * hook.mid_frac))
                def _():
                    hook.mid(car, land, send_sems, recv_sems)

            body(*pre, *ins, *outs, *scr[:n_scr])

            @pl.when(last)
            def _():
                hook.finish(car, land, send_sems, recv_sems)

    params = pltpu.CompilerParams(vmem_limit_bytes=vmem * _MIB, dimension_semantics=sem,
                                  has_side_effects=hook is not None)
    if n_pre:
        call = pl.pallas_call(
            kernel_body, name=name, out_shape=out_shape, input_output_aliases=io_alias, compiler_params=params,
            grid_spec=pltpu.PrefetchScalarGridSpec(num_scalar_prefetch=n_pre, grid=grid, in_specs=in_specs,
                                                   out_specs=out_specs, scratch_shapes=scratch))
    else:
        call = pl.pallas_call(kernel_body, name=name, grid=grid, in_specs=in_specs, out_specs=out_specs,
                              out_shape=out_shape, scratch_shapes=scratch, input_output_aliases=io_alias,
                              compiler_params=params)
    return call(*prefetch, *operands)


def _gather_small(blk):
    r, cc = blk.shape

    def body(x_ref, o_ref, send_sems, recv_sems):
        x, y, c = _my_pos()
        me = 2 * x + y
        o_ref[me] = x_ref[...]
        sends = [_remote(x_ref, o_ref.at[me], send_sems, recv_sems, k, (px, py, c))
                 for k, (px, py) in enumerate(_other_chips(x, y))]
        for cp in sends:
            cp.start()
        for k, (px, py) in enumerate(_other_chips(x, y)):
            _remote(x_ref, o_ref.at[2 * px + py], send_sems, recv_sems, k, (px, py, c)).wait_recv()
        for cp in sends:
            cp.wait_send()

    vm = pl.BlockSpec(memory_space=pltpu.VMEM)
    return pl.pallas_call(
        body, name="gather_small", in_specs=[vm], out_specs=vm, out_shape=_sds((N_SH, r, cc), blk.dtype),
        scratch_shapes=[pltpu.SemaphoreType.DMA((3,)), pltpu.SemaphoreType.DMA((3,))],
        compiler_params=pltpu.CompilerParams(has_side_effects=True),
    )(blk)


N_DEV = 8


def _allreduce_small(pack, halves):
    r, cc = pack.shape
    ride = _hook_halves(halves)

    def body(x_ref, h_in_ref, o_ref, h_ref, buf_ref, send_sems, recv_sems, h_ss, h_rs):
        del h_in_ref
        ride.start([h_ref], [], h_ss, h_rs)
        x, y, c = _my_pos()
        me = 4 * x + 2 * y + c
        buf_ref[me] = x_ref[...]
        cps = []
        for k in range(1, N_DEV):
            peer = (x ^ ((k >> 2) & 1), y ^ ((k >> 1) & 1), c ^ (k & 1))
            cps.append(_remote(x_ref, buf_ref.at[me], send_sems, recv_sems, k - 1, peer))
        for cp in cps:
            cp.start()
        for k in range(1, N_DEV):
            peer = (x ^ ((k >> 2) & 1), y ^ ((k >> 1) & 1), c ^ (k & 1))
            src = 4 * peer[0] + 2 * peer[1] + peer[2]
            _remote(x_ref, buf_ref.at[src], send_sems, recv_sems, k - 1, peer).wait_recv()
        acc = buf_ref[0]
        for d in range(1, N_DEV):
            acc = acc + buf_ref[d]
        o_ref[...] = acc
        for cp in cps:
            cp.wait_send()
        ride.finish([h_ref], [], h_ss, h_rs)

    vm = pl.BlockSpec(memory_space=pltpu.VMEM)
    return pl.pallas_call(
        body, name="allreduce_small", in_specs=[vm, _ANY], out_specs=[vm, _ANY],
        out_shape=[_sds((r, cc), F32), _sds(halves.shape, halves.dtype)], input_output_aliases={1: 1},
        scratch_shapes=[pltpu.VMEM((N_DEV, r, cc), F32), pltpu.SemaphoreType.DMA((N_DEV - 1,)),
                        pltpu.SemaphoreType.DMA((N_DEV - 1,)), pltpu.SemaphoreType.DMA((1,)), pltpu.SemaphoreType.DMA((1,))],
        compiler_params=pltpu.CompilerParams(has_side_effects=True),
    )(pack, halves)


def _mm_nn(a, w, bias, *, name, grid, tm, tn, k, a_index, w_block, w_index, out_cols, out_index, carried=None, hook=None):
    m = a.shape[0]

    def body(a_ref, w_ref, b_ref, *rest):
        o_ref = rest[-1]
        wv = w_ref[...]
        acc = jnp.dot(a_ref[...].astype(_MXU), wv.reshape(k, tn).astype(_MXU), preferred_element_type=F32)
        o_ref[...] = acc + b_ref[...]

    operands = [a, w, bias] + ([carried] if carried is not None else [])
    return _pcall(
        body, name=name, grid=grid,
        in_specs=[pl.BlockSpec((tm, k), a_index), pl.BlockSpec(w_block, w_index),
                  pl.BlockSpec((1, tn), lambda i, j: (0, j))] + ([_ANY] if carried is not None else []),
        out_specs=[pl.BlockSpec((tm, tn), out_index)], out_shape=[_sds((m, out_cols), F32)], operands=operands,
        vmem=56, sem=("parallel", "parallel"), aliases={3: 0} if carried is not None else None, hook=hook)


def _mm_nt(a, w, *, name, grid, tm, tn, tk, a_index, w_block, w_index, out_cols, hook=None):
    m = a.shape[0]
    nk = grid[2]

    def body(a_ref, w_ref, o_ref, acc_ref):
        part = lax.dot_general(a_ref[...].astype(_MXU), w_ref[...].astype(_MXU), _NT, preferred_element_type=F32)
        if nk == 1:
            o_ref[...] = part
        else:
            kidx = pl.program_id(2)

            @pl.when(kidx == 0)
            def _():
                acc_ref[...] = part

            @pl.when(kidx > 0)
            def _():
                acc_ref[...] += part

            @pl.when(kidx == nk - 1)
            def _():
                o_ref[...] = acc_ref[...]

    return _pcall(
        body, name=name, grid=grid, in_specs=[pl.BlockSpec((tm, tk), a_index), pl.BlockSpec(w_block, w_index)],
        out_specs=[pl.BlockSpec((tm, tn), lambda i, j, q: (i, j))], out_shape=[_sds((m, out_cols), F32)],
        operands=[a, w], scratch=[pltpu.VMEM((tm, tn) if nk > 1 else (8, 128), F32)], vmem=60,
        sem=("parallel", "parallel", "arbitrary"), hook=hook)


def _mm_tn(at, b, *, name, grid, tm, tn, a_index, b_index, out_shape, out_block, out_index, carried=None,
           prefetch=(), hook=None, out_dtype=F32, colsum=False):
    t = at.shape[1]
    n_pre = len(prefetch)

    def body(*refs):
        a_ref, b_ref = refs[n_pre], refs[n_pre + 1]
        bv = b_ref[...]
        o_ref = refs[-2] if colsum else refs[-1]
        o_ref[...] = jnp.dot(a_ref[...].astype(_MXU), bv.astype(_MXU), preferred_element_type=F32).astype(out_dtype)
        if colsum:
            refs[-1][...] = jnp.sum(bv.astype(F32), axis=0, keepdims=True)

    operands = [at, b] + ([carried] if carried is not None else [])
    out_specs = [pl.BlockSpec(out_block, out_index)]
    out_shapes = [_sds(out_shape, out_dtype)]
    if colsum:
        out_specs.append(pl.BlockSpec((1, tn), b_index))
        out_shapes.append(_sds((1, b.shape[1]), F32))
    return _pcall(
        body, name=name, grid=grid,
        in_specs=[pl.BlockSpec((tm, t), a_index), pl.BlockSpec((t, tn), b_index)] + ([_ANY] if carried is not None else []),
        out_specs=out_specs, out_shape=out_shapes, operands=operands,
        vmem=56, sem=("parallel", "parallel"), aliases={2: 0} if carried is not None else None, prefetch=prefetch, hook=hook)


def _remote_tile(n):
    near = 2 * PER_IN
    if isinstance(n, int):
        return (n % 2, n // 2) if n < near else (2, n - near)
    return jnp.where(n < near, n % 2, 2), jnp.where(n < near, n // 2, n - near)


def _mm_z_gather(h16, w_own, b_in, order):
    n_tiles = N_SH * PER_IN
    n_remote = 3 * PER_IN
    half = D // 2

    def body(order_ref, a_ref, b_ref, w_in_ref, z_ref, w_ref, wbuf, tile_sems, ss, rs):
        del w_in_ref
        j = pl.program_id(0)
        x, y, c = _my_pos()
        me = 2 * x + y
        chips = _other_chips(x, y)
        mine_rows = pl.ds(pl.multiple_of(c * half, 16), half)
        sib_rows = pl.ds(pl.multiple_of((1 - c) * half, 16), half)

        slots = [2 * px + py for px, py in chips]
        cols = lambda t: pl.ds(t * TN_IN, TN_IN)
        q_rows = lambda r: pl.ds(pl.multiple_of(c * half + r * (half // 2), 16), half // 2)

        def direct(rel, t, slot):
            px, py = chips[rel]
            part = w_ref.at[slot, mine_rows, cols(t)]
            return _remote(part, part, ss, rs, 2 * t + rel, (px, py, c))

        def relay(r, t, slot):
            px, py = chips[1 - r]
            part = w_ref.at[slot, q_rows(r), cols(t)]
            return _remote(part, part, ss, rs, 2 * PER_IN + 2 * t + r, (px, py, c))

        def d2d(n, rows):
            rel, t = _remote_tile(n)
            part = w_ref.at[slots[rel], rows, cols(t)]
            return _remote(part, part, ss, rs, 4 * PER_IN + n, (x, y, 1 - c))

        def tile_copy(step):
            rel, t = _remote_tile(jnp.maximum(step - PER_IN, 0))
            slot = jnp.where(step < PER_IN, me, me ^ jnp.where(rel == 0, 2, jnp.where(rel == 1, 1, 3)))
            col = pl.multiple_of(jnp.where(step < PER_IN, step, t) * TN_IN, 128)
            return pltpu.make_async_copy(w_ref.at[slot, :, pl.ds(col, TN_IN)], wbuf.at[step % 2], tile_sems.at[step % 2])

        @pl.when(j == 0)
        def _():
            for t in range(PER_IN):
                direct(0, t, me).start()
                direct(1, t, me).start()
            tile_copy(0).start()

        for n in range(n_remote):
            rel, t = _remote_tile(n)

            @pl.when(j == n + PER_IN - 3)
            def _():
                if rel < 2:
                    direct(rel, t, slots[rel]).wait_recv()
                    relay(rel, t, slots[rel]).start()
                else:
                    relay(0, t, slots[2]).wait_recv()
                    relay(1, t, slots[2]).wait_recv()
                d2d(n, mine_rows).start()

            @pl.when(j == n + PER_IN - 2)
            def _():
                d2d(n, sib_rows).wait_recv()

        @pl.when(j + 1 < n_tiles)
        def _():
            tile_copy(j + 1).start()

        tile_copy(j).wait()
        z_ref[...] = jnp.dot(a_ref[...], wbuf[j % 2], preferred_element_type=F32) + b_ref[...]

        @pl.when(j == n_tiles - 1)
        def _():
            for t in range(PER_IN):
                for r in range(2):
                    direct(r, t, me).wait_send()
                    relay(r, t, slots[r]).wait_send()
            for n in range(n_remote):
                d2d(n, mine_rows).wait_send()

    def col_tile(j, o):
        rel, t = _remote_tile(jnp.maximum(j - PER_IN, 0))
        return 0, jnp.where(j < PER_IN, o[0] * PER_IN + j, o[1 + rel] * PER_IN + t)

    return pl.pallas_call(
        body, name="mm_z_gather",
        grid_spec=pltpu.PrefetchScalarGridSpec(
            num_scalar_prefetch=1, grid=(n_tiles,),
            in_specs=[pl.BlockSpec((TP, D), lambda j, o: (0, 0)), pl.BlockSpec((1, TN_IN), col_tile), _ANY],
            out_specs=[pl.BlockSpec((TP, TN_IN), col_tile), _ANY],
            scratch_shapes=[pltpu.VMEM((2, D, TN_IN), BF16), pltpu.SemaphoreType.DMA((2,)),
                            pltpu.SemaphoreType.DMA((4 * PER_IN + n_remote,)),
                            pltpu.SemaphoreType.DMA((4 * PER_IN + n_remote,))]),
        out_shape=[_sds((TP, D_IN), F32), _sds(w_own.shape, w_own.dtype)],
        input_output_aliases={3: 1},
        compiler_params=pltpu.CompilerParams(vmem_limit_bytes=60 * _MIB, dimension_semantics=("arbitrary",),
                                             has_side_effects=True),
    )(order, h16, b_in, w_own)


def _padded_rows(i, x_ref, meta_ref):
    head = jnp.concatenate([jnp.zeros((PAD, D), F32), meta_ref[...]], axis=0)
    return jnp.where(i == 0, head, x_ref[...])


def _stream_specs():
    return [pl.BlockSpec((BLK, D), lambda i: (jnp.maximum(i - 1, 0), 0)), pl.BlockSpec((N_META, D), lambda i: (0, 0))]


def _ln_emb_fwd(x, meta, g, b):
    def body(x_ref, meta_ref, g_ref, b_ref, h32_ref, h16_ref, h16t_ref):
        x = _padded_rows(pl.program_id(0), x_ref, meta_ref)
        mu = jnp.mean(x, axis=-1, keepdims=True)
        xc = x - mu
        var = jnp.mean(xc * xc, axis=-1, keepdims=True)
        y = xc * lax.rsqrt(var + LN_EPS) * g_ref[...] + b_ref[...]
        h32_ref[...] = y
        h16_ref[...] = y.astype(_MXU)
        h16t_ref[...] = y.T.astype(_MXU)

    row = pl.BlockSpec((BLK, D), lambda i: (i, 0))
    vec = pl.BlockSpec((1, D), lambda i: (0, 0))
    return _pcall(body, name="ln_emb_fwd", grid=(NBLK,), in_specs=_stream_specs() + [vec, vec],
                  out_specs=[row, row, pl.BlockSpec((D, BLK), lambda i: (0, i))],
                  out_shape=[_sds((TP, D), F32), _sds((TP, D), _MXU), _sds((D, TP), _MXU)], operands=[x, meta, g, b],
                  vmem=32, sem=("parallel",))


def _ln_emb_bwd(x, meta, g, dr, dhz, *, hook):
    def body(x_ref, meta_ref, g_ref, dr_ref, dhz_ref, dx_ref, dmeta_ref, acc_ref):
        i = pl.program_id(0)

        @pl.when(i == 0)
        def _():
            acc_ref[...] = jnp.zeros_like(acc_ref)

        x = _padded_rows(i, x_ref, meta_ref)
        mu = jnp.mean(x, axis=-1, keepdims=True)
        xc = x - mu
        var = jnp.mean(xc * xc, axis=-1, keepdims=True)
        rstd = lax.rsqrt(var + LN_EPS)
        xhat = xc * rstd
        dh = ALPHA * dr_ref[...] + dhz_ref[...]
        acc_ref[0:1, :] += jnp.sum(dh * xhat, axis=0, keepdims=True)
        acc_ref[1:2, :] += jnp.sum(dh, axis=0, keepdims=True)
        dxh = dh * g_ref[...]
        m1 = jnp.mean(dxh, axis=-1, keepdims=True)
        m2 = jnp.mean(dxh * xhat, axis=-1, keepdims=True)
        dx = rstd * (dxh - m1 - xhat * m2)
        dx_ref[...] = dx

        @pl.when(i == 0)
        def _():
            dmeta_ref[...] = dx[PAD:BLK]

    row = pl.BlockSpec((BLK, D), lambda i: (i, 0))
    vec = pl.BlockSpec((1, D), lambda i: (0, 0))
    xs, ms = _stream_specs()
    return _pcall(body, name="ln_emb_bwd", grid=(NBLK,), in_specs=[xs, ms, vec, row, row],
                  out_specs=[xs, ms, pl.BlockSpec((8, D), lambda i: (0, 0))],
                  out_shape=[_sds((SEQ, D), F32), _sds((N_META, D), F32), _sds((8, D), F32)],
                  operands=[x, meta, g, dr, dhz], vmem=32, sem=("arbitrary",), hook=hook)


def _mul_silu_fwd(a, z, off, *, name):
    def body(a_ref, z_ref, o_ref, t_ref):
        zz = z_ref[...]
        y = a_ref[...] * (zz * _sigmoid(zz))
        o_ref[...] = y.astype(_MXU)
        t_ref[...] = y.T.astype(_MXU)

    strip = pl.BlockSpec((TP, CW), lambda j: (0, j))
    return _pcall(body, name=name, grid=(D // CW,),
                  in_specs=[strip, pl.BlockSpec((TP, CW), lambda j: (0, off // CW + j))],
                  out_specs=[strip, pl.BlockSpec((CW, TP), lambda j: (j, 0))],
                  out_shape=[_sds((TP, D), _MXU), _sds((D, TP), _MXU)], operands=[a, z], vmem=56, sem=("parallel",))


def _mul_silu_bwd(dy, a, z, off, dz, *, name, hook=None):
    def body(dy_ref, a_ref, z_ref, dz_in, da_ref, dg_ref):
        zz = z_ref[...]
        sg = _sigmoid(zz)
        d = dy_ref[...]
        da_ref[...] = d * (zz * sg)
        dg_ref[...] = (d * a_ref[...] * (sg * (1.0 + zz * (1.0 - sg)))).astype(_MXU)

    blk = pl.BlockSpec((RT, CW), lambda i, j: (i, j))
    zblk = pl.BlockSpec((RT, CW), lambda i, j: (i, off // CW + j))
    return _pcall(body, name=name, grid=(TP // RT, D // CW), in_specs=[blk, blk, zblk, _ANY], out_specs=[blk, zblk],
                  out_shape=[_sds((TP, D), F32), _sds((TP, D_IN), _MXU)], operands=[dy, a, z, dz], vmem=32,
                  sem=("parallel", "parallel"), aliases={3: 1}, hook=hook)


def _merge_fwd(y2, z):
    w = 256

    def body(ya_ref, yb_ref, ga_ref, gb_ref, o_ref, t_ref):
        y = _sigmoid(ga_ref[...]) * ya_ref[...] + _sigmoid(gb_ref[...]) * yb_ref[...]
        o_ref[...] = y.astype(_MXU)
        t_ref[...] = y.T.astype(_MXU)

    nb = D // w
    strip = pl.BlockSpec((TP, w), lambda j: (0, j))
    return _pcall(body, name="merge_fwd", grid=(nb,),
                  in_specs=[strip, pl.BlockSpec((TP, w), lambda j: (0, nb + j)),
                            pl.BlockSpec((TP, w), lambda j: (0, OFF_G // w + j)),
                            pl.BlockSpec((TP, w), lambda j: (0, (OFF_G + D) // w + j))],
                  out_specs=[strip, pl.BlockSpec((w, TP), lambda j: (j, 0))],
                  out_shape=[_sds((TP, D), _MXU), _sds((D, TP), _MXU)], operands=[y2, y2, z, z], vmem=56,
                  sem=("parallel",))


def _merge_bwd(dmix, y2, z):
    nb = D // CW

    def body(dm_ref, y_ref, g_ref, dy_ref, dg_ref):
        dm = dm_ref[...]
        sg = _sigmoid(g_ref[...])
        dy_ref[...] = (dm * sg).astype(_MXU)
        dg_ref[...] = (dm * y_ref[...] * sg * (1.0 - sg)).astype(_MXU)

    blk = pl.BlockSpec((RT, CW), lambda i, j: (i, j))
    gblk = pl.BlockSpec((RT, CW), lambda i, j: (i, OFF_G // CW + j))
    return _pcall(body, name="merge_bwd", grid=(TP // RT, 2 * nb),
                  in_specs=[pl.BlockSpec((RT, CW), lambda i, j: (i, j % nb)), blk, gblk], out_specs=[blk, gblk],
                  out_shape=[_sds((TP, 2 * D), _MXU), _sds((TP, D_IN), _MXU)], operands=[dmix, y2, z], vmem=32,
                  sem=("parallel", "parallel"))


def _ln_out_loss(h32, out, target, g, b):
    def body(h_ref, o_ref, t_ref, g_ref, b_ref, dr_ref, dr16_ref, acc_ref):
        i = pl.program_id(0)

        @pl.when(i == 0)
        def _():
            acc_ref[...] = jnp.zeros_like(acc_ref)

        r = ALPHA * h_ref[...] + o_ref[...]
        mu = jnp.mean(r, axis=-1, keepdims=True)
        rc = r - mu
        var = jnp.mean(rc * rc, axis=-1, keepdims=True)
        rstd = lax.rsqrt(var + LN_EPS)
        xhat = rc * rstd
        gg = g_ref[...]
        y = xhat * gg + b_ref[...]
        real = (i >= 1).astype(F32)
        diff = (y - t_ref[...]) * real
        dy = diff * (1.0 / D)
        dxh = dy * gg
        m1 = jnp.mean(dxh, axis=-1, keepdims=True)
        m2 = jnp.mean(dxh * xhat, axis=-1, keepdims=True)
        dr = rstd * (dxh - m1 - xhat * m2)
        dr_ref[...] = dr
        dr16_ref[...] = dr.astype(_MXU)
        acc_ref[0:1, :] += jnp.sum(dy * xhat, axis=0, keepdims=True)
        acc_ref[1:2, :] += jnp.sum(dy, axis=0, keepdims=True)
        acc_ref[2:3, :] += jnp.sum(dr, axis=0, keepdims=True)
        acc_ref[3:4, :] += (0.5 / D) * jnp.sum(diff * diff)

    row = pl.BlockSpec((BLK, D), lambda i: (i, 0))
    vec = pl.BlockSpec((1, D), lambda i: (0, 0))
    return _pcall(body, name="ln_out_loss", grid=(NBLK,),
                  in_specs=[row, row, pl.BlockSpec((BLK, D), lambda i: (jnp.maximum(i - 1, 0), 0)), vec, vec],
                  out_specs=[row, row, pl.BlockSpec((8, D), lambda i: (0, 0))],
                  out_shape=[_sds((TP, D), F32), _sds((TP, D), _MXU), _sds((8, D), F32)],
                  operands=[h32, out, target, g, b], vmem=32, sem=("arbitrary",))


def _rnn_recompute(xr_ref, cw_ref, cb_ref, wra_ref, wri_ref, bra_ref, bri_ref, lam_ref):
    rows = lax.broadcasted_iota(jnp.int32, (TP, 1), 0)
    valid = (rows >= PAD).astype(F32)
    first = rows == PAD
    x = xr_ref[...] * valid
    cw = cw_ref[...]
    shifted = [x, pltpu.roll(x, 1, 0), pltpu.roll(x, 2, 0), pltpu.roll(x, 3, 0)]
    c = cb_ref[...] + cw[0:1, :] * shifted[0] + cw[1:2, :] * shifted[1] + cw[2:3, :] * shifted[2] + cw[3:4, :] * shifted[3]
    cm = c.astype(_MXU)
    gr = _sigmoid(jnp.dot(cm, wra_ref[...].astype(_MXU), preferred_element_type=F32) + bra_ref[...])
    gi = _sigmoid(jnp.dot(cm, wri_ref[...].astype(_MXU), preferred_element_type=F32) + bri_ref[...])
    lam = lam_ref[...]
    ls = jnp.minimum(lam, 0.0) - jnp.log(1.0 + jnp.exp(-jnp.abs(lam)))
    log_a = LRU_C * gr * ls
    a = jnp.exp(log_a)
    mult = jnp.where(first, 1.0, jnp.sqrt(1.0 - jnp.exp(2.0 * log_a)))
    return dict(valid=valid, first=first, shifted=shifted, c=c, cm=cm, gr=gr, gi=gi, ls=ls, a=a, mult=mult, lam=lam)


def _rnn_specs():
    col = pl.BlockSpec((TP, RB), lambda n: (0, n))
    vec = pl.BlockSpec((1, RB), lambda n: (0, n))
    return dict(col=col, vec=vec, cw=pl.BlockSpec((CONV_W, RB), lambda n: (0, n)),
                wblk=pl.BlockSpec((None, RB, RB), lambda n: (n, 0, 0)))


def _rnn_gates_fwd(z, conv_w, conv_b, w_ra, w_ri, b_ra, b_ri, lam):
    def body(xr_ref, cw_ref, cb_ref, wra_ref, wri_ref, bra_ref, bri_ref, lam_ref, a_ref, u_ref):
        r = _rnn_recompute(xr_ref, cw_ref, cb_ref, wra_ref, wri_ref, bra_ref, bri_ref, lam_ref)
        a_ref[...] = r["a"]
        u_ref[...] = r["mult"] * r["gi"] * r["c"] * r["valid"]

    s = _rnn_specs()
    return _pcall(body, name="rnn_gates_fwd", grid=(N_RB,),
                  in_specs=[s["col"], s["cw"], s["vec"], s["wblk"], s["wblk"], s["vec"], s["vec"], s["vec"]],
                  out_specs=[s["col"], s["col"]], out_shape=[_sds((TP, D), F32)] * 2,
                  operands=[z, conv_w, conv_b, w_ra, w_ri, b_ra, b_ri, lam], vmem=56, sem=("parallel",))


SCAN_ROWS = 272


SUB = 8


def _tile_scan(a, u, reverse):
    rows = lax.broadcasted_iota(jnp.int32, a.shape, 0)
    for d in (1, 2, 4):
        shift = SUB - d if reverse else d
        inside = (rows < SUB - d) if reverse else (rows >= d)
        u = u + a * jnp.where(inside, pltpu.roll(u, shift, 0), 0.0)
        a = a * jnp.where(inside, pltpu.roll(a, shift, 0), 1.0)
    return a, u


def _scan_fwd(a, u, *, hook):
    def body(a_ref, u_ref, h_ref, carry_ref):
        @pl.when(pl.program_id(1) == 0)
        def _():
            carry_ref[...] = jnp.zeros_like(carry_ref)

        def step(r, h):
            rows = pl.ds(pl.multiple_of(r * SUB, SUB), SUB)
            prod, part = _tile_scan(a_ref[rows, :], u_ref[rows, :], False)
            ht = part + prod * h
            h_ref[rows, :] = ht
            return ht[SUB - 1:SUB, :]

        carry_ref[...] = lax.fori_loop(0, SCAN_ROWS // SUB, step, carry_ref[...], unroll=2)

    blk = pl.BlockSpec((SCAN_ROWS, CW), lambda j, i: (i, j))
    return _pcall(body, name="scan_fwd", grid=(D // CW, TP // SCAN_ROWS), in_specs=[blk, blk], out_specs=[blk],
                  out_shape=[_sds((TP, D), F32)], operands=[a, u], scratch=[pltpu.VMEM((1, CW), F32)], vmem=32,
                  sem=("parallel", "arbitrary"), hook=hook)


def _scan_bwd(a, dh):
    nst = TP // SCAN_ROWS
    n_tiles = SCAN_ROWS // SUB

    def body(a_ref, d_ref, o_ref, lam_ref, anext_ref):
        @pl.when(pl.program_id(1) == 0)
        def _():
            lam_ref[...] = jnp.zeros_like(lam_ref)
            anext_ref[...] = jnp.zeros_like(anext_ref)

        def step(q, carry):
            lam_next, a_next = carry
            rows = pl.ds(pl.multiple_of((n_tiles - 1 - q) * SUB, SUB), SUB)
            at = a_ref[rows, :]
            last = lax.broadcasted_iota(jnp.int32, at.shape, 0) == SUB - 1
            b = jnp.where(last, a_next, pltpu.roll(at, SUB - 1, 0))
            prod, part = _tile_scan(b, d_ref[rows, :], True)
            lam = part + prod * lam_next
            o_ref[rows, :] = lam
            return lam[0:1, :], at[0:1, :]

        lam, an = lax.fori_loop(0, n_tiles, step, (lam_ref[...], anext_ref[...]), unroll=2)
        lam_ref[...] = lam
        anext_ref[...] = an

    blk = pl.BlockSpec((SCAN_ROWS, CW), lambda j, i: (nst - 1 - i, j))
    return _pcall(body, name="scan_bwd", grid=(D // CW, nst), in_specs=[blk, blk], out_specs=[blk],
                  out_shape=[_sds((TP, D), F32)], operands=[a, dh],
                  scratch=[pltpu.VMEM((1, CW), F32), pltpu.VMEM((1, CW), F32)], vmem=32,
                  sem=("parallel", "arbitrary"))[0]


def _rnn_gates_bwd(z, lam_s, hr, conv_w, conv_b, w_ra, w_ri, b_ra, b_ri, lam, dz, gsq):
    def body(xr_ref, ls_ref, hr_ref, cw_ref, cb_ref, wra_ref, wri_ref, bra_ref, bri_ref, lam_ref, dz_in, gsq_in,
             dx_ref, dw_ref, sums_ref):
        r = _rnn_recompute(xr_ref, cw_ref, cb_ref, wra_ref, wri_ref, bra_ref, bri_ref, lam_ref)
        valid, c, gr, gi, a, mult = r["valid"], r["c"], r["gr"], r["gi"], r["a"], r["mult"]
        du = ls_ref[...] * valid
        da = du * pltpu.roll(hr_ref[...], 1, 0)
        d_gi = du * mult * c
        dc = du * mult * gi
        dmult = du * gi * c
        dlog_a = da * a + jnp.where(r["first"], 0.0, -dmult * a * a / mult)
        d_gr = dlog_a * (LRU_C * r["ls"])
        dls = jnp.sum(dlog_a * (LRU_C * gr), axis=0, keepdims=True)
        dpre_r = d_gr * gr * (1.0 - gr)
        dpre_i = d_gi * gi * (1.0 - gi)
        pr = dpre_r.astype(_MXU)
        pi = dpre_i.astype(_MXU)
        dwra = lax.dot_general(r["cm"], pr, _TN, preferred_element_type=F32)
        dwri = lax.dot_general(r["cm"], pi, _TN, preferred_element_type=F32)
        for s in range(N_SH):
            dw_ref[s, 0:64, :] = dwra[64 * s:64 * (s + 1)]
            dw_ref[s, 64:128, :] = dwri[64 * s:64 * (s + 1)]
        dc = dc + lax.dot_general(pr, wra_ref[...].astype(_MXU), _NT, preferred_element_type=F32)
        dc = dc + lax.dot_general(pi, wri_ref[...].astype(_MXU), _NT, preferred_element_type=F32)
        cw = cw_ref[...]
        dx = cw[0:1, :] * dc
        for k in range(1, CONV_W):
            dx = dx + cw[k:k + 1, :] * pltpu.roll(dc, TP - k, 0)
        dx_ref[...] = (dx * valid).astype(_MXU)
        for k in range(CONV_W):
            sums_ref[k:k + 1, :] = jnp.sum(dc * r["shifted"][k], axis=0, keepdims=True)
        sums_ref[4:5, :] = jnp.sum(dc, axis=0, keepdims=True)
        sums_ref[5:6, :] = jnp.sum(dpre_r, axis=0, keepdims=True)
        sums_ref[6:7, :] = jnp.sum(dpre_i, axis=0, keepdims=True)
        sums_ref[7:8, :] = dls * _sigmoid(-r["lam"])

    s = _rnn_specs()
    return _pcall(
        body, name="rnn_gates_bwd", grid=(N_RB,),
        in_specs=[s["col"], s["col"], s["col"], s["cw"], s["vec"], s["wblk"], s["wblk"], s["vec"], s["vec"], s["vec"],
                  _ANY, _ANY],
        out_specs=[s["col"], pl.BlockSpec((N_SH, 128, RB), lambda n: (0, 3 * SQ_ROWS // 128, n)),
                   pl.BlockSpec((8, RB), lambda n: (0, n))],
        out_shape=[_sds((TP, D_IN), _MXU), _sds((N_SH, PACK_ROWS, D), F32), _sds((8, D), F32)],
        operands=[z, lam_s, hr, conv_w, conv_b, w_ra, w_ri, b_ra, b_ri, lam, dz, gsq], vmem=60, sem=("parallel",),
        aliases={10: 0, 11: 1})


def _rope_tables():
    half = HD // 2
    inv = ROPE_THETA ** (-jnp.arange(half, dtype=F32) / half)
    pos = (jnp.arange(TP) - PAD).astype(F32)
    ang = pos[:, None] * inv[None, :]
    return jnp.tile(jnp.cos(ang), (1, 4)), jnp.tile(jnp.sin(ang), (1, 4))


def _rope(x, cos_t, sin_t, sign):
    w = x.shape[1]
    lane = lax.broadcasted_iota(jnp.int32, x.shape, 1)
    first = (lane % HD) < (HD // 2)
    swapped = jnp.where(first, pltpu.roll(x, w - HD // 2, 1), pltpu.roll(x, HD // 2, 1))
    ct = jnp.tile(cos_t, (1, w // 128))
    st = jnp.tile(sin_t, (1, w // 128))
    return x * ct + swapped * jnp.where(first, -sign * st, sign * st)


def _rope_fwd(z, cos_t, sin_t):
    def body(q_ref, k_ref, v_ref, c_ref, s_ref, qo_ref, ko_ref, vo_ref):
        c = c_ref[...]
        s = s_ref[...]
        qo_ref[...] = _rope(q_ref[...], c, s, 1.0).astype(_MXU)
        ko_ref[...] = _rope(k_ref[...], c, s, 1.0).astype(_MXU)
        vo_ref[...] = v_ref[...].astype(_MXU)

    tab = pl.BlockSpec((BLK, 128), lambda i: (i, 0))
    kv = pl.BlockSpec((BLK, D_KV), lambda i: (i, 0))
    return _pcall(body, name="rope_fwd", grid=(NBLK,),
                  in_specs=[pl.BlockSpec((BLK, D), lambda i: (i, OFF_Q // D)),
                            pl.BlockSpec((BLK, D_KV), lambda i: (i, OFF_K // D_KV)),
                            pl.BlockSpec((BLK, D_KV), lambda i: (i, OFF_V // D_KV)), tab, tab],
                  out_specs=[pl.BlockSpec((BLK, D), lambda i: (i, 0)), kv, kv],
                  out_shape=[_sds((TP, D), _MXU), _sds((TP, D_KV), _MXU), _sds((TP, D_KV), _MXU)],
                  operands=[z, z, z, cos_t, sin_t], vmem=32, sem=("parallel",))


def _rope_bwd_k(dk, dv, cos_t, sin_t, dz):
    def body(dk_ref, dv_ref, c_ref, s_ref, dz_in, o_ref):
        o_ref[:, 0:D_KV] = _rope(dk_ref[...], c_ref[...], s_ref[...], -1.0).astype(_MXU)
        o_ref[:, D_KV:2 * D_KV] = dv_ref[...].astype(_MXU)

    tab = pl.BlockSpec((BLK, 128), lambda i: (i, 0))
    kv = pl.BlockSpec((BLK, D_KV), lambda i: (i, 0))
    return _pcall(body, name="rope_bwd_k", grid=(NBLK,), in_specs=[kv, kv, tab, tab, _ANY],
                  out_specs=[pl.BlockSpec((BLK, 2 * D_KV), lambda i: (i, OFF_K // (2 * D_KV)))],
                  out_shape=[_sds((TP, D_IN), _MXU)], operands=[dk, dv, cos_t, sin_t, dz], vmem=32, sem=("parallel",),
                  aliases={4: 0})[0]


def _attn_mask(i):
    ql = lax.broadcasted_iota(jnp.int32, (BLK, 3 * BLK), 0)
    kk = lax.broadcasted_iota(jnp.int32, (BLK, 3 * BLK), 1)
    kl = kk % BLK
    part = kk // BLK
    meta = (part == 0) & (kl >= PAD) & ((i >= 1) | (kl <= ql))
    prev = (part == 1) & (i >= 2) & (kl > ql)
    cur = (part == 2) & (i >= 1) & (kl <= ql)
    return meta | prev | cur


def _cat_kv(refs, g):
    return jnp.concatenate([r[:, HD * g:HD * (g + 1)] for r in refs], axis=0)


def _kv_specs():
    return [pl.BlockSpec((BLK, D_KV), lambda i: (0, 0)),
            pl.BlockSpec((BLK, D_KV), lambda i: (jnp.maximum(i - 1, 0), 0)),
            pl.BlockSpec((BLK, D_KV), lambda i: (i, 0))]


def _stack_heads(ref, g, width=HD):
    return jnp.concatenate([ref[:, width * (GRP * g + j):width * (GRP * g + j + 1)] for j in range(GRP)], axis=0)


def _attn_fwd(q, k, v, sinks, *, hook):
    def body(q_ref, k0_ref, kp_ref, kc_ref, v0_ref, vp_ref, vc_ref, sink_ref, o_ref, lse_ref, s_scr, p_scr):
        mask = _attn_mask(pl.program_id(0))
        for g in range(N_KV):
            s_scr[...] = lax.dot_general(_stack_heads(q_ref, g), _cat_kv((k0_ref, kp_ref, kc_ref), g), _NT,
                                         preferred_element_type=F32)
            for j in range(GRP):
                h = GRP * g + j
                rows = slice(BLK * j, BLK * (j + 1))
                sink = sink_ref[h]
                s = jnp.where(mask, s_scr[rows, :] * (HD ** -0.5), NEG_INF)
                mx = jnp.maximum(jnp.max(s, -1, keepdims=True), sink)
                p = jnp.exp(s - mx)
                den = jnp.sum(p, -1, keepdims=True) + jnp.exp(sink - mx)
                p_scr[rows, :] = (p * (1.0 / den)).astype(_MXU)
                lse_ref[:, h:h + 1] = mx + jnp.log(den)
            o8 = jnp.dot(p_scr[...], _cat_kv((v0_ref, vp_ref, vc_ref), g), preferred_element_type=F32)
            for j in range(GRP):
                h = GRP * g + j
                o_ref[:, HD * h:HD * (h + 1)] = o8[BLK * j:BLK * (j + 1)]

    return _pcall(body, name="attn_fwd", grid=(NBLK,),
                  in_specs=[pl.BlockSpec((BLK, D), lambda i: (i, 0))] + _kv_specs() + _kv_specs()
                           + [pl.BlockSpec(memory_space=pltpu.SMEM)],
                  out_specs=[pl.BlockSpec((BLK, D), lambda i: (i, 0)), pl.BlockSpec((BLK, N_Q), lambda i: (i, 0))],
                  out_shape=[_sds((TP, D), F32), _sds((TP, N_Q), F32)], operands=[q, k, k, k, v, v, v, sinks],
                  scratch=[pltpu.VMEM((GRP * BLK, 3 * BLK), F32), pltpu.VMEM((GRP * BLK, 3 * BLK), _MXU)],
                  vmem=40, sem=("parallel",), hook=hook)


def _attn_bwd(q, k, v, sinks, do, o, lse, cos_t, sin_t, dz, *, hook):
    def body(q_ref, k0_ref, kp_ref, kc_ref, v0_ref, vp_ref, vc_ref, sink_ref, do_ref, o_ref, lse_ref, c_ref, s_ref, dz_in,
             dq_ref, dk_ref, dv_ref, dsink_ref, dqrot_ref, s_scr, dp_scr, p_scr, ds_scr):
        i = pl.program_id(0)

        @pl.when(i == 0)
        def _():
            dk_ref[...] = jnp.zeros_like(dk_ref)
            dv_ref[...] = jnp.zeros_like(dv_ref)
            dsink_ref[...] = jnp.zeros_like(dsink_ref)

        mask = _attn_mask(i)
        row_starts = (0, pl.multiple_of(jnp.maximum(i - 1, 0) * BLK, BLK), pl.multiple_of(i * BLK, BLK))
        scale = HD ** -0.5
        for g in range(N_KV):
            gs = slice(HD * g, HD * (g + 1))
            q8 = _stack_heads(q_ref, g)
            dom = _stack_heads(do_ref, g).astype(_MXU)
            kcat = _cat_kv((k0_ref, kp_ref, kc_ref), g)
            s_scr[...] = lax.dot_general(q8, kcat, _NT, preferred_element_type=F32)
            dp_scr[...] = lax.dot_general(dom, _cat_kv((v0_ref, vp_ref, vc_ref), g), _NT, preferred_element_type=F32)
            for j in range(GRP):
                h = GRP * g + j
                hs = slice(HD * h, HD * (h + 1))
                rows = slice(BLK * j, BLK * (j + 1))
                lse_h = lse_ref[:, h:h + 1]
                delta = jnp.sum(do_ref[:, hs] * o_ref[:, hs], axis=-1, keepdims=True)
                dsink_ref[0:1, h:h + 1] += jnp.sum(-jnp.exp(sink_ref[h] - lse_h) * delta, axis=0, keepdims=True)
                p = jnp.exp(jnp.where(mask, s_scr[rows, :] * scale, NEG_INF) - lse_h)
                p_scr[rows, :] = p.astype(_MXU)
                ds_scr[rows, :] = (p * (dp_scr[rows, :] - delta) * scale).astype(_MXU)
            ds = ds_scr[...]
            dq8 = jnp.dot(ds, kcat, preferred_element_type=F32)
            dkcat = lax.dot_general(ds, q8, _TN, preferred_element_type=F32)
            dvcat = lax.dot_general(p_scr[...], dom, _TN, preferred_element_type=F32)
            for part in range(3):
                rows = pl.ds(row_starts[part], BLK)
                dk_ref[rows, gs] += dkcat[BLK * part:BLK * (part + 1)]
                dv_ref[rows, gs] += dvcat[BLK * part:BLK * (part + 1)]
            for j in range(GRP):
                h = GRP * g + j
                dqrot_ref[:, HD * h:HD * (h + 1)] = dq8[BLK * j:BLK * (j + 1)]
        dq_ref[...] = _rope(dqrot_ref[...], c_ref[...], s_ref[...], -1.0).astype(_MXU)

    row = pl.BlockSpec((BLK, D), lambda i: (i, 0))
    tab = pl.BlockSpec((BLK, 128), lambda i: (i, 0))
    full_kv = pl.BlockSpec((TP, D_KV), lambda i: (0, 0))
    return _pcall(
        body, name="attn_bwd", grid=(NBLK,),
        in_specs=[row] + _kv_specs() + _kv_specs() + [pl.BlockSpec(memory_space=pltpu.SMEM), row, row,
                  pl.BlockSpec((BLK, N_Q), lambda i: (i, 0)), tab, tab, _ANY],
        out_specs=[pl.BlockSpec((BLK, D), lambda i: (i, OFF_Q // D)), full_kv, full_kv,
                   pl.BlockSpec((8, 128), lambda i: (0, 0))],
        out_shape=[_sds((TP, D_IN), _MXU), _sds((TP, D_KV), F32), _sds((TP, D_KV), F32), _sds((8, 128), F32)],
        operands=[q, k, k, k, v, v, v, sinks, do, o, lse, cos_t, sin_t, dz],
        scratch=[pltpu.VMEM((BLK, D), F32), pltpu.VMEM((GRP * BLK, 3 * BLK), F32), pltpu.VMEM((GRP * BLK, 3 * BLK), F32),
                 pltpu.VMEM((GRP * BLK, 3 * BLK), _MXU), pltpu.VMEM((GRP * BLK, 3 * BLK), _MXU)],
        vmem=48, sem=("arbitrary",), aliases={13: 0}, hook=hook)


def _cast_into_slot(w32, pos, *, name, tr):
    r, cc = w32.shape

    def body(pos_ref, w_ref, o_ref):
        o_ref[...] = w_ref[...].astype(BF16)

    return _pcall(body, name=name, grid=(r // tr,), in_specs=[pl.BlockSpec((tr, cc), lambda i, p: (i, 0))],
                  out_specs=[pl.BlockSpec((None, tr, cc), lambda i, p: (p[1], i, 0))],
                  out_shape=[_sds((N_SH, r, cc), BF16)], operands=[w32], vmem=32, sem=("parallel",), prefetch=(pos,))[0]


def _pair_add(g, got, pos, *, name, tr, g_has_both_halves, hook=None):
    n, h, cc = got.shape
    nt = h // tr

    def body(pos_ref, g_ref, r_ref, own_ref, s16_ref):
        s = g_ref[...] + r_ref[...].astype(F32)
        s16_ref[...] = s.astype(BF16)

        @pl.when(pl.program_id(1) == pos_ref[1])
        def _():
            own_ref[...] = s

    g_index = (lambda i, s, p: (s, p[0] * nt + i, 0)) if g_has_both_halves else (lambda i, s, p: (s, i, 0))
    return _pcall(body, name=name, grid=(nt, n),
                  in_specs=[pl.BlockSpec((None, tr, cc), g_index), pl.BlockSpec((None, tr, cc), lambda i, s, p: (s, i, 0))],
                  out_specs=[pl.BlockSpec((tr, cc), lambda i, s, p: (i, 0)),
                             pl.BlockSpec((None, tr, cc), lambda i, s, p: (s, i, 0))],
                  out_shape=[_sds((h, cc), F32), _sds((n, h, cc), BF16)], operands=[g, got], vmem=40,
                  sem=("parallel", "arbitrary"), prefetch=(pos,), hook=hook)


def _relay_add(s16, raw, pos, *, name, tr):
    _, q, cc = raw.shape
    nt = q // tr

    def body(pos_ref, s_ref, r_ref, o_ref):
        o_ref[...] = (s_ref[...].astype(F32) + r_ref[...].astype(F32)).astype(BF16)

    blk = pl.BlockSpec((None, tr, cc), lambda k, i, p: (k, i, 0))
    return _pcall(body, name=name, grid=(2, nt),
                  in_specs=[pl.BlockSpec((None, tr, cc), lambda k, i, p: (p[3 + k], (1 - k) * nt + i, 0)), blk],
                  out_specs=[blk], out_shape=[_sds((2, q, cc), BF16)], operands=[s16, raw], vmem=40,
                  sem=("parallel", "parallel"), prefetch=(pos,))[0]


def _sum_chips(own, got, pos, *, name, tr):
    h, cc = own.shape
    n_got = got.shape[0]
    nt = h // tr

    def body(pos_ref, o_ref, r_ref, out_ref):
        acc = o_ref[...]
        for k in range(n_got):
            acc = acc + r_ref[k].astype(F32)
        out_ref[...] = acc

    return _pcall(body, name=name, grid=(nt,),
                  in_specs=[pl.BlockSpec((tr, cc), lambda i, p: (i, 0)),
                            pl.BlockSpec((n_got, tr, cc), lambda i, p: (0, i, 0))],
                  out_specs=[pl.BlockSpec((tr, cc), lambda i, p: (p[0] * nt + i, 0))],
                  out_shape=[_sds((2 * h, cc), F32)], operands=[own, got], vmem=40, sem=("parallel",), prefetch=(pos,))[0]


def _adamw(w, g, m, v, *, name, tr, g_row0=0):
    r, cc = w.shape
    g_blk0 = g_row0 // tr

    def body(w_ref, g_ref, m_ref, v_ref, go_ref, d_ref, mo_ref, vo_ref):
        gg = g_ref[...]
        go_ref[...] = gg
        m_new = ADAM_B1 * m_ref[...] + (1.0 - ADAM_B1) * gg
        v_new = ADAM_B2 * v_ref[...] + (1.0 - ADAM_B2) * (gg * gg)
        m_hat = m_new / (1.0 - ADAM_B1 ** ADAM_STEP)
        v_hat = v_new / (1.0 - ADAM_B2 ** ADAM_STEP)
        d_ref[...] = -ADAM_LR * (m_hat / (jnp.sqrt(v_hat) + ADAM_EPS) + ADAM_WD * w_ref[...])
        mo_ref[...] = m_new
        vo_ref[...] = v_new

    blk = pl.BlockSpec((tr, cc), lambda i: (i, 0))
    gblk = pl.BlockSpec((tr, cc), lambda i: (g_blk0 + i, 0))
    return _pcall(body, name=name, grid=(r // tr,), in_specs=[blk, gblk, blk, blk], out_specs=[blk] * 4,
                  out_shape=[_sds((r, cc), F32)] * 4, operands=[w, g, m, v], vmem=48, sem=("parallel",))


_SMALL_ROWS = 40
_B_IN_ROWS = 7


def _row_pad(v, rows):
    flat = v.reshape(-1)
    return jnp.pad(flat, (0, rows * D - flat.shape[0])).reshape(rows, D)


def _pack_ra(w):
    return w.transpose(1, 0, 2).reshape(64, D)


def _unpack_ra(p, like):
    return p.reshape(64, N_RB, RB).transpose(1, 0, 2).reshape(like.shape)


def _gate_full(g4):
    return g4.reshape(N_SH, 64, N_RB, RB).transpose(2, 0, 1, 3).reshape(N_RB, RB, RB)


def kernel(x, meta_tokens, ln_emb_g, ln_emb_b, w_in, b_in, conv_w, conv_b, w_ra, b_ra, w_ri, b_ri, lru_lambda, sinks, w_rnn_out, w_attn_out, w_o, b_o, ln_g, ln_b, loss_target, m_meta_tokens, m_ln_emb_g, m_ln_emb_b, m_w_in, m_b_in, m_conv_w, m_conv_b, m_w_ra, m_b_ra, m_w_ri, m_b_ri, m_lru_lambda, m_sinks, m_w_rnn_out, m_w_attn_out, m_w_o, m_b_o, m_ln_g, m_ln_b, v_meta_tokens, v_ln_emb_g, v_ln_emb_b, v_w_in, v_b_in, v_conv_w, v_conv_b, v_w_ra, v_b_ra, v_w_ri, v_b_ri, v_lru_lambda, v_sinks, v_w_rnn_out, v_w_attn_out, v_w_o, v_b_o, v_ln_g, v_ln_b):
    xi, yi, ci = _my_pos()
    shard = 2 * xi + yi
    pos = jnp.stack([ci, shard, 1 - ci, shard ^ 2, shard ^ 1]).astype(jnp.int32)
    cos_t, sin_t = _rope_tables()
    zero_bias = jnp.zeros((1, D), F32)
    ln_emb_g2, ln_emb_b2 = ln_emb_g[None], ln_emb_b[None]

    small = jnp.concatenate([conv_w[0], meta_tokens, jnp.zeros((4, 512), F32)], axis=0)
    small4 = _gather_small(small)
    conv_w_full = small4[:, 0:4].transpose(1, 0, 2).reshape(CONV_W, D)
    meta_full = small4[:, 4:20].transpose(1, 0, 2).reshape(N_META, D)
    w_own = _cast_into_slot(w_in[0], pos, name="cast_w_in", tr=256)
    wa_own = _cast_into_slot(jnp.concatenate([w_attn_out[0], w_o[0], _pack_ra(w_ra[0]), _pack_ra(w_ri[0])], axis=0),
                             pos, name="cast_w_a", tr=288)
    wb_own = _cast_into_slot(w_rnn_out[0], pos, name="cast_w_b", tr=256)

    h32, h16, h16_t = _ln_emb_fwd(x[0], meta_full, ln_emb_g2, ln_emb_b2)
    order = jnp.stack([shard, shard ^ 2, shard ^ 1, shard ^ 3]).astype(jnp.int32)
    z, w_in4 = _mm_z_gather(h16, w_own, b_in, order)
    q, k, v = _rope_fwd(z, cos_t, sin_t)
    o, lse, wa4 = _attn_fwd(q, k, v, sinks[0], hook=_hook_gather(wa_own, 0.6))
    w_ra_full = _gate_full(wa4[:, 2 * SQ_ROWS:2 * SQ_ROWS + 64])
    w_ri_full = _gate_full(wa4[:, 2 * SQ_ROWS + 64:2 * SQ_ROWS + 128])
    a_dec, u_in = _rnn_gates_fwd(z, conv_w_full, conv_b, w_ra_full, w_ri_full, b_ra, b_ri, lru_lambda)
    hr, wb4 = _scan_fwd(a_dec, u_in, hook=_hook_gather(wb_own, 0.6))
    sq_w = {0: (wb4, 0), 1: (wa4, 0), 2: (wa4, 1)}

    def sq_nn(a, kk, bias, name, out_cols, out_index, carried=None):
        wp, blk = sq_w[kk]
        return _mm_nn(a, wp, bias, name=name, grid=(2, D // CW), tm=HALF_TP, tn=CW, k=D, a_index=lambda i, j: (i, 0),
                      w_block=(N_SH, SQ_ROWS, CW), w_index=lambda i, j: (0, blk, j), out_cols=out_cols,
                      out_index=out_index, carried=carried)[0]

    def sq_nt(a, a_blk, kk, name, hook=None):
        wp, blk = sq_w[kk]
        return _mm_nt(a, wp, name=name, grid=(2, N_SH, 1), tm=HALF_TP, tn=SQ_ROWS, tk=D,
                      a_index=lambda i, j, q: (i, a_blk), w_block=(None, SQ_ROWS, D),
                      w_index=lambda i, j, q: (j, blk, 0), out_cols=D, hook=hook)

    ya_in, ya_in_t = _mul_silu_fwd(hr, z, OFF_GR, name="gate_a_fwd")
    y2 = sq_nn(ya_in, 0, zero_bias, "mm_ya", 2 * D, lambda i, j: (i, j))
    yb_in, yb_in_t = _mul_silu_fwd(o, z, OFF_GA, name="gate_b_fwd")
    y2 = sq_nn(yb_in, 1, zero_bias, "mm_yb", 2 * D, lambda i, j: (i, D // CW + j), carried=y2)
    mixed, mixed_t = _merge_fwd(y2, z)
    out = sq_nn(mixed, 2, b_o, "mm_out", D, lambda i, j: (i, j))
    dr, dr16, sums_o = _ln_out_loss(h32, out, loss_target[0], ln_g, ln_b)

    def sq_tn(at, b, b_blk, kk, name, carried=None):
        return _mm_tn(at, b, name=name, grid=(N_SH, 1), tm=SQ_ROWS, tn=D, a_index=lambda i, j: (i, 0),
                      b_index=lambda i, j: (0, b_blk), out_shape=(N_SH, PACK_ROWS, D),
                      out_block=(None, SQ_ROWS, D), out_index=lambda i, j: (i, kk, 0), carried=carried)[0]

    gsq = sq_tn(mixed_t, dr16, 0, 2, "mm_dwo")
    dmix = sq_nt(dr16, 0, 2, "mm_dmix")[0]
    dy2, dz = _merge_bwd(dmix, y2, z)
    gsq = sq_tn(ya_in_t, dy2, 0, 0, "mm_dwrnn", carried=gsq)
    gsq = sq_tn(yb_in_t, dy2, 1, 1, "mm_dwattn", carried=gsq)
    dya_in = sq_nt(dy2, 0, 0, "mm_dyain")[0]
    dhr, dz = _mul_silu_bwd(dya_in, hr, z, OFF_GR, dz, name="gate_a_bwd")
    lam_s = _scan_bwd(a_dec, dhr)
    dz, gsq, sums_r = _rnn_gates_bwd(z, lam_s, hr, conv_w_full, conv_b, w_ra_full, w_ri_full, b_ra, b_ri, lru_lambda, dz, gsq)
    dyb_in, gsq, got_sq = sq_nt(dy2, 1, 1, "mm_dybin", hook=_hook_pair(gsq, True, (0, 1)))
    do, dz, gsq, got_sq = _mul_silu_bwd(dyb_in, o, z, OFF_GA, dz, name="gate_b_bwd",
                                        hook=_hook_pair(gsq, True, (2, 3), land=got_sq))
    own_sq, s16_sq = _pair_add(gsq, got_sq, pos, name="red_w_sq_add", tr=208, g_has_both_halves=True)
    dz, dk_rot, dv32, dsink, s16_sq, fin_sq, raw_sq = _attn_bwd(q, k, v, sinks[0], do, o, lse, cos_t, sin_t, dz,
                                                                hook=_hook_scatter_direct(s16_sq))
    dz = _rope_bwd_k(dk_rot, dv32, cos_t, sin_t, dz)
    comb_sq = _relay_add(s16_sq, raw_sq, pos, name="red_w_sq_relay", tr=208)

    def dwin(half_idx, name, hook, **kw):
        return _mm_tn(h16_t, dz, name=name, grid=(1, N_SH * PER_IN), tm=D // 2, tn=TN_IN,
                      a_index=lambda i, j, p: (p[half_idx], 0), b_index=lambda i, j, p: (0, j),
                      out_shape=(N_SH, D // 2, W_IN_COLS), out_block=(None, D // 2, TN_IN),
                      out_index=lambda i, j, p: (j // PER_IN, 0, j % PER_IN), prefetch=(pos,), hook=hook, **kw)

    gin_sib, g_b_in, comb_sq, fin_sq = dwin(2, "mm_dwin_sib", _hook_scatter_relay(comb_sq, fin_sq),
                                            out_dtype=BF16, colsum=True)
    red_sq = _sum_chips(own_sq, fin_sq, pos, name="red_w_sq_sum", tr=208)
    gin_own, gin_sib, got_in = dwin(0, "mm_dwin_own", _hook_pair(gin_sib, False))
    own_in, s16_in, g_sq = _pair_add(gin_own, got_in, pos, name="red_w_in_add", tr=128, g_has_both_halves=False,
                                     hook=_hook_halves(red_sq))
    dhz, s16_in, fin_in, raw_in = _mm_nt(dz, w_in4, name="mm_dhz", grid=(2, 2, N_SH), tm=HALF_TP, tn=D // 2,
                                         tk=W_IN_COLS, a_index=lambda i, j, q: (i, q),
                                         w_block=(None, D // 2, W_IN_COLS), w_index=lambda i, j, q: (q, j, 0), out_cols=D,
                                         hook=_hook_scatter_direct(s16_in))
    comb_in = _relay_add(s16_in, raw_in, pos, name="red_w_in_relay", tr=128)
    g_x, g_meta_local, sums_e, comb_in, fin_in = _ln_emb_bwd(x[0], meta_full, ln_emb_g2, dr, dhz,
                                                             hook=_hook_scatter_relay(comb_in, fin_in))
    red_in = _sum_chips(own_in, fin_in, pos, name="red_w_in_sum", tr=128)

    spack = jnp.concatenate([
        sums_e[0:1], sums_e[1:2], _row_pad(g_b_in, _B_IN_ROWS), sums_r[0:4], sums_r[4:5], sums_r[5:6], sums_r[6:7],
        sums_r[7:8], _row_pad(dsink[0:1, 0:N_Q], 1), sums_o[2:3], sums_o[0:1], sums_o[1:2], g_meta_local, sums_o[3:4],
        jnp.zeros((_SMALL_ROWS - 38, D), F32)], axis=0)
    sred, g_in = _allreduce_small(spack, red_in)

    big = {"w_in": [t.reshape(w_in.shape) for t in
                    _adamw(w_in[0], g_in, m_w_in[0], v_w_in[0], name="adamw_w_in", tr=128)]}
    for kk, (n, w_, m_, v_) in enumerate([("w_rnn_out", w_rnn_out, m_w_rnn_out, v_w_rnn_out),
                                          ("w_attn_out", w_attn_out, m_w_attn_out, v_w_attn_out), ("w_o", w_o, m_w_o, v_w_o)]):
        big[n] = [t.reshape(w_.shape) for t in
                  _adamw(w_[0], g_sq, m_[0], v_[0], name="adamw_" + n, tr=256, g_row0=SQ_ROWS * kk)]
    for kk, (n, w_, m_, v_) in enumerate([("w_ra", w_ra, m_w_ra, v_w_ra), ("w_ri", w_ri, m_w_ri, v_w_ri)]):
        big[n] = [_unpack_ra(t, w_) for t in
                  _adamw(_pack_ra(w_[0]), g_sq, _pack_ra(m_[0]), _pack_ra(v_[0]), name="adamw_" + n, tr=64,
                         g_row0=3 * SQ_ROWS + 64 * kk)]

    loss = sred[37, 0]
    col0 = shard * 512
    g_conv_w = lax.dynamic_slice(sred[9:13], (0, col0), (CONV_W, 512))
    g_meta = lax.dynamic_slice(sred[21:37], (0, col0), (N_META, 512))
    small_g = {"ln_emb_g": sred[0:1], "ln_emb_b": sred[1:2], "b_in": sred[2:9], "conv_w": g_conv_w.reshape(1, D),
               "conv_b": sred[13:14], "b_ra": sred[14:15], "b_ri": sred[15:16], "lru_lambda": sred[16:17],
               "sinks": sred[17:18], "b_o": sred[18:19], "ln_g": sred[19:20], "ln_b": sred[20:21],
               "meta_tokens": g_meta.reshape(4, D)}
    small_names = list(small_g)

    def small_pack(vals):
        rows = []
        for n in small_names:
            a = vals[n]
            if n == "b_in":
                a = _row_pad(a, _B_IN_ROWS)
            elif n == "sinks":
                a = _row_pad(a, 1)
            else:
                a = a.reshape(-1, D)
            rows.append(a)
        return jnp.concatenate(rows + [jnp.zeros((24 - 22, D), F32)], axis=0)

    w_small = dict(ln_emb_g=ln_emb_g, ln_emb_b=ln_emb_b, b_in=b_in, conv_w=conv_w, conv_b=conv_b, b_ra=b_ra, b_ri=b_ri,
                   lru_lambda=lru_lambda, sinks=sinks, b_o=b_o, ln_g=ln_g, ln_b=ln_b, meta_tokens=meta_tokens)
    m_small = dict(ln_emb_g=m_ln_emb_g, ln_emb_b=m_ln_emb_b, b_in=m_b_in, conv_w=m_conv_w, conv_b=m_conv_b, b_ra=m_b_ra,
                   b_ri=m_b_ri, lru_lambda=m_lru_lambda, sinks=m_sinks, b_o=m_b_o, ln_g=m_ln_g, ln_b=m_ln_b,
                   meta_tokens=m_meta_tokens)
    v_small = dict(ln_emb_g=v_ln_emb_g, ln_emb_b=v_ln_emb_b, b_in=v_b_in, conv_w=v_conv_w, conv_b=v_conv_b, b_ra=v_b_ra,
                   b_ri=v_b_ri, lru_lambda=v_lru_lambda, sinks=v_sinks, b_o=v_b_o, ln_g=v_ln_g, ln_b=v_ln_b,
                   meta_tokens=v_meta_tokens)
    g_small_pack = jnp.concatenate([small_g[n] for n in small_names] + [jnp.zeros((2, D), F32)], axis=0)
    small_res = _adamw(small_pack(w_small), g_small_pack, small_pack(m_small), small_pack(v_small),
                       name="adamw_small", tr=24)

    small_rows = {}
    r0 = 0
    for n in small_names:
        nrows = small_g[n].shape[0]
        small_rows[n] = (r0, nrows)
        r0 += nrows

    def small_out(packed, n, like):
        a, nrows = small_rows[n]
        flat = packed[a:a + nrows].reshape(-1)
        return flat[:like.size].reshape(like.shape)

    weights = dict(meta_tokens=meta_tokens, ln_emb_g=ln_emb_g, ln_emb_b=ln_emb_b, w_in=w_in, b_in=b_in, conv_w=conv_w,
                   conv_b=conv_b, w_ra=w_ra, b_ra=b_ra, w_ri=w_ri, b_ri=b_ri, lru_lambda=lru_lambda, sinks=sinks,
                   w_rnn_out=w_rnn_out, w_attn_out=w_attn_out, w_o=w_o, b_o=b_o, ln_g=ln_g, ln_b=ln_b)

    def outputs(which):
        return [big[n][which] if n in big else small_out(small_res[which], n, like) for n, like in weights.items()]

    return (loss, g_x[None], *outputs(0), *outputs(1), *outputs(2), *outputs(3))
```

```python
import jax
import jax.numpy as jnp
from jax import lax
from jax.experimental import pallas as pl
from jax.experimental.pallas import tpu as pltpu

F32 = jnp.float32
BF16 = jnp.bfloat16
_MXU = jnp.bfloat16

D = 2048
SEQ = 2048
N_META = 16
BLK = 128
PAD = BLK - N_META
TP = PAD + N_META + SEQ
NBLK = TP // BLK
HALF_TP = TP // 2
N_RB = 8
RB = 256
CONV_W = 4
LRU_C = 8.0
HD = 64
N_Q = 32
N_KV = 4
GRP = 8
D_KV = 256
NEG_INF = -1e30
LN_EPS = 1e-5
ALPHA = 2.0 ** 0.25
ROPE_THETA = 10000.0
OFF_GR, OFF_Q, OFF_K, OFF_V, OFF_GA, OFF_G = 2048, 4096, 6144, 6400, 6656, 8704
D_IN = 12800
N_SH = 4
W_IN_COLS = D_IN // N_SH
TN_IN = 640
PER_IN = W_IN_COLS // TN_IN
CW = 512
RT = TP // 4
SQ_ROWS = 512
PACK_ROWS = 3 * SQ_ROWS + 128

ADAM_LR = 0.001
ADAM_B1 = 0.9
ADAM_B2 = 0.999
ADAM_EPS = 1e-08
ADAM_WD = 0.01
ADAM_STEP = 10

MESH = pl.DeviceIdType.MESH
_MIB = 1024 * 1024
_ANY = pl.BlockSpec(memory_space=pl.ANY)
_NT = (((1,), (1,)), ((), ()))
_TN = (((0,), (0,)), ((), ()))


def _sds(shape, dtype):
    return jax.ShapeDtypeStruct(shape, dtype)


def _sigmoid(x):
    return 1.0 / (1.0 + jnp.exp(-x))


def _my_pos():
    return lax.axis_index("x"), lax.axis_index("y"), lax.axis_index("c")


def _other_chips(x, y):
    return [(1 - x, y), (x, 1 - y), (1 - x, 1 - y)]


def _remote(src, dst, send_sems, recv_sems, k, dev):
    return pltpu.make_async_remote_copy(src_ref=src, dst_ref=dst, send_sem=send_sems.at[k], recv_sem=recv_sems.at[k],
                                        device_id=dev, device_id_type=MESH)


class _Hook:
    def __init__(self, carried, landing, n_sems, start, finish, mid=None, mid_frac=0.5):
        self.carried, self.landing, self.n_sems, self.start, self.finish = list(carried), list(landing), n_sems, start, finish
        self.mid, self.mid_frac = mid, mid_frac


def _hook_gather(buf, mid_frac):
    half = buf.shape[1] // 2
    quarter = half // 2

    def geom(o, ss, rs):
        x, y, c = _my_pos()
        xn, yn, dg = _other_chips(x, y)
        slot = lambda p: 2 * p[0] + p[1]
        mine_rows = pl.ds(pl.multiple_of(c * half, 16), half)
        sib_rows = pl.ds(pl.multiple_of((1 - c) * half, 16), half)
        q_rows = lambda r: pl.ds(pl.multiple_of(c * half + r * quarter, 16), quarter)

        def cp(k, s, rows, dev):
            part = o.at[s, rows]
            return _remote(part, part, ss, rs, k, dev)

        return dict(
            direct=lambda k, s: cp(k, s, mine_rows, ((xn, yn)[k][0], (xn, yn)[k][1], c)),
            relay=lambda r, s: cp(2 + r, s, q_rows(r), ((yn, xn)[r][0], (yn, xn)[r][1], c)),
            sibling=lambda k, s, mine: cp(4 + k, s, mine_rows if mine else sib_rows, (x, y, 1 - c)),
            me=slot((x, y)), slots=(slot(xn), slot(yn), slot(dg)))

    def start(car, land, ss, rs):
        g = geom(car[0], ss, rs)
        g["direct"](0, g["me"]).start()
        g["direct"](1, g["me"]).start()

    def mid(car, land, ss, rs):
        g = geom(car[0], ss, rs)
        for k in range(2):
            g["direct"](k, g["slots"][k]).wait_recv()
            g["relay"](k, g["slots"][k]).start()
            g["sibling"](k, g["slots"][k], True).start()

    def finish(car, land, ss, rs):
        g = geom(car[0], ss, rs)
        dslot = g["slots"][2]
        g["relay"](0, dslot).wait_recv()
        g["relay"](1, dslot).wait_recv()
        g["sibling"](2, dslot, True).start()
        for k in range(3):
            g["sibling"](k, g["slots"][k], False).wait_recv()
        for k in range(2):
            g["direct"](k, g["me"]).wait_send()
            g["relay"](k, g["slots"][k]).wait_send()
        for k in range(3):
            g["sibling"](k, g["slots"][k], True).wait_send()

    return _Hook([buf], [], 7, start, finish, mid=mid, mid_frac=mid_frac)


def _hook_pair(g, half_rows, slots=(0, 1, 2, 3), land=None):
    n, r, cc = g.shape
    h = r // 2 if half_rows else r

    def plan(car, landing, ss, rs):
        x, y, c = _my_pos()
        dst = landing[0] if land is None else car[1]
        cps = []
        for k, s in enumerate(slots):
            src = car[0].at[s, pl.ds(pl.multiple_of((1 - c) * h, 8), h)] if half_rows else car[0].at[s]
            cps.append(_remote(src, dst.at[s], ss, rs, k, (x, y, 1 - c)))
        return cps

    def start(car, land, ss, rs):
        for cp in plan(car, land, ss, rs):
            cp.start()

    def finish(car, land, ss, rs):
        cps = plan(car, land, ss, rs)
        for cp in cps:
            cp.wait_recv()
        for cp in cps:
            cp.wait_send()

    if land is None:
        return _Hook([g], [_sds((n, h, cc), g.dtype)], len(slots), start, finish)
    return _Hook([g, land], [], len(slots), start, finish)


def _hook_scatter_direct(s16):
    _, h, cc = s16.shape
    q = h // 2

    def plan(car, land, ss, rs):
        x, y, c = _my_pos()
        xn, yn, dg = _other_chips(x, y)
        s, (final, raw) = car[0], land
        slot = lambda p: 2 * p[0] + p[1]
        q0, q1 = pl.ds(0, q), pl.ds(q, q)
        return [_remote(s.at[slot(xn), q0], final.at[0, q0], ss, rs, 0, (xn[0], xn[1], c)),
                _remote(s.at[slot(yn), q1], final.at[1, q1], ss, rs, 1, (yn[0], yn[1], c)),
                _remote(s.at[slot(dg), q0], raw.at[1], ss, rs, 2, (xn[0], xn[1], c)),
                _remote(s.at[slot(dg), q1], raw.at[0], ss, rs, 3, (yn[0], yn[1], c))]

    def start(car, land, ss, rs):
        for cp in plan(car, land, ss, rs):
            cp.start()

    def finish(car, land, ss, rs):
        cps = plan(car, land, ss, rs)
        for cp in cps:
            cp.wait_recv()
        for cp in cps:
            cp.wait_send()

    return _Hook([s16], [_sds((2, h, cc), s16.dtype), _sds((2, q, cc), s16.dtype)], 4, start, finish)


def _hook_scatter_relay(comb, final):
    _, q, _ = comb.shape

    def plan(car, ss, rs):
        x, y, c = _my_pos()
        xn, yn, _ = _other_chips(x, y)
        cb, final_ref = car
        return [_remote(cb.at[0], final_ref.at[0, pl.ds(q, q)], ss, rs, 0, (xn[0], xn[1], c)),
                _remote(cb.at[1], final_ref.at[1, pl.ds(0, q)], ss, rs, 1, (yn[0], yn[1], c))]

    def start(car, land, ss, rs):
        for cp in plan(car, ss, rs):
            cp.start()

    def finish(car, land, ss, rs):
        cps = plan(car, ss, rs)
        for cp in cps:
            cp.wait_recv()
        for cp in cps:
            cp.wait_send()

    return _Hook([comb, final], [], 2, start, finish)


def _hook_halves(full):
    h = full.shape[0] // 2

    def half_copy(car, ss, rs, which):
        x, y, c = _my_pos()
        rows = car[0].at[pl.ds(pl.multiple_of((c + which - 2 * c * which) * h, 8), h)]
        return _remote(rows, rows, ss, rs, 0, (x, y, 1 - c))

    def start(car, land, ss, rs):
        half_copy(car, ss, rs, 0).start()

    def finish(car, land, ss, rs):
        half_copy(car, ss, rs, 1).wait_recv()
        half_copy(car, ss, rs, 0).wait_send()

    return _Hook([full], [], 1, start, finish)


def _pcall(body, *, name, grid, in_specs, out_specs, out_shape, operands, scratch=(), vmem=48, sem=None,
           prefetch=(), aliases=None, hook=None):
    n_pre, n_in, n_out, n_scr = len(prefetch), len(in_specs), len(out_specs), len(scratch)
    in_specs, out_specs, out_shape, scratch = list(in_specs), list(out_specs), list(out_shape), list(scratch)
    io_alias = {n_pre + a: b for a, b in (aliases or {}).items()}
    operands = list(operands)
    kernel_body = body
    if hook is not None:
        n_car, n_land = len(hook.carried), len(hook.landing)
        for t, arr in enumerate(hook.carried):
            io_alias[n_pre + n_in + t] = n_out + t
        in_specs += [_ANY] * n_car
        out_specs += [_ANY] * (n_car + n_land)
        out_shape += [_sds(a.shape, a.dtype) for a in hook.carried] + hook.landing
        scratch += [pltpu.SemaphoreType.DMA((hook.n_sems,)), pltpu.SemaphoreType.DMA((hook.n_sems,))]
        operands += hook.carried
        sem = ("arbitrary",) * len(grid)

        def kernel_body(*refs):
            pre, rest = refs[:n_pre], refs[n_pre:]
            ins = rest[:n_in]
            outs = rest[n_in + n_car:n_in + n_car + n_out]
            car = rest[n_in + n_car + n_out:n_in + 2 * n_car + n_out]
            land = rest[n_in + 2 * n_car + n_out:n_in + 2 * n_car + n_out + n_land]
            scr = rest[n_in + 2 * n_car + n_out + n_land:]
            send_sems, recv_sems = scr[n_scr], scr[n_scr + 1]
            first = pl.program_id(0) == 0
            last = pl.program_id(0) == grid[0] - 1
            for d in range(1, len(grid)):
                first = first & (pl.program_id(d) == 0)
                last = last & (pl.program_id(d) == grid[d] - 1)

            @pl.when(first)
            def _():
                hook.start(car, land, send_sems, recv_sems)

            if hook.mid is not None:
                step = pl.program_id(0)
                total = grid[0]
                for d in range(1, len(grid)):
                    step = step * grid[d] + pl.program_id(d)
                    total *= grid[d]

                @pl.when(step == int(total * hook.mid_frac))
                def _():
                    hook.mid(car, land, send_sems, recv_sems)

            body(*pre, *ins, *outs, *scr[:n_scr])

            @pl.when(last)
            def _():
                hook.finish(car, land, send_sems, recv_sems)

    params = pltpu.CompilerParams(vmem_limit_bytes=vmem * _MIB, dimension_semantics=sem,
                                  has_side_effects=hook is not None)
    if n_pre:
        call = pl.pallas_call(
            kernel_body, name=name, out_shape=out_shape, input_output_aliases=io_alias, compiler_params=params,
            grid_spec=pltpu.PrefetchScalarGridSpec(num_scalar_prefetch=n_pre, grid=grid, in_specs=in_specs,
                                                   out_specs=out_specs, scratch_shapes=scratch))
    else:
        call = pl.pallas_call(kernel_body, name=name, grid=grid, in_specs=in_specs, out_specs=out_specs,
                              out_shape=out_shape, scratch_shapes=scratch, input_output_aliases=io_alias,
                              compiler_params=params)
    return call(*prefetch, *operands)


def _gather_small(blk):
    r, cc = blk.shape

    def body(x_ref, o_ref, send_sems, recv_sems):
        x, y, c = _my_pos()
        me = 2 * x + y
        o_ref[me] = x_ref[...]
        sends = [_remote(x_ref, o_ref.at[me], send_sems, recv_sems, k, (px, py, c))
                 for k, (px, py) in enumerate(_other_chips(x, y))]
        for cp in sends:
            cp.start()
        for k, (px, py) in enumerate(_other_chips(x, y)):
            _remote(x_ref, o_ref.at[2 * px + py], send_sems, recv_sems, k, (px, py, c)).wait_recv()
        for cp in sends:
            cp.wait_send()

    vm = pl.BlockSpec(memory_space=pltpu.VMEM)
    return pl.pallas_call(
        body, name="gather_small", in_specs=[vm], out_specs=vm, out_shape=_sds((N_SH, r, cc), blk.dtype),
        scratch_shapes=[pltpu.SemaphoreType.DMA((3,)), pltpu.SemaphoreType.DMA((3,))],
        compiler_params=pltpu.CompilerParams(has_side_effects=True),
    )(blk)


N_DEV = 8


def _allreduce_small(pack, halves):
    r, cc = pack.shape
    ride = _hook_halves(halves)

    def body(x_ref, h_in_ref, o_ref, h_ref, buf_ref, send_sems, recv_sems, h_ss, h_rs):
        del h_in_ref
        ride.start([h_ref], [], h_ss, h_rs)
        x, y, c = _my_pos()
        me = 4 * x + 2 * y + c
        buf_ref[me] = x_ref[...]
        cps = []
        for k in range(1, N_DEV):
            peer = (x ^ ((k >> 2) & 1), y ^ ((k >> 1) & 1), c ^ (k & 1))
            cps.append(_remote(x_ref, buf_ref.at[me], send_sems, recv_sems, k - 1, peer))
        for cp in cps:
            cp.start()
        for k in range(1, N_DEV):
            peer = (x ^ ((k >> 2) & 1), y ^ ((k >> 1) & 1), c ^ (k & 1))
            src = 4 * peer[0] + 2 * peer[1] + peer[2]
            _remote(x_ref, buf_ref.at[src], send_sems, recv_sems, k - 1, peer).wait_recv()
        acc = buf_ref[0]
        for d in range(1, N_DEV):
            acc = acc + buf_ref[d]
        o_ref[...] = acc
        for cp in cps:
            cp.wait_send()
        ride.finish([h_ref], [], h_ss, h_rs)

    vm = pl.BlockSpec(memory_space=pltpu.VMEM)
    return pl.pallas_call(
        body, name="allreduce_small", in_specs=[vm, _ANY], out_specs=[vm, _ANY],
        out_shape=[_sds((r, cc), F32), _sds(halves.shape, halves.dtype)], input_output_aliases={1: 1},
        scratch_shapes=[pltpu.VMEM((N_DEV, r, cc), F32), pltpu.SemaphoreType.DMA((N_DEV - 1,)),
                        pltpu.SemaphoreType.DMA((N_DEV - 1,)), pltpu.SemaphoreType.DMA((1,)), pltpu.SemaphoreType.DMA((1,))],
        compiler_params=pltpu.CompilerParams(has_side_effects=True),
    )(pack, halves)


def _mm_nn(a, w, bias, *, name, grid, tm, tn, k, a_index, w_block, w_index, out_cols, out_index, carried=None, hook=None):
    m = a.shape[0]

    def body(a_ref, w_ref, b_ref, *rest):
        o_ref = rest[-1]
        wv = w_ref[...]
        acc = jnp.dot(a_ref[...].astype(_MXU), wv.reshape(k, tn).astype(_MXU), preferred_element_type=F32)
        o_ref[...] = acc + b_ref[...]

    operands = [a, w, bias] + ([carried] if carried is not None else [])
    return _pcall(
        body, name=name, grid=grid,
        in_specs=[pl.BlockSpec((tm, k), a_index), pl.BlockSpec(w_block, w_index),
                  pl.BlockSpec((1, tn), lambda i, j: (0, j))] + ([_ANY] if carried is not None else []),
        out_specs=[pl.BlockSpec((tm, tn), out_index)], out_shape=[_sds((m, out_cols), F32)], operands=operands,
        vmem=56, sem=("parallel", "parallel"), aliases={3: 0} if carried is not None else None, hook=hook)


def _mm_nt(a, w, *, name, grid, tm, tn, tk, a_index, w_block, w_index, out_cols, hook=None):
    m = a.shape[0]
    nk = grid[2]

    def body(a_ref, w_ref, o_ref, acc_ref):
        part = lax.dot_general(a_ref[...].astype(_MXU), w_ref[...].astype(_MXU), _NT, preferred_element_type=F32)
        if nk == 1:
            o_ref[...] = part
        else:
            kidx = pl.program_id(2)

            @pl.when(kidx == 0)
            def _():
                acc_ref[...] = part

            @pl.when(kidx > 0)
            def _():
                acc_ref[...] += part

            @pl.when(kidx == nk - 1)
            def _():
                o_ref[...] = acc_ref[...]

    return _pcall(
        body, name=name, grid=grid, in_specs=[pl.BlockSpec((tm, tk), a_index), pl.BlockSpec(w_block, w_index)],
        out_specs=[pl.BlockSpec((tm, tn), lambda i, j, q: (i, j))], out_shape=[_sds((m, out_cols), F32)],
        operands=[a, w], scratch=[pltpu.VMEM((tm, tn) if nk > 1 else (8, 128), F32)], vmem=60,
        sem=("parallel", "parallel", "arbitrary"), hook=hook)


def _mm_tn(at, b, *, name, grid, tm, tn, a_index, b_index, out_shape, out_block, out_index, carried=None,
           prefetch=(), hook=None, out_dtype=F32, colsum=False):
    t = at.shape[1]
    n_pre = len(prefetch)

    def body(*refs):
        a_ref, b_ref = refs[n_pre], refs[n_pre + 1]
        bv = b_ref[...]
        o_ref = refs[-2] if colsum else refs[-1]
        o_ref[...] = jnp.dot(a_ref[...].astype(_MXU), bv.astype(_MXU), preferred_element_type=F32).astype(out_dtype)
        if colsum:
            refs[-1][...] = jnp.sum(bv.astype(F32), axis=0, keepdims=True)

    operands = [at, b] + ([carried] if carried is not None else [])
    out_specs = [pl.BlockSpec(out_block, out_index)]
    out_shapes = [_sds(out_shape, out_dtype)]
    if colsum:
        out_specs.append(pl.BlockSpec((1, tn), b_index))
        out_shapes.append(_sds((1, b.shape[1]), F32))
    return _pcall(
        body, name=name, grid=grid,
        in_specs=[pl.BlockSpec((tm, t), a_index), pl.BlockSpec((t, tn), b_index)] + ([_ANY] if carried is not None else []),
        out_specs=out_specs, out_shape=out_shapes, operands=operands,
        vmem=56, sem=("parallel", "parallel"), aliases={2: 0} if carried is not None else None, prefetch=prefetch, hook=hook)


def _remote_tile(n):
    near = 2 * PER_IN
    if isinstance(n, int):
        return (n % 2, n // 2) if n < near else (2, n - near)
    return jnp.where(n < near, n % 2, 2), jnp.where(n < near, n // 2, n - near)


def _mm_z_gather(h16, w_own, b_in, order):
    n_tiles = N_SH * PER_IN
    n_remote = 3 * PER_IN
    half = D // 2

    def body(order_ref, a_ref, b_ref, w_in_ref, z_ref, w_ref, wbuf, tile_sems, ss, rs):
        del w_in_ref
        j = pl.program_id(0)
        x, y, c = _my_pos()
        me = 2 * x + y
        chips = _other_chips(x, y)
        mine_rows = pl.ds(pl.multiple_of(c * half, 16), half)
        sib_rows = pl.ds(pl.multiple_of((1 - c) * half, 16), half)

        slots = [2 * px + py for px, py in chips]
        cols = lambda t: pl.ds(t * TN_IN, TN_IN)
        q_rows = lambda r: pl.ds(pl.multiple_of(c * half + r * (half // 2), 16), half // 2)

        def direct(rel, t, slot):
            px, py = chips[rel]
            part = w_ref.at[slot, mine_rows, cols(t)]
            return _remote(part, part, ss, rs, 2 * t + rel, (px, py, c))

        def relay(r, t, slot):
            px, py = chips[1 - r]
            part = w_ref.at[slot, q_rows(r), cols(t)]
            return _remote(part, part, ss, rs, 2 * PER_IN + 2 * t + r, (px, py, c))

        def d2d(n, rows):
            rel, t = _remote_tile(n)
            part = w_ref.at[slots[rel], rows, cols(t)]
            return _remote(part, part, ss, rs, 4 * PER_IN + n, (x, y, 1 - c))

        def tile_copy(step):
            rel, t = _remote_tile(jnp.maximum(step - PER_IN, 0))
            slot = jnp.where(step < PER_IN, me, me ^ jnp.where(rel == 0, 2, jnp.where(rel == 1, 1, 3)))
            col = pl.multiple_of(jnp.where(step < PER_IN, step, t) * TN_IN, 128)
            return pltpu.make_async_copy(w_ref.at[slot, :, pl.ds(col, TN_IN)], wbuf.at[step % 2], tile_sems.at[step % 2])

        @pl.when(j == 0)
        def _():
            for t in range(PER_IN):
                direct(0, t, me).start()
                direct(1, t, me).start()
            tile_copy(0).start()

        for n in range(n_remote):
            rel, t = _remote_tile(n)

            @pl.when(j == n + PER_IN - 3)
            def _():
                if rel < 2:
                    direct(rel, t, slots[rel]).wait_recv()
                    relay(rel, t, slots[rel]).start()
                else:
                    relay(0, t, slots[2]).wait_recv()
                    relay(1, t, slots[2]).wait_recv()
                d2d(n, mine_rows).start()

            @pl.when(j == n + PER_IN - 2)
            def _():
                d2d(n, sib_rows).wait_recv()

        @pl.when(j + 1 < n_tiles)
        def _():
            tile_copy(j + 1).start()

        tile_copy(j).wait()
        z_ref[...] = jnp.dot(a_ref[...], wbuf[j % 2], preferred_element_type=F32) + b_ref[...]

        @pl.when(j == n_tiles - 1)
        def _():
            for t in range(PER_IN):
                for r in range(2):
                    direct(r, t, me).wait_send()
                    relay(r, t, slots[r]).wait_send()
            for n in range(n_remote):
                d2d(n, mine_rows).wait_send()

    def col_tile(j, o):
        rel, t = _remote_tile(jnp.maximum(j - PER_IN, 0))
        return 0, jnp.where(j < PER_IN, o[0] * PER_IN + j, o[1 + rel] * PER_IN + t)

    return pl.pallas_call(
        body, name="mm_z_gather",
        grid_spec=pltpu.PrefetchScalarGridSpec(
            num_scalar_prefetch=1, grid=(n_tiles,),
            in_specs=[pl.BlockSpec((TP, D), lambda j, o: (0, 0)), pl.BlockSpec((1, TN_IN), col_tile), _ANY],
            out_specs=[pl.BlockSpec((TP, TN_IN), col_tile), _ANY],
            scratch_shapes=[pltpu.VMEM((2, D, TN_IN), BF16), pltpu.SemaphoreType.DMA((2,)),
                            pltpu.SemaphoreType.DMA((4 * PER_IN + n_remote,)),
                            pltpu.SemaphoreType.DMA((4 * PER_IN + n_remote,))]),
        out_shape=[_sds((TP, D_IN), F32), _sds(w_own.shape, w_own.dtype)],
        input_output_aliases={3: 1},
        compiler_params=pltpu.CompilerParams(vmem_limit_bytes=60 * _MIB, dimension_semantics=("arbitrary",),
                                             has_side_effects=True),
    )(order, h16, b_in, w_own)


def _padded_rows(i, x_ref, meta_ref):
    head = jnp.concatenate([jnp.zeros((PAD, D), F32), meta_ref[...]], axis=0)
    return jnp.where(i == 0, head, x_ref[...])


def _stream_specs():
    return [pl.BlockSpec((BLK, D), lambda i: (jnp.maximum(i - 1, 0), 0)), pl.BlockSpec((N_META, D), lambda i: (0, 0))]


def _ln_emb_fwd(x, meta, g, b):
    def body(x_ref, meta_ref, g_ref, b_ref, h32_ref, h16_ref, h16t_ref):
        x = _padded_rows(pl.program_id(0), x_ref, meta_ref)
        mu = jnp.mean(x, axis=-1, keepdims=True)
        xc = x - mu
        var = jnp.mean(xc * xc, axis=-1, keepdims=True)
        y = xc * lax.rsqrt(var + LN_EPS) * g_ref[...] + b_ref[...]
        h32_ref[...] = y
        h16_ref[...] = y.astype(_MXU)
        h16t_ref[...] = y.T.astype(_MXU)

    row = pl.BlockSpec((BLK, D), lambda i: (i, 0))
    vec = pl.BlockSpec((1, D), lambda i: (0, 0))
    return _pcall(body, name="ln_emb_fwd", grid=(NBLK,), in_specs=_stream_specs() + [vec, vec],
                  out_specs=[row, row, pl.BlockSpec((D, BLK), lambda i: (0, i))],
                  out_shape=[_sds((TP, D), F32), _sds((TP, D), _MXU), _sds((D, TP), _MXU)], operands=[x, meta, g, b],
                  vmem=32, sem=("parallel",))


def _ln_emb_bwd(x, meta, g, dr, dhz, *, hook):
    def body(x_ref, meta_ref, g_ref, dr_ref, dhz_ref, dx_ref, dmeta_ref, acc_ref):
        i = pl.program_id(0)

        @pl.when(i == 0)
        def _():
            acc_ref[...] = jnp.zeros_like(acc_ref)

        x = _padded_rows(i, x_ref, meta_ref)
        mu = jnp.mean(x, axis=-1, keepdims=True)
        xc = x - mu
        var = jnp.mean(xc * xc, axis=-1, keepdims=True)
        rstd = lax.rsqrt(var + LN_EPS)
        xhat = xc * rstd
        dh = ALPHA * dr_ref[...] + dhz_ref[...]
        acc_ref[0:1, :] += jnp.sum(dh * xhat, axis=0, keepdims=True)
        acc_ref[1:2, :] += jnp.sum(dh, axis=0, keepdims=True)
        dxh = dh * g_ref[...]
        m1 = jnp.mean(dxh, axis=-1, keepdims=True)
        m2 = jnp.mean(dxh * xhat, axis=-1, keepdims=True)
        dx = rstd * (dxh - m1 - xhat * m2)
        dx_ref[...] = dx

        @pl.when(i == 0)
        def _():
            dmeta_ref[...] = dx[PAD:BLK]

    row = pl.BlockSpec((BLK, D), lambda i: (i, 0))
    vec = pl.BlockSpec((1, D), lambda i: (0, 0))
    xs, ms = _stream_specs()
    return _pcall(body, name="ln_emb_bwd", grid=(NBLK,), in_specs=[xs, ms, vec, row, row],
                  out_specs=[xs, ms, pl.BlockSpec((8, D), lambda i: (0, 0))],
                  out_shape=[_sds((SEQ, D), F32), _sds((N_META, D), F32), _sds((8, D), F32)],
                  operands=[x, meta, g, dr, dhz], vmem=32, sem=("arbitrary",), hook=hook)


def _mul_silu_fwd(a, z, off, *, name):
    def body(a_ref, z_ref, o_ref, t_ref):
        zz = z_ref[...]
        y = a_ref[...] * (zz * _sigmoid(zz))
        o_ref[...] = y.astype(_MXU)
        t_ref[...] = y.T.astype(_MXU)

    strip = pl.BlockSpec((TP, CW), lambda j: (0, j))
    return _pcall(body, name=name, grid=(D // CW,),
                  in_specs=[strip, pl.BlockSpec((TP, CW), lambda j: (0, off // CW + j))],
                  out_specs=[strip, pl.BlockSpec((CW, TP), lambda j: (j, 0))],
                  out_shape=[_sds((TP, D), _MXU), _sds((D, TP), _MXU)], operands=[a, z], vmem=56, sem=("parallel",))


def _mul_silu_bwd(dy, a, z, off, dz, *, name, hook=None):
    def body(dy_ref, a_ref, z_ref, dz_in, da_ref, dg_ref):
        zz = z_ref[...]
        sg = _sigmoid(zz)
        d = dy_ref[...]
        da_ref[...] = d * (zz * sg)
        dg_ref[...] = (d * a_ref[...] * (sg * (1.0 + zz * (1.0 - sg)))).astype(_MXU)

    blk = pl.BlockSpec((RT, CW), lambda i, j: (i, j))
    zblk = pl.BlockSpec((RT, CW), lambda i, j: (i, off // CW + j))
    return _pcall(body, name=name, grid=(TP // RT, D // CW), in_specs=[blk, blk, zblk, _ANY], out_specs=[blk, zblk],
                  out_shape=[_sds((TP, D), F32), _sds((TP, D_IN), _MXU)], operands=[dy, a, z, dz], vmem=32,
                  sem=("parallel", "parallel"), aliases={3: 1}, hook=hook)


def _merge_fwd(y2, z):
    w = 256

    def body(ya_ref, yb_ref, ga_ref, gb_ref, o_ref, t_ref):
        y = _sigmoid(ga_ref[...]) * ya_ref[...] + _sigmoid(gb_ref[...]) * yb_ref[...]
        o_ref[...] = y.astype(_MXU)
        t_ref[...] = y.T.astype(_MXU)

    nb = D // w
    strip = pl.BlockSpec((TP, w), lambda j: (0, j))
    return _pcall(body, name="merge_fwd", grid=(nb,),
                  in_specs=[strip, pl.BlockSpec((TP, w), lambda j: (0, nb + j)),
                            pl.BlockSpec((TP, w), lambda j: (0, OFF_G // w + j)),
                            pl.BlockSpec((TP, w), lambda j: (0, (OFF_G + D) // w + j))],
                  out_specs=[strip, pl.BlockSpec((w, TP), lambda j: (j, 0))],
                  out_shape=[_sds((TP, D), _MXU), _sds((D, TP), _MXU)], operands=[y2, y2, z, z], vmem=56,
                  sem=("parallel",))


def _merge_bwd(dmix, y2, z):
    nb = D // CW

    def body(dm_ref, y_ref, g_ref, dy_ref, dg_ref):
        dm = dm_ref[...]
        sg = _sigmoid(g_ref[...])
        dy_ref[...] = (dm * sg).astype(_MXU)
        dg_ref[...] = (dm * y_ref[...] * sg * (1.0 - sg)).astype(_MXU)

    blk = pl.BlockSpec((RT, CW), lambda i, j: (i, j))
    gblk = pl.BlockSpec((RT, CW), lambda i, j: (i, OFF_G // CW + j))
    return _pcall(body, name="merge_bwd", grid=(TP // RT, 2 * nb),
                  in_specs=[pl.BlockSpec((RT, CW), lambda i, j: (i, j % nb)), blk, gblk], out_specs=[blk, gblk],
                  out_shape=[_sds((TP, 2 * D), _MXU), _sds((TP, D_IN), _MXU)], operands=[dmix, y2, z], vmem=32,
                  sem=("parallel", "parallel"))


def _ln_out_loss(h32, out, target, g, b):
    def body(h_ref, o_ref, t_ref, g_ref, b_ref, dr_ref, dr16_ref, acc_ref):
        i = pl.program_id(0)

        @pl.when(i == 0)
        def _():
            acc_ref[...] = jnp.zeros_like(acc_ref)

        r = ALPHA * h_ref[...] + o_ref[...]
        mu = jnp.mean(r, axis=-1, keepdims=True)
        rc = r - mu
        var = jnp.mean(rc * rc, axis=-1, keepdims=True)
        rstd = lax.rsqrt(var + LN_EPS)
        xhat = rc * rstd
        gg = g_ref[...]
        y = xhat * gg + b_ref[...]
        real = (i >= 1).astype(F32)
        diff = (y - t_ref[...]) * real
        dy = diff * (1.0 / D)
        dxh = dy * gg
        m1 = jnp.mean(dxh, axis=-1, keepdims=True)
        m2 = jnp.mean(dxh * xhat, axis=-1, keepdims=True)
        dr = rstd * (dxh - m1 - xhat * m2)
        dr_ref[...] = dr
        dr16_ref[...] = dr.astype(_MXU)
        acc_ref[0:1, :] += jnp.sum(dy * xhat, axis=0, keepdims=True)
        acc_ref[1:2, :] += jnp.sum(dy, axis=0, keepdims=True)
        acc_ref[2:3, :] += jnp.sum(dr, axis=0, keepdims=True)
        acc_ref[3:4, :] += (0.5 / D) * jnp.sum(diff * diff)

    row = pl.BlockSpec((BLK, D), lambda i: (i, 0))
    vec = pl.BlockSpec((1, D), lambda i: (0, 0))
    return _pcall(body, name="ln_out_loss", grid=(NBLK,),
                  in_specs=[row, row, pl.BlockSpec((BLK, D), lambda i: (jnp.maximum(i - 1, 0), 0)), vec, vec],
                  out_specs=[row, row, pl.BlockSpec((8, D), lambda i: (0, 0))],
                  out_shape=[_sds((TP, D), F32), _sds((TP, D), _MXU), _sds((8, D), F32)],
                  operands=[h32, out, target, g, b], vmem=32, sem=("arbitrary",))


def _rnn_recompute(xr_ref, cw_ref, cb_ref, wra_ref, wri_ref, bra_ref, bri_ref, lam_ref):
    rows = lax.broadcasted_iota(jnp.int32, (TP, 1), 0)
    valid = (rows >= PAD).astype(F32)
    first = rows == PAD
    x = xr_ref[...] * valid
    cw = cw_ref[...]
    shifted = [x, pltpu.roll(x, 1, 0), pltpu.roll(x, 2, 0), pltpu.roll(x, 3, 0)]
    c = cb_ref[...] + cw[0:1, :] * shifted[0] + cw[1:2, :] * shifted[1] + cw[2:3, :] * shifted[2] + cw[3:4, :] * shifted[3]
    cm = c.astype(_MXU)
    gr = _sigmoid(jnp.dot(cm, wra_ref[...].astype(_MXU), preferred_element_type=F32) + bra_ref[...])
    gi = _sigmoid(jnp.dot(cm, wri_ref[...].astype(_MXU), preferred_element_type=F32) + bri_ref[...])
    lam = lam_ref[...]
    ls = jnp.minimum(lam, 0.0) - jnp.log(1.0 + jnp.exp(-jnp.abs(lam)))
    log_a = LRU_C * gr * ls
    a = jnp.exp(log_a)
    mult = jnp.where(first, 1.0, jnp.sqrt(1.0 - jnp.exp(2.0 * log_a)))
    return dict(valid=valid, first=first, shifted=shifted, c=c, cm=cm, gr=gr, gi=gi, ls=ls, a=a, mult=mult, lam=lam)


def _rnn_specs():
    col = pl.BlockSpec((TP, RB), lambda n: (0, n))
    vec = pl.BlockSpec((1, RB), lambda n: (0, n))
    return dict(col=col, vec=vec, cw=pl.BlockSpec((CONV_W, RB), lambda n: (0, n)),
                wblk=pl.BlockSpec((None, RB, RB), lambda n: (n, 0, 0)))


def _rnn_gates_fwd(z, conv_w, conv_b, w_ra, w_ri, b_ra, b_ri, lam):
    def body(xr_ref, cw_ref, cb_ref, wra_ref, wri_ref, bra_ref, bri_ref, lam_ref, a_ref, u_ref):
        r = _rnn_recompute(xr_ref, cw_ref, cb_ref, wra_ref, wri_ref, bra_ref, bri_ref, lam_ref)
        a_ref[...] = r["a"]
        u_ref[...] = r["mult"] * r["gi"] * r["c"] * r["valid"]

    s = _rnn_specs()
    return _pcall(body, name="rnn_gates_fwd", grid=(N_RB,),
                  in_specs=[s["col"], s["cw"], s["vec"], s["wblk"], s["wblk"], s["vec"], s["vec"], s["vec"]],
                  out_specs=[s["col"], s["col"]], out_shape=[_sds((TP, D), F32)] * 2,
                  operands=[z, conv_w, conv_b, w_ra, w_ri, b_ra, b_ri, lam], vmem=56, sem=("parallel",))


SCAN_ROWS = 272


SUB = 8


def _tile_scan(a, u, reverse):
    rows = lax.broadcasted_iota(jnp.int32, a.shape, 0)
    for d in (1, 2, 4):
        shift = SUB - d if reverse else d
        inside = (rows < SUB - d) if reverse else (rows >= d)
        u = u + a * jnp.where(inside, pltpu.roll(u, shift, 0), 0.0)
        a = a * jnp.where(inside, pltpu.roll(a, shift, 0), 1.0)
    return a, u


def _scan_fwd(a, u, *, hook):
    def body(a_ref, u_ref, h_ref, carry_ref):
        @pl.when(pl.program_id(1) == 0)
        def _():
            carry_ref[...] = jnp.zeros_like(carry_ref)

        def step(r, h):
            rows = pl.ds(pl.multiple_of(r * SUB, SUB), SUB)
            prod, part = _tile_scan(a_ref[rows, :], u_ref[rows, :], False)
            ht = part + prod * h
            h_ref[rows, :] = ht
            return ht[SUB - 1:SUB, :]

        carry_ref[...] = lax.fori_loop(0, SCAN_ROWS // SUB, step, carry_ref[...], unroll=2)

    blk = pl.BlockSpec((SCAN_ROWS, CW), lambda j, i: (i, j))
    return _pcall(body, name="scan_fwd", grid=(D // CW, TP // SCAN_ROWS), in_specs=[blk, blk], out_specs=[blk],
                  out_shape=[_sds((TP, D), F32)], operands=[a, u], scratch=[pltpu.VMEM((1, CW), F32)], vmem=32,
                  sem=("parallel", "arbitrary"), hook=hook)


def _scan_bwd(a, dh):
    nst = TP // SCAN_ROWS
    n_tiles = SCAN_ROWS // SUB

    def body(a_ref, d_ref, o_ref, lam_ref, anext_ref):
        @pl.when(pl.program_id(1) == 0)
        def _():
            lam_ref[...] = jnp.zeros_like(lam_ref)
            anext_ref[...] = jnp.zeros_like(anext_ref)

        def step(q, carry):
            lam_next, a_next = carry
            rows = pl.ds(pl.multiple_of((n_tiles - 1 - q) * SUB, SUB), SUB)
            at = a_ref[rows, :]
            last = lax.broadcasted_iota(jnp.int32, at.shape, 0) == SUB - 1
            b = jnp.where(last, a_next, pltpu.roll(at, SUB - 1, 0))
            prod, part = _tile_scan(b, d_ref[rows, :], True)
            lam = part + prod * lam_next
            o_ref[rows, :] = lam
            return lam[0:1, :], at[0:1, :]

        lam, an = lax.fori_loop(0, n_tiles, step, (lam_ref[...], anext_ref[...]), unroll=2)
        lam_ref[...] = lam
        anext_ref[...] = an

    blk = pl.BlockSpec((SCAN_ROWS, CW), lambda j, i: (nst - 1 - i, j))
    return _pcall(body, name="scan_bwd", grid=(D // CW, nst), in_specs=[blk, blk], out_specs=[blk],
                  out_shape=[_sds((TP, D), F32)], operands=[a, dh],
                  scratch=[pltpu.VMEM((1, CW), F32), pltpu.VMEM((1, CW), F32)], vmem=32,
                  sem=("parallel", "arbitrary"))[0]


def _rnn_gates_bwd(z, lam_s, hr, conv_w, conv_b, w_ra, w_ri, b_ra, b_ri, lam, dz, gsq):
    def body(xr_ref, ls_ref, hr_ref, cw_ref, cb_ref, wra_ref, wri_ref, bra_ref, bri_ref, lam_ref, dz_in, gsq_in,
             dx_ref, dw_ref, sums_ref):
        r = _rnn_recompute(xr_ref, cw_ref, cb_ref, wra_ref, wri_ref, bra_ref, bri_ref, lam_ref)
        valid, c, gr, gi, a, mult = r["valid"], r["c"], r["gr"], r["gi"], r["a"], r["mult"]
        du = ls_ref[...] * valid
        da = du * pltpu.roll(hr_ref[...], 1, 0)
        d_gi = du * mult * c
        dc = du * mult * gi
        dmult = du * gi * c
        dlog_a = da * a + jnp.where(r["first"], 0.0, -dmult * a * a / mult)
        d_gr = dlog_a * (LRU_C * r["ls"])
        dls = jnp.sum(dlog_a * (LRU_C * gr), axis=0, keepdims=True)
        dpre_r = d_gr * gr * (1.0 - gr)
        dpre_i = d_gi * gi * (1.0 - gi)
        pr = dpre_r.astype(_MXU)
        pi = dpre_i.astype(_MXU)
        dwra = lax.dot_general(r["cm"], pr, _TN, preferred_element_type=F32)
        dwri = lax.dot_general(r["cm"], pi, _TN, preferred_element_type=F32)
        for s in range(N_SH):
            dw_ref[s, 0:64, :] = dwra[64 * s:64 * (s + 1)]
            dw_ref[s, 64:128, :] = dwri[64 * s:64 * (s + 1)]
        dc = dc + lax.dot_general(pr, wra_ref[...].astype(_MXU), _NT, preferred_element_type=F32)
        dc = dc + lax.dot_general(pi, wri_ref[...].astype(_MXU), _NT, preferred_element_type=F32)
        cw = cw_ref[...]
        dx = cw[0:1, :] * dc
        for k in range(1, CONV_W):
            dx = dx + cw[k:k + 1, :] * pltpu.roll(dc, TP - k, 0)
        dx_ref[...] = (dx * valid).astype(_MXU)
        for k in range(CONV_W):
            sums_ref[k:k + 1, :] = jnp.sum(dc * r["shifted"][k], axis=0, keepdims=True)
        sums_ref[4:5, :] = jnp.sum(dc, axis=0, keepdims=True)
        sums_ref[5:6, :] = jnp.sum(dpre_r, axis=0, keepdims=True)
        sums_ref[6:7, :] = jnp.sum(dpre_i, axis=0, keepdims=True)
        sums_ref[7:8, :] = dls * _sigmoid(-r["lam"])

    s = _rnn_specs()
    return _pcall(
        body, name="rnn_gates_bwd", grid=(N_RB,),
        in_specs=[s["col"], s["col"], s["col"], s["cw"], s["vec"], s["wblk"], s["wblk"], s["vec"], s["vec"], s["vec"],
                  _ANY, _ANY],
        out_specs=[s["col"], pl.BlockSpec((N_SH, 128, RB), lambda n: (0, 3 * SQ_ROWS // 128, n)),
                   pl.BlockSpec((8, RB), lambda n: (0, n))],
        out_shape=[_sds((TP, D_IN), _MXU), _sds((N_SH, PACK_ROWS, D), F32), _sds((8, D), F32)],
        operands=[z, lam_s, hr, conv_w, conv_b, w_ra, w_ri, b_ra, b_ri, lam, dz, gsq], vmem=60, sem=("parallel",),
        aliases={10: 0, 11: 1})


def _rope_tables():
    half = HD // 2
    inv = ROPE_THETA ** (-jnp.arange(half, dtype=F32) / half)
    pos = (jnp.arange(TP) - PAD).astype(F32)
    ang = pos[:, None] * inv[None, :]
    return jnp.tile(jnp.cos(ang), (1, 4)), jnp.tile(jnp.sin(ang), (1, 4))


def _rope(x, cos_t, sin_t, sign):
    w = x.shape[1]
    lane = lax.broadcasted_iota(jnp.int32, x.shape, 1)
    first = (lane % HD) < (HD // 2)
    swapped = jnp.where(first, pltpu.roll(x, w - HD // 2, 1), pltpu.roll(x, HD // 2, 1))
    ct = jnp.tile(cos_t, (1, w // 128))
    st = jnp.tile(sin_t, (1, w // 128))
    return x * ct + swapped * jnp.where(first, -sign * st, sign * st)


def _rope_fwd(z, cos_t, sin_t):
    def body(q_ref, k_ref, v_ref, c_ref, s_ref, qo_ref, ko_ref, vo_ref):
        c = c_ref[...]
        s = s_ref[...]
        qo_ref[...] = _rope(q_ref[...], c, s, 1.0).astype(_MXU)
        ko_ref[...] = _rope(k_ref[...], c, s, 1.0).astype(_MXU)
        vo_ref[...] = v_ref[...].astype(_MXU)

    tab = pl.BlockSpec((BLK, 128), lambda i: (i, 0))
    kv = pl.BlockSpec((BLK, D_KV), lambda i: (i, 0))
    return _pcall(body, name="rope_fwd", grid=(NBLK,),
                  in_specs=[pl.BlockSpec((BLK, D), lambda i: (i, OFF_Q // D)),
                            pl.BlockSpec((BLK, D_KV), lambda i: (i, OFF_K // D_KV)),
                            pl.BlockSpec((BLK, D_KV), lambda i: (i, OFF_V // D_KV)), tab, tab],
                  out_specs=[pl.BlockSpec((BLK, D), lambda i: (i, 0)), kv, kv],
                  out_shape=[_sds((TP, D), _MXU), _sds((TP, D_KV), _MXU), _sds((TP, D_KV), _MXU)],
                  operands=[z, z, z, cos_t, sin_t], vmem=32, sem=("parallel",))


def _rope_bwd_k(dk, dv, cos_t, sin_t, dz):
    def body(dk_ref, dv_ref, c_ref, s_ref, dz_in, o_ref):
        o_ref[:, 0:D_KV] = _rope(dk_ref[...], c_ref[...], s_ref[...], -1.0).astype(_MXU)
        o_ref[:, D_KV:2 * D_KV] = dv_ref[...].astype(_MXU)

    tab = pl.BlockSpec((BLK, 128), lambda i: (i, 0))
    kv = pl.BlockSpec((BLK, D_KV), lambda i: (i, 0))
    return _pcall(body, name="rope_bwd_k", grid=(NBLK,), in_specs=[kv, kv, tab, tab, _ANY],
                  out_specs=[pl.BlockSpec((BLK, 2 * D_KV), lambda i: (i, OFF_K // (2 * D_KV)))],
                  out_shape=[_sds((TP, D_IN), _MXU)], operands=[dk, dv, cos_t, sin_t, dz], vmem=32, sem=("parallel",),
                  aliases={4: 0})[0]


def _attn_mask(i):
    ql = lax.broadcasted_iota(jnp.int32, (BLK, 3 * BLK), 0)
    kk = lax.broadcasted_iota(jnp.int32, (BLK, 3 * BLK), 1)
    kl = kk % BLK
    part = kk // BLK
    meta = (part == 0) & (kl >= PAD) & ((i >= 1) | (kl <= ql))
    prev = (part == 1) & (i >= 2) & (kl > ql)
    cur = (part == 2) & (i >= 1) & (kl <= ql)
    return meta | prev | cur


def _cat_kv(refs, g):
    return jnp.concatenate([r[:, HD * g:HD * (g + 1)] for r in refs], axis=0)


def _kv_specs():
    return [pl.BlockSpec((BLK, D_KV), lambda i: (0, 0)),
            pl.BlockSpec((BLK, D_KV), lambda i: (jnp.maximum(i - 1, 0), 0)),
            pl.BlockSpec((BLK, D_KV), lambda i: (i, 0))]


def _stack_heads(ref, g, width=HD):
    return jnp.concatenate([ref[:, width * (GRP * g + j):width * (GRP * g + j + 1)] for j in range(GRP)], axis=0)


def _attn_fwd(q, k, v, sinks, *, hook):
    def body(q_ref, k0_ref, kp_ref, kc_ref, v0_ref, vp_ref, vc_ref, sink_ref, o_ref, lse_ref, s_scr, p_scr):
        mask = _attn_mask(pl.program_id(0))
        for g in range(N_KV):
            s_scr[...] = lax.dot_general(_stack_heads(q_ref, g), _cat_kv((k0_ref, kp_ref, kc_ref), g), _NT,
                                         preferred_element_type=F32)
            for j in range(GRP):
                h = GRP * g + j
                rows = slice(BLK * j, BLK * (j + 1))
                sink = sink_ref[h]
                s = jnp.where(mask, s_scr[rows, :] * (HD ** -0.5), NEG_INF)
                mx = jnp.maximum(jnp.max(s, -1, keepdims=True), sink)
                p = jnp.exp(s - mx)
                den = jnp.sum(p, -1, keepdims=True) + jnp.exp(sink - mx)
                p_scr[rows, :] = (p * (1.0 / den)).astype(_MXU)
                lse_ref[:, h:h + 1] = mx + jnp.log(den)
            o8 = jnp.dot(p_scr[...], _cat_kv((v0_ref, vp_ref, vc_ref), g), preferred_element_type=F32)
            for j in range(GRP):
                h = GRP * g + j
                o_ref[:, HD * h:HD * (h + 1)] = o8[BLK * j:BLK * (j + 1)]

    return _pcall(body, name="attn_fwd", grid=(NBLK,),
                  in_specs=[pl.BlockSpec((BLK, D), lambda i: (i, 0))] + _kv_specs() + _kv_specs()
                           + [pl.BlockSpec(memory_space=pltpu.SMEM)],
                  out_specs=[pl.BlockSpec((BLK, D), lambda i: (i, 0)), pl.BlockSpec((BLK, N_Q), lambda i: (i, 0))],
                  out_shape=[_sds((TP, D), F32), _sds((TP, N_Q), F32)], operands=[q, k, k, k, v, v, v, sinks],
                  scratch=[pltpu.VMEM((GRP * BLK, 3 * BLK), F32), pltpu.VMEM((GRP * BLK, 3 * BLK), _MXU)],
                  vmem=40, sem=("parallel",), hook=hook)


def _attn_bwd(q, k, v, sinks, do, o, lse, cos_t, sin_t, dz, *, hook):
    def body(q_ref, k0_ref, kp_ref, kc_ref, v0_ref, vp_ref, vc_ref, sink_ref, do_ref, o_ref, lse_ref, c_ref, s_ref, dz_in,
             dq_ref, dk_ref, dv_ref, dsink_ref, dqrot_ref, s_scr, dp_scr, p_scr, ds_scr):
        i = pl.program_id(0)

        @pl.when(i == 0)
        def _():
            dk_ref[...] = jnp.zeros_like(dk_ref)
            dv_ref[...] = jnp.zeros_like(dv_ref)
            dsink_ref[...] = jnp.zeros_like(dsink_ref)

        mask = _attn_mask(i)
        row_starts = (0, pl.multiple_of(jnp.maximum(i - 1, 0) * BLK, BLK), pl.multiple_of(i * BLK, BLK))
        scale = HD ** -0.5
        nk = 3 * BLK
        low_lanes = lax.broadcasted_iota(jnp.int32, (nk, 2 * HD), 1) < HD

        def two_heads(ref, gp, j):
            h0 = 2 * GRP * gp + j
            return jnp.concatenate([ref[:, HD * h0:HD * (h0 + 1)], ref[:, HD * (h0 + GRP):HD * (h0 + GRP + 1)]], axis=1)

        def block_diag(refs, gp):
            both = jnp.concatenate([r[:, 2 * HD * gp:2 * HD * (gp + 1)] for r in refs], axis=0)
            zero = jnp.zeros_like(both)
            return jnp.concatenate([jnp.where(low_lanes, both, zero), jnp.where(low_lanes, zero, both)], axis=0)

        for gp in range(N_KV // 2):
            q2 = jnp.concatenate([two_heads(q_ref, gp, j) for j in range(GRP)], axis=0)
            dom2 = jnp.concatenate([two_heads(do_ref, gp, j) for j in range(GRP)], axis=0).astype(_MXU)
            k2 = block_diag((k0_ref, kp_ref, kc_ref), gp)
            s_scr[...] = lax.dot_general(q2, k2, _NT, preferred_element_type=F32)
            dp_scr[...] = lax.dot_general(dom2, block_diag((v0_ref, vp_ref, vc_ref), gp), _NT, preferred_element_type=F32)
            for a in range(2):
                for j in range(GRP):
                    h = GRP * (2 * gp + a) + j
                    hs = slice(HD * h, HD * (h + 1))
                    rows = slice(BLK * j, BLK * (j + 1))
                    cols = slice(nk * a, nk * (a + 1))
                    lse_h = lse_ref[:, h:h + 1]
                    delta = jnp.sum(do_ref[:, hs] * o_ref[:, hs], axis=-1, keepdims=True)
                    dsink_ref[0:1, h:h + 1] += jnp.sum(-jnp.exp(sink_ref[h] - lse_h) * delta, axis=0, keepdims=True)
                    p = jnp.exp(jnp.where(mask, s_scr[rows, cols] * scale, NEG_INF) - lse_h)
                    p_scr[rows, cols] = p.astype(_MXU)
                    ds_scr[rows, cols] = (p * (dp_scr[rows, cols] - delta) * scale).astype(_MXU)
            ds = ds_scr[...]
            dq2 = jnp.dot(ds, k2, preferred_element_type=F32)
            dk2 = lax.dot_general(ds, q2, _TN, preferred_element_type=F32)
            dv2 = lax.dot_general(p_scr[...], dom2, _TN, preferred_element_type=F32)
            for a in range(2):
                g = 2 * gp + a
                gs = slice(HD * g, HD * (g + 1))
                for part in range(3):
                    rows = pl.ds(row_starts[part], BLK)
                    blk = slice(nk * a + BLK * part, nk * a + BLK * (part + 1))
                    dk_ref[rows, gs] += dk2[blk, HD * a:HD * (a + 1)]
                    dv_ref[rows, gs] += dv2[blk, HD * a:HD * (a + 1)]
                for j in range(GRP):
                    h = GRP * g + j
                    dqrot_ref[:, HD * h:HD * (h + 1)] = dq2[BLK * j:BLK * (j + 1), HD * a:HD * (a + 1)]
        dq_ref[...] = _rope(dqrot_ref[...], c_ref[...], s_ref[...], -1.0).astype(_MXU)

    row = pl.BlockSpec((BLK, D), lambda i: (i, 0))
    tab = pl.BlockSpec((BLK, 128), lambda i: (i, 0))
    full_kv = pl.BlockSpec((TP, D_KV), lambda i: (0, 0))
    return _pcall(
        body, name="attn_bwd", grid=(NBLK,),
        in_specs=[row] + _kv_specs() + _kv_specs() + [pl.BlockSpec(memory_space=pltpu.SMEM), row, row,
                  pl.BlockSpec((BLK, N_Q), lambda i: (i, 0)), tab, tab, _ANY],
        out_specs=[pl.BlockSpec((BLK, D), lambda i: (i, OFF_Q // D)), full_kv, full_kv,
                   pl.BlockSpec((8, 128), lambda i: (0, 0))],
        out_shape=[_sds((TP, D_IN), _MXU), _sds((TP, D_KV), F32), _sds((TP, D_KV), F32), _sds((8, 128), F32)],
        operands=[q, k, k, k, v, v, v, sinks, do, o, lse, cos_t, sin_t, dz],
        scratch=[pltpu.VMEM((BLK, D), F32), pltpu.VMEM((GRP * BLK, 6 * BLK), F32), pltpu.VMEM((GRP * BLK, 6 * BLK), F32),
                 pltpu.VMEM((GRP * BLK, 6 * BLK), _MXU), pltpu.VMEM((GRP * BLK, 6 * BLK), _MXU)],
        vmem=48, sem=("arbitrary",), aliases={13: 0}, hook=hook)


def _cast_into_slot(w32, pos, *, name, tr):
    r, cc = w32.shape

    def body(pos_ref, w_ref, o_ref):
        o_ref[...] = w_ref[...].astype(BF16)

    return _pcall(body, name=name, grid=(r // tr,), in_specs=[pl.BlockSpec((tr, cc), lambda i, p: (i, 0))],
                  out_specs=[pl.BlockSpec((None, tr, cc), lambda i, p: (p[1], i, 0))],
                  out_shape=[_sds((N_SH, r, cc), BF16)], operands=[w32], vmem=32, sem=("parallel",), prefetch=(pos,))[0]


def _pair_add(g, got, pos, *, name, tr, g_has_both_halves, hook=None):
    n, h, cc = got.shape
    nt = h // tr

    def body(pos_ref, g_ref, r_ref, own_ref, s16_ref):
        s = g_ref[...] + r_ref[...].astype(F32)
        s16_ref[...] = s.astype(BF16)

        @pl.when(pl.program_id(1) == pos_ref[1])
        def _():
            own_ref[...] = s

    g_index = (lambda i, s, p: (s, p[0] * nt + i, 0)) if g_has_both_halves else (lambda i, s, p: (s, i, 0))
    return _pcall(body, name=name, grid=(nt, n),
                  in_specs=[pl.BlockSpec((None, tr, cc), g_index), pl.BlockSpec((None, tr, cc), lambda i, s, p: (s, i, 0))],
                  out_specs=[pl.BlockSpec((tr, cc), lambda i, s, p: (i, 0)),
                             pl.BlockSpec((None, tr, cc), lambda i, s, p: (s, i, 0))],
                  out_shape=[_sds((h, cc), F32), _sds((n, h, cc), BF16)], operands=[g, got], vmem=40,
                  sem=("parallel", "arbitrary"), prefetch=(pos,), hook=hook)


def _relay_add(s16, raw, pos, *, name, tr):
    _, q, cc = raw.shape
    nt = q // tr

    def body(pos_ref, s_ref, r_ref, o_ref):
        o_ref[...] = (s_ref[...].astype(F32) + r_ref[...].astype(F32)).astype(BF16)

    blk = pl.BlockSpec((None, tr, cc), lambda k, i, p: (k, i, 0))
    return _pcall(body, name=name, grid=(2, nt),
                  in_specs=[pl.BlockSpec((None, tr, cc), lambda k, i, p: (p[3 + k], (1 - k) * nt + i, 0)), blk],
                  out_specs=[blk], out_shape=[_sds((2, q, cc), BF16)], operands=[s16, raw], vmem=40,
                  sem=("parallel", "parallel"), prefetch=(pos,))[0]


def _sum_chips(own, got, pos, *, name, tr):
    h, cc = own.shape
    n_got = got.shape[0]
    nt = h // tr

    def body(pos_ref, o_ref, r_ref, out_ref):
        acc = o_ref[...]
        for k in range(n_got):
            acc = acc + r_ref[k].astype(F32)
        out_ref[...] = acc

    return _pcall(body, name=name, grid=(nt,),
                  in_specs=[pl.BlockSpec((tr, cc), lambda i, p: (i, 0)),
                            pl.BlockSpec((n_got, tr, cc), lambda i, p: (0, i, 0))],
                  out_specs=[pl.BlockSpec((tr, cc), lambda i, p: (p[0] * nt + i, 0))],
                  out_shape=[_sds((2 * h, cc), F32)], operands=[own, got], vmem=40, sem=("parallel",), prefetch=(pos,))[0]


def _adamw(w, g, m, v, *, name, tr, g_row0=0):
    r, cc = w.shape
    g_blk0 = g_row0 // tr

    def body(w_ref, g_ref, m_ref, v_ref, go_ref, d_ref, mo_ref, vo_ref):
        gg = g_ref[...]
        go_ref[...] = gg
        m_new = ADAM_B1 * m_ref[...] + (1.0 - ADAM_B1) * gg
        v_new = ADAM_B2 * v_ref[...] + (1.0 - ADAM_B2) * (gg * gg)
        m_hat = m_new / (1.0 - ADAM_B1 ** ADAM_STEP)
        v_hat = v_new / (1.0 - ADAM_B2 ** ADAM_STEP)
        d_ref[...] = -ADAM_LR * (m_hat / (jnp.sqrt(v_hat) + ADAM_EPS) + ADAM_WD * w_ref[...])
        mo_ref[...] = m_new
        vo_ref[...] = v_new

    blk = pl.BlockSpec((tr, cc), lambda i: (i, 0))
    gblk = pl.BlockSpec((tr, cc), lambda i: (g_blk0 + i, 0))
    return _pcall(body, name=name, grid=(r // tr,), in_specs=[blk, gblk, blk, blk], out_specs=[blk] * 4,
                  out_shape=[_sds((r, cc), F32)] * 4, operands=[w, g, m, v], vmem=48, sem=("parallel",))


_SMALL_ROWS = 40
_B_IN_ROWS = 7


def _row_pad(v, rows):
    flat = v.reshape(-1)
    return jnp.pad(flat, (0, rows * D - flat.shape[0])).reshape(rows, D)


def _pack_ra(w):
    return w.transpose(1, 0, 2).reshape(64, D)


def _unpack_ra(p, like):
    return p.reshape(64, N_RB, RB).transpose(1, 0, 2).reshape(like.shape)


def _gate_full(g4):
    return g4.reshape(N_SH, 64, N_RB, RB).transpose(2, 0, 1, 3).reshape(N_RB, RB, RB)


def kernel(x, meta_tokens, ln_emb_g, ln_emb_b, w_in, b_in, conv_w, conv_b, w_ra, b_ra, w_ri, b_ri, lru_lambda, sinks, w_rnn_out, w_attn_out, w_o, b_o, ln_g, ln_b, loss_target, m_meta_tokens, m_ln_emb_g, m_ln_emb_b, m_w_in, m_b_in, m_conv_w, m_conv_b, m_w_ra, m_b_ra, m_w_ri, m_b_ri, m_lru_lambda, m_sinks, m_w_rnn_out, m_w_attn_out, m_w_o, m_b_o, m_ln_g, m_ln_b, v_meta_tokens, v_ln_emb_g, v_ln_emb_b, v_w_in, v_b_in, v_conv_w, v_conv_b, v_w_ra, v_b_ra, v_w_ri, v_b_ri, v_lru_lambda, v_sinks, v_w_rnn_out, v_w_attn_out, v_w_o, v_b_o, v_ln_g, v_ln_b):
    xi, yi, ci = _my_pos()
    shard = 2 * xi + yi
    pos = jnp.stack([ci, shard, 1 - ci, shard ^ 2, shard ^ 1]).astype(jnp.int32)
    cos_t, sin_t = _rope_tables()
    zero_bias = jnp.zeros((1, D), F32)
    ln_emb_g2, ln_emb_b2 = ln_emb_g[None], ln_emb_b[None]

    small = jnp.concatenate([conv_w[0], meta_tokens, jnp.zeros((4, 512), F32)], axis=0)
    small4 = _gather_small(small)
    conv_w_full = small4[:, 0:4].transpose(1, 0, 2).reshape(CONV_W, D)
    meta_full = small4[:, 4:20].transpose(1, 0, 2).reshape(N_META, D)
    w_own = _cast_into_slot(w_in[0], pos, name="cast_w_in", tr=256)
    wa_own = _cast_into_slot(jnp.concatenate([w_attn_out[0], w_o[0], _pack_ra(w_ra[0]), _pack_ra(w_ri[0])], axis=0),
                             pos, name="cast_w_a", tr=288)
    wb_own = _cast_into_slot(w_rnn_out[0], pos, name="cast_w_b", tr=256)

    h32, h16, h16_t = _ln_emb_fwd(x[0], meta_full, ln_emb_g2, ln_emb_b2)
    order = jnp.stack([shard, shard ^ 2, shard ^ 1, shard ^ 3]).astype(jnp.int32)
    z, w_in4 = _mm_z_gather(h16, w_own, b_in, order)
    q, k, v = _rope_fwd(z, cos_t, sin_t)
    o, lse, wa4 = _attn_fwd(q, k, v, sinks[0], hook=_hook_gather(wa_own, 0.6))
    w_ra_full = _gate_full(wa4[:, 2 * SQ_ROWS:2 * SQ_ROWS + 64])
    w_ri_full = _gate_full(wa4[:, 2 * SQ_ROWS + 64:2 * SQ_ROWS + 128])
    a_dec, u_in = _rnn_gates_fwd(z, conv_w_full, conv_b, w_ra_full, w_ri_full, b_ra, b_ri, lru_lambda)
    hr, wb4 = _scan_fwd(a_dec, u_in, hook=_hook_gather(wb_own, 0.6))
    sq_w = {0: (wb4, 0), 1: (wa4, 0), 2: (wa4, 1)}

    def sq_nn(a, kk, bias, name, out_cols, out_index, carried=None):
        wp, blk = sq_w[kk]
        return _mm_nn(a, wp, bias, name=name, grid=(2, D // CW), tm=HALF_TP, tn=CW, k=D, a_index=lambda i, j: (i, 0),
                      w_block=(N_SH, SQ_ROWS, CW), w_index=lambda i, j: (0, blk, j), out_cols=out_cols,
                      out_index=out_index, carried=carried)[0]

    def sq_nt(a, a_blk, kk, name, hook=None):
        wp, blk = sq_w[kk]
        return _mm_nt(a, wp, name=name, grid=(2, N_SH, 1), tm=HALF_TP, tn=SQ_ROWS, tk=D,
                      a_index=lambda i, j, q: (i, a_blk), w_block=(None, SQ_ROWS, D),
                      w_index=lambda i, j, q: (j, blk, 0), out_cols=D, hook=hook)

    ya_in, ya_in_t = _mul_silu_fwd(hr, z, OFF_GR, name="gate_a_fwd")
    y2 = sq_nn(ya_in, 0, zero_bias, "mm_ya", 2 * D, lambda i, j: (i, j))
    yb_in, yb_in_t = _mul_silu_fwd(o, z, OFF_GA, name="gate_b_fwd")
    y2 = sq_nn(yb_in, 1, zero_bias, "mm_yb", 2 * D, lambda i, j: (i, D // CW + j), carried=y2)
    mixed, mixed_t = _merge_fwd(y2, z)
    out = sq_nn(mixed, 2, b_o, "mm_out", D, lambda i, j: (i, j))
    dr, dr16, sums_o = _ln_out_loss(h32, out, loss_target[0], ln_g, ln_b)

    def sq_tn(at, b, b_blk, kk, name, carried=None):
        return _mm_tn(at, b, name=name, grid=(N_SH, 1), tm=SQ_ROWS, tn=D, a_index=lambda i, j: (i, 0),
                      b_index=lambda i, j: (0, b_blk), out_shape=(N_SH, PACK_ROWS, D),
                      out_block=(None, SQ_ROWS, D), out_index=lambda i, j: (i, kk, 0), carried=carried)[0]

    gsq = sq_tn(mixed_t, dr16, 0, 2, "mm_dwo")
    dmix = sq_nt(dr16, 0, 2, "mm_dmix")[0]
    dy2, dz = _merge_bwd(dmix, y2, z)
    gsq = sq_tn(ya_in_t, dy2, 0, 0, "mm_dwrnn", carried=gsq)
    gsq = sq_tn(yb_in_t, dy2, 1, 1, "mm_dwattn", carried=gsq)
    dya_in = sq_nt(dy2, 0, 0, "mm_dyain")[0]
    dhr, dz = _mul_silu_bwd(dya_in, hr, z, OFF_GR, dz, name="gate_a_bwd")
    lam_s = _scan_bwd(a_dec, dhr)
    dz, gsq, sums_r = _rnn_gates_bwd(z, lam_s, hr, conv_w_full, conv_b, w_ra_full, w_ri_full, b_ra, b_ri, lru_lambda, dz, gsq)
    dyb_in, gsq, got_sq = sq_nt(dy2, 1, 1, "mm_dybin", hook=_hook_pair(gsq, True, (0, 1)))
    do, dz, gsq, got_sq = _mul_silu_bwd(dyb_in, o, z, OFF_GA, dz, name="gate_b_bwd",
                                        hook=_hook_pair(gsq, True, (2, 3), land=got_sq))
    own_sq, s16_sq = _pair_add(gsq, got_sq, pos, name="red_w_sq_add", tr=208, g_has_both_halves=True)
    dz, dk_rot, dv32, dsink, s16_sq, fin_sq, raw_sq = _attn_bwd(q, k, v, sinks[0], do, o, lse, cos_t, sin_t, dz,
                                                                hook=_hook_scatter_direct(s16_sq))
    dz = _rope_bwd_k(dk_rot, dv32, cos_t, sin_t, dz)
    comb_sq = _relay_add(s16_sq, raw_sq, pos, name="red_w_sq_relay", tr=208)

    def dwin(half_idx, name, hook, **kw):
        return _mm_tn(h16_t, dz, name=name, grid=(1, N_SH * PER_IN), tm=D // 2, tn=TN_IN,
                      a_index=lambda i, j, p: (p[half_idx], 0), b_index=lambda i, j, p: (0, j),
                      out_shape=(N_SH, D // 2, W_IN_COLS), out_block=(None, D // 2, TN_IN),
                      out_index=lambda i, j, p: (j // PER_IN, 0, j % PER_IN), prefetch=(pos,), hook=hook, **kw)

    gin_sib, g_b_in, comb_sq, fin_sq = dwin(2, "mm_dwin_sib", _hook_scatter_relay(comb_sq, fin_sq),
                                            out_dtype=BF16, colsum=True)
    red_sq = _sum_chips(own_sq, fin_sq, pos, name="red_w_sq_sum", tr=208)
    gin_own, gin_sib, got_in = dwin(0, "mm_dwin_own", _hook_pair(gin_sib, False))
    own_in, s16_in, g_sq = _pair_add(gin_own, got_in, pos, name="red_w_in_add", tr=128, g_has_both_halves=False,
                                     hook=_hook_halves(red_sq))
    dhz, s16_in, fin_in, raw_in = _mm_nt(dz, w_in4, name="mm_dhz", grid=(2, 2, N_SH), tm=HALF_TP, tn=D // 2,
                                         tk=W_IN_COLS, a_index=lambda i, j, q: (i, q),
                                         w_block=(None, D // 2, W_IN_COLS), w_index=lambda i, j, q: (q, j, 0), out_cols=D,
                                         hook=_hook_scatter_direct(s16_in))
    comb_in = _relay_add(s16_in, raw_in, pos, name="red_w_in_relay", tr=128)
    g_x, g_meta_local, sums_e, comb_in, fin_in = _ln_emb_bwd(x[0], meta_full, ln_emb_g2, dr, dhz,
                                                             hook=_hook_scatter_relay(comb_in, fin_in))
    red_in = _sum_chips(own_in, fin_in, pos, name="red_w_in_sum", tr=128)

    spack = jnp.concatenate([
        sums_e[0:1], sums_e[1:2], _row_pad(g_b_in, _B_IN_ROWS), sums_r[0:4], sums_r[4:5], sums_r[5:6], sums_r[6:7],
        sums_r[7:8], _row_pad(dsink[0:1, 0:N_Q], 1), sums_o[2:3], sums_o[0:1], sums_o[1:2], g_meta_local, sums_o[3:4],
        jnp.zeros((_SMALL_ROWS - 38, D), F32)], axis=0)
    sred, g_in = _allreduce_small(spack, red_in)

    big = {"w_in": [t.reshape(w_in.shape) for t in
                    _adamw(w_in[0], g_in, m_w_in[0], v_w_in[0], name="adamw_w_in", tr=128)]}
    for kk, (n, w_, m_, v_) in enumerate([("w_rnn_out", w_rnn_out, m_w_rnn_out, v_w_rnn_out),
                                          ("w_attn_out", w_attn_out, m_w_attn_out, v_w_attn_out), ("w_o", w_o, m_w_o, v_w_o)]):
        big[n] = [t.reshape(w_.shape) for t in
                  _adamw(w_[0], g_sq, m_[0], v_[0], name="adamw_" + n, tr=256, g_row0=SQ_ROWS * kk)]
    for kk, (n, w_, m_, v_) in enumerate([("w_ra", w_ra, m_w_ra, v_w_ra), ("w_ri", w_ri, m_w_ri, v_w_ri)]):
        big[n] = [_unpack_ra(t, w_) for t in
                  _adamw(_pack_ra(w_[0]), g_sq, _pack_ra(m_[0]), _pack_ra(v_[0]), name="adamw_" + n, tr=64,
                         g_row0=3 * SQ_ROWS + 64 * kk)]

    loss = sred[37, 0]
    col0 = shard * 512
    g_conv_w = lax.dynamic_slice(sred[9:13], (0, col0), (CONV_W, 512))
    g_meta = lax.dynamic_slice(sred[21:37], (0, col0), (N_META, 512))
    small_g = {"ln_emb_g": sred[0:1], "ln_emb_b": sred[1:2], "b_in": sred[2:9], "conv_w": g_conv_w.reshape(1, D),
               "conv_b": sred[13:14], "b_ra": sred[14:15], "b_ri": sred[15:16], "lru_lambda": sred[16:17],
               "sinks": sred[17:18], "b_o": sred[18:19], "ln_g": sred[19:20], "ln_b": sred[20:21],
               "meta_tokens": g_meta.reshape(4, D)}
    small_names = list(small_g)

    def small_pack(vals):
        rows = []
        for n in small_names:
            a = vals[n]
            if n == "b_in":
                a = _row_pad(a, _B_IN_ROWS)
            elif n == "sinks":
                a = _row_pad(a, 1)
            else:
                a = a.reshape(-1, D)
            rows.append(a)
        return jnp.concatenate(rows + [jnp.zeros((24 - 22, D), F32)], axis=0)

    w_small = dict(ln_emb_g=ln_emb_g, ln_emb_b=ln_emb_b, b_in=b_in, conv_w=conv_w, conv_b=conv_b, b_ra=b_ra, b_ri=b_ri,
                   lru_lambda=lru_lambda, sinks=sinks, b_o=b_o, ln_g=ln_g, ln_b=ln_b, meta_tokens=meta_tokens)
    m_small = dict(ln_emb_g=m_ln_emb_g, ln_emb_b=m_ln_emb_b, b_in=m_b_in, conv_w=m_conv_w, conv_b=m_conv_b, b_ra=m_b_ra,
                   b_ri=m_b_ri, lru_lambda=m_lru_lambda, sinks=m_sinks, b_o=m_b_o, ln_g=m_ln_g, ln_b=m_ln_b,
                   meta_tokens=m_meta_tokens)
    v_small = dict(ln_emb_g=v_ln_emb_g, ln_emb_b=v_ln_emb_b, b_in=v_b_in, conv_w=v_conv_w, conv_b=v_conv_b, b_ra=v_b_ra,
                   b_ri=v_b_ri, lru_lambda=v_lru_lambda, sinks=v_sinks, b_o=v_b_o, ln_g=v_ln_g, ln_b=v_ln_b,
                   meta_tokens=v_meta_tokens)
    g_small_pack = jnp.concatenate([small_g[n] for n in small_names] + [jnp.zeros((2, D), F32)], axis=0)
    small_res = _adamw(small_pack(w_small), g_small_pack, small_pack(m_small), small_pack(v_small),
                       name="adamw_small", tr=24)

    small_rows = {}
    r0 = 0
    for n in small_names:
        nrows = small_g[n].shape[0]
        small_rows[n] = (r0, nrows)
        r0 += nrows

    def small_out(packed, n, like):
        a, nrows = small_rows[n]
        flat = packed[a:a + nrows].reshape(-1)
        return flat[:like.size].reshape(like.shape)

    weights = dict(meta_tokens=meta_tokens, ln_emb_g=ln_emb_g, ln_emb_b=ln_emb_b, w_in=w_in, b_in=b_in, conv_w=conv_w,
                   conv_b=conv_b, w_ra=w_ra, b_ra=b_ra, w_ri=w_ri, b_ri=b_ri, lru_lambda=lru_lambda, sinks=sinks,
                   w_rnn_out=w_rnn_out, w_attn_out=w_attn_out, w_o=w_o, b_o=b_o, ln_g=ln_g, ln_b=ln_b)

    def outputs(which):
        return [big[n][which] if n in big else small_out(small_res[which], n, like) for n, like in weights.items()]

    return (loss, g_x[None], *outputs(0), *outputs(1), *outputs(2), *outputs(3))
```

```python
import jax
import jax.numpy as jnp
from jax import lax
from jax.experimental import pallas as pl
from jax.experimental.pallas import tpu as pltpu

F32 = jnp.float32
BF16 = jnp.bfloat16
_MXU = jnp.bfloat16

D = 2048
SEQ = 2048
N_META = 16
BLK = 128
PAD = BLK - N_META
TP = PAD + N_META + SEQ
NBLK = TP // BLK
HALF_TP = TP // 2
N_RB = 8
RB = 256
CONV_W = 4
LRU_C = 8.0
HD = 64
N_Q = 32
N_KV = 4
GRP = 8
D_KV = 256
NEG_INF = -1e30
LN_EPS = 1e-5
ALPHA = 2.0 ** 0.25
ROPE_THETA = 10000.0
OFF_GR, OFF_Q, OFF_K, OFF_V, OFF_GA, OFF_G = 2048, 4096, 6144, 6400, 6656, 8704
D_IN = 12800
N_SH = 4
W_IN_COLS = D_IN // N_SH
TN_IN = 640
PER_IN = W_IN_COLS // TN_IN
CW = 512
RT = TP // 4
SQ_ROWS = 512
PACK_ROWS = 3 * SQ_ROWS + 128

ADAM_LR = 0.001
ADAM_B1 = 0.9
ADAM_B2 = 0.999
ADAM_EPS = 1e-08
ADAM_WD = 0.01
ADAM_STEP = 10

MESH = pl.DeviceIdType.MESH
_MIB = 1024 * 1024
_ANY = pl.BlockSpec(memory_space=pl.ANY)
_NT = (((1,), (1,)), ((), ()))
_TN = (((0,), (0,)), ((), ()))


def _sds(shape, dtype):
    return jax.ShapeDtypeStruct(shape, dtype)


def _sigmoid(x):
    return 1.0 / (1.0 + jnp.exp(-x))


def _my_pos():
    return lax.axis_index("x"), lax.axis_index("y"), lax.axis_index("c")


def _other_chips(x, y):
    return [(1 - x, y), (x, 1 - y), (1 - x, 1 - y)]


def _remote(src, dst, send_sems, recv_sems, k, dev):
    return pltpu.make_async_remote_copy(src_ref=src, dst_ref=dst, send_sem=send_sems.at[k], recv_sem=recv_sems.at[k],
                                        device_id=dev, device_id_type=MESH)


class _Hook:
    def __init__(self, carried, landing, n_sems, start, finish, mid=None, mid_frac=0.5):
        self.carried, self.landing, self.n_sems, self.start, self.finish = list(carried), list(landing), n_sems, start, finish
        self.mid, self.mid_frac = mid, mid_frac


def _hook_gather(buf, mid_frac):
    half = buf.shape[1] // 2
    quarter = half // 2

    def geom(o, ss, rs):
        x, y, c = _my_pos()
        xn, yn, dg = _other_chips(x, y)
        slot = lambda p: 2 * p[0] + p[1]
        mine_rows = pl.ds(pl.multiple_of(c * half, 16), half)
        sib_rows = pl.ds(pl.multiple_of((1 - c) * half, 16), half)
        q_rows = lambda r: pl.ds(pl.multiple_of(c * half + r * quarter, 16), quarter)

        def cp(k, s, rows, dev):
            part = o.at[s, rows]
            return _remote(part, part, ss, rs, k, dev)

        return dict(
            direct=lambda k, s: cp(k, s, mine_rows, ((xn, yn)[k][0], (xn, yn)[k][1], c)),
            relay=lambda r, s: cp(2 + r, s, q_rows(r), ((yn, xn)[r][0], (yn, xn)[r][1], c)),
            sibling=lambda k, s, mine: cp(4 + k, s, mine_rows if mine else sib_rows, (x, y, 1 - c)),
            me=slot((x, y)), slots=(slot(xn), slot(yn), slot(dg)))

    def start(car, land, ss, rs):
        g = geom(car[0], ss, rs)
        g["direct"](0, g["me"]).start()
        g["direct"](1, g["me"]).start()

    def mid(car, land, ss, rs):
        g = geom(car[0], ss, rs)
        for k in range(2):
            g["direct"](k, g["slots"][k]).wait_recv()
            g["relay"](k, g["slots"][k]).start()
            g["sibling"](k, g["slots"][k], True).start()

    def finish(car, land, ss, rs):
        g = geom(car[0], ss, rs)
        dslot = g["slots"][2]
        g["relay"](0, dslot).wait_recv()
        g["relay"](1, dslot).wait_recv()
        g["sibling"](2, dslot, True).start()
        for k in range(3):
            g["sibling"](k, g["slots"][k], False).wait_recv()
        for k in range(2):
            g["direct"](k, g["me"]).wait_send()
            g["relay"](k, g["slots"][k]).wait_send()
        for k in range(3):
            g["sibling"](k, g["slots"][k], True).wait_send()

    return _Hook([buf], [], 7, start, finish, mid=mid, mid_frac=mid_frac)


def _hook_pair(g, half_rows, slots=(0, 1, 2, 3), land=None):
    n, r, cc = g.shape
    h = r // 2 if half_rows else r

    def plan(car, landing, ss, rs):
        x, y, c = _my_pos()
        dst = landing[0] if land is None else car[1]
        cps = []
        for k, s in enumerate(slots):
            src = car[0].at[s, pl.ds(pl.multiple_of((1 - c) * h, 8), h)] if half_rows else car[0].at[s]
            cps.append(_remote(src, dst.at[s], ss, rs, k, (x, y, 1 - c)))
        return cps

    def start(car, land, ss, rs):
        for cp in plan(car, land, ss, rs):
            cp.start()

    def finish(car, land, ss, rs):
        cps = plan(car, land, ss, rs)
        for cp in cps:
            cp.wait_recv()
        for cp in cps:
            cp.wait_send()

    if land is None:
        return _Hook([g], [_sds((n, h, cc), g.dtype)], len(slots), start, finish)
    return _Hook([g, land], [], len(slots), start, finish)


def _hook_scatter_direct(s16):
    _, h, cc = s16.shape
    q = h // 2

    def plan(car, land, ss, rs):
        x, y, c = _my_pos()
        xn, yn, dg = _other_chips(x, y)
        s, (final, raw) = car[0], land
        slot = lambda p: 2 * p[0] + p[1]
        q0, q1 = pl.ds(0, q), pl.ds(q, q)
        return [_remote(s.at[slot(xn), q0], final.at[0, q0], ss, rs, 0, (xn[0], xn[1], c)),
                _remote(s.at[slot(yn), q1], final.at[1, q1], ss, rs, 1, (yn[0], yn[1], c)),
                _remote(s.at[slot(dg), q0], raw.at[1], ss, rs, 2, (xn[0], xn[1], c)),
                _remote(s.at[slot(dg), q1], raw.at[0], ss, rs, 3, (yn[0], yn[1], c))]

    def start(car, land, ss, rs):
        for cp in plan(car, land, ss, rs):
            cp.start()

    def finish(car, land, ss, rs):
        cps = plan(car, land, ss, rs)
        for cp in cps:
            cp.wait_recv()
        for cp in cps:
            cp.wait_send()

    return _Hook([s16], [_sds((2, h, cc), s16.dtype), _sds((2, q, cc), s16.dtype)], 4, start, finish)


def _hook_scatter_relay(comb, final):
    _, q, _ = comb.shape

    def plan(car, ss, rs):
        x, y, c = _my_pos()
        xn, yn, _ = _other_chips(x, y)
        cb, final_ref = car
        return [_remote(cb.at[0], final_ref.at[0, pl.ds(q, q)], ss, rs, 0, (xn[0], xn[1], c)),
                _remote(cb.at[1], final_ref.at[1, pl.ds(0, q)], ss, rs, 1, (yn[0], yn[1], c))]

    def start(car, land, ss, rs):
        for cp in plan(car, ss, rs):
            cp.start()

    def finish(car, land, ss, rs):
        cps = plan(car, ss, rs)
        for cp in cps:
            cp.wait_recv()
        for cp in cps:
            cp.wait_send()

    return _Hook([comb, final], [], 2, start, finish)


def _hook_halves(full):
    h = full.shape[0] // 2

    def half_copy(car, ss, rs, which):
        x, y, c = _my_pos()
        rows = car[0].at[pl.ds(pl.multiple_of((c + which - 2 * c * which) * h, 8), h)]
        return _remote(rows, rows, ss, rs, 0, (x, y, 1 - c))

    def start(car, land, ss, rs):
        half_copy(car, ss, rs, 0).start()

    def finish(car, land, ss, rs):
        half_copy(car, ss, rs, 1).wait_recv()
        half_copy(car, ss, rs, 0).wait_send()

    return _Hook([full], [], 1, start, finish)


def _pcall(body, *, name, grid, in_specs, out_specs, out_shape, operands, scratch=(), vmem=48, sem=None,
           prefetch=(), aliases=None, hook=None):
    n_pre, n_in, n_out, n_scr = len(prefetch), len(in_specs), len(out_specs), len(scratch)
    in_specs, out_specs, out_shape, scratch = list(in_specs), list(out_specs), list(out_shape), list(scratch)
    io_alias = {n_pre + a: b for a, b in (aliases or {}).items()}
    operands = list(operands)
    kernel_body = body
    if hook is not None:
        n_car, n_land = len(hook.carried), len(hook.landing)
        for t, arr in enumerate(hook.carried):
            io_alias[n_pre + n_in + t] = n_out + t
        in_specs += [_ANY] * n_car
        out_specs += [_ANY] * (n_car + n_land)
        out_shape += [_sds(a.shape, a.dtype) for a in hook.carried] + hook.landing
        scratch += [pltpu.SemaphoreType.DMA((hook.n_sems,)), pltpu.SemaphoreType.DMA((hook.n_sems,))]
        operands += hook.carried
        sem = ("arbitrary",) * len(grid)

        def kernel_body(*refs):
            pre, rest = refs[:n_pre], refs[n_pre:]
            ins = rest[:n_in]
            outs = rest[n_in + n_car:n_in + n_car + n_out]
            car = rest[n_in + n_car + n_out:n_in + 2 * n_car + n_out]
            land = rest[n_in + 2 * n_car + n_out:n_in + 2 * n_car + n_out + n_land]
            scr = rest[n_in + 2 * n_car + n_out + n_land:]
            send_sems, recv_sems = scr[n_scr], scr[n_scr + 1]
            first = pl.program_id(0) == 0
            last = pl.program_id(0) == grid[0] - 1
            for d in range(1, len(grid)):
                first = first & (pl.program_id(d) == 0)
                last = last & (pl.program_id(d) == grid[d] - 1)

            @pl.when(first)
            def _():
                hook.start(car, land, send_sems, recv_sems)

            if hook.mid is not None:
                step = pl.program_id(0)
                total = grid[0]
                for d in range(1, len(grid)):
                    step = step * grid[d] + pl.program_id(d)
                    total *= grid[d]

                @pl.when(step == int(total * hook.mid_frac))
                def _():
                    hook.mid(car, land, send_sems, recv_sems)

            body(*pre, *ins, *outs, *scr[:n_scr])

            @pl.when(last)
            def _():
                hook.finish(car, land, send_sems, recv_sems)

    params = pltpu.CompilerParams(vmem_limit_bytes=vmem * _MIB, dimension_semantics=sem,
                                  has_side_effects=hook is not None)
    if n_pre:
        call = pl.pallas_call(
            kernel_body, name=name, out_shape=out_shape, input_output_aliases=io_alias, compiler_params=params,
            grid_spec=pltpu.PrefetchScalarGridSpec(num_scalar_prefetch=n_pre, grid=grid, in_specs=in_specs,
                                                   out_specs=out_specs, scratch_shapes=scratch))
    else:
        call = pl.pallas_call(kernel_body, name=name, grid=grid, in_specs=in_specs, out_specs=out_specs,
                              out_shape=out_shape, scratch_shapes=scratch, input_output_aliases=io_alias,
                              compiler_params=params)
    return call(*prefetch, *operands)


def _gather_small(blk):
    r, cc = blk.shape

    def body(x_ref, o_ref, send_sems, recv_sems):
        x, y, c = _my_pos()
        me = 2 * x + y
        o_ref[me] = x_ref[...]
        sends = [_remote(x_ref, o_ref.at[me], send_sems, recv_sems, k, (px, py, c))
                 for k, (px, py) in enumerate(_other_chips(x, y))]
        for cp in sends:
            cp.start()
        for k, (px, py) in enumerate(_other_chips(x, y)):
            _remote(x_ref, o_ref.at[2 * px + py], send_sems, recv_sems, k, (px, py, c)).wait_recv()
        for cp in sends:
            cp.wait_send()

    vm = pl.BlockSpec(memory_space=pltpu.VMEM)
    return pl.pallas_call(
        body, name="gather_small", in_specs=[vm], out_specs=vm, out_shape=_sds((N_SH, r, cc), blk.dtype),
        scratch_shapes=[pltpu.SemaphoreType.DMA((3,)), pltpu.SemaphoreType.DMA((3,))],
        compiler_params=pltpu.CompilerParams(has_side_effects=True),
    )(blk)


N_DEV = 8


def _allreduce_small(pack, halves):
    r, cc = pack.shape
    ride = _hook_halves(halves)

    def body(x_ref, h_in_ref, o_ref, h_ref, buf_ref, send_sems, recv_sems, h_ss, h_rs):
        del h_in_ref
        ride.start([h_ref], [], h_ss, h_rs)
        x, y, c = _my_pos()
        me = 4 * x + 2 * y + c
        buf_ref[me] = x_ref[...]
        cps = []
        for k in range(1, N_DEV):
            peer = (x ^ ((k >> 2) & 1), y ^ ((k >> 1) & 1), c ^ (k & 1))
            cps.append(_remote(x_ref, buf_ref.at[me], send_sems, recv_sems, k - 1, peer))
        for cp in cps:
            cp.start()
        for k in range(1, N_DEV):
            peer = (x ^ ((k >> 2) & 1), y ^ ((k >> 1) & 1), c ^ (k & 1))
            src = 4 * peer[0] + 2 * peer[1] + peer[2]
            _remote(x_ref, buf_ref.at[src], send_sems, recv_sems, k - 1, peer).wait_recv()
        acc = buf_ref[0]
        for d in range(1, N_DEV):
            acc = acc + buf_ref[d]
        o_ref[...] = acc
        for cp in cps:
            cp.wait_send()
        ride.finish([h_ref], [], h_ss, h_rs)

    vm = pl.BlockSpec(memory_space=pltpu.VMEM)
    return pl.pallas_call(
        body, name="allreduce_small", in_specs=[vm, _ANY], out_specs=[vm, _ANY],
        out_shape=[_sds((r, cc), F32), _sds(halves.shape, halves.dtype)], input_output_aliases={1: 1},
        scratch_shapes=[pltpu.VMEM((N_DEV, r, cc), F32), pltpu.SemaphoreType.DMA((N_DEV - 1,)),
                        pltpu.SemaphoreType.DMA((N_DEV - 1,)), pltpu.SemaphoreType.DMA((1,)), pltpu.SemaphoreType.DMA((1,))],
        compiler_params=pltpu.CompilerParams(has_side_effects=True),
    )(pack, halves)


def _mm_nn(a, w, bias, *, name, grid, tm, tn, k, a_index, w_block, w_index, out_cols, out_index, carried=None, hook=None):
    m = a.shape[0]

    def body(a_ref, w_ref, b_ref, *rest):
        o_ref = rest[-1]
        wv = w_ref[...]
        acc = jnp.dot(a_ref[...].astype(_MXU), wv.reshape(k, tn).astype(_MXU), preferred_element_type=F32)
        o_ref[...] = acc + b_ref[...]

    operands = [a, w, bias] + ([carried] if carried is not None else [])
    return _pcall(
        body, name=name, grid=grid,
        in_specs=[pl.BlockSpec((tm, k), a_index), pl.BlockSpec(w_block, w_index),
                  pl.BlockSpec((1, tn), lambda i, j: (0, j))] + ([_ANY] if carried is not None else []),
        out_specs=[pl.BlockSpec((tm, tn), out_index)], out_shape=[_sds((m, out_cols), F32)], operands=operands,
        vmem=56, sem=("parallel", "parallel"), aliases={3: 0} if carried is not None else None, hook=hook)


def _mm_nt(a, w, *, name, grid, tm, tn, tk, a_index, w_block, w_index, out_cols, hook=None):
    m = a.shape[0]
    nk = grid[2]

    def body(a_ref, w_ref, o_ref, acc_ref):
        part = lax.dot_general(a_ref[...].astype(_MXU), w_ref[...].astype(_MXU), _NT, preferred_element_type=F32)
        if nk == 1:
            o_ref[...] = part
        else:
            kidx = pl.program_id(2)

            @pl.when(kidx == 0)
            def _():
                acc_ref[...] = part

            @pl.when(kidx > 0)
            def _():
                acc_ref[...] += part

            @pl.when(kidx == nk - 1)
            def _():
                o_ref[...] = acc_ref[...]

    return _pcall(
        body, name=name, grid=grid, in_specs=[pl.BlockSpec((tm, tk), a_index), pl.BlockSpec(w_block, w_index)],
        out_specs=[pl.BlockSpec((tm, tn), lambda i, j, q: (i, j))], out_shape=[_sds((m, out_cols), F32)],
        operands=[a, w], scratch=[pltpu.VMEM((tm, tn) if nk > 1 else (8, 128), F32)], vmem=60,
        sem=("parallel", "parallel", "arbitrary"), hook=hook)


def _mm_tn(at, b, *, name, grid, tm, tn, a_index, b_index, out_shape, out_block, out_index, carried=None,
           prefetch=(), hook=None, out_dtype=F32, colsum=False):
    t = at.shape[1]
    n_pre = len(prefetch)

    def body(*refs):
        a_ref, b_ref = refs[n_pre], refs[n_pre + 1]
        bv = b_ref[...]
        o_ref = refs[-2] if colsum else refs[-1]
        o_ref[...] = jnp.dot(a_ref[...].astype(_MXU), bv.astype(_MXU), preferred_element_type=F32).astype(out_dtype)
        if colsum:
            refs[-1][...] = jnp.sum(bv.astype(F32), axis=0, keepdims=True)

    operands = [at, b] + ([carried] if carried is not None else [])
    out_specs = [pl.BlockSpec(out_block, out_index)]
    out_shapes = [_sds(out_shape, out_dtype)]
    if colsum:
        out_specs.append(pl.BlockSpec((1, tn), b_index))
        out_shapes.append(_sds((1, b.shape[1]), F32))
    return _pcall(
        body, name=name, grid=grid,
        in_specs=[pl.BlockSpec((tm, t), a_index), pl.BlockSpec((t, tn), b_index)] + ([_ANY] if carried is not None else []),
        out_specs=out_specs, out_shape=out_shapes, operands=operands,
        vmem=56, sem=("parallel", "parallel"), aliases={2: 0} if carried is not None else None, prefetch=prefetch, hook=hook)


def _remote_tile(n):
    near = 2 * PER_IN
    if isinstance(n, int):
        return (n % 2, n // 2) if n < near else (2, n - near)
    return jnp.where(n < near, n % 2, 2), jnp.where(n < near, n // 2, n - near)


def _mm_z_gather(h16, w_own, b_in, order):
    n_tiles = N_SH * PER_IN
    n_remote = 3 * PER_IN
    half = D // 2

    def body(order_ref, a_ref, b_ref, w_in_ref, z_ref, w_ref, wbuf, tile_sems, ss, rs):
        del w_in_ref
        j = pl.program_id(0)
        x, y, c = _my_pos()
        me = 2 * x + y
        chips = _other_chips(x, y)
        mine_rows = pl.ds(pl.multiple_of(c * half, 16), half)
        sib_rows = pl.ds(pl.multiple_of((1 - c) * half, 16), half)

        slots = [2 * px + py for px, py in chips]
        cols = lambda t: pl.ds(t * TN_IN, TN_IN)
        q_rows = lambda r: pl.ds(pl.multiple_of(c * half + r * (half // 2), 16), half // 2)

        def direct(rel, t, slot):
            px, py = chips[rel]
            part = w_ref.at[slot, mine_rows, cols(t)]
            return _remote(part, part, ss, rs, 2 * t + rel, (px, py, c))

        def relay(r, t, slot):
            px, py = chips[1 - r]
            part = w_ref.at[slot, q_rows(r), cols(t)]
            return _remote(part, part, ss, rs, 2 * PER_IN + 2 * t + r, (px, py, c))

        def d2d(n, rows):
            rel, t = _remote_tile(n)
            part = w_ref.at[slots[rel], rows, cols(t)]
            return _remote(part, part, ss, rs, 4 * PER_IN + n, (x, y, 1 - c))

        def tile_copy(step):
            rel, t = _remote_tile(jnp.maximum(step - PER_IN, 0))
            slot = jnp.where(step < PER_IN, me, me ^ jnp.where(rel == 0, 2, jnp.where(rel == 1, 1, 3)))
            col = pl.multiple_of(jnp.where(step < PER_IN, step, t) * TN_IN, 128)
            return pltpu.make_async_copy(w_ref.at[slot, :, pl.ds(col, TN_IN)], wbuf.at[step % 2], tile_sems.at[step % 2])

        @pl.when(j == 0)
        def _():
            for t in range(PER_IN):
                direct(0, t, me).start()
                direct(1, t, me).start()
            tile_copy(0).start()

        for n in range(n_remote):
            rel, t = _remote_tile(n)

            @pl.when(j == n + PER_IN - 3)
            def _():
                if rel < 2:
                    direct(rel, t, slots[rel]).wait_recv()
                    relay(rel, t, slots[rel]).start()
                else:
                    relay(0, t, slots[2]).wait_recv()
                    relay(1, t, slots[2]).wait_recv()
                d2d(n, mine_rows).start()

            @pl.when(j == n + PER_IN - 2)
            def _():
                d2d(n, sib_rows).wait_recv()

        @pl.when(j + 1 < n_tiles)
        def _():
            tile_copy(j + 1).start()

        tile_copy(j).wait()
        z_ref[...] = jnp.dot(a_ref[...], wbuf[j % 2], preferred_element_type=F32) + b_ref[...]

        @pl.when(j == n_tiles - 1)
        def _():
            for t in range(PER_IN):
                for r in range(2):
                    direct(r, t, me).wait_send()
                    relay(r, t, slots[r]).wait_send()
            for n in range(n_remote):
                d2d(n, mine_rows).wait_send()

    def col_tile(j, o):
        rel, t = _remote_tile(jnp.maximum(j - PER_IN, 0))
        return 0, jnp.where(j < PER_IN, o[0] * PER_IN + j, o[1 + rel] * PER_IN + t)

    return pl.pallas_call(
        body, name="mm_z_gather",
        grid_spec=pltpu.PrefetchScalarGridSpec(
            num_scalar_prefetch=1, grid=(n_tiles,),
            in_specs=[pl.BlockSpec((TP, D), lambda j, o: (0, 0)), pl.BlockSpec((1, TN_IN), col_tile), _ANY],
            out_specs=[pl.BlockSpec((TP, TN_IN), col_tile), _ANY],
            scratch_shapes=[pltpu.VMEM((2, D, TN_IN), BF16), pltpu.SemaphoreType.DMA((2,)),
                            pltpu.SemaphoreType.DMA((4 * PER_IN + n_remote,)),
                            pltpu.SemaphoreType.DMA((4 * PER_IN + n_remote,))]),
        out_shape=[_sds((TP, D_IN), F32), _sds(w_own.shape, w_own.dtype)],
        input_output_aliases={3: 1},
        compiler_params=pltpu.CompilerParams(vmem_limit_bytes=60 * _MIB, dimension_semantics=("arbitrary",),
                                             has_side_effects=True),
    )(order, h16, b_in, w_own)


def _padded_rows(i, x_ref, meta_ref):
    head = jnp.concatenate([jnp.zeros((PAD, D), F32), meta_ref[...]], axis=0)
    return jnp.where(i == 0, head, x_ref[...])


def _stream_specs():
    return [pl.BlockSpec((BLK, D), lambda i: (jnp.maximum(i - 1, 0), 0)), pl.BlockSpec((N_META, D), lambda i: (0, 0))]


def _ln_emb_fwd(x, meta, g, b):
    def body(x_ref, meta_ref, g_ref, b_ref, h32_ref, h16_ref, h16t_ref):
        x = _padded_rows(pl.program_id(0), x_ref, meta_ref)
        mu = jnp.mean(x, axis=-1, keepdims=True)
        xc = x - mu
        var = jnp.mean(xc * xc, axis=-1, keepdims=True)
        y = xc * lax.rsqrt(var + LN_EPS) * g_ref[...] + b_ref[...]
        h32_ref[...] = y
        h16_ref[...] = y.astype(_MXU)
        h16t_ref[...] = y.T.astype(_MXU)

    row = pl.BlockSpec((BLK, D), lambda i: (i, 0))
    vec = pl.BlockSpec((1, D), lambda i: (0, 0))
    return _pcall(body, name="ln_emb_fwd", grid=(NBLK,), in_specs=_stream_specs() + [vec, vec],
                  out_specs=[row, row, pl.BlockSpec((D, BLK), lambda i: (0, i))],
                  out_shape=[_sds((TP, D), F32), _sds((TP, D), _MXU), _sds((D, TP), _MXU)], operands=[x, meta, g, b],
                  vmem=32, sem=("parallel",))


def _ln_emb_bwd(x, meta, g, dr, dhz, *, hook):
    def body(x_ref, meta_ref, g_ref, dr_ref, dhz_ref, dx_ref, dmeta_ref, acc_ref):
        i = pl.program_id(0)

        @pl.when(i == 0)
        def _():
            acc_ref[...] = jnp.zeros_like(acc_ref)

        x = _padded_rows(i, x_ref, meta_ref)
        mu = jnp.mean(x, axis=-1, keepdims=True)
        xc = x - mu
        var = jnp.mean(xc * xc, axis=-1, keepdims=True)
        rstd = lax.rsqrt(var + LN_EPS)
        xhat = xc * rstd
        dh = ALPHA * dr_ref[...] + dhz_ref[...]
        acc_ref[0:1, :] += jnp.sum(dh * xhat, axis=0, keepdims=True)
        acc_ref[1:2, :] += jnp.sum(dh, axis=0, keepdims=True)
        dxh = dh * g_ref[...]
        m1 = jnp.mean(dxh, axis=-1, keepdims=True)
        m2 = jnp.mean(dxh * xhat, axis=-1, keepdims=True)
        dx = rstd * (dxh - m1 - xhat * m2)
        dx_ref[...] = dx

        @pl.when(i == 0)
        def _():
            dmeta_ref[...] = dx[PAD:BLK]

    row = pl.BlockSpec((BLK, D), lambda i: (i, 0))
    vec = pl.BlockSpec((1, D), lambda i: (0, 0))
    xs, ms = _stream_specs()
    return _pcall(body, name="ln_emb_bwd", grid=(NBLK,), in_specs=[xs, ms, vec, row, row],
                  out_specs=[xs, ms, pl.BlockSpec((8, D), lambda i: (0, 0))],
                  out_shape=[_sds((SEQ, D), F32), _sds((N_META, D), F32), _sds((8, D), F32)],
                  operands=[x, meta, g, dr, dhz], vmem=32, sem=("arbitrary",), hook=hook)


def _mul_silu_fwd(a, z, off, *, name):
    def body(a_ref, z_ref, o_ref, t_ref):
        zz = z_ref[...]
        y = a_ref[...] * (zz * _sigmoid(zz))
        o_ref[...] = y.astype(_MXU)
        t_ref[...] = y.T.astype(_MXU)

    strip = pl.BlockSpec((TP, CW), lambda j: (0, j))
    return _pcall(body, name=name, grid=(D // CW,),
                  in_specs=[strip, pl.BlockSpec((TP, CW), lambda j: (0, off // CW + j))],
                  out_specs=[strip, pl.BlockSpec((CW, TP), lambda j: (j, 0))],
                  out_shape=[_sds((TP, D), _MXU), _sds((D, TP), _MXU)], operands=[a, z], vmem=56, sem=("parallel",))


def _mul_silu_bwd(dy, a, z, off, dz, *, name, hook=None):
    def body(dy_ref, a_ref, z_ref, dz_in, da_ref, dg_ref):
        zz = z_ref[...]
        sg = _sigmoid(zz)
        d = dy_ref[...]
        da_ref[...] = d * (zz * sg)
        dg_ref[...] = (d * a_ref[...] * (sg * (1.0 + zz * (1.0 - sg)))).astype(_MXU)

    blk = pl.BlockSpec((RT, CW), lambda i, j: (i, j))
    zblk = pl.BlockSpec((RT, CW), lambda i, j: (i, off // CW + j))
    return _pcall(body, name=name, grid=(TP // RT, D // CW), in_specs=[blk, blk, zblk, _ANY], out_specs=[blk, zblk],
                  out_shape=[_sds((TP, D), F32), _sds((TP, D_IN), _MXU)], operands=[dy, a, z, dz], vmem=32,
                  sem=("parallel", "parallel"), aliases={3: 1}, hook=hook)


def _merge_fwd(y2, z, *, hook):
    w = 256

    def body(ya_ref, yb_ref, ga_ref, gb_ref, o_ref, t_ref):
        y = _sigmoid(ga_ref[...]) * ya_ref[...] + _sigmoid(gb_ref[...]) * yb_ref[...]
        o_ref[...] = y.astype(_MXU)
        t_ref[...] = y.T.astype(_MXU)

    nb = D // w
    strip = pl.BlockSpec((TP, w), lambda j: (0, j))
    return _pcall(body, name="merge_fwd", grid=(nb,),
                  in_specs=[strip, pl.BlockSpec((TP, w), lambda j: (0, nb + j)),
                            pl.BlockSpec((TP, w), lambda j: (0, OFF_G // w + j)),
                            pl.BlockSpec((TP, w), lambda j: (0, (OFF_G + D) // w + j))],
                  out_specs=[strip, pl.BlockSpec((w, TP), lambda j: (j, 0))],
                  out_shape=[_sds((TP, D), _MXU), _sds((D, TP), _MXU)], operands=[y2, y2, z, z], vmem=56,
                  sem=("parallel",), hook=hook)


def _merge_bwd(dmix, y2, z):
    nb = D // CW

    def body(dm_ref, y_ref, g_ref, dy_ref, dg_ref):
        dm = dm_ref[...]
        sg = _sigmoid(g_ref[...])
        dy_ref[...] = (dm * sg).astype(_MXU)
        dg_ref[...] = (dm * y_ref[...] * sg * (1.0 - sg)).astype(_MXU)

    blk = pl.BlockSpec((RT, CW), lambda i, j: (i, j))
    gblk = pl.BlockSpec((RT, CW), lambda i, j: (i, OFF_G // CW + j))
    return _pcall(body, name="merge_bwd", grid=(TP // RT, 2 * nb),
                  in_specs=[pl.BlockSpec((RT, CW), lambda i, j: (i, j % nb)), blk, gblk], out_specs=[blk, gblk],
                  out_shape=[_sds((TP, 2 * D), _MXU), _sds((TP, D_IN), _MXU)], operands=[dmix, y2, z], vmem=32,
                  sem=("parallel", "parallel"))


def _ln_out_loss(h32, out, target, g, b):
    def body(h_ref, o_ref, t_ref, g_ref, b_ref, dr_ref, dr16_ref, acc_ref):
        i = pl.program_id(0)

        @pl.when(i == 0)
        def _():
            acc_ref[...] = jnp.zeros_like(acc_ref)

        r = ALPHA * h_ref[...] + o_ref[...]
        mu = jnp.mean(r, axis=-1, keepdims=True)
        rc = r - mu
        var = jnp.mean(rc * rc, axis=-1, keepdims=True)
        rstd = lax.rsqrt(var + LN_EPS)
        xhat = rc * rstd
        gg = g_ref[...]
        y = xhat * gg + b_ref[...]
        real = (i >= 1).astype(F32)
        diff = (y - t_ref[...]) * real
        dy = diff * (1.0 / D)
        dxh = dy * gg
        m1 = jnp.mean(dxh, axis=-1, keepdims=True)
        m2 = jnp.mean(dxh * xhat, axis=-1, keepdims=True)
        dr = rstd * (dxh - m1 - xhat * m2)
        dr_ref[...] = dr
        dr16_ref[...] = dr.astype(_MXU)
        acc_ref[0:1, :] += jnp.sum(dy * xhat, axis=0, keepdims=True)
        acc_ref[1:2, :] += jnp.sum(dy, axis=0, keepdims=True)
        acc_ref[2:3, :] += jnp.sum(dr, axis=0, keepdims=True)
        acc_ref[3:4, :] += (0.5 / D) * jnp.sum(diff * diff)

    row = pl.BlockSpec((BLK, D), lambda i: (i, 0))
    vec = pl.BlockSpec((1, D), lambda i: (0, 0))
    return _pcall(body, name="ln_out_loss", grid=(NBLK,),
                  in_specs=[row, row, pl.BlockSpec((BLK, D), lambda i: (jnp.maximum(i - 1, 0), 0)), vec, vec],
                  out_specs=[row, row, pl.BlockSpec((8, D), lambda i: (0, 0))],
                  out_shape=[_sds((TP, D), F32), _sds((TP, D), _MXU), _sds((8, D), F32)],
                  operands=[h32, out, target, g, b], vmem=32, sem=("arbitrary",))


def _rnn_recompute(xr_ref, cw_ref, cb_ref, wra_ref, wri_ref, bra_ref, bri_ref, lam_ref):
    rows = lax.broadcasted_iota(jnp.int32, (TP, 1), 0)
    valid = (rows >= PAD).astype(F32)
    first = rows == PAD
    x = xr_ref[...] * valid
    cw = cw_ref[...]
    shifted = [x, pltpu.roll(x, 1, 0), pltpu.roll(x, 2, 0), pltpu.roll(x, 3, 0)]
    c = cb_ref[...] + cw[0:1, :] * shifted[0] + cw[1:2, :] * shifted[1] + cw[2:3, :] * shifted[2] + cw[3:4, :] * shifted[3]
    cm = c.astype(_MXU)
    gr = _sigmoid(jnp.dot(cm, wra_ref[...].astype(_MXU), preferred_element_type=F32) + bra_ref[...])
    gi = _sigmoid(jnp.dot(cm, wri_ref[...].astype(_MXU), preferred_element_type=F32) + bri_ref[...])
    lam = lam_ref[...]
    ls = jnp.minimum(lam, 0.0) - jnp.log(1.0 + jnp.exp(-jnp.abs(lam)))
    log_a = LRU_C * gr * ls
    a = jnp.exp(log_a)
    mult = jnp.where(first, 1.0, jnp.sqrt(1.0 - jnp.exp(2.0 * log_a)))
    return dict(valid=valid, first=first, shifted=shifted, c=c, cm=cm, gr=gr, gi=gi, ls=ls, a=a, mult=mult, lam=lam)


def _rnn_specs():
    col = pl.BlockSpec((TP, RB), lambda n: (0, n))
    vec = pl.BlockSpec((1, RB), lambda n: (0, n))
    return dict(col=col, vec=vec, cw=pl.BlockSpec((CONV_W, RB), lambda n: (0, n)),
                wblk=pl.BlockSpec((None, RB, RB), lambda n: (n, 0, 0)))


def _rnn_gates_fwd(z, conv_w, conv_b, w_ra, w_ri, b_ra, b_ri, lam):
    def body(xr_ref, cw_ref, cb_ref, wra_ref, wri_ref, bra_ref, bri_ref, lam_ref, a_ref, u_ref):
        r = _rnn_recompute(xr_ref, cw_ref, cb_ref, wra_ref, wri_ref, bra_ref, bri_ref, lam_ref)
        a_ref[...] = r["a"]
        u_ref[...] = r["mult"] * r["gi"] * r["c"] * r["valid"]

    s = _rnn_specs()
    return _pcall(body, name="rnn_gates_fwd", grid=(N_RB,),
                  in_specs=[s["col"], s["cw"], s["vec"], s["wblk"], s["wblk"], s["vec"], s["vec"], s["vec"]],
                  out_specs=[s["col"], s["col"]], out_shape=[_sds((TP, D), F32)] * 2,
                  operands=[z, conv_w, conv_b, w_ra, w_ri, b_ra, b_ri, lam], vmem=56, sem=("parallel",))


SCAN_ROWS = 272


SUB = 8


def _tile_scan(a, u, reverse):
    rows = lax.broadcasted_iota(jnp.int32, a.shape, 0)
    for d in (1, 2, 4):
        shift = SUB - d if reverse else d
        inside = (rows < SUB - d) if reverse else (rows >= d)
        u = u + a * jnp.where(inside, pltpu.roll(u, shift, 0), 0.0)
        a = a * jnp.where(inside, pltpu.roll(a, shift, 0), 1.0)
    return a, u


def _scan_fwd(a, u, *, hook):
    def body(a_ref, u_ref, h_ref, carry_ref):
        @pl.when(pl.program_id(1) == 0)
        def _():
            carry_ref[...] = jnp.zeros_like(carry_ref)

        def step(r, h):
            rows = pl.ds(pl.multiple_of(r * SUB, SUB), SUB)
            prod, part = _tile_scan(a_ref[rows, :], u_ref[rows, :], False)
            ht = part + prod * h
            h_ref[rows, :] = ht
            return ht[SUB - 1:SUB, :]

        carry_ref[...] = lax.fori_loop(0, SCAN_ROWS // SUB, step, carry_ref[...], unroll=2)

    blk = pl.BlockSpec((SCAN_ROWS, CW), lambda j, i: (i, j))
    return _pcall(body, name="scan_fwd", grid=(D // CW, TP // SCAN_ROWS), in_specs=[blk, blk], out_specs=[blk],
                  out_shape=[_sds((TP, D), F32)], operands=[a, u], scratch=[pltpu.VMEM((1, CW), F32)], vmem=32,
                  sem=("parallel", "arbitrary"), hook=hook)


def _scan_bwd(a, dh):
    nst = TP // SCAN_ROWS
    n_tiles = SCAN_ROWS // SUB

    def body(a_ref, d_ref, o_ref, lam_ref, anext_ref):
        @pl.when(pl.program_id(1) == 0)
        def _():
            lam_ref[...] = jnp.zeros_like(lam_ref)
            anext_ref[...] = jnp.zeros_like(anext_ref)

        def step(q, carry):
            lam_next, a_next = carry
            rows = pl.ds(pl.multiple_of((n_tiles - 1 - q) * SUB, SUB), SUB)
            at = a_ref[rows, :]
            last = lax.broadcasted_iota(jnp.int32, at.shape, 0) == SUB - 1
            b = jnp.where(last, a_next, pltpu.roll(at, SUB - 1, 0))
            prod, part = _tile_scan(b, d_ref[rows, :], True)
            lam = part + prod * lam_next
            o_ref[rows, :] = lam
            return lam[0:1, :], at[0:1, :]

        lam, an = lax.fori_loop(0, n_tiles, step, (lam_ref[...], anext_ref[...]), unroll=2)
        lam_ref[...] = lam
        anext_ref[...] = an

    blk = pl.BlockSpec((SCAN_ROWS, CW), lambda j, i: (nst - 1 - i, j))
    return _pcall(body, name="scan_bwd", grid=(D // CW, nst), in_specs=[blk, blk], out_specs=[blk],
                  out_shape=[_sds((TP, D), F32)], operands=[a, dh],
                  scratch=[pltpu.VMEM((1, CW), F32), pltpu.VMEM((1, CW), F32)], vmem=32,
                  sem=("parallel", "arbitrary"))[0]


def _rnn_gates_bwd(z, lam_s, hr, conv_w, conv_b, w_ra, w_ri, b_ra, b_ri, lam, dz, gsq):
    def body(xr_ref, ls_ref, hr_ref, cw_ref, cb_ref, wra_ref, wri_ref, bra_ref, bri_ref, lam_ref, dz_in, gsq_in,
             dx_ref, dw_ref, sums_ref):
        r = _rnn_recompute(xr_ref, cw_ref, cb_ref, wra_ref, wri_ref, bra_ref, bri_ref, lam_ref)
        valid, c, gr, gi, a, mult = r["valid"], r["c"], r["gr"], r["gi"], r["a"], r["mult"]
        du = ls_ref[...] * valid
        da = du * pltpu.roll(hr_ref[...], 1, 0)
        d_gi = du * mult * c
        dc = du * mult * gi
        dmult = du * gi * c
        dlog_a = da * a + jnp.where(r["first"], 0.0, -dmult * a * a / mult)
        d_gr = dlog_a * (LRU_C * r["ls"])
        dls = jnp.sum(dlog_a * (LRU_C * gr), axis=0, keepdims=True)
        dpre_r = d_gr * gr * (1.0 - gr)
        dpre_i = d_gi * gi * (1.0 - gi)
        pr = dpre_r.astype(_MXU)
        pi = dpre_i.astype(_MXU)
        dwra = lax.dot_general(r["cm"], pr, _TN, preferred_element_type=F32)
        dwri = lax.dot_general(r["cm"], pi, _TN, preferred_element_type=F32)
        for s in range(N_SH):
            dw_ref[s, 0:64, :] = dwra[64 * s:64 * (s + 1)]
            dw_ref[s, 64:128, :] = dwri[64 * s:64 * (s + 1)]
        dc = dc + lax.dot_general(pr, wra_ref[...].astype(_MXU), _NT, preferred_element_type=F32)
        dc = dc + lax.dot_general(pi, wri_ref[...].astype(_MXU), _NT, preferred_element_type=F32)
        cw = cw_ref[...]
        dx = cw[0:1, :] * dc
        for k in range(1, CONV_W):
            dx = dx + cw[k:k + 1, :] * pltpu.roll(dc, TP - k, 0)
        dx_ref[...] = (dx * valid).astype(_MXU)
        for k in range(CONV_W):
            sums_ref[k:k + 1, :] = jnp.sum(dc * r["shifted"][k], axis=0, keepdims=True)
        sums_ref[4:5, :] = jnp.sum(dc, axis=0, keepdims=True)
        sums_ref[5:6, :] = jnp.sum(dpre_r, axis=0, keepdims=True)
        sums_ref[6:7, :] = jnp.sum(dpre_i, axis=0, keepdims=True)
        sums_ref[7:8, :] = dls * _sigmoid(-r["lam"])

    s = _rnn_specs()
    return _pcall(
        body, name="rnn_gates_bwd", grid=(N_RB,),
        in_specs=[s["col"], s["col"], s["col"], s["cw"], s["vec"], s["wblk"], s["wblk"], s["vec"], s["vec"], s["vec"],
                  _ANY, _ANY],
        out_specs=[s["col"], pl.BlockSpec((N_SH, 128, RB), lambda n: (0, 3 * SQ_ROWS // 128, n)),
                   pl.BlockSpec((8, RB), lambda n: (0, n))],
        out_shape=[_sds((TP, D_IN), _MXU), _sds((N_SH, PACK_ROWS, D), F32), _sds((8, D), F32)],
        operands=[z, lam_s, hr, conv_w, conv_b, w_ra, w_ri, b_ra, b_ri, lam, dz, gsq], vmem=60, sem=("parallel",),
        aliases={10: 0, 11: 1})


def _rope_tables():
    half = HD // 2
    inv = ROPE_THETA ** (-jnp.arange(half, dtype=F32) / half)
    pos = (jnp.arange(TP) - PAD).astype(F32)
    ang = pos[:, None] * inv[None, :]
    return jnp.tile(jnp.cos(ang), (1, 4)), jnp.tile(jnp.sin(ang), (1, 4))


def _rope(x, cos_t, sin_t, sign):
    w = x.shape[1]
    lane = lax.broadcasted_iota(jnp.int32, x.shape, 1)
    first = (lane % HD) < (HD // 2)
    swapped = jnp.where(first, pltpu.roll(x, w - HD // 2, 1), pltpu.roll(x, HD // 2, 1))
    ct = jnp.tile(cos_t, (1, w // 128))
    st = jnp.tile(sin_t, (1, w // 128))
    return x * ct + swapped * jnp.where(first, -sign * st, sign * st)


def _rope_fwd(z, cos_t, sin_t):
    def body(q_ref, k_ref, v_ref, c_ref, s_ref, qo_ref, ko_ref, vo_ref):
        c = c_ref[...]
        s = s_ref[...]
        qo_ref[...] = _rope(q_ref[...], c, s, 1.0).astype(_MXU)
        ko_ref[...] = _rope(k_ref[...], c, s, 1.0).astype(_MXU)
        vo_ref[...] = v_ref[...].astype(_MXU)

    tab = pl.BlockSpec((BLK, 128), lambda i: (i, 0))
    kv = pl.BlockSpec((BLK, D_KV), lambda i: (i, 0))
    return _pcall(body, name="rope_fwd", grid=(NBLK,),
                  in_specs=[pl.BlockSpec((BLK, D), lambda i: (i, OFF_Q // D)),
                            pl.BlockSpec((BLK, D_KV), lambda i: (i, OFF_K // D_KV)),
                            pl.BlockSpec((BLK, D_KV), lambda i: (i, OFF_V // D_KV)), tab, tab],
                  out_specs=[pl.BlockSpec((BLK, D), lambda i: (i, 0)), kv, kv],
                  out_shape=[_sds((TP, D), _MXU), _sds((TP, D_KV), _MXU), _sds((TP, D_KV), _MXU)],
                  operands=[z, z, z, cos_t, sin_t], vmem=32, sem=("parallel",))


def _rope_bwd_k(dk, dv, cos_t, sin_t, dz):
    def body(dk_ref, dv_ref, c_ref, s_ref, dz_in, o_ref):
        o_ref[:, 0:D_KV] = _rope(dk_ref[...], c_ref[...], s_ref[...], -1.0).astype(_MXU)
        o_ref[:, D_KV:2 * D_KV] = dv_ref[...].astype(_MXU)

    tab = pl.BlockSpec((BLK, 128), lambda i: (i, 0))
    kv = pl.BlockSpec((BLK, D_KV), lambda i: (i, 0))
    return _pcall(body, name="rope_bwd_k", grid=(NBLK,), in_specs=[kv, kv, tab, tab, _ANY],
                  out_specs=[pl.BlockSpec((BLK, 2 * D_KV), lambda i: (i, OFF_K // (2 * D_KV)))],
                  out_shape=[_sds((TP, D_IN), _MXU)], operands=[dk, dv, cos_t, sin_t, dz], vmem=32, sem=("parallel",),
                  aliases={4: 0})[0]


def _attn_mask(i):
    ql = lax.broadcasted_iota(jnp.int32, (BLK, 3 * BLK), 0)
    kk = lax.broadcasted_iota(jnp.int32, (BLK, 3 * BLK), 1)
    kl = kk % BLK
    part = kk // BLK
    meta = (part == 0) & (kl >= PAD) & ((i >= 1) | (kl <= ql))
    prev = (part == 1) & (i >= 2) & (kl > ql)
    cur = (part == 2) & (i >= 1) & (kl <= ql)
    return meta | prev | cur


def _kv_specs():
    return [pl.BlockSpec((BLK, D_KV), lambda i: (0, 0)),
            pl.BlockSpec((BLK, D_KV), lambda i: (jnp.maximum(i - 1, 0), 0)),
            pl.BlockSpec((BLK, D_KV), lambda i: (i, 0))]


def _pair_heads(ref, gp):
    def two(j):
        h0 = 2 * GRP * gp + j
        return jnp.concatenate([ref[:, HD * h0:HD * (h0 + 1)], ref[:, HD * (h0 + GRP):HD * (h0 + GRP + 1)]], axis=1)

    return jnp.concatenate([two(j) for j in range(GRP)], axis=0)


def _pair_kv(refs, gp):
    both = jnp.concatenate([r[:, 2 * HD * gp:2 * HD * (gp + 1)] for r in refs], axis=0)
    low_lanes = lax.broadcasted_iota(jnp.int32, both.shape, 1) < HD
    zero = jnp.zeros_like(both)
    return jnp.concatenate([jnp.where(low_lanes, both, zero), jnp.where(low_lanes, zero, both)], axis=0)


def _attn_fwd(q, k, v, sinks, *, hook):
    nk = 3 * BLK

    def body(q_ref, k0_ref, kp_ref, kc_ref, v0_ref, vp_ref, vc_ref, sink_ref, o_ref, lse_ref, s_scr, p_scr):
        mask = _attn_mask(pl.program_id(0))
        for gp in range(N_KV // 2):
            s_scr[...] = lax.dot_general(_pair_heads(q_ref, gp), _pair_kv((k0_ref, kp_ref, kc_ref), gp), _NT,
                                         preferred_element_type=F32)
            for a in range(2):
                for j in range(GRP):
                    h = GRP * (2 * gp + a) + j
                    rows, cols = slice(BLK * j, BLK * (j + 1)), slice(nk * a, nk * (a + 1))
                    sink = sink_ref[h]
                    s = jnp.where(mask, s_scr[rows, cols] * (HD ** -0.5), NEG_INF)
                    mx = jnp.maximum(jnp.max(s, -1, keepdims=True), sink)
                    p = jnp.exp(s - mx)
                    den = jnp.sum(p, -1, keepdims=True) + jnp.exp(sink - mx)
                    p_scr[rows, cols] = (p * (1.0 / den)).astype(_MXU)
                    lse_ref[:, h:h + 1] = mx + jnp.log(den)
            o2 = jnp.dot(p_scr[...], _pair_kv((v0_ref, vp_ref, vc_ref), gp), preferred_element_type=F32)
            for a in range(2):
                for j in range(GRP):
                    h = GRP * (2 * gp + a) + j
                    o_ref[:, HD * h:HD * (h + 1)] = o2[BLK * j:BLK * (j + 1), HD * a:HD * (a + 1)]

    return _pcall(body, name="attn_fwd", grid=(NBLK,),
                  in_specs=[pl.BlockSpec((BLK, D), lambda i: (i, 0))] + _kv_specs() + _kv_specs()
                           + [pl.BlockSpec(memory_space=pltpu.SMEM)],
                  out_specs=[pl.BlockSpec((BLK, D), lambda i: (i, 0)), pl.BlockSpec((BLK, N_Q), lambda i: (i, 0))],
                  out_shape=[_sds((TP, D), F32), _sds((TP, N_Q), F32)], operands=[q, k, k, k, v, v, v, sinks],
                  scratch=[pltpu.VMEM((GRP * BLK, 2 * nk), F32), pltpu.VMEM((GRP * BLK, 2 * nk), _MXU)],
                  vmem=40, sem=("parallel",), hook=hook)


def _attn_bwd(q, k, v, sinks, do, o, lse, cos_t, sin_t, dz, *, hook):
    def body(q_ref, k0_ref, kp_ref, kc_ref, v0_ref, vp_ref, vc_ref, sink_ref, do_ref, o_ref, lse_ref, c_ref, s_ref, dz_in,
             dq_ref, dk_ref, dv_ref, dsink_ref, dqrot_ref, s_scr, dp_scr, p_scr, ds_scr):
        i = pl.program_id(0)

        @pl.when(i == 0)
        def _():
            dk_ref[...] = jnp.zeros_like(dk_ref)
            dv_ref[...] = jnp.zeros_like(dv_ref)
            dsink_ref[...] = jnp.zeros_like(dsink_ref)

        mask = _attn_mask(i)
        row_starts = (0, pl.multiple_of(jnp.maximum(i - 1, 0) * BLK, BLK), pl.multiple_of(i * BLK, BLK))
        scale = HD ** -0.5
        nk = 3 * BLK
        for gp in range(N_KV // 2):
            q2 = _pair_heads(q_ref, gp)
            dom2 = _pair_heads(do_ref, gp).astype(_MXU)
            k2 = _pair_kv((k0_ref, kp_ref, kc_ref), gp)
            s_scr[...] = lax.dot_general(q2, k2, _NT, preferred_element_type=F32)
            dp_scr[...] = lax.dot_general(dom2, _pair_kv((v0_ref, vp_ref, vc_ref), gp), _NT, preferred_element_type=F32)
            for a in range(2):
                for j in range(GRP):
                    h = GRP * (2 * gp + a) + j
                    hs = slice(HD * h, HD * (h + 1))
                    rows = slice(BLK * j, BLK * (j + 1))
                    cols = slice(nk * a, nk * (a + 1))
                    lse_h = lse_ref[:, h:h + 1]
                    delta = jnp.sum(do_ref[:, hs] * o_ref[:, hs], axis=-1, keepdims=True)
                    dsink_ref[0:1, h:h + 1] += jnp.sum(-jnp.exp(sink_ref[h] - lse_h) * delta, axis=0, keepdims=True)
                    p = jnp.exp(jnp.where(mask, s_scr[rows, cols] * scale, NEG_INF) - lse_h)
                    p_scr[rows, cols] = p.astype(_MXU)
                    ds_scr[rows, cols] = (p * (dp_scr[rows, cols] - delta) * scale).astype(_MXU)
            ds = ds_scr[...]
            dq2 = jnp.dot(ds, k2, preferred_element_type=F32)
            dk2 = lax.dot_general(ds, q2, _TN, preferred_element_type=F32)
            dv2 = lax.dot_general(p_scr[...], dom2, _TN, preferred_element_type=F32)
            for a in range(2):
                g = 2 * gp + a
                gs = slice(HD * g, HD * (g + 1))
                for part in range(3):
                    rows = pl.ds(row_starts[part], BLK)
                    blk = slice(nk * a + BLK * part, nk * a + BLK * (part + 1))
                    dk_ref[rows, gs] += dk2[blk, HD * a:HD * (a + 1)]
                    dv_ref[rows, gs] += dv2[blk, HD * a:HD * (a + 1)]
                for j in range(GRP):
                    h = GRP * g + j
                    dqrot_ref[:, HD * h:HD * (h + 1)] = dq2[BLK * j:BLK * (j + 1), HD * a:HD * (a + 1)]
        dq_ref[...] = _rope(dqrot_ref[...], c_ref[...], s_ref[...], -1.0).astype(_MXU)

    row = pl.BlockSpec((BLK, D), lambda i: (i, 0))
    tab = pl.BlockSpec((BLK, 128), lambda i: (i, 0))
    full_kv = pl.BlockSpec((TP, D_KV), lambda i: (0, 0))
    return _pcall(
        body, name="attn_bwd", grid=(NBLK,),
        in_specs=[row] + _kv_specs() + _kv_specs() + [pl.BlockSpec(memory_space=pltpu.SMEM), row, row,
                  pl.BlockSpec((BLK, N_Q), lambda i: (i, 0)), tab, tab, _ANY],
        out_specs=[pl.BlockSpec((BLK, D), lambda i: (i, OFF_Q // D)), full_kv, full_kv,
                   pl.BlockSpec((8, 128), lambda i: (0, 0))],
        out_shape=[_sds((TP, D_IN), _MXU), _sds((TP, D_KV), F32), _sds((TP, D_KV), F32), _sds((8, 128), F32)],
        operands=[q, k, k, k, v, v, v, sinks, do, o, lse, cos_t, sin_t, dz],
        scratch=[pltpu.VMEM((BLK, D), F32), pltpu.VMEM((GRP * BLK, 6 * BLK), F32), pltpu.VMEM((GRP * BLK, 6 * BLK), F32),
                 pltpu.VMEM((GRP * BLK, 6 * BLK), _MXU), pltpu.VMEM((GRP * BLK, 6 * BLK), _MXU)],
        vmem=48, sem=("arbitrary",), aliases={13: 0}, hook=hook)


def _cast_into_slot(w32, pos, *, name, tr):
    r, cc = w32.shape

    def body(pos_ref, w_ref, o_ref):
        o_ref[...] = w_ref[...].astype(BF16)

    return _pcall(body, name=name, grid=(r // tr,), in_specs=[pl.BlockSpec((tr, cc), lambda i, p: (i, 0))],
                  out_specs=[pl.BlockSpec((None, tr, cc), lambda i, p: (p[1], i, 0))],
                  out_shape=[_sds((N_SH, r, cc), BF16)], operands=[w32], vmem=32, sem=("parallel",), prefetch=(pos,))[0]


def _pair_add(g, got, pos, *, name, tr, g_has_both_halves, hook=None):
    n, h, cc = got.shape
    nt = h // tr

    def body(pos_ref, g_ref, r_ref, own_ref, s16_ref):
        s = g_ref[...] + r_ref[...].astype(F32)
        s16_ref[...] = s.astype(BF16)

        @pl.when(pl.program_id(1) == pos_ref[1])
        def _():
            own_ref[...] = s

    g_index = (lambda i, s, p: (s, p[0] * nt + i, 0)) if g_has_both_halves else (lambda i, s, p: (s, i, 0))
    return _pcall(body, name=name, grid=(nt, n),
                  in_specs=[pl.BlockSpec((None, tr, cc), g_index), pl.BlockSpec((None, tr, cc), lambda i, s, p: (s, i, 0))],
                  out_specs=[pl.BlockSpec((tr, cc), lambda i, s, p: (i, 0)),
                             pl.BlockSpec((None, tr, cc), lambda i, s, p: (s, i, 0))],
                  out_shape=[_sds((h, cc), F32), _sds((n, h, cc), BF16)], operands=[g, got], vmem=40,
                  sem=("parallel", "arbitrary"), prefetch=(pos,), hook=hook)


def _relay_add(s16, raw, pos, *, name, tr):
    _, q, cc = raw.shape
    nt = q // tr

    def body(pos_ref, s_ref, r_ref, o_ref):
        o_ref[...] = (s_ref[...].astype(F32) + r_ref[...].astype(F32)).astype(BF16)

    blk = pl.BlockSpec((None, tr, cc), lambda k, i, p: (k, i, 0))
    return _pcall(body, name=name, grid=(2, nt),
                  in_specs=[pl.BlockSpec((None, tr, cc), lambda k, i, p: (p[3 + k], (1 - k) * nt + i, 0)), blk],
                  out_specs=[blk], out_shape=[_sds((2, q, cc), BF16)], operands=[s16, raw], vmem=40,
                  sem=("parallel", "parallel"), prefetch=(pos,))[0]


def _sum_chips(own, got, pos, *, name, tr):
    h, cc = own.shape
    n_got = got.shape[0]
    nt = h // tr

    def body(pos_ref, o_ref, r_ref, out_ref):
        acc = o_ref[...]
        for k in range(n_got):
            acc = acc + r_ref[k].astype(F32)
        out_ref[...] = acc

    return _pcall(body, name=name, grid=(nt,),
                  in_specs=[pl.BlockSpec((tr, cc), lambda i, p: (i, 0)),
                            pl.BlockSpec((n_got, tr, cc), lambda i, p: (0, i, 0))],
                  out_specs=[pl.BlockSpec((tr, cc), lambda i, p: (p[0] * nt + i, 0))],
                  out_shape=[_sds((2 * h, cc), F32)], operands=[own, got], vmem=40, sem=("parallel",), prefetch=(pos,))[0]


def _adamw(w, g, m, v, *, name, tr, g_row0=0):
    r, cc = w.shape
    g_blk0 = g_row0 // tr

    def body(w_ref, g_ref, m_ref, v_ref, go_ref, d_ref, mo_ref, vo_ref):
        gg = g_ref[...]
        go_ref[...] = gg
        m_new = ADAM_B1 * m_ref[...] + (1.0 - ADAM_B1) * gg
        v_new = ADAM_B2 * v_ref[...] + (1.0 - ADAM_B2) * (gg * gg)
        m_hat = m_new / (1.0 - ADAM_B1 ** ADAM_STEP)
        v_hat = v_new / (1.0 - ADAM_B2 ** ADAM_STEP)
        d_ref[...] = -ADAM_LR * (m_hat / (jnp.sqrt(v_hat) + ADAM_EPS) + ADAM_WD * w_ref[...])
        mo_ref[...] = m_new
        vo_ref[...] = v_new

    blk = pl.BlockSpec((tr, cc), lambda i: (i, 0))
    gblk = pl.BlockSpec((tr, cc), lambda i: (g_blk0 + i, 0))
    return _pcall(body, name=name, grid=(r // tr,), in_specs=[blk, gblk, blk, blk], out_specs=[blk] * 4,
                  out_shape=[_sds((r, cc), F32)] * 4, operands=[w, g, m, v], vmem=48, sem=("parallel",))


_SMALL_ROWS = 40
_B_IN_ROWS = 7


def _row_pad(v, rows):
    flat = v.reshape(-1)
    return jnp.pad(flat, (0, rows * D - flat.shape[0])).reshape(rows, D)


def _pack_ra(w):
    return w.transpose(1, 0, 2).reshape(64, D)


def _unpack_ra(p, like):
    return p.reshape(64, N_RB, RB).transpose(1, 0, 2).reshape(like.shape)


def _gate_full(g4):
    return g4.reshape(N_SH, 64, N_RB, RB).transpose(2, 0, 1, 3).reshape(N_RB, RB, RB)


def kernel(x, meta_tokens, ln_emb_g, ln_emb_b, w_in, b_in, conv_w, conv_b, w_ra, b_ra, w_ri, b_ri, lru_lambda, sinks, w_rnn_out, w_attn_out, w_o, b_o, ln_g, ln_b, loss_target, m_meta_tokens, m_ln_emb_g, m_ln_emb_b, m_w_in, m_b_in, m_conv_w, m_conv_b, m_w_ra, m_b_ra, m_w_ri, m_b_ri, m_lru_lambda, m_sinks, m_w_rnn_out, m_w_attn_out, m_w_o, m_b_o, m_ln_g, m_ln_b, v_meta_tokens, v_ln_emb_g, v_ln_emb_b, v_w_in, v_b_in, v_conv_w, v_conv_b, v_w_ra, v_b_ra, v_w_ri, v_b_ri, v_lru_lambda, v_sinks, v_w_rnn_out, v_w_attn_out, v_w_o, v_b_o, v_ln_g, v_ln_b):
    xi, yi, ci = _my_pos()
    shard = 2 * xi + yi
    pos = jnp.stack([ci, shard, 1 - ci, shard ^ 2, shard ^ 1]).astype(jnp.int32)
    cos_t, sin_t = _rope_tables()
    zero_bias = jnp.zeros((1, D), F32)
    ln_emb_g2, ln_emb_b2 = ln_emb_g[None], ln_emb_b[None]

    small = jnp.concatenate([conv_w[0], meta_tokens, jnp.zeros((4, 512), F32)], axis=0)
    small4 = _gather_small(small)
    conv_w_full = small4[:, 0:4].transpose(1, 0, 2).reshape(CONV_W, D)
    meta_full = small4[:, 4:20].transpose(1, 0, 2).reshape(N_META, D)
    w_own = _cast_into_slot(w_in[0], pos, name="cast_w_in", tr=256)
    wa_own = _cast_into_slot(jnp.concatenate([w_attn_out[0], _pack_ra(w_ra[0]), _pack_ra(w_ri[0])], axis=0),
                             pos, name="cast_w_a", tr=320)
    wb_own = _cast_into_slot(w_rnn_out[0], pos, name="cast_w_b", tr=256)
    wc_own = _cast_into_slot(w_o[0], pos, name="cast_w_c", tr=256)

    h32, h16, h16_t = _ln_emb_fwd(x[0], meta_full, ln_emb_g2, ln_emb_b2)
    order = jnp.stack([shard, shard ^ 2, shard ^ 1, shard ^ 3]).astype(jnp.int32)
    z, w_in4 = _mm_z_gather(h16, w_own, b_in, order)
    q, k, v = _rope_fwd(z, cos_t, sin_t)
    o, lse, wa4 = _attn_fwd(q, k, v, sinks[0], hook=_hook_gather(wa_own, 0.6))
    w_ra_full = _gate_full(wa4[:, SQ_ROWS:SQ_ROWS + 64])
    w_ri_full = _gate_full(wa4[:, SQ_ROWS + 64:SQ_ROWS + 128])
    a_dec, u_in = _rnn_gates_fwd(z, conv_w_full, conv_b, w_ra_full, w_ri_full, b_ra, b_ri, lru_lambda)
    hr, wb4 = _scan_fwd(a_dec, u_in, hook=_hook_gather(wb_own, 0.6))
    sq_w = {0: (wb4, 0), 1: (wa4, 0)}

    def sq_nn(a, kk, bias, name, out_cols, out_index, carried=None):
        wp, blk = sq_w[kk]
        return _mm_nn(a, wp, bias, name=name, grid=(2, D // CW), tm=HALF_TP, tn=CW, k=D, a_index=lambda i, j: (i, 0),
                      w_block=(N_SH, SQ_ROWS, CW), w_index=lambda i, j: (0, blk, j), out_cols=out_cols,
                      out_index=out_index, carried=carried)[0]

    def sq_nt(a, a_blk, kk, name, hook=None):
        wp, blk = sq_w[kk]
        return _mm_nt(a, wp, name=name, grid=(2, N_SH, 1), tm=HALF_TP, tn=SQ_ROWS, tk=D,
                      a_index=lambda i, j, q: (i, a_blk), w_block=(None, SQ_ROWS, D),
                      w_index=lambda i, j, q: (j, blk, 0), out_cols=D, hook=hook)

    ya_in, ya_in_t = _mul_silu_fwd(hr, z, OFF_GR, name="gate_a_fwd")
    y2 = sq_nn(ya_in, 0, zero_bias, "mm_ya", 2 * D, lambda i, j: (i, j))
    yb_in, yb_in_t = _mul_silu_fwd(o, z, OFF_GA, name="gate_b_fwd")
    y2 = sq_nn(yb_in, 1, zero_bias, "mm_yb", 2 * D, lambda i, j: (i, D // CW + j), carried=y2)
    mixed, mixed_t, wc4 = _merge_fwd(y2, z, hook=_hook_gather(wc_own, 0.6))
    sq_w[2] = (wc4, 0)
    out = sq_nn(mixed, 2, b_o, "mm_out", D, lambda i, j: (i, j))
    dr, dr16, sums_o = _ln_out_loss(h32, out, loss_target[0], ln_g, ln_b)

    def sq_tn(at, b, b_blk, kk, name, carried=None):
        return _mm_tn(at, b, name=name, grid=(N_SH, 1), tm=SQ_ROWS, tn=D, a_index=lambda i, j: (i, 0),
                      b_index=lambda i, j: (0, b_blk), out_shape=(N_SH, PACK_ROWS, D),
                      out_block=(None, SQ_ROWS, D), out_index=lambda i, j: (i, kk, 0), carried=carried)[0]

    gsq = sq_tn(mixed_t, dr16, 0, 2, "mm_dwo")
    dmix = sq_nt(dr16, 0, 2, "mm_dmix")[0]
    dy2, dz = _merge_bwd(dmix, y2, z)
    gsq = sq_tn(ya_in_t, dy2, 0, 0, "mm_dwrnn", carried=gsq)
    gsq = sq_tn(yb_in_t, dy2, 1, 1, "mm_dwattn", carried=gsq)
    dya_in = sq_nt(dy2, 0, 0, "mm_dyain")[0]
    dhr, dz = _mul_silu_bwd(dya_in, hr, z, OFF_GR, dz, name="gate_a_bwd")
    lam_s = _scan_bwd(a_dec, dhr)
    dz, gsq, sums_r = _rnn_gates_bwd(z, lam_s, hr, conv_w_full, conv_b, w_ra_full, w_ri_full, b_ra, b_ri, lru_lambda, dz, gsq)
    dyb_in, gsq, got_sq = sq_nt(dy2, 1, 1, "mm_dybin", hook=_hook_pair(gsq, True, (0, 1)))
    do, dz, gsq, got_sq = _mul_silu_bwd(dyb_in, o, z, OFF_GA, dz, name="gate_b_bwd",
                                        hook=_hook_pair(gsq, True, (2, 3), land=got_sq))
    own_sq, s16_sq = _pair_add(gsq, got_sq, pos, name="red_w_sq_add", tr=208, g_has_both_halves=True)
    dz, dk_rot, dv32, dsink, s16_sq, fin_sq, raw_sq = _attn_bwd(q, k, v, sinks[0], do, o, lse, cos_t, sin_t, dz,
                                                                hook=_hook_scatter_direct(s16_sq))
    dz = _rope_bwd_k(dk_rot, dv32, cos_t, sin_t, dz)
    comb_sq = _relay_add(s16_sq, raw_sq, pos, name="red_w_sq_relay", tr=208)

    def dwin(half_idx, name, hook, **kw):
        return _mm_tn(h16_t, dz, name=name, grid=(1, N_SH * PER_IN), tm=D // 2, tn=TN_IN,
                      a_index=lambda i, j, p: (p[half_idx], 0), b_index=lambda i, j, p: (0, j),
                      out_shape=(N_SH, D // 2, W_IN_COLS), out_block=(None, D // 2, TN_IN),
                      out_index=lambda i, j, p: (j // PER_IN, 0, j % PER_IN), prefetch=(pos,), hook=hook, **kw)

    gin_sib, g_b_in, comb_sq, fin_sq = dwin(2, "mm_dwin_sib", _hook_scatter_relay(comb_sq, fin_sq),
                                            out_dtype=BF16, colsum=True)
    red_sq = _sum_chips(own_sq, fin_sq, pos, name="red_w_sq_sum", tr=208)
    gin_own, gin_sib, got_in = dwin(0, "mm_dwin_own", _hook_pair(gin_sib, False))
    own_in, s16_in, g_sq = _pair_add(gin_own, got_in, pos, name="red_w_in_add", tr=128, g_has_both_halves=False,
                                     hook=_hook_halves(red_sq))
    dhz, s16_in, fin_in, raw_in = _mm_nt(dz, w_in4, name="mm_dhz", grid=(2, 2, N_SH), tm=HALF_TP, tn=D // 2,
                                         tk=W_IN_COLS, a_index=lambda i, j, q: (i, q),
                                         w_block=(None, D // 2, W_IN_COLS), w_index=lambda i, j, q: (q, j, 0), out_cols=D,
                                         hook=_hook_scatter_direct(s16_in))
    comb_in = _relay_add(s16_in, raw_in, pos, name="red_w_in_relay", tr=128)
    g_x, g_meta_local, sums_e, comb_in, fin_in = _ln_emb_bwd(x[0], meta_full, ln_emb_g2, dr, dhz,
                                                             hook=_hook_scatter_relay(comb_in, fin_in))
    red_in = _sum_chips(own_in, fin_in, pos, name="red_w_in_sum", tr=128)

    spack = jnp.concatenate([
        sums_e[0:1], sums_e[1:2], _row_pad(g_b_in, _B_IN_ROWS), sums_r[0:4], sums_r[4:5], sums_r[5:6], sums_r[6:7],
        sums_r[7:8], _row_pad(dsink[0:1, 0:N_Q], 1), sums_o[2:3], sums_o[0:1], sums_o[1:2], g_meta_local, sums_o[3:4],
        jnp.zeros((_SMALL_ROWS - 38, D), F32)], axis=0)
    sred, g_in = _allreduce_small(spack, red_in)

    big = {"w_in": [t.reshape(w_in.shape) for t in
                    _adamw(w_in[0], g_in, m_w_in[0], v_w_in[0], name="adamw_w_in", tr=128)]}
    for kk, (n, w_, m_, v_) in enumerate([("w_rnn_out", w_rnn_out, m_w_rnn_out, v_w_rnn_out),
                                          ("w_attn_out", w_attn_out, m_w_attn_out, v_w_attn_out), ("w_o", w_o, m_w_o, v_w_o)]):
        big[n] = [t.reshape(w_.shape) for t in
                  _adamw(w_[0], g_sq, m_[0], v_[0], name="adamw_" + n, tr=256, g_row0=SQ_ROWS * kk)]
    for kk, (n, w_, m_, v_) in enumerate([("w_ra", w_ra, m_w_ra, v_w_ra), ("w_ri", w_ri, m_w_ri, v_w_ri)]):
        big[n] = [_unpack_ra(t, w_) for t in
                  _adamw(_pack_ra(w_[0]), g_sq, _pack_ra(m_[0]), _pack_ra(v_[0]), name="adamw_" + n, tr=64,
                         g_row0=3 * SQ_ROWS + 64 * kk)]

    loss = sred[37, 0]
    col0 = shard * 512
    g_conv_w = lax.dynamic_slice(sred[9:13], (0, col0), (CONV_W, 512))
    g_meta = lax.dynamic_slice(sred[21:37], (0, col0), (N_META, 512))
    small_g = {"ln_emb_g": sred[0:1], "ln_emb_b": sred[1:2], "b_in": sred[2:9], "conv_w": g_conv_w.reshape(1, D),
               "conv_b": sred[13:14], "b_ra": sred[14:15], "b_ri": sred[15:16], "lru_lambda": sred[16:17],
               "sinks": sred[17:18], "b_o": sred[18:19], "ln_g": sred[19:20], "ln_b": sred[20:21],
               "meta_tokens": g_meta.reshape(4, D)}
    small_names = list(small_g)

    def small_pack(vals):
        rows = []
        for n in small_names:
            a = vals[n]
            if n == "b_in":
                a = _row_pad(a, _B_IN_ROWS)
            elif n == "sinks":
                a = _row_pad(a, 1)
            else:
                a = a.reshape(-1, D)
            rows.append(a)
        return jnp.concatenate(rows + [jnp.zeros((24 - 22, D), F32)], axis=0)

    w_small = dict(ln_emb_g=ln_emb_g, ln_emb_b=ln_emb_b, b_in=b_in, conv_w=conv_w, conv_b=conv_b, b_ra=b_ra, b_ri=b_ri,
                   lru_lambda=lru_lambda, sinks=sinks, b_o=b_o, ln_g=ln_g, ln_b=ln_b, meta_tokens=meta_tokens)
    m_small = dict(ln_emb_g=m_ln_emb_g, ln_emb_b=m_ln_emb_b, b_in=m_b_in, conv_w=m_conv_w, conv_b=m_conv_b, b_ra=m_b_ra,
                   b_ri=m_b_ri, lru_lambda=m_lru_lambda, sinks=m_sinks, b_o=m_b_o, ln_g=m_ln_g, ln_b=m_ln_b,
                   meta_tokens=m_meta_tokens)
    v_small = dict(ln_emb_g=v_ln_emb_g, ln_emb_b=v_ln_emb_b, b_in=v_b_in, conv_w=v_conv_w, conv_b=v_conv_b, b_ra=v_b_ra,
                   b_ri=v_b_ri, lru_lambda=v_lru_lambda, sinks=v_sinks, b_o=v_b_o, ln_g=v_ln_g, ln_b=v_ln_b,
                   meta_tokens=v_meta_tokens)
    g_small_pack = jnp.concatenate([small_g[n] for n in small_names] + [jnp.zeros((2, D), F32)], axis=0)
    small_res = _adamw(small_pack(w_small), g_small_pack, small_pack(m_small), small_pack(v_small),
                       name="adamw_small", tr=24)

    small_rows = {}
    r0 = 0
    for n in small_names:
        nrows = small_g[n].shape[0]
        small_rows[n] = (r0, nrows)
        r0 += nrows

    def small_out(packed, n, like):
        a, nrows = small_rows[n]
        flat = packed[a:a + nrows].reshape(-1)
        return flat[:like.size].reshape(like.shape)

    weights = dict(meta_tokens=meta_tokens, ln_emb_g=ln_emb_g, ln_emb_b=ln_emb_b, w_in=w_in, b_in=b_in, conv_w=conv_w,
                   conv_b=conv_b, w_ra=w_ra, b_ra=b_ra, w_ri=w_ri, b_ri=b_ri, lru_lambda=lru_lambda, sinks=sinks,
                   w_rnn_out=w_rnn_out, w_attn_out=w_attn_out, w_o=w_o, b_o=b_o, ln_g=ln_g, ln_b=ln_b)

    def outputs(which):
        return [big[n][which] if n in big else small_out(small_res[which], n, like) for n, like in weights.items()]

    return (loss, g_x[None], *outputs(0), *outputs(1), *outputs(2), *outputs(3))
```

```python
import jax
import jax.numpy as jnp
from jax import lax
from jax.experimental import pallas as pl
from jax.experimental.pallas import tpu as pltpu

F32 = jnp.float32
BF16 = jnp.bfloat16
_MXU = jnp.bfloat16

D = 2048
SEQ = 2048
N_META = 16
BLK = 128
PAD = BLK - N_META
TP = PAD + N_META + SEQ
NBLK = TP // BLK
HALF_TP = TP // 2
N_RB = 8
RB = 256
CONV_W = 4
LRU_C = 8.0
HD = 64
N_Q = 32
N_KV = 4
GRP = 8
D_KV = 256
NEG_INF = -1e30
LN_EPS = 1e-5
ALPHA = 2.0 ** 0.25
ROPE_THETA = 10000.0
OFF_GR, OFF_Q, OFF_K, OFF_V, OFF_GA, OFF_G = 2048, 4096, 6144, 6400, 6656, 8704
D_IN = 12800
N_SH = 4
W_IN_COLS = D_IN // N_SH
TN_IN = 640
PER_IN = W_IN_COLS // TN_IN
CW = 512
RT = TP // 4
SQ_ROWS = 512
PACK_ROWS = 3 * SQ_ROWS + 128

ADAM_LR = 0.001
ADAM_B1 = 0.9
ADAM_B2 = 0.999
ADAM_EPS = 1e-08
ADAM_WD = 0.01
ADAM_STEP = 10

MESH = pl.DeviceIdType.MESH
_MIB = 1024 * 1024
_ANY = pl.BlockSpec(memory_space=pl.ANY)
_NT = (((1,), (1,)), ((), ()))
_TN = (((0,), (0,)), ((), ()))


def _sds(shape, dtype):
    return jax.ShapeDtypeStruct(shape, dtype)


def _sigmoid(x):
    return 1.0 / (1.0 + jnp.exp(-x))


def _my_pos():
    return lax.axis_index("x"), lax.axis_index("y"), lax.axis_index("c")


def _other_chips(x, y):
    return [(1 - x, y), (x, 1 - y), (1 - x, 1 - y)]


def _remote(src, dst, send_sems, recv_sems, k, dev):
    return pltpu.make_async_remote_copy(src_ref=src, dst_ref=dst, send_sem=send_sems.at[k], recv_sem=recv_sems.at[k],
                                        device_id=dev, device_id_type=MESH)


class _Hook:
    def __init__(self, carried, landing, n_sems, start, finish, mid=None, mid_frac=0.5):
        self.carried, self.landing, self.n_sems, self.start, self.finish = list(carried), list(landing), n_sems, start, finish
        self.mid, self.mid_frac = mid, mid_frac


def _hook_gather(buf, mid_frac):
    half = buf.shape[1] // 2
    quarter = half // 2

    def geom(o, ss, rs):
        x, y, c = _my_pos()
        xn, yn, dg = _other_chips(x, y)
        slot = lambda p: 2 * p[0] + p[1]
        mine_rows = pl.ds(pl.multiple_of(c * half, 16), half)
        sib_rows = pl.ds(pl.multiple_of((1 - c) * half, 16), half)
        q_rows = lambda r: pl.ds(pl.multiple_of(c * half + r * quarter, 16), quarter)

        def cp(k, s, rows, dev):
            part = o.at[s, rows]
            return _remote(part, part, ss, rs, k, dev)

        return dict(
            direct=lambda k, s: cp(k, s, mine_rows, ((xn, yn)[k][0], (xn, yn)[k][1], c)),
            relay=lambda r, s: cp(2 + r, s, q_rows(r), ((yn, xn)[r][0], (yn, xn)[r][1], c)),
            sibling=lambda k, s, mine: cp(4 + k, s, mine_rows if mine else sib_rows, (x, y, 1 - c)),
            me=slot((x, y)), slots=(slot(xn), slot(yn), slot(dg)))

    def start(car, land, ss, rs):
        g = geom(car[0], ss, rs)
        g["direct"](0, g["me"]).start()
        g["direct"](1, g["me"]).start()

    def mid(car, land, ss, rs):
        g = geom(car[0], ss, rs)
        for k in range(2):
            g["direct"](k, g["slots"][k]).wait_recv()
            g["relay"](k, g["slots"][k]).start()
            g["sibling"](k, g["slots"][k], True).start()

    def finish(car, land, ss, rs):
        g = geom(car[0], ss, rs)
        dslot = g["slots"][2]
        g["relay"](0, dslot).wait_recv()
        g["relay"](1, dslot).wait_recv()
        g["sibling"](2, dslot, True).start()
        for k in range(3):
            g["sibling"](k, g["slots"][k], False).wait_recv()
        for k in range(2):
            g["direct"](k, g["me"]).wait_send()
            g["relay"](k, g["slots"][k]).wait_send()
        for k in range(3):
            g["sibling"](k, g["slots"][k], True).wait_send()

    return _Hook([buf], [], 7, start, finish, mid=mid, mid_frac=mid_frac)


def _hook_pair(g, half_rows, slots=(0, 1, 2, 3), land=None):
    n, r, cc = g.shape
    h = r // 2 if half_rows else r

    def plan(car, landing, ss, rs):
        x, y, c = _my_pos()
        dst = landing[0] if land is None else car[1]
        cps = []
        for k, s in enumerate(slots):
            src = car[0].at[s, pl.ds(pl.multiple_of((1 - c) * h, 8), h)] if half_rows else car[0].at[s]
            cps.append(_remote(src, dst.at[s], ss, rs, k, (x, y, 1 - c)))
        return cps

    def start(car, land, ss, rs):
        for cp in plan(car, land, ss, rs):
            cp.start()

    def finish(car, land, ss, rs):
        cps = plan(car, land, ss, rs)
        for cp in cps:
            cp.wait_recv()
        for cp in cps:
            cp.wait_send()

    if land is None:
        return _Hook([g], [_sds((n, h, cc), g.dtype)], len(slots), start, finish)
    return _Hook([g, land], [], len(slots), start, finish)


def _hook_scatter_direct(s16):
    _, h, cc = s16.shape
    q = h // 2

    def plan(car, land, ss, rs):
        x, y, c = _my_pos()
        xn, yn, dg = _other_chips(x, y)
        s, (final, raw) = car[0], land
        slot = lambda p: 2 * p[0] + p[1]
        q0, q1 = pl.ds(0, q), pl.ds(q, q)
        return [_remote(s.at[slot(xn), q0], final.at[0, q0], ss, rs, 0, (xn[0], xn[1], c)),
                _remote(s.at[slot(yn), q1], final.at[1, q1], ss, rs, 1, (yn[0], yn[1], c)),
                _remote(s.at[slot(dg), q0], raw.at[1], ss, rs, 2, (xn[0], xn[1], c)),
                _remote(s.at[slot(dg), q1], raw.at[0], ss, rs, 3, (yn[0], yn[1], c))]

    def start(car, land, ss, rs):
        for cp in plan(car, land, ss, rs):
            cp.start()

    def finish(car, land, ss, rs):
        cps = plan(car, land, ss, rs)
        for cp in cps:
            cp.wait_recv()
        for cp in cps:
            cp.wait_send()

    return _Hook([s16], [_sds((2, h, cc), s16.dtype), _sds((2, q, cc), s16.dtype)], 4, start, finish)


def _hook_scatter_relay(comb, final):
    _, q, _ = comb.shape

    def plan(car, ss, rs):
        x, y, c = _my_pos()
        xn, yn, _ = _other_chips(x, y)
        cb, final_ref = car
        return [_remote(cb.at[0], final_ref.at[0, pl.ds(q, q)], ss, rs, 0, (xn[0], xn[1], c)),
                _remote(cb.at[1], final_ref.at[1, pl.ds(0, q)], ss, rs, 1, (yn[0], yn[1], c))]

    def start(car, land, ss, rs):
        for cp in plan(car, ss, rs):
            cp.start()

    def finish(car, land, ss, rs):
        cps = plan(car, ss, rs)
        for cp in cps:
            cp.wait_recv()
        for cp in cps:
            cp.wait_send()

    return _Hook([comb, final], [], 2, start, finish)


def _hook_halves(full):
    h = full.shape[0] // 2

    def half_copy(car, ss, rs, which):
        x, y, c = _my_pos()
        rows = car[0].at[pl.ds(pl.multiple_of((c + which - 2 * c * which) * h, 8), h)]
        return _remote(rows, rows, ss, rs, 0, (x, y, 1 - c))

    def start(car, land, ss, rs):
        half_copy(car, ss, rs, 0).start()

    def finish(car, land, ss, rs):
        half_copy(car, ss, rs, 1).wait_recv()
        half_copy(car, ss, rs, 0).wait_send()

    return _Hook([full], [], 1, start, finish)


def _pcall(body, *, name, grid, in_specs, out_specs, out_shape, operands, scratch=(), vmem=48, sem=None,
           prefetch=(), aliases=None, hook=None):
    n_pre, n_in, n_out, n_scr = len(prefetch), len(in_specs), len(out_specs), len(scratch)
    in_specs, out_specs, out_shape, scratch = list(in_specs), list(out_specs), list(out_shape), list(scratch)
    io_alias = {n_pre + a: b for a, b in (aliases or {}).items()}
    operands = list(operands)
    kernel_body = body
    if hook is not None:
        n_car, n_land = len(hook.carried), len(hook.landing)
        for t, arr in enumerate(hook.carried):
            io_alias[n_pre + n_in + t] = n_out + t
        in_specs += [_ANY] * n_car
        out_specs += [_ANY] * (n_car + n_land)
        out_shape += [_sds(a.shape, a.dtype) for a in hook.carried] + hook.landing
        scratch += [pltpu.SemaphoreType.DMA((hook.n_sems,)), pltpu.SemaphoreType.DMA((hook.n_sems,))]
        operands += hook.carried
        sem = ("arbitrary",) * len(grid)

        def kernel_body(*refs):
            pre, rest = refs[:n_pre], refs[n_pre:]
            ins = rest[:n_in]
            outs = rest[n_in + n_car:n_in + n_car + n_out]
            car = rest[n_in + n_car + n_out:n_in + 2 * n_car + n_out]
            land = rest[n_in + 2 * n_car + n_out:n_in + 2 * n_car + n_out + n_land]
            scr = rest[n_in + 2 * n_car + n_out + n_land:]
            send_sems, recv_sems = scr[n_scr], scr[n_scr + 1]
            first = pl.program_id(0) == 0
            last = pl.program_id(0) == grid[0] - 1
            for d in range(1, len(grid)):
                first = first & (pl.program_id(d) == 0)
                last = last & (pl.program_id(d) == grid[d] - 1)

            @pl.when(first)
            def _():
                hook.start(car, land, send_sems, recv_sems)

            if hook.mid is not None:
                step = pl.program_id(0)
                total = grid[0]
                for d in range(1, len(grid)):
                    step = step * grid[d] + pl.program_id(d)
                    total *= grid[d]

                @pl.when(step == int(total * hook.mid_frac))
                def _():
                    hook.mid(car, land, send_sems, recv_sems)

            body(*pre, *ins, *outs, *scr[:n_scr])

            @pl.when(last)
            def _():
                hook.finish(car, land, send_sems, recv_sems)

    params = pltpu.CompilerParams(vmem_limit_bytes=vmem * _MIB, dimension_semantics=sem,
                                  has_side_effects=hook is not None)
    if n_pre:
        call = pl.pallas_call(
            kernel_body, name=name, out_shape=out_shape, input_output_aliases=io_alias, compiler_params=params,
            grid_spec=pltpu.PrefetchScalarGridSpec(num_scalar_prefetch=n_pre, grid=grid, in_specs=in_specs,
                                                   out_specs=out_specs, scratch_shapes=scratch))
    else:
        call = pl.pallas_call(kernel_body, name=name, grid=grid, in_specs=in_specs, out_specs=out_specs,
                              out_shape=out_shape, scratch_shapes=scratch, input_output_aliases=io_alias,
                              compiler_params=params)
    return call(*prefetch, *operands)


def _gather_small(blk):
    r, cc = blk.shape

    def body(x_ref, o_ref, send_sems, recv_sems):
        x, y, c = _my_pos()
        me = 2 * x + y
        o_ref[me] = x_ref[...]
        sends = [_remote(x_ref, o_ref.at[me], send_sems, recv_sems, k, (px, py, c))
                 for k, (px, py) in enumerate(_other_chips(x, y))]
        for cp in sends:
            cp.start()
        for k, (px, py) in enumerate(_other_chips(x, y)):
            _remote(x_ref, o_ref.at[2 * px + py], send_sems, recv_sems, k, (px, py, c)).wait_recv()
        for cp in sends:
            cp.wait_send()

    vm = pl.BlockSpec(memory_space=pltpu.VMEM)
    return pl.pallas_call(
        body, name="gather_small", in_specs=[vm], out_specs=vm, out_shape=_sds((N_SH, r, cc), blk.dtype),
        scratch_shapes=[pltpu.SemaphoreType.DMA((3,)), pltpu.SemaphoreType.DMA((3,))],
        compiler_params=pltpu.CompilerParams(has_side_effects=True),
    )(blk)


N_DEV = 8


def _allreduce_small(pack, halves):
    r, cc = pack.shape
    ride = _hook_halves(halves)

    def body(x_ref, h_in_ref, o_ref, h_ref, buf_ref, send_sems, recv_sems, h_ss, h_rs):
        del h_in_ref
        ride.start([h_ref], [], h_ss, h_rs)
        x, y, c = _my_pos()
        me = 4 * x + 2 * y + c
        buf_ref[me] = x_ref[...]
        cps = []
        for k in range(1, N_DEV):
            peer = (x ^ ((k >> 2) & 1), y ^ ((k >> 1) & 1), c ^ (k & 1))
            cps.append(_remote(x_ref, buf_ref.at[me], send_sems, recv_sems, k - 1, peer))
        for cp in cps:
            cp.start()
        for k in range(1, N_DEV):
            peer = (x ^ ((k >> 2) & 1), y ^ ((k >> 1) & 1), c ^ (k & 1))
            src = 4 * peer[0] + 2 * peer[1] + peer[2]
            _remote(x_ref, buf_ref.at[src], send_sems, recv_sems, k - 1, peer).wait_recv()
        acc = buf_ref[0]
        for d in range(1, N_DEV):
            acc = acc + buf_ref[d]
        o_ref[...] = acc
        for cp in cps:
            cp.wait_send()
        ride.finish([h_ref], [], h_ss, h_rs)

    vm = pl.BlockSpec(memory_space=pltpu.VMEM)
    return pl.pallas_call(
        body, name="allreduce_small", in_specs=[vm, _ANY], out_specs=[vm, _ANY],
        out_shape=[_sds((r, cc), F32), _sds(halves.shape, halves.dtype)], input_output_aliases={1: 1},
        scratch_shapes=[pltpu.VMEM((N_DEV, r, cc), F32), pltpu.SemaphoreType.DMA((N_DEV - 1,)),
                        pltpu.SemaphoreType.DMA((N_DEV - 1,)), pltpu.SemaphoreType.DMA((1,)), pltpu.SemaphoreType.DMA((1,))],
        compiler_params=pltpu.CompilerParams(has_side_effects=True),
    )(pack, halves)


def _mm_nn(a, w, bias, *, name, grid, tm, tn, k, a_index, w_block, w_index, out_cols, out_index, carried=None, hook=None):
    m = a.shape[0]

    def body(a_ref, w_ref, b_ref, *rest):
        o_ref = rest[-1]
        wv = w_ref[...]
        acc = jnp.dot(a_ref[...].astype(_MXU), wv.reshape(k, tn).astype(_MXU), preferred_element_type=F32)
        o_ref[...] = acc + b_ref[...]

    operands = [a, w, bias] + ([carried] if carried is not None else [])
    return _pcall(
        body, name=name, grid=grid,
        in_specs=[pl.BlockSpec((tm, k), a_index), pl.BlockSpec(w_block, w_index),
                  pl.BlockSpec((1, tn), lambda i, j: (0, j))] + ([_ANY] if carried is not None else []),
        out_specs=[pl.BlockSpec((tm, tn), out_index)], out_shape=[_sds((m, out_cols), F32)], operands=operands,
        vmem=56, sem=("parallel", "parallel"), aliases={3: 0} if carried is not None else None, hook=hook)


def _mm_nt(a, w, *, name, grid, tm, tn, tk, a_index, w_block, w_index, out_cols, hook=None):
    m = a.shape[0]
    nk = grid[2]

    def body(a_ref, w_ref, o_ref, acc_ref):
        part = lax.dot_general(a_ref[...].astype(_MXU), w_ref[...].astype(_MXU), _NT, preferred_element_type=F32)
        if nk == 1:
            o_ref[...] = part
        else:
            kidx = pl.program_id(2)

            @pl.when(kidx == 0)
            def _():
                acc_ref[...] = part

            @pl.when(kidx > 0)
            def _():
                acc_ref[...] += part

            @pl.when(kidx == nk - 1)
            def _():
                o_ref[...] = acc_ref[...]

    return _pcall(
        body, name=name, grid=grid, in_specs=[pl.BlockSpec((tm, tk), a_index), pl.BlockSpec(w_block, w_index)],
        out_specs=[pl.BlockSpec((tm, tn), lambda i, j, q: (i, j))], out_shape=[_sds((m, out_cols), F32)],
        operands=[a, w], scratch=[pltpu.VMEM((tm, tn) if nk > 1 else (8, 128), F32)], vmem=60,
        sem=("parallel", "parallel", "arbitrary"), hook=hook)


def _mm_tn(at, b, *, name, grid, tm, tn, a_index, b_index, out_shape, out_block, out_index, carried=None,
           prefetch=(), hook=None, out_dtype=F32, colsum=False):
    t = at.shape[1]
    n_pre = len(prefetch)

    def body(*refs):
        a_ref, b_ref = refs[n_pre], refs[n_pre + 1]
        bv = b_ref[...]
        o_ref = refs[-2] if colsum else refs[-1]
        o_ref[...] = jnp.dot(a_ref[...].astype(_MXU), bv.astype(_MXU), preferred_element_type=F32).astype(out_dtype)
        if colsum:
            refs[-1][...] = jnp.sum(bv.astype(F32), axis=0, keepdims=True)

    operands = [at, b] + ([carried] if carried is not None else [])
    out_specs = [pl.BlockSpec(out_block, out_index)]
    out_shapes = [_sds(out_shape, out_dtype)]
    if colsum:
        out_specs.append(pl.BlockSpec((1, tn), b_index))
        out_shapes.append(_sds((1, b.shape[1]), F32))
    return _pcall(
        body, name=name, grid=grid,
        in_specs=[pl.BlockSpec((tm, t), a_index), pl.BlockSpec((t, tn), b_index)] + ([_ANY] if carried is not None else []),
        out_specs=out_specs, out_shape=out_shapes, operands=operands,
        vmem=56, sem=("parallel", "parallel"), aliases={2: 0} if carried is not None else None, prefetch=prefetch, hook=hook)


def _remote_tile(n):
    near = 2 * PER_IN
    if isinstance(n, int):
        return (n % 2, n // 2) if n < near else (2, n - near)
    return jnp.where(n < near, n % 2, 2), jnp.where(n < near, n // 2, n - near)


def _mm_z_gather(h16, w_own, b_in, order):
    n_tiles = N_SH * PER_IN
    n_remote = 3 * PER_IN
    half = D // 2

    def body(order_ref, a_ref, b_ref, w_in_ref, z_ref, w_ref, wbuf, tile_sems, ss, rs):
        del w_in_ref
        j = pl.program_id(0)
        x, y, c = _my_pos()
        me = 2 * x + y
        chips = _other_chips(x, y)
        mine_rows = pl.ds(pl.multiple_of(c * half, 16), half)
        sib_rows = pl.ds(pl.multiple_of((1 - c) * half, 16), half)

        slots = [2 * px + py for px, py in chips]
        cols = lambda t: pl.ds(t * TN_IN, TN_IN)
        q_rows = lambda r: pl.ds(pl.multiple_of(c * half + r * (half // 2), 16), half // 2)

        def direct(rel, t, slot):
            px, py = chips[rel]
            part = w_ref.at[slot, mine_rows, cols(t)]
            return _remote(part, part, ss, rs, 2 * t + rel, (px, py, c))

        def relay(r, t, slot):
            px, py = chips[1 - r]
            part = w_ref.at[slot, q_rows(r), cols(t)]
            return _remote(part, part, ss, rs, 2 * PER_IN + 2 * t + r, (px, py, c))

        def d2d(n, rows):
            rel, t = _remote_tile(n)
            part = w_ref.at[slots[rel], rows, cols(t)]
            return _remote(part, part, ss, rs, 4 * PER_IN + n, (x, y, 1 - c))

        def tile_copy(step):
            rel, t = _remote_tile(jnp.maximum(step - PER_IN, 0))
            slot = jnp.where(step < PER_IN, me, me ^ jnp.where(rel == 0, 2, jnp.where(rel == 1, 1, 3)))
            col = pl.multiple_of(jnp.where(step < PER_IN, step, t) * TN_IN, 128)
            return pltpu.make_async_copy(w_ref.at[slot, :, pl.ds(col, TN_IN)], wbuf.at[step % 2], tile_sems.at[step % 2])

        @pl.when(j == 0)
        def _():
            for t in range(PER_IN):
                direct(0, t, me).start()
                direct(1, t, me).start()
            tile_copy(0).start()

        for n in range(n_remote):
            rel, t = _remote_tile(n)

            @pl.when(j == n + PER_IN - 3)
            def _():
                if rel < 2:
                    direct(rel, t, slots[rel]).wait_recv()
                    relay(rel, t, slots[rel]).start()
                else:
                    relay(0, t, slots[2]).wait_recv()
                    relay(1, t, slots[2]).wait_recv()
                d2d(n, mine_rows).start()

            @pl.when(j == n + PER_IN - 2)
            def _():
                d2d(n, sib_rows).wait_recv()

        @pl.when(j + 1 < n_tiles)
        def _():
            tile_copy(j + 1).start()

        tile_copy(j).wait()
        z_ref[...] = jnp.dot(a_ref[...], wbuf[j % 2], preferred_element_type=F32) + b_ref[...]

        @pl.when(j == n_tiles - 1)
        def _():
            for t in range(PER_IN):
                for r in range(2):
                    direct(r, t, me).wait_send()
                    relay(r, t, slots[r]).wait_send()
            for n in range(n_remote):
                d2d(n, mine_rows).wait_send()

    def col_tile(j, o):
        rel, t = _remote_tile(jnp.maximum(j - PER_IN, 0))
        return 0, jnp.where(j < PER_IN, o[0] * PER_IN + j, o[1 + rel] * PER_IN + t)

    return pl.pallas_call(
        body, name="mm_z_gather",
        grid_spec=pltpu.PrefetchScalarGridSpec(
            num_scalar_prefetch=1, grid=(n_tiles,),
            in_specs=[pl.BlockSpec((TP, D), lambda j, o: (0, 0)), pl.BlockSpec((1, TN_IN), col_tile), _ANY],
            out_specs=[pl.BlockSpec((TP, TN_IN), col_tile), _ANY],
            scratch_shapes=[pltpu.VMEM((2, D, TN_IN), BF16), pltpu.SemaphoreType.DMA((2,)),
                            pltpu.SemaphoreType.DMA((4 * PER_IN + n_remote,)),
                            pltpu.SemaphoreType.DMA((4 * PER_IN + n_remote,))]),
        out_shape=[_sds((TP, D_IN), F32), _sds(w_own.shape, w_own.dtype)],
        input_output_aliases={3: 1},
        compiler_params=pltpu.CompilerParams(vmem_limit_bytes=60 * _MIB, dimension_semantics=("arbitrary",),
                                             has_side_effects=True),
    )(order, h16, b_in, w_own)


def _padded_rows(i, x_ref, meta_ref):
    head = jnp.concatenate([jnp.zeros((PAD, D), F32), meta_ref[...]], axis=0)
    return jnp.where(i == 0, head, x_ref[...])


def _stream_specs():
    return [pl.BlockSpec((BLK, D), lambda i: (jnp.maximum(i - 1, 0), 0)), pl.BlockSpec((N_META, D), lambda i: (0, 0))]


def _ln_emb_fwd(x, meta, g, b):
    def body(x_ref, meta_ref, g_ref, b_ref, h32_ref, h16_ref, h16t_ref):
        x = _padded_rows(pl.program_id(0), x_ref, meta_ref)
        mu = jnp.mean(x, axis=-1, keepdims=True)
        xc = x - mu
        var = jnp.mean(xc * xc, axis=-1, keepdims=True)
        y = xc * lax.rsqrt(var + LN_EPS) * g_ref[...] + b_ref[...]
        h32_ref[...] = y
        h16_ref[...] = y.astype(_MXU)
        h16t_ref[...] = y.T.astype(_MXU)

    row = pl.BlockSpec((BLK, D), lambda i: (i, 0))
    vec = pl.BlockSpec((1, D), lambda i: (0, 0))
    return _pcall(body, name="ln_emb_fwd", grid=(NBLK,), in_specs=_stream_specs() + [vec, vec],
                  out_specs=[row, row, pl.BlockSpec((D, BLK), lambda i: (0, i))],
                  out_shape=[_sds((TP, D), F32), _sds((TP, D), _MXU), _sds((D, TP), _MXU)], operands=[x, meta, g, b],
                  vmem=32, sem=("parallel",))


def _ln_emb_bwd(x, meta, g, dr, dhz, *, hook):
    def body(x_ref, meta_ref, g_ref, dr_ref, dhz_ref, dx_ref, dmeta_ref, acc_ref):
        i = pl.program_id(0)

        @pl.when(i == 0)
        def _():
            acc_ref[...] = jnp.zeros_like(acc_ref)

        x = _padded_rows(i, x_ref, meta_ref)
        mu = jnp.mean(x, axis=-1, keepdims=True)
        xc = x - mu
        var = jnp.mean(xc * xc, axis=-1, keepdims=True)
        rstd = lax.rsqrt(var + LN_EPS)
        xhat = xc * rstd
        dh = ALPHA * dr_ref[...] + dhz_ref[...]
        acc_ref[0:1, :] += jnp.sum(dh * xhat, axis=0, keepdims=True)
        acc_ref[1:2, :] += jnp.sum(dh, axis=0, keepdims=True)
        dxh = dh * g_ref[...]
        m1 = jnp.mean(dxh, axis=-1, keepdims=True)
        m2 = jnp.mean(dxh * xhat, axis=-1, keepdims=True)
        dx = rstd * (dxh - m1 - xhat * m2)
        dx_ref[...] = dx

        @pl.when(i == 0)
        def _():
            dmeta_ref[...] = dx[PAD:BLK]

    row = pl.BlockSpec((BLK, D), lambda i: (i, 0))
    vec = pl.BlockSpec((1, D), lambda i: (0, 0))
    xs, ms = _stream_specs()
    return _pcall(body, name="ln_emb_bwd", grid=(NBLK,), in_specs=[xs, ms, vec, row, row],
                  out_specs=[xs, ms, pl.BlockSpec((8, D), lambda i: (0, 0))],
                  out_shape=[_sds((SEQ, D), F32), _sds((N_META, D), F32), _sds((8, D), F32)],
                  operands=[x, meta, g, dr, dhz], vmem=32, sem=("arbitrary",), hook=hook)


def _mul_silu_fwd(a, z, off, *, name):
    def body(a_ref, z_ref, o_ref, t_ref):
        zz = z_ref[...]
        y = a_ref[...] * (zz * _sigmoid(zz))
        o_ref[...] = y.astype(_MXU)
        t_ref[...] = y.T.astype(_MXU)

    strip = pl.BlockSpec((TP, CW), lambda j: (0, j))
    return _pcall(body, name=name, grid=(D // CW,),
                  in_specs=[strip, pl.BlockSpec((TP, CW), lambda j: (0, off // CW + j))],
                  out_specs=[strip, pl.BlockSpec((CW, TP), lambda j: (j, 0))],
                  out_shape=[_sds((TP, D), _MXU), _sds((D, TP), _MXU)], operands=[a, z], vmem=56, sem=("parallel",))


def _mul_silu_bwd(dy, a, z, off, dz, *, name, hook=None):
    def body(dy_ref, a_ref, z_ref, dz_in, da_ref, dg_ref):
        zz = z_ref[...]
        sg = _sigmoid(zz)
        d = dy_ref[...]
        da_ref[...] = d * (zz * sg)
        dg_ref[...] = (d * a_ref[...] * (sg * (1.0 + zz * (1.0 - sg)))).astype(_MXU)

    blk = pl.BlockSpec((RT, CW), lambda i, j: (i, j))
    zblk = pl.BlockSpec((RT, CW), lambda i, j: (i, off // CW + j))
    return _pcall(body, name=name, grid=(TP // RT, D // CW), in_specs=[blk, blk, zblk, _ANY], out_specs=[blk, zblk],
                  out_shape=[_sds((TP, D), F32), _sds((TP, D_IN), _MXU)], operands=[dy, a, z, dz], vmem=32,
                  sem=("parallel", "parallel"), aliases={3: 1}, hook=hook)


def _merge_fwd(y2, z):
    w = 256

    def body(ya_ref, yb_ref, ga_ref, gb_ref, o_ref, t_ref):
        y = _sigmoid(ga_ref[...]) * ya_ref[...] + _sigmoid(gb_ref[...]) * yb_ref[...]
        o_ref[...] = y.astype(_MXU)
        t_ref[...] = y.T.astype(_MXU)

    nb = D // w
    strip = pl.BlockSpec((TP, w), lambda j: (0, j))
    return _pcall(body, name="merge_fwd", grid=(nb,),
                  in_specs=[strip, pl.BlockSpec((TP, w), lambda j: (0, nb + j)),
                            pl.BlockSpec((TP, w), lambda j: (0, OFF_G // w + j)),
                            pl.BlockSpec((TP, w), lambda j: (0, (OFF_G + D) // w + j))],
                  out_specs=[strip, pl.BlockSpec((w, TP), lambda j: (j, 0))],
                  out_shape=[_sds((TP, D), _MXU), _sds((D, TP), _MXU)], operands=[y2, y2, z, z], vmem=56,
                  sem=("parallel",))


def _merge_bwd(dmix, y2, z):
    nb = D // CW

    def body(dm_ref, y_ref, g_ref, dy_ref, dg_ref):
        dm = dm_ref[...]
        sg = _sigmoid(g_ref[...])
        dy_ref[...] = (dm * sg).astype(_MXU)
        dg_ref[...] = (dm * y_ref[...] * sg * (1.0 - sg)).astype(_MXU)

    blk = pl.BlockSpec((RT, CW), lambda i, j: (i, j))
    gblk = pl.BlockSpec((RT, CW), lambda i, j: (i, OFF_G // CW + j))
    return _pcall(body, name="merge_bwd", grid=(TP // RT, 2 * nb),
                  in_specs=[pl.BlockSpec((RT, CW), lambda i, j: (i, j % nb)), blk, gblk], out_specs=[blk, gblk],
                  out_shape=[_sds((TP, 2 * D), _MXU), _sds((TP, D_IN), _MXU)], operands=[dmix, y2, z], vmem=32,
                  sem=("parallel", "parallel"))


def _ln_out_loss(h32, out, target, g, b):
    def body(h_ref, o_ref, t_ref, g_ref, b_ref, dr_ref, dr16_ref, acc_ref):
        i = pl.program_id(0)

        @pl.when(i == 0)
        def _():
            acc_ref[...] = jnp.zeros_like(acc_ref)

        r = ALPHA * h_ref[...] + o_ref[...]
        mu = jnp.mean(r, axis=-1, keepdims=True)
        rc = r - mu
        var = jnp.mean(rc * rc, axis=-1, keepdims=True)
        rstd = lax.rsqrt(var + LN_EPS)
        xhat = rc * rstd
        gg = g_ref[...]
        y = xhat * gg + b_ref[...]
        real = (i >= 1).astype(F32)
        diff = (y - t_ref[...]) * real
        dy = diff * (1.0 / D)
        dxh = dy * gg
        m1 = jnp.mean(dxh, axis=-1, keepdims=True)
        m2 = jnp.mean(dxh * xhat, axis=-1, keepdims=True)
        dr = rstd * (dxh - m1 - xhat * m2)
        dr_ref[...] = dr
        dr16_ref[...] = dr.astype(_MXU)
        acc_ref[0:1, :] += jnp.sum(dy * xhat, axis=0, keepdims=True)
        acc_ref[1:2, :] += jnp.sum(dy, axis=0, keepdims=True)
        acc_ref[2:3, :] += jnp.sum(dr, axis=0, keepdims=True)
        acc_ref[3:4, :] += (0.5 / D) * jnp.sum(diff * diff)

    row = pl.BlockSpec((BLK, D), lambda i: (i, 0))
    vec = pl.BlockSpec((1, D), lambda i: (0, 0))
    return _pcall(body, name="ln_out_loss", grid=(NBLK,),
                  in_specs=[row, row, pl.BlockSpec((BLK, D), lambda i: (jnp.maximum(i - 1, 0), 0)), vec, vec],
                  out_specs=[row, row, pl.BlockSpec((8, D), lambda i: (0, 0))],
                  out_shape=[_sds((TP, D), F32), _sds((TP, D), _MXU), _sds((8, D), F32)],
                  operands=[h32, out, target, g, b], vmem=32, sem=("arbitrary",))


def _rnn_recompute(xr_ref, cw_ref, cb_ref, wra_ref, wri_ref, bra_ref, bri_ref, lam_ref):
    rows = lax.broadcasted_iota(jnp.int32, (TP, 1), 0)
    valid = (rows >= PAD).astype(F32)
    first = rows == PAD
    x = xr_ref[...] * valid
    cw = cw_ref[...]
    shifted = [x, pltpu.roll(x, 1, 0), pltpu.roll(x, 2, 0), pltpu.roll(x, 3, 0)]
    c = cb_ref[...] + cw[0:1, :] * shifted[0] + cw[1:2, :] * shifted[1] + cw[2:3, :] * shifted[2] + cw[3:4, :] * shifted[3]
    cm = c.astype(_MXU)
    gr = _sigmoid(jnp.dot(cm, wra_ref[...].astype(_MXU), preferred_element_type=F32) + bra_ref[...])
    gi = _sigmoid(jnp.dot(cm, wri_ref[...].astype(_MXU), preferred_element_type=F32) + bri_ref[...])
    lam = lam_ref[...]
    ls = jnp.minimum(lam, 0.0) - jnp.log(1.0 + jnp.exp(-jnp.abs(lam)))
    log_a = LRU_C * gr * ls
    a = jnp.exp(log_a)
    mult = jnp.where(first, 1.0, jnp.sqrt(1.0 - jnp.exp(2.0 * log_a)))
    return dict(valid=valid, first=first, shifted=shifted, c=c, cm=cm, gr=gr, gi=gi, ls=ls, a=a, mult=mult, lam=lam)


def _rnn_specs():
    col = pl.BlockSpec((TP, RB), lambda n: (0, n))
    vec = pl.BlockSpec((1, RB), lambda n: (0, n))
    return dict(col=col, vec=vec, cw=pl.BlockSpec((CONV_W, RB), lambda n: (0, n)),
                wblk=pl.BlockSpec((None, RB, RB), lambda n: (n, 0, 0)))


def _rnn_gates_fwd(z, conv_w, conv_b, w_ra, w_ri, b_ra, b_ri, lam):
    def body(xr_ref, cw_ref, cb_ref, wra_ref, wri_ref, bra_ref, bri_ref, lam_ref, a_ref, u_ref):
        r = _rnn_recompute(xr_ref, cw_ref, cb_ref, wra_ref, wri_ref, bra_ref, bri_ref, lam_ref)
        a_ref[...] = r["a"]
        u_ref[...] = r["mult"] * r["gi"] * r["c"] * r["valid"]

    s = _rnn_specs()
    return _pcall(body, name="rnn_gates_fwd", grid=(N_RB,),
                  in_specs=[s["col"], s["cw"], s["vec"], s["wblk"], s["wblk"], s["vec"], s["vec"], s["vec"]],
                  out_specs=[s["col"], s["col"]], out_shape=[_sds((TP, D), F32)] * 2,
                  operands=[z, conv_w, conv_b, w_ra, w_ri, b_ra, b_ri, lam], vmem=56, sem=("parallel",))


SCAN_ROWS = 272


SUB = 8


def _tile_scan(a, u, reverse):
    rows = lax.broadcasted_iota(jnp.int32, a.shape, 0)
    for d in (1, 2, 4):
        shift = SUB - d if reverse else d
        inside = (rows < SUB - d) if reverse else (rows >= d)
        u = u + a * jnp.where(inside, pltpu.roll(u, shift, 0), 0.0)
        a = a * jnp.where(inside, pltpu.roll(a, shift, 0), 1.0)
    return a, u


def _scan_fwd(a, u, *, hook):
    def body(a_ref, u_ref, h_ref, carry_ref):
        @pl.when(pl.program_id(1) == 0)
        def _():
            carry_ref[...] = jnp.zeros_like(carry_ref)

        def step(r, h):
            rows = pl.ds(pl.multiple_of(r * SUB, SUB), SUB)
            prod, part = _tile_scan(a_ref[rows, :], u_ref[rows, :], False)
            ht = part + prod * h
            h_ref[rows, :] = ht
            return ht[SUB - 1:SUB, :]

        carry_ref[...] = lax.fori_loop(0, SCAN_ROWS // SUB, step, carry_ref[...], unroll=2)

    blk = pl.BlockSpec((SCAN_ROWS, CW), lambda j, i: (i, j))
    return _pcall(body, name="scan_fwd", grid=(D // CW, TP // SCAN_ROWS), in_specs=[blk, blk], out_specs=[blk],
                  out_shape=[_sds((TP, D), F32)], operands=[a, u], scratch=[pltpu.VMEM((1, CW), F32)], vmem=32,
                  sem=("parallel", "arbitrary"), hook=hook)


def _scan_bwd(a, dh):
    nst = TP // SCAN_ROWS
    n_tiles = SCAN_ROWS // SUB

    def body(a_ref, d_ref, o_ref, lam_ref, anext_ref):
        @pl.when(pl.program_id(1) == 0)
        def _():
            lam_ref[...] = jnp.zeros_like(lam_ref)
            anext_ref[...] = jnp.zeros_like(anext_ref)

        def step(q, carry):
            lam_next, a_next = carry
            rows = pl.ds(pl.multiple_of((n_tiles - 1 - q) * SUB, SUB), SUB)
            at = a_ref[rows, :]
            last = lax.broadcasted_iota(jnp.int32, at.shape, 0) == SUB - 1
            b = jnp.where(last, a_next, pltpu.roll(at, SUB - 1, 0))
            prod, part = _tile_scan(b, d_ref[rows, :], True)
            lam = part + prod * lam_next
            o_ref[rows, :] = lam
            return lam[0:1, :], at[0:1, :]

        lam, an = lax.fori_loop(0, n_tiles, step, (lam_ref[...], anext_ref[...]), unroll=2)
        lam_ref[...] = lam
        anext_ref[...] = an

    blk = pl.BlockSpec((SCAN_ROWS, CW), lambda j, i: (nst - 1 - i, j))
    return _pcall(body, name="scan_bwd", grid=(D // CW, nst), in_specs=[blk, blk], out_specs=[blk],
                  out_shape=[_sds((TP, D), F32)], operands=[a, dh],
                  scratch=[pltpu.VMEM((1, CW), F32), pltpu.VMEM((1, CW), F32)], vmem=32,
                  sem=("parallel", "arbitrary"))[0]


def _rnn_gates_bwd(z, lam_s, hr, conv_w, conv_b, w_ra, w_ri, b_ra, b_ri, lam, dz, gsq):
    def body(xr_ref, ls_ref, hr_ref, cw_ref, cb_ref, wra_ref, wri_ref, bra_ref, bri_ref, lam_ref, dz_in, gsq_in,
             dx_ref, dw_ref, sums_ref):
        r = _rnn_recompute(xr_ref, cw_ref, cb_ref, wra_ref, wri_ref, bra_ref, bri_ref, lam_ref)
        valid, c, gr, gi, a, mult = r["valid"], r["c"], r["gr"], r["gi"], r["a"], r["mult"]
        du = ls_ref[...] * valid
        da = du * pltpu.roll(hr_ref[...], 1, 0)
        d_gi = du * mult * c
        dc = du * mult * gi
        dmult = du * gi * c
        dlog_a = da * a + jnp.where(r["first"], 0.0, -dmult * a * a / mult)
        d_gr = dlog_a * (LRU_C * r["ls"])
        dls = jnp.sum(dlog_a * (LRU_C * gr), axis=0, keepdims=True)
        dpre_r = d_gr * gr * (1.0 - gr)
        dpre_i = d_gi * gi * (1.0 - gi)
        pr = dpre_r.astype(_MXU)
        pi = dpre_i.astype(_MXU)
        dwra = lax.dot_general(r["cm"], pr, _TN, preferred_element_type=F32)
        dwri = lax.dot_general(r["cm"], pi, _TN, preferred_element_type=F32)
        for s in range(N_SH):
            dw_ref[s, 0:64, :] = dwra[64 * s:64 * (s + 1)]
            dw_ref[s, 64:128, :] = dwri[64 * s:64 * (s + 1)]
        dc = dc + lax.dot_general(pr, wra_ref[...].astype(_MXU), _NT, preferred_element_type=F32)
        dc = dc + lax.dot_general(pi, wri_ref[...].astype(_MXU), _NT, preferred_element_type=F32)
        cw = cw_ref[...]
        dx = cw[0:1, :] * dc
        for k in range(1, CONV_W):
            dx = dx + cw[k:k + 1, :] * pltpu.roll(dc, TP - k, 0)
        dx_ref[...] = (dx * valid).astype(_MXU)
        for k in range(CONV_W):
            sums_ref[k:k + 1, :] = jnp.sum(dc * r["shifted"][k], axis=0, keepdims=True)
        sums_ref[4:5, :] = jnp.sum(dc, axis=0, keepdims=True)
        sums_ref[5:6, :] = jnp.sum(dpre_r, axis=0, keepdims=True)
        sums_ref[6:7, :] = jnp.sum(dpre_i, axis=0, keepdims=True)
        sums_ref[7:8, :] = dls * _sigmoid(-r["lam"])

    s = _rnn_specs()
    return _pcall(
        body, name="rnn_gates_bwd", grid=(N_RB,),
        in_specs=[s["col"], s["col"], s["col"], s["cw"], s["vec"], s["wblk"], s["wblk"], s["vec"], s["vec"], s["vec"],
                  _ANY, _ANY],
        out_specs=[s["col"], pl.BlockSpec((N_SH, 128, RB), lambda n: (0, 3 * SQ_ROWS // 128, n)),
                   pl.BlockSpec((8, RB), lambda n: (0, n))],
        out_shape=[_sds((TP, D_IN), _MXU), _sds((N_SH, PACK_ROWS, D), F32), _sds((8, D), F32)],
        operands=[z, lam_s, hr, conv_w, conv_b, w_ra, w_ri, b_ra, b_ri, lam, dz, gsq], vmem=60, sem=("parallel",),
        aliases={10: 0, 11: 1})


def _rope_tables():
    half = HD // 2
    inv = ROPE_THETA ** (-jnp.arange(half, dtype=F32) / half)
    pos = (jnp.arange(TP) - PAD).astype(F32)
    ang = pos[:, None] * inv[None, :]
    return jnp.tile(jnp.cos(ang), (1, 4)), jnp.tile(jnp.sin(ang), (1, 4))


def _rope(x, cos_t, sin_t, sign):
    w = x.shape[1]
    lane = lax.broadcasted_iota(jnp.int32, x.shape, 1)
    first = (lane % HD) < (HD // 2)
    swapped = jnp.where(first, pltpu.roll(x, w - HD // 2, 1), pltpu.roll(x, HD // 2, 1))
    ct = jnp.tile(cos_t, (1, w // 128))
    st = jnp.tile(sin_t, (1, w // 128))
    return x * ct + swapped * jnp.where(first, -sign * st, sign * st)


def _rope_fwd(z, cos_t, sin_t):
    def body(q_ref, k_ref, v_ref, c_ref, s_ref, qo_ref, ko_ref, vo_ref):
        c = c_ref[...]
        s = s_ref[...]
        qo_ref[...] = _rope(q_ref[...], c, s, 1.0).astype(_MXU)
        ko_ref[...] = _rope(k_ref[...], c, s, 1.0).astype(_MXU)
        vo_ref[...] = v_ref[...].astype(_MXU)

    tab = pl.BlockSpec((BLK, 128), lambda i: (i, 0))
    kv = pl.BlockSpec((BLK, D_KV), lambda i: (i, 0))
    return _pcall(body, name="rope_fwd", grid=(NBLK,),
                  in_specs=[pl.BlockSpec((BLK, D), lambda i: (i, OFF_Q // D)),
                            pl.BlockSpec((BLK, D_KV), lambda i: (i, OFF_K // D_KV)),
                            pl.BlockSpec((BLK, D_KV), lambda i: (i, OFF_V // D_KV)), tab, tab],
                  out_specs=[pl.BlockSpec((BLK, D), lambda i: (i, 0)), kv, kv],
                  out_shape=[_sds((TP, D), _MXU), _sds((TP, D_KV), _MXU), _sds((TP, D_KV), _MXU)],
                  operands=[z, z, z, cos_t, sin_t], vmem=32, sem=("parallel",))


def _rope_bwd_k(dk, dv, cos_t, sin_t, dz):
    def body(dk_ref, dv_ref, c_ref, s_ref, dz_in, o_ref):
        o_ref[:, 0:D_KV] = _rope(dk_ref[...], c_ref[...], s_ref[...], -1.0).astype(_MXU)
        o_ref[:, D_KV:2 * D_KV] = dv_ref[...].astype(_MXU)

    tab = pl.BlockSpec((BLK, 128), lambda i: (i, 0))
    kv = pl.BlockSpec((BLK, D_KV), lambda i: (i, 0))
    return _pcall(body, name="rope_bwd_k", grid=(NBLK,), in_specs=[kv, kv, tab, tab, _ANY],
                  out_specs=[pl.BlockSpec((BLK, 2 * D_KV), lambda i: (i, OFF_K // (2 * D_KV)))],
                  out_shape=[_sds((TP, D_IN), _MXU)], operands=[dk, dv, cos_t, sin_t, dz], vmem=32, sem=("parallel",),
                  aliases={4: 0})[0]


def _attn_mask(i):
    ql = lax.broadcasted_iota(jnp.int32, (BLK, 3 * BLK), 0)
    kk = lax.broadcasted_iota(jnp.int32, (BLK, 3 * BLK), 1)
    kl = kk % BLK
    part = kk // BLK
    meta = (part == 0) & (kl >= PAD) & ((i >= 1) | (kl <= ql))
    prev = (part == 1) & (i >= 2) & (kl > ql)
    cur = (part == 2) & (i >= 1) & (kl <= ql)
    return meta | prev | cur


def _kv_specs():
    return [pl.BlockSpec((BLK, D_KV), lambda i: (0, 0)),
            pl.BlockSpec((BLK, D_KV), lambda i: (jnp.maximum(i - 1, 0), 0)),
            pl.BlockSpec((BLK, D_KV), lambda i: (i, 0))]


def _pair_heads(ref, gp):
    def two(j):
        h0 = 2 * GRP * gp + j
        return jnp.concatenate([ref[:, HD * h0:HD * (h0 + 1)], ref[:, HD * (h0 + GRP):HD * (h0 + GRP + 1)]], axis=1)

    return jnp.concatenate([two(j) for j in range(GRP)], axis=0)


def _pair_kv(refs, gp):
    both = jnp.concatenate([r[:, 2 * HD * gp:2 * HD * (gp + 1)] for r in refs], axis=0)
    low_lanes = lax.broadcasted_iota(jnp.int32, both.shape, 1) < HD
    zero = jnp.zeros_like(both)
    return jnp.concatenate([jnp.where(low_lanes, both, zero), jnp.where(low_lanes, zero, both)], axis=0)


def _attn_fwd(q, k, v, sinks, *, hook):
    nk = 3 * BLK

    def body(q_ref, k0_ref, kp_ref, kc_ref, v0_ref, vp_ref, vc_ref, sink_ref, o_ref, lse_ref, s_scr, p_scr):
        mask = _attn_mask(pl.program_id(0))
        for gp in range(N_KV // 2):
            s_scr[...] = lax.dot_general(_pair_heads(q_ref, gp), _pair_kv((k0_ref, kp_ref, kc_ref), gp), _NT,
                                         preferred_element_type=F32)
            for a in range(2):
                for j in range(GRP):
                    h = GRP * (2 * gp + a) + j
                    rows, cols = slice(BLK * j, BLK * (j + 1)), slice(nk * a, nk * (a + 1))
                    sink = sink_ref[h]
                    s = jnp.where(mask, s_scr[rows, cols] * (HD ** -0.5), NEG_INF)
                    mx = jnp.maximum(jnp.max(s, -1, keepdims=True), sink)
                    p = jnp.exp(s - mx)
                    den = jnp.sum(p, -1, keepdims=True) + jnp.exp(sink - mx)
                    p_scr[rows, cols] = (p * (1.0 / den)).astype(_MXU)
                    lse_ref[:, h:h + 1] = mx + jnp.log(den)
            o2 = jnp.dot(p_scr[...], _pair_kv((v0_ref, vp_ref, vc_ref), gp), preferred_element_type=F32)
            for a in range(2):
                for j in range(GRP):
                    h = GRP * (2 * gp + a) + j
                    o_ref[:, HD * h:HD * (h + 1)] = o2[BLK * j:BLK * (j + 1), HD * a:HD * (a + 1)]

    return _pcall(body, name="attn_fwd", grid=(NBLK,),
                  in_specs=[pl.BlockSpec((BLK, D), lambda i: (i, 0))] + _kv_specs() + _kv_specs()
                           + [pl.BlockSpec(memory_space=pltpu.SMEM)],
                  out_specs=[pl.BlockSpec((BLK, D), lambda i: (i, 0)), pl.BlockSpec((BLK, N_Q), lambda i: (i, 0))],
                  out_shape=[_sds((TP, D), F32), _sds((TP, N_Q), F32)], operands=[q, k, k, k, v, v, v, sinks],
                  scratch=[pltpu.VMEM((GRP * BLK, 2 * nk), F32), pltpu.VMEM((GRP * BLK, 2 * nk), _MXU)],
                  vmem=40, sem=("parallel",), hook=hook)


def _attn_bwd(q, k, v, sinks, do, o, lse, cos_t, sin_t, dz, *, hook):
    def body(q_ref, k0_ref, kp_ref, kc_ref, v0_ref, vp_ref, vc_ref, sink_ref, do_ref, o_ref, lse_ref, c_ref, s_ref, dz_in,
             dq_ref, dk_ref, dv_ref, dsink_ref, dqrot_ref, s_scr, dp_scr, p_scr, ds_scr):
        i = pl.program_id(0)

        @pl.when(i == 0)
        def _():
            dk_ref[...] = jnp.zeros_like(dk_ref)
            dv_ref[...] = jnp.zeros_like(dv_ref)
            dsink_ref[...] = jnp.zeros_like(dsink_ref)

        mask = _attn_mask(i)
        row_starts = (0, pl.multiple_of(jnp.maximum(i - 1, 0) * BLK, BLK), pl.multiple_of(i * BLK, BLK))
        scale = HD ** -0.5
        nk = 3 * BLK
        for gp in range(N_KV // 2):
            q2 = _pair_heads(q_ref, gp)
            dom2 = _pair_heads(do_ref, gp).astype(_MXU)
            k2 = _pair_kv((k0_ref, kp_ref, kc_ref), gp)
            s_scr[...] = lax.dot_general(q2, k2, _NT, preferred_element_type=F32)
            dp_scr[...] = lax.dot_general(dom2, _pair_kv((v0_ref, vp_ref, vc_ref), gp), _NT, preferred_element_type=F32)
            for a in range(2):
                for j in range(GRP):
                    h = GRP * (2 * gp + a) + j
                    hs = slice(HD * h, HD * (h + 1))
                    rows = slice(BLK * j, BLK * (j + 1))
                    cols = slice(nk * a, nk * (a + 1))
                    lse_h = lse_ref[:, h:h + 1]
                    delta = jnp.sum(do_ref[:, hs] * o_ref[:, hs], axis=-1, keepdims=True)
                    dsink_ref[0:1, h:h + 1] += jnp.sum(-jnp.exp(sink_ref[h] - lse_h) * delta, axis=0, keepdims=True)
                    p = jnp.exp(jnp.where(mask, s_scr[rows, cols] * scale, NEG_INF) - lse_h)
                    p_scr[rows, cols] = p.astype(_MXU)
                    ds_scr[rows, cols] = (p * (dp_scr[rows, cols] - delta) * scale).astype(_MXU)
            ds = ds_scr[...]
            dq2 = jnp.dot(ds, k2, preferred_element_type=F32)
            dk2 = lax.dot_general(ds, q2, _TN, preferred_element_type=F32)
            dv2 = lax.dot_general(p_scr[...], dom2, _TN, preferred_element_type=F32)
            for a in range(2):
                g = 2 * gp + a
                gs = slice(HD * g, HD * (g + 1))
                for part in range(3):
                    rows = pl.ds(row_starts[part], BLK)
                    blk = slice(nk * a + BLK * part, nk * a + BLK * (part + 1))
                    dk_ref[rows, gs] += dk2[blk, HD * a:HD * (a + 1)]
                    dv_ref[rows, gs] += dv2[blk, HD * a:HD * (a + 1)]
                for j in range(GRP):
                    h = GRP * g + j
                    dqrot_ref[:, HD * h:HD * (h + 1)] = dq2[BLK * j:BLK * (j + 1), HD * a:HD * (a + 1)]
        dq_ref[...] = _rope(dqrot_ref[...], c_ref[...], s_ref[...], -1.0).astype(_MXU)

    row = pl.BlockSpec((BLK, D), lambda i: (i, 0))
    tab = pl.BlockSpec((BLK, 128), lambda i: (i, 0))
    full_kv = pl.BlockSpec((TP, D_KV), lambda i: (0, 0))
    return _pcall(
        body, name="attn_bwd", grid=(NBLK,),
        in_specs=[row] + _kv_specs() + _kv_specs() + [pl.BlockSpec(memory_space=pltpu.SMEM), row, row,
                  pl.BlockSpec((BLK, N_Q), lambda i: (i, 0)), tab, tab, _ANY],
        out_specs=[pl.BlockSpec((BLK, D), lambda i: (i, OFF_Q // D)), full_kv, full_kv,
                   pl.BlockSpec((8, 128), lambda i: (0, 0))],
        out_shape=[_sds((TP, D_IN), _MXU), _sds((TP, D_KV), F32), _sds((TP, D_KV), F32), _sds((8, 128), F32)],
        operands=[q, k, k, k, v, v, v, sinks, do, o, lse, cos_t, sin_t, dz],
        scratch=[pltpu.VMEM((BLK, D), F32), pltpu.VMEM((GRP * BLK, 6 * BLK), F32), pltpu.VMEM((GRP * BLK, 6 * BLK), F32),
                 pltpu.VMEM((GRP * BLK, 6 * BLK), _MXU), pltpu.VMEM((GRP * BLK, 6 * BLK), _MXU)],
        vmem=48, sem=("arbitrary",), aliases={13: 0}, hook=hook)


def _cast_into_slot(w32, pos, *, name, tr):
    r, cc = w32.shape

    def body(pos_ref, w_ref, o_ref):
        o_ref[...] = w_ref[...].astype(BF16)

    return _pcall(body, name=name, grid=(r // tr,), in_specs=[pl.BlockSpec((tr, cc), lambda i, p: (i, 0))],
                  out_specs=[pl.BlockSpec((None, tr, cc), lambda i, p: (p[1], i, 0))],
                  out_shape=[_sds((N_SH, r, cc), BF16)], operands=[w32], vmem=32, sem=("parallel",), prefetch=(pos,))[0]


def _pair_add(g, got, pos, *, name, tr, g_has_both_halves, hook=None):
    n, h, cc = got.shape
    nt = h // tr

    def body(pos_ref, g_ref, r_ref, own_ref, s16_ref):
        s = g_ref[...] + r_ref[...].astype(F32)
        s16_ref[...] = s.astype(BF16)

        @pl.when(pl.program_id(1) == pos_ref[1])
        def _():
            own_ref[...] = s

    g_index = (lambda i, s, p: (s, p[0] * nt + i, 0)) if g_has_both_halves else (lambda i, s, p: (s, i, 0))
    return _pcall(body, name=name, grid=(nt, n),
                  in_specs=[pl.BlockSpec((None, tr, cc), g_index), pl.BlockSpec((None, tr, cc), lambda i, s, p: (s, i, 0))],
                  out_specs=[pl.BlockSpec((tr, cc), lambda i, s, p: (i, 0)),
                             pl.BlockSpec((None, tr, cc), lambda i, s, p: (s, i, 0))],
                  out_shape=[_sds((h, cc), F32), _sds((n, h, cc), BF16)], operands=[g, got], vmem=40,
                  sem=("parallel", "arbitrary"), prefetch=(pos,), hook=hook)


def _relay_add(s16, raw, pos, *, name, tr):
    _, q, cc = raw.shape
    nt = q // tr

    def body(pos_ref, s_ref, r_ref, o_ref):
        o_ref[...] = (s_ref[...].astype(F32) + r_ref[...].astype(F32)).astype(BF16)

    blk = pl.BlockSpec((None, tr, cc), lambda k, i, p: (k, i, 0))
    return _pcall(body, name=name, grid=(2, nt),
                  in_specs=[pl.BlockSpec((None, tr, cc), lambda k, i, p: (p[3 + k], (1 - k) * nt + i, 0)), blk],
                  out_specs=[blk], out_shape=[_sds((2, q, cc), BF16)], operands=[s16, raw], vmem=40,
                  sem=("parallel", "parallel"), prefetch=(pos,))[0]


def _sum_chips(own, got, pos, *, name, tr):
    h, cc = own.shape
    n_got = got.shape[0]
    nt = h // tr

    def body(pos_ref, o_ref, r_ref, out_ref):
        acc = o_ref[...]
        for k in range(n_got):
            acc = acc + r_ref[k].astype(F32)
        out_ref[...] = acc

    return _pcall(body, name=name, grid=(nt,),
                  in_specs=[pl.BlockSpec((tr, cc), lambda i, p: (i, 0)),
                            pl.BlockSpec((n_got, tr, cc), lambda i, p: (0, i, 0))],
                  out_specs=[pl.BlockSpec((tr, cc), lambda i, p: (p[0] * nt + i, 0))],
                  out_shape=[_sds((2 * h, cc), F32)], operands=[own, got], vmem=40, sem=("parallel",), prefetch=(pos,))[0]


def _adamw(w, g, m, v, *, name, tr, g_row0=0):
    r, cc = w.shape
    g_blk0 = g_row0 // tr

    def body(w_ref, g_ref, m_ref, v_ref, go_ref, d_ref, mo_ref, vo_ref):
        gg = g_ref[...]
        go_ref[...] = gg
        m_new = ADAM_B1 * m_ref[...] + (1.0 - ADAM_B1) * gg
        v_new = ADAM_B2 * v_ref[...] + (1.0 - ADAM_B2) * (gg * gg)
        m_hat = m_new / (1.0 - ADAM_B1 ** ADAM_STEP)
        v_hat = v_new / (1.0 - ADAM_B2 ** ADAM_STEP)
        d_ref[...] = -ADAM_LR * (m_hat / (jnp.sqrt(v_hat) + ADAM_EPS) + ADAM_WD * w_ref[...])
        mo_ref[...] = m_new
        vo_ref[...] = v_new

    blk = pl.BlockSpec((tr, cc), lambda i: (i, 0))
    gblk = pl.BlockSpec((tr, cc), lambda i: (g_blk0 + i, 0))
    return _pcall(body, name=name, grid=(r // tr,), in_specs=[blk, gblk, blk, blk], out_specs=[blk] * 4,
                  out_shape=[_sds((r, cc), F32)] * 4, operands=[w, g, m, v], vmem=48, sem=("parallel",))


_SMALL_ROWS = 40
_B_IN_ROWS = 7


def _row_pad(v, rows):
    flat = v.reshape(-1)
    return jnp.pad(flat, (0, rows * D - flat.shape[0])).reshape(rows, D)


def _pack_ra(w):
    return w.transpose(1, 0, 2).reshape(64, D)


def _unpack_ra(p, like):
    return p.reshape(64, N_RB, RB).transpose(1, 0, 2).reshape(like.shape)


def _gate_full(g4):
    return g4.reshape(N_SH, 64, N_RB, RB).transpose(2, 0, 1, 3).reshape(N_RB, RB, RB)


def kernel(x, meta_tokens, ln_emb_g, ln_emb_b, w_in, b_in, conv_w, conv_b, w_ra, b_ra, w_ri, b_ri, lru_lambda, sinks, w_rnn_out, w_attn_out, w_o, b_o, ln_g, ln_b, loss_target, m_meta_tokens, m_ln_emb_g, m_ln_emb_b, m_w_in, m_b_in, m_conv_w, m_conv_b, m_w_ra, m_b_ra, m_w_ri, m_b_ri, m_lru_lambda, m_sinks, m_w_rnn_out, m_w_attn_out, m_w_o, m_b_o, m_ln_g, m_ln_b, v_meta_tokens, v_ln_emb_g, v_ln_emb_b, v_w_in, v_b_in, v_conv_w, v_conv_b, v_w_ra, v_b_ra, v_w_ri, v_b_ri, v_lru_lambda, v_sinks, v_w_rnn_out, v_w_attn_out, v_w_o, v_b_o, v_ln_g, v_ln_b):
    xi, yi, ci = _my_pos()
    shard = 2 * xi + yi
    pos = jnp.stack([ci, shard, 1 - ci, shard ^ 2, shard ^ 1]).astype(jnp.int32)
    cos_t, sin_t = _rope_tables()
    zero_bias = jnp.zeros((1, D), F32)
    ln_emb_g2, ln_emb_b2 = ln_emb_g[None], ln_emb_b[None]

    small = jnp.concatenate([conv_w[0], meta_tokens, jnp.zeros((4, 512), F32)], axis=0)
    small4 = _gather_small(small)
    conv_w_full = small4[:, 0:4].transpose(1, 0, 2).reshape(CONV_W, D)
    meta_full = small4[:, 4:20].transpose(1, 0, 2).reshape(N_META, D)
    w_own = _cast_into_slot(w_in[0], pos, name="cast_w_in", tr=256)
    wa_own = _cast_into_slot(jnp.concatenate([w_attn_out[0], w_o[0], _pack_ra(w_ra[0]), _pack_ra(w_ri[0])], axis=0),
                             pos, name="cast_w_a", tr=288)
    wb_own = _cast_into_slot(w_rnn_out[0], pos, name="cast_w_b", tr=256)

    h32, h16, h16_t = _ln_emb_fwd(x[0], meta_full, ln_emb_g2, ln_emb_b2)
    order = jnp.stack([shard, shard ^ 2, shard ^ 1, shard ^ 3]).astype(jnp.int32)
    z, w_in4 = _mm_z_gather(h16, w_own, b_in, order)
    q, k, v = _rope_fwd(z, cos_t, sin_t)
    o, lse, wa4 = _attn_fwd(q, k, v, sinks[0], hook=_hook_gather(wa_own, 0.6))
    w_ra_full = _gate_full(wa4[:, 2 * SQ_ROWS:2 * SQ_ROWS + 64])
    w_ri_full = _gate_full(wa4[:, 2 * SQ_ROWS + 64:2 * SQ_ROWS + 128])
    a_dec, u_in = _rnn_gates_fwd(z, conv_w_full, conv_b, w_ra_full, w_ri_full, b_ra, b_ri, lru_lambda)
    hr, wb4 = _scan_fwd(a_dec, u_in, hook=_hook_gather(wb_own, 0.6))
    sq_w = {0: (wb4, 0), 1: (wa4, 0), 2: (wa4, 1)}

    def sq_nn(a, kk, bias, name, out_cols, out_index, carried=None):
        wp, blk = sq_w[kk]
        return _mm_nn(a, wp, bias, name=name, grid=(2, D // CW), tm=HALF_TP, tn=CW, k=D, a_index=lambda i, j: (i, 0),
                      w_block=(N_SH, SQ_ROWS, CW), w_index=lambda i, j: (0, blk, j), out_cols=out_cols,
                      out_index=out_index, carried=carried)[0]

    def sq_nt(a, a_blk, kk, name, hook=None):
        wp, blk = sq_w[kk]
        return _mm_nt(a, wp, name=name, grid=(2, N_SH, 1), tm=HALF_TP, tn=SQ_ROWS, tk=D,
                      a_index=lambda i, j, q: (i, a_blk), w_block=(None, SQ_ROWS, D),
                      w_index=lambda i, j, q: (j, blk, 0), out_cols=D, hook=hook)

    ya_in, ya_in_t = _mul_silu_fwd(hr, z, OFF_GR, name="gate_a_fwd")
    y2 = sq_nn(ya_in, 0, zero_bias, "mm_ya", 2 * D, lambda i, j: (i, j))
    yb_in, yb_in_t = _mul_silu_fwd(o, z, OFF_GA, name="gate_b_fwd")
    y2 = sq_nn(yb_in, 1, zero_bias, "mm_yb", 2 * D, lambda i, j: (i, D // CW + j), carried=y2)
    mixed, mixed_t = _merge_fwd(y2, z)
    out = sq_nn(mixed, 2, b_o, "mm_out", D, lambda i, j: (i, j))
    dr, dr16, sums_o = _ln_out_loss(h32, out, loss_target[0], ln_g, ln_b)

    def sq_tn(at, b, b_blk, kk, name, carried=None):
        return _mm_tn(at, b, name=name, grid=(N_SH, 1), tm=SQ_ROWS, tn=D, a_index=lambda i, j: (i, 0),
                      b_index=lambda i, j: (0, b_blk), out_shape=(N_SH, PACK_ROWS, D),
                      out_block=(None, SQ_ROWS, D), out_index=lambda i, j: (i, kk, 0), carried=carried)[0]

    gsq = sq_tn(mixed_t, dr16, 0, 2, "mm_dwo")
    dmix = sq_nt(dr16, 0, 2, "mm_dmix")[0]
    dy2, dz = _merge_bwd(dmix, y2, z)
    gsq = sq_tn(ya_in_t, dy2, 0, 0, "mm_dwrnn", carried=gsq)
    gsq = sq_tn(yb_in_t, dy2, 1, 1, "mm_dwattn", carried=gsq)
    dya_in = sq_nt(dy2, 0, 0, "mm_dyain")[0]
    dhr, dz = _mul_silu_bwd(dya_in, hr, z, OFF_GR, dz, name="gate_a_bwd")
    lam_s = _scan_bwd(a_dec, dhr)
    dz, gsq, sums_r = _rnn_gates_bwd(z, lam_s, hr, conv_w_full, conv_b, w_ra_full, w_ri_full, b_ra, b_ri, lru_lambda, dz, gsq)
    dyb_in, gsq, got_sq = sq_nt(dy2, 1, 1, "mm_dybin", hook=_hook_pair(gsq, True, (0, 1)))
    do, dz, gsq, got_sq = _mul_silu_bwd(dyb_in, o, z, OFF_GA, dz, name="gate_b_bwd",
                                        hook=_hook_pair(gsq, True, (2, 3), land=got_sq))
    own_sq, s16_sq = _pair_add(gsq, got_sq, pos, name="red_w_sq_add", tr=208, g_has_both_halves=True)
    dz, dk_rot, dv32, dsink, s16_sq, fin_sq, raw_sq = _attn_bwd(q, k, v, sinks[0], do, o, lse, cos_t, sin_t, dz,
                                                                hook=_hook_scatter_direct(s16_sq))
    dz = _rope_bwd_k(dk_rot, dv32, cos_t, sin_t, dz)
    comb_sq = _relay_add(s16_sq, raw_sq, pos, name="red_w_sq_relay", tr=208)

    def dwin(half_idx, name, hook, **kw):
        return _mm_tn(h16_t, dz, name=name, grid=(1, N_SH * PER_IN), tm=D // 2, tn=TN_IN,
                      a_index=lambda i, j, p: (p[half_idx], 0), b_index=lambda i, j, p: (0, j),
                      out_shape=(N_SH, D // 2, W_IN_COLS), out_block=(None, D // 2, TN_IN),
                      out_index=lambda i, j, p: (j // PER_IN, 0, j % PER_IN), prefetch=(pos,), hook=hook, **kw)

    gin_sib, g_b_in, comb_sq, fin_sq = dwin(2, "mm_dwin_sib", _hook_scatter_relay(comb_sq, fin_sq),
                                            out_dtype=BF16, colsum=True)
    red_sq = _sum_chips(own_sq, fin_sq, pos, name="red_w_sq_sum", tr=208)
    gin_own, gin_sib, got_in = dwin(0, "mm_dwin_own", _hook_pair(gin_sib, False))
    own_in, s16_in, g_sq = _pair_add(gin_own, got_in, pos, name="red_w_in_add", tr=128, g_has_both_halves=False,
                                     hook=_hook_halves(red_sq))
    dhz, s16_in, fin_in, raw_in = _mm_nt(dz, w_in4, name="mm_dhz", grid=(2, 2, N_SH), tm=HALF_TP, tn=D // 2,
                                         tk=W_IN_COLS, a_index=lambda i, j, q: (i, q),
                                         w_block=(None, D // 2, W_IN_COLS), w_index=lambda i, j, q: (q, j, 0), out_cols=D,
                                         hook=_hook_scatter_direct(s16_in))
    comb_in = _relay_add(s16_in, raw_in, pos, name="red_w_in_relay", tr=128)
    g_x, g_meta_local, sums_e, comb_in, fin_in = _ln_emb_bwd(x[0], meta_full, ln_emb_g2, dr, dhz,
                                                             hook=_hook_scatter_relay(comb_in, fin_in))
    red_in = _sum_chips(own_in, fin_in, pos, name="red_w_in_sum", tr=128)

    spack = jnp.concatenate([
        sums_e[0:1], sums_e[1:2], _row_pad(g_b_in, _B_IN_ROWS), sums_r[0:4], sums_r[4:5], sums_r[5:6], sums_r[6:7],
        sums_r[7:8], _row_pad(dsink[0:1, 0:N_Q], 1), sums_o[2:3], sums_o[0:1], sums_o[1:2], g_meta_local, sums_o[3:4],
        jnp.zeros((_SMALL_ROWS - 38, D), F32)], axis=0)
    sred, g_in = _allreduce_small(spack, red_in)

    big = {"w_in": [t.reshape(w_in.shape) for t in
                    _adamw(w_in[0], g_in, m_w_in[0], v_w_in[0], name="adamw_w_in", tr=128)]}
    for kk, (n, w_, m_, v_) in enumerate([("w_rnn_out", w_rnn_out, m_w_rnn_out, v_w_rnn_out),
                                          ("w_attn_out", w_attn_out, m_w_attn_out, v_w_attn_out), ("w_o", w_o, m_w_o, v_w_o)]):
        big[n] = [t.reshape(w_.shape) for t in
                  _adamw(w_[0], g_sq, m_[0], v_[0], name="adamw_" + n, tr=256, g_row0=SQ_ROWS * kk)]
    for kk, (n, w_, m_, v_) in enumerate([("w_ra", w_ra, m_w_ra, v_w_ra), ("w_ri", w_ri, m_w_ri, v_w_ri)]):
        big[n] = [_unpack_ra(t, w_) for t in
                  _adamw(_pack_ra(w_[0]), g_sq, _pack_ra(m_[0]), _pack_ra(v_[0]), name="adamw_" + n, tr=64,
                         g_row0=3 * SQ_ROWS + 64 * kk)]

    loss = sred[37, 0]
    col0 = shard * 512
    g_conv_w = lax.dynamic_slice(sred[9:13], (0, col0), (CONV_W, 512))
    g_meta = lax.dynamic_slice(sred[21:37], (0, col0), (N_META, 512))
    small_g = {"ln_emb_g": sred[0:1], "ln_emb_b": sred[1:2], "b_in": sred[2:9], "conv_w": g_conv_w.reshape(1, D),
               "conv_b": sred[13:14], "b_ra": sred[14:15], "b_ri": sred[15:16], "lru_lambda": sred[16:17],
               "sinks": sred[17:18], "b_o": sred[18:19], "ln_g": sred[19:20], "ln_b": sred[20:21],
               "meta_tokens": g_meta.reshape(4, D)}
    small_names = list(small_g)

    def small_pack(vals):
        rows = []
        for n in small_names:
            a = vals[n]
            if n == "b_in":
                a = _row_pad(a, _B_IN_ROWS)
            elif n == "sinks":
                a = _row_pad(a, 1)
            else:
                a = a.reshape(-1, D)
            rows.append(a)
        return jnp.concatenate(rows + [jnp.zeros((24 - 22, D), F32)], axis=0)

    w_small = dict(ln_emb_g=ln_emb_g, ln_emb_b=ln_emb_b, b_in=b_in, conv_w=conv_w, conv_b=conv_b, b_ra=b_ra, b_ri=b_ri,
                   lru_lambda=lru_lambda, sinks=sinks, b_o=b_o, ln_g=ln_g, ln_b=ln_b, meta_tokens=meta_tokens)
    m_small = dict(ln_emb_g=m_ln_emb_g, ln_emb_b=m_ln_emb_b, b_in=m_b_in, conv_w=m_conv_w, conv_b=m_conv_b, b_ra=m_b_ra,
                   b_ri=m_b_ri, lru_lambda=m_lru_lambda, sinks=m_sinks, b_o=m_b_o, ln_g=m_ln_g, ln_b=m_ln_b,
                   meta_tokens=m_meta_tokens)
    v_small = dict(ln_emb_g=v_ln_emb_g, ln_emb_b=v_ln_emb_b, b_in=v_b_in, conv_w=v_conv_w, conv_b=v_conv_b, b_ra=v_b_ra,
                   b_ri=v_b_ri, lru_lambda=v_lru_lambda, sinks=v_sinks, b_o=v_b_o, ln_g=v_ln_g, ln_b=v_ln_b,
                   meta_tokens=v_meta_tokens)
    g_small_pack = jnp.concatenate([small_g[n] for n in small_names] + [jnp.zeros((2, D), F32)], axis=0)
    small_res = _adamw(small_pack(w_small), g_small_pack, small_pack(m_small), small_pack(v_small),
                       name="adamw_small", tr=24)

    small_rows = {}
    r0 = 0
    for n in small_names:
        nrows = small_g[n].shape[0]
        small_rows[n] = (r0, nrows)
        r0 += nrows

    def small_out(packed, n, like):
        a, nrows = small_rows[n]
        flat = packed[a:a + nrows].reshape(-1)
        return flat[:like.size].reshape(like.shape)

    weights = dict(meta_tokens=meta_tokens, ln_emb_g=ln_emb_g, ln_emb_b=ln_emb_b, w_in=w_in, b_in=b_in, conv_w=conv_w,
                   conv_b=conv_b, w_ra=w_ra, b_ra=b_ra, w_ri=w_ri, b_ri=b_ri, lru_lambda=lru_lambda, sinks=sinks,
                   w_rnn_out=w_rnn_out, w_attn_out=w_attn_out, w_o=w_o, b_o=b_o, ln_g=ln_g, ln_b=ln_b)

    def outputs(which):
        return [big[n][which] if n in big else small_out(small_res[which], n, like) for n, like in weights.items()]

    return (loss, g_x[None], *outputs(0), *outputs(1), *outputs(2), *outputs(3))
```

```python
import jax
import jax.numpy as jnp
from jax import lax
from jax.experimental import pallas as pl
from jax.experimental.pallas import tpu as pltpu

F32 = jnp.float32
BF16 = jnp.bfloat16
_MXU = jnp.bfloat16

D = 2048
SEQ = 2048
N_META = 16
BLK = 128
PAD = BLK - N_META
TP = PAD + N_META + SEQ
NBLK = TP // BLK
HALF_TP = TP // 2
N_RB = 8
RB = 256
CONV_W = 4
LRU_C = 8.0
HD = 64
N_Q = 32
N_KV = 4
GRP = 8
D_KV = 256
NEG_INF = -1e30
LN_EPS = 1e-5
ALPHA = 2.0 ** 0.25
ROPE_THETA = 10000.0
OFF_GR, OFF_Q, OFF_K, OFF_V, OFF_GA, OFF_G = 2048, 4096, 6144, 6400, 6656, 8704
D_IN = 12800
N_SH = 4
W_IN_COLS = D_IN // N_SH
TN_IN = 640
PER_IN = W_IN_COLS // TN_IN
CW = 512
RT = TP // 4
SQ_ROWS = 512
PACK_ROWS = 3 * SQ_ROWS + 128

ADAM_LR = 0.001
ADAM_B1 = 0.9
ADAM_B2 = 0.999
ADAM_EPS = 1e-08
ADAM_WD = 0.01
ADAM_STEP = 10

MESH = pl.DeviceIdType.MESH
_MIB = 1024 * 1024
_ANY = pl.BlockSpec(memory_space=pl.ANY)
_NT = (((1,), (1,)), ((), ()))
_TN = (((0,), (0,)), ((), ()))


def _sds(shape, dtype):
    return jax.ShapeDtypeStruct(shape, dtype)


def _sigmoid(x):
    return 1.0 / (1.0 + jnp.exp(-x))


def _my_pos():
    return lax.axis_index("x"), lax.axis_index("y"), lax.axis_index("c")


def _other_chips(x, y):
    return [(1 - x, y), (x, 1 - y), (1 - x, 1 - y)]


def _remote(src, dst, send_sems, recv_sems, k, dev):
    return pltpu.make_async_remote_copy(src_ref=src, dst_ref=dst, send_sem=send_sems.at[k], recv_sem=recv_sems.at[k],
                                        device_id=dev, device_id_type=MESH)


class _Hook:
    def __init__(self, carried, landing, n_sems, start, finish, mid=None, mid_frac=0.5):
        self.carried, self.landing, self.n_sems, self.start, self.finish = list(carried), list(landing), n_sems, start, finish
        self.mid, self.mid_frac = mid, mid_frac


def _hook_gather(buf, mid_frac):
    half = buf.shape[1] // 2
    quarter = half // 2

    def geom(o, ss, rs):
        x, y, c = _my_pos()
        xn, yn, dg = _other_chips(x, y)
        slot = lambda p: 2 * p[0] + p[1]
        mine_rows = pl.ds(pl.multiple_of(c * half, 16), half)
        sib_rows = pl.ds(pl.multiple_of((1 - c) * half, 16), half)
        q_rows = lambda r: pl.ds(pl.multiple_of(c * half + r * quarter, 16), quarter)

        def cp(k, s, rows, dev):
            part = o.at[s, rows]
            return _remote(part, part, ss, rs, k, dev)

        return dict(
            direct=lambda k, s: cp(k, s, mine_rows, ((xn, yn)[k][0], (xn, yn)[k][1], c)),
            relay=lambda r, s: cp(2 + r, s, q_rows(r), ((yn, xn)[r][0], (yn, xn)[r][1], c)),
            sibling=lambda k, s, mine: cp(4 + k, s, mine_rows if mine else sib_rows, (x, y, 1 - c)),
            me=slot((x, y)), slots=(slot(xn), slot(yn), slot(dg)))

    def start(car, land, ss, rs):
        g = geom(car[0], ss, rs)
        g["direct"](0, g["me"]).start()
        g["direct"](1, g["me"]).start()

    def mid(car, land, ss, rs):
        g = geom(car[0], ss, rs)
        for k in range(2):
            g["direct"](k, g["slots"][k]).wait_recv()
            g["relay"](k, g["slots"][k]).start()
            g["sibling"](k, g["slots"][k], True).start()

    def finish(car, land, ss, rs):
        g = geom(car[0], ss, rs)
        dslot = g["slots"][2]
        g["relay"](0, dslot).wait_recv()
        g["relay"](1, dslot).wait_recv()
        g["sibling"](2, dslot, True).start()
        for k in range(3):
            g["sibling"](k, g["slots"][k], False).wait_recv()
        for k in range(2):
            g["direct"](k, g["me"]).wait_send()
            g["relay"](k, g["slots"][k]).wait_send()
        for k in range(3):
            g["sibling"](k, g["slots"][k], True).wait_send()

    return _Hook([buf], [], 7, start, finish, mid=mid, mid_frac=mid_frac)


def _hook_pair(g, half_rows, slots=(0, 1, 2, 3), land=None):
    n, r, cc = g.shape
    h = r // 2 if half_rows else r

    def plan(car, landing, ss, rs):
        x, y, c = _my_pos()
        dst = landing[0] if land is None else car[1]
        cps = []
        for k, s in enumerate(slots):
            src = car[0].at[s, pl.ds(pl.multiple_of((1 - c) * h, 8), h)] if half_rows else car[0].at[s]
            cps.append(_remote(src, dst.at[s], ss, rs, k, (x, y, 1 - c)))
        return cps

    def start(car, land, ss, rs):
        for cp in plan(car, land, ss, rs):
            cp.start()

    def finish(car, land, ss, rs):
        cps = plan(car, land, ss, rs)
        for cp in cps:
            cp.wait_recv()
        for cp in cps:
            cp.wait_send()

    if land is None:
        return _Hook([g], [_sds((n, h, cc), g.dtype)], len(slots), start, finish)
    return _Hook([g, land], [], len(slots), start, finish)


def _hook_scatter_direct(s16):
    _, h, cc = s16.shape
    q = h // 2

    def plan(car, land, ss, rs):
        x, y, c = _my_pos()
        xn, yn, dg = _other_chips(x, y)
        s, (final, raw) = car[0], land
        slot = lambda p: 2 * p[0] + p[1]
        q0, q1 = pl.ds(0, q), pl.ds(q, q)
        return [_remote(s.at[slot(xn), q0], final.at[0, q0], ss, rs, 0, (xn[0], xn[1], c)),
                _remote(s.at[slot(yn), q1], final.at[1, q1], ss, rs, 1, (yn[0], yn[1], c)),
                _remote(s.at[slot(dg), q0], raw.at[1], ss, rs, 2, (xn[0], xn[1], c)),
                _remote(s.at[slot(dg), q1], raw.at[0], ss, rs, 3, (yn[0], yn[1], c))]

    def start(car, land, ss, rs):
        for cp in plan(car, land, ss, rs):
            cp.start()

    def finish(car, land, ss, rs):
        cps = plan(car, land, ss, rs)
        for cp in cps:
            cp.wait_recv()
        for cp in cps:
            cp.wait_send()

    return _Hook([s16], [_sds((2, h, cc), s16.dtype), _sds((2, q, cc), s16.dtype)], 4, start, finish)


def _hook_scatter_relay(comb, final):
    _, q, _ = comb.shape

    def plan(car, ss, rs):
        x, y, c = _my_pos()
        xn, yn, _ = _other_chips(x, y)
        cb, final_ref = car
        return [_remote(cb.at[0], final_ref.at[0, pl.ds(q, q)], ss, rs, 0, (xn[0], xn[1], c)),
                _remote(cb.at[1], final_ref.at[1, pl.ds(0, q)], ss, rs, 1, (yn[0], yn[1], c))]

    def start(car, land, ss, rs):
        for cp in plan(car, ss, rs):
            cp.start()

    def finish(car, land, ss, rs):
        cps = plan(car, ss, rs)
        for cp in cps:
            cp.wait_recv()
        for cp in cps:
            cp.wait_send()

    return _Hook([comb, final], [], 2, start, finish)


def _hook_halves(full):
    h = full.shape[0] // 2

    def half_copy(car, ss, rs, which):
        x, y, c = _my_pos()
        rows = car[0].at[pl.ds(pl.multiple_of((c + which - 2 * c * which) * h, 8), h)]
        return _remote(rows, rows, ss, rs, 0, (x, y, 1 - c))

    def start(car, land, ss, rs):
        half_copy(car, ss, rs, 0).start()

    def finish(car, land, ss, rs):
        half_copy(car, ss, rs, 1).wait_recv()
        half_copy(car, ss, rs, 0).wait_send()

    return _Hook([full], [], 1, start, finish)


def _pcall(body, *, name, grid, in_specs, out_specs, out_shape, operands, scratch=(), vmem=48, sem=None,
           prefetch=(), aliases=None, hook=None):
    n_pre, n_in, n_out, n_scr = len(prefetch), len(in_specs), len(out_specs), len(scratch)
    in_specs, out_specs, out_shape, scratch = list(in_specs), list(out_specs), list(out_shape), list(scratch)
    io_alias = {n_pre + a: b for a, b in (aliases or {}).items()}
    operands = list(operands)
    kernel_body = body
    if hook is not None:
        n_car, n_land = len(hook.carried), len(hook.landing)
        for t, arr in enumerate(hook.carried):
            io_alias[n_pre + n_in + t] = n_out + t
        in_specs += [_ANY] * n_car
        out_specs += [_ANY] * (n_car + n_land)
        out_shape += [_sds(a.shape, a.dtype) for a in hook.carried] + hook.landing
        scratch += [pltpu.SemaphoreType.DMA((hook.n_sems,)), pltpu.SemaphoreType.DMA((hook.n_sems,))]
        operands += hook.carried
        sem = ("arbitrary",) * len(grid)

        def kernel_body(*refs):
            pre, rest = refs[:n_pre], refs[n_pre:]
            ins = rest[:n_in]
            outs = rest[n_in + n_car:n_in + n_car + n_out]
            car = rest[n_in + n_car + n_out:n_in + 2 * n_car + n_out]
            land = rest[n_in + 2 * n_car + n_out:n_in + 2 * n_car + n_out + n_land]
            scr = rest[n_in + 2 * n_car + n_out + n_land:]
            send_sems, recv_sems = scr[n_scr], scr[n_scr + 1]
            first = pl.program_id(0) == 0
            last = pl.program_id(0) == grid[0] - 1
            for d in range(1, len(grid)):
                first = first & (pl.program_id(d) == 0)
                last = last & (pl.program_id(d) == grid[d] - 1)

            @pl.when(first)
            def _():
                hook.start(car, land, send_sems, recv_sems)

            if hook.mid is not None:
                step = pl.program_id(0)
                total = grid[0]
                for d in range(1, len(grid)):
                    step = step * grid[d] + pl.program_id(d)
                    total *= grid[d]

                @pl.when(step == int(total * hook.mid_frac))
                def _():
                    hook.mid(car, land, send_sems, recv_sems)

            body(*pre, *ins, *outs, *scr[:n_scr])

            @pl.when(last)
            def _():
                hook.finish(car, land, send_sems, recv_sems)

    params = pltpu.CompilerParams(vmem_limit_bytes=vmem * _MIB, dimension_semantics=sem,
                                  has_side_effects=hook is not None)
    if n_pre:
        call = pl.pallas_call(
            kernel_body, name=name, out_shape=out_shape, input_output_aliases=io_alias, compiler_params=params,
            grid_spec=pltpu.PrefetchScalarGridSpec(num_scalar_prefetch=n_pre, grid=grid, in_specs=in_specs,
                                                   out_specs=out_specs, scratch_shapes=scratch))
    else:
        call = pl.pallas_call(kernel_body, name=name, grid=grid, in_specs=in_specs, out_specs=out_specs,
                              out_shape=out_shape, scratch_shapes=scratch, input_output_aliases=io_alias,
                              compiler_params=params)
    return call(*prefetch, *operands)


def _gather_small(blk):
    r, cc = blk.shape

    def body(x_ref, o_ref, send_sems, recv_sems):
        x, y, c = _my_pos()
        me = 2 * x + y
        o_ref[me] = x_ref[...]
        sends = [_remote(x_ref, o_ref.at[me], send_sems, recv_sems, k, (px, py, c))
                 for k, (px, py) in enumerate(_other_chips(x, y))]
        for cp in sends:
            cp.start()
        for k, (px, py) in enumerate(_other_chips(x, y)):
            _remote(x_ref, o_ref.at[2 * px + py], send_sems, recv_sems, k, (px, py, c)).wait_recv()
        for cp in sends:
            cp.wait_send()

    vm = pl.BlockSpec(memory_space=pltpu.VMEM)
    return pl.pallas_call(
        body, name="gather_small", in_specs=[vm], out_specs=vm, out_shape=_sds((N_SH, r, cc), blk.dtype),
        scratch_shapes=[pltpu.SemaphoreType.DMA((3,)), pltpu.SemaphoreType.DMA((3,))],
        compiler_params=pltpu.CompilerParams(has_side_effects=True),
    )(blk)


N_DEV = 8


def _allreduce_small(pack, halves):
    r, cc = pack.shape
    ride = _hook_halves(halves)

    def body(x_ref, h_in_ref, o_ref, h_ref, buf_ref, send_sems, recv_sems, h_ss, h_rs):
        del h_in_ref
        ride.start([h_ref], [], h_ss, h_rs)
        x, y, c = _my_pos()
        me = 4 * x + 2 * y + c
        buf_ref[me] = x_ref[...]
        cps = []
        for k in range(1, N_DEV):
            peer = (x ^ ((k >> 2) & 1), y ^ ((k >> 1) & 1), c ^ (k & 1))
            cps.append(_remote(x_ref, buf_ref.at[me], send_sems, recv_sems, k - 1, peer))
        for cp in cps:
            cp.start()
        for k in range(1, N_DEV):
            peer = (x ^ ((k >> 2) & 1), y ^ ((k >> 1) & 1), c ^ (k & 1))
            src = 4 * peer[0] + 2 * peer[1] + peer[2]
            _remote(x_ref, buf_ref.at[src], send_sems, recv_sems, k - 1, peer).wait_recv()
        acc = buf_ref[0]
        for d in range(1, N_DEV):
            acc = acc + buf_ref[d]
        o_ref[...] = acc
        for cp in cps:
            cp.wait_send()
        ride.finish([h_ref], [], h_ss, h_rs)

    vm = pl.BlockSpec(memory_space=pltpu.VMEM)
    return pl.pallas_call(
        body, name="allreduce_small", in_specs=[vm, _ANY], out_specs=[vm, _ANY],
        out_shape=[_sds((r, cc), F32), _sds(halves.shape, halves.dtype)], input_output_aliases={1: 1},
        scratch_shapes=[pltpu.VMEM((N_DEV, r, cc), F32), pltpu.SemaphoreType.DMA((N_DEV - 1,)),
                        pltpu.SemaphoreType.DMA((N_DEV - 1,)), pltpu.SemaphoreType.DMA((1,)), pltpu.SemaphoreType.DMA((1,))],
        compiler_params=pltpu.CompilerParams(has_side_effects=True),
    )(pack, halves)


def _mm_nn(a, w, bias, *, name, grid, tm, tn, k, a_index, w_block, w_index, out_cols, out_index, carried=None, hook=None):
    m = a.shape[0]

    def body(a_ref, w_ref, b_ref, *rest):
        o_ref = rest[-1]
        wv = w_ref[...]
        acc = jnp.dot(a_ref[...].astype(_MXU), wv.reshape(k, tn).astype(_MXU), preferred_element_type=F32)
        o_ref[...] = acc + b_ref[...]

    operands = [a, w, bias] + ([carried] if carried is not None else [])
    return _pcall(
        body, name=name, grid=grid,
        in_specs=[pl.BlockSpec((tm, k), a_index), pl.BlockSpec(w_block, w_index),
                  pl.BlockSpec((1, tn), lambda i, j: (0, j))] + ([_ANY] if carried is not None else []),
        out_specs=[pl.BlockSpec((tm, tn), out_index)], out_shape=[_sds((m, out_cols), F32)], operands=operands,
        vmem=56, sem=("parallel", "parallel"), aliases={3: 0} if carried is not None else None, hook=hook)


def _mm_nt(a, w, *, name, grid, tm, tn, tk, a_index, w_block, w_index, out_cols, hook=None):
    m = a.shape[0]
    nk = grid[2]

    def body(a_ref, w_ref, o_ref, acc_ref):
        part = lax.dot_general(a_ref[...].astype(_MXU), w_ref[...].astype(_MXU), _NT, preferred_element_type=F32)
        if nk == 1:
            o_ref[...] = part
        else:
            kidx = pl.program_id(2)

            @pl.when(kidx == 0)
            def _():
                acc_ref[...] = part

            @pl.when(kidx > 0)
            def _():
                acc_ref[...] += part

            @pl.when(kidx == nk - 1)
            def _():
                o_ref[...] = acc_ref[...]

    return _pcall(
        body, name=name, grid=grid, in_specs=[pl.BlockSpec((tm, tk), a_index), pl.BlockSpec(w_block, w_index)],
        out_specs=[pl.BlockSpec((tm, tn), lambda i, j, q: (i, j))], out_shape=[_sds((m, out_cols), F32)],
        operands=[a, w], scratch=[pltpu.VMEM((tm, tn) if nk > 1 else (8, 128), F32)], vmem=60,
        sem=("parallel", "parallel", "arbitrary"), hook=hook)


def _mm_tn(at, b, *, name, grid, tm, tn, a_index, b_index, out_shape, out_block, out_index, carried=None,
           prefetch=(), hook=None, out_dtype=F32, colsum=False):
    t = at.shape[1]
    n_pre = len(prefetch)

    def body(*refs):
        a_ref, b_ref = refs[n_pre], refs[n_pre + 1]
        bv = b_ref[...]
        o_ref = refs[-2] if colsum else refs[-1]
        o_ref[...] = jnp.dot(a_ref[...].astype(_MXU), bv.astype(_MXU), preferred_element_type=F32).astype(out_dtype)
        if colsum:
            refs[-1][...] = jnp.sum(bv.astype(F32), axis=0, keepdims=True)

    operands = [at, b] + ([carried] if carried is not None else [])
    out_specs = [pl.BlockSpec(out_block, out_index)]
    out_shapes = [_sds(out_shape, out_dtype)]
    if colsum:
        out_specs.append(pl.BlockSpec((1, tn), b_index))
        out_shapes.append(_sds((1, b.shape[1]), F32))
    return _pcall(
        body, name=name, grid=grid,
        in_specs=[pl.BlockSpec((tm, t), a_index), pl.BlockSpec((t, tn), b_index)] + ([_ANY] if carried is not None else []),
        out_specs=out_specs, out_shape=out_shapes, operands=operands,
        vmem=56, sem=("parallel", "parallel"), aliases={2: 0} if carried is not None else None, prefetch=prefetch, hook=hook)


def _remote_tile(n):
    near = 2 * PER_IN
    if isinstance(n, int):
        return (n % 2, n // 2) if n < near else (2, n - near)
    return jnp.where(n < near, n % 2, 2), jnp.where(n < near, n // 2, n - near)


def _mm_z_gather(h16, w_own, b_in, order):
    n_tiles = N_SH * PER_IN
    n_remote = 3 * PER_IN
    half = D // 2

    def body(order_ref, a_ref, b_ref, w_in_ref, z_ref, w_ref, wbuf, tile_sems, ss, rs):
        del w_in_ref
        j = pl.program_id(0)
        x, y, c = _my_pos()
        me = 2 * x + y
        chips = _other_chips(x, y)
        mine_rows = pl.ds(pl.multiple_of(c * half, 16), half)
        sib_rows = pl.ds(pl.multiple_of((1 - c) * half, 16), half)

        slots = [2 * px + py for px, py in chips]
        cols = lambda t: pl.ds(t * TN_IN, TN_IN)
        q_rows = lambda r: pl.ds(pl.multiple_of(c * half + r * (half // 2), 16), half // 2)

        def direct(rel, t, slot):
            px, py = chips[rel]
            part = w_ref.at[slot, mine_rows, cols(t)]
            return _remote(part, part, ss, rs, 2 * t + rel, (px, py, c))

        def relay(r, t, slot):
            px, py = chips[1 - r]
            part = w_ref.at[slot, q_rows(r), cols(t)]
            return _remote(part, part, ss, rs, 2 * PER_IN + 2 * t + r, (px, py, c))

        def d2d(n, rows):
            rel, t = _remote_tile(n)
            part = w_ref.at[slots[rel], rows, cols(t)]
            return _remote(part, part, ss, rs, 4 * PER_IN + n, (x, y, 1 - c))

        def tile_copy(step):
            rel, t = _remote_tile(jnp.maximum(step - PER_IN, 0))
            slot = jnp.where(step < PER_IN, me, me ^ jnp.where(rel == 0, 2, jnp.where(rel == 1, 1, 3)))
            col = pl.multiple_of(jnp.where(step < PER_IN, step, t) * TN_IN, 128)
            return pltpu.make_async_copy(w_ref.at[slot, :, pl.ds(col, TN_IN)], wbuf.at[step % 2], tile_sems.at[step % 2])

        @pl.when(j == 0)
        def _():
            for t in range(PER_IN):
                direct(0, t, me).start()
                direct(1, t, me).start()
            tile_copy(0).start()

        for n in range(n_remote):
            rel, t = _remote_tile(n)

            @pl.when(j == n + PER_IN - 3)
            def _():
                if rel < 2:
                    direct(rel, t, slots[rel]).wait_recv()
                    relay(rel, t, slots[rel]).start()
                else:
                    relay(0, t, slots[2]).wait_recv()
                    relay(1, t, slots[2]).wait_recv()
                d2d(n, mine_rows).start()

            @pl.when(j == n + PER_IN - 2)
            def _():
                d2d(n, sib_rows).wait_recv()

        @pl.when(j + 1 < n_tiles)
        def _():
            tile_copy(j + 1).start()

        tile_copy(j).wait()
        z_ref[...] = jnp.dot(a_ref[...], wbuf[j % 2], preferred_element_type=F32) + b_ref[...]

        @pl.when(j == n_tiles - 1)
        def _():
            for t in range(PER_IN):
                for r in range(2):
                    direct(r, t, me).wait_send()
                    relay(r, t, slots[r]).wait_send()
            for n in range(n_remote):
                d2d(n, mine_rows).wait_send()

    def col_tile(j, o):
        rel, t = _remote_tile(jnp.maximum(j - PER_IN, 0))
        return 0, jnp.where(j < PER_IN, o[0] * PER_IN + j, o[1 + rel] * PER_IN + t)

    return pl.pallas_call(
        body, name="mm_z_gather",
        grid_spec=pltpu.PrefetchScalarGridSpec(
            num_scalar_prefetch=1, grid=(n_tiles,),
            in_specs=[pl.BlockSpec((TP, D), lambda j, o: (0, 0)), pl.BlockSpec((1, TN_IN), col_tile), _ANY],
            out_specs=[pl.BlockSpec((TP, TN_IN), col_tile), _ANY],
            scratch_shapes=[pltpu.VMEM((2, D, TN_IN), BF16), pltpu.SemaphoreType.DMA((2,)),
                            pltpu.SemaphoreType.DMA((4 * PER_IN + n_remote,)),
                            pltpu.SemaphoreType.DMA((4 * PER_IN + n_remote,))]),
        out_shape=[_sds((TP, D_IN), F32), _sds(w_own.shape, w_own.dtype)],
        input_output_aliases={3: 1},
        compiler_params=pltpu.CompilerParams(vmem_limit_bytes=60 * _MIB, dimension_semantics=("arbitrary",),
                                             has_side_effects=True),
    )(order, h16, b_in, w_own)


def _padded_rows(i, x_ref, meta_ref):
    head = jnp.concatenate([jnp.zeros((PAD, D), F32), meta_ref[...]], axis=0)
    return jnp.where(i == 0, head, x_ref[...])


def _stream_specs():
    return [pl.BlockSpec((BLK, D), lambda i: (jnp.maximum(i - 1, 0), 0)), pl.BlockSpec((N_META, D), lambda i: (0, 0))]


def _ln_emb_fwd(x, meta, g, b):
    def body(x_ref, meta_ref, g_ref, b_ref, h32_ref, h16_ref, h16t_ref):
        x = _padded_rows(pl.program_id(0), x_ref, meta_ref)
        mu = jnp.mean(x, axis=-1, keepdims=True)
        xc = x - mu
        var = jnp.mean(xc * xc, axis=-1, keepdims=True)
        y = xc * lax.rsqrt(var + LN_EPS) * g_ref[...] + b_ref[...]
        h32_ref[...] = y
        h16_ref[...] = y.astype(_MXU)
        h16t_ref[...] = y.T.astype(_MXU)

    row = pl.BlockSpec((BLK, D), lambda i: (i, 0))
    vec = pl.BlockSpec((1, D), lambda i: (0, 0))
    return _pcall(body, name="ln_emb_fwd", grid=(NBLK,), in_specs=_stream_specs() + [vec, vec],
                  out_specs=[row, row, pl.BlockSpec((D, BLK), lambda i: (0, i))],
                  out_shape=[_sds((TP, D), F32), _sds((TP, D), _MXU), _sds((D, TP), _MXU)], operands=[x, meta, g, b],
                  vmem=32, sem=("parallel",))


def _adamw_step(w_ref, g_ref, m_ref, v_ref, go_ref, d_ref, mo_ref, vo_ref):
    gg = g_ref[...]
    go_ref[...] = gg
    m_new = ADAM_B1 * m_ref[...] + (1.0 - ADAM_B1) * gg
    v_new = ADAM_B2 * v_ref[...] + (1.0 - ADAM_B2) * (gg * gg)
    m_hat = m_new / (1.0 - ADAM_B1 ** ADAM_STEP)
    v_hat = v_new / (1.0 - ADAM_B2 ** ADAM_STEP)
    d_ref[...] = -ADAM_LR * (m_hat / (jnp.sqrt(v_hat) + ADAM_EPS) + ADAM_WD * w_ref[...])
    mo_ref[...] = m_new
    vo_ref[...] = v_new


def _ln_emb_bwd(x, meta, g, dr, dhz, g_sq, sq, *, hook):
    n_sq = len(sq)
    sq_steps = 16
    sq_rows = SQ_ROWS // sq_steps

    def body(x_ref, meta_ref, g_ref, dr_ref, dhz_ref, *rest):
        sq_in = rest[:4 * n_sq]
        dx_ref, dmeta_ref, acc_ref = rest[4 * n_sq:4 * n_sq + 3]
        sq_out = rest[4 * n_sq + 3:]
        i = pl.program_id(0)

        @pl.when(i < sq_steps)
        def _():
            for kk in range(n_sq):
                _adamw_step(*sq_in[4 * kk:4 * kk + 4], *sq_out[4 * kk:4 * kk + 4])


        @pl.when(i == 0)
        def _():
            acc_ref[...] = jnp.zeros_like(acc_ref)

        x = _padded_rows(i, x_ref, meta_ref)
        mu = jnp.mean(x, axis=-1, keepdims=True)
        xc = x - mu
        var = jnp.mean(xc * xc, axis=-1, keepdims=True)
        rstd = lax.rsqrt(var + LN_EPS)
        xhat = xc * rstd
        dh = ALPHA * dr_ref[...] + dhz_ref[...]
        acc_ref[0:1, :] += jnp.sum(dh * xhat, axis=0, keepdims=True)
        acc_ref[1:2, :] += jnp.sum(dh, axis=0, keepdims=True)
        dxh = dh * g_ref[...]
        m1 = jnp.mean(dxh, axis=-1, keepdims=True)
        m2 = jnp.mean(dxh * xhat, axis=-1, keepdims=True)
        dx = rstd * (dxh - m1 - xhat * m2)
        dx_ref[...] = dx

        @pl.when(i == 0)
        def _():
            dmeta_ref[...] = dx[PAD:BLK]

    row = pl.BlockSpec((BLK, D), lambda i: (i, 0))
    vec = pl.BlockSpec((1, D), lambda i: (0, 0))
    xs, ms = _stream_specs()
    sq_blk = pl.BlockSpec((sq_rows, D), lambda i: (jnp.minimum(i, sq_steps - 1), 0))
    sq_specs, sq_operands = [], []
    for kk, (w_, m_, v_) in enumerate(sq):
        g_blk = pl.BlockSpec((sq_rows, D), lambda i, kk=kk: (sq_steps * kk + jnp.minimum(i, sq_steps - 1), 0))
        sq_specs += [sq_blk, g_blk, sq_blk, sq_blk]
        sq_operands += [w_, g_sq, m_, v_]
    return _pcall(body, name="ln_emb_bwd", grid=(NBLK,), in_specs=[xs, ms, vec, row, row] + sq_specs,
                  out_specs=[xs, ms, pl.BlockSpec((8, D), lambda i: (0, 0))] + [sq_blk] * (4 * n_sq),
                  out_shape=[_sds((SEQ, D), F32), _sds((N_META, D), F32), _sds((8, D), F32)]
                            + [_sds((SQ_ROWS, D), F32)] * (4 * n_sq),
                  operands=[x, meta, g, dr, dhz] + sq_operands, vmem=40, sem=("arbitrary",), hook=hook)


def _mul_silu_fwd(a, z, off, *, name):
    def body(a_ref, z_ref, o_ref, t_ref):
        zz = z_ref[...]
        y = a_ref[...] * (zz * _sigmoid(zz))
        o_ref[...] = y.astype(_MXU)
        t_ref[...] = y.T.astype(_MXU)

    strip = pl.BlockSpec((TP, CW), lambda j: (0, j))
    return _pcall(body, name=name, grid=(D // CW,),
                  in_specs=[strip, pl.BlockSpec((TP, CW), lambda j: (0, off // CW + j))],
                  out_specs=[strip, pl.BlockSpec((CW, TP), lambda j: (j, 0))],
                  out_shape=[_sds((TP, D), _MXU), _sds((D, TP), _MXU)], operands=[a, z], vmem=56, sem=("parallel",))


def _mul_silu_bwd(dy, a, z, off, dz, *, name, hook=None):
    def body(dy_ref, a_ref, z_ref, dz_in, da_ref, dg_ref):
        zz = z_ref[...]
        sg = _sigmoid(zz)
        d = dy_ref[...]
        da_ref[...] = d * (zz * sg)
        dg_ref[...] = (d * a_ref[...] * (sg * (1.0 + zz * (1.0 - sg)))).astype(_MXU)

    blk = pl.BlockSpec((RT, CW), lambda i, j: (i, j))
    zblk = pl.BlockSpec((RT, CW), lambda i, j: (i, off // CW + j))
    return _pcall(body, name=name, grid=(TP // RT, D // CW), in_specs=[blk, blk, zblk, _ANY], out_specs=[blk, zblk],
                  out_shape=[_sds((TP, D), F32), _sds((TP, D_IN), _MXU)], operands=[dy, a, z, dz], vmem=32,
                  sem=("parallel", "parallel"), aliases={3: 1}, hook=hook)


def _merge_fwd(y2, z):
    w = 256

    def body(ya_ref, yb_ref, ga_ref, gb_ref, o_ref, t_ref):
        y = _sigmoid(ga_ref[...]) * ya_ref[...] + _sigmoid(gb_ref[...]) * yb_ref[...]
        o_ref[...] = y.astype(_MXU)
        t_ref[...] = y.T.astype(_MXU)

    nb = D // w
    strip = pl.BlockSpec((TP, w), lambda j: (0, j))
    return _pcall(body, name="merge_fwd", grid=(nb,),
                  in_specs=[strip, pl.BlockSpec((TP, w), lambda j: (0, nb + j)),
                            pl.BlockSpec((TP, w), lambda j: (0, OFF_G // w + j)),
                            pl.BlockSpec((TP, w), lambda j: (0, (OFF_G + D) // w + j))],
                  out_specs=[strip, pl.BlockSpec((w, TP), lambda j: (j, 0))],
                  out_shape=[_sds((TP, D), _MXU), _sds((D, TP), _MXU)], operands=[y2, y2, z, z], vmem=56,
                  sem=("parallel",))


def _merge_bwd(dmix, y2, z):
    nb = D // CW

    def body(dm_ref, y_ref, g_ref, dy_ref, dg_ref):
        dm = dm_ref[...]
        sg = _sigmoid(g_ref[...])
        dy_ref[...] = (dm * sg).astype(_MXU)
        dg_ref[...] = (dm * y_ref[...] * sg * (1.0 - sg)).astype(_MXU)

    blk = pl.BlockSpec((RT, CW), lambda i, j: (i, j))
    gblk = pl.BlockSpec((RT, CW), lambda i, j: (i, OFF_G // CW + j))
    return _pcall(body, name="merge_bwd", grid=(TP // RT, 2 * nb),
                  in_specs=[pl.BlockSpec((RT, CW), lambda i, j: (i, j % nb)), blk, gblk], out_specs=[blk, gblk],
                  out_shape=[_sds((TP, 2 * D), _MXU), _sds((TP, D_IN), _MXU)], operands=[dmix, y2, z], vmem=32,
                  sem=("parallel", "parallel"))


def _ln_out_loss(h32, out, target, g, b):
    def body(h_ref, o_ref, t_ref, g_ref, b_ref, dr_ref, dr16_ref, acc_ref):
        i = pl.program_id(0)

        @pl.when(i == 0)
        def _():
            acc_ref[...] = jnp.zeros_like(acc_ref)

        r = ALPHA * h_ref[...] + o_ref[...]
        mu = jnp.mean(r, axis=-1, keepdims=True)
        rc = r - mu
        var = jnp.mean(rc * rc, axis=-1, keepdims=True)
        rstd = lax.rsqrt(var + LN_EPS)
        xhat = rc * rstd
        gg = g_ref[...]
        y = xhat * gg + b_ref[...]
        real = (i >= 1).astype(F32)
        diff = (y - t_ref[...]) * real
        dy = diff * (1.0 / D)
        dxh = dy * gg
        m1 = jnp.mean(dxh, axis=-1, keepdims=True)
        m2 = jnp.mean(dxh * xhat, axis=-1, keepdims=True)
        dr = rstd * (dxh - m1 - xhat * m2)
        dr_ref[...] = dr
        dr16_ref[...] = dr.astype(_MXU)
        acc_ref[0:1, :] += jnp.sum(dy * xhat, axis=0, keepdims=True)
        acc_ref[1:2, :] += jnp.sum(dy, axis=0, keepdims=True)
        acc_ref[2:3, :] += jnp.sum(dr, axis=0, keepdims=True)
        acc_ref[3:4, :] += (0.5 / D) * jnp.sum(diff * diff)

    row = pl.BlockSpec((BLK, D), lambda i: (i, 0))
    vec = pl.BlockSpec((1, D), lambda i: (0, 0))
    return _pcall(body, name="ln_out_loss", grid=(NBLK,),
                  in_specs=[row, row, pl.BlockSpec((BLK, D), lambda i: (jnp.maximum(i - 1, 0), 0)), vec, vec],
                  out_specs=[row, row, pl.BlockSpec((8, D), lambda i: (0, 0))],
                  out_shape=[_sds((TP, D), F32), _sds((TP, D), _MXU), _sds((8, D), F32)],
                  operands=[h32, out, target, g, b], vmem=32, sem=("arbitrary",))


def _rnn_recompute(xr_ref, cw_ref, cb_ref, wra_ref, wri_ref, bra_ref, bri_ref, lam_ref):
    rows = lax.broadcasted_iota(jnp.int32, (TP, 1), 0)
    valid = (rows >= PAD).astype(F32)
    first = rows == PAD
    x = xr_ref[...] * valid
    cw = cw_ref[...]
    shifted = [x, pltpu.roll(x, 1, 0), pltpu.roll(x, 2, 0), pltpu.roll(x, 3, 0)]
    c = cb_ref[...] + cw[0:1, :] * shifted[0] + cw[1:2, :] * shifted[1] + cw[2:3, :] * shifted[2] + cw[3:4, :] * shifted[3]
    cm = c.astype(_MXU)
    gr = _sigmoid(jnp.dot(cm, wra_ref[...].astype(_MXU), preferred_element_type=F32) + bra_ref[...])
    gi = _sigmoid(jnp.dot(cm, wri_ref[...].astype(_MXU), preferred_element_type=F32) + bri_ref[...])
    lam = lam_ref[...]
    ls = jnp.minimum(lam, 0.0) - jnp.log(1.0 + jnp.exp(-jnp.abs(lam)))
    log_a = LRU_C * gr * ls
    a = jnp.exp(log_a)
    mult = jnp.where(first, 1.0, jnp.sqrt(1.0 - jnp.exp(2.0 * log_a)))
    return dict(valid=valid, first=first, shifted=shifted, c=c, cm=cm, gr=gr, gi=gi, ls=ls, a=a, mult=mult, lam=lam)


def _rnn_specs():
    col = pl.BlockSpec((TP, RB), lambda n: (0, n))
    vec = pl.BlockSpec((1, RB), lambda n: (0, n))
    return dict(col=col, vec=vec, cw=pl.BlockSpec((CONV_W, RB), lambda n: (0, n)),
                wblk=pl.BlockSpec((None, RB, RB), lambda n: (n, 0, 0)))


def _rnn_gates_fwd(z, conv_w, conv_b, w_ra, w_ri, b_ra, b_ri, lam):
    def body(xr_ref, cw_ref, cb_ref, wra_ref, wri_ref, bra_ref, bri_ref, lam_ref, a_ref, u_ref):
        r = _rnn_recompute(xr_ref, cw_ref, cb_ref, wra_ref, wri_ref, bra_ref, bri_ref, lam_ref)
        a_ref[...] = r["a"]
        u_ref[...] = r["mult"] * r["gi"] * r["c"] * r["valid"]

    s = _rnn_specs()
    return _pcall(body, name="rnn_gates_fwd", grid=(N_RB,),
                  in_specs=[s["col"], s["cw"], s["vec"], s["wblk"], s["wblk"], s["vec"], s["vec"], s["vec"]],
                  out_specs=[s["col"], s["col"]], out_shape=[_sds((TP, D), F32)] * 2,
                  operands=[z, conv_w, conv_b, w_ra, w_ri, b_ra, b_ri, lam], vmem=56, sem=("parallel",))


SCAN_ROWS = 272


SUB = 8


def _tile_scan(a, u, reverse):
    rows = lax.broadcasted_iota(jnp.int32, a.shape, 0)
    for d in (1, 2, 4):
        shift = SUB - d if reverse else d
        inside = (rows < SUB - d) if reverse else (rows >= d)
        u = u + a * jnp.where(inside, pltpu.roll(u, shift, 0), 0.0)
        a = a * jnp.where(inside, pltpu.roll(a, shift, 0), 1.0)
    return a, u


def _scan_fwd(a, u, *, hook):
    def body(a_ref, u_ref, h_ref, carry_ref):
        @pl.when(pl.program_id(1) == 0)
        def _():
            carry_ref[...] = jnp.zeros_like(carry_ref)

        def step(r, h):
            rows = pl.ds(pl.multiple_of(r * SUB, SUB), SUB)
            prod, part = _tile_scan(a_ref[rows, :], u_ref[rows, :], False)
            ht = part + prod * h
            h_ref[rows, :] = ht
            return ht[SUB - 1:SUB, :]

        carry_ref[...] = lax.fori_loop(0, SCAN_ROWS // SUB, step, carry_ref[...], unroll=2)

    blk = pl.BlockSpec((SCAN_ROWS, CW), lambda j, i: (i, j))
    return _pcall(body, name="scan_fwd", grid=(D // CW, TP // SCAN_ROWS), in_specs=[blk, blk], out_specs=[blk],
                  out_shape=[_sds((TP, D), F32)], operands=[a, u], scratch=[pltpu.VMEM((1, CW), F32)], vmem=32,
                  sem=("parallel", "arbitrary"), hook=hook)


def _scan_bwd(a, dh):
    nst = TP // SCAN_ROWS
    n_tiles = SCAN_ROWS // SUB

    def body(a_ref, d_ref, o_ref, lam_ref, anext_ref):
        @pl.when(pl.program_id(1) == 0)
        def _():
            lam_ref[...] = jnp.zeros_like(lam_ref)
            anext_ref[...] = jnp.zeros_like(anext_ref)

        def step(q, carry):
            lam_next, a_next = carry
            rows = pl.ds(pl.multiple_of((n_tiles - 1 - q) * SUB, SUB), SUB)
            at = a_ref[rows, :]
            last = lax.broadcasted_iota(jnp.int32, at.shape, 0) == SUB - 1
            b = jnp.where(last, a_next, pltpu.roll(at, SUB - 1, 0))
            prod, part = _tile_scan(b, d_ref[rows, :], True)
            lam = part + prod * lam_next
            o_ref[rows, :] = lam
            return lam[0:1, :], at[0:1, :]

        lam, an = lax.fori_loop(0, n_tiles, step, (lam_ref[...], anext_ref[...]), unroll=2)
        lam_ref[...] = lam
        anext_ref[...] = an

    blk = pl.BlockSpec((SCAN_ROWS, CW), lambda j, i: (nst - 1 - i, j))
    return _pcall(body, name="scan_bwd", grid=(D // CW, nst), in_specs=[blk, blk], out_specs=[blk],
                  out_shape=[_sds((TP, D), F32)], operands=[a, dh],
                  scratch=[pltpu.VMEM((1, CW), F32), pltpu.VMEM((1, CW), F32)], vmem=32,
                  sem=("parallel", "arbitrary"))[0]


def _rnn_gates_bwd(z, lam_s, hr, conv_w, conv_b, w_ra, w_ri, b_ra, b_ri, lam, dz, gsq):
    def body(xr_ref, ls_ref, hr_ref, cw_ref, cb_ref, wra_ref, wri_ref, bra_ref, bri_ref, lam_ref, dz_in, gsq_in,
             dx_ref, dw_ref, sums_ref):
        r = _rnn_recompute(xr_ref, cw_ref, cb_ref, wra_ref, wri_ref, bra_ref, bri_ref, lam_ref)
        valid, c, gr, gi, a, mult = r["valid"], r["c"], r["gr"], r["gi"], r["a"], r["mult"]
        du = ls_ref[...] * valid
        da = du * pltpu.roll(hr_ref[...], 1, 0)
        d_gi = du * mult * c
        dc = du * mult * gi
        dmult = du * gi * c
        dlog_a = da * a + jnp.where(r["first"], 0.0, -dmult * a * a / mult)
        d_gr = dlog_a * (LRU_C * r["ls"])
        dls = jnp.sum(dlog_a * (LRU_C * gr), axis=0, keepdims=True)
        dpre_r = d_gr * gr * (1.0 - gr)
        dpre_i = d_gi * gi * (1.0 - gi)
        pr = dpre_r.astype(_MXU)
        pi = dpre_i.astype(_MXU)
        dwra = lax.dot_general(r["cm"], pr, _TN, preferred_element_type=F32)
        dwri = lax.dot_general(r["cm"], pi, _TN, preferred_element_type=F32)
        for s in range(N_SH):
            dw_ref[s, 0:64, :] = dwra[64 * s:64 * (s + 1)]
            dw_ref[s, 64:128, :] = dwri[64 * s:64 * (s + 1)]
        dc = dc + lax.dot_general(pr, wra_ref[...].astype(_MXU), _NT, preferred_element_type=F32)
        dc = dc + lax.dot_general(pi, wri_ref[...].astype(_MXU), _NT, preferred_element_type=F32)
        cw = cw_ref[...]
        dx = cw[0:1, :] * dc
        for k in range(1, CONV_W):
            dx = dx + cw[k:k + 1, :] * pltpu.roll(dc, TP - k, 0)
        dx_ref[...] = (dx * valid).astype(_MXU)
        for k in range(CONV_W):
            sums_ref[k:k + 1, :] = jnp.sum(dc * r["shifted"][k], axis=0, keepdims=True)
        sums_ref[4:5, :] = jnp.sum(dc, axis=0, keepdims=True)
        sums_ref[5:6, :] = jnp.sum(dpre_r, axis=0, keepdims=True)
        sums_ref[6:7, :] = jnp.sum(dpre_i, axis=0, keepdims=True)
        sums_ref[7:8, :] = dls * _sigmoid(-r["lam"])

    s = _rnn_specs()
    return _pcall(
        body, name="rnn_gates_bwd", grid=(N_RB,),
        in_specs=[s["col"], s["col"], s["col"], s["cw"], s["vec"], s["wblk"], s["wblk"], s["vec"], s["vec"], s["vec"],
                  _ANY, _ANY],
        out_specs=[s["col"], pl.BlockSpec((N_SH, 128, RB), lambda n: (0, 3 * SQ_ROWS // 128, n)),
                   pl.BlockSpec((8, RB), lambda n: (0, n))],
        out_shape=[_sds((TP, D_IN), _MXU), _sds((N_SH, PACK_ROWS, D), F32), _sds((8, D), F32)],
        operands=[z, lam_s, hr, conv_w, conv_b, w_ra, w_ri, b_ra, b_ri, lam, dz, gsq], vmem=60, sem=("parallel",),
        aliases={10: 0, 11: 1})


def _rope_tables():
    half = HD // 2
    inv = ROPE_THETA ** (-jnp.arange(half, dtype=F32) / half)
    pos = (jnp.arange(TP) - PAD).astype(F32)
    ang = pos[:, None] * inv[None, :]
    return jnp.tile(jnp.cos(ang), (1, 4)), jnp.tile(jnp.sin(ang), (1, 4))


def _rope(x, cos_t, sin_t, sign):
    w = x.shape[1]
    lane = lax.broadcasted_iota(jnp.int32, x.shape, 1)
    first = (lane % HD) < (HD // 2)
    swapped = jnp.where(first, pltpu.roll(x, w - HD // 2, 1), pltpu.roll(x, HD // 2, 1))
    ct = jnp.tile(cos_t, (1, w // 128))
    st = jnp.tile(sin_t, (1, w // 128))
    return x * ct + swapped * jnp.where(first, -sign * st, sign * st)


def _rope_fwd(z, cos_t, sin_t):
    def body(q_ref, k_ref, v_ref, c_ref, s_ref, qo_ref, ko_ref, vo_ref):
        c = c_ref[...]
        s = s_ref[...]
        qo_ref[...] = _rope(q_ref[...], c, s, 1.0).astype(_MXU)
        ko_ref[...] = _rope(k_ref[...], c, s, 1.0).astype(_MXU)
        vo_ref[...] = v_ref[...].astype(_MXU)

    tab = pl.BlockSpec((BLK, 128), lambda i: (i, 0))
    kv = pl.BlockSpec((BLK, D_KV), lambda i: (i, 0))
    return _pcall(body, name="rope_fwd", grid=(NBLK,),
                  in_specs=[pl.BlockSpec((BLK, D), lambda i: (i, OFF_Q // D)),
                            pl.BlockSpec((BLK, D_KV), lambda i: (i, OFF_K // D_KV)),
                            pl.BlockSpec((BLK, D_KV), lambda i: (i, OFF_V // D_KV)), tab, tab],
                  out_specs=[pl.BlockSpec((BLK, D), lambda i: (i, 0)), kv, kv],
                  out_shape=[_sds((TP, D), _MXU), _sds((TP, D_KV), _MXU), _sds((TP, D_KV), _MXU)],
                  operands=[z, z, z, cos_t, sin_t], vmem=32, sem=("parallel",))


def _rope_bwd_k(dk, dv, cos_t, sin_t, dz):
    def body(dk_ref, dv_ref, c_ref, s_ref, dz_in, o_ref):
        o_ref[:, 0:D_KV] = _rope(dk_ref[...], c_ref[...], s_ref[...], -1.0).astype(_MXU)
        o_ref[:, D_KV:2 * D_KV] = dv_ref[...].astype(_MXU)

    tab = pl.BlockSpec((BLK, 128), lambda i: (i, 0))
    kv = pl.BlockSpec((BLK, D_KV), lambda i: (i, 0))
    return _pcall(body, name="rope_bwd_k", grid=(NBLK,), in_specs=[kv, kv, tab, tab, _ANY],
                  out_specs=[pl.BlockSpec((BLK, 2 * D_KV), lambda i: (i, OFF_K // (2 * D_KV)))],
                  out_shape=[_sds((TP, D_IN), _MXU)], operands=[dk, dv, cos_t, sin_t, dz], vmem=32, sem=("parallel",),
                  aliases={4: 0})[0]


def _attn_mask(i):
    ql = lax.broadcasted_iota(jnp.int32, (BLK, 3 * BLK), 0)
    kk = lax.broadcasted_iota(jnp.int32, (BLK, 3 * BLK), 1)
    kl = kk % BLK
    part = kk // BLK
    meta = (part == 0) & (kl >= PAD) & ((i >= 1) | (kl <= ql))
    prev = (part == 1) & (i >= 2) & (kl > ql)
    cur = (part == 2) & (i >= 1) & (kl <= ql)
    return meta | prev | cur


def _kv_specs():
    return [pl.BlockSpec((BLK, D_KV), lambda i: (0, 0)),
            pl.BlockSpec((BLK, D_KV), lambda i: (jnp.maximum(i - 1, 0), 0)),
            pl.BlockSpec((BLK, D_KV), lambda i: (i, 0))]


def _pair_heads(ref, gp):
    def two(j):
        h0 = 2 * GRP * gp + j
        return jnp.concatenate([ref[:, HD * h0:HD * (h0 + 1)], ref[:, HD * (h0 + GRP):HD * (h0 + GRP + 1)]], axis=1)

    return jnp.concatenate([two(j) for j in range(GRP)], axis=0)


def _pair_kv(refs, gp):
    both = jnp.concatenate([r[:, 2 * HD * gp:2 * HD * (gp + 1)] for r in refs], axis=0)
    low_lanes = lax.broadcasted_iota(jnp.int32, both.shape, 1) < HD
    zero = jnp.zeros_like(both)
    return jnp.concatenate([jnp.where(low_lanes, both, zero), jnp.where(low_lanes, zero, both)], axis=0)


def _attn_fwd(q, k, v, sinks, z, *, hook):
    nk = 3 * BLK
    n_gate = D // CW

    def body(q_ref, k0_ref, kp_ref, kc_ref, v0_ref, vp_ref, vc_ref, sink_ref, *rest):
        gate_refs, (o_ref, lse_ref, y_ref, yt_ref, s_scr, p_scr) = rest[:n_gate], rest[n_gate:]
        mask = _attn_mask(pl.program_id(0))
        for gp in range(N_KV // 2):
            s_scr[...] = lax.dot_general(_pair_heads(q_ref, gp), _pair_kv((k0_ref, kp_ref, kc_ref), gp), _NT,
                                         preferred_element_type=F32)
            for a in range(2):
                for j in range(GRP):
                    h = GRP * (2 * gp + a) + j
                    rows, cols = slice(BLK * j, BLK * (j + 1)), slice(nk * a, nk * (a + 1))
                    sink = sink_ref[h]
                    s = jnp.where(mask, s_scr[rows, cols] * (HD ** -0.5), NEG_INF)
                    mx = jnp.maximum(jnp.max(s, -1, keepdims=True), sink)
                    p = jnp.exp(s - mx)
                    den = jnp.sum(p, -1, keepdims=True) + jnp.exp(sink - mx)
                    p_scr[rows, cols] = (p * (1.0 / den)).astype(_MXU)
                    lse_ref[:, h:h + 1] = mx + jnp.log(den)
            o2 = jnp.dot(p_scr[...], _pair_kv((v0_ref, vp_ref, vc_ref), gp), preferred_element_type=F32)
            for a in range(2):
                for j in range(GRP):
                    h = GRP * (2 * gp + a) + j
                    o_ref[:, HD * h:HD * (h + 1)] = o2[BLK * j:BLK * (j + 1), HD * a:HD * (a + 1)]
        for t, g_ref in enumerate(gate_refs):
            cs = slice(CW * t, CW * (t + 1))
            zz = g_ref[...]
            y = o_ref[:, cs] * (zz * _sigmoid(zz))
            y_ref[:, cs] = y.astype(_MXU)
            yt_ref[cs, :] = y.T.astype(_MXU)

    row = pl.BlockSpec((BLK, D), lambda i: (i, 0))
    gates = [pl.BlockSpec((BLK, CW), lambda i, t=t: (i, OFF_GA // CW + t)) for t in range(n_gate)]
    return _pcall(body, name="attn_fwd", grid=(NBLK,),
                  in_specs=[row] + _kv_specs() + _kv_specs() + [pl.BlockSpec(memory_space=pltpu.SMEM)] + gates,
                  out_specs=[row, pl.BlockSpec((BLK, N_Q), lambda i: (i, 0)), row, pl.BlockSpec((D, BLK), lambda i: (0, i))],
                  out_shape=[_sds((TP, D), F32), _sds((TP, N_Q), F32), _sds((TP, D), _MXU), _sds((D, TP), _MXU)],
                  operands=[q, k, k, k, v, v, v, sinks] + [z] * n_gate,
                  scratch=[pltpu.VMEM((GRP * BLK, 2 * nk), F32), pltpu.VMEM((GRP * BLK, 2 * nk), _MXU)],
                  vmem=40, sem=("parallel",), hook=hook)


def _attn_bwd(q, k, v, sinks, do, o, lse, cos_t, sin_t, dz, *, hook):
    def body(q_ref, k0_ref, kp_ref, kc_ref, v0_ref, vp_ref, vc_ref, sink_ref, do_ref, o_ref, lse_ref, c_ref, s_ref, dz_in,
             dq_ref, dk_ref, dv_ref, dsink_ref, dqrot_ref, s_scr, dp_scr, p_scr, ds_scr):
        i = pl.program_id(0)

        @pl.when(i == 0)
        def _():
            dk_ref[...] = jnp.zeros_like(dk_ref)
            dv_ref[...] = jnp.zeros_like(dv_ref)
            dsink_ref[...] = jnp.zeros_like(dsink_ref)

        mask = _attn_mask(i)
        row_starts = (0, pl.multiple_of(jnp.maximum(i - 1, 0) * BLK, BLK), pl.multiple_of(i * BLK, BLK))
        scale = HD ** -0.5
        nk = 3 * BLK
        for gp in range(N_KV // 2):
            q2 = _pair_heads(q_ref, gp)
            dom2 = _pair_heads(do_ref, gp).astype(_MXU)
            k2 = _pair_kv((k0_ref, kp_ref, kc_ref), gp)
            s_scr[...] = lax.dot_general(q2, k2, _NT, preferred_element_type=F32)
            dp_scr[...] = lax.dot_general(dom2, _pair_kv((v0_ref, vp_ref, vc_ref), gp), _NT, preferred_element_type=F32)
            for a in range(2):
                for j in range(GRP):
                    h = GRP * (2 * gp + a) + j
                    hs = slice(HD * h, HD * (h + 1))
                    rows = slice(BLK * j, BLK * (j + 1))
                    cols = slice(nk * a, nk * (a + 1))
                    lse_h = lse_ref[:, h:h + 1]
                    delta = jnp.sum(do_ref[:, hs] * o_ref[:, hs], axis=-1, keepdims=True)
                    dsink_ref[0:1, h:h + 1] += jnp.sum(-jnp.exp(sink_ref[h] - lse_h) * delta, axis=0, keepdims=True)
                    p = jnp.exp(jnp.where(mask, s_scr[rows, cols] * scale, NEG_INF) - lse_h)
                    p_scr[rows, cols] = p.astype(_MXU)
                    ds_scr[rows, cols] = (p * (dp_scr[rows, cols] - delta) * scale).astype(_MXU)
            ds = ds_scr[...]
            dq2 = jnp.dot(ds, k2, preferred_element_type=F32)
            dk2 = lax.dot_general(ds, q2, _TN, preferred_element_type=F32)
            dv2 = lax.dot_general(p_scr[...], dom2, _TN, preferred_element_type=F32)
            for a in range(2):
                g = 2 * gp + a
                gs = slice(HD * g, HD * (g + 1))
                for part in range(3):
                    rows = pl.ds(row_starts[part], BLK)
                    blk = slice(nk * a + BLK * part, nk * a + BLK * (part + 1))
                    dk_ref[rows, gs] += dk2[blk, HD * a:HD * (a + 1)]
                    dv_ref[rows, gs] += dv2[blk, HD * a:HD * (a + 1)]
                for j in range(GRP):
                    h = GRP * g + j
                    dqrot_ref[:, HD * h:HD * (h + 1)] = dq2[BLK * j:BLK * (j + 1), HD * a:HD * (a + 1)]
        dq_ref[...] = _rope(dqrot_ref[...], c_ref[...], s_ref[...], -1.0).astype(_MXU)

    row = pl.BlockSpec((BLK, D), lambda i: (i, 0))
    tab = pl.BlockSpec((BLK, 128), lambda i: (i, 0))
    full_kv = pl.BlockSpec((TP, D_KV), lambda i: (0, 0))
    return _pcall(
        body, name="attn_bwd", grid=(NBLK,),
        in_specs=[row] + _kv_specs() + _kv_specs() + [pl.BlockSpec(memory_space=pltpu.SMEM), row, row,
                  pl.BlockSpec((BLK, N_Q), lambda i: (i, 0)), tab, tab, _ANY],
        out_specs=[pl.BlockSpec((BLK, D), lambda i: (i, OFF_Q // D)), full_kv, full_kv,
                   pl.BlockSpec((8, 128), lambda i: (0, 0))],
        out_shape=[_sds((TP, D_IN), _MXU), _sds((TP, D_KV), F32), _sds((TP, D_KV), F32), _sds((8, 128), F32)],
        operands=[q, k, k, k, v, v, v, sinks, do, o, lse, cos_t, sin_t, dz],
        scratch=[pltpu.VMEM((BLK, D), F32), pltpu.VMEM((GRP * BLK, 6 * BLK), F32), pltpu.VMEM((GRP * BLK, 6 * BLK), F32),
                 pltpu.VMEM((GRP * BLK, 6 * BLK), _MXU), pltpu.VMEM((GRP * BLK, 6 * BLK), _MXU)],
        vmem=48, sem=("arbitrary",), aliases={13: 0}, hook=hook)


def _cast_into_slot(w32, pos, *, name, tr):
    r, cc = w32.shape

    def body(pos_ref, w_ref, o_ref):
        o_ref[...] = w_ref[...].astype(BF16)

    return _pcall(body, name=name, grid=(r // tr,), in_specs=[pl.BlockSpec((tr, cc), lambda i, p: (i, 0))],
                  out_specs=[pl.BlockSpec((None, tr, cc), lambda i, p: (p[1], i, 0))],
                  out_shape=[_sds((N_SH, r, cc), BF16)], operands=[w32], vmem=32, sem=("parallel",), prefetch=(pos,))[0]


def _pair_add(g, got, pos, *, name, tr, g_has_both_halves, hook=None):
    n, h, cc = got.shape
    nt = h // tr

    def body(pos_ref, g_ref, r_ref, own_ref, s16_ref):
        s = g_ref[...] + r_ref[...].astype(F32)
        s16_ref[...] = s.astype(BF16)

        @pl.when(pl.program_id(1) == pos_ref[1])
        def _():
            own_ref[...] = s

    g_index = (lambda i, s, p: (s, p[0] * nt + i, 0)) if g_has_both_halves else (lambda i, s, p: (s, i, 0))
    return _pcall(body, name=name, grid=(nt, n),
                  in_specs=[pl.BlockSpec((None, tr, cc), g_index), pl.BlockSpec((None, tr, cc), lambda i, s, p: (s, i, 0))],
                  out_specs=[pl.BlockSpec((tr, cc), lambda i, s, p: (i, 0)),
                             pl.BlockSpec((None, tr, cc), lambda i, s, p: (s, i, 0))],
                  out_shape=[_sds((h, cc), F32), _sds((n, h, cc), BF16)], operands=[g, got], vmem=40,
                  sem=("parallel", "arbitrary"), prefetch=(pos,), hook=hook)


def _relay_add(s16, raw, pos, *, name, tr):
    _, q, cc = raw.shape
    nt = q // tr

    def body(pos_ref, s_ref, r_ref, o_ref):
        o_ref[...] = (s_ref[...].astype(F32) + r_ref[...].astype(F32)).astype(BF16)

    blk = pl.BlockSpec((None, tr, cc), lambda k, i, p: (k, i, 0))
    return _pcall(body, name=name, grid=(2, nt),
                  in_specs=[pl.BlockSpec((None, tr, cc), lambda k, i, p: (p[3 + k], (1 - k) * nt + i, 0)), blk],
                  out_specs=[blk], out_shape=[_sds((2, q, cc), BF16)], operands=[s16, raw], vmem=40,
                  sem=("parallel", "parallel"), prefetch=(pos,))[0]


def _sum_chips(own, got, pos, *, name, tr):
    h, cc = own.shape
    n_got = got.shape[0]
    nt = h // tr

    def body(pos_ref, o_ref, r_ref, out_ref):
        acc = o_ref[...]
        for k in range(n_got):
            acc = acc + r_ref[k].astype(F32)
        out_ref[...] = acc

    return _pcall(body, name=name, grid=(nt,),
                  in_specs=[pl.BlockSpec((tr, cc), lambda i, p: (i, 0)),
                            pl.BlockSpec((n_got, tr, cc), lambda i, p: (0, i, 0))],
                  out_specs=[pl.BlockSpec((tr, cc), lambda i, p: (p[0] * nt + i, 0))],
                  out_shape=[_sds((2 * h, cc), F32)], operands=[own, got], vmem=40, sem=("parallel",), prefetch=(pos,))[0]


def _adamw(w, g, m, v, *, name, tr, g_row0=0):
    r, cc = w.shape
    g_blk0 = g_row0 // tr

    def body(*refs):
        _adamw_step(*refs)

    blk = pl.BlockSpec((tr, cc), lambda i: (i, 0))
    gblk = pl.BlockSpec((tr, cc), lambda i: (g_blk0 + i, 0))
    return _pcall(body, name=name, grid=(r // tr,), in_specs=[blk, gblk, blk, blk], out_specs=[blk] * 4,
                  out_shape=[_sds((r, cc), F32)] * 4, operands=[w, g, m, v], vmem=48, sem=("parallel",))


_SMALL_ROWS = 40
_B_IN_ROWS = 7


def _row_pad(v, rows):
    flat = v.reshape(-1)
    return jnp.pad(flat, (0, rows * D - flat.shape[0])).reshape(rows, D)


def _pack_ra(w):
    return w.transpose(1, 0, 2).reshape(64, D)


def _unpack_ra(p, like):
    return p.reshape(64, N_RB, RB).transpose(1, 0, 2).reshape(like.shape)


def _gate_full(g4):
    return g4.reshape(N_SH, 64, N_RB, RB).transpose(2, 0, 1, 3).reshape(N_RB, RB, RB)


def kernel(x, meta_tokens, ln_emb_g, ln_emb_b, w_in, b_in, conv_w, conv_b, w_ra, b_ra, w_ri, b_ri, lru_lambda, sinks, w_rnn_out, w_attn_out, w_o, b_o, ln_g, ln_b, loss_target, m_meta_tokens, m_ln_emb_g, m_ln_emb_b, m_w_in, m_b_in, m_conv_w, m_conv_b, m_w_ra, m_b_ra, m_w_ri, m_b_ri, m_lru_lambda, m_sinks, m_w_rnn_out, m_w_attn_out, m_w_o, m_b_o, m_ln_g, m_ln_b, v_meta_tokens, v_ln_emb_g, v_ln_emb_b, v_w_in, v_b_in, v_conv_w, v_conv_b, v_w_ra, v_b_ra, v_w_ri, v_b_ri, v_lru_lambda, v_sinks, v_w_rnn_out, v_w_attn_out, v_w_o, v_b_o, v_ln_g, v_ln_b):
    xi, yi, ci = _my_pos()
    shard = 2 * xi + yi
    pos = jnp.stack([ci, shard, 1 - ci, shard ^ 2, shard ^ 1]).astype(jnp.int32)
    cos_t, sin_t = _rope_tables()
    zero_bias = jnp.zeros((1, D), F32)
    ln_emb_g2, ln_emb_b2 = ln_emb_g[None], ln_emb_b[None]

    small = jnp.concatenate([conv_w[0], meta_tokens, jnp.zeros((4, 512), F32)], axis=0)
    small4 = _gather_small(small)
    conv_w_full = small4[:, 0:4].transpose(1, 0, 2).reshape(CONV_W, D)
    meta_full = small4[:, 4:20].transpose(1, 0, 2).reshape(N_META, D)
    w_own = _cast_into_slot(w_in[0], pos, name="cast_w_in", tr=256)
    wa_own = _cast_into_slot(jnp.concatenate([w_attn_out[0], w_o[0], _pack_ra(w_ra[0]), _pack_ra(w_ri[0])], axis=0),
                             pos, name="cast_w_a", tr=288)
    wb_own = _cast_into_slot(w_rnn_out[0], pos, name="cast_w_b", tr=256)

    h32, h16, h16_t = _ln_emb_fwd(x[0], meta_full, ln_emb_g2, ln_emb_b2)
    order = jnp.stack([shard, shard ^ 2, shard ^ 1, shard ^ 3]).astype(jnp.int32)
    z, w_in4 = _mm_z_gather(h16, w_own, b_in, order)
    q, k, v = _rope_fwd(z, cos_t, sin_t)
    o, lse, yb_in, yb_in_t, wa4 = _attn_fwd(q, k, v, sinks[0], z, hook=_hook_gather(wa_own, 0.6))
    w_ra_full = _gate_full(wa4[:, 2 * SQ_ROWS:2 * SQ_ROWS + 64])
    w_ri_full = _gate_full(wa4[:, 2 * SQ_ROWS + 64:2 * SQ_ROWS + 128])
    a_dec, u_in = _rnn_gates_fwd(z, conv_w_full, conv_b, w_ra_full, w_ri_full, b_ra, b_ri, lru_lambda)
    hr, wb4 = _scan_fwd(a_dec, u_in, hook=_hook_gather(wb_own, 0.6))
    sq_w = {0: (wb4, 0), 1: (wa4, 0), 2: (wa4, 1)}

    def sq_nn(a, kk, bias, name, out_cols, out_index, carried=None):
        wp, blk = sq_w[kk]
        return _mm_nn(a, wp, bias, name=name, grid=(2, D // CW), tm=HALF_TP, tn=CW, k=D, a_index=lambda i, j: (i, 0),
                      w_block=(N_SH, SQ_ROWS, CW), w_index=lambda i, j: (0, blk, j), out_cols=out_cols,
                      out_index=out_index, carried=carried)[0]

    def sq_nt(a, a_blk, kk, name, hook=None):
        wp, blk = sq_w[kk]
        return _mm_nt(a, wp, name=name, grid=(2, N_SH, 1), tm=HALF_TP, tn=SQ_ROWS, tk=D,
                      a_index=lambda i, j, q: (i, a_blk), w_block=(None, SQ_ROWS, D),
                      w_index=lambda i, j, q: (j, blk, 0), out_cols=D, hook=hook)

    ya_in, ya_in_t = _mul_silu_fwd(hr, z, OFF_GR, name="gate_a_fwd")
    y2 = sq_nn(ya_in, 0, zero_bias, "mm_ya", 2 * D, lambda i, j: (i, j))
    y2 = sq_nn(yb_in, 1, zero_bias, "mm_yb", 2 * D, lambda i, j: (i, D // CW + j), carried=y2)
    mixed, mixed_t = _merge_fwd(y2, z)
    out = sq_nn(mixed, 2, b_o, "mm_out", D, lambda i, j: (i, j))
    dr, dr16, sums_o = _ln_out_loss(h32, out, loss_target[0], ln_g, ln_b)

    def sq_tn(at, b, b_blk, kk, name, carried=None):
        return _mm_tn(at, b, name=name, grid=(N_SH, 1), tm=SQ_ROWS, tn=D, a_index=lambda i, j: (i, 0),
                      b_index=lambda i, j: (0, b_blk), out_shape=(N_SH, PACK_ROWS, D),
                      out_block=(None, SQ_ROWS, D), out_index=lambda i, j: (i, kk, 0), carried=carried)[0]

    gsq = sq_tn(mixed_t, dr16, 0, 2, "mm_dwo")
    dmix = sq_nt(dr16, 0, 2, "mm_dmix")[0]
    dy2, dz = _merge_bwd(dmix, y2, z)
    gsq = sq_tn(ya_in_t, dy2, 0, 0, "mm_dwrnn", carried=gsq)
    gsq = sq_tn(yb_in_t, dy2, 1, 1, "mm_dwattn", carried=gsq)
    dya_in = sq_nt(dy2, 0, 0, "mm_dyain")[0]
    dhr, dz = _mul_silu_bwd(dya_in, hr, z, OFF_GR, dz, name="gate_a_bwd")
    lam_s = _scan_bwd(a_dec, dhr)
    dz, gsq, sums_r = _rnn_gates_bwd(z, lam_s, hr, conv_w_full, conv_b, w_ra_full, w_ri_full, b_ra, b_ri, lru_lambda, dz, gsq)
    dyb_in, gsq, got_sq = sq_nt(dy2, 1, 1, "mm_dybin", hook=_hook_pair(gsq, True, (0, 1)))
    do, dz, gsq, got_sq = _mul_silu_bwd(dyb_in, o, z, OFF_GA, dz, name="gate_b_bwd",
                                        hook=_hook_pair(gsq, True, (2, 3), land=got_sq))
    own_sq, s16_sq = _pair_add(gsq, got_sq, pos, name="red_w_sq_add", tr=208, g_has_both_halves=True)
    dz, dk_rot, dv32, dsink, s16_sq, fin_sq, raw_sq = _attn_bwd(q, k, v, sinks[0], do, o, lse, cos_t, sin_t, dz,
                                                                hook=_hook_scatter_direct(s16_sq))
    dz = _rope_bwd_k(dk_rot, dv32, cos_t, sin_t, dz)
    comb_sq = _relay_add(s16_sq, raw_sq, pos, name="red_w_sq_relay", tr=208)

    def dwin(half_idx, name, hook, **kw):
        return _mm_tn(h16_t, dz, name=name, grid=(1, N_SH * PER_IN), tm=D // 2, tn=TN_IN,
                      a_index=lambda i, j, p: (p[half_idx], 0), b_index=lambda i, j, p: (0, j),
                      out_shape=(N_SH, D // 2, W_IN_COLS), out_block=(None, D // 2, TN_IN),
                      out_index=lambda i, j, p: (j // PER_IN, 0, j % PER_IN), prefetch=(pos,), hook=hook, **kw)

    gin_sib, g_b_in, comb_sq, fin_sq = dwin(2, "mm_dwin_sib", _hook_scatter_relay(comb_sq, fin_sq),
                                            out_dtype=BF16, colsum=True)
    red_sq = _sum_chips(own_sq, fin_sq, pos, name="red_w_sq_sum", tr=208)
    gin_own, gin_sib, got_in = dwin(0, "mm_dwin_own", _hook_pair(gin_sib, False))
    own_in, s16_in, g_sq = _pair_add(gin_own, got_in, pos, name="red_w_in_add", tr=128, g_has_both_halves=False,
                                     hook=_hook_halves(red_sq))
    dhz, s16_in, fin_in, raw_in = _mm_nt(dz, w_in4, name="mm_dhz", grid=(2, 2, N_SH), tm=HALF_TP, tn=D // 2,
                                         tk=W_IN_COLS, a_index=lambda i, j, q: (i, q),
                                         w_block=(None, D // 2, W_IN_COLS), w_index=lambda i, j, q: (q, j, 0), out_cols=D,
                                         hook=_hook_scatter_direct(s16_in))
    comb_in = _relay_add(s16_in, raw_in, pos, name="red_w_in_relay", tr=128)
    sq_params = [("w_rnn_out", w_rnn_out, m_w_rnn_out, v_w_rnn_out), ("w_attn_out", w_attn_out, m_w_attn_out, v_w_attn_out),
                 ("w_o", w_o, m_w_o, v_w_o)]
    g_x, g_meta_local, sums_e, *sq_res, comb_in, fin_in = _ln_emb_bwd(
        x[0], meta_full, ln_emb_g2, dr, dhz, g_sq, [(w_[0], m_[0], v_[0]) for _, w_, m_, v_ in sq_params],
        hook=_hook_scatter_relay(comb_in, fin_in))
    red_in = _sum_chips(own_in, fin_in, pos, name="red_w_in_sum", tr=128)

    spack = jnp.concatenate([
        sums_e[0:1], sums_e[1:2], _row_pad(g_b_in, _B_IN_ROWS), sums_r[0:4], sums_r[4:5], sums_r[5:6], sums_r[6:7],
        sums_r[7:8], _row_pad(dsink[0:1, 0:N_Q], 1), sums_o[2:3], sums_o[0:1], sums_o[1:2], g_meta_local, sums_o[3:4],
        jnp.zeros((_SMALL_ROWS - 38, D), F32)], axis=0)
    sred, g_in = _allreduce_small(spack, red_in)

    big = {"w_in": [t.reshape(w_in.shape) for t in
                    _adamw(w_in[0], g_in, m_w_in[0], v_w_in[0], name="adamw_w_in", tr=128)]}
    for kk, (n, w_, _, _) in enumerate(sq_params):
        big[n] = [t.reshape(w_.shape) for t in sq_res[4 * kk:4 * kk + 4]]
    for kk, (n, w_, m_, v_) in enumerate([("w_ra", w_ra, m_w_ra, v_w_ra), ("w_ri", w_ri, m_w_ri, v_w_ri)]):
        big[n] = [_unpack_ra(t, w_) for t in
                  _adamw(_pack_ra(w_[0]), g_sq, _pack_ra(m_[0]), _pack_ra(v_[0]), name="adamw_" + n, tr=64,
                         g_row0=3 * SQ_ROWS + 64 * kk)]

    loss = sred[37, 0]
    col0 = shard * 512
    g_conv_w = lax.dynamic_slice(sred[9:13], (0, col0), (CONV_W, 512))
    g_meta = lax.dynamic_slice(sred[21:37], (0, col0), (N_META, 512))
    small_g = {"ln_emb_g": sred[0:1], "ln_emb_b": sred[1:2], "b_in": sred[2:9], "conv_w": g_conv_w.reshape(1, D),
               "conv_b": sred[13:14], "b_ra": sred[14:15], "b_ri": sred[15:16], "lru_lambda": sred[16:17],
               "sinks": sred[17:18], "b_o": sred[18:19], "ln_g": sred[19:20], "ln_b": sred[20:21],
               "meta_tokens": g_meta.reshape(4, D)}
    small_names = list(small_g)

    def small_pack(vals):
        rows = []
        for n in small_names:
            a = vals[n]
            if n == "b_in":
                a = _row_pad(a, _B_IN_ROWS)
            elif n == "sinks":
                a = _row_pad(a, 1)
            else:
                a = a.reshape(-1, D)
            rows.append(a)
        return jnp.concatenate(rows + [jnp.zeros((24 - 22, D), F32)], axis=0)

    w_small = dict(ln_emb_g=ln_emb_g, ln_emb_b=ln_emb_b, b_in=b_in, conv_w=conv_w, conv_b=conv_b, b_ra=b_ra, b_ri=b_ri,
                   lru_lambda=lru_lambda, sinks=sinks, b_o=b_o, ln_g=ln_g, ln_b=ln_b, meta_tokens=meta_tokens)
    m_small = dict(ln_emb_g=m_ln_emb_g, ln_emb_b=m_ln_emb_b, b_in=m_b_in, conv_w=m_conv_w, conv_b=m_conv_b, b_ra=m_b_ra,
                   b_ri=m_b_ri, lru_lambda=m_lru_lambda, sinks=m_sinks, b_o=m_b_o, ln_g=m_ln_g, ln_b=m_ln_b,
                   meta_tokens=m_meta_tokens)
    v_small = dict(ln_emb_g=v_ln_emb_g, ln_emb_b=v_ln_emb_b, b_in=v_b_in, conv_w=v_conv_w, conv_b=v_conv_b, b_ra=v_b_ra,
                   b_ri=v_b_ri, lru_lambda=v_lru_lambda, sinks=v_sinks, b_o=v_b_o, ln_g=v_ln_g, ln_b=v_ln_b,
                   meta_tokens=v_meta_tokens)
    g_small_pack = jnp.concatenate([small_g[n] for n in small_names] + [jnp.zeros((2, D), F32)], axis=0)
    small_res = _adamw(small_pack(w_small), g_small_pack, small_pack(m_small), small_pack(v_small),
                       name="adamw_small", tr=24)

    small_rows = {}
    r0 = 0
    for n in small_names:
        nrows = small_g[n].shape[0]
        small_rows[n] = (r0, nrows)
        r0 += nrows

    def small_out(packed, n, like):
        a, nrows = small_rows[n]
        flat = packed[a:a + nrows].reshape(-1)
        return flat[:like.size].reshape(like.shape)

    weights = dict(meta_tokens=meta_tokens, ln_emb_g=ln_emb_g, ln_emb_b=ln_emb_b, w_in=w_in, b_in=b_in, conv_w=conv_w,
                   conv_b=conv_b, w_ra=w_ra, b_ra=b_ra, w_ri=w_ri, b_ri=b_ri, lru_lambda=lru_lambda, sinks=sinks,
                   w_rnn_out=w_rnn_out, w_attn_out=w_attn_out, w_o=w_o, b_o=b_o, ln_g=ln_g, ln_b=ln_b)

    def outputs(which):
        return [big[n][which] if n in big else small_out(small_res[which], n, like) for n, like in weights.items()]

    return (loss, g_x[None], *outputs(0), *outputs(1), *outputs(2), *outputs(3))
```

```python
import jax
import jax.numpy as jnp
from jax import lax
from jax.experimental import pallas as pl
from jax.experimental.pallas import tpu as pltpu

F32 = jnp.float32
BF16 = jnp.bfloat16
_MXU = jnp.bfloat16

D = 2048
SEQ = 2048
N_META = 16
BLK = 128
PAD = BLK - N_META
TP = PAD + N_META + SEQ
NBLK = TP // BLK
HALF_TP = TP // 2
N_RB = 8
RB = 256
CONV_W = 4
LRU_C = 8.0
HD = 64
N_Q = 32
N_KV = 4
GRP = 8
D_KV = 256
NEG_INF = -1e30
LN_EPS = 1e-5
ALPHA = 2.0 ** 0.25
ROPE_THETA = 10000.0
OFF_GR, OFF_Q, OFF_K, OFF_V, OFF_GA, OFF_G = 2048, 4096, 6144, 6400, 6656, 8704
D_IN = 12800
N_SH = 4
W_IN_COLS = D_IN // N_SH
TN_IN = 640
PER_IN = W_IN_COLS // TN_IN
CW = 512
RT = TP // 4
SQ_ROWS = 512
PACK_ROWS = 3 * SQ_ROWS + 128

ADAM_LR = 0.001
ADAM_B1 = 0.9
ADAM_B2 = 0.999
ADAM_EPS = 1e-08
ADAM_WD = 0.01
ADAM_STEP = 10

MESH = pl.DeviceIdType.MESH
_MIB = 1024 * 1024
_ANY = pl.BlockSpec(memory_space=pl.ANY)
_NT = (((1,), (1,)), ((), ()))
_TN = (((0,), (0,)), ((), ()))


def _sds(shape, dtype):
    return jax.ShapeDtypeStruct(shape, dtype)


def _sigmoid(x):
    return 1.0 / (1.0 + jnp.exp(-x))


def _my_pos():
    return lax.axis_index("x"), lax.axis_index("y"), lax.axis_index("c")


def _other_chips(x, y):
    return [(1 - x, y), (x, 1 - y), (1 - x, 1 - y)]


def _remote(src, dst, send_sems, recv_sems, k, dev):
    return pltpu.make_async_remote_copy(src_ref=src, dst_ref=dst, send_sem=send_sems.at[k], recv_sem=recv_sems.at[k],
                                        device_id=dev, device_id_type=MESH)


class _Hook:
    def __init__(self, carried, landing, n_sems, start, finish, mid=None, mid_frac=0.5):
        self.carried, self.landing, self.n_sems, self.start, self.finish = list(carried), list(landing), n_sems, start, finish
        self.mid, self.mid_frac = mid, mid_frac


def _hook_gather(buf, mid_frac):
    half = buf.shape[1] // 2
    quarter = half // 2

    def geom(o, ss, rs):
        x, y, c = _my_pos()
        xn, yn, dg = _other_chips(x, y)
        slot = lambda p: 2 * p[0] + p[1]
        mine_rows = pl.ds(pl.multiple_of(c * half, 16), half)
        sib_rows = pl.ds(pl.multiple_of((1 - c) * half, 16), half)
        q_rows = lambda r: pl.ds(pl.multiple_of(c * half + r * quarter, 16), quarter)

        def cp(k, s, rows, dev):
            part = o.at[s, rows]
            return _remote(part, part, ss, rs, k, dev)

        return dict(
            direct=lambda k, s: cp(k, s, mine_rows, ((xn, yn)[k][0], (xn, yn)[k][1], c)),
            relay=lambda r, s: cp(2 + r, s, q_rows(r), ((yn, xn)[r][0], (yn, xn)[r][1], c)),
            sibling=lambda k, s, mine: cp(4 + k, s, mine_rows if mine else sib_rows, (x, y, 1 - c)),
            me=slot((x, y)), slots=(slot(xn), slot(yn), slot(dg)))

    def start(car, land, ss, rs):
        g = geom(car[0], ss, rs)
        g["direct"](0, g["me"]).start()
        g["direct"](1, g["me"]).start()

    def mid(car, land, ss, rs):
        g = geom(car[0], ss, rs)
        for k in range(2):
            g["direct"](k, g["slots"][k]).wait_recv()
            g["relay"](k, g["slots"][k]).start()
            g["sibling"](k, g["slots"][k], True).start()

    def finish(car, land, ss, rs):
        g = geom(car[0], ss, rs)
        dslot = g["slots"][2]
        g["relay"](0, dslot).wait_recv()
        g["relay"](1, dslot).wait_recv()
        g["sibling"](2, dslot, True).start()
        for k in range(3):
            g["sibling"](k, g["slots"][k], False).wait_recv()
        for k in range(2):
            g["direct"](k, g["me"]).wait_send()
            g["relay"](k, g["slots"][k]).wait_send()
        for k in range(3):
            g["sibling"](k, g["slots"][k], True).wait_send()

    return _Hook([buf], [], 7, start, finish, mid=mid, mid_frac=mid_frac)


def _hook_pair(g, half_rows, slots=(0, 1, 2, 3), land=None):
    n, r, cc = g.shape
    h = r // 2 if half_rows else r

    def plan(car, landing, ss, rs):
        x, y, c = _my_pos()
        dst = landing[0] if land is None else car[1]
        cps = []
        for k, s in enumerate(slots):
            src = car[0].at[s, pl.ds(pl.multiple_of((1 - c) * h, 8), h)] if half_rows else car[0].at[s]
            cps.append(_remote(src, dst.at[s], ss, rs, k, (x, y, 1 - c)))
        return cps

    def start(car, land, ss, rs):
        for cp in plan(car, land, ss, rs):
            cp.start()

    def finish(car, land, ss, rs):
        cps = plan(car, land, ss, rs)
        for cp in cps:
            cp.wait_recv()
        for cp in cps:
            cp.wait_send()

    if land is None:
        return _Hook([g], [_sds((n, h, cc), g.dtype)], len(slots), start, finish)
    return _Hook([g, land], [], len(slots), start, finish)


def _hook_scatter_direct(s16):
    _, h, cc = s16.shape
    q = h // 2

    def plan(car, land, ss, rs):
        x, y, c = _my_pos()
        xn, yn, dg = _other_chips(x, y)
        s, (final, raw) = car[0], land
        slot = lambda p: 2 * p[0] + p[1]
        q0, q1 = pl.ds(0, q), pl.ds(q, q)
        return [_remote(s.at[slot(xn), q0], final.at[0, q0], ss, rs, 0, (xn[0], xn[1], c)),
                _remote(s.at[slot(yn), q1], final.at[1, q1], ss, rs, 1, (yn[0], yn[1], c)),
                _remote(s.at[slot(dg), q0], raw.at[1], ss, rs, 2, (xn[0], xn[1], c)),
                _remote(s.at[slot(dg), q1], raw.at[0], ss, rs, 3, (yn[0], yn[1], c))]

    def start(car, land, ss, rs):
        for cp in plan(car, land, ss, rs):
            cp.start()

    def finish(car, land, ss, rs):
        cps = plan(car, land, ss, rs)
        for cp in cps:
            cp.wait_recv()
        for cp in cps:
            cp.wait_send()

    return _Hook([s16], [_sds((2, h, cc), s16.dtype), _sds((2, q, cc), s16.dtype)], 4, start, finish)


def _hook_scatter_relay(comb, final):
    _, q, _ = comb.shape

    def plan(car, ss, rs):
        x, y, c = _my_pos()
        xn, yn, _ = _other_chips(x, y)
        cb, final_ref = car
        return [_remote(cb.at[0], final_ref.at[0, pl.ds(q, q)], ss, rs, 0, (xn[0], xn[1], c)),
                _remote(cb.at[1], final_ref.at[1, pl.ds(0, q)], ss, rs, 1, (yn[0], yn[1], c))]

    def start(car, land, ss, rs):
        for cp in plan(car, ss, rs):
            cp.start()

    def finish(car, land, ss, rs):
        cps = plan(car, ss, rs)
        for cp in cps:
            cp.wait_recv()
        for cp in cps:
            cp.wait_send()

    return _Hook([comb, final], [], 2, start, finish)


def _hook_halves(full):
    h = full.shape[0] // 2

    def half_copy(car, ss, rs, which):
        x, y, c = _my_pos()
        rows = car[0].at[pl.ds(pl.multiple_of((c + which - 2 * c * which) * h, 8), h)]
        return _remote(rows, rows, ss, rs, 0, (x, y, 1 - c))

    def start(car, land, ss, rs):
        half_copy(car, ss, rs, 0).start()

    def finish(car, land, ss, rs):
        half_copy(car, ss, rs, 1).wait_recv()
        half_copy(car, ss, rs, 0).wait_send()

    return _Hook([full], [], 1, start, finish)


def _pcall(body, *, name, grid, in_specs, out_specs, out_shape, operands, scratch=(), vmem=48, sem=None,
           prefetch=(), aliases=None, hook=None):
    n_pre, n_in, n_out, n_scr = len(prefetch), len(in_specs), len(out_specs), len(scratch)
    in_specs, out_specs, out_shape, scratch = list(in_specs), list(out_specs), list(out_shape), list(scratch)
    io_alias = {n_pre + a: b for a, b in (aliases or {}).items()}
    operands = list(operands)
    kernel_body = body
    if hook is not None:
        n_car, n_land = len(hook.carried), len(hook.landing)
        for t, arr in enumerate(hook.carried):
            io_alias[n_pre + n_in + t] = n_out + t
        in_specs += [_ANY] * n_car
        out_specs += [_ANY] * (n_car + n_land)
        out_shape += [_sds(a.shape, a.dtype) for a in hook.carried] + hook.landing
        scratch += [pltpu.SemaphoreType.DMA((hook.n_sems,)), pltpu.SemaphoreType.DMA((hook.n_sems,))]
        operands += hook.carried
        sem = ("arbitrary",) * len(grid)

        def kernel_body(*refs):
            pre, rest = refs[:n_pre], refs[n_pre:]
            ins = rest[:n_in]
            outs = rest[n_in + n_car:n_in + n_car + n_out]
            car = rest[n_in + n_car + n_out:n_in + 2 * n_car + n_out]
            land = rest[n_in + 2 * n_car + n_out:n_in + 2 * n_car + n_out + n_land]
            scr = rest[n_in + 2 * n_car + n_out + n_land:]
            send_sems, recv_sems = scr[n_scr], scr[n_scr + 1]
            first = pl.program_id(0) == 0
            last = pl.program_id(0) == grid[0] - 1
            for d in range(1, len(grid)):
                first = first & (pl.program_id(d) == 0)
                last = last & (pl.program_id(d) == grid[d] - 1)

            @pl.when(first)
            def _():
                hook.start(car, land, send_sems, recv_sems)

            if hook.mid is not None:
                step = pl.program_id(0)
                total = grid[0]
                for d in range(1, len(grid)):
                    step = step * grid[d] + pl.program_id(d)
                    total *= grid[d]

                @pl.when(step == int(total * hook.mid_frac))
                def _():
                    hook.mid(car, land, send_sems, recv_sems)

            body(*pre, *ins, *outs, *scr[:n_scr])

            @pl.when(last)
            def _():
                hook.finish(car, land, send_sems, recv_sems)

    params = pltpu.CompilerParams(vmem_limit_bytes=vmem * _MIB, dimension_semantics=sem,
                                  has_side_effects=hook is not None)
    if n_pre:
        call = pl.pallas_call(
            kernel_body, name=name, out_shape=out_shape, input_output_aliases=io_alias, compiler_params=params,
            grid_spec=pltpu.PrefetchScalarGridSpec(num_scalar_prefetch=n_pre, grid=grid, in_specs=in_specs,
                                                   out_specs=out_specs, scratch_shapes=scratch))
    else:
        call = pl.pallas_call(kernel_body, name=name, grid=grid, in_specs=in_specs, out_specs=out_specs,
                              out_shape=out_shape, scratch_shapes=scratch, input_output_aliases=io_alias,
                              compiler_params=params)
    return call(*prefetch, *operands)


def _gather_small(blk):
    r, cc = blk.shape

    def body(x_ref, o_ref, send_sems, recv_sems):
        x, y, c = _my_pos()
        me = 2 * x + y
        o_ref[me] = x_ref[...]
        sends = [_remote(x_ref, o_ref.at[me], send_sems, recv_sems, k, (px, py, c))
                 for k, (px, py) in enumerate(_other_chips(x, y))]
        for cp in sends:
            cp.start()
        for k, (px, py) in enumerate(_other_chips(x, y)):
            _remote(x_ref, o_ref.at[2 * px + py], send_sems, recv_sems, k, (px, py, c)).wait_recv()
        for cp in sends:
            cp.wait_send()

    vm = pl.BlockSpec(memory_space=pltpu.VMEM)
    return pl.pallas_call(
        body, name="gather_small", in_specs=[vm], out_specs=vm, out_shape=_sds((N_SH, r, cc), blk.dtype),
        scratch_shapes=[pltpu.SemaphoreType.DMA((3,)), pltpu.SemaphoreType.DMA((3,))],
        compiler_params=pltpu.CompilerParams(has_side_effects=True),
    )(blk)


N_DEV = 8


def _allreduce_small(pack, halves):
    r, cc = pack.shape
    ride = _hook_halves(halves)

    def body(x_ref, h_in_ref, o_ref, h_ref, buf_ref, send_sems, recv_sems, h_ss, h_rs):
        del h_in_ref
        ride.start([h_ref], [], h_ss, h_rs)
        x, y, c = _my_pos()
        me = 4 * x + 2 * y + c
        buf_ref[me] = x_ref[...]
        cps = []
        for k in range(1, N_DEV):
            peer = (x ^ ((k >> 2) & 1), y ^ ((k >> 1) & 1), c ^ (k & 1))
            cps.append(_remote(x_ref, buf_ref.at[me], send_sems, recv_sems, k - 1, peer))
        for cp in cps:
            cp.start()
        for k in range(1, N_DEV):
            peer = (x ^ ((k >> 2) & 1), y ^ ((k >> 1) & 1), c ^ (k & 1))
            src = 4 * peer[0] + 2 * peer[1] + peer[2]
            _remote(x_ref, buf_ref.at[src], send_sems, recv_sems, k - 1, peer).wait_recv()
        acc = buf_ref[0]
        for d in range(1, N_DEV):
            acc = acc + buf_ref[d]
        o_ref[...] = acc
        for cp in cps:
            cp.wait_send()
        ride.finish([h_ref], [], h_ss, h_rs)

    vm = pl.BlockSpec(memory_space=pltpu.VMEM)
    return pl.pallas_call(
        body, name="allreduce_small", in_specs=[vm, _ANY], out_specs=[vm, _ANY],
        out_shape=[_sds((r, cc), F32), _sds(halves.shape, halves.dtype)], input_output_aliases={1: 1},
        scratch_shapes=[pltpu.VMEM((N_DEV, r, cc), F32), pltpu.SemaphoreType.DMA((N_DEV - 1,)),
                        pltpu.SemaphoreType.DMA((N_DEV - 1,)), pltpu.SemaphoreType.DMA((1,)), pltpu.SemaphoreType.DMA((1,))],
        compiler_params=pltpu.CompilerParams(has_side_effects=True),
    )(pack, halves)


def _mm_nn(a, w, bias, *, name, grid, tm, tn, k, a_index, w_block, w_index, out_cols, out_index, carried=None, hook=None):
    m = a.shape[0]

    def body(a_ref, w_ref, b_ref, *rest):
        o_ref = rest[-1]
        wv = w_ref[...]
        acc = jnp.dot(a_ref[...].astype(_MXU), wv.reshape(k, tn).astype(_MXU), preferred_element_type=F32)
        o_ref[...] = acc + b_ref[...]

    operands = [a, w, bias] + ([carried] if carried is not None else [])
    return _pcall(
        body, name=name, grid=grid,
        in_specs=[pl.BlockSpec((tm, k), a_index), pl.BlockSpec(w_block, w_index),
                  pl.BlockSpec((1, tn), lambda i, j: (0, j))] + ([_ANY] if carried is not None else []),
        out_specs=[pl.BlockSpec((tm, tn), out_index)], out_shape=[_sds((m, out_cols), F32)], operands=operands,
        vmem=56, sem=("parallel", "parallel"), aliases={3: 0} if carried is not None else None, hook=hook)


def _mm_nt(a, w, *, name, grid, tm, tn, tk, a_index, w_block, w_index, out_cols, hook=None):
    m = a.shape[0]
    nk = grid[2]

    def body(a_ref, w_ref, o_ref, acc_ref):
        part = lax.dot_general(a_ref[...].astype(_MXU), w_ref[...].astype(_MXU), _NT, preferred_element_type=F32)
        if nk == 1:
            o_ref[...] = part
        else:
            kidx = pl.program_id(2)

            @pl.when(kidx == 0)
            def _():
                acc_ref[...] = part

            @pl.when(kidx > 0)
            def _():
                acc_ref[...] += part

            @pl.when(kidx == nk - 1)
            def _():
                o_ref[...] = acc_ref[...]

    return _pcall(
        body, name=name, grid=grid, in_specs=[pl.BlockSpec((tm, tk), a_index), pl.BlockSpec(w_block, w_index)],
        out_specs=[pl.BlockSpec((tm, tn), lambda i, j, q: (i, j))], out_shape=[_sds((m, out_cols), F32)],
        operands=[a, w], scratch=[pltpu.VMEM((tm, tn) if nk > 1 else (8, 128), F32)], vmem=60,
        sem=("parallel", "parallel", "arbitrary"), hook=hook)


def _mm_tn(at, b, *, name, grid, tm, tn, a_index, b_index, out_shape, out_block, out_index, carried=None,
           prefetch=(), hook=None, out_dtype=F32, colsum=False):
    t = at.shape[1]
    n_pre = len(prefetch)

    def body(*refs):
        a_ref, b_ref = refs[n_pre], refs[n_pre + 1]
        bv = b_ref[...]
        o_ref = refs[-2] if colsum else refs[-1]
        o_ref[...] = jnp.dot(a_ref[...].astype(_MXU), bv.astype(_MXU), preferred_element_type=F32).astype(out_dtype)
        if colsum:
            refs[-1][...] = jnp.sum(bv.astype(F32), axis=0, keepdims=True)

    operands = [at, b] + ([carried] if carried is not None else [])
    out_specs = [pl.BlockSpec(out_block, out_index)]
    out_shapes = [_sds(out_shape, out_dtype)]
    if colsum:
        out_specs.append(pl.BlockSpec((1, tn), b_index))
        out_shapes.append(_sds((1, b.shape[1]), F32))
    return _pcall(
        body, name=name, grid=grid,
        in_specs=[pl.BlockSpec((tm, t), a_index), pl.BlockSpec((t, tn), b_index)] + ([_ANY] if carried is not None else []),
        out_specs=out_specs, out_shape=out_shapes, operands=operands,
        vmem=56, sem=("parallel", "parallel"), aliases={2: 0} if carried is not None else None, prefetch=prefetch, hook=hook)


def _remote_tile(n):
    near = 2 * PER_IN
    if isinstance(n, int):
        return (n % 2, n // 2) if n < near else (2, n - near)
    return jnp.where(n < near, n % 2, 2), jnp.where(n < near, n // 2, n - near)


def _mm_z_gather(h16, w_own, b_in, order):
    n_tiles = N_SH * PER_IN
    n_remote = 3 * PER_IN
    half = D // 2

    def body(order_ref, a_ref, b_ref, w_in_ref, z_ref, w_ref, wbuf, tile_sems, ss, rs):
        del w_in_ref
        j = pl.program_id(0)
        x, y, c = _my_pos()
        me = 2 * x + y
        chips = _other_chips(x, y)
        mine_rows = pl.ds(pl.multiple_of(c * half, 16), half)
        sib_rows = pl.ds(pl.multiple_of((1 - c) * half, 16), half)

        slots = [2 * px + py for px, py in chips]
        cols = lambda t: pl.ds(t * TN_IN, TN_IN)
        q_rows = lambda r: pl.ds(pl.multiple_of(c * half + r * (half // 2), 16), half // 2)

        def direct(rel, t, slot):
            px, py = chips[rel]
            part = w_ref.at[slot, mine_rows, cols(t)]
            return _remote(part, part, ss, rs, 2 * t + rel, (px, py, c))

        def relay(r, t, slot):
            px, py = chips[1 - r]
            part = w_ref.at[slot, q_rows(r), cols(t)]
            return _remote(part, part, ss, rs, 2 * PER_IN + 2 * t + r, (px, py, c))

        def d2d(n, rows):
            rel, t = _remote_tile(n)
            part = w_ref.at[slots[rel], rows, cols(t)]
            return _remote(part, part, ss, rs, 4 * PER_IN + n, (x, y, 1 - c))

        def tile_copy(step):
            rel, t = _remote_tile(jnp.maximum(step - PER_IN, 0))
            slot = jnp.where(step < PER_IN, me, me ^ jnp.where(rel == 0, 2, jnp.where(rel == 1, 1, 3)))
            col = pl.multiple_of(jnp.where(step < PER_IN, step, t) * TN_IN, 128)
            return pltpu.make_async_copy(w_ref.at[slot, :, pl.ds(col, TN_IN)], wbuf.at[step % 2], tile_sems.at[step % 2])

        @pl.when(j == 0)
        def _():
            for t in range(PER_IN):
                direct(0, t, me).start()
                direct(1, t, me).start()
            tile_copy(0).start()

        for n in range(n_remote):
            rel, t = _remote_tile(n)

            @pl.when(j == n + PER_IN - 3)
            def _():
                if rel < 2:
                    direct(rel, t, slots[rel]).wait_recv()
                    relay(rel, t, slots[rel]).start()
                else:
                    relay(0, t, slots[2]).wait_recv()
                    relay(1, t, slots[2]).wait_recv()
                d2d(n, mine_rows).start()

            @pl.when(j == n + PER_IN - 2)
            def _():
                d2d(n, sib_rows).wait_recv()

        @pl.when(j + 1 < n_tiles)
        def _():
            tile_copy(j + 1).start()

        tile_copy(j).wait()
        z_ref[...] = jnp.dot(a_ref[...], wbuf[j % 2], preferred_element_type=F32) + b_ref[...]

        @pl.when(j == n_tiles - 1)
        def _():
            for t in range(PER_IN):
                for r in range(2):
                    direct(r, t, me).wait_send()
                    relay(r, t, slots[r]).wait_send()
            for n in range(n_remote):
                d2d(n, mine_rows).wait_send()

    def col_tile(j, o):
        rel, t = _remote_tile(jnp.maximum(j - PER_IN, 0))
        return 0, jnp.where(j < PER_IN, o[0] * PER_IN + j, o[1 + rel] * PER_IN + t)

    return pl.pallas_call(
        body, name="mm_z_gather",
        grid_spec=pltpu.PrefetchScalarGridSpec(
            num_scalar_prefetch=1, grid=(n_tiles,),
            in_specs=[pl.BlockSpec((TP, D), lambda j, o: (0, 0)), pl.BlockSpec((1, TN_IN), col_tile), _ANY],
            out_specs=[pl.BlockSpec((TP, TN_IN), col_tile), _ANY],
            scratch_shapes=[pltpu.VMEM((2, D, TN_IN), BF16), pltpu.SemaphoreType.DMA((2,)),
                            pltpu.SemaphoreType.DMA((4 * PER_IN + n_remote,)),
                            pltpu.SemaphoreType.DMA((4 * PER_IN + n_remote,))]),
        out_shape=[_sds((TP, D_IN), F32), _sds(w_own.shape, w_own.dtype)],
        input_output_aliases={3: 1},
        compiler_params=pltpu.CompilerParams(vmem_limit_bytes=60 * _MIB, dimension_semantics=("arbitrary",),
                                             has_side_effects=True),
    )(order, h16, b_in, w_own)


def _padded_rows(i, x_ref, meta_ref):
    head = jnp.concatenate([jnp.zeros((PAD, D), F32), meta_ref[...]], axis=0)
    return jnp.where(i == 0, head, x_ref[...])


def _stream_specs():
    return [pl.BlockSpec((BLK, D), lambda i: (jnp.maximum(i - 1, 0), 0)), pl.BlockSpec((N_META, D), lambda i: (0, 0))]


def _ln_emb_fwd(x, meta, g, b):
    def body(x_ref, meta_ref, g_ref, b_ref, h32_ref, h16_ref, h16t_ref):
        x = _padded_rows(pl.program_id(0), x_ref, meta_ref)
        mu = jnp.mean(x, axis=-1, keepdims=True)
        xc = x - mu
        var = jnp.mean(xc * xc, axis=-1, keepdims=True)
        y = xc * lax.rsqrt(var + LN_EPS) * g_ref[...] + b_ref[...]
        h32_ref[...] = y
        h16_ref[...] = y.astype(_MXU)
        h16t_ref[...] = y.T.astype(_MXU)

    row = pl.BlockSpec((BLK, D), lambda i: (i, 0))
    vec = pl.BlockSpec((1, D), lambda i: (0, 0))
    return _pcall(body, name="ln_emb_fwd", grid=(NBLK,), in_specs=_stream_specs() + [vec, vec],
                  out_specs=[row, row, pl.BlockSpec((D, BLK), lambda i: (0, i))],
                  out_shape=[_sds((TP, D), F32), _sds((TP, D), _MXU), _sds((D, TP), _MXU)], operands=[x, meta, g, b],
                  vmem=32, sem=("parallel",))


def _adamw_step(w_ref, g_ref, m_ref, v_ref, go_ref, d_ref, mo_ref, vo_ref):
    gg = g_ref[...]
    go_ref[...] = gg
    m_new = ADAM_B1 * m_ref[...] + (1.0 - ADAM_B1) * gg
    v_new = ADAM_B2 * v_ref[...] + (1.0 - ADAM_B2) * (gg * gg)
    m_hat = m_new / (1.0 - ADAM_B1 ** ADAM_STEP)
    v_hat = v_new / (1.0 - ADAM_B2 ** ADAM_STEP)
    d_ref[...] = -ADAM_LR * (m_hat / (jnp.sqrt(v_hat) + ADAM_EPS) + ADAM_WD * w_ref[...])
    mo_ref[...] = m_new
    vo_ref[...] = v_new


def _ln_emb_bwd(x, meta, g, dr, dhz, g_sq, sq, *, hook):
    n_sq = len(sq)
    sq_steps = 16
    sq_rows = SQ_ROWS // sq_steps

    def body(x_ref, meta_ref, g_ref, dr_ref, dhz_ref, *rest):
        sq_in = rest[:4 * n_sq]
        dx_ref, dmeta_ref, acc_ref = rest[4 * n_sq:4 * n_sq + 3]
        sq_out = rest[4 * n_sq + 3:]
        i = pl.program_id(0)

        @pl.when(i < sq_steps)
        def _():
            for kk in range(n_sq):
                _adamw_step(*sq_in[4 * kk:4 * kk + 4], *sq_out[4 * kk:4 * kk + 4])


        @pl.when(i == 0)
        def _():
            acc_ref[...] = jnp.zeros_like(acc_ref)

        x = _padded_rows(i, x_ref, meta_ref)
        mu = jnp.mean(x, axis=-1, keepdims=True)
        xc = x - mu
        var = jnp.mean(xc * xc, axis=-1, keepdims=True)
        rstd = lax.rsqrt(var + LN_EPS)
        xhat = xc * rstd
        dh = ALPHA * dr_ref[...] + dhz_ref[...]
        acc_ref[0:1, :] += jnp.sum(dh * xhat, axis=0, keepdims=True)
        acc_ref[1:2, :] += jnp.sum(dh, axis=0, keepdims=True)
        dxh = dh * g_ref[...]
        m1 = jnp.mean(dxh, axis=-1, keepdims=True)
        m2 = jnp.mean(dxh * xhat, axis=-1, keepdims=True)
        dx = rstd * (dxh - m1 - xhat * m2)
        dx_ref[...] = dx

        @pl.when(i == 0)
        def _():
            dmeta_ref[...] = dx[PAD:BLK]

    row = pl.BlockSpec((BLK, D), lambda i: (i, 0))
    vec = pl.BlockSpec((1, D), lambda i: (0, 0))
    xs, ms = _stream_specs()
    sq_blk = pl.BlockSpec((sq_rows, D), lambda i: (jnp.minimum(i, sq_steps - 1), 0))
    sq_specs, sq_operands = [], []
    for kk, (w_, m_, v_) in enumerate(sq):
        g_blk = pl.BlockSpec((sq_rows, D), lambda i, kk=kk: (sq_steps * kk + jnp.minimum(i, sq_steps - 1), 0))
        sq_specs += [sq_blk, g_blk, sq_blk, sq_blk]
        sq_operands += [w_, g_sq, m_, v_]
    return _pcall(body, name="ln_emb_bwd", grid=(NBLK,), in_specs=[xs, ms, vec, row, row] + sq_specs,
                  out_specs=[xs, ms, pl.BlockSpec((8, D), lambda i: (0, 0))] + [sq_blk] * (4 * n_sq),
                  out_shape=[_sds((SEQ, D), F32), _sds((N_META, D), F32), _sds((8, D), F32)]
                            + [_sds((SQ_ROWS, D), F32)] * (4 * n_sq),
                  operands=[x, meta, g, dr, dhz] + sq_operands, vmem=40, sem=("arbitrary",), hook=hook)


def _mul_silu_fwd(a, z, off, *, name):
    def body(a_ref, z_ref, o_ref, t_ref):
        zz = z_ref[...]
        y = a_ref[...] * (zz * _sigmoid(zz))
        o_ref[...] = y.astype(_MXU)
        t_ref[...] = y.T.astype(_MXU)

    strip = pl.BlockSpec((TP, CW), lambda j: (0, j))
    return _pcall(body, name=name, grid=(D // CW,),
                  in_specs=[strip, pl.BlockSpec((TP, CW), lambda j: (0, off // CW + j))],
                  out_specs=[strip, pl.BlockSpec((CW, TP), lambda j: (j, 0))],
                  out_shape=[_sds((TP, D), _MXU), _sds((D, TP), _MXU)], operands=[a, z], vmem=56, sem=("parallel",))


def _mul_silu_bwd(dy, a, z, off, dz, *, name, hook=None):
    def body(dy_ref, a_ref, z_ref, dz_in, da_ref, dg_ref):
        zz = z_ref[...]
        sg = _sigmoid(zz)
        d = dy_ref[...]
        da_ref[...] = d * (zz * sg)
        dg_ref[...] = (d * a_ref[...] * (sg * (1.0 + zz * (1.0 - sg)))).astype(_MXU)

    blk = pl.BlockSpec((RT, CW), lambda i, j: (i, j))
    zblk = pl.BlockSpec((RT, CW), lambda i, j: (i, off // CW + j))
    return _pcall(body, name=name, grid=(TP // RT, D // CW), in_specs=[blk, blk, zblk, _ANY], out_specs=[blk, zblk],
                  out_shape=[_sds((TP, D), F32), _sds((TP, D_IN), _MXU)], operands=[dy, a, z, dz], vmem=32,
                  sem=("parallel", "parallel"), aliases={3: 1}, hook=hook)


def _merge_fwd(y2, z):
    w = 256

    def body(ya_ref, yb_ref, ga_ref, gb_ref, o_ref, t_ref):
        y = _sigmoid(ga_ref[...]) * ya_ref[...] + _sigmoid(gb_ref[...]) * yb_ref[...]
        o_ref[...] = y.astype(_MXU)
        t_ref[...] = y.T.astype(_MXU)

    nb = D // w
    strip = pl.BlockSpec((TP, w), lambda j: (0, j))
    return _pcall(body, name="merge_fwd", grid=(nb,),
                  in_specs=[strip, pl.BlockSpec((TP, w), lambda j: (0, nb + j)),
                            pl.BlockSpec((TP, w), lambda j: (0, OFF_G // w + j)),
                            pl.BlockSpec((TP, w), lambda j: (0, (OFF_G + D) // w + j))],
                  out_specs=[strip, pl.BlockSpec((w, TP), lambda j: (j, 0))],
                  out_shape=[_sds((TP, D), _MXU), _sds((D, TP), _MXU)], operands=[y2, y2, z, z], vmem=56,
                  sem=("parallel",))


def _merge_bwd(dmix, y2, z):
    nb = D // CW

    def body(dm_ref, y_ref, g_ref, dy_ref, dg_ref):
        dm = dm_ref[...]
        sg = _sigmoid(g_ref[...])
        dy_ref[...] = (dm * sg).astype(_MXU)
        dg_ref[...] = (dm * y_ref[...] * sg * (1.0 - sg)).astype(_MXU)

    blk = pl.BlockSpec((RT, CW), lambda i, j: (i, j))
    gblk = pl.BlockSpec((RT, CW), lambda i, j: (i, OFF_G // CW + j))
    return _pcall(body, name="merge_bwd", grid=(TP // RT, 2 * nb),
                  in_specs=[pl.BlockSpec((RT, CW), lambda i, j: (i, j % nb)), blk, gblk], out_specs=[blk, gblk],
                  out_shape=[_sds((TP, 2 * D), _MXU), _sds((TP, D_IN), _MXU)], operands=[dmix, y2, z], vmem=32,
                  sem=("parallel", "parallel"))


def _ln_out_loss(h32, out, target, g, b):
    def body(h_ref, o_ref, t_ref, g_ref, b_ref, dr_ref, dr16_ref, acc_ref):
        i = pl.program_id(0)

        @pl.when(i == 0)
        def _():
            acc_ref[...] = jnp.zeros_like(acc_ref)

        r = ALPHA * h_ref[...] + o_ref[...]
        mu = jnp.mean(r, axis=-1, keepdims=True)
        rc = r - mu
        var = jnp.mean(rc * rc, axis=-1, keepdims=True)
        rstd = lax.rsqrt(var + LN_EPS)
        xhat = rc * rstd
        gg = g_ref[...]
        y = xhat * gg + b_ref[...]
        real = (i >= 1).astype(F32)
        diff = (y - t_ref[...]) * real
        dy = diff * (1.0 / D)
        dxh = dy * gg
        m1 = jnp.mean(dxh, axis=-1, keepdims=True)
        m2 = jnp.mean(dxh * xhat, axis=-1, keepdims=True)
        dr = rstd * (dxh - m1 - xhat * m2)
        dr_ref[...] = dr
        dr16_ref[...] = dr.astype(_MXU)
        acc_ref[0:1, :] += jnp.sum(dy * xhat, axis=0, keepdims=True)
        acc_ref[1:2, :] += jnp.sum(dy, axis=0, keepdims=True)
        acc_ref[2:3, :] += jnp.sum(dr, axis=0, keepdims=True)
        acc_ref[3:4, :] += (0.5 / D) * jnp.sum(diff * diff)

    row = pl.BlockSpec((BLK, D), lambda i: (i, 0))
    vec = pl.BlockSpec((1, D), lambda i: (0, 0))
    return _pcall(body, name="ln_out_loss", grid=(NBLK,),
                  in_specs=[row, row, pl.BlockSpec((BLK, D), lambda i: (jnp.maximum(i - 1, 0), 0)), vec, vec],
                  out_specs=[row, row, pl.BlockSpec((8, D), lambda i: (0, 0))],
                  out_shape=[_sds((TP, D), F32), _sds((TP, D), _MXU), _sds((8, D), F32)],
                  operands=[h32, out, target, g, b], vmem=32, sem=("arbitrary",))


def _rnn_recompute(xr_ref, cw_ref, cb_ref, wra_ref, wri_ref, bra_ref, bri_ref, lam_ref):
    rows = lax.broadcasted_iota(jnp.int32, (TP, 1), 0)
    valid = (rows >= PAD).astype(F32)
    first = rows == PAD
    x = xr_ref[...] * valid
    cw = cw_ref[...]
    shifted = [x, pltpu.roll(x, 1, 0), pltpu.roll(x, 2, 0), pltpu.roll(x, 3, 0)]
    c = cb_ref[...] + cw[0:1, :] * shifted[0] + cw[1:2, :] * shifted[1] + cw[2:3, :] * shifted[2] + cw[3:4, :] * shifted[3]
    cm = c.astype(_MXU)
    gr = _sigmoid(jnp.dot(cm, wra_ref[...].astype(_MXU), preferred_element_type=F32) + bra_ref[...])
    gi = _sigmoid(jnp.dot(cm, wri_ref[...].astype(_MXU), preferred_element_type=F32) + bri_ref[...])
    lam = lam_ref[...]
    ls = jnp.minimum(lam, 0.0) - jnp.log(1.0 + jnp.exp(-jnp.abs(lam)))
    log_a = LRU_C * gr * ls
    a = jnp.exp(log_a)
    mult = jnp.where(first, 1.0, jnp.sqrt(1.0 - jnp.exp(2.0 * log_a)))
    return dict(valid=valid, first=first, shifted=shifted, c=c, cm=cm, gr=gr, gi=gi, ls=ls, a=a, mult=mult, lam=lam)


def _rnn_specs():
    col = pl.BlockSpec((TP, RB), lambda n: (0, n))
    vec = pl.BlockSpec((1, RB), lambda n: (0, n))
    return dict(col=col, vec=vec, cw=pl.BlockSpec((CONV_W, RB), lambda n: (0, n)),
                wblk=pl.BlockSpec((None, RB, RB), lambda n: (n, 0, 0)))


def _rnn_gates_fwd(z, conv_w, conv_b, w_ra, w_ri, b_ra, b_ri, lam):
    def body(xr_ref, cw_ref, cb_ref, wra_ref, wri_ref, bra_ref, bri_ref, lam_ref, a_ref, u_ref):
        r = _rnn_recompute(xr_ref, cw_ref, cb_ref, wra_ref, wri_ref, bra_ref, bri_ref, lam_ref)
        a_ref[...] = r["a"]
        u_ref[...] = r["mult"] * r["gi"] * r["c"] * r["valid"]

    s = _rnn_specs()
    return _pcall(body, name="rnn_gates_fwd", grid=(N_RB,),
                  in_specs=[s["col"], s["cw"], s["vec"], s["wblk"], s["wblk"], s["vec"], s["vec"], s["vec"]],
                  out_specs=[s["col"], s["col"]], out_shape=[_sds((TP, D), F32)] * 2,
                  operands=[z, conv_w, conv_b, w_ra, w_ri, b_ra, b_ri, lam], vmem=56, sem=("parallel",))


SCAN_ROWS = 272


SUB = 8


def _tile_scan(a, u, reverse):
    rows = lax.broadcasted_iota(jnp.int32, a.shape, 0)
    for d in (1, 2, 4):
        shift = SUB - d if reverse else d
        inside = (rows < SUB - d) if reverse else (rows >= d)
        u = u + a * jnp.where(inside, pltpu.roll(u, shift, 0), 0.0)
        a = a * jnp.where(inside, pltpu.roll(a, shift, 0), 1.0)
    return a, u


def _scan_fwd(a, u, *, hook):
    def body(a_ref, u_ref, h_ref, carry_ref):
        @pl.when(pl.program_id(1) == 0)
        def _():
            carry_ref[...] = jnp.zeros_like(carry_ref)

        def step(r, h):
            rows = pl.ds(pl.multiple_of(r * SUB, SUB), SUB)
            prod, part = _tile_scan(a_ref[rows, :], u_ref[rows, :], False)
            ht = part + prod * h
            h_ref[rows, :] = ht
            return ht[SUB - 1:SUB, :]

        carry_ref[...] = lax.fori_loop(0, SCAN_ROWS // SUB, step, carry_ref[...], unroll=2)

    blk = pl.BlockSpec((SCAN_ROWS, CW), lambda j, i: (i, j))
    return _pcall(body, name="scan_fwd", grid=(D // CW, TP // SCAN_ROWS), in_specs=[blk, blk], out_specs=[blk],
                  out_shape=[_sds((TP, D), F32)], operands=[a, u], scratch=[pltpu.VMEM((1, CW), F32)], vmem=32,
                  sem=("parallel", "arbitrary"), hook=hook)


def _scan_bwd(a, dh):
    nst = TP // SCAN_ROWS
    n_tiles = SCAN_ROWS // SUB

    def body(a_ref, d_ref, o_ref, lam_ref, anext_ref):
        @pl.when(pl.program_id(1) == 0)
        def _():
            lam_ref[...] = jnp.zeros_like(lam_ref)
            anext_ref[...] = jnp.zeros_like(anext_ref)

        def step(q, carry):
            lam_next, a_next = carry
            rows = pl.ds(pl.multiple_of((n_tiles - 1 - q) * SUB, SUB), SUB)
            at = a_ref[rows, :]
            last = lax.broadcasted_iota(jnp.int32, at.shape, 0) == SUB - 1
            b = jnp.where(last, a_next, pltpu.roll(at, SUB - 1, 0))
            prod, part = _tile_scan(b, d_ref[rows, :], True)
            lam = part + prod * lam_next
            o_ref[rows, :] = lam
            return lam[0:1, :], at[0:1, :]

        lam, an = lax.fori_loop(0, n_tiles, step, (lam_ref[...], anext_ref[...]), unroll=2)
        lam_ref[...] = lam
        anext_ref[...] = an

    blk = pl.BlockSpec((SCAN_ROWS, CW), lambda j, i: (nst - 1 - i, j))
    return _pcall(body, name="scan_bwd", grid=(D // CW, nst), in_specs=[blk, blk], out_specs=[blk],
                  out_shape=[_sds((TP, D), F32)], operands=[a, dh],
                  scratch=[pltpu.VMEM((1, CW), F32), pltpu.VMEM((1, CW), F32)], vmem=32,
                  sem=("parallel", "arbitrary"))[0]


def _rnn_gates_bwd(z, lam_s, hr, conv_w, conv_b, w_ra, w_ri, b_ra, b_ri, lam, dz, gsq):
    def body(xr_ref, ls_ref, hr_ref, cw_ref, cb_ref, wra_ref, wri_ref, bra_ref, bri_ref, lam_ref, dz_in, gsq_in,
             dx_ref, dw_ref, sums_ref):
        r = _rnn_recompute(xr_ref, cw_ref, cb_ref, wra_ref, wri_ref, bra_ref, bri_ref, lam_ref)
        valid, c, gr, gi, a, mult = r["valid"], r["c"], r["gr"], r["gi"], r["a"], r["mult"]
        du = ls_ref[...] * valid
        da = du * pltpu.roll(hr_ref[...], 1, 0)
        d_gi = du * mult * c
        dc = du * mult * gi
        dmult = du * gi * c
        dlog_a = da * a + jnp.where(r["first"], 0.0, -dmult * a * a / mult)
        d_gr = dlog_a * (LRU_C * r["ls"])
        dls = jnp.sum(dlog_a * (LRU_C * gr), axis=0, keepdims=True)
        dpre_r = d_gr * gr * (1.0 - gr)
        dpre_i = d_gi * gi * (1.0 - gi)
        pr = dpre_r.astype(_MXU)
        pi = dpre_i.astype(_MXU)
        dwra = lax.dot_general(r["cm"], pr, _TN, preferred_element_type=F32)
        dwri = lax.dot_general(r["cm"], pi, _TN, preferred_element_type=F32)
        for s in range(N_SH):
            dw_ref[s, 0:64, :] = dwra[64 * s:64 * (s + 1)]
            dw_ref[s, 64:128, :] = dwri[64 * s:64 * (s + 1)]
        dc = dc + lax.dot_general(pr, wra_ref[...].astype(_MXU), _NT, preferred_element_type=F32)
        dc = dc + lax.dot_general(pi, wri_ref[...].astype(_MXU), _NT, preferred_element_type=F32)
        cw = cw_ref[...]
        dx = cw[0:1, :] * dc
        for k in range(1, CONV_W):
            dx = dx + cw[k:k + 1, :] * pltpu.roll(dc, TP - k, 0)
        dx_ref[...] = (dx * valid).astype(_MXU)
        for k in range(CONV_W):
            sums_ref[k:k + 1, :] = jnp.sum(dc * r["shifted"][k], axis=0, keepdims=True)
        sums_ref[4:5, :] = jnp.sum(dc, axis=0, keepdims=True)
        sums_ref[5:6, :] = jnp.sum(dpre_r, axis=0, keepdims=True)
        sums_ref[6:7, :] = jnp.sum(dpre_i, axis=0, keepdims=True)
        sums_ref[7:8, :] = dls * _sigmoid(-r["lam"])

    s = _rnn_specs()
    return _pcall(
        body, name="rnn_gates_bwd", grid=(N_RB,),
        in_specs=[s["col"], s["col"], s["col"], s["cw"], s["vec"], s["wblk"], s["wblk"], s["vec"], s["vec"], s["vec"],
                  _ANY, _ANY],
        out_specs=[s["col"], pl.BlockSpec((N_SH, 128, RB), lambda n: (0, 3 * SQ_ROWS // 128, n)),
                   pl.BlockSpec((8, RB), lambda n: (0, n))],
        out_shape=[_sds((TP, D_IN), _MXU), _sds((N_SH, PACK_ROWS, D), F32), _sds((8, D), F32)],
        operands=[z, lam_s, hr, conv_w, conv_b, w_ra, w_ri, b_ra, b_ri, lam, dz, gsq], vmem=60, sem=("parallel",),
        aliases={10: 0, 11: 1})


def _rope_tables():
    half = HD // 2
    inv = ROPE_THETA ** (-jnp.arange(half, dtype=F32) / half)
    pos = (jnp.arange(TP) - PAD).astype(F32)
    ang = pos[:, None] * inv[None, :]
    return jnp.tile(jnp.cos(ang), (1, 4)), jnp.tile(jnp.sin(ang), (1, 4))


def _rope(x, cos_t, sin_t, sign):
    w = x.shape[1]
    lane = lax.broadcasted_iota(jnp.int32, x.shape, 1)
    first = (lane % HD) < (HD // 2)
    swapped = jnp.where(first, pltpu.roll(x, w - HD // 2, 1), pltpu.roll(x, HD // 2, 1))
    ct = jnp.tile(cos_t, (1, w // 128))
    st = jnp.tile(sin_t, (1, w // 128))
    return x * ct + swapped * jnp.where(first, -sign * st, sign * st)


def _rope_fwd(z, cos_t, sin_t):
    def body(q_ref, k_ref, v_ref, c_ref, s_ref, qo_ref, ko_ref, vo_ref):
        c = c_ref[...]
        s = s_ref[...]
        qo_ref[...] = _rope(q_ref[...], c, s, 1.0).astype(_MXU)
        ko_ref[...] = _rope(k_ref[...], c, s, 1.0).astype(_MXU)
        vo_ref[...] = v_ref[...].astype(_MXU)

    tab = pl.BlockSpec((BLK, 128), lambda i: (i, 0))
    kv = pl.BlockSpec((BLK, D_KV), lambda i: (i, 0))
    return _pcall(body, name="rope_fwd", grid=(NBLK,),
                  in_specs=[pl.BlockSpec((BLK, D), lambda i: (i, OFF_Q // D)),
                            pl.BlockSpec((BLK, D_KV), lambda i: (i, OFF_K // D_KV)),
                            pl.BlockSpec((BLK, D_KV), lambda i: (i, OFF_V // D_KV)), tab, tab],
                  out_specs=[pl.BlockSpec((BLK, D), lambda i: (i, 0)), kv, kv],
                  out_shape=[_sds((TP, D), _MXU), _sds((TP, D_KV), _MXU), _sds((TP, D_KV), _MXU)],
                  operands=[z, z, z, cos_t, sin_t], vmem=32, sem=("parallel",))


def _rope_bwd_k(dk, dv, cos_t, sin_t, dz):
    def body(dk_ref, dv_ref, c_ref, s_ref, dz_in, o_ref):
        o_ref[:, 0:D_KV] = _rope(dk_ref[...], c_ref[...], s_ref[...], -1.0).astype(_MXU)
        o_ref[:, D_KV:2 * D_KV] = dv_ref[...].astype(_MXU)

    tab = pl.BlockSpec((BLK, 128), lambda i: (i, 0))
    kv = pl.BlockSpec((BLK, D_KV), lambda i: (i, 0))
    return _pcall(body, name="rope_bwd_k", grid=(NBLK,), in_specs=[kv, kv, tab, tab, _ANY],
                  out_specs=[pl.BlockSpec((BLK, 2 * D_KV), lambda i: (i, OFF_K // (2 * D_KV)))],
                  out_shape=[_sds((TP, D_IN), _MXU)], operands=[dk, dv, cos_t, sin_t, dz], vmem=32, sem=("parallel",),
                  aliases={4: 0})[0]


def _attn_mask(i):
    ql = lax.broadcasted_iota(jnp.int32, (BLK, 3 * BLK), 0)
    kk = lax.broadcasted_iota(jnp.int32, (BLK, 3 * BLK), 1)
    kl = kk % BLK
    part = kk // BLK
    meta = (part == 0) & (kl >= PAD) & ((i >= 1) | (kl <= ql))
    prev = (part == 1) & (i >= 2) & (kl > ql)
    cur = (part == 2) & (i >= 1) & (kl <= ql)
    return meta | prev | cur


def _kv_specs():
    return [pl.BlockSpec((BLK, D_KV), lambda i: (0, 0)),
            pl.BlockSpec((BLK, D_KV), lambda i: (jnp.maximum(i - 1, 0), 0)),
            pl.BlockSpec((BLK, D_KV), lambda i: (i, 0))]


def _pair_heads(ref, gp):
    def two(j):
        h0 = 2 * GRP * gp + j
        return jnp.concatenate([ref[:, HD * h0:HD * (h0 + 1)], ref[:, HD * (h0 + GRP):HD * (h0 + GRP + 1)]], axis=1)

    return jnp.concatenate([two(j) for j in range(GRP)], axis=0)


def _pair_kv(refs, gp):
    both = jnp.concatenate([r[:, 2 * HD * gp:2 * HD * (gp + 1)] for r in refs], axis=0)
    low_lanes = lax.broadcasted_iota(jnp.int32, both.shape, 1) < HD
    zero = jnp.zeros_like(both)
    return jnp.concatenate([jnp.where(low_lanes, both, zero), jnp.where(low_lanes, zero, both)], axis=0)


def _attn_fwd(q, k, v, sinks, z, *, hook):
    nk = 3 * BLK
    n_gate = D // CW

    def body(q_ref, k0_ref, kp_ref, kc_ref, v0_ref, vp_ref, vc_ref, sink_ref, *rest):
        gate_refs, (o_ref, lse_ref, y_ref, yt_ref, s_scr, p_scr) = rest[:n_gate], rest[n_gate:]
        mask = _attn_mask(pl.program_id(0))
        for gp in range(N_KV // 2):
            s_scr[...] = lax.dot_general(_pair_heads(q_ref, gp), _pair_kv((k0_ref, kp_ref, kc_ref), gp), _NT,
                                         preferred_element_type=F32)
            for a in range(2):
                for j in range(GRP):
                    h = GRP * (2 * gp + a) + j
                    rows, cols = slice(BLK * j, BLK * (j + 1)), slice(nk * a, nk * (a + 1))
                    sink = sink_ref[h]
                    s = jnp.where(mask, s_scr[rows, cols] * (HD ** -0.5), NEG_INF)
                    mx = jnp.maximum(jnp.max(s, -1, keepdims=True), sink)
                    p = jnp.exp(s - mx)
                    den = jnp.sum(p, -1, keepdims=True) + jnp.exp(sink - mx)
                    p_scr[rows, cols] = (p * (1.0 / den)).astype(_MXU)
                    lse_ref[:, h:h + 1] = mx + jnp.log(den)
            o2 = jnp.dot(p_scr[...], _pair_kv((v0_ref, vp_ref, vc_ref), gp), preferred_element_type=F32)
            for a in range(2):
                for j in range(GRP):
                    h = GRP * (2 * gp + a) + j
                    o_ref[:, HD * h:HD * (h + 1)] = o2[BLK * j:BLK * (j + 1), HD * a:HD * (a + 1)]
        for t, g_ref in enumerate(gate_refs):
            cs = slice(CW * t, CW * (t + 1))
            zz = g_ref[...]
            y = o_ref[:, cs] * (zz * _sigmoid(zz))
            y_ref[:, cs] = y.astype(_MXU)
            yt_ref[cs, :] = y.T.astype(_MXU)

    row = pl.BlockSpec((BLK, D), lambda i: (i, 0))
    gates = [pl.BlockSpec((BLK, CW), lambda i, t=t: (i, OFF_GA // CW + t)) for t in range(n_gate)]
    return _pcall(body, name="attn_fwd", grid=(NBLK,),
                  in_specs=[row] + _kv_specs() + _kv_specs() + [pl.BlockSpec(memory_space=pltpu.SMEM)] + gates,
                  out_specs=[row, pl.BlockSpec((BLK, N_Q), lambda i: (i, 0)), row, pl.BlockSpec((D, BLK), lambda i: (0, i))],
                  out_shape=[_sds((TP, D), F32), _sds((TP, N_Q), F32), _sds((TP, D), _MXU), _sds((D, TP), _MXU)],
                  operands=[q, k, k, k, v, v, v, sinks] + [z] * n_gate,
                  scratch=[pltpu.VMEM((GRP * BLK, 2 * nk), F32), pltpu.VMEM((GRP * BLK, 2 * nk), _MXU)],
                  vmem=40, sem=("parallel",), hook=hook)


def _attn_bwd(q, k, v, sinks, do, o, lse, cos_t, sin_t, dz, *, hook):
    def body(q_ref, k0_ref, kp_ref, kc_ref, v0_ref, vp_ref, vc_ref, sink_ref, do_ref, o_ref, lse_ref, c_ref, s_ref, dz_in,
             dq_ref, dk_ref, dv_ref, dsink_ref, dqrot_ref, s_scr, dp_scr, p_scr, ds_scr):
        i = pl.program_id(0)

        @pl.when(i == 0)
        def _():
            dk_ref[...] = jnp.zeros_like(dk_ref)
            dv_ref[...] = jnp.zeros_like(dv_ref)
            dsink_ref[...] = jnp.zeros_like(dsink_ref)

        mask = _attn_mask(i)
        row_starts = (0, pl.multiple_of(jnp.maximum(i - 1, 0) * BLK, BLK), pl.multiple_of(i * BLK, BLK))
        scale = HD ** -0.5
        nk = 3 * BLK
        for gp in range(N_KV // 2):
            q2 = _pair_heads(q_ref, gp)
            dom2 = _pair_heads(do_ref, gp).astype(_MXU)
            k2 = _pair_kv((k0_ref, kp_ref, kc_ref), gp)
            s_scr[...] = lax.dot_general(q2, k2, _NT, preferred_element_type=F32)
            dp_scr[...] = lax.dot_general(dom2, _pair_kv((v0_ref, vp_ref, vc_ref), gp), _NT, preferred_element_type=F32)
            for a in range(2):
                for j in range(GRP):
                    h = GRP * (2 * gp + a) + j
                    hs = slice(HD * h, HD * (h + 1))
                    rows = slice(BLK * j, BLK * (j + 1))
                    cols = slice(nk * a, nk * (a + 1))
                    lse_h = lse_ref[:, h:h + 1]
                    delta = jnp.sum(do_ref[:, hs] * o_ref[:, hs], axis=-1, keepdims=True)
                    dsink_ref[0:1, h:h + 1] += jnp.sum(-jnp.exp(sink_ref[h] - lse_h) * delta, axis=0, keepdims=True)
                    p = jnp.exp(jnp.where(mask, s_scr[rows, cols] * scale, NEG_INF) - lse_h)
                    p_scr[rows, cols] = p.astype(_MXU)
                    ds_scr[rows, cols] = (p * (dp_scr[rows, cols] - delta) * scale).astype(_MXU)
            ds = ds_scr[...]
            dq2 = jnp.dot(ds, k2, preferred_element_type=F32)
            dk2 = lax.dot_general(ds, q2, _TN, preferred_element_type=F32)
            dv2 = lax.dot_general(p_scr[...], dom2, _TN, preferred_element_type=F32)
            for a in range(2):
                g = 2 * gp + a
                gs = slice(HD * g, HD * (g + 1))
                for part in range(3):
                    rows = pl.ds(row_starts[part], BLK)
                    blk = slice(nk * a + BLK * part, nk * a + BLK * (part + 1))
                    dk_ref[rows, gs] += dk2[blk, HD * a:HD * (a + 1)]
                    dv_ref[rows, gs] += dv2[blk, HD * a:HD * (a + 1)]
                for j in range(GRP):
                    h = GRP * g + j
                    dqrot_ref[:, HD * h:HD * (h + 1)] = dq2[BLK * j:BLK * (j + 1), HD * a:HD * (a + 1)]
        dq_ref[...] = _rope(dqrot_ref[...], c_ref[...], s_ref[...], -1.0).astype(_MXU)

    row = pl.BlockSpec((BLK, D), lambda i: (i, 0))
    tab = pl.BlockSpec((BLK, 128), lambda i: (i, 0))
    full_kv = pl.BlockSpec((TP, D_KV), lambda i: (0, 0))
    return _pcall(
        body, name="attn_bwd", grid=(NBLK,),
        in_specs=[row] + _kv_specs() + _kv_specs() + [pl.BlockSpec(memory_space=pltpu.SMEM), row, row,
                  pl.BlockSpec((BLK, N_Q), lambda i: (i, 0)), tab, tab, _ANY],
        out_specs=[pl.BlockSpec((BLK, D), lambda i: (i, OFF_Q // D)), full_kv, full_kv,
                   pl.BlockSpec((8, 128), lambda i: (0, 0))],
        out_shape=[_sds((TP, D_IN), _MXU), _sds((TP, D_KV), F32), _sds((TP, D_KV), F32), _sds((8, 128), F32)],
        operands=[q, k, k, k, v, v, v, sinks, do, o, lse, cos_t, sin_t, dz],
        scratch=[pltpu.VMEM((BLK, D), F32), pltpu.VMEM((GRP * BLK, 6 * BLK), F32), pltpu.VMEM((GRP * BLK, 6 * BLK), F32),
                 pltpu.VMEM((GRP * BLK, 6 * BLK), _MXU), pltpu.VMEM((GRP * BLK, 6 * BLK), _MXU)],
        vmem=48, sem=("arbitrary",), aliases={13: 0}, hook=hook)


def _cast_into_slot(w32, pos, *, name, tr):
    r, cc = w32.shape

    def body(pos_ref, w_ref, o_ref):
        o_ref[...] = w_ref[...].astype(BF16)

    return _pcall(body, name=name, grid=(r // tr,), in_specs=[pl.BlockSpec((tr, cc), lambda i, p: (i, 0))],
                  out_specs=[pl.BlockSpec((None, tr, cc), lambda i, p: (p[1], i, 0))],
                  out_shape=[_sds((N_SH, r, cc), BF16)], operands=[w32], vmem=32, sem=("parallel",), prefetch=(pos,))[0]


def _pair_add(g, got, pos, *, name, tr, g_has_both_halves, hook=None):
    n, h, cc = got.shape
    nt = h // tr

    def body(pos_ref, g_ref, r_ref, own_ref, s16_ref):
        s = g_ref[...] + r_ref[...].astype(F32)
        s16_ref[...] = s.astype(BF16)

        @pl.when(pl.program_id(1) == pos_ref[1])
        def _():
            own_ref[...] = s

    g_index = (lambda i, s, p: (s, p[0] * nt + i, 0)) if g_has_both_halves else (lambda i, s, p: (s, i, 0))
    return _pcall(body, name=name, grid=(nt, n),
                  in_specs=[pl.BlockSpec((None, tr, cc), g_index), pl.BlockSpec((None, tr, cc), lambda i, s, p: (s, i, 0))],
                  out_specs=[pl.BlockSpec((tr, cc), lambda i, s, p: (i, 0)),
                             pl.BlockSpec((None, tr, cc), lambda i, s, p: (s, i, 0))],
                  out_shape=[_sds((h, cc), F32), _sds((n, h, cc), BF16)], operands=[g, got], vmem=40,
                  sem=("parallel", "arbitrary"), prefetch=(pos,), hook=hook)


def _relay_add(s16, raw, pos, *, name, tr):
    _, q, cc = raw.shape
    nt = q // tr

    def body(pos_ref, s_ref, r_ref, o_ref):
        o_ref[...] = (s_ref[...].astype(F32) + r_ref[...].astype(F32)).astype(BF16)

    blk = pl.BlockSpec((None, tr, cc), lambda k, i, p: (k, i, 0))
    return _pcall(body, name=name, grid=(2, nt),
                  in_specs=[pl.BlockSpec((None, tr, cc), lambda k, i, p: (p[3 + k], (1 - k) * nt + i, 0)), blk],
                  out_specs=[blk], out_shape=[_sds((2, q, cc), BF16)], operands=[s16, raw], vmem=40,
                  sem=("parallel", "parallel"), prefetch=(pos,))[0]


def _sum_chips(own, got, pos, *, name, tr):
    h, cc = own.shape
    n_got = got.shape[0]
    nt = h // tr

    def body(pos_ref, o_ref, r_ref, out_ref):
        acc = o_ref[...]
        for k in range(n_got):
            acc = acc + r_ref[k].astype(F32)
        out_ref[...] = acc

    return _pcall(body, name=name, grid=(nt,),
                  in_specs=[pl.BlockSpec((tr, cc), lambda i, p: (i, 0)),
                            pl.BlockSpec((n_got, tr, cc), lambda i, p: (0, i, 0))],
                  out_specs=[pl.BlockSpec((tr, cc), lambda i, p: (p[0] * nt + i, 0))],
                  out_shape=[_sds((2 * h, cc), F32)], operands=[own, got], vmem=40, sem=("parallel",), prefetch=(pos,))[0]


def _hand_over(arrays):
    def body(*refs):
        del refs

    n = len(arrays)
    return pl.pallas_call(body, name="hand_over", in_specs=[_ANY] * n, out_specs=[_ANY] * n,
                          out_shape=[_sds(a.shape, a.dtype) for a in arrays],
                          input_output_aliases={t: t for t in range(n)})(*arrays)


def _adamw(w, g, m, v, *, name, tr, g_row0=0):
    r, cc = w.shape
    g_blk0 = g_row0 // tr

    def body(*refs):
        _adamw_step(*refs)

    blk = pl.BlockSpec((tr, cc), lambda i: (i, 0))
    gblk = pl.BlockSpec((tr, cc), lambda i: (g_blk0 + i, 0))
    return _pcall(body, name=name, grid=(r // tr,), in_specs=[blk, gblk, blk, blk], out_specs=[blk] * 4,
                  out_shape=[_sds((r, cc), F32)] * 4, operands=[w, g, m, v], vmem=48, sem=("parallel",))


_SMALL_ROWS = 40
_B_IN_ROWS = 7


def _row_pad(v, rows):
    flat = v.reshape(-1)
    return jnp.pad(flat, (0, rows * D - flat.shape[0])).reshape(rows, D)


def _pack_ra(w):
    return w.transpose(1, 0, 2).reshape(64, D)


def _unpack_ra(p, like):
    return p.reshape(64, N_RB, RB).transpose(1, 0, 2).reshape(like.shape)


def _gate_full(g4):
    return g4.reshape(N_SH, 64, N_RB, RB).transpose(2, 0, 1, 3).reshape(N_RB, RB, RB)


def kernel(x, meta_tokens, ln_emb_g, ln_emb_b, w_in, b_in, conv_w, conv_b, w_ra, b_ra, w_ri, b_ri, lru_lambda, sinks, w_rnn_out, w_attn_out, w_o, b_o, ln_g, ln_b, loss_target, m_meta_tokens, m_ln_emb_g, m_ln_emb_b, m_w_in, m_b_in, m_conv_w, m_conv_b, m_w_ra, m_b_ra, m_w_ri, m_b_ri, m_lru_lambda, m_sinks, m_w_rnn_out, m_w_attn_out, m_w_o, m_b_o, m_ln_g, m_ln_b, v_meta_tokens, v_ln_emb_g, v_ln_emb_b, v_w_in, v_b_in, v_conv_w, v_conv_b, v_w_ra, v_b_ra, v_w_ri, v_b_ri, v_lru_lambda, v_sinks, v_w_rnn_out, v_w_attn_out, v_w_o, v_b_o, v_ln_g, v_ln_b):
    xi, yi, ci = _my_pos()
    shard = 2 * xi + yi
    pos = jnp.stack([ci, shard, 1 - ci, shard ^ 2, shard ^ 1]).astype(jnp.int32)
    cos_t, sin_t = _rope_tables()
    zero_bias = jnp.zeros((1, D), F32)
    ln_emb_g2, ln_emb_b2 = ln_emb_g[None], ln_emb_b[None]

    small = jnp.concatenate([conv_w[0], meta_tokens, jnp.zeros((4, 512), F32)], axis=0)
    small4 = _gather_small(small)
    conv_w_full = small4[:, 0:4].transpose(1, 0, 2).reshape(CONV_W, D)
    meta_full = small4[:, 4:20].transpose(1, 0, 2).reshape(N_META, D)
    w_own = _cast_into_slot(w_in[0], pos, name="cast_w_in", tr=256)
    wa_own = _cast_into_slot(jnp.concatenate([w_attn_out[0], w_o[0], _pack_ra(w_ra[0]), _pack_ra(w_ri[0])], axis=0),
                             pos, name="cast_w_a", tr=288)
    wb_own = _cast_into_slot(w_rnn_out[0], pos, name="cast_w_b", tr=256)

    h32, h16, h16_t = _ln_emb_fwd(x[0], meta_full, ln_emb_g2, ln_emb_b2)
    order = jnp.stack([shard, shard ^ 2, shard ^ 1, shard ^ 3]).astype(jnp.int32)
    z, w_in4 = _mm_z_gather(h16, w_own, b_in, order)
    q, k, v = _rope_fwd(z, cos_t, sin_t)
    o, lse, yb_in, yb_in_t, wa4 = _attn_fwd(q, k, v, sinks[0], z, hook=_hook_gather(wa_own, 0.6))
    w_ra_full = _gate_full(wa4[:, 2 * SQ_ROWS:2 * SQ_ROWS + 64])
    w_ri_full = _gate_full(wa4[:, 2 * SQ_ROWS + 64:2 * SQ_ROWS + 128])
    a_dec, u_in = _rnn_gates_fwd(z, conv_w_full, conv_b, w_ra_full, w_ri_full, b_ra, b_ri, lru_lambda)
    hr, wb4 = _scan_fwd(a_dec, u_in, hook=_hook_gather(wb_own, 0.6))
    sq_w = {0: (wb4, 0), 1: (wa4, 0), 2: (wa4, 1)}

    def sq_nn(a, kk, bias, name, out_cols, out_index, carried=None):
        wp, blk = sq_w[kk]
        return _mm_nn(a, wp, bias, name=name, grid=(2, D // CW), tm=HALF_TP, tn=CW, k=D, a_index=lambda i, j: (i, 0),
                      w_block=(N_SH, SQ_ROWS, CW), w_index=lambda i, j: (0, blk, j), out_cols=out_cols,
                      out_index=out_index, carried=carried)[0]

    def sq_nt(a, a_blk, kk, name, hook=None):
        wp, blk = sq_w[kk]
        return _mm_nt(a, wp, name=name, grid=(2, N_SH, 1), tm=HALF_TP, tn=SQ_ROWS, tk=D,
                      a_index=lambda i, j, q: (i, a_blk), w_block=(None, SQ_ROWS, D),
                      w_index=lambda i, j, q: (j, blk, 0), out_cols=D, hook=hook)

    ya_in, ya_in_t = _mul_silu_fwd(hr, z, OFF_GR, name="gate_a_fwd")
    y2 = sq_nn(ya_in, 0, zero_bias, "mm_ya", 2 * D, lambda i, j: (i, j))
    y2 = sq_nn(yb_in, 1, zero_bias, "mm_yb", 2 * D, lambda i, j: (i, D // CW + j), carried=y2)
    mixed, mixed_t = _merge_fwd(y2, z)
    out = sq_nn(mixed, 2, b_o, "mm_out", D, lambda i, j: (i, j))
    dr, dr16, sums_o = _ln_out_loss(h32, out, loss_target[0], ln_g, ln_b)

    def sq_tn(at, b, b_blk, kk, name, carried=None):
        return _mm_tn(at, b, name=name, grid=(N_SH, 1), tm=SQ_ROWS, tn=D, a_index=lambda i, j: (i, 0),
                      b_index=lambda i, j: (0, b_blk), out_shape=(N_SH, PACK_ROWS, D),
                      out_block=(None, SQ_ROWS, D), out_index=lambda i, j: (i, kk, 0), carried=carried)[0]

    gsq = sq_tn(mixed_t, dr16, 0, 2, "mm_dwo")
    dmix = sq_nt(dr16, 0, 2, "mm_dmix")[0]
    dy2, dz = _merge_bwd(dmix, y2, z)
    gsq = sq_tn(ya_in_t, dy2, 0, 0, "mm_dwrnn", carried=gsq)
    gsq = sq_tn(yb_in_t, dy2, 1, 1, "mm_dwattn", carried=gsq)
    dya_in = sq_nt(dy2, 0, 0, "mm_dyain")[0]
    dhr, dz = _mul_silu_bwd(dya_in, hr, z, OFF_GR, dz, name="gate_a_bwd")
    lam_s = _scan_bwd(a_dec, dhr)
    dz, gsq, sums_r = _rnn_gates_bwd(z, lam_s, hr, conv_w_full, conv_b, w_ra_full, w_ri_full, b_ra, b_ri, lru_lambda, dz, gsq)
    dyb_in, gsq, got_sq = sq_nt(dy2, 1, 1, "mm_dybin", hook=_hook_pair(gsq, True, (0, 1)))
    do, dz, gsq, got_sq = _mul_silu_bwd(dyb_in, o, z, OFF_GA, dz, name="gate_b_bwd",
                                        hook=_hook_pair(gsq, True, (2, 3), land=got_sq))
    own_sq, s16_sq = _pair_add(gsq, got_sq, pos, name="red_w_sq_add", tr=208, g_has_both_halves=True)
    dz, dk_rot, dv32, dsink, s16_sq, fin_sq, raw_sq = _attn_bwd(q, k, v, sinks[0], do, o, lse, cos_t, sin_t, dz,
                                                                hook=_hook_scatter_direct(s16_sq))
    dz = _rope_bwd_k(dk_rot, dv32, cos_t, sin_t, dz)
    comb_sq = _relay_add(s16_sq, raw_sq, pos, name="red_w_sq_relay", tr=208)

    def dwin(half_idx, name, hook, **kw):
        return _mm_tn(h16_t, dz, name=name, grid=(1, N_SH * PER_IN), tm=D // 2, tn=TN_IN,
                      a_index=lambda i, j, p: (p[half_idx], 0), b_index=lambda i, j, p: (0, j),
                      out_shape=(N_SH, D // 2, W_IN_COLS), out_block=(None, D // 2, TN_IN),
                      out_index=lambda i, j, p: (j // PER_IN, 0, j % PER_IN), prefetch=(pos,), hook=hook, **kw)

    gin_sib, g_b_in, comb_sq, fin_sq = dwin(2, "mm_dwin_sib", _hook_scatter_relay(comb_sq, fin_sq),
                                            out_dtype=BF16, colsum=True)
    red_sq = _sum_chips(own_sq, fin_sq, pos, name="red_w_sq_sum", tr=208)
    gin_own, gin_sib, got_in = dwin(0, "mm_dwin_own", _hook_pair(gin_sib, False))
    own_in, s16_in, g_sq = _pair_add(gin_own, got_in, pos, name="red_w_in_add", tr=128, g_has_both_halves=False,
                                     hook=_hook_halves(red_sq))
    dhz, s16_in, fin_in, raw_in = _mm_nt(dz, w_in4, name="mm_dhz", grid=(2, 2, N_SH), tm=HALF_TP, tn=D // 2,
                                         tk=W_IN_COLS, a_index=lambda i, j, q: (i, q),
                                         w_block=(None, D // 2, W_IN_COLS), w_index=lambda i, j, q: (q, j, 0), out_cols=D,
                                         hook=_hook_scatter_direct(s16_in))
    comb_in = _relay_add(s16_in, raw_in, pos, name="red_w_in_relay", tr=128)
    sq_params = [("w_rnn_out", w_rnn_out, m_w_rnn_out, v_w_rnn_out), ("w_attn_out", w_attn_out, m_w_attn_out, v_w_attn_out),
                 ("w_o", w_o, m_w_o, v_w_o)]
    g_x, g_meta_local, sums_e, *sq_res, comb_in, fin_in = _ln_emb_bwd(
        x[0], meta_full, ln_emb_g2, dr, dhz, g_sq, [(w_[0], m_[0], v_[0]) for _, w_, m_, v_ in sq_params],
        hook=_hook_scatter_relay(comb_in, fin_in))
    red_in = _sum_chips(own_in, fin_in, pos, name="red_w_in_sum", tr=128)

    spack = jnp.concatenate([
        sums_e[0:1], sums_e[1:2], _row_pad(g_b_in, _B_IN_ROWS), sums_r[0:4], sums_r[4:5], sums_r[5:6], sums_r[6:7],
        sums_r[7:8], _row_pad(dsink[0:1, 0:N_Q], 1), sums_o[2:3], sums_o[0:1], sums_o[1:2], g_meta_local, sums_o[3:4],
        jnp.zeros((_SMALL_ROWS - 38, D), F32)], axis=0)
    sred, g_in = _allreduce_small(spack, red_in)

    big = {"w_in": [t.reshape(w_in.shape) for t in
                    _adamw(w_in[0], g_in, m_w_in[0], v_w_in[0], name="adamw_w_in", tr=128)]}
    g_x, *sq_res = _hand_over([g_x] + sq_res)
    for kk, (n, w_, _, _) in enumerate(sq_params):
        big[n] = [t.reshape(w_.shape) for t in sq_res[4 * kk:4 * kk + 4]]
    for kk, (n, w_, m_, v_) in enumerate([("w_ra", w_ra, m_w_ra, v_w_ra), ("w_ri", w_ri, m_w_ri, v_w_ri)]):
        big[n] = [_unpack_ra(t, w_) for t in
                  _adamw(_pack_ra(w_[0]), g_sq, _pack_ra(m_[0]), _pack_ra(v_[0]), name="adamw_" + n, tr=64,
                         g_row0=3 * SQ_ROWS + 64 * kk)]

    loss = sred[37, 0]
    col0 = shard * 512
    g_conv_w = lax.dynamic_slice(sred[9:13], (0, col0), (CONV_W, 512))
    g_meta = lax.dynamic_slice(sred[21:37], (0, col0), (N_META, 512))
    small_g = {"ln_emb_g": sred[0:1], "ln_emb_b": sred[1:2], "b_in": sred[2:9], "conv_w": g_conv_w.reshape(1, D),
               "conv_b": sred[13:14], "b_ra": sred[14:15], "b_ri": sred[15:16], "lru_lambda": sred[16:17],
               "sinks": sred[17:18], "b_o": sred[18:19], "ln_g": sred[19:20], "ln_b": sred[20:21],
               "meta_tokens": g_meta.reshape(4, D)}
    small_names = list(small_g)

    def small_pack(vals):
        rows = []
        for n in small_names:
            a = vals[n]
            if n == "b_in":
                a = _row_pad(a, _B_IN_ROWS)
            elif n == "sinks":
                a = _row_pad(a, 1)
            else:
                a = a.reshape(-1, D)
            rows.append(a)
        return jnp.concatenate(rows + [jnp.zeros((24 - 22, D), F32)], axis=0)

    w_small = dict(ln_emb_g=ln_emb_g, ln_emb_b=ln_emb_b, b_in=b_in, conv_w=conv_w, conv_b=conv_b, b_ra=b_ra, b_ri=b_ri,
                   lru_lambda=lru_lambda, sinks=sinks, b_o=b_o, ln_g=ln_g, ln_b=ln_b, meta_tokens=meta_tokens)
    m_small = dict(ln_emb_g=m_ln_emb_g, ln_emb_b=m_ln_emb_b, b_in=m_b_in, conv_w=m_conv_w, conv_b=m_conv_b, b_ra=m_b_ra,
                   b_ri=m_b_ri, lru_lambda=m_lru_lambda, sinks=m_sinks, b_o=m_b_o, ln_g=m_ln_g, ln_b=m_ln_b,
                   meta_tokens=m_meta_tokens)
    v_small = dict(ln_emb_g=v_ln_emb_g, ln_emb_b=v_ln_emb_b, b_in=v_b_in, conv_w=v_conv_w, conv_b=v_conv_b, b_ra=v_b_ra,
                   b_ri=v_b_ri, lru_lambda=v_lru_lambda, sinks=v_sinks, b_o=v_b_o, ln_g=v_ln_g, ln_b=v_ln_b,
                   meta_tokens=v_meta_tokens)
    g_small_pack = jnp.concatenate([small_g[n] for n in small_names] + [jnp.zeros((2, D), F32)], axis=0)
    small_res = _adamw(small_pack(w_small), g_small_pack, small_pack(m_small), small_pack(v_small),
                       name="adamw_small", tr=24)

    small_rows = {}
    r0 = 0
    for n in small_names:
        nrows = small_g[n].shape[0]
        small_rows[n] = (r0, nrows)
        r0 += nrows

    def small_out(packed, n, like):
        a, nrows = small_rows[n]
        flat = packed[a:a + nrows].reshape(-1)
        return flat[:like.size].reshape(like.shape)

    weights = dict(meta_tokens=meta_tokens, ln_emb_g=ln_emb_g, ln_emb_b=ln_emb_b, w_in=w_in, b_in=b_in, conv_w=conv_w,
                   conv_b=conv_b, w_ra=w_ra, b_ra=b_ra, w_ri=w_ri, b_ri=b_ri, lru_lambda=lru_lambda, sinks=sinks,
                   w_rnn_out=w_rnn_out, w_attn_out=w_attn_out, w_o=w_o, b_o=b_o, ln_g=ln_g, ln_b=ln_b)

    def outputs(which):
        return [big[n][which] if n in big else small_out(small_res[which], n, like) for n, like in weights.items()]

    return (loss, g_x[None], *outputs(0), *outputs(1), *outputs(2), *outputs(3))
```

```python
import jax
import jax.numpy as jnp
from jax import lax
from jax.experimental import pallas as pl
from jax.experimental.pallas import tpu as pltpu

F32 = jnp.float32
BF16 = jnp.bfloat16
_MXU = jnp.bfloat16

D = 2048
SEQ = 2048
N_META = 16
BLK = 128
PAD = BLK - N_META
TP = PAD + N_META + SEQ
NBLK = TP // BLK
HALF_TP = TP // 2
N_RB = 8
RB = 256
CONV_W = 4
LRU_C = 8.0
HD = 64
N_Q = 32
N_KV = 4
GRP = 8
D_KV = 256
NEG_INF = -1e30
LN_EPS = 1e-5
ALPHA = 2.0 ** 0.25
ROPE_THETA = 10000.0
OFF_GR, OFF_Q, OFF_K, OFF_V, OFF_GA, OFF_G = 2048, 4096, 6144, 6400, 6656, 8704
D_IN = 12800
N_SH = 4
W_IN_COLS = D_IN // N_SH
TN_IN = 640
PER_IN = W_IN_COLS // TN_IN
CW = 512
RT = TP // 4
SQ_ROWS = 512
PACK_ROWS = 3 * SQ_ROWS + 128

ADAM_LR = 0.001
ADAM_B1 = 0.9
ADAM_B2 = 0.999
ADAM_EPS = 1e-08
ADAM_WD = 0.01
ADAM_STEP = 10

MESH = pl.DeviceIdType.MESH
_MIB = 1024 * 1024
_ANY = pl.BlockSpec(memory_space=pl.ANY)
_NT = (((1,), (1,)), ((), ()))
_TN = (((0,), (0,)), ((), ()))


def _sds(shape, dtype):
    return jax.ShapeDtypeStruct(shape, dtype)


def _sigmoid(x):
    return 1.0 / (1.0 + jnp.exp(-x))


def _my_pos():
    return lax.axis_index("x"), lax.axis_index("y"), lax.axis_index("c")


def _other_chips(x, y):
    return [(1 - x, y), (x, 1 - y), (1 - x, 1 - y)]


def _remote(src, dst, send_sems, recv_sems, k, dev):
    return pltpu.make_async_remote_copy(src_ref=src, dst_ref=dst, send_sem=send_sems.at[k], recv_sem=recv_sems.at[k],
                                        device_id=dev, device_id_type=MESH)


class _Hook:
    def __init__(self, carried, landing, n_sems, start, finish, mid=None, mid_frac=0.5):
        self.carried, self.landing, self.n_sems, self.start, self.finish = list(carried), list(landing), n_sems, start, finish
        self.mid, self.mid_frac = mid, mid_frac


def _hook_gather(buf, mid_frac):
    half = buf.shape[1] // 2
    quarter = half // 2

    def geom(o, ss, rs):
        x, y, c = _my_pos()
        xn, yn, dg = _other_chips(x, y)
        slot = lambda p: 2 * p[0] + p[1]
        mine_rows = pl.ds(pl.multiple_of(c * half, 16), half)
        sib_rows = pl.ds(pl.multiple_of((1 - c) * half, 16), half)
        q_rows = lambda r: pl.ds(pl.multiple_of(c * half + r * quarter, 16), quarter)

        def cp(k, s, rows, dev):
            part = o.at[s, rows]
            return _remote(part, part, ss, rs, k, dev)

        return dict(
            direct=lambda k, s: cp(k, s, mine_rows, ((xn, yn)[k][0], (xn, yn)[k][1], c)),
            relay=lambda r, s: cp(2 + r, s, q_rows(r), ((yn, xn)[r][0], (yn, xn)[r][1], c)),
            sibling=lambda k, s, mine: cp(4 + k, s, mine_rows if mine else sib_rows, (x, y, 1 - c)),
            me=slot((x, y)), slots=(slot(xn), slot(yn), slot(dg)))

    def start(car, land, ss, rs):
        g = geom(car[0], ss, rs)
        g["direct"](0, g["me"]).start()
        g["direct"](1, g["me"]).start()

    def mid(car, land, ss, rs):
        g = geom(car[0], ss, rs)
        for k in range(2):
            g["direct"](k, g["slots"][k]).wait_recv()
            g["relay"](k, g["slots"][k]).start()
            g["sibling"](k, g["slots"][k], True).start()

    def finish(car, land, ss, rs):
        g = geom(car[0], ss, rs)
        dslot = g["slots"][2]
        g["relay"](0, dslot).wait_recv()
        g["relay"](1, dslot).wait_recv()
        g["sibling"](2, dslot, True).start()
        for k in range(3):
            g["sibling"](k, g["slots"][k], False).wait_recv()
        for k in range(2):
            g["direct"](k, g["me"]).wait_send()
            g["relay"](k, g["slots"][k]).wait_send()
        for k in range(3):
            g["sibling"](k, g["slots"][k], True).wait_send()

    return _Hook([buf], [], 7, start, finish, mid=mid, mid_frac=mid_frac)


def _hook_pair(g, half_rows, slots=(0, 1, 2, 3), land=None):
    n, r, cc = g.shape
    h = r // 2 if half_rows else r

    def plan(car, landing, ss, rs):
        x, y, c = _my_pos()
        dst = landing[0] if land is None else car[1]
        cps = []
        for k, s in enumerate(slots):
            src = car[0].at[s, pl.ds(pl.multiple_of((1 - c) * h, 8), h)] if half_rows else car[0].at[s]
            cps.append(_remote(src, dst.at[s], ss, rs, k, (x, y, 1 - c)))
        return cps

    def start(car, land, ss, rs):
        for cp in plan(car, land, ss, rs):
            cp.start()

    def finish(car, land, ss, rs):
        cps = plan(car, land, ss, rs)
        for cp in cps:
            cp.wait_recv()
        for cp in cps:
            cp.wait_send()

    if land is None:
        return _Hook([g], [_sds((n, h, cc), g.dtype)], len(slots), start, finish)
    return _Hook([g, land], [], len(slots), start, finish)


def _hook_scatter_direct(s16):
    _, h, cc = s16.shape
    q = h // 2

    def plan(car, land, ss, rs):
        x, y, c = _my_pos()
        xn, yn, dg = _other_chips(x, y)
        s, (final, raw) = car[0], land
        slot = lambda p: 2 * p[0] + p[1]
        q0, q1 = pl.ds(0, q), pl.ds(q, q)
        return [_remote(s.at[slot(xn), q0], final.at[0, q0], ss, rs, 0, (xn[0], xn[1], c)),
                _remote(s.at[slot(yn), q1], final.at[1, q1], ss, rs, 1, (yn[0], yn[1], c)),
                _remote(s.at[slot(dg), q0], raw.at[1], ss, rs, 2, (xn[0], xn[1], c)),
                _remote(s.at[slot(dg), q1], raw.at[0], ss, rs, 3, (yn[0], yn[1], c))]

    def start(car, land, ss, rs):
        for cp in plan(car, land, ss, rs):
            cp.start()

    def finish(car, land, ss, rs):
        cps = plan(car, land, ss, rs)
        for cp in cps:
            cp.wait_recv()
        for cp in cps:
            cp.wait_send()

    return _Hook([s16], [_sds((2, h, cc), s16.dtype), _sds((2, q, cc), s16.dtype)], 4, start, finish)


def _hook_scatter_relay(comb, final):
    _, q, _ = comb.shape

    def plan(car, ss, rs):
        x, y, c = _my_pos()
        xn, yn, _ = _other_chips(x, y)
        cb, final_ref = car
        return [_remote(cb.at[0], final_ref.at[0, pl.ds(q, q)], ss, rs, 0, (xn[0], xn[1], c)),
                _remote(cb.at[1], final_ref.at[1, pl.ds(0, q)], ss, rs, 1, (yn[0], yn[1], c))]

    def start(car, land, ss, rs):
        for cp in plan(car, ss, rs):
            cp.start()

    def finish(car, land, ss, rs):
        cps = plan(car, ss, rs)
        for cp in cps:
            cp.wait_recv()
        for cp in cps:
            cp.wait_send()

    return _Hook([comb, final], [], 2, start, finish)


def _hook_halves(full):
    h = full.shape[0] // 2

    def half_copy(car, ss, rs, which):
        x, y, c = _my_pos()
        rows = car[0].at[pl.ds(pl.multiple_of((c + which - 2 * c * which) * h, 8), h)]
        return _remote(rows, rows, ss, rs, 0, (x, y, 1 - c))

    def start(car, land, ss, rs):
        half_copy(car, ss, rs, 0).start()

    def finish(car, land, ss, rs):
        half_copy(car, ss, rs, 1).wait_recv()
        half_copy(car, ss, rs, 0).wait_send()

    return _Hook([full], [], 1, start, finish)


def _pcall(body, *, name, grid, in_specs, out_specs, out_shape, operands, scratch=(), vmem=48, sem=None,
           prefetch=(), aliases=None, hook=None):
    n_pre, n_in, n_out, n_scr = len(prefetch), len(in_specs), len(out_specs), len(scratch)
    in_specs, out_specs, out_shape, scratch = list(in_specs), list(out_specs), list(out_shape), list(scratch)
    io_alias = {n_pre + a: b for a, b in (aliases or {}).items()}
    operands = list(operands)
    kernel_body = body
    if hook is not None:
        n_car, n_land = len(hook.carried), len(hook.landing)
        for t, arr in enumerate(hook.carried):
            io_alias[n_pre + n_in + t] = n_out + t
        in_specs += [_ANY] * n_car
        out_specs += [_ANY] * (n_car + n_land)
        out_shape += [_sds(a.shape, a.dtype) for a in hook.carried] + hook.landing
        scratch += [pltpu.SemaphoreType.DMA((hook.n_sems,)), pltpu.SemaphoreType.DMA((hook.n_sems,))]
        operands += hook.carried
        sem = ("arbitrary",) * len(grid)

        def kernel_body(*refs):
            pre, rest = refs[:n_pre], refs[n_pre:]
            ins = rest[:n_in]
            outs = rest[n_in + n_car:n_in + n_car + n_out]
            car = rest[n_in + n_car + n_out:n_in + 2 * n_car + n_out]
            land = rest[n_in + 2 * n_car + n_out:n_in + 2 * n_car + n_out + n_land]
            scr = rest[n_in + 2 * n_car + n_out + n_land:]
            send_sems, recv_sems = scr[n_scr], scr[n_scr + 1]
            first = pl.program_id(0) == 0
            last = pl.program_id(0) == grid[0] - 1
            for d in range(1, len(grid)):
                first = first & (pl.program_id(d) == 0)
                last = last & (pl.program_id(d) == grid[d] - 1)

            @pl.when(first)
            def _():
                hook.start(car, land, send_sems, recv_sems)

            if hook.mid is not None:
                step = pl.program_id(0)
                total = grid[0]
                for d in range(1, len(grid)):
                    step = step * grid[d] + pl.program_id(d)
                    total *= grid[d]

                @pl.when(step == int(total * hook.mid_frac))
                def _():
                    hook.mid(car, land, send_sems, recv_sems)

            body(*pre, *ins, *outs, *scr[:n_scr])

            @pl.when(last)
            def _():
                hook.finish(car, land, send_sems, recv_sems)

    params = pltpu.CompilerParams(vmem_limit_bytes=vmem * _MIB, dimension_semantics=sem,
                                  has_side_effects=hook is not None)
    if n_pre:
        call = pl.pallas_call(
            kernel_body, name=name, out_shape=out_shape, input_output_aliases=io_alias, compiler_params=params,
            grid_spec=pltpu.PrefetchScalarGridSpec(num_scalar_prefetch=n_pre, grid=grid, in_specs=in_specs,
                                                   out_specs=out_specs, scratch_shapes=scratch))
    else:
        call = pl.pallas_call(kernel_body, name=name, grid=grid, in_specs=in_specs, out_specs=out_specs,
                              out_shape=out_shape, scratch_shapes=scratch, input_output_aliases=io_alias,
                              compiler_params=params)
    return call(*prefetch, *operands)


def _gather_small(blk):
    r, cc = blk.shape

    def body(x_ref, o_ref, send_sems, recv_sems):
        x, y, c = _my_pos()
        me = 2 * x + y
        o_ref[me] = x_ref[...]
        sends = [_remote(x_ref, o_ref.at[me], send_sems, recv_sems, k, (px, py, c))
                 for k, (px, py) in enumerate(_other_chips(x, y))]
        for cp in sends:
            cp.start()
        for k, (px, py) in enumerate(_other_chips(x, y)):
            _remote(x_ref, o_ref.at[2 * px + py], send_sems, recv_sems, k, (px, py, c)).wait_recv()
        for cp in sends:
            cp.wait_send()

    vm = pl.BlockSpec(memory_space=pltpu.VMEM)
    return pl.pallas_call(
        body, name="gather_small", in_specs=[vm], out_specs=vm, out_shape=_sds((N_SH, r, cc), blk.dtype),
        scratch_shapes=[pltpu.SemaphoreType.DMA((3,)), pltpu.SemaphoreType.DMA((3,))],
        compiler_params=pltpu.CompilerParams(has_side_effects=True),
    )(blk)


N_DEV = 8


def _allreduce_small(pack, halves):
    r, cc = pack.shape
    ride = _hook_halves(halves)

    def body(x_ref, h_in_ref, o_ref, h_ref, buf_ref, send_sems, recv_sems, h_ss, h_rs):
        del h_in_ref
        ride.start([h_ref], [], h_ss, h_rs)
        x, y, c = _my_pos()
        me = 4 * x + 2 * y + c
        buf_ref[me] = x_ref[...]
        cps = []
        for k in range(1, N_DEV):
            peer = (x ^ ((k >> 2) & 1), y ^ ((k >> 1) & 1), c ^ (k & 1))
            cps.append(_remote(x_ref, buf_ref.at[me], send_sems, recv_sems, k - 1, peer))
        for cp in cps:
            cp.start()
        for k in range(1, N_DEV):
            peer = (x ^ ((k >> 2) & 1), y ^ ((k >> 1) & 1), c ^ (k & 1))
            src = 4 * peer[0] + 2 * peer[1] + peer[2]
            _remote(x_ref, buf_ref.at[src], send_sems, recv_sems, k - 1, peer).wait_recv()
        acc = buf_ref[0]
        for d in range(1, N_DEV):
            acc = acc + buf_ref[d]
        o_ref[...] = acc
        for cp in cps:
            cp.wait_send()
        ride.finish([h_ref], [], h_ss, h_rs)

    vm = pl.BlockSpec(memory_space=pltpu.VMEM)
    return pl.pallas_call(
        body, name="allreduce_small", in_specs=[vm, _ANY], out_specs=[vm, _ANY],
        out_shape=[_sds((r, cc), F32), _sds(halves.shape, halves.dtype)], input_output_aliases={1: 1},
        scratch_shapes=[pltpu.VMEM((N_DEV, r, cc), F32), pltpu.SemaphoreType.DMA((N_DEV - 1,)),
                        pltpu.SemaphoreType.DMA((N_DEV - 1,)), pltpu.SemaphoreType.DMA((1,)), pltpu.SemaphoreType.DMA((1,))],
        compiler_params=pltpu.CompilerParams(has_side_effects=True),
    )(pack, halves)


def _mm_nn(a, w, bias, *, name, grid, tm, tn, k, a_index, w_block, w_index, out_cols, out_index, carried=None, hook=None):
    m = a.shape[0]

    def body(a_ref, w_ref, b_ref, *rest):
        o_ref = rest[-1]
        wv = w_ref[...]
        acc = jnp.dot(a_ref[...].astype(_MXU), wv.reshape(k, tn).astype(_MXU), preferred_element_type=F32)
        o_ref[...] = acc + b_ref[...]

    operands = [a, w, bias] + ([carried] if carried is not None else [])
    return _pcall(
        body, name=name, grid=grid,
        in_specs=[pl.BlockSpec((tm, k), a_index), pl.BlockSpec(w_block, w_index),
                  pl.BlockSpec((1, tn), lambda i, j: (0, j))] + ([_ANY] if carried is not None else []),
        out_specs=[pl.BlockSpec((tm, tn), out_index)], out_shape=[_sds((m, out_cols), F32)], operands=operands,
        vmem=56, sem=("parallel", "parallel"), aliases={3: 0} if carried is not None else None, hook=hook)


def _mm_nt(a, w, *, name, grid, tm, tn, tk, a_index, w_block, w_index, out_cols, hook=None):
    m = a.shape[0]
    nk = grid[2]

    def body(a_ref, w_ref, o_ref, acc_ref):
        part = lax.dot_general(a_ref[...].astype(_MXU), w_ref[...].astype(_MXU), _NT, preferred_element_type=F32)
        if nk == 1:
            o_ref[...] = part
        else:
            kidx = pl.program_id(2)

            @pl.when(kidx == 0)
            def _():
                acc_ref[...] = part

            @pl.when(kidx > 0)
            def _():
                acc_ref[...] += part

            @pl.when(kidx == nk - 1)
            def _():
                o_ref[...] = acc_ref[...]

    return _pcall(
        body, name=name, grid=grid, in_specs=[pl.BlockSpec((tm, tk), a_index), pl.BlockSpec(w_block, w_index)],
        out_specs=[pl.BlockSpec((tm, tn), lambda i, j, q: (i, j))], out_shape=[_sds((m, out_cols), F32)],
        operands=[a, w], scratch=[pltpu.VMEM((tm, tn) if nk > 1 else (8, 128), F32)], vmem=60,
        sem=("parallel", "parallel", "arbitrary"), hook=hook)


def _mm_tn(at, b, *, name, grid, tm, tn, a_index, b_index, out_shape, out_block, out_index, carried=None,
           prefetch=(), hook=None, out_dtype=F32, colsum=False):
    t = at.shape[1]
    n_pre = len(prefetch)

    def body(*refs):
        a_ref, b_ref = refs[n_pre], refs[n_pre + 1]
        bv = b_ref[...]
        o_ref = refs[-2] if colsum else refs[-1]
        o_ref[...] = jnp.dot(a_ref[...].astype(_MXU), bv.astype(_MXU), preferred_element_type=F32).astype(out_dtype)
        if colsum:
            refs[-1][...] = jnp.sum(bv.astype(F32), axis=0, keepdims=True)

    operands = [at, b] + ([carried] if carried is not None else [])
    out_specs = [pl.BlockSpec(out_block, out_index)]
    out_shapes = [_sds(out_shape, out_dtype)]
    if colsum:
        out_specs.append(pl.BlockSpec((1, tn), b_index))
        out_shapes.append(_sds((1, b.shape[1]), F32))
    return _pcall(
        body, name=name, grid=grid,
        in_specs=[pl.BlockSpec((tm, t), a_index), pl.BlockSpec((t, tn), b_index)] + ([_ANY] if carried is not None else []),
        out_specs=out_specs, out_shape=out_shapes, operands=operands,
        vmem=56, sem=("parallel", "parallel"), aliases={2: 0} if carried is not None else None, prefetch=prefetch, hook=hook)


def _remote_tile(n):
    near = 2 * PER_IN
    if isinstance(n, int):
        return (n % 2, n // 2) if n < near else (2, n - near)
    return jnp.where(n < near, n % 2, 2), jnp.where(n < near, n // 2, n - near)


def _mm_z_gather(h16, w_own, b_in, order):
    n_tiles = N_SH * PER_IN
    n_remote = 3 * PER_IN
    half = D // 2

    def body(order_ref, a_ref, b_ref, w_in_ref, z_ref, w_ref, wbuf, tile_sems, ss, rs):
        del w_in_ref
        j = pl.program_id(0)
        x, y, c = _my_pos()
        me = 2 * x + y
        chips = _other_chips(x, y)
        mine_rows = pl.ds(pl.multiple_of(c * half, 16), half)
        sib_rows = pl.ds(pl.multiple_of((1 - c) * half, 16), half)

        slots = [2 * px + py for px, py in chips]
        cols = lambda t: pl.ds(t * TN_IN, TN_IN)
        q_rows = lambda r: pl.ds(pl.multiple_of(c * half + r * (half // 2), 16), half // 2)

        def direct(rel, t, slot):
            px, py = chips[rel]
            part = w_ref.at[slot, mine_rows, cols(t)]
            return _remote(part, part, ss, rs, 2 * t + rel, (px, py, c))

        def relay(r, t, slot):
            px, py = chips[1 - r]
            part = w_ref.at[slot, q_rows(r), cols(t)]
            return _remote(part, part, ss, rs, 2 * PER_IN + 2 * t + r, (px, py, c))

        def d2d(n, rows):
            rel, t = _remote_tile(n)
            part = w_ref.at[slots[rel], rows, cols(t)]
            return _remote(part, part, ss, rs, 4 * PER_IN + n, (x, y, 1 - c))

        def tile_copy(step):
            rel, t = _remote_tile(jnp.maximum(step - PER_IN, 0))
            slot = jnp.where(step < PER_IN, me, me ^ jnp.where(rel == 0, 2, jnp.where(rel == 1, 1, 3)))
            col = pl.multiple_of(jnp.where(step < PER_IN, step, t) * TN_IN, 128)
            return pltpu.make_async_copy(w_ref.at[slot, :, pl.ds(col, TN_IN)], wbuf.at[step % 2], tile_sems.at[step % 2])

        @pl.when(j == 0)
        def _():
            for t in range(PER_IN):
                direct(0, t, me).start()
                direct(1, t, me).start()
            tile_copy(0).start()

        for n in range(n_remote):
            rel, t = _remote_tile(n)

            @pl.when(j == n + PER_IN - 3)
            def _():
                if rel < 2:
                    direct(rel, t, slots[rel]).wait_recv()
                    relay(rel, t, slots[rel]).start()
                else:
                    relay(0, t, slots[2]).wait_recv()
                    relay(1, t, slots[2]).wait_recv()
                d2d(n, mine_rows).start()

            @pl.when(j == n + PER_IN - 2)
            def _():
                d2d(n, sib_rows).wait_recv()

        @pl.when(j + 1 < n_tiles)
        def _():
            tile_copy(j + 1).start()

        tile_copy(j).wait()
        z_ref[...] = jnp.dot(a_ref[...], wbuf[j % 2], preferred_element_type=F32) + b_ref[...]

        @pl.when(j == n_tiles - 1)
        def _():
            for t in range(PER_IN):
                for r in range(2):
                    direct(r, t, me).wait_send()
                    relay(r, t, slots[r]).wait_send()
            for n in range(n_remote):
                d2d(n, mine_rows).wait_send()

    def col_tile(j, o):
        rel, t = _remote_tile(jnp.maximum(j - PER_IN, 0))
        return 0, jnp.where(j < PER_IN, o[0] * PER_IN + j, o[1 + rel] * PER_IN + t)

    return pl.pallas_call(
        body, name="mm_z_gather",
        grid_spec=pltpu.PrefetchScalarGridSpec(
            num_scalar_prefetch=1, grid=(n_tiles,),
            in_specs=[pl.BlockSpec((TP, D), lambda j, o: (0, 0)), pl.BlockSpec((1, TN_IN), col_tile), _ANY],
            out_specs=[pl.BlockSpec((TP, TN_IN), col_tile), _ANY],
            scratch_shapes=[pltpu.VMEM((2, D, TN_IN), BF16), pltpu.SemaphoreType.DMA((2,)),
                            pltpu.SemaphoreType.DMA((4 * PER_IN + n_remote,)),
                            pltpu.SemaphoreType.DMA((4 * PER_IN + n_remote,))]),
        out_shape=[_sds((TP, D_IN), F32), _sds(w_own.shape, w_own.dtype)],
        input_output_aliases={3: 1},
        compiler_params=pltpu.CompilerParams(vmem_limit_bytes=60 * _MIB, dimension_semantics=("arbitrary",),
                                             has_side_effects=True),
    )(order, h16, b_in, w_own)


def _padded_rows(i, x_ref, meta_ref):
    head = jnp.concatenate([jnp.zeros((PAD, D), F32), meta_ref[...]], axis=0)
    return jnp.where(i == 0, head, x_ref[...])


def _stream_specs():
    return [pl.BlockSpec((BLK, D), lambda i: (jnp.maximum(i - 1, 0), 0)), pl.BlockSpec((N_META, D), lambda i: (0, 0))]


def _ln_emb_fwd(x, meta, g, b):
    def body(x_ref, meta_ref, g_ref, b_ref, h32_ref, h16_ref, h16t_ref):
        x = _padded_rows(pl.program_id(0), x_ref, meta_ref)
        mu = jnp.mean(x, axis=-1, keepdims=True)
        xc = x - mu
        var = jnp.mean(xc * xc, axis=-1, keepdims=True)
        y = xc * lax.rsqrt(var + LN_EPS) * g_ref[...] + b_ref[...]
        h32_ref[...] = y
        h16_ref[...] = y.astype(_MXU)
        h16t_ref[...] = y.T.astype(_MXU)

    row = pl.BlockSpec((BLK, D), lambda i: (i, 0))
    vec = pl.BlockSpec((1, D), lambda i: (0, 0))
    return _pcall(body, name="ln_emb_fwd", grid=(NBLK,), in_specs=_stream_specs() + [vec, vec],
                  out_specs=[row, row, pl.BlockSpec((D, BLK), lambda i: (0, i))],
                  out_shape=[_sds((TP, D), F32), _sds((TP, D), _MXU), _sds((D, TP), _MXU)], operands=[x, meta, g, b],
                  vmem=32, sem=("parallel",))


def _adamw_step(w_ref, g_ref, m_ref, v_ref, go_ref, d_ref, mo_ref, vo_ref):
    gg = g_ref[...]
    go_ref[...] = gg
    m_new = ADAM_B1 * m_ref[...] + (1.0 - ADAM_B1) * gg
    v_new = ADAM_B2 * v_ref[...] + (1.0 - ADAM_B2) * (gg * gg)
    m_hat = m_new / (1.0 - ADAM_B1 ** ADAM_STEP)
    v_hat = v_new / (1.0 - ADAM_B2 ** ADAM_STEP)
    d_ref[...] = -ADAM_LR * (m_hat / (jnp.sqrt(v_hat) + ADAM_EPS) + ADAM_WD * w_ref[...])
    mo_ref[...] = m_new
    vo_ref[...] = v_new


def _ln_emb_bwd(x, meta, g, dr, dhz, g_sq, sq, *, hook):
    n_sq = len(sq)
    sq_steps = 16
    sq_rows = SQ_ROWS // sq_steps

    def body(x_ref, meta_ref, g_ref, dr_ref, dhz_ref, *rest):
        sq_in = rest[:4 * n_sq]
        dx_ref, dmeta_ref, acc_ref = rest[4 * n_sq:4 * n_sq + 3]
        sq_out = rest[4 * n_sq + 3:]
        i = pl.program_id(0)

        @pl.when(i < sq_steps)
        def _():
            for kk in range(n_sq):
                _adamw_step(*sq_in[4 * kk:4 * kk + 4], *sq_out[4 * kk:4 * kk + 4])


        @pl.when(i == 0)
        def _():
            acc_ref[...] = jnp.zeros_like(acc_ref)

        x = _padded_rows(i, x_ref, meta_ref)
        mu = jnp.mean(x, axis=-1, keepdims=True)
        xc = x - mu
        var = jnp.mean(xc * xc, axis=-1, keepdims=True)
        rstd = lax.rsqrt(var + LN_EPS)
        xhat = xc * rstd
        dh = ALPHA * dr_ref[...] + dhz_ref[...]
        acc_ref[0:1, :] += jnp.sum(dh * xhat, axis=0, keepdims=True)
        acc_ref[1:2, :] += jnp.sum(dh, axis=0, keepdims=True)
        dxh = dh * g_ref[...]
        m1 = jnp.mean(dxh, axis=-1, keepdims=True)
        m2 = jnp.mean(dxh * xhat, axis=-1, keepdims=True)
        dx = rstd * (dxh - m1 - xhat * m2)
        dx_ref[...] = dx

        @pl.when(i == 0)
        def _():
            dmeta_ref[...] = dx[PAD:BLK]

    row = pl.BlockSpec((BLK, D), lambda i: (i, 0))
    vec = pl.BlockSpec((1, D), lambda i: (0, 0))
    xs, ms = _stream_specs()
    sq_blk = pl.BlockSpec((sq_rows, D), lambda i: (jnp.minimum(i, sq_steps - 1), 0))
    sq_specs, sq_operands = [], []
    for kk, (w_, m_, v_) in enumerate(sq):
        g_blk = pl.BlockSpec((sq_rows, D), lambda i, kk=kk: (sq_steps * kk + jnp.minimum(i, sq_steps - 1), 0))
        sq_specs += [sq_blk, g_blk, sq_blk, sq_blk]
        sq_operands += [w_, g_sq, m_, v_]
    return _pcall(body, name="ln_emb_bwd", grid=(NBLK,), in_specs=[xs, ms, vec, row, row] + sq_specs,
                  out_specs=[xs, ms, pl.BlockSpec((8, D), lambda i: (0, 0))] + [sq_blk] * (4 * n_sq),
                  out_shape=[_sds((SEQ, D), F32), _sds((N_META, D), F32), _sds((8, D), F32)]
                            + [_sds((SQ_ROWS, D), F32)] * (4 * n_sq),
                  operands=[x, meta, g, dr, dhz] + sq_operands, vmem=40, sem=("arbitrary",), hook=hook)


def _mul_silu_fwd(a, z, off, *, name):
    def body(a_ref, z_ref, o_ref, t_ref):
        zz = z_ref[...]
        y = a_ref[...] * (zz * _sigmoid(zz))
        o_ref[...] = y.astype(_MXU)
        t_ref[...] = y.T.astype(_MXU)

    strip = pl.BlockSpec((TP, CW), lambda j: (0, j))
    return _pcall(body, name=name, grid=(D // CW,),
                  in_specs=[strip, pl.BlockSpec((TP, CW), lambda j: (0, off // CW + j))],
                  out_specs=[strip, pl.BlockSpec((CW, TP), lambda j: (j, 0))],
                  out_shape=[_sds((TP, D), _MXU), _sds((D, TP), _MXU)], operands=[a, z], vmem=56, sem=("parallel",))


def _mul_silu_bwd(dy, a, z, off, dz, *, name, hook=None):
    def body(dy_ref, a_ref, z_ref, dz_in, da_ref, dg_ref):
        zz = z_ref[...]
        sg = _sigmoid(zz)
        d = dy_ref[...]
        da_ref[...] = d * (zz * sg)
        dg_ref[...] = (d * a_ref[...] * (sg * (1.0 + zz * (1.0 - sg)))).astype(_MXU)

    blk = pl.BlockSpec((RT, CW), lambda i, j: (i, j))
    zblk = pl.BlockSpec((RT, CW), lambda i, j: (i, off // CW + j))
    return _pcall(body, name=name, grid=(TP // RT, D // CW), in_specs=[blk, blk, zblk, _ANY], out_specs=[blk, zblk],
                  out_shape=[_sds((TP, D), F32), _sds((TP, D_IN), _MXU)], operands=[dy, a, z, dz], vmem=32,
                  sem=("parallel", "parallel"), aliases={3: 1}, hook=hook)


def _merge_fwd(y2, z):
    w = 256

    def body(ya_ref, yb_ref, ga_ref, gb_ref, o_ref, t_ref):
        y = _sigmoid(ga_ref[...]) * ya_ref[...] + _sigmoid(gb_ref[...]) * yb_ref[...]
        o_ref[...] = y.astype(_MXU)
        t_ref[...] = y.T.astype(_MXU)

    nb = D // w
    strip = pl.BlockSpec((TP, w), lambda j: (0, j))
    return _pcall(body, name="merge_fwd", grid=(nb,),
                  in_specs=[strip, pl.BlockSpec((TP, w), lambda j: (0, nb + j)),
                            pl.BlockSpec((TP, w), lambda j: (0, OFF_G // w + j)),
                            pl.BlockSpec((TP, w), lambda j: (0, (OFF_G + D) // w + j))],
                  out_specs=[strip, pl.BlockSpec((w, TP), lambda j: (j, 0))],
                  out_shape=[_sds((TP, D), _MXU), _sds((D, TP), _MXU)], operands=[y2, y2, z, z], vmem=56,
                  sem=("parallel",))


def _merge_bwd(dmix, y2, z):
    nb = D // CW

    def body(dm_ref, y_ref, g_ref, dy_ref, dg_ref):
        dm = dm_ref[...]
        sg = _sigmoid(g_ref[...])
        dy_ref[...] = (dm * sg).astype(_MXU)
        dg_ref[...] = (dm * y_ref[...] * sg * (1.0 - sg)).astype(_MXU)

    blk = pl.BlockSpec((RT, CW), lambda i, j: (i, j))
    gblk = pl.BlockSpec((RT, CW), lambda i, j: (i, OFF_G // CW + j))
    return _pcall(body, name="merge_bwd", grid=(TP // RT, 2 * nb),
                  in_specs=[pl.BlockSpec((RT, CW), lambda i, j: (i, j % nb)), blk, gblk], out_specs=[blk, gblk],
                  out_shape=[_sds((TP, 2 * D), _MXU), _sds((TP, D_IN), _MXU)], operands=[dmix, y2, z], vmem=32,
                  sem=("parallel", "parallel"))


def _ln_out_loss(h32, out, target, g, b):
    def body(h_ref, o_ref, t_ref, g_ref, b_ref, dr_ref, dr16_ref, acc_ref):
        i = pl.program_id(0)

        @pl.when(i == 0)
        def _():
            acc_ref[...] = jnp.zeros_like(acc_ref)

        r = ALPHA * h_ref[...] + o_ref[...]
        mu = jnp.mean(r, axis=-1, keepdims=True)
        rc = r - mu
        var = jnp.mean(rc * rc, axis=-1, keepdims=True)
        rstd = lax.rsqrt(var + LN_EPS)
        xhat = rc * rstd
        gg = g_ref[...]
        y = xhat * gg + b_ref[...]
        real = (i >= 1).astype(F32)
        diff = (y - t_ref[...]) * real
        dy = diff * (1.0 / D)
        dxh = dy * gg
        m1 = jnp.mean(dxh, axis=-1, keepdims=True)
        m2 = jnp.mean(dxh * xhat, axis=-1, keepdims=True)
        dr = rstd * (dxh - m1 - xhat * m2)
        dr_ref[...] = dr
        dr16_ref[...] = dr.astype(_MXU)
        acc_ref[0:1, :] += jnp.sum(dy * xhat, axis=0, keepdims=True)
        acc_ref[1:2, :] += jnp.sum(dy, axis=0, keepdims=True)
        acc_ref[2:3, :] += jnp.sum(dr, axis=0, keepdims=True)
        acc_ref[3:4, :] += (0.5 / D) * jnp.sum(diff * diff)

    row = pl.BlockSpec((BLK, D), lambda i: (i, 0))
    vec = pl.BlockSpec((1, D), lambda i: (0, 0))
    return _pcall(body, name="ln_out_loss", grid=(NBLK,),
                  in_specs=[row, row, pl.BlockSpec((BLK, D), lambda i: (jnp.maximum(i - 1, 0), 0)), vec, vec],
                  out_specs=[row, row, pl.BlockSpec((8, D), lambda i: (0, 0))],
                  out_shape=[_sds((TP, D), F32), _sds((TP, D), _MXU), _sds((8, D), F32)],
                  operands=[h32, out, target, g, b], vmem=32, sem=("arbitrary",))


def _rnn_recompute(xr_ref, cw_ref, cb_ref, wra_ref, wri_ref, bra_ref, bri_ref, lam_ref):
    rows = lax.broadcasted_iota(jnp.int32, (TP, 1), 0)
    valid = (rows >= PAD).astype(F32)
    first = rows == PAD
    x = xr_ref[...] * valid
    cw = cw_ref[...]
    shifted = [x, pltpu.roll(x, 1, 0), pltpu.roll(x, 2, 0), pltpu.roll(x, 3, 0)]
    c = cb_ref[...] + cw[0:1, :] * shifted[0] + cw[1:2, :] * shifted[1] + cw[2:3, :] * shifted[2] + cw[3:4, :] * shifted[3]
    cm = c.astype(_MXU)
    gr = _sigmoid(jnp.dot(cm, wra_ref[...].astype(_MXU), preferred_element_type=F32) + bra_ref[...])
    gi = _sigmoid(jnp.dot(cm, wri_ref[...].astype(_MXU), preferred_element_type=F32) + bri_ref[...])
    lam = lam_ref[...]
    ls = jnp.minimum(lam, 0.0) - jnp.log(1.0 + jnp.exp(-jnp.abs(lam)))
    log_a = LRU_C * gr * ls
    a = jnp.exp(log_a)
    mult = jnp.where(first, 1.0, jnp.sqrt(1.0 - jnp.exp(2.0 * log_a)))
    return dict(valid=valid, first=first, shifted=shifted, c=c, cm=cm, gr=gr, gi=gi, ls=ls, a=a, mult=mult, lam=lam)


def _rnn_specs():
    col = pl.BlockSpec((TP, RB), lambda n: (0, n))
    vec = pl.BlockSpec((1, RB), lambda n: (0, n))
    return dict(col=col, vec=vec, cw=pl.BlockSpec((CONV_W, RB), lambda n: (0, n)),
                wblk=pl.BlockSpec((None, RB, RB), lambda n: (n, 0, 0)))


def _rnn_gates_fwd(z, conv_w, conv_b, w_ra, w_ri, b_ra, b_ri, lam):
    def body(xr_ref, cw_ref, cb_ref, wra_ref, wri_ref, bra_ref, bri_ref, lam_ref, a_ref, u_ref):
        r = _rnn_recompute(xr_ref, cw_ref, cb_ref, wra_ref, wri_ref, bra_ref, bri_ref, lam_ref)
        a_ref[...] = r["a"]
        u_ref[...] = r["mult"] * r["gi"] * r["c"] * r["valid"]

    s = _rnn_specs()
    return _pcall(body, name="rnn_gates_fwd", grid=(N_RB,),
                  in_specs=[s["col"], s["cw"], s["vec"], s["wblk"], s["wblk"], s["vec"], s["vec"], s["vec"]],
                  out_specs=[s["col"], s["col"]], out_shape=[_sds((TP, D), F32)] * 2,
                  operands=[z, conv_w, conv_b, w_ra, w_ri, b_ra, b_ri, lam], vmem=56, sem=("parallel",))


SCAN_ROWS = 272


SUB = 8


def _tile_scan(a, u, reverse):
    rows = lax.broadcasted_iota(jnp.int32, a.shape, 0)
    for d in (1, 2, 4):
        shift = SUB - d if reverse else d
        inside = (rows < SUB - d) if reverse else (rows >= d)
        u = u + a * jnp.where(inside, pltpu.roll(u, shift, 0), 0.0)
        a = a * jnp.where(inside, pltpu.roll(a, shift, 0), 1.0)
    return a, u


def _scan_fwd(a, u, *, hook):
    def body(a_ref, u_ref, h_ref, carry_ref):
        @pl.when(pl.program_id(1) == 0)
        def _():
            carry_ref[...] = jnp.zeros_like(carry_ref)

        def step(r, h):
            rows = pl.ds(pl.multiple_of(r * SUB, SUB), SUB)
            prod, part = _tile_scan(a_ref[rows, :], u_ref[rows, :], False)
            ht = part + prod * h
            h_ref[rows, :] = ht
            return ht[SUB - 1:SUB, :]

        carry_ref[...] = lax.fori_loop(0, SCAN_ROWS // SUB, step, carry_ref[...], unroll=2)

    blk = pl.BlockSpec((SCAN_ROWS, CW), lambda j, i: (i, j))
    return _pcall(body, name="scan_fwd", grid=(D // CW, TP // SCAN_ROWS), in_specs=[blk, blk], out_specs=[blk],
                  out_shape=[_sds((TP, D), F32)], operands=[a, u], scratch=[pltpu.VMEM((1, CW), F32)], vmem=32,
                  sem=("parallel", "arbitrary"), hook=hook)


def _scan_bwd(a, dh):
    nst = TP // SCAN_ROWS
    n_tiles = SCAN_ROWS // SUB

    def body(a_ref, d_ref, o_ref, lam_ref, anext_ref):
        @pl.when(pl.program_id(1) == 0)
        def _():
            lam_ref[...] = jnp.zeros_like(lam_ref)
            anext_ref[...] = jnp.zeros_like(anext_ref)

        def step(q, carry):
            lam_next, a_next = carry
            rows = pl.ds(pl.multiple_of((n_tiles - 1 - q) * SUB, SUB), SUB)
            at = a_ref[rows, :]
            last = lax.broadcasted_iota(jnp.int32, at.shape, 0) == SUB - 1
            b = jnp.where(last, a_next, pltpu.roll(at, SUB - 1, 0))
            prod, part = _tile_scan(b, d_ref[rows, :], True)
            lam = part + prod * lam_next
            o_ref[rows, :] = lam
            return lam[0:1, :], at[0:1, :]

        lam, an = lax.fori_loop(0, n_tiles, step, (lam_ref[...], anext_ref[...]), unroll=2)
        lam_ref[...] = lam
        anext_ref[...] = an

    blk = pl.BlockSpec((SCAN_ROWS, CW), lambda j, i: (nst - 1 - i, j))
    return _pcall(body, name="scan_bwd", grid=(D // CW, nst), in_specs=[blk, blk], out_specs=[blk],
                  out_shape=[_sds((TP, D), F32)], operands=[a, dh],
                  scratch=[pltpu.VMEM((1, CW), F32), pltpu.VMEM((1, CW), F32)], vmem=32,
                  sem=("parallel", "arbitrary"))[0]


def _rnn_gates_bwd(z, lam_s, hr, conv_w, conv_b, w_ra, w_ri, b_ra, b_ri, lam, dz, gsq):
    def body(xr_ref, ls_ref, hr_ref, cw_ref, cb_ref, wra_ref, wri_ref, bra_ref, bri_ref, lam_ref, dz_in, gsq_in,
             dx_ref, dw_ref, sums_ref):
        r = _rnn_recompute(xr_ref, cw_ref, cb_ref, wra_ref, wri_ref, bra_ref, bri_ref, lam_ref)
        valid, c, gr, gi, a, mult = r["valid"], r["c"], r["gr"], r["gi"], r["a"], r["mult"]
        du = ls_ref[...] * valid
        da = du * pltpu.roll(hr_ref[...], 1, 0)
        d_gi = du * mult * c
        dc = du * mult * gi
        dmult = du * gi * c
        dlog_a = da * a + jnp.where(r["first"], 0.0, -dmult * a * a / mult)
        d_gr = dlog_a * (LRU_C * r["ls"])
        dls = jnp.sum(dlog_a * (LRU_C * gr), axis=0, keepdims=True)
        dpre_r = d_gr * gr * (1.0 - gr)
        dpre_i = d_gi * gi * (1.0 - gi)
        pr = dpre_r.astype(_MXU)
        pi = dpre_i.astype(_MXU)
        dwra = lax.dot_general(r["cm"], pr, _TN, preferred_element_type=F32)
        dwri = lax.dot_general(r["cm"], pi, _TN, preferred_element_type=F32)
        for s in range(N_SH):
            dw_ref[s, 0:64, :] = dwra[64 * s:64 * (s + 1)]
            dw_ref[s, 64:128, :] = dwri[64 * s:64 * (s + 1)]
        dc = dc + lax.dot_general(pr, wra_ref[...].astype(_MXU), _NT, preferred_element_type=F32)
        dc = dc + lax.dot_general(pi, wri_ref[...].astype(_MXU), _NT, preferred_element_type=F32)
        cw = cw_ref[...]
        dx = cw[0:1, :] * dc
        for k in range(1, CONV_W):
            dx = dx + cw[k:k + 1, :] * pltpu.roll(dc, TP - k, 0)
        dx_ref[...] = (dx * valid).astype(_MXU)
        for k in range(CONV_W):
            sums_ref[k:k + 1, :] = jnp.sum(dc * r["shifted"][k], axis=0, keepdims=True)
        sums_ref[4:5, :] = jnp.sum(dc, axis=0, keepdims=True)
        sums_ref[5:6, :] = jnp.sum(dpre_r, axis=0, keepdims=True)
        sums_ref[6:7, :] = jnp.sum(dpre_i, axis=0, keepdims=True)
        sums_ref[7:8, :] = dls * _sigmoid(-r["lam"])

    s = _rnn_specs()
    return _pcall(
        body, name="rnn_gates_bwd", grid=(N_RB,),
        in_specs=[s["col"], s["col"], s["col"], s["cw"], s["vec"], s["wblk"], s["wblk"], s["vec"], s["vec"], s["vec"],
                  _ANY, _ANY],
        out_specs=[s["col"], pl.BlockSpec((N_SH, 128, RB), lambda n: (0, 3 * SQ_ROWS // 128, n)),
                   pl.BlockSpec((8, RB), lambda n: (0, n))],
        out_shape=[_sds((TP, D_IN), _MXU), _sds((N_SH, PACK_ROWS, D), F32), _sds((8, D), F32)],
        operands=[z, lam_s, hr, conv_w, conv_b, w_ra, w_ri, b_ra, b_ri, lam, dz, gsq], vmem=60, sem=("parallel",),
        aliases={10: 0, 11: 1})


def _rope_tables():
    half = HD // 2
    inv = ROPE_THETA ** (-jnp.arange(half, dtype=F32) / half)
    pos = (jnp.arange(TP) - PAD).astype(F32)
    ang = pos[:, None] * inv[None, :]
    return jnp.tile(jnp.cos(ang), (1, 4)), jnp.tile(jnp.sin(ang), (1, 4))


def _rope(x, cos_t, sin_t, sign):
    w = x.shape[1]
    lane = lax.broadcasted_iota(jnp.int32, x.shape, 1)
    first = (lane % HD) < (HD // 2)
    swapped = jnp.where(first, pltpu.roll(x, w - HD // 2, 1), pltpu.roll(x, HD // 2, 1))
    ct = jnp.tile(cos_t, (1, w // 128))
    st = jnp.tile(sin_t, (1, w // 128))
    return x * ct + swapped * jnp.where(first, -sign * st, sign * st)


def _rope_fwd(z, cos_t, sin_t):
    def body(q_ref, k_ref, v_ref, c_ref, s_ref, qo_ref, ko_ref, vo_ref):
        c = c_ref[...]
        s = s_ref[...]
        qo_ref[...] = _rope(q_ref[...], c, s, 1.0).astype(_MXU)
        ko_ref[...] = _rope(k_ref[...], c, s, 1.0).astype(_MXU)
        vo_ref[...] = v_ref[...].astype(_MXU)

    tab = pl.BlockSpec((BLK, 128), lambda i: (i, 0))
    kv = pl.BlockSpec((BLK, D_KV), lambda i: (i, 0))
    return _pcall(body, name="rope_fwd", grid=(NBLK,),
                  in_specs=[pl.BlockSpec((BLK, D), lambda i: (i, OFF_Q // D)),
                            pl.BlockSpec((BLK, D_KV), lambda i: (i, OFF_K // D_KV)),
                            pl.BlockSpec((BLK, D_KV), lambda i: (i, OFF_V // D_KV)), tab, tab],
                  out_specs=[pl.BlockSpec((BLK, D), lambda i: (i, 0)), kv, kv],
                  out_shape=[_sds((TP, D), _MXU), _sds((TP, D_KV), _MXU), _sds((TP, D_KV), _MXU)],
                  operands=[z, z, z, cos_t, sin_t], vmem=32, sem=("parallel",))


def _rope_bwd_k(dk, dv, cos_t, sin_t, dz):
    def body(dk_ref, dv_ref, c_ref, s_ref, dz_in, o_ref):
        o_ref[:, 0:D_KV] = _rope(dk_ref[...], c_ref[...], s_ref[...], -1.0).astype(_MXU)
        o_ref[:, D_KV:2 * D_KV] = dv_ref[...].astype(_MXU)

    tab = pl.BlockSpec((BLK, 128), lambda i: (i, 0))
    kv = pl.BlockSpec((BLK, D_KV), lambda i: (i, 0))
    return _pcall(body, name="rope_bwd_k", grid=(NBLK,), in_specs=[kv, kv, tab, tab, _ANY],
                  out_specs=[pl.BlockSpec((BLK, 2 * D_KV), lambda i: (i, OFF_K // (2 * D_KV)))],
                  out_shape=[_sds((TP, D_IN), _MXU)], operands=[dk, dv, cos_t, sin_t, dz], vmem=32, sem=("parallel",),
                  aliases={4: 0})[0]


def _attn_mask(i):
    ql = lax.broadcasted_iota(jnp.int32, (BLK, 3 * BLK), 0)
    kk = lax.broadcasted_iota(jnp.int32, (BLK, 3 * BLK), 1)
    kl = kk % BLK
    part = kk // BLK
    meta = (part == 0) & (kl >= PAD) & ((i >= 1) | (kl <= ql))
    prev = (part == 1) & (i >= 2) & (kl > ql)
    cur = (part == 2) & (i >= 1) & (kl <= ql)
    return meta | prev | cur


def _kv_specs():
    return [pl.BlockSpec((BLK, D_KV), lambda i: (0, 0)),
            pl.BlockSpec((BLK, D_KV), lambda i: (jnp.maximum(i - 1, 0), 0)),
            pl.BlockSpec((BLK, D_KV), lambda i: (i, 0))]


def _pair_heads(ref, gp):
    def two(j):
        h0 = 2 * GRP * gp + j
        return jnp.concatenate([ref[:, HD * h0:HD * (h0 + 1)], ref[:, HD * (h0 + GRP):HD * (h0 + GRP + 1)]], axis=1)

    return jnp.concatenate([two(j) for j in range(GRP)], axis=0)


def _pair_kv(refs, gp):
    both = jnp.concatenate([r[:, 2 * HD * gp:2 * HD * (gp + 1)] for r in refs], axis=0)
    low_lanes = lax.broadcasted_iota(jnp.int32, both.shape, 1) < HD
    zero = jnp.zeros_like(both)
    return jnp.concatenate([jnp.where(low_lanes, both, zero), jnp.where(low_lanes, zero, both)], axis=0)


def _attn_fwd(q, k, v, sinks, z, *, hook):
    nk = 3 * BLK
    n_gate = D // CW

    def body(q_ref, k0_ref, kp_ref, kc_ref, v0_ref, vp_ref, vc_ref, sink_ref, *rest):
        gate_refs, (o_ref, lse_ref, y_ref, yt_ref, s_scr, p_scr) = rest[:n_gate], rest[n_gate:]
        mask = _attn_mask(pl.program_id(0))
        for gp in range(N_KV // 2):
            s_scr[...] = lax.dot_general(_pair_heads(q_ref, gp), _pair_kv((k0_ref, kp_ref, kc_ref), gp), _NT,
                                         preferred_element_type=F32)
            for a in range(2):
                for j in range(GRP):
                    h = GRP * (2 * gp + a) + j
                    rows, cols = slice(BLK * j, BLK * (j + 1)), slice(nk * a, nk * (a + 1))
                    sink = sink_ref[h]
                    s = jnp.where(mask, s_scr[rows, cols] * (HD ** -0.5), NEG_INF)
                    mx = jnp.maximum(jnp.max(s, -1, keepdims=True), sink)
                    p = jnp.exp(s - mx)
                    den = jnp.sum(p, -1, keepdims=True) + jnp.exp(sink - mx)
                    p_scr[rows, cols] = (p * (1.0 / den)).astype(_MXU)
                    lse_ref[:, h:h + 1] = mx + jnp.log(den)
            o2 = jnp.dot(p_scr[...], _pair_kv((v0_ref, vp_ref, vc_ref), gp), preferred_element_type=F32)
            for a in range(2):
                for j in range(GRP):
                    h = GRP * (2 * gp + a) + j
                    o_ref[:, HD * h:HD * (h + 1)] = o2[BLK * j:BLK * (j + 1), HD * a:HD * (a + 1)]
        for t, g_ref in enumerate(gate_refs):
            cs = slice(CW * t, CW * (t + 1))
            zz = g_ref[...]
            y = o_ref[:, cs] * (zz * _sigmoid(zz))
            y_ref[:, cs] = y.astype(_MXU)
            yt_ref[cs, :] = y.T.astype(_MXU)

    row = pl.BlockSpec((BLK, D), lambda i: (i, 0))
    gates = [pl.BlockSpec((BLK, CW), lambda i, t=t: (i, OFF_GA // CW + t)) for t in range(n_gate)]
    return _pcall(body, name="attn_fwd", grid=(NBLK,),
                  in_specs=[row] + _kv_specs() + _kv_specs() + [pl.BlockSpec(memory_space=pltpu.SMEM)] + gates,
                  out_specs=[row, pl.BlockSpec((BLK, N_Q), lambda i: (i, 0)), row, pl.BlockSpec((D, BLK), lambda i: (0, i))],
                  out_shape=[_sds((TP, D), F32), _sds((TP, N_Q), F32), _sds((TP, D), _MXU), _sds((D, TP), _MXU)],
                  operands=[q, k, k, k, v, v, v, sinks] + [z] * n_gate,
                  scratch=[pltpu.VMEM((GRP * BLK, 2 * nk), F32), pltpu.VMEM((GRP * BLK, 2 * nk), _MXU)],
                  vmem=40, sem=("parallel",), hook=hook)


def _attn_bwd(q, k, v, sinks, do, o, lse, cos_t, sin_t, dz, *, hook):
    def body(q_ref, k0_ref, kp_ref, kc_ref, v0_ref, vp_ref, vc_ref, sink_ref, do_ref, o_ref, lse_ref, c_ref, s_ref, dz_in,
             dq_ref, dk_ref, dv_ref, dsink_ref, dqrot_ref, s_scr, dp_scr, p_scr, ds_scr):
        i = pl.program_id(0)

        @pl.when(i == 0)
        def _():
            dk_ref[...] = jnp.zeros_like(dk_ref)
            dv_ref[...] = jnp.zeros_like(dv_ref)
            dsink_ref[...] = jnp.zeros_like(dsink_ref)

        mask = _attn_mask(i)
        row_starts = (0, pl.multiple_of(jnp.maximum(i - 1, 0) * BLK, BLK), pl.multiple_of(i * BLK, BLK))
        scale = HD ** -0.5
        nk = 3 * BLK
        for gp in range(N_KV // 2):
            q2 = _pair_heads(q_ref, gp)
            dom2 = _pair_heads(do_ref, gp).astype(_MXU)
            k2 = _pair_kv((k0_ref, kp_ref, kc_ref), gp)
            s_scr[...] = lax.dot_general(q2, k2, _NT, preferred_element_type=F32)
            dp_scr[...] = lax.dot_general(dom2, _pair_kv((v0_ref, vp_ref, vc_ref), gp), _NT, preferred_element_type=F32)
            for a in range(2):
                for j in range(GRP):
                    h = GRP * (2 * gp + a) + j
                    hs = slice(HD * h, HD * (h + 1))
                    rows = slice(BLK * j, BLK * (j + 1))
                    cols = slice(nk * a, nk * (a + 1))
                    lse_h = lse_ref[:, h:h + 1]
                    delta = jnp.sum(do_ref[:, hs] * o_ref[:, hs], axis=-1, keepdims=True)
                    dsink_ref[0:1, h:h + 1] += jnp.sum(-jnp.exp(sink_ref[h] - lse_h) * delta, axis=0, keepdims=True)
                    p = jnp.exp(jnp.where(mask, s_scr[rows, cols] * scale, NEG_INF) - lse_h)
                    p_scr[rows, cols] = p.astype(_MXU)
                    ds_scr[rows, cols] = (p * (dp_scr[rows, cols] - delta) * scale).astype(_MXU)
            ds = ds_scr[...]
            dq2 = jnp.dot(ds, k2, preferred_element_type=F32)
            dk2 = lax.dot_general(ds, q2, _TN, preferred_element_type=F32)
            dv2 = lax.dot_general(p_scr[...], dom2, _TN, preferred_element_type=F32)
            for a in range(2):
                g = 2 * gp + a
                gs = slice(HD * g, HD * (g + 1))
                for part in range(3):
                    rows = pl.ds(row_starts[part], BLK)
                    blk = slice(nk * a + BLK * part, nk * a + BLK * (part + 1))
                    dk_ref[rows, gs] += dk2[blk, HD * a:HD * (a + 1)]
                    dv_ref[rows, gs] += dv2[blk, HD * a:HD * (a + 1)]
                for j in range(GRP):
                    h = GRP * g + j
                    dqrot_ref[:, HD * h:HD * (h + 1)] = dq2[BLK * j:BLK * (j + 1), HD * a:HD * (a + 1)]
        dq_ref[...] = _rope(dqrot_ref[...], c_ref[...], s_ref[...], -1.0).astype(_MXU)

    row = pl.BlockSpec((BLK, D), lambda i: (i, 0))
    tab = pl.BlockSpec((BLK, 128), lambda i: (i, 0))
    full_kv = pl.BlockSpec((TP, D_KV), lambda i: (0, 0))
    return _pcall(
        body, name="attn_bwd", grid=(NBLK,),
        in_specs=[row] + _kv_specs() + _kv_specs() + [pl.BlockSpec(memory_space=pltpu.SMEM), row, row,
                  pl.BlockSpec((BLK, N_Q), lambda i: (i, 0)), tab, tab, _ANY],
        out_specs=[pl.BlockSpec((BLK, D), lambda i: (i, OFF_Q // D)), full_kv, full_kv,
                   pl.BlockSpec((8, 128), lambda i: (0, 0))],
        out_shape=[_sds((TP, D_IN), _MXU), _sds((TP, D_KV), F32), _sds((TP, D_KV), F32), _sds((8, 128), F32)],
        operands=[q, k, k, k, v, v, v, sinks, do, o, lse, cos_t, sin_t, dz],
        scratch=[pltpu.VMEM((BLK, D), F32), pltpu.VMEM((GRP * BLK, 6 * BLK), F32), pltpu.VMEM((GRP * BLK, 6 * BLK), F32),
                 pltpu.VMEM((GRP * BLK, 6 * BLK), _MXU), pltpu.VMEM((GRP * BLK, 6 * BLK), _MXU)],
        vmem=48, sem=("arbitrary",), aliases={13: 0}, hook=hook)


def _cast_into_slot(w32, pos, *, name, tr):
    r, cc = w32.shape

    def body(pos_ref, w_ref, o_ref):
        o_ref[...] = w_ref[...].astype(BF16)

    return _pcall(body, name=name, grid=(r // tr,), in_specs=[pl.BlockSpec((tr, cc), lambda i, p: (i, 0))],
                  out_specs=[pl.BlockSpec((None, tr, cc), lambda i, p: (p[1], i, 0))],
                  out_shape=[_sds((N_SH, r, cc), BF16)], operands=[w32], vmem=32, sem=("parallel",), prefetch=(pos,))[0]


def _pair_add(g, got, pos, *, name, tr, g_has_both_halves, hook=None):
    n, h, cc = got.shape
    nt = h // tr

    def body(pos_ref, g_ref, r_ref, own_ref, s16_ref):
        s = g_ref[...] + r_ref[...].astype(F32)
        s16_ref[...] = s.astype(BF16)

        @pl.when(pl.program_id(1) == pos_ref[1])
        def _():
            own_ref[...] = s

    g_index = (lambda i, s, p: (s, p[0] * nt + i, 0)) if g_has_both_halves else (lambda i, s, p: (s, i, 0))
    return _pcall(body, name=name, grid=(nt, n),
                  in_specs=[pl.BlockSpec((None, tr, cc), g_index), pl.BlockSpec((None, tr, cc), lambda i, s, p: (s, i, 0))],
                  out_specs=[pl.BlockSpec((tr, cc), lambda i, s, p: (i, 0)),
                             pl.BlockSpec((None, tr, cc), lambda i, s, p: (s, i, 0))],
                  out_shape=[_sds((h, cc), F32), _sds((n, h, cc), BF16)], operands=[g, got], vmem=40,
                  sem=("parallel", "arbitrary"), prefetch=(pos,), hook=hook)


def _relay_add(s16, raw, pos, *, name, tr):
    _, q, cc = raw.shape
    nt = q // tr

    def body(pos_ref, s_ref, r_ref, o_ref):
        o_ref[...] = (s_ref[...].astype(F32) + r_ref[...].astype(F32)).astype(BF16)

    blk = pl.BlockSpec((None, tr, cc), lambda k, i, p: (k, i, 0))
    return _pcall(body, name=name, grid=(2, nt),
                  in_specs=[pl.BlockSpec((None, tr, cc), lambda k, i, p: (p[3 + k], (1 - k) * nt + i, 0)), blk],
                  out_specs=[blk], out_shape=[_sds((2, q, cc), BF16)], operands=[s16, raw], vmem=40,
                  sem=("parallel", "parallel"), prefetch=(pos,))[0]


def _sum_chips(own, got, pos, *, name, tr):
    h, cc = own.shape
    n_got = got.shape[0]
    nt = h // tr

    def body(pos_ref, o_ref, r_ref, out_ref):
        acc = o_ref[...]
        for k in range(n_got):
            acc = acc + r_ref[k].astype(F32)
        out_ref[...] = acc

    return _pcall(body, name=name, grid=(nt,),
                  in_specs=[pl.BlockSpec((tr, cc), lambda i, p: (i, 0)),
                            pl.BlockSpec((n_got, tr, cc), lambda i, p: (0, i, 0))],
                  out_specs=[pl.BlockSpec((tr, cc), lambda i, p: (p[0] * nt + i, 0))],
                  out_shape=[_sds((2 * h, cc), F32)], operands=[own, got], vmem=40, sem=("parallel",), prefetch=(pos,))[0]


def _adamw_rows(w, g, m, v, row_counts):
    def body(w_ref, g_ref, m_ref, v_ref, *outs):
        gg = g_ref[...]
        m_new = ADAM_B1 * m_ref[...] + (1.0 - ADAM_B1) * gg
        v_new = ADAM_B2 * v_ref[...] + (1.0 - ADAM_B2) * (gg * gg)
        m_hat = m_new / (1.0 - ADAM_B1 ** ADAM_STEP)
        v_hat = v_new / (1.0 - ADAM_B2 ** ADAM_STEP)
        delta = -ADAM_LR * (m_hat / (jnp.sqrt(v_hat) + ADAM_EPS) + ADAM_WD * w_ref[...])
        t = 0
        for val in (gg, delta, m_new, v_new):
            r0 = 0
            for nr in row_counts:
                outs[t][...] = val[r0:r0 + nr]
                r0 += nr
                t += 1

    return pl.pallas_call(body, name="adamw_small",
                          out_shape=[_sds((nr, w.shape[1]), F32) for _ in range(4) for nr in row_counts])(w, g, m, v)


def _hand_over(arrays):
    def body(*refs):
        del refs

    n = len(arrays)
    return pl.pallas_call(body, name="hand_over", in_specs=[_ANY] * n, out_specs=[_ANY] * n,
                          out_shape=[_sds(a.shape, a.dtype) for a in arrays],
                          input_output_aliases={t: t for t in range(n)})(*arrays)


def _adamw(w, g, m, v, *, name, tr, g_row0=0):
    r, cc = w.shape
    g_blk0 = g_row0 // tr

    def body(*refs):
        _adamw_step(*refs)

    blk = pl.BlockSpec((tr, cc), lambda i: (i, 0))
    gblk = pl.BlockSpec((tr, cc), lambda i: (g_blk0 + i, 0))
    return _pcall(body, name=name, grid=(r // tr,), in_specs=[blk, gblk, blk, blk], out_specs=[blk] * 4,
                  out_shape=[_sds((r, cc), F32)] * 4, operands=[w, g, m, v], vmem=48, sem=("parallel",))


_SMALL_ROWS = 40
_B_IN_ROWS = 7


def _row_pad(v, rows):
    flat = v.reshape(-1)
    return jnp.pad(flat, (0, rows * D - flat.shape[0])).reshape(rows, D)


def _pack_ra(w):
    return w.transpose(1, 0, 2).reshape(64, D)


def _unpack_ra(p, like):
    return p.reshape(64, N_RB, RB).transpose(1, 0, 2).reshape(like.shape)


def _gate_full(g4):
    return g4.reshape(N_SH, 64, N_RB, RB).transpose(2, 0, 1, 3).reshape(N_RB, RB, RB)


def kernel(x, meta_tokens, ln_emb_g, ln_emb_b, w_in, b_in, conv_w, conv_b, w_ra, b_ra, w_ri, b_ri, lru_lambda, sinks, w_rnn_out, w_attn_out, w_o, b_o, ln_g, ln_b, loss_target, m_meta_tokens, m_ln_emb_g, m_ln_emb_b, m_w_in, m_b_in, m_conv_w, m_conv_b, m_w_ra, m_b_ra, m_w_ri, m_b_ri, m_lru_lambda, m_sinks, m_w_rnn_out, m_w_attn_out, m_w_o, m_b_o, m_ln_g, m_ln_b, v_meta_tokens, v_ln_emb_g, v_ln_emb_b, v_w_in, v_b_in, v_conv_w, v_conv_b, v_w_ra, v_b_ra, v_w_ri, v_b_ri, v_lru_lambda, v_sinks, v_w_rnn_out, v_w_attn_out, v_w_o, v_b_o, v_ln_g, v_ln_b):
    xi, yi, ci = _my_pos()
    shard = 2 * xi + yi
    pos = jnp.stack([ci, shard, 1 - ci, shard ^ 2, shard ^ 1]).astype(jnp.int32)
    cos_t, sin_t = _rope_tables()
    zero_bias = jnp.zeros((1, D), F32)
    ln_emb_g2, ln_emb_b2 = ln_emb_g[None], ln_emb_b[None]

    small = jnp.concatenate([conv_w[0], meta_tokens, jnp.zeros((4, 512), F32)], axis=0)
    small4 = _gather_small(small)
    conv_w_full = small4[:, 0:4].transpose(1, 0, 2).reshape(CONV_W, D)
    meta_full = small4[:, 4:20].transpose(1, 0, 2).reshape(N_META, D)
    w_own = _cast_into_slot(w_in[0], pos, name="cast_w_in", tr=256)
    wa_own = _cast_into_slot(jnp.concatenate([w_attn_out[0], w_o[0], _pack_ra(w_ra[0]), _pack_ra(w_ri[0])], axis=0),
                             pos, name="cast_w_a", tr=288)
    wb_own = _cast_into_slot(w_rnn_out[0], pos, name="cast_w_b", tr=256)

    h32, h16, h16_t = _ln_emb_fwd(x[0], meta_full, ln_emb_g2, ln_emb_b2)
    order = jnp.stack([shard, shard ^ 2, shard ^ 1, shard ^ 3]).astype(jnp.int32)
    z, w_in4 = _mm_z_gather(h16, w_own, b_in, order)
    q, k, v = _rope_fwd(z, cos_t, sin_t)
    o, lse, yb_in, yb_in_t, wa4 = _attn_fwd(q, k, v, sinks[0], z, hook=_hook_gather(wa_own, 0.6))
    w_ra_full = _gate_full(wa4[:, 2 * SQ_ROWS:2 * SQ_ROWS + 64])
    w_ri_full = _gate_full(wa4[:, 2 * SQ_ROWS + 64:2 * SQ_ROWS + 128])
    a_dec, u_in = _rnn_gates_fwd(z, conv_w_full, conv_b, w_ra_full, w_ri_full, b_ra, b_ri, lru_lambda)
    hr, wb4 = _scan_fwd(a_dec, u_in, hook=_hook_gather(wb_own, 0.6))
    sq_w = {0: (wb4, 0), 1: (wa4, 0), 2: (wa4, 1)}

    def sq_nn(a, kk, bias, name, out_cols, out_index, carried=None):
        wp, blk = sq_w[kk]
        return _mm_nn(a, wp, bias, name=name, grid=(2, D // CW), tm=HALF_TP, tn=CW, k=D, a_index=lambda i, j: (i, 0),
                      w_block=(N_SH, SQ_ROWS, CW), w_index=lambda i, j: (0, blk, j), out_cols=out_cols,
                      out_index=out_index, carried=carried)[0]

    def sq_nt(a, a_blk, kk, name, hook=None):
        wp, blk = sq_w[kk]
        return _mm_nt(a, wp, name=name, grid=(2, N_SH, 1), tm=HALF_TP, tn=SQ_ROWS, tk=D,
                      a_index=lambda i, j, q: (i, a_blk), w_block=(None, SQ_ROWS, D),
                      w_index=lambda i, j, q: (j, blk, 0), out_cols=D, hook=hook)

    ya_in, ya_in_t = _mul_silu_fwd(hr, z, OFF_GR, name="gate_a_fwd")
    y2 = sq_nn(ya_in, 0, zero_bias, "mm_ya", 2 * D, lambda i, j: (i, j))
    y2 = sq_nn(yb_in, 1, zero_bias, "mm_yb", 2 * D, lambda i, j: (i, D // CW + j), carried=y2)
    mixed, mixed_t = _merge_fwd(y2, z)
    out = sq_nn(mixed, 2, b_o, "mm_out", D, lambda i, j: (i, j))
    dr, dr16, sums_o = _ln_out_loss(h32, out, loss_target[0], ln_g, ln_b)

    def sq_tn(at, b, b_blk, kk, name, carried=None):
        return _mm_tn(at, b, name=name, grid=(N_SH, 1), tm=SQ_ROWS, tn=D, a_index=lambda i, j: (i, 0),
                      b_index=lambda i, j: (0, b_blk), out_shape=(N_SH, PACK_ROWS, D),
                      out_block=(None, SQ_ROWS, D), out_index=lambda i, j: (i, kk, 0), carried=carried)[0]

    gsq = sq_tn(mixed_t, dr16, 0, 2, "mm_dwo")
    dmix = sq_nt(dr16, 0, 2, "mm_dmix")[0]
    dy2, dz = _merge_bwd(dmix, y2, z)
    gsq = sq_tn(ya_in_t, dy2, 0, 0, "mm_dwrnn", carried=gsq)
    gsq = sq_tn(yb_in_t, dy2, 1, 1, "mm_dwattn", carried=gsq)
    dya_in = sq_nt(dy2, 0, 0, "mm_dyain")[0]
    dhr, dz = _mul_silu_bwd(dya_in, hr, z, OFF_GR, dz, name="gate_a_bwd")
    lam_s = _scan_bwd(a_dec, dhr)
    dz, gsq, sums_r = _rnn_gates_bwd(z, lam_s, hr, conv_w_full, conv_b, w_ra_full, w_ri_full, b_ra, b_ri, lru_lambda, dz, gsq)
    dyb_in, gsq, got_sq = sq_nt(dy2, 1, 1, "mm_dybin", hook=_hook_pair(gsq, True, (0, 1)))
    do, dz, gsq, got_sq = _mul_silu_bwd(dyb_in, o, z, OFF_GA, dz, name="gate_b_bwd",
                                        hook=_hook_pair(gsq, True, (2, 3), land=got_sq))
    own_sq, s16_sq = _pair_add(gsq, got_sq, pos, name="red_w_sq_add", tr=208, g_has_both_halves=True)
    dz, dk_rot, dv32, dsink, s16_sq, fin_sq, raw_sq = _attn_bwd(q, k, v, sinks[0], do, o, lse, cos_t, sin_t, dz,
                                                                hook=_hook_scatter_direct(s16_sq))
    dz = _rope_bwd_k(dk_rot, dv32, cos_t, sin_t, dz)
    comb_sq = _relay_add(s16_sq, raw_sq, pos, name="red_w_sq_relay", tr=208)

    def dwin(half_idx, name, hook, **kw):
        return _mm_tn(h16_t, dz, name=name, grid=(1, N_SH * PER_IN), tm=D // 2, tn=TN_IN,
                      a_index=lambda i, j, p: (p[half_idx], 0), b_index=lambda i, j, p: (0, j),
                      out_shape=(N_SH, D // 2, W_IN_COLS), out_block=(None, D // 2, TN_IN),
                      out_index=lambda i, j, p: (j // PER_IN, 0, j % PER_IN), prefetch=(pos,), hook=hook, **kw)

    gin_sib, g_b_in, comb_sq, fin_sq = dwin(2, "mm_dwin_sib", _hook_scatter_relay(comb_sq, fin_sq),
                                            out_dtype=BF16, colsum=True)
    red_sq = _sum_chips(own_sq, fin_sq, pos, name="red_w_sq_sum", tr=208)
    gin_own, gin_sib, got_in = dwin(0, "mm_dwin_own", _hook_pair(gin_sib, False))
    own_in, s16_in, g_sq = _pair_add(gin_own, got_in, pos, name="red_w_in_add", tr=128, g_has_both_halves=False,
                                     hook=_hook_halves(red_sq))
    dhz, s16_in, fin_in, raw_in = _mm_nt(dz, w_in4, name="mm_dhz", grid=(2, 2, N_SH), tm=HALF_TP, tn=D // 2,
                                         tk=W_IN_COLS, a_index=lambda i, j, q: (i, q),
                                         w_block=(None, D // 2, W_IN_COLS), w_index=lambda i, j, q: (q, j, 0), out_cols=D,
                                         hook=_hook_scatter_direct(s16_in))
    comb_in = _relay_add(s16_in, raw_in, pos, name="red_w_in_relay", tr=128)
    sq_params = [("w_rnn_out", w_rnn_out, m_w_rnn_out, v_w_rnn_out), ("w_attn_out", w_attn_out, m_w_attn_out, v_w_attn_out),
                 ("w_o", w_o, m_w_o, v_w_o)]
    g_x, g_meta_local, sums_e, *sq_res, comb_in, fin_in = _ln_emb_bwd(
        x[0], meta_full, ln_emb_g2, dr, dhz, g_sq, [(w_[0], m_[0], v_[0]) for _, w_, m_, v_ in sq_params],
        hook=_hook_scatter_relay(comb_in, fin_in))
    red_in = _sum_chips(own_in, fin_in, pos, name="red_w_in_sum", tr=128)

    spack = jnp.concatenate([
        sums_e[0:1], sums_e[1:2], _row_pad(g_b_in, _B_IN_ROWS), sums_r[0:4], sums_r[4:5], sums_r[5:6], sums_r[6:7],
        sums_r[7:8], _row_pad(dsink[0:1, 0:N_Q], 1), sums_o[2:3], sums_o[0:1], sums_o[1:2], g_meta_local, sums_o[3:4],
        jnp.zeros((_SMALL_ROWS - 38, D), F32)], axis=0)
    sred, g_in = _allreduce_small(spack, red_in)

    big = {"w_in": [t.reshape(w_in.shape) for t in
                    _adamw(w_in[0], g_in, m_w_in[0], v_w_in[0], name="adamw_w_in", tr=128)]}
    g_x, *sq_res = _hand_over([g_x] + sq_res)
    for kk, (n, w_, _, _) in enumerate(sq_params):
        big[n] = [t.reshape(w_.shape) for t in sq_res[4 * kk:4 * kk + 4]]
    for kk, (n, w_, m_, v_) in enumerate([("w_ra", w_ra, m_w_ra, v_w_ra), ("w_ri", w_ri, m_w_ri, v_w_ri)]):
        big[n] = [_unpack_ra(t, w_) for t in
                  _adamw(_pack_ra(w_[0]), g_sq, _pack_ra(m_[0]), _pack_ra(v_[0]), name="adamw_" + n, tr=64,
                         g_row0=3 * SQ_ROWS + 64 * kk)]

    loss = sred[37, 0]
    col0 = shard * 512
    g_conv_w = lax.dynamic_slice(sred[9:13], (0, col0), (CONV_W, 512))
    g_meta = lax.dynamic_slice(sred[21:37], (0, col0), (N_META, 512))
    small_g = {"ln_emb_g": sred[0:1], "ln_emb_b": sred[1:2], "b_in": sred[2:9], "conv_w": g_conv_w.reshape(1, D),
               "conv_b": sred[13:14], "b_ra": sred[14:15], "b_ri": sred[15:16], "lru_lambda": sred[16:17],
               "sinks": sred[17:18], "b_o": sred[18:19], "ln_g": sred[19:20], "ln_b": sred[20:21],
               "meta_tokens": g_meta.reshape(4, D)}
    small_names = list(small_g)

    def small_pack(vals):
        rows = []
        for n in small_names:
            a = vals[n]
            if n == "b_in":
                a = _row_pad(a, _B_IN_ROWS)
            elif n == "sinks":
                a = _row_pad(a, 1)
            else:
                a = a.reshape(-1, D)
            rows.append(a)
        return jnp.concatenate(rows + [jnp.zeros((24 - 22, D), F32)], axis=0)

    w_small = dict(ln_emb_g=ln_emb_g, ln_emb_b=ln_emb_b, b_in=b_in, conv_w=conv_w, conv_b=conv_b, b_ra=b_ra, b_ri=b_ri,
                   lru_lambda=lru_lambda, sinks=sinks, b_o=b_o, ln_g=ln_g, ln_b=ln_b, meta_tokens=meta_tokens)
    m_small = dict(ln_emb_g=m_ln_emb_g, ln_emb_b=m_ln_emb_b, b_in=m_b_in, conv_w=m_conv_w, conv_b=m_conv_b, b_ra=m_b_ra,
                   b_ri=m_b_ri, lru_lambda=m_lru_lambda, sinks=m_sinks, b_o=m_b_o, ln_g=m_ln_g, ln_b=m_ln_b,
                   meta_tokens=m_meta_tokens)
    v_small = dict(ln_emb_g=v_ln_emb_g, ln_emb_b=v_ln_emb_b, b_in=v_b_in, conv_w=v_conv_w, conv_b=v_conv_b, b_ra=v_b_ra,
                   b_ri=v_b_ri, lru_lambda=v_lru_lambda, sinks=v_sinks, b_o=v_b_o, ln_g=v_ln_g, ln_b=v_ln_b,
                   meta_tokens=v_meta_tokens)
    g_small_pack = jnp.concatenate([small_g[n] for n in small_names] + [jnp.zeros((2, D), F32)], axis=0)
    row_counts = [small_g[n].shape[0] for n in small_names]
    small_flat = _adamw_rows(small_pack(w_small), g_small_pack, small_pack(m_small), small_pack(v_small), row_counts)
    small_res = [dict(zip(small_names, small_flat[len(small_names) * t:len(small_names) * (t + 1)])) for t in range(4)]

    def small_out(per_name, n, like):
        return per_name[n].reshape(-1)[:like.size].reshape(like.shape)

    weights = dict(meta_tokens=meta_tokens, ln_emb_g=ln_emb_g, ln_emb_b=ln_emb_b, w_in=w_in, b_in=b_in, conv_w=conv_w,
                   conv_b=conv_b, w_ra=w_ra, b_ra=b_ra, w_ri=w_ri, b_ri=b_ri, lru_lambda=lru_lambda, sinks=sinks,
                   w_rnn_out=w_rnn_out, w_attn_out=w_attn_out, w_o=w_o, b_o=b_o, ln_g=ln_g, ln_b=ln_b)

    def outputs(which):
        return [big[n][which] if n in big else small_out(small_res[which], n, like) for n, like in weights.items()]

    return (loss, g_x[None], *outputs(0), *outputs(1), *outputs(2), *outputs(3))
```

```python
import jax
import jax.numpy as jnp
from jax import lax
from jax.experimental import pallas as pl
from jax.experimental.pallas import tpu as pltpu

F32 = jnp.float32
BF16 = jnp.bfloat16
_MXU = jnp.bfloat16

D = 2048
SEQ = 2048
N_META = 16
BLK = 128
PAD = BLK - N_META
TP = PAD + N_META + SEQ
NBLK = TP // BLK
HALF_TP = TP // 2
N_RB = 8
RB = 256
CONV_W = 4
LRU_C = 8.0
HD = 64
N_Q = 32
N_KV = 4
GRP = 8
D_KV = 256
NEG_INF = -1e30
LN_EPS = 1e-5
ALPHA = 2.0 ** 0.25
ROPE_THETA = 10000.0
OFF_GR, OFF_Q, OFF_K, OFF_V, OFF_GA, OFF_G = 2048, 4096, 6144, 6400, 6656, 8704
D_IN = 12800
N_SH = 4
W_IN_COLS = D_IN // N_SH
TN_IN = 640
PER_IN = W_IN_COLS // TN_IN
CW = 512
RT = TP // 4
SQ_ROWS = 512
PACK_ROWS = 3 * SQ_ROWS + 128

ADAM_LR = 0.001
ADAM_B1 = 0.9
ADAM_B2 = 0.999
ADAM_EPS = 1e-08
ADAM_WD = 0.01
ADAM_STEP = 10

MESH = pl.DeviceIdType.MESH
_MIB = 1024 * 1024
_ANY = pl.BlockSpec(memory_space=pl.ANY)
_NT = (((1,), (1,)), ((), ()))
_TN = (((0,), (0,)), ((), ()))


def _sds(shape, dtype):
    return jax.ShapeDtypeStruct(shape, dtype)


def _sigmoid(x):
    return 1.0 / (1.0 + jnp.exp(-x))


def _my_pos():
    return lax.axis_index("x"), lax.axis_index("y"), lax.axis_index("c")


def _other_chips(x, y):
    return [(1 - x, y), (x, 1 - y), (1 - x, 1 - y)]


def _remote(src, dst, send_sems, recv_sems, k, dev):
    return pltpu.make_async_remote_copy(src_ref=src, dst_ref=dst, send_sem=send_sems.at[k], recv_sem=recv_sems.at[k],
                                        device_id=dev, device_id_type=MESH)


class _Hook:
    def __init__(self, carried, landing, n_sems, start, finish, mid=None, mid_frac=0.5):
        self.carried, self.landing, self.n_sems, self.start, self.finish = list(carried), list(landing), n_sems, start, finish
        self.mid, self.mid_frac = mid, mid_frac


def _hook_gather(buf, mid_frac):
    half = buf.shape[1] // 2
    quarter = half // 2

    def geom(o, ss, rs):
        x, y, c = _my_pos()
        xn, yn, dg = _other_chips(x, y)
        slot = lambda p: 2 * p[0] + p[1]
        mine_rows = pl.ds(pl.multiple_of(c * half, 16), half)
        sib_rows = pl.ds(pl.multiple_of((1 - c) * half, 16), half)
        q_rows = lambda r: pl.ds(pl.multiple_of(c * half + r * quarter, 16), quarter)

        def cp(k, s, rows, dev):
            part = o.at[s, rows]
            return _remote(part, part, ss, rs, k, dev)

        return dict(
            direct=lambda k, s: cp(k, s, mine_rows, ((xn, yn)[k][0], (xn, yn)[k][1], c)),
            relay=lambda r, s: cp(2 + r, s, q_rows(r), ((yn, xn)[r][0], (yn, xn)[r][1], c)),
            sibling=lambda k, s, mine: cp(4 + k, s, mine_rows if mine else sib_rows, (x, y, 1 - c)),
            me=slot((x, y)), slots=(slot(xn), slot(yn), slot(dg)))

    def start(car, land, ss, rs):
        g = geom(car[0], ss, rs)
        g["direct"](0, g["me"]).start()
        g["direct"](1, g["me"]).start()

    def mid(car, land, ss, rs):
        g = geom(car[0], ss, rs)
        for k in range(2):
            g["direct"](k, g["slots"][k]).wait_recv()
            g["relay"](k, g["slots"][k]).start()
            g["sibling"](k, g["slots"][k], True).start()

    def finish(car, land, ss, rs):
        g = geom(car[0], ss, rs)
        dslot = g["slots"][2]
        g["relay"](0, dslot).wait_recv()
        g["relay"](1, dslot).wait_recv()
        g["sibling"](2, dslot, True).start()
        for k in range(3):
            g["sibling"](k, g["slots"][k], False).wait_recv()
        for k in range(2):
            g["direct"](k, g["me"]).wait_send()
            g["relay"](k, g["slots"][k]).wait_send()
        for k in range(3):
            g["sibling"](k, g["slots"][k], True).wait_send()

    return _Hook([buf], [], 7, start, finish, mid=mid, mid_frac=mid_frac)


def _hook_pair(g, half_rows, slots=(0, 1, 2, 3), land=None):
    n, r, cc = g.shape
    h = r // 2 if half_rows else r

    def plan(car, landing, ss, rs):
        x, y, c = _my_pos()
        dst = landing[0] if land is None else car[1]
        cps = []
        for k, s in enumerate(slots):
            src = car[0].at[s, pl.ds(pl.multiple_of((1 - c) * h, 8), h)] if half_rows else car[0].at[s]
            cps.append(_remote(src, dst.at[s], ss, rs, k, (x, y, 1 - c)))
        return cps

    def start(car, land, ss, rs):
        for cp in plan(car, land, ss, rs):
            cp.start()

    def finish(car, land, ss, rs):
        cps = plan(car, land, ss, rs)
        for cp in cps:
            cp.wait_recv()
        for cp in cps:
            cp.wait_send()

    if land is None:
        return _Hook([g], [_sds((n, h, cc), g.dtype)], len(slots), start, finish)
    return _Hook([g, land], [], len(slots), start, finish)


def _hook_scatter_direct(s16):
    _, h, cc = s16.shape
    q = h // 2

    def plan(car, land, ss, rs):
        x, y, c = _my_pos()
        xn, yn, dg = _other_chips(x, y)
        s, (final, raw) = car[0], land
        slot = lambda p: 2 * p[0] + p[1]
        q0, q1 = pl.ds(0, q), pl.ds(q, q)
        return [_remote(s.at[slot(xn), q0], final.at[0, q0], ss, rs, 0, (xn[0], xn[1], c)),
                _remote(s.at[slot(yn), q1], final.at[1, q1], ss, rs, 1, (yn[0], yn[1], c)),
                _remote(s.at[slot(dg), q0], raw.at[1], ss, rs, 2, (xn[0], xn[1], c)),
                _remote(s.at[slot(dg), q1], raw.at[0], ss, rs, 3, (yn[0], yn[1], c))]

    def start(car, land, ss, rs):
        for cp in plan(car, land, ss, rs):
            cp.start()

    def finish(car, land, ss, rs):
        cps = plan(car, land, ss, rs)
        for cp in cps:
            cp.wait_recv()
        for cp in cps:
            cp.wait_send()

    return _Hook([s16], [_sds((2, h, cc), s16.dtype), _sds((2, q, cc), s16.dtype)], 4, start, finish)


def _hook_scatter_relay(comb, final):
    _, q, _ = comb.shape

    def plan(car, ss, rs):
        x, y, c = _my_pos()
        xn, yn, _ = _other_chips(x, y)
        cb, final_ref = car
        return [_remote(cb.at[0], final_ref.at[0, pl.ds(q, q)], ss, rs, 0, (xn[0], xn[1], c)),
                _remote(cb.at[1], final_ref.at[1, pl.ds(0, q)], ss, rs, 1, (yn[0], yn[1], c))]

    def start(car, land, ss, rs):
        for cp in plan(car, ss, rs):
            cp.start()

    def finish(car, land, ss, rs):
        cps = plan(car, ss, rs)
        for cp in cps:
            cp.wait_recv()
        for cp in cps:
            cp.wait_send()

    return _Hook([comb, final], [], 2, start, finish)


def _hook_halves(full):
    h = full.shape[0] // 2

    def half_copy(car, ss, rs, which):
        x, y, c = _my_pos()
        rows = car[0].at[pl.ds(pl.multiple_of((c + which - 2 * c * which) * h, 8), h)]
        return _remote(rows, rows, ss, rs, 0, (x, y, 1 - c))

    def start(car, land, ss, rs):
        half_copy(car, ss, rs, 0).start()

    def finish(car, land, ss, rs):
        half_copy(car, ss, rs, 1).wait_recv()
        half_copy(car, ss, rs, 0).wait_send()

    return _Hook([full], [], 1, start, finish)


def _pcall(body, *, name, grid, in_specs, out_specs, out_shape, operands, scratch=(), vmem=48, sem=None,
           prefetch=(), aliases=None, hook=None):
    n_pre, n_in, n_out, n_scr = len(prefetch), len(in_specs), len(out_specs), len(scratch)
    in_specs, out_specs, out_shape, scratch = list(in_specs), list(out_specs), list(out_shape), list(scratch)
    io_alias = {n_pre + a: b for a, b in (aliases or {}).items()}
    operands = list(operands)
    kernel_body = body
    if hook is not None:
        n_car, n_land = len(hook.carried), len(hook.landing)
        for t, arr in enumerate(hook.carried):
            io_alias[n_pre + n_in + t] = n_out + t
        in_specs += [_ANY] * n_car
        out_specs += [_ANY] * (n_car + n_land)
        out_shape += [_sds(a.shape, a.dtype) for a in hook.carried] + hook.landing
        scratch += [pltpu.SemaphoreType.DMA((hook.n_sems,)), pltpu.SemaphoreType.DMA((hook.n_sems,))]
        operands += hook.carried
        sem = ("arbitrary",) * len(grid)

        def kernel_body(*refs):
            pre, rest = refs[:n_pre], refs[n_pre:]
            ins = rest[:n_in]
            outs = rest[n_in + n_car:n_in + n_car + n_out]
            car = rest[n_in + n_car + n_out:n_in + 2 * n_car + n_out]
            land = rest[n_in + 2 * n_car + n_out:n_in + 2 * n_car + n_out + n_land]
            scr = rest[n_in + 2 * n_car + n_out + n_land:]
            send_sems, recv_sems = scr[n_scr], scr[n_scr + 1]
            first = pl.program_id(0) == 0
            last = pl.program_id(0) == grid[0] - 1
            for d in range(1, len(grid)):
                first = first & (pl.program_id(d) == 0)
                last = last & (pl.program_id(d) == grid[d] - 1)

            @pl.when(first)
            def _():
                hook.start(car, land, send_sems, recv_sems)

            if hook.mid is not None:
                step = pl.program_id(0)
                total = grid[0]
                for d in range(1, len(grid)):
                    step = step * grid[d] + pl.program_id(d)
                    total *= grid[d]

                @pl.when(step == int(total * hook.mid_frac))
                def _():
                    hook.mid(car, land, send_sems, recv_sems)

            body(*pre, *ins, *outs, *scr[:n_scr])

            @pl.when(last)
            def _():
                hook.finish(car, land, send_sems, recv_sems)

    params = pltpu.CompilerParams(vmem_limit_bytes=vmem * _MIB, dimension_semantics=sem,
                                  has_side_effects=hook is not None)
    if n_pre:
        call = pl.pallas_call(
            kernel_body, name=name, out_shape=out_shape, input_output_aliases=io_alias, compiler_params=params,
            grid_spec=pltpu.PrefetchScalarGridSpec(num_scalar_prefetch=n_pre, grid=grid, in_specs=in_specs,
                                                   out_specs=out_specs, scratch_shapes=scratch))
    else:
        call = pl.pallas_call(kernel_body, name=name, grid=grid, in_specs=in_specs, out_specs=out_specs,
                              out_shape=out_shape, scratch_shapes=scratch, input_output_aliases=io_alias,
                              compiler_params=params)
    return call(*prefetch, *operands)


def _gather_small(blk):
    r, cc = blk.shape

    def body(x_ref, o_ref, send_sems, recv_sems):
        x, y, c = _my_pos()
        me = 2 * x + y
        o_ref[me] = x_ref[...]
        sends = [_remote(x_ref, o_ref.at[me], send_sems, recv_sems, k, (px, py, c))
                 for k, (px, py) in enumerate(_other_chips(x, y))]
        for cp in sends:
            cp.start()
        for k, (px, py) in enumerate(_other_chips(x, y)):
            _remote(x_ref, o_ref.at[2 * px + py], send_sems, recv_sems, k, (px, py, c)).wait_recv()
        for cp in sends:
            cp.wait_send()

    vm = pl.BlockSpec(memory_space=pltpu.VMEM)
    return pl.pallas_call(
        body, name="gather_small", in_specs=[vm], out_specs=vm, out_shape=_sds((N_SH, r, cc), blk.dtype),
        scratch_shapes=[pltpu.SemaphoreType.DMA((3,)), pltpu.SemaphoreType.DMA((3,))],
        compiler_params=pltpu.CompilerParams(has_side_effects=True),
    )(blk)


N_DEV = 8


def _allreduce_small(pack, halves):
    r, cc = pack.shape
    ride = _hook_halves(halves)

    def body(x_ref, h_in_ref, o_ref, h_ref, buf_ref, send_sems, recv_sems, h_ss, h_rs):
        del h_in_ref
        ride.start([h_ref], [], h_ss, h_rs)
        x, y, c = _my_pos()
        me = 4 * x + 2 * y + c
        buf_ref[me] = x_ref[...]
        cps = []
        for k in range(1, N_DEV):
            peer = (x ^ ((k >> 2) & 1), y ^ ((k >> 1) & 1), c ^ (k & 1))
            cps.append(_remote(x_ref, buf_ref.at[me], send_sems, recv_sems, k - 1, peer))
        for cp in cps:
            cp.start()
        for k in range(1, N_DEV):
            peer = (x ^ ((k >> 2) & 1), y ^ ((k >> 1) & 1), c ^ (k & 1))
            src = 4 * peer[0] + 2 * peer[1] + peer[2]
            _remote(x_ref, buf_ref.at[src], send_sems, recv_sems, k - 1, peer).wait_recv()
        acc = buf_ref[0]
        for d in range(1, N_DEV):
            acc = acc + buf_ref[d]
        o_ref[...] = acc
        for cp in cps:
            cp.wait_send()
        ride.finish([h_ref], [], h_ss, h_rs)

    vm = pl.BlockSpec(memory_space=pltpu.VMEM)
    return pl.pallas_call(
        body, name="allreduce_small", in_specs=[vm, _ANY], out_specs=[vm, _ANY],
        out_shape=[_sds((r, cc), F32), _sds(halves.shape, halves.dtype)], input_output_aliases={1: 1},
        scratch_shapes=[pltpu.VMEM((N_DEV, r, cc), F32), pltpu.SemaphoreType.DMA((N_DEV - 1,)),
                        pltpu.SemaphoreType.DMA((N_DEV - 1,)), pltpu.SemaphoreType.DMA((1,)), pltpu.SemaphoreType.DMA((1,))],
        compiler_params=pltpu.CompilerParams(has_side_effects=True),
    )(pack, halves)


def _mm_nn(a, w, bias, *, name, grid, tm, tn, k, a_index, w_block, w_index, out_cols, out_index, carried=None, hook=None):
    m = a.shape[0]

    def body(a_ref, w_ref, b_ref, *rest):
        o_ref = rest[-1]
        wv = w_ref[...]
        acc = jnp.dot(a_ref[...].astype(_MXU), wv.reshape(k, tn).astype(_MXU), preferred_element_type=F32)
        o_ref[...] = acc + b_ref[...]

    operands = [a, w, bias] + ([carried] if carried is not None else [])
    return _pcall(
        body, name=name, grid=grid,
        in_specs=[pl.BlockSpec((tm, k), a_index), pl.BlockSpec(w_block, w_index),
                  pl.BlockSpec((1, tn), lambda i, j: (0, j))] + ([_ANY] if carried is not None else []),
        out_specs=[pl.BlockSpec((tm, tn), out_index)], out_shape=[_sds((m, out_cols), F32)], operands=operands,
        vmem=56, sem=("parallel", "parallel"), aliases={3: 0} if carried is not None else None, hook=hook)


def _mm_nt(a, w, *, name, grid, tm, tn, tk, a_index, w_block, w_index, out_cols, hook=None):
    m = a.shape[0]
    nk = grid[2]

    def body(a_ref, w_ref, o_ref, acc_ref):
        part = lax.dot_general(a_ref[...].astype(_MXU), w_ref[...].astype(_MXU), _NT, preferred_element_type=F32)
        if nk == 1:
            o_ref[...] = part
        else:
            kidx = pl.program_id(2)

            @pl.when(kidx == 0)
            def _():
                acc_ref[...] = part

            @pl.when(kidx > 0)
            def _():
                acc_ref[...] += part

            @pl.when(kidx == nk - 1)
            def _():
                o_ref[...] = acc_ref[...]

    return _pcall(
        body, name=name, grid=grid, in_specs=[pl.BlockSpec((tm, tk), a_index), pl.BlockSpec(w_block, w_index)],
        out_specs=[pl.BlockSpec((tm, tn), lambda i, j, q: (i, j))], out_shape=[_sds((m, out_cols), F32)],
        operands=[a, w], scratch=[pltpu.VMEM((tm, tn) if nk > 1 else (8, 128), F32)], vmem=60,
        sem=("parallel", "parallel", "arbitrary"), hook=hook)


def _mm_tn(at, b, *, name, grid, tm, tn, a_index, b_index, out_shape, out_block, out_index, carried=None,
           prefetch=(), hook=None, out_dtype=F32, colsum=False, a_transposed=True):
    t = at.shape[1] if a_transposed else at.shape[0]
    n_pre = len(prefetch)

    def body(*refs):
        a_ref, b_ref = refs[n_pre], refs[n_pre + 1]
        bv = b_ref[...]
        o_ref = refs[-2] if colsum else refs[-1]
        acc = lax.dot_general(a_ref[...].astype(_MXU), bv.astype(_MXU), (((1,), (0,)), ((), ())) if a_transposed else _TN,
                              preferred_element_type=F32)
        o_ref[...] = acc.astype(out_dtype)
        if colsum:
            refs[-1][...] = jnp.sum(bv.astype(F32), axis=0, keepdims=True)

    operands = [at, b] + ([carried] if carried is not None else [])
    out_specs = [pl.BlockSpec(out_block, out_index)]
    out_shapes = [_sds(out_shape, out_dtype)]
    if colsum:
        out_specs.append(pl.BlockSpec((1, tn), b_index))
        out_shapes.append(_sds((1, b.shape[1]), F32))
    return _pcall(
        body, name=name, grid=grid,
        in_specs=[pl.BlockSpec((tm, t) if a_transposed else (t, tm), a_index), pl.BlockSpec((t, tn), b_index)]
                 + ([_ANY] if carried is not None else []),
        out_specs=out_specs, out_shape=out_shapes, operands=operands,
        vmem=56, sem=("parallel", "parallel"), aliases={2: 0} if carried is not None else None, prefetch=prefetch, hook=hook)


def _remote_tile(n):
    near = 2 * PER_IN
    if isinstance(n, int):
        return (n % 2, n // 2) if n < near else (2, n - near)
    return jnp.where(n < near, n % 2, 2), jnp.where(n < near, n // 2, n - near)


def _mm_z_gather(h16, w_own, b_in, order):
    n_tiles = N_SH * PER_IN
    n_remote = 3 * PER_IN
    half = D // 2

    def body(order_ref, a_ref, b_ref, w_in_ref, z_ref, w_ref, wbuf, tile_sems, ss, rs):
        del w_in_ref
        j = pl.program_id(0)
        x, y, c = _my_pos()
        me = 2 * x + y
        chips = _other_chips(x, y)
        mine_rows = pl.ds(pl.multiple_of(c * half, 16), half)
        sib_rows = pl.ds(pl.multiple_of((1 - c) * half, 16), half)

        slots = [2 * px + py for px, py in chips]
        cols = lambda t: pl.ds(t * TN_IN, TN_IN)
        q_rows = lambda r: pl.ds(pl.multiple_of(c * half + r * (half // 2), 16), half // 2)

        def direct(rel, t, slot):
            px, py = chips[rel]
            part = w_ref.at[slot, mine_rows, cols(t)]
            return _remote(part, part, ss, rs, 2 * t + rel, (px, py, c))

        def relay(r, t, slot):
            px, py = chips[1 - r]
            part = w_ref.at[slot, q_rows(r), cols(t)]
            return _remote(part, part, ss, rs, 2 * PER_IN + 2 * t + r, (px, py, c))

        def d2d(n, rows):
            rel, t = _remote_tile(n)
            part = w_ref.at[slots[rel], rows, cols(t)]
            return _remote(part, part, ss, rs, 4 * PER_IN + n, (x, y, 1 - c))

        def tile_copy(step):
            rel, t = _remote_tile(jnp.maximum(step - PER_IN, 0))
            slot = jnp.where(step < PER_IN, me, me ^ jnp.where(rel == 0, 2, jnp.where(rel == 1, 1, 3)))
            col = pl.multiple_of(jnp.where(step < PER_IN, step, t) * TN_IN, 128)
            return pltpu.make_async_copy(w_ref.at[slot, :, pl.ds(col, TN_IN)], wbuf.at[step % 2], tile_sems.at[step % 2])

        @pl.when(j == 0)
        def _():
            for t in range(PER_IN):
                direct(0, t, me).start()
                direct(1, t, me).start()
            tile_copy(0).start()

        for n in range(n_remote):
            rel, t = _remote_tile(n)

            @pl.when(j == n + PER_IN - 3)
            def _():
                if rel < 2:
                    direct(rel, t, slots[rel]).wait_recv()
                    relay(rel, t, slots[rel]).start()
                else:
                    relay(0, t, slots[2]).wait_recv()
                    relay(1, t, slots[2]).wait_recv()
                d2d(n, mine_rows).start()

            @pl.when(j == n + PER_IN - 2)
            def _():
                d2d(n, sib_rows).wait_recv()

        @pl.when(j + 1 < n_tiles)
        def _():
            tile_copy(j + 1).start()

        tile_copy(j).wait()
        z_ref[...] = jnp.dot(a_ref[...], wbuf[j % 2], preferred_element_type=F32) + b_ref[...]

        @pl.when(j == n_tiles - 1)
        def _():
            for t in range(PER_IN):
                for r in range(2):
                    direct(r, t, me).wait_send()
                    relay(r, t, slots[r]).wait_send()
            for n in range(n_remote):
                d2d(n, mine_rows).wait_send()

    def col_tile(j, o):
        rel, t = _remote_tile(jnp.maximum(j - PER_IN, 0))
        return 0, jnp.where(j < PER_IN, o[0] * PER_IN + j, o[1 + rel] * PER_IN + t)

    return pl.pallas_call(
        body, name="mm_z_gather",
        grid_spec=pltpu.PrefetchScalarGridSpec(
            num_scalar_prefetch=1, grid=(n_tiles,),
            in_specs=[pl.BlockSpec((TP, D), lambda j, o: (0, 0)), pl.BlockSpec((1, TN_IN), col_tile), _ANY],
            out_specs=[pl.BlockSpec((TP, TN_IN), col_tile), _ANY],
            scratch_shapes=[pltpu.VMEM((2, D, TN_IN), BF16), pltpu.SemaphoreType.DMA((2,)),
                            pltpu.SemaphoreType.DMA((4 * PER_IN + n_remote,)),
                            pltpu.SemaphoreType.DMA((4 * PER_IN + n_remote,))]),
        out_shape=[_sds((TP, D_IN), F32), _sds(w_own.shape, w_own.dtype)],
        input_output_aliases={3: 1},
        compiler_params=pltpu.CompilerParams(vmem_limit_bytes=60 * _MIB, dimension_semantics=("arbitrary",),
                                             has_side_effects=True),
    )(order, h16, b_in, w_own)


def _padded_rows(i, x_ref, meta_ref):
    head = jnp.concatenate([jnp.zeros((PAD, D), F32), meta_ref[...]], axis=0)
    return jnp.where(i == 0, head, x_ref[...])


def _stream_specs():
    return [pl.BlockSpec((BLK, D), lambda i: (jnp.maximum(i - 1, 0), 0)), pl.BlockSpec((N_META, D), lambda i: (0, 0))]


def _ln_emb_fwd(x, meta, g, b):
    def body(x_ref, meta_ref, g_ref, b_ref, h32_ref, h16_ref, h16t_ref):
        x = _padded_rows(pl.program_id(0), x_ref, meta_ref)
        mu = jnp.mean(x, axis=-1, keepdims=True)
        xc = x - mu
        var = jnp.mean(xc * xc, axis=-1, keepdims=True)
        y = xc * lax.rsqrt(var + LN_EPS) * g_ref[...] + b_ref[...]
        h32_ref[...] = y
        h16_ref[...] = y.astype(_MXU)
        h16t_ref[...] = y.T.astype(_MXU)

    row = pl.BlockSpec((BLK, D), lambda i: (i, 0))
    vec = pl.BlockSpec((1, D), lambda i: (0, 0))
    return _pcall(body, name="ln_emb_fwd", grid=(NBLK,), in_specs=_stream_specs() + [vec, vec],
                  out_specs=[row, row, pl.BlockSpec((D, BLK), lambda i: (0, i))],
                  out_shape=[_sds((TP, D), F32), _sds((TP, D), _MXU), _sds((D, TP), _MXU)], operands=[x, meta, g, b],
                  vmem=32, sem=("parallel",))


def _adamw_step(w_ref, g_ref, m_ref, v_ref, go_ref, d_ref, mo_ref, vo_ref):
    gg = g_ref[...]
    go_ref[...] = gg
    m_new = ADAM_B1 * m_ref[...] + (1.0 - ADAM_B1) * gg
    v_new = ADAM_B2 * v_ref[...] + (1.0 - ADAM_B2) * (gg * gg)
    m_hat = m_new / (1.0 - ADAM_B1 ** ADAM_STEP)
    v_hat = v_new / (1.0 - ADAM_B2 ** ADAM_STEP)
    d_ref[...] = -ADAM_LR * (m_hat / (jnp.sqrt(v_hat) + ADAM_EPS) + ADAM_WD * w_ref[...])
    mo_ref[...] = m_new
    vo_ref[...] = v_new


def _ln_emb_bwd(x, meta, g, dr, dhz, g_sq, sq, *, hook):
    n_sq = len(sq)
    sq_steps = 16
    sq_rows = SQ_ROWS // sq_steps

    def body(x_ref, meta_ref, g_ref, dr_ref, dhz_ref, *rest):
        sq_in = rest[:4 * n_sq]
        dx_ref, dmeta_ref, acc_ref = rest[4 * n_sq:4 * n_sq + 3]
        sq_out = rest[4 * n_sq + 3:]
        i = pl.program_id(0)

        @pl.when(i < sq_steps)
        def _():
            for kk in range(n_sq):
                _adamw_step(*sq_in[4 * kk:4 * kk + 4], *sq_out[4 * kk:4 * kk + 4])


        @pl.when(i == 0)
        def _():
            acc_ref[...] = jnp.zeros_like(acc_ref)

        x = _padded_rows(i, x_ref, meta_ref)
        mu = jnp.mean(x, axis=-1, keepdims=True)
        xc = x - mu
        var = jnp.mean(xc * xc, axis=-1, keepdims=True)
        rstd = lax.rsqrt(var + LN_EPS)
        xhat = xc * rstd
        dh = ALPHA * dr_ref[...] + dhz_ref[...]
        acc_ref[0:1, :] += jnp.sum(dh * xhat, axis=0, keepdims=True)
        acc_ref[1:2, :] += jnp.sum(dh, axis=0, keepdims=True)
        dxh = dh * g_ref[...]
        m1 = jnp.mean(dxh, axis=-1, keepdims=True)
        m2 = jnp.mean(dxh * xhat, axis=-1, keepdims=True)
        dx = rstd * (dxh - m1 - xhat * m2)
        dx_ref[...] = dx

        @pl.when(i == 0)
        def _():
            dmeta_ref[...] = dx[PAD:BLK]

    row = pl.BlockSpec((BLK, D), lambda i: (i, 0))
    vec = pl.BlockSpec((1, D), lambda i: (0, 0))
    xs, ms = _stream_specs()
    sq_blk = pl.BlockSpec((sq_rows, D), lambda i: (jnp.minimum(i, sq_steps - 1), 0))
    sq_specs, sq_operands = [], []
    for kk, (w_, m_, v_) in enumerate(sq):
        g_blk = pl.BlockSpec((sq_rows, D), lambda i, kk=kk: (sq_steps * kk + jnp.minimum(i, sq_steps - 1), 0))
        sq_specs += [sq_blk, g_blk, sq_blk, sq_blk]
        sq_operands += [w_, g_sq, m_, v_]
    return _pcall(body, name="ln_emb_bwd", grid=(NBLK,), in_specs=[xs, ms, vec, row, row] + sq_specs,
                  out_specs=[xs, ms, pl.BlockSpec((8, D), lambda i: (0, 0))] + [sq_blk] * (4 * n_sq),
                  out_shape=[_sds((SEQ, D), F32), _sds((N_META, D), F32), _sds((8, D), F32)]
                            + [_sds((SQ_ROWS, D), F32)] * (4 * n_sq),
                  operands=[x, meta, g, dr, dhz] + sq_operands, vmem=40, sem=("arbitrary",), hook=hook)


def _mul_silu_bwd(dy, a, z, off, dz, *, name, hook=None):
    def body(dy_ref, a_ref, z_ref, dz_in, da_ref, dg_ref):
        zz = z_ref[...]
        sg = _sigmoid(zz)
        d = dy_ref[...]
        da_ref[...] = d * (zz * sg)
        dg_ref[...] = (d * a_ref[...] * (sg * (1.0 + zz * (1.0 - sg)))).astype(_MXU)

    blk = pl.BlockSpec((RT, CW), lambda i, j: (i, j))
    zblk = pl.BlockSpec((RT, CW), lambda i, j: (i, off // CW + j))
    return _pcall(body, name=name, grid=(TP // RT, D // CW), in_specs=[blk, blk, zblk, _ANY], out_specs=[blk, zblk],
                  out_shape=[_sds((TP, D), F32), _sds((TP, D_IN), _MXU)], operands=[dy, a, z, dz], vmem=32,
                  sem=("parallel", "parallel"), aliases={3: 1}, hook=hook)


def _merge_fwd(y2, z):
    w = 256

    def body(ya_ref, yb_ref, ga_ref, gb_ref, o_ref, t_ref):
        y = _sigmoid(ga_ref[...]) * ya_ref[...] + _sigmoid(gb_ref[...]) * yb_ref[...]
        o_ref[...] = y.astype(_MXU)
        t_ref[...] = y.T.astype(_MXU)

    nb = D // w
    strip = pl.BlockSpec((TP, w), lambda j: (0, j))
    return _pcall(body, name="merge_fwd", grid=(nb,),
                  in_specs=[strip, pl.BlockSpec((TP, w), lambda j: (0, nb + j)),
                            pl.BlockSpec((TP, w), lambda j: (0, OFF_G // w + j)),
                            pl.BlockSpec((TP, w), lambda j: (0, (OFF_G + D) // w + j))],
                  out_specs=[strip, pl.BlockSpec((w, TP), lambda j: (j, 0))],
                  out_shape=[_sds((TP, D), _MXU), _sds((D, TP), _MXU)], operands=[y2, y2, z, z], vmem=56,
                  sem=("parallel",))


def _merge_bwd(dmix, y2, z):
    nb = D // CW

    def body(dm_ref, y_ref, g_ref, dy_ref, dg_ref):
        dm = dm_ref[...]
        sg = _sigmoid(g_ref[...])
        dy_ref[...] = (dm * sg).astype(_MXU)
        dg_ref[...] = (dm * y_ref[...] * sg * (1.0 - sg)).astype(_MXU)

    blk = pl.BlockSpec((RT, CW), lambda i, j: (i, j))
    gblk = pl.BlockSpec((RT, CW), lambda i, j: (i, OFF_G // CW + j))
    return _pcall(body, name="merge_bwd", grid=(TP // RT, 2 * nb),
                  in_specs=[pl.BlockSpec((RT, CW), lambda i, j: (i, j % nb)), blk, gblk], out_specs=[blk, gblk],
                  out_shape=[_sds((TP, 2 * D), _MXU), _sds((TP, D_IN), _MXU)], operands=[dmix, y2, z], vmem=32,
                  sem=("parallel", "parallel"))


def _ln_out_loss(h32, out, target, g, b):
    def body(h_ref, o_ref, t_ref, g_ref, b_ref, dr_ref, dr16_ref, acc_ref):
        i = pl.program_id(0)

        @pl.when(i == 0)
        def _():
            acc_ref[...] = jnp.zeros_like(acc_ref)

        r = ALPHA * h_ref[...] + o_ref[...]
        mu = jnp.mean(r, axis=-1, keepdims=True)
        rc = r - mu
        var = jnp.mean(rc * rc, axis=-1, keepdims=True)
        rstd = lax.rsqrt(var + LN_EPS)
        xhat = rc * rstd
        gg = g_ref[...]
        y = xhat * gg + b_ref[...]
        real = (i >= 1).astype(F32)
        diff = (y - t_ref[...]) * real
        dy = diff * (1.0 / D)
        dxh = dy * gg
        m1 = jnp.mean(dxh, axis=-1, keepdims=True)
        m2 = jnp.mean(dxh * xhat, axis=-1, keepdims=True)
        dr = rstd * (dxh - m1 - xhat * m2)
        dr_ref[...] = dr
        dr16_ref[...] = dr.astype(_MXU)
        acc_ref[0:1, :] += jnp.sum(dy * xhat, axis=0, keepdims=True)
        acc_ref[1:2, :] += jnp.sum(dy, axis=0, keepdims=True)
        acc_ref[2:3, :] += jnp.sum(dr, axis=0, keepdims=True)
        acc_ref[3:4, :] += (0.5 / D) * jnp.sum(diff * diff)

    row = pl.BlockSpec((BLK, D), lambda i: (i, 0))
    vec = pl.BlockSpec((1, D), lambda i: (0, 0))
    return _pcall(body, name="ln_out_loss", grid=(NBLK,),
                  in_specs=[row, row, pl.BlockSpec((BLK, D), lambda i: (jnp.maximum(i - 1, 0), 0)), vec, vec],
                  out_specs=[row, row, pl.BlockSpec((8, D), lambda i: (0, 0))],
                  out_shape=[_sds((TP, D), F32), _sds((TP, D), _MXU), _sds((8, D), F32)],
                  operands=[h32, out, target, g, b], vmem=32, sem=("arbitrary",))


def _rnn_recompute(xr_ref, cw_ref, cb_ref, wra_ref, wri_ref, bra_ref, bri_ref, lam_ref):
    rows = lax.broadcasted_iota(jnp.int32, (TP, 1), 0)
    valid = (rows >= PAD).astype(F32)
    first = rows == PAD
    x = xr_ref[...] * valid
    cw = cw_ref[...]
    shifted = [x, pltpu.roll(x, 1, 0), pltpu.roll(x, 2, 0), pltpu.roll(x, 3, 0)]
    c = cb_ref[...] + cw[0:1, :] * shifted[0] + cw[1:2, :] * shifted[1] + cw[2:3, :] * shifted[2] + cw[3:4, :] * shifted[3]
    cm = c.astype(_MXU)
    gr = _sigmoid(jnp.dot(cm, wra_ref[...].astype(_MXU), preferred_element_type=F32) + bra_ref[...])
    gi = _sigmoid(jnp.dot(cm, wri_ref[...].astype(_MXU), preferred_element_type=F32) + bri_ref[...])
    lam = lam_ref[...]
    ls = jnp.minimum(lam, 0.0) - jnp.log(1.0 + jnp.exp(-jnp.abs(lam)))
    log_a = LRU_C * gr * ls
    a = jnp.exp(log_a)
    mult = jnp.where(first, 1.0, jnp.sqrt(1.0 - jnp.exp(2.0 * log_a)))
    return dict(valid=valid, first=first, shifted=shifted, c=c, cm=cm, gr=gr, gi=gi, ls=ls, a=a, mult=mult, lam=lam)


def _rnn_specs():
    col = pl.BlockSpec((TP, RB), lambda n: (0, n))
    vec = pl.BlockSpec((1, RB), lambda n: (0, n))
    return dict(col=col, vec=vec, cw=pl.BlockSpec((CONV_W, RB), lambda n: (0, n)),
                wblk=pl.BlockSpec((None, RB, RB), lambda n: (n, 0, 0)))


def _rnn_gates_fwd(z, conv_w, conv_b, w_ra, w_ri, b_ra, b_ri, lam):
    def body(xr_ref, cw_ref, cb_ref, wra_ref, wri_ref, bra_ref, bri_ref, lam_ref, a_ref, u_ref):
        r = _rnn_recompute(xr_ref, cw_ref, cb_ref, wra_ref, wri_ref, bra_ref, bri_ref, lam_ref)
        a_ref[...] = r["a"]
        u_ref[...] = r["mult"] * r["gi"] * r["c"] * r["valid"]

    s = _rnn_specs()
    return _pcall(body, name="rnn_gates_fwd", grid=(N_RB,),
                  in_specs=[s["col"], s["cw"], s["vec"], s["wblk"], s["wblk"], s["vec"], s["vec"], s["vec"]],
                  out_specs=[s["col"], s["col"]], out_shape=[_sds((TP, D), F32)] * 2,
                  operands=[z, conv_w, conv_b, w_ra, w_ri, b_ra, b_ri, lam], vmem=56, sem=("parallel",))


SCAN_ROWS = 272


SUB = 8


def _tile_scan(a, u, reverse):
    rows = lax.broadcasted_iota(jnp.int32, a.shape, 0)
    for d in (1, 2, 4):
        shift = SUB - d if reverse else d
        inside = (rows < SUB - d) if reverse else (rows >= d)
        u = u + a * jnp.where(inside, pltpu.roll(u, shift, 0), 0.0)
        a = a * jnp.where(inside, pltpu.roll(a, shift, 0), 1.0)
    return a, u


def _scan_fwd(a, u, z, *, hook):
    def body(a_ref, u_ref, gr_ref, h_ref, y_ref, carry_ref):
        @pl.when(pl.program_id(1) == 0)
        def _():
            carry_ref[...] = jnp.zeros_like(carry_ref)

        def step(r, h):
            rows = pl.ds(pl.multiple_of(r * SUB, SUB), SUB)
            prod, part = _tile_scan(a_ref[rows, :], u_ref[rows, :], False)
            ht = part + prod * h
            h_ref[rows, :] = ht
            return ht[SUB - 1:SUB, :]

        carry_ref[...] = lax.fori_loop(0, SCAN_ROWS // SUB, step, carry_ref[...], unroll=2)
        zz = gr_ref[...]
        y_ref[...] = (h_ref[...] * (zz * _sigmoid(zz))).astype(_MXU)

    blk = pl.BlockSpec((SCAN_ROWS, CW), lambda j, i: (i, j))
    return _pcall(body, name="scan_fwd", grid=(D // CW, TP // SCAN_ROWS),
                  in_specs=[blk, blk, pl.BlockSpec((SCAN_ROWS, CW), lambda j, i: (i, OFF_GR // CW + j))],
                  out_specs=[blk, blk], out_shape=[_sds((TP, D), F32), _sds((TP, D), _MXU)], operands=[a, u, z],
                  scratch=[pltpu.VMEM((1, CW), F32)], vmem=32, sem=("parallel", "arbitrary"), hook=hook)


def _scan_bwd(a, dh):
    nst = TP // SCAN_ROWS
    n_tiles = SCAN_ROWS // SUB

    def body(a_ref, d_ref, o_ref, lam_ref, anext_ref):
        @pl.when(pl.program_id(1) == 0)
        def _():
            lam_ref[...] = jnp.zeros_like(lam_ref)
            anext_ref[...] = jnp.zeros_like(anext_ref)

        def step(q, carry):
            lam_next, a_next = carry
            rows = pl.ds(pl.multiple_of((n_tiles - 1 - q) * SUB, SUB), SUB)
            at = a_ref[rows, :]
            last = lax.broadcasted_iota(jnp.int32, at.shape, 0) == SUB - 1
            b = jnp.where(last, a_next, pltpu.roll(at, SUB - 1, 0))
            prod, part = _tile_scan(b, d_ref[rows, :], True)
            lam = part + prod * lam_next
            o_ref[rows, :] = lam
            return lam[0:1, :], at[0:1, :]

        lam, an = lax.fori_loop(0, n_tiles, step, (lam_ref[...], anext_ref[...]), unroll=2)
        lam_ref[...] = lam
        anext_ref[...] = an

    blk = pl.BlockSpec((SCAN_ROWS, CW), lambda j, i: (nst - 1 - i, j))
    return _pcall(body, name="scan_bwd", grid=(D // CW, nst), in_specs=[blk, blk], out_specs=[blk],
                  out_shape=[_sds((TP, D), F32)], operands=[a, dh],
                  scratch=[pltpu.VMEM((1, CW), F32), pltpu.VMEM((1, CW), F32)], vmem=32,
                  sem=("parallel", "arbitrary"))[0]


def _rnn_gates_bwd(z, lam_s, hr, conv_w, conv_b, w_ra, w_ri, b_ra, b_ri, lam, dz, gsq):
    def body(xr_ref, ls_ref, hr_ref, cw_ref, cb_ref, wra_ref, wri_ref, bra_ref, bri_ref, lam_ref, dz_in, gsq_in,
             dx_ref, dw_ref, sums_ref):
        r = _rnn_recompute(xr_ref, cw_ref, cb_ref, wra_ref, wri_ref, bra_ref, bri_ref, lam_ref)
        valid, c, gr, gi, a, mult = r["valid"], r["c"], r["gr"], r["gi"], r["a"], r["mult"]
        du = ls_ref[...] * valid
        da = du * pltpu.roll(hr_ref[...], 1, 0)
        d_gi = du * mult * c
        dc = du * mult * gi
        dmult = du * gi * c
        dlog_a = da * a + jnp.where(r["first"], 0.0, -dmult * a * a / mult)
        d_gr = dlog_a * (LRU_C * r["ls"])
        dls = jnp.sum(dlog_a * (LRU_C * gr), axis=0, keepdims=True)
        dpre_r = d_gr * gr * (1.0 - gr)
        dpre_i = d_gi * gi * (1.0 - gi)
        pr = dpre_r.astype(_MXU)
        pi = dpre_i.astype(_MXU)
        dwra = lax.dot_general(r["cm"], pr, _TN, preferred_element_type=F32)
        dwri = lax.dot_general(r["cm"], pi, _TN, preferred_element_type=F32)
        for s in range(N_SH):
            dw_ref[s, 0:64, :] = dwra[64 * s:64 * (s + 1)]
            dw_ref[s, 64:128, :] = dwri[64 * s:64 * (s + 1)]
        dc = dc + lax.dot_general(pr, wra_ref[...].astype(_MXU), _NT, preferred_element_type=F32)
        dc = dc + lax.dot_general(pi, wri_ref[...].astype(_MXU), _NT, preferred_element_type=F32)
        cw = cw_ref[...]
        dx = cw[0:1, :] * dc
        for k in range(1, CONV_W):
            dx = dx + cw[k:k + 1, :] * pltpu.roll(dc, TP - k, 0)
        dx_ref[...] = (dx * valid).astype(_MXU)
        for k in range(CONV_W):
            sums_ref[k:k + 1, :] = jnp.sum(dc * r["shifted"][k], axis=0, keepdims=True)
        sums_ref[4:5, :] = jnp.sum(dc, axis=0, keepdims=True)
        sums_ref[5:6, :] = jnp.sum(dpre_r, axis=0, keepdims=True)
        sums_ref[6:7, :] = jnp.sum(dpre_i, axis=0, keepdims=True)
        sums_ref[7:8, :] = dls * _sigmoid(-r["lam"])

    s = _rnn_specs()
    return _pcall(
        body, name="rnn_gates_bwd", grid=(N_RB,),
        in_specs=[s["col"], s["col"], s["col"], s["cw"], s["vec"], s["wblk"], s["wblk"], s["vec"], s["vec"], s["vec"],
                  _ANY, _ANY],
        out_specs=[s["col"], pl.BlockSpec((N_SH, 128, RB), lambda n: (0, 3 * SQ_ROWS // 128, n)),
                   pl.BlockSpec((8, RB), lambda n: (0, n))],
        out_shape=[_sds((TP, D_IN), _MXU), _sds((N_SH, PACK_ROWS, D), F32), _sds((8, D), F32)],
        operands=[z, lam_s, hr, conv_w, conv_b, w_ra, w_ri, b_ra, b_ri, lam, dz, gsq], vmem=60, sem=("parallel",),
        aliases={10: 0, 11: 1})


def _rope_tables():
    half = HD // 2
    inv = ROPE_THETA ** (-jnp.arange(half, dtype=F32) / half)
    pos = (jnp.arange(TP) - PAD).astype(F32)
    ang = pos[:, None] * inv[None, :]
    return jnp.tile(jnp.cos(ang), (1, 4)), jnp.tile(jnp.sin(ang), (1, 4))


def _rope(x, cos_t, sin_t, sign):
    w = x.shape[1]
    lane = lax.broadcasted_iota(jnp.int32, x.shape, 1)
    first = (lane % HD) < (HD // 2)
    swapped = jnp.where(first, pltpu.roll(x, w - HD // 2, 1), pltpu.roll(x, HD // 2, 1))
    ct = jnp.tile(cos_t, (1, w // 128))
    st = jnp.tile(sin_t, (1, w // 128))
    return x * ct + swapped * jnp.where(first, -sign * st, sign * st)


def _rope_fwd(z, cos_t, sin_t):
    def body(q_ref, k_ref, v_ref, c_ref, s_ref, qo_ref, ko_ref, vo_ref):
        c = c_ref[...]
        s = s_ref[...]
        qo_ref[...] = _rope(q_ref[...], c, s, 1.0).astype(_MXU)
        ko_ref[...] = _rope(k_ref[...], c, s, 1.0).astype(_MXU)
        vo_ref[...] = v_ref[...].astype(_MXU)

    tab = pl.BlockSpec((BLK, 128), lambda i: (i, 0))
    kv = pl.BlockSpec((BLK, D_KV), lambda i: (i, 0))
    return _pcall(body, name="rope_fwd", grid=(NBLK,),
                  in_specs=[pl.BlockSpec((BLK, D), lambda i: (i, OFF_Q // D)),
                            pl.BlockSpec((BLK, D_KV), lambda i: (i, OFF_K // D_KV)),
                            pl.BlockSpec((BLK, D_KV), lambda i: (i, OFF_V // D_KV)), tab, tab],
                  out_specs=[pl.BlockSpec((BLK, D), lambda i: (i, 0)), kv, kv],
                  out_shape=[_sds((TP, D), _MXU), _sds((TP, D_KV), _MXU), _sds((TP, D_KV), _MXU)],
                  operands=[z, z, z, cos_t, sin_t], vmem=32, sem=("parallel",))


def _rope_bwd_k(dk, dv, cos_t, sin_t, dz):
    def body(dk_ref, dv_ref, c_ref, s_ref, dz_in, o_ref):
        o_ref[:, 0:D_KV] = _rope(dk_ref[...], c_ref[...], s_ref[...], -1.0).astype(_MXU)
        o_ref[:, D_KV:2 * D_KV] = dv_ref[...].astype(_MXU)

    tab = pl.BlockSpec((BLK, 128), lambda i: (i, 0))
    kv = pl.BlockSpec((BLK, D_KV), lambda i: (i, 0))
    return _pcall(body, name="rope_bwd_k", grid=(NBLK,), in_specs=[kv, kv, tab, tab, _ANY],
                  out_specs=[pl.BlockSpec((BLK, 2 * D_KV), lambda i: (i, OFF_K // (2 * D_KV)))],
                  out_shape=[_sds((TP, D_IN), _MXU)], operands=[dk, dv, cos_t, sin_t, dz], vmem=32, sem=("parallel",),
                  aliases={4: 0})[0]


def _attn_mask(i):
    ql = lax.broadcasted_iota(jnp.int32, (BLK, 3 * BLK), 0)
    kk = lax.broadcasted_iota(jnp.int32, (BLK, 3 * BLK), 1)
    kl = kk % BLK
    part = kk // BLK
    meta = (part == 0) & (kl >= PAD) & ((i >= 1) | (kl <= ql))
    prev = (part == 1) & (i >= 2) & (kl > ql)
    cur = (part == 2) & (i >= 1) & (kl <= ql)
    return meta | prev | cur


def _kv_specs():
    return [pl.BlockSpec((BLK, D_KV), lambda i: (0, 0)),
            pl.BlockSpec((BLK, D_KV), lambda i: (jnp.maximum(i - 1, 0), 0)),
            pl.BlockSpec((BLK, D_KV), lambda i: (i, 0))]


def _pair_heads(ref, gp):
    def two(j):
        h0 = 2 * GRP * gp + j
        return jnp.concatenate([ref[:, HD * h0:HD * (h0 + 1)], ref[:, HD * (h0 + GRP):HD * (h0 + GRP + 1)]], axis=1)

    return jnp.concatenate([two(j) for j in range(GRP)], axis=0)


def _pair_kv(refs, gp):
    both = jnp.concatenate([r[:, 2 * HD * gp:2 * HD * (gp + 1)] for r in refs], axis=0)
    low_lanes = lax.broadcasted_iota(jnp.int32, both.shape, 1) < HD
    zero = jnp.zeros_like(both)
    return jnp.concatenate([jnp.where(low_lanes, both, zero), jnp.where(low_lanes, zero, both)], axis=0)


def _attn_fwd(q, k, v, sinks, z, *, hook):
    nk = 3 * BLK
    n_gate = D // CW

    def body(q_ref, k0_ref, kp_ref, kc_ref, v0_ref, vp_ref, vc_ref, sink_ref, *rest):
        gate_refs, (o_ref, lse_ref, y_ref, yt_ref, s_scr, p_scr) = rest[:n_gate], rest[n_gate:]
        mask = _attn_mask(pl.program_id(0))
        for gp in range(N_KV // 2):
            s_scr[...] = lax.dot_general(_pair_heads(q_ref, gp), _pair_kv((k0_ref, kp_ref, kc_ref), gp), _NT,
                                         preferred_element_type=F32)
            for a in range(2):
                for j in range(GRP):
                    h = GRP * (2 * gp + a) + j
                    rows, cols = slice(BLK * j, BLK * (j + 1)), slice(nk * a, nk * (a + 1))
                    sink = sink_ref[h]
                    s = jnp.where(mask, s_scr[rows, cols] * (HD ** -0.5), NEG_INF)
                    mx = jnp.maximum(jnp.max(s, -1, keepdims=True), sink)
                    p = jnp.exp(s - mx)
                    den = jnp.sum(p, -1, keepdims=True) + jnp.exp(sink - mx)
                    p_scr[rows, cols] = (p * (1.0 / den)).astype(_MXU)
                    lse_ref[:, h:h + 1] = mx + jnp.log(den)
            o2 = jnp.dot(p_scr[...], _pair_kv((v0_ref, vp_ref, vc_ref), gp), preferred_element_type=F32)
            for a in range(2):
                for j in range(GRP):
                    h = GRP * (2 * gp + a) + j
                    o_ref[:, HD * h:HD * (h + 1)] = o2[BLK * j:BLK * (j + 1), HD * a:HD * (a + 1)]
        for t, g_ref in enumerate(gate_refs):
            cs = slice(CW * t, CW * (t + 1))
            zz = g_ref[...]
            y = o_ref[:, cs] * (zz * _sigmoid(zz))
            y_ref[:, cs] = y.astype(_MXU)
            yt_ref[cs, :] = y.T.astype(_MXU)

    row = pl.BlockSpec((BLK, D), lambda i: (i, 0))
    gates = [pl.BlockSpec((BLK, CW), lambda i, t=t: (i, OFF_GA // CW + t)) for t in range(n_gate)]
    return _pcall(body, name="attn_fwd", grid=(NBLK,),
                  in_specs=[row] + _kv_specs() + _kv_specs() + [pl.BlockSpec(memory_space=pltpu.SMEM)] + gates,
                  out_specs=[row, pl.BlockSpec((BLK, N_Q), lambda i: (i, 0)), row, pl.BlockSpec((D, BLK), lambda i: (0, i))],
                  out_shape=[_sds((TP, D), F32), _sds((TP, N_Q), F32), _sds((TP, D), _MXU), _sds((D, TP), _MXU)],
                  operands=[q, k, k, k, v, v, v, sinks] + [z] * n_gate,
                  scratch=[pltpu.VMEM((GRP * BLK, 2 * nk), F32), pltpu.VMEM((GRP * BLK, 2 * nk), _MXU)],
                  vmem=40, sem=("parallel",), hook=hook)


def _attn_bwd(q, k, v, sinks, do, o, lse, cos_t, sin_t, dz, *, hook):
    def body(q_ref, k0_ref, kp_ref, kc_ref, v0_ref, vp_ref, vc_ref, sink_ref, do_ref, o_ref, lse_ref, c_ref, s_ref, dz_in,
             dq_ref, dk_ref, dv_ref, dsink_ref, dqrot_ref, s_scr, dp_scr, p_scr, ds_scr):
        i = pl.program_id(0)

        @pl.when(i == 0)
        def _():
            dk_ref[...] = jnp.zeros_like(dk_ref)
            dv_ref[...] = jnp.zeros_like(dv_ref)
            dsink_ref[...] = jnp.zeros_like(dsink_ref)

        mask = _attn_mask(i)
        row_starts = (0, pl.multiple_of(jnp.maximum(i - 1, 0) * BLK, BLK), pl.multiple_of(i * BLK, BLK))
        scale = HD ** -0.5
        nk = 3 * BLK
        for gp in range(N_KV // 2):
            q2 = _pair_heads(q_ref, gp)
            dom2 = _pair_heads(do_ref, gp).astype(_MXU)
            k2 = _pair_kv((k0_ref, kp_ref, kc_ref), gp)
            s_scr[...] = lax.dot_general(q2, k2, _NT, preferred_element_type=F32)
            dp_scr[...] = lax.dot_general(dom2, _pair_kv((v0_ref, vp_ref, vc_ref), gp), _NT, preferred_element_type=F32)
            for a in range(2):
                for j in range(GRP):
                    h = GRP * (2 * gp + a) + j
                    hs = slice(HD * h, HD * (h + 1))
                    rows = slice(BLK * j, BLK * (j + 1))
                    cols = slice(nk * a, nk * (a + 1))
                    lse_h = lse_ref[:, h:h + 1]
                    delta = jnp.sum(do_ref[:, hs] * o_ref[:, hs], axis=-1, keepdims=True)
                    dsink_ref[0:1, h:h + 1] += jnp.sum(-jnp.exp(sink_ref[h] - lse_h) * delta, axis=0, keepdims=True)
                    p = jnp.exp(jnp.where(mask, s_scr[rows, cols] * scale, NEG_INF) - lse_h)
                    p_scr[rows, cols] = p.astype(_MXU)
                    ds_scr[rows, cols] = (p * (dp_scr[rows, cols] - delta) * scale).astype(_MXU)
            ds = ds_scr[...]
            dq2 = jnp.dot(ds, k2, preferred_element_type=F32)
            dk2 = lax.dot_general(ds, q2, _TN, preferred_element_type=F32)
            dv2 = lax.dot_general(p_scr[...], dom2, _TN, preferred_element_type=F32)
            for a in range(2):
                g = 2 * gp + a
                gs = slice(HD * g, HD * (g + 1))
                for part in range(3):
                    rows = pl.ds(row_starts[part], BLK)
                    blk = slice(nk * a + BLK * part, nk * a + BLK * (part + 1))
                    dk_ref[rows, gs] += dk2[blk, HD * a:HD * (a + 1)]
                    dv_ref[rows, gs] += dv2[blk, HD * a:HD * (a + 1)]
                for j in range(GRP):
                    h = GRP * g + j
                    dqrot_ref[:, HD * h:HD * (h + 1)] = dq2[BLK * j:BLK * (j + 1), HD * a:HD * (a + 1)]
        dq_ref[...] = _rope(dqrot_ref[...], c_ref[...], s_ref[...], -1.0).astype(_MXU)

    row = pl.BlockSpec((BLK, D), lambda i: (i, 0))
    tab = pl.BlockSpec((BLK, 128), lambda i: (i, 0))
    full_kv = pl.BlockSpec((TP, D_KV), lambda i: (0, 0))
    return _pcall(
        body, name="attn_bwd", grid=(NBLK,),
        in_specs=[row] + _kv_specs() + _kv_specs() + [pl.BlockSpec(memory_space=pltpu.SMEM), row, row,
                  pl.BlockSpec((BLK, N_Q), lambda i: (i, 0)), tab, tab, _ANY],
        out_specs=[pl.BlockSpec((BLK, D), lambda i: (i, OFF_Q // D)), full_kv, full_kv,
                   pl.BlockSpec((8, 128), lambda i: (0, 0))],
        out_shape=[_sds((TP, D_IN), _MXU), _sds((TP, D_KV), F32), _sds((TP, D_KV), F32), _sds((8, 128), F32)],
        operands=[q, k, k, k, v, v, v, sinks, do, o, lse, cos_t, sin_t, dz],
        scratch=[pltpu.VMEM((BLK, D), F32), pltpu.VMEM((GRP * BLK, 6 * BLK), F32), pltpu.VMEM((GRP * BLK, 6 * BLK), F32),
                 pltpu.VMEM((GRP * BLK, 6 * BLK), _MXU), pltpu.VMEM((GRP * BLK, 6 * BLK), _MXU)],
        vmem=48, sem=("arbitrary",), aliases={13: 0}, hook=hook)


def _cast_into_slot(w32, pos, *, name, tr):
    r, cc = w32.shape

    def body(pos_ref, w_ref, o_ref):
        o_ref[...] = w_ref[...].astype(BF16)

    return _pcall(body, name=name, grid=(r // tr,), in_specs=[pl.BlockSpec((tr, cc), lambda i, p: (i, 0))],
                  out_specs=[pl.BlockSpec((None, tr, cc), lambda i, p: (p[1], i, 0))],
                  out_shape=[_sds((N_SH, r, cc), BF16)], operands=[w32], vmem=32, sem=("parallel",), prefetch=(pos,))[0]


def _pair_add(g, got, pos, *, name, tr, g_has_both_halves, hook=None):
    n, h, cc = got.shape
    nt = h // tr

    def body(pos_ref, g_ref, r_ref, own_ref, s16_ref):
        s = g_ref[...] + r_ref[...].astype(F32)
        s16_ref[...] = s.astype(BF16)

        @pl.when(pl.program_id(1) == pos_ref[1])
        def _():
            own_ref[...] = s

    g_index = (lambda i, s, p: (s, p[0] * nt + i, 0)) if g_has_both_halves else (lambda i, s, p: (s, i, 0))
    return _pcall(body, name=name, grid=(nt, n),
                  in_specs=[pl.BlockSpec((None, tr, cc), g_index), pl.BlockSpec((None, tr, cc), lambda i, s, p: (s, i, 0))],
                  out_specs=[pl.BlockSpec((tr, cc), lambda i, s, p: (i, 0)),
                             pl.BlockSpec((None, tr, cc), lambda i, s, p: (s, i, 0))],
                  out_shape=[_sds((h, cc), F32), _sds((n, h, cc), BF16)], operands=[g, got], vmem=40,
                  sem=("parallel", "arbitrary"), prefetch=(pos,), hook=hook)


def _relay_add(s16, raw, pos, *, name, tr):
    _, q, cc = raw.shape
    nt = q // tr

    def body(pos_ref, s_ref, r_ref, o_ref):
        o_ref[...] = (s_ref[...].astype(F32) + r_ref[...].astype(F32)).astype(BF16)

    blk = pl.BlockSpec((None, tr, cc), lambda k, i, p: (k, i, 0))
    return _pcall(body, name=name, grid=(2, nt),
                  in_specs=[pl.BlockSpec((None, tr, cc), lambda k, i, p: (p[3 + k], (1 - k) * nt + i, 0)), blk],
                  out_specs=[blk], out_shape=[_sds((2, q, cc), BF16)], operands=[s16, raw], vmem=40,
                  sem=("parallel", "parallel"), prefetch=(pos,))[0]


def _sum_chips(own, got, pos, *, name, tr):
    h, cc = own.shape
    n_got = got.shape[0]
    nt = h // tr

    def body(pos_ref, o_ref, r_ref, out_ref):
        acc = o_ref[...]
        for k in range(n_got):
            acc = acc + r_ref[k].astype(F32)
        out_ref[...] = acc

    return _pcall(body, name=name, grid=(nt,),
                  in_specs=[pl.BlockSpec((tr, cc), lambda i, p: (i, 0)),
                            pl.BlockSpec((n_got, tr, cc), lambda i, p: (0, i, 0))],
                  out_specs=[pl.BlockSpec((tr, cc), lambda i, p: (p[0] * nt + i, 0))],
                  out_shape=[_sds((2 * h, cc), F32)], operands=[own, got], vmem=40, sem=("parallel",), prefetch=(pos,))[0]


def _hand_over(arrays):
    def body(*refs):
        del refs

    n = len(arrays)
    return pl.pallas_call(body, name="hand_over", in_specs=[_ANY] * n, out_specs=[_ANY] * n,
                          out_shape=[_sds(a.shape, a.dtype) for a in arrays],
                          input_output_aliases={t: t for t in range(n)})(*arrays)


def _adamw(w, g, m, v, *, name, tr, g_row0=0):
    r, cc = w.shape
    g_blk0 = g_row0 // tr

    def body(*refs):
        _adamw_step(*refs)

    blk = pl.BlockSpec((tr, cc), lambda i: (i, 0))
    gblk = pl.BlockSpec((tr, cc), lambda i: (g_blk0 + i, 0))
    return _pcall(body, name=name, grid=(r // tr,), in_specs=[blk, gblk, blk, blk], out_specs=[blk] * 4,
                  out_shape=[_sds((r, cc), F32)] * 4, operands=[w, g, m, v], vmem=48, sem=("parallel",))


_SMALL_ROWS = 40
_B_IN_ROWS = 7


def _row_pad(v, rows):
    flat = v.reshape(-1)
    return jnp.pad(flat, (0, rows * D - flat.shape[0])).reshape(rows, D)


def _pack_ra(w):
    return w.transpose(1, 0, 2).reshape(64, D)


def _unpack_ra(p, like):
    return p.reshape(64, N_RB, RB).transpose(1, 0, 2).reshape(like.shape)


def _gate_full(g4):
    return g4.reshape(N_SH, 64, N_RB, RB).transpose(2, 0, 1, 3).reshape(N_RB, RB, RB)


def kernel(x, meta_tokens, ln_emb_g, ln_emb_b, w_in, b_in, conv_w, conv_b, w_ra, b_ra, w_ri, b_ri, lru_lambda, sinks, w_rnn_out, w_attn_out, w_o, b_o, ln_g, ln_b, loss_target, m_meta_tokens, m_ln_emb_g, m_ln_emb_b, m_w_in, m_b_in, m_conv_w, m_conv_b, m_w_ra, m_b_ra, m_w_ri, m_b_ri, m_lru_lambda, m_sinks, m_w_rnn_out, m_w_attn_out, m_w_o, m_b_o, m_ln_g, m_ln_b, v_meta_tokens, v_ln_emb_g, v_ln_emb_b, v_w_in, v_b_in, v_conv_w, v_conv_b, v_w_ra, v_b_ra, v_w_ri, v_b_ri, v_lru_lambda, v_sinks, v_w_rnn_out, v_w_attn_out, v_w_o, v_b_o, v_ln_g, v_ln_b):
    xi, yi, ci = _my_pos()
    shard = 2 * xi + yi
    pos = jnp.stack([ci, shard, 1 - ci, shard ^ 2, shard ^ 1]).astype(jnp.int32)
    cos_t, sin_t = _rope_tables()
    zero_bias = jnp.zeros((1, D), F32)
    ln_emb_g2, ln_emb_b2 = ln_emb_g[None], ln_emb_b[None]

    small = jnp.concatenate([conv_w[0], meta_tokens, jnp.zeros((4, 512), F32)], axis=0)
    small4 = _gather_small(small)
    conv_w_full = small4[:, 0:4].transpose(1, 0, 2).reshape(CONV_W, D)
    meta_full = small4[:, 4:20].transpose(1, 0, 2).reshape(N_META, D)
    w_own = _cast_into_slot(w_in[0], pos, name="cast_w_in", tr=256)
    wa_own = _cast_into_slot(jnp.concatenate([w_attn_out[0], w_o[0], _pack_ra(w_ra[0]), _pack_ra(w_ri[0])], axis=0),
                             pos, name="cast_w_a", tr=288)
    wb_own = _cast_into_slot(w_rnn_out[0], pos, name="cast_w_b", tr=256)

    h32, h16, h16_t = _ln_emb_fwd(x[0], meta_full, ln_emb_g2, ln_emb_b2)
    order = jnp.stack([shard, shard ^ 2, shard ^ 1, shard ^ 3]).astype(jnp.int32)
    z, w_in4 = _mm_z_gather(h16, w_own, b_in, order)
    q, k, v = _rope_fwd(z, cos_t, sin_t)
    o, lse, yb_in, yb_in_t, wa4 = _attn_fwd(q, k, v, sinks[0], z, hook=_hook_gather(wa_own, 0.6))
    w_ra_full = _gate_full(wa4[:, 2 * SQ_ROWS:2 * SQ_ROWS + 64])
    w_ri_full = _gate_full(wa4[:, 2 * SQ_ROWS + 64:2 * SQ_ROWS + 128])
    a_dec, u_in = _rnn_gates_fwd(z, conv_w_full, conv_b, w_ra_full, w_ri_full, b_ra, b_ri, lru_lambda)
    hr, ya_in, wb4 = _scan_fwd(a_dec, u_in, z, hook=_hook_gather(wb_own, 0.6))
    sq_w = {0: (wb4, 0), 1: (wa4, 0), 2: (wa4, 1)}

    def sq_nn(a, kk, bias, name, out_cols, out_index, carried=None):
        wp, blk = sq_w[kk]
        return _mm_nn(a, wp, bias, name=name, grid=(2, D // CW), tm=HALF_TP, tn=CW, k=D, a_index=lambda i, j: (i, 0),
                      w_block=(N_SH, SQ_ROWS, CW), w_index=lambda i, j: (0, blk, j), out_cols=out_cols,
                      out_index=out_index, carried=carried)[0]

    def sq_nt(a, a_blk, kk, name, hook=None):
        wp, blk = sq_w[kk]
        return _mm_nt(a, wp, name=name, grid=(2, N_SH, 1), tm=HALF_TP, tn=SQ_ROWS, tk=D,
                      a_index=lambda i, j, q: (i, a_blk), w_block=(None, SQ_ROWS, D),
                      w_index=lambda i, j, q: (j, blk, 0), out_cols=D, hook=hook)

    y2 = sq_nn(ya_in, 0, zero_bias, "mm_ya", 2 * D, lambda i, j: (i, j))
    y2 = sq_nn(yb_in, 1, zero_bias, "mm_yb", 2 * D, lambda i, j: (i, D // CW + j), carried=y2)
    mixed, mixed_t = _merge_fwd(y2, z)
    out = sq_nn(mixed, 2, b_o, "mm_out", D, lambda i, j: (i, j))
    dr, dr16, sums_o = _ln_out_loss(h32, out, loss_target[0], ln_g, ln_b)

    def sq_tn(at, b, b_blk, kk, name, carried=None, a_transposed=True):
        return _mm_tn(at, b, name=name, grid=(N_SH, 1), tm=SQ_ROWS, tn=D,
                      a_index=(lambda i, j: (i, 0)) if a_transposed else (lambda i, j: (0, i)),
                      b_index=lambda i, j: (0, b_blk), out_shape=(N_SH, PACK_ROWS, D), out_block=(None, SQ_ROWS, D),
                      out_index=lambda i, j: (i, kk, 0), carried=carried, a_transposed=a_transposed)[0]

    gsq = sq_tn(mixed_t, dr16, 0, 2, "mm_dwo")
    dmix = sq_nt(dr16, 0, 2, "mm_dmix")[0]
    dy2, dz = _merge_bwd(dmix, y2, z)
    gsq = sq_tn(ya_in, dy2, 0, 0, "mm_dwrnn", carried=gsq, a_transposed=False)
    gsq = sq_tn(yb_in_t, dy2, 1, 1, "mm_dwattn", carried=gsq)
    dya_in = sq_nt(dy2, 0, 0, "mm_dyain")[0]
    dhr, dz = _mul_silu_bwd(dya_in, hr, z, OFF_GR, dz, name="gate_a_bwd")
    lam_s = _scan_bwd(a_dec, dhr)
    dz, gsq, sums_r = _rnn_gates_bwd(z, lam_s, hr, conv_w_full, conv_b, w_ra_full, w_ri_full, b_ra, b_ri, lru_lambda, dz, gsq)
    dyb_in, gsq, got_sq = sq_nt(dy2, 1, 1, "mm_dybin", hook=_hook_pair(gsq, True, (0, 1)))
    do, dz, gsq, got_sq = _mul_silu_bwd(dyb_in, o, z, OFF_GA, dz, name="gate_b_bwd",
                                        hook=_hook_pair(gsq, True, (2, 3), land=got_sq))
    own_sq, s16_sq = _pair_add(gsq, got_sq, pos, name="red_w_sq_add", tr=208, g_has_both_halves=True)
    dz, dk_rot, dv32, dsink, s16_sq, fin_sq, raw_sq = _attn_bwd(q, k, v, sinks[0], do, o, lse, cos_t, sin_t, dz,
                                                                hook=_hook_scatter_direct(s16_sq))
    dz = _rope_bwd_k(dk_rot, dv32, cos_t, sin_t, dz)
    comb_sq = _relay_add(s16_sq, raw_sq, pos, name="red_w_sq_relay", tr=208)

    def dwin(half_idx, name, hook, **kw):
        return _mm_tn(h16_t, dz, name=name, grid=(1, N_SH * PER_IN), tm=D // 2, tn=TN_IN,
                      a_index=lambda i, j, p: (p[half_idx], 0), b_index=lambda i, j, p: (0, j),
                      out_shape=(N_SH, D // 2, W_IN_COLS), out_block=(None, D // 2, TN_IN),
                      out_index=lambda i, j, p: (j // PER_IN, 0, j % PER_IN), prefetch=(pos,), hook=hook, **kw)

    gin_sib, g_b_in, comb_sq, fin_sq = dwin(2, "mm_dwin_sib", _hook_scatter_relay(comb_sq, fin_sq),
                                            out_dtype=BF16, colsum=True)
    red_sq = _sum_chips(own_sq, fin_sq, pos, name="red_w_sq_sum", tr=208)
    gin_own, gin_sib, got_in = dwin(0, "mm_dwin_own", _hook_pair(gin_sib, False))
    own_in, s16_in, g_sq = _pair_add(gin_own, got_in, pos, name="red_w_in_add", tr=128, g_has_both_halves=False,
                                     hook=_hook_halves(red_sq))
    dhz, s16_in, fin_in, raw_in = _mm_nt(dz, w_in4, name="mm_dhz", grid=(2, 2, N_SH), tm=HALF_TP, tn=D // 2,
                                         tk=W_IN_COLS, a_index=lambda i, j, q: (i, q),
                                         w_block=(None, D // 2, W_IN_COLS), w_index=lambda i, j, q: (q, j, 0), out_cols=D,
                                         hook=_hook_scatter_direct(s16_in))
    comb_in = _relay_add(s16_in, raw_in, pos, name="red_w_in_relay", tr=128)
    sq_params = [("w_rnn_out", w_rnn_out, m_w_rnn_out, v_w_rnn_out), ("w_attn_out", w_attn_out, m_w_attn_out, v_w_attn_out),
                 ("w_o", w_o, m_w_o, v_w_o)]
    g_x, g_meta_local, sums_e, *sq_res, comb_in, fin_in = _ln_emb_bwd(
        x[0], meta_full, ln_emb_g2, dr, dhz, g_sq, [(w_[0], m_[0], v_[0]) for _, w_, m_, v_ in sq_params],
        hook=_hook_scatter_relay(comb_in, fin_in))
    red_in = _sum_chips(own_in, fin_in, pos, name="red_w_in_sum", tr=128)

    spack = jnp.concatenate([
        sums_e[0:1], sums_e[1:2], _row_pad(g_b_in, _B_IN_ROWS), sums_r[0:4], sums_r[4:5], sums_r[5:6], sums_r[6:7],
        sums_r[7:8], _row_pad(dsink[0:1, 0:N_Q], 1), sums_o[2:3], sums_o[0:1], sums_o[1:2], g_meta_local, sums_o[3:4],
        jnp.zeros((_SMALL_ROWS - 38, D), F32)], axis=0)
    sred, g_in = _allreduce_small(spack, red_in)

    big = {"w_in": [t.reshape(w_in.shape) for t in
                    _adamw(w_in[0], g_in, m_w_in[0], v_w_in[0], name="adamw_w_in", tr=128)]}
    g_x, *sq_res = _hand_over([g_x] + sq_res)
    for kk, (n, w_, _, _) in enumerate(sq_params):
        big[n] = [t.reshape(w_.shape) for t in sq_res[4 * kk:4 * kk + 4]]
    for kk, (n, w_, m_, v_) in enumerate([("w_ra", w_ra, m_w_ra, v_w_ra), ("w_ri", w_ri, m_w_ri, v_w_ri)]):
        big[n] = [_unpack_ra(t, w_) for t in
                  _adamw(_pack_ra(w_[0]), g_sq, _pack_ra(m_[0]), _pack_ra(v_[0]), name="adamw_" + n, tr=64,
                         g_row0=3 * SQ_ROWS + 64 * kk)]

    loss = sred[37, 0]
    col0 = shard * 512
    g_conv_w = lax.dynamic_slice(sred[9:13], (0, col0), (CONV_W, 512))
    g_meta = lax.dynamic_slice(sred[21:37], (0, col0), (N_META, 512))
    small_g = {"ln_emb_g": sred[0:1], "ln_emb_b": sred[1:2], "b_in": sred[2:9], "conv_w": g_conv_w.reshape(1, D),
               "conv_b": sred[13:14], "b_ra": sred[14:15], "b_ri": sred[15:16], "lru_lambda": sred[16:17],
               "sinks": sred[17:18], "b_o": sred[18:19], "ln_g": sred[19:20], "ln_b": sred[20:21],
               "meta_tokens": g_meta.reshape(4, D)}
    small_names = list(small_g)

    def small_pack(vals):
        rows = []
        for n in small_names:
            a = vals[n]
            if n == "b_in":
                a = _row_pad(a, _B_IN_ROWS)
            elif n == "sinks":
                a = _row_pad(a, 1)
            else:
                a = a.reshape(-1, D)
            rows.append(a)
        return jnp.concatenate(rows + [jnp.zeros((24 - 22, D), F32)], axis=0)

    w_small = dict(ln_emb_g=ln_emb_g, ln_emb_b=ln_emb_b, b_in=b_in, conv_w=conv_w, conv_b=conv_b, b_ra=b_ra, b_ri=b_ri,
                   lru_lambda=lru_lambda, sinks=sinks, b_o=b_o, ln_g=ln_g, ln_b=ln_b, meta_tokens=meta_tokens)
    m_small = dict(ln_emb_g=m_ln_emb_g, ln_emb_b=m_ln_emb_b, b_in=m_b_in, conv_w=m_conv_w, conv_b=m_conv_b, b_ra=m_b_ra,
                   b_ri=m_b_ri, lru_lambda=m_lru_lambda, sinks=m_sinks, b_o=m_b_o, ln_g=m_ln_g, ln_b=m_ln_b,
                   meta_tokens=m_meta_tokens)
    v_small = dict(ln_emb_g=v_ln_emb_g, ln_emb_b=v_ln_emb_b, b_in=v_b_in, conv_w=v_conv_w, conv_b=v_conv_b, b_ra=v_b_ra,
                   b_ri=v_b_ri, lru_lambda=v_lru_lambda, sinks=v_sinks, b_o=v_b_o, ln_g=v_ln_g, ln_b=v_ln_b,
                   meta_tokens=v_meta_tokens)
    g_small_pack = jnp.concatenate([small_g[n] for n in small_names] + [jnp.zeros((2, D), F32)], axis=0)
    small_res = _adamw(small_pack(w_small), g_small_pack, small_pack(m_small), small_pack(v_small),
                       name="adamw_small", tr=24)

    small_rows = {}
    r0 = 0
    for n in small_names:
        nrows = small_g[n].shape[0]
        small_rows[n] = (r0, nrows)
        r0 += nrows

    def small_out(packed, n, like):
        a, nrows = small_rows[n]
        flat = packed[a:a + nrows].reshape(-1)
        return flat[:like.size].reshape(like.shape)

    weights = dict(meta_tokens=meta_tokens, ln_emb_g=ln_emb_g, ln_emb_b=ln_emb_b, w_in=w_in, b_in=b_in, conv_w=conv_w,
                   conv_b=conv_b, w_ra=w_ra, b_ra=b_ra, w_ri=w_ri, b_ri=b_ri, lru_lambda=lru_lambda, sinks=sinks,
                   w_rnn_out=w_rnn_out, w_attn_out=w_attn_out, w_o=w_o, b_o=b_o, ln_g=ln_g, ln_b=ln_b)

    def outputs(which):
        return [big[n][which] if n in big else small_out(small_res[which], n, like) for n, like in weights.items()]

    return (loss, g_x[None], *outputs(0), *outputs(1), *outputs(2), *outputs(3))
```

```python
import jax
import jax.numpy as jnp
from jax import lax
from jax.experimental import pallas as pl
from jax.experimental.pallas import tpu as pltpu

F32 = jnp.float32
BF16 = jnp.bfloat16
_MXU = jnp.bfloat16

D = 2048
SEQ = 2048
N_META = 16
BLK = 128
PAD = BLK - N_META
TP = PAD + N_META + SEQ
NBLK = TP // BLK
HALF_TP = TP // 2
N_RB = 8
RB = 256
CONV_W = 4
LRU_C = 8.0
HD = 64
N_Q = 32
N_KV = 4
GRP = 8
D_KV = 256
NEG_INF = -1e30
LN_EPS = 1e-5
ALPHA = 2.0 ** 0.25
ROPE_THETA = 10000.0
OFF_GR, OFF_Q, OFF_K, OFF_V, OFF_GA, OFF_G = 2048, 4096, 6144, 6400, 6656, 8704
D_IN = 12800
N_SH = 4
W_IN_COLS = D_IN // N_SH
TN_IN = 640
PER_IN = W_IN_COLS // TN_IN
CW = 512
RT = TP // 4
SQ_ROWS = 512
PACK_ROWS = 3 * SQ_ROWS + 128

ADAM_LR = 0.001
ADAM_B1 = 0.9
ADAM_B2 = 0.999
ADAM_EPS = 1e-08
ADAM_WD = 0.01
ADAM_STEP = 10

MESH = pl.DeviceIdType.MESH
_MIB = 1024 * 1024
_ANY = pl.BlockSpec(memory_space=pl.ANY)
_NT = (((1,), (1,)), ((), ()))
_TN = (((0,), (0,)), ((), ()))


def _sds(shape, dtype):
    return jax.ShapeDtypeStruct(shape, dtype)


def _sigmoid(x):
    return 1.0 / (1.0 + jnp.exp(-x))


def _my_pos():
    return lax.axis_index("x"), lax.axis_index("y"), lax.axis_index("c")


def _other_chips(x, y):
    return [(1 - x, y), (x, 1 - y), (1 - x, 1 - y)]


def _remote(src, dst, send_sems, recv_sems, k, dev):
    return pltpu.make_async_remote_copy(src_ref=src, dst_ref=dst, send_sem=send_sems.at[k], recv_sem=recv_sems.at[k],
                                        device_id=dev, device_id_type=MESH)


class _Hook:
    def __init__(self, carried, landing, n_sems, start, finish, mid=None, mid_frac=0.5):
        self.carried, self.landing, self.n_sems, self.start, self.finish = list(carried), list(landing), n_sems, start, finish
        self.mid, self.mid_frac = mid, mid_frac


def _hook_gather(buf, mid_frac):
    half = buf.shape[1] // 2
    quarter = half // 2

    def geom(o, ss, rs):
        x, y, c = _my_pos()
        xn, yn, dg = _other_chips(x, y)
        slot = lambda p: 2 * p[0] + p[1]
        mine_rows = pl.ds(pl.multiple_of(c * half, 16), half)
        sib_rows = pl.ds(pl.multiple_of((1 - c) * half, 16), half)
        q_rows = lambda r: pl.ds(pl.multiple_of(c * half + r * quarter, 16), quarter)

        def cp(k, s, rows, dev):
            part = o.at[s, rows]
            return _remote(part, part, ss, rs, k, dev)

        return dict(
            direct=lambda k, s: cp(k, s, mine_rows, ((xn, yn)[k][0], (xn, yn)[k][1], c)),
            relay=lambda r, s: cp(2 + r, s, q_rows(r), ((yn, xn)[r][0], (yn, xn)[r][1], c)),
            sibling=lambda k, s, mine: cp(4 + k, s, mine_rows if mine else sib_rows, (x, y, 1 - c)),
            me=slot((x, y)), slots=(slot(xn), slot(yn), slot(dg)))

    def start(car, land, ss, rs):
        g = geom(car[0], ss, rs)
        g["direct"](0, g["me"]).start()
        g["direct"](1, g["me"]).start()

    def mid(car, land, ss, rs):
        g = geom(car[0], ss, rs)
        for k in range(2):
            g["direct"](k, g["slots"][k]).wait_recv()
            g["relay"](k, g["slots"][k]).start()
            g["sibling"](k, g["slots"][k], True).start()

    def finish(car, land, ss, rs):
        g = geom(car[0], ss, rs)
        dslot = g["slots"][2]
        g["relay"](0, dslot).wait_recv()
        g["relay"](1, dslot).wait_recv()
        g["sibling"](2, dslot, True).start()
        for k in range(3):
            g["sibling"](k, g["slots"][k], False).wait_recv()
        for k in range(2):
            g["direct"](k, g["me"]).wait_send()
            g["relay"](k, g["slots"][k]).wait_send()
        for k in range(3):
            g["sibling"](k, g["slots"][k], True).wait_send()

    return _Hook([buf], [], 7, start, finish, mid=mid, mid_frac=mid_frac)


def _hook_pair(g, half_rows, slots=(0, 1, 2, 3), land=None):
    n, r, cc = g.shape
    h = r // 2 if half_rows else r

    def plan(car, landing, ss, rs):
        x, y, c = _my_pos()
        dst = landing[0] if land is None else car[1]
        cps = []
        for k, s in enumerate(slots):
            src = car[0].at[s, pl.ds(pl.multiple_of((1 - c) * h, 8), h)] if half_rows else car[0].at[s]
            cps.append(_remote(src, dst.at[s], ss, rs, k, (x, y, 1 - c)))
        return cps

    def start(car, land, ss, rs):
        for cp in plan(car, land, ss, rs):
            cp.start()

    def finish(car, land, ss, rs):
        cps = plan(car, land, ss, rs)
        for cp in cps:
            cp.wait_recv()
        for cp in cps:
            cp.wait_send()

    if land is None:
        return _Hook([g], [_sds((n, h, cc), g.dtype)], len(slots), start, finish)
    return _Hook([g, land], [], len(slots), start, finish)


def _hook_scatter_direct(s16):
    _, h, cc = s16.shape
    q = h // 2

    def plan(car, land, ss, rs):
        x, y, c = _my_pos()
        xn, yn, dg = _other_chips(x, y)
        s, (final, raw) = car[0], land
        slot = lambda p: 2 * p[0] + p[1]
        q0, q1 = pl.ds(0, q), pl.ds(q, q)
        return [_remote(s.at[slot(xn), q0], final.at[0, q0], ss, rs, 0, (xn[0], xn[1], c)),
                _remote(s.at[slot(yn), q1], final.at[1, q1], ss, rs, 1, (yn[0], yn[1], c)),
                _remote(s.at[slot(dg), q0], raw.at[1], ss, rs, 2, (xn[0], xn[1], c)),
                _remote(s.at[slot(dg), q1], raw.at[0], ss, rs, 3, (yn[0], yn[1], c))]

    def start(car, land, ss, rs):
        for cp in plan(car, land, ss, rs):
            cp.start()

    def finish(car, land, ss, rs):
        cps = plan(car, land, ss, rs)
        for cp in cps:
            cp.wait_recv()
        for cp in cps:
            cp.wait_send()

    return _Hook([s16], [_sds((2, h, cc), s16.dtype), _sds((2, q, cc), s16.dtype)], 4, start, finish)


def _hook_scatter_relay(comb, final):
    _, q, _ = comb.shape

    def plan(car, ss, rs):
        x, y, c = _my_pos()
        xn, yn, _ = _other_chips(x, y)
        cb, final_ref = car
        return [_remote(cb.at[0], final_ref.at[0, pl.ds(q, q)], ss, rs, 0, (xn[0], xn[1], c)),
                _remote(cb.at[1], final_ref.at[1, pl.ds(0, q)], ss, rs, 1, (yn[0], yn[1], c))]

    def start(car, land, ss, rs):
        for cp in plan(car, ss, rs):
            cp.start()

    def finish(car, land, ss, rs):
        cps = plan(car, ss, rs)
        for cp in cps:
            cp.wait_recv()
        for cp in cps:
            cp.wait_send()

    return _Hook([comb, final], [], 2, start, finish)


def _hook_halves(full):
    h = full.shape[0] // 2

    def half_copy(car, ss, rs, which):
        x, y, c = _my_pos()
        rows = car[0].at[pl.ds(pl.multiple_of((c + which - 2 * c * which) * h, 8), h)]
        return _remote(rows, rows, ss, rs, 0, (x, y, 1 - c))

    def start(car, land, ss, rs):
        half_copy(car, ss, rs, 0).start()

    def finish(car, land, ss, rs):
        half_copy(car, ss, rs, 1).wait_recv()
        half_copy(car, ss, rs, 0).wait_send()

    return _Hook([full], [], 1, start, finish)


def _pcall(body, *, name, grid, in_specs, out_specs, out_shape, operands, scratch=(), vmem=48, sem=None,
           prefetch=(), aliases=None, hook=None):
    n_pre, n_in, n_out, n_scr = len(prefetch), len(in_specs), len(out_specs), len(scratch)
    in_specs, out_specs, out_shape, scratch = list(in_specs), list(out_specs), list(out_shape), list(scratch)
    io_alias = {n_pre + a: b for a, b in (aliases or {}).items()}
    operands = list(operands)
    kernel_body = body
    if hook is not None:
        n_car, n_land = len(hook.carried), len(hook.landing)
        for t, arr in enumerate(hook.carried):
            io_alias[n_pre + n_in + t] = n_out + t
        in_specs += [_ANY] * n_car
        out_specs += [_ANY] * (n_car + n_land)
        out_shape += [_sds(a.shape, a.dtype) for a in hook.carried] + hook.landing
        scratch += [pltpu.SemaphoreType.DMA((hook.n_sems,)), pltpu.SemaphoreType.DMA((hook.n_sems,))]
        operands += hook.carried
        sem = ("arbitrary",) * len(grid)

        def kernel_body(*refs):
            pre, rest = refs[:n_pre], refs[n_pre:]
            ins = rest[:n_in]
            outs = rest[n_in + n_car:n_in + n_car + n_out]
            car = rest[n_in + n_car + n_out:n_in + 2 * n_car + n_out]
            land = rest[n_in + 2 * n_car + n_out:n_in + 2 * n_car + n_out + n_land]
            scr = rest[n_in + 2 * n_car + n_out + n_land:]
            send_sems, recv_sems = scr[n_scr], scr[n_scr + 1]
            first = pl.program_id(0) == 0
            last = pl.program_id(0) == grid[0] - 1
            for d in range(1, len(grid)):
                first = first & (pl.program_id(d) == 0)
                last = last & (pl.program_id(d) == grid[d] - 1)

            @pl.when(first)
            def _():
                hook.start(car, land, send_sems, recv_sems)

            if hook.mid is not None:
                step = pl.program_id(0)
                total = grid[0]
                for d in range(1, len(grid)):
                    step = step * grid[d] + pl.program_id(d)
                    total *= grid[d]

                @pl.when(step == int(total * hook.mid_frac))
                def _():
                    hook.mid(car, land, send_sems, recv_sems)

            body(*pre, *ins, *outs, *scr[:n_scr])

            @pl.when(last)
            def _():
                hook.finish(car, land, send_sems, recv_sems)

    params = pltpu.CompilerParams(vmem_limit_bytes=vmem * _MIB, dimension_semantics=sem,
                                  has_side_effects=hook is not None)
    if n_pre:
        call = pl.pallas_call(
            kernel_body, name=name, out_shape=out_shape, input_output_aliases=io_alias, compiler_params=params,
            grid_spec=pltpu.PrefetchScalarGridSpec(num_scalar_prefetch=n_pre, grid=grid, in_specs=in_specs,
                                                   out_specs=out_specs, scratch_shapes=scratch))
    else:
        call = pl.pallas_call(kernel_body, name=name, grid=grid, in_specs=in_specs, out_specs=out_specs,
                              out_shape=out_shape, scratch_shapes=scratch, input_output_aliases=io_alias,
                              compiler_params=params)
    return call(*prefetch, *operands)


def _gather_small(blk):
    r, cc = blk.shape

    def body(x_ref, o_ref, send_sems, recv_sems):
        x, y, c = _my_pos()
        me = 2 * x + y
        o_ref[me] = x_ref[...]
        sends = [_remote(x_ref, o_ref.at[me], send_sems, recv_sems, k, (px, py, c))
                 for k, (px, py) in enumerate(_other_chips(x, y))]
        for cp in sends:
            cp.start()
        for k, (px, py) in enumerate(_other_chips(x, y)):
            _remote(x_ref, o_ref.at[2 * px + py], send_sems, recv_sems, k, (px, py, c)).wait_recv()
        for cp in sends:
            cp.wait_send()

    vm = pl.BlockSpec(memory_space=pltpu.VMEM)
    return pl.pallas_call(
        body, name="gather_small", in_specs=[vm], out_specs=vm, out_shape=_sds((N_SH, r, cc), blk.dtype),
        scratch_shapes=[pltpu.SemaphoreType.DMA((3,)), pltpu.SemaphoreType.DMA((3,))],
        compiler_params=pltpu.CompilerParams(has_side_effects=True),
    )(blk)


N_DEV = 8


def _allreduce_small(pack, halves):
    r, cc = pack.shape
    ride = _hook_halves(halves)

    def body(x_ref, h_in_ref, o_ref, h_ref, buf_ref, send_sems, recv_sems, h_ss, h_rs):
        del h_in_ref
        ride.start([h_ref], [], h_ss, h_rs)
        x, y, c = _my_pos()
        me = 4 * x + 2 * y + c
        buf_ref[me] = x_ref[...]
        cps = []
        for k in range(1, N_DEV):
            peer = (x ^ ((k >> 2) & 1), y ^ ((k >> 1) & 1), c ^ (k & 1))
            cps.append(_remote(x_ref, buf_ref.at[me], send_sems, recv_sems, k - 1, peer))
        for cp in cps:
            cp.start()
        for k in range(1, N_DEV):
            peer = (x ^ ((k >> 2) & 1), y ^ ((k >> 1) & 1), c ^ (k & 1))
            src = 4 * peer[0] + 2 * peer[1] + peer[2]
            _remote(x_ref, buf_ref.at[src], send_sems, recv_sems, k - 1, peer).wait_recv()
        acc = buf_ref[0]
        for d in range(1, N_DEV):
            acc = acc + buf_ref[d]
        o_ref[...] = acc
        for cp in cps:
            cp.wait_send()
        ride.finish([h_ref], [], h_ss, h_rs)

    vm = pl.BlockSpec(memory_space=pltpu.VMEM)
    return pl.pallas_call(
        body, name="allreduce_small", in_specs=[vm, _ANY], out_specs=[vm, _ANY],
        out_shape=[_sds((r, cc), F32), _sds(halves.shape, halves.dtype)], input_output_aliases={1: 1},
        scratch_shapes=[pltpu.VMEM((N_DEV, r, cc), F32), pltpu.SemaphoreType.DMA((N_DEV - 1,)),
                        pltpu.SemaphoreType.DMA((N_DEV - 1,)), pltpu.SemaphoreType.DMA((1,)), pltpu.SemaphoreType.DMA((1,))],
        compiler_params=pltpu.CompilerParams(has_side_effects=True),
    )(pack, halves)


def _mm_nn(a, w, bias, *, name, grid, tm, tn, k, a_index, w_block, w_index, out_cols, out_index, carried=None, hook=None):
    m = a.shape[0]

    def body(a_ref, w_ref, b_ref, *rest):
        o_ref = rest[-1]
        wv = w_ref[...]
        acc = jnp.dot(a_ref[...].astype(_MXU), wv.reshape(k, tn).astype(_MXU), preferred_element_type=F32)
        o_ref[...] = acc + b_ref[...]

    operands = [a, w, bias] + ([carried] if carried is not None else [])
    return _pcall(
        body, name=name, grid=grid,
        in_specs=[pl.BlockSpec((tm, k), a_index), pl.BlockSpec(w_block, w_index),
                  pl.BlockSpec((1, tn), lambda i, j: (0, j))] + ([_ANY] if carried is not None else []),
        out_specs=[pl.BlockSpec((tm, tn), out_index)], out_shape=[_sds((m, out_cols), F32)], operands=operands,
        vmem=56, sem=("parallel", "parallel"), aliases={3: 0} if carried is not None else None, hook=hook)


def _mm_nt(a, w, *, name, grid, tm, tn, tk, a_index, w_block, w_index, out_cols, hook=None):
    m = a.shape[0]
    nk = grid[2]

    def body(a_ref, w_ref, o_ref, acc_ref):
        part = lax.dot_general(a_ref[...].astype(_MXU), w_ref[...].astype(_MXU), _NT, preferred_element_type=F32)
        if nk == 1:
            o_ref[...] = part
        else:
            kidx = pl.program_id(2)

            @pl.when(kidx == 0)
            def _():
                acc_ref[...] = part

            @pl.when(kidx > 0)
            def _():
                acc_ref[...] += part

            @pl.when(kidx == nk - 1)
            def _():
                o_ref[...] = acc_ref[...]

    return _pcall(
        body, name=name, grid=grid, in_specs=[pl.BlockSpec((tm, tk), a_index), pl.BlockSpec(w_block, w_index)],
        out_specs=[pl.BlockSpec((tm, tn), lambda i, j, q: (i, j))], out_shape=[_sds((m, out_cols), F32)],
        operands=[a, w], scratch=[pltpu.VMEM((tm, tn) if nk > 1 else (8, 128), F32)], vmem=60,
        sem=("parallel", "parallel", "arbitrary"), hook=hook)


def _mm_tn(at, b, *, name, grid, tm, tn, a_index, b_index, out_shape, out_block, out_index, carried=None,
           prefetch=(), hook=None, out_dtype=F32, colsum=False, a_transposed=True):
    t = at.shape[1] if a_transposed else at.shape[0]
    n_pre = len(prefetch)

    def body(*refs):
        a_ref, b_ref = refs[n_pre], refs[n_pre + 1]
        bv = b_ref[...]
        o_ref = refs[-2] if colsum else refs[-1]
        acc = lax.dot_general(a_ref[...].astype(_MXU), bv.astype(_MXU), (((1,), (0,)), ((), ())) if a_transposed else _TN,
                              preferred_element_type=F32)
        o_ref[...] = acc.astype(out_dtype)
        if colsum:
            refs[-1][...] = jnp.sum(bv.astype(F32), axis=0, keepdims=True)

    operands = [at, b] + ([carried] if carried is not None else [])
    out_specs = [pl.BlockSpec(out_block, out_index)]
    out_shapes = [_sds(out_shape, out_dtype)]
    if colsum:
        out_specs.append(pl.BlockSpec((1, tn), b_index))
        out_shapes.append(_sds((1, b.shape[1]), F32))
    return _pcall(
        body, name=name, grid=grid,
        in_specs=[pl.BlockSpec((tm, t) if a_transposed else (t, tm), a_index), pl.BlockSpec((t, tn), b_index)]
                 + ([_ANY] if carried is not None else []),
        out_specs=out_specs, out_shape=out_shapes, operands=operands,
        vmem=56, sem=("parallel", "parallel"), aliases={2: 0} if carried is not None else None, prefetch=prefetch, hook=hook)


def _remote_tile(n):
    near = 2 * PER_IN
    if isinstance(n, int):
        return (n % 2, n // 2) if n < near else (2, n - near)
    return jnp.where(n < near, n % 2, 2), jnp.where(n < near, n // 2, n - near)


def _mm_z_gather(h16, w_own, b_in, order):
    n_tiles = N_SH * PER_IN
    n_remote = 3 * PER_IN
    half = D // 2

    def body(order_ref, a_ref, b_ref, w_in_ref, z_ref, w_ref, wbuf, tile_sems, ss, rs):
        del w_in_ref
        j = pl.program_id(0)
        x, y, c = _my_pos()
        me = 2 * x + y
        chips = _other_chips(x, y)
        mine_rows = pl.ds(pl.multiple_of(c * half, 16), half)
        sib_rows = pl.ds(pl.multiple_of((1 - c) * half, 16), half)

        slots = [2 * px + py for px, py in chips]
        cols = lambda t: pl.ds(t * TN_IN, TN_IN)
        q_rows = lambda r: pl.ds(pl.multiple_of(c * half + r * (half // 2), 16), half // 2)

        def direct(rel, t, slot):
            px, py = chips[rel]
            part = w_ref.at[slot, mine_rows, cols(t)]
            return _remote(part, part, ss, rs, 2 * t + rel, (px, py, c))

        def relay(r, t, slot):
            px, py = chips[1 - r]
            part = w_ref.at[slot, q_rows(r), cols(t)]
            return _remote(part, part, ss, rs, 2 * PER_IN + 2 * t + r, (px, py, c))

        def d2d(n, rows):
            rel, t = _remote_tile(n)
            part = w_ref.at[slots[rel], rows, cols(t)]
            return _remote(part, part, ss, rs, 4 * PER_IN + n, (x, y, 1 - c))

        def tile_copy(step):
            rel, t = _remote_tile(jnp.maximum(step - PER_IN, 0))
            slot = jnp.where(step < PER_IN, me, me ^ jnp.where(rel == 0, 2, jnp.where(rel == 1, 1, 3)))
            col = pl.multiple_of(jnp.where(step < PER_IN, step, t) * TN_IN, 128)
            return pltpu.make_async_copy(w_ref.at[slot, :, pl.ds(col, TN_IN)], wbuf.at[step % 2], tile_sems.at[step % 2])

        @pl.when(j == 0)
        def _():
            for t in range(PER_IN):
                direct(0, t, me).start()
                direct(1, t, me).start()
            tile_copy(0).start()

        for n in range(n_remote):
            rel, t = _remote_tile(n)

            @pl.when(j == n + PER_IN - 3)
            def _():
                if rel < 2:
                    direct(rel, t, slots[rel]).wait_recv()
                    relay(rel, t, slots[rel]).start()
                else:
                    relay(0, t, slots[2]).wait_recv()
                    relay(1, t, slots[2]).wait_recv()
                d2d(n, mine_rows).start()

            @pl.when(j == n + PER_IN - 2)
            def _():
                d2d(n, sib_rows).wait_recv()

        @pl.when(j + 1 < n_tiles)
        def _():
            tile_copy(j + 1).start()

        tile_copy(j).wait()
        z_ref[...] = jnp.dot(a_ref[...], wbuf[j % 2], preferred_element_type=F32) + b_ref[...]

        @pl.when(j == n_tiles - 1)
        def _():
            for t in range(PER_IN):
                for r in range(2):
                    direct(r, t, me).wait_send()
                    relay(r, t, slots[r]).wait_send()
            for n in range(n_remote):
                d2d(n, mine_rows).wait_send()

    def col_tile(j, o):
        rel, t = _remote_tile(jnp.maximum(j - PER_IN, 0))
        return 0, jnp.where(j < PER_IN, o[0] * PER_IN + j, o[1 + rel] * PER_IN + t)

    return pl.pallas_call(
        body, name="mm_z_gather",
        grid_spec=pltpu.PrefetchScalarGridSpec(
            num_scalar_prefetch=1, grid=(n_tiles,),
            in_specs=[pl.BlockSpec((TP, D), lambda j, o: (0, 0)), pl.BlockSpec((1, TN_IN), col_tile), _ANY],
            out_specs=[pl.BlockSpec((TP, TN_IN), col_tile), _ANY],
            scratch_shapes=[pltpu.VMEM((2, D, TN_IN), BF16), pltpu.SemaphoreType.DMA((2,)),
                            pltpu.SemaphoreType.DMA((4 * PER_IN + n_remote,)),
                            pltpu.SemaphoreType.DMA((4 * PER_IN + n_remote,))]),
        out_shape=[_sds((TP, D_IN), F32), _sds(w_own.shape, w_own.dtype)],
        input_output_aliases={3: 1},
        compiler_params=pltpu.CompilerParams(vmem_limit_bytes=60 * _MIB, dimension_semantics=("arbitrary",),
                                             has_side_effects=True),
    )(order, h16, b_in, w_own)


def _padded_rows(i, x_ref, meta_ref):
    head = jnp.concatenate([jnp.zeros((PAD, D), F32), meta_ref[...]], axis=0)
    return jnp.where(i == 0, head, x_ref[...])


def _stream_specs():
    return [pl.BlockSpec((BLK, D), lambda i: (jnp.maximum(i - 1, 0), 0)), pl.BlockSpec((N_META, D), lambda i: (0, 0))]


def _ln_emb_fwd(x, meta, g, b):
    def body(x_ref, meta_ref, g_ref, b_ref, h32_ref, h16_ref, h16t_ref):
        x = _padded_rows(pl.program_id(0), x_ref, meta_ref)
        mu = jnp.mean(x, axis=-1, keepdims=True)
        xc = x - mu
        var = jnp.mean(xc * xc, axis=-1, keepdims=True)
        y = xc * lax.rsqrt(var + LN_EPS) * g_ref[...] + b_ref[...]
        h32_ref[...] = y
        h16_ref[...] = y.astype(_MXU)
        h16t_ref[...] = y.T.astype(_MXU)

    row = pl.BlockSpec((BLK, D), lambda i: (i, 0))
    vec = pl.BlockSpec((1, D), lambda i: (0, 0))
    return _pcall(body, name="ln_emb_fwd", grid=(NBLK,), in_specs=_stream_specs() + [vec, vec],
                  out_specs=[row, row, pl.BlockSpec((D, BLK), lambda i: (0, i))],
                  out_shape=[_sds((TP, D), F32), _sds((TP, D), _MXU), _sds((D, TP), _MXU)], operands=[x, meta, g, b],
                  vmem=32, sem=("parallel",))


def _adamw_step(w_ref, g_ref, m_ref, v_ref, go_ref, d_ref, mo_ref, vo_ref):
    gg = g_ref[...]
    go_ref[...] = gg
    m_new = ADAM_B1 * m_ref[...] + (1.0 - ADAM_B1) * gg
    v_new = ADAM_B2 * v_ref[...] + (1.0 - ADAM_B2) * (gg * gg)
    m_hat = m_new / (1.0 - ADAM_B1 ** ADAM_STEP)
    v_hat = v_new / (1.0 - ADAM_B2 ** ADAM_STEP)
    d_ref[...] = -ADAM_LR * (m_hat / (jnp.sqrt(v_hat) + ADAM_EPS) + ADAM_WD * w_ref[...])
    mo_ref[...] = m_new
    vo_ref[...] = v_new


def _ln_emb_bwd(x, meta, g, dr, dhz, g_sq, sq, *, hook):
    n_sq = len(sq)
    sq_steps = 16
    sq_rows = SQ_ROWS // sq_steps

    def body(x_ref, meta_ref, g_ref, dr_ref, dhz_ref, *rest):
        sq_in = rest[:4 * n_sq]
        dx_ref, dmeta_ref, acc_ref = rest[4 * n_sq:4 * n_sq + 3]
        sq_out = rest[4 * n_sq + 3:]
        i = pl.program_id(0)

        @pl.when(i < sq_steps)
        def _():
            for kk in range(n_sq):
                _adamw_step(*sq_in[4 * kk:4 * kk + 4], *sq_out[4 * kk:4 * kk + 4])


        @pl.when(i == 0)
        def _():
            acc_ref[...] = jnp.zeros_like(acc_ref)

        x = _padded_rows(i, x_ref, meta_ref)
        mu = jnp.mean(x, axis=-1, keepdims=True)
        xc = x - mu
        var = jnp.mean(xc * xc, axis=-1, keepdims=True)
        rstd = lax.rsqrt(var + LN_EPS)
        xhat = xc * rstd
        dh = ALPHA * dr_ref[...] + dhz_ref[...]
        acc_ref[0:1, :] += jnp.sum(dh * xhat, axis=0, keepdims=True)
        acc_ref[1:2, :] += jnp.sum(dh, axis=0, keepdims=True)
        dxh = dh * g_ref[...]
        m1 = jnp.mean(dxh, axis=-1, keepdims=True)
        m2 = jnp.mean(dxh * xhat, axis=-1, keepdims=True)
        dx = rstd * (dxh - m1 - xhat * m2)
        dx_ref[...] = dx

        @pl.when(i == 0)
        def _():
            dmeta_ref[...] = dx[PAD:BLK]

    row = pl.BlockSpec((BLK, D), lambda i: (i, 0))
    vec = pl.BlockSpec((1, D), lambda i: (0, 0))
    xs, ms = _stream_specs()
    sq_blk = pl.BlockSpec((sq_rows, D), lambda i: (jnp.minimum(i, sq_steps - 1), 0))
    sq_specs, sq_operands = [], []
    for kk, (w_, m_, v_) in enumerate(sq):
        g_blk = pl.BlockSpec((sq_rows, D), lambda i, kk=kk: (sq_steps * kk + jnp.minimum(i, sq_steps - 1), 0))
        sq_specs += [sq_blk, g_blk, sq_blk, sq_blk]
        sq_operands += [w_, g_sq, m_, v_]
    return _pcall(body, name="ln_emb_bwd", grid=(NBLK,), in_specs=[xs, ms, vec, row, row] + sq_specs,
                  out_specs=[xs, ms, pl.BlockSpec((8, D), lambda i: (0, 0))] + [sq_blk] * (4 * n_sq),
                  out_shape=[_sds((SEQ, D), F32), _sds((N_META, D), F32), _sds((8, D), F32)]
                            + [_sds((SQ_ROWS, D), F32)] * (4 * n_sq),
                  operands=[x, meta, g, dr, dhz] + sq_operands, vmem=40, sem=("arbitrary",), hook=hook)


def _mul_silu_bwd(dy, a, z, off, dz, *, name, hook=None):
    def body(dy_ref, a_ref, z_ref, dz_in, da_ref, dg_ref):
        zz = z_ref[...]
        sg = _sigmoid(zz)
        d = dy_ref[...]
        da_ref[...] = d * (zz * sg)
        dg_ref[...] = (d * a_ref[...] * (sg * (1.0 + zz * (1.0 - sg)))).astype(_MXU)

    blk = pl.BlockSpec((RT, CW), lambda i, j: (i, j))
    zblk = pl.BlockSpec((RT, CW), lambda i, j: (i, off // CW + j))
    return _pcall(body, name=name, grid=(TP // RT, D // CW), in_specs=[blk, blk, zblk, _ANY], out_specs=[blk, zblk],
                  out_shape=[_sds((TP, D), F32), _sds((TP, D_IN), _MXU)], operands=[dy, a, z, dz], vmem=32,
                  sem=("parallel", "parallel"), aliases={3: 1}, hook=hook)


def _merge_fwd(y2, z):
    w = 256

    def body(ya_ref, yb_ref, ga_ref, gb_ref, o_ref):
        y = _sigmoid(ga_ref[...]) * ya_ref[...] + _sigmoid(gb_ref[...]) * yb_ref[...]
        o_ref[...] = y.astype(_MXU)

    nb = D // w
    strip = pl.BlockSpec((TP, w), lambda j: (0, j))
    return _pcall(body, name="merge_fwd", grid=(nb,),
                  in_specs=[strip, pl.BlockSpec((TP, w), lambda j: (0, nb + j)),
                            pl.BlockSpec((TP, w), lambda j: (0, OFF_G // w + j)),
                            pl.BlockSpec((TP, w), lambda j: (0, (OFF_G + D) // w + j))],
                  out_specs=[strip], out_shape=[_sds((TP, D), _MXU)], operands=[y2, y2, z, z], vmem=56,
                  sem=("parallel",))[0]


def _merge_bwd(dmix, y2, z):
    nb = D // CW

    def body(dm_ref, y_ref, g_ref, dy_ref, dg_ref):
        dm = dm_ref[...]
        sg = _sigmoid(g_ref[...])
        dy_ref[...] = (dm * sg).astype(_MXU)
        dg_ref[...] = (dm * y_ref[...] * sg * (1.0 - sg)).astype(_MXU)

    blk = pl.BlockSpec((RT, CW), lambda i, j: (i, j))
    gblk = pl.BlockSpec((RT, CW), lambda i, j: (i, OFF_G // CW + j))
    return _pcall(body, name="merge_bwd", grid=(TP // RT, 2 * nb),
                  in_specs=[pl.BlockSpec((RT, CW), lambda i, j: (i, j % nb)), blk, gblk], out_specs=[blk, gblk],
                  out_shape=[_sds((TP, 2 * D), _MXU), _sds((TP, D_IN), _MXU)], operands=[dmix, y2, z], vmem=32,
                  sem=("parallel", "parallel"))


def _ln_out_loss(h32, out, target, g, b):
    def body(h_ref, o_ref, t_ref, g_ref, b_ref, dr_ref, dr16_ref, acc_ref):
        i = pl.program_id(0)

        @pl.when(i == 0)
        def _():
            acc_ref[...] = jnp.zeros_like(acc_ref)

        r = ALPHA * h_ref[...] + o_ref[...]
        mu = jnp.mean(r, axis=-1, keepdims=True)
        rc = r - mu
        var = jnp.mean(rc * rc, axis=-1, keepdims=True)
        rstd = lax.rsqrt(var + LN_EPS)
        xhat = rc * rstd
        gg = g_ref[...]
        y = xhat * gg + b_ref[...]
        real = (i >= 1).astype(F32)
        diff = (y - t_ref[...]) * real
        dy = diff * (1.0 / D)
        dxh = dy * gg
        m1 = jnp.mean(dxh, axis=-1, keepdims=True)
        m2 = jnp.mean(dxh * xhat, axis=-1, keepdims=True)
        dr = rstd * (dxh - m1 - xhat * m2)
        dr_ref[...] = dr
        dr16_ref[...] = dr.astype(_MXU)
        acc_ref[0:1, :] += jnp.sum(dy * xhat, axis=0, keepdims=True)
        acc_ref[1:2, :] += jnp.sum(dy, axis=0, keepdims=True)
        acc_ref[2:3, :] += jnp.sum(dr, axis=0, keepdims=True)
        acc_ref[3:4, :] += (0.5 / D) * jnp.sum(diff * diff)

    row = pl.BlockSpec((BLK, D), lambda i: (i, 0))
    vec = pl.BlockSpec((1, D), lambda i: (0, 0))
    return _pcall(body, name="ln_out_loss", grid=(NBLK,),
                  in_specs=[row, row, pl.BlockSpec((BLK, D), lambda i: (jnp.maximum(i - 1, 0), 0)), vec, vec],
                  out_specs=[row, row, pl.BlockSpec((8, D), lambda i: (0, 0))],
                  out_shape=[_sds((TP, D), F32), _sds((TP, D), _MXU), _sds((8, D), F32)],
                  operands=[h32, out, target, g, b], vmem=32, sem=("arbitrary",))


def _rnn_recompute(xr_ref, cw_ref, cb_ref, wra_ref, wri_ref, bra_ref, bri_ref, lam_ref):
    rows = lax.broadcasted_iota(jnp.int32, (TP, 1), 0)
    valid = (rows >= PAD).astype(F32)
    first = rows == PAD
    x = xr_ref[...] * valid
    cw = cw_ref[...]
    shifted = [x, pltpu.roll(x, 1, 0), pltpu.roll(x, 2, 0), pltpu.roll(x, 3, 0)]
    c = cb_ref[...] + cw[0:1, :] * shifted[0] + cw[1:2, :] * shifted[1] + cw[2:3, :] * shifted[2] + cw[3:4, :] * shifted[3]
    cm = c.astype(_MXU)
    gr = _sigmoid(jnp.dot(cm, wra_ref[...].astype(_MXU), preferred_element_type=F32) + bra_ref[...])
    gi = _sigmoid(jnp.dot(cm, wri_ref[...].astype(_MXU), preferred_element_type=F32) + bri_ref[...])
    lam = lam_ref[...]
    ls = jnp.minimum(lam, 0.0) - jnp.log(1.0 + jnp.exp(-jnp.abs(lam)))
    log_a = LRU_C * gr * ls
    a = jnp.exp(log_a)
    mult = jnp.where(first, 1.0, jnp.sqrt(1.0 - jnp.exp(2.0 * log_a)))
    return dict(valid=valid, first=first, shifted=shifted, c=c, cm=cm, gr=gr, gi=gi, ls=ls, a=a, mult=mult, lam=lam)


def _rnn_specs():
    col = pl.BlockSpec((TP, RB), lambda n: (0, n))
    vec = pl.BlockSpec((1, RB), lambda n: (0, n))
    return dict(col=col, vec=vec, cw=pl.BlockSpec((CONV_W, RB), lambda n: (0, n)),
                wblk=pl.BlockSpec((None, RB, RB), lambda n: (n, 0, 0)))


def _rnn_gates_fwd(z, conv_w, conv_b, w_ra, w_ri, b_ra, b_ri, lam):
    def body(xr_ref, cw_ref, cb_ref, wra_ref, wri_ref, bra_ref, bri_ref, lam_ref, a_ref, u_ref):
        r = _rnn_recompute(xr_ref, cw_ref, cb_ref, wra_ref, wri_ref, bra_ref, bri_ref, lam_ref)
        a_ref[...] = r["a"]
        u_ref[...] = r["mult"] * r["gi"] * r["c"] * r["valid"]

    s = _rnn_specs()
    return _pcall(body, name="rnn_gates_fwd", grid=(N_RB,),
                  in_specs=[s["col"], s["cw"], s["vec"], s["wblk"], s["wblk"], s["vec"], s["vec"], s["vec"]],
                  out_specs=[s["col"], s["col"]], out_shape=[_sds((TP, D), F32)] * 2,
                  operands=[z, conv_w, conv_b, w_ra, w_ri, b_ra, b_ri, lam], vmem=56, sem=("parallel",))


SCAN_ROWS = 272


SUB = 8


def _tile_scan(a, u, reverse):
    rows = lax.broadcasted_iota(jnp.int32, a.shape, 0)
    for d in (1, 2, 4):
        shift = SUB - d if reverse else d
        inside = (rows < SUB - d) if reverse else (rows >= d)
        u = u + a * jnp.where(inside, pltpu.roll(u, shift, 0), 0.0)
        a = a * jnp.where(inside, pltpu.roll(a, shift, 0), 1.0)
    return a, u


def _scan_fwd(a, u, z, *, hook):
    def body(a_ref, u_ref, gr_ref, h_ref, y_ref, carry_ref):
        @pl.when(pl.program_id(1) == 0)
        def _():
            carry_ref[...] = jnp.zeros_like(carry_ref)

        def step(r, h):
            rows = pl.ds(pl.multiple_of(r * SUB, SUB), SUB)
            prod, part = _tile_scan(a_ref[rows, :], u_ref[rows, :], False)
            ht = part + prod * h
            h_ref[rows, :] = ht
            return ht[SUB - 1:SUB, :]

        carry_ref[...] = lax.fori_loop(0, SCAN_ROWS // SUB, step, carry_ref[...], unroll=2)
        zz = gr_ref[...]
        y_ref[...] = (h_ref[...] * (zz * _sigmoid(zz))).astype(_MXU)

    blk = pl.BlockSpec((SCAN_ROWS, CW), lambda j, i: (i, j))
    return _pcall(body, name="scan_fwd", grid=(D // CW, TP // SCAN_ROWS),
                  in_specs=[blk, blk, pl.BlockSpec((SCAN_ROWS, CW), lambda j, i: (i, OFF_GR // CW + j))],
                  out_specs=[blk, blk], out_shape=[_sds((TP, D), F32), _sds((TP, D), _MXU)], operands=[a, u, z],
                  scratch=[pltpu.VMEM((1, CW), F32)], vmem=32, sem=("parallel", "arbitrary"), hook=hook)


def _scan_bwd(a, dh):
    nst = TP // SCAN_ROWS
    n_tiles = SCAN_ROWS // SUB

    def body(a_ref, d_ref, o_ref, lam_ref, anext_ref):
        @pl.when(pl.program_id(1) == 0)
        def _():
            lam_ref[...] = jnp.zeros_like(lam_ref)
            anext_ref[...] = jnp.zeros_like(anext_ref)

        def step(q, carry):
            lam_next, a_next = carry
            rows = pl.ds(pl.multiple_of((n_tiles - 1 - q) * SUB, SUB), SUB)
            at = a_ref[rows, :]
            last = lax.broadcasted_iota(jnp.int32, at.shape, 0) == SUB - 1
            b = jnp.where(last, a_next, pltpu.roll(at, SUB - 1, 0))
            prod, part = _tile_scan(b, d_ref[rows, :], True)
            lam = part + prod * lam_next
            o_ref[rows, :] = lam
            return lam[0:1, :], at[0:1, :]

        lam, an = lax.fori_loop(0, n_tiles, step, (lam_ref[...], anext_ref[...]), unroll=2)
        lam_ref[...] = lam
        anext_ref[...] = an

    blk = pl.BlockSpec((SCAN_ROWS, CW), lambda j, i: (nst - 1 - i, j))
    return _pcall(body, name="scan_bwd", grid=(D // CW, nst), in_specs=[blk, blk], out_specs=[blk],
                  out_shape=[_sds((TP, D), F32)], operands=[a, dh],
                  scratch=[pltpu.VMEM((1, CW), F32), pltpu.VMEM((1, CW), F32)], vmem=32,
                  sem=("parallel", "arbitrary"))[0]


def _rnn_gates_bwd(z, lam_s, hr, conv_w, conv_b, w_ra, w_ri, b_ra, b_ri, lam, dz, gsq):
    def body(xr_ref, ls_ref, hr_ref, cw_ref, cb_ref, wra_ref, wri_ref, bra_ref, bri_ref, lam_ref, dz_in, gsq_in,
             dx_ref, dw_ref, sums_ref):
        r = _rnn_recompute(xr_ref, cw_ref, cb_ref, wra_ref, wri_ref, bra_ref, bri_ref, lam_ref)
        valid, c, gr, gi, a, mult = r["valid"], r["c"], r["gr"], r["gi"], r["a"], r["mult"]
        du = ls_ref[...] * valid
        da = du * pltpu.roll(hr_ref[...], 1, 0)
        d_gi = du * mult * c
        dc = du * mult * gi
        dmult = du * gi * c
        dlog_a = da * a + jnp.where(r["first"], 0.0, -dmult * a * a / mult)
        d_gr = dlog_a * (LRU_C * r["ls"])
        dls = jnp.sum(dlog_a * (LRU_C * gr), axis=0, keepdims=True)
        dpre_r = d_gr * gr * (1.0 - gr)
        dpre_i = d_gi * gi * (1.0 - gi)
        pr = dpre_r.astype(_MXU)
        pi = dpre_i.astype(_MXU)
        dwra = lax.dot_general(r["cm"], pr, _TN, preferred_element_type=F32)
        dwri = lax.dot_general(r["cm"], pi, _TN, preferred_element_type=F32)
        for s in range(N_SH):
            dw_ref[s, 0:64, :] = dwra[64 * s:64 * (s + 1)]
            dw_ref[s, 64:128, :] = dwri[64 * s:64 * (s + 1)]
        dc = dc + lax.dot_general(pr, wra_ref[...].astype(_MXU), _NT, preferred_element_type=F32)
        dc = dc + lax.dot_general(pi, wri_ref[...].astype(_MXU), _NT, preferred_element_type=F32)
        cw = cw_ref[...]
        dx = cw[0:1, :] * dc
        for k in range(1, CONV_W):
            dx = dx + cw[k:k + 1, :] * pltpu.roll(dc, TP - k, 0)
        dx_ref[...] = (dx * valid).astype(_MXU)
        for k in range(CONV_W):
            sums_ref[k:k + 1, :] = jnp.sum(dc * r["shifted"][k], axis=0, keepdims=True)
        sums_ref[4:5, :] = jnp.sum(dc, axis=0, keepdims=True)
        sums_ref[5:6, :] = jnp.sum(dpre_r, axis=0, keepdims=True)
        sums_ref[6:7, :] = jnp.sum(dpre_i, axis=0, keepdims=True)
        sums_ref[7:8, :] = dls * _sigmoid(-r["lam"])

    s = _rnn_specs()
    return _pcall(
        body, name="rnn_gates_bwd", grid=(N_RB,),
        in_specs=[s["col"], s["col"], s["col"], s["cw"], s["vec"], s["wblk"], s["wblk"], s["vec"], s["vec"], s["vec"],
                  _ANY, _ANY],
        out_specs=[s["col"], pl.BlockSpec((N_SH, 128, RB), lambda n: (0, 3 * SQ_ROWS // 128, n)),
                   pl.BlockSpec((8, RB), lambda n: (0, n))],
        out_shape=[_sds((TP, D_IN), _MXU), _sds((N_SH, PACK_ROWS, D), F32), _sds((8, D), F32)],
        operands=[z, lam_s, hr, conv_w, conv_b, w_ra, w_ri, b_ra, b_ri, lam, dz, gsq], vmem=60, sem=("parallel",),
        aliases={10: 0, 11: 1})


def _rope_tables():
    half = HD // 2
    inv = ROPE_THETA ** (-jnp.arange(half, dtype=F32) / half)
    pos = (jnp.arange(TP) - PAD).astype(F32)
    ang = pos[:, None] * inv[None, :]
    return jnp.tile(jnp.cos(ang), (1, 4)), jnp.tile(jnp.sin(ang), (1, 4))


def _rope(x, cos_t, sin_t, sign):
    w = x.shape[1]
    lane = lax.broadcasted_iota(jnp.int32, x.shape, 1)
    first = (lane % HD) < (HD // 2)
    swapped = jnp.where(first, pltpu.roll(x, w - HD // 2, 1), pltpu.roll(x, HD // 2, 1))
    ct = jnp.tile(cos_t, (1, w // 128))
    st = jnp.tile(sin_t, (1, w // 128))
    return x * ct + swapped * jnp.where(first, -sign * st, sign * st)


def _rope_fwd(z, cos_t, sin_t):
    def body(q_ref, k_ref, v_ref, c_ref, s_ref, qo_ref, ko_ref, vo_ref):
        c = c_ref[...]
        s = s_ref[...]
        qo_ref[...] = _rope(q_ref[...], c, s, 1.0).astype(_MXU)
        ko_ref[...] = _rope(k_ref[...], c, s, 1.0).astype(_MXU)
        vo_ref[...] = v_ref[...].astype(_MXU)

    tab = pl.BlockSpec((BLK, 128), lambda i: (i, 0))
    kv = pl.BlockSpec((BLK, D_KV), lambda i: (i, 0))
    return _pcall(body, name="rope_fwd", grid=(NBLK,),
                  in_specs=[pl.BlockSpec((BLK, D), lambda i: (i, OFF_Q // D)),
                            pl.BlockSpec((BLK, D_KV), lambda i: (i, OFF_K // D_KV)),
                            pl.BlockSpec((BLK, D_KV), lambda i: (i, OFF_V // D_KV)), tab, tab],
                  out_specs=[pl.BlockSpec((BLK, D), lambda i: (i, 0)), kv, kv],
                  out_shape=[_sds((TP, D), _MXU), _sds((TP, D_KV), _MXU), _sds((TP, D_KV), _MXU)],
                  operands=[z, z, z, cos_t, sin_t], vmem=32, sem=("parallel",))


def _rope_bwd_k(dk, dv, cos_t, sin_t, dz):
    def body(dk_ref, dv_ref, c_ref, s_ref, dz_in, o_ref):
        o_ref[:, 0:D_KV] = _rope(dk_ref[...], c_ref[...], s_ref[...], -1.0).astype(_MXU)
        o_ref[:, D_KV:2 * D_KV] = dv_ref[...].astype(_MXU)

    tab = pl.BlockSpec((BLK, 128), lambda i: (i, 0))
    kv = pl.BlockSpec((BLK, D_KV), lambda i: (i, 0))
    return _pcall(body, name="rope_bwd_k", grid=(NBLK,), in_specs=[kv, kv, tab, tab, _ANY],
                  out_specs=[pl.BlockSpec((BLK, 2 * D_KV), lambda i: (i, OFF_K // (2 * D_KV)))],
                  out_shape=[_sds((TP, D_IN), _MXU)], operands=[dk, dv, cos_t, sin_t, dz], vmem=32, sem=("parallel",),
                  aliases={4: 0})[0]


def _attn_mask(i):
    ql = lax.broadcasted_iota(jnp.int32, (BLK, 3 * BLK), 0)
    kk = lax.broadcasted_iota(jnp.int32, (BLK, 3 * BLK), 1)
    kl = kk % BLK
    part = kk // BLK
    meta = (part == 0) & (kl >= PAD) & ((i >= 1) | (kl <= ql))
    prev = (part == 1) & (i >= 2) & (kl > ql)
    cur = (part == 2) & (i >= 1) & (kl <= ql)
    return meta | prev | cur


def _kv_specs():
    return [pl.BlockSpec((BLK, D_KV), lambda i: (0, 0)),
            pl.BlockSpec((BLK, D_KV), lambda i: (jnp.maximum(i - 1, 0), 0)),
            pl.BlockSpec((BLK, D_KV), lambda i: (i, 0))]


def _pair_heads(ref, gp):
    def two(j):
        h0 = 2 * GRP * gp + j
        return jnp.concatenate([ref[:, HD * h0:HD * (h0 + 1)], ref[:, HD * (h0 + GRP):HD * (h0 + GRP + 1)]], axis=1)

    return jnp.concatenate([two(j) for j in range(GRP)], axis=0)


def _pair_kv(refs, gp):
    both = jnp.concatenate([r[:, 2 * HD * gp:2 * HD * (gp + 1)] for r in refs], axis=0)
    low_lanes = lax.broadcasted_iota(jnp.int32, both.shape, 1) < HD
    zero = jnp.zeros_like(both)
    return jnp.concatenate([jnp.where(low_lanes, both, zero), jnp.where(low_lanes, zero, both)], axis=0)


def _attn_fwd(q, k, v, sinks, z, *, hook):
    nk = 3 * BLK
    n_gate = D // CW

    def body(q_ref, k0_ref, kp_ref, kc_ref, v0_ref, vp_ref, vc_ref, sink_ref, *rest):
        gate_refs, (o_ref, lse_ref, y_ref, yt_ref, s_scr, p_scr) = rest[:n_gate], rest[n_gate:]
        mask = _attn_mask(pl.program_id(0))
        for gp in range(N_KV // 2):
            s_scr[...] = lax.dot_general(_pair_heads(q_ref, gp), _pair_kv((k0_ref, kp_ref, kc_ref), gp), _NT,
                                         preferred_element_type=F32)
            for a in range(2):
                for j in range(GRP):
                    h = GRP * (2 * gp + a) + j
                    rows, cols = slice(BLK * j, BLK * (j + 1)), slice(nk * a, nk * (a + 1))
                    sink = sink_ref[h]
                    s = jnp.where(mask, s_scr[rows, cols] * (HD ** -0.5), NEG_INF)
                    mx = jnp.maximum(jnp.max(s, -1, keepdims=True), sink)
                    p = jnp.exp(s - mx)
                    den = jnp.sum(p, -1, keepdims=True) + jnp.exp(sink - mx)
                    p_scr[rows, cols] = (p * (1.0 / den)).astype(_MXU)
                    lse_ref[:, h:h + 1] = mx + jnp.log(den)
            o2 = jnp.dot(p_scr[...], _pair_kv((v0_ref, vp_ref, vc_ref), gp), preferred_element_type=F32)
            for a in range(2):
                for j in range(GRP):
                    h = GRP * (2 * gp + a) + j
                    o_ref[:, HD * h:HD * (h + 1)] = o2[BLK * j:BLK * (j + 1), HD * a:HD * (a + 1)]
        for t, g_ref in enumerate(gate_refs):
            cs = slice(CW * t, CW * (t + 1))
            zz = g_ref[...]
            y = o_ref[:, cs] * (zz * _sigmoid(zz))
            y_ref[:, cs] = y.astype(_MXU)
            yt_ref[cs, :] = y.T.astype(_MXU)

    row = pl.BlockSpec((BLK, D), lambda i: (i, 0))
    gates = [pl.BlockSpec((BLK, CW), lambda i, t=t: (i, OFF_GA // CW + t)) for t in range(n_gate)]
    return _pcall(body, name="attn_fwd", grid=(NBLK,),
                  in_specs=[row] + _kv_specs() + _kv_specs() + [pl.BlockSpec(memory_space=pltpu.SMEM)] + gates,
                  out_specs=[row, pl.BlockSpec((BLK, N_Q), lambda i: (i, 0)), row, pl.BlockSpec((D, BLK), lambda i: (0, i))],
                  out_shape=[_sds((TP, D), F32), _sds((TP, N_Q), F32), _sds((TP, D), _MXU), _sds((D, TP), _MXU)],
                  operands=[q, k, k, k, v, v, v, sinks] + [z] * n_gate,
                  scratch=[pltpu.VMEM((GRP * BLK, 2 * nk), F32), pltpu.VMEM((GRP * BLK, 2 * nk), _MXU)],
                  vmem=40, sem=("parallel",), hook=hook)


def _attn_bwd(q, k, v, sinks, do, o, lse, cos_t, sin_t, dz, *, hook):
    def body(q_ref, k0_ref, kp_ref, kc_ref, v0_ref, vp_ref, vc_ref, sink_ref, do_ref, o_ref, lse_ref, c_ref, s_ref, dz_in,
             dq_ref, dk_ref, dv_ref, dsink_ref, dqrot_ref, s_scr, dp_scr, p_scr, ds_scr):
        i = pl.program_id(0)

        @pl.when(i == 0)
        def _():
            dk_ref[...] = jnp.zeros_like(dk_ref)
            dv_ref[...] = jnp.zeros_like(dv_ref)
            dsink_ref[...] = jnp.zeros_like(dsink_ref)

        mask = _attn_mask(i)
        row_starts = (0, pl.multiple_of(jnp.maximum(i - 1, 0) * BLK, BLK), pl.multiple_of(i * BLK, BLK))
        scale = HD ** -0.5
        nk = 3 * BLK
        for gp in range(N_KV // 2):
            q2 = _pair_heads(q_ref, gp)
            dom2 = _pair_heads(do_ref, gp).astype(_MXU)
            k2 = _pair_kv((k0_ref, kp_ref, kc_ref), gp)
            s_scr[...] = lax.dot_general(q2, k2, _NT, preferred_element_type=F32)
            dp_scr[...] = lax.dot_general(dom2, _pair_kv((v0_ref, vp_ref, vc_ref), gp), _NT, preferred_element_type=F32)
            for a in range(2):
                for j in range(GRP):
                    h = GRP * (2 * gp + a) + j
                    hs = slice(HD * h, HD * (h + 1))
                    rows = slice(BLK * j, BLK * (j + 1))
                    cols = slice(nk * a, nk * (a + 1))
                    lse_h = lse_ref[:, h:h + 1]
                    delta = jnp.sum(do_ref[:, hs] * o_ref[:, hs], axis=-1, keepdims=True)
                    dsink_ref[0:1, h:h + 1] += jnp.sum(-jnp.exp(sink_ref[h] - lse_h) * delta, axis=0, keepdims=True)
                    p = jnp.exp(jnp.where(mask, s_scr[rows, cols] * scale, NEG_INF) - lse_h)
                    p_scr[rows, cols] = p.astype(_MXU)
                    ds_scr[rows, cols] = (p * (dp_scr[rows, cols] - delta) * scale).astype(_MXU)
            ds = ds_scr[...]
            dq2 = jnp.dot(ds, k2, preferred_element_type=F32)
            dk2 = lax.dot_general(ds, q2, _TN, preferred_element_type=F32)
            dv2 = lax.dot_general(p_scr[...], dom2, _TN, preferred_element_type=F32)
            for a in range(2):
                g = 2 * gp + a
                gs = slice(HD * g, HD * (g + 1))
                for part in range(3):
                    rows = pl.ds(row_starts[part], BLK)
                    blk = slice(nk * a + BLK * part, nk * a + BLK * (part + 1))
                    dk_ref[rows, gs] += dk2[blk, HD * a:HD * (a + 1)]
                    dv_ref[rows, gs] += dv2[blk, HD * a:HD * (a + 1)]
                for j in range(GRP):
                    h = GRP * g + j
                    dqrot_ref[:, HD * h:HD * (h + 1)] = dq2[BLK * j:BLK * (j + 1), HD * a:HD * (a + 1)]
        dq_ref[...] = _rope(dqrot_ref[...], c_ref[...], s_ref[...], -1.0).astype(_MXU)

    row = pl.BlockSpec((BLK, D), lambda i: (i, 0))
    tab = pl.BlockSpec((BLK, 128), lambda i: (i, 0))
    full_kv = pl.BlockSpec((TP, D_KV), lambda i: (0, 0))
    return _pcall(
        body, name="attn_bwd", grid=(NBLK,),
        in_specs=[row] + _kv_specs() + _kv_specs() + [pl.BlockSpec(memory_space=pltpu.SMEM), row, row,
                  pl.BlockSpec((BLK, N_Q), lambda i: (i, 0)), tab, tab, _ANY],
        out_specs=[pl.BlockSpec((BLK, D), lambda i: (i, OFF_Q // D)), full_kv, full_kv,
                   pl.BlockSpec((8, 128), lambda i: (0, 0))],
        out_shape=[_sds((TP, D_IN), _MXU), _sds((TP, D_KV), F32), _sds((TP, D_KV), F32), _sds((8, 128), F32)],
        operands=[q, k, k, k, v, v, v, sinks, do, o, lse, cos_t, sin_t, dz],
        scratch=[pltpu.VMEM((BLK, D), F32), pltpu.VMEM((GRP * BLK, 6 * BLK), F32), pltpu.VMEM((GRP * BLK, 6 * BLK), F32),
                 pltpu.VMEM((GRP * BLK, 6 * BLK), _MXU), pltpu.VMEM((GRP * BLK, 6 * BLK), _MXU)],
        vmem=48, sem=("arbitrary",), aliases={13: 0}, hook=hook)


def _cast_into_slot(w32, pos, *, name, tr):
    r, cc = w32.shape

    def body(pos_ref, w_ref, o_ref):
        o_ref[...] = w_ref[...].astype(BF16)

    return _pcall(body, name=name, grid=(r // tr,), in_specs=[pl.BlockSpec((tr, cc), lambda i, p: (i, 0))],
                  out_specs=[pl.BlockSpec((None, tr, cc), lambda i, p: (p[1], i, 0))],
                  out_shape=[_sds((N_SH, r, cc), BF16)], operands=[w32], vmem=32, sem=("parallel",), prefetch=(pos,))[0]


def _pair_add(g, got, pos, *, name, tr, g_has_both_halves, hook=None):
    n, h, cc = got.shape
    nt = h // tr

    def body(pos_ref, g_ref, r_ref, own_ref, s16_ref):
        s = g_ref[...] + r_ref[...].astype(F32)
        s16_ref[...] = s.astype(BF16)

        @pl.when(pl.program_id(1) == pos_ref[1])
        def _():
            own_ref[...] = s

    g_index = (lambda i, s, p: (s, p[0] * nt + i, 0)) if g_has_both_halves else (lambda i, s, p: (s, i, 0))
    return _pcall(body, name=name, grid=(nt, n),
                  in_specs=[pl.BlockSpec((None, tr, cc), g_index), pl.BlockSpec((None, tr, cc), lambda i, s, p: (s, i, 0))],
                  out_specs=[pl.BlockSpec((tr, cc), lambda i, s, p: (i, 0)),
                             pl.BlockSpec((None, tr, cc), lambda i, s, p: (s, i, 0))],
                  out_shape=[_sds((h, cc), F32), _sds((n, h, cc), BF16)], operands=[g, got], vmem=40,
                  sem=("parallel", "arbitrary"), prefetch=(pos,), hook=hook)


def _relay_add(s16, raw, pos, *, name, tr):
    _, q, cc = raw.shape
    nt = q // tr

    def body(pos_ref, s_ref, r_ref, o_ref):
        o_ref[...] = (s_ref[...].astype(F32) + r_ref[...].astype(F32)).astype(BF16)

    blk = pl.BlockSpec((None, tr, cc), lambda k, i, p: (k, i, 0))
    return _pcall(body, name=name, grid=(2, nt),
                  in_specs=[pl.BlockSpec((None, tr, cc), lambda k, i, p: (p[3 + k], (1 - k) * nt + i, 0)), blk],
                  out_specs=[blk], out_shape=[_sds((2, q, cc), BF16)], operands=[s16, raw], vmem=40,
                  sem=("parallel", "parallel"), prefetch=(pos,))[0]


def _sum_chips(own, got, pos, *, name, tr):
    h, cc = own.shape
    n_got = got.shape[0]
    nt = h // tr

    def body(pos_ref, o_ref, r_ref, out_ref):
        acc = o_ref[...]
        for k in range(n_got):
            acc = acc + r_ref[k].astype(F32)
        out_ref[...] = acc

    return _pcall(body, name=name, grid=(nt,),
                  in_specs=[pl.BlockSpec((tr, cc), lambda i, p: (i, 0)),
                            pl.BlockSpec((n_got, tr, cc), lambda i, p: (0, i, 0))],
                  out_specs=[pl.BlockSpec((tr, cc), lambda i, p: (p[0] * nt + i, 0))],
                  out_shape=[_sds((2 * h, cc), F32)], operands=[own, got], vmem=40, sem=("parallel",), prefetch=(pos,))[0]


def _hand_over(arrays):
    def body(*refs):
        del refs

    n = len(arrays)
    return pl.pallas_call(body, name="hand_over", in_specs=[_ANY] * n, out_specs=[_ANY] * n,
                          out_shape=[_sds(a.shape, a.dtype) for a in arrays],
                          input_output_aliases={t: t for t in range(n)})(*arrays)


def _adamw(w, g, m, v, *, name, tr, g_row0=0):
    r, cc = w.shape
    g_blk0 = g_row0 // tr

    def body(*refs):
        _adamw_step(*refs)

    blk = pl.BlockSpec((tr, cc), lambda i: (i, 0))
    gblk = pl.BlockSpec((tr, cc), lambda i: (g_blk0 + i, 0))
    return _pcall(body, name=name, grid=(r // tr,), in_specs=[blk, gblk, blk, blk], out_specs=[blk] * 4,
                  out_shape=[_sds((r, cc), F32)] * 4, operands=[w, g, m, v], vmem=48, sem=("parallel",))


_SMALL_ROWS = 40
_B_IN_ROWS = 7


def _row_pad(v, rows):
    flat = v.reshape(-1)
    return jnp.pad(flat, (0, rows * D - flat.shape[0])).reshape(rows, D)


def _pack_ra(w):
    return w.transpose(1, 0, 2).reshape(64, D)


def _unpack_ra(p, like):
    return p.reshape(64, N_RB, RB).transpose(1, 0, 2).reshape(like.shape)


def _gate_full(g4):
    return g4.reshape(N_SH, 64, N_RB, RB).transpose(2, 0, 1, 3).reshape(N_RB, RB, RB)


def kernel(x, meta_tokens, ln_emb_g, ln_emb_b, w_in, b_in, conv_w, conv_b, w_ra, b_ra, w_ri, b_ri, lru_lambda, sinks, w_rnn_out, w_attn_out, w_o, b_o, ln_g, ln_b, loss_target, m_meta_tokens, m_ln_emb_g, m_ln_emb_b, m_w_in, m_b_in, m_conv_w, m_conv_b, m_w_ra, m_b_ra, m_w_ri, m_b_ri, m_lru_lambda, m_sinks, m_w_rnn_out, m_w_attn_out, m_w_o, m_b_o, m_ln_g, m_ln_b, v_meta_tokens, v_ln_emb_g, v_ln_emb_b, v_w_in, v_b_in, v_conv_w, v_conv_b, v_w_ra, v_b_ra, v_w_ri, v_b_ri, v_lru_lambda, v_sinks, v_w_rnn_out, v_w_attn_out, v_w_o, v_b_o, v_ln_g, v_ln_b):
    xi, yi, ci = _my_pos()
    shard = 2 * xi + yi
    pos = jnp.stack([ci, shard, 1 - ci, shard ^ 2, shard ^ 1]).astype(jnp.int32)
    cos_t, sin_t = _rope_tables()
    zero_bias = jnp.zeros((1, D), F32)
    ln_emb_g2, ln_emb_b2 = ln_emb_g[None], ln_emb_b[None]

    small = jnp.concatenate([conv_w[0], meta_tokens, jnp.zeros((4, 512), F32)], axis=0)
    small4 = _gather_small(small)
    conv_w_full = small4[:, 0:4].transpose(1, 0, 2).reshape(CONV_W, D)
    meta_full = small4[:, 4:20].transpose(1, 0, 2).reshape(N_META, D)
    w_own = _cast_into_slot(w_in[0], pos, name="cast_w_in", tr=256)
    wa_own = _cast_into_slot(jnp.concatenate([w_attn_out[0], w_o[0], _pack_ra(w_ra[0]), _pack_ra(w_ri[0])], axis=0),
                             pos, name="cast_w_a", tr=288)
    wb_own = _cast_into_slot(w_rnn_out[0], pos, name="cast_w_b", tr=256)

    h32, h16, h16_t = _ln_emb_fwd(x[0], meta_full, ln_emb_g2, ln_emb_b2)
    order = jnp.stack([shard, shard ^ 2, shard ^ 1, shard ^ 3]).astype(jnp.int32)
    z, w_in4 = _mm_z_gather(h16, w_own, b_in, order)
    q, k, v = _rope_fwd(z, cos_t, sin_t)
    o, lse, yb_in, yb_in_t, wa4 = _attn_fwd(q, k, v, sinks[0], z, hook=_hook_gather(wa_own, 0.6))
    w_ra_full = _gate_full(wa4[:, 2 * SQ_ROWS:2 * SQ_ROWS + 64])
    w_ri_full = _gate_full(wa4[:, 2 * SQ_ROWS + 64:2 * SQ_ROWS + 128])
    a_dec, u_in = _rnn_gates_fwd(z, conv_w_full, conv_b, w_ra_full, w_ri_full, b_ra, b_ri, lru_lambda)
    hr, ya_in, wb4 = _scan_fwd(a_dec, u_in, z, hook=_hook_gather(wb_own, 0.6))
    sq_w = {0: (wb4, 0), 1: (wa4, 0), 2: (wa4, 1)}

    def sq_nn(a, kk, bias, name, out_cols, out_index, carried=None):
        wp, blk = sq_w[kk]
        return _mm_nn(a, wp, bias, name=name, grid=(2, D // CW), tm=HALF_TP, tn=CW, k=D, a_index=lambda i, j: (i, 0),
                      w_block=(N_SH, SQ_ROWS, CW), w_index=lambda i, j: (0, blk, j), out_cols=out_cols,
                      out_index=out_index, carried=carried)[0]

    def sq_nt(a, a_blk, kk, name, hook=None):
        wp, blk = sq_w[kk]
        return _mm_nt(a, wp, name=name, grid=(2, N_SH, 1), tm=HALF_TP, tn=SQ_ROWS, tk=D,
                      a_index=lambda i, j, q: (i, a_blk), w_block=(None, SQ_ROWS, D),
                      w_index=lambda i, j, q: (j, blk, 0), out_cols=D, hook=hook)

    y2 = sq_nn(ya_in, 0, zero_bias, "mm_ya", 2 * D, lambda i, j: (i, j))
    y2 = sq_nn(yb_in, 1, zero_bias, "mm_yb", 2 * D, lambda i, j: (i, D // CW + j), carried=y2)
    mixed = _merge_fwd(y2, z)
    out = sq_nn(mixed, 2, b_o, "mm_out", D, lambda i, j: (i, j))
    dr, dr16, sums_o = _ln_out_loss(h32, out, loss_target[0], ln_g, ln_b)

    def sq_tn(at, b, b_blk, kk, name, carried=None, a_transposed=True):
        return _mm_tn(at, b, name=name, grid=(N_SH, 1), tm=SQ_ROWS, tn=D,
                      a_index=(lambda i, j: (i, 0)) if a_transposed else (lambda i, j: (0, i)),
                      b_index=lambda i, j: (0, b_blk), out_shape=(N_SH, PACK_ROWS, D), out_block=(None, SQ_ROWS, D),
                      out_index=lambda i, j: (i, kk, 0), carried=carried, a_transposed=a_transposed)[0]

    gsq = sq_tn(mixed, dr16, 0, 2, "mm_dwo", a_transposed=False)
    dmix = sq_nt(dr16, 0, 2, "mm_dmix")[0]
    dy2, dz = _merge_bwd(dmix, y2, z)
    gsq = sq_tn(ya_in, dy2, 0, 0, "mm_dwrnn", carried=gsq, a_transposed=False)
    gsq = sq_tn(yb_in_t, dy2, 1, 1, "mm_dwattn", carried=gsq)
    dya_in = sq_nt(dy2, 0, 0, "mm_dyain")[0]
    dhr, dz = _mul_silu_bwd(dya_in, hr, z, OFF_GR, dz, name="gate_a_bwd")
    lam_s = _scan_bwd(a_dec, dhr)
    dz, gsq, sums_r = _rnn_gates_bwd(z, lam_s, hr, conv_w_full, conv_b, w_ra_full, w_ri_full, b_ra, b_ri, lru_lambda, dz, gsq)
    dyb_in, gsq, got_sq = sq_nt(dy2, 1, 1, "mm_dybin", hook=_hook_pair(gsq, True, (0, 1)))
    do, dz, gsq, got_sq = _mul_silu_bwd(dyb_in, o, z, OFF_GA, dz, name="gate_b_bwd",
                                        hook=_hook_pair(gsq, True, (2, 3), land=got_sq))
    own_sq, s16_sq = _pair_add(gsq, got_sq, pos, name="red_w_sq_add", tr=208, g_has_both_halves=True)
    dz, dk_rot, dv32, dsink, s16_sq, fin_sq, raw_sq = _attn_bwd(q, k, v, sinks[0], do, o, lse, cos_t, sin_t, dz,
                                                                hook=_hook_scatter_direct(s16_sq))
    dz = _rope_bwd_k(dk_rot, dv32, cos_t, sin_t, dz)
    comb_sq = _relay_add(s16_sq, raw_sq, pos, name="red_w_sq_relay", tr=208)

    def dwin(half_idx, name, hook, **kw):
        return _mm_tn(h16_t, dz, name=name, grid=(1, N_SH * PER_IN), tm=D // 2, tn=TN_IN,
                      a_index=lambda i, j, p: (p[half_idx], 0), b_index=lambda i, j, p: (0, j),
                      out_shape=(N_SH, D // 2, W_IN_COLS), out_block=(None, D // 2, TN_IN),
                      out_index=lambda i, j, p: (j // PER_IN, 0, j % PER_IN), prefetch=(pos,), hook=hook, **kw)

    gin_sib, g_b_in, comb_sq, fin_sq = dwin(2, "mm_dwin_sib", _hook_scatter_relay(comb_sq, fin_sq),
                                            out_dtype=BF16, colsum=True)
    red_sq = _sum_chips(own_sq, fin_sq, pos, name="red_w_sq_sum", tr=208)
    gin_own, gin_sib, got_in = dwin(0, "mm_dwin_own", _hook_pair(gin_sib, False))
    own_in, s16_in, g_sq = _pair_add(gin_own, got_in, pos, name="red_w_in_add", tr=128, g_has_both_halves=False,
                                     hook=_hook_halves(red_sq))
    dhz, s16_in, fin_in, raw_in = _mm_nt(dz, w_in4, name="mm_dhz", grid=(2, 2, N_SH), tm=HALF_TP, tn=D // 2,
                                         tk=W_IN_COLS, a_index=lambda i, j, q: (i, q),
                                         w_block=(None, D // 2, W_IN_COLS), w_index=lambda i, j, q: (q, j, 0), out_cols=D,
                                         hook=_hook_scatter_direct(s16_in))
    comb_in = _relay_add(s16_in, raw_in, pos, name="red_w_in_relay", tr=128)
    sq_params = [("w_rnn_out", w_rnn_out, m_w_rnn_out, v_w_rnn_out), ("w_attn_out", w_attn_out, m_w_attn_out, v_w_attn_out),
                 ("w_o", w_o, m_w_o, v_w_o)]
    g_x, g_meta_local, sums_e, *sq_res, comb_in, fin_in = _ln_emb_bwd(
        x[0], meta_full, ln_emb_g2, dr, dhz, g_sq, [(w_[0], m_[0], v_[0]) for _, w_, m_, v_ in sq_params],
        hook=_hook_scatter_relay(comb_in, fin_in))
    red_in = _sum_chips(own_in, fin_in, pos, name="red_w_in_sum", tr=128)

    spack = jnp.concatenate([
        sums_e[0:1], sums_e[1:2], _row_pad(g_b_in, _B_IN_ROWS), sums_r[0:4], sums_r[4:5], sums_r[5:6], sums_r[6:7],
        sums_r[7:8], _row_pad(dsink[0:1, 0:N_Q], 1), sums_o[2:3], sums_o[0:1], sums_o[1:2], g_meta_local, sums_o[3:4],
        jnp.zeros((_SMALL_ROWS - 38, D), F32)], axis=0)
    sred, g_in = _allreduce_small(spack, red_in)

    big = {"w_in": [t.reshape(w_in.shape) for t in
                    _adamw(w_in[0], g_in, m_w_in[0], v_w_in[0], name="adamw_w_in", tr=128)]}
    g_x, *sq_res = _hand_over([g_x] + sq_res)
    for kk, (n, w_, _, _) in enumerate(sq_params):
        big[n] = [t.reshape(w_.shape) for t in sq_res[4 * kk:4 * kk + 4]]
    for kk, (n, w_, m_, v_) in enumerate([("w_ra", w_ra, m_w_ra, v_w_ra), ("w_ri", w_ri, m_w_ri, v_w_ri)]):
        big[n] = [_unpack_ra(t, w_) for t in
                  _adamw(_pack_ra(w_[0]), g_sq, _pack_ra(m_[0]), _pack_ra(v_[0]), name="adamw_" + n, tr=64,
                         g_row0=3 * SQ_ROWS + 64 * kk)]

    loss = sred[37, 0]
    col0 = shard * 512
    g_conv_w = lax.dynamic_slice(sred[9:13], (0, col0), (CONV_W, 512))
    g_meta = lax.dynamic_slice(sred[21:37], (0, col0), (N_META, 512))
    small_g = {"ln_emb_g": sred[0:1], "ln_emb_b": sred[1:2], "b_in": sred[2:9], "conv_w": g_conv_w.reshape(1, D),
               "conv_b": sred[13:14], "b_ra": sred[14:15], "b_ri": sred[15:16], "lru_lambda": sred[16:17],
               "sinks": sred[17:18], "b_o": sred[18:19], "ln_g": sred[19:20], "ln_b": sred[20:21],
               "meta_tokens": g_meta.reshape(4, D)}
    small_names = list(small_g)

    def small_pack(vals):
        rows = []
        for n in small_names:
            a = vals[n]
            if n == "b_in":
                a = _row_pad(a, _B_IN_ROWS)
            elif n == "sinks":
                a = _row_pad(a, 1)
            else:
                a = a.reshape(-1, D)
            rows.append(a)
        return jnp.concatenate(rows + [jnp.zeros((24 - 22, D), F32)], axis=0)

    w_small = dict(ln_emb_g=ln_emb_g, ln_emb_b=ln_emb_b, b_in=b_in, conv_w=conv_w, conv_b=conv_b, b_ra=b_ra, b_ri=b_ri,
                   lru_lambda=lru_lambda, sinks=sinks, b_o=b_o, ln_g=ln_g, ln_b=ln_b, meta_tokens=meta_tokens)
    m_small = dict(ln_emb_g=m_ln_emb_g, ln_emb_b=m_ln_emb_b, b_in=m_b_in, conv_w=m_conv_w, conv_b=m_conv_b, b_ra=m_b_ra,
                   b_ri=m_b_ri, lru_lambda=m_lru_lambda, sinks=m_sinks, b_o=m_b_o, ln_g=m_ln_g, ln_b=m_ln_b,
                   meta_tokens=m_meta_tokens)
    v_small = dict(ln_emb_g=v_ln_emb_g, ln_emb_b=v_ln_emb_b, b_in=v_b_in, conv_w=v_conv_w, conv_b=v_conv_b, b_ra=v_b_ra,
                   b_ri=v_b_ri, lru_lambda=v_lru_lambda, sinks=v_sinks, b_o=v_b_o, ln_g=v_ln_g, ln_b=v_ln_b,
                   meta_tokens=v_meta_tokens)
    g_small_pack = jnp.concatenate([small_g[n] for n in small_names] + [jnp.zeros((2, D), F32)], axis=0)
    small_res = _adamw(small_pack(w_small), g_small_pack, small_pack(m_small), small_pack(v_small),
                       name="adamw_small", tr=24)

    small_rows = {}
    r0 = 0
    for n in small_names:
        nrows = small_g[n].shape[0]
        small_rows[n] = (r0, nrows)
        r0 += nrows

    def small_out(packed, n, like):
        a, nrows = small_rows[n]
        flat = packed[a:a + nrows].reshape(-1)
        return flat[:like.size].reshape(like.shape)

    weights = dict(meta_tokens=meta_tokens, ln_emb_g=ln_emb_g, ln_emb_b=ln_emb_b, w_in=w_in, b_in=b_in, conv_w=conv_w,
                   conv_b=conv_b, w_ra=w_ra, b_ra=b_ra, w_ri=w_ri, b_ri=b_ri, lru_lambda=lru_lambda, sinks=sinks,
                   w_rnn_out=w_rnn_out, w_attn_out=w_attn_out, w_o=w_o, b_o=b_o, ln_g=ln_g, ln_b=ln_b)

    def outputs(which):
        return [big[n][which] if n in big else small_out(small_res[which], n, like) for n, like in weights.items()]

    return (loss, g_x[None], *outputs(0), *outputs(1), *outputs(2), *outputs(3))
```

```python
import jax
import jax.numpy as jnp
from jax import lax
from jax.experimental import pallas as pl
from jax.experimental.pallas import tpu as pltpu

F32 = jnp.float32
BF16 = jnp.bfloat16
_MXU = jnp.bfloat16

D = 2048
SEQ = 2048
N_META = 16
BLK = 128
PAD = BLK - N_META
TP = PAD + N_META + SEQ
NBLK = TP // BLK
HALF_TP = TP // 2
N_RB = 8
RB = 256
CONV_W = 4
LRU_C = 8.0
HD = 64
N_Q = 32
N_KV = 4
GRP = 8
D_KV = 256
NEG_INF = -1e30
LN_EPS = 1e-5
ALPHA = 2.0 ** 0.25
ROPE_THETA = 10000.0
OFF_GR, OFF_Q, OFF_K, OFF_V, OFF_GA, OFF_G = 2048, 4096, 6144, 6400, 6656, 8704
D_IN = 12800
N_SH = 4
W_IN_COLS = D_IN // N_SH
TN_IN = 640
PER_IN = W_IN_COLS // TN_IN
CW = 512
RT = TP // 4
SQ_ROWS = 512
PACK_ROWS = 3 * SQ_ROWS + 128

ADAM_LR = 0.001
ADAM_B1 = 0.9
ADAM_B2 = 0.999
ADAM_EPS = 1e-08
ADAM_WD = 0.01
ADAM_STEP = 10

MESH = pl.DeviceIdType.MESH
_MIB = 1024 * 1024
_ANY = pl.BlockSpec(memory_space=pl.ANY)
_NT = (((1,), (1,)), ((), ()))
_TN = (((0,), (0,)), ((), ()))


def _sds(shape, dtype):
    return jax.ShapeDtypeStruct(shape, dtype)


def _sigmoid(x):
    return 1.0 / (1.0 + jnp.exp(-x))


def _my_pos():
    return lax.axis_index("x"), lax.axis_index("y"), lax.axis_index("c")


def _other_chips(x, y):
    return [(1 - x, y), (x, 1 - y), (1 - x, 1 - y)]


def _remote(src, dst, send_sems, recv_sems, k, dev):
    return pltpu.make_async_remote_copy(src_ref=src, dst_ref=dst, send_sem=send_sems.at[k], recv_sem=recv_sems.at[k],
                                        device_id=dev, device_id_type=MESH)


class _Hook:
    def __init__(self, carried, landing, n_sems, start, finish, mid=None, mid_frac=0.5):
        self.carried, self.landing, self.n_sems, self.start, self.finish = list(carried), list(landing), n_sems, start, finish
        self.mid, self.mid_frac = mid, mid_frac


def _hook_gather(buf, mid_frac):
    half = buf.shape[1] // 2
    quarter = half // 2

    def geom(o, ss, rs):
        x, y, c = _my_pos()
        xn, yn, dg = _other_chips(x, y)
        slot = lambda p: 2 * p[0] + p[1]
        mine_rows = pl.ds(pl.multiple_of(c * half, 16), half)
        sib_rows = pl.ds(pl.multiple_of((1 - c) * half, 16), half)
        q_rows = lambda r: pl.ds(pl.multiple_of(c * half + r * quarter, 16), quarter)

        def cp(k, s, rows, dev):
            part = o.at[s, rows]
            return _remote(part, part, ss, rs, k, dev)

        return dict(
            direct=lambda k, s: cp(k, s, mine_rows, ((xn, yn)[k][0], (xn, yn)[k][1], c)),
            relay=lambda r, s: cp(2 + r, s, q_rows(r), ((yn, xn)[r][0], (yn, xn)[r][1], c)),
            sibling=lambda k, s, mine: cp(4 + k, s, mine_rows if mine else sib_rows, (x, y, 1 - c)),
            me=slot((x, y)), slots=(slot(xn), slot(yn), slot(dg)))

    def start(car, land, ss, rs):
        g = geom(car[0], ss, rs)
        g["direct"](0, g["me"]).start()
        g["direct"](1, g["me"]).start()

    def mid(car, land, ss, rs):
        g = geom(car[0], ss, rs)
        for k in range(2):
            g["direct"](k, g["slots"][k]).wait_recv()
            g["relay"](k, g["slots"][k]).start()
            g["sibling"](k, g["slots"][k], True).start()

    def finish(car, land, ss, rs):
        g = geom(car[0], ss, rs)
        dslot = g["slots"][2]
        g["relay"](0, dslot).wait_recv()
        g["relay"](1, dslot).wait_recv()
        g["sibling"](2, dslot, True).start()
        for k in range(3):
            g["sibling"](k, g["slots"][k], False).wait_recv()
        for k in range(2):
            g["direct"](k, g["me"]).wait_send()
            g["relay"](k, g["slots"][k]).wait_send()
        for k in range(3):
            g["sibling"](k, g["slots"][k], True).wait_send()

    return _Hook([buf], [], 7, start, finish, mid=mid, mid_frac=mid_frac)


def _hook_pair(g, half_rows, slots=(0, 1, 2, 3), land=None):
    n, r, cc = g.shape
    h = r // 2 if half_rows else r

    def plan(car, landing, ss, rs):
        x, y, c = _my_pos()
        dst = landing[0] if land is None else car[1]
        cps = []
        for k, s in enumerate(slots):
            src = car[0].at[s, pl.ds(pl.multiple_of((1 - c) * h, 8), h)] if half_rows else car[0].at[s]
            cps.append(_remote(src, dst.at[s], ss, rs, k, (x, y, 1 - c)))
        return cps

    def start(car, land, ss, rs):
        for cp in plan(car, land, ss, rs):
            cp.start()

    def finish(car, land, ss, rs):
        cps = plan(car, land, ss, rs)
        for cp in cps:
            cp.wait_recv()
        for cp in cps:
            cp.wait_send()

    if land is None:
        return _Hook([g], [_sds((n, h, cc), g.dtype)], len(slots), start, finish)
    return _Hook([g, land], [], len(slots), start, finish)


def _hook_scatter_direct(s16):
    _, h, cc = s16.shape
    q = h // 2

    def plan(car, land, ss, rs):
        x, y, c = _my_pos()
        xn, yn, dg = _other_chips(x, y)
        s, (final, raw) = car[0], land
        slot = lambda p: 2 * p[0] + p[1]
        q0, q1 = pl.ds(0, q), pl.ds(q, q)
        return [_remote(s.at[slot(xn), q0], final.at[0, q0], ss, rs, 0, (xn[0], xn[1], c)),
                _remote(s.at[slot(yn), q1], final.at[1, q1], ss, rs, 1, (yn[0], yn[1], c)),
                _remote(s.at[slot(dg), q0], raw.at[1], ss, rs, 2, (xn[0], xn[1], c)),
                _remote(s.at[slot(dg), q1], raw.at[0], ss, rs, 3, (yn[0], yn[1], c))]

    def start(car, land, ss, rs):
        for cp in plan(car, land, ss, rs):
            cp.start()

    def finish(car, land, ss, rs):
        cps = plan(car, land, ss, rs)
        for cp in cps:
            cp.wait_recv()
        for cp in cps:
            cp.wait_send()

    return _Hook([s16], [_sds((2, h, cc), s16.dtype), _sds((2, q, cc), s16.dtype)], 4, start, finish)


def _hook_scatter_relay(comb, final):
    _, q, _ = comb.shape

    def plan(car, ss, rs):
        x, y, c = _my_pos()
        xn, yn, _ = _other_chips(x, y)
        cb, final_ref = car
        return [_remote(cb.at[0], final_ref.at[0, pl.ds(q, q)], ss, rs, 0, (xn[0], xn[1], c)),
                _remote(cb.at[1], final_ref.at[1, pl.ds(0, q)], ss, rs, 1, (yn[0], yn[1], c))]

    def start(car, land, ss, rs):
        for cp in plan(car, ss, rs):
            cp.start()

    def finish(car, land, ss, rs):
        cps = plan(car, ss, rs)
        for cp in cps:
            cp.wait_recv()
        for cp in cps:
            cp.wait_send()

    return _Hook([comb, final], [], 2, start, finish)


def _hook_halves(full):
    h = full.shape[0] // 2

    def half_copy(car, ss, rs, which):
        x, y, c = _my_pos()
        rows = car[0].at[pl.ds(pl.multiple_of((c + which - 2 * c * which) * h, 8), h)]
        return _remote(rows, rows, ss, rs, 0, (x, y, 1 - c))

    def start(car, land, ss, rs):
        half_copy(car, ss, rs, 0).start()

    def finish(car, land, ss, rs):
        half_copy(car, ss, rs, 1).wait_recv()
        half_copy(car, ss, rs, 0).wait_send()

    return _Hook([full], [], 1, start, finish)


def _pcall(body, *, name, grid, in_specs, out_specs, out_shape, operands, scratch=(), vmem=48, sem=None,
           prefetch=(), aliases=None, hook=None):
    n_pre, n_in, n_out, n_scr = len(prefetch), len(in_specs), len(out_specs), len(scratch)
    in_specs, out_specs, out_shape, scratch = list(in_specs), list(out_specs), list(out_shape), list(scratch)
    io_alias = {n_pre + a: b for a, b in (aliases or {}).items()}
    operands = list(operands)
    kernel_body = body
    if hook is not None:
        n_car, n_land = len(hook.carried), len(hook.landing)
        for t, arr in enumerate(hook.carried):
            io_alias[n_pre + n_in + t] = n_out + t
        in_specs += [_ANY] * n_car
        out_specs += [_ANY] * (n_car + n_land)
        out_shape += [_sds(a.shape, a.dtype) for a in hook.carried] + hook.landing
        scratch += [pltpu.SemaphoreType.DMA((hook.n_sems,)), pltpu.SemaphoreType.DMA((hook.n_sems,))]
        operands += hook.carried
        sem = ("arbitrary",) * len(grid)

        def kernel_body(*refs):
            pre, rest = refs[:n_pre], refs[n_pre:]
            ins = rest[:n_in]
            outs = rest[n_in + n_car:n_in + n_car + n_out]
            car = rest[n_in + n_car + n_out:n_in + 2 * n_car + n_out]
            land = rest[n_in + 2 * n_car + n_out:n_in + 2 * n_car + n_out + n_land]
            scr = rest[n_in + 2 * n_car + n_out + n_land:]
            send_sems, recv_sems = scr[n_scr], scr[n_scr + 1]
            first = pl.program_id(0) == 0
            last = pl.program_id(0) == grid[0] - 1
            for d in range(1, len(grid)):
                first = first & (pl.program_id(d) == 0)
                last = last & (pl.program_id(d) == grid[d] - 1)

            @pl.when(first)
            def _():
                hook.start(car, land, send_sems, recv_sems)

            if hook.mid is not None:
                step = pl.program_id(0)
                total = grid[0]
                for d in range(1, len(grid)):
                    step = step * grid[d] + pl.program_id(d)
                    total *= grid[d]

                @pl.when(step == int(total * hook.mid_frac))
                def _():
                    hook.mid(car, land, send_sems, recv_sems)

            body(*pre, *ins, *outs, *scr[:n_scr])

            @pl.when(last)
            def _():
                hook.finish(car, land, send_sems, recv_sems)

    params = pltpu.CompilerParams(vmem_limit_bytes=vmem * _MIB, dimension_semantics=sem,
                                  has_side_effects=hook is not None)
    if n_pre:
        call = pl.pallas_call(
            kernel_body, name=name, out_shape=out_shape, input_output_aliases=io_alias, compiler_params=params,
            grid_spec=pltpu.PrefetchScalarGridSpec(num_scalar_prefetch=n_pre, grid=grid, in_specs=in_specs,
                                                   out_specs=out_specs, scratch_shapes=scratch))
    else:
        call = pl.pallas_call(kernel_body, name=name, grid=grid, in_specs=in_specs, out_specs=out_specs,
                              out_shape=out_shape, scratch_shapes=scratch, input_output_aliases=io_alias,
                              compiler_params=params)
    return call(*prefetch, *operands)


def _gather_small(blk):
    r, cc = blk.shape

    def body(x_ref, o_ref, send_sems, recv_sems):
        x, y, c = _my_pos()
        me = 2 * x + y
        o_ref[me] = x_ref[...]
        sends = [_remote(x_ref, o_ref.at[me], send_sems, recv_sems, k, (px, py, c))
                 for k, (px, py) in enumerate(_other_chips(x, y))]
        for cp in sends:
            cp.start()
        for k, (px, py) in enumerate(_other_chips(x, y)):
            _remote(x_ref, o_ref.at[2 * px + py], send_sems, recv_sems, k, (px, py, c)).wait_recv()
        for cp in sends:
            cp.wait_send()

    vm = pl.BlockSpec(memory_space=pltpu.VMEM)
    return pl.pallas_call(
        body, name="gather_small", in_specs=[vm], out_specs=vm, out_shape=_sds((N_SH, r, cc), blk.dtype),
        scratch_shapes=[pltpu.SemaphoreType.DMA((3,)), pltpu.SemaphoreType.DMA((3,))],
        compiler_params=pltpu.CompilerParams(has_side_effects=True),
    )(blk)


N_DEV = 8


def _allreduce_small(pack, halves):
    r, cc = pack.shape
    ride = _hook_halves(halves)

    def body(x_ref, h_in_ref, o_ref, h_ref, buf_ref, send_sems, recv_sems, h_ss, h_rs):
        del h_in_ref
        ride.start([h_ref], [], h_ss, h_rs)
        x, y, c = _my_pos()
        me = 4 * x + 2 * y + c
        buf_ref[me] = x_ref[...]
        cps = []
        for k in range(1, N_DEV):
            peer = (x ^ ((k >> 2) & 1), y ^ ((k >> 1) & 1), c ^ (k & 1))
            cps.append(_remote(x_ref, buf_ref.at[me], send_sems, recv_sems, k - 1, peer))
        for cp in cps:
            cp.start()
        for k in range(1, N_DEV):
            peer = (x ^ ((k >> 2) & 1), y ^ ((k >> 1) & 1), c ^ (k & 1))
            src = 4 * peer[0] + 2 * peer[1] + peer[2]
            _remote(x_ref, buf_ref.at[src], send_sems, recv_sems, k - 1, peer).wait_recv()
        acc = buf_ref[0]
        for d in range(1, N_DEV):
            acc = acc + buf_ref[d]
        o_ref[...] = acc
        for cp in cps:
            cp.wait_send()
        ride.finish([h_ref], [], h_ss, h_rs)

    vm = pl.BlockSpec(memory_space=pltpu.VMEM)
    return pl.pallas_call(
        body, name="allreduce_small", in_specs=[vm, _ANY], out_specs=[vm, _ANY],
        out_shape=[_sds((r, cc), F32), _sds(halves.shape, halves.dtype)], input_output_aliases={1: 1},
        scratch_shapes=[pltpu.VMEM((N_DEV, r, cc), F32), pltpu.SemaphoreType.DMA((N_DEV - 1,)),
                        pltpu.SemaphoreType.DMA((N_DEV - 1,)), pltpu.SemaphoreType.DMA((1,)), pltpu.SemaphoreType.DMA((1,))],
        compiler_params=pltpu.CompilerParams(has_side_effects=True),
    )(pack, halves)


def _mm_nn(a, w, bias, *, name, grid, tm, tn, k, a_index, w_block, w_index, out_cols, out_index, carried=None, hook=None):
    m = a.shape[0]

    def body(a_ref, w_ref, b_ref, *rest):
        o_ref = rest[-1]
        wv = w_ref[...]
        acc = jnp.dot(a_ref[...].astype(_MXU), wv.reshape(k, tn).astype(_MXU), preferred_element_type=F32)
        o_ref[...] = acc + b_ref[...]

    operands = [a, w, bias] + ([carried] if carried is not None else [])
    return _pcall(
        body, name=name, grid=grid,
        in_specs=[pl.BlockSpec((tm, k), a_index), pl.BlockSpec(w_block, w_index),
                  pl.BlockSpec((1, tn), lambda i, j: (0, j))] + ([_ANY] if carried is not None else []),
        out_specs=[pl.BlockSpec((tm, tn), out_index)], out_shape=[_sds((m, out_cols), F32)], operands=operands,
        vmem=56, sem=("parallel", "parallel"), aliases={3: 0} if carried is not None else None, hook=hook)


def _mm_nt(a, w, *, name, grid, tm, tn, tk, a_index, w_block, w_index, out_cols, hook=None):
    m = a.shape[0]
    nk = grid[2]

    def body(a_ref, w_ref, o_ref, acc_ref):
        part = lax.dot_general(a_ref[...].astype(_MXU), w_ref[...].astype(_MXU), _NT, preferred_element_type=F32)
        if nk == 1:
            o_ref[...] = part
        else:
            kidx = pl.program_id(2)

            @pl.when(kidx == 0)
            def _():
                acc_ref[...] = part

            @pl.when(kidx > 0)
            def _():
                acc_ref[...] += part

            @pl.when(kidx == nk - 1)
            def _():
                o_ref[...] = acc_ref[...]

    return _pcall(
        body, name=name, grid=grid, in_specs=[pl.BlockSpec((tm, tk), a_index), pl.BlockSpec(w_block, w_index)],
        out_specs=[pl.BlockSpec((tm, tn), lambda i, j, q: (i, j))], out_shape=[_sds((m, out_cols), F32)],
        operands=[a, w], scratch=[pltpu.VMEM((tm, tn) if nk > 1 else (8, 128), F32)], vmem=60,
        sem=("parallel", "parallel", "arbitrary"), hook=hook)


def _mm_tn(at, b, *, name, grid, tm, tn, a_index, b_index, out_shape, out_block, out_index, carried=None,
           prefetch=(), hook=None, out_dtype=F32, colsum=False, a_transposed=True):
    t = at.shape[1] if a_transposed else at.shape[0]
    n_pre = len(prefetch)

    def body(*refs):
        a_ref, b_ref = refs[n_pre], refs[n_pre + 1]
        bv = b_ref[...]
        o_ref = refs[-2] if colsum else refs[-1]
        acc = lax.dot_general(a_ref[...].astype(_MXU), bv.astype(_MXU), (((1,), (0,)), ((), ())) if a_transposed else _TN,
                              preferred_element_type=F32)
        o_ref[...] = acc.astype(out_dtype)
        if colsum:
            refs[-1][...] = jnp.sum(bv.astype(F32), axis=0, keepdims=True)

    operands = [at, b] + ([carried] if carried is not None else [])
    out_specs = [pl.BlockSpec(out_block, out_index)]
    out_shapes = [_sds(out_shape, out_dtype)]
    if colsum:
        out_specs.append(pl.BlockSpec((1, tn), b_index))
        out_shapes.append(_sds((1, b.shape[1]), F32))
    return _pcall(
        body, name=name, grid=grid,
        in_specs=[pl.BlockSpec((tm, t) if a_transposed else (t, tm), a_index), pl.BlockSpec((t, tn), b_index)]
                 + ([_ANY] if carried is not None else []),
        out_specs=out_specs, out_shape=out_shapes, operands=operands,
        vmem=56, sem=("parallel", "parallel"), aliases={2: 0} if carried is not None else None, prefetch=prefetch, hook=hook)


def _remote_tile(n):
    near = 2 * PER_IN
    if isinstance(n, int):
        return (n % 2, n // 2) if n < near else (2, n - near)
    return jnp.where(n < near, n % 2, 2), jnp.where(n < near, n // 2, n - near)


def _mm_z_gather(h16, w_own, b_in, order):
    n_tiles = N_SH * PER_IN
    n_remote = 3 * PER_IN
    half = D // 2

    def body(order_ref, a_ref, b_ref, w_in_ref, z_ref, w_ref, wbuf, tile_sems, ss, rs):
        del w_in_ref
        j = pl.program_id(0)
        x, y, c = _my_pos()
        me = 2 * x + y
        chips = _other_chips(x, y)
        mine_rows = pl.ds(pl.multiple_of(c * half, 16), half)
        sib_rows = pl.ds(pl.multiple_of((1 - c) * half, 16), half)

        slots = [2 * px + py for px, py in chips]
        cols = lambda t: pl.ds(t * TN_IN, TN_IN)
        q_rows = lambda r: pl.ds(pl.multiple_of(c * half + r * (half // 2), 16), half // 2)

        def direct(rel, t, slot):
            px, py = chips[rel]
            part = w_ref.at[slot, mine_rows, cols(t)]
            return _remote(part, part, ss, rs, 2 * t + rel, (px, py, c))

        def relay(r, t, slot):
            px, py = chips[1 - r]
            part = w_ref.at[slot, q_rows(r), cols(t)]
            return _remote(part, part, ss, rs, 2 * PER_IN + 2 * t + r, (px, py, c))

        def d2d(n, rows):
            rel, t = _remote_tile(n)
            part = w_ref.at[slots[rel], rows, cols(t)]
            return _remote(part, part, ss, rs, 4 * PER_IN + n, (x, y, 1 - c))

        def tile_copy(step):
            rel, t = _remote_tile(jnp.maximum(step - PER_IN, 0))
            slot = jnp.where(step < PER_IN, me, me ^ jnp.where(rel == 0, 2, jnp.where(rel == 1, 1, 3)))
            col = pl.multiple_of(jnp.where(step < PER_IN, step, t) * TN_IN, 128)
            return pltpu.make_async_copy(w_ref.at[slot, :, pl.ds(col, TN_IN)], wbuf.at[step % 2], tile_sems.at[step % 2])

        @pl.when(j == 0)
        def _():
            for t in range(PER_IN):
                direct(0, t, me).start()
                direct(1, t, me).start()
            tile_copy(0).start()

        for n in range(n_remote):
            rel, t = _remote_tile(n)

            @pl.when(j == n + PER_IN - 3)
            def _():
                if rel < 2:
                    direct(rel, t, slots[rel]).wait_recv()
                    relay(rel, t, slots[rel]).start()
                else:
                    relay(0, t, slots[2]).wait_recv()
                    relay(1, t, slots[2]).wait_recv()
                d2d(n, mine_rows).start()

            @pl.when(j == n + PER_IN - 2)
            def _():
                d2d(n, sib_rows).wait_recv()

        @pl.when(j + 1 < n_tiles)
        def _():
            tile_copy(j + 1).start()

        tile_copy(j).wait()
        z_ref[...] = jnp.dot(a_ref[...], wbuf[j % 2], preferred_element_type=F32) + b_ref[...]

        @pl.when(j == n_tiles - 1)
        def _():
            for t in range(PER_IN):
                for r in range(2):
                    direct(r, t, me).wait_send()
                    relay(r, t, slots[r]).wait_send()
            for n in range(n_remote):
                d2d(n, mine_rows).wait_send()

    def col_tile(j, o):
        rel, t = _remote_tile(jnp.maximum(j - PER_IN, 0))
        return 0, jnp.where(j < PER_IN, o[0] * PER_IN + j, o[1 + rel] * PER_IN + t)

    return pl.pallas_call(
        body, name="mm_z_gather",
        grid_spec=pltpu.PrefetchScalarGridSpec(
            num_scalar_prefetch=1, grid=(n_tiles,),
            in_specs=[pl.BlockSpec((TP, D), lambda j, o: (0, 0)), pl.BlockSpec((1, TN_IN), col_tile), _ANY],
            out_specs=[pl.BlockSpec((TP, TN_IN), col_tile), _ANY],
            scratch_shapes=[pltpu.VMEM((2, D, TN_IN), BF16), pltpu.SemaphoreType.DMA((2,)),
                            pltpu.SemaphoreType.DMA((4 * PER_IN + n_remote,)),
                            pltpu.SemaphoreType.DMA((4 * PER_IN + n_remote,))]),
        out_shape=[_sds((TP, D_IN), F32), _sds(w_own.shape, w_own.dtype)],
        input_output_aliases={3: 1},
        compiler_params=pltpu.CompilerParams(vmem_limit_bytes=60 * _MIB, dimension_semantics=("arbitrary",),
                                             has_side_effects=True),
    )(order, h16, b_in, w_own)


def _padded_rows(i, x_ref, meta_ref):
    head = jnp.concatenate([jnp.zeros((PAD, D), F32), meta_ref[...]], axis=0)
    return jnp.where(i == 0, head, x_ref[...])


def _stream_specs():
    return [pl.BlockSpec((BLK, D), lambda i: (jnp.maximum(i - 1, 0), 0)), pl.BlockSpec((N_META, D), lambda i: (0, 0))]


def _ln_emb_fwd(x, meta, g, b):
    def body(x_ref, meta_ref, g_ref, b_ref, h32_ref, h16_ref):
        x = _padded_rows(pl.program_id(0), x_ref, meta_ref)
        mu = jnp.mean(x, axis=-1, keepdims=True)
        xc = x - mu
        var = jnp.mean(xc * xc, axis=-1, keepdims=True)
        y = xc * lax.rsqrt(var + LN_EPS) * g_ref[...] + b_ref[...]
        h32_ref[...] = y
        h16_ref[...] = y.astype(_MXU)

    row = pl.BlockSpec((BLK, D), lambda i: (i, 0))
    vec = pl.BlockSpec((1, D), lambda i: (0, 0))
    return _pcall(body, name="ln_emb_fwd", grid=(NBLK,), in_specs=_stream_specs() + [vec, vec],
                  out_specs=[row, row],
                  out_shape=[_sds((TP, D), F32), _sds((TP, D), _MXU)], operands=[x, meta, g, b],
                  vmem=32, sem=("parallel",))


def _adamw_step(w_ref, g_ref, m_ref, v_ref, go_ref, d_ref, mo_ref, vo_ref):
    gg = g_ref[...]
    go_ref[...] = gg
    m_new = ADAM_B1 * m_ref[...] + (1.0 - ADAM_B1) * gg
    v_new = ADAM_B2 * v_ref[...] + (1.0 - ADAM_B2) * (gg * gg)
    m_hat = m_new / (1.0 - ADAM_B1 ** ADAM_STEP)
    v_hat = v_new / (1.0 - ADAM_B2 ** ADAM_STEP)
    d_ref[...] = -ADAM_LR * (m_hat / (jnp.sqrt(v_hat) + ADAM_EPS) + ADAM_WD * w_ref[...])
    mo_ref[...] = m_new
    vo_ref[...] = v_new


def _ln_emb_bwd(x, meta, g, dr, dhz, g_sq, sq, *, hook):
    n_sq = len(sq)
    sq_steps = 16
    sq_rows = SQ_ROWS // sq_steps

    def body(x_ref, meta_ref, g_ref, dr_ref, dhz_ref, *rest):
        sq_in = rest[:4 * n_sq]
        dx_ref, dmeta_ref, acc_ref = rest[4 * n_sq:4 * n_sq + 3]
        sq_out = rest[4 * n_sq + 3:]
        i = pl.program_id(0)

        @pl.when(i < sq_steps)
        def _():
            for kk in range(n_sq):
                _adamw_step(*sq_in[4 * kk:4 * kk + 4], *sq_out[4 * kk:4 * kk + 4])


        @pl.when(i == 0)
        def _():
            acc_ref[...] = jnp.zeros_like(acc_ref)

        x = _padded_rows(i, x_ref, meta_ref)
        mu = jnp.mean(x, axis=-1, keepdims=True)
        xc = x - mu
        var = jnp.mean(xc * xc, axis=-1, keepdims=True)
        rstd = lax.rsqrt(var + LN_EPS)
        xhat = xc * rstd
        dh = ALPHA * dr_ref[...] + dhz_ref[...]
        acc_ref[0:1, :] += jnp.sum(dh * xhat, axis=0, keepdims=True)
        acc_ref[1:2, :] += jnp.sum(dh, axis=0, keepdims=True)
        dxh = dh * g_ref[...]
        m1 = jnp.mean(dxh, axis=-1, keepdims=True)
        m2 = jnp.mean(dxh * xhat, axis=-1, keepdims=True)
        dx = rstd * (dxh - m1 - xhat * m2)
        dx_ref[...] = dx

        @pl.when(i == 0)
        def _():
            dmeta_ref[...] = dx[PAD:BLK]

    row = pl.BlockSpec((BLK, D), lambda i: (i, 0))
    vec = pl.BlockSpec((1, D), lambda i: (0, 0))
    xs, ms = _stream_specs()
    sq_blk = pl.BlockSpec((sq_rows, D), lambda i: (jnp.minimum(i, sq_steps - 1), 0))
    sq_specs, sq_operands = [], []
    for kk, (w_, m_, v_) in enumerate(sq):
        g_blk = pl.BlockSpec((sq_rows, D), lambda i, kk=kk: (sq_steps * kk + jnp.minimum(i, sq_steps - 1), 0))
        sq_specs += [sq_blk, g_blk, sq_blk, sq_blk]
        sq_operands += [w_, g_sq, m_, v_]
    return _pcall(body, name="ln_emb_bwd", grid=(NBLK,), in_specs=[xs, ms, vec, row, row] + sq_specs,
                  out_specs=[xs, ms, pl.BlockSpec((8, D), lambda i: (0, 0))] + [sq_blk] * (4 * n_sq),
                  out_shape=[_sds((SEQ, D), F32), _sds((N_META, D), F32), _sds((8, D), F32)]
                            + [_sds((SQ_ROWS, D), F32)] * (4 * n_sq),
                  operands=[x, meta, g, dr, dhz] + sq_operands, vmem=40, sem=("arbitrary",), hook=hook)


def _mul_silu_bwd(dy, a, z, off, dz, *, name, hook=None):
    def body(dy_ref, a_ref, z_ref, dz_in, da_ref, dg_ref):
        zz = z_ref[...]
        sg = _sigmoid(zz)
        d = dy_ref[...]
        da_ref[...] = d * (zz * sg)
        dg_ref[...] = (d * a_ref[...] * (sg * (1.0 + zz * (1.0 - sg)))).astype(_MXU)

    blk = pl.BlockSpec((RT, CW), lambda i, j: (i, j))
    zblk = pl.BlockSpec((RT, CW), lambda i, j: (i, off // CW + j))
    return _pcall(body, name=name, grid=(TP // RT, D // CW), in_specs=[blk, blk, zblk, _ANY], out_specs=[blk, zblk],
                  out_shape=[_sds((TP, D), F32), _sds((TP, D_IN), _MXU)], operands=[dy, a, z, dz], vmem=32,
                  sem=("parallel", "parallel"), aliases={3: 1}, hook=hook)


def _merge_fwd(y2, z):
    w = 256

    def body(ya_ref, yb_ref, ga_ref, gb_ref, o_ref):
        y = _sigmoid(ga_ref[...]) * ya_ref[...] + _sigmoid(gb_ref[...]) * yb_ref[...]
        o_ref[...] = y.astype(_MXU)

    nb = D // w
    strip = pl.BlockSpec((TP, w), lambda j: (0, j))
    return _pcall(body, name="merge_fwd", grid=(nb,),
                  in_specs=[strip, pl.BlockSpec((TP, w), lambda j: (0, nb + j)),
                            pl.BlockSpec((TP, w), lambda j: (0, OFF_G // w + j)),
                            pl.BlockSpec((TP, w), lambda j: (0, (OFF_G + D) // w + j))],
                  out_specs=[strip], out_shape=[_sds((TP, D), _MXU)], operands=[y2, y2, z, z], vmem=56,
                  sem=("parallel",))[0]


def _merge_bwd(dmix, y2, z):
    nb = D // CW

    def body(dm_ref, y_ref, g_ref, dy_ref, dg_ref):
        dm = dm_ref[...]
        sg = _sigmoid(g_ref[...])
        dy_ref[...] = (dm * sg).astype(_MXU)
        dg_ref[...] = (dm * y_ref[...] * sg * (1.0 - sg)).astype(_MXU)

    blk = pl.BlockSpec((RT, CW), lambda i, j: (i, j))
    gblk = pl.BlockSpec((RT, CW), lambda i, j: (i, OFF_G // CW + j))
    return _pcall(body, name="merge_bwd", grid=(TP // RT, 2 * nb),
                  in_specs=[pl.BlockSpec((RT, CW), lambda i, j: (i, j % nb)), blk, gblk], out_specs=[blk, gblk],
                  out_shape=[_sds((TP, 2 * D), _MXU), _sds((TP, D_IN), _MXU)], operands=[dmix, y2, z], vmem=32,
                  sem=("parallel", "parallel"))


def _ln_out_loss(h32, out, target, g, b):
    def body(h_ref, o_ref, t_ref, g_ref, b_ref, dr_ref, dr16_ref, acc_ref):
        i = pl.program_id(0)

        @pl.when(i == 0)
        def _():
            acc_ref[...] = jnp.zeros_like(acc_ref)

        r = ALPHA * h_ref[...] + o_ref[...]
        mu = jnp.mean(r, axis=-1, keepdims=True)
        rc = r - mu
        var = jnp.mean(rc * rc, axis=-1, keepdims=True)
        rstd = lax.rsqrt(var + LN_EPS)
        xhat = rc * rstd
        gg = g_ref[...]
        y = xhat * gg + b_ref[...]
        real = (i >= 1).astype(F32)
        diff = (y - t_ref[...]) * real
        dy = diff * (1.0 / D)
        dxh = dy * gg
        m1 = jnp.mean(dxh, axis=-1, keepdims=True)
        m2 = jnp.mean(dxh * xhat, axis=-1, keepdims=True)
        dr = rstd * (dxh - m1 - xhat * m2)
        dr_ref[...] = dr
        dr16_ref[...] = dr.astype(_MXU)
        acc_ref[0:1, :] += jnp.sum(dy * xhat, axis=0, keepdims=True)
        acc_ref[1:2, :] += jnp.sum(dy, axis=0, keepdims=True)
        acc_ref[2:3, :] += jnp.sum(dr, axis=0, keepdims=True)
        acc_ref[3:4, :] += (0.5 / D) * jnp.sum(diff * diff)

    row = pl.BlockSpec((BLK, D), lambda i: (i, 0))
    vec = pl.BlockSpec((1, D), lambda i: (0, 0))
    return _pcall(body, name="ln_out_loss", grid=(NBLK,),
                  in_specs=[row, row, pl.BlockSpec((BLK, D), lambda i: (jnp.maximum(i - 1, 0), 0)), vec, vec],
                  out_specs=[row, row, pl.BlockSpec((8, D), lambda i: (0, 0))],
                  out_shape=[_sds((TP, D), F32), _sds((TP, D), _MXU), _sds((8, D), F32)],
                  operands=[h32, out, target, g, b], vmem=32, sem=("arbitrary",))


def _rnn_recompute(xr_ref, cw_ref, cb_ref, wra_ref, wri_ref, bra_ref, bri_ref, lam_ref):
    rows = lax.broadcasted_iota(jnp.int32, (TP, 1), 0)
    valid = (rows >= PAD).astype(F32)
    first = rows == PAD
    x = xr_ref[...] * valid
    cw = cw_ref[...]
    shifted = [x, pltpu.roll(x, 1, 0), pltpu.roll(x, 2, 0), pltpu.roll(x, 3, 0)]
    c = cb_ref[...] + cw[0:1, :] * shifted[0] + cw[1:2, :] * shifted[1] + cw[2:3, :] * shifted[2] + cw[3:4, :] * shifted[3]
    cm = c.astype(_MXU)
    gr = _sigmoid(jnp.dot(cm, wra_ref[...].astype(_MXU), preferred_element_type=F32) + bra_ref[...])
    gi = _sigmoid(jnp.dot(cm, wri_ref[...].astype(_MXU), preferred_element_type=F32) + bri_ref[...])
    lam = lam_ref[...]
    ls = jnp.minimum(lam, 0.0) - jnp.log(1.0 + jnp.exp(-jnp.abs(lam)))
    log_a = LRU_C * gr * ls
    a = jnp.exp(log_a)
    mult = jnp.where(first, 1.0, jnp.sqrt(1.0 - jnp.exp(2.0 * log_a)))
    return dict(valid=valid, first=first, shifted=shifted, c=c, cm=cm, gr=gr, gi=gi, ls=ls, a=a, mult=mult, lam=lam)


def _rnn_specs():
    col = pl.BlockSpec((TP, RB), lambda n: (0, n))
    vec = pl.BlockSpec((1, RB), lambda n: (0, n))
    return dict(col=col, vec=vec, cw=pl.BlockSpec((CONV_W, RB), lambda n: (0, n)),
                wblk=pl.BlockSpec((None, RB, RB), lambda n: (n, 0, 0)))


def _rnn_gates_fwd(z, conv_w, conv_b, w_ra, w_ri, b_ra, b_ri, lam):
    def body(xr_ref, cw_ref, cb_ref, wra_ref, wri_ref, bra_ref, bri_ref, lam_ref, a_ref, u_ref):
        r = _rnn_recompute(xr_ref, cw_ref, cb_ref, wra_ref, wri_ref, bra_ref, bri_ref, lam_ref)
        a_ref[...] = r["a"]
        u_ref[...] = r["mult"] * r["gi"] * r["c"] * r["valid"]

    s = _rnn_specs()
    return _pcall(body, name="rnn_gates_fwd", grid=(N_RB,),
                  in_specs=[s["col"], s["cw"], s["vec"], s["wblk"], s["wblk"], s["vec"], s["vec"], s["vec"]],
                  out_specs=[s["col"], s["col"]], out_shape=[_sds((TP, D), F32)] * 2,
                  operands=[z, conv_w, conv_b, w_ra, w_ri, b_ra, b_ri, lam], vmem=56, sem=("parallel",))


SCAN_ROWS = 272


SUB = 8


def _tile_scan(a, u, reverse):
    rows = lax.broadcasted_iota(jnp.int32, a.shape, 0)
    for d in (1, 2, 4):
        shift = SUB - d if reverse else d
        inside = (rows < SUB - d) if reverse else (rows >= d)
        u = u + a * jnp.where(inside, pltpu.roll(u, shift, 0), 0.0)
        a = a * jnp.where(inside, pltpu.roll(a, shift, 0), 1.0)
    return a, u


def _scan_fwd(a, u, z, *, hook):
    def body(a_ref, u_ref, gr_ref, h_ref, y_ref, carry_ref):
        @pl.when(pl.program_id(1) == 0)
        def _():
            carry_ref[...] = jnp.zeros_like(carry_ref)

        def step(r, h):
            rows = pl.ds(pl.multiple_of(r * SUB, SUB), SUB)
            prod, part = _tile_scan(a_ref[rows, :], u_ref[rows, :], False)
            ht = part + prod * h
            h_ref[rows, :] = ht
            return ht[SUB - 1:SUB, :]

        carry_ref[...] = lax.fori_loop(0, SCAN_ROWS // SUB, step, carry_ref[...], unroll=2)
        zz = gr_ref[...]
        y_ref[...] = (h_ref[...] * (zz * _sigmoid(zz))).astype(_MXU)

    blk = pl.BlockSpec((SCAN_ROWS, CW), lambda j, i: (i, j))
    return _pcall(body, name="scan_fwd", grid=(D // CW, TP // SCAN_ROWS),
                  in_specs=[blk, blk, pl.BlockSpec((SCAN_ROWS, CW), lambda j, i: (i, OFF_GR // CW + j))],
                  out_specs=[blk, blk], out_shape=[_sds((TP, D), F32), _sds((TP, D), _MXU)], operands=[a, u, z],
                  scratch=[pltpu.VMEM((1, CW), F32)], vmem=32, sem=("parallel", "arbitrary"), hook=hook)


def _scan_bwd(a, dh):
    nst = TP // SCAN_ROWS
    n_tiles = SCAN_ROWS // SUB

    def body(a_ref, d_ref, o_ref, lam_ref, anext_ref):
        @pl.when(pl.program_id(1) == 0)
        def _():
            lam_ref[...] = jnp.zeros_like(lam_ref)
            anext_ref[...] = jnp.zeros_like(anext_ref)

        def step(q, carry):
            lam_next, a_next = carry
            rows = pl.ds(pl.multiple_of((n_tiles - 1 - q) * SUB, SUB), SUB)
            at = a_ref[rows, :]
            last = lax.broadcasted_iota(jnp.int32, at.shape, 0) == SUB - 1
            b = jnp.where(last, a_next, pltpu.roll(at, SUB - 1, 0))
            prod, part = _tile_scan(b, d_ref[rows, :], True)
            lam = part + prod * lam_next
            o_ref[rows, :] = lam
            return lam[0:1, :], at[0:1, :]

        lam, an = lax.fori_loop(0, n_tiles, step, (lam_ref[...], anext_ref[...]), unroll=2)
        lam_ref[...] = lam
        anext_ref[...] = an

    blk = pl.BlockSpec((SCAN_ROWS, CW), lambda j, i: (nst - 1 - i, j))
    return _pcall(body, name="scan_bwd", grid=(D // CW, nst), in_specs=[blk, blk], out_specs=[blk],
                  out_shape=[_sds((TP, D), F32)], operands=[a, dh],
                  scratch=[pltpu.VMEM((1, CW), F32), pltpu.VMEM((1, CW), F32)], vmem=32,
                  sem=("parallel", "arbitrary"))[0]


def _rnn_gates_bwd(z, lam_s, hr, conv_w, conv_b, w_ra, w_ri, b_ra, b_ri, lam, dz, gsq):
    def body(xr_ref, ls_ref, hr_ref, cw_ref, cb_ref, wra_ref, wri_ref, bra_ref, bri_ref, lam_ref, dz_in, gsq_in,
             dx_ref, dw_ref, sums_ref):
        r = _rnn_recompute(xr_ref, cw_ref, cb_ref, wra_ref, wri_ref, bra_ref, bri_ref, lam_ref)
        valid, c, gr, gi, a, mult = r["valid"], r["c"], r["gr"], r["gi"], r["a"], r["mult"]
        du = ls_ref[...] * valid
        da = du * pltpu.roll(hr_ref[...], 1, 0)
        d_gi = du * mult * c
        dc = du * mult * gi
        dmult = du * gi * c
        dlog_a = da * a + jnp.where(r["first"], 0.0, -dmult * a * a / mult)
        d_gr = dlog_a * (LRU_C * r["ls"])
        dls = jnp.sum(dlog_a * (LRU_C * gr), axis=0, keepdims=True)
        dpre_r = d_gr * gr * (1.0 - gr)
        dpre_i = d_gi * gi * (1.0 - gi)
        pr = dpre_r.astype(_MXU)
        pi = dpre_i.astype(_MXU)
        dwra = lax.dot_general(r["cm"], pr, _TN, preferred_element_type=F32)
        dwri = lax.dot_general(r["cm"], pi, _TN, preferred_element_type=F32)
        for s in range(N_SH):
            dw_ref[s, 0:64, :] = dwra[64 * s:64 * (s + 1)]
            dw_ref[s, 64:128, :] = dwri[64 * s:64 * (s + 1)]
        dc = dc + lax.dot_general(pr, wra_ref[...].astype(_MXU), _NT, preferred_element_type=F32)
        dc = dc + lax.dot_general(pi, wri_ref[...].astype(_MXU), _NT, preferred_element_type=F32)
        cw = cw_ref[...]
        dx = cw[0:1, :] * dc
        for k in range(1, CONV_W):
            dx = dx + cw[k:k + 1, :] * pltpu.roll(dc, TP - k, 0)
        dx_ref[...] = (dx * valid).astype(_MXU)
        for k in range(CONV_W):
            sums_ref[k:k + 1, :] = jnp.sum(dc * r["shifted"][k], axis=0, keepdims=True)
        sums_ref[4:5, :] = jnp.sum(dc, axis=0, keepdims=True)
        sums_ref[5:6, :] = jnp.sum(dpre_r, axis=0, keepdims=True)
        sums_ref[6:7, :] = jnp.sum(dpre_i, axis=0, keepdims=True)
        sums_ref[7:8, :] = dls * _sigmoid(-r["lam"])

    s = _rnn_specs()
    return _pcall(
        body, name="rnn_gates_bwd", grid=(N_RB,),
        in_specs=[s["col"], s["col"], s["col"], s["cw"], s["vec"], s["wblk"], s["wblk"], s["vec"], s["vec"], s["vec"],
                  _ANY, _ANY],
        out_specs=[s["col"], pl.BlockSpec((N_SH, 128, RB), lambda n: (0, 3 * SQ_ROWS // 128, n)),
                   pl.BlockSpec((8, RB), lambda n: (0, n))],
        out_shape=[_sds((TP, D_IN), _MXU), _sds((N_SH, PACK_ROWS, D), F32), _sds((8, D), F32)],
        operands=[z, lam_s, hr, conv_w, conv_b, w_ra, w_ri, b_ra, b_ri, lam, dz, gsq], vmem=60, sem=("parallel",),
        aliases={10: 0, 11: 1})


def _rope_tables():
    half = HD // 2
    inv = ROPE_THETA ** (-jnp.arange(half, dtype=F32) / half)
    pos = (jnp.arange(TP) - PAD).astype(F32)
    ang = pos[:, None] * inv[None, :]
    return jnp.tile(jnp.cos(ang), (1, 4)), jnp.tile(jnp.sin(ang), (1, 4))


def _rope(x, cos_t, sin_t, sign):
    w = x.shape[1]
    lane = lax.broadcasted_iota(jnp.int32, x.shape, 1)
    first = (lane % HD) < (HD // 2)
    swapped = jnp.where(first, pltpu.roll(x, w - HD // 2, 1), pltpu.roll(x, HD // 2, 1))
    ct = jnp.tile(cos_t, (1, w // 128))
    st = jnp.tile(sin_t, (1, w // 128))
    return x * ct + swapped * jnp.where(first, -sign * st, sign * st)


def _rope_fwd(z, cos_t, sin_t):
    def body(q_ref, k_ref, v_ref, c_ref, s_ref, qo_ref, ko_ref, vo_ref):
        c = c_ref[...]
        s = s_ref[...]
        qo_ref[...] = _rope(q_ref[...], c, s, 1.0).astype(_MXU)
        ko_ref[...] = _rope(k_ref[...], c, s, 1.0).astype(_MXU)
        vo_ref[...] = v_ref[...].astype(_MXU)

    tab = pl.BlockSpec((BLK, 128), lambda i: (i, 0))
    kv = pl.BlockSpec((BLK, D_KV), lambda i: (i, 0))
    return _pcall(body, name="rope_fwd", grid=(NBLK,),
                  in_specs=[pl.BlockSpec((BLK, D), lambda i: (i, OFF_Q // D)),
                            pl.BlockSpec((BLK, D_KV), lambda i: (i, OFF_K // D_KV)),
                            pl.BlockSpec((BLK, D_KV), lambda i: (i, OFF_V // D_KV)), tab, tab],
                  out_specs=[pl.BlockSpec((BLK, D), lambda i: (i, 0)), kv, kv],
                  out_shape=[_sds((TP, D), _MXU), _sds((TP, D_KV), _MXU), _sds((TP, D_KV), _MXU)],
                  operands=[z, z, z, cos_t, sin_t], vmem=32, sem=("parallel",))


def _rope_bwd_k(dk, dv, cos_t, sin_t, dz):
    def body(dk_ref, dv_ref, c_ref, s_ref, dz_in, o_ref):
        o_ref[:, 0:D_KV] = _rope(dk_ref[...], c_ref[...], s_ref[...], -1.0).astype(_MXU)
        o_ref[:, D_KV:2 * D_KV] = dv_ref[...].astype(_MXU)

    tab = pl.BlockSpec((BLK, 128), lambda i: (i, 0))
    kv = pl.BlockSpec((BLK, D_KV), lambda i: (i, 0))
    return _pcall(body, name="rope_bwd_k", grid=(NBLK,), in_specs=[kv, kv, tab, tab, _ANY],
                  out_specs=[pl.BlockSpec((BLK, 2 * D_KV), lambda i: (i, OFF_K // (2 * D_KV)))],
                  out_shape=[_sds((TP, D_IN), _MXU)], operands=[dk, dv, cos_t, sin_t, dz], vmem=32, sem=("parallel",),
                  aliases={4: 0})[0]


def _attn_mask(i):
    ql = lax.broadcasted_iota(jnp.int32, (BLK, 3 * BLK), 0)
    kk = lax.broadcasted_iota(jnp.int32, (BLK, 3 * BLK), 1)
    kl = kk % BLK
    part = kk // BLK
    meta = (part == 0) & (kl >= PAD) & ((i >= 1) | (kl <= ql))
    prev = (part == 1) & (i >= 2) & (kl > ql)
    cur = (part == 2) & (i >= 1) & (kl <= ql)
    return meta | prev | cur


def _kv_specs():
    return [pl.BlockSpec((BLK, D_KV), lambda i: (0, 0)),
            pl.BlockSpec((BLK, D_KV), lambda i: (jnp.maximum(i - 1, 0), 0)),
            pl.BlockSpec((BLK, D_KV), lambda i: (i, 0))]


def _pair_heads(ref, gp):
    def two(j):
        h0 = 2 * GRP * gp + j
        return jnp.concatenate([ref[:, HD * h0:HD * (h0 + 1)], ref[:, HD * (h0 + GRP):HD * (h0 + GRP + 1)]], axis=1)

    return jnp.concatenate([two(j) for j in range(GRP)], axis=0)


def _pair_kv(refs, gp):
    both = jnp.concatenate([r[:, 2 * HD * gp:2 * HD * (gp + 1)] for r in refs], axis=0)
    low_lanes = lax.broadcasted_iota(jnp.int32, both.shape, 1) < HD
    zero = jnp.zeros_like(both)
    return jnp.concatenate([jnp.where(low_lanes, both, zero), jnp.where(low_lanes, zero, both)], axis=0)


def _attn_fwd(q, k, v, sinks, z, *, hook):
    nk = 3 * BLK
    n_gate = D // CW

    def body(q_ref, k0_ref, kp_ref, kc_ref, v0_ref, vp_ref, vc_ref, sink_ref, *rest):
        gate_refs, (o_ref, lse_ref, y_ref, yt_ref, s_scr, p_scr) = rest[:n_gate], rest[n_gate:]
        mask = _attn_mask(pl.program_id(0))
        for gp in range(N_KV // 2):
            s_scr[...] = lax.dot_general(_pair_heads(q_ref, gp), _pair_kv((k0_ref, kp_ref, kc_ref), gp), _NT,
                                         preferred_element_type=F32)
            for a in range(2):
                for j in range(GRP):
                    h = GRP * (2 * gp + a) + j
                    rows, cols = slice(BLK * j, BLK * (j + 1)), slice(nk * a, nk * (a + 1))
                    sink = sink_ref[h]
                    s = jnp.where(mask, s_scr[rows, cols] * (HD ** -0.5), NEG_INF)
                    mx = jnp.maximum(jnp.max(s, -1, keepdims=True), sink)
                    p = jnp.exp(s - mx)
                    den = jnp.sum(p, -1, keepdims=True) + jnp.exp(sink - mx)
                    p_scr[rows, cols] = (p * (1.0 / den)).astype(_MXU)
                    lse_ref[:, h:h + 1] = mx + jnp.log(den)
            o2 = jnp.dot(p_scr[...], _pair_kv((v0_ref, vp_ref, vc_ref), gp), preferred_element_type=F32)
            for a in range(2):
                for j in range(GRP):
                    h = GRP * (2 * gp + a) + j
                    o_ref[:, HD * h:HD * (h + 1)] = o2[BLK * j:BLK * (j + 1), HD * a:HD * (a + 1)]
        for t, g_ref in enumerate(gate_refs):
            cs = slice(CW * t, CW * (t + 1))
            zz = g_ref[...]
            y = o_ref[:, cs] * (zz * _sigmoid(zz))
            y_ref[:, cs] = y.astype(_MXU)
            yt_ref[cs, :] = y.T.astype(_MXU)

    row = pl.BlockSpec((BLK, D), lambda i: (i, 0))
    gates = [pl.BlockSpec((BLK, CW), lambda i, t=t: (i, OFF_GA // CW + t)) for t in range(n_gate)]
    return _pcall(body, name="attn_fwd", grid=(NBLK,),
                  in_specs=[row] + _kv_specs() + _kv_specs() + [pl.BlockSpec(memory_space=pltpu.SMEM)] + gates,
                  out_specs=[row, pl.BlockSpec((BLK, N_Q), lambda i: (i, 0)), row, pl.BlockSpec((D, BLK), lambda i: (0, i))],
                  out_shape=[_sds((TP, D), F32), _sds((TP, N_Q), F32), _sds((TP, D), _MXU), _sds((D, TP), _MXU)],
                  operands=[q, k, k, k, v, v, v, sinks] + [z] * n_gate,
                  scratch=[pltpu.VMEM((GRP * BLK, 2 * nk), F32), pltpu.VMEM((GRP * BLK, 2 * nk), _MXU)],
                  vmem=40, sem=("parallel",), hook=hook)


def _attn_bwd(q, k, v, sinks, do, o, lse, cos_t, sin_t, dz, *, hook):
    def body(q_ref, k0_ref, kp_ref, kc_ref, v0_ref, vp_ref, vc_ref, sink_ref, do_ref, o_ref, lse_ref, c_ref, s_ref, dz_in,
             dq_ref, dk_ref, dv_ref, dsink_ref, dqrot_ref, s_scr, dp_scr, p_scr, ds_scr):
        i = pl.program_id(0)

        @pl.when(i == 0)
        def _():
            dk_ref[...] = jnp.zeros_like(dk_ref)
            dv_ref[...] = jnp.zeros_like(dv_ref)
            dsink_ref[...] = jnp.zeros_like(dsink_ref)

        mask = _attn_mask(i)
        row_starts = (0, pl.multiple_of(jnp.maximum(i - 1, 0) * BLK, BLK), pl.multiple_of(i * BLK, BLK))
        scale = HD ** -0.5
        nk = 3 * BLK
        for gp in range(N_KV // 2):
            q2 = _pair_heads(q_ref, gp)
            dom2 = _pair_heads(do_ref, gp).astype(_MXU)
            k2 = _pair_kv((k0_ref, kp_ref, kc_ref), gp)
            s_scr[...] = lax.dot_general(q2, k2, _NT, preferred_element_type=F32)
            dp_scr[...] = lax.dot_general(dom2, _pair_kv((v0_ref, vp_ref, vc_ref), gp), _NT, preferred_element_type=F32)
            for a in range(2):
                for j in range(GRP):
                    h = GRP * (2 * gp + a) + j
                    hs = slice(HD * h, HD * (h + 1))
                    rows = slice(BLK * j, BLK * (j + 1))
                    cols = slice(nk * a, nk * (a + 1))
                    lse_h = lse_ref[:, h:h + 1]
                    delta = jnp.sum(do_ref[:, hs] * o_ref[:, hs], axis=-1, keepdims=True)
                    dsink_ref[0:1, h:h + 1] += jnp.sum(-jnp.exp(sink_ref[h] - lse_h) * delta, axis=0, keepdims=True)
                    p = jnp.exp(jnp.where(mask, s_scr[rows, cols] * scale, NEG_INF) - lse_h)
                    p_scr[rows, cols] = p.astype(_MXU)
                    ds_scr[rows, cols] = (p * (dp_scr[rows, cols] - delta) * scale).astype(_MXU)
            ds = ds_scr[...]
            dq2 = jnp.dot(ds, k2, preferred_element_type=F32)
            dk2 = lax.dot_general(ds, q2, _TN, preferred_element_type=F32)
            dv2 = lax.dot_general(p_scr[...], dom2, _TN, preferred_element_type=F32)
            for a in range(2):
                g = 2 * gp + a
                gs = slice(HD * g, HD * (g + 1))
                for part in range(3):
                    rows = pl.ds(row_starts[part], BLK)
                    blk = slice(nk * a + BLK * part, nk * a + BLK * (part + 1))
                    dk_ref[rows, gs] += dk2[blk, HD * a:HD * (a + 1)]
                    dv_ref[rows, gs] += dv2[blk, HD * a:HD * (a + 1)]
                for j in range(GRP):
                    h = GRP * g + j
                    dqrot_ref[:, HD * h:HD * (h + 1)] = dq2[BLK * j:BLK * (j + 1), HD * a:HD * (a + 1)]
        dq_ref[...] = _rope(dqrot_ref[...], c_ref[...], s_ref[...], -1.0).astype(_MXU)

    row = pl.BlockSpec((BLK, D), lambda i: (i, 0))
    tab = pl.BlockSpec((BLK, 128), lambda i: (i, 0))
    full_kv = pl.BlockSpec((TP, D_KV), lambda i: (0, 0))
    return _pcall(
        body, name="attn_bwd", grid=(NBLK,),
        in_specs=[row] + _kv_specs() + _kv_specs() + [pl.BlockSpec(memory_space=pltpu.SMEM), row, row,
                  pl.BlockSpec((BLK, N_Q), lambda i: (i, 0)), tab, tab, _ANY],
        out_specs=[pl.BlockSpec((BLK, D), lambda i: (i, OFF_Q // D)), full_kv, full_kv,
                   pl.BlockSpec((8, 128), lambda i: (0, 0))],
        out_shape=[_sds((TP, D_IN), _MXU), _sds((TP, D_KV), F32), _sds((TP, D_KV), F32), _sds((8, 128), F32)],
        operands=[q, k, k, k, v, v, v, sinks, do, o, lse, cos_t, sin_t, dz],
        scratch=[pltpu.VMEM((BLK, D), F32), pltpu.VMEM((GRP * BLK, 6 * BLK), F32), pltpu.VMEM((GRP * BLK, 6 * BLK), F32),
                 pltpu.VMEM((GRP * BLK, 6 * BLK), _MXU), pltpu.VMEM((GRP * BLK, 6 * BLK), _MXU)],
        vmem=48, sem=("arbitrary",), aliases={13: 0}, hook=hook)


def _cast_into_slot(w32, pos, *, name, tr):
    r, cc = w32.shape

    def body(pos_ref, w_ref, o_ref):
        o_ref[...] = w_ref[...].astype(BF16)

    return _pcall(body, name=name, grid=(r // tr,), in_specs=[pl.BlockSpec((tr, cc), lambda i, p: (i, 0))],
                  out_specs=[pl.BlockSpec((None, tr, cc), lambda i, p: (p[1], i, 0))],
                  out_shape=[_sds((N_SH, r, cc), BF16)], operands=[w32], vmem=32, sem=("parallel",), prefetch=(pos,))[0]


def _pair_add(g, got, pos, *, name, tr, g_has_both_halves, hook=None):
    n, h, cc = got.shape
    nt = h // tr

    def body(pos_ref, g_ref, r_ref, own_ref, s16_ref):
        s = g_ref[...] + r_ref[...].astype(F32)
        s16_ref[...] = s.astype(BF16)

        @pl.when(pl.program_id(1) == pos_ref[1])
        def _():
            own_ref[...] = s

    g_index = (lambda i, s, p: (s, p[0] * nt + i, 0)) if g_has_both_halves else (lambda i, s, p: (s, i, 0))
    return _pcall(body, name=name, grid=(nt, n),
                  in_specs=[pl.BlockSpec((None, tr, cc), g_index), pl.BlockSpec((None, tr, cc), lambda i, s, p: (s, i, 0))],
                  out_specs=[pl.BlockSpec((tr, cc), lambda i, s, p: (i, 0)),
                             pl.BlockSpec((None, tr, cc), lambda i, s, p: (s, i, 0))],
                  out_shape=[_sds((h, cc), F32), _sds((n, h, cc), BF16)], operands=[g, got], vmem=40,
                  sem=("parallel", "arbitrary"), prefetch=(pos,), hook=hook)


def _relay_add(s16, raw, pos, *, name, tr):
    _, q, cc = raw.shape
    nt = q // tr

    def body(pos_ref, s_ref, r_ref, o_ref):
        o_ref[...] = (s_ref[...].astype(F32) + r_ref[...].astype(F32)).astype(BF16)

    blk = pl.BlockSpec((None, tr, cc), lambda k, i, p: (k, i, 0))
    return _pcall(body, name=name, grid=(2, nt),
                  in_specs=[pl.BlockSpec((None, tr, cc), lambda k, i, p: (p[3 + k], (1 - k) * nt + i, 0)), blk],
                  out_specs=[blk], out_shape=[_sds((2, q, cc), BF16)], operands=[s16, raw], vmem=40,
                  sem=("parallel", "parallel"), prefetch=(pos,))[0]


def _sum_chips(own, got, pos, *, name, tr):
    h, cc = own.shape
    n_got = got.shape[0]
    nt = h // tr

    def body(pos_ref, o_ref, r_ref, out_ref):
        acc = o_ref[...]
        for k in range(n_got):
            acc = acc + r_ref[k].astype(F32)
        out_ref[...] = acc

    return _pcall(body, name=name, grid=(nt,),
                  in_specs=[pl.BlockSpec((tr, cc), lambda i, p: (i, 0)),
                            pl.BlockSpec((n_got, tr, cc), lambda i, p: (0, i, 0))],
                  out_specs=[pl.BlockSpec((tr, cc), lambda i, p: (p[0] * nt + i, 0))],
                  out_shape=[_sds((2 * h, cc), F32)], operands=[own, got], vmem=40, sem=("parallel",), prefetch=(pos,))[0]


def _hand_over(arrays):
    def body(*refs):
        del refs

    n = len(arrays)
    return pl.pallas_call(body, name="hand_over", in_specs=[_ANY] * n, out_specs=[_ANY] * n,
                          out_shape=[_sds(a.shape, a.dtype) for a in arrays],
                          input_output_aliases={t: t for t in range(n)})(*arrays)


def _adamw(w, g, m, v, *, name, tr, g_row0=0):
    r, cc = w.shape
    g_blk0 = g_row0 // tr

    def body(*refs):
        _adamw_step(*refs)

    blk = pl.BlockSpec((tr, cc), lambda i: (i, 0))
    gblk = pl.BlockSpec((tr, cc), lambda i: (g_blk0 + i, 0))
    return _pcall(body, name=name, grid=(r // tr,), in_specs=[blk, gblk, blk, blk], out_specs=[blk] * 4,
                  out_shape=[_sds((r, cc), F32)] * 4, operands=[w, g, m, v], vmem=48, sem=("parallel",))


_SMALL_ROWS = 40
_B_IN_ROWS = 7


def _row_pad(v, rows):
    flat = v.reshape(-1)
    return jnp.pad(flat, (0, rows * D - flat.shape[0])).reshape(rows, D)


def _pack_ra(w):
    return w.transpose(1, 0, 2).reshape(64, D)


def _unpack_ra(p, like):
    return p.reshape(64, N_RB, RB).transpose(1, 0, 2).reshape(like.shape)


def _gate_full(g4):
    return g4.reshape(N_SH, 64, N_RB, RB).transpose(2, 0, 1, 3).reshape(N_RB, RB, RB)


def kernel(x, meta_tokens, ln_emb_g, ln_emb_b, w_in, b_in, conv_w, conv_b, w_ra, b_ra, w_ri, b_ri, lru_lambda, sinks, w_rnn_out, w_attn_out, w_o, b_o, ln_g, ln_b, loss_target, m_meta_tokens, m_ln_emb_g, m_ln_emb_b, m_w_in, m_b_in, m_conv_w, m_conv_b, m_w_ra, m_b_ra, m_w_ri, m_b_ri, m_lru_lambda, m_sinks, m_w_rnn_out, m_w_attn_out, m_w_o, m_b_o, m_ln_g, m_ln_b, v_meta_tokens, v_ln_emb_g, v_ln_emb_b, v_w_in, v_b_in, v_conv_w, v_conv_b, v_w_ra, v_b_ra, v_w_ri, v_b_ri, v_lru_lambda, v_sinks, v_w_rnn_out, v_w_attn_out, v_w_o, v_b_o, v_ln_g, v_ln_b):
    xi, yi, ci = _my_pos()
    shard = 2 * xi + yi
    pos = jnp.stack([ci, shard, 1 - ci, shard ^ 2, shard ^ 1]).astype(jnp.int32)
    cos_t, sin_t = _rope_tables()
    zero_bias = jnp.zeros((1, D), F32)
    ln_emb_g2, ln_emb_b2 = ln_emb_g[None], ln_emb_b[None]

    small = jnp.concatenate([conv_w[0], meta_tokens, jnp.zeros((4, 512), F32)], axis=0)
    small4 = _gather_small(small)
    conv_w_full = small4[:, 0:4].transpose(1, 0, 2).reshape(CONV_W, D)
    meta_full = small4[:, 4:20].transpose(1, 0, 2).reshape(N_META, D)
    w_own = _cast_into_slot(w_in[0], pos, name="cast_w_in", tr=256)
    wa_own = _cast_into_slot(jnp.concatenate([w_attn_out[0], w_o[0], _pack_ra(w_ra[0]), _pack_ra(w_ri[0])], axis=0),
                             pos, name="cast_w_a", tr=288)
    wb_own = _cast_into_slot(w_rnn_out[0], pos, name="cast_w_b", tr=256)

    h32, h16 = _ln_emb_fwd(x[0], meta_full, ln_emb_g2, ln_emb_b2)
    order = jnp.stack([shard, shard ^ 2, shard ^ 1, shard ^ 3]).astype(jnp.int32)
    z, w_in4 = _mm_z_gather(h16, w_own, b_in, order)
    q, k, v = _rope_fwd(z, cos_t, sin_t)
    o, lse, yb_in, yb_in_t, wa4 = _attn_fwd(q, k, v, sinks[0], z, hook=_hook_gather(wa_own, 0.6))
    w_ra_full = _gate_full(wa4[:, 2 * SQ_ROWS:2 * SQ_ROWS + 64])
    w_ri_full = _gate_full(wa4[:, 2 * SQ_ROWS + 64:2 * SQ_ROWS + 128])
    a_dec, u_in = _rnn_gates_fwd(z, conv_w_full, conv_b, w_ra_full, w_ri_full, b_ra, b_ri, lru_lambda)
    hr, ya_in, wb4 = _scan_fwd(a_dec, u_in, z, hook=_hook_gather(wb_own, 0.6))
    sq_w = {0: (wb4, 0), 1: (wa4, 0), 2: (wa4, 1)}

    def sq_nn(a, kk, bias, name, out_cols, out_index, carried=None):
        wp, blk = sq_w[kk]
        return _mm_nn(a, wp, bias, name=name, grid=(2, D // CW), tm=HALF_TP, tn=CW, k=D, a_index=lambda i, j: (i, 0),
                      w_block=(N_SH, SQ_ROWS, CW), w_index=lambda i, j: (0, blk, j), out_cols=out_cols,
                      out_index=out_index, carried=carried)[0]

    def sq_nt(a, a_blk, kk, name, hook=None):
        wp, blk = sq_w[kk]
        return _mm_nt(a, wp, name=name, grid=(2, N_SH, 1), tm=HALF_TP, tn=SQ_ROWS, tk=D,
                      a_index=lambda i, j, q: (i, a_blk), w_block=(None, SQ_ROWS, D),
                      w_index=lambda i, j, q: (j, blk, 0), out_cols=D, hook=hook)

    y2 = sq_nn(ya_in, 0, zero_bias, "mm_ya", 2 * D, lambda i, j: (i, j))
    y2 = sq_nn(yb_in, 1, zero_bias, "mm_yb", 2 * D, lambda i, j: (i, D // CW + j), carried=y2)
    mixed = _merge_fwd(y2, z)
    out = sq_nn(mixed, 2, b_o, "mm_out", D, lambda i, j: (i, j))
    dr, dr16, sums_o = _ln_out_loss(h32, out, loss_target[0], ln_g, ln_b)

    def sq_tn(at, b, b_blk, kk, name, carried=None, a_transposed=True):
        return _mm_tn(at, b, name=name, grid=(N_SH, 1), tm=SQ_ROWS, tn=D,
                      a_index=(lambda i, j: (i, 0)) if a_transposed else (lambda i, j: (0, i)),
                      b_index=lambda i, j: (0, b_blk), out_shape=(N_SH, PACK_ROWS, D), out_block=(None, SQ_ROWS, D),
                      out_index=lambda i, j: (i, kk, 0), carried=carried, a_transposed=a_transposed)[0]

    gsq = sq_tn(mixed, dr16, 0, 2, "mm_dwo", a_transposed=False)
    dmix = sq_nt(dr16, 0, 2, "mm_dmix")[0]
    dy2, dz = _merge_bwd(dmix, y2, z)
    gsq = sq_tn(ya_in, dy2, 0, 0, "mm_dwrnn", carried=gsq, a_transposed=False)
    gsq = sq_tn(yb_in_t, dy2, 1, 1, "mm_dwattn", carried=gsq)
    dya_in = sq_nt(dy2, 0, 0, "mm_dyain")[0]
    dhr, dz = _mul_silu_bwd(dya_in, hr, z, OFF_GR, dz, name="gate_a_bwd")
    lam_s = _scan_bwd(a_dec, dhr)
    dz, gsq, sums_r = _rnn_gates_bwd(z, lam_s, hr, conv_w_full, conv_b, w_ra_full, w_ri_full, b_ra, b_ri, lru_lambda, dz, gsq)
    dyb_in, gsq, got_sq = sq_nt(dy2, 1, 1, "mm_dybin", hook=_hook_pair(gsq, True, (0, 1)))
    do, dz, gsq, got_sq = _mul_silu_bwd(dyb_in, o, z, OFF_GA, dz, name="gate_b_bwd",
                                        hook=_hook_pair(gsq, True, (2, 3), land=got_sq))
    own_sq, s16_sq = _pair_add(gsq, got_sq, pos, name="red_w_sq_add", tr=208, g_has_both_halves=True)
    dz, dk_rot, dv32, dsink, s16_sq, fin_sq, raw_sq = _attn_bwd(q, k, v, sinks[0], do, o, lse, cos_t, sin_t, dz,
                                                                hook=_hook_scatter_direct(s16_sq))
    dz = _rope_bwd_k(dk_rot, dv32, cos_t, sin_t, dz)
    comb_sq = _relay_add(s16_sq, raw_sq, pos, name="red_w_sq_relay", tr=208)

    def dwin(half_idx, name, hook, **kw):
        return _mm_tn(h16, dz, name=name, grid=(1, N_SH * PER_IN), tm=D // 2, tn=TN_IN, a_transposed=False,
                      a_index=lambda i, j, p: (0, p[half_idx]), b_index=lambda i, j, p: (0, j),
                      out_shape=(N_SH, D // 2, W_IN_COLS), out_block=(None, D // 2, TN_IN),
                      out_index=lambda i, j, p: (j // PER_IN, 0, j % PER_IN), prefetch=(pos,), hook=hook, **kw)

    gin_sib, g_b_in, comb_sq, fin_sq = dwin(2, "mm_dwin_sib", _hook_scatter_relay(comb_sq, fin_sq),
                                            out_dtype=BF16, colsum=True)
    red_sq = _sum_chips(own_sq, fin_sq, pos, name="red_w_sq_sum", tr=208)
    gin_own, gin_sib, got_in = dwin(0, "mm_dwin_own", _hook_pair(gin_sib, False))
    own_in, s16_in, g_sq = _pair_add(gin_own, got_in, pos, name="red_w_in_add", tr=128, g_has_both_halves=False,
                                     hook=_hook_halves(red_sq))
    dhz, s16_in, fin_in, raw_in = _mm_nt(dz, w_in4, name="mm_dhz", grid=(2, 2, N_SH), tm=HALF_TP, tn=D // 2,
                                         tk=W_IN_COLS, a_index=lambda i, j, q: (i, q),
                                         w_block=(None, D // 2, W_IN_COLS), w_index=lambda i, j, q: (q, j, 0), out_cols=D,
                                         hook=_hook_scatter_direct(s16_in))
    comb_in = _relay_add(s16_in, raw_in, pos, name="red_w_in_relay", tr=128)
    sq_params = [("w_rnn_out", w_rnn_out, m_w_rnn_out, v_w_rnn_out), ("w_attn_out", w_attn_out, m_w_attn_out, v_w_attn_out),
                 ("w_o", w_o, m_w_o, v_w_o)]
    g_x, g_meta_local, sums_e, *sq_res, comb_in, fin_in = _ln_emb_bwd(
        x[0], meta_full, ln_emb_g2, dr, dhz, g_sq, [(w_[0], m_[0], v_[0]) for _, w_, m_, v_ in sq_params],
        hook=_hook_scatter_relay(comb_in, fin_in))
    red_in = _sum_chips(own_in, fin_in, pos, name="red_w_in_sum", tr=128)

    spack = jnp.concatenate([
        sums_e[0:1], sums_e[1:2], _row_pad(g_b_in, _B_IN_ROWS), sums_r[0:4], sums_r[4:5], sums_r[5:6], sums_r[6:7],
        sums_r[7:8], _row_pad(dsink[0:1, 0:N_Q], 1), sums_o[2:3], sums_o[0:1], sums_o[1:2], g_meta_local, sums_o[3:4],
        jnp.zeros((_SMALL_ROWS - 38, D), F32)], axis=0)
    sred, g_in = _allreduce_small(spack, red_in)

    big = {"w_in": [t.reshape(w_in.shape) for t in
                    _adamw(w_in[0], g_in, m_w_in[0], v_w_in[0], name="adamw_w_in", tr=128)]}
    g_x, *sq_res = _hand_over([g_x] + sq_res)
    for kk, (n, w_, _, _) in enumerate(sq_params):
        big[n] = [t.reshape(w_.shape) for t in sq_res[4 * kk:4 * kk + 4]]
    for kk, (n, w_, m_, v_) in enumerate([("w_ra", w_ra, m_w_ra, v_w_ra), ("w_ri", w_ri, m_w_ri, v_w_ri)]):
        big[n] = [_unpack_ra(t, w_) for t in
                  _adamw(_pack_ra(w_[0]), g_sq, _pack_ra(m_[0]), _pack_ra(v_[0]), name="adamw_" + n, tr=64,
                         g_row0=3 * SQ_ROWS + 64 * kk)]

    loss = sred[37, 0]
    col0 = shard * 512
    g_conv_w = lax.dynamic_slice(sred[9:13], (0, col0), (CONV_W, 512))
    g_meta = lax.dynamic_slice(sred[21:37], (0, col0), (N_META, 512))
    small_g = {"ln_emb_g": sred[0:1], "ln_emb_b": sred[1:2], "b_in": sred[2:9], "conv_w": g_conv_w.reshape(1, D),
               "conv_b": sred[13:14], "b_ra": sred[14:15], "b_ri": sred[15:16], "lru_lambda": sred[16:17],
               "sinks": sred[17:18], "b_o": sred[18:19], "ln_g": sred[19:20], "ln_b": sred[20:21],
               "meta_tokens": g_meta.reshape(4, D)}
    small_names = list(small_g)

    def small_pack(vals):
        rows = []
        for n in small_names:
            a = vals[n]
            if n == "b_in":
                a = _row_pad(a, _B_IN_ROWS)
            elif n == "sinks":
                a = _row_pad(a, 1)
            else:
                a = a.reshape(-1, D)
            rows.append(a)
        return jnp.concatenate(rows + [jnp.zeros((24 - 22, D), F32)], axis=0)

    w_small = dict(ln_emb_g=ln_emb_g, ln_emb_b=ln_emb_b, b_in=b_in, conv_w=conv_w, conv_b=conv_b, b_ra=b_ra, b_ri=b_ri,
                   lru_lambda=lru_lambda, sinks=sinks, b_o=b_o, ln_g=ln_g, ln_b=ln_b, meta_tokens=meta_tokens)
    m_small = dict(ln_emb_g=m_ln_emb_g, ln_emb_b=m_ln_emb_b, b_in=m_b_in, conv_w=m_conv_w, conv_b=m_conv_b, b_ra=m_b_ra,
                   b_ri=m_b_ri, lru_lambda=m_lru_lambda, sinks=m_sinks, b_o=m_b_o, ln_g=m_ln_g, ln_b=m_ln_b,
                   meta_tokens=m_meta_tokens)
    v_small = dict(ln_emb_g=v_ln_emb_g, ln_emb_b=v_ln_emb_b, b_in=v_b_in, conv_w=v_conv_w, conv_b=v_conv_b, b_ra=v_b_ra,
                   b_ri=v_b_ri, lru_lambda=v_lru_lambda, sinks=v_sinks, b_o=v_b_o, ln_g=v_ln_g, ln_b=v_ln_b,
                   meta_tokens=v_meta_tokens)
    g_small_pack = jnp.concatenate([small_g[n] for n in small_names] + [jnp.zeros((2, D), F32)], axis=0)
    small_res = _adamw(small_pack(w_small), g_small_pack, small_pack(m_small), small_pack(v_small),
                       name="adamw_small", tr=24)

    small_rows = {}
    r0 = 0
    for n in small_names:
        nrows = small_g[n].shape[0]
        small_rows[n] = (r0, nrows)
        r0 += nrows

    def small_out(packed, n, like):
        a, nrows = small_rows[n]
        flat = packed[a:a + nrows].reshape(-1)
        return flat[:like.size].reshape(like.shape)

    weights = dict(meta_tokens=meta_tokens, ln_emb_g=ln_emb_g, ln_emb_b=ln_emb_b, w_in=w_in, b_in=b_in, conv_w=conv_w,
                   conv_b=conv_b, w_ra=w_ra, b_ra=b_ra, w_ri=w_ri, b_ri=b_ri, lru_lambda=lru_lambda, sinks=sinks,
                   w_rnn_out=w_rnn_out, w_attn_out=w_attn_out, w_o=w_o, b_o=b_o, ln_g=ln_g, ln_b=ln_b)

    def outputs(which):
        return [big[n][which] if n in big else small_out(small_res[which], n, like) for n, like in weights.items()]

    return (loss, g_x[None], *outputs(0), *outputs(1), *outputs(2), *outputs(3))
```
